```python
import jax, jax.numpy as jnp
from jax import lax
import numpy as np

D_MODEL = 1024
BATCH = 8
SEQ = 2048
DEPTH = 4

N_A = DEPTH // 2
N_B = DEPTH - N_A
D_RNN = D_MODEL
LRU_BLOCK_W = 256
LRU_BLOCKS = D_RNN // LRU_BLOCK_W
CONV_W = 4
LRU_C = 8.0
N_HEADS = 16
HEAD_DIM = D_MODEL // N_HEADS
Q_BLOCK = 128
D_FF = ((8 * D_MODEL + 3 * 256 - 1) // (3 * 256)) * 256
EPS = 1e-6

kernel_name = "hawk_fox_yoco_hybrid"


def rmsnorm(x, g):
    xf = x.astype(jnp.float32)
    y = xf * lax.rsqrt(jnp.mean(xf * xf, axis=-1, keepdims=True) + EPS)
    return (y * g.astype(jnp.float32)).astype(x.dtype)


def swiglu(x, w_in, w_out):
    gate, up = jnp.split(x @ w_in, 2, axis=-1)
    return (jax.nn.silu(gate) * up) @ w_out


def causal_depthwise_conv(x, w, b):
    S = x.shape[1]
    xp = jnp.pad(x, ((0, 0), (CONV_W - 1, 0), (0, 0)))
    out = b
    for tap in range(CONV_W):
        out = out + xp[:, tap:tap + S] * w[tap]
    return out


def _lin_rec_combine(left, right):
    a1, b1 = left
    a2, b2 = right
    return a1 * a2, a2 * b1 + b2


def rg_lru(x, w_gates, b_gates, lru_param):
    B, S, _ = x.shape
    xb = x.reshape(B, S, LRU_BLOCKS, LRU_BLOCK_W)
    g = (jnp.einsum('bsnw,nwv->bsnv', xb, w_gates) + b_gates).astype(jnp.float32)
    gate_i = jax.nn.sigmoid(g[..., :LRU_BLOCK_W])
    gate_r = jax.nn.sigmoid(g[..., LRU_BLOCK_W:])
    log_a = -LRU_C * gate_r * jax.nn.softplus(-lru_param.astype(jnp.float32).reshape(LRU_BLOCKS, LRU_BLOCK_W))
    a = jnp.exp(log_a)
    mult = jnp.sqrt(-jnp.expm1(2.0 * log_a))
    u = xb.astype(jnp.float32) * gate_i * mult
    a = a.reshape(B, S, D_RNN)
    u = u.reshape(B, S, D_RNN)
    _, h = lax.associative_scan(_lin_rec_combine, (a, u), axis=1)
    return h.astype(x.dtype)


def recurrent_mixer(xn, w_in, conv_w, conv_b, w_gates, b_gates, lru_param, w_out):
    proj = xn @ w_in
    gate_branch, rec = proj[..., :D_RNN], proj[..., D_RNN:]
    rec = causal_depthwise_conv(rec, conv_w, conv_b)
    h = rg_lru(rec, w_gates, b_gates, lru_param)
    return (jax.nn.gelu(gate_branch) * h) @ w_out


def shared_kv_forget(xs, norm_kv, w_kvf, b_forget):
    B, S, _ = xs.shape
    p = rmsnorm(xs, norm_kv) @ w_kvf
    k = p[..., :D_MODEL].reshape(B, S, N_HEADS, HEAD_DIM).transpose(0, 2, 1, 3)
    v = p[..., D_MODEL:2 * D_MODEL].reshape(B, S, N_HEADS, HEAD_DIM).transpose(0, 2, 1, 3)
    f_logit = (p[..., 2 * D_MODEL:] + b_forget).astype(jnp.float32)
    c = jnp.cumsum(jax.nn.log_sigmoid(f_logit), axis=1).transpose(0, 2, 1)
    return k, v, c


def forgetting_attention(xn, w_q, w_o, k, v, c):
    B, S, _ = xn.shape
    q = (xn @ w_q).reshape(B, S, N_HEADS, HEAD_DIM).transpose(0, 2, 1, 3)
    scale = HEAD_DIM ** -0.5
    outs = []
    for blk in range(S // Q_BLOCK):
        q0 = blk * Q_BLOCK
        q1 = q0 + Q_BLOCK
        s = jnp.einsum('bhqd,bhkd->bhqk', q[:, :, q0:q1], k[:, :, :q1],
                       preferred_element_type=jnp.float32) * scale
        s = s + c[:, :, q0:q1, None] - c[:, :, None, :q1]
        mask = (q0 + jnp.arange(Q_BLOCK))[:, None] >= jnp.arange(q1)[None, :]
        s = jnp.where(mask, s, -jnp.inf)
        p = jax.nn.softmax(s, axis=-1)
        outs.append(jnp.einsum('bhqk,bhkd->bhqd', p.astype(v.dtype), v[:, :, :q1]))
    o = jnp.concatenate(outs, axis=2).transpose(0, 2, 1, 3).reshape(B, S, D_MODEL)
    return o @ w_o


def _fwd_setup_inputs(seed: int = 0) -> dict:
    key = jax.random.key(seed)
    ks = jax.random.split(key, 20)
    f32 = jnp.float32
    nrm = lambda k, shape, fan_in: jax.random.normal(k, shape, f32) * (fan_in ** -0.5)
    gain = lambda k, shape: 1.0 + 0.01 * jax.random.normal(k, shape, f32)

    x = jax.random.normal(ks[0], (BATCH, SEQ, D_MODEL), f32)
    norm_mix = gain(ks[1], (DEPTH, D_MODEL))
    norm_ffn = gain(ks[2], (DEPTH, D_MODEL))
    w_ffn_in = nrm(ks[3], (DEPTH, D_MODEL, 2 * D_FF), D_MODEL)
    w_ffn_out = nrm(ks[4], (DEPTH, D_FF, D_MODEL), D_FF)

    w_rec_in = nrm(ks[5], (N_A, D_MODEL, 2 * D_RNN), D_MODEL)
    conv_w = nrm(ks[6], (N_A, CONV_W, D_RNN), CONV_W)
    conv_b = 0.01 * jax.random.normal(ks[7], (N_A, D_RNN), f32)
    w_lru_gates = nrm(ks[8], (N_A, LRU_BLOCKS, LRU_BLOCK_W, 2 * LRU_BLOCK_W), LRU_BLOCK_W)
    b_lru_gates = 0.01 * jax.random.normal(ks[9], (N_A, LRU_BLOCKS, 2 * LRU_BLOCK_W), f32)
    u = jax.random.uniform(ks[10], (N_A, D_RNN), f32, 0.9, 0.999)
    s = u ** (1.0 / LRU_C)
    lru_param = jnp.log(s) - jnp.log1p(-s)
    w_rec_out = nrm(ks[11], (N_A, D_RNN, D_MODEL), D_RNN)

    norm_kv = gain(ks[12], (D_MODEL,))
    w_kvf = nrm(ks[13], (D_MODEL, 2 * D_MODEL + N_HEADS), D_MODEL)
    b_forget = jnp.linspace(1.0, 5.0, N_HEADS, dtype=f32) + 0.1 * jax.random.normal(ks[14], (N_HEADS,), f32)
    w_q = nrm(ks[15], (N_B, D_MODEL, D_MODEL), D_MODEL)
    w_o = nrm(ks[16], (N_B, D_MODEL, D_MODEL), D_MODEL)
    norm_final = gain(ks[17], (D_MODEL,))
    return {"x": x, "norm_mix": norm_mix, "norm_ffn": norm_ffn, "w_ffn_in": w_ffn_in,
            "w_ffn_out": w_ffn_out, "w_rec_in": w_rec_in, "conv_w": conv_w, "conv_b": conv_b,
            "w_lru_gates": w_lru_gates, "b_lru_gates": b_lru_gates, "lru_param": lru_param,
            "w_rec_out": w_rec_out, "norm_kv": norm_kv, "w_kvf": w_kvf, "b_forget": b_forget,
            "w_q": w_q, "w_o": w_o, "norm_final": norm_final}


def _fwd_reference(x, norm_mix, norm_ffn, w_ffn_in, w_ffn_out, w_rec_in, conv_w, conv_b,
              w_lru_gates, b_lru_gates, lru_param, w_rec_out, norm_kv, w_kvf, b_forget,
              w_q, w_o, norm_final):
    h = x
    k = v = c = None
    for layer in range(DEPTH):
        xn = rmsnorm(h, norm_mix[layer])
        if layer < N_A:
            i = layer
            mix = recurrent_mixer(xn, w_rec_in[i], conv_w[i], conv_b[i], w_lru_gates[i],
                                  b_lru_gates[i], lru_param[i], w_rec_out[i])
        else:
            if layer == N_A:
                k, v, c = shared_kv_forget(h, norm_kv, w_kvf, b_forget)
            j = layer - N_A
            mix = forgetting_attention(xn, w_q[j], w_o[j], k, v, c)
        h = h + mix
        h = h + swiglu(rmsnorm(h, norm_ffn[layer]), w_ffn_in[layer], w_ffn_out[layer])
    return rmsnorm(h, norm_final)


import jax as _jax
import jax.numpy as _jnp

TWIN_FORMAT = 'train_step'
FWD_PARAMS = ['x', 'norm_mix', 'norm_ffn', 'w_ffn_in', 'w_ffn_out', 'w_rec_in', 'conv_w', 'conv_b', 'w_lru_gates', 'b_lru_gates', 'lru_param', 'w_rec_out', 'norm_kv', 'w_kvf', 'b_forget', 'w_q', 'w_o', 'norm_final']
TWIN_WEIGHTS = ['norm_mix', 'norm_ffn', 'w_ffn_in', 'w_ffn_out', 'w_rec_in', 'conv_w', 'conv_b', 'w_lru_gates', 'b_lru_gates', 'lru_param', 'w_rec_out', 'norm_kv', 'w_kvf', 'b_forget', 'w_q', 'w_o', 'norm_final']
TWIN_DIFF_INPUT = 'x'
TWIN_INPUTS = ['x', 'norm_mix', 'norm_ffn', 'w_ffn_in', 'w_ffn_out', 'w_rec_in', 'conv_w', 'conv_b', 'w_lru_gates', 'b_lru_gates', 'lru_param', 'w_rec_out', 'norm_kv', 'w_kvf', 'b_forget', 'w_q', 'w_o', 'norm_final', 'loss_target', 'm_norm_mix', 'm_norm_ffn', 'm_w_ffn_in', 'm_w_ffn_out', 'm_w_rec_in', 'm_conv_w', 'm_conv_b', 'm_w_lru_gates', 'm_b_lru_gates', 'm_lru_param', 'm_w_rec_out', 'm_norm_kv', 'm_w_kvf', 'm_b_forget', 'm_w_q', 'm_w_o', 'm_norm_final', 'v_norm_mix', 'v_norm_ffn', 'v_w_ffn_in', 'v_w_ffn_out', 'v_w_rec_in', 'v_conv_w', 'v_conv_b', 'v_w_lru_gates', 'v_b_lru_gates', 'v_lru_param', 'v_w_rec_out', 'v_norm_kv', 'v_w_kvf', 'v_b_forget', 'v_w_q', 'v_w_o', 'v_norm_final']
TWIN_OUTPUTS = ['loss', 'grad_x', 'grad_norm_mix', 'grad_norm_ffn', 'grad_w_ffn_in', 'grad_w_ffn_out', 'grad_w_rec_in', 'grad_conv_w', 'grad_conv_b', 'grad_w_lru_gates', 'grad_b_lru_gates', 'grad_lru_param', 'grad_w_rec_out', 'grad_norm_kv', 'grad_w_kvf', 'grad_b_forget', 'grad_w_q', 'grad_w_o', 'grad_norm_final', 'delta_norm_mix', 'delta_norm_ffn', 'delta_w_ffn_in', 'delta_w_ffn_out', 'delta_w_rec_in', 'delta_conv_w', 'delta_conv_b', 'delta_w_lru_gates', 'delta_b_lru_gates', 'delta_lru_param', 'delta_w_rec_out', 'delta_norm_kv', 'delta_w_kvf', 'delta_b_forget', 'delta_w_q', 'delta_w_o', 'delta_norm_final', 'new_m_norm_mix', 'new_m_norm_ffn', 'new_m_w_ffn_in', 'new_m_w_ffn_out', 'new_m_w_rec_in', 'new_m_conv_w', 'new_m_conv_b', 'new_m_w_lru_gates', 'new_m_b_lru_gates', 'new_m_lru_param', 'new_m_w_rec_out', 'new_m_norm_kv', 'new_m_w_kvf', 'new_m_b_forget', 'new_m_w_q', 'new_m_w_o', 'new_m_norm_final', 'new_v_norm_mix', 'new_v_norm_ffn', 'new_v_w_ffn_in', 'new_v_w_ffn_out', 'new_v_w_rec_in', 'new_v_conv_w', 'new_v_conv_b', 'new_v_w_lru_gates', 'new_v_b_lru_gates', 'new_v_lru_param', 'new_v_w_rec_out', 'new_v_norm_kv', 'new_v_w_kvf', 'new_v_b_forget', 'new_v_w_q', 'new_v_w_o', 'new_v_norm_final']
TWIN_LEAF_KINDS = {'loss': 'loss', 'grad_x': 'grad_x', 'grad_norm_mix': 'grad_w', 'grad_norm_ffn': 'grad_w', 'grad_w_ffn_in': 'grad_w', 'grad_w_ffn_out': 'grad_w', 'grad_w_rec_in': 'grad_w', 'grad_conv_w': 'grad_w', 'grad_conv_b': 'grad_w', 'grad_w_lru_gates': 'grad_w', 'grad_b_lru_gates': 'grad_w', 'grad_lru_param': 'grad_w', 'grad_w_rec_out': 'grad_w', 'grad_norm_kv': 'grad_w', 'grad_w_kvf': 'grad_w', 'grad_b_forget': 'grad_w', 'grad_w_q': 'grad_w', 'grad_w_o': 'grad_w', 'grad_norm_final': 'grad_w', 'delta_norm_mix': 'delta_w', 'delta_norm_ffn': 'delta_w', 'delta_w_ffn_in': 'delta_w', 'delta_w_ffn_out': 'delta_w', 'delta_w_rec_in': 'delta_w', 'delta_conv_w': 'delta_w', 'delta_conv_b': 'delta_w', 'delta_w_lru_gates': 'delta_w', 'delta_b_lru_gates': 'delta_w', 'delta_lru_param': 'delta_w', 'delta_w_rec_out': 'delta_w', 'delta_norm_kv': 'delta_w', 'delta_w_kvf': 'delta_w', 'delta_b_forget': 'delta_w', 'delta_w_q': 'delta_w', 'delta_w_o': 'delta_w', 'delta_norm_final': 'delta_w', 'new_m_norm_mix': 'new_m', 'new_m_norm_ffn': 'new_m', 'new_m_w_ffn_in': 'new_m', 'new_m_w_ffn_out': 'new_m', 'new_m_w_rec_in': 'new_m', 'new_m_conv_w': 'new_m', 'new_m_conv_b': 'new_m', 'new_m_w_lru_gates': 'new_m', 'new_m_b_lru_gates': 'new_m', 'new_m_lru_param': 'new_m', 'new_m_w_rec_out': 'new_m', 'new_m_norm_kv': 'new_m', 'new_m_w_kvf': 'new_m', 'new_m_b_forget': 'new_m', 'new_m_w_q': 'new_m', 'new_m_w_o': 'new_m', 'new_m_norm_final': 'new_m', 'new_v_norm_mix': 'new_v', 'new_v_norm_ffn': 'new_v', 'new_v_w_ffn_in': 'new_v', 'new_v_w_ffn_out': 'new_v', 'new_v_w_rec_in': 'new_v', 'new_v_conv_w': 'new_v', 'new_v_conv_b': 'new_v', 'new_v_w_lru_gates': 'new_v', 'new_v_b_lru_gates': 'new_v', 'new_v_lru_param': 'new_v', 'new_v_w_rec_out': 'new_v', 'new_v_norm_kv': 'new_v', 'new_v_w_kvf': 'new_v', 'new_v_b_forget': 'new_v', 'new_v_w_q': 'new_v', 'new_v_w_o': 'new_v', 'new_v_norm_final': 'new_v'}


def _forward(args):
    return _fwd_reference(*[args[k] for k in FWD_PARAMS])


def _output_shape():
    out = _jax.eval_shape(lambda: _forward(_fwd_setup_inputs(0)))
    return out.shape, out.dtype

N_MICROBATCH = 1
ADAM_LR = 0.001
ADAM_B1 = 0.9
ADAM_B2 = 0.999
ADAM_EPS = 1e-08
ADAM_WD = 0.01
ADAM_STEP = 10
PER_EXAMPLE_BATCH_AXIS = {'x': 0, 'loss_target': 0}
SHARED_INPUTS = []
_WEIGHT_DTYPES = {'norm_mix': _jnp.float32, 'norm_ffn': _jnp.float32, 'w_ffn_in': _jnp.float32, 'w_ffn_out': _jnp.float32, 'w_rec_in': _jnp.float32, 'conv_w': _jnp.float32, 'conv_b': _jnp.float32, 'w_lru_gates': _jnp.float32, 'b_lru_gates': _jnp.float32, 'lru_param': _jnp.float32, 'w_rec_out': _jnp.float32, 'norm_kv': _jnp.float32, 'w_kvf': _jnp.float32, 'b_forget': _jnp.float32, 'w_q': _jnp.float32, 'w_o': _jnp.float32, 'norm_final': _jnp.float32}
MOMENT_SCALE = {'norm_mix': 7.692141e-02, 'norm_ffn': 9.015691e-02, 'w_ffn_in': 3.797290e-02, 'w_ffn_out': 6.201435e-02, 'w_rec_in': 7.379235e-02, 'conv_w': 8.407306e-02, 'conv_b': 9.830304e-01, 'w_lru_gates': 3.116754e-02, 'b_lru_gates': 2.506634e-02, 'lru_param': 4.061955e-02, 'w_rec_out': 7.814071e-02, 'norm_kv': 5.957206e-02, 'w_kvf': 4.185535e-02, 'b_forget': 1.659264e-01, 'w_q': 2.356635e-02, 'w_o': 3.398623e-02, 'norm_final': 1.599743e+01}


def _to_microbatches(a, axis):
    t = _jnp.moveaxis(a, axis, 0)
    t = t.reshape((N_MICROBATCH, t.shape[0] // N_MICROBATCH) + t.shape[1:])
    return _jnp.moveaxis(t, 1, axis + 1)


def setup_inputs(seed: int = 0) -> dict:
    inp = _fwd_setup_inputs(seed)
    key = _jax.random.fold_in(_jax.random.key(seed), 7919)
    shape, _ = _output_shape()
    out = dict(inp)
    out["loss_target"] = _jax.random.normal(_jax.random.fold_in(key, 0), shape, _jnp.float32)
    for i, name in enumerate(TWIN_WEIGHTS):
        w = inp[name].astype(_jnp.float32)
        if MOMENT_SCALE is None:
            s = _jnp.sqrt(_jnp.mean(_jnp.square(w)) + 1e-30)
        else:
            s = MOMENT_SCALE[name]
        km, kv = _jax.random.split(_jax.random.fold_in(key, i + 1))
        out[name] = w
        out["m_" + name] = s * _jax.random.normal(km, w.shape, _jnp.float32)
        out["v_" + name] = (s * s) * _jax.random.uniform(kv, w.shape, _jnp.float32, 0.5, 1.5)
    if N_MICROBATCH > 1:
        for name, axis in PER_EXAMPLE_BATCH_AXIS.items():
            out[name] = _to_microbatches(out[name], axis)
    return {'x': out['x'], 'norm_mix': out['norm_mix'], 'norm_ffn': out['norm_ffn'], 'w_ffn_in': out['w_ffn_in'], 'w_ffn_out': out['w_ffn_out'], 'w_rec_in': out['w_rec_in'], 'conv_w': out['conv_w'], 'conv_b': out['conv_b'], 'w_lru_gates': out['w_lru_gates'], 'b_lru_gates': out['b_lru_gates'], 'lru_param': out['lru_param'], 'w_rec_out': out['w_rec_out'], 'norm_kv': out['norm_kv'], 'w_kvf': out['w_kvf'], 'b_forget': out['b_forget'], 'w_q': out['w_q'], 'w_o': out['w_o'], 'norm_final': out['norm_final'], 'loss_target': out['loss_target'], 'm_norm_mix': out['m_norm_mix'], 'm_norm_ffn': out['m_norm_ffn'], 'm_w_ffn_in': out['m_w_ffn_in'], 'm_w_ffn_out': out['m_w_ffn_out'], 'm_w_rec_in': out['m_w_rec_in'], 'm_conv_w': out['m_conv_w'], 'm_conv_b': out['m_conv_b'], 'm_w_lru_gates': out['m_w_lru_gates'], 'm_b_lru_gates': out['m_b_lru_gates'], 'm_lru_param': out['m_lru_param'], 'm_w_rec_out': out['m_w_rec_out'], 'm_norm_kv': out['m_norm_kv'], 'm_w_kvf': out['m_w_kvf'], 'm_b_forget': out['m_b_forget'], 'm_w_q': out['m_w_q'], 'm_w_o': out['m_w_o'], 'm_norm_final': out['m_norm_final'], 'v_norm_mix': out['v_norm_mix'], 'v_norm_ffn': out['v_norm_ffn'], 'v_w_ffn_in': out['v_w_ffn_in'], 'v_w_ffn_out': out['v_w_ffn_out'], 'v_w_rec_in': out['v_w_rec_in'], 'v_conv_w': out['v_conv_w'], 'v_conv_b': out['v_conv_b'], 'v_w_lru_gates': out['v_w_lru_gates'], 'v_b_lru_gates': out['v_b_lru_gates'], 'v_lru_param': out['v_lru_param'], 'v_w_rec_out': out['v_w_rec_out'], 'v_norm_kv': out['v_norm_kv'], 'v_w_kvf': out['v_w_kvf'], 'v_b_forget': out['v_b_forget'], 'v_w_q': out['v_w_q'], 'v_w_o': out['v_w_o'], 'v_norm_final': out['v_norm_final']}


def _loss(weights, diff, rest, loss_target):
    with _jax.named_scope("forward"):
        args = {**rest, TWIN_DIFF_INPUT: diff, **{k: w.astype(_WEIGHT_DTYPES[k]) for k, w in weights.items()}}
        y = _forward(args)
    with _jax.named_scope("loss_head"):
        err = _jnp.square(y.astype(_jnp.float32) - loss_target)
        return 0.5 * _jnp.sum(_jnp.mean(err, axis=-1)) if err.ndim else 0.5 * err


def _adamw(w, g, m, v):
    m = ADAM_B1 * m + (1.0 - ADAM_B1) * g
    v = ADAM_B2 * v + (1.0 - ADAM_B2) * _jnp.square(g)
    m_hat = m / (1.0 - ADAM_B1 ** ADAM_STEP)
    v_hat = v / (1.0 - ADAM_B2 ** ADAM_STEP)
    delta = -ADAM_LR * (m_hat / (_jnp.sqrt(v_hat) + ADAM_EPS) + ADAM_WD * w)
    return delta, m, v


def reference(x, norm_mix, norm_ffn, w_ffn_in, w_ffn_out, w_rec_in, conv_w, conv_b, w_lru_gates, b_lru_gates, lru_param, w_rec_out, norm_kv, w_kvf, b_forget, w_q, w_o, norm_final, loss_target, m_norm_mix, m_norm_ffn, m_w_ffn_in, m_w_ffn_out, m_w_rec_in, m_conv_w, m_conv_b, m_w_lru_gates, m_b_lru_gates, m_lru_param, m_w_rec_out, m_norm_kv, m_w_kvf, m_b_forget, m_w_q, m_w_o, m_norm_final, v_norm_mix, v_norm_ffn, v_w_ffn_in, v_w_ffn_out, v_w_rec_in, v_conv_w, v_conv_b, v_w_lru_gates, v_b_lru_gates, v_lru_param, v_w_rec_out, v_norm_kv, v_w_kvf, v_b_forget, v_w_q, v_w_o, v_norm_final):
    given = dict(x=x, norm_mix=norm_mix, norm_ffn=norm_ffn, w_ffn_in=w_ffn_in, w_ffn_out=w_ffn_out, w_rec_in=w_rec_in, conv_w=conv_w, conv_b=conv_b, w_lru_gates=w_lru_gates, b_lru_gates=b_lru_gates, lru_param=lru_param, w_rec_out=w_rec_out, norm_kv=norm_kv, w_kvf=w_kvf, b_forget=b_forget, w_q=w_q, w_o=w_o, norm_final=norm_final, loss_target=loss_target, m_norm_mix=m_norm_mix, m_norm_ffn=m_norm_ffn, m_w_ffn_in=m_w_ffn_in, m_w_ffn_out=m_w_ffn_out, m_w_rec_in=m_w_rec_in, m_conv_w=m_conv_w, m_conv_b=m_conv_b, m_w_lru_gates=m_w_lru_gates, m_b_lru_gates=m_b_lru_gates, m_lru_param=m_lru_param, m_w_rec_out=m_w_rec_out, m_norm_kv=m_norm_kv, m_w_kvf=m_w_kvf, m_b_forget=m_b_forget, m_w_q=m_w_q, m_w_o=m_w_o, m_norm_final=m_norm_final, v_norm_mix=v_norm_mix, v_norm_ffn=v_norm_ffn, v_w_ffn_in=v_w_ffn_in, v_w_ffn_out=v_w_ffn_out, v_w_rec_in=v_w_rec_in, v_conv_w=v_conv_w, v_conv_b=v_conv_b, v_w_lru_gates=v_w_lru_gates, v_b_lru_gates=v_b_lru_gates, v_lru_param=v_lru_param, v_w_rec_out=v_w_rec_out, v_norm_kv=v_norm_kv, v_w_kvf=v_w_kvf, v_b_forget=v_b_forget, v_w_q=v_w_q, v_w_o=v_w_o, v_norm_final=v_norm_final)
    weights = {n: given[n] for n in TWIN_WEIGHTS}
    shared = {n: given[n] for n in SHARED_INPUTS}
    per_example = {n: given[n] for n in ['x']}
    grad_fn = _jax.value_and_grad(_loss, argnums=(0, 1))

    def one_microbatch(ex, loss_target):
        ex = dict(ex)
        diff = ex.pop(TWIN_DIFF_INPUT)
        return grad_fn(weights, diff, {**shared, **ex}, loss_target)

    if N_MICROBATCH == 1:
        loss, (grad_w, grad_x) = one_microbatch(per_example, given["loss_target"])
    else:
        def body(carry, xs):
            loss_sum, grad_sum = carry
            l_k, (gw_k, gx_k) = one_microbatch(xs[0], xs[1])
            with _jax.named_scope("update"):
                return (loss_sum + l_k, _jax.tree.map(_jnp.add, grad_sum, gw_k)), gx_k

        init = (_jnp.zeros((), _jnp.float32), _jax.tree.map(_jnp.zeros_like, weights))
        (loss, grad_w), grad_x = _jax.lax.scan(body, init, (per_example, given["loss_target"]))
    with _jax.named_scope("update"):
        delta_w, new_m, new_v = {}, {}, {}
        for n in TWIN_WEIGHTS:
            delta_w[n], new_m[n], new_v[n] = _adamw(weights[n], grad_w[n], given["m_" + n], given["v_" + n])
    return (loss, grad_x, *[grad_w[n] for n in TWIN_WEIGHTS], *[delta_w[n] for n in TWIN_WEIGHTS],
            *[new_m[n] for n in TWIN_WEIGHTS], *[new_v[n] for n in TWIN_WEIGHTS])
```

```python
import functools
import math

import jax
import jax.numpy as jnp
from jax import lax
from jax.experimental import pallas as pl
from jax.experimental.pallas import tpu as pltpu

F32 = jnp.float32
BF16 = jnp.bfloat16

EPS = 1e-6
LRU_C = 8.0
HEAD_DIM = 64
LANES = 128
SUBLANES = 8
VMEM_LIMIT = 48 * 1024 * 1024
N_CHIPS = 4
N_DEV = 8

ADAM_LR = 0.001
ADAM_B1 = 0.9
ADAM_B2 = 0.999
ADAM_EPS = 1e-08
ADAM_WD = 0.01
ADAM_STEP = 10

_NN = (((1,), (0,)), ((), ()))
_NT = (((1,), (1,)), ((), ()))
_TN = (((0,), (0,)), ((), ()))
_DN = {"nn": _NN, "nt": _NT, "tn": _TN}
MESH = pl.DeviceIdType.MESH


def _params(sem):
    return pltpu.CompilerParams(dimension_semantics=sem, vmem_limit_bytes=VMEM_LIMIT)


def _tile(n, want):
    if n <= want:
        return n
    t = (want // LANES) * LANES
    while t >= LANES:
        if n % t == 0:
            return t
        t -= LANES
    return n


def _sigmoid(x):
    return 1.0 / (1.0 + jnp.exp(-x))


def _softplus(x):
    return jnp.maximum(x, 0.0) + jnp.log(1.0 + jnp.exp(-jnp.abs(x)))


_GELU_C = math.sqrt(2.0 / math.pi)


def _gelu_and_grad(x):
    inner = _GELU_C * (x + 0.044715 * x * x * x)
    t = jnp.tanh(inner)
    g = 0.5 * x * (1.0 + t)
    dg = 0.5 * (1.0 + t) + 0.5 * x * (1.0 - t * t) * _GELU_C * (1.0 + 3.0 * 0.044715 * x * x)
    return g, dg


def _mm(name, mode, a, b, *, grid, a_spec, b_spec, out_shape, out_dtype, out_spec, nk=1,
        res=None, res_spec=None, bias=None, bias_spec=None, scale=None):
    dn = _DN[mode]
    has_res, has_bias = res is not None, bias is not None
    blk = tuple(d for d in out_spec.block_shape if d is not None)

    def body(*refs):
        a_ref, b_ref = refs[0], refs[1]
        p = 2
        r_ref = refs[p] if has_res else None
        p += int(has_res)
        bias_ref = refs[p] if has_bias else None
        p += int(has_bias)
        o_ref = refs[p]
        part = lax.dot_general(a_ref[...], b_ref[...], dn, preferred_element_type=F32)

        def finish(acc):
            if scale is not None:
                acc = acc * scale
            if has_bias:
                acc = acc + bias_ref[...]
            if has_res:
                acc = r_ref[...] + acc
            o_ref[...] = acc.astype(o_ref.dtype)

        if nk == 1:
            finish(part)
        else:
            acc_ref = refs[p + 1]
            k = pl.program_id(2)

            @pl.when(k == 0)
            def _():
                acc_ref[...] = part

            @pl.when(k > 0)
            def _():
                acc_ref[...] += part

            @pl.when(k == nk - 1)
            def _():
                finish(acc_ref[...])

    ins, specs = [a, b], [a_spec, b_spec]
    if has_res:
        ins.append(res)
        specs.append(res_spec)
    if has_bias:
        ins.append(bias)
        specs.append(bias_spec)
    sem = ("parallel", "parallel") + (("arbitrary",) if len(grid) == 3 else ())
    return pl.pallas_call(
        body, name=name, grid=grid, in_specs=specs, out_specs=out_spec,
        out_shape=jax.ShapeDtypeStruct(out_shape, out_dtype),
        scratch_shapes=[pltpu.VMEM(blk, F32)] if nk > 1 else [],
        compiler_params=_params(sem),
    )(*ins)


def _mm_nn(name, a, b, *, b_lead=(), out_dtype, tm=512, tn=512, res=None, bias=None, scale=None):
    M, K = a.shape
    N = b.shape[-1]
    tm, tn = _tile(M, tm), _tile(N, tn)
    nl = len(b_lead)
    return _mm(
        name, "nn", a, b, grid=(M // tm, N // tn),
        a_spec=pl.BlockSpec((tm, K), lambda i, j: (i, 0)),
        b_spec=pl.BlockSpec((None,) * nl + (K, tn), lambda i, j: tuple(b_lead) + (0, j)),
        out_shape=(M, N), out_dtype=out_dtype, out_spec=pl.BlockSpec((tm, tn), lambda i, j: (i, j)),
        res=res, res_spec=pl.BlockSpec((tm, tn), lambda i, j: (i, j)),
        bias=bias, bias_spec=pl.BlockSpec((1, tn), lambda i, j: (0, j)), scale=scale)


def _mm_nt(name, a, b, *, b_lead=(), out_dtype, tm=512, tn=512, tk=2048):
    M, K = a.shape
    N = b.shape[-2]
    tm, tn, tk = _tile(M, tm), _tile(N, tn), _tile(K, tk)
    nk = K // tk
    nl = len(b_lead)
    return _mm(
        name, "nt", a, b, grid=(M // tm, N // tn, nk), nk=nk,
        a_spec=pl.BlockSpec((tm, tk), lambda i, j, k: (i, k)),
        b_spec=pl.BlockSpec((None,) * nl + (tn, tk), lambda i, j, k: tuple(b_lead) + (j, k)),
        out_shape=(M, N), out_dtype=out_dtype, out_spec=pl.BlockSpec((tm, tn), lambda i, j, k: (i, j)))


def _mm_tn(name, a, b, *, out_dtype, tm=512, tn=512):
    S, M = a.shape
    N = b.shape[1]
    tm, tn = _tile(M, tm), _tile(N, tn)
    return _mm(
        name, "tn", a, b, grid=(M // tm, N // tn),
        a_spec=pl.BlockSpec((S, tm), lambda i, j: (0, i)),
        b_spec=pl.BlockSpec((S, tn), lambda i, j: (0, j)),
        out_shape=(M, N), out_dtype=out_dtype, out_spec=pl.BlockSpec((tm, tn), lambda i, j: (i, j)))


def _rmsnorm_fwd(name, h, g, tr=256):
    S, D = h.shape
    tr = _tile(S, tr)

    def body(h_ref, g_ref, o_ref):
        x = h_ref[...]
        r = lax.rsqrt(jnp.mean(x * x, axis=-1, keepdims=True) + EPS)
        o_ref[...] = (x * r * g_ref[...]).astype(o_ref.dtype)

    return pl.pallas_call(
        body, name=name, grid=(S // tr,),
        in_specs=[pl.BlockSpec((tr, D), lambda i: (i, 0)), pl.BlockSpec((1, D), lambda i: (0, 0))],
        out_specs=pl.BlockSpec((tr, D), lambda i: (i, 0)),
        out_shape=jax.ShapeDtypeStruct((S, D), BF16),
        compiler_params=_params(("parallel",)),
    )(h, g)


def _rmsnorm_bwd(name, dxn, h, g, dh_in, tr=256):
    S, D = h.shape
    tr = _tile(S, tr)

    def body(dxn_ref, h_ref, g_ref, dh_ref, o_ref, ob_ref, dg_ref):
        i = pl.program_id(0)
        x = h_ref[...]
        dy = dxn_ref[...].astype(F32)
        r = lax.rsqrt(jnp.mean(x * x, axis=-1, keepdims=True) + EPS)
        xr = x * r
        dyg = dy * g_ref[...]
        dx = r * dyg - xr * (r * jnp.mean(dyg * xr, axis=-1, keepdims=True))
        out = dh_ref[...] + dx
        o_ref[...] = out
        ob_ref[...] = out.astype(BF16)
        part = jnp.sum(dy * xr, axis=0, keepdims=True)

        @pl.when(i == 0)
        def _():
            dg_ref[...] = part

        @pl.when(i > 0)
        def _():
            dg_ref[...] += part

    row = pl.BlockSpec((tr, D), lambda i: (i, 0))
    vec = pl.BlockSpec((1, D), lambda i: (0, 0))
    return pl.pallas_call(
        body, name=name, grid=(S // tr,),
        in_specs=[row, row, vec, row], out_specs=[row, row, vec],
        out_shape=[jax.ShapeDtypeStruct((S, D), F32), jax.ShapeDtypeStruct((S, D), BF16),
                   jax.ShapeDtypeStruct((1, D), F32)],
        compiler_params=_params(("arbitrary",)),
    )(dxn, h, g, dh_in)


def _loss_head(name, h, target, g, tr=256):
    S, D = h.shape
    tr = _tile(S, tr)

    def body(h_ref, t_ref, g_ref, o_ref, ob_ref, dg_ref, loss_ref):
        i = pl.program_id(0)
        x = h_ref[...]
        gg = g_ref[...]
        r = lax.rsqrt(jnp.mean(x * x, axis=-1, keepdims=True) + EPS)
        xr = x * r
        err = xr * gg - t_ref[...]
        lpart = 0.5 * jnp.sum(jnp.mean(err * err, axis=-1, keepdims=True), axis=0, keepdims=True)
        dy = err * (1.0 / D)
        dyg = dy * gg
        dx = r * dyg - xr * (r * jnp.mean(dyg * xr, axis=-1, keepdims=True))
        o_ref[...] = dx
        ob_ref[...] = dx.astype(BF16)
        part = jnp.sum(dy * xr, axis=0, keepdims=True)
        lrow = jnp.broadcast_to(lpart, (1, LANES))

        @pl.when(i == 0)
        def _():
            dg_ref[...] = part
            loss_ref[...] = lrow

        @pl.when(i > 0)
        def _():
            dg_ref[...] += part
            loss_ref[...] += lrow

    row = pl.BlockSpec((tr, D), lambda i: (i, 0))
    vec = pl.BlockSpec((1, D), lambda i: (0, 0))
    return pl.pallas_call(
        body, name=name, grid=(S // tr,),
        in_specs=[row, row, vec], out_specs=[row, row, vec, pl.BlockSpec((1, LANES), lambda i: (0, 0))],
        out_shape=[jax.ShapeDtypeStruct((S, D), F32), jax.ShapeDtypeStruct((S, D), BF16),
                   jax.ShapeDtypeStruct((1, D), F32), jax.ShapeDtypeStruct((1, LANES), F32)],
        compiler_params=_params(("arbitrary",)),
    )(h, target, g)


def _swiglu_fwd(name, z3, tr=256, tc=1408):
    _, S, F = z3.shape
    tr, tc = _tile(S, tr), _tile(F, tc)

    def body(z_ref, a_ref):
        zg = z_ref[0].astype(F32)
        zu = z_ref[1].astype(F32)
        a_ref[...] = (zg * _sigmoid(zg) * zu).astype(a_ref.dtype)

    return pl.pallas_call(
        body, name=name, grid=(S // tr, F // tc),
        in_specs=[pl.BlockSpec((2, tr, tc), lambda i, j: (0, i, j))],
        out_specs=pl.BlockSpec((tr, tc), lambda i, j: (i, j)),
        out_shape=jax.ShapeDtypeStruct((S, F), BF16),
        compiler_params=_params(("parallel", "parallel")),
    )(z3)


def _swiglu_bwd(name, da, z3, tr=256, tc=1408):
    _, S, F = z3.shape
    tr, tc = _tile(S, tr), _tile(F, tc)

    def body(da_ref, z_ref, dz_ref):
        zg = z_ref[0].astype(F32)
        zu = z_ref[1].astype(F32)
        d = da_ref[...].astype(F32)
        sg = _sigmoid(zg)
        silu = zg * sg
        dz_ref[0] = (d * zu * (sg * (1.0 + zg * (1.0 - sg)))).astype(dz_ref.dtype)
        dz_ref[1] = (d * silu).astype(dz_ref.dtype)

    return pl.pallas_call(
        body, name=name, grid=(S // tr, F // tc),
        in_specs=[pl.BlockSpec((tr, tc), lambda i, j: (i, j)),
                  pl.BlockSpec((2, tr, tc), lambda i, j: (0, i, j))],
        out_specs=pl.BlockSpec((2, tr, tc), lambda i, j: (0, i, j)),
        out_shape=jax.ShapeDtypeStruct((2, S, F), BF16),
        compiler_params=_params(("parallel", "parallel")),
    )(da, z3)


SCAN_ROWS = 64


def _group_scan(A, B, reverse):
    n = A.shape[0]
    sub = lax.broadcasted_iota(jnp.int32, A.shape, 0) % SUBLANES
    for d in (1, 2, 4):
        if reverse:
            A_sh, B_sh = pltpu.roll(A, n - d, 0), pltpu.roll(B, n - d, 0)
            keep = sub < SUBLANES - d
        else:
            A_sh, B_sh = pltpu.roll(A, d, 0), pltpu.roll(B, d, 0)
            keep = sub >= d
        B = jnp.where(keep, A * B_sh + B, B)
        A = jnp.where(keep, A * A_sh, A)
    return A, B


def _block_scan(a, u, carry, reverse):
    A, B = _group_scan(a, u, reverse)
    ng = a.shape[0] // SUBLANES
    out = [None] * ng
    order = range(ng - 1, -1, -1) if reverse else range(ng)
    for gi in order:
        sl = slice(gi * SUBLANES, (gi + 1) * SUBLANES)
        hg = A[sl] * carry + B[sl]
        out[gi] = hg
        carry = hg[0:1] if reverse else hg[SUBLANES - 1:SUBLANES]
    return jnp.concatenate(out, axis=0), carry


def _lru_gates(rc, gip, grp, sp):
    gi = _sigmoid(gip)
    gr = _sigmoid(grp)
    la = -LRU_C * gr * sp
    a = jnp.exp(la)
    om = -jnp.tanh(la) * (a * a + 1.0)
    mult = jnp.sqrt(om)
    return gi, gr, a, mult


def _lru_fwd(name, proj, rc, gip, grp, lru_p, tc=256):
    S, C = rc.shape
    tc = _tile(C, tc)
    nb = S // SCAN_ROWS

    def body(gb_ref, rc_ref, gi_ref, gr_ref, l_ref, h_ref, m_ref):
        sp = _softplus(-l_ref[...])

        def step(b, carry):
            rows = pl.ds(pl.multiple_of(b * SCAN_ROWS, SCAN_ROWS), SCAN_ROWS)
            rcb = rc_ref[rows, :]
            gi, _, a, mult = _lru_gates(rcb, gi_ref[rows, :], gr_ref[rows, :], sp)
            h, carry = _block_scan(a, rcb * gi * mult, carry, False)
            h_ref[rows, :] = h
            gel, _ = _gelu_and_grad(gb_ref[rows, :])
            m_ref[rows, :] = (gel * h).astype(m_ref.dtype)
            return carry

        lax.fori_loop(0, nb, step, jnp.zeros((1, tc), F32))

    col = pl.BlockSpec((S, tc), lambda j: (0, j))
    return pl.pallas_call(
        body, name=name, grid=(C // tc,),
        in_specs=[col, col, col, col, pl.BlockSpec((1, tc), lambda j: (0, j))],
        out_specs=[col, col],
        out_shape=[jax.ShapeDtypeStruct((S, C), F32), jax.ShapeDtypeStruct((S, C), BF16)],
        compiler_params=_params(("parallel",)),
    )(proj, rc, gip, grp, lru_p)


def _lru_bwd(name, dm, proj, hrec, rc, gip, grp, lru_p, tc=256):
    S, C = rc.shape
    tc = _tile(C, tc)
    nb = S // SCAN_ROWS
    R = SCAN_ROWS

    def body(dm_ref, gb_ref, h_ref, rc_ref, gi_ref, gr_ref, l_ref,
             dgb_ref, dgi_ref, dgr_ref, drc_ref, dbi_ref, dbr_ref, dl_ref):
        lp = l_ref[...]
        sp = _softplus(-lp)
        row = lax.broadcasted_iota(jnp.int32, (R, tc), 0)
        zero = jnp.zeros((1, tc), F32)

        def step(t, carry):
            mu_in, s_i, s_r, s_sp = carry
            b = nb - 1 - t
            r0 = pl.multiple_of(b * R, R)
            rows = pl.ds(r0, R)
            rcb = rc_ref[rows, :]
            gi, gr, a, mult = _lru_gates(rcb, gi_ref[rows, :], gr_ref[rows, :], sp)
            gel, dgel = _gelu_and_grad(gb_ref[rows, :])
            dmb = dm_ref[rows, :]
            h = h_ref[rows, :]
            dgb_ref[rows, :] = (dmb * h * dgel).astype(dgb_ref.dtype)
            dh = dmb * gel
            mu, mu_out = _block_scan(a, a * dh, mu_in, True)
            mu_next = jnp.where(row == R - 1, mu_in, pltpu.roll(mu, R - 1, 0))
            lam = dh + mu_next
            p0 = pl.multiple_of(jnp.maximum(r0 - SUBLANES, 0), SUBLANES)
            prev = h_ref[pl.ds(p0, SUBLANES), :][SUBLANES - 1:SUBLANES]
            prev = jnp.where(b > 0, prev, 0.0)
            h_prev = jnp.where(row == 0, prev, pltpu.roll(h, 1, 0))
            da = lam * h_prev
            d_mult = lam * rcb * gi
            d_la = da * a - d_mult * (a * a) / mult
            d_grp = d_la * (-LRU_C * sp) * gr * (1.0 - gr)
            d_gip = lam * rcb * mult * gi * (1.0 - gi)
            dgr_ref[rows, :] = d_grp.astype(dgr_ref.dtype)
            dgi_ref[rows, :] = d_gip.astype(dgi_ref.dtype)
            drc_ref[rows, :] = lam * gi * mult
            s_i = s_i + jnp.sum(d_gip, axis=0, keepdims=True)
            s_r = s_r + jnp.sum(d_grp, axis=0, keepdims=True)
            s_sp = s_sp + jnp.sum(d_la * gr, axis=0, keepdims=True)
            return mu_out, s_i, s_r, s_sp

        _, s_i, s_r, s_sp = lax.fori_loop(0, nb, step, (zero, zero, zero, zero))
        dbi_ref[...] = s_i
        dbr_ref[...] = s_r
        dl_ref[...] = (-LRU_C * s_sp) * (-_sigmoid(-lp))

    col = pl.BlockSpec((S, tc), lambda j: (0, j))
    vec = pl.BlockSpec((1, tc), lambda j: (0, j))
    return pl.pallas_call(
        body, name=name, grid=(C // tc,),
        in_specs=[col, col, col, col, col, col, vec],
        out_specs=[col, col, col, col, vec, vec, vec],
        out_shape=[jax.ShapeDtypeStruct((S, C), BF16), jax.ShapeDtypeStruct((S, C), BF16),
                   jax.ShapeDtypeStruct((S, C), BF16), jax.ShapeDtypeStruct((S, C), F32),
                   jax.ShapeDtypeStruct((1, C), F32), jax.ShapeDtypeStruct((1, C), F32),
                   jax.ShapeDtypeStruct((1, C), F32)],
        compiler_params=_params(("parallel",)),
    )(dm, proj, hrec, rc, gip, grp, lru_p)


def _cumsum_rows(name, u, reverse):
    S, C = u.shape
    nb = S // SCAN_ROWS

    def body(u_ref, o_ref):
        def step(t, carry):
            b = nb - 1 - t if reverse else t
            rows = pl.ds(pl.multiple_of(b * SCAN_ROWS, SCAN_ROWS), SCAN_ROWS)
            ub = u_ref[rows, :]
            h, carry = _block_scan(jnp.ones_like(ub), ub, carry, reverse)
            o_ref[rows, :] = h
            return carry

        lax.fori_loop(0, nb, step, jnp.zeros((1, C), F32))

    spec = pl.BlockSpec((S, C), lambda i: (0, 0))
    return pl.pallas_call(
        body, name=name, grid=(1,), in_specs=[spec], out_specs=spec,
        out_shape=jax.ShapeDtypeStruct((S, C), F32),
        compiler_params=_params(("arbitrary",)),
    )(u)


def _shift_down(x, k):
    row = lax.broadcasted_iota(jnp.int32, x.shape, 0)
    return jnp.where(row >= k, pltpu.roll(x, k, 0), 0.0)


def _shift_up(x, k):
    n = x.shape[0]
    row = lax.broadcasted_iota(jnp.int32, x.shape, 0)
    return jnp.where(row < n - k, pltpu.roll(x, n - k, 0), 0.0)


def _conv_fwd(name, proj, w, b, tc=256):
    S, C2 = proj.shape
    C = C2 // 2
    tc = _tile(C, tc)
    off = C // tc

    def body(x_ref, w_ref, b_ref, o_ref, ob_ref):
        x = x_ref[...]
        out = b_ref[...] + w_ref[3:4, :] * x
        for k in (1, 2, 3):
            out = out + w_ref[3 - k:4 - k, :] * _shift_down(x, k)
        o_ref[...] = out
        ob_ref[...] = out.astype(BF16)

    col = pl.BlockSpec((S, tc), lambda j: (0, j))
    return pl.pallas_call(
        body, name=name, grid=(C // tc,),
        in_specs=[pl.BlockSpec((S, tc), lambda j: (0, off + j)),
                  pl.BlockSpec((4, tc), lambda j: (0, j)), pl.BlockSpec((1, tc), lambda j: (0, j))],
        out_specs=[col, col],
        out_shape=[jax.ShapeDtypeStruct((S, C), F32), jax.ShapeDtypeStruct((S, C), BF16)],
        compiler_params=_params(("parallel",)),
    )(proj, w, b)


def _conv_bwd(name, drc, proj, w, tc=256):
    S, C = drc.shape
    tc = _tile(C, tc)
    off = C // tc

    def body(y_ref, x_ref, w_ref, dx_ref, dw_ref, db_ref):
        y = y_ref[...]
        x = x_ref[...]
        dx = w_ref[3:4, :] * y
        dw_ref[3:4, :] = jnp.sum(y * x, axis=0, keepdims=True)
        for k in (1, 2, 3):
            dx = dx + w_ref[3 - k:4 - k, :] * _shift_up(y, k)
            dw_ref[3 - k:4 - k, :] = jnp.sum(y * _shift_down(x, k), axis=0, keepdims=True)
        dx_ref[...] = dx.astype(dx_ref.dtype)
        db_ref[...] = jnp.sum(y, axis=0, keepdims=True)

    col = pl.BlockSpec((S, tc), lambda j: (0, j))
    return pl.pallas_call(
        body, name=name, grid=(C // tc,),
        in_specs=[col, pl.BlockSpec((S, tc), lambda j: (0, off + j)), pl.BlockSpec((4, tc), lambda j: (0, j))],
        out_specs=[col, pl.BlockSpec((4, tc), lambda j: (0, j)), pl.BlockSpec((1, tc), lambda j: (0, j))],
        out_shape=[jax.ShapeDtypeStruct((S, C), BF16), jax.ShapeDtypeStruct((4, C), F32),
                   jax.ShapeDtypeStruct((1, C), F32)],
        compiler_params=_params(("parallel",)),
    )(drc, proj, w)


def _gates_fwd(name, rcb, wg, bg):
    S, C = rcb.shape
    nblk, bw, _ = wg.shape

    def body(x_ref, w_ref, b_ref, gi_ref, gr_ref):
        g = jnp.dot(x_ref[...], w_ref[...], preferred_element_type=F32) + b_ref[...]
        gi_ref[...] = g[:, :bw]
        gr_ref[...] = g[:, bw:]

    col = pl.BlockSpec((S, bw), lambda n: (0, n))
    return pl.pallas_call(
        body, name=name, grid=(nblk,),
        in_specs=[col, pl.BlockSpec((None, bw, 2 * bw), lambda n: (n, 0, 0)),
                  pl.BlockSpec((None, 1, 2 * bw), lambda n: (n, 0, 0))],
        out_specs=[col, col],
        out_shape=[jax.ShapeDtypeStruct((S, C), F32), jax.ShapeDtypeStruct((S, C), F32)],
        compiler_params=_params(("parallel",)),
    )(rcb, wg, bg)


def _gates_bwd(name, dgi, dgr, rcb, wg, drc1):
    S, C = rcb.shape
    nblk, bw, _ = wg.shape

    def body(dgi_ref, dgr_ref, x_ref, w_ref, d1_ref, drc_ref, dw_ref):
        w = w_ref[...]
        x = x_ref[...]
        di, dr = dgi_ref[...], dgr_ref[...]
        drc_ref[...] = (d1_ref[...]
                        + lax.dot_general(di, w[:, :bw], _NT, preferred_element_type=F32)
                        + lax.dot_general(dr, w[:, bw:], _NT, preferred_element_type=F32))
        dw_ref[:, :bw] = lax.dot_general(x, di, _TN, preferred_element_type=F32).astype(dw_ref.dtype)
        dw_ref[:, bw:] = lax.dot_general(x, dr, _TN, preferred_element_type=F32).astype(dw_ref.dtype)

    col = pl.BlockSpec((S, bw), lambda n: (0, n))
    wspec = pl.BlockSpec((None, bw, 2 * bw), lambda n: (n, 0, 0))
    return pl.pallas_call(
        body, name=name, grid=(nblk,),
        in_specs=[col, col, col, wspec, col], out_specs=[col, wspec],
        out_shape=[jax.ShapeDtypeStruct((S, C), F32), jax.ShapeDtypeStruct((nblk, bw, 2 * bw), BF16)],
        compiler_params=_params(("parallel",)),
    )(dgi, dgr, rcb, wg, drc1)


def _att_tile(S):
    return 256 if S % 256 == 0 else 128


def _causal(T):
    r = lax.broadcasted_iota(jnp.int32, (T, T), 0)
    c = lax.broadcasted_iota(jnp.int32, (T, T), 1)
    return r >= c


def _attn_fwd(name, q, kv, negc3):
    S, D = q.shape
    HP = D // LANES
    T = _att_tile(S)
    nq = S // T

    def body(q_ref, k_ref, v_ref, nc_ref, o_ref, of_ref, lse_ref):
        is0 = lax.broadcasted_iota(jnp.int32, (T, LANES), 1) < HEAD_DIM
        tri = _causal(T)

        def q_step(qi, _):
            rows = pl.ds(pl.multiple_of(qi * T, T), T)
            qf = q_ref[rows, :].astype(F32)
            outs, lses = [], []
            for hh in range(2):
                qm = jnp.where(is0 if hh == 0 else jnp.logical_not(is0), qf, 0.0).astype(BF16)

                def tile(kj, carry, masked):
                    m, l, acc = carry
                    ks = pl.ds(pl.multiple_of(kj * T, T), T)
                    s = lax.dot_general(qm, k_ref[ks, :], _NT, preferred_element_type=F32)
                    s = s + nc_ref[hh:hh + 1, ks]
                    if masked:
                        s = jnp.where(tri, s, -jnp.inf)
                    m_new = jnp.maximum(m, jnp.max(s, axis=1, keepdims=True))
                    alpha = jnp.exp(m - m_new)
                    p = jnp.exp(s - m_new)
                    l = alpha * l + jnp.sum(p, axis=1, keepdims=True)
                    acc = alpha * acc + jnp.dot(p.astype(BF16), v_ref[ks, :], preferred_element_type=F32)
                    return m_new, l, acc

                init = (jnp.full((T, 1), -jnp.inf, F32), jnp.zeros((T, 1), F32), jnp.zeros((T, LANES), F32))
                carry = lax.fori_loop(0, qi, lambda kj, c: tile(kj, c, False), init)
                m, l, acc = tile(qi, carry, True)
                outs.append(acc / l)
                lses.append(jnp.broadcast_to(m + jnp.log(l), (T, LANES)))
            out = jnp.where(is0, outs[0], outs[1])
            o_ref[rows, :] = out.astype(o_ref.dtype)
            of_ref[rows, :] = out
            lse_ref[rows, :] = jnp.where(is0, lses[0], lses[1])
            return 0

        lax.fori_loop(0, nq, q_step, 0)

    return pl.pallas_call(
        body, name=name, grid=(HP,),
        in_specs=[pl.BlockSpec((S, LANES), lambda p: (0, p)),
                  pl.BlockSpec((S, LANES), lambda p: (0, p)),
                  pl.BlockSpec((S, LANES), lambda p: (0, HP + p)),
                  pl.BlockSpec((None, 2, S), lambda p: (p, 0, 0))],
        out_specs=[pl.BlockSpec((S, LANES), lambda p: (0, p))] * 3,
        out_shape=[jax.ShapeDtypeStruct((S, D), BF16), jax.ShapeDtypeStruct((S, D), F32),
                   jax.ShapeDtypeStruct((S, D), F32)],
        compiler_params=_params(("parallel",)),
    )(q, kv, kv, negc3)


def _attn_bwd(name, q, kv, negc3, o, do, lse):
    S, D = q.shape
    HP = D // LANES
    T = _att_tile(S)
    nq = S // T
    rep = T // LANES
    scale = HEAD_DIM ** -0.5

    def body(q_ref, k_ref, v_ref, nc_ref, o_ref, do_ref, lse_ref,
             dq_ref, dk_ref, dv_ref, dc_ref, dr_ref, dq_acc, lse_rep, dl_rep, dr_rep):
        is0 = lax.broadcasted_iota(jnp.int32, (T, LANES), 1) < HEAD_DIM
        tri = _causal(T)

        def prologue(qi, _):
            rows = pl.ds(pl.multiple_of(qi * T, T), T)
            prod = do_ref[rows, :].astype(F32) * o_ref[rows, :]
            lse_b = lse_ref[rows, :]
            for hh in range(2):
                msk = is0 if hh == 0 else jnp.logical_not(is0)
                dl = jnp.sum(jnp.where(msk, prod, 0.0), axis=1, keepdims=True)
                ls = jnp.max(jnp.where(msk, lse_b, -jnp.inf), axis=1, keepdims=True)
                dl_rep[hh, rows, :] = jnp.broadcast_to(dl, (T, LANES))
                lse_rep[hh, rows, :] = jnp.broadcast_to(ls, (T, LANES))
                dr_rep[hh, rows, :] = jnp.zeros((T, LANES), F32)
            dq_acc[rows, :] = jnp.zeros((T, LANES), F32)
            return 0

        lax.fori_loop(0, nq, prologue, 0)

        def kv_step(kj, _):
            ks = pl.ds(pl.multiple_of(kj * T, T), T)
            kf = k_ref[ks, :].astype(F32)
            vf = v_ref[ks, :].astype(F32)
            dks, dvs = [], []
            for hh in range(2):
                msk = is0 if hh == 0 else jnp.logical_not(is0)
                km = jnp.where(msk, kf, 0.0).astype(BF16)
                vm = jnp.where(msk, vf, 0.0).astype(BF16)
                ncr = nc_ref[hh:hh + 1, ks]

                def tile(qi, carry, masked):
                    dk_a, dv_a, dc_a = carry
                    rows = pl.ds(pl.multiple_of(qi * T, T), T)
                    qb = q_ref[rows, :]
                    dob = do_ref[rows, :]
                    s = lax.dot_general(qb, km, _NT, preferred_element_type=F32) + ncr
                    lse_t = jnp.tile(lse_rep[hh, rows, :], (1, rep))
                    dl_t = jnp.tile(dl_rep[hh, rows, :], (1, rep))
                    p = jnp.exp(s - lse_t)
                    if masked:
                        p = jnp.where(tri, p, 0.0)
                    dp = lax.dot_general(dob, vm, _NT, preferred_element_type=F32)
                    ds = p * (dp - dl_t)
                    pb, dsb = p.astype(BF16), ds.astype(BF16)
                    dv_a = dv_a + lax.dot_general(pb, dob, _TN, preferred_element_type=F32)
                    dk_a = dk_a + lax.dot_general(dsb, qb, _TN, preferred_element_type=F32)
                    dq_acc[rows, :] += jnp.dot(dsb, km, preferred_element_type=F32)
                    dc_a = dc_a + jnp.sum(ds, axis=0, keepdims=True)
                    dr_rep[hh, rows, :] += jnp.broadcast_to(jnp.sum(ds, axis=1, keepdims=True), (T, LANES))
                    return dk_a, dv_a, dc_a

                init = (jnp.zeros((T, LANES), F32), jnp.zeros((T, LANES), F32), jnp.zeros((1, T), F32))
                carry = tile(kj, init, True)
                dk_a, dv_a, dc_a = lax.fori_loop(kj + 1, nq, lambda qi, c: tile(qi, c, False), carry)
                dks.append(dk_a)
                dvs.append(dv_a)
                dc_ref[hh:hh + 1, ks] = -dc_a
            dk_ref[ks, :] = jnp.where(is0, dks[0], dks[1])
            dv_ref[ks, :] = jnp.where(is0, dvs[0], dvs[1])
            return 0

        lax.fori_loop(0, nq, kv_step, 0)
        dq_ref[...] = (dq_acc[...] * scale).astype(dq_ref.dtype)
        first = lax.broadcasted_iota(jnp.int32, (S, LANES), 1) < HEAD_DIM
        dr_ref[...] = jnp.where(first, dr_rep[0], dr_rep[1])

    blk = lambda off: pl.BlockSpec((S, LANES), lambda p: (0, off + p))
    nc_spec = pl.BlockSpec((None, 2, S), lambda p: (p, 0, 0))
    return pl.pallas_call(
        body, name=name, grid=(HP,),
        in_specs=[blk(0), blk(0), blk(HP), nc_spec, blk(0), blk(0), blk(0)],
        out_specs=[blk(0), blk(0), blk(0), nc_spec, blk(0)],
        out_shape=[jax.ShapeDtypeStruct((S, D), BF16), jax.ShapeDtypeStruct((S, D), F32),
                   jax.ShapeDtypeStruct((S, D), F32), jax.ShapeDtypeStruct((HP, 2, S), F32),
                   jax.ShapeDtypeStruct((S, D), F32)],
        scratch_shapes=[pltpu.VMEM((S, LANES), F32), pltpu.VMEM((2, S, LANES), F32),
                        pltpu.VMEM((2, S, LANES), F32), pltpu.VMEM((2, S, LANES), F32)],
        compiler_params=_params(("parallel",)),
    )(q, kv, kv, negc3, o, do, lse)


def _logsig_fwd(name, f):
    S, C = f.shape

    def body(f_ref, o_ref):
        o_ref[...] = -_softplus(-f_ref[...])

    spec = pl.BlockSpec((S, C), lambda i: (0, 0))
    return pl.pallas_call(body, name=name, grid=(1,), in_specs=[spec], out_specs=spec,
                          out_shape=jax.ShapeDtypeStruct((S, C), F32),
                          compiler_params=_params(("arbitrary",)))(f)


def _logsig_bwd(name, dls, f):
    S, C = f.shape

    def body(d_ref, f_ref, o_ref, s_ref):
        df = d_ref[...] * _sigmoid(-f_ref[...])
        o_ref[...] = df.astype(o_ref.dtype)
        s_ref[...] = jnp.sum(df, axis=0, keepdims=True)

    spec = pl.BlockSpec((S, C), lambda i: (0, 0))
    return pl.pallas_call(body, name=name, grid=(1,), in_specs=[spec, spec],
                          out_specs=[spec, pl.BlockSpec((1, C), lambda i: (0, 0))],
                          out_shape=[jax.ShapeDtypeStruct((S, C), BF16), jax.ShapeDtypeStruct((1, C), F32)],
                          compiler_params=_params(("arbitrary",)))(dls, f)


def _add_cast(name, parts, out_dtype, tr=256):
    S, C = parts[0].shape
    tr = _tile(S, tr)
    n = len(parts)

    def body(*refs):
        acc = refs[0][...].astype(F32)
        for r in refs[1:n]:
            acc = acc + r[...].astype(F32)
        refs[n][...] = acc.astype(out_dtype)

    spec = pl.BlockSpec((tr, C), lambda i: (i, 0))
    return pl.pallas_call(body, name=name, grid=(S // tr,), in_specs=[spec] * n, out_specs=spec,
                          out_shape=jax.ShapeDtypeStruct((S, C), out_dtype),
                          compiler_params=_params(("parallel",)))(*parts)


def _local_step(x, target, W):
    S, D = x.shape
    L = W["norm_mix"].shape[0]
    NA = W["w_rec_in"].shape[0]
    NB = L - NA
    C = W["w_rec_out"].shape[1]
    F = W["w_ffn_out"].shape[1]
    FH = F // 2
    CH = C // 2
    HP = D // LANES
    scale = HEAD_DIM ** -0.5
    saved = []
    h = x

    def ffn_fwd(l, h_mid):
        hn = _rmsnorm_fwd(f"ffn_norm_{l}", h_mid, W["norm_ffn"][l])
        z3 = _mm(f"ffn_in_{l}", "nn", hn, W["w_ffn_in"], grid=(S // _tile(S, 512), N_CHIPS),
                 a_spec=pl.BlockSpec((_tile(S, 512), D), lambda i, j: (i, 0)),
                 b_spec=pl.BlockSpec((None, None, D, FH), lambda i, j: (l, j, 0, 0)),
                 out_shape=(2, S, F), out_dtype=BF16,
                 out_spec=pl.BlockSpec((None, _tile(S, 512), FH), lambda i, j: (j // 2, i, j % 2)))
        act = _swiglu_fwd(f"swiglu_{l}", z3)
        h_out = _mm_nn(f"ffn_out_{l}", act, W["w_ffn_out"], b_lead=(l,), out_dtype=F32, res=h_mid, tn=D)
        return h_out, (hn, z3, act)

    kv = negc3 = f_pre = hn_kv = h_kv = None
    for l in range(L):
        xn = _rmsnorm_fwd(f"mix_norm_{l}", h, W["norm_mix"][l])
        if l < NA:
            proj = _mm(f"rec_in_{l}", "nn", xn, W["w_rec_in"], grid=(S // _tile(S, 512), N_CHIPS),
                       a_spec=pl.BlockSpec((_tile(S, 512), D), lambda i, j: (i, 0)),
                       b_spec=pl.BlockSpec((None, None, D, CH), lambda i, j: (l, j, 0, 0)),
                       out_shape=(S, 2 * C), out_dtype=F32,
                       out_spec=pl.BlockSpec((_tile(S, 512), CH), lambda i, j: (i, j)))
            rc, rcb = _conv_fwd(f"conv_{l}", proj, W["conv_w"][l], W["conv_b"][l])
            gip, grp = _gates_fwd(f"gates_{l}", rcb, W["w_gates"][l], W["b_gates"][l])
            hrec, m = _lru_fwd(f"lru_{l}", proj, rc, gip, grp, W["lru_param"][l])
            h_mid = _mm_nn(f"rec_out_{l}", m, W["w_rec_out"], b_lead=(l,), out_dtype=F32, res=h, tn=D)
            mix_saved = (xn, proj, rc, rcb, gip, grp, hrec, m)
        else:
            j = l - NA
            if j == 0:
                h_kv = h
                hn_kv = _rmsnorm_fwd("kv_norm", h, W["norm_kv"])
                kv = _mm_nn("kv_proj", hn_kv, W["w_kv"], out_dtype=BF16)
                f_pre = _mm_nn("f_proj", hn_kv, W["w_f"], out_dtype=F32, bias=W["b_f"])
                c = _cumsum_rows("c_cumsum", _logsig_fwd("logsig", f_pre), False)
                negc3 = (-c[:, :2 * HP]).T.reshape(HP, 2, S)
            q = _mm_nn(f"q_proj_{j}", xn, W["w_q"], b_lead=(j,), out_dtype=BF16, scale=scale)
            o, of, lse = _attn_fwd(f"attn_fwd_{j}", q, kv, negc3)
            h_mid = _mm_nn(f"o_proj_{j}", o, W["w_o"], b_lead=(j,), out_dtype=F32, res=h, tn=D)
            mix_saved = (xn, q, o, of, lse)
        h_out, ffn_saved = ffn_fwd(l, h_mid)
        saved.append((h, h_mid, mix_saved, ffn_saved))
        h = h_out

    dh, dhb, dg_final, loss_row = _loss_head("loss_head", h, target, W["norm_final"])

    G = {"norm_final": dg_final, "norm_mix": [None] * L, "norm_ffn": [None] * L,
         "w_ffn_in": [None] * L, "w_ffn_out": [None] * L,
         "w_rec_in": [None] * NA, "w_gates": [None] * NA, "w_rec_out": [None] * NA,
         "conv_w": [None] * NA, "conv_b": [None] * NA, "b_gi": [None] * NA, "b_gr": [None] * NA,
         "lru_param": [None] * NA, "w_q": [None] * NB, "w_o": [None] * NB}
    dk_parts, dv_parts, dc_parts = [], [], []
    tm = _tile(S, 512)
    td = _tile(D, 512)

    for l in reversed(range(L)):
        h_in, h_mid, mix_saved, (hn, z3, act) = saved[l]
        G["w_ffn_out"][l] = _mm_tn(f"d_ffn_out_{l}", act, dhb, out_dtype=BF16, tn=D)
        da = _mm_nt(f"d_act_{l}", dhb, W["w_ffn_out"], b_lead=(l,), out_dtype=BF16, tn=FH)
        dz3 = _swiglu_bwd(f"d_swiglu_{l}", da, z3)
        G["w_ffn_in"][l] = _mm(
            f"d_ffn_in_{l}", "tn", hn, dz3, grid=(D // td, N_CHIPS),
            a_spec=pl.BlockSpec((S, td), lambda i, j: (0, i)),
            b_spec=pl.BlockSpec((None, S, FH), lambda i, j: (j // 2, 0, j % 2)),
            out_shape=(N_CHIPS, D, FH), out_dtype=BF16,
            out_spec=pl.BlockSpec((None, td, FH), lambda i, j: (j, i, 0)))
        dhn = _mm(f"d_ffn_hn_{l}", "nt", dz3, W["w_ffn_in"], grid=(S // tm, 1, N_CHIPS), nk=N_CHIPS,
                  a_spec=pl.BlockSpec((None, tm, FH), lambda i, j, k: (k // 2, i, k % 2)),
                  b_spec=pl.BlockSpec((None, None, D, FH), lambda i, j, k: (l, k, 0, 0)),
                  out_shape=(S, D), out_dtype=F32, out_spec=pl.BlockSpec((tm, D), lambda i, j, k: (i, 0)))
        dh, dhb, G["norm_ffn"][l] = _rmsnorm_bwd(f"d_ffn_norm_{l}", dhn, h_mid, W["norm_ffn"][l], dh)
        if l < NA:
            xn, proj, rc, rcb, gip, grp, hrec, m = mix_saved
            G["w_rec_out"][l] = _mm_tn(f"d_rec_out_{l}", m, dhb, out_dtype=BF16, tn=D)
            dm = _mm_nt(f"d_m_{l}", dhb, W["w_rec_out"], b_lead=(l,), out_dtype=F32, tn=C)
            dgb, dgi, dgr, drc1, dbi, dbr, dlp = _lru_bwd(f"d_lru_{l}", dm, proj, hrec, rc, gip, grp, W["lru_param"][l])
            drc, G["w_gates"][l] = _gates_bwd(f"d_gates_{l}", dgi, dgr, rcb, W["w_gates"][l], drc1)
            drec, G["conv_w"][l], G["conv_b"][l] = _conv_bwd(f"d_conv_{l}", drc, proj, W["conv_w"][l])
            G["b_gi"][l], G["b_gr"][l], G["lru_param"][l] = dbi, dbr, dlp
            dproj = jnp.concatenate([dgb, drec], axis=1)
            G["w_rec_in"][l] = _mm(
                f"d_rec_in_{l}", "tn", xn, dproj, grid=(1, N_CHIPS),
                a_spec=pl.BlockSpec((S, D), lambda i, j: (0, 0)),
                b_spec=pl.BlockSpec((S, CH), lambda i, j: (0, j)),
                out_shape=(N_CHIPS, D, CH), out_dtype=BF16,
                out_spec=pl.BlockSpec((None, D, CH), lambda i, j: (j, 0, 0)))
            dxn = _mm(f"d_rec_xn_{l}", "nt", dproj, W["w_rec_in"], grid=(S // tm, 1, N_CHIPS), nk=N_CHIPS,
                      a_spec=pl.BlockSpec((tm, CH), lambda i, j, k: (i, k)),
                      b_spec=pl.BlockSpec((None, None, D, CH), lambda i, j, k: (l, k, 0, 0)),
                      out_shape=(S, D), out_dtype=F32, out_spec=pl.BlockSpec((tm, D), lambda i, j, k: (i, 0)))
        else:
            j = l - NA
            xn, q, o, of, lse = mix_saved
            G["w_o"][j] = _mm_tn(f"d_o_proj_{j}", o, dhb, out_dtype=BF16, tn=D)
            do = _mm_nt(f"d_o_{j}", dhb, W["w_o"], b_lead=(j,), out_dtype=BF16, tn=D)
            dq, dk, dv, dc3, dcr = _attn_bwd(f"attn_bwd_{j}", q, kv, negc3, of, do, lse)
            dk_parts.append(dk)
            dv_parts.append(dv)
            dc_parts.append(dc3.reshape(2 * HP, S).T + dcr[:, ::HEAD_DIM])
            G["w_q"][j] = _mm_tn(f"d_q_proj_{j}", xn, dq, out_dtype=BF16, tn=D)
            dxn = _mm_nt(f"d_q_xn_{j}", dq, W["w_q"], b_lead=(j,), out_dtype=F32, tn=D)
        dh, dhb, G["norm_mix"][l] = _rmsnorm_bwd(f"d_mix_norm_{l}", dxn, h_in, W["norm_mix"][l], dh)
        if l == NA:
            dkb = _add_cast("dk_sum", dk_parts, BF16)
            dvb = _add_cast("dv_sum", dv_parts, BF16)
            dkv = jnp.concatenate([dkb, dvb], axis=1)
            dc = sum(dc_parts[1:], dc_parts[0])
            dc_pad = jnp.pad(dc, ((0, 0), (0, LANES - 2 * HP)))
            dls = _cumsum_rows("dc_cumsum", dc_pad, True)
            dfb, G["b_f"] = _logsig_bwd("d_logsig", dls, f_pre)
            G["w_kv"] = _mm_tn("d_kv_proj", hn_kv, dkv, out_dtype=BF16)
            G["w_f"] = _mm_tn("d_f_proj", hn_kv, dfb, out_dtype=F32)
            dhn1 = _mm_nt("d_kv_hn", dkv, W["w_kv"], out_dtype=F32, tn=D)
            dhn2 = _mm_nt("d_f_hn", dfb, W["w_f"], out_dtype=F32, tn=D)
            dhn_kv = _add_cast("d_kv_hn_sum", [dhn1, dhn2], F32)
            dh, dhb, G["norm_kv"] = _rmsnorm_bwd("d_kv_norm", dhn_kv, h_kv, W["norm_kv"], dh)
    return loss_row, dh, G


_ANY = pl.BlockSpec(memory_space=pl.ANY)


def _position():
    return lax.axis_index("x"), lax.axis_index("y"), lax.axis_index("c")


def _chip_peers(x, y):
    return [(1 - x, y), (x, 1 - y), (1 - x, 1 - y)]


def _slab(ref, ax, idx):
    return ref.at[(slice(None),) * ax + (idx,)]


def _gather_chips(name, shards, axes):
    n = len(shards)
    out_shapes = [jax.ShapeDtypeStruct(s.shape[:ax] + (N_CHIPS,) + s.shape[ax:], s.dtype)
                  for s, ax in zip(shards, axes)]

    def body(*refs):
        ins, outs = refs[:n], refs[n:2 * n]
        send_sems, recv_sems, local_sems = refs[2 * n:]
        x, y, c = _position()
        me = 2 * x + y
        peers = _chip_peers(x, y)

        def remote(t, k, chip):
            px, py = peers[k]
            return pltpu.make_async_remote_copy(
                src_ref=ins[t], dst_ref=_slab(outs[t], axes[t], chip),
                send_sem=send_sems.at[3 * t + k], recv_sem=recv_sems.at[3 * t + k],
                device_id=(px, py, c), device_id_type=MESH)

        local = [pltpu.make_async_copy(ins[t], _slab(outs[t], axes[t], me), local_sems.at[t]) for t in range(n)]
        for t in range(n):
            local[t].start()
            for k in range(3):
                remote(t, k, me).start()
        for t in range(n):
            for k in range(3):
                px, py = peers[k]
                remote(t, k, 2 * px + py).wait_recv()
        for t in range(n):
            for k in range(3):
                remote(t, k, me).wait_send()
            local[t].wait()

    return pl.pallas_call(
        body, name=name, in_specs=[_ANY] * n, out_specs=[_ANY] * n, out_shape=out_shapes,
        scratch_shapes=[pltpu.SemaphoreType.DMA((3 * n,)), pltpu.SemaphoreType.DMA((3 * n,)),
                        pltpu.SemaphoreType.DMA((n,))],
    )(*shards)


def _scatter_chips(name, groups):
    flat = [(w, l, g) for w, layers in enumerate(groups) for l, g in enumerate(layers)]
    n = len(flat)
    out_shapes = [jax.ShapeDtypeStruct((N_CHIPS, len(layers)) + layers[0].shape[1:], layers[0].dtype)
                  for layers in groups]
    nw = len(groups)

    def body(*refs):
        ins, outs = refs[:n], refs[n:n + nw]
        send_sems, recv_sems, local_sems = refs[n + nw:]
        x, y, c = _position()
        me = 2 * x + y
        peers = _chip_peers(x, y)

        def remote(i, k):
            w, l, _ = flat[i]
            px, py = peers[k]
            return pltpu.make_async_remote_copy(
                src_ref=ins[i].at[2 * px + py], dst_ref=outs[w].at[k, l],
                send_sem=send_sems.at[3 * i + k], recv_sem=recv_sems.at[3 * i + k],
                device_id=(px, py, c), device_id_type=MESH)

        local = [pltpu.make_async_copy(ins[i].at[me], outs[flat[i][0]].at[3, flat[i][1]], local_sems.at[i])
                 for i in range(n)]
        for i in range(n):
            local[i].start()
            for k in range(3):
                remote(i, k).start()
        for i in range(n):
            for k in range(3):
                remote(i, k).wait_recv()
        for i in range(n):
            for k in range(3):
                remote(i, k).wait_send()
            local[i].wait()

    return pl.pallas_call(
        body, name=name, in_specs=[_ANY] * n, out_specs=[_ANY] * nw, out_shape=out_shapes,
        scratch_shapes=[pltpu.SemaphoreType.DMA((3 * n,)), pltpu.SemaphoreType.DMA((3 * n,)),
                        pltpu.SemaphoreType.DMA((n,))],
    )(*[g for _, _, g in flat])


def _swap_sibling(name, arrays):
    n = len(arrays)

    def body(*refs):
        ins, outs = refs[:n], refs[n:2 * n]
        send_sems, recv_sems = refs[2 * n:]
        x, y, c = _position()
        copies = [pltpu.make_async_remote_copy(
            src_ref=ins[t], dst_ref=outs[t], send_sem=send_sems.at[t], recv_sem=recv_sems.at[t],
            device_id=(x, y, 1 - c), device_id_type=MESH) for t in range(n)]
        for cp in copies:
            cp.start()
        for cp in copies:
            cp.wait_recv()
        for cp in copies:
            cp.wait_send()

    return pl.pallas_call(
        body, name=name, in_specs=[_ANY] * n, out_specs=[_ANY] * n,
        out_shape=[jax.ShapeDtypeStruct(a.shape, a.dtype) for a in arrays],
        scratch_shapes=[pltpu.SemaphoreType.DMA((n,)), pltpu.SemaphoreType.DMA((n,))],
    )(*arrays)


def _gather_all(name, a):
    def body(a_ref, o_ref, send_sems, recv_sems, local_sem):
        x, y, c = _position()
        me = 4 * x + 2 * y + c

        def peer(k):
            return (x ^ ((k >> 2) & 1), y ^ ((k >> 1) & 1), c ^ (k & 1))

        def remote(k, slot):
            return pltpu.make_async_remote_copy(
                src_ref=a_ref, dst_ref=o_ref.at[slot], send_sem=send_sems.at[k - 1], recv_sem=recv_sems.at[k - 1],
                device_id=peer(k), device_id_type=MESH)

        local = pltpu.make_async_copy(a_ref, o_ref.at[me], local_sem)
        local.start()
        for k in range(1, N_DEV):
            remote(k, me).start()
        for k in range(1, N_DEV):
            px, py, pc = peer(k)
            remote(k, 4 * px + 2 * py + pc).wait_recv()
        for k in range(1, N_DEV):
            remote(k, me).wait_send()
        local.wait()

    return pl.pallas_call(
        body, name=name, in_specs=[_ANY], out_specs=_ANY,
        out_shape=jax.ShapeDtypeStruct((N_DEV,) + a.shape, a.dtype),
        scratch_shapes=[pltpu.SemaphoreType.DMA((N_DEV - 1,)), pltpu.SemaphoreType.DMA((N_DEV - 1,)),
                        pltpu.SemaphoreType.DMA],
    )(a)


def _rows2d(a, lead=0):
    return a.reshape(a.shape[:lead] + (-1, a.shape[-1]))


def _row_tile(rows, cols, itemsize=4, target=1 << 20):
    want = max(SUBLANES, target // (cols * itemsize))
    t = min(rows, (want // 16) * 16)
    while t > 16 and rows % t:
        t -= 16
    return t if rows % t == 0 else rows


def _sum_slots(name, r):
    ns = r.shape[0]
    r2 = _rows2d(r, 1)
    _, rows, cols = r2.shape
    tr = _row_tile(rows, cols)

    def body(r_ref, o_ref):
        acc = r_ref[0].astype(F32)
        for s in range(1, ns):
            acc = acc + r_ref[s].astype(F32)
        o_ref[...] = acc

    out = pl.pallas_call(
        body, name=name, grid=(rows // tr,),
        in_specs=[pl.BlockSpec((ns, tr, cols), lambda i: (0, i, 0))],
        out_specs=pl.BlockSpec((tr, cols), lambda i: (i, 0)),
        out_shape=jax.ShapeDtypeStruct((rows, cols), F32),
        compiler_params=_params(("parallel",)),
    )(r2)
    return out.reshape(r.shape[1:])


def _adamw(name, g_parts, w, m, v):
    shape = w.shape
    ng = len(g_parts)
    args = [_rows2d(a) for a in (*g_parts, w, m, v)]
    rows, cols = args[0].shape
    tr = _row_tile(rows, cols, target=1 << 19)
    c1 = 1.0 - ADAM_B1 ** ADAM_STEP
    c2 = 1.0 - ADAM_B2 ** ADAM_STEP

    def body(*refs):
        g = refs[0][...]
        for r in refs[1:ng]:
            g = g + r[...]
        w_ref, m_ref, v_ref = refs[ng:ng + 3]
        g_out, d_out, m_out, v_out = refs[ng + 3:]
        mn = ADAM_B1 * m_ref[...] + (1.0 - ADAM_B1) * g
        vn = ADAM_B2 * v_ref[...] + (1.0 - ADAM_B2) * (g * g)
        m_hat = mn / c1
        v_hat = vn / c2
        g_out[...] = g
        d_out[...] = -ADAM_LR * (m_hat / (jnp.sqrt(v_hat) + ADAM_EPS) + ADAM_WD * w_ref[...])
        m_out[...] = mn
        v_out[...] = vn

    spec = pl.BlockSpec((tr, cols), lambda i: (i, 0))
    outs = pl.pallas_call(
        body, name=name, grid=(rows // tr,), in_specs=[spec] * (ng + 3), out_specs=[spec] * 4,
        out_shape=[jax.ShapeDtypeStruct((rows, cols), F32)] * 4,
        compiler_params=_params(("parallel",)),
    )(*args)
    return tuple(o.reshape(shape) for o in outs)


_WEIGHTS = ["norm_mix", "norm_ffn", "w_ffn_in", "w_ffn_out", "w_rec_in", "conv_w", "conv_b", "w_lru_gates",
            "b_lru_gates", "lru_param", "w_rec_out", "norm_kv", "w_kvf", "b_forget", "w_q", "w_o", "norm_final"]
_BIG = ["w_ffn_in", "w_ffn_out", "w_rec_in", "w_lru_gates", "w_rec_out", "w_kvf", "w_q", "w_o"]
_GATHER_AXIS = {"w_ffn_in": 1, "w_ffn_out": 1, "w_rec_in": 1, "w_lru_gates": 1, "w_rec_out": 1, "w_kvf": 0,
                "w_q": 1, "w_o": 1}


def _pad_lanes(a, n):
    return jnp.pad(a, ((0, 0),) * (a.ndim - 1) + ((0, n - a.shape[-1]),))


def kernel(x, norm_mix, norm_ffn, w_ffn_in, w_ffn_out, w_rec_in, conv_w, conv_b, w_lru_gates, b_lru_gates, lru_param, w_rec_out, norm_kv, w_kvf, b_forget, w_q, w_o, norm_final, loss_target, m_norm_mix, m_norm_ffn, m_w_ffn_in, m_w_ffn_out, m_w_rec_in, m_conv_w, m_conv_b, m_w_lru_gates, m_b_lru_gates, m_lru_param, m_w_rec_out, m_norm_kv, m_w_kvf, m_b_forget, m_w_q, m_w_o, m_norm_final, v_norm_mix, v_norm_ffn, v_w_ffn_in, v_w_ffn_out, v_w_rec_in, v_conv_w, v_conv_b, v_w_lru_gates, v_b_lru_gates, v_lru_param, v_w_rec_out, v_norm_kv, v_w_kvf, v_b_forget, v_w_q, v_w_o, v_norm_final):
    P = dict(norm_mix=norm_mix, norm_ffn=norm_ffn, w_ffn_in=w_ffn_in, w_ffn_out=w_ffn_out, w_rec_in=w_rec_in,
             conv_w=conv_w, conv_b=conv_b, w_lru_gates=w_lru_gates, b_lru_gates=b_lru_gates, lru_param=lru_param,
             w_rec_out=w_rec_out, norm_kv=norm_kv, w_kvf=w_kvf, b_forget=b_forget, w_q=w_q, w_o=w_o,
             norm_final=norm_final)
    M1 = dict(norm_mix=m_norm_mix, norm_ffn=m_norm_ffn, w_ffn_in=m_w_ffn_in, w_ffn_out=m_w_ffn_out,
              w_rec_in=m_w_rec_in, conv_w=m_conv_w, conv_b=m_conv_b, w_lru_gates=m_w_lru_gates,
              b_lru_gates=m_b_lru_gates, lru_param=m_lru_param, w_rec_out=m_w_rec_out, norm_kv=m_norm_kv,
              w_kvf=m_w_kvf, b_forget=m_b_forget, w_q=m_w_q, w_o=m_w_o, norm_final=m_norm_final)
    M2 = dict(norm_mix=v_norm_mix, norm_ffn=v_norm_ffn, w_ffn_in=v_w_ffn_in, w_ffn_out=v_w_ffn_out,
              w_rec_in=v_w_rec_in, conv_w=v_conv_w, conv_b=v_conv_b, w_lru_gates=v_w_lru_gates,
              b_lru_gates=v_b_lru_gates, lru_param=v_lru_param, w_rec_out=v_w_rec_out, norm_kv=v_norm_kv,
              w_kvf=v_w_kvf, b_forget=v_b_forget, w_q=v_w_q, w_o=v_w_o, norm_final=v_norm_final)

    _, S, D = x.shape
    L = norm_mix.shape[0]
    NA, NBLK, BW, GS = w_lru_gates.shape
    NB = w_q.shape[0]
    C = NBLK * BW
    CS = C // N_CHIPS
    H = b_forget.shape[0]
    assert C == D and H * HEAD_DIM == D and H <= LANES
    chip = 2 * lax.axis_index("x") + lax.axis_index("y")

    small_a = jnp.concatenate([conv_w, conv_b[:, None], lru_param[:, None]], axis=1)
    shards = [P[w].astype(BF16) for w in _BIG] + [small_a, b_lru_gates]
    axes = [_GATHER_AXIS[w] for w in _BIG] + [0, 0]
    gathered = _gather_chips("gather_weights", shards, axes)
    Gw = dict(zip(_BIG, gathered[:len(_BIG)]))
    small_a = gathered[-2].transpose(1, 2, 0, 3).reshape(NA, 6, C)
    b_gates = gathered[-1].transpose(1, 2, 0, 3).reshape(NA, NBLK, 1, N_CHIPS * GS)
    w_kvf_full = Gw["w_kvf"].transpose(1, 0, 2).reshape(D, -1)
    W = dict(
        w_ffn_in=Gw["w_ffn_in"],
        w_ffn_out=Gw["w_ffn_out"].reshape(L, -1, D),
        w_rec_in=Gw["w_rec_in"],
        w_gates=Gw["w_lru_gates"].transpose(0, 2, 3, 1, 4).reshape(NA, NBLK, BW, N_CHIPS * GS),
        b_gates=b_gates,
        w_rec_out=Gw["w_rec_out"].reshape(NA, C, D),
        w_kv=w_kvf_full[:, :2 * D],
        w_f=_pad_lanes(w_kvf_full[:, 2 * D:], LANES),
        w_q=Gw["w_q"].reshape(NB, D, D),
        w_o=Gw["w_o"].reshape(NB, D, D),
        conv_w=small_a[:, :4], conv_b=small_a[:, 4:5], lru_param=small_a[:, 5:6],
        norm_mix=norm_mix[:, None], norm_ffn=norm_ffn[:, None], norm_kv=norm_kv[None], norm_final=norm_final[None],
        b_f=_pad_lanes(b_forget[None], LANES),
    )

    loss_row, grad_x, G = _local_step(x.reshape(S, D), loss_target.reshape(S, D), W)

    rows = [*G["norm_mix"], *G["norm_ffn"], G["norm_kv"], G["norm_final"],
            _pad_lanes(G["b_f"], D), _pad_lanes(loss_row, D)]
    for a in range(NA):
        rows += [G["conv_w"][a], G["conv_b"][a], G["b_gi"][a], G["b_gr"][a], G["lru_param"][a]]
    packed = jnp.concatenate(rows, axis=0)
    tot = _sum_slots("sum_small", _gather_all("gather_small", packed))
    loss = tot[2 * L + 3, 0]
    g_rep = jnp.concatenate([tot[:2 * L + 2], tot[2 * L + 2:2 * L + 3]], axis=0)
    base = 2 * L + 4
    g_sh = []
    for a in range(NA):
        blk = lax.dynamic_slice_in_dim(tot[base + 8 * a:base + 8 * a + 8], chip * CS, CS, axis=1)
        gi = tot[base + 8 * a + 5].reshape(NBLK, BW)
        gr = tot[base + 8 * a + 6].reshape(NBLK, BW)
        bl = lax.dynamic_slice_in_dim(jnp.concatenate([gi, gr], axis=1), chip * GS, GS, axis=1)
        g_sh += [blk[:5], bl.reshape(-1, CS), blk[7:8]]
    g_sh = jnp.concatenate(g_sh, axis=0)
    nrow = g_sh.shape[0] // NA

    def pack_rep(T):
        return jnp.concatenate([T["norm_mix"], T["norm_ffn"], T["norm_kv"][None], T["norm_final"][None],
                                _pad_lanes(T["b_forget"][None], D)], axis=0)

    def pack_sh(T):
        return jnp.concatenate([jnp.concatenate([T["conv_w"][a], T["conv_b"][a][None],
                                                 T["b_lru_gates"][a].reshape(-1, CS), T["lru_param"][a][None]], axis=0)
                                for a in range(NA)], axis=0)

    rep = _adamw("adamw_replicated", [g_rep], pack_rep(P), pack_rep(M1), pack_rep(M2))
    shd = _adamw("adamw_small_sharded", [g_sh], pack_sh(P), pack_sh(M1), pack_sh(M2))

    def unpack_rep(t):
        return dict(norm_mix=t[:L], norm_ffn=t[L:2 * L], norm_kv=t[2 * L], norm_final=t[2 * L + 1],
                    b_forget=t[2 * L + 2, :H])

    def unpack_sh(t):
        t = t.reshape(NA, nrow, CS)
        return dict(conv_w=t[:, :4], conv_b=t[:, 4], b_lru_gates=t[:, 5:nrow - 1].reshape(NA, NBLK, GS),
                    lru_param=t[:, nrow - 1])

    dkvf = jnp.concatenate([G["w_kv"].astype(F32), G["w_f"][:, :H]], axis=1)
    groups = [
        G["w_ffn_in"],
        [g.reshape(N_CHIPS, -1, D) for g in G["w_ffn_out"]],
        G["w_rec_in"],
        [g.reshape(NBLK, BW, N_CHIPS, GS).transpose(2, 0, 1, 3) for g in G["w_gates"]],
        [g.reshape(N_CHIPS, -1, D) for g in G["w_rec_out"]],
        [dkvf.reshape(D, N_CHIPS, -1).transpose(1, 0, 2).astype(BF16)],
        [g.reshape(N_CHIPS, -1, D) for g in G["w_q"]],
        [g.reshape(N_CHIPS, -1, D) for g in G["w_o"]],
    ]
    received = _scatter_chips("scatter_grads", groups)
    mine = [_sum_slots(f"sum_grads_{w}", r) for w, r in zip(_BIG, received)]
    theirs = _swap_sibling("swap_grads", mine)
    big = {}
    for w, a, b in zip(_BIG, mine, theirs):
        shape = P[w].shape
        big[w] = _adamw(f"adamw_{w}", [a.reshape(shape), b.reshape(shape)], P[w], M1[w], M2[w])

    outs = []
    for i in range(4):
        small = {**unpack_rep(rep[i]), **unpack_sh(shd[i])}
        outs.append([big[w][i] if w in big else small[w] for w in _WEIGHTS])
    return (loss, grad_x.reshape(1, S, D), *outs[0], *outs[1], *outs[2], *outs[3])
```

```python
import functools
import math

import jax
import jax.numpy as jnp
from jax import lax
from jax.experimental import pallas as pl
from jax.experimental.pallas import tpu as pltpu

F32 = jnp.float32
BF16 = jnp.bfloat16

EPS = 1e-6
LRU_C = 8.0
HEAD_DIM = 64
LANES = 128
SUBLANES = 8
VMEM_LIMIT = 48 * 1024 * 1024
N_CHIPS = 4
N_DEV = 8

ADAM_LR = 0.001
ADAM_B1 = 0.9
ADAM_B2 = 0.999
ADAM_EPS = 1e-08
ADAM_WD = 0.01
ADAM_STEP = 10

_NN = (((1,), (0,)), ((), ()))
_NT = (((1,), (1,)), ((), ()))
_TN = (((0,), (0,)), ((), ()))
_DN = {"nn": _NN, "nt": _NT, "tn": _TN}
MESH = pl.DeviceIdType.MESH


def _params(sem):
    return pltpu.CompilerParams(dimension_semantics=sem, vmem_limit_bytes=VMEM_LIMIT)


def _tile(n, want):
    if n <= want:
        return n
    t = (want // LANES) * LANES
    while t >= LANES:
        if n % t == 0:
            return t
        t -= LANES
    return n


def _sigmoid(x):
    return 1.0 / (1.0 + jnp.exp(-x))


def _softplus(x):
    return jnp.maximum(x, 0.0) + jnp.log(1.0 + jnp.exp(-jnp.abs(x)))


_GELU_C = math.sqrt(2.0 / math.pi)


def _gelu_and_grad(x):
    inner = _GELU_C * (x + 0.044715 * x * x * x)
    t = jnp.tanh(inner)
    g = 0.5 * x * (1.0 + t)
    dg = 0.5 * (1.0 + t) + 0.5 * x * (1.0 - t * t) * _GELU_C * (1.0 + 3.0 * 0.044715 * x * x)
    return g, dg


def _mm(name, mode, a, b, *, grid, a_spec, b_spec, out_shape, out_dtype, out_spec, nk=1,
        res=None, res_spec=None, bias=None, bias_spec=None, scale=None):
    dn = _DN[mode]
    has_res, has_bias = res is not None, bias is not None
    blk = tuple(d for d in out_spec.block_shape if d is not None)

    def body(*refs):
        a_ref, b_ref = refs[0], refs[1]
        p = 2
        r_ref = refs[p] if has_res else None
        p += int(has_res)
        bias_ref = refs[p] if has_bias else None
        p += int(has_bias)
        o_ref = refs[p]
        part = lax.dot_general(a_ref[...], b_ref[...], dn, preferred_element_type=F32)

        def finish(acc):
            if scale is not None:
                acc = acc * scale
            if has_bias:
                acc = acc + bias_ref[...]
            if has_res:
                acc = r_ref[...] + acc
            o_ref[...] = acc.astype(o_ref.dtype)

        if nk == 1:
            finish(part)
        else:
            acc_ref = refs[p + 1]
            k = pl.program_id(2)

            @pl.when(k == 0)
            def _():
                acc_ref[...] = part

            @pl.when(k > 0)
            def _():
                acc_ref[...] += part

            @pl.when(k == nk - 1)
            def _():
                finish(acc_ref[...])

    ins, specs = [a, b], [a_spec, b_spec]
    if has_res:
        ins.append(res)
        specs.append(res_spec)
    if has_bias:
        ins.append(bias)
        specs.append(bias_spec)
    sem = ("parallel", "parallel") + (("arbitrary",) if len(grid) == 3 else ())
    return pl.pallas_call(
        body, name=name, grid=grid, in_specs=specs, out_specs=out_spec,
        out_shape=jax.ShapeDtypeStruct(out_shape, out_dtype),
        scratch_shapes=[pltpu.VMEM(blk, F32)] if nk > 1 else [],
        compiler_params=_params(sem),
    )(*ins)


def _mm_nn(name, a, b, *, b_lead=(), out_dtype, tm=512, tn=512, res=None, bias=None, scale=None):
    M, K = a.shape
    N = b.shape[-1]
    tm, tn = _tile(M, tm), _tile(N, tn)
    nl = len(b_lead)
    return _mm(
        name, "nn", a, b, grid=(M // tm, N // tn),
        a_spec=pl.BlockSpec((tm, K), lambda i, j: (i, 0)),
        b_spec=pl.BlockSpec((None,) * nl + (K, tn), lambda i, j: tuple(b_lead) + (0, j)),
        out_shape=(M, N), out_dtype=out_dtype, out_spec=pl.BlockSpec((tm, tn), lambda i, j: (i, j)),
        res=res, res_spec=pl.BlockSpec((tm, tn), lambda i, j: (i, j)),
        bias=bias, bias_spec=pl.BlockSpec((1, tn), lambda i, j: (0, j)), scale=scale)


def _mm_nt(name, a, b, *, b_lead=(), out_dtype, tm=512, tn=512, tk=2048):
    M, K = a.shape
    N = b.shape[-2]
    tm, tn, tk = _tile(M, tm), _tile(N, tn), _tile(K, tk)
    nk = K // tk
    nl = len(b_lead)
    return _mm(
        name, "nt", a, b, grid=(M // tm, N // tn, nk), nk=nk,
        a_spec=pl.BlockSpec((tm, tk), lambda i, j, k: (i, k)),
        b_spec=pl.BlockSpec((None,) * nl + (tn, tk), lambda i, j, k: tuple(b_lead) + (j, k)),
        out_shape=(M, N), out_dtype=out_dtype, out_spec=pl.BlockSpec((tm, tn), lambda i, j, k: (i, j)))


def _mm_tn(name, a, b, *, out_dtype, tm=512, tn=512):
    S, M = a.shape
    N = b.shape[1]
    tm, tn = _tile(M, tm), _tile(N, tn)
    return _mm(
        name, "tn", a, b, grid=(M // tm, N // tn),
        a_spec=pl.BlockSpec((S, tm), lambda i, j: (0, i)),
        b_spec=pl.BlockSpec((S, tn), lambda i, j: (0, j)),
        out_shape=(M, N), out_dtype=out_dtype, out_spec=pl.BlockSpec((tm, tn), lambda i, j: (i, j)))


def _rmsnorm_fwd(name, h, g, tr=256):
    S, D = h.shape
    tr = _tile(S, tr)

    def body(h_ref, g_ref, o_ref):
        x = h_ref[...]
        r = lax.rsqrt(jnp.mean(x * x, axis=-1, keepdims=True) + EPS)
        o_ref[...] = (x * r * g_ref[...]).astype(o_ref.dtype)

    return pl.pallas_call(
        body, name=name, grid=(S // tr,),
        in_specs=[pl.BlockSpec((tr, D), lambda i: (i, 0)), pl.BlockSpec((1, D), lambda i: (0, 0))],
        out_specs=pl.BlockSpec((tr, D), lambda i: (i, 0)),
        out_shape=jax.ShapeDtypeStruct((S, D), BF16),
        compiler_params=_params(("parallel",)),
    )(h, g)


def _rmsnorm_bwd(name, dxn, h, g, dh_in, tr=256):
    S, D = h.shape
    tr = _tile(S, tr)

    def body(dxn_ref, h_ref, g_ref, dh_ref, o_ref, ob_ref, dg_ref):
        i = pl.program_id(0)
        x = h_ref[...]
        dy = dxn_ref[...].astype(F32)
        r = lax.rsqrt(jnp.mean(x * x, axis=-1, keepdims=True) + EPS)
        xr = x * r
        dyg = dy * g_ref[...]
        dx = r * dyg - xr * (r * jnp.mean(dyg * xr, axis=-1, keepdims=True))
        out = dh_ref[...] + dx
        o_ref[...] = out
        ob_ref[...] = out.astype(BF16)
        part = jnp.sum(dy * xr, axis=0, keepdims=True)

        @pl.when(i == 0)
        def _():
            dg_ref[...] = part

        @pl.when(i > 0)
        def _():
            dg_ref[...] += part

    row = pl.BlockSpec((tr, D), lambda i: (i, 0))
    vec = pl.BlockSpec((1, D), lambda i: (0, 0))
    return pl.pallas_call(
        body, name=name, grid=(S // tr,),
        in_specs=[row, row, vec, row], out_specs=[row, row, vec],
        out_shape=[jax.ShapeDtypeStruct((S, D), F32), jax.ShapeDtypeStruct((S, D), BF16),
                   jax.ShapeDtypeStruct((1, D), F32)],
        compiler_params=_params(("arbitrary",)),
    )(dxn, h, g, dh_in)


def _loss_head(name, h, target, g, tr=256):
    S, D = h.shape
    tr = _tile(S, tr)

    def body(h_ref, t_ref, g_ref, o_ref, ob_ref, dg_ref, loss_ref):
        i = pl.program_id(0)
        x = h_ref[...]
        gg = g_ref[...]
        r = lax.rsqrt(jnp.mean(x * x, axis=-1, keepdims=True) + EPS)
        xr = x * r
        err = xr * gg - t_ref[...]
        lpart = 0.5 * jnp.sum(jnp.mean(err * err, axis=-1, keepdims=True), axis=0, keepdims=True)
        dy = err * (1.0 / D)
        dyg = dy * gg
        dx = r * dyg - xr * (r * jnp.mean(dyg * xr, axis=-1, keepdims=True))
        o_ref[...] = dx
        ob_ref[...] = dx.astype(BF16)
        part = jnp.sum(dy * xr, axis=0, keepdims=True)
        lrow = jnp.broadcast_to(lpart, (1, LANES))

        @pl.when(i == 0)
        def _():
            dg_ref[...] = part
            loss_ref[...] = lrow

        @pl.when(i > 0)
        def _():
            dg_ref[...] += part
            loss_ref[...] += lrow

    row = pl.BlockSpec((tr, D), lambda i: (i, 0))
    vec = pl.BlockSpec((1, D), lambda i: (0, 0))
    return pl.pallas_call(
        body, name=name, grid=(S // tr,),
        in_specs=[row, row, vec], out_specs=[row, row, vec, pl.BlockSpec((1, LANES), lambda i: (0, 0))],
        out_shape=[jax.ShapeDtypeStruct((S, D), F32), jax.ShapeDtypeStruct((S, D), BF16),
                   jax.ShapeDtypeStruct((1, D), F32), jax.ShapeDtypeStruct((1, LANES), F32)],
        compiler_params=_params(("arbitrary",)),
    )(h, target, g)


def _swiglu_fwd(name, z3, tr=256, tc=1408):
    _, S, F = z3.shape
    tr, tc = _tile(S, tr), _tile(F, tc)

    def body(z_ref, a_ref):
        zg = z_ref[0].astype(F32)
        zu = z_ref[1].astype(F32)
        a_ref[...] = (zg * _sigmoid(zg) * zu).astype(a_ref.dtype)

    return pl.pallas_call(
        body, name=name, grid=(S // tr, F // tc),
        in_specs=[pl.BlockSpec((2, tr, tc), lambda i, j: (0, i, j))],
        out_specs=pl.BlockSpec((tr, tc), lambda i, j: (i, j)),
        out_shape=jax.ShapeDtypeStruct((S, F), BF16),
        compiler_params=_params(("parallel", "parallel")),
    )(z3)


def _swiglu_bwd(name, da, z3, tr=256, tc=1408):
    _, S, F = z3.shape
    tr, tc = _tile(S, tr), _tile(F, tc)

    def body(da_ref, z_ref, dz_ref):
        zg = z_ref[0].astype(F32)
        zu = z_ref[1].astype(F32)
        d = da_ref[...].astype(F32)
        sg = _sigmoid(zg)
        silu = zg * sg
        dz_ref[0] = (d * zu * (sg * (1.0 + zg * (1.0 - sg)))).astype(dz_ref.dtype)
        dz_ref[1] = (d * silu).astype(dz_ref.dtype)

    return pl.pallas_call(
        body, name=name, grid=(S // tr, F // tc),
        in_specs=[pl.BlockSpec((tr, tc), lambda i, j: (i, j)),
                  pl.BlockSpec((2, tr, tc), lambda i, j: (0, i, j))],
        out_specs=pl.BlockSpec((2, tr, tc), lambda i, j: (0, i, j)),
        out_shape=jax.ShapeDtypeStruct((2, S, F), BF16),
        compiler_params=_params(("parallel", "parallel")),
    )(da, z3)


SCAN_ROWS = 64


def _group_scan(A, B, reverse):
    n = A.shape[0]
    sub = lax.broadcasted_iota(jnp.int32, A.shape, 0) % SUBLANES
    for d in (1, 2, 4):
        if reverse:
            A_sh, B_sh = pltpu.roll(A, n - d, 0), pltpu.roll(B, n - d, 0)
            keep = sub < SUBLANES - d
        else:
            A_sh, B_sh = pltpu.roll(A, d, 0), pltpu.roll(B, d, 0)
            keep = sub >= d
        B = jnp.where(keep, A * B_sh + B, B)
        A = jnp.where(keep, A * A_sh, A)
    return A, B


def _block_scan(a, u, carry, reverse):
    A, B = _group_scan(a, u, reverse)
    ng = a.shape[0] // SUBLANES
    out = [None] * ng
    order = range(ng - 1, -1, -1) if reverse else range(ng)
    for gi in order:
        sl = slice(gi * SUBLANES, (gi + 1) * SUBLANES)
        hg = A[sl] * carry + B[sl]
        out[gi] = hg
        carry = hg[0:1] if reverse else hg[SUBLANES - 1:SUBLANES]
    return jnp.concatenate(out, axis=0), carry


def _lru_gates(rc, gip, grp, sp):
    gi = _sigmoid(gip)
    gr = _sigmoid(grp)
    la = -LRU_C * gr * sp
    a = jnp.exp(la)
    om = -jnp.tanh(la) * (a * a + 1.0)
    mult = jnp.sqrt(om)
    return gi, gr, a, mult


def _lru_fwd(name, proj, rc, gip, grp, lru_p, tc=256):
    S, C = rc.shape
    tc = _tile(C, tc)
    nb = S // SCAN_ROWS

    def body(gb_ref, rc_ref, gi_ref, gr_ref, l_ref, h_ref, m_ref):
        sp = _softplus(-l_ref[...])

        def step(b, carry):
            rows = pl.ds(pl.multiple_of(b * SCAN_ROWS, SCAN_ROWS), SCAN_ROWS)
            rcb = rc_ref[rows, :]
            gi, _, a, mult = _lru_gates(rcb, gi_ref[rows, :], gr_ref[rows, :], sp)
            h, carry = _block_scan(a, rcb * gi * mult, carry, False)
            h_ref[rows, :] = h
            gel, _ = _gelu_and_grad(gb_ref[rows, :])
            m_ref[rows, :] = (gel * h).astype(m_ref.dtype)
            return carry

        lax.fori_loop(0, nb, step, jnp.zeros((1, tc), F32))

    col = pl.BlockSpec((S, tc), lambda j: (0, j))
    return pl.pallas_call(
        body, name=name, grid=(C // tc,),
        in_specs=[col, col, col, col, pl.BlockSpec((1, tc), lambda j: (0, j))],
        out_specs=[col, col],
        out_shape=[jax.ShapeDtypeStruct((S, C), F32), jax.ShapeDtypeStruct((S, C), BF16)],
        compiler_params=_params(("parallel",)),
    )(proj, rc, gip, grp, lru_p)


def _lru_bwd(name, dm, proj, hrec, rc, gip, grp, lru_p, tc=256):
    S, C = rc.shape
    tc = _tile(C, tc)
    nb = S // SCAN_ROWS
    R = SCAN_ROWS

    def body(dm_ref, gb_ref, h_ref, rc_ref, gi_ref, gr_ref, l_ref,
             dgb_ref, dgi_ref, dgr_ref, drc_ref, dbi_ref, dbr_ref, dl_ref):
        lp = l_ref[...]
        sp = _softplus(-lp)
        row = lax.broadcasted_iota(jnp.int32, (R, tc), 0)
        zero = jnp.zeros((1, tc), F32)

        def step(t, carry):
            mu_in, s_i, s_r, s_sp = carry
            b = nb - 1 - t
            r0 = pl.multiple_of(b * R, R)
            rows = pl.ds(r0, R)
            rcb = rc_ref[rows, :]
            gi, gr, a, mult = _lru_gates(rcb, gi_ref[rows, :], gr_ref[rows, :], sp)
            gel, dgel = _gelu_and_grad(gb_ref[rows, :])
            dmb = dm_ref[rows, :]
            h = h_ref[rows, :]
            dgb_ref[rows, :] = (dmb * h * dgel).astype(dgb_ref.dtype)
            dh = dmb * gel
            mu, mu_out = _block_scan(a, a * dh, mu_in, True)
            mu_next = jnp.where(row == R - 1, mu_in, pltpu.roll(mu, R - 1, 0))
            lam = dh + mu_next
            p0 = pl.multiple_of(jnp.maximum(r0 - SUBLANES, 0), SUBLANES)
            prev = h_ref[pl.ds(p0, SUBLANES), :][SUBLANES - 1:SUBLANES]
            prev = jnp.where(b > 0, prev, 0.0)
            h_prev = jnp.where(row == 0, prev, pltpu.roll(h, 1, 0))
            da = lam * h_prev
            d_mult = lam * rcb * gi
            d_la = da * a - d_mult * (a * a) / mult
            d_grp = d_la * (-LRU_C * sp) * gr * (1.0 - gr)
            d_gip = lam * rcb * mult * gi * (1.0 - gi)
            dgr_ref[rows, :] = d_grp.astype(dgr_ref.dtype)
            dgi_ref[rows, :] = d_gip.astype(dgi_ref.dtype)
            drc_ref[rows, :] = lam * gi * mult
            s_i = s_i + jnp.sum(d_gip, axis=0, keepdims=True)
            s_r = s_r + jnp.sum(d_grp, axis=0, keepdims=True)
            s_sp = s_sp + jnp.sum(d_la * gr, axis=0, keepdims=True)
            return mu_out, s_i, s_r, s_sp

        _, s_i, s_r, s_sp = lax.fori_loop(0, nb, step, (zero, zero, zero, zero))
        dbi_ref[...] = s_i
        dbr_ref[...] = s_r
        dl_ref[...] = (-LRU_C * s_sp) * (-_sigmoid(-lp))

    col = pl.BlockSpec((S, tc), lambda j: (0, j))
    vec = pl.BlockSpec((1, tc), lambda j: (0, j))
    return pl.pallas_call(
        body, name=name, grid=(C // tc,),
        in_specs=[col, col, col, col, col, col, vec],
        out_specs=[col, col, col, col, vec, vec, vec],
        out_shape=[jax.ShapeDtypeStruct((S, C), BF16), jax.ShapeDtypeStruct((S, C), BF16),
                   jax.ShapeDtypeStruct((S, C), BF16), jax.ShapeDtypeStruct((S, C), F32),
                   jax.ShapeDtypeStruct((1, C), F32), jax.ShapeDtypeStruct((1, C), F32),
                   jax.ShapeDtypeStruct((1, C), F32)],
        compiler_params=_params(("parallel",)),
    )(dm, proj, hrec, rc, gip, grp, lru_p)


def _cumsum_rows(name, u, reverse):
    S, C = u.shape
    nb = S // SCAN_ROWS

    def body(u_ref, o_ref):
        def step(t, carry):
            b = nb - 1 - t if reverse else t
            rows = pl.ds(pl.multiple_of(b * SCAN_ROWS, SCAN_ROWS), SCAN_ROWS)
            ub = u_ref[rows, :]
            h, carry = _block_scan(jnp.ones_like(ub), ub, carry, reverse)
            o_ref[rows, :] = h
            return carry

        lax.fori_loop(0, nb, step, jnp.zeros((1, C), F32))

    spec = pl.BlockSpec((S, C), lambda i: (0, 0))
    return pl.pallas_call(
        body, name=name, grid=(1,), in_specs=[spec], out_specs=spec,
        out_shape=jax.ShapeDtypeStruct((S, C), F32),
        compiler_params=_params(("arbitrary",)),
    )(u)


def _shift_down(x, k):
    row = lax.broadcasted_iota(jnp.int32, x.shape, 0)
    return jnp.where(row >= k, pltpu.roll(x, k, 0), 0.0)


def _shift_up(x, k):
    n = x.shape[0]
    row = lax.broadcasted_iota(jnp.int32, x.shape, 0)
    return jnp.where(row < n - k, pltpu.roll(x, n - k, 0), 0.0)


def _conv_fwd(name, proj, w, b, tc=256):
    S, C2 = proj.shape
    C = C2 // 2
    tc = _tile(C, tc)
    off = C // tc

    def body(x_ref, w_ref, b_ref, o_ref, ob_ref):
        x = x_ref[...]
        out = b_ref[...] + w_ref[3:4, :] * x
        for k in (1, 2, 3):
            out = out + w_ref[3 - k:4 - k, :] * _shift_down(x, k)
        o_ref[...] = out
        ob_ref[...] = out.astype(BF16)

    col = pl.BlockSpec((S, tc), lambda j: (0, j))
    return pl.pallas_call(
        body, name=name, grid=(C // tc,),
        in_specs=[pl.BlockSpec((S, tc), lambda j: (0, off + j)),
                  pl.BlockSpec((4, tc), lambda j: (0, j)), pl.BlockSpec((1, tc), lambda j: (0, j))],
        out_specs=[col, col],
        out_shape=[jax.ShapeDtypeStruct((S, C), F32), jax.ShapeDtypeStruct((S, C), BF16)],
        compiler_params=_params(("parallel",)),
    )(proj, w, b)


def _conv_bwd(name, drc, proj, w, tc=256):
    S, C = drc.shape
    tc = _tile(C, tc)
    off = C // tc

    def body(y_ref, x_ref, w_ref, dx_ref, dw_ref, db_ref):
        y = y_ref[...]
        x = x_ref[...]
        dx = w_ref[3:4, :] * y
        dw_ref[3:4, :] = jnp.sum(y * x, axis=0, keepdims=True)
        for k in (1, 2, 3):
            dx = dx + w_ref[3 - k:4 - k, :] * _shift_up(y, k)
            dw_ref[3 - k:4 - k, :] = jnp.sum(y * _shift_down(x, k), axis=0, keepdims=True)
        dx_ref[...] = dx.astype(dx_ref.dtype)
        db_ref[...] = jnp.sum(y, axis=0, keepdims=True)

    col = pl.BlockSpec((S, tc), lambda j: (0, j))
    return pl.pallas_call(
        body, name=name, grid=(C // tc,),
        in_specs=[col, pl.BlockSpec((S, tc), lambda j: (0, off + j)), pl.BlockSpec((4, tc), lambda j: (0, j))],
        out_specs=[col, pl.BlockSpec((4, tc), lambda j: (0, j)), pl.BlockSpec((1, tc), lambda j: (0, j))],
        out_shape=[jax.ShapeDtypeStruct((S, C), BF16), jax.ShapeDtypeStruct((4, C), F32),
                   jax.ShapeDtypeStruct((1, C), F32)],
        compiler_params=_params(("parallel",)),
    )(drc, proj, w)


def _gates_fwd(name, rcb, wg, bg):
    S, C = rcb.shape
    nblk, bw, _ = wg.shape

    def body(x_ref, w_ref, b_ref, gi_ref, gr_ref):
        g = jnp.dot(x_ref[...], w_ref[...], preferred_element_type=F32) + b_ref[...]
        gi_ref[...] = g[:, :bw]
        gr_ref[...] = g[:, bw:]

    col = pl.BlockSpec((S, bw), lambda n: (0, n))
    return pl.pallas_call(
        body, name=name, grid=(nblk,),
        in_specs=[col, pl.BlockSpec((None, bw, 2 * bw), lambda n: (n, 0, 0)),
                  pl.BlockSpec((None, 1, 2 * bw), lambda n: (n, 0, 0))],
        out_specs=[col, col],
        out_shape=[jax.ShapeDtypeStruct((S, C), F32), jax.ShapeDtypeStruct((S, C), F32)],
        compiler_params=_params(("parallel",)),
    )(rcb, wg, bg)


def _gates_bwd(name, dgi, dgr, rcb, wg, drc1):
    S, C = rcb.shape
    nblk, bw, _ = wg.shape

    def body(dgi_ref, dgr_ref, x_ref, w_ref, d1_ref, drc_ref, dw_ref):
        w = w_ref[...]
        x = x_ref[...]
        di, dr = dgi_ref[...], dgr_ref[...]
        drc_ref[...] = (d1_ref[...]
                        + lax.dot_general(di, w[:, :bw], _NT, preferred_element_type=F32)
                        + lax.dot_general(dr, w[:, bw:], _NT, preferred_element_type=F32))
        dw_ref[:, :bw] = lax.dot_general(x, di, _TN, preferred_element_type=F32).astype(dw_ref.dtype)
        dw_ref[:, bw:] = lax.dot_general(x, dr, _TN, preferred_element_type=F32).astype(dw_ref.dtype)

    col = pl.BlockSpec((S, bw), lambda n: (0, n))
    wspec = pl.BlockSpec((None, bw, 2 * bw), lambda n: (n, 0, 0))
    return pl.pallas_call(
        body, name=name, grid=(nblk,),
        in_specs=[col, col, col, wspec, col], out_specs=[col, wspec],
        out_shape=[jax.ShapeDtypeStruct((S, C), F32), jax.ShapeDtypeStruct((nblk, bw, 2 * bw), BF16)],
        compiler_params=_params(("parallel",)),
    )(dgi, dgr, rcb, wg, drc1)


def _att_tile(S):
    return 256 if S % 256 == 0 else 128


def _causal(T):
    r = lax.broadcasted_iota(jnp.int32, (T, T), 0)
    c = lax.broadcasted_iota(jnp.int32, (T, T), 1)
    return r >= c


def _attn_fwd(name, q, kv, negc3):
    S, D = q.shape
    HP = D // LANES
    T = _att_tile(S)
    nq = S // T

    def body(q_ref, k_ref, v_ref, nc_ref, o_ref, of_ref, lse_ref):
        is0 = lax.broadcasted_iota(jnp.int32, (T, LANES), 1) < HEAD_DIM
        tri = _causal(T)

        def q_step(qi, _):
            rows = pl.ds(pl.multiple_of(qi * T, T), T)
            qf = q_ref[rows, :].astype(F32)
            outs, lses = [], []
            for hh in range(2):
                qm = jnp.where(is0 if hh == 0 else jnp.logical_not(is0), qf, 0.0).astype(BF16)

                def tile(kj, carry, masked):
                    m, l, acc = carry
                    ks = pl.ds(pl.multiple_of(kj * T, T), T)
                    s = lax.dot_general(qm, k_ref[ks, :], _NT, preferred_element_type=F32)
                    s = s + nc_ref[hh:hh + 1, ks]
                    if masked:
                        s = jnp.where(tri, s, -jnp.inf)
                    m_new = jnp.maximum(m, jnp.max(s, axis=1, keepdims=True))
                    alpha = jnp.exp(m - m_new)
                    p = jnp.exp(s - m_new)
                    l = alpha * l + jnp.sum(p, axis=1, keepdims=True)
                    acc = alpha * acc + jnp.dot(p.astype(BF16), v_ref[ks, :], preferred_element_type=F32)
                    return m_new, l, acc

                init = (jnp.full((T, 1), -jnp.inf, F32), jnp.zeros((T, 1), F32), jnp.zeros((T, LANES), F32))
                carry = lax.fori_loop(0, qi, lambda kj, c: tile(kj, c, False), init)
                m, l, acc = tile(qi, carry, True)
                outs.append(acc / l)
                lses.append(jnp.broadcast_to(m + jnp.log(l), (T, LANES)))
            out = jnp.where(is0, outs[0], outs[1])
            o_ref[rows, :] = out.astype(o_ref.dtype)
            of_ref[rows, :] = out
            lse_ref[rows, :] = jnp.where(is0, lses[0], lses[1])
            return 0

        lax.fori_loop(0, nq, q_step, 0)

    return pl.pallas_call(
        body, name=name, grid=(HP,),
        in_specs=[pl.BlockSpec((S, LANES), lambda p: (0, p)),
                  pl.BlockSpec((S, LANES), lambda p: (0, p)),
                  pl.BlockSpec((S, LANES), lambda p: (0, HP + p)),
                  pl.BlockSpec((None, 2, S), lambda p: (p, 0, 0))],
        out_specs=[pl.BlockSpec((S, LANES), lambda p: (0, p))] * 3,
        out_shape=[jax.ShapeDtypeStruct((S, D), BF16), jax.ShapeDtypeStruct((S, D), F32),
                   jax.ShapeDtypeStruct((S, D), F32)],
        compiler_params=_params(("parallel",)),
    )(q, kv, kv, negc3)


def _attn_bwd(name, q, kv, negc3, o, do, lse):
    S, D = q.shape
    HP = D // LANES
    T = _att_tile(S)
    nq = S // T
    rep = T // LANES
    scale = HEAD_DIM ** -0.5

    def body(q_ref, k_ref, v_ref, nc_ref, o_ref, do_ref, lse_ref,
             dq_ref, dk_ref, dv_ref, dc_ref, dr_ref, dq_acc, lse_rep, dl_rep, dr_rep):
        is0 = lax.broadcasted_iota(jnp.int32, (T, LANES), 1) < HEAD_DIM
        tri = _causal(T)

        def prologue(qi, _):
            rows = pl.ds(pl.multiple_of(qi * T, T), T)
            prod = do_ref[rows, :].astype(F32) * o_ref[rows, :]
            lse_b = lse_ref[rows, :]
            for hh in range(2):
                msk = is0 if hh == 0 else jnp.logical_not(is0)
                dl = jnp.sum(jnp.where(msk, prod, 0.0), axis=1, keepdims=True)
                ls = jnp.max(jnp.where(msk, lse_b, -jnp.inf), axis=1, keepdims=True)
                dl_rep[hh, rows, :] = jnp.broadcast_to(dl, (T, LANES))
                lse_rep[hh, rows, :] = jnp.broadcast_to(ls, (T, LANES))
                dr_rep[hh, rows, :] = jnp.zeros((T, LANES), F32)
            dq_acc[rows, :] = jnp.zeros((T, LANES), F32)
            return 0

        lax.fori_loop(0, nq, prologue, 0)

        def kv_step(kj, _):
            ks = pl.ds(pl.multiple_of(kj * T, T), T)
            kf = k_ref[ks, :].astype(F32)
            vf = v_ref[ks, :].astype(F32)
            dks, dvs = [], []
            for hh in range(2):
                msk = is0 if hh == 0 else jnp.logical_not(is0)
                km = jnp.where(msk, kf, 0.0).astype(BF16)
                vm = jnp.where(msk, vf, 0.0).astype(BF16)
                ncr = nc_ref[hh:hh + 1, ks]

                def tile(qi, carry, masked):
                    dk_a, dv_a, dc_a = carry
                    rows = pl.ds(pl.multiple_of(qi * T, T), T)
                    qb = q_ref[rows, :]
                    dob = do_ref[rows, :]
                    s = lax.dot_general(qb, km, _NT, preferred_element_type=F32) + ncr
                    lse_t = jnp.tile(lse_rep[hh, rows, :], (1, rep))
                    dl_t = jnp.tile(dl_rep[hh, rows, :], (1, rep))
                    p = jnp.exp(s - lse_t)
                    if masked:
                        p = jnp.where(tri, p, 0.0)
                    dp = lax.dot_general(dob, vm, _NT, preferred_element_type=F32)
                    ds = p * (dp - dl_t)
                    pb, dsb = p.astype(BF16), ds.astype(BF16)
                    dv_a = dv_a + lax.dot_general(pb, dob, _TN, preferred_element_type=F32)
                    dk_a = dk_a + lax.dot_general(dsb, qb, _TN, preferred_element_type=F32)
                    dq_acc[rows, :] += jnp.dot(dsb, km, preferred_element_type=F32)
                    dc_a = dc_a + jnp.sum(ds, axis=0, keepdims=True)
                    dr_rep[hh, rows, :] += jnp.broadcast_to(jnp.sum(ds, axis=1, keepdims=True), (T, LANES))
                    return dk_a, dv_a, dc_a

                init = (jnp.zeros((T, LANES), F32), jnp.zeros((T, LANES), F32), jnp.zeros((1, T), F32))
                carry = tile(kj, init, True)
                dk_a, dv_a, dc_a = lax.fori_loop(kj + 1, nq, lambda qi, c: tile(qi, c, False), carry)
                dks.append(dk_a)
                dvs.append(dv_a)
                dc_ref[hh:hh + 1, ks] = -dc_a
            dk_ref[ks, :] = jnp.where(is0, dks[0], dks[1])
            dv_ref[ks, :] = jnp.where(is0, dvs[0], dvs[1])
            return 0

        lax.fori_loop(0, nq, kv_step, 0)
        dq_ref[...] = (dq_acc[...] * scale).astype(dq_ref.dtype)
        first = lax.broadcasted_iota(jnp.int32, (S, LANES), 1) < HEAD_DIM
        dr_ref[...] = jnp.where(first, dr_rep[0], dr_rep[1])

    blk = lambda off: pl.BlockSpec((S, LANES), lambda p: (0, off + p))
    nc_spec = pl.BlockSpec((None, 2, S), lambda p: (p, 0, 0))
    return pl.pallas_call(
        body, name=name, grid=(HP,),
        in_specs=[blk(0), blk(0), blk(HP), nc_spec, blk(0), blk(0), blk(0)],
        out_specs=[blk(0), blk(0), blk(0), nc_spec, blk(0)],
        out_shape=[jax.ShapeDtypeStruct((S, D), BF16), jax.ShapeDtypeStruct((S, D), F32),
                   jax.ShapeDtypeStruct((S, D), F32), jax.ShapeDtypeStruct((HP, 2, S), F32),
                   jax.ShapeDtypeStruct((S, D), F32)],
        scratch_shapes=[pltpu.VMEM((S, LANES), F32), pltpu.VMEM((2, S, LANES), F32),
                        pltpu.VMEM((2, S, LANES), F32), pltpu.VMEM((2, S, LANES), F32)],
        compiler_params=_params(("parallel",)),
    )(q, kv, kv, negc3, o, do, lse)


def _logsig_fwd(name, f):
    S, C = f.shape

    def body(f_ref, o_ref):
        o_ref[...] = -_softplus(-f_ref[...])

    spec = pl.BlockSpec((S, C), lambda i: (0, 0))
    return pl.pallas_call(body, name=name, grid=(1,), in_specs=[spec], out_specs=spec,
                          out_shape=jax.ShapeDtypeStruct((S, C), F32),
                          compiler_params=_params(("arbitrary",)))(f)


def _logsig_bwd(name, dls, f):
    S, C = f.shape

    def body(d_ref, f_ref, o_ref, s_ref):
        df = d_ref[...] * _sigmoid(-f_ref[...])
        o_ref[...] = df.astype(o_ref.dtype)
        s_ref[...] = jnp.sum(df, axis=0, keepdims=True)

    spec = pl.BlockSpec((S, C), lambda i: (0, 0))
    return pl.pallas_call(body, name=name, grid=(1,), in_specs=[spec, spec],
                          out_specs=[spec, pl.BlockSpec((1, C), lambda i: (0, 0))],
                          out_shape=[jax.ShapeDtypeStruct((S, C), BF16), jax.ShapeDtypeStruct((1, C), F32)],
                          compiler_params=_params(("arbitrary",)))(dls, f)


def _add_cast(name, parts, out_dtype, tr=256):
    S, C = parts[0].shape
    tr = _tile(S, tr)
    n = len(parts)

    def body(*refs):
        acc = refs[0][...].astype(F32)
        for r in refs[1:n]:
            acc = acc + r[...].astype(F32)
        refs[n][...] = acc.astype(out_dtype)

    spec = pl.BlockSpec((tr, C), lambda i: (i, 0))
    return pl.pallas_call(body, name=name, grid=(S // tr,), in_specs=[spec] * n, out_specs=spec,
                          out_shape=jax.ShapeDtypeStruct((S, C), out_dtype),
                          compiler_params=_params(("parallel",)))(*parts)


def _local_step(x, target, W):
    S, D = x.shape
    L = W["norm_mix"].shape[0]
    NA = W["w_rec_in"].shape[0]
    NB = L - NA
    C = W["w_rec_out"].shape[1]
    F = W["w_ffn_out"].shape[1]
    FH = F // 2
    CH = C // 2
    HP = D // LANES
    scale = HEAD_DIM ** -0.5
    saved = []
    h = x

    def ffn_fwd(l, h_mid):
        hn = _rmsnorm_fwd(f"ffn_norm_{l}", h_mid, W["norm_ffn"][l])
        z3 = _mm(f"ffn_in_{l}", "nn", hn, W["w_ffn_in"], grid=(S // _tile(S, 512), N_CHIPS),
                 a_spec=pl.BlockSpec((_tile(S, 512), D), lambda i, j: (i, 0)),
                 b_spec=pl.BlockSpec((None, None, D, FH), lambda i, j: (l, j, 0, 0)),
                 out_shape=(2, S, F), out_dtype=BF16,
                 out_spec=pl.BlockSpec((None, _tile(S, 512), FH), lambda i, j: (j // 2, i, j % 2)))
        act = _swiglu_fwd(f"swiglu_{l}", z3)
        h_out = _mm_nn(f"ffn_out_{l}", act, W["w_ffn_out"], b_lead=(l,), out_dtype=F32, res=h_mid, tn=D)
        return h_out, (hn, z3, act)

    kv = negc3 = f_pre = hn_kv = h_kv = None
    for l in range(L):
        xn = _rmsnorm_fwd(f"mix_norm_{l}", h, W["norm_mix"][l])
        if l < NA:
            proj = _mm(f"rec_in_{l}", "nn", xn, W["w_rec_in"], grid=(S // _tile(S, 512), N_CHIPS),
                       a_spec=pl.BlockSpec((_tile(S, 512), D), lambda i, j: (i, 0)),
                       b_spec=pl.BlockSpec((None, None, D, CH), lambda i, j: (l, j, 0, 0)),
                       out_shape=(S, 2 * C), out_dtype=F32,
                       out_spec=pl.BlockSpec((_tile(S, 512), CH), lambda i, j: (i, j)))
            rc, rcb = _conv_fwd(f"conv_{l}", proj, W["conv_w"][l], W["conv_b"][l])
            gip, grp = _gates_fwd(f"gates_{l}", rcb, W["w_gates"][l], W["b_gates"][l])
            hrec, m = _lru_fwd(f"lru_{l}", proj, rc, gip, grp, W["lru_param"][l])
            h_mid = _mm_nn(f"rec_out_{l}", m, W["w_rec_out"], b_lead=(l,), out_dtype=F32, res=h, tn=D)
            mix_saved = (xn, proj, rc, rcb, gip, grp, hrec, m)
        else:
            j = l - NA
            if j == 0:
                h_kv = h
                hn_kv = _rmsnorm_fwd("kv_norm", h, W["norm_kv"])
                kv = _mm_nn("kv_proj", hn_kv, W["w_kv"], out_dtype=BF16)
                f_pre = _mm_nn("f_proj", hn_kv, W["w_f"], out_dtype=F32, bias=W["b_f"])
                c = _cumsum_rows("c_cumsum", _logsig_fwd("logsig", f_pre), False)
                negc3 = (-c[:, :2 * HP]).T.reshape(HP, 2, S)
            q = _mm_nn(f"q_proj_{j}", xn, W["w_q"], b_lead=(j,), out_dtype=BF16, scale=scale)
            o, of, lse = _attn_fwd(f"attn_fwd_{j}", q, kv, negc3)
            h_mid = _mm_nn(f"o_proj_{j}", o, W["w_o"], b_lead=(j,), out_dtype=F32, res=h, tn=D)
            mix_saved = (xn, q, o, of, lse)
        h_out, ffn_saved = ffn_fwd(l, h_mid)
        saved.append((h, h_mid, mix_saved, ffn_saved))
        h = h_out

    dh, dhb, dg_final, loss_row = _loss_head("loss_head", h, target, W["norm_final"])

    G = {"norm_final": dg_final, "norm_mix": [None] * L, "norm_ffn": [None] * L,
         "w_ffn_in": [None] * L, "w_ffn_out": [None] * L,
         "w_rec_in": [None] * NA, "w_gates": [None] * NA, "w_rec_out": [None] * NA,
         "conv_w": [None] * NA, "conv_b": [None] * NA, "b_gi": [None] * NA, "b_gr": [None] * NA,
         "lru_param": [None] * NA, "w_q": [None] * NB, "w_o": [None] * NB}
    dk_parts, dv_parts, dc_parts = [], [], []
    tm = _tile(S, 512)
    td = _tile(D, 512)

    for l in reversed(range(L)):
        h_in, h_mid, mix_saved, (hn, z3, act) = saved[l]
        G["w_ffn_out"][l] = _mm_tn(f"d_ffn_out_{l}", act, dhb, out_dtype=BF16, tn=D)
        da = _mm_nt(f"d_act_{l}", dhb, W["w_ffn_out"], b_lead=(l,), out_dtype=BF16, tn=FH)
        dz3 = _swiglu_bwd(f"d_swiglu_{l}", da, z3)
        G["w_ffn_in"][l] = _mm(
            f"d_ffn_in_{l}", "tn", hn, dz3, grid=(D // td, N_CHIPS),
            a_spec=pl.BlockSpec((S, td), lambda i, j: (0, i)),
            b_spec=pl.BlockSpec((None, S, FH), lambda i, j: (j // 2, 0, j % 2)),
            out_shape=(N_CHIPS, D, FH), out_dtype=BF16,
            out_spec=pl.BlockSpec((None, td, FH), lambda i, j: (j, i, 0)))
        dhn = _mm(f"d_ffn_hn_{l}", "nt", dz3, W["w_ffn_in"], grid=(S // tm, 1, N_CHIPS), nk=N_CHIPS,
                  a_spec=pl.BlockSpec((None, tm, FH), lambda i, j, k: (k // 2, i, k % 2)),
                  b_spec=pl.BlockSpec((None, None, D, FH), lambda i, j, k: (l, k, 0, 0)),
                  out_shape=(S, D), out_dtype=F32, out_spec=pl.BlockSpec((tm, D), lambda i, j, k: (i, 0)))
        dh, dhb, G["norm_ffn"][l] = _rmsnorm_bwd(f"d_ffn_norm_{l}", dhn, h_mid, W["norm_ffn"][l], dh)
        if l < NA:
            xn, proj, rc, rcb, gip, grp, hrec, m = mix_saved
            G["w_rec_out"][l] = _mm_tn(f"d_rec_out_{l}", m, dhb, out_dtype=BF16, tn=D)
            dm = _mm_nt(f"d_m_{l}", dhb, W["w_rec_out"], b_lead=(l,), out_dtype=F32, tn=C)
            dgb, dgi, dgr, drc1, dbi, dbr, dlp = _lru_bwd(f"d_lru_{l}", dm, proj, hrec, rc, gip, grp, W["lru_param"][l])
            drc, G["w_gates"][l] = _gates_bwd(f"d_gates_{l}", dgi, dgr, rcb, W["w_gates"][l], drc1)
            drec, G["conv_w"][l], G["conv_b"][l] = _conv_bwd(f"d_conv_{l}", drc, proj, W["conv_w"][l])
            G["b_gi"][l], G["b_gr"][l], G["lru_param"][l] = dbi, dbr, dlp
            dproj = jnp.concatenate([dgb, drec], axis=1)
            G["w_rec_in"][l] = _mm(
                f"d_rec_in_{l}", "tn", xn, dproj, grid=(1, N_CHIPS),
                a_spec=pl.BlockSpec((S, D), lambda i, j: (0, 0)),
                b_spec=pl.BlockSpec((S, CH), lambda i, j: (0, j)),
                out_shape=(N_CHIPS, D, CH), out_dtype=BF16,
                out_spec=pl.BlockSpec((None, D, CH), lambda i, j: (j, 0, 0)))
            dxn = _mm(f"d_rec_xn_{l}", "nt", dproj, W["w_rec_in"], grid=(S // tm, 1, N_CHIPS), nk=N_CHIPS,
                      a_spec=pl.BlockSpec((tm, CH), lambda i, j, k: (i, k)),
                      b_spec=pl.BlockSpec((None, None, D, CH), lambda i, j, k: (l, k, 0, 0)),
                      out_shape=(S, D), out_dtype=F32, out_spec=pl.BlockSpec((tm, D), lambda i, j, k: (i, 0)))
        else:
            j = l - NA
            xn, q, o, of, lse = mix_saved
            G["w_o"][j] = _mm_tn(f"d_o_proj_{j}", o, dhb, out_dtype=BF16, tn=D)
            do = _mm_nt(f"d_o_{j}", dhb, W["w_o"], b_lead=(j,), out_dtype=BF16, tn=D)
            dq, dk, dv, dc3, dcr = _attn_bwd(f"attn_bwd_{j}", q, kv, negc3, of, do, lse)
            dk_parts.append(dk)
            dv_parts.append(dv)
            dc_parts.append(dc3.reshape(2 * HP, S).T + dcr[:, ::HEAD_DIM])
            G["w_q"][j] = _mm_tn(f"d_q_proj_{j}", xn, dq, out_dtype=BF16, tn=D)
            dxn = _mm_nt(f"d_q_xn_{j}", dq, W["w_q"], b_lead=(j,), out_dtype=F32, tn=D)
        dh, dhb, G["norm_mix"][l] = _rmsnorm_bwd(f"d_mix_norm_{l}", dxn, h_in, W["norm_mix"][l], dh)
        if l == NA:
            dkb = _add_cast("dk_sum", dk_parts, BF16)
            dvb = _add_cast("dv_sum", dv_parts, BF16)
            dkv = jnp.concatenate([dkb, dvb], axis=1)
            dc = sum(dc_parts[1:], dc_parts[0])
            dc_pad = jnp.pad(dc, ((0, 0), (0, LANES - 2 * HP)))
            dls = _cumsum_rows("dc_cumsum", dc_pad, True)
            dfb, G["b_f"] = _logsig_bwd("d_logsig", dls, f_pre)
            G["w_kv"] = _mm_tn("d_kv_proj", hn_kv, dkv, out_dtype=BF16)
            G["w_f"] = _mm_tn("d_f_proj", hn_kv, dfb, out_dtype=F32)
            dhn1 = _mm_nt("d_kv_hn", dkv, W["w_kv"], out_dtype=F32, tn=D)
            dhn2 = _mm_nt("d_f_hn", dfb, W["w_f"], out_dtype=F32, tn=D)
            dhn_kv = _add_cast("d_kv_hn_sum", [dhn1, dhn2], F32)
            dh, dhb, G["norm_kv"] = _rmsnorm_bwd("d_kv_norm", dhn_kv, h_kv, W["norm_kv"], dh)
    return loss_row, dh, G


_ANY = pl.BlockSpec(memory_space=pl.ANY)


def _position():
    return lax.axis_index("x"), lax.axis_index("y"), lax.axis_index("c")


def _chip_peers(x, y):
    return [(1 - x, y), (x, 1 - y), (1 - x, 1 - y)]


def _half_rows(c, n):
    h = n // 2
    assert h % 16 == 0
    return pl.ds(pl.multiple_of(c * h, 16), h)


def _items(arrays):
    return [(w, l) for w, a in enumerate(arrays) for l in range(a.shape[0])]


def _gather_ici(name, shards, smalls):
    nw, ns = len(shards), len(smalls)
    items = _items(shards)
    n = len(items)
    out_shapes = [jax.ShapeDtypeStruct((s.shape[0], N_CHIPS) + s.shape[1:], s.dtype) for s in shards]
    out_shapes += [jax.ShapeDtypeStruct((N_CHIPS,) + s.shape, s.dtype) for s in smalls]

    def body(*refs):
        ins, s_ins = refs[:nw], refs[nw:nw + ns]
        outs, s_outs = refs[nw + ns:2 * nw + ns], refs[2 * nw + ns:2 * (nw + ns)]
        send_sems, recv_sems, local_sems = refs[2 * (nw + ns):]
        x, y, c = _position()
        me = 2 * x + y
        peers = _chip_peers(x, y)

        def remote(i, k, chip):
            px, py = peers[k]
            if i < n:
                w, l = items[i]
                half = _half_rows(c, ins[w].shape[1])
                src, dst = ins[w].at[l, half], outs[w].at[l, chip, half]
            else:
                src, dst = s_ins[i - n], s_outs[i - n].at[chip]
            return pltpu.make_async_remote_copy(
                src_ref=src, dst_ref=dst, send_sem=send_sems.at[3 * i + k], recv_sem=recv_sems.at[3 * i + k],
                device_id=(px, py, c), device_id_type=MESH)

        local = [pltpu.make_async_copy(ins[w], outs[w].at[:, me], local_sems.at[w]) for w in range(nw)]
        local += [pltpu.make_async_copy(s_ins[t], s_outs[t].at[me], local_sems.at[nw + t]) for t in range(ns)]
        for cp in local:
            cp.start()
        for i in range(n + ns):
            for k in range(3):
                remote(i, k, me).start()
        for i in range(n + ns):
            for k in range(3):
                px, py = peers[k]
                remote(i, k, 2 * px + py).wait_recv()
        for i in range(n + ns):
            for k in range(3):
                remote(i, k, me).wait_send()
        for cp in local:
            cp.wait()

    return pl.pallas_call(
        body, name=name, in_specs=[_ANY] * (nw + ns), out_specs=[_ANY] * (nw + ns), out_shape=out_shapes,
        scratch_shapes=[pltpu.SemaphoreType.DMA((3 * (n + ns),)), pltpu.SemaphoreType.DMA((3 * (n + ns),)),
                        pltpu.SemaphoreType.DMA((nw + ns,))],
    )(*shards, *smalls)


def _gather_d2d(name, gathered):
    nw = len(gathered)
    items = _items(gathered)
    n = len(items)

    def body(*refs):
        ins, outs = refs[:nw], refs[nw:2 * nw]
        send_sems, recv_sems = refs[2 * nw:]
        x, y, c = _position()
        peers = _chip_peers(x, y)

        def remote(i, k, core):
            w, l = items[i]
            px, py = peers[k]
            half = _half_rows(core, ins[w].shape[2])
            return pltpu.make_async_remote_copy(
                src_ref=ins[w].at[l, 2 * px + py, half], dst_ref=outs[w].at[l, 2 * px + py, half],
                send_sem=send_sems.at[3 * i + k], recv_sem=recv_sems.at[3 * i + k],
                device_id=(x, y, 1 - c), device_id_type=MESH)

        for i in range(n):
            for k in range(3):
                remote(i, k, c).start()
        for i in range(n):
            for k in range(3):
                remote(i, k, 1 - c).wait_recv()
        for i in range(n):
            for k in range(3):
                remote(i, k, c).wait_send()

    return pl.pallas_call(
        body, name=name, in_specs=[_ANY] * nw, out_specs=[_ANY] * nw,
        out_shape=[jax.ShapeDtypeStruct(g.shape, g.dtype) for g in gathered],
        input_output_aliases={w: w for w in range(nw)},
        scratch_shapes=[pltpu.SemaphoreType.DMA((3 * n,)), pltpu.SemaphoreType.DMA((3 * n,))],
    )(*gathered)


def _reduce_d2d(name, grads):
    n = len(grads)

    def body(*refs):
        ins, outs = refs[:n], refs[n:2 * n]
        send_sems, recv_sems, local_sems = refs[2 * n:]
        x, y, c = _position()
        local = [pltpu.make_async_copy(ins[i].at[:, _half_rows(c, ins[i].shape[1])], outs[i].at[0], local_sems.at[i])
                 for i in range(n)]
        remote = [pltpu.make_async_remote_copy(
            src_ref=ins[i].at[:, _half_rows(1 - c, ins[i].shape[1])], dst_ref=outs[i].at[1],
            send_sem=send_sems.at[i], recv_sem=recv_sems.at[i],
            device_id=(x, y, 1 - c), device_id_type=MESH) for i in range(n)]
        for i in range(n):
            remote[i].start()
            local[i].start()
        for i in range(n):
            remote[i].wait_recv()
        for i in range(n):
            remote[i].wait_send()
            local[i].wait()

    return pl.pallas_call(
        body, name=name, in_specs=[_ANY] * n, out_specs=[_ANY] * n,
        out_shape=[jax.ShapeDtypeStruct((2, N_CHIPS, g.shape[1] // 2, g.shape[2]), g.dtype) for g in grads],
        scratch_shapes=[pltpu.SemaphoreType.DMA((n,)), pltpu.SemaphoreType.DMA((n,)), pltpu.SemaphoreType.DMA((n,))],
    )(*grads)


def _scatter_ici(name, parts, layout):
    n = len(parts)
    nw = 1 + max(w for w, _ in layout)
    layers = [1 + max(l for w2, l in layout if w2 == w) for w in range(nw)]
    first = [next(i for i, (w2, _) in enumerate(layout) if w2 == w) for w in range(nw)]
    out_shapes = [jax.ShapeDtypeStruct((N_CHIPS, layers[w]) + parts[first[w]].shape[1:], parts[first[w]].dtype)
                  for w in range(nw)]

    def body(*refs):
        ins, outs = refs[:n], refs[n:n + nw]
        send_sems, recv_sems, local_sems = refs[n + nw:]
        x, y, c = _position()
        me = 2 * x + y
        peers = _chip_peers(x, y)

        def remote(i, k):
            w, l = layout[i]
            px, py = peers[k]
            return pltpu.make_async_remote_copy(
                src_ref=ins[i].at[2 * px + py], dst_ref=outs[w].at[k, l],
                send_sem=send_sems.at[3 * i + k], recv_sem=recv_sems.at[3 * i + k],
                device_id=(px, py, c), device_id_type=MESH)

        local = [pltpu.make_async_copy(ins[i].at[me], outs[layout[i][0]].at[3, layout[i][1]], local_sems.at[i])
                 for i in range(n)]
        for i in range(n):
            for k in range(3):
                remote(i, k).start()
            local[i].start()
        for i in range(n):
            for k in range(3):
                remote(i, k).wait_recv()
        for i in range(n):
            for k in range(3):
                remote(i, k).wait_send()
            local[i].wait()

    return pl.pallas_call(
        body, name=name, in_specs=[_ANY] * n, out_specs=[_ANY] * nw, out_shape=out_shapes,
        scratch_shapes=[pltpu.SemaphoreType.DMA((3 * n,)), pltpu.SemaphoreType.DMA((3 * n,)),
                        pltpu.SemaphoreType.DMA((n,))],
    )(*parts)


def _share_d2d(name, halves):
    n = len(halves)

    def body(*refs):
        ins, outs = refs[:n], refs[n:2 * n]
        send_sems, recv_sems, local_sems = refs[2 * n:]
        x, y, c = _position()

        def remote(w, core):
            return pltpu.make_async_remote_copy(
                src_ref=ins[w], dst_ref=outs[w].at[:, _half_rows(core, outs[w].shape[1])],
                send_sem=send_sems.at[w], recv_sem=recv_sems.at[w],
                device_id=(x, y, 1 - c), device_id_type=MESH)

        local = [pltpu.make_async_copy(ins[w], outs[w].at[:, _half_rows(c, outs[w].shape[1])], local_sems.at[w])
                 for w in range(n)]
        for w in range(n):
            remote(w, c).start()
            local[w].start()
        for w in range(n):
            remote(w, 1 - c).wait_recv()
        for w in range(n):
            remote(w, c).wait_send()
            local[w].wait()

    return pl.pallas_call(
        body, name=name, in_specs=[_ANY] * n, out_specs=[_ANY] * n,
        out_shape=[jax.ShapeDtypeStruct((h.shape[0], 2 * h.shape[1], h.shape[2]), h.dtype) for h in halves],
        scratch_shapes=[pltpu.SemaphoreType.DMA((n,)), pltpu.SemaphoreType.DMA((n,)), pltpu.SemaphoreType.DMA((n,))],
    )(*halves)


def _gather_all(name, a):
    def body(a_ref, o_ref, send_sems, recv_sems, local_sem):
        x, y, c = _position()
        me = 4 * x + 2 * y + c

        def peer(k):
            return (x ^ ((k >> 2) & 1), y ^ ((k >> 1) & 1), c ^ (k & 1))

        def remote(k, slot):
            return pltpu.make_async_remote_copy(
                src_ref=a_ref, dst_ref=o_ref.at[slot], send_sem=send_sems.at[k - 1], recv_sem=recv_sems.at[k - 1],
                device_id=peer(k), device_id_type=MESH)

        local = pltpu.make_async_copy(a_ref, o_ref.at[me], local_sem)
        local.start()
        for k in range(1, N_DEV):
            remote(k, me).start()
        for k in range(1, N_DEV):
            px, py, pc = peer(k)
            remote(k, 4 * px + 2 * py + pc).wait_recv()
        for k in range(1, N_DEV):
            remote(k, me).wait_send()
        local.wait()

    return pl.pallas_call(
        body, name=name, in_specs=[_ANY], out_specs=_ANY,
        out_shape=jax.ShapeDtypeStruct((N_DEV,) + a.shape, a.dtype),
        scratch_shapes=[pltpu.SemaphoreType.DMA((N_DEV - 1,)), pltpu.SemaphoreType.DMA((N_DEV - 1,)),
                        pltpu.SemaphoreType.DMA],
    )(a)


def _rows2d(a, lead=0):
    return a.reshape(a.shape[:lead] + (-1, a.shape[-1]))


def _row_tile(rows, cols, itemsize=4, target=1 << 20):
    want = max(SUBLANES, target // (cols * itemsize))
    t = min(rows, (want // 16) * 16)
    while t > 16 and rows % t:
        t -= 16
    return t if rows % t == 0 else rows


def _sum_slots(name, r, out_dtype=F32):
    ns = r.shape[0]
    r2 = _rows2d(r, 1)
    _, rows, cols = r2.shape
    tr = _row_tile(rows, cols)

    def body(r_ref, o_ref):
        acc = r_ref[0].astype(F32)
        for s in range(1, ns):
            acc = acc + r_ref[s].astype(F32)
        o_ref[...] = acc.astype(o_ref.dtype)

    out = pl.pallas_call(
        body, name=name, grid=(rows // tr,),
        in_specs=[pl.BlockSpec((ns, tr, cols), lambda i: (0, i, 0))],
        out_specs=pl.BlockSpec((tr, cols), lambda i: (i, 0)),
        out_shape=jax.ShapeDtypeStruct((rows, cols), out_dtype),
        compiler_params=_params(("parallel",)),
    )(r2)
    return out.reshape(r.shape[1:])


def _adamw(name, g_parts, w, m, v):
    shape = w.shape
    ng = len(g_parts)
    args = [_rows2d(a) for a in (*g_parts, w, m, v)]
    rows, cols = args[0].shape
    tr = _row_tile(rows, cols, target=1 << 19)
    c1 = 1.0 - ADAM_B1 ** ADAM_STEP
    c2 = 1.0 - ADAM_B2 ** ADAM_STEP

    def body(*refs):
        g = refs[0][...]
        for r in refs[1:ng]:
            g = g + r[...]
        w_ref, m_ref, v_ref = refs[ng:ng + 3]
        g_out, d_out, m_out, v_out = refs[ng + 3:]
        mn = ADAM_B1 * m_ref[...] + (1.0 - ADAM_B1) * g
        vn = ADAM_B2 * v_ref[...] + (1.0 - ADAM_B2) * (g * g)
        m_hat = mn / c1
        v_hat = vn / c2
        g_out[...] = g
        d_out[...] = -ADAM_LR * (m_hat / (jnp.sqrt(v_hat) + ADAM_EPS) + ADAM_WD * w_ref[...])
        m_out[...] = mn
        v_out[...] = vn

    spec = pl.BlockSpec((tr, cols), lambda i: (i, 0))
    outs = pl.pallas_call(
        body, name=name, grid=(rows // tr,), in_specs=[spec] * (ng + 3), out_specs=[spec] * 4,
        out_shape=[jax.ShapeDtypeStruct((rows, cols), F32)] * 4,
        compiler_params=_params(("parallel",)),
    )(*args)
    return tuple(o.reshape(shape) for o in outs)


_WEIGHTS = ["norm_mix", "norm_ffn", "w_ffn_in", "w_ffn_out", "w_rec_in", "conv_w", "conv_b", "w_lru_gates",
            "b_lru_gates", "lru_param", "w_rec_out", "norm_kv", "w_kvf", "b_forget", "w_q", "w_o", "norm_final"]
_BIG = ["w_ffn_in", "w_ffn_out", "w_rec_in", "w_lru_gates", "w_rec_out", "w_kvf", "w_q", "w_o"]


def _stack3(a):
    return a[None] if a.ndim == 2 else a.reshape(a.shape[0], -1, a.shape[-1])


def _pad_lanes(a, n):
    return jnp.pad(a, ((0, 0),) * (a.ndim - 1) + ((0, n - a.shape[-1]),))


def kernel(x, norm_mix, norm_ffn, w_ffn_in, w_ffn_out, w_rec_in, conv_w, conv_b, w_lru_gates, b_lru_gates, lru_param, w_rec_out, norm_kv, w_kvf, b_forget, w_q, w_o, norm_final, loss_target, m_norm_mix, m_norm_ffn, m_w_ffn_in, m_w_ffn_out, m_w_rec_in, m_conv_w, m_conv_b, m_w_lru_gates, m_b_lru_gates, m_lru_param, m_w_rec_out, m_norm_kv, m_w_kvf, m_b_forget, m_w_q, m_w_o, m_norm_final, v_norm_mix, v_norm_ffn, v_w_ffn_in, v_w_ffn_out, v_w_rec_in, v_conv_w, v_conv_b, v_w_lru_gates, v_b_lru_gates, v_lru_param, v_w_rec_out, v_norm_kv, v_w_kvf, v_b_forget, v_w_q, v_w_o, v_norm_final):
    P = dict(norm_mix=norm_mix, norm_ffn=norm_ffn, w_ffn_in=w_ffn_in, w_ffn_out=w_ffn_out, w_rec_in=w_rec_in,
             conv_w=conv_w, conv_b=conv_b, w_lru_gates=w_lru_gates, b_lru_gates=b_lru_gates, lru_param=lru_param,
             w_rec_out=w_rec_out, norm_kv=norm_kv, w_kvf=w_kvf, b_forget=b_forget, w_q=w_q, w_o=w_o,
             norm_final=norm_final)
    M1 = dict(norm_mix=m_norm_mix, norm_ffn=m_norm_ffn, w_ffn_in=m_w_ffn_in, w_ffn_out=m_w_ffn_out,
              w_rec_in=m_w_rec_in, conv_w=m_conv_w, conv_b=m_conv_b, w_lru_gates=m_w_lru_gates,
              b_lru_gates=m_b_lru_gates, lru_param=m_lru_param, w_rec_out=m_w_rec_out, norm_kv=m_norm_kv,
              w_kvf=m_w_kvf, b_forget=m_b_forget, w_q=m_w_q, w_o=m_w_o, norm_final=m_norm_final)
    M2 = dict(norm_mix=v_norm_mix, norm_ffn=v_norm_ffn, w_ffn_in=v_w_ffn_in, w_ffn_out=v_w_ffn_out,
              w_rec_in=v_w_rec_in, conv_w=v_conv_w, conv_b=v_conv_b, w_lru_gates=v_w_lru_gates,
              b_lru_gates=v_b_lru_gates, lru_param=v_lru_param, w_rec_out=v_w_rec_out, norm_kv=v_norm_kv,
              w_kvf=v_w_kvf, b_forget=v_b_forget, w_q=v_w_q, w_o=v_w_o, norm_final=v_norm_final)

    _, S, D = x.shape
    L = norm_mix.shape[0]
    NA, NBLK, BW, GS = w_lru_gates.shape
    NB = w_q.shape[0]
    C = NBLK * BW
    CS = C // N_CHIPS
    H = b_forget.shape[0]
    assert C == D and H * HEAD_DIM == D and H <= LANES
    chip = 2 * lax.axis_index("x") + lax.axis_index("y")

    small_a = jnp.concatenate([conv_w, conv_b[:, None], lru_param[:, None]], axis=1)
    shards = [_stack3(P[w]).astype(BF16) for w in _BIG]
    gathered = _gather_ici("gather_ici", shards, [small_a, b_lru_gates])
    small_a = gathered[-2].transpose(1, 2, 0, 3).reshape(NA, 6, C)
    b_gates = gathered[-1].transpose(1, 2, 0, 3).reshape(NA, NBLK, 1, N_CHIPS * GS)
    Gw = dict(zip(_BIG, _gather_d2d("gather_d2d", gathered[:len(_BIG)])))
    w_kvf_full = Gw["w_kvf"][0].transpose(1, 0, 2).reshape(D, -1)
    W = dict(
        w_ffn_in=Gw["w_ffn_in"],
        w_ffn_out=Gw["w_ffn_out"].reshape(L, -1, D),
        w_rec_in=Gw["w_rec_in"],
        w_gates=Gw["w_lru_gates"].reshape(NA, N_CHIPS, NBLK, BW, GS).transpose(0, 2, 3, 1, 4).reshape(
            NA, NBLK, BW, N_CHIPS * GS),
        b_gates=b_gates,
        w_rec_out=Gw["w_rec_out"].reshape(NA, C, D),
        w_kv=w_kvf_full[:, :2 * D],
        w_f=_pad_lanes(w_kvf_full[:, 2 * D:], LANES),
        w_q=Gw["w_q"].reshape(NB, D, D),
        w_o=Gw["w_o"].reshape(NB, D, D),
        conv_w=small_a[:, :4], conv_b=small_a[:, 4:5], lru_param=small_a[:, 5:6],
        norm_mix=norm_mix[:, None], norm_ffn=norm_ffn[:, None], norm_kv=norm_kv[None], norm_final=norm_final[None],
        b_f=_pad_lanes(b_forget[None], LANES),
    )

    loss_row, grad_x, G = _local_step(x.reshape(S, D), loss_target.reshape(S, D), W)

    rows = [*G["norm_mix"], *G["norm_ffn"], G["norm_kv"], G["norm_final"],
            _pad_lanes(G["b_f"], D), _pad_lanes(loss_row, D)]
    for a in range(NA):
        rows += [G["conv_w"][a], G["conv_b"][a], G["b_gi"][a], G["b_gr"][a], G["lru_param"][a]]
    packed = jnp.concatenate(rows, axis=0)
    tot = _sum_slots("sum_small", _gather_all("gather_small", packed))
    loss = tot[2 * L + 3, 0]
    g_rep = jnp.concatenate([tot[:2 * L + 2], tot[2 * L + 2:2 * L + 3]], axis=0)
    base = 2 * L + 4
    g_sh = []
    for a in range(NA):
        blk = lax.dynamic_slice_in_dim(tot[base + 8 * a:base + 8 * a + 8], chip * CS, CS, axis=1)
        gi = tot[base + 8 * a + 5].reshape(NBLK, BW)
        gr = tot[base + 8 * a + 6].reshape(NBLK, BW)
        bl = lax.dynamic_slice_in_dim(jnp.concatenate([gi, gr], axis=1), chip * GS, GS, axis=1)
        g_sh += [blk[:5], bl.reshape(-1, CS), blk[7:8]]
    g_sh = jnp.concatenate(g_sh, axis=0)
    nrow = g_sh.shape[0] // NA

    def pack_rep(T):
        return jnp.concatenate([T["norm_mix"], T["norm_ffn"], T["norm_kv"][None], T["norm_final"][None],
                                _pad_lanes(T["b_forget"][None], D)], axis=0)

    def pack_sh(T):
        return jnp.concatenate([jnp.concatenate([T["conv_w"][a], T["conv_b"][a][None],
                                                 T["b_lru_gates"][a].reshape(-1, CS), T["lru_param"][a][None]], axis=0)
                                for a in range(NA)], axis=0)

    rep = _adamw("adamw_replicated", [g_rep], pack_rep(P), pack_rep(M1), pack_rep(M2))
    shd = _adamw("adamw_small_sharded", [g_sh], pack_sh(P), pack_sh(M1), pack_sh(M2))

    def unpack_rep(t):
        return dict(norm_mix=t[:L], norm_ffn=t[L:2 * L], norm_kv=t[2 * L], norm_final=t[2 * L + 1],
                    b_forget=t[2 * L + 2, :H])

    def unpack_sh(t):
        t = t.reshape(NA, nrow, CS)
        return dict(conv_w=t[:, :4], conv_b=t[:, 4], b_lru_gates=t[:, 5:nrow - 1].reshape(NA, NBLK, GS),
                    lru_param=t[:, nrow - 1])

    dkvf = jnp.concatenate([G["w_kv"].astype(F32), G["w_f"][:, :H]], axis=1)
    groups = [
        G["w_ffn_in"],
        [g.reshape(N_CHIPS, -1, D) for g in G["w_ffn_out"]],
        G["w_rec_in"],
        [g.reshape(NBLK, BW, N_CHIPS, GS).transpose(2, 0, 1, 3).reshape(N_CHIPS, NBLK * BW, GS) for g in G["w_gates"]],
        [g.reshape(N_CHIPS, -1, D) for g in G["w_rec_out"]],
        [dkvf.reshape(D, N_CHIPS, -1).transpose(1, 0, 2).astype(BF16)],
        [g.reshape(N_CHIPS, -1, D) for g in G["w_q"]],
        [g.reshape(N_CHIPS, -1, D) for g in G["w_o"]],
    ]
    layout = [(w, l) for w, layers in enumerate(groups) for l in range(len(layers))]
    flat = [g for layers in groups for g in layers]
    pairs = _reduce_d2d("reduce_d2d", flat)
    chip_parts = [_sum_slots(f"sum_cores_{_BIG[w]}_{l}", p, BF16) for (w, l), p in zip(layout, pairs)]
    received = _scatter_ici("scatter_ici", chip_parts, layout)
    halves = [_sum_slots(f"sum_chips_{w}", r) for w, r in zip(_BIG, received)]
    full = _share_d2d("share_d2d", halves)
    big = {w: _adamw(f"adamw_{w}", [g.reshape(P[w].shape)], P[w], M1[w], M2[w]) for w, g in zip(_BIG, full)}

    outs = []
    for i in range(4):
        small = {**unpack_rep(rep[i]), **unpack_sh(shd[i])}
        outs.append([big[w][i] if w in big else small[w] for w in _WEIGHTS])
    return (loss, grad_x.reshape(1, S, D), *outs[0], *outs[1], *outs[2], *outs[3])
```

```python
import functools
import math

import jax
import jax.numpy as jnp
from jax import lax
from jax.experimental import pallas as pl
from jax.experimental.pallas import tpu as pltpu

F32 = jnp.float32
BF16 = jnp.bfloat16

EPS = 1e-6
LRU_C = 8.0
HEAD_DIM = 64
LANES = 128
SUBLANES = 8
VMEM_LIMIT = 48 * 1024 * 1024
N_CHIPS = 4
N_DEV = 8

ADAM_LR = 0.001
ADAM_B1 = 0.9
ADAM_B2 = 0.999
ADAM_EPS = 1e-08
ADAM_WD = 0.01
ADAM_STEP = 10

_NN = (((1,), (0,)), ((), ()))
_NT = (((1,), (1,)), ((), ()))
_TN = (((0,), (0,)), ((), ()))
_DN = {"nn": _NN, "nt": _NT, "tn": _TN}
MESH = pl.DeviceIdType.MESH


def _params(sem):
    return pltpu.CompilerParams(dimension_semantics=sem, vmem_limit_bytes=VMEM_LIMIT)


def _tile(n, want):
    if n <= want:
        return n
    t = (want // LANES) * LANES
    while t >= LANES:
        if n % t == 0:
            return t
        t -= LANES
    return n


def _sigmoid(x):
    return 1.0 / (1.0 + jnp.exp(-x))


def _softplus(x):
    return jnp.maximum(x, 0.0) + jnp.log(1.0 + jnp.exp(-jnp.abs(x)))


_GELU_C = math.sqrt(2.0 / math.pi)


def _gelu_and_grad(x):
    inner = _GELU_C * (x + 0.044715 * x * x * x)
    t = jnp.tanh(inner)
    g = 0.5 * x * (1.0 + t)
    dg = 0.5 * (1.0 + t) + 0.5 * x * (1.0 - t * t) * _GELU_C * (1.0 + 3.0 * 0.044715 * x * x)
    return g, dg


def _mm(name, mode, a, b, *, grid, a_spec, b_spec, out_shape, out_dtype, out_spec, nk=1,
        res=None, res_spec=None, bias=None, bias_spec=None, scale=None):
    dn = _DN[mode]
    has_res, has_bias = res is not None, bias is not None
    blk = tuple(d for d in out_spec.block_shape if d is not None)

    def body(*refs):
        a_ref, b_ref = refs[0], refs[1]
        p = 2
        r_ref = refs[p] if has_res else None
        p += int(has_res)
        bias_ref = refs[p] if has_bias else None
        p += int(has_bias)
        o_ref = refs[p]
        part = lax.dot_general(a_ref[...], b_ref[...], dn, preferred_element_type=F32)

        def finish(acc):
            if scale is not None:
                acc = acc * scale
            if has_bias:
                acc = acc + bias_ref[...]
            if has_res:
                acc = r_ref[...] + acc
            o_ref[...] = acc.astype(o_ref.dtype)

        if nk == 1:
            finish(part)
        else:
            acc_ref = refs[p + 1]
            k = pl.program_id(2)

            @pl.when(k == 0)
            def _():
                acc_ref[...] = part

            @pl.when(k > 0)
            def _():
                acc_ref[...] += part

            @pl.when(k == nk - 1)
            def _():
                finish(acc_ref[...])

    ins, specs = [a, b], [a_spec, b_spec]
    if has_res:
        ins.append(res)
        specs.append(res_spec)
    if has_bias:
        ins.append(bias)
        specs.append(bias_spec)
    sem = ("parallel", "parallel") + (("arbitrary",) if len(grid) == 3 else ())
    return pl.pallas_call(
        body, name=name, grid=grid, in_specs=specs, out_specs=out_spec,
        out_shape=jax.ShapeDtypeStruct(out_shape, out_dtype),
        scratch_shapes=[pltpu.VMEM(blk, F32)] if nk > 1 else [],
        compiler_params=_params(sem),
    )(*ins)


def _mm_nn(name, a, b, *, b_lead=(), out_dtype, tm=512, tn=512, res=None, bias=None, scale=None):
    M, K = a.shape
    N = b.shape[-1]
    tm, tn = _tile(M, tm), _tile(N, tn)
    nl = len(b_lead)
    return _mm(
        name, "nn", a, b, grid=(M // tm, N // tn),
        a_spec=pl.BlockSpec((tm, K), lambda i, j: (i, 0)),
        b_spec=pl.BlockSpec((None,) * nl + (K, tn), lambda i, j: tuple(b_lead) + (0, j)),
        out_shape=(M, N), out_dtype=out_dtype, out_spec=pl.BlockSpec((tm, tn), lambda i, j: (i, j)),
        res=res, res_spec=pl.BlockSpec((tm, tn), lambda i, j: (i, j)),
        bias=bias, bias_spec=pl.BlockSpec((1, tn), lambda i, j: (0, j)), scale=scale)


def _mm_nt(name, a, b, *, b_lead=(), out_dtype, tm=512, tn=512, tk=2048):
    M, K = a.shape
    N = b.shape[-2]
    tm, tn, tk = _tile(M, tm), _tile(N, tn), _tile(K, tk)
    nk = K // tk
    nl = len(b_lead)
    return _mm(
        name, "nt", a, b, grid=(M // tm, N // tn, nk), nk=nk,
        a_spec=pl.BlockSpec((tm, tk), lambda i, j, k: (i, k)),
        b_spec=pl.BlockSpec((None,) * nl + (tn, tk), lambda i, j, k: tuple(b_lead) + (j, k)),
        out_shape=(M, N), out_dtype=out_dtype, out_spec=pl.BlockSpec((tm, tn), lambda i, j, k: (i, j)))


def _mm_tn(name, a, b, *, out_dtype, tm=512, tn=512):
    S, M = a.shape
    N = b.shape[1]
    tm, tn = _tile(M, tm), _tile(N, tn)
    return _mm(
        name, "tn", a, b, grid=(M // tm, N // tn),
        a_spec=pl.BlockSpec((S, tm), lambda i, j: (0, i)),
        b_spec=pl.BlockSpec((S, tn), lambda i, j: (0, j)),
        out_shape=(M, N), out_dtype=out_dtype, out_spec=pl.BlockSpec((tm, tn), lambda i, j: (i, j)))


def _rmsnorm_fwd(name, h, g, tr=256):
    S, D = h.shape
    tr = _tile(S, tr)

    def body(h_ref, g_ref, o_ref):
        x = h_ref[...]
        r = lax.rsqrt(jnp.mean(x * x, axis=-1, keepdims=True) + EPS)
        o_ref[...] = (x * r * g_ref[...]).astype(o_ref.dtype)

    return pl.pallas_call(
        body, name=name, grid=(S // tr,),
        in_specs=[pl.BlockSpec((tr, D), lambda i: (i, 0)), pl.BlockSpec((1, D), lambda i: (0, 0))],
        out_specs=pl.BlockSpec((tr, D), lambda i: (i, 0)),
        out_shape=jax.ShapeDtypeStruct((S, D), BF16),
        compiler_params=_params(("parallel",)),
    )(h, g)


def _rmsnorm_bwd(name, dxn, h, g, dh_in, tr=256):
    S, D = h.shape
    tr = _tile(S, tr)

    def body(dxn_ref, h_ref, g_ref, dh_ref, o_ref, ob_ref, dg_ref):
        i = pl.program_id(0)
        x = h_ref[...]
        dy = dxn_ref[...].astype(F32)
        r = lax.rsqrt(jnp.mean(x * x, axis=-1, keepdims=True) + EPS)
        xr = x * r
        dyg = dy * g_ref[...]
        dx = r * dyg - xr * (r * jnp.mean(dyg * xr, axis=-1, keepdims=True))
        out = dh_ref[...] + dx
        o_ref[...] = out
        ob_ref[...] = out.astype(BF16)
        part = jnp.sum(dy * xr, axis=0, keepdims=True)

        @pl.when(i == 0)
        def _():
            dg_ref[...] = part

        @pl.when(i > 0)
        def _():
            dg_ref[...] += part

    row = pl.BlockSpec((tr, D), lambda i: (i, 0))
    vec = pl.BlockSpec((1, D), lambda i: (0, 0))
    return pl.pallas_call(
        body, name=name, grid=(S // tr,),
        in_specs=[row, row, vec, row], out_specs=[row, row, vec],
        out_shape=[jax.ShapeDtypeStruct((S, D), F32), jax.ShapeDtypeStruct((S, D), BF16),
                   jax.ShapeDtypeStruct((1, D), F32)],
        compiler_params=_params(("arbitrary",)),
    )(dxn, h, g, dh_in)


def _loss_head(name, h, target, g, tr=256):
    S, D = h.shape
    tr = _tile(S, tr)

    def body(h_ref, t_ref, g_ref, o_ref, ob_ref, dg_ref, loss_ref):
        i = pl.program_id(0)
        x = h_ref[...]
        gg = g_ref[...]
        r = lax.rsqrt(jnp.mean(x * x, axis=-1, keepdims=True) + EPS)
        xr = x * r
        err = xr * gg - t_ref[...]
        lpart = 0.5 * jnp.sum(jnp.mean(err * err, axis=-1, keepdims=True), axis=0, keepdims=True)
        dy = err * (1.0 / D)
        dyg = dy * gg
        dx = r * dyg - xr * (r * jnp.mean(dyg * xr, axis=-1, keepdims=True))
        o_ref[...] = dx
        ob_ref[...] = dx.astype(BF16)
        part = jnp.sum(dy * xr, axis=0, keepdims=True)
        lrow = jnp.broadcast_to(lpart, (1, LANES))

        @pl.when(i == 0)
        def _():
            dg_ref[...] = part
            loss_ref[...] = lrow

        @pl.when(i > 0)
        def _():
            dg_ref[...] += part
            loss_ref[...] += lrow

    row = pl.BlockSpec((tr, D), lambda i: (i, 0))
    vec = pl.BlockSpec((1, D), lambda i: (0, 0))
    return pl.pallas_call(
        body, name=name, grid=(S // tr,),
        in_specs=[row, row, vec], out_specs=[row, row, vec, pl.BlockSpec((1, LANES), lambda i: (0, 0))],
        out_shape=[jax.ShapeDtypeStruct((S, D), F32), jax.ShapeDtypeStruct((S, D), BF16),
                   jax.ShapeDtypeStruct((1, D), F32), jax.ShapeDtypeStruct((1, LANES), F32)],
        compiler_params=_params(("arbitrary",)),
    )(h, target, g)


def _swiglu_fwd(name, z3, tr=256, tc=1408):
    _, S, F = z3.shape
    tr, tc = _tile(S, tr), _tile(F, tc)

    def body(z_ref, a_ref):
        zg = z_ref[0].astype(F32)
        zu = z_ref[1].astype(F32)
        a_ref[...] = (zg * _sigmoid(zg) * zu).astype(a_ref.dtype)

    return pl.pallas_call(
        body, name=name, grid=(S // tr, F // tc),
        in_specs=[pl.BlockSpec((2, tr, tc), lambda i, j: (0, i, j))],
        out_specs=pl.BlockSpec((tr, tc), lambda i, j: (i, j)),
        out_shape=jax.ShapeDtypeStruct((S, F), BF16),
        compiler_params=_params(("parallel", "parallel")),
    )(z3)


def _swiglu_bwd(name, da, z3, tr=256, tc=1408):
    _, S, F = z3.shape
    tr, tc = _tile(S, tr), _tile(F, tc)

    def body(da_ref, z_ref, dz_ref):
        zg = z_ref[0].astype(F32)
        zu = z_ref[1].astype(F32)
        d = da_ref[...].astype(F32)
        sg = _sigmoid(zg)
        silu = zg * sg
        dz_ref[0] = (d * zu * (sg * (1.0 + zg * (1.0 - sg)))).astype(dz_ref.dtype)
        dz_ref[1] = (d * silu).astype(dz_ref.dtype)

    return pl.pallas_call(
        body, name=name, grid=(S // tr, F // tc),
        in_specs=[pl.BlockSpec((tr, tc), lambda i, j: (i, j)),
                  pl.BlockSpec((2, tr, tc), lambda i, j: (0, i, j))],
        out_specs=pl.BlockSpec((2, tr, tc), lambda i, j: (0, i, j)),
        out_shape=jax.ShapeDtypeStruct((2, S, F), BF16),
        compiler_params=_params(("parallel", "parallel")),
    )(da, z3)


SCAN_ROWS = 64


def _group_scan(A, B, reverse):
    n = A.shape[0]
    sub = lax.broadcasted_iota(jnp.int32, A.shape, 0) % SUBLANES
    for d in (1, 2, 4):
        if reverse:
            A_sh, B_sh = pltpu.roll(A, n - d, 0), pltpu.roll(B, n - d, 0)
            keep = sub < SUBLANES - d
        else:
            A_sh, B_sh = pltpu.roll(A, d, 0), pltpu.roll(B, d, 0)
            keep = sub >= d
        B = jnp.where(keep, A * B_sh + B, B)
        A = jnp.where(keep, A * A_sh, A)
    return A, B


def _block_scan(a, u, carry, reverse):
    A, B = _group_scan(a, u, reverse)
    ng = a.shape[0] // SUBLANES
    out = [None] * ng
    order = range(ng - 1, -1, -1) if reverse else range(ng)
    for gi in order:
        sl = slice(gi * SUBLANES, (gi + 1) * SUBLANES)
        hg = A[sl] * carry + B[sl]
        out[gi] = hg
        carry = hg[0:1] if reverse else hg[SUBLANES - 1:SUBLANES]
    return jnp.concatenate(out, axis=0), carry


def _lru_gates(rc, gip, grp, sp):
    gi = _sigmoid(gip)
    gr = _sigmoid(grp)
    la = -LRU_C * gr * sp
    a = jnp.exp(la)
    om = -jnp.tanh(la) * (a * a + 1.0)
    mult = jnp.sqrt(om)
    return gi, gr, a, mult


def _lru_fwd(name, proj, rc, gip, grp, lru_p, tc=256):
    S, C = rc.shape
    tc = _tile(C, tc)
    nb = S // SCAN_ROWS

    def body(gb_ref, rc_ref, gi_ref, gr_ref, l_ref, h_ref, m_ref):
        sp = _softplus(-l_ref[...])

        def step(b, carry):
            rows = pl.ds(pl.multiple_of(b * SCAN_ROWS, SCAN_ROWS), SCAN_ROWS)
            rcb = rc_ref[rows, :]
            gi, _, a, mult = _lru_gates(rcb, gi_ref[rows, :], gr_ref[rows, :], sp)
            h, carry = _block_scan(a, rcb * gi * mult, carry, False)
            h_ref[rows, :] = h
            gel, _ = _gelu_and_grad(gb_ref[rows, :])
            m_ref[rows, :] = (gel * h).astype(m_ref.dtype)
            return carry

        lax.fori_loop(0, nb, step, jnp.zeros((1, tc), F32))

    col = pl.BlockSpec((S, tc), lambda j: (0, j))
    return pl.pallas_call(
        body, name=name, grid=(C // tc,),
        in_specs=[col, col, col, col, pl.BlockSpec((1, tc), lambda j: (0, j))],
        out_specs=[col, col],
        out_shape=[jax.ShapeDtypeStruct((S, C), F32), jax.ShapeDtypeStruct((S, C), BF16)],
        compiler_params=_params(("parallel",)),
    )(proj, rc, gip, grp, lru_p)


def _lru_bwd(name, dm, proj, hrec, rc, gip, grp, lru_p, tc=256):
    S, C = rc.shape
    tc = _tile(C, tc)
    nb = S // SCAN_ROWS
    R = SCAN_ROWS

    def body(dm_ref, gb_ref, h_ref, rc_ref, gi_ref, gr_ref, l_ref,
             dgb_ref, dgi_ref, dgr_ref, drc_ref, dbi_ref, dbr_ref, dl_ref):
        lp = l_ref[...]
        sp = _softplus(-lp)
        row = lax.broadcasted_iota(jnp.int32, (R, tc), 0)
        zero = jnp.zeros((1, tc), F32)

        def step(t, carry):
            mu_in, s_i, s_r, s_sp = carry
            b = nb - 1 - t
            r0 = pl.multiple_of(b * R, R)
            rows = pl.ds(r0, R)
            rcb = rc_ref[rows, :]
            gi, gr, a, mult = _lru_gates(rcb, gi_ref[rows, :], gr_ref[rows, :], sp)
            gel, dgel = _gelu_and_grad(gb_ref[rows, :])
            dmb = dm_ref[rows, :]
            h = h_ref[rows, :]
            dgb_ref[rows, :] = (dmb * h * dgel).astype(dgb_ref.dtype)
            dh = dmb * gel
            mu, mu_out = _block_scan(a, a * dh, mu_in, True)
            mu_next = jnp.where(row == R - 1, mu_in, pltpu.roll(mu, R - 1, 0))
            lam = dh + mu_next
            p0 = pl.multiple_of(jnp.maximum(r0 - SUBLANES, 0), SUBLANES)
            prev = h_ref[pl.ds(p0, SUBLANES), :][SUBLANES - 1:SUBLANES]
            prev = jnp.where(b > 0, prev, 0.0)
            h_prev = jnp.where(row == 0, prev, pltpu.roll(h, 1, 0))
            da = lam * h_prev
            d_mult = lam * rcb * gi
            d_la = da * a - d_mult * (a * a) / mult
            d_grp = d_la * (-LRU_C * sp) * gr * (1.0 - gr)
            d_gip = lam * rcb * mult * gi * (1.0 - gi)
            dgr_ref[rows, :] = d_grp.astype(dgr_ref.dtype)
            dgi_ref[rows, :] = d_gip.astype(dgi_ref.dtype)
            drc_ref[rows, :] = lam * gi * mult
            s_i = s_i + jnp.sum(d_gip, axis=0, keepdims=True)
            s_r = s_r + jnp.sum(d_grp, axis=0, keepdims=True)
            s_sp = s_sp + jnp.sum(d_la * gr, axis=0, keepdims=True)
            return mu_out, s_i, s_r, s_sp

        _, s_i, s_r, s_sp = lax.fori_loop(0, nb, step, (zero, zero, zero, zero))
        dbi_ref[...] = s_i
        dbr_ref[...] = s_r
        dl_ref[...] = (-LRU_C * s_sp) * (-_sigmoid(-lp))

    col = pl.BlockSpec((S, tc), lambda j: (0, j))
    vec = pl.BlockSpec((1, tc), lambda j: (0, j))
    return pl.pallas_call(
        body, name=name, grid=(C // tc,),
        in_specs=[col, col, col, col, col, col, vec],
        out_specs=[col, col, col, col, vec, vec, vec],
        out_shape=[jax.ShapeDtypeStruct((S, C), BF16), jax.ShapeDtypeStruct((S, C), BF16),
                   jax.ShapeDtypeStruct((S, C), BF16), jax.ShapeDtypeStruct((S, C), F32),
                   jax.ShapeDtypeStruct((1, C), F32), jax.ShapeDtypeStruct((1, C), F32),
                   jax.ShapeDtypeStruct((1, C), F32)],
        compiler_params=_params(("parallel",)),
    )(dm, proj, hrec, rc, gip, grp, lru_p)


def _cumsum_rows(name, u, reverse):
    S, C = u.shape
    nb = S // SCAN_ROWS

    def body(u_ref, o_ref):
        def step(t, carry):
            b = nb - 1 - t if reverse else t
            rows = pl.ds(pl.multiple_of(b * SCAN_ROWS, SCAN_ROWS), SCAN_ROWS)
            ub = u_ref[rows, :]
            h, carry = _block_scan(jnp.ones_like(ub), ub, carry, reverse)
            o_ref[rows, :] = h
            return carry

        lax.fori_loop(0, nb, step, jnp.zeros((1, C), F32))

    spec = pl.BlockSpec((S, C), lambda i: (0, 0))
    return pl.pallas_call(
        body, name=name, grid=(1,), in_specs=[spec], out_specs=spec,
        out_shape=jax.ShapeDtypeStruct((S, C), F32),
        compiler_params=_params(("arbitrary",)),
    )(u)


def _shift_down(x, k):
    row = lax.broadcasted_iota(jnp.int32, x.shape, 0)
    return jnp.where(row >= k, pltpu.roll(x, k, 0), 0.0)


def _shift_up(x, k):
    n = x.shape[0]
    row = lax.broadcasted_iota(jnp.int32, x.shape, 0)
    return jnp.where(row < n - k, pltpu.roll(x, n - k, 0), 0.0)


def _conv_fwd(name, proj, w, b, tc=256):
    S, C2 = proj.shape
    C = C2 // 2
    tc = _tile(C, tc)
    off = C // tc

    def body(x_ref, w_ref, b_ref, o_ref, ob_ref):
        x = x_ref[...]
        out = b_ref[...] + w_ref[3:4, :] * x
        for k in (1, 2, 3):
            out = out + w_ref[3 - k:4 - k, :] * _shift_down(x, k)
        o_ref[...] = out
        ob_ref[...] = out.astype(BF16)

    col = pl.BlockSpec((S, tc), lambda j: (0, j))
    return pl.pallas_call(
        body, name=name, grid=(C // tc,),
        in_specs=[pl.BlockSpec((S, tc), lambda j: (0, off + j)),
                  pl.BlockSpec((4, tc), lambda j: (0, j)), pl.BlockSpec((1, tc), lambda j: (0, j))],
        out_specs=[col, col],
        out_shape=[jax.ShapeDtypeStruct((S, C), F32), jax.ShapeDtypeStruct((S, C), BF16)],
        compiler_params=_params(("parallel",)),
    )(proj, w, b)


def _conv_bwd(name, drc, proj, w, tc=256):
    S, C = drc.shape
    tc = _tile(C, tc)
    off = C // tc

    def body(y_ref, x_ref, w_ref, dx_ref, dw_ref, db_ref):
        y = y_ref[...]
        x = x_ref[...]
        dx = w_ref[3:4, :] * y
        dw_ref[3:4, :] = jnp.sum(y * x, axis=0, keepdims=True)
        for k in (1, 2, 3):
            dx = dx + w_ref[3 - k:4 - k, :] * _shift_up(y, k)
            dw_ref[3 - k:4 - k, :] = jnp.sum(y * _shift_down(x, k), axis=0, keepdims=True)
        dx_ref[...] = dx.astype(dx_ref.dtype)
        db_ref[...] = jnp.sum(y, axis=0, keepdims=True)

    col = pl.BlockSpec((S, tc), lambda j: (0, j))
    return pl.pallas_call(
        body, name=name, grid=(C // tc,),
        in_specs=[col, pl.BlockSpec((S, tc), lambda j: (0, off + j)), pl.BlockSpec((4, tc), lambda j: (0, j))],
        out_specs=[col, pl.BlockSpec((4, tc), lambda j: (0, j)), pl.BlockSpec((1, tc), lambda j: (0, j))],
        out_shape=[jax.ShapeDtypeStruct((S, C), BF16), jax.ShapeDtypeStruct((4, C), F32),
                   jax.ShapeDtypeStruct((1, C), F32)],
        compiler_params=_params(("parallel",)),
    )(drc, proj, w)


def _gates_fwd(name, rcb, wg, bg):
    S, C = rcb.shape
    nblk, bw, _ = wg.shape

    def body(x_ref, w_ref, b_ref, gi_ref, gr_ref):
        g = jnp.dot(x_ref[...], w_ref[...], preferred_element_type=F32) + b_ref[...]
        gi_ref[...] = g[:, :bw]
        gr_ref[...] = g[:, bw:]

    col = pl.BlockSpec((S, bw), lambda n: (0, n))
    return pl.pallas_call(
        body, name=name, grid=(nblk,),
        in_specs=[col, pl.BlockSpec((None, bw, 2 * bw), lambda n: (n, 0, 0)),
                  pl.BlockSpec((None, 1, 2 * bw), lambda n: (n, 0, 0))],
        out_specs=[col, col],
        out_shape=[jax.ShapeDtypeStruct((S, C), F32), jax.ShapeDtypeStruct((S, C), F32)],
        compiler_params=_params(("parallel",)),
    )(rcb, wg, bg)


def _gates_bwd(name, dgi, dgr, rcb, wg, drc1):
    S, C = rcb.shape
    nblk, bw, _ = wg.shape

    def body(dgi_ref, dgr_ref, x_ref, w_ref, d1_ref, drc_ref, dw_ref):
        w = w_ref[...]
        x = x_ref[...]
        di, dr = dgi_ref[...], dgr_ref[...]
        drc_ref[...] = (d1_ref[...]
                        + lax.dot_general(di, w[:, :bw], _NT, preferred_element_type=F32)
                        + lax.dot_general(dr, w[:, bw:], _NT, preferred_element_type=F32))
        dw_ref[:, :bw] = lax.dot_general(x, di, _TN, preferred_element_type=F32).astype(dw_ref.dtype)
        dw_ref[:, bw:] = lax.dot_general(x, dr, _TN, preferred_element_type=F32).astype(dw_ref.dtype)

    col = pl.BlockSpec((S, bw), lambda n: (0, n))
    wspec = pl.BlockSpec((None, bw, 2 * bw), lambda n: (n, 0, 0))
    return pl.pallas_call(
        body, name=name, grid=(nblk,),
        in_specs=[col, col, col, wspec, col], out_specs=[col, wspec],
        out_shape=[jax.ShapeDtypeStruct((S, C), F32), jax.ShapeDtypeStruct((nblk, bw, 2 * bw), BF16)],
        compiler_params=_params(("parallel",)),
    )(dgi, dgr, rcb, wg, drc1)


def _att_tile(S):
    return 256 if S % 256 == 0 else 128


def _causal(T):
    r = lax.broadcasted_iota(jnp.int32, (T, T), 0)
    c = lax.broadcasted_iota(jnp.int32, (T, T), 1)
    return r >= c


def _attn_fwd(name, q, kv, negc3):
    S, D = q.shape
    HP = D // LANES
    T = _att_tile(S)
    nq = S // T

    def body(q_ref, k_ref, v_ref, nc_ref, o_ref, of_ref, lse_ref):
        is0 = lax.broadcasted_iota(jnp.int32, (T, LANES), 1) < HEAD_DIM
        tri = _causal(T)

        def q_step(qi, _):
            rows = pl.ds(pl.multiple_of(qi * T, T), T)
            qf = q_ref[rows, :].astype(F32)
            outs, lses = [], []
            for hh in range(2):
                qm = jnp.where(is0 if hh == 0 else jnp.logical_not(is0), qf, 0.0).astype(BF16)

                def tile(kj, carry, masked):
                    m, l, acc = carry
                    ks = pl.ds(pl.multiple_of(kj * T, T), T)
                    s = lax.dot_general(qm, k_ref[ks, :], _NT, preferred_element_type=F32)
                    s = s + nc_ref[hh:hh + 1, ks]
                    if masked:
                        s = jnp.where(tri, s, -jnp.inf)
                    m_new = jnp.maximum(m, jnp.max(s, axis=1, keepdims=True))
                    alpha = jnp.exp(m - m_new)
                    p = jnp.exp(s - m_new)
                    l = alpha * l + jnp.sum(p, axis=1, keepdims=True)
                    acc = alpha * acc + jnp.dot(p.astype(BF16), v_ref[ks, :], preferred_element_type=F32)
                    return m_new, l, acc

                init = (jnp.full((T, 1), -jnp.inf, F32), jnp.zeros((T, 1), F32), jnp.zeros((T, LANES), F32))
                carry = lax.fori_loop(0, qi, lambda kj, c: tile(kj, c, False), init)
                m, l, acc = tile(qi, carry, True)
                outs.append(acc / l)
                lses.append(jnp.broadcast_to(m + jnp.log(l), (T, LANES)))
            out = jnp.where(is0, outs[0], outs[1])
            o_ref[rows, :] = out.astype(o_ref.dtype)
            of_ref[rows, :] = out
            lse_ref[rows, :] = jnp.where(is0, lses[0], lses[1])
            return 0

        lax.fori_loop(0, nq, q_step, 0)

    return pl.pallas_call(
        body, name=name, grid=(HP,),
        in_specs=[pl.BlockSpec((S, LANES), lambda p: (0, p)),
                  pl.BlockSpec((S, LANES), lambda p: (0, p)),
                  pl.BlockSpec((S, LANES), lambda p: (0, HP + p)),
                  pl.BlockSpec((None, 2, S), lambda p: (p, 0, 0))],
        out_specs=[pl.BlockSpec((S, LANES), lambda p: (0, p))] * 3,
        out_shape=[jax.ShapeDtypeStruct((S, D), BF16), jax.ShapeDtypeStruct((S, D), F32),
                   jax.ShapeDtypeStruct((S, D), F32)],
        compiler_params=_params(("parallel",)),
    )(q, kv, kv, negc3)


def _attn_bwd(name, q, kv, negc3, o, do, lse):
    S, D = q.shape
    HP = D // LANES
    T = _att_tile(S)
    nq = S // T
    rep = T // LANES
    scale = HEAD_DIM ** -0.5

    def body(q_ref, k_ref, v_ref, nc_ref, o_ref, do_ref, lse_ref,
             dq_ref, dk_ref, dv_ref, dc_ref, dr_ref, dq_acc, lse_rep, dl_rep, dr_rep):
        is0 = lax.broadcasted_iota(jnp.int32, (T, LANES), 1) < HEAD_DIM
        tri = _causal(T)

        def prologue(qi, _):
            rows = pl.ds(pl.multiple_of(qi * T, T), T)
            prod = do_ref[rows, :].astype(F32) * o_ref[rows, :]
            lse_b = lse_ref[rows, :]
            for hh in range(2):
                msk = is0 if hh == 0 else jnp.logical_not(is0)
                dl = jnp.sum(jnp.where(msk, prod, 0.0), axis=1, keepdims=True)
                ls = jnp.max(jnp.where(msk, lse_b, -jnp.inf), axis=1, keepdims=True)
                dl_rep[hh, rows, :] = jnp.broadcast_to(dl, (T, LANES))
                lse_rep[hh, rows, :] = jnp.broadcast_to(ls, (T, LANES))
                dr_rep[hh, rows, :] = jnp.zeros((T, LANES), F32)
            dq_acc[rows, :] = jnp.zeros((T, LANES), F32)
            return 0

        lax.fori_loop(0, nq, prologue, 0)

        def kv_step(kj, _):
            ks = pl.ds(pl.multiple_of(kj * T, T), T)
            kf = k_ref[ks, :].astype(F32)
            vf = v_ref[ks, :].astype(F32)
            dks, dvs = [], []
            for hh in range(2):
                msk = is0 if hh == 0 else jnp.logical_not(is0)
                km = jnp.where(msk, kf, 0.0).astype(BF16)
                vm = jnp.where(msk, vf, 0.0).astype(BF16)
                ncr = nc_ref[hh:hh + 1, ks]

                def tile(qi, carry, masked):
                    dk_a, dv_a, dc_a = carry
                    rows = pl.ds(pl.multiple_of(qi * T, T), T)
                    qb = q_ref[rows, :]
                    dob = do_ref[rows, :]
                    s = lax.dot_general(qb, km, _NT, preferred_element_type=F32) + ncr
                    lse_t = jnp.tile(lse_rep[hh, rows, :], (1, rep))
                    dl_t = jnp.tile(dl_rep[hh, rows, :], (1, rep))
                    p = jnp.exp(s - lse_t)
                    if masked:
                        p = jnp.where(tri, p, 0.0)
                    dp = lax.dot_general(dob, vm, _NT, preferred_element_type=F32)
                    ds = p * (dp - dl_t)
                    pb, dsb = p.astype(BF16), ds.astype(BF16)
                    dv_a = dv_a + lax.dot_general(pb, dob, _TN, preferred_element_type=F32)
                    dk_a = dk_a + lax.dot_general(dsb, qb, _TN, preferred_element_type=F32)
                    dq_acc[rows, :] += jnp.dot(dsb, km, preferred_element_type=F32)
                    dc_a = dc_a + jnp.sum(ds, axis=0, keepdims=True)
                    dr_rep[hh, rows, :] += jnp.broadcast_to(jnp.sum(ds, axis=1, keepdims=True), (T, LANES))
                    return dk_a, dv_a, dc_a

                init = (jnp.zeros((T, LANES), F32), jnp.zeros((T, LANES), F32), jnp.zeros((1, T), F32))
                carry = tile(kj, init, True)
                dk_a, dv_a, dc_a = lax.fori_loop(kj + 1, nq, lambda qi, c: tile(qi, c, False), carry)
                dks.append(dk_a)
                dvs.append(dv_a)
                dc_ref[hh:hh + 1, ks] = -dc_a
            dk_ref[ks, :] = jnp.where(is0, dks[0], dks[1])
            dv_ref[ks, :] = jnp.where(is0, dvs[0], dvs[1])
            return 0

        lax.fori_loop(0, nq, kv_step, 0)
        dq_ref[...] = (dq_acc[...] * scale).astype(dq_ref.dtype)
        first = lax.broadcasted_iota(jnp.int32, (S, LANES), 1) < HEAD_DIM
        dr_ref[...] = jnp.where(first, dr_rep[0], dr_rep[1])

    blk = lambda off: pl.BlockSpec((S, LANES), lambda p: (0, off + p))
    nc_spec = pl.BlockSpec((None, 2, S), lambda p: (p, 0, 0))
    return pl.pallas_call(
        body, name=name, grid=(HP,),
        in_specs=[blk(0), blk(0), blk(HP), nc_spec, blk(0), blk(0), blk(0)],
        out_specs=[blk(0), blk(0), blk(0), nc_spec, blk(0)],
        out_shape=[jax.ShapeDtypeStruct((S, D), BF16), jax.ShapeDtypeStruct((S, D), F32),
                   jax.ShapeDtypeStruct((S, D), F32), jax.ShapeDtypeStruct((HP, 2, S), F32),
                   jax.ShapeDtypeStruct((S, D), F32)],
        scratch_shapes=[pltpu.VMEM((S, LANES), F32), pltpu.VMEM((2, S, LANES), F32),
                        pltpu.VMEM((2, S, LANES), F32), pltpu.VMEM((2, S, LANES), F32)],
        compiler_params=_params(("parallel",)),
    )(q, kv, kv, negc3, o, do, lse)


def _logsig_fwd(name, f):
    S, C = f.shape

    def body(f_ref, o_ref):
        o_ref[...] = -_softplus(-f_ref[...])

    spec = pl.BlockSpec((S, C), lambda i: (0, 0))
    return pl.pallas_call(body, name=name, grid=(1,), in_specs=[spec], out_specs=spec,
                          out_shape=jax.ShapeDtypeStruct((S, C), F32),
                          compiler_params=_params(("arbitrary",)))(f)


def _logsig_bwd(name, dls, f):
    S, C = f.shape

    def body(d_ref, f_ref, o_ref, s_ref):
        df = d_ref[...] * _sigmoid(-f_ref[...])
        o_ref[...] = df.astype(o_ref.dtype)
        s_ref[...] = jnp.sum(df, axis=0, keepdims=True)

    spec = pl.BlockSpec((S, C), lambda i: (0, 0))
    return pl.pallas_call(body, name=name, grid=(1,), in_specs=[spec, spec],
                          out_specs=[spec, pl.BlockSpec((1, C), lambda i: (0, 0))],
                          out_shape=[jax.ShapeDtypeStruct((S, C), BF16), jax.ShapeDtypeStruct((1, C), F32)],
                          compiler_params=_params(("arbitrary",)))(dls, f)


def _add_cast(name, parts, out_dtype, tr=256):
    S, C = parts[0].shape
    tr = _tile(S, tr)
    n = len(parts)

    def body(*refs):
        acc = refs[0][...].astype(F32)
        for r in refs[1:n]:
            acc = acc + r[...].astype(F32)
        refs[n][...] = acc.astype(out_dtype)

    spec = pl.BlockSpec((tr, C), lambda i: (i, 0))
    return pl.pallas_call(body, name=name, grid=(S // tr,), in_specs=[spec] * n, out_specs=spec,
                          out_shape=jax.ShapeDtypeStruct((S, C), out_dtype),
                          compiler_params=_params(("parallel",)))(*parts)


def _local_step(x, target, W):
    S, D = x.shape
    L = W["norm_mix"].shape[0]
    NA = W["w_rec_in"].shape[0]
    NB = L - NA
    C = W["w_rec_out"].shape[1]
    F = W["w_ffn_out"].shape[1]
    FH = F // 2
    CH = C // 2
    HP = D // LANES
    scale = HEAD_DIM ** -0.5
    saved = []
    h = x

    def ffn_fwd(l, h_mid):
        hn = _rmsnorm_fwd(f"ffn_norm_{l}", h_mid, W["norm_ffn"][l])
        z3 = _mm(f"ffn_in_{l}", "nn", hn, W["w_ffn_in"], grid=(S // _tile(S, 512), N_CHIPS),
                 a_spec=pl.BlockSpec((_tile(S, 512), D), lambda i, j: (i, 0)),
                 b_spec=pl.BlockSpec((None, None, D, FH), lambda i, j: (l, j, 0, 0)),
                 out_shape=(2, S, F), out_dtype=BF16,
                 out_spec=pl.BlockSpec((None, _tile(S, 512), FH), lambda i, j: (j // 2, i, j % 2)))
        act = _swiglu_fwd(f"swiglu_{l}", z3)
        h_out = _mm_nn(f"ffn_out_{l}", act, W["w_ffn_out"], b_lead=(l,), out_dtype=F32, res=h_mid, tn=D)
        return h_out, (hn, z3, act)

    kv = negc3 = f_pre = hn_kv = h_kv = None
    for l in range(L):
        xn = _rmsnorm_fwd(f"mix_norm_{l}", h, W["norm_mix"][l])
        if l < NA:
            proj = _mm(f"rec_in_{l}", "nn", xn, W["w_rec_in"], grid=(S // _tile(S, 512), N_CHIPS),
                       a_spec=pl.BlockSpec((_tile(S, 512), D), lambda i, j: (i, 0)),
                       b_spec=pl.BlockSpec((None, None, D, CH), lambda i, j: (l, j, 0, 0)),
                       out_shape=(S, 2 * C), out_dtype=F32,
                       out_spec=pl.BlockSpec((_tile(S, 512), CH), lambda i, j: (i, j)))
            rc, rcb = _conv_fwd(f"conv_{l}", proj, W["conv_w"][l], W["conv_b"][l])
            gip, grp = _gates_fwd(f"gates_{l}", rcb, W["w_gates"][l], W["b_gates"][l])
            hrec, m = _lru_fwd(f"lru_{l}", proj, rc, gip, grp, W["lru_param"][l])
            h_mid = _mm_nn(f"rec_out_{l}", m, W["w_rec_out"], b_lead=(l,), out_dtype=F32, res=h, tn=D)
            mix_saved = (xn, proj, rc, rcb, gip, grp, hrec, m)
        else:
            j = l - NA
            if j == 0:
                h_kv = h
                hn_kv = _rmsnorm_fwd("kv_norm", h, W["norm_kv"])
                kv = _mm_nn("kv_proj", hn_kv, W["w_kv"], out_dtype=BF16)
                f_pre = _mm_nn("f_proj", hn_kv, W["w_f"], out_dtype=F32, bias=W["b_f"])
                c = _cumsum_rows("c_cumsum", _logsig_fwd("logsig", f_pre), False)
                negc3 = (-c[:, :2 * HP]).T.reshape(HP, 2, S)
            q = _mm_nn(f"q_proj_{j}", xn, W["w_q"], b_lead=(j,), out_dtype=BF16, scale=scale)
            o, of, lse = _attn_fwd(f"attn_fwd_{j}", q, kv, negc3)
            h_mid = _mm_nn(f"o_proj_{j}", o, W["w_o"], b_lead=(j,), out_dtype=F32, res=h, tn=D)
            mix_saved = (xn, q, o, of, lse)
        h_out, ffn_saved = ffn_fwd(l, h_mid)
        saved.append((h, h_mid, mix_saved, ffn_saved))
        h = h_out

    dh, dhb, dg_final, loss_row = _loss_head("loss_head", h, target, W["norm_final"])

    G = {"norm_final": dg_final, "norm_mix": [None] * L, "norm_ffn": [None] * L,
         "w_ffn_in": [None] * L, "w_ffn_out": [None] * L,
         "w_rec_in": [None] * NA, "w_gates": [None] * NA, "w_rec_out": [None] * NA,
         "conv_w": [None] * NA, "conv_b": [None] * NA, "b_gi": [None] * NA, "b_gr": [None] * NA,
         "lru_param": [None] * NA, "w_q": [None] * NB, "w_o": [None] * NB}
    dk_parts, dv_parts, dc_parts = [], [], []
    tm = _tile(S, 512)
    td = _tile(D, 512)

    for l in reversed(range(L)):
        h_in, h_mid, mix_saved, (hn, z3, act) = saved[l]
        G["w_ffn_out"][l] = _mm_tn(f"d_ffn_out_{l}", act, dhb, out_dtype=BF16, tn=D)
        da = _mm_nt(f"d_act_{l}", dhb, W["w_ffn_out"], b_lead=(l,), out_dtype=BF16, tn=FH)
        dz3 = _swiglu_bwd(f"d_swiglu_{l}", da, z3)
        G["w_ffn_in"][l] = _mm(
            f"d_ffn_in_{l}", "tn", hn, dz3, grid=(D // td, N_CHIPS),
            a_spec=pl.BlockSpec((S, td), lambda i, j: (0, i)),
            b_spec=pl.BlockSpec((None, S, FH), lambda i, j: (j // 2, 0, j % 2)),
            out_shape=(N_CHIPS, D, FH), out_dtype=BF16,
            out_spec=pl.BlockSpec((None, td, FH), lambda i, j: (j, i, 0)))
        dhn = _mm(f"d_ffn_hn_{l}", "nt", dz3, W["w_ffn_in"], grid=(S // tm, 1, N_CHIPS), nk=N_CHIPS,
                  a_spec=pl.BlockSpec((None, tm, FH), lambda i, j, k: (k // 2, i, k % 2)),
                  b_spec=pl.BlockSpec((None, None, D, FH), lambda i, j, k: (l, k, 0, 0)),
                  out_shape=(S, D), out_dtype=F32, out_spec=pl.BlockSpec((tm, D), lambda i, j, k: (i, 0)))
        dh, dhb, G["norm_ffn"][l] = _rmsnorm_bwd(f"d_ffn_norm_{l}", dhn, h_mid, W["norm_ffn"][l], dh)
        if l < NA:
            xn, proj, rc, rcb, gip, grp, hrec, m = mix_saved
            G["w_rec_out"][l] = _mm_tn(f"d_rec_out_{l}", m, dhb, out_dtype=BF16, tn=D)
            dm = _mm_nt(f"d_m_{l}", dhb, W["w_rec_out"], b_lead=(l,), out_dtype=F32, tn=C)
            dgb, dgi, dgr, drc1, dbi, dbr, dlp = _lru_bwd(f"d_lru_{l}", dm, proj, hrec, rc, gip, grp, W["lru_param"][l])
            drc, G["w_gates"][l] = _gates_bwd(f"d_gates_{l}", dgi, dgr, rcb, W["w_gates"][l], drc1)
            drec, G["conv_w"][l], G["conv_b"][l] = _conv_bwd(f"d_conv_{l}", drc, proj, W["conv_w"][l])
            G["b_gi"][l], G["b_gr"][l], G["lru_param"][l] = dbi, dbr, dlp
            dproj = jnp.concatenate([dgb, drec], axis=1)
            G["w_rec_in"][l] = _mm(
                f"d_rec_in_{l}", "tn", xn, dproj, grid=(1, N_CHIPS),
                a_spec=pl.BlockSpec((S, D), lambda i, j: (0, 0)),
                b_spec=pl.BlockSpec((S, CH), lambda i, j: (0, j)),
                out_shape=(N_CHIPS, D, CH), out_dtype=BF16,
                out_spec=pl.BlockSpec((None, D, CH), lambda i, j: (j, 0, 0)))
            dxn = _mm(f"d_rec_xn_{l}", "nt", dproj, W["w_rec_in"], grid=(S // tm, 1, N_CHIPS), nk=N_CHIPS,
                      a_spec=pl.BlockSpec((tm, CH), lambda i, j, k: (i, k)),
                      b_spec=pl.BlockSpec((None, None, D, CH), lambda i, j, k: (l, k, 0, 0)),
                      out_shape=(S, D), out_dtype=F32, out_spec=pl.BlockSpec((tm, D), lambda i, j, k: (i, 0)))
        else:
            j = l - NA
            xn, q, o, of, lse = mix_saved
            G["w_o"][j] = _mm_tn(f"d_o_proj_{j}", o, dhb, out_dtype=BF16, tn=D)
            do = _mm_nt(f"d_o_{j}", dhb, W["w_o"], b_lead=(j,), out_dtype=BF16, tn=D)
            dq, dk, dv, dc3, dcr = _attn_bwd(f"attn_bwd_{j}", q, kv, negc3, of, do, lse)
            dk_parts.append(dk)
            dv_parts.append(dv)
            dc_parts.append(dc3.reshape(2 * HP, S).T + dcr[:, ::HEAD_DIM])
            G["w_q"][j] = _mm_tn(f"d_q_proj_{j}", xn, dq, out_dtype=BF16, tn=D)
            dxn = _mm_nt(f"d_q_xn_{j}", dq, W["w_q"], b_lead=(j,), out_dtype=F32, tn=D)
        dh, dhb, G["norm_mix"][l] = _rmsnorm_bwd(f"d_mix_norm_{l}", dxn, h_in, W["norm_mix"][l], dh)
        if l == NA:
            dkb = _add_cast("dk_sum", dk_parts, BF16)
            dvb = _add_cast("dv_sum", dv_parts, BF16)
            dkv = jnp.concatenate([dkb, dvb], axis=1)
            dc = sum(dc_parts[1:], dc_parts[0])
            dc_pad = jnp.pad(dc, ((0, 0), (0, LANES - 2 * HP)))
            dls = _cumsum_rows("dc_cumsum", dc_pad, True)
            dfb, G["b_f"] = _logsig_bwd("d_logsig", dls, f_pre)
            G["w_kv"] = _mm_tn("d_kv_proj", hn_kv, dkv, out_dtype=BF16)
            G["w_f"] = _mm_tn("d_f_proj", hn_kv, dfb, out_dtype=F32)
            dhn1 = _mm_nt("d_kv_hn", dkv, W["w_kv"], out_dtype=F32, tn=D)
            dhn2 = _mm_nt("d_f_hn", dfb, W["w_f"], out_dtype=F32, tn=D)
            dhn_kv = _add_cast("d_kv_hn_sum", [dhn1, dhn2], F32)
            dh, dhb, G["norm_kv"] = _rmsnorm_bwd("d_kv_norm", dhn_kv, h_kv, W["norm_kv"], dh)
    return loss_row, dh, G


_ANY = pl.BlockSpec(memory_space=pl.ANY)


def _position():
    return lax.axis_index("x"), lax.axis_index("y"), lax.axis_index("c")


def _chip_peers(x, y):
    return [(1 - x, y), (x, 1 - y), (1 - x, 1 - y)]


def _half_rows(c, n):
    h = n // 2
    assert h % 16 == 0
    return pl.ds(pl.multiple_of(c * h, 16), h)


def _items(arrays):
    return [(w, l) for w, a in enumerate(arrays) for l in range(a.shape[0])]


def _place_own(name, shard, me):
    nl, R, C = shard.shape
    tr = _row_tile(R, C, shard.dtype.itemsize)

    def body(me_ref, x_ref, o_ref):
        o_ref[...] = x_ref[...]

    return pl.pallas_call(
        body, name=name,
        grid_spec=pltpu.PrefetchScalarGridSpec(
            num_scalar_prefetch=1, grid=(nl, R // tr),
            in_specs=[pl.BlockSpec((None, tr, C), lambda l, i, me_ref: (l, i, 0))],
            out_specs=pl.BlockSpec((None, None, tr, C), lambda l, i, me_ref: (l, me_ref[0], i, 0))),
        out_shape=jax.ShapeDtypeStruct((nl, N_CHIPS, R, C), shard.dtype),
        compiler_params=_params(("parallel", "parallel")),
    )(me, shard)


def _gather_ici(name, shards, bufs, smalls):
    nw, ns = len(shards), len(smalls)
    items = _items(shards)
    n = len(items)
    out_shapes = [jax.ShapeDtypeStruct(b.shape, b.dtype) for b in bufs]
    out_shapes += [jax.ShapeDtypeStruct((N_CHIPS,) + s.shape, s.dtype) for s in smalls]

    def body(*refs):
        ins, s_ins = refs[:nw], refs[2 * nw:2 * nw + ns]
        outs, s_outs = refs[2 * nw + ns:3 * nw + ns], refs[3 * nw + ns:3 * nw + 2 * ns]
        send_sems, recv_sems, local_sems = refs[3 * nw + 2 * ns:]
        x, y, c = _position()
        me = 2 * x + y
        peers = _chip_peers(x, y)

        def remote(i, k, chip):
            px, py = peers[k]
            if i < n:
                w, l = items[i]
                half = _half_rows(c, ins[w].shape[1])
                src, dst = ins[w].at[l, half], outs[w].at[l, chip, half]
            else:
                src, dst = s_ins[i - n], s_outs[i - n].at[chip]
            return pltpu.make_async_remote_copy(
                src_ref=src, dst_ref=dst, send_sem=send_sems.at[3 * i + k], recv_sem=recv_sems.at[3 * i + k],
                device_id=(px, py, c), device_id_type=MESH)

        local = [pltpu.make_async_copy(s_ins[t], s_outs[t].at[me], local_sems.at[t]) for t in range(ns)]
        for cp in local:
            cp.start()
        for i in range(n + ns):
            for k in range(3):
                remote(i, k, me).start()
        for i in range(n + ns):
            for k in range(3):
                px, py = peers[k]
                remote(i, k, 2 * px + py).wait_recv()
        for i in range(n + ns):
            for k in range(3):
                remote(i, k, me).wait_send()
        for cp in local:
            cp.wait()

    return pl.pallas_call(
        body, name=name, in_specs=[_ANY] * (2 * nw + ns), out_specs=[_ANY] * (nw + ns), out_shape=out_shapes,
        input_output_aliases={nw + w: w for w in range(nw)},
        scratch_shapes=[pltpu.SemaphoreType.DMA((3 * (n + ns),)), pltpu.SemaphoreType.DMA((3 * (n + ns),)),
                        pltpu.SemaphoreType.DMA((ns,))],
    )(*shards, *bufs, *smalls)


def _gather_d2d(name, gathered):
    nw = len(gathered)
    items = _items(gathered)
    n = len(items)

    def body(*refs):
        ins, outs = refs[:nw], refs[nw:2 * nw]
        send_sems, recv_sems = refs[2 * nw:]
        x, y, c = _position()
        peers = _chip_peers(x, y)

        def remote(i, k, core):
            w, l = items[i]
            px, py = peers[k]
            half = _half_rows(core, ins[w].shape[2])
            return pltpu.make_async_remote_copy(
                src_ref=ins[w].at[l, 2 * px + py, half], dst_ref=outs[w].at[l, 2 * px + py, half],
                send_sem=send_sems.at[3 * i + k], recv_sem=recv_sems.at[3 * i + k],
                device_id=(x, y, 1 - c), device_id_type=MESH)

        for i in range(n):
            for k in range(3):
                remote(i, k, c).start()
        for i in range(n):
            for k in range(3):
                remote(i, k, 1 - c).wait_recv()
        for i in range(n):
            for k in range(3):
                remote(i, k, c).wait_send()

    return pl.pallas_call(
        body, name=name, in_specs=[_ANY] * nw, out_specs=[_ANY] * nw,
        out_shape=[jax.ShapeDtypeStruct(g.shape, g.dtype) for g in gathered],
        input_output_aliases={w: w for w in range(nw)},
        scratch_shapes=[pltpu.SemaphoreType.DMA((3 * n,)), pltpu.SemaphoreType.DMA((3 * n,))],
    )(*gathered)


def _reduce_d2d(name, grads):
    n = len(grads)

    def body(*refs):
        ins, outs = refs[:n], refs[n:2 * n]
        send_sems, recv_sems = refs[2 * n:]
        x, y, c = _position()
        remote = [pltpu.make_async_remote_copy(
            src_ref=ins[i].at[:, _half_rows(1 - c, ins[i].shape[1])], dst_ref=outs[i],
            send_sem=send_sems.at[i], recv_sem=recv_sems.at[i],
            device_id=(x, y, 1 - c), device_id_type=MESH) for i in range(n)]
        for cp in remote:
            cp.start()
        for cp in remote:
            cp.wait_recv()
        for cp in remote:
            cp.wait_send()

    return pl.pallas_call(
        body, name=name, in_specs=[_ANY] * n, out_specs=[_ANY] * n,
        out_shape=[jax.ShapeDtypeStruct((N_CHIPS, g.shape[1] // 2, g.shape[2]), g.dtype) for g in grads],
        scratch_shapes=[pltpu.SemaphoreType.DMA((n,)), pltpu.SemaphoreType.DMA((n,))],
    )(*grads)


def _sum_cores(name, g, other, core):
    _, R, C = g.shape
    H = R // 2
    tr = _row_tile(H, C)
    nb = H // tr

    def body(c_ref, g_ref, o_ref, out_ref):
        out_ref[...] = (g_ref[...].astype(F32) + o_ref[...].astype(F32)).astype(out_ref.dtype)

    return pl.pallas_call(
        body, name=name,
        grid_spec=pltpu.PrefetchScalarGridSpec(
            num_scalar_prefetch=1, grid=(N_CHIPS, nb),
            in_specs=[pl.BlockSpec((None, tr, C), lambda j, i, c_ref: (j, c_ref[0] * nb + i, 0)),
                      pl.BlockSpec((None, tr, C), lambda j, i, c_ref: (j, i, 0))],
            out_specs=pl.BlockSpec((None, tr, C), lambda j, i, c_ref: (j, i, 0))),
        out_shape=jax.ShapeDtypeStruct((N_CHIPS, H, C), BF16),
        compiler_params=_params(("parallel", "parallel")),
    )(core, g, other)


def _sum_chips(name, received, own, full, layer, me_core):
    _, _, H, C = received.shape
    tr = _row_tile(H, C)
    nb = H // tr

    def body(s_ref, r_ref, own_ref, full_ref, out_ref):
        acc = r_ref[0].astype(F32)
        for k in (1, 2):
            acc = acc + r_ref[k].astype(F32)
        out_ref[...] = acc + own_ref[...].astype(F32)

    return pl.pallas_call(
        body, name=name,
        grid_spec=pltpu.PrefetchScalarGridSpec(
            num_scalar_prefetch=1, grid=(nb,),
            in_specs=[pl.BlockSpec((3, None, tr, C), lambda i, s_ref: (0, layer, i, 0)),
                      pl.BlockSpec((None, tr, C), lambda i, s_ref: (s_ref[0], i, 0)),
                      _ANY],
            out_specs=pl.BlockSpec((None, tr, C), lambda i, s_ref: (layer, s_ref[1] * nb + i, 0))),
        out_shape=jax.ShapeDtypeStruct(full.shape, full.dtype),
        input_output_aliases={3: 0},
        compiler_params=_params(("parallel",)),
    )(me_core, received, own, full)


def _scatter_ici(name, parts, layout):
    n = len(parts)
    nw = 1 + max(w for w, _ in layout)
    layers = [1 + max(l for w2, l in layout if w2 == w) for w in range(nw)]
    first = [next(i for i, (w2, _) in enumerate(layout) if w2 == w) for w in range(nw)]
    out_shapes = [jax.ShapeDtypeStruct((3, layers[w]) + parts[first[w]].shape[1:], parts[first[w]].dtype)
                  for w in range(nw)]

    def body(*refs):
        ins, outs = refs[:n], refs[n:n + nw]
        send_sems, recv_sems = refs[n + nw:]
        x, y, c = _position()
        peers = _chip_peers(x, y)

        def remote(i, k):
            w, l = layout[i]
            px, py = peers[k]
            return pltpu.make_async_remote_copy(
                src_ref=ins[i].at[2 * px + py], dst_ref=outs[w].at[k, l],
                send_sem=send_sems.at[3 * i + k], recv_sem=recv_sems.at[3 * i + k],
                device_id=(px, py, c), device_id_type=MESH)

        for i in range(n):
            for k in range(3):
                remote(i, k).start()
        for i in range(n):
            for k in range(3):
                remote(i, k).wait_recv()
        for i in range(n):
            for k in range(3):
                remote(i, k).wait_send()

    return pl.pallas_call(
        body, name=name, in_specs=[_ANY] * n, out_specs=[_ANY] * nw, out_shape=out_shapes,
        scratch_shapes=[pltpu.SemaphoreType.DMA((3 * n,)), pltpu.SemaphoreType.DMA((3 * n,))],
    )(*parts)


def _share_d2d(name, full):
    n = len(full)

    def body(*refs):
        ins, outs = refs[:n], refs[n:2 * n]
        send_sems, recv_sems = refs[2 * n:]
        x, y, c = _position()

        def remote(w, core):
            half = _half_rows(core, ins[w].shape[1])
            return pltpu.make_async_remote_copy(
                src_ref=ins[w].at[:, half], dst_ref=outs[w].at[:, half],
                send_sem=send_sems.at[w], recv_sem=recv_sems.at[w],
                device_id=(x, y, 1 - c), device_id_type=MESH)

        for w in range(n):
            remote(w, c).start()
        for w in range(n):
            remote(w, 1 - c).wait_recv()
        for w in range(n):
            remote(w, c).wait_send()

    return pl.pallas_call(
        body, name=name, in_specs=[_ANY] * n, out_specs=[_ANY] * n,
        out_shape=[jax.ShapeDtypeStruct(f.shape, f.dtype) for f in full],
        input_output_aliases={w: w for w in range(n)},
        scratch_shapes=[pltpu.SemaphoreType.DMA((n,)), pltpu.SemaphoreType.DMA((n,))],
    )(*full)


def _gather_all(name, a):
    def body(a_ref, o_ref, send_sems, recv_sems, local_sem):
        x, y, c = _position()
        me = 4 * x + 2 * y + c

        def peer(k):
            return (x ^ ((k >> 2) & 1), y ^ ((k >> 1) & 1), c ^ (k & 1))

        def remote(k, slot):
            return pltpu.make_async_remote_copy(
                src_ref=a_ref, dst_ref=o_ref.at[slot], send_sem=send_sems.at[k - 1], recv_sem=recv_sems.at[k - 1],
                device_id=peer(k), device_id_type=MESH)

        local = pltpu.make_async_copy(a_ref, o_ref.at[me], local_sem)
        local.start()
        for k in range(1, N_DEV):
            remote(k, me).start()
        for k in range(1, N_DEV):
            px, py, pc = peer(k)
            remote(k, 4 * px + 2 * py + pc).wait_recv()
        for k in range(1, N_DEV):
            remote(k, me).wait_send()
        local.wait()

    return pl.pallas_call(
        body, name=name, in_specs=[_ANY], out_specs=_ANY,
        out_shape=jax.ShapeDtypeStruct((N_DEV,) + a.shape, a.dtype),
        scratch_shapes=[pltpu.SemaphoreType.DMA((N_DEV - 1,)), pltpu.SemaphoreType.DMA((N_DEV - 1,)),
                        pltpu.SemaphoreType.DMA],
    )(a)


def _rows2d(a, lead=0):
    return a.reshape(a.shape[:lead] + (-1, a.shape[-1]))


def _row_tile(rows, cols, itemsize=4, target=1 << 20):
    want = max(SUBLANES, target // (cols * itemsize))
    t = min(rows, (want // 16) * 16)
    while t > 16 and rows % t:
        t -= 16
    return t if rows % t == 0 else rows


def _sum_slots(name, r, out_dtype=F32):
    ns = r.shape[0]
    r2 = _rows2d(r, 1)
    _, rows, cols = r2.shape
    tr = _row_tile(rows, cols)

    def body(r_ref, o_ref):
        acc = r_ref[0].astype(F32)
        for s in range(1, ns):
            acc = acc + r_ref[s].astype(F32)
        o_ref[...] = acc.astype(o_ref.dtype)

    out = pl.pallas_call(
        body, name=name, grid=(rows // tr,),
        in_specs=[pl.BlockSpec((ns, tr, cols), lambda i: (0, i, 0))],
        out_specs=pl.BlockSpec((tr, cols), lambda i: (i, 0)),
        out_shape=jax.ShapeDtypeStruct((rows, cols), out_dtype),
        compiler_params=_params(("parallel",)),
    )(r2)
    return out.reshape(r.shape[1:])


def _adamw(name, g_parts, w, m, v):
    shape = w.shape
    ng = len(g_parts)
    args = [_rows2d(a) for a in (*g_parts, w, m, v)]
    rows, cols = args[0].shape
    tr = _row_tile(rows, cols, target=1 << 19)
    c1 = 1.0 - ADAM_B1 ** ADAM_STEP
    c2 = 1.0 - ADAM_B2 ** ADAM_STEP

    def body(*refs):
        g = refs[0][...]
        for r in refs[1:ng]:
            g = g + r[...]
        w_ref, m_ref, v_ref = refs[ng:ng + 3]
        g_out, d_out, m_out, v_out = refs[ng + 3:]
        mn = ADAM_B1 * m_ref[...] + (1.0 - ADAM_B1) * g
        vn = ADAM_B2 * v_ref[...] + (1.0 - ADAM_B2) * (g * g)
        m_hat = mn / c1
        v_hat = vn / c2
        g_out[...] = g
        d_out[...] = -ADAM_LR * (m_hat / (jnp.sqrt(v_hat) + ADAM_EPS) + ADAM_WD * w_ref[...])
        m_out[...] = mn
        v_out[...] = vn

    spec = pl.BlockSpec((tr, cols), lambda i: (i, 0))
    outs = pl.pallas_call(
        body, name=name, grid=(rows // tr,), in_specs=[spec] * (ng + 3), out_specs=[spec] * 4,
        out_shape=[jax.ShapeDtypeStruct((rows, cols), F32)] * 4,
        compiler_params=_params(("parallel",)),
    )(*args)
    return tuple(o.reshape(shape) for o in outs)


_WEIGHTS = ["norm_mix", "norm_ffn", "w_ffn_in", "w_ffn_out", "w_rec_in", "conv_w", "conv_b", "w_lru_gates",
            "b_lru_gates", "lru_param", "w_rec_out", "norm_kv", "w_kvf", "b_forget", "w_q", "w_o", "norm_final"]
_BIG = ["w_ffn_in", "w_ffn_out", "w_rec_in", "w_lru_gates", "w_rec_out", "w_kvf", "w_q", "w_o"]


def _stack3(a):
    return a[None] if a.ndim == 2 else a.reshape(a.shape[0], -1, a.shape[-1])


def _pad_lanes(a, n):
    return jnp.pad(a, ((0, 0),) * (a.ndim - 1) + ((0, n - a.shape[-1]),))


def kernel(x, norm_mix, norm_ffn, w_ffn_in, w_ffn_out, w_rec_in, conv_w, conv_b, w_lru_gates, b_lru_gates, lru_param, w_rec_out, norm_kv, w_kvf, b_forget, w_q, w_o, norm_final, loss_target, m_norm_mix, m_norm_ffn, m_w_ffn_in, m_w_ffn_out, m_w_rec_in, m_conv_w, m_conv_b, m_w_lru_gates, m_b_lru_gates, m_lru_param, m_w_rec_out, m_norm_kv, m_w_kvf, m_b_forget, m_w_q, m_w_o, m_norm_final, v_norm_mix, v_norm_ffn, v_w_ffn_in, v_w_ffn_out, v_w_rec_in, v_conv_w, v_conv_b, v_w_lru_gates, v_b_lru_gates, v_lru_param, v_w_rec_out, v_norm_kv, v_w_kvf, v_b_forget, v_w_q, v_w_o, v_norm_final):
    P = dict(norm_mix=norm_mix, norm_ffn=norm_ffn, w_ffn_in=w_ffn_in, w_ffn_out=w_ffn_out, w_rec_in=w_rec_in,
             conv_w=conv_w, conv_b=conv_b, w_lru_gates=w_lru_gates, b_lru_gates=b_lru_gates, lru_param=lru_param,
             w_rec_out=w_rec_out, norm_kv=norm_kv, w_kvf=w_kvf, b_forget=b_forget, w_q=w_q, w_o=w_o,
             norm_final=norm_final)
    M1 = dict(norm_mix=m_norm_mix, norm_ffn=m_norm_ffn, w_ffn_in=m_w_ffn_in, w_ffn_out=m_w_ffn_out,
              w_rec_in=m_w_rec_in, conv_w=m_conv_w, conv_b=m_conv_b, w_lru_gates=m_w_lru_gates,
              b_lru_gates=m_b_lru_gates, lru_param=m_lru_param, w_rec_out=m_w_rec_out, norm_kv=m_norm_kv,
              w_kvf=m_w_kvf, b_forget=m_b_forget, w_q=m_w_q, w_o=m_w_o, norm_final=m_norm_final)
    M2 = dict(norm_mix=v_norm_mix, norm_ffn=v_norm_ffn, w_ffn_in=v_w_ffn_in, w_ffn_out=v_w_ffn_out,
              w_rec_in=v_w_rec_in, conv_w=v_conv_w, conv_b=v_conv_b, w_lru_gates=v_w_lru_gates,
              b_lru_gates=v_b_lru_gates, lru_param=v_lru_param, w_rec_out=v_w_rec_out, norm_kv=v_norm_kv,
              w_kvf=v_w_kvf, b_forget=v_b_forget, w_q=v_w_q, w_o=v_w_o, norm_final=v_norm_final)

    _, S, D = x.shape
    L = norm_mix.shape[0]
    NA, NBLK, BW, GS = w_lru_gates.shape
    NB = w_q.shape[0]
    C = NBLK * BW
    CS = C // N_CHIPS
    H = b_forget.shape[0]
    assert C == D and H * HEAD_DIM == D and H <= LANES
    chip = 2 * lax.axis_index("x") + lax.axis_index("y")

    small_a = jnp.concatenate([conv_w, conv_b[:, None], lru_param[:, None]], axis=1)
    shards = [_stack3(P[w]).astype(BF16) for w in _BIG]
    chip_id = jnp.reshape(chip, (1,)).astype(jnp.int32)
    core = lax.axis_index("c")
    core_id = jnp.reshape(core, (1,)).astype(jnp.int32)
    bufs = [_place_own(f"place_{w}", s, chip_id) for w, s in zip(_BIG, shards)]
    gathered = _gather_ici("gather_ici", shards, bufs, [small_a, b_lru_gates])
    small_a = gathered[-2].transpose(1, 2, 0, 3).reshape(NA, 6, C)
    b_gates = gathered[-1].transpose(1, 2, 0, 3).reshape(NA, NBLK, 1, N_CHIPS * GS)
    Gw = dict(zip(_BIG, _gather_d2d("gather_d2d", gathered[:len(_BIG)])))
    w_kvf_full = Gw["w_kvf"][0].transpose(1, 0, 2).reshape(D, -1)
    W = dict(
        w_ffn_in=Gw["w_ffn_in"],
        w_ffn_out=Gw["w_ffn_out"].reshape(L, -1, D),
        w_rec_in=Gw["w_rec_in"],
        w_gates=Gw["w_lru_gates"].reshape(NA, N_CHIPS, NBLK, BW, GS).transpose(0, 2, 3, 1, 4).reshape(
            NA, NBLK, BW, N_CHIPS * GS),
        b_gates=b_gates,
        w_rec_out=Gw["w_rec_out"].reshape(NA, C, D),
        w_kv=w_kvf_full[:, :2 * D],
        w_f=_pad_lanes(w_kvf_full[:, 2 * D:], LANES),
        w_q=Gw["w_q"].reshape(NB, D, D),
        w_o=Gw["w_o"].reshape(NB, D, D),
        conv_w=small_a[:, :4], conv_b=small_a[:, 4:5], lru_param=small_a[:, 5:6],
        norm_mix=norm_mix[:, None], norm_ffn=norm_ffn[:, None], norm_kv=norm_kv[None], norm_final=norm_final[None],
        b_f=_pad_lanes(b_forget[None], LANES),
    )

    loss_row, grad_x, G = _local_step(x.reshape(S, D), loss_target.reshape(S, D), W)

    rows = [*G["norm_mix"], *G["norm_ffn"], G["norm_kv"], G["norm_final"],
            _pad_lanes(G["b_f"], D), _pad_lanes(loss_row, D)]
    for a in range(NA):
        rows += [G["conv_w"][a], G["conv_b"][a], G["b_gi"][a], G["b_gr"][a], G["lru_param"][a]]
    packed = jnp.concatenate(rows, axis=0)
    tot = _sum_slots("sum_small", _gather_all("gather_small", packed))
    loss = tot[2 * L + 3, 0]
    g_rep = jnp.concatenate([tot[:2 * L + 2], tot[2 * L + 2:2 * L + 3]], axis=0)
    base = 2 * L + 4
    g_sh = []
    for a in range(NA):
        blk = lax.dynamic_slice_in_dim(tot[base + 8 * a:base + 8 * a + 8], chip * CS, CS, axis=1)
        gi = tot[base + 8 * a + 5].reshape(NBLK, BW)
        gr = tot[base + 8 * a + 6].reshape(NBLK, BW)
        bl = lax.dynamic_slice_in_dim(jnp.concatenate([gi, gr], axis=1), chip * GS, GS, axis=1)
        g_sh += [blk[:5], bl.reshape(-1, CS), blk[7:8]]
    g_sh = jnp.concatenate(g_sh, axis=0)
    nrow = g_sh.shape[0] // NA

    def pack_rep(T):
        return jnp.concatenate([T["norm_mix"], T["norm_ffn"], T["norm_kv"][None], T["norm_final"][None],
                                _pad_lanes(T["b_forget"][None], D)], axis=0)

    def pack_sh(T):
        return jnp.concatenate([jnp.concatenate([T["conv_w"][a], T["conv_b"][a][None],
                                                 T["b_lru_gates"][a].reshape(-1, CS), T["lru_param"][a][None]], axis=0)
                                for a in range(NA)], axis=0)

    rep = _adamw("adamw_replicated", [g_rep], pack_rep(P), pack_rep(M1), pack_rep(M2))
    shd = _adamw("adamw_small_sharded", [g_sh], pack_sh(P), pack_sh(M1), pack_sh(M2))

    def unpack_rep(t):
        return dict(norm_mix=t[:L], norm_ffn=t[L:2 * L], norm_kv=t[2 * L], norm_final=t[2 * L + 1],
                    b_forget=t[2 * L + 2, :H])

    def unpack_sh(t):
        t = t.reshape(NA, nrow, CS)
        return dict(conv_w=t[:, :4], conv_b=t[:, 4], b_lru_gates=t[:, 5:nrow - 1].reshape(NA, NBLK, GS),
                    lru_param=t[:, nrow - 1])

    dkvf = jnp.concatenate([G["w_kv"].astype(F32), G["w_f"][:, :H]], axis=1)
    groups = [
        G["w_ffn_in"],
        [g.reshape(N_CHIPS, -1, D) for g in G["w_ffn_out"]],
        G["w_rec_in"],
        [g.reshape(NBLK, BW, N_CHIPS, GS).transpose(2, 0, 1, 3).reshape(N_CHIPS, NBLK * BW, GS) for g in G["w_gates"]],
        [g.reshape(N_CHIPS, -1, D) for g in G["w_rec_out"]],
        [dkvf.reshape(D, N_CHIPS, -1).transpose(1, 0, 2).astype(BF16)],
        [g.reshape(N_CHIPS, -1, D) for g in G["w_q"]],
        [g.reshape(N_CHIPS, -1, D) for g in G["w_o"]],
    ]
    layout = [(w, l) for w, layers in enumerate(groups) for l in range(len(layers))]
    flat = [g for layers in groups for g in layers]
    others = _reduce_d2d("reduce_d2d", flat)
    chip_parts = [_sum_cores(f"sum_cores_{_BIG[w]}_{l}", g, o, core_id)
                  for (w, l), g, o in zip(layout, flat, others)]
    received = _scatter_ici("scatter_ici", chip_parts, layout)
    me_core = jnp.stack([chip, core]).astype(jnp.int32)
    full = [lax.empty(s.shape, F32) for s in shards]
    for (w, l), part in zip(layout, chip_parts):
        full[w] = _sum_chips(f"sum_chips_{_BIG[w]}_{l}", received[w], part, full[w], l, me_core)
    full = _share_d2d("share_d2d", full)
    big = {w: _adamw(f"adamw_{w}", [g.reshape(P[w].shape)], P[w], M1[w], M2[w]) for w, g in zip(_BIG, full)}

    outs = []
    for i in range(4):
        small = {**unpack_rep(rep[i]), **unpack_sh(shd[i])}
        outs.append([big[w][i] if w in big else small[w] for w in _WEIGHTS])
    return (loss, grad_x.reshape(1, S, D), *outs[0], *outs[1], *outs[2], *outs[3])
```

```python
import functools
import math

import jax
import jax.numpy as jnp
from jax import lax
from jax.experimental import pallas as pl
from jax.experimental.pallas import tpu as pltpu

F32 = jnp.float32
BF16 = jnp.bfloat16

EPS = 1e-6
LRU_C = 8.0
HEAD_DIM = 64
LANES = 128
SUBLANES = 8
VMEM_LIMIT = 48 * 1024 * 1024
N_CHIPS = 4
N_DEV = 8

ADAM_LR = 0.001
ADAM_B1 = 0.9
ADAM_B2 = 0.999
ADAM_EPS = 1e-08
ADAM_WD = 0.01
ADAM_STEP = 10

_NN = (((1,), (0,)), ((), ()))
_NT = (((1,), (1,)), ((), ()))
_TN = (((0,), (0,)), ((), ()))
_DN = {"nn": _NN, "nt": _NT, "tn": _TN}
MESH = pl.DeviceIdType.MESH


def _params(sem):
    return pltpu.CompilerParams(dimension_semantics=sem, vmem_limit_bytes=VMEM_LIMIT)


def _tile(n, want):
    if n <= want:
        return n
    t = (want // LANES) * LANES
    while t >= LANES:
        if n % t == 0:
            return t
        t -= LANES
    return n


def _sigmoid(x):
    return 1.0 / (1.0 + jnp.exp(-x))


def _softplus(x):
    return jnp.maximum(x, 0.0) + jnp.log(1.0 + jnp.exp(-jnp.abs(x)))


_GELU_C = math.sqrt(2.0 / math.pi)


def _gelu_and_grad(x):
    inner = _GELU_C * (x + 0.044715 * x * x * x)
    t = jnp.tanh(inner)
    g = 0.5 * x * (1.0 + t)
    dg = 0.5 * (1.0 + t) + 0.5 * x * (1.0 - t * t) * _GELU_C * (1.0 + 3.0 * 0.044715 * x * x)
    return g, dg


def _mm(name, mode, a, b, *, grid, a_spec, b_spec, out_shape, out_dtype, out_spec, nk=1,
        res=None, res_spec=None, bias=None, bias_spec=None, scale=None):
    dn = _DN[mode]
    has_res, has_bias = res is not None, bias is not None
    blk = tuple(d for d in out_spec.block_shape if d is not None)

    def body(*refs):
        a_ref, b_ref = refs[0], refs[1]
        p = 2
        r_ref = refs[p] if has_res else None
        p += int(has_res)
        bias_ref = refs[p] if has_bias else None
        p += int(has_bias)
        o_ref = refs[p]
        part = lax.dot_general(a_ref[...], b_ref[...], dn, preferred_element_type=F32)

        def finish(acc):
            if scale is not None:
                acc = acc * scale
            if has_bias:
                acc = acc + bias_ref[...]
            if has_res:
                acc = r_ref[...] + acc
            o_ref[...] = acc.astype(o_ref.dtype)

        if nk == 1:
            finish(part)
        else:
            acc_ref = refs[p + 1]
            k = pl.program_id(2)

            @pl.when(k == 0)
            def _():
                acc_ref[...] = part

            @pl.when(k > 0)
            def _():
                acc_ref[...] += part

            @pl.when(k == nk - 1)
            def _():
                finish(acc_ref[...])

    ins, specs = [a, b], [a_spec, b_spec]
    if has_res:
        ins.append(res)
        specs.append(res_spec)
    if has_bias:
        ins.append(bias)
        specs.append(bias_spec)
    sem = ("parallel", "parallel") + (("arbitrary",) if len(grid) == 3 else ())
    return pl.pallas_call(
        body, name=name, grid=grid, in_specs=specs, out_specs=out_spec,
        out_shape=jax.ShapeDtypeStruct(out_shape, out_dtype),
        scratch_shapes=[pltpu.VMEM(blk, F32)] if nk > 1 else [],
        compiler_params=_params(sem),
    )(*ins)


def _mm_nn(name, a, b, *, b_lead=(), out_dtype, tm=512, tn=512, res=None, bias=None, scale=None):
    M, K = a.shape
    N = b.shape[-1]
    tm, tn = _tile(M, tm), _tile(N, tn)
    nl = len(b_lead)
    return _mm(
        name, "nn", a, b, grid=(M // tm, N // tn),
        a_spec=pl.BlockSpec((tm, K), lambda i, j: (i, 0)),
        b_spec=pl.BlockSpec((None,) * nl + (K, tn), lambda i, j: tuple(b_lead) + (0, j)),
        out_shape=(M, N), out_dtype=out_dtype, out_spec=pl.BlockSpec((tm, tn), lambda i, j: (i, j)),
        res=res, res_spec=pl.BlockSpec((tm, tn), lambda i, j: (i, j)),
        bias=bias, bias_spec=pl.BlockSpec((1, tn), lambda i, j: (0, j)), scale=scale)


def _mm_nt(name, a, b, *, b_lead=(), out_dtype, tm=512, tn=512, tk=2048):
    M, K = a.shape
    N = b.shape[-2]
    tm, tn, tk = _tile(M, tm), _tile(N, tn), _tile(K, tk)
    nk = K // tk
    nl = len(b_lead)
    return _mm(
        name, "nt", a, b, grid=(M // tm, N // tn, nk), nk=nk,
        a_spec=pl.BlockSpec((tm, tk), lambda i, j, k: (i, k)),
        b_spec=pl.BlockSpec((None,) * nl + (tn, tk), lambda i, j, k: tuple(b_lead) + (j, k)),
        out_shape=(M, N), out_dtype=out_dtype, out_spec=pl.BlockSpec((tm, tn), lambda i, j, k: (i, j)))


def _mm_tn(name, a, b, *, out_dtype, tm=512, tn=512):
    S, M = a.shape
    N = b.shape[1]
    tm, tn = _tile(M, tm), _tile(N, tn)
    return _mm(
        name, "tn", a, b, grid=(M // tm, N // tn),
        a_spec=pl.BlockSpec((S, tm), lambda i, j: (0, i)),
        b_spec=pl.BlockSpec((S, tn), lambda i, j: (0, j)),
        out_shape=(M, N), out_dtype=out_dtype, out_spec=pl.BlockSpec((tm, tn), lambda i, j: (i, j)))


def _rmsnorm_fwd(name, h, g, tr=256):
    S, D = h.shape
    tr = _tile(S, tr)

    def body(h_ref, g_ref, o_ref):
        x = h_ref[...]
        r = lax.rsqrt(jnp.mean(x * x, axis=-1, keepdims=True) + EPS)
        o_ref[...] = (x * r * g_ref[...]).astype(o_ref.dtype)

    return pl.pallas_call(
        body, name=name, grid=(S // tr,),
        in_specs=[pl.BlockSpec((tr, D), lambda i: (i, 0)), pl.BlockSpec((1, D), lambda i: (0, 0))],
        out_specs=pl.BlockSpec((tr, D), lambda i: (i, 0)),
        out_shape=jax.ShapeDtypeStruct((S, D), BF16),
        compiler_params=_params(("parallel",)),
    )(h, g)


def _rmsnorm_bwd(name, dxn, h, g, dh_in, tr=256):
    S, D = h.shape
    tr = _tile(S, tr)

    def body(dxn_ref, h_ref, g_ref, dh_ref, o_ref, ob_ref, dg_ref):
        i = pl.program_id(0)
        x = h_ref[...]
        dy = dxn_ref[...].astype(F32)
        r = lax.rsqrt(jnp.mean(x * x, axis=-1, keepdims=True) + EPS)
        xr = x * r
        dyg = dy * g_ref[...]
        dx = r * dyg - xr * (r * jnp.mean(dyg * xr, axis=-1, keepdims=True))
        out = dh_ref[...] + dx
        o_ref[...] = out
        ob_ref[...] = out.astype(BF16)
        part = jnp.sum(dy * xr, axis=0, keepdims=True)

        @pl.when(i == 0)
        def _():
            dg_ref[...] = part

        @pl.when(i > 0)
        def _():
            dg_ref[...] += part

    row = pl.BlockSpec((tr, D), lambda i: (i, 0))
    vec = pl.BlockSpec((1, D), lambda i: (0, 0))
    return pl.pallas_call(
        body, name=name, grid=(S // tr,),
        in_specs=[row, row, vec, row], out_specs=[row, row, vec],
        out_shape=[jax.ShapeDtypeStruct((S, D), F32), jax.ShapeDtypeStruct((S, D), BF16),
                   jax.ShapeDtypeStruct((1, D), F32)],
        compiler_params=_params(("arbitrary",)),
    )(dxn, h, g, dh_in)


def _loss_head(name, h, target, g, tr=256):
    S, D = h.shape
    tr = _tile(S, tr)

    def body(h_ref, t_ref, g_ref, o_ref, ob_ref, dg_ref, loss_ref):
        i = pl.program_id(0)
        x = h_ref[...]
        gg = g_ref[...]
        r = lax.rsqrt(jnp.mean(x * x, axis=-1, keepdims=True) + EPS)
        xr = x * r
        err = xr * gg - t_ref[...]
        lpart = 0.5 * jnp.sum(jnp.mean(err * err, axis=-1, keepdims=True), axis=0, keepdims=True)
        dy = err * (1.0 / D)
        dyg = dy * gg
        dx = r * dyg - xr * (r * jnp.mean(dyg * xr, axis=-1, keepdims=True))
        o_ref[...] = dx
        ob_ref[...] = dx.astype(BF16)
        part = jnp.sum(dy * xr, axis=0, keepdims=True)
        lrow = jnp.broadcast_to(lpart, (1, LANES))

        @pl.when(i == 0)
        def _():
            dg_ref[...] = part
            loss_ref[...] = lrow

        @pl.when(i > 0)
        def _():
            dg_ref[...] += part
            loss_ref[...] += lrow

    row = pl.BlockSpec((tr, D), lambda i: (i, 0))
    vec = pl.BlockSpec((1, D), lambda i: (0, 0))
    return pl.pallas_call(
        body, name=name, grid=(S // tr,),
        in_specs=[row, row, vec], out_specs=[row, row, vec, pl.BlockSpec((1, LANES), lambda i: (0, 0))],
        out_shape=[jax.ShapeDtypeStruct((S, D), F32), jax.ShapeDtypeStruct((S, D), BF16),
                   jax.ShapeDtypeStruct((1, D), F32), jax.ShapeDtypeStruct((1, LANES), F32)],
        compiler_params=_params(("arbitrary",)),
    )(h, target, g)


def _swiglu_fwd(name, z3, tr=256, tc=1408):
    _, S, F = z3.shape
    tr, tc = _tile(S, tr), _tile(F, tc)

    def body(z_ref, a_ref):
        zg = z_ref[0].astype(F32)
        zu = z_ref[1].astype(F32)
        a_ref[...] = (zg * _sigmoid(zg) * zu).astype(a_ref.dtype)

    return pl.pallas_call(
        body, name=name, grid=(S // tr, F // tc),
        in_specs=[pl.BlockSpec((2, tr, tc), lambda i, j: (0, i, j))],
        out_specs=pl.BlockSpec((tr, tc), lambda i, j: (i, j)),
        out_shape=jax.ShapeDtypeStruct((S, F), BF16),
        compiler_params=_params(("parallel", "parallel")),
    )(z3)


def _swiglu_bwd(name, da, z3, tr=256, tc=1408):
    _, S, F = z3.shape
    tr, tc = _tile(S, tr), _tile(F, tc)

    def body(da_ref, z_ref, dz_ref):
        zg = z_ref[0].astype(F32)
        zu = z_ref[1].astype(F32)
        d = da_ref[...].astype(F32)
        sg = _sigmoid(zg)
        silu = zg * sg
        dz_ref[0] = (d * zu * (sg * (1.0 + zg * (1.0 - sg)))).astype(dz_ref.dtype)
        dz_ref[1] = (d * silu).astype(dz_ref.dtype)

    return pl.pallas_call(
        body, name=name, grid=(S // tr, F // tc),
        in_specs=[pl.BlockSpec((tr, tc), lambda i, j: (i, j)),
                  pl.BlockSpec((2, tr, tc), lambda i, j: (0, i, j))],
        out_specs=pl.BlockSpec((2, tr, tc), lambda i, j: (0, i, j)),
        out_shape=jax.ShapeDtypeStruct((2, S, F), BF16),
        compiler_params=_params(("parallel", "parallel")),
    )(da, z3)


SCAN_ROWS = 64


def _group_scan(A, B, reverse):
    n = A.shape[0]
    sub = lax.broadcasted_iota(jnp.int32, A.shape, 0) % SUBLANES
    for d in (1, 2, 4):
        if reverse:
            A_sh, B_sh = pltpu.roll(A, n - d, 0), pltpu.roll(B, n - d, 0)
            keep = sub < SUBLANES - d
        else:
            A_sh, B_sh = pltpu.roll(A, d, 0), pltpu.roll(B, d, 0)
            keep = sub >= d
        B = jnp.where(keep, A * B_sh + B, B)
        A = jnp.where(keep, A * A_sh, A)
    return A, B


def _block_scan(a, u, carry, reverse):
    A, B = _group_scan(a, u, reverse)
    ng = a.shape[0] // SUBLANES
    out = [None] * ng
    order = range(ng - 1, -1, -1) if reverse else range(ng)
    for gi in order:
        sl = slice(gi * SUBLANES, (gi + 1) * SUBLANES)
        hg = A[sl] * carry + B[sl]
        out[gi] = hg
        carry = hg[0:1] if reverse else hg[SUBLANES - 1:SUBLANES]
    return jnp.concatenate(out, axis=0), carry


def _lru_gates(rc, gip, grp, sp):
    gi = _sigmoid(gip)
    gr = _sigmoid(grp)
    la = -LRU_C * gr * sp
    a = jnp.exp(la)
    om = -jnp.tanh(la) * (a * a + 1.0)
    mult = jnp.sqrt(om)
    return gi, gr, a, mult


def _lru_fwd(name, proj, rc, gip, grp, lru_p, tc=256):
    S, C = rc.shape
    tc = _tile(C, tc)
    nb = S // SCAN_ROWS

    def body(gb_ref, rc_ref, gi_ref, gr_ref, l_ref, h_ref, m_ref):
        sp = _softplus(-l_ref[...])

        def step(b, carry):
            rows = pl.ds(pl.multiple_of(b * SCAN_ROWS, SCAN_ROWS), SCAN_ROWS)
            rcb = rc_ref[rows, :]
            gi, _, a, mult = _lru_gates(rcb, gi_ref[rows, :], gr_ref[rows, :], sp)
            h, carry = _block_scan(a, rcb * gi * mult, carry, False)
            h_ref[rows, :] = h
            gel, _ = _gelu_and_grad(gb_ref[rows, :])
            m_ref[rows, :] = (gel * h).astype(m_ref.dtype)
            return carry

        lax.fori_loop(0, nb, step, jnp.zeros((1, tc), F32))

    col = pl.BlockSpec((S, tc), lambda j: (0, j))
    return pl.pallas_call(
        body, name=name, grid=(C // tc,),
        in_specs=[col, col, col, col, pl.BlockSpec((1, tc), lambda j: (0, j))],
        out_specs=[col, col],
        out_shape=[jax.ShapeDtypeStruct((S, C), F32), jax.ShapeDtypeStruct((S, C), BF16)],
        compiler_params=_params(("parallel",)),
    )(proj, rc, gip, grp, lru_p)


def _lru_bwd(name, dm, proj, hrec, rc, gip, grp, lru_p, tc=256):
    S, C = rc.shape
    tc = _tile(C, tc)
    nb = S // SCAN_ROWS
    R = SCAN_ROWS

    def body(dm_ref, gb_ref, h_ref, rc_ref, gi_ref, gr_ref, l_ref,
             dgb_ref, dgi_ref, dgr_ref, drc_ref, dbi_ref, dbr_ref, dl_ref):
        lp = l_ref[...]
        sp = _softplus(-lp)
        row = lax.broadcasted_iota(jnp.int32, (R, tc), 0)
        zero = jnp.zeros((1, tc), F32)

        def step(t, carry):
            mu_in, s_i, s_r, s_sp = carry
            b = nb - 1 - t
            r0 = pl.multiple_of(b * R, R)
            rows = pl.ds(r0, R)
            rcb = rc_ref[rows, :]
            gi, gr, a, mult = _lru_gates(rcb, gi_ref[rows, :], gr_ref[rows, :], sp)
            gel, dgel = _gelu_and_grad(gb_ref[rows, :])
            dmb = dm_ref[rows, :]
            h = h_ref[rows, :]
            dgb_ref[rows, :] = (dmb * h * dgel).astype(dgb_ref.dtype)
            dh = dmb * gel
            mu, mu_out = _block_scan(a, a * dh, mu_in, True)
            mu_next = jnp.where(row == R - 1, mu_in, pltpu.roll(mu, R - 1, 0))
            lam = dh + mu_next
            p0 = pl.multiple_of(jnp.maximum(r0 - SUBLANES, 0), SUBLANES)
            prev = h_ref[pl.ds(p0, SUBLANES), :][SUBLANES - 1:SUBLANES]
            prev = jnp.where(b > 0, prev, 0.0)
            h_prev = jnp.where(row == 0, prev, pltpu.roll(h, 1, 0))
            da = lam * h_prev
            d_mult = lam * rcb * gi
            d_la = da * a - d_mult * (a * a) / mult
            d_grp = d_la * (-LRU_C * sp) * gr * (1.0 - gr)
            d_gip = lam * rcb * mult * gi * (1.0 - gi)
            dgr_ref[rows, :] = d_grp.astype(dgr_ref.dtype)
            dgi_ref[rows, :] = d_gip.astype(dgi_ref.dtype)
            drc_ref[rows, :] = lam * gi * mult
            s_i = s_i + jnp.sum(d_gip, axis=0, keepdims=True)
            s_r = s_r + jnp.sum(d_grp, axis=0, keepdims=True)
            s_sp = s_sp + jnp.sum(d_la * gr, axis=0, keepdims=True)
            return mu_out, s_i, s_r, s_sp

        _, s_i, s_r, s_sp = lax.fori_loop(0, nb, step, (zero, zero, zero, zero))
        dbi_ref[...] = s_i
        dbr_ref[...] = s_r
        dl_ref[...] = (-LRU_C * s_sp) * (-_sigmoid(-lp))

    col = pl.BlockSpec((S, tc), lambda j: (0, j))
    vec = pl.BlockSpec((1, tc), lambda j: (0, j))
    return pl.pallas_call(
        body, name=name, grid=(C // tc,),
        in_specs=[col, col, col, col, col, col, vec],
        out_specs=[col, col, col, col, vec, vec, vec],
        out_shape=[jax.ShapeDtypeStruct((S, C), BF16), jax.ShapeDtypeStruct((S, C), BF16),
                   jax.ShapeDtypeStruct((S, C), BF16), jax.ShapeDtypeStruct((S, C), F32),
                   jax.ShapeDtypeStruct((1, C), F32), jax.ShapeDtypeStruct((1, C), F32),
                   jax.ShapeDtypeStruct((1, C), F32)],
        compiler_params=_params(("parallel",)),
    )(dm, proj, hrec, rc, gip, grp, lru_p)


def _cumsum_rows(name, u, reverse):
    S, C = u.shape
    nb = S // SCAN_ROWS

    def body(u_ref, o_ref):
        def step(t, carry):
            b = nb - 1 - t if reverse else t
            rows = pl.ds(pl.multiple_of(b * SCAN_ROWS, SCAN_ROWS), SCAN_ROWS)
            ub = u_ref[rows, :]
            h, carry = _block_scan(jnp.ones_like(ub), ub, carry, reverse)
            o_ref[rows, :] = h
            return carry

        lax.fori_loop(0, nb, step, jnp.zeros((1, C), F32))

    spec = pl.BlockSpec((S, C), lambda i: (0, 0))
    return pl.pallas_call(
        body, name=name, grid=(1,), in_specs=[spec], out_specs=spec,
        out_shape=jax.ShapeDtypeStruct((S, C), F32),
        compiler_params=_params(("arbitrary",)),
    )(u)


def _shift_down(x, k):
    row = lax.broadcasted_iota(jnp.int32, x.shape, 0)
    return jnp.where(row >= k, pltpu.roll(x, k, 0), 0.0)


def _shift_up(x, k):
    n = x.shape[0]
    row = lax.broadcasted_iota(jnp.int32, x.shape, 0)
    return jnp.where(row < n - k, pltpu.roll(x, n - k, 0), 0.0)


def _conv_fwd(name, proj, w, b, tc=256):
    S, C2 = proj.shape
    C = C2 // 2
    tc = _tile(C, tc)
    off = C // tc

    def body(x_ref, w_ref, b_ref, o_ref, ob_ref):
        x = x_ref[...]
        out = b_ref[...] + w_ref[3:4, :] * x
        for k in (1, 2, 3):
            out = out + w_ref[3 - k:4 - k, :] * _shift_down(x, k)
        o_ref[...] = out
        ob_ref[...] = out.astype(BF16)

    col = pl.BlockSpec((S, tc), lambda j: (0, j))
    return pl.pallas_call(
        body, name=name, grid=(C // tc,),
        in_specs=[pl.BlockSpec((S, tc), lambda j: (0, off + j)),
                  pl.BlockSpec((4, tc), lambda j: (0, j)), pl.BlockSpec((1, tc), lambda j: (0, j))],
        out_specs=[col, col],
        out_shape=[jax.ShapeDtypeStruct((S, C), F32), jax.ShapeDtypeStruct((S, C), BF16)],
        compiler_params=_params(("parallel",)),
    )(proj, w, b)


def _conv_bwd(name, drc, proj, w, tc=256):
    S, C = drc.shape
    tc = _tile(C, tc)
    off = C // tc

    def body(y_ref, x_ref, w_ref, dx_ref, dw_ref, db_ref):
        y = y_ref[...]
        x = x_ref[...]
        dx = w_ref[3:4, :] * y
        dw_ref[3:4, :] = jnp.sum(y * x, axis=0, keepdims=True)
        for k in (1, 2, 3):
            dx = dx + w_ref[3 - k:4 - k, :] * _shift_up(y, k)
            dw_ref[3 - k:4 - k, :] = jnp.sum(y * _shift_down(x, k), axis=0, keepdims=True)
        dx_ref[...] = dx.astype(dx_ref.dtype)
        db_ref[...] = jnp.sum(y, axis=0, keepdims=True)

    col = pl.BlockSpec((S, tc), lambda j: (0, j))
    return pl.pallas_call(
        body, name=name, grid=(C // tc,),
        in_specs=[col, pl.BlockSpec((S, tc), lambda j: (0, off + j)), pl.BlockSpec((4, tc), lambda j: (0, j))],
        out_specs=[col, pl.BlockSpec((4, tc), lambda j: (0, j)), pl.BlockSpec((1, tc), lambda j: (0, j))],
        out_shape=[jax.ShapeDtypeStruct((S, C), BF16), jax.ShapeDtypeStruct((4, C), F32),
                   jax.ShapeDtypeStruct((1, C), F32)],
        compiler_params=_params(("parallel",)),
    )(drc, proj, w)


def _gates_fwd(name, rcb, wg, bg):
    S, C = rcb.shape
    nblk, bw, _ = wg.shape

    def body(x_ref, w_ref, b_ref, gi_ref, gr_ref):
        g = jnp.dot(x_ref[...], w_ref[...], preferred_element_type=F32) + b_ref[...]
        gi_ref[...] = g[:, :bw]
        gr_ref[...] = g[:, bw:]

    col = pl.BlockSpec((S, bw), lambda n: (0, n))
    return pl.pallas_call(
        body, name=name, grid=(nblk,),
        in_specs=[col, pl.BlockSpec((None, bw, 2 * bw), lambda n: (n, 0, 0)),
                  pl.BlockSpec((None, 1, 2 * bw), lambda n: (n, 0, 0))],
        out_specs=[col, col],
        out_shape=[jax.ShapeDtypeStruct((S, C), F32), jax.ShapeDtypeStruct((S, C), F32)],
        compiler_params=_params(("parallel",)),
    )(rcb, wg, bg)


def _gates_bwd(name, dgi, dgr, rcb, wg, drc1):
    S, C = rcb.shape
    nblk, bw, _ = wg.shape

    def body(dgi_ref, dgr_ref, x_ref, w_ref, d1_ref, drc_ref, dw_ref):
        w = w_ref[...]
        x = x_ref[...]
        di, dr = dgi_ref[...], dgr_ref[...]
        drc_ref[...] = (d1_ref[...]
                        + lax.dot_general(di, w[:, :bw], _NT, preferred_element_type=F32)
                        + lax.dot_general(dr, w[:, bw:], _NT, preferred_element_type=F32))
        dw_ref[:, :bw] = lax.dot_general(x, di, _TN, preferred_element_type=F32).astype(dw_ref.dtype)
        dw_ref[:, bw:] = lax.dot_general(x, dr, _TN, preferred_element_type=F32).astype(dw_ref.dtype)

    col = pl.BlockSpec((S, bw), lambda n: (0, n))
    wspec = pl.BlockSpec((None, bw, 2 * bw), lambda n: (n, 0, 0))
    return pl.pallas_call(
        body, name=name, grid=(nblk,),
        in_specs=[col, col, col, wspec, col], out_specs=[col, wspec],
        out_shape=[jax.ShapeDtypeStruct((S, C), F32), jax.ShapeDtypeStruct((nblk, bw, 2 * bw), BF16)],
        compiler_params=_params(("parallel",)),
    )(dgi, dgr, rcb, wg, drc1)


def _att_tile(S):
    return 256 if S % 256 == 0 else 128


def _causal(T):
    r = lax.broadcasted_iota(jnp.int32, (T, T), 0)
    c = lax.broadcasted_iota(jnp.int32, (T, T), 1)
    return r >= c


def _attn_fwd(name, q, kv, negc3):
    S, D = q.shape
    HP = D // LANES
    T = _att_tile(S)
    nq = S // T

    def body(q_ref, k_ref, v_ref, nc_ref, o_ref, of_ref, lse_ref):
        is0 = lax.broadcasted_iota(jnp.int32, (T, LANES), 1) < HEAD_DIM
        tri = _causal(T)

        def q_step(qi, _):
            rows = pl.ds(pl.multiple_of(qi * T, T), T)
            qf = q_ref[rows, :].astype(F32)
            outs, lses = [], []
            for hh in range(2):
                qm = jnp.where(is0 if hh == 0 else jnp.logical_not(is0), qf, 0.0).astype(BF16)

                def tile(kj, carry, masked):
                    m, l, acc = carry
                    ks = pl.ds(pl.multiple_of(kj * T, T), T)
                    s = lax.dot_general(qm, k_ref[ks, :], _NT, preferred_element_type=F32)
                    s = s + nc_ref[hh:hh + 1, ks]
                    if masked:
                        s = jnp.where(tri, s, -jnp.inf)
                    m_new = jnp.maximum(m, jnp.max(s, axis=1, keepdims=True))
                    alpha = jnp.exp(m - m_new)
                    p = jnp.exp(s - m_new)
                    l = alpha * l + jnp.sum(p, axis=1, keepdims=True)
                    acc = alpha * acc + jnp.dot(p.astype(BF16), v_ref[ks, :], preferred_element_type=F32)
                    return m_new, l, acc

                init = (jnp.full((T, 1), -jnp.inf, F32), jnp.zeros((T, 1), F32), jnp.zeros((T, LANES), F32))
                carry = lax.fori_loop(0, qi, lambda kj, c: tile(kj, c, False), init)
                m, l, acc = tile(qi, carry, True)
                outs.append(acc / l)
                lses.append(jnp.broadcast_to(m + jnp.log(l), (T, LANES)))
            out = jnp.where(is0, outs[0], outs[1])
            o_ref[rows, :] = out.astype(o_ref.dtype)
            of_ref[rows, :] = out
            lse_ref[rows, :] = jnp.where(is0, lses[0], lses[1])
            return 0

        lax.fori_loop(0, nq, q_step, 0)

    return pl.pallas_call(
        body, name=name, grid=(HP,),
        in_specs=[pl.BlockSpec((S, LANES), lambda p: (0, p)),
                  pl.BlockSpec((S, LANES), lambda p: (0, p)),
                  pl.BlockSpec((S, LANES), lambda p: (0, HP + p)),
                  pl.BlockSpec((None, 2, S), lambda p: (p, 0, 0))],
        out_specs=[pl.BlockSpec((S, LANES), lambda p: (0, p))] * 3,
        out_shape=[jax.ShapeDtypeStruct((S, D), BF16), jax.ShapeDtypeStruct((S, D), F32),
                   jax.ShapeDtypeStruct((S, D), F32)],
        compiler_params=_params(("parallel",)),
    )(q, kv, kv, negc3)


def _attn_bwd(name, q, kv, negc3, o, do, lse):
    S, D = q.shape
    HP = D // LANES
    T = _att_tile(S)
    nq = S // T
    rep = T // LANES
    scale = HEAD_DIM ** -0.5

    def body(q_ref, k_ref, v_ref, nc_ref, o_ref, do_ref, lse_ref,
             dq_ref, dk_ref, dv_ref, dc_ref, dr_ref, dq_acc, lse_rep, dl_rep, dr_rep):
        is0 = lax.broadcasted_iota(jnp.int32, (T, LANES), 1) < HEAD_DIM
        tri = _causal(T)

        def prologue(qi, _):
            rows = pl.ds(pl.multiple_of(qi * T, T), T)
            prod = do_ref[rows, :].astype(F32) * o_ref[rows, :]
            lse_b = lse_ref[rows, :]
            for hh in range(2):
                msk = is0 if hh == 0 else jnp.logical_not(is0)
                dl = jnp.sum(jnp.where(msk, prod, 0.0), axis=1, keepdims=True)
                ls = jnp.max(jnp.where(msk, lse_b, -jnp.inf), axis=1, keepdims=True)
                dl_rep[hh, rows, :] = jnp.broadcast_to(dl, (T, LANES))
                lse_rep[hh, rows, :] = jnp.broadcast_to(ls, (T, LANES))
                dr_rep[hh, rows, :] = jnp.zeros((T, LANES), F32)
            dq_acc[rows, :] = jnp.zeros((T, LANES), F32)
            return 0

        lax.fori_loop(0, nq, prologue, 0)

        def kv_step(kj, _):
            ks = pl.ds(pl.multiple_of(kj * T, T), T)
            kf = k_ref[ks, :].astype(F32)
            vf = v_ref[ks, :].astype(F32)
            dks, dvs = [], []
            for hh in range(2):
                msk = is0 if hh == 0 else jnp.logical_not(is0)
                km = jnp.where(msk, kf, 0.0).astype(BF16)
                vm = jnp.where(msk, vf, 0.0).astype(BF16)
                ncr = nc_ref[hh:hh + 1, ks]

                def tile(qi, carry, masked):
                    dk_a, dv_a, dc_a = carry
                    rows = pl.ds(pl.multiple_of(qi * T, T), T)
                    qb = q_ref[rows, :]
                    dob = do_ref[rows, :]
                    s = lax.dot_general(qb, km, _NT, preferred_element_type=F32) + ncr
                    lse_t = jnp.tile(lse_rep[hh, rows, :], (1, rep))
                    dl_t = jnp.tile(dl_rep[hh, rows, :], (1, rep))
                    p = jnp.exp(s - lse_t)
                    if masked:
                        p = jnp.where(tri, p, 0.0)
                    dp = lax.dot_general(dob, vm, _NT, preferred_element_type=F32)
                    ds = p * (dp - dl_t)
                    pb, dsb = p.astype(BF16), ds.astype(BF16)
                    dv_a = dv_a + lax.dot_general(pb, dob, _TN, preferred_element_type=F32)
                    dk_a = dk_a + lax.dot_general(dsb, qb, _TN, preferred_element_type=F32)
                    dq_acc[rows, :] += jnp.dot(dsb, km, preferred_element_type=F32)
                    dc_a = dc_a + jnp.sum(ds, axis=0, keepdims=True)
                    dr_rep[hh, rows, :] += jnp.broadcast_to(jnp.sum(ds, axis=1, keepdims=True), (T, LANES))
                    return dk_a, dv_a, dc_a

                init = (jnp.zeros((T, LANES), F32), jnp.zeros((T, LANES), F32), jnp.zeros((1, T), F32))
                carry = tile(kj, init, True)
                dk_a, dv_a, dc_a = lax.fori_loop(kj + 1, nq, lambda qi, c: tile(qi, c, False), carry)
                dks.append(dk_a)
                dvs.append(dv_a)
                dc_ref[hh:hh + 1, ks] = -dc_a
            dk_ref[ks, :] = jnp.where(is0, dks[0], dks[1])
            dv_ref[ks, :] = jnp.where(is0, dvs[0], dvs[1])
            return 0

        lax.fori_loop(0, nq, kv_step, 0)
        dq_ref[...] = (dq_acc[...] * scale).astype(dq_ref.dtype)
        first = lax.broadcasted_iota(jnp.int32, (S, LANES), 1) < HEAD_DIM
        dr_ref[...] = jnp.where(first, dr_rep[0], dr_rep[1])

    blk = lambda off: pl.BlockSpec((S, LANES), lambda p: (0, off + p))
    nc_spec = pl.BlockSpec((None, 2, S), lambda p: (p, 0, 0))
    return pl.pallas_call(
        body, name=name, grid=(HP,),
        in_specs=[blk(0), blk(0), blk(HP), nc_spec, blk(0), blk(0), blk(0)],
        out_specs=[blk(0), blk(0), blk(0), nc_spec, blk(0)],
        out_shape=[jax.ShapeDtypeStruct((S, D), BF16), jax.ShapeDtypeStruct((S, D), F32),
                   jax.ShapeDtypeStruct((S, D), F32), jax.ShapeDtypeStruct((HP, 2, S), F32),
                   jax.ShapeDtypeStruct((S, D), F32)],
        scratch_shapes=[pltpu.VMEM((S, LANES), F32), pltpu.VMEM((2, S, LANES), F32),
                        pltpu.VMEM((2, S, LANES), F32), pltpu.VMEM((2, S, LANES), F32)],
        compiler_params=_params(("parallel",)),
    )(q, kv, kv, negc3, o, do, lse)


def _logsig_fwd(name, f):
    S, C = f.shape

    def body(f_ref, o_ref):
        o_ref[...] = -_softplus(-f_ref[...])

    spec = pl.BlockSpec((S, C), lambda i: (0, 0))
    return pl.pallas_call(body, name=name, grid=(1,), in_specs=[spec], out_specs=spec,
                          out_shape=jax.ShapeDtypeStruct((S, C), F32),
                          compiler_params=_params(("arbitrary",)))(f)


def _logsig_bwd(name, dls, f):
    S, C = f.shape

    def body(d_ref, f_ref, o_ref, s_ref):
        df = d_ref[...] * _sigmoid(-f_ref[...])
        o_ref[...] = df.astype(o_ref.dtype)
        s_ref[...] = jnp.sum(df, axis=0, keepdims=True)

    spec = pl.BlockSpec((S, C), lambda i: (0, 0))
    return pl.pallas_call(body, name=name, grid=(1,), in_specs=[spec, spec],
                          out_specs=[spec, pl.BlockSpec((1, C), lambda i: (0, 0))],
                          out_shape=[jax.ShapeDtypeStruct((S, C), BF16), jax.ShapeDtypeStruct((1, C), F32)],
                          compiler_params=_params(("arbitrary",)))(dls, f)


def _add_cast(name, parts, out_dtype, tr=256):
    S, C = parts[0].shape
    tr = _tile(S, tr)
    n = len(parts)

    def body(*refs):
        acc = refs[0][...].astype(F32)
        for r in refs[1:n]:
            acc = acc + r[...].astype(F32)
        refs[n][...] = acc.astype(out_dtype)

    spec = pl.BlockSpec((tr, C), lambda i: (i, 0))
    return pl.pallas_call(body, name=name, grid=(S // tr,), in_specs=[spec] * n, out_specs=spec,
                          out_shape=jax.ShapeDtypeStruct((S, C), out_dtype),
                          compiler_params=_params(("parallel",)))(*parts)


def _local_step(x, target, norm_final, layer_weights, layer_grads):
    S, D = x.shape
    HP = D // LANES
    scale = HEAD_DIM ** -0.5
    tm = _tile(S, 512)
    td = _tile(D, 512)
    saved = []
    h = x
    l = 0
    kv = negc3 = f_pre = hn_kv = h_kv = None
    while True:
        W = layer_weights(l, h)
        if W is None:
            break
        recurrent = "w_rec_in" in W
        FH = W["w_ffn_in"].shape[-1]
        F = 2 * FH
        xn = _rmsnorm_fwd(f"mix_norm_{l}", h, W["norm_mix"])
        if recurrent:
            CH = W["w_rec_in"].shape[-1]
            C = 2 * CH
            proj = _mm(f"rec_in_{l}", "nn", xn, W["w_rec_in"], grid=(S // tm, N_CHIPS),
                       a_spec=pl.BlockSpec((tm, D), lambda i, j: (i, 0)),
                       b_spec=pl.BlockSpec((None, D, CH), lambda i, j: (j, 0, 0)),
                       out_shape=(S, 2 * C), out_dtype=F32,
                       out_spec=pl.BlockSpec((tm, CH), lambda i, j: (i, j)))
            rc, rcb = _conv_fwd(f"conv_{l}", proj, W["conv_w"], W["conv_b"])
            gip, grp = _gates_fwd(f"gates_{l}", rcb, W["w_gates"], W["b_gates"])
            hrec, m = _lru_fwd(f"lru_{l}", proj, rc, gip, grp, W["lru_param"])
            h_mid = _mm_nn(f"rec_out_{l}", m, W["w_rec_out"], out_dtype=F32, res=h, tn=D)
            mix_saved = (xn, proj, rc, rcb, gip, grp, hrec, m)
        else:
            if "w_kv" in W:
                h_kv = h
                hn_kv = _rmsnorm_fwd("kv_norm", h, W["norm_kv"])
                kv = _mm_nn("kv_proj", hn_kv, W["w_kv"], out_dtype=BF16)
                f_pre = _mm_nn("f_proj", hn_kv, W["w_f"], out_dtype=F32, bias=W["b_f"])
                c = _cumsum_rows("c_cumsum", _logsig_fwd("logsig", f_pre), False)
                negc3 = (-c[:, :2 * HP]).T.reshape(HP, 2, S)
            q = _mm_nn(f"q_proj_{l}", xn, W["w_q"], out_dtype=BF16, scale=scale)
            o, of, lse = _attn_fwd(f"attn_fwd_{l}", q, kv, negc3)
            h_mid = _mm_nn(f"o_proj_{l}", o, W["w_o"], out_dtype=F32, res=h, tn=D)
            mix_saved = (xn, q, o, of, lse)
        hn = _rmsnorm_fwd(f"ffn_norm_{l}", h_mid, W["norm_ffn"])
        z3 = _mm(f"ffn_in_{l}", "nn", hn, W["w_ffn_in"], grid=(S // tm, N_CHIPS),
                 a_spec=pl.BlockSpec((tm, D), lambda i, j: (i, 0)),
                 b_spec=pl.BlockSpec((None, D, FH), lambda i, j: (j, 0, 0)),
                 out_shape=(2, S, F), out_dtype=BF16,
                 out_spec=pl.BlockSpec((None, tm, FH), lambda i, j: (j // 2, i, j % 2)))
        act = _swiglu_fwd(f"swiglu_{l}", z3)
        h_out = _mm_nn(f"ffn_out_{l}", act, W["w_ffn_out"], out_dtype=F32, res=h_mid, tn=D)
        saved.append((W, h, h_mid, mix_saved, (hn, z3, act)))
        h = h_out
        l += 1

    dh, dhb, dg_final, loss_row = _loss_head("loss_head", h, target, norm_final)

    dk_parts, dv_parts, dc_parts = [], [], []
    token = None
    for l in reversed(range(len(saved))):
        W, h_in, h_mid, mix_saved, (hn, z3, act) = saved[l]
        recurrent = "w_rec_in" in W
        FH = W["w_ffn_in"].shape[-1]
        G = {}
        norm_ffn = W["norm_ffn"]
        if token is not None:
            norm_ffn = norm_ffn + jnp.minimum(token[:1, :1], 0.0)
        G["w_ffn_out"] = _mm_tn(f"d_ffn_out_{l}", act, dhb, out_dtype=BF16, tn=D)
        da = _mm_nt(f"d_act_{l}", dhb, W["w_ffn_out"], out_dtype=BF16, tn=FH)
        dz3 = _swiglu_bwd(f"d_swiglu_{l}", da, z3)
        G["w_ffn_in"] = _mm(
            f"d_ffn_in_{l}", "tn", hn, dz3, grid=(D // td, N_CHIPS),
            a_spec=pl.BlockSpec((S, td), lambda i, j: (0, i)),
            b_spec=pl.BlockSpec((None, S, FH), lambda i, j: (j // 2, 0, j % 2)),
            out_shape=(N_CHIPS, D, FH), out_dtype=BF16,
            out_spec=pl.BlockSpec((None, td, FH), lambda i, j: (j, i, 0)))
        dhn = _mm(f"d_ffn_hn_{l}", "nt", dz3, W["w_ffn_in"], grid=(S // tm, 1, N_CHIPS), nk=N_CHIPS,
                  a_spec=pl.BlockSpec((None, tm, FH), lambda i, j, k: (k // 2, i, k % 2)),
                  b_spec=pl.BlockSpec((None, D, FH), lambda i, j, k: (k, 0, 0)),
                  out_shape=(S, D), out_dtype=F32, out_spec=pl.BlockSpec((tm, D), lambda i, j, k: (i, 0)))
        dh, dhb, G["norm_ffn"] = _rmsnorm_bwd(f"d_ffn_norm_{l}", dhn, h_mid, norm_ffn, dh)
        if recurrent:
            CH = W["w_rec_in"].shape[-1]
            C = 2 * CH
            xn, proj, rc, rcb, gip, grp, hrec, m = mix_saved
            G["w_rec_out"] = _mm_tn(f"d_rec_out_{l}", m, dhb, out_dtype=BF16, tn=D)
            dm = _mm_nt(f"d_m_{l}", dhb, W["w_rec_out"], out_dtype=F32, tn=C)
            dgb, dgi, dgr, drc1, G["b_gi"], G["b_gr"], G["lru_param"] = _lru_bwd(
                f"d_lru_{l}", dm, proj, hrec, rc, gip, grp, W["lru_param"])
            drc, G["w_gates"] = _gates_bwd(f"d_gates_{l}", dgi, dgr, rcb, W["w_gates"], drc1)
            drec, G["conv_w"], G["conv_b"] = _conv_bwd(f"d_conv_{l}", drc, proj, W["conv_w"])
            dproj = jnp.concatenate([dgb, drec], axis=1)
            G["w_rec_in"] = _mm(
                f"d_rec_in_{l}", "tn", xn, dproj, grid=(1, N_CHIPS),
                a_spec=pl.BlockSpec((S, D), lambda i, j: (0, 0)),
                b_spec=pl.BlockSpec((S, CH), lambda i, j: (0, j)),
                out_shape=(N_CHIPS, D, CH), out_dtype=BF16,
                out_spec=pl.BlockSpec((None, D, CH), lambda i, j: (j, 0, 0)))
            dxn = _mm(f"d_rec_xn_{l}", "nt", dproj, W["w_rec_in"], grid=(S // tm, 1, N_CHIPS), nk=N_CHIPS,
                      a_spec=pl.BlockSpec((tm, CH), lambda i, j, k: (i, k)),
                      b_spec=pl.BlockSpec((None, D, CH), lambda i, j, k: (k, 0, 0)),
                      out_shape=(S, D), out_dtype=F32, out_spec=pl.BlockSpec((tm, D), lambda i, j, k: (i, 0)))
        else:
            xn, q, o, of, lse = mix_saved
            G["w_o"] = _mm_tn(f"d_o_proj_{l}", o, dhb, out_dtype=BF16, tn=D)
            do = _mm_nt(f"d_o_{l}", dhb, W["w_o"], out_dtype=BF16, tn=D)
            dq, dk, dv, dc3, dcr = _attn_bwd(f"attn_bwd_{l}", q, kv, negc3, of, do, lse)
            dk_parts.append(dk)
            dv_parts.append(dv)
            dc_parts.append(dc3.reshape(2 * HP, S).T + dcr[:, ::HEAD_DIM])
            G["w_q"] = _mm_tn(f"d_q_proj_{l}", xn, dq, out_dtype=BF16, tn=D)
            dxn = _mm_nt(f"d_q_xn_{l}", dq, W["w_q"], out_dtype=F32, tn=D)
        dh, dhb, G["norm_mix"] = _rmsnorm_bwd(f"d_mix_norm_{l}", dxn, h_in, W["norm_mix"], dh)
        if "w_kv" in W:
            dkb = _add_cast("dk_sum", dk_parts, BF16)
            dvb = _add_cast("dv_sum", dv_parts, BF16)
            dkv = jnp.concatenate([dkb, dvb], axis=1)
            dc = sum(dc_parts[1:], dc_parts[0])
            dc_pad = jnp.pad(dc, ((0, 0), (0, LANES - 2 * HP)))
            dls = _cumsum_rows("dc_cumsum", dc_pad, True)
            dfb, G["b_f"] = _logsig_bwd("d_logsig", dls, f_pre)
            G["w_kv"] = _mm_tn("d_kv_proj", hn_kv, dkv, out_dtype=BF16)
            G["w_f"] = _mm_tn("d_f_proj", hn_kv, dfb, out_dtype=F32)
            dhn1 = _mm_nt("d_kv_hn", dkv, W["w_kv"], out_dtype=F32, tn=D)
            dhn2 = _mm_nt("d_f_hn", dfb, W["w_f"], out_dtype=F32, tn=D)
            dhn_kv = _add_cast("d_kv_hn_sum", [dhn1, dhn2], F32)
            dh, dhb, G["norm_kv"] = _rmsnorm_bwd("d_kv_norm", dhn_kv, h_kv, W["norm_kv"], dh)
        token = layer_grads(l, G)
    return loss_row, dh, dg_final


_ANY = pl.BlockSpec(memory_space=pl.ANY)


def _position():
    return lax.axis_index("x"), lax.axis_index("y"), lax.axis_index("c")


def _chip_peers(x, y):
    return [(1 - x, y), (x, 1 - y), (1 - x, 1 - y)]


def _half_rows(c, n):
    h = n // 2
    assert h % 16 == 0
    return pl.ds(pl.multiple_of(c * h, 16), h)


def _items(arrays):
    return [(w, l) for w, a in enumerate(arrays) for l in range(a.shape[0])]


def _place_own(name, shard, layer, me):
    _, R, C = shard.shape
    tr = _row_tile(R, C, shard.dtype.itemsize)

    def body(me_ref, x_ref, o_ref):
        o_ref[...] = x_ref[...]

    return pl.pallas_call(
        body, name=name,
        grid_spec=pltpu.PrefetchScalarGridSpec(
            num_scalar_prefetch=1, grid=(R // tr,),
            in_specs=[pl.BlockSpec((None, tr, C), lambda i, me_ref: (layer, i, 0))],
            out_specs=pl.BlockSpec((None, tr, C), lambda i, me_ref: (me_ref[0], i, 0))),
        out_shape=jax.ShapeDtypeStruct((N_CHIPS, R, C), shard.dtype),
        compiler_params=_params(("parallel",)),
    )(me, shard)


def _gather_smalls(name, smalls):
    ns = len(smalls)

    def body(*refs):
        ins, outs = refs[:ns], refs[ns:2 * ns]
        send_sems, recv_sems, local_sems = refs[2 * ns:]
        x, y, c = _position()
        me = 2 * x + y
        peers = _chip_peers(x, y)

        def remote(t, k, chip):
            px, py = peers[k]
            return pltpu.make_async_remote_copy(
                src_ref=ins[t], dst_ref=outs[t].at[chip], send_sem=send_sems.at[3 * t + k],
                recv_sem=recv_sems.at[3 * t + k], device_id=(px, py, c), device_id_type=MESH)

        local = [pltpu.make_async_copy(ins[t], outs[t].at[me], local_sems.at[t]) for t in range(ns)]
        for t in range(ns):
            local[t].start()
            for k in range(3):
                remote(t, k, me).start()
        for t in range(ns):
            for k in range(3):
                px, py = peers[k]
                remote(t, k, 2 * px + py).wait_recv()
        for t in range(ns):
            for k in range(3):
                remote(t, k, me).wait_send()
            local[t].wait()

    return pl.pallas_call(
        body, name=name, in_specs=[_ANY] * ns, out_specs=[_ANY] * ns,
        out_shape=[jax.ShapeDtypeStruct((N_CHIPS,) + s.shape, s.dtype) for s in smalls],
        scratch_shapes=[pltpu.SemaphoreType.DMA((3 * ns,)), pltpu.SemaphoreType.DMA((3 * ns,)),
                        pltpu.SemaphoreType.DMA((ns,))],
    )(*smalls)


_SEM = pl.BlockSpec(memory_space=pltpu.SEMAPHORE)
_SPLIT = pltpu.CompilerParams(has_side_effects=pltpu.SideEffectType.DATAFLOW_SIDE_EFFECTING)


def _weight_copy(shards, buf, items, sems, i, k, chip_of_dst, peers, c):
    w, l = items[i]
    px, py = peers[k]
    half = _half_rows(c, shards[w].shape[1])
    return pltpu.make_async_remote_copy(
        src_ref=shards[w].at[l, half], dst_ref=buf.at[chip_of_dst, half],
        send_sem=sems[0].at[3 * i + k], recv_sem=sems[1].at[3 * i + k],
        device_id=(px, py, c), device_id_type=MESH)


def _gather_start(name, shards, bufs, items):
    nw, n = len(shards), len(bufs)

    def body(*refs):
        ins, outs, sems = refs[:nw], refs[nw + n:nw + 2 * n], refs[nw + 2 * n:]
        x, y, c = _position()
        peers = _chip_peers(x, y)
        for i in range(n):
            for k in range(3):
                _weight_copy(ins, outs[i], items, sems, i, k, 2 * x + y, peers, c).start()

    res = pl.pallas_call(
        body, name=name, in_specs=[_ANY] * (nw + n), out_specs=[_ANY] * n + [_SEM, _SEM],
        out_shape=[jax.ShapeDtypeStruct(b.shape, b.dtype) for b in bufs]
        + [pltpu.SemaphoreType.DMA((3 * n,)), pltpu.SemaphoreType.DMA((3 * n,))],
        input_output_aliases={nw + i: i for i in range(n)}, compiler_params=_SPLIT,
    )(*shards, *bufs)
    return res[:n], res[n:]


def _gather_wait(name, shards, bufs, items, ids, sems, after):
    nw, m = len(shards), len(ids)

    def body(*refs):
        ins, bs = refs[:nw], refs[nw:nw + m]
        sem_refs = refs[nw + m:nw + m + 2]
        x, y, c = _position()
        peers = _chip_peers(x, y)
        for j, i in enumerate(ids):
            for k in range(3):
                px, py = peers[k]
                _weight_copy(ins, bs[j], items, sem_refs, i, k, 2 * px + py, peers, c).wait_recv()
        for j, i in enumerate(ids):
            for k in range(3):
                _weight_copy(ins, bs[j], items, sem_refs, i, k, 2 * x + y, peers, c).wait_send()

    res = pl.pallas_call(
        body, name=name, in_specs=[_ANY] * (nw + m) + [_SEM, _SEM, _ANY], out_specs=[_ANY] * m,
        out_shape=[jax.ShapeDtypeStruct(bufs[i].shape, bufs[i].dtype) for i in ids],
        input_output_aliases={nw + j: j for j in range(m)}, compiler_params=_SPLIT,
    )(*shards, *[bufs[i] for i in ids], *sems, after)
    return list(res)


def _gather_d2d(name, bufs):
    n = len(bufs)

    def body(*refs):
        ins, outs = refs[:n], refs[n:2 * n]
        send_sems, recv_sems = refs[2 * n:]
        x, y, c = _position()
        peers = _chip_peers(x, y)

        def remote(i, k, core):
            px, py = peers[k]
            half = _half_rows(core, ins[i].shape[1])
            return pltpu.make_async_remote_copy(
                src_ref=ins[i].at[2 * px + py, half], dst_ref=outs[i].at[2 * px + py, half],
                send_sem=send_sems.at[3 * i + k], recv_sem=recv_sems.at[3 * i + k],
                device_id=(x, y, 1 - c), device_id_type=MESH)

        for i in range(n):
            for k in range(3):
                remote(i, k, c).start()
        for i in range(n):
            for k in range(3):
                remote(i, k, 1 - c).wait_recv()
        for i in range(n):
            for k in range(3):
                remote(i, k, c).wait_send()

    return list(pl.pallas_call(
        body, name=name, in_specs=[_ANY] * n, out_specs=[_ANY] * n,
        out_shape=[jax.ShapeDtypeStruct(g.shape, g.dtype) for g in bufs],
        input_output_aliases={i: i for i in range(n)},
        scratch_shapes=[pltpu.SemaphoreType.DMA((3 * n,)), pltpu.SemaphoreType.DMA((3 * n,))],
    )(*bufs))


def _reduce_d2d(name, grads):
    n = len(grads)

    def body(*refs):
        ins, outs = refs[:n], refs[n:2 * n]
        send_sems, recv_sems = refs[2 * n:]
        x, y, c = _position()
        remote = [pltpu.make_async_remote_copy(
            src_ref=ins[i].at[:, _half_rows(1 - c, ins[i].shape[1])], dst_ref=outs[i],
            send_sem=send_sems.at[i], recv_sem=recv_sems.at[i],
            device_id=(x, y, 1 - c), device_id_type=MESH) for i in range(n)]
        for cp in remote:
            cp.start()
        for cp in remote:
            cp.wait_recv()
        for cp in remote:
            cp.wait_send()

    return pl.pallas_call(
        body, name=name, in_specs=[_ANY] * n, out_specs=[_ANY] * n,
        out_shape=[jax.ShapeDtypeStruct((N_CHIPS, g.shape[1] // 2, g.shape[2]), g.dtype) for g in grads],
        scratch_shapes=[pltpu.SemaphoreType.DMA((n,)), pltpu.SemaphoreType.DMA((n,))],
    )(*grads)


def _sum_cores(name, g, other, core):
    _, R, C = g.shape
    H = R // 2
    tr = _row_tile(H, C)
    nb = H // tr

    def body(c_ref, g_ref, o_ref, out_ref):
        out_ref[...] = (g_ref[...].astype(F32) + o_ref[...].astype(F32)).astype(out_ref.dtype)

    return pl.pallas_call(
        body, name=name,
        grid_spec=pltpu.PrefetchScalarGridSpec(
            num_scalar_prefetch=1, grid=(N_CHIPS, nb),
            in_specs=[pl.BlockSpec((None, tr, C), lambda j, i, c_ref: (j, c_ref[0] * nb + i, 0)),
                      pl.BlockSpec((None, tr, C), lambda j, i, c_ref: (j, i, 0))],
            out_specs=pl.BlockSpec((None, tr, C), lambda j, i, c_ref: (j, i, 0))),
        out_shape=jax.ShapeDtypeStruct((N_CHIPS, H, C), BF16),
        compiler_params=_params(("parallel", "parallel")),
    )(core, g, other)


def _sum_chips(name, received, own, full, layer, me_core):
    _, H, C = received.shape
    tr = _row_tile(H, C)
    nb = H // tr

    def body(s_ref, r_ref, own_ref, full_ref, out_ref):
        acc = r_ref[0].astype(F32)
        for k in (1, 2):
            acc = acc + r_ref[k].astype(F32)
        out_ref[...] = acc + own_ref[...].astype(F32)

    return pl.pallas_call(
        body, name=name,
        grid_spec=pltpu.PrefetchScalarGridSpec(
            num_scalar_prefetch=1, grid=(nb,),
            in_specs=[pl.BlockSpec((3, tr, C), lambda i, s_ref: (0, i, 0)),
                      pl.BlockSpec((None, tr, C), lambda i, s_ref: (s_ref[0], i, 0)),
                      _ANY],
            out_specs=pl.BlockSpec((None, tr, C), lambda i, s_ref: (layer, s_ref[1] * nb + i, 0))),
        out_shape=jax.ShapeDtypeStruct(full.shape, full.dtype),
        input_output_aliases={3: 0},
        compiler_params=_params(("parallel",)),
    )(me_core, received, own, full)


def _part_copy(parts, recv, sems, i, k, peers, c):
    px, py = peers[k]
    return pltpu.make_async_remote_copy(
        src_ref=parts[i].at[2 * px + py], dst_ref=recv[i].at[k],
        send_sem=sems[0].at[3 * i + k], recv_sem=sems[1].at[3 * i + k],
        device_id=(px, py, c), device_id_type=MESH)


def _scatter_start(name, parts):
    n = len(parts)

    def body(*refs):
        ins, outs, sems, token = refs[:n], refs[n:2 * n], refs[2 * n:2 * n + 2], refs[2 * n + 2]
        x, y, c = _position()
        peers = _chip_peers(x, y)
        for i in range(n):
            for k in range(3):
                _part_copy(ins, outs, sems, i, k, peers, c).start()
        token[...] = jnp.zeros_like(token)

    res = pl.pallas_call(
        body, name=name, in_specs=[_ANY] * n,
        out_specs=[_ANY] * n + [_SEM, _SEM, pl.BlockSpec(memory_space=pltpu.VMEM)],
        out_shape=[jax.ShapeDtypeStruct((3,) + p.shape[1:], p.dtype) for p in parts]
        + [pltpu.SemaphoreType.DMA((3 * n,)), pltpu.SemaphoreType.DMA((3 * n,)),
           jax.ShapeDtypeStruct((SUBLANES, LANES), F32)],
        compiler_params=_SPLIT,
    )(*parts)
    return list(res[:n]), res[n:n + 2], res[n + 2]


def _scatter_wait(name, parts, recv, sems):
    n = len(parts)

    def body(*refs):
        ins, rs, sem_refs = refs[:n], refs[n:2 * n], refs[2 * n:2 * n + 2]
        x, y, c = _position()
        peers = _chip_peers(x, y)
        for i in range(n):
            for k in range(3):
                _part_copy(ins, rs, sem_refs, i, k, peers, c).wait_recv()
        for i in range(n):
            for k in range(3):
                _part_copy(ins, rs, sem_refs, i, k, peers, c).wait_send()

    return list(pl.pallas_call(
        body, name=name, in_specs=[_ANY] * (2 * n) + [_SEM, _SEM], out_specs=[_ANY] * n,
        out_shape=[jax.ShapeDtypeStruct(r.shape, r.dtype) for r in recv],
        input_output_aliases={n + i: i for i in range(n)}, compiler_params=_SPLIT,
    )(*parts, *recv, *sems))


def _share_d2d(name, full):
    n = len(full)

    def body(*refs):
        ins, outs = refs[:n], refs[n:2 * n]
        send_sems, recv_sems = refs[2 * n:]
        x, y, c = _position()

        def remote(w, core):
            half = _half_rows(core, ins[w].shape[1])
            return pltpu.make_async_remote_copy(
                src_ref=ins[w].at[:, half], dst_ref=outs[w].at[:, half],
                send_sem=send_sems.at[w], recv_sem=recv_sems.at[w],
                device_id=(x, y, 1 - c), device_id_type=MESH)

        for w in range(n):
            remote(w, c).start()
        for w in range(n):
            remote(w, 1 - c).wait_recv()
        for w in range(n):
            remote(w, c).wait_send()

    return pl.pallas_call(
        body, name=name, in_specs=[_ANY] * n, out_specs=[_ANY] * n,
        out_shape=[jax.ShapeDtypeStruct(f.shape, f.dtype) for f in full],
        input_output_aliases={w: w for w in range(n)},
        scratch_shapes=[pltpu.SemaphoreType.DMA((n,)), pltpu.SemaphoreType.DMA((n,))],
    )(*full)


def _gather_all(name, a):
    def body(a_ref, o_ref, send_sems, recv_sems, local_sem):
        x, y, c = _position()
        me = 4 * x + 2 * y + c

        def peer(k):
            return (x ^ ((k >> 2) & 1), y ^ ((k >> 1) & 1), c ^ (k & 1))

        def remote(k, slot):
            return pltpu.make_async_remote_copy(
                src_ref=a_ref, dst_ref=o_ref.at[slot], send_sem=send_sems.at[k - 1], recv_sem=recv_sems.at[k - 1],
                device_id=peer(k), device_id_type=MESH)

        local = pltpu.make_async_copy(a_ref, o_ref.at[me], local_sem)
        local.start()
        for k in range(1, N_DEV):
            remote(k, me).start()
        for k in range(1, N_DEV):
            px, py, pc = peer(k)
            remote(k, 4 * px + 2 * py + pc).wait_recv()
        for k in range(1, N_DEV):
            remote(k, me).wait_send()
        local.wait()

    return pl.pallas_call(
        body, name=name, in_specs=[_ANY], out_specs=_ANY,
        out_shape=jax.ShapeDtypeStruct((N_DEV,) + a.shape, a.dtype),
        scratch_shapes=[pltpu.SemaphoreType.DMA((N_DEV - 1,)), pltpu.SemaphoreType.DMA((N_DEV - 1,)),
                        pltpu.SemaphoreType.DMA],
    )(a)


def _rows2d(a, lead=0):
    return a.reshape(a.shape[:lead] + (-1, a.shape[-1]))


def _row_tile(rows, cols, itemsize=4, target=1 << 20):
    want = max(SUBLANES, target // (cols * itemsize))
    t = min(rows, (want // 16) * 16)
    while t > 16 and rows % t:
        t -= 16
    return t if rows % t == 0 else rows


def _sum_slots(name, r, out_dtype=F32):
    ns = r.shape[0]
    r2 = _rows2d(r, 1)
    _, rows, cols = r2.shape
    tr = _row_tile(rows, cols)

    def body(r_ref, o_ref):
        acc = r_ref[0].astype(F32)
        for s in range(1, ns):
            acc = acc + r_ref[s].astype(F32)
        o_ref[...] = acc.astype(o_ref.dtype)

    out = pl.pallas_call(
        body, name=name, grid=(rows // tr,),
        in_specs=[pl.BlockSpec((ns, tr, cols), lambda i: (0, i, 0))],
        out_specs=pl.BlockSpec((tr, cols), lambda i: (i, 0)),
        out_shape=jax.ShapeDtypeStruct((rows, cols), out_dtype),
        compiler_params=_params(("parallel",)),
    )(r2)
    return out.reshape(r.shape[1:])


def _adamw(name, g_parts, w, m, v):
    shape = w.shape
    ng = len(g_parts)
    args = [_rows2d(a) for a in (*g_parts, w, m, v)]
    rows, cols = args[0].shape
    tr = _row_tile(rows, cols, target=1 << 19)
    c1 = 1.0 - ADAM_B1 ** ADAM_STEP
    c2 = 1.0 - ADAM_B2 ** ADAM_STEP

    def body(*refs):
        g = refs[0][...]
        for r in refs[1:ng]:
            g = g + r[...]
        w_ref, m_ref, v_ref = refs[ng:ng + 3]
        g_out, d_out, m_out, v_out = refs[ng + 3:]
        mn = ADAM_B1 * m_ref[...] + (1.0 - ADAM_B1) * g
        vn = ADAM_B2 * v_ref[...] + (1.0 - ADAM_B2) * (g * g)
        m_hat = mn / c1
        v_hat = vn / c2
        g_out[...] = g
        d_out[...] = -ADAM_LR * (m_hat / (jnp.sqrt(v_hat) + ADAM_EPS) + ADAM_WD * w_ref[...])
        m_out[...] = mn
        v_out[...] = vn

    spec = pl.BlockSpec((tr, cols), lambda i: (i, 0))
    outs = pl.pallas_call(
        body, name=name, grid=(rows // tr,), in_specs=[spec] * (ng + 3), out_specs=[spec] * 4,
        out_shape=[jax.ShapeDtypeStruct((rows, cols), F32)] * 4,
        compiler_params=_params(("parallel",)),
    )(*args)
    return tuple(o.reshape(shape) for o in outs)


_WEIGHTS = ["norm_mix", "norm_ffn", "w_ffn_in", "w_ffn_out", "w_rec_in", "conv_w", "conv_b", "w_lru_gates",
            "b_lru_gates", "lru_param", "w_rec_out", "norm_kv", "w_kvf", "b_forget", "w_q", "w_o", "norm_final"]
_BIG = ["w_ffn_in", "w_ffn_out", "w_rec_in", "w_lru_gates", "w_rec_out", "w_kvf", "w_q", "w_o"]


def _stack3(a):
    return a[None] if a.ndim == 2 else a.reshape(a.shape[0], -1, a.shape[-1])


def _pad_lanes(a, n):
    return jnp.pad(a, ((0, 0),) * (a.ndim - 1) + ((0, n - a.shape[-1]),))


def kernel(x, norm_mix, norm_ffn, w_ffn_in, w_ffn_out, w_rec_in, conv_w, conv_b, w_lru_gates, b_lru_gates, lru_param, w_rec_out, norm_kv, w_kvf, b_forget, w_q, w_o, norm_final, loss_target, m_norm_mix, m_norm_ffn, m_w_ffn_in, m_w_ffn_out, m_w_rec_in, m_conv_w, m_conv_b, m_w_lru_gates, m_b_lru_gates, m_lru_param, m_w_rec_out, m_norm_kv, m_w_kvf, m_b_forget, m_w_q, m_w_o, m_norm_final, v_norm_mix, v_norm_ffn, v_w_ffn_in, v_w_ffn_out, v_w_rec_in, v_conv_w, v_conv_b, v_w_lru_gates, v_b_lru_gates, v_lru_param, v_w_rec_out, v_norm_kv, v_w_kvf, v_b_forget, v_w_q, v_w_o, v_norm_final):
    P = dict(norm_mix=norm_mix, norm_ffn=norm_ffn, w_ffn_in=w_ffn_in, w_ffn_out=w_ffn_out, w_rec_in=w_rec_in,
             conv_w=conv_w, conv_b=conv_b, w_lru_gates=w_lru_gates, b_lru_gates=b_lru_gates, lru_param=lru_param,
             w_rec_out=w_rec_out, norm_kv=norm_kv, w_kvf=w_kvf, b_forget=b_forget, w_q=w_q, w_o=w_o,
             norm_final=norm_final)
    M1 = dict(norm_mix=m_norm_mix, norm_ffn=m_norm_ffn, w_ffn_in=m_w_ffn_in, w_ffn_out=m_w_ffn_out,
              w_rec_in=m_w_rec_in, conv_w=m_conv_w, conv_b=m_conv_b, w_lru_gates=m_w_lru_gates,
              b_lru_gates=m_b_lru_gates, lru_param=m_lru_param, w_rec_out=m_w_rec_out, norm_kv=m_norm_kv,
              w_kvf=m_w_kvf, b_forget=m_b_forget, w_q=m_w_q, w_o=m_w_o, norm_final=m_norm_final)
    M2 = dict(norm_mix=v_norm_mix, norm_ffn=v_norm_ffn, w_ffn_in=v_w_ffn_in, w_ffn_out=v_w_ffn_out,
              w_rec_in=v_w_rec_in, conv_w=v_conv_w, conv_b=v_conv_b, w_lru_gates=v_w_lru_gates,
              b_lru_gates=v_b_lru_gates, lru_param=v_lru_param, w_rec_out=v_w_rec_out, norm_kv=v_norm_kv,
              w_kvf=v_w_kvf, b_forget=v_b_forget, w_q=v_w_q, w_o=v_w_o, norm_final=v_norm_final)

    _, S, D = x.shape
    L = norm_mix.shape[0]
    NA, NBLK, BW, GS = w_lru_gates.shape
    NB = w_q.shape[0]
    C = NBLK * BW
    CS = C // N_CHIPS
    H = b_forget.shape[0]
    assert C == D and H * HEAD_DIM == D and H <= LANES
    chip = 2 * lax.axis_index("x") + lax.axis_index("y")

    small_a = jnp.concatenate([conv_w, conv_b[:, None], lru_param[:, None]], axis=1)
    small_a, b_gates = _gather_smalls("gather_smalls", [small_a, b_lru_gates])
    small_a = small_a.transpose(1, 2, 0, 3).reshape(NA, 6, C)
    b_gates = b_gates.transpose(1, 2, 0, 3).reshape(NA, NBLK, 1, N_CHIPS * GS)
    shards = [_stack3(P[w]).astype(BF16) for w in _BIG]
    core = lax.axis_index("c")
    chip_id = jnp.reshape(chip, (1,)).astype(jnp.int32)
    core_id = jnp.reshape(core, (1,)).astype(jnp.int32)
    me_core = jnp.stack([chip, core]).astype(jnp.int32)

    def layer_items(l):
        if l < NA:
            names, at = ["w_rec_in", "w_lru_gates", "w_rec_out"], l
        else:
            names, at = (["w_kvf"] if l == NA else []) + ["w_q", "w_o"], l - NA
        its = [(_BIG.index(n), 0 if n == "w_kvf" else at) for n in names]
        return its + [(_BIG.index("w_ffn_in"), l), (_BIG.index("w_ffn_out"), l)]

    items = [it for l in range(L) for it in layer_items(l)]
    ids_of = [[items.index(it) for it in layer_items(l)] for l in range(L)]
    bufs = [_place_own(f"place_{_BIG[w]}_{li}", shards[w], li, chip_id) for w, li in items]
    bufs, gather_sems = _gather_start("gather_start", shards, bufs, items)

    def layer_weights(l, after):
        if l >= L:
            return None
        got = _gather_wait(f"gather_wait_{l}", shards, bufs, items, ids_of[l], gather_sems, after)
        got = _gather_d2d(f"gather_d2d_{l}", got)
        B = {_BIG[items[i][0]]: g for i, g in zip(ids_of[l], got)}
        W = dict(norm_mix=norm_mix[l][None], norm_ffn=norm_ffn[l][None], w_ffn_in=B["w_ffn_in"],
                 w_ffn_out=B["w_ffn_out"].reshape(-1, D))
        if l < NA:
            W.update(w_rec_in=B["w_rec_in"],
                     w_gates=B["w_lru_gates"].reshape(N_CHIPS, NBLK, BW, GS).transpose(1, 2, 0, 3).reshape(
                         NBLK, BW, N_CHIPS * GS),
                     b_gates=b_gates[l], w_rec_out=B["w_rec_out"].reshape(C, D),
                     conv_w=small_a[l, :4], conv_b=small_a[l, 4:5], lru_param=small_a[l, 5:6])
        else:
            W.update(w_q=B["w_q"].reshape(D, D), w_o=B["w_o"].reshape(D, D))
            if l == NA:
                w_kvf_full = B["w_kvf"].transpose(1, 0, 2).reshape(D, -1)
                W.update(norm_kv=norm_kv[None], w_kv=w_kvf_full[:, :2 * D],
                         w_f=_pad_lanes(w_kvf_full[:, 2 * D:], LANES), b_f=_pad_lanes(b_forget[None], LANES))
        return W

    G_small = {}
    pending = [None] * L

    def layer_grads(l, G):
        G_small[l] = G
        by_name = dict(
            w_ffn_in=lambda: G["w_ffn_in"], w_ffn_out=lambda: G["w_ffn_out"].reshape(N_CHIPS, -1, D),
            w_rec_in=lambda: G["w_rec_in"],
            w_lru_gates=lambda: G["w_gates"].reshape(NBLK, BW, N_CHIPS, GS).transpose(2, 0, 1, 3).reshape(
                N_CHIPS, NBLK * BW, GS),
            w_rec_out=lambda: G["w_rec_out"].reshape(N_CHIPS, -1, D),
            w_kvf=lambda: jnp.concatenate([G["w_kv"].astype(F32), G["w_f"][:, :H]], axis=1).reshape(
                D, N_CHIPS, -1).transpose(1, 0, 2).astype(BF16),
            w_q=lambda: G["w_q"].reshape(N_CHIPS, -1, D), w_o=lambda: G["w_o"].reshape(N_CHIPS, -1, D))
        grads = [by_name[_BIG[w]]() for w, _ in layer_items(l)]
        others = _reduce_d2d(f"reduce_d2d_{l}", grads)
        parts = [_sum_cores(f"sum_cores_{l}_{_BIG[w]}", g, o, core_id)
                 for (w, _), g, o in zip(layer_items(l), grads, others)]
        recv, sems, token = _scatter_start(f"scatter_start_{l}", parts)
        pending[l] = (parts, recv, sems)
        return token

    loss_row, grad_x, dg_final = _local_step(x.reshape(S, D), loss_target.reshape(S, D), norm_final[None],
                                             layer_weights, layer_grads)

    rows = [*[G_small[l]["norm_mix"] for l in range(L)], *[G_small[l]["norm_ffn"] for l in range(L)],
            G_small[NA]["norm_kv"], dg_final, _pad_lanes(G_small[NA]["b_f"], D), _pad_lanes(loss_row, D)]
    for a in range(NA):
        rows += [G_small[a][n] for n in ("conv_w", "conv_b", "b_gi", "b_gr", "lru_param")]
    packed = jnp.concatenate(rows, axis=0)
    tot = _sum_slots("sum_small", _gather_all("gather_small", packed))
    loss = tot[2 * L + 3, 0]
    g_rep = jnp.concatenate([tot[:2 * L + 2], tot[2 * L + 2:2 * L + 3]], axis=0)
    base = 2 * L + 4
    g_sh = []
    for a in range(NA):
        blk = lax.dynamic_slice_in_dim(tot[base + 8 * a:base + 8 * a + 8], chip * CS, CS, axis=1)
        gi = tot[base + 8 * a + 5].reshape(NBLK, BW)
        gr = tot[base + 8 * a + 6].reshape(NBLK, BW)
        bl = lax.dynamic_slice_in_dim(jnp.concatenate([gi, gr], axis=1), chip * GS, GS, axis=1)
        g_sh += [blk[:5], bl.reshape(-1, CS), blk[7:8]]
    g_sh = jnp.concatenate(g_sh, axis=0)
    nrow = g_sh.shape[0] // NA

    def pack_rep(T):
        return jnp.concatenate([T["norm_mix"], T["norm_ffn"], T["norm_kv"][None], T["norm_final"][None],
                                _pad_lanes(T["b_forget"][None], D)], axis=0)

    def pack_sh(T):
        return jnp.concatenate([jnp.concatenate([T["conv_w"][a], T["conv_b"][a][None],
                                                 T["b_lru_gates"][a].reshape(-1, CS), T["lru_param"][a][None]], axis=0)
                                for a in range(NA)], axis=0)

    rep = _adamw("adamw_replicated", [g_rep], pack_rep(P), pack_rep(M1), pack_rep(M2))
    shd = _adamw("adamw_small_sharded", [g_sh], pack_sh(P), pack_sh(M1), pack_sh(M2))

    def unpack_rep(t):
        return dict(norm_mix=t[:L], norm_ffn=t[L:2 * L], norm_kv=t[2 * L], norm_final=t[2 * L + 1],
                    b_forget=t[2 * L + 2, :H])

    def unpack_sh(t):
        t = t.reshape(NA, nrow, CS)
        return dict(conv_w=t[:, :4], conv_b=t[:, 4], b_lru_gates=t[:, 5:nrow - 1].reshape(NA, NBLK, GS),
                    lru_param=t[:, nrow - 1])

    full = [lax.empty(sh.shape, F32) for sh in shards]
    for l in reversed(range(L)):
        parts, recv, sems = pending[l]
        recv = _scatter_wait(f"scatter_wait_{l}", parts, recv, sems)
        for (w, li), part, r in zip(layer_items(l), parts, recv):
            full[w] = _sum_chips(f"sum_chips_{l}_{_BIG[w]}", r, part, full[w], li, me_core)
    full = _share_d2d("share_d2d", full)
    big = {w: _adamw(f"adamw_{w}", [g.reshape(P[w].shape)], P[w], M1[w], M2[w]) for w, g in zip(_BIG, full)}

    outs = []
    for i in range(4):
        small = {**unpack_rep(rep[i]), **unpack_sh(shd[i])}
        outs.append([big[w][i] if w in big else small[w] for w in _WEIGHTS])
    return (loss, grad_x.reshape(1, S, D), *outs[0], *outs[1], *outs[2], *outs[3])
```

```python
import functools
import math

import jax
import jax.numpy as jnp
from jax import lax
from jax.experimental import pallas as pl
from jax.experimental.pallas import tpu as pltpu

F32 = jnp.float32
BF16 = jnp.bfloat16

EPS = 1e-6
LRU_C = 8.0
HEAD_DIM = 64
LANES = 128
SUBLANES = 8
VMEM_LIMIT = 48 * 1024 * 1024
N_CHIPS = 4
N_DEV = 8

ADAM_LR = 0.001
ADAM_B1 = 0.9
ADAM_B2 = 0.999
ADAM_EPS = 1e-08
ADAM_WD = 0.01
ADAM_STEP = 10

_NN = (((1,), (0,)), ((), ()))
_NT = (((1,), (1,)), ((), ()))
_TN = (((0,), (0,)), ((), ()))
_DN = {"nn": _NN, "nt": _NT, "tn": _TN}
MESH = pl.DeviceIdType.MESH


def _params(sem):
    return pltpu.CompilerParams(dimension_semantics=sem, vmem_limit_bytes=VMEM_LIMIT)


def _tile(n, want):
    if n <= want:
        return n
    t = (want // LANES) * LANES
    while t >= LANES:
        if n % t == 0:
            return t
        t -= LANES
    return n


def _sigmoid(x):
    return 1.0 / (1.0 + jnp.exp(-x))


def _softplus(x):
    return jnp.maximum(x, 0.0) + jnp.log(1.0 + jnp.exp(-jnp.abs(x)))


_GELU_C = math.sqrt(2.0 / math.pi)


def _gelu_and_grad(x):
    inner = _GELU_C * (x + 0.044715 * x * x * x)
    t = jnp.tanh(inner)
    g = 0.5 * x * (1.0 + t)
    dg = 0.5 * (1.0 + t) + 0.5 * x * (1.0 - t * t) * _GELU_C * (1.0 + 3.0 * 0.044715 * x * x)
    return g, dg


def _mm(name, mode, a, b, *, grid, a_spec, b_spec, out_shape, out_dtype, out_spec, nk=1,
        res=None, res_spec=None, bias=None, bias_spec=None, scale=None):
    dn = _DN[mode]
    has_res, has_bias = res is not None, bias is not None
    blk = tuple(d for d in out_spec.block_shape if d is not None)

    def body(*refs):
        a_ref, b_ref = refs[0], refs[1]
        p = 2
        r_ref = refs[p] if has_res else None
        p += int(has_res)
        bias_ref = refs[p] if has_bias else None
        p += int(has_bias)
        o_ref = refs[p]
        part = lax.dot_general(a_ref[...], b_ref[...], dn, preferred_element_type=F32)

        def finish(acc):
            if scale is not None:
                acc = acc * scale
            if has_bias:
                acc = acc + bias_ref[...]
            if has_res:
                acc = r_ref[...] + acc
            o_ref[...] = acc.astype(o_ref.dtype)

        if nk == 1:
            finish(part)
        else:
            acc_ref = refs[p + 1]
            k = pl.program_id(2)

            @pl.when(k == 0)
            def _():
                acc_ref[...] = part

            @pl.when(k > 0)
            def _():
                acc_ref[...] += part

            @pl.when(k == nk - 1)
            def _():
                finish(acc_ref[...])

    ins, specs = [a, b], [a_spec, b_spec]
    if has_res:
        ins.append(res)
        specs.append(res_spec)
    if has_bias:
        ins.append(bias)
        specs.append(bias_spec)
    sem = ("parallel", "parallel") + (("arbitrary",) if len(grid) == 3 else ())
    return pl.pallas_call(
        body, name=name, grid=grid, in_specs=specs, out_specs=out_spec,
        out_shape=jax.ShapeDtypeStruct(out_shape, out_dtype),
        scratch_shapes=[pltpu.VMEM(blk, F32)] if nk > 1 else [],
        compiler_params=_params(sem),
    )(*ins)


def _mm_nn(name, a, b, *, b_lead=(), out_dtype, tm=512, tn=512, res=None, bias=None, scale=None):
    M, K = a.shape
    N = b.shape[-1]
    tm, tn = _tile(M, tm), _tile(N, tn)
    nl = len(b_lead)
    return _mm(
        name, "nn", a, b, grid=(M // tm, N // tn),
        a_spec=pl.BlockSpec((tm, K), lambda i, j: (i, 0)),
        b_spec=pl.BlockSpec((None,) * nl + (K, tn), lambda i, j: tuple(b_lead) + (0, j)),
        out_shape=(M, N), out_dtype=out_dtype, out_spec=pl.BlockSpec((tm, tn), lambda i, j: (i, j)),
        res=res, res_spec=pl.BlockSpec((tm, tn), lambda i, j: (i, j)),
        bias=bias, bias_spec=pl.BlockSpec((1, tn), lambda i, j: (0, j)), scale=scale)


def _mm_nt(name, a, b, *, b_lead=(), out_dtype, tm=512, tn=512, tk=2048):
    M, K = a.shape
    N = b.shape[-2]
    tm, tn, tk = _tile(M, tm), _tile(N, tn), _tile(K, tk)
    nk = K // tk
    nl = len(b_lead)
    return _mm(
        name, "nt", a, b, grid=(M // tm, N // tn, nk), nk=nk,
        a_spec=pl.BlockSpec((tm, tk), lambda i, j, k: (i, k)),
        b_spec=pl.BlockSpec((None,) * nl + (tn, tk), lambda i, j, k: tuple(b_lead) + (j, k)),
        out_shape=(M, N), out_dtype=out_dtype, out_spec=pl.BlockSpec((tm, tn), lambda i, j, k: (i, j)))


def _mm_tn(name, a, b, *, out_dtype, tm=512, tn=512):
    S, M = a.shape
    N = b.shape[1]
    tm, tn = _tile(M, tm), _tile(N, tn)
    return _mm(
        name, "tn", a, b, grid=(M // tm, N // tn),
        a_spec=pl.BlockSpec((S, tm), lambda i, j: (0, i)),
        b_spec=pl.BlockSpec((S, tn), lambda i, j: (0, j)),
        out_shape=(M, N), out_dtype=out_dtype, out_spec=pl.BlockSpec((tm, tn), lambda i, j: (i, j)))


def _rmsnorm_fwd(name, h, g, tr=256):
    S, D = h.shape
    tr = _tile(S, tr)

    def body(h_ref, g_ref, o_ref):
        x = h_ref[...]
        r = lax.rsqrt(jnp.mean(x * x, axis=-1, keepdims=True) + EPS)
        o_ref[...] = (x * r * g_ref[...]).astype(o_ref.dtype)

    return pl.pallas_call(
        body, name=name, grid=(S // tr,),
        in_specs=[pl.BlockSpec((tr, D), lambda i: (i, 0)), pl.BlockSpec((1, D), lambda i: (0, 0))],
        out_specs=pl.BlockSpec((tr, D), lambda i: (i, 0)),
        out_shape=jax.ShapeDtypeStruct((S, D), BF16),
        compiler_params=_params(("parallel",)),
    )(h, g)


def _rmsnorm_bwd(name, dxn, h, g, dh_in, tr=256):
    S, D = h.shape
    tr = _tile(S, tr)

    def body(dxn_ref, h_ref, g_ref, dh_ref, o_ref, ob_ref, dg_ref):
        i = pl.program_id(0)
        x = h_ref[...]
        dy = dxn_ref[...].astype(F32)
        r = lax.rsqrt(jnp.mean(x * x, axis=-1, keepdims=True) + EPS)
        xr = x * r
        dyg = dy * g_ref[...]
        dx = r * dyg - xr * (r * jnp.mean(dyg * xr, axis=-1, keepdims=True))
        out = dh_ref[...] + dx
        o_ref[...] = out
        ob_ref[...] = out.astype(BF16)
        part = jnp.sum(dy * xr, axis=0, keepdims=True)

        @pl.when(i == 0)
        def _():
            dg_ref[...] = part

        @pl.when(i > 0)
        def _():
            dg_ref[...] += part

    row = pl.BlockSpec((tr, D), lambda i: (i, 0))
    vec = pl.BlockSpec((1, D), lambda i: (0, 0))
    return pl.pallas_call(
        body, name=name, grid=(S // tr,),
        in_specs=[row, row, vec, row], out_specs=[row, row, vec],
        out_shape=[jax.ShapeDtypeStruct((S, D), F32), jax.ShapeDtypeStruct((S, D), BF16),
                   jax.ShapeDtypeStruct((1, D), F32)],
        compiler_params=_params(("arbitrary",)),
    )(dxn, h, g, dh_in)


def _loss_head(name, h, target, g, tr=256):
    S, D = h.shape
    tr = _tile(S, tr)

    def body(h_ref, t_ref, g_ref, o_ref, ob_ref, dg_ref, loss_ref):
        i = pl.program_id(0)
        x = h_ref[...]
        gg = g_ref[...]
        r = lax.rsqrt(jnp.mean(x * x, axis=-1, keepdims=True) + EPS)
        xr = x * r
        err = xr * gg - t_ref[...]
        lpart = 0.5 * jnp.sum(jnp.mean(err * err, axis=-1, keepdims=True), axis=0, keepdims=True)
        dy = err * (1.0 / D)
        dyg = dy * gg
        dx = r * dyg - xr * (r * jnp.mean(dyg * xr, axis=-1, keepdims=True))
        o_ref[...] = dx
        ob_ref[...] = dx.astype(BF16)
        part = jnp.sum(dy * xr, axis=0, keepdims=True)
        lrow = jnp.broadcast_to(lpart, (1, LANES))

        @pl.when(i == 0)
        def _():
            dg_ref[...] = part
            loss_ref[...] = lrow

        @pl.when(i > 0)
        def _():
            dg_ref[...] += part
            loss_ref[...] += lrow

    row = pl.BlockSpec((tr, D), lambda i: (i, 0))
    vec = pl.BlockSpec((1, D), lambda i: (0, 0))
    return pl.pallas_call(
        body, name=name, grid=(S // tr,),
        in_specs=[row, row, vec], out_specs=[row, row, vec, pl.BlockSpec((1, LANES), lambda i: (0, 0))],
        out_shape=[jax.ShapeDtypeStruct((S, D), F32), jax.ShapeDtypeStruct((S, D), BF16),
                   jax.ShapeDtypeStruct((1, D), F32), jax.ShapeDtypeStruct((1, LANES), F32)],
        compiler_params=_params(("arbitrary",)),
    )(h, target, g)


def _swiglu_fwd(name, z3, tr=256, tc=1408):
    _, S, F = z3.shape
    tr, tc = _tile(S, tr), _tile(F, tc)

    def body(z_ref, a_ref):
        zg = z_ref[0].astype(F32)
        zu = z_ref[1].astype(F32)
        a_ref[...] = (zg * _sigmoid(zg) * zu).astype(a_ref.dtype)

    return pl.pallas_call(
        body, name=name, grid=(S // tr, F // tc),
        in_specs=[pl.BlockSpec((2, tr, tc), lambda i, j: (0, i, j))],
        out_specs=pl.BlockSpec((tr, tc), lambda i, j: (i, j)),
        out_shape=jax.ShapeDtypeStruct((S, F), BF16),
        compiler_params=_params(("parallel", "parallel")),
    )(z3)


def _swiglu_bwd(name, da, z3, tr=256, tc=1408):
    _, S, F = z3.shape
    tr, tc = _tile(S, tr), _tile(F, tc)

    def body(da_ref, z_ref, dz_ref):
        zg = z_ref[0].astype(F32)
        zu = z_ref[1].astype(F32)
        d = da_ref[...].astype(F32)
        sg = _sigmoid(zg)
        silu = zg * sg
        dz_ref[0] = (d * zu * (sg * (1.0 + zg * (1.0 - sg)))).astype(dz_ref.dtype)
        dz_ref[1] = (d * silu).astype(dz_ref.dtype)

    return pl.pallas_call(
        body, name=name, grid=(S // tr, F // tc),
        in_specs=[pl.BlockSpec((tr, tc), lambda i, j: (i, j)),
                  pl.BlockSpec((2, tr, tc), lambda i, j: (0, i, j))],
        out_specs=pl.BlockSpec((2, tr, tc), lambda i, j: (0, i, j)),
        out_shape=jax.ShapeDtypeStruct((2, S, F), BF16),
        compiler_params=_params(("parallel", "parallel")),
    )(da, z3)


SCAN_ROWS = 64


def _group_scan(A, B, reverse):
    n = A.shape[0]
    sub = lax.broadcasted_iota(jnp.int32, A.shape, 0) % SUBLANES
    for d in (1, 2, 4):
        if reverse:
            A_sh, B_sh = pltpu.roll(A, n - d, 0), pltpu.roll(B, n - d, 0)
            keep = sub < SUBLANES - d
        else:
            A_sh, B_sh = pltpu.roll(A, d, 0), pltpu.roll(B, d, 0)
            keep = sub >= d
        B = jnp.where(keep, A * B_sh + B, B)
        A = jnp.where(keep, A * A_sh, A)
    return A, B


def _block_scan(a, u, carry, reverse):
    A, B = _group_scan(a, u, reverse)
    ng = a.shape[0] // SUBLANES
    out = [None] * ng
    order = range(ng - 1, -1, -1) if reverse else range(ng)
    for gi in order:
        sl = slice(gi * SUBLANES, (gi + 1) * SUBLANES)
        hg = A[sl] * carry + B[sl]
        out[gi] = hg
        carry = hg[0:1] if reverse else hg[SUBLANES - 1:SUBLANES]
    return jnp.concatenate(out, axis=0), carry


def _lru_gates(rc, gip, grp, sp):
    gi = _sigmoid(gip)
    gr = _sigmoid(grp)
    la = -LRU_C * gr * sp
    a = jnp.exp(la)
    om = -jnp.tanh(la) * (a * a + 1.0)
    mult = jnp.sqrt(om)
    return gi, gr, a, mult


def _lru_fwd(name, proj, rc, gip, grp, lru_p, tc=256):
    S, C = rc.shape
    tc = _tile(C, tc)
    nb = S // SCAN_ROWS

    def body(gb_ref, rc_ref, gi_ref, gr_ref, l_ref, h_ref, m_ref):
        sp = _softplus(-l_ref[...])

        def step(b, carry):
            rows = pl.ds(pl.multiple_of(b * SCAN_ROWS, SCAN_ROWS), SCAN_ROWS)
            rcb = rc_ref[rows, :]
            gi, _, a, mult = _lru_gates(rcb, gi_ref[rows, :], gr_ref[rows, :], sp)
            h, carry = _block_scan(a, rcb * gi * mult, carry, False)
            h_ref[rows, :] = h
            gel, _ = _gelu_and_grad(gb_ref[rows, :])
            m_ref[rows, :] = (gel * h).astype(m_ref.dtype)
            return carry

        lax.fori_loop(0, nb, step, jnp.zeros((1, tc), F32))

    col = pl.BlockSpec((S, tc), lambda j: (0, j))
    return pl.pallas_call(
        body, name=name, grid=(C // tc,),
        in_specs=[col, col, col, col, pl.BlockSpec((1, tc), lambda j: (0, j))],
        out_specs=[col, col],
        out_shape=[jax.ShapeDtypeStruct((S, C), F32), jax.ShapeDtypeStruct((S, C), BF16)],
        compiler_params=_params(("parallel",)),
    )(proj, rc, gip, grp, lru_p)


def _lru_bwd(name, dm, proj, hrec, rc, gip, grp, lru_p, tc=256):
    S, C = rc.shape
    tc = _tile(C, tc)
    nb = S // SCAN_ROWS
    R = SCAN_ROWS

    def body(dm_ref, gb_ref, h_ref, rc_ref, gi_ref, gr_ref, l_ref,
             dgb_ref, dgi_ref, dgr_ref, drc_ref, dbi_ref, dbr_ref, dl_ref):
        lp = l_ref[...]
        sp = _softplus(-lp)
        row = lax.broadcasted_iota(jnp.int32, (R, tc), 0)
        zero = jnp.zeros((1, tc), F32)

        def step(t, carry):
            mu_in, s_i, s_r, s_sp = carry
            b = nb - 1 - t
            r0 = pl.multiple_of(b * R, R)
            rows = pl.ds(r0, R)
            rcb = rc_ref[rows, :]
            gi, gr, a, mult = _lru_gates(rcb, gi_ref[rows, :], gr_ref[rows, :], sp)
            gel, dgel = _gelu_and_grad(gb_ref[rows, :])
            dmb = dm_ref[rows, :]
            h = h_ref[rows, :]
            dgb_ref[rows, :] = (dmb * h * dgel).astype(dgb_ref.dtype)
            dh = dmb * gel
            mu, mu_out = _block_scan(a, a * dh, mu_in, True)
            mu_next = jnp.where(row == R - 1, mu_in, pltpu.roll(mu, R - 1, 0))
            lam = dh + mu_next
            p0 = pl.multiple_of(jnp.maximum(r0 - SUBLANES, 0), SUBLANES)
            prev = h_ref[pl.ds(p0, SUBLANES), :][SUBLANES - 1:SUBLANES]
            prev = jnp.where(b > 0, prev, 0.0)
            h_prev = jnp.where(row == 0, prev, pltpu.roll(h, 1, 0))
            da = lam * h_prev
            d_mult = lam * rcb * gi
            d_la = da * a - d_mult * (a * a) / mult
            d_grp = d_la * (-LRU_C * sp) * gr * (1.0 - gr)
            d_gip = lam * rcb * mult * gi * (1.0 - gi)
            dgr_ref[rows, :] = d_grp.astype(dgr_ref.dtype)
            dgi_ref[rows, :] = d_gip.astype(dgi_ref.dtype)
            drc_ref[rows, :] = lam * gi * mult
            s_i = s_i + jnp.sum(d_gip, axis=0, keepdims=True)
            s_r = s_r + jnp.sum(d_grp, axis=0, keepdims=True)
            s_sp = s_sp + jnp.sum(d_la * gr, axis=0, keepdims=True)
            return mu_out, s_i, s_r, s_sp

        _, s_i, s_r, s_sp = lax.fori_loop(0, nb, step, (zero, zero, zero, zero))
        dbi_ref[...] = s_i
        dbr_ref[...] = s_r
        dl_ref[...] = (-LRU_C * s_sp) * (-_sigmoid(-lp))

    col = pl.BlockSpec((S, tc), lambda j: (0, j))
    vec = pl.BlockSpec((1, tc), lambda j: (0, j))
    return pl.pallas_call(
        body, name=name, grid=(C // tc,),
        in_specs=[col, col, col, col, col, col, vec],
        out_specs=[col, col, col, col, vec, vec, vec],
        out_shape=[jax.ShapeDtypeStruct((S, C), BF16), jax.ShapeDtypeStruct((S, C), BF16),
                   jax.ShapeDtypeStruct((S, C), BF16), jax.ShapeDtypeStruct((S, C), F32),
                   jax.ShapeDtypeStruct((1, C), F32), jax.ShapeDtypeStruct((1, C), F32),
                   jax.ShapeDtypeStruct((1, C), F32)],
        compiler_params=_params(("parallel",)),
    )(dm, proj, hrec, rc, gip, grp, lru_p)


def _cumsum_rows(name, u, reverse):
    S, C = u.shape
    nb = S // SCAN_ROWS

    def body(u_ref, o_ref):
        def step(t, carry):
            b = nb - 1 - t if reverse else t
            rows = pl.ds(pl.multiple_of(b * SCAN_ROWS, SCAN_ROWS), SCAN_ROWS)
            ub = u_ref[rows, :]
            h, carry = _block_scan(jnp.ones_like(ub), ub, carry, reverse)
            o_ref[rows, :] = h
            return carry

        lax.fori_loop(0, nb, step, jnp.zeros((1, C), F32))

    spec = pl.BlockSpec((S, C), lambda i: (0, 0))
    return pl.pallas_call(
        body, name=name, grid=(1,), in_specs=[spec], out_specs=spec,
        out_shape=jax.ShapeDtypeStruct((S, C), F32),
        compiler_params=_params(("arbitrary",)),
    )(u)


def _shift_down(x, k):
    row = lax.broadcasted_iota(jnp.int32, x.shape, 0)
    return jnp.where(row >= k, pltpu.roll(x, k, 0), 0.0)


def _shift_up(x, k):
    n = x.shape[0]
    row = lax.broadcasted_iota(jnp.int32, x.shape, 0)
    return jnp.where(row < n - k, pltpu.roll(x, n - k, 0), 0.0)


def _conv_fwd(name, proj, w, b, tc=256):
    S, C2 = proj.shape
    C = C2 // 2
    tc = _tile(C, tc)
    off = C // tc

    def body(x_ref, w_ref, b_ref, o_ref, ob_ref):
        x = x_ref[...]
        out = b_ref[...] + w_ref[3:4, :] * x
        for k in (1, 2, 3):
            out = out + w_ref[3 - k:4 - k, :] * _shift_down(x, k)
        o_ref[...] = out
        ob_ref[...] = out.astype(BF16)

    col = pl.BlockSpec((S, tc), lambda j: (0, j))
    return pl.pallas_call(
        body, name=name, grid=(C // tc,),
        in_specs=[pl.BlockSpec((S, tc), lambda j: (0, off + j)),
                  pl.BlockSpec((4, tc), lambda j: (0, j)), pl.BlockSpec((1, tc), lambda j: (0, j))],
        out_specs=[col, col],
        out_shape=[jax.ShapeDtypeStruct((S, C), F32), jax.ShapeDtypeStruct((S, C), BF16)],
        compiler_params=_params(("parallel",)),
    )(proj, w, b)


def _conv_bwd(name, drc, proj, w, tc=256):
    S, C = drc.shape
    tc = _tile(C, tc)
    off = C // tc

    def body(y_ref, x_ref, w_ref, dx_ref, dw_ref, db_ref):
        y = y_ref[...]
        x = x_ref[...]
        dx = w_ref[3:4, :] * y
        dw_ref[3:4, :] = jnp.sum(y * x, axis=0, keepdims=True)
        for k in (1, 2, 3):
            dx = dx + w_ref[3 - k:4 - k, :] * _shift_up(y, k)
            dw_ref[3 - k:4 - k, :] = jnp.sum(y * _shift_down(x, k), axis=0, keepdims=True)
        dx_ref[...] = dx.astype(dx_ref.dtype)
        db_ref[...] = jnp.sum(y, axis=0, keepdims=True)

    col = pl.BlockSpec((S, tc), lambda j: (0, j))
    return pl.pallas_call(
        body, name=name, grid=(C // tc,),
        in_specs=[col, pl.BlockSpec((S, tc), lambda j: (0, off + j)), pl.BlockSpec((4, tc), lambda j: (0, j))],
        out_specs=[col, pl.BlockSpec((4, tc), lambda j: (0, j)), pl.BlockSpec((1, tc), lambda j: (0, j))],
        out_shape=[jax.ShapeDtypeStruct((S, C), BF16), jax.ShapeDtypeStruct((4, C), F32),
                   jax.ShapeDtypeStruct((1, C), F32)],
        compiler_params=_params(("parallel",)),
    )(drc, proj, w)


def _gates_fwd(name, rcb, wg, bg):
    S, C = rcb.shape
    nblk, bw, _ = wg.shape

    def body(x_ref, w_ref, b_ref, gi_ref, gr_ref):
        g = jnp.dot(x_ref[...], w_ref[...], preferred_element_type=F32) + b_ref[...]
        gi_ref[...] = g[:, :bw]
        gr_ref[...] = g[:, bw:]

    col = pl.BlockSpec((S, bw), lambda n: (0, n))
    return pl.pallas_call(
        body, name=name, grid=(nblk,),
        in_specs=[col, pl.BlockSpec((None, bw, 2 * bw), lambda n: (n, 0, 0)),
                  pl.BlockSpec((None, 1, 2 * bw), lambda n: (n, 0, 0))],
        out_specs=[col, col],
        out_shape=[jax.ShapeDtypeStruct((S, C), F32), jax.ShapeDtypeStruct((S, C), F32)],
        compiler_params=_params(("parallel",)),
    )(rcb, wg, bg)


def _gates_bwd(name, dgi, dgr, rcb, wg, drc1):
    S, C = rcb.shape
    nblk, bw, _ = wg.shape

    def body(dgi_ref, dgr_ref, x_ref, w_ref, d1_ref, drc_ref, dw_ref):
        w = w_ref[...]
        x = x_ref[...]
        di, dr = dgi_ref[...], dgr_ref[...]
        drc_ref[...] = (d1_ref[...]
                        + lax.dot_general(di, w[:, :bw], _NT, preferred_element_type=F32)
                        + lax.dot_general(dr, w[:, bw:], _NT, preferred_element_type=F32))
        dw_ref[:, :bw] = lax.dot_general(x, di, _TN, preferred_element_type=F32).astype(dw_ref.dtype)
        dw_ref[:, bw:] = lax.dot_general(x, dr, _TN, preferred_element_type=F32).astype(dw_ref.dtype)

    col = pl.BlockSpec((S, bw), lambda n: (0, n))
    wspec = pl.BlockSpec((None, bw, 2 * bw), lambda n: (n, 0, 0))
    return pl.pallas_call(
        body, name=name, grid=(nblk,),
        in_specs=[col, col, col, wspec, col], out_specs=[col, wspec],
        out_shape=[jax.ShapeDtypeStruct((S, C), F32), jax.ShapeDtypeStruct((nblk, bw, 2 * bw), BF16)],
        compiler_params=_params(("parallel",)),
    )(dgi, dgr, rcb, wg, drc1)


def _att_tile(S):
    return 256 if S % 256 == 0 else 128


def _causal(T):
    r = lax.broadcasted_iota(jnp.int32, (T, T), 0)
    c = lax.broadcasted_iota(jnp.int32, (T, T), 1)
    return r >= c


def _attn_fwd(name, q, kv, negc3):
    S, D = q.shape
    HP = D // LANES
    T = _att_tile(S)
    nq = S // T

    def body(q_ref, k_ref, v_ref, nc_ref, o_ref, of_ref, lse_ref):
        is0 = lax.broadcasted_iota(jnp.int32, (T, LANES), 1) < HEAD_DIM
        tri = _causal(T)

        def q_step(qi, _):
            rows = pl.ds(pl.multiple_of(qi * T, T), T)
            qf = q_ref[rows, :].astype(F32)
            outs, lses = [], []
            for hh in range(2):
                qm = jnp.where(is0 if hh == 0 else jnp.logical_not(is0), qf, 0.0).astype(BF16)

                def tile(kj, carry, masked):
                    m, l, acc = carry
                    ks = pl.ds(pl.multiple_of(kj * T, T), T)
                    s = lax.dot_general(qm, k_ref[ks, :], _NT, preferred_element_type=F32)
                    s = s + nc_ref[hh:hh + 1, ks]
                    if masked:
                        s = jnp.where(tri, s, -jnp.inf)
                    m_new = jnp.maximum(m, jnp.max(s, axis=1, keepdims=True))
                    alpha = jnp.exp(m - m_new)
                    p = jnp.exp(s - m_new)
                    l = alpha * l + jnp.sum(p, axis=1, keepdims=True)
                    acc = alpha * acc + jnp.dot(p.astype(BF16), v_ref[ks, :], preferred_element_type=F32)
                    return m_new, l, acc

                init = (jnp.full((T, 1), -jnp.inf, F32), jnp.zeros((T, 1), F32), jnp.zeros((T, LANES), F32))
                carry = lax.fori_loop(0, qi, lambda kj, c: tile(kj, c, False), init)
                m, l, acc = tile(qi, carry, True)
                outs.append(acc / l)
                lses.append(jnp.broadcast_to(m + jnp.log(l), (T, LANES)))
            out = jnp.where(is0, outs[0], outs[1])
            o_ref[rows, :] = out.astype(o_ref.dtype)
            of_ref[rows, :] = out
            lse_ref[rows, :] = jnp.where(is0, lses[0], lses[1])
            return 0

        lax.fori_loop(0, nq, q_step, 0)

    return pl.pallas_call(
        body, name=name, grid=(HP,),
        in_specs=[pl.BlockSpec((S, LANES), lambda p: (0, p)),
                  pl.BlockSpec((S, LANES), lambda p: (0, p)),
                  pl.BlockSpec((S, LANES), lambda p: (0, HP + p)),
                  pl.BlockSpec((None, 2, S), lambda p: (p, 0, 0))],
        out_specs=[pl.BlockSpec((S, LANES), lambda p: (0, p))] * 3,
        out_shape=[jax.ShapeDtypeStruct((S, D), BF16), jax.ShapeDtypeStruct((S, D), F32),
                   jax.ShapeDtypeStruct((S, D), F32)],
        compiler_params=_params(("parallel",)),
    )(q, kv, kv, negc3)


def _attn_bwd(name, q, kv, negc3, o, do, lse):
    S, D = q.shape
    HP = D // LANES
    T = _att_tile(S)
    nq = S // T
    rep = T // LANES
    scale = HEAD_DIM ** -0.5

    def body(q_ref, k_ref, v_ref, nc_ref, o_ref, do_ref, lse_ref,
             dq_ref, dk_ref, dv_ref, dc_ref, dr_ref, dq_acc, lse_rep, dl_rep, dr_rep):
        is0 = lax.broadcasted_iota(jnp.int32, (T, LANES), 1) < HEAD_DIM
        tri = _causal(T)

        def prologue(qi, _):
            rows = pl.ds(pl.multiple_of(qi * T, T), T)
            prod = do_ref[rows, :].astype(F32) * o_ref[rows, :]
            lse_b = lse_ref[rows, :]
            for hh in range(2):
                msk = is0 if hh == 0 else jnp.logical_not(is0)
                dl = jnp.sum(jnp.where(msk, prod, 0.0), axis=1, keepdims=True)
                ls = jnp.max(jnp.where(msk, lse_b, -jnp.inf), axis=1, keepdims=True)
                dl_rep[hh, rows, :] = jnp.broadcast_to(dl, (T, LANES))
                lse_rep[hh, rows, :] = jnp.broadcast_to(ls, (T, LANES))
                dr_rep[hh, rows, :] = jnp.zeros((T, LANES), F32)
            dq_acc[rows, :] = jnp.zeros((T, LANES), F32)
            return 0

        lax.fori_loop(0, nq, prologue, 0)

        def kv_step(kj, _):
            ks = pl.ds(pl.multiple_of(kj * T, T), T)
            kf = k_ref[ks, :].astype(F32)
            vf = v_ref[ks, :].astype(F32)
            dks, dvs = [], []
            for hh in range(2):
                msk = is0 if hh == 0 else jnp.logical_not(is0)
                km = jnp.where(msk, kf, 0.0).astype(BF16)
                vm = jnp.where(msk, vf, 0.0).astype(BF16)
                ncr = nc_ref[hh:hh + 1, ks]

                def tile(qi, carry, masked):
                    dk_a, dv_a, dc_a = carry
                    rows = pl.ds(pl.multiple_of(qi * T, T), T)
                    qb = q_ref[rows, :]
                    dob = do_ref[rows, :]
                    s = lax.dot_general(qb, km, _NT, preferred_element_type=F32) + ncr
                    lse_t = jnp.tile(lse_rep[hh, rows, :], (1, rep))
                    dl_t = jnp.tile(dl_rep[hh, rows, :], (1, rep))
                    p = jnp.exp(s - lse_t)
                    if masked:
                        p = jnp.where(tri, p, 0.0)
                    dp = lax.dot_general(dob, vm, _NT, preferred_element_type=F32)
                    ds = p * (dp - dl_t)
                    pb, dsb = p.astype(BF16), ds.astype(BF16)
                    dv_a = dv_a + lax.dot_general(pb, dob, _TN, preferred_element_type=F32)
                    dk_a = dk_a + lax.dot_general(dsb, qb, _TN, preferred_element_type=F32)
                    dq_acc[rows, :] += jnp.dot(dsb, km, preferred_element_type=F32)
                    dc_a = dc_a + jnp.sum(ds, axis=0, keepdims=True)
                    dr_rep[hh, rows, :] += jnp.broadcast_to(jnp.sum(ds, axis=1, keepdims=True), (T, LANES))
                    return dk_a, dv_a, dc_a

                init = (jnp.zeros((T, LANES), F32), jnp.zeros((T, LANES), F32), jnp.zeros((1, T), F32))
                carry = tile(kj, init, True)
                dk_a, dv_a, dc_a = lax.fori_loop(kj + 1, nq, lambda qi, c: tile(qi, c, False), carry)
                dks.append(dk_a)
                dvs.append(dv_a)
                dc_ref[hh:hh + 1, ks] = -dc_a
            dk_ref[ks, :] = jnp.where(is0, dks[0], dks[1])
            dv_ref[ks, :] = jnp.where(is0, dvs[0], dvs[1])
            return 0

        lax.fori_loop(0, nq, kv_step, 0)
        dq_ref[...] = (dq_acc[...] * scale).astype(dq_ref.dtype)
        first = lax.broadcasted_iota(jnp.int32, (S, LANES), 1) < HEAD_DIM
        dr_ref[...] = jnp.where(first, dr_rep[0], dr_rep[1])

    blk = lambda off: pl.BlockSpec((S, LANES), lambda p: (0, off + p))
    nc_spec = pl.BlockSpec((None, 2, S), lambda p: (p, 0, 0))
    return pl.pallas_call(
        body, name=name, grid=(HP,),
        in_specs=[blk(0), blk(0), blk(HP), nc_spec, blk(0), blk(0), blk(0)],
        out_specs=[blk(0), blk(0), blk(0), nc_spec, blk(0)],
        out_shape=[jax.ShapeDtypeStruct((S, D), BF16), jax.ShapeDtypeStruct((S, D), F32),
                   jax.ShapeDtypeStruct((S, D), F32), jax.ShapeDtypeStruct((HP, 2, S), F32),
                   jax.ShapeDtypeStruct((S, D), F32)],
        scratch_shapes=[pltpu.VMEM((S, LANES), F32), pltpu.VMEM((2, S, LANES), F32),
                        pltpu.VMEM((2, S, LANES), F32), pltpu.VMEM((2, S, LANES), F32)],
        compiler_params=_params(("parallel",)),
    )(q, kv, kv, negc3, o, do, lse)


def _logsig_fwd(name, f):
    S, C = f.shape

    def body(f_ref, o_ref):
        o_ref[...] = -_softplus(-f_ref[...])

    spec = pl.BlockSpec((S, C), lambda i: (0, 0))
    return pl.pallas_call(body, name=name, grid=(1,), in_specs=[spec], out_specs=spec,
                          out_shape=jax.ShapeDtypeStruct((S, C), F32),
                          compiler_params=_params(("arbitrary",)))(f)


def _logsig_bwd(name, dls, f):
    S, C = f.shape

    def body(d_ref, f_ref, o_ref, s_ref):
        df = d_ref[...] * _sigmoid(-f_ref[...])
        o_ref[...] = df.astype(o_ref.dtype)
        s_ref[...] = jnp.sum(df, axis=0, keepdims=True)

    spec = pl.BlockSpec((S, C), lambda i: (0, 0))
    return pl.pallas_call(body, name=name, grid=(1,), in_specs=[spec, spec],
                          out_specs=[spec, pl.BlockSpec((1, C), lambda i: (0, 0))],
                          out_shape=[jax.ShapeDtypeStruct((S, C), BF16), jax.ShapeDtypeStruct((1, C), F32)],
                          compiler_params=_params(("arbitrary",)))(dls, f)


def _add_cast(name, parts, out_dtype, tr=256):
    S, C = parts[0].shape
    tr = _tile(S, tr)
    n = len(parts)

    def body(*refs):
        acc = refs[0][...].astype(F32)
        for r in refs[1:n]:
            acc = acc + r[...].astype(F32)
        refs[n][...] = acc.astype(out_dtype)

    spec = pl.BlockSpec((tr, C), lambda i: (i, 0))
    return pl.pallas_call(body, name=name, grid=(S // tr,), in_specs=[spec] * n, out_specs=spec,
                          out_shape=jax.ShapeDtypeStruct((S, C), out_dtype),
                          compiler_params=_params(("parallel",)))(*parts)


def _local_step(x, target, norm_final, layer_weights, layer_grads):
    S, D = x.shape
    HP = D // LANES
    scale = HEAD_DIM ** -0.5
    tm = _tile(S, 512)
    td = _tile(D, 512)
    saved = []
    h = x
    l = 0
    kv = negc3 = f_pre = hn_kv = h_kv = None
    while True:
        W = layer_weights(l, "mix", h)
        if W is None:
            break
        recurrent = "w_rec_in" in W
        xn = _rmsnorm_fwd(f"mix_norm_{l}", h, W["norm_mix"])
        if recurrent:
            CH = W["w_rec_in"].shape[-1]
            C = 2 * CH
            proj = _mm(f"rec_in_{l}", "nn", xn, W["w_rec_in"], grid=(S // tm, N_CHIPS),
                       a_spec=pl.BlockSpec((tm, D), lambda i, j: (i, 0)),
                       b_spec=pl.BlockSpec((None, D, CH), lambda i, j: (j, 0, 0)),
                       out_shape=(S, 2 * C), out_dtype=F32,
                       out_spec=pl.BlockSpec((tm, CH), lambda i, j: (i, j)))
            rc, rcb = _conv_fwd(f"conv_{l}", proj, W["conv_w"], W["conv_b"])
            gip, grp = _gates_fwd(f"gates_{l}", rcb, W["w_gates"], W["b_gates"])
            hrec, m = _lru_fwd(f"lru_{l}", proj, rc, gip, grp, W["lru_param"])
            h_mid = _mm_nn(f"rec_out_{l}", m, W["w_rec_out"], out_dtype=F32, res=h, tn=D)
            mix_saved = (xn, proj, rc, rcb, gip, grp, hrec, m)
        else:
            if "w_kv" in W:
                h_kv = h
                hn_kv = _rmsnorm_fwd("kv_norm", h, W["norm_kv"])
                kv = _mm_nn("kv_proj", hn_kv, W["w_kv"], out_dtype=BF16)
                f_pre = _mm_nn("f_proj", hn_kv, W["w_f"], out_dtype=F32, bias=W["b_f"])
                c = _cumsum_rows("c_cumsum", _logsig_fwd("logsig", f_pre), False)
                negc3 = (-c[:, :2 * HP]).T.reshape(HP, 2, S)
            q = _mm_nn(f"q_proj_{l}", xn, W["w_q"], out_dtype=BF16, scale=scale)
            o, of, lse = _attn_fwd(f"attn_fwd_{l}", q, kv, negc3)
            h_mid = _mm_nn(f"o_proj_{l}", o, W["w_o"], out_dtype=F32, res=h, tn=D)
            mix_saved = (xn, q, o, of, lse)
        W = {**W, **layer_weights(l, "ffn", h_mid)}
        FH = W["w_ffn_in"].shape[-1]
        F = 2 * FH
        hn = _rmsnorm_fwd(f"ffn_norm_{l}", h_mid, W["norm_ffn"])
        z3 = _mm(f"ffn_in_{l}", "nn", hn, W["w_ffn_in"], grid=(S // tm, N_CHIPS),
                 a_spec=pl.BlockSpec((tm, D), lambda i, j: (i, 0)),
                 b_spec=pl.BlockSpec((None, D, FH), lambda i, j: (j, 0, 0)),
                 out_shape=(2, S, F), out_dtype=BF16,
                 out_spec=pl.BlockSpec((None, tm, FH), lambda i, j: (j // 2, i, j % 2)))
        act = _swiglu_fwd(f"swiglu_{l}", z3)
        h_out = _mm_nn(f"ffn_out_{l}", act, W["w_ffn_out"], out_dtype=F32, res=h_mid, tn=D)
        saved.append((W, h, h_mid, mix_saved, (hn, z3, act)))
        h = h_out
        l += 1

    dh, dhb, dg_final, loss_row = _loss_head("loss_head", h, target, norm_final)

    dk_parts, dv_parts, dc_parts = [], [], []
    token = None
    for l in reversed(range(len(saved))):
        W, h_in, h_mid, mix_saved, (hn, z3, act) = saved[l]
        recurrent = "w_rec_in" in W
        FH = W["w_ffn_in"].shape[-1]
        G = {}
        norm_ffn = W["norm_ffn"]
        if token is not None:
            norm_ffn = norm_ffn + jnp.minimum(token[:1, :1], 0.0)
        G["w_ffn_out"] = _mm_tn(f"d_ffn_out_{l}", act, dhb, out_dtype=BF16, tn=D)
        da = _mm_nt(f"d_act_{l}", dhb, W["w_ffn_out"], out_dtype=BF16, tn=FH)
        dz3 = _swiglu_bwd(f"d_swiglu_{l}", da, z3)
        G["w_ffn_in"] = _mm(
            f"d_ffn_in_{l}", "tn", hn, dz3, grid=(D // td, N_CHIPS),
            a_spec=pl.BlockSpec((S, td), lambda i, j: (0, i)),
            b_spec=pl.BlockSpec((None, S, FH), lambda i, j: (j // 2, 0, j % 2)),
            out_shape=(N_CHIPS, D, FH), out_dtype=BF16,
            out_spec=pl.BlockSpec((None, td, FH), lambda i, j: (j, i, 0)))
        token = layer_grads(l, "ffn", G)
        G = {}
        dhn = _mm(f"d_ffn_hn_{l}", "nt", dz3, W["w_ffn_in"], grid=(S // tm, 1, N_CHIPS), nk=N_CHIPS,
                  a_spec=pl.BlockSpec((None, tm, FH), lambda i, j, k: (k // 2, i, k % 2)),
                  b_spec=pl.BlockSpec((None, D, FH), lambda i, j, k: (k, 0, 0)),
                  out_shape=(S, D), out_dtype=F32, out_spec=pl.BlockSpec((tm, D), lambda i, j, k: (i, 0)))
        norm_ffn = norm_ffn + jnp.minimum(token[:1, :1], 0.0)
        dh, dhb, G["norm_ffn"] = _rmsnorm_bwd(f"d_ffn_norm_{l}", dhn, h_mid, norm_ffn, dh)
        if recurrent:
            CH = W["w_rec_in"].shape[-1]
            C = 2 * CH
            xn, proj, rc, rcb, gip, grp, hrec, m = mix_saved
            G["w_rec_out"] = _mm_tn(f"d_rec_out_{l}", m, dhb, out_dtype=BF16, tn=D)
            dm = _mm_nt(f"d_m_{l}", dhb, W["w_rec_out"], out_dtype=F32, tn=C)
            dgb, dgi, dgr, drc1, G["b_gi"], G["b_gr"], G["lru_param"] = _lru_bwd(
                f"d_lru_{l}", dm, proj, hrec, rc, gip, grp, W["lru_param"])
            drc, G["w_gates"] = _gates_bwd(f"d_gates_{l}", dgi, dgr, rcb, W["w_gates"], drc1)
            drec, G["conv_w"], G["conv_b"] = _conv_bwd(f"d_conv_{l}", drc, proj, W["conv_w"])
            dproj = jnp.concatenate([dgb, drec], axis=1)
            G["w_rec_in"] = _mm(
                f"d_rec_in_{l}", "tn", xn, dproj, grid=(1, N_CHIPS),
                a_spec=pl.BlockSpec((S, D), lambda i, j: (0, 0)),
                b_spec=pl.BlockSpec((S, CH), lambda i, j: (0, j)),
                out_shape=(N_CHIPS, D, CH), out_dtype=BF16,
                out_spec=pl.BlockSpec((None, D, CH), lambda i, j: (j, 0, 0)))
            dxn = _mm(f"d_rec_xn_{l}", "nt", dproj, W["w_rec_in"], grid=(S // tm, 1, N_CHIPS), nk=N_CHIPS,
                      a_spec=pl.BlockSpec((tm, CH), lambda i, j, k: (i, k)),
                      b_spec=pl.BlockSpec((None, D, CH), lambda i, j, k: (k, 0, 0)),
                      out_shape=(S, D), out_dtype=F32, out_spec=pl.BlockSpec((tm, D), lambda i, j, k: (i, 0)))
        else:
            xn, q, o, of, lse = mix_saved
            G["w_o"] = _mm_tn(f"d_o_proj_{l}", o, dhb, out_dtype=BF16, tn=D)
            do = _mm_nt(f"d_o_{l}", dhb, W["w_o"], out_dtype=BF16, tn=D)
            dq, dk, dv, dc3, dcr = _attn_bwd(f"attn_bwd_{l}", q, kv, negc3, of, do, lse)
            dk_parts.append(dk)
            dv_parts.append(dv)
            dc_parts.append(dc3.reshape(2 * HP, S).T + dcr[:, ::HEAD_DIM])
            G["w_q"] = _mm_tn(f"d_q_proj_{l}", xn, dq, out_dtype=BF16, tn=D)
            dxn = _mm_nt(f"d_q_xn_{l}", dq, W["w_q"], out_dtype=F32, tn=D)
        dh, dhb, G["norm_mix"] = _rmsnorm_bwd(f"d_mix_norm_{l}", dxn, h_in, W["norm_mix"], dh)
        if "w_kv" in W:
            dkb = _add_cast("dk_sum", dk_parts, BF16)
            dvb = _add_cast("dv_sum", dv_parts, BF16)
            dkv = jnp.concatenate([dkb, dvb], axis=1)
            dc = sum(dc_parts[1:], dc_parts[0])
            dc_pad = jnp.pad(dc, ((0, 0), (0, LANES - 2 * HP)))
            dls = _cumsum_rows("dc_cumsum", dc_pad, True)
            dfb, G["b_f"] = _logsig_bwd("d_logsig", dls, f_pre)
            G["w_kv"] = _mm_tn("d_kv_proj", hn_kv, dkv, out_dtype=BF16)
            G["w_f"] = _mm_tn("d_f_proj", hn_kv, dfb, out_dtype=F32)
            dhn1 = _mm_nt("d_kv_hn", dkv, W["w_kv"], out_dtype=F32, tn=D)
            dhn2 = _mm_nt("d_f_hn", dfb, W["w_f"], out_dtype=F32, tn=D)
            dhn_kv = _add_cast("d_kv_hn_sum", [dhn1, dhn2], F32)
            dh, dhb, G["norm_kv"] = _rmsnorm_bwd("d_kv_norm", dhn_kv, h_kv, W["norm_kv"], dh)
        token = layer_grads(l, "mix", G)
    return loss_row, dh, dg_final


_ANY = pl.BlockSpec(memory_space=pl.ANY)


def _position():
    return lax.axis_index("x"), lax.axis_index("y"), lax.axis_index("c")


def _chip_peers(x, y):
    return [(1 - x, y), (x, 1 - y), (1 - x, 1 - y)]


def _half_rows(c, n):
    h = n // 2
    assert h % 16 == 0
    return pl.ds(pl.multiple_of(c * h, 16), h)


def _place_own(name, shard, layer, me):
    _, R, C = shard.shape
    tr = _row_tile(R, C, shard.dtype.itemsize)

    def body(me_ref, x_ref, o_ref):
        o_ref[...] = x_ref[...]

    return pl.pallas_call(
        body, name=name,
        grid_spec=pltpu.PrefetchScalarGridSpec(
            num_scalar_prefetch=1, grid=(R // tr,),
            in_specs=[pl.BlockSpec((None, tr, C), lambda i, me_ref: (layer, i, 0))],
            out_specs=pl.BlockSpec((None, tr, C), lambda i, me_ref: (me_ref[0], i, 0))),
        out_shape=jax.ShapeDtypeStruct((N_CHIPS, R, C), shard.dtype),
        compiler_params=_params(("parallel",)),
    )(me, shard)


def _gather_smalls(name, smalls):
    ns = len(smalls)

    def body(*refs):
        ins, outs = refs[:ns], refs[ns:2 * ns]
        send_sems, recv_sems, local_sems = refs[2 * ns:]
        x, y, c = _position()
        me = 2 * x + y
        peers = _chip_peers(x, y)

        def remote(t, k, chip):
            px, py = peers[k]
            return pltpu.make_async_remote_copy(
                src_ref=ins[t], dst_ref=outs[t].at[chip], send_sem=send_sems.at[3 * t + k],
                recv_sem=recv_sems.at[3 * t + k], device_id=(px, py, c), device_id_type=MESH)

        local = [pltpu.make_async_copy(ins[t], outs[t].at[me], local_sems.at[t]) for t in range(ns)]
        for t in range(ns):
            local[t].start()
            for k in range(3):
                remote(t, k, me).start()
        for t in range(ns):
            for k in range(3):
                px, py = peers[k]
                remote(t, k, 2 * px + py).wait_recv()
        for t in range(ns):
            for k in range(3):
                remote(t, k, me).wait_send()
            local[t].wait()

    return pl.pallas_call(
        body, name=name, in_specs=[_ANY] * ns, out_specs=[_ANY] * ns,
        out_shape=[jax.ShapeDtypeStruct((N_CHIPS,) + s.shape, s.dtype) for s in smalls],
        scratch_shapes=[pltpu.SemaphoreType.DMA((3 * ns,)), pltpu.SemaphoreType.DMA((3 * ns,)),
                        pltpu.SemaphoreType.DMA((ns,))],
    )(*smalls)


_SEM = pl.BlockSpec(memory_space=pltpu.SEMAPHORE)
_SPLIT = pltpu.CompilerParams(has_side_effects=pltpu.SideEffectType.DATAFLOW_SIDE_EFFECTING)


def _weight_copy(shards, buf, items, sems, i, k, chip_of_dst, peers, c):
    w, l = items[i]
    px, py = peers[k]
    half = _half_rows(c, shards[w].shape[1])
    return pltpu.make_async_remote_copy(
        src_ref=shards[w].at[l, half], dst_ref=buf.at[chip_of_dst, half],
        send_sem=sems[0].at[3 * i + k], recv_sem=sems[1].at[3 * i + k],
        device_id=(px, py, c), device_id_type=MESH)


def _gather_start(name, shards, bufs, items, after):
    nw, n = len(shards), len(bufs)

    def body(*refs):
        ins, outs, sems = refs[:nw], refs[nw + n + 1:nw + 2 * n + 1], refs[nw + 2 * n + 1:]
        x, y, c = _position()
        peers = _chip_peers(x, y)
        for i in range(n):
            for k in range(3):
                _weight_copy(ins, outs[i], items, sems, i, k, 2 * x + y, peers, c).start()

    res = pl.pallas_call(
        body, name=name, in_specs=[_ANY] * (nw + n + 1), out_specs=[_ANY] * n + [_SEM, _SEM],
        out_shape=[jax.ShapeDtypeStruct(b.shape, b.dtype) for b in bufs]
        + [pltpu.SemaphoreType.DMA((3 * n,)), pltpu.SemaphoreType.DMA((3 * n,))],
        input_output_aliases={nw + i: i for i in range(n)}, compiler_params=_SPLIT,
    )(*shards, *bufs, after)
    return res[:n], res[n:]


def _gather_wait(name, shards, bufs, items, ids, sems, after):
    nw, m = len(shards), len(ids)

    def body(*refs):
        ins, bs = refs[:nw], refs[nw:nw + m]
        sem_refs = refs[nw + m:nw + m + 2]
        x, y, c = _position()
        peers = _chip_peers(x, y)
        for j, i in enumerate(ids):
            for k in range(3):
                px, py = peers[k]
                _weight_copy(ins, bs[j], items, sem_refs, i, k, 2 * px + py, peers, c).wait_recv()
        for j, i in enumerate(ids):
            for k in range(3):
                _weight_copy(ins, bs[j], items, sem_refs, i, k, 2 * x + y, peers, c).wait_send()

    res = pl.pallas_call(
        body, name=name, in_specs=[_ANY] * (nw + m) + [_SEM, _SEM, _ANY], out_specs=[_ANY] * m,
        out_shape=[jax.ShapeDtypeStruct(bufs[i].shape, bufs[i].dtype) for i in ids],
        input_output_aliases={nw + j: j for j in range(m)}, compiler_params=_SPLIT,
    )(*shards, *[bufs[i] for i in ids], *sems, after)
    return list(res)


def _gather_d2d(name, bufs):
    n = len(bufs)

    def body(*refs):
        ins, outs = refs[:n], refs[n:2 * n]
        send_sems, recv_sems = refs[2 * n:]
        x, y, c = _position()
        peers = _chip_peers(x, y)

        def remote(i, k, core):
            px, py = peers[k]
            half = _half_rows(core, ins[i].shape[1])
            return pltpu.make_async_remote_copy(
                src_ref=ins[i].at[2 * px + py, half], dst_ref=outs[i].at[2 * px + py, half],
                send_sem=send_sems.at[3 * i + k], recv_sem=recv_sems.at[3 * i + k],
                device_id=(x, y, 1 - c), device_id_type=MESH)

        for i in range(n):
            for k in range(3):
                remote(i, k, c).start()
        for i in range(n):
            for k in range(3):
                remote(i, k, 1 - c).wait_recv()
        for i in range(n):
            for k in range(3):
                remote(i, k, c).wait_send()

    return list(pl.pallas_call(
        body, name=name, in_specs=[_ANY] * n, out_specs=[_ANY] * n,
        out_shape=[jax.ShapeDtypeStruct(g.shape, g.dtype) for g in bufs],
        input_output_aliases={i: i for i in range(n)},
        scratch_shapes=[pltpu.SemaphoreType.DMA((3 * n,)), pltpu.SemaphoreType.DMA((3 * n,))],
    )(*bufs))


def _reduce_d2d(name, grads):
    n = len(grads)

    def body(*refs):
        ins, outs = refs[:n], refs[n:2 * n]
        send_sems, recv_sems = refs[2 * n:]
        x, y, c = _position()
        remote = [pltpu.make_async_remote_copy(
            src_ref=ins[i].at[:, _half_rows(1 - c, ins[i].shape[1])], dst_ref=outs[i],
            send_sem=send_sems.at[i], recv_sem=recv_sems.at[i],
            device_id=(x, y, 1 - c), device_id_type=MESH) for i in range(n)]
        for cp in remote:
            cp.start()
        for cp in remote:
            cp.wait_recv()
        for cp in remote:
            cp.wait_send()

    return pl.pallas_call(
        body, name=name, in_specs=[_ANY] * n, out_specs=[_ANY] * n,
        out_shape=[jax.ShapeDtypeStruct((N_CHIPS, g.shape[1] // 2, g.shape[2]), g.dtype) for g in grads],
        scratch_shapes=[pltpu.SemaphoreType.DMA((n,)), pltpu.SemaphoreType.DMA((n,))],
    )(*grads)


def _sum_cores(name, g, other, core):
    _, R, C = g.shape
    H = R // 2
    tr = _row_tile(H, C)
    nb = H // tr

    def body(c_ref, g_ref, o_ref, out_ref):
        out_ref[...] = (g_ref[...].astype(F32) + o_ref[...].astype(F32)).astype(out_ref.dtype)

    return pl.pallas_call(
        body, name=name,
        grid_spec=pltpu.PrefetchScalarGridSpec(
            num_scalar_prefetch=1, grid=(N_CHIPS, nb),
            in_specs=[pl.BlockSpec((None, tr, C), lambda j, i, c_ref: (j, c_ref[0] * nb + i, 0)),
                      pl.BlockSpec((None, tr, C), lambda j, i, c_ref: (j, i, 0))],
            out_specs=pl.BlockSpec((None, tr, C), lambda j, i, c_ref: (j, i, 0))),
        out_shape=jax.ShapeDtypeStruct((N_CHIPS, H, C), BF16),
        compiler_params=_params(("parallel", "parallel")),
    )(core, g, other)


def _sum_chips(name, received, own, full, layer, me_core):
    _, H, C = received.shape
    tr = _row_tile(H, C)
    nb = H // tr

    def body(s_ref, r_ref, own_ref, full_ref, out_ref):
        acc = r_ref[0].astype(F32)
        for k in (1, 2):
            acc = acc + r_ref[k].astype(F32)
        out_ref[...] = acc + own_ref[...].astype(F32)

    return pl.pallas_call(
        body, name=name,
        grid_spec=pltpu.PrefetchScalarGridSpec(
            num_scalar_prefetch=1, grid=(nb,),
            in_specs=[pl.BlockSpec((3, tr, C), lambda i, s_ref: (0, i, 0)),
                      pl.BlockSpec((None, tr, C), lambda i, s_ref: (s_ref[0], i, 0)),
                      _ANY],
            out_specs=pl.BlockSpec((None, tr, C), lambda i, s_ref: (layer, s_ref[1] * nb + i, 0))),
        out_shape=jax.ShapeDtypeStruct(full.shape, full.dtype),
        input_output_aliases={3: 0},
        compiler_params=_params(("parallel",)),
    )(me_core, received, own, full)


def _part_copy(parts, recv, sems, i, k, peers, c):
    px, py = peers[k]
    return pltpu.make_async_remote_copy(
        src_ref=parts[i].at[2 * px + py], dst_ref=recv[i].at[k],
        send_sem=sems[0].at[3 * i + k], recv_sem=sems[1].at[3 * i + k],
        device_id=(px, py, c), device_id_type=MESH)


def _scatter_start(name, parts):
    n = len(parts)

    def body(*refs):
        ins, outs, sems, token = refs[:n], refs[n:2 * n], refs[2 * n:2 * n + 2], refs[2 * n + 2]
        x, y, c = _position()
        peers = _chip_peers(x, y)
        for i in range(n):
            for k in range(3):
                _part_copy(ins, outs, sems, i, k, peers, c).start()
        token[...] = jnp.zeros_like(token)

    res = pl.pallas_call(
        body, name=name, in_specs=[_ANY] * n,
        out_specs=[_ANY] * n + [_SEM, _SEM, pl.BlockSpec(memory_space=pltpu.VMEM)],
        out_shape=[jax.ShapeDtypeStruct((3,) + p.shape[1:], p.dtype) for p in parts]
        + [pltpu.SemaphoreType.DMA((3 * n,)), pltpu.SemaphoreType.DMA((3 * n,)),
           jax.ShapeDtypeStruct((SUBLANES, LANES), F32)],
        compiler_params=_SPLIT,
    )(*parts)
    return list(res[:n]), res[n:n + 2], res[n + 2]


def _scatter_wait(name, parts, recv, sems):
    n = len(parts)

    def body(*refs):
        ins, rs, sem_refs = refs[:n], refs[n:2 * n], refs[2 * n:2 * n + 2]
        x, y, c = _position()
        peers = _chip_peers(x, y)
        for i in range(n):
            for k in range(3):
                _part_copy(ins, rs, sem_refs, i, k, peers, c).wait_recv()
        for i in range(n):
            for k in range(3):
                _part_copy(ins, rs, sem_refs, i, k, peers, c).wait_send()

    return list(pl.pallas_call(
        body, name=name, in_specs=[_ANY] * (2 * n) + [_SEM, _SEM], out_specs=[_ANY] * n,
        out_shape=[jax.ShapeDtypeStruct(r.shape, r.dtype) for r in recv],
        input_output_aliases={n + i: i for i in range(n)}, compiler_params=_SPLIT,
    )(*parts, *recv, *sems))


def _share_d2d(name, full):
    n = len(full)

    def body(*refs):
        ins, outs = refs[:n], refs[n:2 * n]
        send_sems, recv_sems = refs[2 * n:]
        x, y, c = _position()

        def remote(w, core):
            half = _half_rows(core, ins[w].shape[1])
            return pltpu.make_async_remote_copy(
                src_ref=ins[w].at[:, half], dst_ref=outs[w].at[:, half],
                send_sem=send_sems.at[w], recv_sem=recv_sems.at[w],
                device_id=(x, y, 1 - c), device_id_type=MESH)

        for w in range(n):
            remote(w, c).start()
        for w in range(n):
            remote(w, 1 - c).wait_recv()
        for w in range(n):
            remote(w, c).wait_send()

    return pl.pallas_call(
        body, name=name, in_specs=[_ANY] * n, out_specs=[_ANY] * n,
        out_shape=[jax.ShapeDtypeStruct(f.shape, f.dtype) for f in full],
        input_output_aliases={w: w for w in range(n)},
        scratch_shapes=[pltpu.SemaphoreType.DMA((n,)), pltpu.SemaphoreType.DMA((n,))],
    )(*full)


def _gather_all(name, a):
    def body(a_ref, o_ref, send_sems, recv_sems, local_sem):
        x, y, c = _position()
        me = 4 * x + 2 * y + c

        def peer(k):
            return (x ^ ((k >> 2) & 1), y ^ ((k >> 1) & 1), c ^ (k & 1))

        def remote(k, slot):
            return pltpu.make_async_remote_copy(
                src_ref=a_ref, dst_ref=o_ref.at[slot], send_sem=send_sems.at[k - 1], recv_sem=recv_sems.at[k - 1],
                device_id=peer(k), device_id_type=MESH)

        local = pltpu.make_async_copy(a_ref, o_ref.at[me], local_sem)
        local.start()
        for k in range(1, N_DEV):
            remote(k, me).start()
        for k in range(1, N_DEV):
            px, py, pc = peer(k)
            remote(k, 4 * px + 2 * py + pc).wait_recv()
        for k in range(1, N_DEV):
            remote(k, me).wait_send()
        local.wait()

    return pl.pallas_call(
        body, name=name, in_specs=[_ANY], out_specs=_ANY,
        out_shape=jax.ShapeDtypeStruct((N_DEV,) + a.shape, a.dtype),
        scratch_shapes=[pltpu.SemaphoreType.DMA((N_DEV - 1,)), pltpu.SemaphoreType.DMA((N_DEV - 1,)),
                        pltpu.SemaphoreType.DMA],
    )(a)


def _rows2d(a, lead=0):
    return a.reshape(a.shape[:lead] + (-1, a.shape[-1]))


def _row_tile(rows, cols, itemsize=4, target=1 << 20):
    want = max(SUBLANES, target // (cols * itemsize))
    t = min(rows, (want // 16) * 16)
    while t > 16 and rows % t:
        t -= 16
    return t if rows % t == 0 else rows


def _sum_slots(name, r, out_dtype=F32):
    ns = r.shape[0]
    r2 = _rows2d(r, 1)
    _, rows, cols = r2.shape
    tr = _row_tile(rows, cols)

    def body(r_ref, o_ref):
        acc = r_ref[0].astype(F32)
        for s in range(1, ns):
            acc = acc + r_ref[s].astype(F32)
        o_ref[...] = acc.astype(o_ref.dtype)

    out = pl.pallas_call(
        body, name=name, grid=(rows // tr,),
        in_specs=[pl.BlockSpec((ns, tr, cols), lambda i: (0, i, 0))],
        out_specs=pl.BlockSpec((tr, cols), lambda i: (i, 0)),
        out_shape=jax.ShapeDtypeStruct((rows, cols), out_dtype),
        compiler_params=_params(("parallel",)),
    )(r2)
    return out.reshape(r.shape[1:])


def _adamw(name, g_parts, w, m, v):
    shape = w.shape
    ng = len(g_parts)
    args = [_rows2d(a) for a in (*g_parts, w, m, v)]
    rows, cols = args[0].shape
    tr = _row_tile(rows, cols, target=1 << 19)
    c1 = 1.0 - ADAM_B1 ** ADAM_STEP
    c2 = 1.0 - ADAM_B2 ** ADAM_STEP

    def body(*refs):
        g = refs[0][...]
        for r in refs[1:ng]:
            g = g + r[...]
        w_ref, m_ref, v_ref = refs[ng:ng + 3]
        g_out, d_out, m_out, v_out = refs[ng + 3:]
        mn = ADAM_B1 * m_ref[...] + (1.0 - ADAM_B1) * g
        vn = ADAM_B2 * v_ref[...] + (1.0 - ADAM_B2) * (g * g)
        m_hat = mn / c1
        v_hat = vn / c2
        g_out[...] = g
        d_out[...] = -ADAM_LR * (m_hat / (jnp.sqrt(v_hat) + ADAM_EPS) + ADAM_WD * w_ref[...])
        m_out[...] = mn
        v_out[...] = vn

    spec = pl.BlockSpec((tr, cols), lambda i: (i, 0))
    outs = pl.pallas_call(
        body, name=name, grid=(rows // tr,), in_specs=[spec] * (ng + 3), out_specs=[spec] * 4,
        out_shape=[jax.ShapeDtypeStruct((rows, cols), F32)] * 4,
        compiler_params=_params(("parallel",)),
    )(*args)
    return tuple(o.reshape(shape) for o in outs)


_WEIGHTS = ["norm_mix", "norm_ffn", "w_ffn_in", "w_ffn_out", "w_rec_in", "conv_w", "conv_b", "w_lru_gates",
            "b_lru_gates", "lru_param", "w_rec_out", "norm_kv", "w_kvf", "b_forget", "w_q", "w_o", "norm_final"]
_BIG = ["w_ffn_in", "w_ffn_out", "w_rec_in", "w_lru_gates", "w_rec_out", "w_kvf", "w_q", "w_o"]


def _stack3(a):
    return a[None] if a.ndim == 2 else a.reshape(a.shape[0], -1, a.shape[-1])


def _pad_lanes(a, n):
    return jnp.pad(a, ((0, 0),) * (a.ndim - 1) + ((0, n - a.shape[-1]),))


def kernel(x, norm_mix, norm_ffn, w_ffn_in, w_ffn_out, w_rec_in, conv_w, conv_b, w_lru_gates, b_lru_gates, lru_param, w_rec_out, norm_kv, w_kvf, b_forget, w_q, w_o, norm_final, loss_target, m_norm_mix, m_norm_ffn, m_w_ffn_in, m_w_ffn_out, m_w_rec_in, m_conv_w, m_conv_b, m_w_lru_gates, m_b_lru_gates, m_lru_param, m_w_rec_out, m_norm_kv, m_w_kvf, m_b_forget, m_w_q, m_w_o, m_norm_final, v_norm_mix, v_norm_ffn, v_w_ffn_in, v_w_ffn_out, v_w_rec_in, v_conv_w, v_conv_b, v_w_lru_gates, v_b_lru_gates, v_lru_param, v_w_rec_out, v_norm_kv, v_w_kvf, v_b_forget, v_w_q, v_w_o, v_norm_final):
    P = dict(norm_mix=norm_mix, norm_ffn=norm_ffn, w_ffn_in=w_ffn_in, w_ffn_out=w_ffn_out, w_rec_in=w_rec_in,
             conv_w=conv_w, conv_b=conv_b, w_lru_gates=w_lru_gates, b_lru_gates=b_lru_gates, lru_param=lru_param,
             w_rec_out=w_rec_out, norm_kv=norm_kv, w_kvf=w_kvf, b_forget=b_forget, w_q=w_q, w_o=w_o,
             norm_final=norm_final)
    M1 = dict(norm_mix=m_norm_mix, norm_ffn=m_norm_ffn, w_ffn_in=m_w_ffn_in, w_ffn_out=m_w_ffn_out,
              w_rec_in=m_w_rec_in, conv_w=m_conv_w, conv_b=m_conv_b, w_lru_gates=m_w_lru_gates,
              b_lru_gates=m_b_lru_gates, lru_param=m_lru_param, w_rec_out=m_w_rec_out, norm_kv=m_norm_kv,
              w_kvf=m_w_kvf, b_forget=m_b_forget, w_q=m_w_q, w_o=m_w_o, norm_final=m_norm_final)
    M2 = dict(norm_mix=v_norm_mix, norm_ffn=v_norm_ffn, w_ffn_in=v_w_ffn_in, w_ffn_out=v_w_ffn_out,
              w_rec_in=v_w_rec_in, conv_w=v_conv_w, conv_b=v_conv_b, w_lru_gates=v_w_lru_gates,
              b_lru_gates=v_b_lru_gates, lru_param=v_lru_param, w_rec_out=v_w_rec_out, norm_kv=v_norm_kv,
              w_kvf=v_w_kvf, b_forget=v_b_forget, w_q=v_w_q, w_o=v_w_o, norm_final=v_norm_final)

    _, S, D = x.shape
    L = norm_mix.shape[0]
    NA, NBLK, BW, GS = w_lru_gates.shape
    NB = w_q.shape[0]
    C = NBLK * BW
    CS = C // N_CHIPS
    H = b_forget.shape[0]
    assert C == D and H * HEAD_DIM == D and H <= LANES
    chip = 2 * lax.axis_index("x") + lax.axis_index("y")

    small_a = jnp.concatenate([conv_w, conv_b[:, None], lru_param[:, None]], axis=1)
    small_a, b_gates = _gather_smalls("gather_smalls", [small_a, b_lru_gates])
    small_a = small_a.transpose(1, 2, 0, 3).reshape(NA, 6, C)
    b_gates = b_gates.transpose(1, 2, 0, 3).reshape(NA, NBLK, 1, N_CHIPS * GS)
    shards = [_stack3(P[w]).astype(BF16) for w in _BIG]
    core = lax.axis_index("c")
    chip_id = jnp.reshape(chip, (1,)).astype(jnp.int32)
    core_id = jnp.reshape(core, (1,)).astype(jnp.int32)
    me_core = jnp.stack([chip, core]).astype(jnp.int32)

    def stage_items(l, part):
        if part == "ffn":
            return [(_BIG.index("w_ffn_in"), l), (_BIG.index("w_ffn_out"), l)]
        if l < NA:
            names, at = ["w_rec_in", "w_lru_gates", "w_rec_out"], l
        else:
            names, at = (["w_kvf"] if l == NA else []) + ["w_q", "w_o"], l - NA
        return [(_BIG.index(n), 0 if n == "w_kvf" else at) for n in names]

    stages = [(l, part) for l in range(L) for part in ("mix", "ffn")]
    items = [it for st in stages for it in stage_items(*st)]
    ids_of = {st: [items.index(it) for it in stage_items(*st)] for st in stages}
    bufs = [_place_own(f"place_{_BIG[w]}_{li}", shards[w], li, chip_id) for w, li in items]
    bufs, gather_sems = _gather_start("gather_start", shards, bufs, items, small_a)

    def layer_weights(l, part, after):
        if l >= L:
            return None
        ids = ids_of[(l, part)]
        got = _gather_wait(f"gather_wait_{part}_{l}", shards, bufs, items, ids, gather_sems, after)
        got = _gather_d2d(f"gather_d2d_{part}_{l}", got)
        B = {_BIG[items[i][0]]: g for i, g in zip(ids, got)}
        if part == "ffn":
            return dict(norm_ffn=norm_ffn[l][None], w_ffn_in=B["w_ffn_in"], w_ffn_out=B["w_ffn_out"].reshape(-1, D))
        W = dict(norm_mix=norm_mix[l][None])
        if l < NA:
            W.update(w_rec_in=B["w_rec_in"],
                     w_gates=B["w_lru_gates"].reshape(N_CHIPS, NBLK, BW, GS).transpose(1, 2, 0, 3).reshape(
                         NBLK, BW, N_CHIPS * GS),
                     b_gates=b_gates[l], w_rec_out=B["w_rec_out"].reshape(C, D),
                     conv_w=small_a[l, :4], conv_b=small_a[l, 4:5], lru_param=small_a[l, 5:6])
        else:
            W.update(w_q=B["w_q"].reshape(D, D), w_o=B["w_o"].reshape(D, D))
            if l == NA:
                w_kvf_full = B["w_kvf"].transpose(1, 0, 2).reshape(D, -1)
                W.update(norm_kv=norm_kv[None], w_kv=w_kvf_full[:, :2 * D],
                         w_f=_pad_lanes(w_kvf_full[:, 2 * D:], LANES), b_f=_pad_lanes(b_forget[None], LANES))
        return W

    G_small = {l: {} for l in range(L)}
    pending = {}

    def layer_grads(l, part, G):
        G_small[l].update(G)
        by_name = dict(
            w_ffn_in=lambda: G["w_ffn_in"], w_ffn_out=lambda: G["w_ffn_out"].reshape(N_CHIPS, -1, D),
            w_rec_in=lambda: G["w_rec_in"],
            w_lru_gates=lambda: G["w_gates"].reshape(NBLK, BW, N_CHIPS, GS).transpose(2, 0, 1, 3).reshape(
                N_CHIPS, NBLK * BW, GS),
            w_rec_out=lambda: G["w_rec_out"].reshape(N_CHIPS, -1, D),
            w_kvf=lambda: jnp.concatenate([G["w_kv"].astype(F32), G["w_f"][:, :H]], axis=1).reshape(
                D, N_CHIPS, -1).transpose(1, 0, 2).astype(BF16),
            w_q=lambda: G["w_q"].reshape(N_CHIPS, -1, D), w_o=lambda: G["w_o"].reshape(N_CHIPS, -1, D))
        its = stage_items(l, part)
        grads = [by_name[_BIG[w]]() for w, _ in its]
        others = _reduce_d2d(f"reduce_d2d_{part}_{l}", grads)
        parts = [_sum_cores(f"sum_cores_{l}_{_BIG[w]}", g, o, core_id) for (w, _), g, o in zip(its, grads, others)]
        recv, sems, token = _scatter_start(f"scatter_start_{part}_{l}", parts)
        pending[(l, part)] = (parts, recv, sems)
        return token

    loss_row, grad_x, dg_final = _local_step(x.reshape(S, D), loss_target.reshape(S, D), norm_final[None],
                                             layer_weights, layer_grads)

    rows = [*[G_small[l]["norm_mix"] for l in range(L)], *[G_small[l]["norm_ffn"] for l in range(L)],
            G_small[NA]["norm_kv"], dg_final, _pad_lanes(G_small[NA]["b_f"], D), _pad_lanes(loss_row, D)]
    for a in range(NA):
        rows += [G_small[a][n] for n in ("conv_w", "conv_b", "b_gi", "b_gr", "lru_param")]
    packed = jnp.concatenate(rows, axis=0)
    tot = _sum_slots("sum_small", _gather_all("gather_small", packed))
    loss = tot[2 * L + 3, 0]
    g_rep = jnp.concatenate([tot[:2 * L + 2], tot[2 * L + 2:2 * L + 3]], axis=0)
    base = 2 * L + 4
    g_sh = []
    for a in range(NA):
        blk = lax.dynamic_slice_in_dim(tot[base + 8 * a:base + 8 * a + 8], chip * CS, CS, axis=1)
        gi = tot[base + 8 * a + 5].reshape(NBLK, BW)
        gr = tot[base + 8 * a + 6].reshape(NBLK, BW)
        bl = lax.dynamic_slice_in_dim(jnp.concatenate([gi, gr], axis=1), chip * GS, GS, axis=1)
        g_sh += [blk[:5], bl.reshape(-1, CS), blk[7:8]]
    g_sh = jnp.concatenate(g_sh, axis=0)
    nrow = g_sh.shape[0] // NA

    def pack_rep(T):
        return jnp.concatenate([T["norm_mix"], T["norm_ffn"], T["norm_kv"][None], T["norm_final"][None],
                                _pad_lanes(T["b_forget"][None], D)], axis=0)

    def pack_sh(T):
        return jnp.concatenate([jnp.concatenate([T["conv_w"][a], T["conv_b"][a][None],
                                                 T["b_lru_gates"][a].reshape(-1, CS), T["lru_param"][a][None]], axis=0)
                                for a in range(NA)], axis=0)

    rep = _adamw("adamw_replicated", [g_rep], pack_rep(P), pack_rep(M1), pack_rep(M2))
    shd = _adamw("adamw_small_sharded", [g_sh], pack_sh(P), pack_sh(M1), pack_sh(M2))

    def unpack_rep(t):
        return dict(norm_mix=t[:L], norm_ffn=t[L:2 * L], norm_kv=t[2 * L], norm_final=t[2 * L + 1],
                    b_forget=t[2 * L + 2, :H])

    def unpack_sh(t):
        t = t.reshape(NA, nrow, CS)
        return dict(conv_w=t[:, :4], conv_b=t[:, 4], b_lru_gates=t[:, 5:nrow - 1].reshape(NA, NBLK, GS),
                    lru_param=t[:, nrow - 1])

    full = [lax.empty(sh.shape, F32) for sh in shards]
    for l, part in reversed(stages):
        parts, recv, sems = pending[(l, part)]
        recv = _scatter_wait(f"scatter_wait_{part}_{l}", parts, recv, sems)
        for (w, li), own, r in zip(stage_items(l, part), parts, recv):
            full[w] = _sum_chips(f"sum_chips_{l}_{_BIG[w]}", r, own, full[w], li, me_core)
    full = _share_d2d("share_d2d", full)
    big = {w: _adamw(f"adamw_{w}", [g.reshape(P[w].shape)], P[w], M1[w], M2[w]) for w, g in zip(_BIG, full)}

    outs = []
    for i in range(4):
        small = {**unpack_rep(rep[i]), **unpack_sh(shd[i])}
        outs.append([big[w][i] if w in big else small[w] for w in _WEIGHTS])
    return (loss, grad_x.reshape(1, S, D), *outs[0], *outs[1], *outs[2], *outs[3])
```

```python
import functools
import math

import jax
import jax.numpy as jnp
from jax import lax
from jax.experimental import pallas as pl
from jax.experimental.pallas import tpu as pltpu

F32 = jnp.float32
BF16 = jnp.bfloat16

EPS = 1e-6
LRU_C = 8.0
HEAD_DIM = 64
LANES = 128
SUBLANES = 8
VMEM_LIMIT = 48 * 1024 * 1024
N_CHIPS = 4
N_DEV = 8

ADAM_LR = 0.001
ADAM_B1 = 0.9
ADAM_B2 = 0.999
ADAM_EPS = 1e-08
ADAM_WD = 0.01
ADAM_STEP = 10

_NN = (((1,), (0,)), ((), ()))
_NT = (((1,), (1,)), ((), ()))
_TN = (((0,), (0,)), ((), ()))
_DN = {"nn": _NN, "nt": _NT, "tn": _TN}
MESH = pl.DeviceIdType.MESH


def _params(sem):
    return pltpu.CompilerParams(dimension_semantics=sem, vmem_limit_bytes=VMEM_LIMIT)


def _tile(n, want):
    if n <= want:
        return n
    t = (want // LANES) * LANES
    while t >= LANES:
        if n % t == 0:
            return t
        t -= LANES
    return n


def _sigmoid(x):
    return 1.0 / (1.0 + jnp.exp(-x))


def _softplus(x):
    return jnp.maximum(x, 0.0) + jnp.log(1.0 + jnp.exp(-jnp.abs(x)))


_GELU_C = math.sqrt(2.0 / math.pi)


def _gelu_and_grad(x):
    inner = _GELU_C * (x + 0.044715 * x * x * x)
    t = jnp.tanh(inner)
    g = 0.5 * x * (1.0 + t)
    dg = 0.5 * (1.0 + t) + 0.5 * x * (1.0 - t * t) * _GELU_C * (1.0 + 3.0 * 0.044715 * x * x)
    return g, dg


def _mm(name, mode, a, b, *, grid, a_spec, b_spec, out_shape, out_dtype, out_spec, nk=1,
        res=None, res_spec=None, bias=None, bias_spec=None, scale=None):
    dn = _DN[mode]
    has_res, has_bias = res is not None, bias is not None
    blk = tuple(d for d in out_spec.block_shape if d is not None)

    def body(*refs):
        a_ref, b_ref = refs[0], refs[1]
        p = 2
        r_ref = refs[p] if has_res else None
        p += int(has_res)
        bias_ref = refs[p] if has_bias else None
        p += int(has_bias)
        o_ref = refs[p]
        part = lax.dot_general(a_ref[...], b_ref[...], dn, preferred_element_type=F32)

        def finish(acc):
            if scale is not None:
                acc = acc * scale
            if has_bias:
                acc = acc + bias_ref[...]
            if has_res:
                acc = r_ref[...] + acc
            o_ref[...] = acc.astype(o_ref.dtype)

        if nk == 1:
            finish(part)
        else:
            acc_ref = refs[p + 1]
            k = pl.program_id(2)

            @pl.when(k == 0)
            def _():
                acc_ref[...] = part

            @pl.when(k > 0)
            def _():
                acc_ref[...] += part

            @pl.when(k == nk - 1)
            def _():
                finish(acc_ref[...])

    ins, specs = [a, b], [a_spec, b_spec]
    if has_res:
        ins.append(res)
        specs.append(res_spec)
    if has_bias:
        ins.append(bias)
        specs.append(bias_spec)
    sem = ("parallel", "parallel") + (("arbitrary",) if len(grid) == 3 else ())
    return pl.pallas_call(
        body, name=name, grid=grid, in_specs=specs, out_specs=out_spec,
        out_shape=jax.ShapeDtypeStruct(out_shape, out_dtype),
        scratch_shapes=[pltpu.VMEM(blk, F32)] if nk > 1 else [],
        compiler_params=_params(sem),
    )(*ins)


def _mm_nn(name, a, b, *, b_lead=(), out_dtype, tm=512, tn=512, res=None, bias=None, scale=None):
    M, K = a.shape
    N = b.shape[-1]
    tm, tn = _tile(M, tm), _tile(N, tn)
    nl = len(b_lead)
    return _mm(
        name, "nn", a, b, grid=(M // tm, N // tn),
        a_spec=pl.BlockSpec((tm, K), lambda i, j: (i, 0)),
        b_spec=pl.BlockSpec((None,) * nl + (K, tn), lambda i, j: tuple(b_lead) + (0, j)),
        out_shape=(M, N), out_dtype=out_dtype, out_spec=pl.BlockSpec((tm, tn), lambda i, j: (i, j)),
        res=res, res_spec=pl.BlockSpec((tm, tn), lambda i, j: (i, j)),
        bias=bias, bias_spec=pl.BlockSpec((1, tn), lambda i, j: (0, j)), scale=scale)


def _mm_nt(name, a, b, *, b_lead=(), out_dtype, tm=512, tn=512, tk=2048):
    M, K = a.shape
    N = b.shape[-2]
    tm, tn, tk = _tile(M, tm), _tile(N, tn), _tile(K, tk)
    nk = K // tk
    nl = len(b_lead)
    return _mm(
        name, "nt", a, b, grid=(M // tm, N // tn, nk), nk=nk,
        a_spec=pl.BlockSpec((tm, tk), lambda i, j, k: (i, k)),
        b_spec=pl.BlockSpec((None,) * nl + (tn, tk), lambda i, j, k: tuple(b_lead) + (j, k)),
        out_shape=(M, N), out_dtype=out_dtype, out_spec=pl.BlockSpec((tm, tn), lambda i, j, k: (i, j)))


def _mm_tn(name, a, b, *, out_dtype, tm=512, tn=512):
    S, M = a.shape
    N = b.shape[1]
    tm, tn = _tile(M, tm), _tile(N, tn)
    return _mm(
        name, "tn", a, b, grid=(M // tm, N // tn),
        a_spec=pl.BlockSpec((S, tm), lambda i, j: (0, i)),
        b_spec=pl.BlockSpec((S, tn), lambda i, j: (0, j)),
        out_shape=(M, N), out_dtype=out_dtype, out_spec=pl.BlockSpec((tm, tn), lambda i, j: (i, j)))


def _rmsnorm_fwd(name, h, g, tr=256):
    S, D = h.shape
    tr = _tile(S, tr)

    def body(h_ref, g_ref, o_ref):
        x = h_ref[...]
        r = lax.rsqrt(jnp.mean(x * x, axis=-1, keepdims=True) + EPS)
        o_ref[...] = (x * r * g_ref[...]).astype(o_ref.dtype)

    return pl.pallas_call(
        body, name=name, grid=(S // tr,),
        in_specs=[pl.BlockSpec((tr, D), lambda i: (i, 0)), pl.BlockSpec((1, D), lambda i: (0, 0))],
        out_specs=pl.BlockSpec((tr, D), lambda i: (i, 0)),
        out_shape=jax.ShapeDtypeStruct((S, D), BF16),
        compiler_params=_params(("parallel",)),
    )(h, g)


def _rmsnorm_bwd(name, dxn, h, g, dh_in, tr=256):
    S, D = h.shape
    tr = _tile(S, tr)

    def body(dxn_ref, h_ref, g_ref, dh_ref, o_ref, ob_ref, dg_ref):
        i = pl.program_id(0)
        x = h_ref[...]
        dy = dxn_ref[...].astype(F32)
        r = lax.rsqrt(jnp.mean(x * x, axis=-1, keepdims=True) + EPS)
        xr = x * r
        dyg = dy * g_ref[...]
        dx = r * dyg - xr * (r * jnp.mean(dyg * xr, axis=-1, keepdims=True))
        out = dh_ref[...] + dx
        o_ref[...] = out
        ob_ref[...] = out.astype(BF16)
        part = jnp.sum(dy * xr, axis=0, keepdims=True)

        @pl.when(i == 0)
        def _():
            dg_ref[...] = part

        @pl.when(i > 0)
        def _():
            dg_ref[...] += part

    row = pl.BlockSpec((tr, D), lambda i: (i, 0))
    vec = pl.BlockSpec((1, D), lambda i: (0, 0))
    return pl.pallas_call(
        body, name=name, grid=(S // tr,),
        in_specs=[row, row, vec, row], out_specs=[row, row, vec],
        out_shape=[jax.ShapeDtypeStruct((S, D), F32), jax.ShapeDtypeStruct((S, D), BF16),
                   jax.ShapeDtypeStruct((1, D), F32)],
        compiler_params=_params(("arbitrary",)),
    )(dxn, h, g, dh_in)


def _loss_head(name, h, target, g, tr=256):
    S, D = h.shape
    tr = _tile(S, tr)

    def body(h_ref, t_ref, g_ref, o_ref, ob_ref, dg_ref, loss_ref):
        i = pl.program_id(0)
        x = h_ref[...]
        gg = g_ref[...]
        r = lax.rsqrt(jnp.mean(x * x, axis=-1, keepdims=True) + EPS)
        xr = x * r
        err = xr * gg - t_ref[...]
        lpart = 0.5 * jnp.sum(jnp.mean(err * err, axis=-1, keepdims=True), axis=0, keepdims=True)
        dy = err * (1.0 / D)
        dyg = dy * gg
        dx = r * dyg - xr * (r * jnp.mean(dyg * xr, axis=-1, keepdims=True))
        o_ref[...] = dx
        ob_ref[...] = dx.astype(BF16)
        part = jnp.sum(dy * xr, axis=0, keepdims=True)
        lrow = jnp.broadcast_to(lpart, (1, LANES))

        @pl.when(i == 0)
        def _():
            dg_ref[...] = part
            loss_ref[...] = lrow

        @pl.when(i > 0)
        def _():
            dg_ref[...] += part
            loss_ref[...] += lrow

    row = pl.BlockSpec((tr, D), lambda i: (i, 0))
    vec = pl.BlockSpec((1, D), lambda i: (0, 0))
    return pl.pallas_call(
        body, name=name, grid=(S // tr,),
        in_specs=[row, row, vec], out_specs=[row, row, vec, pl.BlockSpec((1, LANES), lambda i: (0, 0))],
        out_shape=[jax.ShapeDtypeStruct((S, D), F32), jax.ShapeDtypeStruct((S, D), BF16),
                   jax.ShapeDtypeStruct((1, D), F32), jax.ShapeDtypeStruct((1, LANES), F32)],
        compiler_params=_params(("arbitrary",)),
    )(h, target, g)


def _swiglu_fwd(name, z3, tr=256, tc=1408):
    _, S, F = z3.shape
    tr, tc = _tile(S, tr), _tile(F, tc)

    def body(z_ref, a_ref):
        zg = z_ref[0].astype(F32)
        zu = z_ref[1].astype(F32)
        a_ref[...] = (zg * _sigmoid(zg) * zu).astype(a_ref.dtype)

    return pl.pallas_call(
        body, name=name, grid=(S // tr, F // tc),
        in_specs=[pl.BlockSpec((2, tr, tc), lambda i, j: (0, i, j))],
        out_specs=pl.BlockSpec((tr, tc), lambda i, j: (i, j)),
        out_shape=jax.ShapeDtypeStruct((S, F), BF16),
        compiler_params=_params(("parallel", "parallel")),
    )(z3)


def _swiglu_bwd(name, da, z3, tr=256, tc=1408):
    _, S, F = z3.shape
    tr, tc = _tile(S, tr), _tile(F, tc)

    def body(da_ref, z_ref, dz_ref):
        zg = z_ref[0].astype(F32)
        zu = z_ref[1].astype(F32)
        d = da_ref[...].astype(F32)
        sg = _sigmoid(zg)
        silu = zg * sg
        dz_ref[0] = (d * zu * (sg * (1.0 + zg * (1.0 - sg)))).astype(dz_ref.dtype)
        dz_ref[1] = (d * silu).astype(dz_ref.dtype)

    return pl.pallas_call(
        body, name=name, grid=(S // tr, F // tc),
        in_specs=[pl.BlockSpec((tr, tc), lambda i, j: (i, j)),
                  pl.BlockSpec((2, tr, tc), lambda i, j: (0, i, j))],
        out_specs=pl.BlockSpec((2, tr, tc), lambda i, j: (0, i, j)),
        out_shape=jax.ShapeDtypeStruct((2, S, F), BF16),
        compiler_params=_params(("parallel", "parallel")),
    )(da, z3)


SCAN_ROWS = 64


def _group_scan(A, B, reverse):
    n = A.shape[0]
    sub = lax.broadcasted_iota(jnp.int32, A.shape, 0) % SUBLANES
    for d in (1, 2, 4):
        if reverse:
            A_sh, B_sh = pltpu.roll(A, n - d, 0), pltpu.roll(B, n - d, 0)
            keep = sub < SUBLANES - d
        else:
            A_sh, B_sh = pltpu.roll(A, d, 0), pltpu.roll(B, d, 0)
            keep = sub >= d
        B = jnp.where(keep, A * B_sh + B, B)
        A = jnp.where(keep, A * A_sh, A)
    return A, B


def _block_scan(a, u, carry, reverse):
    A, B = _group_scan(a, u, reverse)
    ng = a.shape[0] // SUBLANES
    out = [None] * ng
    order = range(ng - 1, -1, -1) if reverse else range(ng)
    for gi in order:
        sl = slice(gi * SUBLANES, (gi + 1) * SUBLANES)
        hg = A[sl] * carry + B[sl]
        out[gi] = hg
        carry = hg[0:1] if reverse else hg[SUBLANES - 1:SUBLANES]
    return jnp.concatenate(out, axis=0), carry


def _lru_gates(rc, gip, grp, sp):
    gi = _sigmoid(gip)
    gr = _sigmoid(grp)
    la = -LRU_C * gr * sp
    a = jnp.exp(la)
    om = -jnp.tanh(la) * (a * a + 1.0)
    mult = jnp.sqrt(om)
    return gi, gr, a, mult


def _lru_fwd(name, proj, rc, gip, grp, lru_p, tc=256):
    S, C = rc.shape
    tc = _tile(C, tc)
    nb = S // SCAN_ROWS

    def body(gb_ref, rc_ref, gi_ref, gr_ref, l_ref, h_ref, m_ref):
        sp = _softplus(-l_ref[...])

        def step(b, carry):
            rows = pl.ds(pl.multiple_of(b * SCAN_ROWS, SCAN_ROWS), SCAN_ROWS)
            rcb = rc_ref[rows, :]
            gi, _, a, mult = _lru_gates(rcb, gi_ref[rows, :], gr_ref[rows, :], sp)
            h, carry = _block_scan(a, rcb * gi * mult, carry, False)
            h_ref[rows, :] = h
            gel, _ = _gelu_and_grad(gb_ref[rows, :])
            m_ref[rows, :] = (gel * h).astype(m_ref.dtype)
            return carry

        lax.fori_loop(0, nb, step, jnp.zeros((1, tc), F32))

    col = pl.BlockSpec((S, tc), lambda j: (0, j))
    return pl.pallas_call(
        body, name=name, grid=(C // tc,),
        in_specs=[col, col, col, col, pl.BlockSpec((1, tc), lambda j: (0, j))],
        out_specs=[col, col],
        out_shape=[jax.ShapeDtypeStruct((S, C), F32), jax.ShapeDtypeStruct((S, C), BF16)],
        compiler_params=_params(("parallel",)),
    )(proj, rc, gip, grp, lru_p)


def _lru_bwd(name, dm, proj, hrec, rc, gip, grp, lru_p, tc=256):
    S, C = rc.shape
    tc = _tile(C, tc)
    nb = S // SCAN_ROWS
    R = SCAN_ROWS

    def body(dm_ref, gb_ref, h_ref, rc_ref, gi_ref, gr_ref, l_ref,
             dgb_ref, dgi_ref, dgr_ref, drc_ref, dbi_ref, dbr_ref, dl_ref):
        lp = l_ref[...]
        sp = _softplus(-lp)
        row = lax.broadcasted_iota(jnp.int32, (R, tc), 0)
        zero = jnp.zeros((1, tc), F32)

        def step(t, carry):
            mu_in, s_i, s_r, s_sp = carry
            b = nb - 1 - t
            r0 = pl.multiple_of(b * R, R)
            rows = pl.ds(r0, R)
            rcb = rc_ref[rows, :]
            gi, gr, a, mult = _lru_gates(rcb, gi_ref[rows, :], gr_ref[rows, :], sp)
            gel, dgel = _gelu_and_grad(gb_ref[rows, :])
            dmb = dm_ref[rows, :]
            h = h_ref[rows, :]
            dgb_ref[rows, :] = (dmb * h * dgel).astype(dgb_ref.dtype)
            dh = dmb * gel
            mu, mu_out = _block_scan(a, a * dh, mu_in, True)
            mu_next = jnp.where(row == R - 1, mu_in, pltpu.roll(mu, R - 1, 0))
            lam = dh + mu_next
            p0 = pl.multiple_of(jnp.maximum(r0 - SUBLANES, 0), SUBLANES)
            prev = h_ref[pl.ds(p0, SUBLANES), :][SUBLANES - 1:SUBLANES]
            prev = jnp.where(b > 0, prev, 0.0)
            h_prev = jnp.where(row == 0, prev, pltpu.roll(h, 1, 0))
            da = lam * h_prev
            d_mult = lam * rcb * gi
            d_la = da * a - d_mult * (a * a) / mult
            d_grp = d_la * (-LRU_C * sp) * gr * (1.0 - gr)
            d_gip = lam * rcb * mult * gi * (1.0 - gi)
            dgr_ref[rows, :] = d_grp.astype(dgr_ref.dtype)
            dgi_ref[rows, :] = d_gip.astype(dgi_ref.dtype)
            drc_ref[rows, :] = lam * gi * mult
            s_i = s_i + jnp.sum(d_gip, axis=0, keepdims=True)
            s_r = s_r + jnp.sum(d_grp, axis=0, keepdims=True)
            s_sp = s_sp + jnp.sum(d_la * gr, axis=0, keepdims=True)
            return mu_out, s_i, s_r, s_sp

        _, s_i, s_r, s_sp = lax.fori_loop(0, nb, step, (zero, zero, zero, zero))
        dbi_ref[...] = s_i
        dbr_ref[...] = s_r
        dl_ref[...] = (-LRU_C * s_sp) * (-_sigmoid(-lp))

    col = pl.BlockSpec((S, tc), lambda j: (0, j))
    vec = pl.BlockSpec((1, tc), lambda j: (0, j))
    return pl.pallas_call(
        body, name=name, grid=(C // tc,),
        in_specs=[col, col, col, col, col, col, vec],
        out_specs=[col, col, col, col, vec, vec, vec],
        out_shape=[jax.ShapeDtypeStruct((S, C), BF16), jax.ShapeDtypeStruct((S, C), BF16),
                   jax.ShapeDtypeStruct((S, C), BF16), jax.ShapeDtypeStruct((S, C), F32),
                   jax.ShapeDtypeStruct((1, C), F32), jax.ShapeDtypeStruct((1, C), F32),
                   jax.ShapeDtypeStruct((1, C), F32)],
        compiler_params=_params(("parallel",)),
    )(dm, proj, hrec, rc, gip, grp, lru_p)


def _cumsum_rows(name, u, reverse):
    S, C = u.shape
    nb = S // SCAN_ROWS

    def body(u_ref, o_ref):
        def step(t, carry):
            b = nb - 1 - t if reverse else t
            rows = pl.ds(pl.multiple_of(b * SCAN_ROWS, SCAN_ROWS), SCAN_ROWS)
            ub = u_ref[rows, :]
            h, carry = _block_scan(jnp.ones_like(ub), ub, carry, reverse)
            o_ref[rows, :] = h
            return carry

        lax.fori_loop(0, nb, step, jnp.zeros((1, C), F32))

    spec = pl.BlockSpec((S, C), lambda i: (0, 0))
    return pl.pallas_call(
        body, name=name, grid=(1,), in_specs=[spec], out_specs=spec,
        out_shape=jax.ShapeDtypeStruct((S, C), F32),
        compiler_params=_params(("arbitrary",)),
    )(u)


def _shift_down(x, k):
    row = lax.broadcasted_iota(jnp.int32, x.shape, 0)
    return jnp.where(row >= k, pltpu.roll(x, k, 0), 0.0)


def _shift_up(x, k):
    n = x.shape[0]
    row = lax.broadcasted_iota(jnp.int32, x.shape, 0)
    return jnp.where(row < n - k, pltpu.roll(x, n - k, 0), 0.0)


def _conv_fwd(name, proj, w, b, tc=256):
    S, C2 = proj.shape
    C = C2 // 2
    tc = _tile(C, tc)
    off = C // tc

    def body(x_ref, w_ref, b_ref, o_ref, ob_ref):
        x = x_ref[...]
        out = b_ref[...] + w_ref[3:4, :] * x
        for k in (1, 2, 3):
            out = out + w_ref[3 - k:4 - k, :] * _shift_down(x, k)
        o_ref[...] = out
        ob_ref[...] = out.astype(BF16)

    col = pl.BlockSpec((S, tc), lambda j: (0, j))
    return pl.pallas_call(
        body, name=name, grid=(C // tc,),
        in_specs=[pl.BlockSpec((S, tc), lambda j: (0, off + j)),
                  pl.BlockSpec((4, tc), lambda j: (0, j)), pl.BlockSpec((1, tc), lambda j: (0, j))],
        out_specs=[col, col],
        out_shape=[jax.ShapeDtypeStruct((S, C), F32), jax.ShapeDtypeStruct((S, C), BF16)],
        compiler_params=_params(("parallel",)),
    )(proj, w, b)


def _conv_bwd(name, drc, proj, w, tc=256):
    S, C = drc.shape
    tc = _tile(C, tc)
    off = C // tc

    def body(y_ref, x_ref, w_ref, dx_ref, dw_ref, db_ref):
        y = y_ref[...]
        x = x_ref[...]
        dx = w_ref[3:4, :] * y
        dw_ref[3:4, :] = jnp.sum(y * x, axis=0, keepdims=True)
        for k in (1, 2, 3):
            dx = dx + w_ref[3 - k:4 - k, :] * _shift_up(y, k)
            dw_ref[3 - k:4 - k, :] = jnp.sum(y * _shift_down(x, k), axis=0, keepdims=True)
        dx_ref[...] = dx.astype(dx_ref.dtype)
        db_ref[...] = jnp.sum(y, axis=0, keepdims=True)

    col = pl.BlockSpec((S, tc), lambda j: (0, j))
    return pl.pallas_call(
        body, name=name, grid=(C // tc,),
        in_specs=[col, pl.BlockSpec((S, tc), lambda j: (0, off + j)), pl.BlockSpec((4, tc), lambda j: (0, j))],
        out_specs=[col, pl.BlockSpec((4, tc), lambda j: (0, j)), pl.BlockSpec((1, tc), lambda j: (0, j))],
        out_shape=[jax.ShapeDtypeStruct((S, C), BF16), jax.ShapeDtypeStruct((4, C), F32),
                   jax.ShapeDtypeStruct((1, C), F32)],
        compiler_params=_params(("parallel",)),
    )(drc, proj, w)


def _gates_fwd(name, rcb, wg, bg):
    S, C = rcb.shape
    nblk, bw, _ = wg.shape

    def body(x_ref, w_ref, b_ref, gi_ref, gr_ref):
        g = jnp.dot(x_ref[...], w_ref[...], preferred_element_type=F32) + b_ref[...]
        gi_ref[...] = g[:, :bw]
        gr_ref[...] = g[:, bw:]

    col = pl.BlockSpec((S, bw), lambda n: (0, n))
    return pl.pallas_call(
        body, name=name, grid=(nblk,),
        in_specs=[col, pl.BlockSpec((None, bw, 2 * bw), lambda n: (n, 0, 0)),
                  pl.BlockSpec((None, 1, 2 * bw), lambda n: (n, 0, 0))],
        out_specs=[col, col],
        out_shape=[jax.ShapeDtypeStruct((S, C), F32), jax.ShapeDtypeStruct((S, C), F32)],
        compiler_params=_params(("parallel",)),
    )(rcb, wg, bg)


def _gates_bwd(name, dgi, dgr, rcb, wg, drc1):
    S, C = rcb.shape
    nblk, bw, _ = wg.shape

    def body(dgi_ref, dgr_ref, x_ref, w_ref, d1_ref, drc_ref, dw_ref):
        w = w_ref[...]
        x = x_ref[...]
        di, dr = dgi_ref[...], dgr_ref[...]
        drc_ref[...] = (d1_ref[...]
                        + lax.dot_general(di, w[:, :bw], _NT, preferred_element_type=F32)
                        + lax.dot_general(dr, w[:, bw:], _NT, preferred_element_type=F32))
        dw_ref[:, :bw] = lax.dot_general(x, di, _TN, preferred_element_type=F32).astype(dw_ref.dtype)
        dw_ref[:, bw:] = lax.dot_general(x, dr, _TN, preferred_element_type=F32).astype(dw_ref.dtype)

    col = pl.BlockSpec((S, bw), lambda n: (0, n))
    wspec = pl.BlockSpec((None, bw, 2 * bw), lambda n: (n, 0, 0))
    return pl.pallas_call(
        body, name=name, grid=(nblk,),
        in_specs=[col, col, col, wspec, col], out_specs=[col, wspec],
        out_shape=[jax.ShapeDtypeStruct((S, C), F32), jax.ShapeDtypeStruct((nblk, bw, 2 * bw), BF16)],
        compiler_params=_params(("parallel",)),
    )(dgi, dgr, rcb, wg, drc1)


def _att_tile(S):
    return next(t for t in (512, 256, 128) if S % t == 0)


def _head_lanes(shape):
    return lax.broadcasted_iota(jnp.int32, shape, len(shape) - 1) < HEAD_DIM


def _key_bias(c_blk):
    first = _head_lanes(c_blk.shape)
    rolled = pltpu.roll(c_blk, HEAD_DIM, 1)
    return jnp.where(first, c_blk, rolled), jnp.where(first, rolled, c_blk)


def _over_keys(x, op):
    n = x.shape[0]
    while n > SUBLANES:
        n //= 2
        x = op(x[:n], x[n:2 * n])
    return (jnp.max if op is jnp.maximum else jnp.sum)(x, axis=0, keepdims=True)


def _causal_t(T, cc):
    r = lax.broadcasted_iota(jnp.int32, (T, LANES), 0)
    c = lax.broadcasted_iota(jnp.int32, (T, LANES), 1) + cc * LANES
    return r <= c


def _attn_fwd(name, q, kv, cfull):
    S, D = q.shape
    HP = D // LANES
    T = _att_tile(S)
    nq = S // T
    NC = T // LANES

    def body(q_ref, k_ref, v_ref, c_ref, o_ref, of_ref, lse_ref, bias, vT, acc, m_scr, l_scr):
        def prologue(i, _):
            rows = pl.ds(pl.multiple_of(i * T, T), T)
            bias[0, rows, :], bias[1, rows, :] = _key_bias(c_ref[rows, :])
            vT[i] = v_ref[rows, :].astype(F32).T.astype(BF16)
            return 0

        lax.fori_loop(0, nq, prologue, 0)

        def q_step(qi, _):
            q0 = pl.multiple_of(qi * T, T)
            qb = q_ref[pl.ds(q0, T), :]
            m_scr[...] = jnp.full(m_scr.shape, -jnp.inf, F32)
            l_scr[...] = jnp.zeros(l_scr.shape, F32)
            acc[...] = jnp.zeros(acc.shape, F32)

            def tile(kj, masked):
                ks = pl.ds(pl.multiple_of(kj * T, T), T)
                kf = k_ref[ks, :].astype(F32)
                first = _head_lanes(kf.shape)
                kms = [jnp.where(first if hh == 0 else jnp.logical_not(first), kf, 0.0).astype(BF16) for hh in range(2)]
                sTs = [lax.dot_general(km, qb, _NT, preferred_element_type=F32) for km in kms]
                for hh in range(2):
                    b = bias[hh, ks, :]
                    ps = []
                    for cc in range(NC):
                        cols = slice(cc * LANES, (cc + 1) * LANES)
                        s = sTs[hh][:, cols] + b
                        if masked:
                            s = jnp.where(_causal_t(T, cc), s, -jnp.inf)
                        m_old = m_scr[hh, cc]
                        m_new = jnp.maximum(m_old, _over_keys(s, jnp.maximum))
                        alpha = jnp.exp(m_old - m_new)
                        p = jnp.exp(s - m_new)
                        l_scr[hh, cc] = alpha * l_scr[hh, cc] + _over_keys(p, jnp.add)
                        m_scr[hh, cc] = m_new
                        ps.append(p.astype(BF16))
                        acc[hh, :, cols] = acc[hh, :, cols] * alpha
                    acc[hh] += jnp.dot(vT[kj, hh * HEAD_DIM:(hh + 1) * HEAD_DIM, :], jnp.concatenate(ps, axis=1),
                                       preferred_element_type=F32)

            def inner(kj, _):
                tile(kj, False)
                return 0

            lax.fori_loop(0, qi, inner, 0)
            tile(qi, True)
            outs = []
            for hh in range(2):
                inv = jnp.concatenate([1.0 / l_scr[hh, cc] for cc in range(NC)], axis=1)
                outs.append(acc[hh] * inv)
                for cc in range(NC):
                    lse_ref[hh:hh + 1, pl.ds(q0 + cc * LANES, LANES)] = m_scr[hh, cc] + jnp.log(l_scr[hh, cc])
            out = jnp.concatenate(outs, axis=0).T
            o_ref[pl.ds(q0, T), :] = out.astype(o_ref.dtype)
            of_ref[pl.ds(q0, T), :] = out
            return 0

        lax.fori_loop(0, nq, q_step, 0)

    blk = lambda off: pl.BlockSpec((S, LANES), lambda p: (0, off + p))
    return pl.pallas_call(
        body, name=name, grid=(HP,),
        in_specs=[blk(0), blk(0), blk(HP), blk(0)],
        out_specs=[blk(0), blk(0), pl.BlockSpec((None, 2, S), lambda p: (p, 0, 0))],
        out_shape=[jax.ShapeDtypeStruct((S, D), BF16), jax.ShapeDtypeStruct((S, D), F32),
                   jax.ShapeDtypeStruct((HP, 2, S), F32)],
        scratch_shapes=[pltpu.VMEM((2, S, LANES), F32), pltpu.VMEM((nq, LANES, T), BF16),
                        pltpu.VMEM((2, HEAD_DIM, T), F32), pltpu.VMEM((2, NC, 1, LANES), F32),
                        pltpu.VMEM((2, NC, 1, LANES), F32)],
        compiler_params=_params(("parallel",)),
    )(q, kv, kv, cfull)


def _attn_bwd(name, q, kv, cfull, of, do, lse3):
    S, D = q.shape
    HP = D // LANES
    T = _att_tile(S)
    nq = S // T
    NC = T // LANES
    scale = HEAD_DIM ** -0.5

    def body(q_ref, k_ref, v_ref, c_ref, of_ref, do_ref, lse_ref,
             dq_ref, dk_ref, dv_ref, dck_ref, drq_ref, bias, kT, dqT, delta, dr_scr):
        def prologue(i, _):
            rows = pl.ds(pl.multiple_of(i * T, T), T)
            bias[0, rows, :], bias[1, rows, :] = _key_bias(c_ref[rows, :])
            kT[i] = k_ref[rows, :].astype(F32).T.astype(BF16)
            prodT = (do_ref[rows, :].astype(F32) * of_ref[rows, :]).T
            for hh in range(2):
                delta[hh:hh + 1, rows] = jnp.sum(prodT[hh * HEAD_DIM:(hh + 1) * HEAD_DIM], axis=0, keepdims=True)
            dqT[i] = jnp.zeros((LANES, T), F32)
            return 0

        lax.fori_loop(0, nq, prologue, 0)
        dr_scr[...] = jnp.zeros(dr_scr.shape, F32)

        def kv_step(kj, _):
            ks = pl.ds(pl.multiple_of(kj * T, T), T)
            kf = k_ref[ks, :].astype(F32)
            vf = v_ref[ks, :].astype(F32)
            first = _head_lanes(kf.shape)
            masks = [first, jnp.logical_not(first)]
            kms = [jnp.where(m, kf, 0.0).astype(BF16) for m in masks]
            vms = [jnp.where(m, vf, 0.0).astype(BF16) for m in masks]

            def tile(qi, carry, masked):
                q0 = pl.multiple_of(qi * T, T)
                qb = q_ref[pl.ds(q0, T), :]
                dob = do_ref[pl.ds(q0, T), :]
                sTs = [lax.dot_general(km, qb, _NT, preferred_element_type=F32) for km in kms]
                dpTs = [lax.dot_general(vm, dob, _NT, preferred_element_type=F32) for vm in vms]
                out = []
                for hh in range(2):
                    dk_a, dv_a, dc_a = carry[3 * hh:3 * hh + 3]
                    b = bias[hh, ks, :]
                    head = slice(hh * HEAD_DIM, (hh + 1) * HEAD_DIM)
                    ps, dss = [], []
                    for cc in range(NC):
                        cols = slice(cc * LANES, (cc + 1) * LANES)
                        at = pl.ds(q0 + cc * LANES, LANES)
                        p = jnp.exp(sTs[hh][:, cols] + b - lse_ref[hh:hh + 1, at])
                        if masked:
                            p = jnp.where(_causal_t(T, cc), p, 0.0)
                        ds = p * (dpTs[hh][:, cols] - delta[hh:hh + 1, at])
                        ps.append(p.astype(BF16))
                        dss.append(ds.astype(BF16))
                        dc_a = dc_a + ds
                        dr_scr[hh:hh + 1, at] += _over_keys(ds, jnp.add)
                    pT = jnp.concatenate(ps, axis=1)
                    dsT = jnp.concatenate(dss, axis=1)
                    dv_a = dv_a + jnp.dot(pT, dob, preferred_element_type=F32)
                    dk_a = dk_a + jnp.dot(dsT, qb, preferred_element_type=F32)
                    dqT[qi, head, :] += jnp.dot(kT[kj, head, :], dsT, preferred_element_type=F32)
                    out += [dk_a, dv_a, dc_a]
                return tuple(out)

            zero = jnp.zeros((T, LANES), F32)
            carry = tile(kj, (zero,) * 6, True)
            dk0, dv0, dc0, dk1, dv1, dc1 = lax.fori_loop(kj + 1, nq, lambda qi, c: tile(qi, c, False), carry)
            dk_ref[ks, :] = jnp.where(first, dk0, dk1)
            dv_ref[ks, :] = jnp.where(first, dv0, dv1)
            dck_ref[ks, :] = jnp.where(first, jnp.broadcast_to(-jnp.sum(dc0, axis=1, keepdims=True), (T, LANES)),
                                       jnp.broadcast_to(-jnp.sum(dc1, axis=1, keepdims=True), (T, LANES)))
            return 0

        lax.fori_loop(0, nq, kv_step, 0)

        def epilogue(i, _):
            rows = pl.ds(pl.multiple_of(i * T, T), T)
            dq_ref[rows, :] = (dqT[i].T * scale).astype(dq_ref.dtype)
            return 0

        lax.fori_loop(0, nq, epilogue, 0)
        drq_ref[...] = dr_scr[...]

    blk = lambda off: pl.BlockSpec((S, LANES), lambda p: (0, off + p))
    row_spec = pl.BlockSpec((None, 2, S), lambda p: (p, 0, 0))
    return pl.pallas_call(
        body, name=name, grid=(HP,),
        in_specs=[blk(0), blk(0), blk(HP), blk(0), blk(0), blk(0), row_spec],
        out_specs=[blk(0), blk(0), blk(0), blk(0), row_spec],
        out_shape=[jax.ShapeDtypeStruct((S, D), BF16), jax.ShapeDtypeStruct((S, D), F32),
                   jax.ShapeDtypeStruct((S, D), F32), jax.ShapeDtypeStruct((S, D), F32),
                   jax.ShapeDtypeStruct((HP, 2, S), F32)],
        scratch_shapes=[pltpu.VMEM((2, S, LANES), F32), pltpu.VMEM((nq, LANES, T), BF16),
                        pltpu.VMEM((nq, LANES, T), F32), pltpu.VMEM((2, S), F32), pltpu.VMEM((2, S), F32)],
        compiler_params=_params(("parallel",)),
    )(q, kv, kv, cfull, of, do, lse3)


def _logsig_fwd(name, f):
    S, C = f.shape

    def body(f_ref, o_ref):
        o_ref[...] = -_softplus(-f_ref[...])

    spec = pl.BlockSpec((S, C), lambda i: (0, 0))
    return pl.pallas_call(body, name=name, grid=(1,), in_specs=[spec], out_specs=spec,
                          out_shape=jax.ShapeDtypeStruct((S, C), F32),
                          compiler_params=_params(("arbitrary",)))(f)


def _logsig_bwd(name, dls, f):
    S, C = f.shape

    def body(d_ref, f_ref, o_ref, s_ref):
        df = d_ref[...] * _sigmoid(-f_ref[...])
        o_ref[...] = df.astype(o_ref.dtype)
        s_ref[...] = jnp.sum(df, axis=0, keepdims=True)

    spec = pl.BlockSpec((S, C), lambda i: (0, 0))
    return pl.pallas_call(body, name=name, grid=(1,), in_specs=[spec, spec],
                          out_specs=[spec, pl.BlockSpec((1, C), lambda i: (0, 0))],
                          out_shape=[jax.ShapeDtypeStruct((S, C), BF16), jax.ShapeDtypeStruct((1, C), F32)],
                          compiler_params=_params(("arbitrary",)))(dls, f)


def _add_cast(name, parts, out_dtype, tr=256):
    S, C = parts[0].shape
    tr = _tile(S, tr)
    n = len(parts)

    def body(*refs):
        acc = refs[0][...].astype(F32)
        for r in refs[1:n]:
            acc = acc + r[...].astype(F32)
        refs[n][...] = acc.astype(out_dtype)

    spec = pl.BlockSpec((tr, C), lambda i: (i, 0))
    return pl.pallas_call(body, name=name, grid=(S // tr,), in_specs=[spec] * n, out_specs=spec,
                          out_shape=jax.ShapeDtypeStruct((S, C), out_dtype),
                          compiler_params=_params(("parallel",)))(*parts)


def _local_step(x, target, norm_final, layer_weights, layer_grads):
    S, D = x.shape
    HP = D // LANES
    scale = HEAD_DIM ** -0.5
    tm = _tile(S, 512)
    td = _tile(D, 512)
    saved = []
    h = x
    l = 0
    kv = cfull = f_pre = hn_kv = h_kv = None
    while True:
        W = layer_weights(l, "mix", h)
        if W is None:
            break
        recurrent = "w_rec_in" in W
        xn = _rmsnorm_fwd(f"mix_norm_{l}", h, W["norm_mix"])
        if recurrent:
            CH = W["w_rec_in"].shape[-1]
            C = 2 * CH
            proj = _mm(f"rec_in_{l}", "nn", xn, W["w_rec_in"], grid=(S // tm, N_CHIPS),
                       a_spec=pl.BlockSpec((tm, D), lambda i, j: (i, 0)),
                       b_spec=pl.BlockSpec((None, D, CH), lambda i, j: (j, 0, 0)),
                       out_shape=(S, 2 * C), out_dtype=F32,
                       out_spec=pl.BlockSpec((tm, CH), lambda i, j: (i, j)))
            rc, rcb = _conv_fwd(f"conv_{l}", proj, W["conv_w"], W["conv_b"])
            gip, grp = _gates_fwd(f"gates_{l}", rcb, W["w_gates"], W["b_gates"])
            hrec, m = _lru_fwd(f"lru_{l}", proj, rc, gip, grp, W["lru_param"])
            h_mid = _mm_nn(f"rec_out_{l}", m, W["w_rec_out"], out_dtype=F32, res=h, tn=D)
            mix_saved = (xn, proj, rc, rcb, gip, grp, hrec, m)
        else:
            if "w_kv" in W:
                h_kv = h
                hn_kv = _rmsnorm_fwd("kv_norm", h, W["norm_kv"])
                kv = _mm_nn("kv_proj", hn_kv, W["w_kv"], out_dtype=BF16)
                f_pre = _mm_nn("f_proj", hn_kv, W["w_f"], out_dtype=F32, bias=W["b_f"])
                c = _cumsum_rows("c_cumsum", _logsig_fwd("logsig", f_pre), False)
                cfull = jnp.repeat(-c[:, :2 * HP], HEAD_DIM, axis=1)
            q = _mm_nn(f"q_proj_{l}", xn, W["w_q"], out_dtype=BF16, scale=scale)
            o, of, lse = _attn_fwd(f"attn_fwd_{l}", q, kv, cfull)
            h_mid = _mm_nn(f"o_proj_{l}", o, W["w_o"], out_dtype=F32, res=h, tn=D)
            mix_saved = (xn, q, o, of, lse)
        W = {**W, **layer_weights(l, "ffn", h_mid)}
        FH = W["w_ffn_in"].shape[-1]
        F = 2 * FH
        hn = _rmsnorm_fwd(f"ffn_norm_{l}", h_mid, W["norm_ffn"])
        z3 = _mm(f"ffn_in_{l}", "nn", hn, W["w_ffn_in"], grid=(S // tm, N_CHIPS),
                 a_spec=pl.BlockSpec((tm, D), lambda i, j: (i, 0)),
                 b_spec=pl.BlockSpec((None, D, FH), lambda i, j: (j, 0, 0)),
                 out_shape=(2, S, F), out_dtype=BF16,
                 out_spec=pl.BlockSpec((None, tm, FH), lambda i, j: (j // 2, i, j % 2)))
        act = _swiglu_fwd(f"swiglu_{l}", z3)
        h_out = _mm_nn(f"ffn_out_{l}", act, W["w_ffn_out"], out_dtype=F32, res=h_mid, tn=D)
        saved.append((W, h, h_mid, mix_saved, (hn, z3, act)))
        h = h_out
        l += 1

    dh, dhb, dg_final, loss_row = _loss_head("loss_head", h, target, norm_final)

    dk_parts, dv_parts, dc_parts = [], [], []
    token = None
    for l in reversed(range(len(saved))):
        W, h_in, h_mid, mix_saved, (hn, z3, act) = saved[l]
        recurrent = "w_rec_in" in W
        FH = W["w_ffn_in"].shape[-1]
        G = {}
        norm_ffn = W["norm_ffn"]
        if token is not None:
            norm_ffn = norm_ffn + jnp.minimum(token[:1, :1], 0.0)
        G["w_ffn_out"] = _mm_tn(f"d_ffn_out_{l}", act, dhb, out_dtype=BF16, tn=D)
        da = _mm_nt(f"d_act_{l}", dhb, W["w_ffn_out"], out_dtype=BF16, tn=FH)
        dz3 = _swiglu_bwd(f"d_swiglu_{l}", da, z3)
        G["w_ffn_in"] = _mm(
            f"d_ffn_in_{l}", "tn", hn, dz3, grid=(D // td, N_CHIPS),
            a_spec=pl.BlockSpec((S, td), lambda i, j: (0, i)),
            b_spec=pl.BlockSpec((None, S, FH), lambda i, j: (j // 2, 0, j % 2)),
            out_shape=(N_CHIPS, D, FH), out_dtype=BF16,
            out_spec=pl.BlockSpec((None, td, FH), lambda i, j: (j, i, 0)))
        token = layer_grads(l, "ffn", G)
        G = {}
        dhn = _mm(f"d_ffn_hn_{l}", "nt", dz3, W["w_ffn_in"], grid=(S // tm, 1, N_CHIPS), nk=N_CHIPS,
                  a_spec=pl.BlockSpec((None, tm, FH), lambda i, j, k: (k // 2, i, k % 2)),
                  b_spec=pl.BlockSpec((None, D, FH), lambda i, j, k: (k, 0, 0)),
                  out_shape=(S, D), out_dtype=F32, out_spec=pl.BlockSpec((tm, D), lambda i, j, k: (i, 0)))
        norm_ffn = norm_ffn + jnp.minimum(token[:1, :1], 0.0)
        dh, dhb, G["norm_ffn"] = _rmsnorm_bwd(f"d_ffn_norm_{l}", dhn, h_mid, norm_ffn, dh)
        if recurrent:
            CH = W["w_rec_in"].shape[-1]
            C = 2 * CH
            xn, proj, rc, rcb, gip, grp, hrec, m = mix_saved
            G["w_rec_out"] = _mm_tn(f"d_rec_out_{l}", m, dhb, out_dtype=BF16, tn=D)
            dm = _mm_nt(f"d_m_{l}", dhb, W["w_rec_out"], out_dtype=F32, tn=C)
            dgb, dgi, dgr, drc1, G["b_gi"], G["b_gr"], G["lru_param"] = _lru_bwd(
                f"d_lru_{l}", dm, proj, hrec, rc, gip, grp, W["lru_param"])
            drc, G["w_gates"] = _gates_bwd(f"d_gates_{l}", dgi, dgr, rcb, W["w_gates"], drc1)
            drec, G["conv_w"], G["conv_b"] = _conv_bwd(f"d_conv_{l}", drc, proj, W["conv_w"])
            dproj = jnp.concatenate([dgb, drec], axis=1)
            G["w_rec_in"] = _mm(
                f"d_rec_in_{l}", "tn", xn, dproj, grid=(1, N_CHIPS),
                a_spec=pl.BlockSpec((S, D), lambda i, j: (0, 0)),
                b_spec=pl.BlockSpec((S, CH), lambda i, j: (0, j)),
                out_shape=(N_CHIPS, D, CH), out_dtype=BF16,
                out_spec=pl.BlockSpec((None, D, CH), lambda i, j: (j, 0, 0)))
            dxn = _mm(f"d_rec_xn_{l}", "nt", dproj, W["w_rec_in"], grid=(S // tm, 1, N_CHIPS), nk=N_CHIPS,
                      a_spec=pl.BlockSpec((tm, CH), lambda i, j, k: (i, k)),
                      b_spec=pl.BlockSpec((None, D, CH), lambda i, j, k: (k, 0, 0)),
                      out_shape=(S, D), out_dtype=F32, out_spec=pl.BlockSpec((tm, D), lambda i, j, k: (i, 0)))
        else:
            xn, q, o, of, lse = mix_saved
            G["w_o"] = _mm_tn(f"d_o_proj_{l}", o, dhb, out_dtype=BF16, tn=D)
            do = _mm_nt(f"d_o_{l}", dhb, W["w_o"], out_dtype=BF16, tn=D)
            dq, dk, dv, dck, drq = _attn_bwd(f"attn_bwd_{l}", q, kv, cfull, of, do, lse)
            dk_parts.append(dk)
            dv_parts.append(dv)
            dc_parts.append(dck[:, ::HEAD_DIM] + drq.reshape(2 * HP, S).T)
            G["w_q"] = _mm_tn(f"d_q_proj_{l}", xn, dq, out_dtype=BF16, tn=D)
            dxn = _mm_nt(f"d_q_xn_{l}", dq, W["w_q"], out_dtype=F32, tn=D)
        dh, dhb, G["norm_mix"] = _rmsnorm_bwd(f"d_mix_norm_{l}", dxn, h_in, W["norm_mix"], dh)
        if "w_kv" in W:
            dkb = _add_cast("dk_sum", dk_parts, BF16)
            dvb = _add_cast("dv_sum", dv_parts, BF16)
            dkv = jnp.concatenate([dkb, dvb], axis=1)
            dc = sum(dc_parts[1:], dc_parts[0])
            dc_pad = jnp.pad(dc, ((0, 0), (0, LANES - 2 * HP)))
            dls = _cumsum_rows("dc_cumsum", dc_pad, True)
            dfb, G["b_f"] = _logsig_bwd("d_logsig", dls, f_pre)
            G["w_kv"] = _mm_tn("d_kv_proj", hn_kv, dkv, out_dtype=BF16)
            G["w_f"] = _mm_tn("d_f_proj", hn_kv, dfb, out_dtype=F32)
            dhn1 = _mm_nt("d_kv_hn", dkv, W["w_kv"], out_dtype=F32, tn=D)
            dhn2 = _mm_nt("d_f_hn", dfb, W["w_f"], out_dtype=F32, tn=D)
            dhn_kv = _add_cast("d_kv_hn_sum", [dhn1, dhn2], F32)
            dh, dhb, G["norm_kv"] = _rmsnorm_bwd("d_kv_norm", dhn_kv, h_kv, W["norm_kv"], dh)
        token = layer_grads(l, "mix", G)
    return loss_row, dh, dg_final


_ANY = pl.BlockSpec(memory_space=pl.ANY)


def _position():
    return lax.axis_index("x"), lax.axis_index("y"), lax.axis_index("c")


def _chip_peers(x, y):
    return [(1 - x, y), (x, 1 - y), (1 - x, 1 - y)]


def _half_rows(c, n):
    h = n // 2
    assert h % 16 == 0
    return pl.ds(pl.multiple_of(c * h, 16), h)


def _place_own(name, shard, layer, me):
    _, R, C = shard.shape
    tr = _row_tile(R, C, shard.dtype.itemsize)

    def body(me_ref, x_ref, o_ref):
        o_ref[...] = x_ref[...]

    return pl.pallas_call(
        body, name=name,
        grid_spec=pltpu.PrefetchScalarGridSpec(
            num_scalar_prefetch=1, grid=(R // tr,),
            in_specs=[pl.BlockSpec((None, tr, C), lambda i, me_ref: (layer, i, 0))],
            out_specs=pl.BlockSpec((None, tr, C), lambda i, me_ref: (me_ref[0], i, 0))),
        out_shape=jax.ShapeDtypeStruct((N_CHIPS, R, C), shard.dtype),
        compiler_params=_params(("parallel",)),
    )(me, shard)


def _gather_smalls(name, smalls):
    ns = len(smalls)

    def body(*refs):
        ins, outs = refs[:ns], refs[ns:2 * ns]
        send_sems, recv_sems, local_sems = refs[2 * ns:]
        x, y, c = _position()
        me = 2 * x + y
        peers = _chip_peers(x, y)

        def remote(t, k, chip):
            px, py = peers[k]
            return pltpu.make_async_remote_copy(
                src_ref=ins[t], dst_ref=outs[t].at[chip], send_sem=send_sems.at[3 * t + k],
                recv_sem=recv_sems.at[3 * t + k], device_id=(px, py, c), device_id_type=MESH)

        local = [pltpu.make_async_copy(ins[t], outs[t].at[me], local_sems.at[t]) for t in range(ns)]
        for t in range(ns):
            local[t].start()
            for k in range(3):
                remote(t, k, me).start()
        for t in range(ns):
            for k in range(3):
                px, py = peers[k]
                remote(t, k, 2 * px + py).wait_recv()
        for t in range(ns):
            for k in range(3):
                remote(t, k, me).wait_send()
            local[t].wait()

    return pl.pallas_call(
        body, name=name, in_specs=[_ANY] * ns, out_specs=[_ANY] * ns,
        out_shape=[jax.ShapeDtypeStruct((N_CHIPS,) + s.shape, s.dtype) for s in smalls],
        scratch_shapes=[pltpu.SemaphoreType.DMA((3 * ns,)), pltpu.SemaphoreType.DMA((3 * ns,)),
                        pltpu.SemaphoreType.DMA((ns,))],
    )(*smalls)


_SEM = pl.BlockSpec(memory_space=pltpu.SEMAPHORE)
_SPLIT = pltpu.CompilerParams(has_side_effects=pltpu.SideEffectType.DATAFLOW_SIDE_EFFECTING)


def _weight_copy(shards, buf, items, sems, i, k, chip_of_dst, peers, c):
    w, l = items[i]
    px, py = peers[k]
    half = _half_rows(c, shards[w].shape[1])
    return pltpu.make_async_remote_copy(
        src_ref=shards[w].at[l, half], dst_ref=buf.at[chip_of_dst, half],
        send_sem=sems[0].at[3 * i + k], recv_sem=sems[1].at[3 * i + k],
        device_id=(px, py, c), device_id_type=MESH)


def _gather_start(name, shards, bufs, items, after):
    nw, n = len(shards), len(bufs)

    def body(*refs):
        ins, outs, sems = refs[:nw], refs[nw + n + 1:nw + 2 * n + 1], refs[nw + 2 * n + 1:]
        x, y, c = _position()
        peers = _chip_peers(x, y)
        for i in range(n):
            for k in range(3):
                _weight_copy(ins, outs[i], items, sems, i, k, 2 * x + y, peers, c).start()

    res = pl.pallas_call(
        body, name=name, in_specs=[_ANY] * (nw + n + 1), out_specs=[_ANY] * n + [_SEM, _SEM],
        out_shape=[jax.ShapeDtypeStruct(b.shape, b.dtype) for b in bufs]
        + [pltpu.SemaphoreType.DMA((3 * n,)), pltpu.SemaphoreType.DMA((3 * n,))],
        input_output_aliases={nw + i: i for i in range(n)}, compiler_params=_SPLIT,
    )(*shards, *bufs, after)
    return res[:n], res[n:]


def _gather_wait(name, shards, bufs, items, ids, sems, after):
    nw, m = len(shards), len(ids)

    def body(*refs):
        ins, bs = refs[:nw], refs[nw:nw + m]
        sem_refs = refs[nw + m:nw + m + 2]
        x, y, c = _position()
        peers = _chip_peers(x, y)
        for j, i in enumerate(ids):
            for k in range(3):
                px, py = peers[k]
                _weight_copy(ins, bs[j], items, sem_refs, i, k, 2 * px + py, peers, c).wait_recv()
        for j, i in enumerate(ids):
            for k in range(3):
                _weight_copy(ins, bs[j], items, sem_refs, i, k, 2 * x + y, peers, c).wait_send()

    res = pl.pallas_call(
        body, name=name, in_specs=[_ANY] * (nw + m) + [_SEM, _SEM, _ANY], out_specs=[_ANY] * m,
        out_shape=[jax.ShapeDtypeStruct(bufs[i].shape, bufs[i].dtype) for i in ids],
        input_output_aliases={nw + j: j for j in range(m)}, compiler_params=_SPLIT,
    )(*shards, *[bufs[i] for i in ids], *sems, after)
    return list(res)


def _gather_d2d(name, bufs):
    n = len(bufs)

    def body(*refs):
        ins, outs = refs[:n], refs[n:2 * n]
        send_sems, recv_sems = refs[2 * n:]
        x, y, c = _position()
        peers = _chip_peers(x, y)

        def remote(i, k, core):
            px, py = peers[k]
            half = _half_rows(core, ins[i].shape[1])
            return pltpu.make_async_remote_copy(
                src_ref=ins[i].at[2 * px + py, half], dst_ref=outs[i].at[2 * px + py, half],
                send_sem=send_sems.at[3 * i + k], recv_sem=recv_sems.at[3 * i + k],
                device_id=(x, y, 1 - c), device_id_type=MESH)

        for i in range(n):
            for k in range(3):
                remote(i, k, c).start()
        for i in range(n):
            for k in range(3):
                remote(i, k, 1 - c).wait_recv()
        for i in range(n):
            for k in range(3):
                remote(i, k, c).wait_send()

    return list(pl.pallas_call(
        body, name=name, in_specs=[_ANY] * n, out_specs=[_ANY] * n,
        out_shape=[jax.ShapeDtypeStruct(g.shape, g.dtype) for g in bufs],
        input_output_aliases={i: i for i in range(n)},
        scratch_shapes=[pltpu.SemaphoreType.DMA((3 * n,)), pltpu.SemaphoreType.DMA((3 * n,))],
    )(*bufs))


def _reduce_d2d(name, grads):
    n = len(grads)

    def body(*refs):
        ins, outs = refs[:n], refs[n:2 * n]
        send_sems, recv_sems = refs[2 * n:]
        x, y, c = _position()
        remote = [pltpu.make_async_remote_copy(
            src_ref=ins[i].at[:, _half_rows(1 - c, ins[i].shape[1])], dst_ref=outs[i],
            send_sem=send_sems.at[i], recv_sem=recv_sems.at[i],
            device_id=(x, y, 1 - c), device_id_type=MESH) for i in range(n)]
        for cp in remote:
            cp.start()
        for cp in remote:
            cp.wait_recv()
        for cp in remote:
            cp.wait_send()

    return pl.pallas_call(
        body, name=name, in_specs=[_ANY] * n, out_specs=[_ANY] * n,
        out_shape=[jax.ShapeDtypeStruct((N_CHIPS, g.shape[1] // 2, g.shape[2]), g.dtype) for g in grads],
        scratch_shapes=[pltpu.SemaphoreType.DMA((n,)), pltpu.SemaphoreType.DMA((n,))],
    )(*grads)


def _sum_cores(name, g, other, core):
    _, R, C = g.shape
    H = R // 2
    tr = _row_tile(H, C)
    nb = H // tr

    def body(c_ref, g_ref, o_ref, out_ref):
        out_ref[...] = (g_ref[...].astype(F32) + o_ref[...].astype(F32)).astype(out_ref.dtype)

    return pl.pallas_call(
        body, name=name,
        grid_spec=pltpu.PrefetchScalarGridSpec(
            num_scalar_prefetch=1, grid=(N_CHIPS, nb),
            in_specs=[pl.BlockSpec((None, tr, C), lambda j, i, c_ref: (j, c_ref[0] * nb + i, 0)),
                      pl.BlockSpec((None, tr, C), lambda j, i, c_ref: (j, i, 0))],
            out_specs=pl.BlockSpec((None, tr, C), lambda j, i, c_ref: (j, i, 0))),
        out_shape=jax.ShapeDtypeStruct((N_CHIPS, H, C), BF16),
        compiler_params=_params(("parallel", "parallel")),
    )(core, g, other)


def _sum_chips(name, received, own, full, layer, me_core):
    _, H, C = received.shape
    tr = _row_tile(H, C)
    nb = H // tr

    def body(s_ref, r_ref, own_ref, full_ref, out_ref):
        acc = r_ref[0].astype(F32)
        for k in (1, 2):
            acc = acc + r_ref[k].astype(F32)
        out_ref[...] = acc + own_ref[...].astype(F32)

    return pl.pallas_call(
        body, name=name,
        grid_spec=pltpu.PrefetchScalarGridSpec(
            num_scalar_prefetch=1, grid=(nb,),
            in_specs=[pl.BlockSpec((3, tr, C), lambda i, s_ref: (0, i, 0)),
                      pl.BlockSpec((None, tr, C), lambda i, s_ref: (s_ref[0], i, 0)),
                      _ANY],
            out_specs=pl.BlockSpec((None, tr, C), lambda i, s_ref: (layer, s_ref[1] * nb + i, 0))),
        out_shape=jax.ShapeDtypeStruct(full.shape, full.dtype),
        input_output_aliases={3: 0},
        compiler_params=_params(("parallel",)),
    )(me_core, received, own, full)


def _part_copy(parts, recv, sems, i, k, peers, c):
    px, py = peers[k]
    return pltpu.make_async_remote_copy(
        src_ref=parts[i].at[2 * px + py], dst_ref=recv[i].at[k],
        send_sem=sems[0].at[3 * i + k], recv_sem=sems[1].at[3 * i + k],
        device_id=(px, py, c), device_id_type=MESH)


def _scatter_start(name, parts):
    n = len(parts)

    def body(*refs):
        ins, outs, sems, token = refs[:n], refs[n:2 * n], refs[2 * n:2 * n + 2], refs[2 * n + 2]
        x, y, c = _position()
        peers = _chip_peers(x, y)
        for i in range(n):
            for k in range(3):
                _part_copy(ins, outs, sems, i, k, peers, c).start()
        token[...] = jnp.zeros_like(token)

    res = pl.pallas_call(
        body, name=name, in_specs=[_ANY] * n,
        out_specs=[_ANY] * n + [_SEM, _SEM, pl.BlockSpec(memory_space=pltpu.VMEM)],
        out_shape=[jax.ShapeDtypeStruct((3,) + p.shape[1:], p.dtype) for p in parts]
        + [pltpu.SemaphoreType.DMA((3 * n,)), pltpu.SemaphoreType.DMA((3 * n,)),
           jax.ShapeDtypeStruct((SUBLANES, LANES), F32)],
        compiler_params=_SPLIT,
    )(*parts)
    return list(res[:n]), res[n:n + 2], res[n + 2]


def _scatter_wait(name, parts, recv, sems):
    n = len(parts)

    def body(*refs):
        ins, rs, sem_refs = refs[:n], refs[n:2 * n], refs[2 * n:2 * n + 2]
        x, y, c = _position()
        peers = _chip_peers(x, y)
        for i in range(n):
            for k in range(3):
                _part_copy(ins, rs, sem_refs, i, k, peers, c).wait_recv()
        for i in range(n):
            for k in range(3):
                _part_copy(ins, rs, sem_refs, i, k, peers, c).wait_send()

    return list(pl.pallas_call(
        body, name=name, in_specs=[_ANY] * (2 * n) + [_SEM, _SEM], out_specs=[_ANY] * n,
        out_shape=[jax.ShapeDtypeStruct(r.shape, r.dtype) for r in recv],
        input_output_aliases={n + i: i for i in range(n)}, compiler_params=_SPLIT,
    )(*parts, *recv, *sems))


def _share_d2d(name, full):
    n = len(full)

    def body(*refs):
        ins, outs = refs[:n], refs[n:2 * n]
        send_sems, recv_sems = refs[2 * n:]
        x, y, c = _position()

        def remote(w, core):
            half = _half_rows(core, ins[w].shape[1])
            return pltpu.make_async_remote_copy(
                src_ref=ins[w].at[:, half], dst_ref=outs[w].at[:, half],
                send_sem=send_sems.at[w], recv_sem=recv_sems.at[w],
                device_id=(x, y, 1 - c), device_id_type=MESH)

        for w in range(n):
            remote(w, c).start()
        for w in range(n):
            remote(w, 1 - c).wait_recv()
        for w in range(n):
            remote(w, c).wait_send()

    return pl.pallas_call(
        body, name=name, in_specs=[_ANY] * n, out_specs=[_ANY] * n,
        out_shape=[jax.ShapeDtypeStruct(f.shape, f.dtype) for f in full],
        input_output_aliases={w: w for w in range(n)},
        scratch_shapes=[pltpu.SemaphoreType.DMA((n,)), pltpu.SemaphoreType.DMA((n,))],
    )(*full)


def _gather_all(name, a):
    def body(a_ref, o_ref, send_sems, recv_sems, local_sem):
        x, y, c = _position()
        me = 4 * x + 2 * y + c

        def peer(k):
            return (x ^ ((k >> 2) & 1), y ^ ((k >> 1) & 1), c ^ (k & 1))

        def remote(k, slot):
            return pltpu.make_async_remote_copy(
                src_ref=a_ref, dst_ref=o_ref.at[slot], send_sem=send_sems.at[k - 1], recv_sem=recv_sems.at[k - 1],
                device_id=peer(k), device_id_type=MESH)

        local = pltpu.make_async_copy(a_ref, o_ref.at[me], local_sem)
        local.start()
        for k in range(1, N_DEV):
            remote(k, me).start()
        for k in range(1, N_DEV):
            px, py, pc = peer(k)
            remote(k, 4 * px + 2 * py + pc).wait_recv()
        for k in range(1, N_DEV):
            remote(k, me).wait_send()
        local.wait()

    return pl.pallas_call(
        body, name=name, in_specs=[_ANY], out_specs=_ANY,
        out_shape=jax.ShapeDtypeStruct((N_DEV,) + a.shape, a.dtype),
        scratch_shapes=[pltpu.SemaphoreType.DMA((N_DEV - 1,)), pltpu.SemaphoreType.DMA((N_DEV - 1,)),
                        pltpu.SemaphoreType.DMA],
    )(a)


def _rows2d(a, lead=0):
    return a.reshape(a.shape[:lead] + (-1, a.shape[-1]))


def _row_tile(rows, cols, itemsize=4, target=1 << 20):
    want = max(SUBLANES, target // (cols * itemsize))
    t = min(rows, (want // 16) * 16)
    while t > 16 and rows % t:
        t -= 16
    return t if rows % t == 0 else rows


def _sum_slots(name, r, out_dtype=F32):
    ns = r.shape[0]
    r2 = _rows2d(r, 1)
    _, rows, cols = r2.shape
    tr = _row_tile(rows, cols)

    def body(r_ref, o_ref):
        acc = r_ref[0].astype(F32)
        for s in range(1, ns):
            acc = acc + r_ref[s].astype(F32)
        o_ref[...] = acc.astype(o_ref.dtype)

    out = pl.pallas_call(
        body, name=name, grid=(rows // tr,),
        in_specs=[pl.BlockSpec((ns, tr, cols), lambda i: (0, i, 0))],
        out_specs=pl.BlockSpec((tr, cols), lambda i: (i, 0)),
        out_shape=jax.ShapeDtypeStruct((rows, cols), out_dtype),
        compiler_params=_params(("parallel",)),
    )(r2)
    return out.reshape(r.shape[1:])


def _adamw(name, g_parts, w, m, v):
    shape = w.shape
    ng = len(g_parts)
    args = [_rows2d(a) for a in (*g_parts, w, m, v)]
    rows, cols = args[0].shape
    tr = _row_tile(rows, cols, target=1 << 19)
    c1 = 1.0 - ADAM_B1 ** ADAM_STEP
    c2 = 1.0 - ADAM_B2 ** ADAM_STEP

    def body(*refs):
        g = refs[0][...]
        for r in refs[1:ng]:
            g = g + r[...]
        w_ref, m_ref, v_ref = refs[ng:ng + 3]
        g_out, d_out, m_out, v_out = refs[ng + 3:]
        mn = ADAM_B1 * m_ref[...] + (1.0 - ADAM_B1) * g
        vn = ADAM_B2 * v_ref[...] + (1.0 - ADAM_B2) * (g * g)
        m_hat = mn / c1
        v_hat = vn / c2
        g_out[...] = g
        d_out[...] = -ADAM_LR * (m_hat / (jnp.sqrt(v_hat) + ADAM_EPS) + ADAM_WD * w_ref[...])
        m_out[...] = mn
        v_out[...] = vn

    spec = pl.BlockSpec((tr, cols), lambda i: (i, 0))
    outs = pl.pallas_call(
        body, name=name, grid=(rows // tr,), in_specs=[spec] * (ng + 3), out_specs=[spec] * 4,
        out_shape=[jax.ShapeDtypeStruct((rows, cols), F32)] * 4,
        compiler_params=_params(("parallel",)),
    )(*args)
    return tuple(o.reshape(shape) for o in outs)


_WEIGHTS = ["norm_mix", "norm_ffn", "w_ffn_in", "w_ffn_out", "w_rec_in", "conv_w", "conv_b", "w_lru_gates",
            "b_lru_gates", "lru_param", "w_rec_out", "norm_kv", "w_kvf", "b_forget", "w_q", "w_o", "norm_final"]
_BIG = ["w_ffn_in", "w_ffn_out", "w_rec_in", "w_lru_gates", "w_rec_out", "w_kvf", "w_q", "w_o"]


def _stack3(a):
    return a[None] if a.ndim == 2 else a.reshape(a.shape[0], -1, a.shape[-1])


def _pad_lanes(a, n):
    return jnp.pad(a, ((0, 0),) * (a.ndim - 1) + ((0, n - a.shape[-1]),))


def kernel(x, norm_mix, norm_ffn, w_ffn_in, w_ffn_out, w_rec_in, conv_w, conv_b, w_lru_gates, b_lru_gates, lru_param, w_rec_out, norm_kv, w_kvf, b_forget, w_q, w_o, norm_final, loss_target, m_norm_mix, m_norm_ffn, m_w_ffn_in, m_w_ffn_out, m_w_rec_in, m_conv_w, m_conv_b, m_w_lru_gates, m_b_lru_gates, m_lru_param, m_w_rec_out, m_norm_kv, m_w_kvf, m_b_forget, m_w_q, m_w_o, m_norm_final, v_norm_mix, v_norm_ffn, v_w_ffn_in, v_w_ffn_out, v_w_rec_in, v_conv_w, v_conv_b, v_w_lru_gates, v_b_lru_gates, v_lru_param, v_w_rec_out, v_norm_kv, v_w_kvf, v_b_forget, v_w_q, v_w_o, v_norm_final):
    P = dict(norm_mix=norm_mix, norm_ffn=norm_ffn, w_ffn_in=w_ffn_in, w_ffn_out=w_ffn_out, w_rec_in=w_rec_in,
             conv_w=conv_w, conv_b=conv_b, w_lru_gates=w_lru_gates, b_lru_gates=b_lru_gates, lru_param=lru_param,
             w_rec_out=w_rec_out, norm_kv=norm_kv, w_kvf=w_kvf, b_forget=b_forget, w_q=w_q, w_o=w_o,
             norm_final=norm_final)
    M1 = dict(norm_mix=m_norm_mix, norm_ffn=m_norm_ffn, w_ffn_in=m_w_ffn_in, w_ffn_out=m_w_ffn_out,
              w_rec_in=m_w_rec_in, conv_w=m_conv_w, conv_b=m_conv_b, w_lru_gates=m_w_lru_gates,
              b_lru_gates=m_b_lru_gates, lru_param=m_lru_param, w_rec_out=m_w_rec_out, norm_kv=m_norm_kv,
              w_kvf=m_w_kvf, b_forget=m_b_forget, w_q=m_w_q, w_o=m_w_o, norm_final=m_norm_final)
    M2 = dict(norm_mix=v_norm_mix, norm_ffn=v_norm_ffn, w_ffn_in=v_w_ffn_in, w_ffn_out=v_w_ffn_out,
              w_rec_in=v_w_rec_in, conv_w=v_conv_w, conv_b=v_conv_b, w_lru_gates=v_w_lru_gates,
              b_lru_gates=v_b_lru_gates, lru_param=v_lru_param, w_rec_out=v_w_rec_out, norm_kv=v_norm_kv,
              w_kvf=v_w_kvf, b_forget=v_b_forget, w_q=v_w_q, w_o=v_w_o, norm_final=v_norm_final)

    _, S, D = x.shape
    L = norm_mix.shape[0]
    NA, NBLK, BW, GS = w_lru_gates.shape
    NB = w_q.shape[0]
    C = NBLK * BW
    CS = C // N_CHIPS
    H = b_forget.shape[0]
    assert C == D and H * HEAD_DIM == D and H <= LANES
    chip = 2 * lax.axis_index("x") + lax.axis_index("y")

    small_a = jnp.concatenate([conv_w, conv_b[:, None], lru_param[:, None]], axis=1)
    small_a, b_gates = _gather_smalls("gather_smalls", [small_a, b_lru_gates])
    small_a = small_a.transpose(1, 2, 0, 3).reshape(NA, 6, C)
    b_gates = b_gates.transpose(1, 2, 0, 3).reshape(NA, NBLK, 1, N_CHIPS * GS)
    shards = [_stack3(P[w]).astype(BF16) for w in _BIG]
    core = lax.axis_index("c")
    chip_id = jnp.reshape(chip, (1,)).astype(jnp.int32)
    core_id = jnp.reshape(core, (1,)).astype(jnp.int32)
    me_core = jnp.stack([chip, core]).astype(jnp.int32)

    def stage_items(l, part):
        if part == "ffn":
            return [(_BIG.index("w_ffn_in"), l), (_BIG.index("w_ffn_out"), l)]
        if l < NA:
            names, at = ["w_rec_in", "w_lru_gates", "w_rec_out"], l
        else:
            names, at = (["w_kvf"] if l == NA else []) + ["w_q", "w_o"], l - NA
        return [(_BIG.index(n), 0 if n == "w_kvf" else at) for n in names]

    stages = [(l, part) for l in range(L) for part in ("mix", "ffn")]
    items = [it for st in stages for it in stage_items(*st)]
    ids_of = {st: [items.index(it) for it in stage_items(*st)] for st in stages}
    bufs = [_place_own(f"place_{_BIG[w]}_{li}", shards[w], li, chip_id) for w, li in items]
    bufs, gather_sems = _gather_start("gather_start", shards, bufs, items, small_a)

    def layer_weights(l, part, after):
        if l >= L:
            return None
        ids = ids_of[(l, part)]
        got = _gather_wait(f"gather_wait_{part}_{l}", shards, bufs, items, ids, gather_sems, after)
        got = _gather_d2d(f"gather_d2d_{part}_{l}", got)
        B = {_BIG[items[i][0]]: g for i, g in zip(ids, got)}
        if part == "ffn":
            return dict(norm_ffn=norm_ffn[l][None], w_ffn_in=B["w_ffn_in"], w_ffn_out=B["w_ffn_out"].reshape(-1, D))
        W = dict(norm_mix=norm_mix[l][None])
        if l < NA:
            W.update(w_rec_in=B["w_rec_in"],
                     w_gates=B["w_lru_gates"].reshape(N_CHIPS, NBLK, BW, GS).transpose(1, 2, 0, 3).reshape(
                         NBLK, BW, N_CHIPS * GS),
                     b_gates=b_gates[l], w_rec_out=B["w_rec_out"].reshape(C, D),
                     conv_w=small_a[l, :4], conv_b=small_a[l, 4:5], lru_param=small_a[l, 5:6])
        else:
            W.update(w_q=B["w_q"].reshape(D, D), w_o=B["w_o"].reshape(D, D))
            if l == NA:
                w_kvf_full = B["w_kvf"].transpose(1, 0, 2).reshape(D, -1)
                W.update(norm_kv=norm_kv[None], w_kv=w_kvf_full[:, :2 * D],
                         w_f=_pad_lanes(w_kvf_full[:, 2 * D:], LANES), b_f=_pad_lanes(b_forget[None], LANES))
        return W

    G_small = {l: {} for l in range(L)}
    pending = {}

    def layer_grads(l, part, G):
        G_small[l].update(G)
        by_name = dict(
            w_ffn_in=lambda: G["w_ffn_in"], w_ffn_out=lambda: G["w_ffn_out"].reshape(N_CHIPS, -1, D),
            w_rec_in=lambda: G["w_rec_in"],
            w_lru_gates=lambda: G["w_gates"].reshape(NBLK, BW, N_CHIPS, GS).transpose(2, 0, 1, 3).reshape(
                N_CHIPS, NBLK * BW, GS),
            w_rec_out=lambda: G["w_rec_out"].reshape(N_CHIPS, -1, D),
            w_kvf=lambda: jnp.concatenate([G["w_kv"].astype(F32), G["w_f"][:, :H]], axis=1).reshape(
                D, N_CHIPS, -1).transpose(1, 0, 2).astype(BF16),
            w_q=lambda: G["w_q"].reshape(N_CHIPS, -1, D), w_o=lambda: G["w_o"].reshape(N_CHIPS, -1, D))
        its = stage_items(l, part)
        grads = [by_name[_BIG[w]]() for w, _ in its]
        others = _reduce_d2d(f"reduce_d2d_{part}_{l}", grads)
        parts = [_sum_cores(f"sum_cores_{l}_{_BIG[w]}", g, o, core_id) for (w, _), g, o in zip(its, grads, others)]
        recv, sems, token = _scatter_start(f"scatter_start_{part}_{l}", parts)
        pending[(l, part)] = (parts, recv, sems)
        return token

    loss_row, grad_x, dg_final = _local_step(x.reshape(S, D), loss_target.reshape(S, D), norm_final[None],
                                             layer_weights, layer_grads)

    rows = [*[G_small[l]["norm_mix"] for l in range(L)], *[G_small[l]["norm_ffn"] for l in range(L)],
            G_small[NA]["norm_kv"], dg_final, _pad_lanes(G_small[NA]["b_f"], D), _pad_lanes(loss_row, D)]
    for a in range(NA):
        rows += [G_small[a][n] for n in ("conv_w", "conv_b", "b_gi", "b_gr", "lru_param")]
    packed = jnp.concatenate(rows, axis=0)
    tot = _sum_slots("sum_small", _gather_all("gather_small", packed))
    loss = tot[2 * L + 3, 0]
    g_rep = jnp.concatenate([tot[:2 * L + 2], tot[2 * L + 2:2 * L + 3]], axis=0)
    base = 2 * L + 4
    g_sh = []
    for a in range(NA):
        blk = lax.dynamic_slice_in_dim(tot[base + 8 * a:base + 8 * a + 8], chip * CS, CS, axis=1)
        gi = tot[base + 8 * a + 5].reshape(NBLK, BW)
        gr = tot[base + 8 * a + 6].reshape(NBLK, BW)
        bl = lax.dynamic_slice_in_dim(jnp.concatenate([gi, gr], axis=1), chip * GS, GS, axis=1)
        g_sh += [blk[:5], bl.reshape(-1, CS), blk[7:8]]
    g_sh = jnp.concatenate(g_sh, axis=0)
    nrow = g_sh.shape[0] // NA

    def pack_rep(T):
        return jnp.concatenate([T["norm_mix"], T["norm_ffn"], T["norm_kv"][None], T["norm_final"][None],
                                _pad_lanes(T["b_forget"][None], D)], axis=0)

    def pack_sh(T):
        return jnp.concatenate([jnp.concatenate([T["conv_w"][a], T["conv_b"][a][None],
                                                 T["b_lru_gates"][a].reshape(-1, CS), T["lru_param"][a][None]], axis=0)
                                for a in range(NA)], axis=0)

    rep = _adamw("adamw_replicated", [g_rep], pack_rep(P), pack_rep(M1), pack_rep(M2))
    shd = _adamw("adamw_small_sharded", [g_sh], pack_sh(P), pack_sh(M1), pack_sh(M2))

    def unpack_rep(t):
        return dict(norm_mix=t[:L], norm_ffn=t[L:2 * L], norm_kv=t[2 * L], norm_final=t[2 * L + 1],
                    b_forget=t[2 * L + 2, :H])

    def unpack_sh(t):
        t = t.reshape(NA, nrow, CS)
        return dict(conv_w=t[:, :4], conv_b=t[:, 4], b_lru_gates=t[:, 5:nrow - 1].reshape(NA, NBLK, GS),
                    lru_param=t[:, nrow - 1])

    full = [lax.empty(sh.shape, F32) for sh in shards]
    for l, part in reversed(stages):
        parts, recv, sems = pending[(l, part)]
        recv = _scatter_wait(f"scatter_wait_{part}_{l}", parts, recv, sems)
        for (w, li), own, r in zip(stage_items(l, part), parts, recv):
            full[w] = _sum_chips(f"sum_chips_{l}_{_BIG[w]}", r, own, full[w], li, me_core)
    full = _share_d2d("share_d2d", full)
    big = {w: _adamw(f"adamw_{w}", [g.reshape(P[w].shape)], P[w], M1[w], M2[w]) for w, g in zip(_BIG, full)}

    outs = []
    for i in range(4):
        small = {**unpack_rep(rep[i]), **unpack_sh(shd[i])}
        outs.append([big[w][i] if w in big else small[w] for w in _WEIGHTS])
    return (loss, grad_x.reshape(1, S, D), *outs[0], *outs[1], *outs[2], *outs[3])
```

```python
import functools
import math

import jax
import jax.numpy as jnp
from jax import lax
from jax.experimental import pallas as pl
from jax.experimental.pallas import tpu as pltpu

F32 = jnp.float32
BF16 = jnp.bfloat16

EPS = 1e-6
LRU_C = 8.0
HEAD_DIM = 64
LANES = 128
SUBLANES = 8
VMEM_LIMIT = 48 * 1024 * 1024
N_CHIPS = 4
N_DEV = 8

ADAM_LR = 0.001
ADAM_B1 = 0.9
ADAM_B2 = 0.999
ADAM_EPS = 1e-08
ADAM_WD = 0.01
ADAM_STEP = 10

_NN = (((1,), (0,)), ((), ()))
_NT = (((1,), (1,)), ((), ()))
_TN = (((0,), (0,)), ((), ()))
_DN = {"nn": _NN, "nt": _NT, "tn": _TN}
MESH = pl.DeviceIdType.MESH


def _params(sem):
    return pltpu.CompilerParams(dimension_semantics=sem, vmem_limit_bytes=VMEM_LIMIT)


def _tile(n, want):
    if n <= want:
        return n
    t = (want // LANES) * LANES
    while t >= LANES:
        if n % t == 0:
            return t
        t -= LANES
    return n


def _sigmoid(x):
    return 1.0 / (1.0 + jnp.exp(-x))


def _softplus(x):
    return jnp.maximum(x, 0.0) + jnp.log(1.0 + jnp.exp(-jnp.abs(x)))


_GELU_C = math.sqrt(2.0 / math.pi)


def _gelu_and_grad(x):
    inner = _GELU_C * (x + 0.044715 * x * x * x)
    t = jnp.tanh(inner)
    g = 0.5 * x * (1.0 + t)
    dg = 0.5 * (1.0 + t) + 0.5 * x * (1.0 - t * t) * _GELU_C * (1.0 + 3.0 * 0.044715 * x * x)
    return g, dg


def _rms(x):
    return lax.rsqrt(jnp.mean(x * x, axis=-1, keepdims=True) + EPS)


def _rms_bwd(dy, x, g):
    r = _rms(x)
    xr = x * r
    dyg = dy * g
    return r * dyg - xr * (r * jnp.mean(dyg * xr, axis=-1, keepdims=True)), jnp.sum(dy * xr, axis=0, keepdims=True)


def _mm(name, mode, a, b, *, grid, a_spec, b_spec, out_shape, out_dtype, out_spec, nk=1,
        res=None, res_spec=None, bias=None, bias_spec=None, scale=None, norm_gain=None, norm_bwd=None):
    dn = _DN[mode]
    has_res, has_bias = res is not None, bias is not None
    blk = tuple(d for d in out_spec.block_shape if d is not None)
    vec = pl.BlockSpec((1, blk[-1]), lambda *g: (0, 0))
    n_in = 2 + int(has_res) + int(has_bias) + (1 if norm_gain is not None else 0) + (3 if norm_bwd else 0)

    def body(*refs):
        a_ref, b_ref = refs[0], refs[1]
        p = 2
        r_ref = refs[p] if has_res else None
        p += int(has_res)
        bias_ref = refs[p] if has_bias else None
        p += int(has_bias)
        extra = refs[p:n_in]
        outs = refs[n_in:]
        o_ref = outs[0]
        part = lax.dot_general(a_ref[...], b_ref[...], dn, preferred_element_type=F32)

        def finish(acc):
            if scale is not None:
                acc = acc * scale
            if has_bias:
                acc = acc + bias_ref[...]
            if has_res:
                acc = r_ref[...] + acc
            if norm_bwd:
                h_ref, g_ref, dh_ref = extra
                dx, dg = _rms_bwd(acc, h_ref[...], g_ref[...])
                acc = dh_ref[...] + dx
                outs[1][...] = acc.astype(BF16)
                outs[2][...] = dg
            if norm_gain is not None:
                outs[1][...] = (acc * _rms(acc) * extra[0][...]).astype(BF16)
            o_ref[...] = acc.astype(o_ref.dtype)

        if nk == 1:
            finish(part)
        else:
            acc_ref = refs[-1]
            k = pl.program_id(2)

            @pl.when(k == 0)
            def _():
                acc_ref[...] = part

            @pl.when(k > 0)
            def _():
                acc_ref[...] += part

            @pl.when(k == nk - 1)
            def _():
                finish(acc_ref[...])

    ins, specs = [a, b], [a_spec, b_spec]
    if has_res:
        ins.append(res)
        specs.append(res_spec)
    if has_bias:
        ins.append(bias)
        specs.append(bias_spec)
    out_specs, out_shapes = [out_spec], [jax.ShapeDtypeStruct(out_shape, out_dtype)]
    if norm_gain is not None:
        ins.append(norm_gain)
        specs.append(vec)
        out_specs.append(out_spec)
        out_shapes.append(jax.ShapeDtypeStruct(out_shape, BF16))
    if norm_bwd:
        h, g, dh = norm_bwd
        ins += [h, g, dh]
        specs += [out_spec, vec, out_spec]
        out_specs += [out_spec, pl.BlockSpec((None, 1, blk[-1]), lambda i, *rest: (i, 0, 0))]
        out_shapes += [jax.ShapeDtypeStruct(out_shape, BF16), jax.ShapeDtypeStruct((grid[0], 1, blk[-1]), F32)]
    sem = ("parallel", "parallel") + (("arbitrary",) if len(grid) == 3 else ())
    single = len(out_specs) == 1
    return pl.pallas_call(
        body, name=name, grid=grid, in_specs=specs, out_specs=out_specs[0] if single else out_specs,
        out_shape=out_shapes[0] if single else out_shapes,
        scratch_shapes=[pltpu.VMEM(blk, F32)] if nk > 1 else [],
        compiler_params=_params(sem),
    )(*ins)


def _mm_nn(name, a, b, *, b_lead=(), out_dtype, tm=512, tn=512, res=None, bias=None, scale=None, norm_gain=None):
    M, K = a.shape
    N = b.shape[-1]
    tm, tn = _tile(M, tm), _tile(N, tn)
    nl = len(b_lead)
    return _mm(
        name, "nn", a, b, grid=(M // tm, N // tn),
        a_spec=pl.BlockSpec((tm, K), lambda i, j: (i, 0)),
        b_spec=pl.BlockSpec((None,) * nl + (K, tn), lambda i, j: tuple(b_lead) + (0, j)),
        out_shape=(M, N), out_dtype=out_dtype, out_spec=pl.BlockSpec((tm, tn), lambda i, j: (i, j)),
        res=res, res_spec=pl.BlockSpec((tm, tn), lambda i, j: (i, j)),
        bias=bias, bias_spec=pl.BlockSpec((1, tn), lambda i, j: (0, j)), scale=scale, norm_gain=norm_gain)


def _mm_nt(name, a, b, *, b_lead=(), out_dtype, tm=512, tn=512, tk=2048, res=None, norm_bwd=None):
    M, K = a.shape
    N = b.shape[-2]
    tm, tn, tk = _tile(M, tm), _tile(N, tn), _tile(K, tk)
    nk = K // tk
    nl = len(b_lead)
    return _mm(
        name, "nt", a, b, grid=(M // tm, N // tn, nk), nk=nk,
        a_spec=pl.BlockSpec((tm, tk), lambda i, j, k: (i, k)),
        b_spec=pl.BlockSpec((None,) * nl + (tn, tk), lambda i, j, k: tuple(b_lead) + (j, k)),
        out_shape=(M, N), out_dtype=out_dtype, out_spec=pl.BlockSpec((tm, tn), lambda i, j, k: (i, j)),
        res=res, res_spec=pl.BlockSpec((tm, tn), lambda i, j, k: (i, j)), norm_bwd=norm_bwd)


def _mm_tn(name, a, b, *, out_dtype, tm=512, tn=512):
    S, M = a.shape
    N = b.shape[1]
    tm, tn = _tile(M, tm), _tile(N, tn)
    return _mm(
        name, "tn", a, b, grid=(M // tm, N // tn),
        a_spec=pl.BlockSpec((S, tm), lambda i, j: (0, i)),
        b_spec=pl.BlockSpec((S, tn), lambda i, j: (0, j)),
        out_shape=(M, N), out_dtype=out_dtype, out_spec=pl.BlockSpec((tm, tn), lambda i, j: (i, j)))


def _rmsnorm_fwd(name, h, g, tr=256):
    S, D = h.shape
    tr = _tile(S, tr)

    def body(h_ref, g_ref, o_ref):
        x = h_ref[...]
        r = lax.rsqrt(jnp.mean(x * x, axis=-1, keepdims=True) + EPS)
        o_ref[...] = (x * r * g_ref[...]).astype(o_ref.dtype)

    return pl.pallas_call(
        body, name=name, grid=(S // tr,),
        in_specs=[pl.BlockSpec((tr, D), lambda i: (i, 0)), pl.BlockSpec((1, D), lambda i: (0, 0))],
        out_specs=pl.BlockSpec((tr, D), lambda i: (i, 0)),
        out_shape=jax.ShapeDtypeStruct((S, D), BF16),
        compiler_params=_params(("parallel",)),
    )(h, g)


def _loss_head(name, h, target, g, tr=256):
    S, D = h.shape
    tr = _tile(S, tr)

    def body(h_ref, t_ref, g_ref, o_ref, ob_ref, dg_ref, loss_ref):
        i = pl.program_id(0)
        x = h_ref[...]
        gg = g_ref[...]
        r = lax.rsqrt(jnp.mean(x * x, axis=-1, keepdims=True) + EPS)
        xr = x * r
        err = xr * gg - t_ref[...]
        lpart = 0.5 * jnp.sum(jnp.mean(err * err, axis=-1, keepdims=True), axis=0, keepdims=True)
        dy = err * (1.0 / D)
        dyg = dy * gg
        dx = r * dyg - xr * (r * jnp.mean(dyg * xr, axis=-1, keepdims=True))
        o_ref[...] = dx
        ob_ref[...] = dx.astype(BF16)
        part = jnp.sum(dy * xr, axis=0, keepdims=True)
        lrow = jnp.broadcast_to(lpart, (1, LANES))

        @pl.when(i == 0)
        def _():
            dg_ref[...] = part
            loss_ref[...] = lrow

        @pl.when(i > 0)
        def _():
            dg_ref[...] += part
            loss_ref[...] += lrow

    row = pl.BlockSpec((tr, D), lambda i: (i, 0))
    vec = pl.BlockSpec((1, D), lambda i: (0, 0))
    return pl.pallas_call(
        body, name=name, grid=(S // tr,),
        in_specs=[row, row, vec], out_specs=[row, row, vec, pl.BlockSpec((1, LANES), lambda i: (0, 0))],
        out_shape=[jax.ShapeDtypeStruct((S, D), F32), jax.ShapeDtypeStruct((S, D), BF16),
                   jax.ShapeDtypeStruct((1, D), F32), jax.ShapeDtypeStruct((1, LANES), F32)],
        compiler_params=_params(("arbitrary",)),
    )(h, target, g)


def _swiglu_fwd(name, hn, w_in, tm=512):
    S, D = hn.shape
    FH = w_in.shape[-1]
    tm = _tile(S, tm)

    def body(x_ref, wg_ref, wu_ref, z_ref, a_ref):
        x = x_ref[...]
        zg = jnp.dot(x, wg_ref[...], preferred_element_type=F32)
        zu = jnp.dot(x, wu_ref[...], preferred_element_type=F32)
        z_ref[0] = zg.astype(z_ref.dtype)
        z_ref[1] = zu.astype(z_ref.dtype)
        a_ref[...] = (zg * _sigmoid(zg) * zu).astype(a_ref.dtype)

    return pl.pallas_call(
        body, name=name, grid=(S // tm, 2),
        in_specs=[pl.BlockSpec((tm, D), lambda i, j: (i, 0)),
                  pl.BlockSpec((None, D, FH), lambda i, j: (j, 0, 0)),
                  pl.BlockSpec((None, D, FH), lambda i, j: (j + 2, 0, 0))],
        out_specs=[pl.BlockSpec((2, tm, FH), lambda i, j: (0, i, j)), pl.BlockSpec((tm, FH), lambda i, j: (i, j))],
        out_shape=[jax.ShapeDtypeStruct((2, S, 2 * FH), BF16), jax.ShapeDtypeStruct((S, 2 * FH), BF16)],
        compiler_params=_params(("parallel", "parallel")),
    )(hn, w_in, w_in)


def _swiglu_bwd(name, dhb, w_out, z3, tm=512):
    S, D = dhb.shape
    F = w_out.shape[0]
    FH = F // 2
    tm = _tile(S, tm)

    def body(d_ref, w_ref, z_ref, dz_ref):
        d = lax.dot_general(d_ref[...], w_ref[...], _NT, preferred_element_type=F32)
        zg = z_ref[0].astype(F32)
        zu = z_ref[1].astype(F32)
        sg = _sigmoid(zg)
        dz_ref[0] = (d * zu * (sg * (1.0 + zg * (1.0 - sg)))).astype(dz_ref.dtype)
        dz_ref[1] = (d * (zg * sg)).astype(dz_ref.dtype)

    zspec = pl.BlockSpec((2, tm, FH), lambda i, j: (0, i, j))
    return pl.pallas_call(
        body, name=name, grid=(S // tm, 2),
        in_specs=[pl.BlockSpec((tm, D), lambda i, j: (i, 0)), pl.BlockSpec((FH, D), lambda i, j: (j, 0)), zspec],
        out_specs=zspec, out_shape=jax.ShapeDtypeStruct((2, S, F), BF16),
        compiler_params=_params(("parallel", "parallel")),
    )(dhb, w_out, z3)


SCAN_ROWS = 64


def _group_scan(A, B, reverse):
    n = A.shape[0]
    sub = lax.broadcasted_iota(jnp.int32, A.shape, 0) % SUBLANES
    for d in (1, 2, 4):
        if reverse:
            A_sh, B_sh = pltpu.roll(A, n - d, 0), pltpu.roll(B, n - d, 0)
            keep = sub < SUBLANES - d
        else:
            A_sh, B_sh = pltpu.roll(A, d, 0), pltpu.roll(B, d, 0)
            keep = sub >= d
        B = jnp.where(keep, A * B_sh + B, B)
        A = jnp.where(keep, A * A_sh, A)
    return A, B


def _block_scan(a, u, carry, reverse):
    A, B = _group_scan(a, u, reverse)
    ng = a.shape[0] // SUBLANES
    out = [None] * ng
    order = range(ng - 1, -1, -1) if reverse else range(ng)
    for gi in order:
        sl = slice(gi * SUBLANES, (gi + 1) * SUBLANES)
        hg = A[sl] * carry + B[sl]
        out[gi] = hg
        carry = hg[0:1] if reverse else hg[SUBLANES - 1:SUBLANES]
    return jnp.concatenate(out, axis=0), carry


def _lru_gates(rc, gip, grp, sp):
    gi = _sigmoid(gip)
    gr = _sigmoid(grp)
    la = -LRU_C * gr * sp
    a = jnp.exp(la)
    om = -jnp.tanh(la) * (a * a + 1.0)
    mult = jnp.sqrt(om)
    return gi, gr, a, mult


def _lru_fwd(name, proj, rc, gip, grp, lru_p, tc=256):
    S, C = rc.shape
    tc = _tile(C, tc)
    nb = S // SCAN_ROWS

    def body(gb_ref, rc_ref, gi_ref, gr_ref, l_ref, h_ref, m_ref):
        sp = _softplus(-l_ref[...])

        def step(b, carry):
            rows = pl.ds(pl.multiple_of(b * SCAN_ROWS, SCAN_ROWS), SCAN_ROWS)
            rcb = rc_ref[rows, :]
            gi, _, a, mult = _lru_gates(rcb, gi_ref[rows, :], gr_ref[rows, :], sp)
            h, carry = _block_scan(a, rcb * gi * mult, carry, False)
            h_ref[rows, :] = h
            gel, _ = _gelu_and_grad(gb_ref[rows, :])
            m_ref[rows, :] = (gel * h).astype(m_ref.dtype)
            return carry

        lax.fori_loop(0, nb, step, jnp.zeros((1, tc), F32))

    col = pl.BlockSpec((S, tc), lambda j: (0, j))
    return pl.pallas_call(
        body, name=name, grid=(C // tc,),
        in_specs=[col, col, col, col, pl.BlockSpec((1, tc), lambda j: (0, j))],
        out_specs=[col, col],
        out_shape=[jax.ShapeDtypeStruct((S, C), F32), jax.ShapeDtypeStruct((S, C), BF16)],
        compiler_params=_params(("parallel",)),
    )(proj, rc, gip, grp, lru_p)


def _lru_bwd(name, dm, proj, hrec, rc, gip, grp, lru_p, tc=256):
    S, C = rc.shape
    tc = _tile(C, tc)
    nb = S // SCAN_ROWS
    R = SCAN_ROWS

    def body(dm_ref, gb_ref, h_ref, rc_ref, gi_ref, gr_ref, l_ref,
             dgb_ref, dgi_ref, dgr_ref, drc_ref, dbi_ref, dbr_ref, dl_ref):
        lp = l_ref[...]
        sp = _softplus(-lp)
        row = lax.broadcasted_iota(jnp.int32, (R, tc), 0)
        zero = jnp.zeros((1, tc), F32)

        def step(t, carry):
            mu_in, s_i, s_r, s_sp = carry
            b = nb - 1 - t
            r0 = pl.multiple_of(b * R, R)
            rows = pl.ds(r0, R)
            rcb = rc_ref[rows, :]
            gi, gr, a, mult = _lru_gates(rcb, gi_ref[rows, :], gr_ref[rows, :], sp)
            gel, dgel = _gelu_and_grad(gb_ref[rows, :])
            dmb = dm_ref[rows, :]
            h = h_ref[rows, :]
            dgb_ref[rows, :] = (dmb * h * dgel).astype(dgb_ref.dtype)
            dh = dmb * gel
            mu, mu_out = _block_scan(a, a * dh, mu_in, True)
            mu_next = jnp.where(row == R - 1, mu_in, pltpu.roll(mu, R - 1, 0))
            lam = dh + mu_next
            p0 = pl.multiple_of(jnp.maximum(r0 - SUBLANES, 0), SUBLANES)
            prev = h_ref[pl.ds(p0, SUBLANES), :][SUBLANES - 1:SUBLANES]
            prev = jnp.where(b > 0, prev, 0.0)
            h_prev = jnp.where(row == 0, prev, pltpu.roll(h, 1, 0))
            da = lam * h_prev
            d_mult = lam * rcb * gi
            d_la = da * a - d_mult * (a * a) / mult
            d_grp = d_la * (-LRU_C * sp) * gr * (1.0 - gr)
            d_gip = lam * rcb * mult * gi * (1.0 - gi)
            dgr_ref[rows, :] = d_grp.astype(dgr_ref.dtype)
            dgi_ref[rows, :] = d_gip.astype(dgi_ref.dtype)
            drc_ref[rows, :] = lam * gi * mult
            s_i = s_i + jnp.sum(d_gip, axis=0, keepdims=True)
            s_r = s_r + jnp.sum(d_grp, axis=0, keepdims=True)
            s_sp = s_sp + jnp.sum(d_la * gr, axis=0, keepdims=True)
            return mu_out, s_i, s_r, s_sp

        _, s_i, s_r, s_sp = lax.fori_loop(0, nb, step, (zero, zero, zero, zero))
        dbi_ref[...] = s_i
        dbr_ref[...] = s_r
        dl_ref[...] = (-LRU_C * s_sp) * (-_sigmoid(-lp))

    col = pl.BlockSpec((S, tc), lambda j: (0, j))
    vec = pl.BlockSpec((1, tc), lambda j: (0, j))
    return pl.pallas_call(
        body, name=name, grid=(C // tc,),
        in_specs=[col, col, col, col, col, col, vec],
        out_specs=[col, col, col, col, vec, vec, vec],
        out_shape=[jax.ShapeDtypeStruct((S, C), BF16), jax.ShapeDtypeStruct((S, C), BF16),
                   jax.ShapeDtypeStruct((S, C), BF16), jax.ShapeDtypeStruct((S, C), F32),
                   jax.ShapeDtypeStruct((1, C), F32), jax.ShapeDtypeStruct((1, C), F32),
                   jax.ShapeDtypeStruct((1, C), F32)],
        compiler_params=_params(("parallel",)),
    )(dm, proj, hrec, rc, gip, grp, lru_p)


def _cumsum_rows(name, u, reverse):
    S, C = u.shape
    nb = S // SCAN_ROWS

    def body(u_ref, o_ref):
        def step(t, carry):
            b = nb - 1 - t if reverse else t
            rows = pl.ds(pl.multiple_of(b * SCAN_ROWS, SCAN_ROWS), SCAN_ROWS)
            ub = u_ref[rows, :]
            h, carry = _block_scan(jnp.ones_like(ub), ub, carry, reverse)
            o_ref[rows, :] = h
            return carry

        lax.fori_loop(0, nb, step, jnp.zeros((1, C), F32))

    spec = pl.BlockSpec((S, C), lambda i: (0, 0))
    return pl.pallas_call(
        body, name=name, grid=(1,), in_specs=[spec], out_specs=spec,
        out_shape=jax.ShapeDtypeStruct((S, C), F32),
        compiler_params=_params(("arbitrary",)),
    )(u)


def _shift_down(x, k):
    row = lax.broadcasted_iota(jnp.int32, x.shape, 0)
    return jnp.where(row >= k, pltpu.roll(x, k, 0), 0.0)


def _shift_up(x, k):
    n = x.shape[0]
    row = lax.broadcasted_iota(jnp.int32, x.shape, 0)
    return jnp.where(row < n - k, pltpu.roll(x, n - k, 0), 0.0)


def _conv_fwd(name, proj, w, b, tc=256):
    S, C2 = proj.shape
    C = C2 // 2
    tc = _tile(C, tc)
    off = C // tc

    def body(x_ref, w_ref, b_ref, o_ref, ob_ref):
        x = x_ref[...]
        out = b_ref[...] + w_ref[3:4, :] * x
        for k in (1, 2, 3):
            out = out + w_ref[3 - k:4 - k, :] * _shift_down(x, k)
        o_ref[...] = out
        ob_ref[...] = out.astype(BF16)

    col = pl.BlockSpec((S, tc), lambda j: (0, j))
    return pl.pallas_call(
        body, name=name, grid=(C // tc,),
        in_specs=[pl.BlockSpec((S, tc), lambda j: (0, off + j)),
                  pl.BlockSpec((4, tc), lambda j: (0, j)), pl.BlockSpec((1, tc), lambda j: (0, j))],
        out_specs=[col, col],
        out_shape=[jax.ShapeDtypeStruct((S, C), F32), jax.ShapeDtypeStruct((S, C), BF16)],
        compiler_params=_params(("parallel",)),
    )(proj, w, b)


def _conv_bwd(name, drc, proj, w, tc=256):
    S, C = drc.shape
    tc = _tile(C, tc)
    off = C // tc

    def body(y_ref, x_ref, w_ref, dx_ref, dw_ref, db_ref):
        y = y_ref[...]
        x = x_ref[...]
        dx = w_ref[3:4, :] * y
        dw_ref[3:4, :] = jnp.sum(y * x, axis=0, keepdims=True)
        for k in (1, 2, 3):
            dx = dx + w_ref[3 - k:4 - k, :] * _shift_up(y, k)
            dw_ref[3 - k:4 - k, :] = jnp.sum(y * _shift_down(x, k), axis=0, keepdims=True)
        dx_ref[...] = dx.astype(dx_ref.dtype)
        db_ref[...] = jnp.sum(y, axis=0, keepdims=True)

    col = pl.BlockSpec((S, tc), lambda j: (0, j))
    return pl.pallas_call(
        body, name=name, grid=(C // tc,),
        in_specs=[col, pl.BlockSpec((S, tc), lambda j: (0, off + j)), pl.BlockSpec((4, tc), lambda j: (0, j))],
        out_specs=[col, pl.BlockSpec((4, tc), lambda j: (0, j)), pl.BlockSpec((1, tc), lambda j: (0, j))],
        out_shape=[jax.ShapeDtypeStruct((S, C), BF16), jax.ShapeDtypeStruct((4, C), F32),
                   jax.ShapeDtypeStruct((1, C), F32)],
        compiler_params=_params(("parallel",)),
    )(drc, proj, w)


def _gates_fwd(name, rcb, wg, bg):
    S, C = rcb.shape
    nblk, bw, _ = wg.shape

    def body(x_ref, w_ref, b_ref, gi_ref, gr_ref):
        g = jnp.dot(x_ref[...], w_ref[...], preferred_element_type=F32) + b_ref[...]
        gi_ref[...] = g[:, :bw]
        gr_ref[...] = g[:, bw:]

    col = pl.BlockSpec((S, bw), lambda n: (0, n))
    return pl.pallas_call(
        body, name=name, grid=(nblk,),
        in_specs=[col, pl.BlockSpec((None, bw, 2 * bw), lambda n: (n, 0, 0)),
                  pl.BlockSpec((None, 1, 2 * bw), lambda n: (n, 0, 0))],
        out_specs=[col, col],
        out_shape=[jax.ShapeDtypeStruct((S, C), F32), jax.ShapeDtypeStruct((S, C), F32)],
        compiler_params=_params(("parallel",)),
    )(rcb, wg, bg)


def _gates_bwd(name, dgi, dgr, rcb, wg, drc1):
    S, C = rcb.shape
    nblk, bw, _ = wg.shape

    def body(dgi_ref, dgr_ref, x_ref, w_ref, d1_ref, drc_ref, dw_ref):
        w = w_ref[...]
        x = x_ref[...]
        di, dr = dgi_ref[...], dgr_ref[...]
        drc_ref[...] = (d1_ref[...]
                        + lax.dot_general(di, w[:, :bw], _NT, preferred_element_type=F32)
                        + lax.dot_general(dr, w[:, bw:], _NT, preferred_element_type=F32))
        dw_ref[:, :bw] = lax.dot_general(x, di, _TN, preferred_element_type=F32).astype(dw_ref.dtype)
        dw_ref[:, bw:] = lax.dot_general(x, dr, _TN, preferred_element_type=F32).astype(dw_ref.dtype)

    col = pl.BlockSpec((S, bw), lambda n: (0, n))
    wspec = pl.BlockSpec((None, bw, 2 * bw), lambda n: (n, 0, 0))
    return pl.pallas_call(
        body, name=name, grid=(nblk,),
        in_specs=[col, col, col, wspec, col], out_specs=[col, wspec],
        out_shape=[jax.ShapeDtypeStruct((S, C), F32), jax.ShapeDtypeStruct((nblk, bw, 2 * bw), BF16)],
        compiler_params=_params(("parallel",)),
    )(dgi, dgr, rcb, wg, drc1)


def _att_tile(S):
    return next(t for t in (512, 256, 128) if S % t == 0)


def _head_lanes(shape):
    return lax.broadcasted_iota(jnp.int32, shape, len(shape) - 1) < HEAD_DIM


def _key_bias(c_blk):
    first = _head_lanes(c_blk.shape)
    rolled = pltpu.roll(c_blk, HEAD_DIM, 1)
    return jnp.where(first, c_blk, rolled), jnp.where(first, rolled, c_blk)


def _over_keys(x, op):
    n = x.shape[0]
    while n > SUBLANES:
        n //= 2
        x = op(x[:n], x[n:2 * n])
    return (jnp.max if op is jnp.maximum else jnp.sum)(x, axis=0, keepdims=True)


def _causal_t(T, cc):
    r = lax.broadcasted_iota(jnp.int32, (T, LANES), 0)
    c = lax.broadcasted_iota(jnp.int32, (T, LANES), 1) + cc * LANES
    return r <= c


def _attn_fwd(name, q, kv, cfull):
    S, D = q.shape
    HP = D // LANES
    T = _att_tile(S)
    nq = S // T
    NC = T // LANES

    def body(q_ref, k_ref, v_ref, c_ref, o_ref, of_ref, lse_ref, bias, vT, acc, m_scr, l_scr):
        def prologue(i, _):
            rows = pl.ds(pl.multiple_of(i * T, T), T)
            bias[0, rows, :], bias[1, rows, :] = _key_bias(c_ref[rows, :])
            vT[i] = v_ref[rows, :].astype(F32).T.astype(BF16)
            return 0

        lax.fori_loop(0, nq, prologue, 0)

        def q_step(qi, _):
            q0 = pl.multiple_of(qi * T, T)
            qb = q_ref[pl.ds(q0, T), :]
            m_scr[...] = jnp.full(m_scr.shape, -jnp.inf, F32)
            l_scr[...] = jnp.zeros(l_scr.shape, F32)
            acc[...] = jnp.zeros(acc.shape, F32)

            def tile(kj, masked):
                ks = pl.ds(pl.multiple_of(kj * T, T), T)
                kf = k_ref[ks, :].astype(F32)
                first = _head_lanes(kf.shape)
                kms = [jnp.where(first if hh == 0 else jnp.logical_not(first), kf, 0.0).astype(BF16) for hh in range(2)]
                sTs = [lax.dot_general(km, qb, _NT, preferred_element_type=F32) for km in kms]
                for hh in range(2):
                    b = bias[hh, ks, :]
                    ps = []
                    for cc in range(NC):
                        cols = slice(cc * LANES, (cc + 1) * LANES)
                        s = sTs[hh][:, cols] + b
                        if masked:
                            s = jnp.where(_causal_t(T, cc), s, -jnp.inf)
                        m_old = m_scr[hh, cc]
                        m_new = jnp.maximum(m_old, _over_keys(s, jnp.maximum))
                        alpha = jnp.exp(m_old - m_new)
                        p = jnp.exp(s - m_new)
                        l_scr[hh, cc] = alpha * l_scr[hh, cc] + _over_keys(p, jnp.add)
                        m_scr[hh, cc] = m_new
                        ps.append(p.astype(BF16))
                        acc[hh, :, cols] = acc[hh, :, cols] * alpha
                    acc[hh] += jnp.dot(vT[kj, hh * HEAD_DIM:(hh + 1) * HEAD_DIM, :], jnp.concatenate(ps, axis=1),
                                       preferred_element_type=F32)

            def inner(kj, _):
                tile(kj, False)
                return 0

            lax.fori_loop(0, qi, inner, 0)
            tile(qi, True)
            outs = []
            for hh in range(2):
                inv = jnp.concatenate([1.0 / l_scr[hh, cc] for cc in range(NC)], axis=1)
                outs.append(acc[hh] * inv)
                for cc in range(NC):
                    lse_ref[hh:hh + 1, pl.ds(q0 + cc * LANES, LANES)] = m_scr[hh, cc] + jnp.log(l_scr[hh, cc])
            out = jnp.concatenate(outs, axis=0).T
            o_ref[pl.ds(q0, T), :] = out.astype(o_ref.dtype)
            of_ref[pl.ds(q0, T), :] = out
            return 0

        lax.fori_loop(0, nq, q_step, 0)

    blk = lambda off: pl.BlockSpec((S, LANES), lambda p: (0, off + p))
    return pl.pallas_call(
        body, name=name, grid=(HP,),
        in_specs=[blk(0), blk(0), blk(HP), blk(0)],
        out_specs=[blk(0), blk(0), pl.BlockSpec((None, 2, S), lambda p: (p, 0, 0))],
        out_shape=[jax.ShapeDtypeStruct((S, D), BF16), jax.ShapeDtypeStruct((S, D), F32),
                   jax.ShapeDtypeStruct((HP, 2, S), F32)],
        scratch_shapes=[pltpu.VMEM((2, S, LANES), F32), pltpu.VMEM((nq, LANES, T), BF16),
                        pltpu.VMEM((2, HEAD_DIM, T), F32), pltpu.VMEM((2, NC, 1, LANES), F32),
                        pltpu.VMEM((2, NC, 1, LANES), F32)],
        compiler_params=_params(("parallel",)),
    )(q, kv, kv, cfull)


def _attn_bwd(name, q, kv, cfull, of, do, lse3):
    S, D = q.shape
    HP = D // LANES
    T = _att_tile(S)
    nq = S // T
    NC = T // LANES
    scale = HEAD_DIM ** -0.5

    def body(q_ref, k_ref, v_ref, c_ref, of_ref, do_ref, lse_ref,
             dq_ref, dk_ref, dv_ref, dck_ref, drq_ref, bias, kT, dqT, delta, dr_scr):
        def prologue(i, _):
            rows = pl.ds(pl.multiple_of(i * T, T), T)
            bias[0, rows, :], bias[1, rows, :] = _key_bias(c_ref[rows, :])
            kT[i] = k_ref[rows, :].astype(F32).T.astype(BF16)
            prodT = (do_ref[rows, :].astype(F32) * of_ref[rows, :]).T
            for hh in range(2):
                delta[hh:hh + 1, rows] = jnp.sum(prodT[hh * HEAD_DIM:(hh + 1) * HEAD_DIM], axis=0, keepdims=True)
            dqT[i] = jnp.zeros((LANES, T), F32)
            return 0

        lax.fori_loop(0, nq, prologue, 0)
        dr_scr[...] = jnp.zeros(dr_scr.shape, F32)

        def kv_step(kj, _):
            ks = pl.ds(pl.multiple_of(kj * T, T), T)
            kf = k_ref[ks, :].astype(F32)
            vf = v_ref[ks, :].astype(F32)
            first = _head_lanes(kf.shape)
            masks = [first, jnp.logical_not(first)]
            kms = [jnp.where(m, kf, 0.0).astype(BF16) for m in masks]
            vms = [jnp.where(m, vf, 0.0).astype(BF16) for m in masks]

            def tile(qi, carry, masked):
                q0 = pl.multiple_of(qi * T, T)
                qb = q_ref[pl.ds(q0, T), :]
                dob = do_ref[pl.ds(q0, T), :]
                sTs = [lax.dot_general(km, qb, _NT, preferred_element_type=F32) for km in kms]
                dpTs = [lax.dot_general(vm, dob, _NT, preferred_element_type=F32) for vm in vms]
                out = []
                for hh in range(2):
                    dk_a, dv_a, dc_a = carry[3 * hh:3 * hh + 3]
                    b = bias[hh, ks, :]
                    head = slice(hh * HEAD_DIM, (hh + 1) * HEAD_DIM)
                    ps, dss = [], []
                    for cc in range(NC):
                        cols = slice(cc * LANES, (cc + 1) * LANES)
                        at = pl.ds(q0 + cc * LANES, LANES)
                        p = jnp.exp(sTs[hh][:, cols] + b - lse_ref[hh:hh + 1, at])
                        if masked:
                            p = jnp.where(_causal_t(T, cc), p, 0.0)
                        ds = p * (dpTs[hh][:, cols] - delta[hh:hh + 1, at])
                        ps.append(p.astype(BF16))
                        dss.append(ds.astype(BF16))
                        dc_a = dc_a + ds
                        dr_scr[hh:hh + 1, at] += _over_keys(ds, jnp.add)
                    pT = jnp.concatenate(ps, axis=1)
                    dsT = jnp.concatenate(dss, axis=1)
                    dv_a = dv_a + jnp.dot(pT, dob, preferred_element_type=F32)
                    dk_a = dk_a + jnp.dot(dsT, qb, preferred_element_type=F32)
                    dqT[qi, head, :] += jnp.dot(kT[kj, head, :], dsT, preferred_element_type=F32)
                    out += [dk_a, dv_a, dc_a]
                return tuple(out)

            zero = jnp.zeros((T, LANES), F32)
            carry = tile(kj, (zero,) * 6, True)
            dk0, dv0, dc0, dk1, dv1, dc1 = lax.fori_loop(kj + 1, nq, lambda qi, c: tile(qi, c, False), carry)
            dk_ref[ks, :] = jnp.where(first, dk0, dk1)
            dv_ref[ks, :] = jnp.where(first, dv0, dv1)
            dck_ref[ks, :] = jnp.where(first, jnp.broadcast_to(-jnp.sum(dc0, axis=1, keepdims=True), (T, LANES)),
                                       jnp.broadcast_to(-jnp.sum(dc1, axis=1, keepdims=True), (T, LANES)))
            return 0

        lax.fori_loop(0, nq, kv_step, 0)

        def epilogue(i, _):
            rows = pl.ds(pl.multiple_of(i * T, T), T)
            dq_ref[rows, :] = (dqT[i].T * scale).astype(dq_ref.dtype)
            return 0

        lax.fori_loop(0, nq, epilogue, 0)
        drq_ref[...] = dr_scr[...]

    blk = lambda off: pl.BlockSpec((S, LANES), lambda p: (0, off + p))
    row_spec = pl.BlockSpec((None, 2, S), lambda p: (p, 0, 0))
    return pl.pallas_call(
        body, name=name, grid=(HP,),
        in_specs=[blk(0), blk(0), blk(HP), blk(0), blk(0), blk(0), row_spec],
        out_specs=[blk(0), blk(0), blk(0), blk(0), row_spec],
        out_shape=[jax.ShapeDtypeStruct((S, D), BF16), jax.ShapeDtypeStruct((S, D), F32),
                   jax.ShapeDtypeStruct((S, D), F32), jax.ShapeDtypeStruct((S, D), F32),
                   jax.ShapeDtypeStruct((HP, 2, S), F32)],
        scratch_shapes=[pltpu.VMEM((2, S, LANES), F32), pltpu.VMEM((nq, LANES, T), BF16),
                        pltpu.VMEM((nq, LANES, T), F32), pltpu.VMEM((2, S), F32), pltpu.VMEM((2, S), F32)],
        compiler_params=_params(("parallel",)),
    )(q, kv, kv, cfull, of, do, lse3)


def _logsig_fwd(name, f):
    S, C = f.shape

    def body(f_ref, o_ref):
        o_ref[...] = -_softplus(-f_ref[...])

    spec = pl.BlockSpec((S, C), lambda i: (0, 0))
    return pl.pallas_call(body, name=name, grid=(1,), in_specs=[spec], out_specs=spec,
                          out_shape=jax.ShapeDtypeStruct((S, C), F32),
                          compiler_params=_params(("arbitrary",)))(f)


def _logsig_bwd(name, dls, f):
    S, C = f.shape

    def body(d_ref, f_ref, o_ref, s_ref):
        df = d_ref[...] * _sigmoid(-f_ref[...])
        o_ref[...] = df.astype(o_ref.dtype)
        s_ref[...] = jnp.sum(df, axis=0, keepdims=True)

    spec = pl.BlockSpec((S, C), lambda i: (0, 0))
    return pl.pallas_call(body, name=name, grid=(1,), in_specs=[spec, spec],
                          out_specs=[spec, pl.BlockSpec((1, C), lambda i: (0, 0))],
                          out_shape=[jax.ShapeDtypeStruct((S, C), BF16), jax.ShapeDtypeStruct((1, C), F32)],
                          compiler_params=_params(("arbitrary",)))(dls, f)


def _add_cast(name, parts, out_dtype, tr=256):
    S, C = parts[0].shape
    tr = _tile(S, tr)
    n = len(parts)

    def body(*refs):
        acc = refs[0][...].astype(F32)
        for r in refs[1:n]:
            acc = acc + r[...].astype(F32)
        refs[n][...] = acc.astype(out_dtype)

    spec = pl.BlockSpec((tr, C), lambda i: (i, 0))
    return pl.pallas_call(body, name=name, grid=(S // tr,), in_specs=[spec] * n, out_specs=spec,
                          out_shape=jax.ShapeDtypeStruct((S, C), out_dtype),
                          compiler_params=_params(("parallel",)))(*parts)


def _local_step(x, target, gains, layer_weights, layer_grads):
    S, D = x.shape
    HP = D // LANES
    scale = HEAD_DIM ** -0.5
    tm = _tile(S, 512)
    td = _tile(D, 512)
    saved = []
    h = x
    l = 0
    kv = cfull = f_pre = hn_kv = h_kv = None
    while True:
        W = layer_weights(l, "mix", h)
        if W is None:
            break
        recurrent = "w_rec_in" in W
        if l == 0:
            xn = _rmsnorm_fwd("mix_norm_0", h, gains["mix"][0])
        if recurrent:
            CH = W["w_rec_in"].shape[-1]
            C = 2 * CH
            proj = _mm(f"rec_in_{l}", "nn", xn, W["w_rec_in"], grid=(S // tm, N_CHIPS),
                       a_spec=pl.BlockSpec((tm, D), lambda i, j: (i, 0)),
                       b_spec=pl.BlockSpec((None, D, CH), lambda i, j: (j, 0, 0)),
                       out_shape=(S, 2 * C), out_dtype=F32,
                       out_spec=pl.BlockSpec((tm, CH), lambda i, j: (i, j)))
            rc, rcb = _conv_fwd(f"conv_{l}", proj, W["conv_w"], W["conv_b"])
            gip, grp = _gates_fwd(f"gates_{l}", rcb, W["w_gates"], W["b_gates"])
            hrec, m = _lru_fwd(f"lru_{l}", proj, rc, gip, grp, W["lru_param"])
            h_mid, hn = _mm_nn(f"rec_out_{l}", m, W["w_rec_out"], out_dtype=F32, res=h, tn=D, norm_gain=gains["ffn"][l])
            mix_saved = (xn, proj, rc, rcb, gip, grp, hrec, m)
        else:
            if "w_kv" in W:
                h_kv = h
                hn_kv = _rmsnorm_fwd("kv_norm", h, W["norm_kv"])
                kv = _mm_nn("kv_proj", hn_kv, W["w_kv"], out_dtype=BF16)
                f_pre = _mm_nn("f_proj", hn_kv, W["w_f"], out_dtype=F32, bias=W["b_f"])
                c = _cumsum_rows("c_cumsum", _logsig_fwd("logsig", f_pre), False)
                cfull = jnp.repeat(-c[:, :2 * HP], HEAD_DIM, axis=1)
            q = _mm_nn(f"q_proj_{l}", xn, W["w_q"], out_dtype=BF16, scale=scale)
            o, of, lse = _attn_fwd(f"attn_fwd_{l}", q, kv, cfull)
            h_mid, hn = _mm_nn(f"o_proj_{l}", o, W["w_o"], out_dtype=F32, res=h, tn=D, norm_gain=gains["ffn"][l])
            mix_saved = (xn, q, o, of, lse)
        W = {**W, **layer_weights(l, "ffn", h_mid)}
        z3, act = _swiglu_fwd(f"ffn_in_{l}", hn, W["w_ffn_in"])
        saved.append((W, h, h_mid, mix_saved, (hn, z3, act)))
        l += 1
        if l < len(gains["mix"]):
            h, xn = _mm_nn(f"ffn_out_{l - 1}", act, W["w_ffn_out"], out_dtype=F32, res=h_mid, tn=D,
                           norm_gain=gains["mix"][l])
        else:
            h = _mm_nn(f"ffn_out_{l - 1}", act, W["w_ffn_out"], out_dtype=F32, res=h_mid, tn=D)

    dh, dhb, dg_final, loss_row = _loss_head("loss_head", h, target, gains["final"])

    dk_parts, dv_parts, dc_parts = [], [], []
    token = None
    for l in reversed(range(len(saved))):
        W, h_in, h_mid, mix_saved, (hn, z3, act) = saved[l]
        recurrent = "w_rec_in" in W
        FH = W["w_ffn_in"].shape[-1]
        G = {}
        norm_ffn = gains["ffn"][l]
        if token is not None:
            norm_ffn = norm_ffn + jnp.minimum(token[:1, :1], 0.0)
        G["w_ffn_out"] = _mm_tn(f"d_ffn_out_{l}", act, dhb, out_dtype=BF16, tn=D)
        dz3 = _swiglu_bwd(f"d_act_{l}", dhb, W["w_ffn_out"], z3)
        G["w_ffn_in"] = _mm(
            f"d_ffn_in_{l}", "tn", hn, dz3, grid=(D // td, N_CHIPS),
            a_spec=pl.BlockSpec((S, td), lambda i, j: (0, i)),
            b_spec=pl.BlockSpec((None, S, FH), lambda i, j: (j // 2, 0, j % 2)),
            out_shape=(N_CHIPS, D, FH), out_dtype=BF16,
            out_spec=pl.BlockSpec((None, td, FH), lambda i, j: (j, i, 0)))
        token = layer_grads(l, "ffn", G)
        G = {}
        norm_ffn = norm_ffn + jnp.minimum(token[:1, :1], 0.0)
        dh, dhb, dgp = _mm(f"d_ffn_hn_{l}", "nt", dz3, W["w_ffn_in"], grid=(S // tm, 1, N_CHIPS), nk=N_CHIPS,
                           a_spec=pl.BlockSpec((None, tm, FH), lambda i, j, k: (k // 2, i, k % 2)),
                           b_spec=pl.BlockSpec((None, D, FH), lambda i, j, k: (k, 0, 0)),
                           out_shape=(S, D), out_dtype=F32, out_spec=pl.BlockSpec((tm, D), lambda i, j, k: (i, 0)),
                           norm_bwd=(h_mid, norm_ffn, dh))
        G["norm_ffn"] = jnp.sum(dgp, axis=0)
        if recurrent:
            CH = W["w_rec_in"].shape[-1]
            C = 2 * CH
            xn, proj, rc, rcb, gip, grp, hrec, m = mix_saved
            G["w_rec_out"] = _mm_tn(f"d_rec_out_{l}", m, dhb, out_dtype=BF16, tn=D)
            dm = _mm_nt(f"d_m_{l}", dhb, W["w_rec_out"], out_dtype=F32, tn=C)
            dgb, dgi, dgr, drc1, G["b_gi"], G["b_gr"], G["lru_param"] = _lru_bwd(
                f"d_lru_{l}", dm, proj, hrec, rc, gip, grp, W["lru_param"])
            drc, G["w_gates"] = _gates_bwd(f"d_gates_{l}", dgi, dgr, rcb, W["w_gates"], drc1)
            drec, G["conv_w"], G["conv_b"] = _conv_bwd(f"d_conv_{l}", drc, proj, W["conv_w"])
            dproj = jnp.concatenate([dgb, drec], axis=1)
            G["w_rec_in"] = _mm(
                f"d_rec_in_{l}", "tn", xn, dproj, grid=(1, N_CHIPS),
                a_spec=pl.BlockSpec((S, D), lambda i, j: (0, 0)),
                b_spec=pl.BlockSpec((S, CH), lambda i, j: (0, j)),
                out_shape=(N_CHIPS, D, CH), out_dtype=BF16,
                out_spec=pl.BlockSpec((None, D, CH), lambda i, j: (j, 0, 0)))
            dh, dhb, dgp = _mm(f"d_rec_xn_{l}", "nt", dproj, W["w_rec_in"], grid=(S // tm, 1, N_CHIPS), nk=N_CHIPS,
                               a_spec=pl.BlockSpec((tm, CH), lambda i, j, k: (i, k)),
                               b_spec=pl.BlockSpec((None, D, CH), lambda i, j, k: (k, 0, 0)),
                               out_shape=(S, D), out_dtype=F32, out_spec=pl.BlockSpec((tm, D), lambda i, j, k: (i, 0)),
                               norm_bwd=(h_in, gains["mix"][l], dh))
        else:
            xn, q, o, of, lse = mix_saved
            G["w_o"] = _mm_tn(f"d_o_proj_{l}", o, dhb, out_dtype=BF16, tn=D)
            do = _mm_nt(f"d_o_{l}", dhb, W["w_o"], out_dtype=BF16, tn=D)
            dq, dk, dv, dck, drq = _attn_bwd(f"attn_bwd_{l}", q, kv, cfull, of, do, lse)
            dk_parts.append(dk)
            dv_parts.append(dv)
            dc_parts.append(dck[:, ::HEAD_DIM] + drq.reshape(2 * HP, S).T)
            G["w_q"] = _mm_tn(f"d_q_proj_{l}", xn, dq, out_dtype=BF16, tn=D)
            dh, dhb, dgp = _mm_nt(f"d_q_xn_{l}", dq, W["w_q"], out_dtype=F32, tn=D, norm_bwd=(h_in, gains["mix"][l], dh))
        G["norm_mix"] = jnp.sum(dgp, axis=0)
        if "w_kv" in W:
            dkb = _add_cast("dk_sum", dk_parts, BF16)
            dvb = _add_cast("dv_sum", dv_parts, BF16)
            dkv = jnp.concatenate([dkb, dvb], axis=1)
            dc = sum(dc_parts[1:], dc_parts[0])
            dc_pad = jnp.pad(dc, ((0, 0), (0, LANES - 2 * HP)))
            dls = _cumsum_rows("dc_cumsum", dc_pad, True)
            dfb, G["b_f"] = _logsig_bwd("d_logsig", dls, f_pre)
            G["w_kv"] = _mm_tn("d_kv_proj", hn_kv, dkv, out_dtype=BF16)
            G["w_f"] = _mm_tn("d_f_proj", hn_kv, dfb, out_dtype=F32)
            dhn_f = _mm_nt("d_f_hn", dfb, W["w_f"], out_dtype=F32, tn=D)
            dh, dhb, dgp = _mm_nt("d_kv_hn", dkv, W["w_kv"], out_dtype=F32, tn=D, res=dhn_f,
                                  norm_bwd=(h_kv, W["norm_kv"], dh))
            G["norm_kv"] = jnp.sum(dgp, axis=0)
        token = layer_grads(l, "mix", G)
    return loss_row, dh, dg_final


_ANY = pl.BlockSpec(memory_space=pl.ANY)


def _position():
    return lax.axis_index("x"), lax.axis_index("y"), lax.axis_index("c")


def _chip_peers(x, y):
    return [(1 - x, y), (x, 1 - y), (1 - x, 1 - y)]


def _half_rows(c, n):
    h = n // 2
    assert h % 16 == 0
    return pl.ds(pl.multiple_of(c * h, 16), h)


def _place_own(name, shard, layer, me):
    _, R, C = shard.shape
    tr = _row_tile(R, C, shard.dtype.itemsize)

    def body(me_ref, x_ref, o_ref):
        o_ref[...] = x_ref[...]

    return pl.pallas_call(
        body, name=name,
        grid_spec=pltpu.PrefetchScalarGridSpec(
            num_scalar_prefetch=1, grid=(R // tr,),
            in_specs=[pl.BlockSpec((None, tr, C), lambda i, me_ref: (layer, i, 0))],
            out_specs=pl.BlockSpec((None, tr, C), lambda i, me_ref: (me_ref[0], i, 0))),
        out_shape=jax.ShapeDtypeStruct((N_CHIPS, R, C), shard.dtype),
        compiler_params=_params(("parallel",)),
    )(me, shard)


def _gather_smalls(name, smalls):
    ns = len(smalls)

    def body(*refs):
        ins, outs = refs[:ns], refs[ns:2 * ns]
        send_sems, recv_sems, local_sems = refs[2 * ns:]
        x, y, c = _position()
        me = 2 * x + y
        peers = _chip_peers(x, y)

        def remote(t, k, chip):
            px, py = peers[k]
            return pltpu.make_async_remote_copy(
                src_ref=ins[t], dst_ref=outs[t].at[chip], send_sem=send_sems.at[3 * t + k],
                recv_sem=recv_sems.at[3 * t + k], device_id=(px, py, c), device_id_type=MESH)

        local = [pltpu.make_async_copy(ins[t], outs[t].at[me], local_sems.at[t]) for t in range(ns)]
        for t in range(ns):
            local[t].start()
            for k in range(3):
                remote(t, k, me).start()
        for t in range(ns):
            for k in range(3):
                px, py = peers[k]
                remote(t, k, 2 * px + py).wait_recv()
        for t in range(ns):
            for k in range(3):
                remote(t, k, me).wait_send()
            local[t].wait()

    return pl.pallas_call(
        body, name=name, in_specs=[_ANY] * ns, out_specs=[_ANY] * ns,
        out_shape=[jax.ShapeDtypeStruct((N_CHIPS,) + s.shape, s.dtype) for s in smalls],
        scratch_shapes=[pltpu.SemaphoreType.DMA((3 * ns,)), pltpu.SemaphoreType.DMA((3 * ns,)),
                        pltpu.SemaphoreType.DMA((ns,))],
    )(*smalls)


_SEM = pl.BlockSpec(memory_space=pltpu.SEMAPHORE)
_SPLIT = pltpu.CompilerParams(has_side_effects=pltpu.SideEffectType.DATAFLOW_SIDE_EFFECTING)


def _weight_copy(shards, buf, items, sems, i, k, chip_of_dst, peers, c):
    w, l = items[i]
    px, py = peers[k]
    half = _half_rows(c, shards[w].shape[1])
    return pltpu.make_async_remote_copy(
        src_ref=shards[w].at[l, half], dst_ref=buf.at[chip_of_dst, half],
        send_sem=sems[0].at[3 * i + k], recv_sem=sems[1].at[3 * i + k],
        device_id=(px, py, c), device_id_type=MESH)


def _gather_start(name, shards, bufs, items, after):
    nw, n = len(shards), len(bufs)

    def body(*refs):
        ins, outs, sems = refs[:nw], refs[nw + n + 1:nw + 2 * n + 1], refs[nw + 2 * n + 1:]
        x, y, c = _position()
        peers = _chip_peers(x, y)
        for i in range(n):
            for k in range(3):
                _weight_copy(ins, outs[i], items, sems, i, k, 2 * x + y, peers, c).start()

    res = pl.pallas_call(
        body, name=name, in_specs=[_ANY] * (nw + n + 1), out_specs=[_ANY] * n + [_SEM, _SEM],
        out_shape=[jax.ShapeDtypeStruct(b.shape, b.dtype) for b in bufs]
        + [pltpu.SemaphoreType.DMA((3 * n,)), pltpu.SemaphoreType.DMA((3 * n,))],
        input_output_aliases={nw + i: i for i in range(n)}, compiler_params=_SPLIT,
    )(*shards, *bufs, after)
    return res[:n], res[n:]


def _gather_wait(name, shards, bufs, items, ids, sems, after):
    nw, m = len(shards), len(ids)

    def body(*refs):
        ins, bs = refs[:nw], refs[nw:nw + m]
        sem_refs = refs[nw + m:nw + m + 2]
        x, y, c = _position()
        peers = _chip_peers(x, y)
        for j, i in enumerate(ids):
            for k in range(3):
                px, py = peers[k]
                _weight_copy(ins, bs[j], items, sem_refs, i, k, 2 * px + py, peers, c).wait_recv()
        for j, i in enumerate(ids):
            for k in range(3):
                _weight_copy(ins, bs[j], items, sem_refs, i, k, 2 * x + y, peers, c).wait_send()

    res = pl.pallas_call(
        body, name=name, in_specs=[_ANY] * (nw + m) + [_SEM, _SEM, _ANY], out_specs=[_ANY] * m,
        out_shape=[jax.ShapeDtypeStruct(bufs[i].shape, bufs[i].dtype) for i in ids],
        input_output_aliases={nw + j: j for j in range(m)}, compiler_params=_SPLIT,
    )(*shards, *[bufs[i] for i in ids], *sems, after)
    return list(res)


def _gather_d2d(name, bufs):
    n = len(bufs)

    def body(*refs):
        ins, outs = refs[:n], refs[n:2 * n]
        send_sems, recv_sems = refs[2 * n:]
        x, y, c = _position()
        peers = _chip_peers(x, y)

        def remote(i, k, core):
            px, py = peers[k]
            half = _half_rows(core, ins[i].shape[1])
            return pltpu.make_async_remote_copy(
                src_ref=ins[i].at[2 * px + py, half], dst_ref=outs[i].at[2 * px + py, half],
                send_sem=send_sems.at[3 * i + k], recv_sem=recv_sems.at[3 * i + k],
                device_id=(x, y, 1 - c), device_id_type=MESH)

        for i in range(n):
            for k in range(3):
                remote(i, k, c).start()
        for i in range(n):
            for k in range(3):
                remote(i, k, 1 - c).wait_recv()
        for i in range(n):
            for k in range(3):
                remote(i, k, c).wait_send()

    return list(pl.pallas_call(
        body, name=name, in_specs=[_ANY] * n, out_specs=[_ANY] * n,
        out_shape=[jax.ShapeDtypeStruct(g.shape, g.dtype) for g in bufs],
        input_output_aliases={i: i for i in range(n)},
        scratch_shapes=[pltpu.SemaphoreType.DMA((3 * n,)), pltpu.SemaphoreType.DMA((3 * n,))],
    )(*bufs))


def _reduce_d2d(name, grads):
    n = len(grads)

    def body(*refs):
        ins, outs = refs[:n], refs[n:2 * n]
        send_sems, recv_sems = refs[2 * n:]
        x, y, c = _position()
        remote = [pltpu.make_async_remote_copy(
            src_ref=ins[i].at[:, _half_rows(1 - c, ins[i].shape[1])], dst_ref=outs[i],
            send_sem=send_sems.at[i], recv_sem=recv_sems.at[i],
            device_id=(x, y, 1 - c), device_id_type=MESH) for i in range(n)]
        for cp in remote:
            cp.start()
        for cp in remote:
            cp.wait_recv()
        for cp in remote:
            cp.wait_send()

    return pl.pallas_call(
        body, name=name, in_specs=[_ANY] * n, out_specs=[_ANY] * n,
        out_shape=[jax.ShapeDtypeStruct((N_CHIPS, g.shape[1] // 2, g.shape[2]), g.dtype) for g in grads],
        scratch_shapes=[pltpu.SemaphoreType.DMA((n,)), pltpu.SemaphoreType.DMA((n,))],
    )(*grads)


def _sum_cores(name, g, other, core):
    _, R, C = g.shape
    H = R // 2
    tr = _row_tile(H, C)
    nb = H // tr

    def body(c_ref, g_ref, o_ref, out_ref):
        out_ref[...] = (g_ref[...].astype(F32) + o_ref[...].astype(F32)).astype(out_ref.dtype)

    return pl.pallas_call(
        body, name=name,
        grid_spec=pltpu.PrefetchScalarGridSpec(
            num_scalar_prefetch=1, grid=(N_CHIPS, nb),
            in_specs=[pl.BlockSpec((None, tr, C), lambda j, i, c_ref: (j, c_ref[0] * nb + i, 0)),
                      pl.BlockSpec((None, tr, C), lambda j, i, c_ref: (j, i, 0))],
            out_specs=pl.BlockSpec((None, tr, C), lambda j, i, c_ref: (j, i, 0))),
        out_shape=jax.ShapeDtypeStruct((N_CHIPS, H, C), BF16),
        compiler_params=_params(("parallel", "parallel")),
    )(core, g, other)


def _sum_chips(name, received, own, full, layer, me_core):
    _, H, C = received.shape
    tr = _row_tile(H, C)
    nb = H // tr

    def body(s_ref, r_ref, own_ref, full_ref, out_ref):
        acc = r_ref[0].astype(F32)
        for k in (1, 2):
            acc = acc + r_ref[k].astype(F32)
        out_ref[...] = acc + own_ref[...].astype(F32)

    return pl.pallas_call(
        body, name=name,
        grid_spec=pltpu.PrefetchScalarGridSpec(
            num_scalar_prefetch=1, grid=(nb,),
            in_specs=[pl.BlockSpec((3, tr, C), lambda i, s_ref: (0, i, 0)),
                      pl.BlockSpec((None, tr, C), lambda i, s_ref: (s_ref[0], i, 0)),
                      _ANY],
            out_specs=pl.BlockSpec((None, tr, C), lambda i, s_ref: (layer, s_ref[1] * nb + i, 0))),
        out_shape=jax.ShapeDtypeStruct(full.shape, full.dtype),
        input_output_aliases={3: 0},
        compiler_params=_params(("parallel",)),
    )(me_core, received, own, full)


def _part_copy(parts, recv, sems, i, k, peers, c):
    px, py = peers[k]
    return pltpu.make_async_remote_copy(
        src_ref=parts[i].at[2 * px + py], dst_ref=recv[i].at[k],
        send_sem=sems[0].at[3 * i + k], recv_sem=sems[1].at[3 * i + k],
        device_id=(px, py, c), device_id_type=MESH)


def _scatter_start(name, parts):
    n = len(parts)

    def body(*refs):
        ins, outs, sems, token = refs[:n], refs[n:2 * n], refs[2 * n:2 * n + 2], refs[2 * n + 2]
        x, y, c = _position()
        peers = _chip_peers(x, y)
        for i in range(n):
            for k in range(3):
                _part_copy(ins, outs, sems, i, k, peers, c).start()
        token[...] = jnp.zeros_like(token)

    res = pl.pallas_call(
        body, name=name, in_specs=[_ANY] * n,
        out_specs=[_ANY] * n + [_SEM, _SEM, pl.BlockSpec(memory_space=pltpu.VMEM)],
        out_shape=[jax.ShapeDtypeStruct((3,) + p.shape[1:], p.dtype) for p in parts]
        + [pltpu.SemaphoreType.DMA((3 * n,)), pltpu.SemaphoreType.DMA((3 * n,)),
           jax.ShapeDtypeStruct((SUBLANES, LANES), F32)],
        compiler_params=_SPLIT,
    )(*parts)
    return list(res[:n]), res[n:n + 2], res[n + 2]


def _scatter_wait(name, parts, recv, sems):
    n = len(parts)

    def body(*refs):
        ins, rs, sem_refs = refs[:n], refs[n:2 * n], refs[2 * n:2 * n + 2]
        x, y, c = _position()
        peers = _chip_peers(x, y)
        for i in range(n):
            for k in range(3):
                _part_copy(ins, rs, sem_refs, i, k, peers, c).wait_recv()
        for i in range(n):
            for k in range(3):
                _part_copy(ins, rs, sem_refs, i, k, peers, c).wait_send()

    return list(pl.pallas_call(
        body, name=name, in_specs=[_ANY] * (2 * n) + [_SEM, _SEM], out_specs=[_ANY] * n,
        out_shape=[jax.ShapeDtypeStruct(r.shape, r.dtype) for r in recv],
        input_output_aliases={n + i: i for i in range(n)}, compiler_params=_SPLIT,
    )(*parts, *recv, *sems))


def _share_d2d(name, full):
    n = len(full)

    def body(*refs):
        ins, outs = refs[:n], refs[n:2 * n]
        send_sems, recv_sems = refs[2 * n:]
        x, y, c = _position()

        def remote(w, core):
            half = _half_rows(core, ins[w].shape[1])
            return pltpu.make_async_remote_copy(
                src_ref=ins[w].at[:, half], dst_ref=outs[w].at[:, half],
                send_sem=send_sems.at[w], recv_sem=recv_sems.at[w],
                device_id=(x, y, 1 - c), device_id_type=MESH)

        for w in range(n):
            remote(w, c).start()
        for w in range(n):
            remote(w, 1 - c).wait_recv()
        for w in range(n):
            remote(w, c).wait_send()

    return pl.pallas_call(
        body, name=name, in_specs=[_ANY] * n, out_specs=[_ANY] * n,
        out_shape=[jax.ShapeDtypeStruct(f.shape, f.dtype) for f in full],
        input_output_aliases={w: w for w in range(n)},
        scratch_shapes=[pltpu.SemaphoreType.DMA((n,)), pltpu.SemaphoreType.DMA((n,))],
    )(*full)


def _gather_all(name, a):
    def body(a_ref, o_ref, send_sems, recv_sems, local_sem):
        x, y, c = _position()
        me = 4 * x + 2 * y + c

        def peer(k):
            return (x ^ ((k >> 2) & 1), y ^ ((k >> 1) & 1), c ^ (k & 1))

        def remote(k, slot):
            return pltpu.make_async_remote_copy(
                src_ref=a_ref, dst_ref=o_ref.at[slot], send_sem=send_sems.at[k - 1], recv_sem=recv_sems.at[k - 1],
                device_id=peer(k), device_id_type=MESH)

        local = pltpu.make_async_copy(a_ref, o_ref.at[me], local_sem)
        local.start()
        for k in range(1, N_DEV):
            remote(k, me).start()
        for k in range(1, N_DEV):
            px, py, pc = peer(k)
            remote(k, 4 * px + 2 * py + pc).wait_recv()
        for k in range(1, N_DEV):
            remote(k, me).wait_send()
        local.wait()

    return pl.pallas_call(
        body, name=name, in_specs=[_ANY], out_specs=_ANY,
        out_shape=jax.ShapeDtypeStruct((N_DEV,) + a.shape, a.dtype),
        scratch_shapes=[pltpu.SemaphoreType.DMA((N_DEV - 1,)), pltpu.SemaphoreType.DMA((N_DEV - 1,)),
                        pltpu.SemaphoreType.DMA],
    )(a)


def _rows2d(a, lead=0):
    return a.reshape(a.shape[:lead] + (-1, a.shape[-1]))


def _row_tile(rows, cols, itemsize=4, target=1 << 20):
    want = max(SUBLANES, target // (cols * itemsize))
    t = min(rows, (want // 16) * 16)
    while t > 16 and rows % t:
        t -= 16
    return t if rows % t == 0 else rows


def _sum_slots(name, r, out_dtype=F32):
    ns = r.shape[0]
    r2 = _rows2d(r, 1)
    _, rows, cols = r2.shape
    tr = _row_tile(rows, cols)

    def body(r_ref, o_ref):
        acc = r_ref[0].astype(F32)
        for s in range(1, ns):
            acc = acc + r_ref[s].astype(F32)
        o_ref[...] = acc.astype(o_ref.dtype)

    out = pl.pallas_call(
        body, name=name, grid=(rows // tr,),
        in_specs=[pl.BlockSpec((ns, tr, cols), lambda i: (0, i, 0))],
        out_specs=pl.BlockSpec((tr, cols), lambda i: (i, 0)),
        out_shape=jax.ShapeDtypeStruct((rows, cols), out_dtype),
        compiler_params=_params(("parallel",)),
    )(r2)
    return out.reshape(r.shape[1:])


def _adamw(name, g_parts, w, m, v):
    shape = w.shape
    ng = len(g_parts)
    args = [_rows2d(a) for a in (*g_parts, w, m, v)]
    rows, cols = args[0].shape
    tr = _row_tile(rows, cols, target=1 << 19)
    c1 = 1.0 - ADAM_B1 ** ADAM_STEP
    c2 = 1.0 - ADAM_B2 ** ADAM_STEP

    def body(*refs):
        g = refs[0][...]
        for r in refs[1:ng]:
            g = g + r[...]
        w_ref, m_ref, v_ref = refs[ng:ng + 3]
        g_out, d_out, m_out, v_out = refs[ng + 3:]
        mn = ADAM_B1 * m_ref[...] + (1.0 - ADAM_B1) * g
        vn = ADAM_B2 * v_ref[...] + (1.0 - ADAM_B2) * (g * g)
        m_hat = mn / c1
        v_hat = vn / c2
        g_out[...] = g
        d_out[...] = -ADAM_LR * (m_hat / (jnp.sqrt(v_hat) + ADAM_EPS) + ADAM_WD * w_ref[...])
        m_out[...] = mn
        v_out[...] = vn

    spec = pl.BlockSpec((tr, cols), lambda i: (i, 0))
    outs = pl.pallas_call(
        body, name=name, grid=(rows // tr,), in_specs=[spec] * (ng + 3), out_specs=[spec] * 4,
        out_shape=[jax.ShapeDtypeStruct((rows, cols), F32)] * 4,
        compiler_params=_params(("parallel",)),
    )(*args)
    return tuple(o.reshape(shape) for o in outs)


_WEIGHTS = ["norm_mix", "norm_ffn", "w_ffn_in", "w_ffn_out", "w_rec_in", "conv_w", "conv_b", "w_lru_gates",
            "b_lru_gates", "lru_param", "w_rec_out", "norm_kv", "w_kvf", "b_forget", "w_q", "w_o", "norm_final"]
_BIG = ["w_ffn_in", "w_ffn_out", "w_rec_in", "w_lru_gates", "w_rec_out", "w_kvf", "w_q", "w_o"]


def _stack3(a):
    return a[None] if a.ndim == 2 else a.reshape(a.shape[0], -1, a.shape[-1])


def _pad_lanes(a, n):
    return jnp.pad(a, ((0, 0),) * (a.ndim - 1) + ((0, n - a.shape[-1]),))


def kernel(x, norm_mix, norm_ffn, w_ffn_in, w_ffn_out, w_rec_in, conv_w, conv_b, w_lru_gates, b_lru_gates, lru_param, w_rec_out, norm_kv, w_kvf, b_forget, w_q, w_o, norm_final, loss_target, m_norm_mix, m_norm_ffn, m_w_ffn_in, m_w_ffn_out, m_w_rec_in, m_conv_w, m_conv_b, m_w_lru_gates, m_b_lru_gates, m_lru_param, m_w_rec_out, m_norm_kv, m_w_kvf, m_b_forget, m_w_q, m_w_o, m_norm_final, v_norm_mix, v_norm_ffn, v_w_ffn_in, v_w_ffn_out, v_w_rec_in, v_conv_w, v_conv_b, v_w_lru_gates, v_b_lru_gates, v_lru_param, v_w_rec_out, v_norm_kv, v_w_kvf, v_b_forget, v_w_q, v_w_o, v_norm_final):
    P = dict(norm_mix=norm_mix, norm_ffn=norm_ffn, w_ffn_in=w_ffn_in, w_ffn_out=w_ffn_out, w_rec_in=w_rec_in,
             conv_w=conv_w, conv_b=conv_b, w_lru_gates=w_lru_gates, b_lru_gates=b_lru_gates, lru_param=lru_param,
             w_rec_out=w_rec_out, norm_kv=norm_kv, w_kvf=w_kvf, b_forget=b_forget, w_q=w_q, w_o=w_o,
             norm_final=norm_final)
    M1 = dict(norm_mix=m_norm_mix, norm_ffn=m_norm_ffn, w_ffn_in=m_w_ffn_in, w_ffn_out=m_w_ffn_out,
              w_rec_in=m_w_rec_in, conv_w=m_conv_w, conv_b=m_conv_b, w_lru_gates=m_w_lru_gates,
              b_lru_gates=m_b_lru_gates, lru_param=m_lru_param, w_rec_out=m_w_rec_out, norm_kv=m_norm_kv,
              w_kvf=m_w_kvf, b_forget=m_b_forget, w_q=m_w_q, w_o=m_w_o, norm_final=m_norm_final)
    M2 = dict(norm_mix=v_norm_mix, norm_ffn=v_norm_ffn, w_ffn_in=v_w_ffn_in, w_ffn_out=v_w_ffn_out,
              w_rec_in=v_w_rec_in, conv_w=v_conv_w, conv_b=v_conv_b, w_lru_gates=v_w_lru_gates,
              b_lru_gates=v_b_lru_gates, lru_param=v_lru_param, w_rec_out=v_w_rec_out, norm_kv=v_norm_kv,
              w_kvf=v_w_kvf, b_forget=v_b_forget, w_q=v_w_q, w_o=v_w_o, norm_final=v_norm_final)

    _, S, D = x.shape
    L = norm_mix.shape[0]
    NA, NBLK, BW, GS = w_lru_gates.shape
    NB = w_q.shape[0]
    C = NBLK * BW
    CS = C // N_CHIPS
    H = b_forget.shape[0]
    assert C == D and H * HEAD_DIM == D and H <= LANES
    chip = 2 * lax.axis_index("x") + lax.axis_index("y")

    small_a = jnp.concatenate([conv_w, conv_b[:, None], lru_param[:, None]], axis=1)
    small_a, b_gates = _gather_smalls("gather_smalls", [small_a, b_lru_gates])
    small_a = small_a.transpose(1, 2, 0, 3).reshape(NA, 6, C)
    b_gates = b_gates.transpose(1, 2, 0, 3).reshape(NA, NBLK, 1, N_CHIPS * GS)
    shards = [_stack3(P[w]).astype(BF16) for w in _BIG]
    core = lax.axis_index("c")
    chip_id = jnp.reshape(chip, (1,)).astype(jnp.int32)
    core_id = jnp.reshape(core, (1,)).astype(jnp.int32)
    me_core = jnp.stack([chip, core]).astype(jnp.int32)

    def stage_items(l, part):
        if part == "ffn":
            return [(_BIG.index("w_ffn_in"), l), (_BIG.index("w_ffn_out"), l)]
        if l < NA:
            names, at = ["w_rec_in", "w_lru_gates", "w_rec_out"], l
        else:
            names, at = (["w_kvf"] if l == NA else []) + ["w_q", "w_o"], l - NA
        return [(_BIG.index(n), 0 if n == "w_kvf" else at) for n in names]

    stages = [(l, part) for l in range(L) for part in ("mix", "ffn")]
    items = [it for st in stages for it in stage_items(*st)]
    ids_of = {st: [items.index(it) for it in stage_items(*st)] for st in stages}
    bufs = [_place_own(f"place_{_BIG[w]}_{li}", shards[w], li, chip_id) for w, li in items]
    bufs, gather_sems = _gather_start("gather_start", shards, bufs, items, small_a)

    def layer_weights(l, part, after):
        if l >= L:
            return None
        ids = ids_of[(l, part)]
        got = _gather_wait(f"gather_wait_{part}_{l}", shards, bufs, items, ids, gather_sems, after)
        got = _gather_d2d(f"gather_d2d_{part}_{l}", got)
        B = {_BIG[items[i][0]]: g for i, g in zip(ids, got)}
        if part == "ffn":
            return dict(w_ffn_in=B["w_ffn_in"], w_ffn_out=B["w_ffn_out"].reshape(-1, D))
        W = {}
        if l < NA:
            W.update(w_rec_in=B["w_rec_in"],
                     w_gates=B["w_lru_gates"].reshape(N_CHIPS, NBLK, BW, GS).transpose(1, 2, 0, 3).reshape(
                         NBLK, BW, N_CHIPS * GS),
                     b_gates=b_gates[l], w_rec_out=B["w_rec_out"].reshape(C, D),
                     conv_w=small_a[l, :4], conv_b=small_a[l, 4:5], lru_param=small_a[l, 5:6])
        else:
            W.update(w_q=B["w_q"].reshape(D, D), w_o=B["w_o"].reshape(D, D))
            if l == NA:
                w_kvf_full = B["w_kvf"].transpose(1, 0, 2).reshape(D, -1)
                W.update(norm_kv=norm_kv[None], w_kv=w_kvf_full[:, :2 * D],
                         w_f=_pad_lanes(w_kvf_full[:, 2 * D:], LANES), b_f=_pad_lanes(b_forget[None], LANES))
        return W

    G_small = {l: {} for l in range(L)}
    pending = {}

    def layer_grads(l, part, G):
        G_small[l].update(G)
        by_name = dict(
            w_ffn_in=lambda: G["w_ffn_in"], w_ffn_out=lambda: G["w_ffn_out"].reshape(N_CHIPS, -1, D),
            w_rec_in=lambda: G["w_rec_in"],
            w_lru_gates=lambda: G["w_gates"].reshape(NBLK, BW, N_CHIPS, GS).transpose(2, 0, 1, 3).reshape(
                N_CHIPS, NBLK * BW, GS),
            w_rec_out=lambda: G["w_rec_out"].reshape(N_CHIPS, -1, D),
            w_kvf=lambda: jnp.concatenate([G["w_kv"].astype(F32), G["w_f"][:, :H]], axis=1).reshape(
                D, N_CHIPS, -1).transpose(1, 0, 2).astype(BF16),
            w_q=lambda: G["w_q"].reshape(N_CHIPS, -1, D), w_o=lambda: G["w_o"].reshape(N_CHIPS, -1, D))
        its = stage_items(l, part)
        grads = [by_name[_BIG[w]]() for w, _ in its]
        others = _reduce_d2d(f"reduce_d2d_{part}_{l}", grads)
        parts = [_sum_cores(f"sum_cores_{l}_{_BIG[w]}", g, o, core_id) for (w, _), g, o in zip(its, grads, others)]
        recv, sems, token = _scatter_start(f"scatter_start_{part}_{l}", parts)
        pending[(l, part)] = (parts, recv, sems)
        return token

    gains = dict(mix=[norm_mix[l][None] for l in range(L)], ffn=[norm_ffn[l][None] for l in range(L)],
                 final=norm_final[None])
    loss_row, grad_x, dg_final = _local_step(x.reshape(S, D), loss_target.reshape(S, D), gains,
                                             layer_weights, layer_grads)

    rows = [*[G_small[l]["norm_mix"] for l in range(L)], *[G_small[l]["norm_ffn"] for l in range(L)],
            G_small[NA]["norm_kv"], dg_final, _pad_lanes(G_small[NA]["b_f"], D), _pad_lanes(loss_row, D)]
    for a in range(NA):
        rows += [G_small[a][n] for n in ("conv_w", "conv_b", "b_gi", "b_gr", "lru_param")]
    packed = jnp.concatenate(rows, axis=0)
    tot = _sum_slots("sum_small", _gather_all("gather_small", packed))
    loss = tot[2 * L + 3, 0]
    g_rep = jnp.concatenate([tot[:2 * L + 2], tot[2 * L + 2:2 * L + 3]], axis=0)
    base = 2 * L + 4
    g_sh = []
    for a in range(NA):
        blk = lax.dynamic_slice_in_dim(tot[base + 8 * a:base + 8 * a + 8], chip * CS, CS, axis=1)
        gi = tot[base + 8 * a + 5].reshape(NBLK, BW)
        gr = tot[base + 8 * a + 6].reshape(NBLK, BW)
        bl = lax.dynamic_slice_in_dim(jnp.concatenate([gi, gr], axis=1), chip * GS, GS, axis=1)
        g_sh += [blk[:5], bl.reshape(-1, CS), blk[7:8]]
    g_sh = jnp.concatenate(g_sh, axis=0)
    nrow = g_sh.shape[0] // NA

    def pack_rep(T):
        return jnp.concatenate([T["norm_mix"], T["norm_ffn"], T["norm_kv"][None], T["norm_final"][None],
                                _pad_lanes(T["b_forget"][None], D)], axis=0)

    def pack_sh(T):
        return jnp.concatenate([jnp.concatenate([T["conv_w"][a], T["conv_b"][a][None],
                                                 T["b_lru_gates"][a].reshape(-1, CS), T["lru_param"][a][None]], axis=0)
                                for a in range(NA)], axis=0)

    rep = _adamw("adamw_replicated", [g_rep], pack_rep(P), pack_rep(M1), pack_rep(M2))
    shd = _adamw("adamw_small_sharded", [g_sh], pack_sh(P), pack_sh(M1), pack_sh(M2))

    def unpack_rep(t):
        return dict(norm_mix=t[:L], norm_ffn=t[L:2 * L], norm_kv=t[2 * L], norm_final=t[2 * L + 1],
                    b_forget=t[2 * L + 2, :H])

    def unpack_sh(t):
        t = t.reshape(NA, nrow, CS)
        return dict(conv_w=t[:, :4], conv_b=t[:, 4], b_lru_gates=t[:, 5:nrow - 1].reshape(NA, NBLK, GS),
                    lru_param=t[:, nrow - 1])

    full = [lax.empty(sh.shape, F32) for sh in shards]
    for l, part in reversed(stages):
        parts, recv, sems = pending[(l, part)]
        recv = _scatter_wait(f"scatter_wait_{part}_{l}", parts, recv, sems)
        for (w, li), own, r in zip(stage_items(l, part), parts, recv):
            full[w] = _sum_chips(f"sum_chips_{l}_{_BIG[w]}", r, own, full[w], li, me_core)
    full = _share_d2d("share_d2d", full)
    big = {w: _adamw(f"adamw_{w}", [g.reshape(P[w].shape)], P[w], M1[w], M2[w]) for w, g in zip(_BIG, full)}

    outs = []
    for i in range(4):
        small = {**unpack_rep(rep[i]), **unpack_sh(shd[i])}
        outs.append([big[w][i] if w in big else small[w] for w in _WEIGHTS])
    return (loss, grad_x.reshape(1, S, D), *outs[0], *outs[1], *outs[2], *outs[3])
```

```python
import functools
import math

import jax
import jax.numpy as jnp
from jax import lax
from jax.experimental import pallas as pl
from jax.experimental.pallas import tpu as pltpu

F32 = jnp.float32
BF16 = jnp.bfloat16

EPS = 1e-6
LRU_C = 8.0
HEAD_DIM = 64
LANES = 128
SUBLANES = 8
VMEM_LIMIT = 48 * 1024 * 1024
N_CHIPS = 4
N_DEV = 8

ADAM_LR = 0.001
ADAM_B1 = 0.9
ADAM_B2 = 0.999
ADAM_EPS = 1e-08
ADAM_WD = 0.01
ADAM_STEP = 10

_NN = (((1,), (0,)), ((), ()))
_NT = (((1,), (1,)), ((), ()))
_TN = (((0,), (0,)), ((), ()))
_DN = {"nn": _NN, "nt": _NT, "tn": _TN}
MESH = pl.DeviceIdType.MESH


def _hbm_out(shape, dtype):
    return pltpu.HBM(shape, dtype)


def _params(sem):
    return pltpu.CompilerParams(dimension_semantics=sem, vmem_limit_bytes=VMEM_LIMIT)


def _tile(n, want):
    if n <= want:
        return n
    t = (want // LANES) * LANES
    while t >= LANES:
        if n % t == 0:
            return t
        t -= LANES
    return n


def _sigmoid(x):
    return 1.0 / (1.0 + jnp.exp(-x))


def _softplus(x):
    return jnp.maximum(x, 0.0) + jnp.log(1.0 + jnp.exp(-jnp.abs(x)))


_GELU_C = math.sqrt(2.0 / math.pi)


def _gelu_and_grad(x):
    inner = _GELU_C * (x + 0.044715 * x * x * x)
    t = jnp.tanh(inner)
    g = 0.5 * x * (1.0 + t)
    dg = 0.5 * (1.0 + t) + 0.5 * x * (1.0 - t * t) * _GELU_C * (1.0 + 3.0 * 0.044715 * x * x)
    return g, dg


def _rms(x):
    return lax.rsqrt(jnp.mean(x * x, axis=-1, keepdims=True) + EPS)


def _rms_bwd(dy, x, g):
    r = _rms(x)
    xr = x * r
    dyg = dy * g
    return r * dyg - xr * (r * jnp.mean(dyg * xr, axis=-1, keepdims=True)), jnp.sum(dy * xr, axis=0, keepdims=True)


def _mm(name, mode, a, b, *, grid, a_spec, b_spec, out_shape, out_dtype, out_spec, nk=1,
        res=None, res_spec=None, bias=None, bias_spec=None, scale=None, norm_gain=None, norm_bwd=None):
    dn = _DN[mode]
    has_res, has_bias = res is not None, bias is not None
    blk = tuple(d for d in out_spec.block_shape if d is not None)
    vec = pl.BlockSpec((1, blk[-1]), lambda *g: (0, 0))
    n_in = 2 + int(has_res) + int(has_bias) + (1 if norm_gain is not None else 0) + (3 if norm_bwd else 0)

    def body(*refs):
        a_ref, b_ref = refs[0], refs[1]
        p = 2
        r_ref = refs[p] if has_res else None
        p += int(has_res)
        bias_ref = refs[p] if has_bias else None
        p += int(has_bias)
        extra = refs[p:n_in]
        outs = refs[n_in:]
        o_ref = outs[0]
        part = lax.dot_general(a_ref[...], b_ref[...], dn, preferred_element_type=F32)

        def finish(acc):
            if scale is not None:
                acc = acc * scale
            if has_bias:
                acc = acc + bias_ref[...]
            if has_res:
                acc = r_ref[...] + acc
            if norm_bwd:
                h_ref, g_ref, dh_ref = extra
                dx, dg = _rms_bwd(acc, h_ref[...], g_ref[...])
                acc = dh_ref[...] + dx
                outs[1][...] = acc.astype(BF16)
                outs[2][...] = dg
            if norm_gain is not None:
                outs[1][...] = (acc * _rms(acc) * extra[0][...]).astype(BF16)
            o_ref[...] = acc.astype(o_ref.dtype)

        if nk == 1:
            finish(part)
        else:
            acc_ref = refs[-1]
            k = pl.program_id(2)

            @pl.when(k == 0)
            def _():
                acc_ref[...] = part

            @pl.when(k > 0)
            def _():
                acc_ref[...] += part

            @pl.when(k == nk - 1)
            def _():
                finish(acc_ref[...])

    ins, specs = [a, b], [a_spec, b_spec]
    if has_res:
        ins.append(res)
        specs.append(res_spec)
    if has_bias:
        ins.append(bias)
        specs.append(bias_spec)
    out_specs, out_shapes = [out_spec], [_hbm_out(out_shape, out_dtype)]
    if norm_gain is not None:
        ins.append(norm_gain)
        specs.append(vec)
        out_specs.append(out_spec)
        out_shapes.append(_hbm_out(out_shape, BF16))
    if norm_bwd:
        h, g, dh = norm_bwd
        ins += [h, g, dh]
        specs += [out_spec, vec, out_spec]
        out_specs += [out_spec, pl.BlockSpec((None, 1, blk[-1]), lambda i, *rest: (i, 0, 0))]
        out_shapes += [_hbm_out(out_shape, BF16), _hbm_out((grid[0], 1, blk[-1]), F32)]
    sem = ("parallel", "parallel") + (("arbitrary",) if len(grid) == 3 else ())
    single = len(out_specs) == 1
    return pl.pallas_call(
        body, name=name, grid=grid, in_specs=specs, out_specs=out_specs[0] if single else out_specs,
        out_shape=out_shapes[0] if single else out_shapes,
        scratch_shapes=[pltpu.VMEM(blk, F32)] if nk > 1 else [],
        compiler_params=_params(sem),
    )(*ins)


def _mm_nn(name, a, b, *, b_lead=(), out_dtype, tm=512, tn=512, res=None, bias=None, scale=None, norm_gain=None):
    M, K = a.shape
    N = b.shape[-1]
    tm, tn = _tile(M, tm), _tile(N, tn)
    nl = len(b_lead)
    return _mm(
        name, "nn", a, b, grid=(M // tm, N // tn),
        a_spec=pl.BlockSpec((tm, K), lambda i, j: (i, 0)),
        b_spec=pl.BlockSpec((None,) * nl + (K, tn), lambda i, j: tuple(b_lead) + (0, j)),
        out_shape=(M, N), out_dtype=out_dtype, out_spec=pl.BlockSpec((tm, tn), lambda i, j: (i, j)),
        res=res, res_spec=pl.BlockSpec((tm, tn), lambda i, j: (i, j)),
        bias=bias, bias_spec=pl.BlockSpec((1, tn), lambda i, j: (0, j)), scale=scale, norm_gain=norm_gain)


def _mm_nt(name, a, b, *, b_lead=(), out_dtype, tm=512, tn=512, tk=2048, res=None, norm_bwd=None):
    M, K = a.shape
    N = b.shape[-2]
    tm, tn, tk = _tile(M, tm), _tile(N, tn), _tile(K, tk)
    nk = K // tk
    nl = len(b_lead)
    return _mm(
        name, "nt", a, b, grid=(M // tm, N // tn, nk), nk=nk,
        a_spec=pl.BlockSpec((tm, tk), lambda i, j, k: (i, k)),
        b_spec=pl.BlockSpec((None,) * nl + (tn, tk), lambda i, j, k: tuple(b_lead) + (j, k)),
        out_shape=(M, N), out_dtype=out_dtype, out_spec=pl.BlockSpec((tm, tn), lambda i, j, k: (i, j)),
        res=res, res_spec=pl.BlockSpec((tm, tn), lambda i, j, k: (i, j)), norm_bwd=norm_bwd)


def _mm_tn(name, a, b, *, out_dtype, tm=512, tn=512):
    S, M = a.shape
    N = b.shape[1]
    tm, tn = _tile(M, tm), _tile(N, tn)
    return _mm(
        name, "tn", a, b, grid=(M // tm, N // tn),
        a_spec=pl.BlockSpec((S, tm), lambda i, j: (0, i)),
        b_spec=pl.BlockSpec((S, tn), lambda i, j: (0, j)),
        out_shape=(M, N), out_dtype=out_dtype, out_spec=pl.BlockSpec((tm, tn), lambda i, j: (i, j)))


def _rmsnorm_fwd(name, h, g, tr=256):
    S, D = h.shape
    tr = _tile(S, tr)

    def body(h_ref, g_ref, o_ref):
        x = h_ref[...]
        r = lax.rsqrt(jnp.mean(x * x, axis=-1, keepdims=True) + EPS)
        o_ref[...] = (x * r * g_ref[...]).astype(o_ref.dtype)

    return pl.pallas_call(
        body, name=name, grid=(S // tr,),
        in_specs=[pl.BlockSpec((tr, D), lambda i: (i, 0)), pl.BlockSpec((1, D), lambda i: (0, 0))],
        out_specs=pl.BlockSpec((tr, D), lambda i: (i, 0)),
        out_shape=_hbm_out((S, D), BF16),
        compiler_params=_params(("parallel",)),
    )(h, g)


def _loss_head(name, h, target, g, tr=256):
    S, D = h.shape
    tr = _tile(S, tr)

    def body(h_ref, t_ref, g_ref, o_ref, ob_ref, dg_ref, loss_ref):
        i = pl.program_id(0)
        x = h_ref[...]
        gg = g_ref[...]
        r = lax.rsqrt(jnp.mean(x * x, axis=-1, keepdims=True) + EPS)
        xr = x * r
        err = xr * gg - t_ref[...]
        lpart = 0.5 * jnp.sum(jnp.mean(err * err, axis=-1, keepdims=True), axis=0, keepdims=True)
        dy = err * (1.0 / D)
        dyg = dy * gg
        dx = r * dyg - xr * (r * jnp.mean(dyg * xr, axis=-1, keepdims=True))
        o_ref[...] = dx
        ob_ref[...] = dx.astype(BF16)
        part = jnp.sum(dy * xr, axis=0, keepdims=True)
        lrow = jnp.broadcast_to(lpart, (1, LANES))

        @pl.when(i == 0)
        def _():
            dg_ref[...] = part
            loss_ref[...] = lrow

        @pl.when(i > 0)
        def _():
            dg_ref[...] += part
            loss_ref[...] += lrow

    row = pl.BlockSpec((tr, D), lambda i: (i, 0))
    vec = pl.BlockSpec((1, D), lambda i: (0, 0))
    return pl.pallas_call(
        body, name=name, grid=(S // tr,),
        in_specs=[row, row, vec], out_specs=[row, row, vec, pl.BlockSpec((1, LANES), lambda i: (0, 0))],
        out_shape=[_hbm_out((S, D), F32), _hbm_out((S, D), BF16),
                   _hbm_out((1, D), F32), _hbm_out((1, LANES), F32)],
        compiler_params=_params(("arbitrary",)),
    )(h, target, g)


def _swiglu_fwd(name, hn, w_in, tm=512):
    S, D = hn.shape
    FH = w_in.shape[-1]
    tm = _tile(S, tm)

    def body(x_ref, wg_ref, wu_ref, z_ref, a_ref):
        x = x_ref[...]
        zg = jnp.dot(x, wg_ref[...], preferred_element_type=F32)
        zu = jnp.dot(x, wu_ref[...], preferred_element_type=F32)
        z_ref[0] = zg.astype(z_ref.dtype)
        z_ref[1] = zu.astype(z_ref.dtype)
        a_ref[...] = (zg * _sigmoid(zg) * zu).astype(a_ref.dtype)

    return pl.pallas_call(
        body, name=name, grid=(S // tm, 2),
        in_specs=[pl.BlockSpec((tm, D), lambda i, j: (i, 0)),
                  pl.BlockSpec((None, D, FH), lambda i, j: (j, 0, 0)),
                  pl.BlockSpec((None, D, FH), lambda i, j: (j + 2, 0, 0))],
        out_specs=[pl.BlockSpec((2, tm, FH), lambda i, j: (0, i, j)), pl.BlockSpec((tm, FH), lambda i, j: (i, j))],
        out_shape=[_hbm_out((2, S, 2 * FH), BF16), _hbm_out((S, 2 * FH), BF16)],
        compiler_params=_params(("parallel", "parallel")),
    )(hn, w_in, w_in)


def _swiglu_bwd(name, dhb, w_out, z3, tm=512):
    S, D = dhb.shape
    F = w_out.shape[0]
    FH = F // 2
    tm = _tile(S, tm)

    def body(d_ref, w_ref, z_ref, dz_ref):
        d = lax.dot_general(d_ref[...], w_ref[...], _NT, preferred_element_type=F32)
        zg = z_ref[0].astype(F32)
        zu = z_ref[1].astype(F32)
        sg = _sigmoid(zg)
        dz_ref[0] = (d * zu * (sg * (1.0 + zg * (1.0 - sg)))).astype(dz_ref.dtype)
        dz_ref[1] = (d * (zg * sg)).astype(dz_ref.dtype)

    zspec = pl.BlockSpec((2, tm, FH), lambda i, j: (0, i, j))
    return pl.pallas_call(
        body, name=name, grid=(S // tm, 2),
        in_specs=[pl.BlockSpec((tm, D), lambda i, j: (i, 0)), pl.BlockSpec((FH, D), lambda i, j: (j, 0)), zspec],
        out_specs=zspec, out_shape=_hbm_out((2, S, F), BF16),
        compiler_params=_params(("parallel", "parallel")),
    )(dhb, w_out, z3)


SCAN_ROWS = 64


def _group_scan(A, B, reverse):
    n = A.shape[0]
    sub = lax.broadcasted_iota(jnp.int32, A.shape, 0) % SUBLANES
    for d in (1, 2, 4):
        if reverse:
            A_sh, B_sh = pltpu.roll(A, n - d, 0), pltpu.roll(B, n - d, 0)
            keep = sub < SUBLANES - d
        else:
            A_sh, B_sh = pltpu.roll(A, d, 0), pltpu.roll(B, d, 0)
            keep = sub >= d
        B = jnp.where(keep, A * B_sh + B, B)
        A = jnp.where(keep, A * A_sh, A)
    return A, B


def _block_scan(a, u, carry, reverse):
    A, B = _group_scan(a, u, reverse)
    ng = a.shape[0] // SUBLANES
    out = [None] * ng
    order = range(ng - 1, -1, -1) if reverse else range(ng)
    for gi in order:
        sl = slice(gi * SUBLANES, (gi + 1) * SUBLANES)
        hg = A[sl] * carry + B[sl]
        out[gi] = hg
        carry = hg[0:1] if reverse else hg[SUBLANES - 1:SUBLANES]
    return jnp.concatenate(out, axis=0), carry


def _lru_gates(rc, gip, grp, sp):
    gi = _sigmoid(gip)
    gr = _sigmoid(grp)
    la = -LRU_C * gr * sp
    a = jnp.exp(la)
    om = -jnp.tanh(la) * (a * a + 1.0)
    mult = jnp.sqrt(om)
    return gi, gr, a, mult


def _lru_fwd(name, proj, rc, gip, grp, lru_p, tc=256):
    S, C = rc.shape
    tc = _tile(C, tc)
    nb = S // SCAN_ROWS

    def body(gb_ref, rc_ref, gi_ref, gr_ref, l_ref, h_ref, m_ref):
        sp = _softplus(-l_ref[...])

        def step(b, carry):
            rows = pl.ds(pl.multiple_of(b * SCAN_ROWS, SCAN_ROWS), SCAN_ROWS)
            rcb = rc_ref[rows, :]
            gi, _, a, mult = _lru_gates(rcb, gi_ref[rows, :], gr_ref[rows, :], sp)
            h, carry = _block_scan(a, rcb * gi * mult, carry, False)
            h_ref[rows, :] = h
            gel, _ = _gelu_and_grad(gb_ref[rows, :])
            m_ref[rows, :] = (gel * h).astype(m_ref.dtype)
            return carry

        lax.fori_loop(0, nb, step, jnp.zeros((1, tc), F32))

    col = pl.BlockSpec((S, tc), lambda j: (0, j))
    return pl.pallas_call(
        body, name=name, grid=(C // tc,),
        in_specs=[col, col, col, col, pl.BlockSpec((1, tc), lambda j: (0, j))],
        out_specs=[col, col],
        out_shape=[_hbm_out((S, C), F32), _hbm_out((S, C), BF16)],
        compiler_params=_params(("parallel",)),
    )(proj, rc, gip, grp, lru_p)


def _lru_bwd(name, dm, proj, hrec, rc, gip, grp, lru_p, tc=256):
    S, C = rc.shape
    tc = _tile(C, tc)
    nb = S // SCAN_ROWS
    R = SCAN_ROWS

    def body(dm_ref, gb_ref, h_ref, rc_ref, gi_ref, gr_ref, l_ref,
             dgb_ref, dgi_ref, dgr_ref, drc_ref, dbi_ref, dbr_ref, dl_ref):
        lp = l_ref[...]
        sp = _softplus(-lp)
        row = lax.broadcasted_iota(jnp.int32, (R, tc), 0)
        zero = jnp.zeros((1, tc), F32)

        def step(t, carry):
            mu_in, s_i, s_r, s_sp = carry
            b = nb - 1 - t
            r0 = pl.multiple_of(b * R, R)
            rows = pl.ds(r0, R)
            rcb = rc_ref[rows, :]
            gi, gr, a, mult = _lru_gates(rcb, gi_ref[rows, :], gr_ref[rows, :], sp)
            gel, dgel = _gelu_and_grad(gb_ref[rows, :])
            dmb = dm_ref[rows, :]
            h = h_ref[rows, :]
            dgb_ref[rows, :] = (dmb * h * dgel).astype(dgb_ref.dtype)
            dh = dmb * gel
            mu, mu_out = _block_scan(a, a * dh, mu_in, True)
            mu_next = jnp.where(row == R - 1, mu_in, pltpu.roll(mu, R - 1, 0))
            lam = dh + mu_next
            p0 = pl.multiple_of(jnp.maximum(r0 - SUBLANES, 0), SUBLANES)
            prev = h_ref[pl.ds(p0, SUBLANES), :][SUBLANES - 1:SUBLANES]
            prev = jnp.where(b > 0, prev, 0.0)
            h_prev = jnp.where(row == 0, prev, pltpu.roll(h, 1, 0))
            da = lam * h_prev
            d_mult = lam * rcb * gi
            d_la = da * a - d_mult * (a * a) / mult
            d_grp = d_la * (-LRU_C * sp) * gr * (1.0 - gr)
            d_gip = lam * rcb * mult * gi * (1.0 - gi)
            dgr_ref[rows, :] = d_grp.astype(dgr_ref.dtype)
            dgi_ref[rows, :] = d_gip.astype(dgi_ref.dtype)
            drc_ref[rows, :] = lam * gi * mult
            s_i = s_i + jnp.sum(d_gip, axis=0, keepdims=True)
            s_r = s_r + jnp.sum(d_grp, axis=0, keepdims=True)
            s_sp = s_sp + jnp.sum(d_la * gr, axis=0, keepdims=True)
            return mu_out, s_i, s_r, s_sp

        _, s_i, s_r, s_sp = lax.fori_loop(0, nb, step, (zero, zero, zero, zero))
        dbi_ref[...] = s_i
        dbr_ref[...] = s_r
        dl_ref[...] = (-LRU_C * s_sp) * (-_sigmoid(-lp))

    col = pl.BlockSpec((S, tc), lambda j: (0, j))
    vec = pl.BlockSpec((1, tc), lambda j: (0, j))
    return pl.pallas_call(
        body, name=name, grid=(C // tc,),
        in_specs=[col, col, col, col, col, col, vec],
        out_specs=[col, col, col, col, vec, vec, vec],
        out_shape=[_hbm_out((S, C), BF16), _hbm_out((S, C), BF16),
                   _hbm_out((S, C), BF16), _hbm_out((S, C), F32),
                   _hbm_out((1, C), F32), _hbm_out((1, C), F32),
                   _hbm_out((1, C), F32)],
        compiler_params=_params(("parallel",)),
    )(dm, proj, hrec, rc, gip, grp, lru_p)


def _cumsum_rows(name, u, reverse):
    S, C = u.shape
    nb = S // SCAN_ROWS

    def body(u_ref, o_ref):
        def step(t, carry):
            b = nb - 1 - t if reverse else t
            rows = pl.ds(pl.multiple_of(b * SCAN_ROWS, SCAN_ROWS), SCAN_ROWS)
            ub = u_ref[rows, :]
            h, carry = _block_scan(jnp.ones_like(ub), ub, carry, reverse)
            o_ref[rows, :] = h
            return carry

        lax.fori_loop(0, nb, step, jnp.zeros((1, C), F32))

    spec = pl.BlockSpec((S, C), lambda i: (0, 0))
    return pl.pallas_call(
        body, name=name, grid=(1,), in_specs=[spec], out_specs=spec,
        out_shape=_hbm_out((S, C), F32),
        compiler_params=_params(("arbitrary",)),
    )(u)


def _shift_down(x, k):
    row = lax.broadcasted_iota(jnp.int32, x.shape, 0)
    return jnp.where(row >= k, pltpu.roll(x, k, 0), 0.0)


def _shift_up(x, k):
    n = x.shape[0]
    row = lax.broadcasted_iota(jnp.int32, x.shape, 0)
    return jnp.where(row < n - k, pltpu.roll(x, n - k, 0), 0.0)


def _conv_fwd(name, proj, w, b, tc=256):
    S, C2 = proj.shape
    C = C2 // 2
    tc = _tile(C, tc)
    off = C // tc

    def body(x_ref, w_ref, b_ref, o_ref, ob_ref):
        x = x_ref[...]
        out = b_ref[...] + w_ref[3:4, :] * x
        for k in (1, 2, 3):
            out = out + w_ref[3 - k:4 - k, :] * _shift_down(x, k)
        o_ref[...] = out
        ob_ref[...] = out.astype(BF16)

    col = pl.BlockSpec((S, tc), lambda j: (0, j))
    return pl.pallas_call(
        body, name=name, grid=(C // tc,),
        in_specs=[pl.BlockSpec((S, tc), lambda j: (0, off + j)),
                  pl.BlockSpec((4, tc), lambda j: (0, j)), pl.BlockSpec((1, tc), lambda j: (0, j))],
        out_specs=[col, col],
        out_shape=[_hbm_out((S, C), F32), _hbm_out((S, C), BF16)],
        compiler_params=_params(("parallel",)),
    )(proj, w, b)


def _conv_bwd(name, drc, proj, w, tc=256):
    S, C = drc.shape
    tc = _tile(C, tc)
    off = C // tc

    def body(y_ref, x_ref, w_ref, dx_ref, dw_ref, db_ref):
        y = y_ref[...]
        x = x_ref[...]
        dx = w_ref[3:4, :] * y
        dw_ref[3:4, :] = jnp.sum(y * x, axis=0, keepdims=True)
        for k in (1, 2, 3):
            dx = dx + w_ref[3 - k:4 - k, :] * _shift_up(y, k)
            dw_ref[3 - k:4 - k, :] = jnp.sum(y * _shift_down(x, k), axis=0, keepdims=True)
        dx_ref[...] = dx.astype(dx_ref.dtype)
        db_ref[...] = jnp.sum(y, axis=0, keepdims=True)

    col = pl.BlockSpec((S, tc), lambda j: (0, j))
    return pl.pallas_call(
        body, name=name, grid=(C // tc,),
        in_specs=[col, pl.BlockSpec((S, tc), lambda j: (0, off + j)), pl.BlockSpec((4, tc), lambda j: (0, j))],
        out_specs=[col, pl.BlockSpec((4, tc), lambda j: (0, j)), pl.BlockSpec((1, tc), lambda j: (0, j))],
        out_shape=[_hbm_out((S, C), BF16), _hbm_out((4, C), F32),
                   _hbm_out((1, C), F32)],
        compiler_params=_params(("parallel",)),
    )(drc, proj, w)


def _gates_fwd(name, rcb, wg, bg):
    S, C = rcb.shape
    nblk, bw, _ = wg.shape

    def body(x_ref, w_ref, b_ref, gi_ref, gr_ref):
        g = jnp.dot(x_ref[...], w_ref[...], preferred_element_type=F32) + b_ref[...]
        gi_ref[...] = g[:, :bw]
        gr_ref[...] = g[:, bw:]

    col = pl.BlockSpec((S, bw), lambda n: (0, n))
    return pl.pallas_call(
        body, name=name, grid=(nblk,),
        in_specs=[col, pl.BlockSpec((None, bw, 2 * bw), lambda n: (n, 0, 0)),
                  pl.BlockSpec((None, 1, 2 * bw), lambda n: (n, 0, 0))],
        out_specs=[col, col],
        out_shape=[_hbm_out((S, C), F32), _hbm_out((S, C), F32)],
        compiler_params=_params(("parallel",)),
    )(rcb, wg, bg)


def _gates_bwd(name, dgi, dgr, rcb, wg, drc1):
    S, C = rcb.shape
    nblk, bw, _ = wg.shape

    def body(dgi_ref, dgr_ref, x_ref, w_ref, d1_ref, drc_ref, dw_ref):
        w = w_ref[...]
        x = x_ref[...]
        di, dr = dgi_ref[...], dgr_ref[...]
        drc_ref[...] = (d1_ref[...]
                        + lax.dot_general(di, w[:, :bw], _NT, preferred_element_type=F32)
                        + lax.dot_general(dr, w[:, bw:], _NT, preferred_element_type=F32))
        dw_ref[:, :bw] = lax.dot_general(x, di, _TN, preferred_element_type=F32).astype(dw_ref.dtype)
        dw_ref[:, bw:] = lax.dot_general(x, dr, _TN, preferred_element_type=F32).astype(dw_ref.dtype)

    col = pl.BlockSpec((S, bw), lambda n: (0, n))
    wspec = pl.BlockSpec((None, bw, 2 * bw), lambda n: (n, 0, 0))
    return pl.pallas_call(
        body, name=name, grid=(nblk,),
        in_specs=[col, col, col, wspec, col], out_specs=[col, wspec],
        out_shape=[_hbm_out((S, C), F32), _hbm_out((nblk, bw, 2 * bw), BF16)],
        compiler_params=_params(("parallel",)),
    )(dgi, dgr, rcb, wg, drc1)


def _att_tile(S):
    return next(t for t in (512, 256, 128) if S % t == 0)


def _head_lanes(shape):
    return lax.broadcasted_iota(jnp.int32, shape, len(shape) - 1) < HEAD_DIM


def _key_bias(c_blk):
    first = _head_lanes(c_blk.shape)
    rolled = pltpu.roll(c_blk, HEAD_DIM, 1)
    return jnp.where(first, c_blk, rolled), jnp.where(first, rolled, c_blk)


def _over_keys(x, op):
    n = x.shape[0]
    while n > SUBLANES:
        n //= 2
        x = op(x[:n], x[n:2 * n])
    return (jnp.max if op is jnp.maximum else jnp.sum)(x, axis=0, keepdims=True)


def _causal_t(T, cc):
    r = lax.broadcasted_iota(jnp.int32, (T, LANES), 0)
    c = lax.broadcasted_iota(jnp.int32, (T, LANES), 1) + cc * LANES
    return r <= c


def _attn_fwd(name, q, kv, cfull):
    S, D = q.shape
    HP = D // LANES
    T = _att_tile(S)
    nq = S // T
    NC = T // LANES

    def body(q_ref, k_ref, v_ref, c_ref, o_ref, of_ref, lse_ref, bias, vT, acc, m_scr, l_scr):
        def prologue(i, _):
            rows = pl.ds(pl.multiple_of(i * T, T), T)
            bias[0, rows, :], bias[1, rows, :] = _key_bias(c_ref[rows, :])
            vT[i] = v_ref[rows, :].astype(F32).T.astype(BF16)
            return 0

        lax.fori_loop(0, nq, prologue, 0)

        def q_step(qi, _):
            q0 = pl.multiple_of(qi * T, T)
            qb = q_ref[pl.ds(q0, T), :]
            m_scr[...] = jnp.full(m_scr.shape, -jnp.inf, F32)
            l_scr[...] = jnp.zeros(l_scr.shape, F32)
            acc[...] = jnp.zeros(acc.shape, F32)

            def tile(kj, masked):
                ks = pl.ds(pl.multiple_of(kj * T, T), T)
                kf = k_ref[ks, :].astype(F32)
                first = _head_lanes(kf.shape)
                kms = [jnp.where(first if hh == 0 else jnp.logical_not(first), kf, 0.0).astype(BF16) for hh in range(2)]
                sTs = [lax.dot_general(km, qb, _NT, preferred_element_type=F32) for km in kms]
                for hh in range(2):
                    b = bias[hh, ks, :]
                    ps = []
                    for cc in range(NC):
                        cols = slice(cc * LANES, (cc + 1) * LANES)
                        s = sTs[hh][:, cols] + b
                        if masked:
                            s = jnp.where(_causal_t(T, cc), s, -jnp.inf)
                        m_old = m_scr[hh, cc]
                        m_new = jnp.maximum(m_old, _over_keys(s, jnp.maximum))
                        alpha = jnp.exp(m_old - m_new)
                        p = jnp.exp(s - m_new)
                        l_scr[hh, cc] = alpha * l_scr[hh, cc] + _over_keys(p, jnp.add)
                        m_scr[hh, cc] = m_new
                        ps.append(p.astype(BF16))
                        acc[hh, :, cols] = acc[hh, :, cols] * alpha
                    acc[hh] += jnp.dot(vT[kj, hh * HEAD_DIM:(hh + 1) * HEAD_DIM, :], jnp.concatenate(ps, axis=1),
                                       preferred_element_type=F32)

            def inner(kj, _):
                tile(kj, False)
                return 0

            lax.fori_loop(0, qi, inner, 0)
            tile(qi, True)
            outs = []
            for hh in range(2):
                inv = jnp.concatenate([1.0 / l_scr[hh, cc] for cc in range(NC)], axis=1)
                outs.append(acc[hh] * inv)
                for cc in range(NC):
                    lse_ref[hh:hh + 1, pl.ds(q0 + cc * LANES, LANES)] = m_scr[hh, cc] + jnp.log(l_scr[hh, cc])
            out = jnp.concatenate(outs, axis=0).T
            o_ref[pl.ds(q0, T), :] = out.astype(o_ref.dtype)
            of_ref[pl.ds(q0, T), :] = out
            return 0

        lax.fori_loop(0, nq, q_step, 0)

    blk = lambda off: pl.BlockSpec((S, LANES), lambda p: (0, off + p))
    return pl.pallas_call(
        body, name=name, grid=(HP,),
        in_specs=[blk(0), blk(0), blk(HP), blk(0)],
        out_specs=[blk(0), blk(0), pl.BlockSpec((None, 2, S), lambda p: (p, 0, 0))],
        out_shape=[_hbm_out((S, D), BF16), _hbm_out((S, D), F32),
                   _hbm_out((HP, 2, S), F32)],
        scratch_shapes=[pltpu.VMEM((2, S, LANES), F32), pltpu.VMEM((nq, LANES, T), BF16),
                        pltpu.VMEM((2, HEAD_DIM, T), F32), pltpu.VMEM((2, NC, 1, LANES), F32),
                        pltpu.VMEM((2, NC, 1, LANES), F32)],
        compiler_params=_params(("parallel",)),
    )(q, kv, kv, cfull)


def _attn_bwd(name, q, kv, cfull, of, do, lse3):
    S, D = q.shape
    HP = D // LANES
    T = _att_tile(S)
    nq = S // T
    NC = T // LANES
    scale = HEAD_DIM ** -0.5

    def body(q_ref, k_ref, v_ref, c_ref, of_ref, do_ref, lse_ref,
             dq_ref, dk_ref, dv_ref, dck_ref, drq_ref, bias, kT, dqT, delta, dr_scr):
        def prologue(i, _):
            rows = pl.ds(pl.multiple_of(i * T, T), T)
            bias[0, rows, :], bias[1, rows, :] = _key_bias(c_ref[rows, :])
            kT[i] = k_ref[rows, :].astype(F32).T.astype(BF16)
            prodT = (do_ref[rows, :].astype(F32) * of_ref[rows, :]).T
            for hh in range(2):
                delta[hh:hh + 1, rows] = jnp.sum(prodT[hh * HEAD_DIM:(hh + 1) * HEAD_DIM], axis=0, keepdims=True)
            dqT[i] = jnp.zeros((LANES, T), F32)
            return 0

        lax.fori_loop(0, nq, prologue, 0)
        dr_scr[...] = jnp.zeros(dr_scr.shape, F32)

        def kv_step(kj, _):
            ks = pl.ds(pl.multiple_of(kj * T, T), T)
            kf = k_ref[ks, :].astype(F32)
            vf = v_ref[ks, :].astype(F32)
            first = _head_lanes(kf.shape)
            masks = [first, jnp.logical_not(first)]
            kms = [jnp.where(m, kf, 0.0).astype(BF16) for m in masks]
            vms = [jnp.where(m, vf, 0.0).astype(BF16) for m in masks]

            def tile(qi, carry, masked):
                q0 = pl.multiple_of(qi * T, T)
                qb = q_ref[pl.ds(q0, T), :]
                dob = do_ref[pl.ds(q0, T), :]
                sTs = [lax.dot_general(km, qb, _NT, preferred_element_type=F32) for km in kms]
                dpTs = [lax.dot_general(vm, dob, _NT, preferred_element_type=F32) for vm in vms]
                out = []
                for hh in range(2):
                    dk_a, dv_a, dc_a = carry[3 * hh:3 * hh + 3]
                    b = bias[hh, ks, :]
                    head = slice(hh * HEAD_DIM, (hh + 1) * HEAD_DIM)
                    ps, dss = [], []
                    for cc in range(NC):
                        cols = slice(cc * LANES, (cc + 1) * LANES)
                        at = pl.ds(q0 + cc * LANES, LANES)
                        p = jnp.exp(sTs[hh][:, cols] + b - lse_ref[hh:hh + 1, at])
                        if masked:
                            p = jnp.where(_causal_t(T, cc), p, 0.0)
                        ds = p * (dpTs[hh][:, cols] - delta[hh:hh + 1, at])
                        ps.append(p.astype(BF16))
                        dss.append(ds.astype(BF16))
                        dc_a = dc_a + ds
                        dr_scr[hh:hh + 1, at] += _over_keys(ds, jnp.add)
                    pT = jnp.concatenate(ps, axis=1)
                    dsT = jnp.concatenate(dss, axis=1)
                    dv_a = dv_a + jnp.dot(pT, dob, preferred_element_type=F32)
                    dk_a = dk_a + jnp.dot(dsT, qb, preferred_element_type=F32)
                    dqT[qi, head, :] += jnp.dot(kT[kj, head, :], dsT, preferred_element_type=F32)
                    out += [dk_a, dv_a, dc_a]
                return tuple(out)

            zero = jnp.zeros((T, LANES), F32)
            carry = tile(kj, (zero,) * 6, True)
            dk0, dv0, dc0, dk1, dv1, dc1 = lax.fori_loop(kj + 1, nq, lambda qi, c: tile(qi, c, False), carry)
            dk_ref[ks, :] = jnp.where(first, dk0, dk1)
            dv_ref[ks, :] = jnp.where(first, dv0, dv1)
            dck_ref[ks, :] = jnp.where(first, jnp.broadcast_to(-jnp.sum(dc0, axis=1, keepdims=True), (T, LANES)),
                                       jnp.broadcast_to(-jnp.sum(dc1, axis=1, keepdims=True), (T, LANES)))
            return 0

        lax.fori_loop(0, nq, kv_step, 0)

        def epilogue(i, _):
            rows = pl.ds(pl.multiple_of(i * T, T), T)
            dq_ref[rows, :] = (dqT[i].T * scale).astype(dq_ref.dtype)
            return 0

        lax.fori_loop(0, nq, epilogue, 0)
        drq_ref[...] = dr_scr[...]

    blk = lambda off: pl.BlockSpec((S, LANES), lambda p: (0, off + p))
    row_spec = pl.BlockSpec((None, 2, S), lambda p: (p, 0, 0))
    return pl.pallas_call(
        body, name=name, grid=(HP,),
        in_specs=[blk(0), blk(0), blk(HP), blk(0), blk(0), blk(0), row_spec],
        out_specs=[blk(0), blk(0), blk(0), blk(0), row_spec],
        out_shape=[_hbm_out((S, D), BF16), _hbm_out((S, D), F32),
                   _hbm_out((S, D), F32), _hbm_out((S, D), F32),
                   _hbm_out((HP, 2, S), F32)],
        scratch_shapes=[pltpu.VMEM((2, S, LANES), F32), pltpu.VMEM((nq, LANES, T), BF16),
                        pltpu.VMEM((nq, LANES, T), F32), pltpu.VMEM((2, S), F32), pltpu.VMEM((2, S), F32)],
        compiler_params=_params(("parallel",)),
    )(q, kv, kv, cfull, of, do, lse3)


def _logsig_fwd(name, f):
    S, C = f.shape

    def body(f_ref, o_ref):
        o_ref[...] = -_softplus(-f_ref[...])

    spec = pl.BlockSpec((S, C), lambda i: (0, 0))
    return pl.pallas_call(body, name=name, grid=(1,), in_specs=[spec], out_specs=spec,
                          out_shape=_hbm_out((S, C), F32),
                          compiler_params=_params(("arbitrary",)))(f)


def _logsig_bwd(name, dls, f):
    S, C = f.shape

    def body(d_ref, f_ref, o_ref, s_ref):
        df = d_ref[...] * _sigmoid(-f_ref[...])
        o_ref[...] = df.astype(o_ref.dtype)
        s_ref[...] = jnp.sum(df, axis=0, keepdims=True)

    spec = pl.BlockSpec((S, C), lambda i: (0, 0))
    return pl.pallas_call(body, name=name, grid=(1,), in_specs=[spec, spec],
                          out_specs=[spec, pl.BlockSpec((1, C), lambda i: (0, 0))],
                          out_shape=[_hbm_out((S, C), BF16), _hbm_out((1, C), F32)],
                          compiler_params=_params(("arbitrary",)))(dls, f)


def _add_cast(name, parts, out_dtype, tr=256):
    S, C = parts[0].shape
    tr = _tile(S, tr)
    n = len(parts)

    def body(*refs):
        acc = refs[0][...].astype(F32)
        for r in refs[1:n]:
            acc = acc + r[...].astype(F32)
        refs[n][...] = acc.astype(out_dtype)

    spec = pl.BlockSpec((tr, C), lambda i: (i, 0))
    return pl.pallas_call(body, name=name, grid=(S // tr,), in_specs=[spec] * n, out_specs=spec,
                          out_shape=_hbm_out((S, C), out_dtype),
                          compiler_params=_params(("parallel",)))(*parts)


def _local_step(x, target, gains, layer_weights, layer_grads):
    S, D = x.shape
    HP = D // LANES
    scale = HEAD_DIM ** -0.5
    tm = _tile(S, 512)
    td = _tile(D, 512)
    saved = []
    h = x
    l = 0
    kv = cfull = f_pre = hn_kv = h_kv = None
    while True:
        W = layer_weights(l, "mix", h)
        if W is None:
            break
        recurrent = "w_rec_in" in W
        if l == 0:
            xn = _rmsnorm_fwd("mix_norm_0", h, gains["mix"][0])
        if recurrent:
            CH = W["w_rec_in"].shape[-1]
            C = 2 * CH
            proj = _mm(f"rec_in_{l}", "nn", xn, W["w_rec_in"], grid=(S // tm, N_CHIPS),
                       a_spec=pl.BlockSpec((tm, D), lambda i, j: (i, 0)),
                       b_spec=pl.BlockSpec((None, D, CH), lambda i, j: (j, 0, 0)),
                       out_shape=(S, 2 * C), out_dtype=F32,
                       out_spec=pl.BlockSpec((tm, CH), lambda i, j: (i, j)))
            rc, rcb = _conv_fwd(f"conv_{l}", proj, W["conv_w"], W["conv_b"])
            gip, grp = _gates_fwd(f"gates_{l}", rcb, W["w_gates"], W["b_gates"])
            hrec, m = _lru_fwd(f"lru_{l}", proj, rc, gip, grp, W["lru_param"])
            h_mid, hn = _mm_nn(f"rec_out_{l}", m, W["w_rec_out"], out_dtype=F32, res=h, tn=D, norm_gain=gains["ffn"][l])
            mix_saved = (xn, proj, rc, rcb, gip, grp, hrec, m)
        else:
            if "w_kv" in W:
                h_kv = h
                hn_kv = _rmsnorm_fwd("kv_norm", h, W["norm_kv"])
                kv = _mm_nn("kv_proj", hn_kv, W["w_kv"], out_dtype=BF16)
                f_pre = _mm_nn("f_proj", hn_kv, W["w_f"], out_dtype=F32, bias=W["b_f"])
                c = _cumsum_rows("c_cumsum", _logsig_fwd("logsig", f_pre), False)
                cfull = jnp.repeat(-c[:, :2 * HP], HEAD_DIM, axis=1)
            q = _mm_nn(f"q_proj_{l}", xn, W["w_q"], out_dtype=BF16, scale=scale)
            o, of, lse = _attn_fwd(f"attn_fwd_{l}", q, kv, cfull)
            h_mid, hn = _mm_nn(f"o_proj_{l}", o, W["w_o"], out_dtype=F32, res=h, tn=D, norm_gain=gains["ffn"][l])
            mix_saved = (xn, q, o, of, lse)
        W = {**W, **layer_weights(l, "ffn", h_mid)}
        z3, act = _swiglu_fwd(f"ffn_in_{l}", hn, W["w_ffn_in"])
        saved.append((W, h, h_mid, mix_saved, (hn, z3, act)))
        l += 1
        if l < len(gains["mix"]):
            h, xn = _mm_nn(f"ffn_out_{l - 1}", act, W["w_ffn_out"], out_dtype=F32, res=h_mid, tn=D,
                           norm_gain=gains["mix"][l])
        else:
            h = _mm_nn(f"ffn_out_{l - 1}", act, W["w_ffn_out"], out_dtype=F32, res=h_mid, tn=D)

    dh, dhb, dg_final, loss_row = _loss_head("loss_head", h, target, gains["final"])

    dk_parts, dv_parts, dc_parts = [], [], []
    token = None
    for l in reversed(range(len(saved))):
        W, h_in, h_mid, mix_saved, (hn, z3, act) = saved[l]
        recurrent = "w_rec_in" in W
        FH = W["w_ffn_in"].shape[-1]
        G = {}
        norm_ffn = gains["ffn"][l]
        if token is not None:
            norm_ffn = norm_ffn + jnp.minimum(token[:1, :1], 0.0)
        G["w_ffn_out"] = _mm_tn(f"d_ffn_out_{l}", act, dhb, out_dtype=BF16, tn=D)
        dz3 = _swiglu_bwd(f"d_act_{l}", dhb, W["w_ffn_out"], z3)
        G["w_ffn_in"] = _mm(
            f"d_ffn_in_{l}", "tn", hn, dz3, grid=(D // td, N_CHIPS),
            a_spec=pl.BlockSpec((S, td), lambda i, j: (0, i)),
            b_spec=pl.BlockSpec((None, S, FH), lambda i, j: (j // 2, 0, j % 2)),
            out_shape=(N_CHIPS, D, FH), out_dtype=BF16,
            out_spec=pl.BlockSpec((None, td, FH), lambda i, j: (j, i, 0)))
        token = layer_grads(l, "ffn", G)
        G = {}
        norm_ffn = norm_ffn + jnp.minimum(token[:1, :1], 0.0)
        dh, dhb, dgp = _mm(f"d_ffn_hn_{l}", "nt", dz3, W["w_ffn_in"], grid=(S // tm, 1, N_CHIPS), nk=N_CHIPS,
                           a_spec=pl.BlockSpec((None, tm, FH), lambda i, j, k: (k // 2, i, k % 2)),
                           b_spec=pl.BlockSpec((None, D, FH), lambda i, j, k: (k, 0, 0)),
                           out_shape=(S, D), out_dtype=F32, out_spec=pl.BlockSpec((tm, D), lambda i, j, k: (i, 0)),
                           norm_bwd=(h_mid, norm_ffn, dh))
        G["norm_ffn"] = jnp.sum(dgp, axis=0)
        if recurrent:
            CH = W["w_rec_in"].shape[-1]
            C = 2 * CH
            xn, proj, rc, rcb, gip, grp, hrec, m = mix_saved
            G["w_rec_out"] = _mm_tn(f"d_rec_out_{l}", m, dhb, out_dtype=BF16, tn=D)
            dm = _mm_nt(f"d_m_{l}", dhb, W["w_rec_out"], out_dtype=F32, tn=C)
            dgb, dgi, dgr, drc1, G["b_gi"], G["b_gr"], G["lru_param"] = _lru_bwd(
                f"d_lru_{l}", dm, proj, hrec, rc, gip, grp, W["lru_param"])
            drc, G["w_gates"] = _gates_bwd(f"d_gates_{l}", dgi, dgr, rcb, W["w_gates"], drc1)
            drec, G["conv_w"], G["conv_b"] = _conv_bwd(f"d_conv_{l}", drc, proj, W["conv_w"])
            dproj = jnp.concatenate([dgb, drec], axis=1)
            G["w_rec_in"] = _mm(
                f"d_rec_in_{l}", "tn", xn, dproj, grid=(1, N_CHIPS),
                a_spec=pl.BlockSpec((S, D), lambda i, j: (0, 0)),
                b_spec=pl.BlockSpec((S, CH), lambda i, j: (0, j)),
                out_shape=(N_CHIPS, D, CH), out_dtype=BF16,
                out_spec=pl.BlockSpec((None, D, CH), lambda i, j: (j, 0, 0)))
            dh, dhb, dgp = _mm(f"d_rec_xn_{l}", "nt", dproj, W["w_rec_in"], grid=(S // tm, 1, N_CHIPS), nk=N_CHIPS,
                               a_spec=pl.BlockSpec((tm, CH), lambda i, j, k: (i, k)),
                               b_spec=pl.BlockSpec((None, D, CH), lambda i, j, k: (k, 0, 0)),
                               out_shape=(S, D), out_dtype=F32, out_spec=pl.BlockSpec((tm, D), lambda i, j, k: (i, 0)),
                               norm_bwd=(h_in, gains["mix"][l], dh))
        else:
            xn, q, o, of, lse = mix_saved
            G["w_o"] = _mm_tn(f"d_o_proj_{l}", o, dhb, out_dtype=BF16, tn=D)
            do = _mm_nt(f"d_o_{l}", dhb, W["w_o"], out_dtype=BF16, tn=D)
            dq, dk, dv, dck, drq = _attn_bwd(f"attn_bwd_{l}", q, kv, cfull, of, do, lse)
            dk_parts.append(dk)
            dv_parts.append(dv)
            dc_parts.append(dck[:, ::HEAD_DIM] + drq.reshape(2 * HP, S).T)
            G["w_q"] = _mm_tn(f"d_q_proj_{l}", xn, dq, out_dtype=BF16, tn=D)
            dh, dhb, dgp = _mm_nt(f"d_q_xn_{l}", dq, W["w_q"], out_dtype=F32, tn=D, norm_bwd=(h_in, gains["mix"][l], dh))
        G["norm_mix"] = jnp.sum(dgp, axis=0)
        if "w_kv" in W:
            dkb = _add_cast("dk_sum", dk_parts, BF16)
            dvb = _add_cast("dv_sum", dv_parts, BF16)
            dkv = jnp.concatenate([dkb, dvb], axis=1)
            dc = sum(dc_parts[1:], dc_parts[0])
            dc_pad = jnp.pad(dc, ((0, 0), (0, LANES - 2 * HP)))
            dls = _cumsum_rows("dc_cumsum", dc_pad, True)
            dfb, G["b_f"] = _logsig_bwd("d_logsig", dls, f_pre)
            G["w_kv"] = _mm_tn("d_kv_proj", hn_kv, dkv, out_dtype=BF16)
            G["w_f"] = _mm_tn("d_f_proj", hn_kv, dfb, out_dtype=F32)
            dhn_f = _mm_nt("d_f_hn", dfb, W["w_f"], out_dtype=F32, tn=D)
            dh, dhb, dgp = _mm_nt("d_kv_hn", dkv, W["w_kv"], out_dtype=F32, tn=D, res=dhn_f,
                                  norm_bwd=(h_kv, W["norm_kv"], dh))
            G["norm_kv"] = jnp.sum(dgp, axis=0)
        token = layer_grads(l, "mix", G)
    return loss_row, dh, dg_final


_ANY = pl.BlockSpec(memory_space=pl.ANY)


def _position():
    return lax.axis_index("x"), lax.axis_index("y"), lax.axis_index("c")


def _chip_peers(x, y):
    return [(1 - x, y), (x, 1 - y), (1 - x, 1 - y)]


def _half_rows(c, n):
    h = n // 2
    assert h % 16 == 0
    return pl.ds(pl.multiple_of(c * h, 16), h)


def _place_own(name, shard, layer, me):
    _, R, C = shard.shape
    tr = _row_tile(R, C, shard.dtype.itemsize)

    def body(me_ref, x_ref, o_ref):
        o_ref[...] = x_ref[...]

    return pl.pallas_call(
        body, name=name,
        grid_spec=pltpu.PrefetchScalarGridSpec(
            num_scalar_prefetch=1, grid=(R // tr,),
            in_specs=[pl.BlockSpec((None, tr, C), lambda i, me_ref: (layer, i, 0))],
            out_specs=pl.BlockSpec((None, tr, C), lambda i, me_ref: (me_ref[0], i, 0))),
        out_shape=_hbm_out((N_CHIPS, R, C), shard.dtype),
        compiler_params=_params(("parallel",)),
    )(me, shard)


def _gather_smalls(name, smalls):
    ns = len(smalls)

    def body(*refs):
        ins, outs = refs[:ns], refs[ns:2 * ns]
        send_sems, recv_sems, local_sems = refs[2 * ns:]
        x, y, c = _position()
        me = 2 * x + y
        peers = _chip_peers(x, y)

        def remote(t, k, chip):
            px, py = peers[k]
            return pltpu.make_async_remote_copy(
                src_ref=ins[t], dst_ref=outs[t].at[chip], send_sem=send_sems.at[3 * t + k],
                recv_sem=recv_sems.at[3 * t + k], device_id=(px, py, c), device_id_type=MESH)

        local = [pltpu.make_async_copy(ins[t], outs[t].at[me], local_sems.at[t]) for t in range(ns)]
        for t in range(ns):
            local[t].start()
            for k in range(3):
                remote(t, k, me).start()
        for t in range(ns):
            for k in range(3):
                px, py = peers[k]
                remote(t, k, 2 * px + py).wait_recv()
        for t in range(ns):
            for k in range(3):
                remote(t, k, me).wait_send()
            local[t].wait()

    return pl.pallas_call(
        body, name=name, in_specs=[_ANY] * ns, out_specs=[_ANY] * ns,
        out_shape=[_hbm_out((N_CHIPS,) + s.shape, s.dtype) for s in smalls],
        scratch_shapes=[pltpu.SemaphoreType.DMA((3 * ns,)), pltpu.SemaphoreType.DMA((3 * ns,)),
                        pltpu.SemaphoreType.DMA((ns,))],
    )(*smalls)


_SEM = pl.BlockSpec(memory_space=pltpu.SEMAPHORE)
_SPLIT = pltpu.CompilerParams(has_side_effects=pltpu.SideEffectType.DATAFLOW_SIDE_EFFECTING)


def _weight_copy(shards, buf, items, sems, i, k, chip_of_dst, peers, c):
    w, l = items[i]
    px, py = peers[k]
    half = _half_rows(c, shards[w].shape[1])
    return pltpu.make_async_remote_copy(
        src_ref=shards[w].at[l, half], dst_ref=buf.at[chip_of_dst, half],
        send_sem=sems[0].at[3 * i + k], recv_sem=sems[1].at[3 * i + k],
        device_id=(px, py, c), device_id_type=MESH)


def _gather_start(name, shards, bufs, items, after):
    nw, n = len(shards), len(bufs)

    def body(*refs):
        ins, outs, sems = refs[:nw], refs[nw + n + 1:nw + 2 * n + 1], refs[nw + 2 * n + 1:]
        x, y, c = _position()
        peers = _chip_peers(x, y)
        for i in range(n):
            for k in range(3):
                _weight_copy(ins, outs[i], items, sems, i, k, 2 * x + y, peers, c).start()

    res = pl.pallas_call(
        body, name=name, in_specs=[_ANY] * (nw + n + 1), out_specs=[_ANY] * n + [_SEM, _SEM],
        out_shape=[_hbm_out(b.shape, b.dtype) for b in bufs]
        + [pltpu.SemaphoreType.DMA((3 * n,)), pltpu.SemaphoreType.DMA((3 * n,))],
        input_output_aliases={nw + i: i for i in range(n)}, compiler_params=_SPLIT,
    )(*shards, *bufs, after)
    return res[:n], res[n:]


def _gather_wait(name, shards, bufs, items, ids, sems, after):
    nw, m = len(shards), len(ids)

    def body(*refs):
        ins, bs = refs[:nw], refs[nw:nw + m]
        sem_refs = refs[nw + m:nw + m + 2]
        x, y, c = _position()
        peers = _chip_peers(x, y)
        for j, i in enumerate(ids):
            for k in range(3):
                px, py = peers[k]
                _weight_copy(ins, bs[j], items, sem_refs, i, k, 2 * px + py, peers, c).wait_recv()
        for j, i in enumerate(ids):
            for k in range(3):
                _weight_copy(ins, bs[j], items, sem_refs, i, k, 2 * x + y, peers, c).wait_send()

    res = pl.pallas_call(
        body, name=name, in_specs=[_ANY] * (nw + m) + [_SEM, _SEM, _ANY], out_specs=[_ANY] * m,
        out_shape=[_hbm_out(bufs[i].shape, bufs[i].dtype) for i in ids],
        input_output_aliases={nw + j: j for j in range(m)}, compiler_params=_SPLIT,
    )(*shards, *[bufs[i] for i in ids], *sems, after)
    return list(res)


def _gather_d2d(name, bufs):
    n = len(bufs)

    def body(*refs):
        ins, outs = refs[:n], refs[n:2 * n]
        send_sems, recv_sems = refs[2 * n:]
        x, y, c = _position()
        peers = _chip_peers(x, y)

        def remote(i, k, core):
            px, py = peers[k]
            half = _half_rows(core, ins[i].shape[1])
            return pltpu.make_async_remote_copy(
                src_ref=ins[i].at[2 * px + py, half], dst_ref=outs[i].at[2 * px + py, half],
                send_sem=send_sems.at[3 * i + k], recv_sem=recv_sems.at[3 * i + k],
                device_id=(x, y, 1 - c), device_id_type=MESH)

        for i in range(n):
            for k in range(3):
                remote(i, k, c).start()
        for i in range(n):
            for k in range(3):
                remote(i, k, 1 - c).wait_recv()
        for i in range(n):
            for k in range(3):
                remote(i, k, c).wait_send()

    return list(pl.pallas_call(
        body, name=name, in_specs=[_ANY] * n, out_specs=[_ANY] * n,
        out_shape=[_hbm_out(g.shape, g.dtype) for g in bufs],
        input_output_aliases={i: i for i in range(n)},
        scratch_shapes=[pltpu.SemaphoreType.DMA((3 * n,)), pltpu.SemaphoreType.DMA((3 * n,))],
    )(*bufs))


def _reduce_d2d(name, grads):
    n = len(grads)

    def body(*refs):
        ins, outs = refs[:n], refs[n:2 * n]
        send_sems, recv_sems = refs[2 * n:]
        x, y, c = _position()
        remote = [pltpu.make_async_remote_copy(
            src_ref=ins[i].at[:, _half_rows(1 - c, ins[i].shape[1])], dst_ref=outs[i],
            send_sem=send_sems.at[i], recv_sem=recv_sems.at[i],
            device_id=(x, y, 1 - c), device_id_type=MESH) for i in range(n)]
        for cp in remote:
            cp.start()
        for cp in remote:
            cp.wait_recv()
        for cp in remote:
            cp.wait_send()

    return pl.pallas_call(
        body, name=name, in_specs=[_ANY] * n, out_specs=[_ANY] * n,
        out_shape=[_hbm_out((N_CHIPS, g.shape[1] // 2, g.shape[2]), g.dtype) for g in grads],
        scratch_shapes=[pltpu.SemaphoreType.DMA((n,)), pltpu.SemaphoreType.DMA((n,))],
    )(*grads)


def _sum_cores(name, g, other, core):
    _, R, C = g.shape
    H = R // 2
    tr = _row_tile(H, C)
    nb = H // tr

    def body(c_ref, g_ref, o_ref, out_ref):
        out_ref[...] = (g_ref[...].astype(F32) + o_ref[...].astype(F32)).astype(out_ref.dtype)

    return pl.pallas_call(
        body, name=name,
        grid_spec=pltpu.PrefetchScalarGridSpec(
            num_scalar_prefetch=1, grid=(N_CHIPS, nb),
            in_specs=[pl.BlockSpec((None, tr, C), lambda j, i, c_ref: (j, c_ref[0] * nb + i, 0)),
                      pl.BlockSpec((None, tr, C), lambda j, i, c_ref: (j, i, 0))],
            out_specs=pl.BlockSpec((None, tr, C), lambda j, i, c_ref: (j, i, 0))),
        out_shape=_hbm_out((N_CHIPS, H, C), BF16),
        compiler_params=_params(("parallel", "parallel")),
    )(core, g, other)


def _sum_chips(name, received, own, full, layer, me_core):
    _, H, C = received.shape
    tr = _row_tile(H, C)
    nb = H // tr

    def body(s_ref, r_ref, own_ref, full_ref, out_ref):
        acc = r_ref[0].astype(F32)
        for k in (1, 2):
            acc = acc + r_ref[k].astype(F32)
        out_ref[...] = acc + own_ref[...].astype(F32)

    return pl.pallas_call(
        body, name=name,
        grid_spec=pltpu.PrefetchScalarGridSpec(
            num_scalar_prefetch=1, grid=(nb,),
            in_specs=[pl.BlockSpec((3, tr, C), lambda i, s_ref: (0, i, 0)),
                      pl.BlockSpec((None, tr, C), lambda i, s_ref: (s_ref[0], i, 0)),
                      _ANY],
            out_specs=pl.BlockSpec((None, tr, C), lambda i, s_ref: (layer, s_ref[1] * nb + i, 0))),
        out_shape=_hbm_out(full.shape, full.dtype),
        input_output_aliases={3: 0},
        compiler_params=_params(("parallel",)),
    )(me_core, received, own, full)


def _part_copy(parts, recv, sems, i, k, peers, c):
    px, py = peers[k]
    return pltpu.make_async_remote_copy(
        src_ref=parts[i].at[2 * px + py], dst_ref=recv[i].at[k],
        send_sem=sems[0].at[3 * i + k], recv_sem=sems[1].at[3 * i + k],
        device_id=(px, py, c), device_id_type=MESH)


def _scatter_start(name, parts):
    n = len(parts)

    def body(*refs):
        ins, outs, sems, token = refs[:n], refs[n:2 * n], refs[2 * n:2 * n + 2], refs[2 * n + 2]
        x, y, c = _position()
        peers = _chip_peers(x, y)
        for i in range(n):
            for k in range(3):
                _part_copy(ins, outs, sems, i, k, peers, c).start()
        token[...] = jnp.zeros_like(token)

    res = pl.pallas_call(
        body, name=name, in_specs=[_ANY] * n,
        out_specs=[_ANY] * n + [_SEM, _SEM, pl.BlockSpec(memory_space=pltpu.VMEM)],
        out_shape=[_hbm_out((3,) + p.shape[1:], p.dtype) for p in parts]
        + [pltpu.SemaphoreType.DMA((3 * n,)), pltpu.SemaphoreType.DMA((3 * n,)),
           jax.ShapeDtypeStruct((SUBLANES, LANES), F32)],
        compiler_params=_SPLIT,
    )(*parts)
    return list(res[:n]), res[n:n + 2], res[n + 2]


def _scatter_wait(name, parts, recv, sems):
    n = len(parts)

    def body(*refs):
        ins, rs, sem_refs = refs[:n], refs[n:2 * n], refs[2 * n:2 * n + 2]
        x, y, c = _position()
        peers = _chip_peers(x, y)
        for i in range(n):
            for k in range(3):
                _part_copy(ins, rs, sem_refs, i, k, peers, c).wait_recv()
        for i in range(n):
            for k in range(3):
                _part_copy(ins, rs, sem_refs, i, k, peers, c).wait_send()

    return list(pl.pallas_call(
        body, name=name, in_specs=[_ANY] * (2 * n) + [_SEM, _SEM], out_specs=[_ANY] * n,
        out_shape=[_hbm_out(r.shape, r.dtype) for r in recv],
        input_output_aliases={n + i: i for i in range(n)}, compiler_params=_SPLIT,
    )(*parts, *recv, *sems))


def _share_d2d(name, full):
    n = len(full)

    def body(*refs):
        ins, outs = refs[:n], refs[n:2 * n]
        send_sems, recv_sems = refs[2 * n:]
        x, y, c = _position()

        def remote(w, core):
            half = _half_rows(core, ins[w].shape[1])
            return pltpu.make_async_remote_copy(
                src_ref=ins[w].at[:, half], dst_ref=outs[w].at[:, half],
                send_sem=send_sems.at[w], recv_sem=recv_sems.at[w],
                device_id=(x, y, 1 - c), device_id_type=MESH)

        for w in range(n):
            remote(w, c).start()
        for w in range(n):
            remote(w, 1 - c).wait_recv()
        for w in range(n):
            remote(w, c).wait_send()

    return pl.pallas_call(
        body, name=name, in_specs=[_ANY] * n, out_specs=[_ANY] * n,
        out_shape=[_hbm_out(f.shape, f.dtype) for f in full],
        input_output_aliases={w: w for w in range(n)},
        scratch_shapes=[pltpu.SemaphoreType.DMA((n,)), pltpu.SemaphoreType.DMA((n,))],
    )(*full)


def _gather_all(name, a):
    def body(a_ref, o_ref, send_sems, recv_sems, local_sem):
        x, y, c = _position()
        me = 4 * x + 2 * y + c

        def peer(k):
            return (x ^ ((k >> 2) & 1), y ^ ((k >> 1) & 1), c ^ (k & 1))

        def remote(k, slot):
            return pltpu.make_async_remote_copy(
                src_ref=a_ref, dst_ref=o_ref.at[slot], send_sem=send_sems.at[k - 1], recv_sem=recv_sems.at[k - 1],
                device_id=peer(k), device_id_type=MESH)

        local = pltpu.make_async_copy(a_ref, o_ref.at[me], local_sem)
        local.start()
        for k in range(1, N_DEV):
            remote(k, me).start()
        for k in range(1, N_DEV):
            px, py, pc = peer(k)
            remote(k, 4 * px + 2 * py + pc).wait_recv()
        for k in range(1, N_DEV):
            remote(k, me).wait_send()
        local.wait()

    return pl.pallas_call(
        body, name=name, in_specs=[_ANY], out_specs=_ANY,
        out_shape=_hbm_out((N_DEV,) + a.shape, a.dtype),
        scratch_shapes=[pltpu.SemaphoreType.DMA((N_DEV - 1,)), pltpu.SemaphoreType.DMA((N_DEV - 1,)),
                        pltpu.SemaphoreType.DMA],
    )(a)


def _rows2d(a, lead=0):
    return a.reshape(a.shape[:lead] + (-1, a.shape[-1]))


def _row_tile(rows, cols, itemsize=4, target=1 << 20):
    want = max(SUBLANES, target // (cols * itemsize))
    t = min(rows, (want // 16) * 16)
    while t > 16 and rows % t:
        t -= 16
    return t if rows % t == 0 else rows


def _sum_slots(name, r, out_dtype=F32):
    ns = r.shape[0]
    r2 = _rows2d(r, 1)
    _, rows, cols = r2.shape
    tr = _row_tile(rows, cols)

    def body(r_ref, o_ref):
        acc = r_ref[0].astype(F32)
        for s in range(1, ns):
            acc = acc + r_ref[s].astype(F32)
        o_ref[...] = acc.astype(o_ref.dtype)

    out = pl.pallas_call(
        body, name=name, grid=(rows // tr,),
        in_specs=[pl.BlockSpec((ns, tr, cols), lambda i: (0, i, 0))],
        out_specs=pl.BlockSpec((tr, cols), lambda i: (i, 0)),
        out_shape=_hbm_out((rows, cols), out_dtype),
        compiler_params=_params(("parallel",)),
    )(r2)
    return out.reshape(r.shape[1:])


def _adamw(name, g_parts, w, m, v):
    shape = w.shape
    ng = len(g_parts)
    args = [_rows2d(a) for a in (*g_parts, w, m, v)]
    rows, cols = args[0].shape
    tr = _row_tile(rows, cols, target=1 << 19)
    c1 = 1.0 - ADAM_B1 ** ADAM_STEP
    c2 = 1.0 - ADAM_B2 ** ADAM_STEP

    def body(*refs):
        g = refs[0][...]
        for r in refs[1:ng]:
            g = g + r[...]
        w_ref, m_ref, v_ref = refs[ng:ng + 3]
        g_out, d_out, m_out, v_out = refs[ng + 3:]
        mn = ADAM_B1 * m_ref[...] + (1.0 - ADAM_B1) * g
        vn = ADAM_B2 * v_ref[...] + (1.0 - ADAM_B2) * (g * g)
        m_hat = mn / c1
        v_hat = vn / c2
        g_out[...] = g
        d_out[...] = -ADAM_LR * (m_hat / (jnp.sqrt(v_hat) + ADAM_EPS) + ADAM_WD * w_ref[...])
        m_out[...] = mn
        v_out[...] = vn

    spec = pl.BlockSpec((tr, cols), lambda i: (i, 0))
    outs = pl.pallas_call(
        body, name=name, grid=(rows // tr,), in_specs=[spec] * (ng + 3), out_specs=[spec] * 4,
        out_shape=[_hbm_out((rows, cols), F32)] * 4,
        compiler_params=_params(("parallel",)),
    )(*args)
    return tuple(o.reshape(shape) for o in outs)


_WEIGHTS = ["norm_mix", "norm_ffn", "w_ffn_in", "w_ffn_out", "w_rec_in", "conv_w", "conv_b", "w_lru_gates",
            "b_lru_gates", "lru_param", "w_rec_out", "norm_kv", "w_kvf", "b_forget", "w_q", "w_o", "norm_final"]
_BIG = ["w_ffn_in", "w_ffn_out", "w_rec_in", "w_lru_gates", "w_rec_out", "w_kvf", "w_q", "w_o"]


def _stack3(a):
    return a[None] if a.ndim == 2 else a.reshape(a.shape[0], -1, a.shape[-1])


def _pad_lanes(a, n):
    return jnp.pad(a, ((0, 0),) * (a.ndim - 1) + ((0, n - a.shape[-1]),))


def kernel(x, norm_mix, norm_ffn, w_ffn_in, w_ffn_out, w_rec_in, conv_w, conv_b, w_lru_gates, b_lru_gates, lru_param, w_rec_out, norm_kv, w_kvf, b_forget, w_q, w_o, norm_final, loss_target, m_norm_mix, m_norm_ffn, m_w_ffn_in, m_w_ffn_out, m_w_rec_in, m_conv_w, m_conv_b, m_w_lru_gates, m_b_lru_gates, m_lru_param, m_w_rec_out, m_norm_kv, m_w_kvf, m_b_forget, m_w_q, m_w_o, m_norm_final, v_norm_mix, v_norm_ffn, v_w_ffn_in, v_w_ffn_out, v_w_rec_in, v_conv_w, v_conv_b, v_w_lru_gates, v_b_lru_gates, v_lru_param, v_w_rec_out, v_norm_kv, v_w_kvf, v_b_forget, v_w_q, v_w_o, v_norm_final):
    P = dict(norm_mix=norm_mix, norm_ffn=norm_ffn, w_ffn_in=w_ffn_in, w_ffn_out=w_ffn_out, w_rec_in=w_rec_in,
             conv_w=conv_w, conv_b=conv_b, w_lru_gates=w_lru_gates, b_lru_gates=b_lru_gates, lru_param=lru_param,
             w_rec_out=w_rec_out, norm_kv=norm_kv, w_kvf=w_kvf, b_forget=b_forget, w_q=w_q, w_o=w_o,
             norm_final=norm_final)
    M1 = dict(norm_mix=m_norm_mix, norm_ffn=m_norm_ffn, w_ffn_in=m_w_ffn_in, w_ffn_out=m_w_ffn_out,
              w_rec_in=m_w_rec_in, conv_w=m_conv_w, conv_b=m_conv_b, w_lru_gates=m_w_lru_gates,
              b_lru_gates=m_b_lru_gates, lru_param=m_lru_param, w_rec_out=m_w_rec_out, norm_kv=m_norm_kv,
              w_kvf=m_w_kvf, b_forget=m_b_forget, w_q=m_w_q, w_o=m_w_o, norm_final=m_norm_final)
    M2 = dict(norm_mix=v_norm_mix, norm_ffn=v_norm_ffn, w_ffn_in=v_w_ffn_in, w_ffn_out=v_w_ffn_out,
              w_rec_in=v_w_rec_in, conv_w=v_conv_w, conv_b=v_conv_b, w_lru_gates=v_w_lru_gates,
              b_lru_gates=v_b_lru_gates, lru_param=v_lru_param, w_rec_out=v_w_rec_out, norm_kv=v_norm_kv,
              w_kvf=v_w_kvf, b_forget=v_b_forget, w_q=v_w_q, w_o=v_w_o, norm_final=v_norm_final)

    _, S, D = x.shape
    L = norm_mix.shape[0]
    NA, NBLK, BW, GS = w_lru_gates.shape
    NB = w_q.shape[0]
    C = NBLK * BW
    CS = C // N_CHIPS
    H = b_forget.shape[0]
    assert C == D and H * HEAD_DIM == D and H <= LANES
    chip = 2 * lax.axis_index("x") + lax.axis_index("y")

    small_a = jnp.concatenate([conv_w, conv_b[:, None], lru_param[:, None]], axis=1)
    small_a, b_gates = _gather_smalls("gather_smalls", [small_a, b_lru_gates])
    small_a = small_a.transpose(1, 2, 0, 3).reshape(NA, 6, C)
    b_gates = b_gates.transpose(1, 2, 0, 3).reshape(NA, NBLK, 1, N_CHIPS * GS)
    shards = [_stack3(P[w]).astype(BF16) for w in _BIG]
    core = lax.axis_index("c")
    chip_id = jnp.reshape(chip, (1,)).astype(jnp.int32)
    core_id = jnp.reshape(core, (1,)).astype(jnp.int32)
    me_core = jnp.stack([chip, core]).astype(jnp.int32)

    def stage_items(l, part):
        if part == "ffn":
            return [(_BIG.index("w_ffn_in"), l), (_BIG.index("w_ffn_out"), l)]
        if l < NA:
            names, at = ["w_rec_in", "w_lru_gates", "w_rec_out"], l
        else:
            names, at = (["w_kvf"] if l == NA else []) + ["w_q", "w_o"], l - NA
        return [(_BIG.index(n), 0 if n == "w_kvf" else at) for n in names]

    stages = [(l, part) for l in range(L) for part in ("mix", "ffn")]
    items = [it for st in stages for it in stage_items(*st)]
    ids_of = {st: [items.index(it) for it in stage_items(*st)] for st in stages}
    bufs = [_place_own(f"place_{_BIG[w]}_{li}", shards[w], li, chip_id) for w, li in items]
    bufs, gather_sems = _gather_start("gather_start", shards, bufs, items, small_a)

    def layer_weights(l, part, after):
        if l >= L:
            return None
        ids = ids_of[(l, part)]
        got = _gather_wait(f"gather_wait_{part}_{l}", shards, bufs, items, ids, gather_sems, after)
        got = _gather_d2d(f"gather_d2d_{part}_{l}", got)
        B = {_BIG[items[i][0]]: g for i, g in zip(ids, got)}
        if part == "ffn":
            return dict(w_ffn_in=B["w_ffn_in"], w_ffn_out=B["w_ffn_out"].reshape(-1, D))
        W = {}
        if l < NA:
            W.update(w_rec_in=B["w_rec_in"],
                     w_gates=B["w_lru_gates"].reshape(N_CHIPS, NBLK, BW, GS).transpose(1, 2, 0, 3).reshape(
                         NBLK, BW, N_CHIPS * GS),
                     b_gates=b_gates[l], w_rec_out=B["w_rec_out"].reshape(C, D),
                     conv_w=small_a[l, :4], conv_b=small_a[l, 4:5], lru_param=small_a[l, 5:6])
        else:
            W.update(w_q=B["w_q"].reshape(D, D), w_o=B["w_o"].reshape(D, D))
            if l == NA:
                w_kvf_full = B["w_kvf"].transpose(1, 0, 2).reshape(D, -1)
                W.update(norm_kv=norm_kv[None], w_kv=w_kvf_full[:, :2 * D],
                         w_f=_pad_lanes(w_kvf_full[:, 2 * D:], LANES), b_f=_pad_lanes(b_forget[None], LANES))
        return W

    G_small = {l: {} for l in range(L)}
    pending = {}

    def layer_grads(l, part, G):
        G_small[l].update(G)
        by_name = dict(
            w_ffn_in=lambda: G["w_ffn_in"], w_ffn_out=lambda: G["w_ffn_out"].reshape(N_CHIPS, -1, D),
            w_rec_in=lambda: G["w_rec_in"],
            w_lru_gates=lambda: G["w_gates"].reshape(NBLK, BW, N_CHIPS, GS).transpose(2, 0, 1, 3).reshape(
                N_CHIPS, NBLK * BW, GS),
            w_rec_out=lambda: G["w_rec_out"].reshape(N_CHIPS, -1, D),
            w_kvf=lambda: jnp.concatenate([G["w_kv"].astype(F32), G["w_f"][:, :H]], axis=1).reshape(
                D, N_CHIPS, -1).transpose(1, 0, 2).astype(BF16),
            w_q=lambda: G["w_q"].reshape(N_CHIPS, -1, D), w_o=lambda: G["w_o"].reshape(N_CHIPS, -1, D))
        its = stage_items(l, part)
        grads = [by_name[_BIG[w]]() for w, _ in its]
        others = _reduce_d2d(f"reduce_d2d_{part}_{l}", grads)
        parts = [_sum_cores(f"sum_cores_{l}_{_BIG[w]}", g, o, core_id) for (w, _), g, o in zip(its, grads, others)]
        recv, sems, token = _scatter_start(f"scatter_start_{part}_{l}", parts)
        pending[(l, part)] = (parts, recv, sems)
        return token

    gains = dict(mix=[norm_mix[l][None] for l in range(L)], ffn=[norm_ffn[l][None] for l in range(L)],
                 final=norm_final[None])
    loss_row, grad_x, dg_final = _local_step(x.reshape(S, D), loss_target.reshape(S, D), gains,
                                             layer_weights, layer_grads)

    rows = [*[G_small[l]["norm_mix"] for l in range(L)], *[G_small[l]["norm_ffn"] for l in range(L)],
            G_small[NA]["norm_kv"], dg_final, _pad_lanes(G_small[NA]["b_f"], D), _pad_lanes(loss_row, D)]
    for a in range(NA):
        rows += [G_small[a][n] for n in ("conv_w", "conv_b", "b_gi", "b_gr", "lru_param")]
    packed = jnp.concatenate(rows, axis=0)
    tot = _sum_slots("sum_small", _gather_all("gather_small", packed))
    loss = tot[2 * L + 3, 0]
    g_rep = jnp.concatenate([tot[:2 * L + 2], tot[2 * L + 2:2 * L + 3]], axis=0)
    base = 2 * L + 4
    g_sh = []
    for a in range(NA):
        blk = lax.dynamic_slice_in_dim(tot[base + 8 * a:base + 8 * a + 8], chip * CS, CS, axis=1)
        gi = tot[base + 8 * a + 5].reshape(NBLK, BW)
        gr = tot[base + 8 * a + 6].reshape(NBLK, BW)
        bl = lax.dynamic_slice_in_dim(jnp.concatenate([gi, gr], axis=1), chip * GS, GS, axis=1)
        g_sh += [blk[:5], bl.reshape(-1, CS), blk[7:8]]
    g_sh = jnp.concatenate(g_sh, axis=0)
    nrow = g_sh.shape[0] // NA

    def pack_rep(T):
        return jnp.concatenate([T["norm_mix"], T["norm_ffn"], T["norm_kv"][None], T["norm_final"][None],
                                _pad_lanes(T["b_forget"][None], D)], axis=0)

    def pack_sh(T):
        return jnp.concatenate([jnp.concatenate([T["conv_w"][a], T["conv_b"][a][None],
                                                 T["b_lru_gates"][a].reshape(-1, CS), T["lru_param"][a][None]], axis=0)
                                for a in range(NA)], axis=0)

    rep = _adamw("adamw_replicated", [g_rep], pack_rep(P), pack_rep(M1), pack_rep(M2))
    shd = _adamw("adamw_small_sharded", [g_sh], pack_sh(P), pack_sh(M1), pack_sh(M2))

    def unpack_rep(t):
        return dict(norm_mix=t[:L], norm_ffn=t[L:2 * L], norm_kv=t[2 * L], norm_final=t[2 * L + 1],
                    b_forget=t[2 * L + 2, :H])

    def unpack_sh(t):
        t = t.reshape(NA, nrow, CS)
        return dict(conv_w=t[:, :4], conv_b=t[:, 4], b_lru_gates=t[:, 5:nrow - 1].reshape(NA, NBLK, GS),
                    lru_param=t[:, nrow - 1])

    full = [lax.empty(sh.shape, F32) for sh in shards]
    for l, part in reversed(stages):
        parts, recv, sems = pending[(l, part)]
        recv = _scatter_wait(f"scatter_wait_{part}_{l}", parts, recv, sems)
        for (w, li), own, r in zip(stage_items(l, part), parts, recv):
            full[w] = _sum_chips(f"sum_chips_{l}_{_BIG[w]}", r, own, full[w], li, me_core)
    full = _share_d2d("share_d2d", full)
    big = {w: _adamw(f"adamw_{w}", [g.reshape(P[w].shape)], P[w], M1[w], M2[w]) for w, g in zip(_BIG, full)}

    outs = []
    for i in range(4):
        small = {**unpack_rep(rep[i]), **unpack_sh(shd[i])}
        outs.append([big[w][i] if w in big else small[w] for w in _WEIGHTS])
    return (loss, grad_x.reshape(1, S, D), *outs[0], *outs[1], *outs[2], *outs[3])
```

```python
import functools
import math

import jax
import jax.numpy as jnp
from jax import lax
from jax.experimental import pallas as pl
from jax.experimental.pallas import tpu as pltpu

F32 = jnp.float32
BF16 = jnp.bfloat16

EPS = 1e-6
LRU_C = 8.0
HEAD_DIM = 64
LANES = 128
SUBLANES = 8
VMEM_LIMIT = 48 * 1024 * 1024
N_CHIPS = 4
N_DEV = 8

ADAM_LR = 0.001
ADAM_B1 = 0.9
ADAM_B2 = 0.999
ADAM_EPS = 1e-08
ADAM_WD = 0.01
ADAM_STEP = 10

_NN = (((1,), (0,)), ((), ()))
_NT = (((1,), (1,)), ((), ()))
_TN = (((0,), (0,)), ((), ()))
_DN = {"nn": _NN, "nt": _NT, "tn": _TN}
MESH = pl.DeviceIdType.MESH


def _hbm_out(shape, dtype):
    return pltpu.HBM(shape, dtype)


def _params(sem):
    return pltpu.CompilerParams(dimension_semantics=sem, vmem_limit_bytes=VMEM_LIMIT)


def _tile(n, want):
    if n <= want:
        return n
    t = (want // LANES) * LANES
    while t >= LANES:
        if n % t == 0:
            return t
        t -= LANES
    return n


def _sigmoid(x):
    return 1.0 / (1.0 + jnp.exp(-x))


def _sigmoid_t(x):
    return 0.5 * jnp.tanh(0.5 * x) + 0.5


def _softplus(x):
    return jnp.maximum(x, 0.0) + jnp.log(1.0 + jnp.exp(-jnp.abs(x)))


_GELU_C = math.sqrt(2.0 / math.pi)


def _gelu_and_grad(x):
    inner = _GELU_C * (x + 0.044715 * x * x * x)
    t = jnp.tanh(inner)
    g = 0.5 * x * (1.0 + t)
    dg = 0.5 * (1.0 + t) + 0.5 * x * (1.0 - t * t) * _GELU_C * (1.0 + 3.0 * 0.044715 * x * x)
    return g, dg


def _rms(x):
    return lax.rsqrt(jnp.mean(x * x, axis=-1, keepdims=True) + EPS)


def _rms_bwd(dy, x, g):
    r = _rms(x)
    xr = x * r
    dyg = dy * g
    return r * dyg - xr * (r * jnp.mean(dyg * xr, axis=-1, keepdims=True)), jnp.sum(dy * xr, axis=0, keepdims=True)


def _mm(name, mode, a, b, *, grid, a_spec, b_spec, out_shape, out_dtype, out_spec, nk=1,
        res=None, res_spec=None, bias=None, bias_spec=None, scale=None, norm_gain=None, norm_bwd=None):
    dn = _DN[mode]
    has_res, has_bias = res is not None, bias is not None
    blk = tuple(d for d in out_spec.block_shape if d is not None)
    vec = pl.BlockSpec((1, blk[-1]), lambda *g: (0, 0))
    a_specs = a_spec if isinstance(a_spec, list) else [a_spec]
    b_specs = b_spec if isinstance(b_spec, list) else [b_spec]
    npair = len(a_specs)
    n_in = 2 * npair + int(has_res) + int(has_bias) + (1 if norm_gain is not None else 0) + (3 if norm_bwd else 0)

    def body(*refs):
        p = 2 * npair
        r_ref = refs[p] if has_res else None
        p += int(has_res)
        bias_ref = refs[p] if has_bias else None
        p += int(has_bias)
        extra = refs[p:n_in]
        outs = refs[n_in:]
        o_ref = outs[0]
        part = lax.dot_general(refs[0][...], refs[npair][...], dn, preferred_element_type=F32)
        for t in range(1, npair):
            part = part + lax.dot_general(refs[t][...], refs[npair + t][...], dn, preferred_element_type=F32)

        def finish(acc):
            if scale is not None:
                acc = acc * scale
            if has_bias:
                acc = acc + bias_ref[...]
            if has_res:
                acc = r_ref[...] + acc
            if norm_bwd:
                h_ref, g_ref, dh_ref = extra
                dx, dg = _rms_bwd(acc, h_ref[...], g_ref[...])
                acc = dh_ref[...] + dx
                outs[1][...] = acc.astype(BF16)
                outs[2][...] = dg
            if norm_gain is not None:
                outs[1][...] = (acc * _rms(acc) * extra[0][...]).astype(BF16)
            o_ref[...] = acc.astype(o_ref.dtype)

        if nk == 1:
            finish(part)
        else:
            acc_ref = refs[-1]
            k = pl.program_id(2)

            @pl.when(k == 0)
            def _():
                acc_ref[...] = part

            @pl.when(k > 0)
            def _():
                acc_ref[...] += part

            @pl.when(k == nk - 1)
            def _():
                finish(acc_ref[...])

    ins, specs = [a] * npair + [b] * npair, a_specs + b_specs
    if has_res:
        ins.append(res)
        specs.append(res_spec)
    if has_bias:
        ins.append(bias)
        specs.append(bias_spec)
    out_specs, out_shapes = [out_spec], [_hbm_out(out_shape, out_dtype)]
    if norm_gain is not None:
        ins.append(norm_gain)
        specs.append(vec)
        out_specs.append(out_spec)
        out_shapes.append(_hbm_out(out_shape, BF16))
    if norm_bwd:
        h, g, dh = norm_bwd
        ins += [h, g, dh]
        specs += [out_spec, vec, out_spec]
        out_specs += [out_spec, pl.BlockSpec((None, 1, blk[-1]), lambda i, *rest: (i, 0, 0))]
        out_shapes += [_hbm_out(out_shape, BF16), _hbm_out((grid[0], 1, blk[-1]), F32)]
    sem = ("parallel", "parallel") + (("arbitrary",) if len(grid) == 3 else ())
    single = len(out_specs) == 1
    return pl.pallas_call(
        body, name=name, grid=grid, in_specs=specs, out_specs=out_specs[0] if single else out_specs,
        out_shape=out_shapes[0] if single else out_shapes,
        scratch_shapes=[pltpu.VMEM(blk, F32)] if nk > 1 else [],
        compiler_params=_params(sem),
    )(*ins)


def _mm_nn(name, a, b, *, b_lead=(), out_dtype, tm=512, tn=512, res=None, bias=None, scale=None, norm_gain=None):
    M, K = a.shape
    N = b.shape[-1]
    tm, tn = _tile(M, tm), _tile(N, tn)
    nl = len(b_lead)
    return _mm(
        name, "nn", a, b, grid=(M // tm, N // tn),
        a_spec=pl.BlockSpec((tm, K), lambda i, j: (i, 0)),
        b_spec=pl.BlockSpec((None,) * nl + (K, tn), lambda i, j: tuple(b_lead) + (0, j)),
        out_shape=(M, N), out_dtype=out_dtype, out_spec=pl.BlockSpec((tm, tn), lambda i, j: (i, j)),
        res=res, res_spec=pl.BlockSpec((tm, tn), lambda i, j: (i, j)),
        bias=bias, bias_spec=pl.BlockSpec((1, tn), lambda i, j: (0, j)), scale=scale, norm_gain=norm_gain)


def _mm_nt(name, a, b, *, b_lead=(), out_dtype, tm=512, tn=512, tk=2048, res=None, norm_bwd=None):
    M, K = a.shape
    N = b.shape[-2]
    tm, tn, tk = _tile(M, tm), _tile(N, tn), _tile(K, tk)
    nk = K // tk
    nl = len(b_lead)
    return _mm(
        name, "nt", a, b, grid=(M // tm, N // tn, nk), nk=nk,
        a_spec=pl.BlockSpec((tm, tk), lambda i, j, k: (i, k)),
        b_spec=pl.BlockSpec((None,) * nl + (tn, tk), lambda i, j, k: tuple(b_lead) + (j, k)),
        out_shape=(M, N), out_dtype=out_dtype, out_spec=pl.BlockSpec((tm, tn), lambda i, j, k: (i, j)),
        res=res, res_spec=pl.BlockSpec((tm, tn), lambda i, j, k: (i, j)), norm_bwd=norm_bwd)


def _mm_tn(name, a, b, *, out_dtype, tm=512, tn=512):
    S, M = a.shape
    N = b.shape[1]
    tm, tn = _tile(M, tm), _tile(N, tn)
    return _mm(
        name, "tn", a, b, grid=(M // tm, N // tn),
        a_spec=pl.BlockSpec((S, tm), lambda i, j: (0, i)),
        b_spec=pl.BlockSpec((S, tn), lambda i, j: (0, j)),
        out_shape=(M, N), out_dtype=out_dtype, out_spec=pl.BlockSpec((tm, tn), lambda i, j: (i, j)))


def _rmsnorm_fwd(name, h, g, tr=256):
    S, D = h.shape
    tr = _tile(S, tr)

    def body(h_ref, g_ref, o_ref):
        x = h_ref[...]
        r = lax.rsqrt(jnp.mean(x * x, axis=-1, keepdims=True) + EPS)
        o_ref[...] = (x * r * g_ref[...]).astype(o_ref.dtype)

    return pl.pallas_call(
        body, name=name, grid=(S // tr,),
        in_specs=[pl.BlockSpec((tr, D), lambda i: (i, 0)), pl.BlockSpec((1, D), lambda i: (0, 0))],
        out_specs=pl.BlockSpec((tr, D), lambda i: (i, 0)),
        out_shape=_hbm_out((S, D), BF16),
        compiler_params=_params(("parallel",)),
    )(h, g)


def _loss_head(name, h, target, g, tr=256):
    S, D = h.shape
    tr = _tile(S, tr)

    def body(h_ref, t_ref, g_ref, o_ref, ob_ref, dg_ref, loss_ref):
        i = pl.program_id(0)
        x = h_ref[...]
        gg = g_ref[...]
        r = lax.rsqrt(jnp.mean(x * x, axis=-1, keepdims=True) + EPS)
        xr = x * r
        err = xr * gg - t_ref[...]
        lpart = 0.5 * jnp.sum(jnp.mean(err * err, axis=-1, keepdims=True), axis=0, keepdims=True)
        dy = err * (1.0 / D)
        dyg = dy * gg
        dx = r * dyg - xr * (r * jnp.mean(dyg * xr, axis=-1, keepdims=True))
        o_ref[...] = dx
        ob_ref[...] = dx.astype(BF16)
        part = jnp.sum(dy * xr, axis=0, keepdims=True)
        lrow = jnp.broadcast_to(lpart, (1, LANES))

        @pl.when(i == 0)
        def _():
            dg_ref[...] = part
            loss_ref[...] = lrow

        @pl.when(i > 0)
        def _():
            dg_ref[...] += part
            loss_ref[...] += lrow

    row = pl.BlockSpec((tr, D), lambda i: (i, 0))
    vec = pl.BlockSpec((1, D), lambda i: (0, 0))
    return pl.pallas_call(
        body, name=name, grid=(S // tr,),
        in_specs=[row, row, vec], out_specs=[row, row, vec, pl.BlockSpec((1, LANES), lambda i: (0, 0))],
        out_shape=[_hbm_out((S, D), F32), _hbm_out((S, D), BF16),
                   _hbm_out((1, D), F32), _hbm_out((1, LANES), F32)],
        compiler_params=_params(("arbitrary",)),
    )(h, target, g)


def _swiglu_fwd(name, hn, w_in, tm=512):
    S, D = hn.shape
    FH = w_in.shape[-1]
    tm = _tile(S, tm)

    def body(x_ref, wg_ref, wu_ref, z_ref, a_ref):
        x = x_ref[...]
        zg = jnp.dot(x, wg_ref[...], preferred_element_type=F32)
        zu = jnp.dot(x, wu_ref[...], preferred_element_type=F32)
        z_ref[0] = zg.astype(z_ref.dtype)
        z_ref[1] = zu.astype(z_ref.dtype)
        a_ref[...] = (zg * _sigmoid_t(zg) * zu).astype(a_ref.dtype)

    return pl.pallas_call(
        body, name=name, grid=(S // tm, 2),
        in_specs=[pl.BlockSpec((tm, D), lambda i, j: (i, 0)),
                  pl.BlockSpec((None, D, FH), lambda i, j: (j, 0, 0)),
                  pl.BlockSpec((None, D, FH), lambda i, j: (j + 2, 0, 0))],
        out_specs=[pl.BlockSpec((2, tm, FH), lambda i, j: (0, i, j)), pl.BlockSpec((tm, FH), lambda i, j: (i, j))],
        out_shape=[_hbm_out((2, S, 2 * FH), BF16), _hbm_out((S, 2 * FH), BF16)],
        compiler_params=_params(("parallel", "parallel")),
    )(hn, w_in, w_in)


def _swiglu_bwd(name, dhb, w_out, z3, tm=512):
    S, D = dhb.shape
    F = w_out.shape[0]
    FH = F // 2
    tm = _tile(S, tm)

    def body(d_ref, w_ref, z_ref, dz_ref):
        d = lax.dot_general(d_ref[...], w_ref[...], _NT, preferred_element_type=F32)
        zg = z_ref[0].astype(F32)
        zu = z_ref[1].astype(F32)
        sg = _sigmoid_t(zg)
        dz_ref[0] = (d * zu * (sg * (1.0 + zg * (1.0 - sg)))).astype(dz_ref.dtype)
        dz_ref[1] = (d * (zg * sg)).astype(dz_ref.dtype)

    zspec = pl.BlockSpec((2, tm, FH), lambda i, j: (0, i, j))
    return pl.pallas_call(
        body, name=name, grid=(S // tm, 2),
        in_specs=[pl.BlockSpec((tm, D), lambda i, j: (i, 0)), pl.BlockSpec((FH, D), lambda i, j: (j, 0)), zspec],
        out_specs=zspec, out_shape=_hbm_out((2, S, F), BF16),
        compiler_params=_params(("parallel", "parallel")),
    )(dhb, w_out, z3)


SCAN_ROWS = 64


def _group_scan(A, B, reverse):
    n = A.shape[0]
    sub = lax.broadcasted_iota(jnp.int32, A.shape, 0) % SUBLANES
    for d in (1, 2, 4):
        if reverse:
            A_sh, B_sh = pltpu.roll(A, n - d, 0), pltpu.roll(B, n - d, 0)
            keep = sub < SUBLANES - d
        else:
            A_sh, B_sh = pltpu.roll(A, d, 0), pltpu.roll(B, d, 0)
            keep = sub >= d
        B = jnp.where(keep, A * B_sh + B, B)
        A = jnp.where(keep, A * A_sh, A)
    return A, B


def _block_scan(a, u, carry, reverse):
    A, B = _group_scan(a, u, reverse)
    ng = a.shape[0] // SUBLANES
    out = [None] * ng
    order = range(ng - 1, -1, -1) if reverse else range(ng)
    for gi in order:
        sl = slice(gi * SUBLANES, (gi + 1) * SUBLANES)
        hg = A[sl] * carry + B[sl]
        out[gi] = hg
        carry = hg[0:1] if reverse else hg[SUBLANES - 1:SUBLANES]
    return jnp.concatenate(out, axis=0), carry


def _lru_gates(rc, gip, grp, sp):
    gi = _sigmoid(gip)
    gr = _sigmoid(grp)
    la = -LRU_C * gr * sp
    a = jnp.exp(la)
    om = -jnp.tanh(la) * (a * a + 1.0)
    mult = jnp.sqrt(om)
    return gi, gr, a, mult


def _lru_fwd(name, proj, rc, gip, grp, lru_p, tc=256):
    S, C = rc.shape
    tc = _tile(C, tc)
    nb = S // SCAN_ROWS

    def body(gb_ref, rc_ref, gi_ref, gr_ref, l_ref, h_ref, m_ref):
        sp = _softplus(-l_ref[...])

        def step(b, carry):
            rows = pl.ds(pl.multiple_of(b * SCAN_ROWS, SCAN_ROWS), SCAN_ROWS)
            rcb = rc_ref[rows, :]
            gi, _, a, mult = _lru_gates(rcb, gi_ref[rows, :], gr_ref[rows, :], sp)
            h, carry = _block_scan(a, rcb * gi * mult, carry, False)
            h_ref[rows, :] = h
            gel, _ = _gelu_and_grad(gb_ref[rows, :])
            m_ref[rows, :] = (gel * h).astype(m_ref.dtype)
            return carry

        lax.fori_loop(0, nb, step, jnp.zeros((1, tc), F32))

    col = pl.BlockSpec((S, tc), lambda j: (0, j))
    return pl.pallas_call(
        body, name=name, grid=(C // tc,),
        in_specs=[col, col, col, col, pl.BlockSpec((1, tc), lambda j: (0, j))],
        out_specs=[col, col],
        out_shape=[_hbm_out((S, C), F32), _hbm_out((S, C), BF16)],
        compiler_params=_params(("parallel",)),
    )(proj, rc, gip, grp, lru_p)


def _lru_bwd(name, dm, proj, hrec, rc, gip, grp, lru_p, tc=256):
    S, C = rc.shape
    tc = _tile(C, tc)
    nb = S // SCAN_ROWS
    R = SCAN_ROWS

    def body(dm_ref, gb_ref, h_ref, rc_ref, gi_ref, gr_ref, l_ref,
             dgb_ref, dgi_ref, dgr_ref, drc_ref, dbi_ref, dbr_ref, dl_ref):
        lp = l_ref[...]
        sp = _softplus(-lp)
        row = lax.broadcasted_iota(jnp.int32, (R, tc), 0)
        zero = jnp.zeros((1, tc), F32)

        def step(t, carry):
            mu_in, s_i, s_r, s_sp = carry
            b = nb - 1 - t
            r0 = pl.multiple_of(b * R, R)
            rows = pl.ds(r0, R)
            rcb = rc_ref[rows, :]
            gi, gr, a, mult = _lru_gates(rcb, gi_ref[rows, :], gr_ref[rows, :], sp)
            gel, dgel = _gelu_and_grad(gb_ref[rows, :])
            dmb = dm_ref[rows, :]
            h = h_ref[rows, :]
            dgb_ref[rows, :] = (dmb * h * dgel).astype(dgb_ref.dtype)
            dh = dmb * gel
            mu, mu_out = _block_scan(a, a * dh, mu_in, True)
            mu_next = jnp.where(row == R - 1, mu_in, pltpu.roll(mu, R - 1, 0))
            lam = dh + mu_next
            p0 = pl.multiple_of(jnp.maximum(r0 - SUBLANES, 0), SUBLANES)
            prev = h_ref[pl.ds(p0, SUBLANES), :][SUBLANES - 1:SUBLANES]
            prev = jnp.where(b > 0, prev, 0.0)
            h_prev = jnp.where(row == 0, prev, pltpu.roll(h, 1, 0))
            da = lam * h_prev
            d_mult = lam * rcb * gi
            d_la = da * a - d_mult * (a * a) / mult
            d_grp = d_la * (-LRU_C * sp) * gr * (1.0 - gr)
            d_gip = lam * rcb * mult * gi * (1.0 - gi)
            dgr_ref[rows, :] = d_grp.astype(dgr_ref.dtype)
            dgi_ref[rows, :] = d_gip.astype(dgi_ref.dtype)
            drc_ref[rows, :] = lam * gi * mult
            s_i = s_i + jnp.sum(d_gip, axis=0, keepdims=True)
            s_r = s_r + jnp.sum(d_grp, axis=0, keepdims=True)
            s_sp = s_sp + jnp.sum(d_la * gr, axis=0, keepdims=True)
            return mu_out, s_i, s_r, s_sp

        _, s_i, s_r, s_sp = lax.fori_loop(0, nb, step, (zero, zero, zero, zero))
        dbi_ref[...] = s_i
        dbr_ref[...] = s_r
        dl_ref[...] = (-LRU_C * s_sp) * (-_sigmoid(-lp))

    col = pl.BlockSpec((S, tc), lambda j: (0, j))
    vec = pl.BlockSpec((1, tc), lambda j: (0, j))
    return pl.pallas_call(
        body, name=name, grid=(C // tc,),
        in_specs=[col, col, col, col, col, col, vec],
        out_specs=[col, col, col, col, vec, vec, vec],
        out_shape=[_hbm_out((S, C), BF16), _hbm_out((S, C), BF16),
                   _hbm_out((S, C), BF16), _hbm_out((S, C), F32),
                   _hbm_out((1, C), F32), _hbm_out((1, C), F32),
                   _hbm_out((1, C), F32)],
        compiler_params=_params(("parallel",)),
    )(dm, proj, hrec, rc, gip, grp, lru_p)


def _cumsum_rows(name, u, reverse):
    S, C = u.shape
    nb = S // SCAN_ROWS

    def body(u_ref, o_ref):
        def step(t, carry):
            b = nb - 1 - t if reverse else t
            rows = pl.ds(pl.multiple_of(b * SCAN_ROWS, SCAN_ROWS), SCAN_ROWS)
            ub = u_ref[rows, :]
            h, carry = _block_scan(jnp.ones_like(ub), ub, carry, reverse)
            o_ref[rows, :] = h
            return carry

        lax.fori_loop(0, nb, step, jnp.zeros((1, C), F32))

    spec = pl.BlockSpec((S, C), lambda i: (0, 0))
    return pl.pallas_call(
        body, name=name, grid=(1,), in_specs=[spec], out_specs=spec,
        out_shape=_hbm_out((S, C), F32),
        compiler_params=_params(("arbitrary",)),
    )(u)


def _shift_down(x, k):
    row = lax.broadcasted_iota(jnp.int32, x.shape, 0)
    return jnp.where(row >= k, pltpu.roll(x, k, 0), 0.0)


def _shift_up(x, k):
    n = x.shape[0]
    row = lax.broadcasted_iota(jnp.int32, x.shape, 0)
    return jnp.where(row < n - k, pltpu.roll(x, n - k, 0), 0.0)


def _conv_fwd(name, proj, w, b, tc=256):
    S, C2 = proj.shape
    C = C2 // 2
    tc = _tile(C, tc)
    off = C // tc

    def body(x_ref, w_ref, b_ref, o_ref, ob_ref):
        x = x_ref[...]
        out = b_ref[...] + w_ref[3:4, :] * x
        for k in (1, 2, 3):
            out = out + w_ref[3 - k:4 - k, :] * _shift_down(x, k)
        o_ref[...] = out
        ob_ref[...] = out.astype(BF16)

    col = pl.BlockSpec((S, tc), lambda j: (0, j))
    return pl.pallas_call(
        body, name=name, grid=(C // tc,),
        in_specs=[pl.BlockSpec((S, tc), lambda j: (0, off + j)),
                  pl.BlockSpec((4, tc), lambda j: (0, j)), pl.BlockSpec((1, tc), lambda j: (0, j))],
        out_specs=[col, col],
        out_shape=[_hbm_out((S, C), F32), _hbm_out((S, C), BF16)],
        compiler_params=_params(("parallel",)),
    )(proj, w, b)


def _conv_bwd(name, drc, proj, w, tc=256):
    S, C = drc.shape
    tc = _tile(C, tc)
    off = C // tc

    def body(y_ref, x_ref, w_ref, dx_ref, dw_ref, db_ref):
        y = y_ref[...]
        x = x_ref[...]
        dx = w_ref[3:4, :] * y
        dw_ref[3:4, :] = jnp.sum(y * x, axis=0, keepdims=True)
        for k in (1, 2, 3):
            dx = dx + w_ref[3 - k:4 - k, :] * _shift_up(y, k)
            dw_ref[3 - k:4 - k, :] = jnp.sum(y * _shift_down(x, k), axis=0, keepdims=True)
        dx_ref[...] = dx.astype(dx_ref.dtype)
        db_ref[...] = jnp.sum(y, axis=0, keepdims=True)

    col = pl.BlockSpec((S, tc), lambda j: (0, j))
    return pl.pallas_call(
        body, name=name, grid=(C // tc,),
        in_specs=[col, pl.BlockSpec((S, tc), lambda j: (0, off + j)), pl.BlockSpec((4, tc), lambda j: (0, j))],
        out_specs=[col, pl.BlockSpec((4, tc), lambda j: (0, j)), pl.BlockSpec((1, tc), lambda j: (0, j))],
        out_shape=[_hbm_out((S, C), BF16), _hbm_out((4, C), F32),
                   _hbm_out((1, C), F32)],
        compiler_params=_params(("parallel",)),
    )(drc, proj, w)


def _gates_fwd(name, rcb, wg, bg):
    S, C = rcb.shape
    nblk, bw, _ = wg.shape

    def body(x_ref, w_ref, b_ref, gi_ref, gr_ref):
        g = jnp.dot(x_ref[...], w_ref[...], preferred_element_type=F32) + b_ref[...]
        gi_ref[...] = g[:, :bw]
        gr_ref[...] = g[:, bw:]

    col = pl.BlockSpec((S, bw), lambda n: (0, n))
    return pl.pallas_call(
        body, name=name, grid=(nblk,),
        in_specs=[col, pl.BlockSpec((None, bw, 2 * bw), lambda n: (n, 0, 0)),
                  pl.BlockSpec((None, 1, 2 * bw), lambda n: (n, 0, 0))],
        out_specs=[col, col],
        out_shape=[_hbm_out((S, C), F32), _hbm_out((S, C), F32)],
        compiler_params=_params(("parallel",)),
    )(rcb, wg, bg)


def _gates_bwd(name, dgi, dgr, rcb, wg, drc1):
    S, C = rcb.shape
    nblk, bw, _ = wg.shape

    def body(dgi_ref, dgr_ref, x_ref, w_ref, d1_ref, drc_ref, dw_ref):
        w = w_ref[...]
        x = x_ref[...]
        di, dr = dgi_ref[...], dgr_ref[...]
        drc_ref[...] = (d1_ref[...]
                        + lax.dot_general(di, w[:, :bw], _NT, preferred_element_type=F32)
                        + lax.dot_general(dr, w[:, bw:], _NT, preferred_element_type=F32))
        dw_ref[:, :bw] = lax.dot_general(x, di, _TN, preferred_element_type=F32).astype(dw_ref.dtype)
        dw_ref[:, bw:] = lax.dot_general(x, dr, _TN, preferred_element_type=F32).astype(dw_ref.dtype)

    col = pl.BlockSpec((S, bw), lambda n: (0, n))
    wspec = pl.BlockSpec((None, bw, 2 * bw), lambda n: (n, 0, 0))
    return pl.pallas_call(
        body, name=name, grid=(nblk,),
        in_specs=[col, col, col, wspec, col], out_specs=[col, wspec],
        out_shape=[_hbm_out((S, C), F32), _hbm_out((nblk, bw, 2 * bw), BF16)],
        compiler_params=_params(("parallel",)),
    )(dgi, dgr, rcb, wg, drc1)


def _att_tile(S):
    return next(t for t in (512, 256, 128) if S % t == 0)


def _head_lanes(shape):
    return lax.broadcasted_iota(jnp.int32, shape, len(shape) - 1) < HEAD_DIM


def _key_bias(c_blk):
    first = _head_lanes(c_blk.shape)
    rolled = pltpu.roll(c_blk, HEAD_DIM, 1)
    return jnp.where(first, c_blk, rolled), jnp.where(first, rolled, c_blk)


def _over_keys(x, op):
    n = x.shape[0]
    while n > SUBLANES:
        n //= 2
        x = op(x[:n], x[n:2 * n])
    return (jnp.max if op is jnp.maximum else jnp.sum)(x, axis=0, keepdims=True)


def _causal_t(T, cc):
    r = lax.broadcasted_iota(jnp.int32, (T, LANES), 0)
    c = lax.broadcasted_iota(jnp.int32, (T, LANES), 1) + cc * LANES
    return r <= c


def _attn_fwd(name, q, kv, cfull):
    S, D = q.shape
    HP = D // LANES
    T = _att_tile(S)
    nq = S // T
    NC = T // LANES

    def body(q_ref, k_ref, v_ref, c_ref, o_ref, of_ref, lse_ref, bias, vT, acc, m_scr, l_scr):
        def prologue(i, _):
            rows = pl.ds(pl.multiple_of(i * T, T), T)
            bias[0, rows, :], bias[1, rows, :] = _key_bias(c_ref[rows, :])
            vT[i] = v_ref[rows, :].astype(F32).T.astype(BF16)
            return 0

        lax.fori_loop(0, nq, prologue, 0)

        def q_step(qi, _):
            q0 = pl.multiple_of(qi * T, T)
            qb = q_ref[pl.ds(q0, T), :]
            m_scr[...] = jnp.full(m_scr.shape, -jnp.inf, F32)
            l_scr[...] = jnp.zeros(l_scr.shape, F32)
            acc[...] = jnp.zeros(acc.shape, F32)

            def tile(kj, masked):
                ks = pl.ds(pl.multiple_of(kj * T, T), T)
                kf = k_ref[ks, :].astype(F32)
                first = _head_lanes(kf.shape)
                kms = [jnp.where(first if hh == 0 else jnp.logical_not(first), kf, 0.0).astype(BF16) for hh in range(2)]
                sTs = [lax.dot_general(km, qb, _NT, preferred_element_type=F32) for km in kms]
                for hh in range(2):
                    b = bias[hh, ks, :]
                    ps = []
                    for cc in range(NC):
                        cols = slice(cc * LANES, (cc + 1) * LANES)
                        s = sTs[hh][:, cols] + b
                        if masked:
                            s = jnp.where(_causal_t(T, cc), s, -jnp.inf)
                        m_old = m_scr[hh, cc]
                        m_new = jnp.maximum(m_old, _over_keys(s, jnp.maximum))
                        alpha = jnp.exp(m_old - m_new)
                        p = jnp.exp(s - m_new)
                        l_scr[hh, cc] = alpha * l_scr[hh, cc] + _over_keys(p, jnp.add)
                        m_scr[hh, cc] = m_new
                        ps.append(p.astype(BF16))
                        acc[hh, :, cols] = acc[hh, :, cols] * alpha
                    acc[hh] += jnp.dot(vT[kj, hh * HEAD_DIM:(hh + 1) * HEAD_DIM, :], jnp.concatenate(ps, axis=1),
                                       preferred_element_type=F32)

            def inner(kj, _):
                tile(kj, False)
                return 0

            lax.fori_loop(0, qi, inner, 0)
            tile(qi, True)
            outs = []
            for hh in range(2):
                inv = jnp.concatenate([1.0 / l_scr[hh, cc] for cc in range(NC)], axis=1)
                outs.append(acc[hh] * inv)
                for cc in range(NC):
                    lse_ref[hh:hh + 1, pl.ds(q0 + cc * LANES, LANES)] = m_scr[hh, cc] + jnp.log(l_scr[hh, cc])
            out = jnp.concatenate(outs, axis=0).T
            o_ref[pl.ds(q0, T), :] = out.astype(o_ref.dtype)
            of_ref[pl.ds(q0, T), :] = out
            return 0

        lax.fori_loop(0, nq, q_step, 0)

    blk = lambda off: pl.BlockSpec((S, LANES), lambda p: (0, off + p))
    return pl.pallas_call(
        body, name=name, grid=(HP,),
        in_specs=[blk(0), blk(0), blk(HP), blk(0)],
        out_specs=[blk(0), blk(0), pl.BlockSpec((None, 2, S), lambda p: (p, 0, 0))],
        out_shape=[_hbm_out((S, D), BF16), _hbm_out((S, D), F32),
                   _hbm_out((HP, 2, S), F32)],
        scratch_shapes=[pltpu.VMEM((2, S, LANES), F32), pltpu.VMEM((nq, LANES, T), BF16),
                        pltpu.VMEM((2, HEAD_DIM, T), F32), pltpu.VMEM((2, NC, 1, LANES), F32),
                        pltpu.VMEM((2, NC, 1, LANES), F32)],
        compiler_params=_params(("parallel",)),
    )(q, kv, kv, cfull)


def _attn_bwd(name, q, kv, cfull, of, do, lse3):
    S, D = q.shape
    HP = D // LANES
    T = _att_tile(S)
    nq = S // T
    NC = T // LANES
    scale = HEAD_DIM ** -0.5

    def body(q_ref, k_ref, v_ref, c_ref, of_ref, do_ref, lse_ref,
             dq_ref, dk_ref, dv_ref, dck_ref, drq_ref, bias, kT, dqT, delta, dr_scr):
        def prologue(i, _):
            rows = pl.ds(pl.multiple_of(i * T, T), T)
            bias[0, rows, :], bias[1, rows, :] = _key_bias(c_ref[rows, :])
            kT[i] = k_ref[rows, :].astype(F32).T.astype(BF16)
            prodT = (do_ref[rows, :].astype(F32) * of_ref[rows, :]).T
            for hh in range(2):
                delta[hh:hh + 1, rows] = jnp.sum(prodT[hh * HEAD_DIM:(hh + 1) * HEAD_DIM], axis=0, keepdims=True)
            dqT[i] = jnp.zeros((LANES, T), F32)
            return 0

        lax.fori_loop(0, nq, prologue, 0)
        dr_scr[...] = jnp.zeros(dr_scr.shape, F32)

        def kv_step(kj, _):
            ks = pl.ds(pl.multiple_of(kj * T, T), T)
            kf = k_ref[ks, :].astype(F32)
            vf = v_ref[ks, :].astype(F32)
            first = _head_lanes(kf.shape)
            masks = [first, jnp.logical_not(first)]
            kms = [jnp.where(m, kf, 0.0).astype(BF16) for m in masks]
            vms = [jnp.where(m, vf, 0.0).astype(BF16) for m in masks]

            def tile(qi, carry, masked):
                q0 = pl.multiple_of(qi * T, T)
                qb = q_ref[pl.ds(q0, T), :]
                dob = do_ref[pl.ds(q0, T), :]
                sTs = [lax.dot_general(km, qb, _NT, preferred_element_type=F32) for km in kms]
                dpTs = [lax.dot_general(vm, dob, _NT, preferred_element_type=F32) for vm in vms]
                out = []
                for hh in range(2):
                    dk_a, dv_a, dc_a = carry[3 * hh:3 * hh + 3]
                    b = bias[hh, ks, :]
                    head = slice(hh * HEAD_DIM, (hh + 1) * HEAD_DIM)
                    ps, dss = [], []
                    for cc in range(NC):
                        cols = slice(cc * LANES, (cc + 1) * LANES)
                        at = pl.ds(q0 + cc * LANES, LANES)
                        p = jnp.exp(sTs[hh][:, cols] + b - lse_ref[hh:hh + 1, at])
                        if masked:
                            p = jnp.where(_causal_t(T, cc), p, 0.0)
                        ds = p * (dpTs[hh][:, cols] - delta[hh:hh + 1, at])
                        ps.append(p.astype(BF16))
                        dss.append(ds.astype(BF16))
                        dc_a = dc_a + ds
                        dr_scr[hh:hh + 1, at] += _over_keys(ds, jnp.add)
                    pT = jnp.concatenate(ps, axis=1)
                    dsT = jnp.concatenate(dss, axis=1)
                    dv_a = dv_a + jnp.dot(pT, dob, preferred_element_type=F32)
                    dk_a = dk_a + jnp.dot(dsT, qb, preferred_element_type=F32)
                    dqT[qi, head, :] += jnp.dot(kT[kj, head, :], dsT, preferred_element_type=F32)
                    out += [dk_a, dv_a, dc_a]
                return tuple(out)

            zero = jnp.zeros((T, LANES), F32)
            carry = tile(kj, (zero,) * 6, True)
            dk0, dv0, dc0, dk1, dv1, dc1 = lax.fori_loop(kj + 1, nq, lambda qi, c: tile(qi, c, False), carry)
            dk_ref[ks, :] = jnp.where(first, dk0, dk1)
            dv_ref[ks, :] = jnp.where(first, dv0, dv1)
            dck_ref[ks, :] = jnp.where(first, jnp.broadcast_to(-jnp.sum(dc0, axis=1, keepdims=True), (T, LANES)),
                                       jnp.broadcast_to(-jnp.sum(dc1, axis=1, keepdims=True), (T, LANES)))
            return 0

        lax.fori_loop(0, nq, kv_step, 0)

        def epilogue(i, _):
            rows = pl.ds(pl.multiple_of(i * T, T), T)
            dq_ref[rows, :] = (dqT[i].T * scale).astype(dq_ref.dtype)
            return 0

        lax.fori_loop(0, nq, epilogue, 0)
        drq_ref[...] = dr_scr[...]

    blk = lambda off: pl.BlockSpec((S, LANES), lambda p: (0, off + p))
    row_spec = pl.BlockSpec((None, 2, S), lambda p: (p, 0, 0))
    return pl.pallas_call(
        body, name=name, grid=(HP,),
        in_specs=[blk(0), blk(0), blk(HP), blk(0), blk(0), blk(0), row_spec],
        out_specs=[blk(0), blk(0), blk(0), blk(0), row_spec],
        out_shape=[_hbm_out((S, D), BF16), _hbm_out((S, D), F32),
                   _hbm_out((S, D), F32), _hbm_out((S, D), F32),
                   _hbm_out((HP, 2, S), F32)],
        scratch_shapes=[pltpu.VMEM((2, S, LANES), F32), pltpu.VMEM((nq, LANES, T), BF16),
                        pltpu.VMEM((nq, LANES, T), F32), pltpu.VMEM((2, S), F32), pltpu.VMEM((2, S), F32)],
        compiler_params=_params(("parallel",)),
    )(q, kv, kv, cfull, of, do, lse3)


def _logsig_fwd(name, f):
    S, C = f.shape

    def body(f_ref, o_ref):
        o_ref[...] = -_softplus(-f_ref[...])

    spec = pl.BlockSpec((S, C), lambda i: (0, 0))
    return pl.pallas_call(body, name=name, grid=(1,), in_specs=[spec], out_specs=spec,
                          out_shape=_hbm_out((S, C), F32),
                          compiler_params=_params(("arbitrary",)))(f)


def _logsig_bwd(name, dls, f):
    S, C = f.shape

    def body(d_ref, f_ref, o_ref, s_ref):
        df = d_ref[...] * _sigmoid(-f_ref[...])
        o_ref[...] = df.astype(o_ref.dtype)
        s_ref[...] = jnp.sum(df, axis=0, keepdims=True)

    spec = pl.BlockSpec((S, C), lambda i: (0, 0))
    return pl.pallas_call(body, name=name, grid=(1,), in_specs=[spec, spec],
                          out_specs=[spec, pl.BlockSpec((1, C), lambda i: (0, 0))],
                          out_shape=[_hbm_out((S, C), BF16), _hbm_out((1, C), F32)],
                          compiler_params=_params(("arbitrary",)))(dls, f)


def _add_cast(name, parts, out_dtype, tr=256):
    S, C = parts[0].shape
    tr = _tile(S, tr)
    n = len(parts)

    def body(*refs):
        acc = refs[0][...].astype(F32)
        for r in refs[1:n]:
            acc = acc + r[...].astype(F32)
        refs[n][...] = acc.astype(out_dtype)

    spec = pl.BlockSpec((tr, C), lambda i: (i, 0))
    return pl.pallas_call(body, name=name, grid=(S // tr,), in_specs=[spec] * n, out_specs=spec,
                          out_shape=_hbm_out((S, C), out_dtype),
                          compiler_params=_params(("parallel",)))(*parts)


def _local_step(x, target, gains, layer_weights, layer_grads):
    S, D = x.shape
    HP = D // LANES
    scale = HEAD_DIM ** -0.5
    tm = _tile(S, 512)
    tx = _tile(S, 256)
    td = _tile(D, 512)
    saved = []
    h = x
    l = 0
    kv = cfull = f_pre = hn_kv = h_kv = None
    while True:
        W = layer_weights(l, "mix", h)
        if W is None:
            break
        recurrent = "w_rec_in" in W
        if l == 0:
            xn = _rmsnorm_fwd("mix_norm_0", h, gains["mix"][0])
        if recurrent:
            CH = W["w_rec_in"].shape[-1]
            C = 2 * CH
            proj = _mm(f"rec_in_{l}", "nn", xn, W["w_rec_in"], grid=(S // tm, N_CHIPS),
                       a_spec=pl.BlockSpec((tm, D), lambda i, j: (i, 0)),
                       b_spec=pl.BlockSpec((None, D, CH), lambda i, j: (j, 0, 0)),
                       out_shape=(S, 2 * C), out_dtype=F32,
                       out_spec=pl.BlockSpec((tm, CH), lambda i, j: (i, j)))
            rc, rcb = _conv_fwd(f"conv_{l}", proj, W["conv_w"], W["conv_b"])
            gip, grp = _gates_fwd(f"gates_{l}", rcb, W["w_gates"], W["b_gates"])
            hrec, m = _lru_fwd(f"lru_{l}", proj, rc, gip, grp, W["lru_param"])
            h_mid, hn = _mm_nn(f"rec_out_{l}", m, W["w_rec_out"], out_dtype=F32, res=h, tn=D, norm_gain=gains["ffn"][l])
            mix_saved = (xn, proj, rc, rcb, gip, grp, hrec, m)
        else:
            if "w_kv" in W:
                h_kv = h
                hn_kv = _rmsnorm_fwd("kv_norm", h, W["norm_kv"])
                kv = _mm_nn("kv_proj", hn_kv, W["w_kv"], out_dtype=BF16)
                f_pre = _mm_nn("f_proj", hn_kv, W["w_f"], out_dtype=F32, bias=W["b_f"])
                c = _cumsum_rows("c_cumsum", _logsig_fwd("logsig", f_pre), False)
                cfull = jnp.repeat(-c[:, :2 * HP], HEAD_DIM, axis=1)
            q = _mm_nn(f"q_proj_{l}", xn, W["w_q"], out_dtype=BF16, scale=scale)
            o, of, lse = _attn_fwd(f"attn_fwd_{l}", q, kv, cfull)
            h_mid, hn = _mm_nn(f"o_proj_{l}", o, W["w_o"], out_dtype=F32, res=h, tn=D, norm_gain=gains["ffn"][l])
            mix_saved = (xn, q, o, of, lse)
        W = {**W, **layer_weights(l, "ffn", h_mid)}
        z3, act = _swiglu_fwd(f"ffn_in_{l}", hn, W["w_ffn_in"])
        saved.append((W, h, h_mid, mix_saved, (hn, z3, act)))
        l += 1
        if l < len(gains["mix"]):
            h, xn = _mm_nn(f"ffn_out_{l - 1}", act, W["w_ffn_out"], out_dtype=F32, res=h_mid, tn=D,
                           norm_gain=gains["mix"][l])
        else:
            h = _mm_nn(f"ffn_out_{l - 1}", act, W["w_ffn_out"], out_dtype=F32, res=h_mid, tn=D)

    dh, dhb, dg_final, loss_row = _loss_head("loss_head", h, target, gains["final"])

    dk_parts, dv_parts, dc_parts = [], [], []
    token = None
    for l in reversed(range(len(saved))):
        W, h_in, h_mid, mix_saved, (hn, z3, act) = saved[l]
        recurrent = "w_rec_in" in W
        FH = W["w_ffn_in"].shape[-1]
        G = {}
        norm_ffn = gains["ffn"][l]
        if token is not None:
            norm_ffn = norm_ffn + jnp.minimum(token[:1, :1], 0.0)
        G["w_ffn_out"] = _mm_tn(f"d_ffn_out_{l}", act, dhb, out_dtype=BF16, tn=D)
        dz3 = _swiglu_bwd(f"d_act_{l}", dhb, W["w_ffn_out"], z3)
        G["w_ffn_in"] = _mm(
            f"d_ffn_in_{l}", "tn", hn, dz3, grid=(D // td, N_CHIPS),
            a_spec=pl.BlockSpec((S, td), lambda i, j: (0, i)),
            b_spec=pl.BlockSpec((None, S, FH), lambda i, j: (j // 2, 0, j % 2)),
            out_shape=(N_CHIPS, D, FH), out_dtype=BF16,
            out_spec=pl.BlockSpec((None, td, FH), lambda i, j: (j, i, 0)))
        token = layer_grads(l, "ffn", G)
        G = {}
        norm_ffn = norm_ffn + jnp.minimum(token[:1, :1], 0.0)
        dh, dhb, dgp = _mm(f"d_ffn_hn_{l}", "nt", dz3, W["w_ffn_in"], grid=(S // tx, 1),
                           a_spec=[pl.BlockSpec((None, tx, FH), functools.partial(lambda i, j, k: (k // 2, i, k % 2), k=k))
                                   for k in range(N_CHIPS)],
                           b_spec=[pl.BlockSpec((None, D, FH), functools.partial(lambda i, j, k: (k, 0, 0), k=k))
                                   for k in range(N_CHIPS)],
                           out_shape=(S, D), out_dtype=F32, out_spec=pl.BlockSpec((tx, D), lambda i, j: (i, 0)),
                           norm_bwd=(h_mid, norm_ffn, dh))
        G["norm_ffn"] = jnp.sum(dgp, axis=0)
        if recurrent:
            CH = W["w_rec_in"].shape[-1]
            C = 2 * CH
            xn, proj, rc, rcb, gip, grp, hrec, m = mix_saved
            G["w_rec_out"] = _mm_tn(f"d_rec_out_{l}", m, dhb, out_dtype=BF16, tn=D)
            dm = _mm_nt(f"d_m_{l}", dhb, W["w_rec_out"], out_dtype=F32, tn=C)
            dgb, dgi, dgr, drc1, G["b_gi"], G["b_gr"], G["lru_param"] = _lru_bwd(
                f"d_lru_{l}", dm, proj, hrec, rc, gip, grp, W["lru_param"])
            drc, G["w_gates"] = _gates_bwd(f"d_gates_{l}", dgi, dgr, rcb, W["w_gates"], drc1)
            drec, G["conv_w"], G["conv_b"] = _conv_bwd(f"d_conv_{l}", drc, proj, W["conv_w"])
            dproj = jnp.concatenate([dgb, drec], axis=1)
            G["w_rec_in"] = _mm(
                f"d_rec_in_{l}", "tn", xn, dproj, grid=(1, N_CHIPS),
                a_spec=pl.BlockSpec((S, D), lambda i, j: (0, 0)),
                b_spec=pl.BlockSpec((S, CH), lambda i, j: (0, j)),
                out_shape=(N_CHIPS, D, CH), out_dtype=BF16,
                out_spec=pl.BlockSpec((None, D, CH), lambda i, j: (j, 0, 0)))
            dh, dhb, dgp = _mm(f"d_rec_xn_{l}", "nt", dproj, W["w_rec_in"], grid=(S // tx, 1),
                               a_spec=[pl.BlockSpec((tx, CH), functools.partial(lambda i, j, k: (i, k), k=k))
                                       for k in range(N_CHIPS)],
                               b_spec=[pl.BlockSpec((None, D, CH), functools.partial(lambda i, j, k: (k, 0, 0), k=k))
                                       for k in range(N_CHIPS)],
                               out_shape=(S, D), out_dtype=F32, out_spec=pl.BlockSpec((tx, D), lambda i, j: (i, 0)),
                               norm_bwd=(h_in, gains["mix"][l], dh))
        else:
            xn, q, o, of, lse = mix_saved
            G["w_o"] = _mm_tn(f"d_o_proj_{l}", o, dhb, out_dtype=BF16, tn=D)
            do = _mm_nt(f"d_o_{l}", dhb, W["w_o"], out_dtype=BF16, tn=D)
            dq, dk, dv, dck, drq = _attn_bwd(f"attn_bwd_{l}", q, kv, cfull, of, do, lse)
            dk_parts.append(dk)
            dv_parts.append(dv)
            dc_parts.append(dck[:, ::HEAD_DIM] + drq.reshape(2 * HP, S).T)
            G["w_q"] = _mm_tn(f"d_q_proj_{l}", xn, dq, out_dtype=BF16, tn=D)
            dh, dhb, dgp = _mm_nt(f"d_q_xn_{l}", dq, W["w_q"], out_dtype=F32, tn=D, norm_bwd=(h_in, gains["mix"][l], dh))
        G["norm_mix"] = jnp.sum(dgp, axis=0)
        if "w_kv" in W:
            dkb = _add_cast("dk_sum", dk_parts, BF16)
            dvb = _add_cast("dv_sum", dv_parts, BF16)
            dkv = jnp.concatenate([dkb, dvb], axis=1)
            dc = sum(dc_parts[1:], dc_parts[0])
            dc_pad = jnp.pad(dc, ((0, 0), (0, LANES - 2 * HP)))
            dls = _cumsum_rows("dc_cumsum", dc_pad, True)
            dfb, G["b_f"] = _logsig_bwd("d_logsig", dls, f_pre)
            G["w_kv"] = _mm_tn("d_kv_proj", hn_kv, dkv, out_dtype=BF16)
            G["w_f"] = _mm_tn("d_f_proj", hn_kv, dfb, out_dtype=F32)
            dhn_f = _mm_nt("d_f_hn", dfb, W["w_f"], out_dtype=F32, tn=D)
            dh, dhb, dgp = _mm_nt("d_kv_hn", dkv, W["w_kv"], out_dtype=F32, tn=D, res=dhn_f,
                                  norm_bwd=(h_kv, W["norm_kv"], dh))
            G["norm_kv"] = jnp.sum(dgp, axis=0)
        token = layer_grads(l, "mix", G)
    return loss_row, dh, dg_final


_ANY = pl.BlockSpec(memory_space=pl.ANY)


def _position():
    return lax.axis_index("x"), lax.axis_index("y"), lax.axis_index("c")


def _chip_peers(x, y):
    return [(1 - x, y), (x, 1 - y), (1 - x, 1 - y)]


def _half_rows(c, n):
    h = n // 2
    assert h % 16 == 0
    return pl.ds(pl.multiple_of(c * h, 16), h)


def _place_own(name, shard, layer, me):
    _, R, C = shard.shape
    tr = _row_tile(R, C, shard.dtype.itemsize)

    def body(me_ref, x_ref, o_ref):
        o_ref[...] = x_ref[...]

    return pl.pallas_call(
        body, name=name,
        grid_spec=pltpu.PrefetchScalarGridSpec(
            num_scalar_prefetch=1, grid=(R // tr,),
            in_specs=[pl.BlockSpec((None, tr, C), lambda i, me_ref: (layer, i, 0))],
            out_specs=pl.BlockSpec((None, tr, C), lambda i, me_ref: (me_ref[0], i, 0))),
        out_shape=_hbm_out((N_CHIPS, R, C), shard.dtype),
        compiler_params=_params(("parallel",)),
    )(me, shard)


def _gather_smalls(name, smalls):
    ns = len(smalls)

    def body(*refs):
        ins, outs = refs[:ns], refs[ns:2 * ns]
        send_sems, recv_sems, local_sems = refs[2 * ns:]
        x, y, c = _position()
        me = 2 * x + y
        peers = _chip_peers(x, y)

        def remote(t, k, chip):
            px, py = peers[k]
            return pltpu.make_async_remote_copy(
                src_ref=ins[t], dst_ref=outs[t].at[chip], send_sem=send_sems.at[3 * t + k],
                recv_sem=recv_sems.at[3 * t + k], device_id=(px, py, c), device_id_type=MESH)

        local = [pltpu.make_async_copy(ins[t], outs[t].at[me], local_sems.at[t]) for t in range(ns)]
        for t in range(ns):
            local[t].start()
            for k in range(3):
                remote(t, k, me).start()
        for t in range(ns):
            for k in range(3):
                px, py = peers[k]
                remote(t, k, 2 * px + py).wait_recv()
        for t in range(ns):
            for k in range(3):
                remote(t, k, me).wait_send()
            local[t].wait()

    return pl.pallas_call(
        body, name=name, in_specs=[_ANY] * ns, out_specs=[_ANY] * ns,
        out_shape=[_hbm_out((N_CHIPS,) + s.shape, s.dtype) for s in smalls],
        scratch_shapes=[pltpu.SemaphoreType.DMA((3 * ns,)), pltpu.SemaphoreType.DMA((3 * ns,)),
                        pltpu.SemaphoreType.DMA((ns,))],
    )(*smalls)


_SEM = pl.BlockSpec(memory_space=pltpu.SEMAPHORE)
_SPLIT = pltpu.CompilerParams(has_side_effects=pltpu.SideEffectType.DATAFLOW_SIDE_EFFECTING)


def _weight_copy(shards, buf, items, sems, i, k, chip_of_dst, peers, c):
    w, l = items[i]
    px, py = peers[k]
    half = _half_rows(c, shards[w].shape[1])
    return pltpu.make_async_remote_copy(
        src_ref=shards[w].at[l, half], dst_ref=buf.at[chip_of_dst, half],
        send_sem=sems[0].at[3 * i + k], recv_sem=sems[1].at[3 * i + k],
        device_id=(px, py, c), device_id_type=MESH)


def _gather_start(name, shards, bufs, items, after):
    nw, n = len(shards), len(bufs)

    def body(*refs):
        ins, outs, sems = refs[:nw], refs[nw + n + 1:nw + 2 * n + 1], refs[nw + 2 * n + 1:]
        x, y, c = _position()
        peers = _chip_peers(x, y)
        for i in range(n):
            for k in range(3):
                _weight_copy(ins, outs[i], items, sems, i, k, 2 * x + y, peers, c).start()

    res = pl.pallas_call(
        body, name=name, in_specs=[_ANY] * (nw + n + 1), out_specs=[_ANY] * n + [_SEM, _SEM],
        out_shape=[_hbm_out(b.shape, b.dtype) for b in bufs]
        + [pltpu.SemaphoreType.DMA((3 * n,)), pltpu.SemaphoreType.DMA((3 * n,))],
        input_output_aliases={nw + i: i for i in range(n)}, compiler_params=_SPLIT,
    )(*shards, *bufs, after)
    return res[:n], res[n:]


def _gather_wait(name, shards, bufs, items, ids, sems, after):
    nw, m = len(shards), len(ids)

    def body(*refs):
        ins, bs = refs[:nw], refs[nw:nw + m]
        sem_refs = refs[nw + m:nw + m + 2]
        x, y, c = _position()
        peers = _chip_peers(x, y)
        for j, i in enumerate(ids):
            for k in range(3):
                px, py = peers[k]
                _weight_copy(ins, bs[j], items, sem_refs, i, k, 2 * px + py, peers, c).wait_recv()
        for j, i in enumerate(ids):
            for k in range(3):
                _weight_copy(ins, bs[j], items, sem_refs, i, k, 2 * x + y, peers, c).wait_send()

    res = pl.pallas_call(
        body, name=name, in_specs=[_ANY] * (nw + m) + [_SEM, _SEM, _ANY], out_specs=[_ANY] * m,
        out_shape=[_hbm_out(bufs[i].shape, bufs[i].dtype) for i in ids],
        input_output_aliases={nw + j: j for j in range(m)}, compiler_params=_SPLIT,
    )(*shards, *[bufs[i] for i in ids], *sems, after)
    return list(res)


def _gather_d2d(name, bufs):
    n = len(bufs)

    def body(*refs):
        ins, outs = refs[:n], refs[n:2 * n]
        send_sems, recv_sems = refs[2 * n:]
        x, y, c = _position()
        peers = _chip_peers(x, y)

        def remote(i, k, core):
            px, py = peers[k]
            half = _half_rows(core, ins[i].shape[1])
            return pltpu.make_async_remote_copy(
                src_ref=ins[i].at[2 * px + py, half], dst_ref=outs[i].at[2 * px + py, half],
                send_sem=send_sems.at[3 * i + k], recv_sem=recv_sems.at[3 * i + k],
                device_id=(x, y, 1 - c), device_id_type=MESH)

        for i in range(n):
            for k in range(3):
                remote(i, k, c).start()
        for i in range(n):
            for k in range(3):
                remote(i, k, 1 - c).wait_recv()
        for i in range(n):
            for k in range(3):
                remote(i, k, c).wait_send()

    return list(pl.pallas_call(
        body, name=name, in_specs=[_ANY] * n, out_specs=[_ANY] * n,
        out_shape=[_hbm_out(g.shape, g.dtype) for g in bufs],
        input_output_aliases={i: i for i in range(n)},
        scratch_shapes=[pltpu.SemaphoreType.DMA((3 * n,)), pltpu.SemaphoreType.DMA((3 * n,))],
    )(*bufs))


def _reduce_d2d(name, grads):
    n = len(grads)

    def body(*refs):
        ins, outs = refs[:n], refs[n:2 * n]
        send_sems, recv_sems = refs[2 * n:]
        x, y, c = _position()
        remote = [pltpu.make_async_remote_copy(
            src_ref=ins[i].at[:, _half_rows(1 - c, ins[i].shape[1])], dst_ref=outs[i],
            send_sem=send_sems.at[i], recv_sem=recv_sems.at[i],
            device_id=(x, y, 1 - c), device_id_type=MESH) for i in range(n)]
        for cp in remote:
            cp.start()
        for cp in remote:
            cp.wait_recv()
        for cp in remote:
            cp.wait_send()

    return pl.pallas_call(
        body, name=name, in_specs=[_ANY] * n, out_specs=[_ANY] * n,
        out_shape=[_hbm_out((N_CHIPS, g.shape[1] // 2, g.shape[2]), g.dtype) for g in grads],
        scratch_shapes=[pltpu.SemaphoreType.DMA((n,)), pltpu.SemaphoreType.DMA((n,))],
    )(*grads)


def _sum_cores(name, g, other, core):
    _, R, C = g.shape
    H = R // 2
    tr = _row_tile(H, C)
    nb = H // tr

    def body(c_ref, g_ref, o_ref, out_ref):
        out_ref[...] = (g_ref[...].astype(F32) + o_ref[...].astype(F32)).astype(out_ref.dtype)

    return pl.pallas_call(
        body, name=name,
        grid_spec=pltpu.PrefetchScalarGridSpec(
            num_scalar_prefetch=1, grid=(N_CHIPS, nb),
            in_specs=[pl.BlockSpec((None, tr, C), lambda j, i, c_ref: (j, c_ref[0] * nb + i, 0)),
                      pl.BlockSpec((None, tr, C), lambda j, i, c_ref: (j, i, 0))],
            out_specs=pl.BlockSpec((None, tr, C), lambda j, i, c_ref: (j, i, 0))),
        out_shape=_hbm_out((N_CHIPS, H, C), BF16),
        compiler_params=_params(("parallel", "parallel")),
    )(core, g, other)


def _sum_chips(name, received, own, full, layer, me_core):
    _, H, C = received.shape
    tr = _row_tile(H, C)
    nb = H // tr

    def body(s_ref, r_ref, own_ref, full_ref, out_ref):
        acc = r_ref[0].astype(F32)
        for k in (1, 2):
            acc = acc + r_ref[k].astype(F32)
        out_ref[...] = acc + own_ref[...].astype(F32)

    return pl.pallas_call(
        body, name=name,
        grid_spec=pltpu.PrefetchScalarGridSpec(
            num_scalar_prefetch=1, grid=(nb,),
            in_specs=[pl.BlockSpec((3, tr, C), lambda i, s_ref: (0, i, 0)),
                      pl.BlockSpec((None, tr, C), lambda i, s_ref: (s_ref[0], i, 0)),
                      _ANY],
            out_specs=pl.BlockSpec((None, tr, C), lambda i, s_ref: (layer, s_ref[1] * nb + i, 0))),
        out_shape=_hbm_out(full.shape, full.dtype),
        input_output_aliases={3: 0},
        compiler_params=_params(("parallel",)),
    )(me_core, received, own, full)


def _part_copy(parts, recv, sems, i, k, peers, c):
    px, py = peers[k]
    return pltpu.make_async_remote_copy(
        src_ref=parts[i].at[2 * px + py], dst_ref=recv[i].at[k],
        send_sem=sems[0].at[3 * i + k], recv_sem=sems[1].at[3 * i + k],
        device_id=(px, py, c), device_id_type=MESH)


def _scatter_start(name, parts):
    n = len(parts)

    def body(*refs):
        ins, outs, sems, token = refs[:n], refs[n:2 * n], refs[2 * n:2 * n + 2], refs[2 * n + 2]
        x, y, c = _position()
        peers = _chip_peers(x, y)
        for i in range(n):
            for k in range(3):
                _part_copy(ins, outs, sems, i, k, peers, c).start()
        token[...] = jnp.zeros_like(token)

    res = pl.pallas_call(
        body, name=name, in_specs=[_ANY] * n,
        out_specs=[_ANY] * n + [_SEM, _SEM, pl.BlockSpec(memory_space=pltpu.VMEM)],
        out_shape=[_hbm_out((3,) + p.shape[1:], p.dtype) for p in parts]
        + [pltpu.SemaphoreType.DMA((3 * n,)), pltpu.SemaphoreType.DMA((3 * n,)),
           jax.ShapeDtypeStruct((SUBLANES, LANES), F32)],
        compiler_params=_SPLIT,
    )(*parts)
    return list(res[:n]), res[n:n + 2], res[n + 2]


def _scatter_wait(name, parts, recv, sems):
    n = len(parts)

    def body(*refs):
        ins, rs, sem_refs = refs[:n], refs[n:2 * n], refs[2 * n:2 * n + 2]
        x, y, c = _position()
        peers = _chip_peers(x, y)
        for i in range(n):
            for k in range(3):
                _part_copy(ins, rs, sem_refs, i, k, peers, c).wait_recv()
        for i in range(n):
            for k in range(3):
                _part_copy(ins, rs, sem_refs, i, k, peers, c).wait_send()

    return list(pl.pallas_call(
        body, name=name, in_specs=[_ANY] * (2 * n) + [_SEM, _SEM], out_specs=[_ANY] * n,
        out_shape=[_hbm_out(r.shape, r.dtype) for r in recv],
        input_output_aliases={n + i: i for i in range(n)}, compiler_params=_SPLIT,
    )(*parts, *recv, *sems))


def _share_d2d(name, full):
    n = len(full)

    def body(*refs):
        ins, outs = refs[:n], refs[n:2 * n]
        send_sems, recv_sems = refs[2 * n:]
        x, y, c = _position()

        def remote(w, core):
            half = _half_rows(core, ins[w].shape[1])
            return pltpu.make_async_remote_copy(
                src_ref=ins[w].at[:, half], dst_ref=outs[w].at[:, half],
                send_sem=send_sems.at[w], recv_sem=recv_sems.at[w],
                device_id=(x, y, 1 - c), device_id_type=MESH)

        for w in range(n):
            remote(w, c).start()
        for w in range(n):
            remote(w, 1 - c).wait_recv()
        for w in range(n):
            remote(w, c).wait_send()

    return pl.pallas_call(
        body, name=name, in_specs=[_ANY] * n, out_specs=[_ANY] * n,
        out_shape=[_hbm_out(f.shape, f.dtype) for f in full],
        input_output_aliases={w: w for w in range(n)},
        scratch_shapes=[pltpu.SemaphoreType.DMA((n,)), pltpu.SemaphoreType.DMA((n,))],
    )(*full)


def _gather_all(name, a):
    def body(a_ref, o_ref, send_sems, recv_sems, local_sem):
        x, y, c = _position()
        me = 4 * x + 2 * y + c

        def peer(k):
            return (x ^ ((k >> 2) & 1), y ^ ((k >> 1) & 1), c ^ (k & 1))

        def remote(k, slot):
            return pltpu.make_async_remote_copy(
                src_ref=a_ref, dst_ref=o_ref.at[slot], send_sem=send_sems.at[k - 1], recv_sem=recv_sems.at[k - 1],
                device_id=peer(k), device_id_type=MESH)

        local = pltpu.make_async_copy(a_ref, o_ref.at[me], local_sem)
        local.start()
        for k in range(1, N_DEV):
            remote(k, me).start()
        for k in range(1, N_DEV):
            px, py, pc = peer(k)
            remote(k, 4 * px + 2 * py + pc).wait_recv()
        for k in range(1, N_DEV):
            remote(k, me).wait_send()
        local.wait()

    return pl.pallas_call(
        body, name=name, in_specs=[_ANY], out_specs=_ANY,
        out_shape=_hbm_out((N_DEV,) + a.shape, a.dtype),
        scratch_shapes=[pltpu.SemaphoreType.DMA((N_DEV - 1,)), pltpu.SemaphoreType.DMA((N_DEV - 1,)),
                        pltpu.SemaphoreType.DMA],
    )(a)


def _rows2d(a, lead=0):
    return a.reshape(a.shape[:lead] + (-1, a.shape[-1]))


def _row_tile(rows, cols, itemsize=4, target=1 << 20):
    want = max(SUBLANES, target // (cols * itemsize))
    t = min(rows, (want // 16) * 16)
    while t > 16 and rows % t:
        t -= 16
    return t if rows % t == 0 else rows


def _sum_slots(name, r, out_dtype=F32):
    ns = r.shape[0]
    r2 = _rows2d(r, 1)
    _, rows, cols = r2.shape
    tr = _row_tile(rows, cols)

    def body(r_ref, o_ref):
        acc = r_ref[0].astype(F32)
        for s in range(1, ns):
            acc = acc + r_ref[s].astype(F32)
        o_ref[...] = acc.astype(o_ref.dtype)

    out = pl.pallas_call(
        body, name=name, grid=(rows // tr,),
        in_specs=[pl.BlockSpec((ns, tr, cols), lambda i: (0, i, 0))],
        out_specs=pl.BlockSpec((tr, cols), lambda i: (i, 0)),
        out_shape=_hbm_out((rows, cols), out_dtype),
        compiler_params=_params(("parallel",)),
    )(r2)
    return out.reshape(r.shape[1:])


def _adamw(name, g_parts, w, m, v):
    shape = w.shape
    ng = len(g_parts)
    args = [_rows2d(a) for a in (*g_parts, w, m, v)]
    rows, cols = args[0].shape
    tr = _row_tile(rows, cols, target=1 << 19)
    c1 = 1.0 - ADAM_B1 ** ADAM_STEP
    c2 = 1.0 - ADAM_B2 ** ADAM_STEP

    def body(*refs):
        g = refs[0][...]
        for r in refs[1:ng]:
            g = g + r[...]
        w_ref, m_ref, v_ref = refs[ng:ng + 3]
        g_out, d_out, m_out, v_out = refs[ng + 3:]
        mn = ADAM_B1 * m_ref[...] + (1.0 - ADAM_B1) * g
        vn = ADAM_B2 * v_ref[...] + (1.0 - ADAM_B2) * (g * g)
        m_hat = mn / c1
        v_hat = vn / c2
        g_out[...] = g
        d_out[...] = -ADAM_LR * (m_hat / (jnp.sqrt(v_hat) + ADAM_EPS) + ADAM_WD * w_ref[...])
        m_out[...] = mn
        v_out[...] = vn

    spec = pl.BlockSpec((tr, cols), lambda i: (i, 0))
    outs = pl.pallas_call(
        body, name=name, grid=(rows // tr,), in_specs=[spec] * (ng + 3), out_specs=[spec] * 4,
        out_shape=[_hbm_out((rows, cols), F32)] * 4,
        compiler_params=_params(("parallel",)),
    )(*args)
    return tuple(o.reshape(shape) for o in outs)


_WEIGHTS = ["norm_mix", "norm_ffn", "w_ffn_in", "w_ffn_out", "w_rec_in", "conv_w", "conv_b", "w_lru_gates",
            "b_lru_gates", "lru_param", "w_rec_out", "norm_kv", "w_kvf", "b_forget", "w_q", "w_o", "norm_final"]
_BIG = ["w_ffn_in", "w_ffn_out", "w_rec_in", "w_lru_gates", "w_rec_out", "w_kvf", "w_q", "w_o"]


def _stack3(a):
    return a[None] if a.ndim == 2 else a.reshape(a.shape[0], -1, a.shape[-1])


def _pad_lanes(a, n):
    return jnp.pad(a, ((0, 0),) * (a.ndim - 1) + ((0, n - a.shape[-1]),))


def kernel(x, norm_mix, norm_ffn, w_ffn_in, w_ffn_out, w_rec_in, conv_w, conv_b, w_lru_gates, b_lru_gates, lru_param, w_rec_out, norm_kv, w_kvf, b_forget, w_q, w_o, norm_final, loss_target, m_norm_mix, m_norm_ffn, m_w_ffn_in, m_w_ffn_out, m_w_rec_in, m_conv_w, m_conv_b, m_w_lru_gates, m_b_lru_gates, m_lru_param, m_w_rec_out, m_norm_kv, m_w_kvf, m_b_forget, m_w_q, m_w_o, m_norm_final, v_norm_mix, v_norm_ffn, v_w_ffn_in, v_w_ffn_out, v_w_rec_in, v_conv_w, v_conv_b, v_w_lru_gates, v_b_lru_gates, v_lru_param, v_w_rec_out, v_norm_kv, v_w_kvf, v_b_forget, v_w_q, v_w_o, v_norm_final):
    P = dict(norm_mix=norm_mix, norm_ffn=norm_ffn, w_ffn_in=w_ffn_in, w_ffn_out=w_ffn_out, w_rec_in=w_rec_in,
             conv_w=conv_w, conv_b=conv_b, w_lru_gates=w_lru_gates, b_lru_gates=b_lru_gates, lru_param=lru_param,
             w_rec_out=w_rec_out, norm_kv=norm_kv, w_kvf=w_kvf, b_forget=b_forget, w_q=w_q, w_o=w_o,
             norm_final=norm_final)
    M1 = dict(norm_mix=m_norm_mix, norm_ffn=m_norm_ffn, w_ffn_in=m_w_ffn_in, w_ffn_out=m_w_ffn_out,
              w_rec_in=m_w_rec_in, conv_w=m_conv_w, conv_b=m_conv_b, w_lru_gates=m_w_lru_gates,
              b_lru_gates=m_b_lru_gates, lru_param=m_lru_param, w_rec_out=m_w_rec_out, norm_kv=m_norm_kv,
              w_kvf=m_w_kvf, b_forget=m_b_forget, w_q=m_w_q, w_o=m_w_o, norm_final=m_norm_final)
    M2 = dict(norm_mix=v_norm_mix, norm_ffn=v_norm_ffn, w_ffn_in=v_w_ffn_in, w_ffn_out=v_w_ffn_out,
              w_rec_in=v_w_rec_in, conv_w=v_conv_w, conv_b=v_conv_b, w_lru_gates=v_w_lru_gates,
              b_lru_gates=v_b_lru_gates, lru_param=v_lru_param, w_rec_out=v_w_rec_out, norm_kv=v_norm_kv,
              w_kvf=v_w_kvf, b_forget=v_b_forget, w_q=v_w_q, w_o=v_w_o, norm_final=v_norm_final)

    _, S, D = x.shape
    L = norm_mix.shape[0]
    NA, NBLK, BW, GS = w_lru_gates.shape
    NB = w_q.shape[0]
    C = NBLK * BW
    CS = C // N_CHIPS
    H = b_forget.shape[0]
    assert C == D and H * HEAD_DIM == D and H <= LANES
    chip = 2 * lax.axis_index("x") + lax.axis_index("y")

    small_a = jnp.concatenate([conv_w, conv_b[:, None], lru_param[:, None]], axis=1)
    small_a, b_gates = _gather_smalls("gather_smalls", [small_a, b_lru_gates])
    small_a = small_a.transpose(1, 2, 0, 3).reshape(NA, 6, C)
    b_gates = b_gates.transpose(1, 2, 0, 3).reshape(NA, NBLK, 1, N_CHIPS * GS)
    shards = [_stack3(P[w]).astype(BF16) for w in _BIG]
    core = lax.axis_index("c")
    chip_id = jnp.reshape(chip, (1,)).astype(jnp.int32)
    core_id = jnp.reshape(core, (1,)).astype(jnp.int32)
    me_core = jnp.stack([chip, core]).astype(jnp.int32)

    def stage_items(l, part):
        if part == "ffn":
            return [(_BIG.index("w_ffn_in"), l), (_BIG.index("w_ffn_out"), l)]
        if l < NA:
            names, at = ["w_rec_in", "w_lru_gates", "w_rec_out"], l
        else:
            names, at = (["w_kvf"] if l == NA else []) + ["w_q", "w_o"], l - NA
        return [(_BIG.index(n), 0 if n == "w_kvf" else at) for n in names]

    stages = [(l, part) for l in range(L) for part in ("mix", "ffn")]
    items = [it for st in stages for it in stage_items(*st)]
    ids_of = {st: [items.index(it) for it in stage_items(*st)] for st in stages}
    bufs = [_place_own(f"place_{_BIG[w]}_{li}", shards[w], li, chip_id) for w, li in items]
    bufs, gather_sems = _gather_start("gather_start", shards, bufs, items, small_a)

    def layer_weights(l, part, after):
        if l >= L:
            return None
        ids = ids_of[(l, part)]
        got = _gather_wait(f"gather_wait_{part}_{l}", shards, bufs, items, ids, gather_sems, after)
        got = _gather_d2d(f"gather_d2d_{part}_{l}", got)
        B = {_BIG[items[i][0]]: g for i, g in zip(ids, got)}
        if part == "ffn":
            return dict(w_ffn_in=B["w_ffn_in"], w_ffn_out=B["w_ffn_out"].reshape(-1, D))
        W = {}
        if l < NA:
            W.update(w_rec_in=B["w_rec_in"],
                     w_gates=B["w_lru_gates"].reshape(N_CHIPS, NBLK, BW, GS).transpose(1, 2, 0, 3).reshape(
                         NBLK, BW, N_CHIPS * GS),
                     b_gates=b_gates[l], w_rec_out=B["w_rec_out"].reshape(C, D),
                     conv_w=small_a[l, :4], conv_b=small_a[l, 4:5], lru_param=small_a[l, 5:6])
        else:
            W.update(w_q=B["w_q"].reshape(D, D), w_o=B["w_o"].reshape(D, D))
            if l == NA:
                w_kvf_full = B["w_kvf"].transpose(1, 0, 2).reshape(D, -1)
                W.update(norm_kv=norm_kv[None], w_kv=w_kvf_full[:, :2 * D],
                         w_f=_pad_lanes(w_kvf_full[:, 2 * D:], LANES), b_f=_pad_lanes(b_forget[None], LANES))
        return W

    G_small = {l: {} for l in range(L)}
    pending = {}

    def layer_grads(l, part, G):
        G_small[l].update(G)
        by_name = dict(
            w_ffn_in=lambda: G["w_ffn_in"], w_ffn_out=lambda: G["w_ffn_out"].reshape(N_CHIPS, -1, D),
            w_rec_in=lambda: G["w_rec_in"],
            w_lru_gates=lambda: G["w_gates"].reshape(NBLK, BW, N_CHIPS, GS).transpose(2, 0, 1, 3).reshape(
                N_CHIPS, NBLK * BW, GS),
            w_rec_out=lambda: G["w_rec_out"].reshape(N_CHIPS, -1, D),
            w_kvf=lambda: jnp.concatenate([G["w_kv"].astype(F32), G["w_f"][:, :H]], axis=1).reshape(
                D, N_CHIPS, -1).transpose(1, 0, 2).astype(BF16),
            w_q=lambda: G["w_q"].reshape(N_CHIPS, -1, D), w_o=lambda: G["w_o"].reshape(N_CHIPS, -1, D))
        its = stage_items(l, part)
        grads = [by_name[_BIG[w]]() for w, _ in its]
        others = _reduce_d2d(f"reduce_d2d_{part}_{l}", grads)
        parts = [_sum_cores(f"sum_cores_{l}_{_BIG[w]}", g, o, core_id) for (w, _), g, o in zip(its, grads, others)]
        recv, sems, token = _scatter_start(f"scatter_start_{part}_{l}", parts)
        pending[(l, part)] = (parts, recv, sems)
        return token

    gains = dict(mix=[norm_mix[l][None] for l in range(L)], ffn=[norm_ffn[l][None] for l in range(L)],
                 final=norm_final[None])
    loss_row, grad_x, dg_final = _local_step(x.reshape(S, D), loss_target.reshape(S, D), gains,
                                             layer_weights, layer_grads)

    rows = [*[G_small[l]["norm_mix"] for l in range(L)], *[G_small[l]["norm_ffn"] for l in range(L)],
            G_small[NA]["norm_kv"], dg_final, _pad_lanes(G_small[NA]["b_f"], D), _pad_lanes(loss_row, D)]
    for a in range(NA):
        rows += [G_small[a][n] for n in ("conv_w", "conv_b", "b_gi", "b_gr", "lru_param")]
    packed = jnp.concatenate(rows, axis=0)
    tot = _sum_slots("sum_small", _gather_all("gather_small", packed))
    loss = tot[2 * L + 3, 0]
    g_rep = jnp.concatenate([tot[:2 * L + 2], tot[2 * L + 2:2 * L + 3]], axis=0)
    base = 2 * L + 4
    g_sh = []
    for a in range(NA):
        blk = lax.dynamic_slice_in_dim(tot[base + 8 * a:base + 8 * a + 8], chip * CS, CS, axis=1)
        gi = tot[base + 8 * a + 5].reshape(NBLK, BW)
        gr = tot[base + 8 * a + 6].reshape(NBLK, BW)
        bl = lax.dynamic_slice_in_dim(jnp.concatenate([gi, gr], axis=1), chip * GS, GS, axis=1)
        g_sh += [blk[:5], bl.reshape(-1, CS), blk[7:8]]
    g_sh = jnp.concatenate(g_sh, axis=0)
    nrow = g_sh.shape[0] // NA

    def pack_rep(T):
        return jnp.concatenate([T["norm_mix"], T["norm_ffn"], T["norm_kv"][None], T["norm_final"][None],
                                _pad_lanes(T["b_forget"][None], D)], axis=0)

    def pack_sh(T):
        return jnp.concatenate([jnp.concatenate([T["conv_w"][a], T["conv_b"][a][None],
                                                 T["b_lru_gates"][a].reshape(-1, CS), T["lru_param"][a][None]], axis=0)
                                for a in range(NA)], axis=0)

    rep = _adamw("adamw_replicated", [g_rep], pack_rep(P), pack_rep(M1), pack_rep(M2))
    shd = _adamw("adamw_small_sharded", [g_sh], pack_sh(P), pack_sh(M1), pack_sh(M2))

    def unpack_rep(t):
        return dict(norm_mix=t[:L], norm_ffn=t[L:2 * L], norm_kv=t[2 * L], norm_final=t[2 * L + 1],
                    b_forget=t[2 * L + 2, :H])

    def unpack_sh(t):
        t = t.reshape(NA, nrow, CS)
        return dict(conv_w=t[:, :4], conv_b=t[:, 4], b_lru_gates=t[:, 5:nrow - 1].reshape(NA, NBLK, GS),
                    lru_param=t[:, nrow - 1])

    full = [lax.empty(sh.shape, F32) for sh in shards]
    for l, part in reversed(stages):
        parts, recv, sems = pending[(l, part)]
        recv = _scatter_wait(f"scatter_wait_{part}_{l}", parts, recv, sems)
        for (w, li), own, r in zip(stage_items(l, part), parts, recv):
            full[w] = _sum_chips(f"sum_chips_{l}_{_BIG[w]}", r, own, full[w], li, me_core)
    full = _share_d2d("share_d2d", full)
    big = {w: _adamw(f"adamw_{w}", [g.reshape(P[w].shape)], P[w], M1[w], M2[w]) for w, g in zip(_BIG, full)}

    outs = []
    for i in range(4):
        small = {**unpack_rep(rep[i]), **unpack_sh(shd[i])}
        outs.append([big[w][i] if w in big else small[w] for w in _WEIGHTS])
    return (loss, grad_x.reshape(1, S, D), *outs[0], *outs[1], *outs[2], *outs[3])
```

```python
import functools
import math

import jax
import jax.numpy as jnp
from jax import lax
from jax.experimental import pallas as pl
from jax.experimental.pallas import tpu as pltpu

F32 = jnp.float32
BF16 = jnp.bfloat16

EPS = 1e-6
LRU_C = 8.0
HEAD_DIM = 64
LANES = 128
SUBLANES = 8
VMEM_LIMIT = 48 * 1024 * 1024
N_CHIPS = 4
N_DEV = 8

ADAM_LR = 0.001
ADAM_B1 = 0.9
ADAM_B2 = 0.999
ADAM_EPS = 1e-08
ADAM_WD = 0.01
ADAM_STEP = 10

_NN = (((1,), (0,)), ((), ()))
_NT = (((1,), (1,)), ((), ()))
_TN = (((0,), (0,)), ((), ()))
_DN = {"nn": _NN, "nt": _NT, "tn": _TN}
MESH = pl.DeviceIdType.MESH


def _hbm_out(shape, dtype):
    return pltpu.HBM(shape, dtype)


def _params(sem):
    return pltpu.CompilerParams(dimension_semantics=sem, vmem_limit_bytes=VMEM_LIMIT)


def _tile(n, want):
    if n <= want:
        return n
    t = (want // LANES) * LANES
    while t >= LANES:
        if n % t == 0:
            return t
        t -= LANES
    return n


def _sigmoid(x):
    return 1.0 / (1.0 + jnp.exp(-x))


def _sigmoid_t(x):
    return 0.5 * jnp.tanh(0.5 * x) + 0.5


def _softplus(x):
    return jnp.maximum(x, 0.0) + jnp.log(1.0 + jnp.exp(-jnp.abs(x)))


_GELU_C = math.sqrt(2.0 / math.pi)


def _gelu_and_grad(x):
    inner = _GELU_C * (x + 0.044715 * x * x * x)
    t = jnp.tanh(inner)
    g = 0.5 * x * (1.0 + t)
    dg = 0.5 * (1.0 + t) + 0.5 * x * (1.0 - t * t) * _GELU_C * (1.0 + 3.0 * 0.044715 * x * x)
    return g, dg


def _rms(x):
    return lax.rsqrt(jnp.mean(x * x, axis=-1, keepdims=True) + EPS)


def _rms_bwd(dy, x, g):
    r = _rms(x)
    xr = x * r
    dyg = dy * g
    return r * dyg - xr * (r * jnp.mean(dyg * xr, axis=-1, keepdims=True)), jnp.sum(dy * xr, axis=0, keepdims=True)


def _mm(name, mode, a, b, *, grid, a_spec, b_spec, out_shape, out_dtype, out_spec, nk=1,
        res=None, res_spec=None, bias=None, bias_spec=None, scale=None, norm_gain=None, norm_bwd=None):
    dn = _DN[mode]
    has_res, has_bias = res is not None, bias is not None
    blk = tuple(d for d in out_spec.block_shape if d is not None)
    vec = pl.BlockSpec((1, blk[-1]), lambda *g: (0, 0))
    a_specs = a_spec if isinstance(a_spec, list) else [a_spec]
    b_specs = b_spec if isinstance(b_spec, list) else [b_spec]
    npair = len(a_specs)
    n_in = 2 * npair + int(has_res) + int(has_bias) + (1 if norm_gain is not None else 0) + (3 if norm_bwd else 0)

    def body(*refs):
        p = 2 * npair
        r_ref = refs[p] if has_res else None
        p += int(has_res)
        bias_ref = refs[p] if has_bias else None
        p += int(has_bias)
        extra = refs[p:n_in]
        outs = refs[n_in:]
        o_ref = outs[0]
        part = lax.dot_general(refs[0][...], refs[npair][...], dn, preferred_element_type=F32)
        for t in range(1, npair):
            part = part + lax.dot_general(refs[t][...], refs[npair + t][...], dn, preferred_element_type=F32)

        def finish(acc):
            if scale is not None:
                acc = acc * scale
            if has_bias:
                acc = acc + bias_ref[...]
            if has_res:
                acc = r_ref[...] + acc
            if norm_bwd:
                h_ref, g_ref, dh_ref = extra
                dx, dg = _rms_bwd(acc, h_ref[...], g_ref[...])
                acc = dh_ref[...] + dx
                outs[1][...] = acc.astype(BF16)
                outs[2][...] = dg
            if norm_gain is not None:
                outs[1][...] = (acc * _rms(acc) * extra[0][...]).astype(BF16)
            o_ref[...] = acc.astype(o_ref.dtype)

        if nk == 1:
            finish(part)
        else:
            acc_ref = refs[-1]
            k = pl.program_id(2)

            @pl.when(k == 0)
            def _():
                acc_ref[...] = part

            @pl.when(k > 0)
            def _():
                acc_ref[...] += part

            @pl.when(k == nk - 1)
            def _():
                finish(acc_ref[...])

    ins, specs = [a] * npair + [b] * npair, a_specs + b_specs
    if has_res:
        ins.append(res)
        specs.append(res_spec)
    if has_bias:
        ins.append(bias)
        specs.append(bias_spec)
    out_specs, out_shapes = [out_spec], [_hbm_out(out_shape, out_dtype)]
    if norm_gain is not None:
        ins.append(norm_gain)
        specs.append(vec)
        out_specs.append(out_spec)
        out_shapes.append(_hbm_out(out_shape, BF16))
    if norm_bwd:
        h, g, dh = norm_bwd
        ins += [h, g, dh]
        specs += [out_spec, vec, out_spec]
        out_specs += [out_spec, pl.BlockSpec((None, 1, blk[-1]), lambda i, *rest: (i, 0, 0))]
        out_shapes += [_hbm_out(out_shape, BF16), _hbm_out((grid[0], 1, blk[-1]), F32)]
    sem = ("parallel", "parallel") + (("arbitrary",) if len(grid) == 3 else ())
    single = len(out_specs) == 1
    return pl.pallas_call(
        body, name=name, grid=grid, in_specs=specs, out_specs=out_specs[0] if single else out_specs,
        out_shape=out_shapes[0] if single else out_shapes,
        scratch_shapes=[pltpu.VMEM(blk, F32)] if nk > 1 else [],
        compiler_params=_params(sem),
    )(*ins)


def _mm_nn(name, a, b, *, b_lead=(), out_dtype, tm=512, tn=512, res=None, bias=None, scale=None, norm_gain=None):
    M, K = a.shape
    N = b.shape[-1]
    tm, tn = _tile(M, tm), _tile(N, tn)
    nl = len(b_lead)
    return _mm(
        name, "nn", a, b, grid=(M // tm, N // tn),
        a_spec=pl.BlockSpec((tm, K), lambda i, j: (i, 0)),
        b_spec=pl.BlockSpec((None,) * nl + (K, tn), lambda i, j: tuple(b_lead) + (0, j)),
        out_shape=(M, N), out_dtype=out_dtype, out_spec=pl.BlockSpec((tm, tn), lambda i, j: (i, j)),
        res=res, res_spec=pl.BlockSpec((tm, tn), lambda i, j: (i, j)),
        bias=bias, bias_spec=pl.BlockSpec((1, tn), lambda i, j: (0, j)), scale=scale, norm_gain=norm_gain)


def _mm_nt(name, a, b, *, b_lead=(), out_dtype, tm=512, tn=512, tk=2048, res=None, norm_bwd=None):
    M, K = a.shape
    N = b.shape[-2]
    tm, tn, tk = _tile(M, tm), _tile(N, tn), _tile(K, tk)
    nk = K // tk
    nl = len(b_lead)
    return _mm(
        name, "nt", a, b, grid=(M // tm, N // tn, nk), nk=nk,
        a_spec=pl.BlockSpec((tm, tk), lambda i, j, k: (i, k)),
        b_spec=pl.BlockSpec((None,) * nl + (tn, tk), lambda i, j, k: tuple(b_lead) + (j, k)),
        out_shape=(M, N), out_dtype=out_dtype, out_spec=pl.BlockSpec((tm, tn), lambda i, j, k: (i, j)),
        res=res, res_spec=pl.BlockSpec((tm, tn), lambda i, j, k: (i, j)), norm_bwd=norm_bwd)


def _mm_tn(name, a, b, *, out_dtype, tm=512, tn=512):
    S, M = a.shape
    N = b.shape[1]
    tm, tn = _tile(M, tm), _tile(N, tn)
    return _mm(
        name, "tn", a, b, grid=(M // tm, N // tn),
        a_spec=pl.BlockSpec((S, tm), lambda i, j: (0, i)),
        b_spec=pl.BlockSpec((S, tn), lambda i, j: (0, j)),
        out_shape=(M, N), out_dtype=out_dtype, out_spec=pl.BlockSpec((tm, tn), lambda i, j: (i, j)))


def _rmsnorm_fwd(name, h, g, tr=256):
    S, D = h.shape
    tr = _tile(S, tr)

    def body(h_ref, g_ref, o_ref):
        x = h_ref[...]
        r = lax.rsqrt(jnp.mean(x * x, axis=-1, keepdims=True) + EPS)
        o_ref[...] = (x * r * g_ref[...]).astype(o_ref.dtype)

    return pl.pallas_call(
        body, name=name, grid=(S // tr,),
        in_specs=[pl.BlockSpec((tr, D), lambda i: (i, 0)), pl.BlockSpec((1, D), lambda i: (0, 0))],
        out_specs=pl.BlockSpec((tr, D), lambda i: (i, 0)),
        out_shape=_hbm_out((S, D), BF16),
        compiler_params=_params(("parallel",)),
    )(h, g)


def _loss_head(name, h, target, g, tr=256):
    S, D = h.shape
    tr = _tile(S, tr)

    def body(h_ref, t_ref, g_ref, o_ref, ob_ref, dg_ref, loss_ref):
        i = pl.program_id(0)
        x = h_ref[...]
        gg = g_ref[...]
        r = lax.rsqrt(jnp.mean(x * x, axis=-1, keepdims=True) + EPS)
        xr = x * r
        err = xr * gg - t_ref[...]
        lpart = 0.5 * jnp.sum(jnp.mean(err * err, axis=-1, keepdims=True), axis=0, keepdims=True)
        dy = err * (1.0 / D)
        dyg = dy * gg
        dx = r * dyg - xr * (r * jnp.mean(dyg * xr, axis=-1, keepdims=True))
        o_ref[...] = dx
        ob_ref[...] = dx.astype(BF16)
        part = jnp.sum(dy * xr, axis=0, keepdims=True)
        lrow = jnp.broadcast_to(lpart, (1, LANES))

        @pl.when(i == 0)
        def _():
            dg_ref[...] = part
            loss_ref[...] = lrow

        @pl.when(i > 0)
        def _():
            dg_ref[...] += part
            loss_ref[...] += lrow

    row = pl.BlockSpec((tr, D), lambda i: (i, 0))
    vec = pl.BlockSpec((1, D), lambda i: (0, 0))
    return pl.pallas_call(
        body, name=name, grid=(S // tr,),
        in_specs=[row, row, vec], out_specs=[row, row, vec, pl.BlockSpec((1, LANES), lambda i: (0, 0))],
        out_shape=[_hbm_out((S, D), F32), _hbm_out((S, D), BF16),
                   _hbm_out((1, D), F32), _hbm_out((1, LANES), F32)],
        compiler_params=_params(("arbitrary",)),
    )(h, target, g)


def _swiglu_fwd(name, hn, w_in, tm=512):
    S, D = hn.shape
    FH = w_in.shape[-1]
    tm = _tile(S, tm)

    def body(x_ref, wg_ref, wu_ref, z_ref, a_ref):
        x = x_ref[...]
        zg = jnp.dot(x, wg_ref[...], preferred_element_type=F32)
        zu = jnp.dot(x, wu_ref[...], preferred_element_type=F32)
        z_ref[0] = zg.astype(z_ref.dtype)
        z_ref[1] = zu.astype(z_ref.dtype)
        a_ref[...] = (zg * _sigmoid_t(zg) * zu).astype(a_ref.dtype)

    return pl.pallas_call(
        body, name=name, grid=(S // tm, 2),
        in_specs=[pl.BlockSpec((tm, D), lambda i, j: (i, 0)),
                  pl.BlockSpec((None, D, FH), lambda i, j: (j, 0, 0)),
                  pl.BlockSpec((None, D, FH), lambda i, j: (j + 2, 0, 0))],
        out_specs=[pl.BlockSpec((2, tm, FH), lambda i, j: (0, i, j)), pl.BlockSpec((tm, FH), lambda i, j: (i, j))],
        out_shape=[_hbm_out((2, S, 2 * FH), BF16), _hbm_out((S, 2 * FH), BF16)],
        compiler_params=_params(("parallel", "parallel")),
    )(hn, w_in, w_in)


def _swiglu_bwd(name, dhb, w_out, z3, tm=512):
    S, D = dhb.shape
    F = w_out.shape[0]
    FH = F // 2
    tm = _tile(S, tm)

    def body(d_ref, w_ref, z_ref, dz_ref):
        d = lax.dot_general(d_ref[...], w_ref[...], _NT, preferred_element_type=F32)
        zg = z_ref[0].astype(F32)
        zu = z_ref[1].astype(F32)
        sg = _sigmoid_t(zg)
        dz_ref[0] = (d * zu * (sg * (1.0 + zg * (1.0 - sg)))).astype(dz_ref.dtype)
        dz_ref[1] = (d * (zg * sg)).astype(dz_ref.dtype)

    zspec = pl.BlockSpec((2, tm, FH), lambda i, j: (0, i, j))
    return pl.pallas_call(
        body, name=name, grid=(S // tm, 2),
        in_specs=[pl.BlockSpec((tm, D), lambda i, j: (i, 0)), pl.BlockSpec((FH, D), lambda i, j: (j, 0)), zspec],
        out_specs=zspec, out_shape=_hbm_out((2, S, F), BF16),
        compiler_params=_params(("parallel", "parallel")),
    )(dhb, w_out, z3)


SCAN_ROWS = 64


def _group_scan(A, B, reverse):
    n = A.shape[0]
    sub = lax.broadcasted_iota(jnp.int32, A.shape, 0) % SUBLANES
    for d in (1, 2, 4):
        if reverse:
            A_sh, B_sh = pltpu.roll(A, n - d, 0), pltpu.roll(B, n - d, 0)
            keep = sub < SUBLANES - d
        else:
            A_sh, B_sh = pltpu.roll(A, d, 0), pltpu.roll(B, d, 0)
            keep = sub >= d
        B = jnp.where(keep, A * B_sh + B, B)
        A = jnp.where(keep, A * A_sh, A)
    return A, B


def _block_scan(a, u, carry, reverse):
    A, B = _group_scan(a, u, reverse)
    ng = a.shape[0] // SUBLANES
    out = [None] * ng
    order = range(ng - 1, -1, -1) if reverse else range(ng)
    for gi in order:
        sl = slice(gi * SUBLANES, (gi + 1) * SUBLANES)
        hg = A[sl] * carry + B[sl]
        out[gi] = hg
        carry = hg[0:1] if reverse else hg[SUBLANES - 1:SUBLANES]
    return jnp.concatenate(out, axis=0), carry


def _lru_gates(rc, gip, grp, sp):
    gi = _sigmoid(gip)
    gr = _sigmoid(grp)
    la = -LRU_C * gr * sp
    a = jnp.exp(la)
    om = -jnp.tanh(la) * (a * a + 1.0)
    mult = jnp.sqrt(om)
    return gi, gr, a, mult


def _lru_fwd(name, proj, rc, gip, grp, lru_p, tc=256):
    S, C = rc.shape
    tc = _tile(C, tc)
    nb = S // SCAN_ROWS

    def body(gb_ref, rc_ref, gi_ref, gr_ref, l_ref, h_ref, m_ref):
        sp = _softplus(-l_ref[...])

        def step(b, carry):
            rows = pl.ds(pl.multiple_of(b * SCAN_ROWS, SCAN_ROWS), SCAN_ROWS)
            rcb = rc_ref[rows, :]
            gi, _, a, mult = _lru_gates(rcb, gi_ref[rows, :], gr_ref[rows, :], sp)
            h, carry = _block_scan(a, rcb * gi * mult, carry, False)
            h_ref[rows, :] = h
            gel, _ = _gelu_and_grad(gb_ref[rows, :])
            m_ref[rows, :] = (gel * h).astype(m_ref.dtype)
            return carry

        lax.fori_loop(0, nb, step, jnp.zeros((1, tc), F32))

    col = pl.BlockSpec((S, tc), lambda j: (0, j))
    return pl.pallas_call(
        body, name=name, grid=(C // tc,),
        in_specs=[col, col, col, col, pl.BlockSpec((1, tc), lambda j: (0, j))],
        out_specs=[col, col],
        out_shape=[_hbm_out((S, C), F32), _hbm_out((S, C), BF16)],
        compiler_params=_params(("parallel",)),
    )(proj, rc, gip, grp, lru_p)


def _lru_bwd(name, dm, proj, hrec, rc, gip, grp, lru_p, tc=256):
    S, C = rc.shape
    tc = _tile(C, tc)
    nb = S // SCAN_ROWS
    R = SCAN_ROWS

    def body(dm_ref, gb_ref, h_ref, rc_ref, gi_ref, gr_ref, l_ref,
             dgb_ref, dgi_ref, dgr_ref, drc_ref, dbi_ref, dbr_ref, dl_ref):
        lp = l_ref[...]
        sp = _softplus(-lp)
        row = lax.broadcasted_iota(jnp.int32, (R, tc), 0)
        zero = jnp.zeros((1, tc), F32)

        def step(t, carry):
            mu_in, s_i, s_r, s_sp = carry
            b = nb - 1 - t
            r0 = pl.multiple_of(b * R, R)
            rows = pl.ds(r0, R)
            rcb = rc_ref[rows, :]
            gi, gr, a, mult = _lru_gates(rcb, gi_ref[rows, :], gr_ref[rows, :], sp)
            gel, dgel = _gelu_and_grad(gb_ref[rows, :])
            dmb = dm_ref[rows, :]
            h = h_ref[rows, :]
            dgb_ref[rows, :] = (dmb * h * dgel).astype(dgb_ref.dtype)
            dh = dmb * gel
            mu, mu_out = _block_scan(a, a * dh, mu_in, True)
            mu_next = jnp.where(row == R - 1, mu_in, pltpu.roll(mu, R - 1, 0))
            lam = dh + mu_next
            p0 = pl.multiple_of(jnp.maximum(r0 - SUBLANES, 0), SUBLANES)
            prev = h_ref[pl.ds(p0, SUBLANES), :][SUBLANES - 1:SUBLANES]
            prev = jnp.where(b > 0, prev, 0.0)
            h_prev = jnp.where(row == 0, prev, pltpu.roll(h, 1, 0))
            da = lam * h_prev
            d_mult = lam * rcb * gi
            d_la = da * a - d_mult * (a * a) / mult
            d_grp = d_la * (-LRU_C * sp) * gr * (1.0 - gr)
            d_gip = lam * rcb * mult * gi * (1.0 - gi)
            dgr_ref[rows, :] = d_grp.astype(dgr_ref.dtype)
            dgi_ref[rows, :] = d_gip.astype(dgi_ref.dtype)
            drc_ref[rows, :] = lam * gi * mult
            s_i = s_i + jnp.sum(d_gip, axis=0, keepdims=True)
            s_r = s_r + jnp.sum(d_grp, axis=0, keepdims=True)
            s_sp = s_sp + jnp.sum(d_la * gr, axis=0, keepdims=True)
            return mu_out, s_i, s_r, s_sp

        _, s_i, s_r, s_sp = lax.fori_loop(0, nb, step, (zero, zero, zero, zero))
        dbi_ref[...] = s_i
        dbr_ref[...] = s_r
        dl_ref[...] = (-LRU_C * s_sp) * (-_sigmoid(-lp))

    col = pl.BlockSpec((S, tc), lambda j: (0, j))
    vec = pl.BlockSpec((1, tc), lambda j: (0, j))
    return pl.pallas_call(
        body, name=name, grid=(C // tc,),
        in_specs=[col, col, col, col, col, col, vec],
        out_specs=[col, col, col, col, vec, vec, vec],
        out_shape=[_hbm_out((S, C), BF16), _hbm_out((S, C), BF16),
                   _hbm_out((S, C), BF16), _hbm_out((S, C), F32),
                   _hbm_out((1, C), F32), _hbm_out((1, C), F32),
                   _hbm_out((1, C), F32)],
        compiler_params=_params(("parallel",)),
    )(dm, proj, hrec, rc, gip, grp, lru_p)


def _cumsum_rows(name, u, reverse):
    S, C = u.shape
    nb = S // SCAN_ROWS

    def body(u_ref, o_ref):
        def step(t, carry):
            b = nb - 1 - t if reverse else t
            rows = pl.ds(pl.multiple_of(b * SCAN_ROWS, SCAN_ROWS), SCAN_ROWS)
            ub = u_ref[rows, :]
            h, carry = _block_scan(jnp.ones_like(ub), ub, carry, reverse)
            o_ref[rows, :] = h
            return carry

        lax.fori_loop(0, nb, step, jnp.zeros((1, C), F32))

    spec = pl.BlockSpec((S, C), lambda i: (0, 0))
    return pl.pallas_call(
        body, name=name, grid=(1,), in_specs=[spec], out_specs=spec,
        out_shape=_hbm_out((S, C), F32),
        compiler_params=_params(("arbitrary",)),
    )(u)


def _shift_down(x, k):
    row = lax.broadcasted_iota(jnp.int32, x.shape, 0)
    return jnp.where(row >= k, pltpu.roll(x, k, 0), 0.0)


def _shift_up(x, k):
    n = x.shape[0]
    row = lax.broadcasted_iota(jnp.int32, x.shape, 0)
    return jnp.where(row < n - k, pltpu.roll(x, n - k, 0), 0.0)


def _conv_fwd(name, proj, w, b, tc=256):
    S, C2 = proj.shape
    C = C2 // 2
    tc = _tile(C, tc)
    off = C // tc

    def body(x_ref, w_ref, b_ref, o_ref, ob_ref):
        x = x_ref[...]
        out = b_ref[...] + w_ref[3:4, :] * x
        for k in (1, 2, 3):
            out = out + w_ref[3 - k:4 - k, :] * _shift_down(x, k)
        o_ref[...] = out
        ob_ref[...] = out.astype(BF16)

    col = pl.BlockSpec((S, tc), lambda j: (0, j))
    return pl.pallas_call(
        body, name=name, grid=(C // tc,),
        in_specs=[pl.BlockSpec((S, tc), lambda j: (0, off + j)),
                  pl.BlockSpec((4, tc), lambda j: (0, j)), pl.BlockSpec((1, tc), lambda j: (0, j))],
        out_specs=[col, col],
        out_shape=[_hbm_out((S, C), F32), _hbm_out((S, C), BF16)],
        compiler_params=_params(("parallel",)),
    )(proj, w, b)


def _conv_bwd(name, drc, proj, w, tc=256):
    S, C = drc.shape
    tc = _tile(C, tc)
    off = C // tc

    def body(y_ref, x_ref, w_ref, dx_ref, dw_ref, db_ref):
        y = y_ref[...]
        x = x_ref[...]
        dx = w_ref[3:4, :] * y
        dw_ref[3:4, :] = jnp.sum(y * x, axis=0, keepdims=True)
        for k in (1, 2, 3):
            dx = dx + w_ref[3 - k:4 - k, :] * _shift_up(y, k)
            dw_ref[3 - k:4 - k, :] = jnp.sum(y * _shift_down(x, k), axis=0, keepdims=True)
        dx_ref[...] = dx.astype(dx_ref.dtype)
        db_ref[...] = jnp.sum(y, axis=0, keepdims=True)

    col = pl.BlockSpec((S, tc), lambda j: (0, j))
    return pl.pallas_call(
        body, name=name, grid=(C // tc,),
        in_specs=[col, pl.BlockSpec((S, tc), lambda j: (0, off + j)), pl.BlockSpec((4, tc), lambda j: (0, j))],
        out_specs=[col, pl.BlockSpec((4, tc), lambda j: (0, j)), pl.BlockSpec((1, tc), lambda j: (0, j))],
        out_shape=[_hbm_out((S, C), BF16), _hbm_out((4, C), F32),
                   _hbm_out((1, C), F32)],
        compiler_params=_params(("parallel",)),
    )(drc, proj, w)


def _gates_fwd(name, rcb, wg, bg):
    S, C = rcb.shape
    nblk, bw, _ = wg.shape

    def body(x_ref, w_ref, b_ref, gi_ref, gr_ref):
        g = jnp.dot(x_ref[...], w_ref[...], preferred_element_type=F32) + b_ref[...]
        gi_ref[...] = g[:, :bw]
        gr_ref[...] = g[:, bw:]

    col = pl.BlockSpec((S, bw), lambda n: (0, n))
    return pl.pallas_call(
        body, name=name, grid=(nblk,),
        in_specs=[col, pl.BlockSpec((None, bw, 2 * bw), lambda n: (n, 0, 0)),
                  pl.BlockSpec((None, 1, 2 * bw), lambda n: (n, 0, 0))],
        out_specs=[col, col],
        out_shape=[_hbm_out((S, C), F32), _hbm_out((S, C), F32)],
        compiler_params=_params(("parallel",)),
    )(rcb, wg, bg)


def _gates_bwd(name, dgi, dgr, rcb, wg, drc1):
    S, C = rcb.shape
    nblk, bw, _ = wg.shape

    def body(dgi_ref, dgr_ref, x_ref, w_ref, d1_ref, drc_ref, dw_ref):
        w = w_ref[...]
        x = x_ref[...]
        di, dr = dgi_ref[...], dgr_ref[...]
        drc_ref[...] = (d1_ref[...]
                        + lax.dot_general(di, w[:, :bw], _NT, preferred_element_type=F32)
                        + lax.dot_general(dr, w[:, bw:], _NT, preferred_element_type=F32))
        dw_ref[:, :bw] = lax.dot_general(x, di, _TN, preferred_element_type=F32).astype(dw_ref.dtype)
        dw_ref[:, bw:] = lax.dot_general(x, dr, _TN, preferred_element_type=F32).astype(dw_ref.dtype)

    col = pl.BlockSpec((S, bw), lambda n: (0, n))
    wspec = pl.BlockSpec((None, bw, 2 * bw), lambda n: (n, 0, 0))
    return pl.pallas_call(
        body, name=name, grid=(nblk,),
        in_specs=[col, col, col, wspec, col], out_specs=[col, wspec],
        out_shape=[_hbm_out((S, C), F32), _hbm_out((nblk, bw, 2 * bw), BF16)],
        compiler_params=_params(("parallel",)),
    )(dgi, dgr, rcb, wg, drc1)


def _att_tile(S):
    return next(t for t in (512, 256, 128) if S % t == 0)


def _head_lanes(shape):
    return lax.broadcasted_iota(jnp.int32, shape, len(shape) - 1) < HEAD_DIM


def _key_bias(c_blk):
    first = _head_lanes(c_blk.shape)
    rolled = pltpu.roll(c_blk, HEAD_DIM, 1)
    return jnp.where(first, c_blk, rolled), jnp.where(first, rolled, c_blk)


def _over_keys(x, op):
    n = x.shape[0]
    while n > SUBLANES:
        n //= 2
        x = op(x[:n], x[n:2 * n])
    return (jnp.max if op is jnp.maximum else jnp.sum)(x, axis=0, keepdims=True)


def _causal_t(T, cc):
    r = lax.broadcasted_iota(jnp.int32, (T, LANES), 0)
    c = lax.broadcasted_iota(jnp.int32, (T, LANES), 1) + cc * LANES
    return r <= c


def _attn_fwd(name, q, kv, cfull):
    S, D = q.shape
    HP = D // LANES
    T = _att_tile(S)
    nq = S // T
    NC = T // LANES

    def body(q_ref, k_ref, v_ref, c_ref, o_ref, of_ref, lse_ref, bias, vT, acc, m_scr, l_scr):
        def prologue(i, _):
            rows = pl.ds(pl.multiple_of(i * T, T), T)
            bias[0, rows, :], bias[1, rows, :] = _key_bias(c_ref[rows, :])
            vT[i] = v_ref[rows, :].astype(F32).T.astype(BF16)
            return 0

        lax.fori_loop(0, nq, prologue, 0)

        def q_step(qi, _):
            q0 = pl.multiple_of(qi * T, T)
            qb = q_ref[pl.ds(q0, T), :]
            m_scr[...] = jnp.full(m_scr.shape, -jnp.inf, F32)
            l_scr[...] = jnp.zeros(l_scr.shape, F32)
            acc[...] = jnp.zeros(acc.shape, F32)

            def tile(kj, masked):
                ks = pl.ds(pl.multiple_of(kj * T, T), T)
                kf = k_ref[ks, :].astype(F32)
                first = _head_lanes(kf.shape)
                kms = [jnp.where(first if hh == 0 else jnp.logical_not(first), kf, 0.0).astype(BF16) for hh in range(2)]
                sTs = [lax.dot_general(km, qb, _NT, preferred_element_type=F32) for km in kms]
                for hh in range(2):
                    b = bias[hh, ks, :]
                    ps = []
                    for cc in range(NC):
                        cols = slice(cc * LANES, (cc + 1) * LANES)
                        s = sTs[hh][:, cols] + b
                        if masked:
                            s = jnp.where(_causal_t(T, cc), s, -jnp.inf)
                        m_old = m_scr[hh, cc]
                        m_new = jnp.maximum(m_old, _over_keys(s, jnp.maximum))
                        alpha = jnp.exp(m_old - m_new)
                        p = jnp.exp(s - m_new)
                        l_scr[hh, cc] = alpha * l_scr[hh, cc] + _over_keys(p, jnp.add)
                        m_scr[hh, cc] = m_new
                        ps.append(p.astype(BF16))
                        acc[hh, :, cols] = acc[hh, :, cols] * alpha
                    acc[hh] += jnp.dot(vT[kj, hh * HEAD_DIM:(hh + 1) * HEAD_DIM, :], jnp.concatenate(ps, axis=1),
                                       preferred_element_type=F32)

            def inner(kj, _):
                tile(kj, False)
                return 0

            lax.fori_loop(0, qi, inner, 0)
            tile(qi, True)
            outs = []
            for hh in range(2):
                inv = jnp.concatenate([1.0 / l_scr[hh, cc] for cc in range(NC)], axis=1)
                outs.append(acc[hh] * inv)
                for cc in range(NC):
                    lse_ref[hh:hh + 1, pl.ds(q0 + cc * LANES, LANES)] = m_scr[hh, cc] + jnp.log(l_scr[hh, cc])
            out = jnp.concatenate(outs, axis=0).T
            o_ref[pl.ds(q0, T), :] = out.astype(o_ref.dtype)
            of_ref[pl.ds(q0, T), :] = out
            return 0

        lax.fori_loop(0, nq, q_step, 0)

    blk = lambda off: pl.BlockSpec((S, LANES), lambda p: (0, off + p))
    return pl.pallas_call(
        body, name=name, grid=(HP,),
        in_specs=[blk(0), blk(0), blk(HP), blk(0)],
        out_specs=[blk(0), blk(0), pl.BlockSpec((None, 2, S), lambda p: (p, 0, 0))],
        out_shape=[_hbm_out((S, D), BF16), _hbm_out((S, D), F32),
                   _hbm_out((HP, 2, S), F32)],
        scratch_shapes=[pltpu.VMEM((2, S, LANES), F32), pltpu.VMEM((nq, LANES, T), BF16),
                        pltpu.VMEM((2, HEAD_DIM, T), F32), pltpu.VMEM((2, NC, 1, LANES), F32),
                        pltpu.VMEM((2, NC, 1, LANES), F32)],
        compiler_params=_params(("parallel",)),
    )(q, kv, kv, cfull)


def _attn_bwd(name, q, kv, cfull, of, do, lse3):
    S, D = q.shape
    HP = D // LANES
    T = _att_tile(S)
    nq = S // T
    NC = T // LANES
    scale = HEAD_DIM ** -0.5

    def body(q_ref, k_ref, v_ref, c_ref, of_ref, do_ref, lse_ref,
             dq_ref, dk_ref, dv_ref, dck_ref, drq_ref, bias, kT, dqT, delta, dr_scr):
        def prologue(i, _):
            rows = pl.ds(pl.multiple_of(i * T, T), T)
            bias[0, rows, :], bias[1, rows, :] = _key_bias(c_ref[rows, :])
            kT[i] = k_ref[rows, :].astype(F32).T.astype(BF16)
            prodT = (do_ref[rows, :].astype(F32) * of_ref[rows, :]).T
            for hh in range(2):
                delta[hh:hh + 1, rows] = jnp.sum(prodT[hh * HEAD_DIM:(hh + 1) * HEAD_DIM], axis=0, keepdims=True)
            dqT[i] = jnp.zeros((LANES, T), F32)
            return 0

        lax.fori_loop(0, nq, prologue, 0)
        dr_scr[...] = jnp.zeros(dr_scr.shape, F32)

        def kv_step(kj, _):
            ks = pl.ds(pl.multiple_of(kj * T, T), T)
            kf = k_ref[ks, :].astype(F32)
            vf = v_ref[ks, :].astype(F32)
            first = _head_lanes(kf.shape)
            masks = [first, jnp.logical_not(first)]
            kms = [jnp.where(m, kf, 0.0).astype(BF16) for m in masks]
            vms = [jnp.where(m, vf, 0.0).astype(BF16) for m in masks]

            def tile(qi, carry, masked):
                q0 = pl.multiple_of(qi * T, T)
                qb = q_ref[pl.ds(q0, T), :]
                dob = do_ref[pl.ds(q0, T), :]
                sTs = [lax.dot_general(km, qb, _NT, preferred_element_type=F32) for km in kms]
                dpTs = [lax.dot_general(vm, dob, _NT, preferred_element_type=F32) for vm in vms]
                out = []
                for hh in range(2):
                    dk_a, dv_a, dc_a = carry[3 * hh:3 * hh + 3]
                    b = bias[hh, ks, :]
                    head = slice(hh * HEAD_DIM, (hh + 1) * HEAD_DIM)
                    ps, dss = [], []
                    for cc in range(NC):
                        cols = slice(cc * LANES, (cc + 1) * LANES)
                        at = pl.ds(q0 + cc * LANES, LANES)
                        p = jnp.exp(sTs[hh][:, cols] + b - lse_ref[hh:hh + 1, at])
                        if masked:
                            p = jnp.where(_causal_t(T, cc), p, 0.0)
                        ds = p * (dpTs[hh][:, cols] - delta[hh:hh + 1, at])
                        ps.append(p.astype(BF16))
                        dss.append(ds.astype(BF16))
                        dc_a = dc_a + ds
                        dr_scr[hh:hh + 1, at] += _over_keys(ds, jnp.add)
                    pT = jnp.concatenate(ps, axis=1)
                    dsT = jnp.concatenate(dss, axis=1)
                    dv_a = dv_a + jnp.dot(pT, dob, preferred_element_type=F32)
                    dk_a = dk_a + jnp.dot(dsT, qb, preferred_element_type=F32)
                    dqT[qi, head, :] += jnp.dot(kT[kj, head, :], dsT, preferred_element_type=F32)
                    out += [dk_a, dv_a, dc_a]
                return tuple(out)

            zero = jnp.zeros((T, LANES), F32)
            carry = tile(kj, (zero,) * 6, True)
            dk0, dv0, dc0, dk1, dv1, dc1 = lax.fori_loop(kj + 1, nq, lambda qi, c: tile(qi, c, False), carry)
            dk_ref[ks, :] = jnp.where(first, dk0, dk1)
            dv_ref[ks, :] = jnp.where(first, dv0, dv1)
            dck_ref[ks, :] = jnp.where(first, jnp.broadcast_to(-jnp.sum(dc0, axis=1, keepdims=True), (T, LANES)),
                                       jnp.broadcast_to(-jnp.sum(dc1, axis=1, keepdims=True), (T, LANES)))
            return 0

        lax.fori_loop(0, nq, kv_step, 0)

        def epilogue(i, _):
            rows = pl.ds(pl.multiple_of(i * T, T), T)
            dq_ref[rows, :] = (dqT[i].T * scale).astype(dq_ref.dtype)
            return 0

        lax.fori_loop(0, nq, epilogue, 0)
        drq_ref[...] = dr_scr[...]

    blk = lambda off: pl.BlockSpec((S, LANES), lambda p: (0, off + p))
    row_spec = pl.BlockSpec((None, 2, S), lambda p: (p, 0, 0))
    return pl.pallas_call(
        body, name=name, grid=(HP,),
        in_specs=[blk(0), blk(0), blk(HP), blk(0), blk(0), blk(0), row_spec],
        out_specs=[blk(0), blk(0), blk(0), blk(0), row_spec],
        out_shape=[_hbm_out((S, D), BF16), _hbm_out((S, D), F32),
                   _hbm_out((S, D), F32), _hbm_out((S, D), F32),
                   _hbm_out((HP, 2, S), F32)],
        scratch_shapes=[pltpu.VMEM((2, S, LANES), F32), pltpu.VMEM((nq, LANES, T), BF16),
                        pltpu.VMEM((nq, LANES, T), F32), pltpu.VMEM((2, S), F32), pltpu.VMEM((2, S), F32)],
        compiler_params=_params(("parallel",)),
    )(q, kv, kv, cfull, of, do, lse3)


def _logsig_fwd(name, f):
    S, C = f.shape

    def body(f_ref, o_ref):
        o_ref[...] = -_softplus(-f_ref[...])

    spec = pl.BlockSpec((S, C), lambda i: (0, 0))
    return pl.pallas_call(body, name=name, grid=(1,), in_specs=[spec], out_specs=spec,
                          out_shape=_hbm_out((S, C), F32),
                          compiler_params=_params(("arbitrary",)))(f)


def _logsig_bwd(name, dls, f):
    S, C = f.shape

    def body(d_ref, f_ref, o_ref, s_ref):
        df = d_ref[...] * _sigmoid(-f_ref[...])
        o_ref[...] = df.astype(o_ref.dtype)
        s_ref[...] = jnp.sum(df, axis=0, keepdims=True)

    spec = pl.BlockSpec((S, C), lambda i: (0, 0))
    return pl.pallas_call(body, name=name, grid=(1,), in_specs=[spec, spec],
                          out_specs=[spec, pl.BlockSpec((1, C), lambda i: (0, 0))],
                          out_shape=[_hbm_out((S, C), BF16), _hbm_out((1, C), F32)],
                          compiler_params=_params(("arbitrary",)))(dls, f)


def _add_cast(name, parts, out_dtype, tr=256):
    S, C = parts[0].shape
    tr = _tile(S, tr)
    n = len(parts)

    def body(*refs):
        acc = refs[0][...].astype(F32)
        for r in refs[1:n]:
            acc = acc + r[...].astype(F32)
        refs[n][...] = acc.astype(out_dtype)

    spec = pl.BlockSpec((tr, C), lambda i: (i, 0))
    return pl.pallas_call(body, name=name, grid=(S // tr,), in_specs=[spec] * n, out_specs=spec,
                          out_shape=_hbm_out((S, C), out_dtype),
                          compiler_params=_params(("parallel",)))(*parts)


def _local_step(x, target, gains, layer_weights, layer_grads):
    S, D = x.shape
    HP = D // LANES
    scale = HEAD_DIM ** -0.5
    tm = _tile(S, 512)
    tx = _tile(S, 256)
    td = _tile(D, 512)
    saved = []
    h = x
    l = 0
    kv = cfull = f_pre = hn_kv = h_kv = None
    while True:
        W = layer_weights(l, "mix", h)
        if W is None:
            break
        recurrent = "w_rec_in" in W
        if l == 0:
            xn = _rmsnorm_fwd("mix_norm_0", h, gains["mix"][0])
        if recurrent:
            CH = W["w_rec_in"].shape[-1]
            C = 2 * CH
            proj = _mm(f"rec_in_{l}", "nn", xn, W["w_rec_in"], grid=(S // tm, N_CHIPS),
                       a_spec=pl.BlockSpec((tm, D), lambda i, j: (i, 0)),
                       b_spec=pl.BlockSpec((None, D, CH), lambda i, j: (j, 0, 0)),
                       out_shape=(S, 2 * C), out_dtype=F32,
                       out_spec=pl.BlockSpec((tm, CH), lambda i, j: (i, j)))
            rc, rcb = _conv_fwd(f"conv_{l}", proj, W["conv_w"], W["conv_b"])
            gip, grp = _gates_fwd(f"gates_{l}", rcb, W["w_gates"], W["b_gates"])
            hrec, m = _lru_fwd(f"lru_{l}", proj, rc, gip, grp, W["lru_param"])
            h_mid, hn = _mm_nn(f"rec_out_{l}", m, W["w_rec_out"], out_dtype=F32, res=h, tn=D, norm_gain=gains["ffn"][l])
            mix_saved = (xn, proj, rc, rcb, gip, grp, hrec, m)
        else:
            if "w_kv" in W:
                h_kv = h
                hn_kv = _rmsnorm_fwd("kv_norm", h, W["norm_kv"])
                kv = _mm_nn("kv_proj", hn_kv, W["w_kv"], out_dtype=BF16)
                f_pre = _mm_nn("f_proj", hn_kv, W["w_f"], out_dtype=F32, bias=W["b_f"])
                c = _cumsum_rows("c_cumsum", _logsig_fwd("logsig", f_pre), False)
                cfull = jnp.repeat(-c[:, :2 * HP], HEAD_DIM, axis=1)
            q = _mm_nn(f"q_proj_{l}", xn, W["w_q"], out_dtype=BF16, scale=scale)
            o, of, lse = _attn_fwd(f"attn_fwd_{l}", q, kv, cfull)
            h_mid, hn = _mm_nn(f"o_proj_{l}", o, W["w_o"], out_dtype=F32, res=h, tn=D, norm_gain=gains["ffn"][l])
            mix_saved = (xn, q, o, of, lse)
        W = {**W, **layer_weights(l, "ffn", h_mid)}
        z3, act = _swiglu_fwd(f"ffn_in_{l}", hn, W["w_ffn_in"])
        saved.append((W, h, h_mid, mix_saved, (hn, z3, act)))
        l += 1
        if l < len(gains["mix"]):
            h, xn = _mm_nn(f"ffn_out_{l - 1}", act, W["w_ffn_out"], out_dtype=F32, res=h_mid, tn=D,
                           norm_gain=gains["mix"][l])
        else:
            h = _mm_nn(f"ffn_out_{l - 1}", act, W["w_ffn_out"], out_dtype=F32, res=h_mid, tn=D)

    dh, dhb, dg_final, loss_row = _loss_head("loss_head", h, target, gains["final"])

    dk_parts, dv_parts, dc_parts = [], [], []
    token = None
    for l in reversed(range(len(saved))):
        W, h_in, h_mid, mix_saved, (hn, z3, act) = saved[l]
        recurrent = "w_rec_in" in W
        FH = W["w_ffn_in"].shape[-1]
        G = {}
        norm_ffn = gains["ffn"][l]
        if token is not None:
            norm_ffn = norm_ffn + jnp.minimum(token[:1, :1], 0.0)
        G["w_ffn_out"] = _mm_tn(f"d_ffn_out_{l}", act, dhb, out_dtype=BF16, tn=D)
        dz3 = _swiglu_bwd(f"d_act_{l}", dhb, W["w_ffn_out"], z3)
        G["w_ffn_in"] = _mm(
            f"d_ffn_in_{l}", "tn", hn, dz3, grid=(D // td, N_CHIPS),
            a_spec=pl.BlockSpec((S, td), lambda i, j: (0, i)),
            b_spec=pl.BlockSpec((None, S, FH), lambda i, j: (j // 2, 0, j % 2)),
            out_shape=(N_CHIPS, D, FH), out_dtype=BF16,
            out_spec=pl.BlockSpec((None, td, FH), lambda i, j: (j, i, 0)))
        token = layer_grads(l, "ffn", G)
        G = {}
        norm_ffn = norm_ffn + jnp.minimum(token[:1, :1], 0.0)
        dh, dhb, dgp = _mm(f"d_ffn_hn_{l}", "nt", dz3, W["w_ffn_in"], grid=(S // tx, 1),
                           a_spec=[pl.BlockSpec((None, tx, FH), functools.partial(lambda i, j, k: (k // 2, i, k % 2), k=k))
                                   for k in range(N_CHIPS)],
                           b_spec=[pl.BlockSpec((None, D, FH), functools.partial(lambda i, j, k: (k, 0, 0), k=k))
                                   for k in range(N_CHIPS)],
                           out_shape=(S, D), out_dtype=F32, out_spec=pl.BlockSpec((tx, D), lambda i, j: (i, 0)),
                           norm_bwd=(h_mid, norm_ffn, dh))
        G["norm_ffn"] = jnp.sum(dgp, axis=0)
        if recurrent:
            CH = W["w_rec_in"].shape[-1]
            C = 2 * CH
            xn, proj, rc, rcb, gip, grp, hrec, m = mix_saved
            G["w_rec_out"] = _mm_tn(f"d_rec_out_{l}", m, dhb, out_dtype=BF16, tn=D)
            dm = _mm_nt(f"d_m_{l}", dhb, W["w_rec_out"], out_dtype=F32, tn=C)
            dgb, dgi, dgr, drc1, G["b_gi"], G["b_gr"], G["lru_param"] = _lru_bwd(
                f"d_lru_{l}", dm, proj, hrec, rc, gip, grp, W["lru_param"])
            drc, G["w_gates"] = _gates_bwd(f"d_gates_{l}", dgi, dgr, rcb, W["w_gates"], drc1)
            drec, G["conv_w"], G["conv_b"] = _conv_bwd(f"d_conv_{l}", drc, proj, W["conv_w"])
            dproj = jnp.concatenate([dgb, drec], axis=1)
            G["w_rec_in"] = _mm(
                f"d_rec_in_{l}", "tn", xn, dproj, grid=(1, N_CHIPS),
                a_spec=pl.BlockSpec((S, D), lambda i, j: (0, 0)),
                b_spec=pl.BlockSpec((S, CH), lambda i, j: (0, j)),
                out_shape=(N_CHIPS, D, CH), out_dtype=BF16,
                out_spec=pl.BlockSpec((None, D, CH), lambda i, j: (j, 0, 0)))
            dh, dhb, dgp = _mm(f"d_rec_xn_{l}", "nt", dproj, W["w_rec_in"], grid=(S // tx, 1),
                               a_spec=[pl.BlockSpec((tx, CH), functools.partial(lambda i, j, k: (i, k), k=k))
                                       for k in range(N_CHIPS)],
                               b_spec=[pl.BlockSpec((None, D, CH), functools.partial(lambda i, j, k: (k, 0, 0), k=k))
                                       for k in range(N_CHIPS)],
                               out_shape=(S, D), out_dtype=F32, out_spec=pl.BlockSpec((tx, D), lambda i, j: (i, 0)),
                               norm_bwd=(h_in, gains["mix"][l], dh))
        else:
            xn, q, o, of, lse = mix_saved
            G["w_o"] = _mm_tn(f"d_o_proj_{l}", o, dhb, out_dtype=BF16, tn=D)
            do = _mm_nt(f"d_o_{l}", dhb, W["w_o"], out_dtype=BF16, tn=D)
            dq, dk, dv, dck, drq = _attn_bwd(f"attn_bwd_{l}", q, kv, cfull, of, do, lse)
            dk_parts.append(dk)
            dv_parts.append(dv)
            dc_parts.append(dck[:, ::HEAD_DIM] + drq.reshape(2 * HP, S).T)
            G["w_q"] = _mm_tn(f"d_q_proj_{l}", xn, dq, out_dtype=BF16, tn=D)
            dh, dhb, dgp = _mm_nt(f"d_q_xn_{l}", dq, W["w_q"], out_dtype=F32, tn=D, norm_bwd=(h_in, gains["mix"][l], dh))
        G["norm_mix"] = jnp.sum(dgp, axis=0)
        if "w_kv" in W:
            dkb = _add_cast("dk_sum", dk_parts, BF16)
            dvb = _add_cast("dv_sum", dv_parts, BF16)
            dkv = jnp.concatenate([dkb, dvb], axis=1)
            dc = sum(dc_parts[1:], dc_parts[0])
            dc_pad = jnp.pad(dc, ((0, 0), (0, LANES - 2 * HP)))
            dls = _cumsum_rows("dc_cumsum", dc_pad, True)
            dfb, G["b_f"] = _logsig_bwd("d_logsig", dls, f_pre)
            G["w_kv"] = _mm_tn("d_kv_proj", hn_kv, dkv, out_dtype=BF16)
            G["w_f"] = _mm_tn("d_f_proj", hn_kv, dfb, out_dtype=F32)
            dhn_f = _mm_nt("d_f_hn", dfb, W["w_f"], out_dtype=F32, tn=D)
            dh, dhb, dgp = _mm_nt("d_kv_hn", dkv, W["w_kv"], out_dtype=F32, tn=D, res=dhn_f,
                                  norm_bwd=(h_kv, W["norm_kv"], dh))
            G["norm_kv"] = jnp.sum(dgp, axis=0)
        token = layer_grads(l, "mix", G)
    return loss_row, dh, dg_final


_ANY = pl.BlockSpec(memory_space=pl.ANY)


def _position():
    return lax.axis_index("x"), lax.axis_index("y"), lax.axis_index("c")


def _chip_peers(x, y):
    return [(1 - x, y), (x, 1 - y), (1 - x, 1 - y)]


def _half_rows(c, n):
    h = n // 2
    assert h % 16 == 0
    return pl.ds(pl.multiple_of(c * h, 16), h)


def _place_own(name, shard, layer, me):
    _, R, C = shard.shape
    tr = _row_tile(R, C, 2 * shard.dtype.itemsize, target=8 << 20)

    def body(me_ref, x_ref, o_ref):
        o_ref[...] = x_ref[...]

    return pl.pallas_call(
        body, name=name,
        grid_spec=pltpu.PrefetchScalarGridSpec(
            num_scalar_prefetch=1, grid=(R // tr,),
            in_specs=[pl.BlockSpec((None, tr, C), lambda i, me_ref: (layer, i, 0))],
            out_specs=pl.BlockSpec((None, tr, C), lambda i, me_ref: (me_ref[0], i, 0))),
        out_shape=_hbm_out((N_CHIPS, R, C), shard.dtype),
        compiler_params=_params(("parallel",)),
    )(me, shard)


def _gather_smalls(name, smalls):
    ns = len(smalls)

    def body(*refs):
        ins, outs = refs[:ns], refs[ns:2 * ns]
        send_sems, recv_sems, local_sems = refs[2 * ns:]
        x, y, c = _position()
        me = 2 * x + y
        peers = _chip_peers(x, y)

        def remote(t, k, chip):
            px, py = peers[k]
            return pltpu.make_async_remote_copy(
                src_ref=ins[t], dst_ref=outs[t].at[chip], send_sem=send_sems.at[3 * t + k],
                recv_sem=recv_sems.at[3 * t + k], device_id=(px, py, c), device_id_type=MESH)

        local = [pltpu.make_async_copy(ins[t], outs[t].at[me], local_sems.at[t]) for t in range(ns)]
        for t in range(ns):
            local[t].start()
            for k in range(3):
                remote(t, k, me).start()
        for t in range(ns):
            for k in range(3):
                px, py = peers[k]
                remote(t, k, 2 * px + py).wait_recv()
        for t in range(ns):
            for k in range(3):
                remote(t, k, me).wait_send()
            local[t].wait()

    return pl.pallas_call(
        body, name=name, in_specs=[_ANY] * ns, out_specs=[_ANY] * ns,
        out_shape=[_hbm_out((N_CHIPS,) + s.shape, s.dtype) for s in smalls],
        scratch_shapes=[pltpu.SemaphoreType.DMA((3 * ns,)), pltpu.SemaphoreType.DMA((3 * ns,)),
                        pltpu.SemaphoreType.DMA((ns,))],
    )(*smalls)


_SEM = pl.BlockSpec(memory_space=pltpu.SEMAPHORE)
_SPLIT = pltpu.CompilerParams(has_side_effects=pltpu.SideEffectType.DATAFLOW_SIDE_EFFECTING)


def _weight_copy(shards, buf, items, sems, i, k, chip_of_dst, peers, c):
    w, l = items[i]
    px, py = peers[k]
    half = _half_rows(c, shards[w].shape[1])
    return pltpu.make_async_remote_copy(
        src_ref=shards[w].at[l, half], dst_ref=buf.at[chip_of_dst, half],
        send_sem=sems[0].at[3 * i + k], recv_sem=sems[1].at[3 * i + k],
        device_id=(px, py, c), device_id_type=MESH)


def _gather_start(name, shards, bufs, items, after):
    nw, n = len(shards), len(bufs)

    def body(*refs):
        ins, outs, sems = refs[:nw], refs[nw + n + 1:nw + 2 * n + 1], refs[nw + 2 * n + 1:]
        x, y, c = _position()
        peers = _chip_peers(x, y)
        for i in range(n):
            for k in range(3):
                _weight_copy(ins, outs[i], items, sems, i, k, 2 * x + y, peers, c).start()

    res = pl.pallas_call(
        body, name=name, in_specs=[_ANY] * (nw + n + 1), out_specs=[_ANY] * n + [_SEM, _SEM],
        out_shape=[_hbm_out(b.shape, b.dtype) for b in bufs]
        + [pltpu.SemaphoreType.DMA((3 * n,)), pltpu.SemaphoreType.DMA((3 * n,))],
        input_output_aliases={nw + i: i for i in range(n)}, compiler_params=_SPLIT,
    )(*shards, *bufs, after)
    return res[:n], res[n:]


def _gather_wait(name, shards, bufs, items, ids, sems, after):
    nw, m = len(shards), len(ids)

    def body(*refs):
        ins, bs = refs[:nw], refs[nw:nw + m]
        sem_refs = refs[nw + m:nw + m + 2]
        x, y, c = _position()
        peers = _chip_peers(x, y)
        for j, i in enumerate(ids):
            for k in range(3):
                px, py = peers[k]
                _weight_copy(ins, bs[j], items, sem_refs, i, k, 2 * px + py, peers, c).wait_recv()
        for j, i in enumerate(ids):
            for k in range(3):
                _weight_copy(ins, bs[j], items, sem_refs, i, k, 2 * x + y, peers, c).wait_send()

    res = pl.pallas_call(
        body, name=name, in_specs=[_ANY] * (nw + m) + [_SEM, _SEM, _ANY], out_specs=[_ANY] * m,
        out_shape=[_hbm_out(bufs[i].shape, bufs[i].dtype) for i in ids],
        input_output_aliases={nw + j: j for j in range(m)}, compiler_params=_SPLIT,
    )(*shards, *[bufs[i] for i in ids], *sems, after)
    return list(res)


def _gather_d2d(name, bufs):
    n = len(bufs)

    def body(*refs):
        ins, outs = refs[:n], refs[n:2 * n]
        send_sems, recv_sems = refs[2 * n:]
        x, y, c = _position()
        peers = _chip_peers(x, y)

        def remote(i, k, core):
            px, py = peers[k]
            half = _half_rows(core, ins[i].shape[1])
            return pltpu.make_async_remote_copy(
                src_ref=ins[i].at[2 * px + py, half], dst_ref=outs[i].at[2 * px + py, half],
                send_sem=send_sems.at[3 * i + k], recv_sem=recv_sems.at[3 * i + k],
                device_id=(x, y, 1 - c), device_id_type=MESH)

        for i in range(n):
            for k in range(3):
                remote(i, k, c).start()
        for i in range(n):
            for k in range(3):
                remote(i, k, 1 - c).wait_recv()
        for i in range(n):
            for k in range(3):
                remote(i, k, c).wait_send()

    return list(pl.pallas_call(
        body, name=name, in_specs=[_ANY] * n, out_specs=[_ANY] * n,
        out_shape=[_hbm_out(g.shape, g.dtype) for g in bufs],
        input_output_aliases={i: i for i in range(n)},
        scratch_shapes=[pltpu.SemaphoreType.DMA((3 * n,)), pltpu.SemaphoreType.DMA((3 * n,))],
    )(*bufs))


def _reduce_d2d(name, grads):
    n = len(grads)

    def body(*refs):
        ins, outs = refs[:n], refs[n:2 * n]
        send_sems, recv_sems = refs[2 * n:]
        x, y, c = _position()
        remote = [pltpu.make_async_remote_copy(
            src_ref=ins[i].at[:, _half_rows(1 - c, ins[i].shape[1])], dst_ref=outs[i],
            send_sem=send_sems.at[i], recv_sem=recv_sems.at[i],
            device_id=(x, y, 1 - c), device_id_type=MESH) for i in range(n)]
        for cp in remote:
            cp.start()
        for cp in remote:
            cp.wait_recv()
        for cp in remote:
            cp.wait_send()

    return pl.pallas_call(
        body, name=name, in_specs=[_ANY] * n, out_specs=[_ANY] * n,
        out_shape=[_hbm_out((N_CHIPS, g.shape[1] // 2, g.shape[2]), g.dtype) for g in grads],
        scratch_shapes=[pltpu.SemaphoreType.DMA((n,)), pltpu.SemaphoreType.DMA((n,))],
    )(*grads)


def _sum_cores(name, g, other, core):
    _, R, C = g.shape
    H = R // 2
    tr = _row_tile(H, C, 3 * 2, target=12 << 20)
    nb = H // tr

    def body(c_ref, g_ref, o_ref, out_ref):
        out_ref[...] = (g_ref[...].astype(F32) + o_ref[...].astype(F32)).astype(out_ref.dtype)

    return pl.pallas_call(
        body, name=name,
        grid_spec=pltpu.PrefetchScalarGridSpec(
            num_scalar_prefetch=1, grid=(N_CHIPS, nb),
            in_specs=[pl.BlockSpec((None, tr, C), lambda j, i, c_ref: (j, c_ref[0] * nb + i, 0)),
                      pl.BlockSpec((None, tr, C), lambda j, i, c_ref: (j, i, 0))],
            out_specs=pl.BlockSpec((None, tr, C), lambda j, i, c_ref: (j, i, 0))),
        out_shape=_hbm_out((N_CHIPS, H, C), BF16),
        compiler_params=_params(("parallel", "parallel")),
    )(core, g, other)


def _sum_chips(name, received, own, full, layer, me_core):
    _, H, C = received.shape
    tr = _row_tile(H, C, 3 * 2 + 2 + 4, target=12 << 20)
    nb = H // tr

    def body(s_ref, r_ref, own_ref, full_ref, out_ref):
        acc = r_ref[0].astype(F32)
        for k in (1, 2):
            acc = acc + r_ref[k].astype(F32)
        out_ref[...] = acc + own_ref[...].astype(F32)

    return pl.pallas_call(
        body, name=name,
        grid_spec=pltpu.PrefetchScalarGridSpec(
            num_scalar_prefetch=1, grid=(nb,),
            in_specs=[pl.BlockSpec((3, tr, C), lambda i, s_ref: (0, i, 0)),
                      pl.BlockSpec((None, tr, C), lambda i, s_ref: (s_ref[0], i, 0)),
                      _ANY],
            out_specs=pl.BlockSpec((None, tr, C), lambda i, s_ref: (layer, s_ref[1] * nb + i, 0))),
        out_shape=_hbm_out(full.shape, full.dtype),
        input_output_aliases={3: 0},
        compiler_params=_params(("parallel",)),
    )(me_core, received, own, full)


def _part_copy(parts, recv, sems, i, k, peers, c):
    px, py = peers[k]
    return pltpu.make_async_remote_copy(
        src_ref=parts[i].at[2 * px + py], dst_ref=recv[i].at[k],
        send_sem=sems[0].at[3 * i + k], recv_sem=sems[1].at[3 * i + k],
        device_id=(px, py, c), device_id_type=MESH)


def _scatter_start(name, parts):
    n = len(parts)

    def body(*refs):
        ins, outs, sems, token = refs[:n], refs[n:2 * n], refs[2 * n:2 * n + 2], refs[2 * n + 2]
        x, y, c = _position()
        peers = _chip_peers(x, y)
        for i in range(n):
            for k in range(3):
                _part_copy(ins, outs, sems, i, k, peers, c).start()
        token[...] = jnp.zeros_like(token)

    res = pl.pallas_call(
        body, name=name, in_specs=[_ANY] * n,
        out_specs=[_ANY] * n + [_SEM, _SEM, pl.BlockSpec(memory_space=pltpu.VMEM)],
        out_shape=[_hbm_out((3,) + p.shape[1:], p.dtype) for p in parts]
        + [pltpu.SemaphoreType.DMA((3 * n,)), pltpu.SemaphoreType.DMA((3 * n,)),
           jax.ShapeDtypeStruct((SUBLANES, LANES), F32)],
        compiler_params=_SPLIT,
    )(*parts)
    return list(res[:n]), res[n:n + 2], res[n + 2]


def _scatter_wait(name, parts, recv, sems):
    n = len(parts)

    def body(*refs):
        ins, rs, sem_refs = refs[:n], refs[n:2 * n], refs[2 * n:2 * n + 2]
        x, y, c = _position()
        peers = _chip_peers(x, y)
        for i in range(n):
            for k in range(3):
                _part_copy(ins, rs, sem_refs, i, k, peers, c).wait_recv()
        for i in range(n):
            for k in range(3):
                _part_copy(ins, rs, sem_refs, i, k, peers, c).wait_send()

    return list(pl.pallas_call(
        body, name=name, in_specs=[_ANY] * (2 * n) + [_SEM, _SEM], out_specs=[_ANY] * n,
        out_shape=[_hbm_out(r.shape, r.dtype) for r in recv],
        input_output_aliases={n + i: i for i in range(n)}, compiler_params=_SPLIT,
    )(*parts, *recv, *sems))


def _share_d2d(name, full):
    n = len(full)

    def body(*refs):
        ins, outs = refs[:n], refs[n:2 * n]
        send_sems, recv_sems = refs[2 * n:]
        x, y, c = _position()

        def remote(w, core):
            half = _half_rows(core, ins[w].shape[1])
            return pltpu.make_async_remote_copy(
                src_ref=ins[w].at[:, half], dst_ref=outs[w].at[:, half],
                send_sem=send_sems.at[w], recv_sem=recv_sems.at[w],
                device_id=(x, y, 1 - c), device_id_type=MESH)

        for w in range(n):
            remote(w, c).start()
        for w in range(n):
            remote(w, 1 - c).wait_recv()
        for w in range(n):
            remote(w, c).wait_send()

    return pl.pallas_call(
        body, name=name, in_specs=[_ANY] * n, out_specs=[_ANY] * n,
        out_shape=[_hbm_out(f.shape, f.dtype) for f in full],
        input_output_aliases={w: w for w in range(n)},
        scratch_shapes=[pltpu.SemaphoreType.DMA((n,)), pltpu.SemaphoreType.DMA((n,))],
    )(*full)


def _gather_all(name, a):
    def body(a_ref, o_ref, send_sems, recv_sems, local_sem):
        x, y, c = _position()
        me = 4 * x + 2 * y + c

        def peer(k):
            return (x ^ ((k >> 2) & 1), y ^ ((k >> 1) & 1), c ^ (k & 1))

        def remote(k, slot):
            return pltpu.make_async_remote_copy(
                src_ref=a_ref, dst_ref=o_ref.at[slot], send_sem=send_sems.at[k - 1], recv_sem=recv_sems.at[k - 1],
                device_id=peer(k), device_id_type=MESH)

        local = pltpu.make_async_copy(a_ref, o_ref.at[me], local_sem)
        local.start()
        for k in range(1, N_DEV):
            remote(k, me).start()
        for k in range(1, N_DEV):
            px, py, pc = peer(k)
            remote(k, 4 * px + 2 * py + pc).wait_recv()
        for k in range(1, N_DEV):
            remote(k, me).wait_send()
        local.wait()

    return pl.pallas_call(
        body, name=name, in_specs=[_ANY], out_specs=_ANY,
        out_shape=_hbm_out((N_DEV,) + a.shape, a.dtype),
        scratch_shapes=[pltpu.SemaphoreType.DMA((N_DEV - 1,)), pltpu.SemaphoreType.DMA((N_DEV - 1,)),
                        pltpu.SemaphoreType.DMA],
    )(a)


def _rows2d(a, lead=0):
    return a.reshape(a.shape[:lead] + (-1, a.shape[-1]))


def _row_tile(rows, cols, itemsize=4, target=1 << 20):
    want = max(SUBLANES, target // (cols * itemsize))
    t = min(rows, (want // 16) * 16)
    while t > 16 and rows % t:
        t -= 16
    return t if rows % t == 0 else rows


def _sum_slots(name, r, out_dtype=F32):
    ns = r.shape[0]
    r2 = _rows2d(r, 1)
    _, rows, cols = r2.shape
    tr = _row_tile(rows, cols)

    def body(r_ref, o_ref):
        acc = r_ref[0].astype(F32)
        for s in range(1, ns):
            acc = acc + r_ref[s].astype(F32)
        o_ref[...] = acc.astype(o_ref.dtype)

    out = pl.pallas_call(
        body, name=name, grid=(rows // tr,),
        in_specs=[pl.BlockSpec((ns, tr, cols), lambda i: (0, i, 0))],
        out_specs=pl.BlockSpec((tr, cols), lambda i: (i, 0)),
        out_shape=_hbm_out((rows, cols), out_dtype),
        compiler_params=_params(("parallel",)),
    )(r2)
    return out.reshape(r.shape[1:])


def _adamw(name, g_parts, w, m, v):
    shape = w.shape
    ng = len(g_parts)
    args = [_rows2d(a) for a in (*g_parts, w, m, v)]
    rows, cols = args[0].shape
    tr = _row_tile(rows, cols, (ng + 7) * 4, target=16 << 20)
    c1 = 1.0 - ADAM_B1 ** ADAM_STEP
    c2 = 1.0 - ADAM_B2 ** ADAM_STEP

    def body(*refs):
        g = refs[0][...]
        for r in refs[1:ng]:
            g = g + r[...]
        w_ref, m_ref, v_ref = refs[ng:ng + 3]
        g_out, d_out, m_out, v_out = refs[ng + 3:]
        mn = ADAM_B1 * m_ref[...] + (1.0 - ADAM_B1) * g
        vn = ADAM_B2 * v_ref[...] + (1.0 - ADAM_B2) * (g * g)
        m_hat = mn / c1
        v_hat = vn / c2
        g_out[...] = g
        d_out[...] = -ADAM_LR * (m_hat / (jnp.sqrt(v_hat) + ADAM_EPS) + ADAM_WD * w_ref[...])
        m_out[...] = mn
        v_out[...] = vn

    spec = pl.BlockSpec((tr, cols), lambda i: (i, 0))
    outs = pl.pallas_call(
        body, name=name, grid=(rows // tr,), in_specs=[spec] * (ng + 3), out_specs=[spec] * 4,
        out_shape=[_hbm_out((rows, cols), F32)] * 4,
        compiler_params=_params(("parallel",)),
    )(*args)
    return tuple(o.reshape(shape) for o in outs)


_WEIGHTS = ["norm_mix", "norm_ffn", "w_ffn_in", "w_ffn_out", "w_rec_in", "conv_w", "conv_b", "w_lru_gates",
            "b_lru_gates", "lru_param", "w_rec_out", "norm_kv", "w_kvf", "b_forget", "w_q", "w_o", "norm_final"]
_BIG = ["w_ffn_in", "w_ffn_out", "w_rec_in", "w_lru_gates", "w_rec_out", "w_kvf", "w_q", "w_o"]


def _stack3(a):
    return a[None] if a.ndim == 2 else a.reshape(a.shape[0], -1, a.shape[-1])


def _pad_lanes(a, n):
    return jnp.pad(a, ((0, 0),) * (a.ndim - 1) + ((0, n - a.shape[-1]),))


def kernel(x, norm_mix, norm_ffn, w_ffn_in, w_ffn_out, w_rec_in, conv_w, conv_b, w_lru_gates, b_lru_gates, lru_param, w_rec_out, norm_kv, w_kvf, b_forget, w_q, w_o, norm_final, loss_target, m_norm_mix, m_norm_ffn, m_w_ffn_in, m_w_ffn_out, m_w_rec_in, m_conv_w, m_conv_b, m_w_lru_gates, m_b_lru_gates, m_lru_param, m_w_rec_out, m_norm_kv, m_w_kvf, m_b_forget, m_w_q, m_w_o, m_norm_final, v_norm_mix, v_norm_ffn, v_w_ffn_in, v_w_ffn_out, v_w_rec_in, v_conv_w, v_conv_b, v_w_lru_gates, v_b_lru_gates, v_lru_param, v_w_rec_out, v_norm_kv, v_w_kvf, v_b_forget, v_w_q, v_w_o, v_norm_final):
    P = dict(norm_mix=norm_mix, norm_ffn=norm_ffn, w_ffn_in=w_ffn_in, w_ffn_out=w_ffn_out, w_rec_in=w_rec_in,
             conv_w=conv_w, conv_b=conv_b, w_lru_gates=w_lru_gates, b_lru_gates=b_lru_gates, lru_param=lru_param,
             w_rec_out=w_rec_out, norm_kv=norm_kv, w_kvf=w_kvf, b_forget=b_forget, w_q=w_q, w_o=w_o,
             norm_final=norm_final)
    M1 = dict(norm_mix=m_norm_mix, norm_ffn=m_norm_ffn, w_ffn_in=m_w_ffn_in, w_ffn_out=m_w_ffn_out,
              w_rec_in=m_w_rec_in, conv_w=m_conv_w, conv_b=m_conv_b, w_lru_gates=m_w_lru_gates,
              b_lru_gates=m_b_lru_gates, lru_param=m_lru_param, w_rec_out=m_w_rec_out, norm_kv=m_norm_kv,
              w_kvf=m_w_kvf, b_forget=m_b_forget, w_q=m_w_q, w_o=m_w_o, norm_final=m_norm_final)
    M2 = dict(norm_mix=v_norm_mix, norm_ffn=v_norm_ffn, w_ffn_in=v_w_ffn_in, w_ffn_out=v_w_ffn_out,
              w_rec_in=v_w_rec_in, conv_w=v_conv_w, conv_b=v_conv_b, w_lru_gates=v_w_lru_gates,
              b_lru_gates=v_b_lru_gates, lru_param=v_lru_param, w_rec_out=v_w_rec_out, norm_kv=v_norm_kv,
              w_kvf=v_w_kvf, b_forget=v_b_forget, w_q=v_w_q, w_o=v_w_o, norm_final=v_norm_final)

    _, S, D = x.shape
    L = norm_mix.shape[0]
    NA, NBLK, BW, GS = w_lru_gates.shape
    NB = w_q.shape[0]
    C = NBLK * BW
    CS = C // N_CHIPS
    H = b_forget.shape[0]
    assert C == D and H * HEAD_DIM == D and H <= LANES
    chip = 2 * lax.axis_index("x") + lax.axis_index("y")

    small_a = jnp.concatenate([conv_w, conv_b[:, None], lru_param[:, None]], axis=1)
    small_a, b_gates = _gather_smalls("gather_smalls", [small_a, b_lru_gates])
    small_a = small_a.transpose(1, 2, 0, 3).reshape(NA, 6, C)
    b_gates = b_gates.transpose(1, 2, 0, 3).reshape(NA, NBLK, 1, N_CHIPS * GS)
    shards = [_stack3(P[w]).astype(BF16) for w in _BIG]
    core = lax.axis_index("c")
    chip_id = jnp.reshape(chip, (1,)).astype(jnp.int32)
    core_id = jnp.reshape(core, (1,)).astype(jnp.int32)
    me_core = jnp.stack([chip, core]).astype(jnp.int32)

    def stage_items(l, part):
        if part == "ffn":
            return [(_BIG.index("w_ffn_in"), l), (_BIG.index("w_ffn_out"), l)]
        if l < NA:
            names, at = ["w_rec_in", "w_lru_gates", "w_rec_out"], l
        else:
            names, at = (["w_kvf"] if l == NA else []) + ["w_q", "w_o"], l - NA
        return [(_BIG.index(n), 0 if n == "w_kvf" else at) for n in names]

    stages = [(l, part) for l in range(L) for part in ("mix", "ffn")]
    items = [it for st in stages for it in stage_items(*st)]
    ids_of = {st: [items.index(it) for it in stage_items(*st)] for st in stages}
    bufs = [_place_own(f"place_{_BIG[w]}_{li}", shards[w], li, chip_id) for w, li in items]
    bufs, gather_sems = _gather_start("gather_start", shards, bufs, items, small_a)

    def layer_weights(l, part, after):
        if l >= L:
            return None
        ids = ids_of[(l, part)]
        got = _gather_wait(f"gather_wait_{part}_{l}", shards, bufs, items, ids, gather_sems, after)
        got = _gather_d2d(f"gather_d2d_{part}_{l}", got)
        B = {_BIG[items[i][0]]: g for i, g in zip(ids, got)}
        if part == "ffn":
            return dict(w_ffn_in=B["w_ffn_in"], w_ffn_out=B["w_ffn_out"].reshape(-1, D))
        W = {}
        if l < NA:
            W.update(w_rec_in=B["w_rec_in"],
                     w_gates=B["w_lru_gates"].reshape(N_CHIPS, NBLK, BW, GS).transpose(1, 2, 0, 3).reshape(
                         NBLK, BW, N_CHIPS * GS),
                     b_gates=b_gates[l], w_rec_out=B["w_rec_out"].reshape(C, D),
                     conv_w=small_a[l, :4], conv_b=small_a[l, 4:5], lru_param=small_a[l, 5:6])
        else:
            W.update(w_q=B["w_q"].reshape(D, D), w_o=B["w_o"].reshape(D, D))
            if l == NA:
                w_kvf_full = B["w_kvf"].transpose(1, 0, 2).reshape(D, -1)
                W.update(norm_kv=norm_kv[None], w_kv=w_kvf_full[:, :2 * D],
                         w_f=_pad_lanes(w_kvf_full[:, 2 * D:], LANES), b_f=_pad_lanes(b_forget[None], LANES))
        return W

    G_small = {l: {} for l in range(L)}
    pending = {}

    def layer_grads(l, part, G):
        G_small[l].update(G)
        by_name = dict(
            w_ffn_in=lambda: G["w_ffn_in"], w_ffn_out=lambda: G["w_ffn_out"].reshape(N_CHIPS, -1, D),
            w_rec_in=lambda: G["w_rec_in"],
            w_lru_gates=lambda: G["w_gates"].reshape(NBLK, BW, N_CHIPS, GS).transpose(2, 0, 1, 3).reshape(
                N_CHIPS, NBLK * BW, GS),
            w_rec_out=lambda: G["w_rec_out"].reshape(N_CHIPS, -1, D),
            w_kvf=lambda: jnp.concatenate([G["w_kv"].astype(F32), G["w_f"][:, :H]], axis=1).reshape(
                D, N_CHIPS, -1).transpose(1, 0, 2).astype(BF16),
            w_q=lambda: G["w_q"].reshape(N_CHIPS, -1, D), w_o=lambda: G["w_o"].reshape(N_CHIPS, -1, D))
        its = stage_items(l, part)
        grads = [by_name[_BIG[w]]() for w, _ in its]
        others = _reduce_d2d(f"reduce_d2d_{part}_{l}", grads)
        parts = [_sum_cores(f"sum_cores_{l}_{_BIG[w]}", g, o, core_id) for (w, _), g, o in zip(its, grads, others)]
        recv, sems, token = _scatter_start(f"scatter_start_{part}_{l}", parts)
        pending[(l, part)] = (parts, recv, sems)
        return token

    gains = dict(mix=[norm_mix[l][None] for l in range(L)], ffn=[norm_ffn[l][None] for l in range(L)],
                 final=norm_final[None])
    loss_row, grad_x, dg_final = _local_step(x.reshape(S, D), loss_target.reshape(S, D), gains,
                                             layer_weights, layer_grads)

    rows = [*[G_small[l]["norm_mix"] for l in range(L)], *[G_small[l]["norm_ffn"] for l in range(L)],
            G_small[NA]["norm_kv"], dg_final, _pad_lanes(G_small[NA]["b_f"], D), _pad_lanes(loss_row, D)]
    for a in range(NA):
        rows += [G_small[a][n] for n in ("conv_w", "conv_b", "b_gi", "b_gr", "lru_param")]
    packed = jnp.concatenate(rows, axis=0)
    tot = _sum_slots("sum_small", _gather_all("gather_small", packed))
    loss = tot[2 * L + 3, 0]
    g_rep = jnp.concatenate([tot[:2 * L + 2], tot[2 * L + 2:2 * L + 3]], axis=0)
    base = 2 * L + 4
    g_sh = []
    for a in range(NA):
        blk = lax.dynamic_slice_in_dim(tot[base + 8 * a:base + 8 * a + 8], chip * CS, CS, axis=1)
        gi = tot[base + 8 * a + 5].reshape(NBLK, BW)
        gr = tot[base + 8 * a + 6].reshape(NBLK, BW)
        bl = lax.dynamic_slice_in_dim(jnp.concatenate([gi, gr], axis=1), chip * GS, GS, axis=1)
        g_sh += [blk[:5], bl.reshape(-1, CS), blk[7:8]]
    g_sh = jnp.concatenate(g_sh, axis=0)
    nrow = g_sh.shape[0] // NA

    def pack_rep(T):
        return jnp.concatenate([T["norm_mix"], T["norm_ffn"], T["norm_kv"][None], T["norm_final"][None],
                                _pad_lanes(T["b_forget"][None], D)], axis=0)

    def pack_sh(T):
        return jnp.concatenate([jnp.concatenate([T["conv_w"][a], T["conv_b"][a][None],
                                                 T["b_lru_gates"][a].reshape(-1, CS), T["lru_param"][a][None]], axis=0)
                                for a in range(NA)], axis=0)

    rep = _adamw("adamw_replicated", [g_rep], pack_rep(P), pack_rep(M1), pack_rep(M2))
    shd = _adamw("adamw_small_sharded", [g_sh], pack_sh(P), pack_sh(M1), pack_sh(M2))

    def unpack_rep(t):
        return dict(norm_mix=t[:L], norm_ffn=t[L:2 * L], norm_kv=t[2 * L], norm_final=t[2 * L + 1],
                    b_forget=t[2 * L + 2, :H])

    def unpack_sh(t):
        t = t.reshape(NA, nrow, CS)
        return dict(conv_w=t[:, :4], conv_b=t[:, 4], b_lru_gates=t[:, 5:nrow - 1].reshape(NA, NBLK, GS),
                    lru_param=t[:, nrow - 1])

    full = [lax.empty(sh.shape, F32) for sh in shards]
    for l, part in reversed(stages):
        parts, recv, sems = pending[(l, part)]
        recv = _scatter_wait(f"scatter_wait_{part}_{l}", parts, recv, sems)
        for (w, li), own, r in zip(stage_items(l, part), parts, recv):
            full[w] = _sum_chips(f"sum_chips_{l}_{_BIG[w]}", r, own, full[w], li, me_core)
    full = _share_d2d("share_d2d", full)
    big = {w: _adamw(f"adamw_{w}", [g.reshape(P[w].shape)], P[w], M1[w], M2[w]) for w, g in zip(_BIG, full)}

    outs = []
    for i in range(4):
        small = {**unpack_rep(rep[i]), **unpack_sh(shd[i])}
        outs.append([big[w][i] if w in big else small[w] for w in _WEIGHTS])
    return (loss, grad_x.reshape(1, S, D), *outs[0], *outs[1], *outs[2], *outs[3])
```

```python
import functools
import math

import jax
import jax.numpy as jnp
from jax import lax
from jax.experimental import pallas as pl
from jax.experimental.pallas import tpu as pltpu

F32 = jnp.float32
BF16 = jnp.bfloat16

EPS = 1e-6
LRU_C = 8.0
HEAD_DIM = 64
LANES = 128
SUBLANES = 8
VMEM_LIMIT = 48 * 1024 * 1024
N_CHIPS = 4
N_DEV = 8

ADAM_LR = 0.001
ADAM_B1 = 0.9
ADAM_B2 = 0.999
ADAM_EPS = 1e-08
ADAM_WD = 0.01
ADAM_STEP = 10

_NN = (((1,), (0,)), ((), ()))
_NT = (((1,), (1,)), ((), ()))
_TN = (((0,), (0,)), ((), ()))
_DN = {"nn": _NN, "nt": _NT, "tn": _TN}
MESH = pl.DeviceIdType.MESH


def _hbm_out(shape, dtype):
    return pltpu.HBM(shape, dtype)


def _params(sem):
    return pltpu.CompilerParams(dimension_semantics=sem, vmem_limit_bytes=VMEM_LIMIT)


def _tile(n, want):
    if n <= want:
        return n
    t = (want // LANES) * LANES
    while t >= LANES:
        if n % t == 0:
            return t
        t -= LANES
    return n


def _sigmoid(x):
    return 1.0 / (1.0 + jnp.exp(-x))


def _sigmoid_t(x):
    return 0.5 * jnp.tanh(0.5 * x) + 0.5


def _softplus(x):
    return jnp.maximum(x, 0.0) + jnp.log(1.0 + jnp.exp(-jnp.abs(x)))


_GELU_C = math.sqrt(2.0 / math.pi)


def _gelu_and_grad(x):
    inner = _GELU_C * (x + 0.044715 * x * x * x)
    t = jnp.tanh(inner)
    g = 0.5 * x * (1.0 + t)
    dg = 0.5 * (1.0 + t) + 0.5 * x * (1.0 - t * t) * _GELU_C * (1.0 + 3.0 * 0.044715 * x * x)
    return g, dg


def _rms(x):
    return lax.rsqrt(jnp.mean(x * x, axis=-1, keepdims=True) + EPS)


def _rms_bwd(dy, x, g):
    r = _rms(x)
    xr = x * r
    dyg = dy * g
    return r * dyg - xr * (r * jnp.mean(dyg * xr, axis=-1, keepdims=True)), jnp.sum(dy * xr, axis=0, keepdims=True)


def _mm(name, mode, a, b, *, grid, a_spec, b_spec, out_shape, out_dtype, out_spec, nk=1,
        res=None, res_spec=None, bias=None, bias_spec=None, scale=None, norm_gain=None, norm_bwd=None):
    dn = _DN[mode]
    has_res, has_bias = res is not None, bias is not None
    blk = tuple(d for d in out_spec.block_shape if d is not None)
    vec = pl.BlockSpec((1, blk[-1]), lambda *g: (0, 0))
    a_specs = a_spec if isinstance(a_spec, list) else [a_spec]
    b_specs = b_spec if isinstance(b_spec, list) else [b_spec]
    npair = len(a_specs)
    n_in = 2 * npair + int(has_res) + int(has_bias) + (1 if norm_gain is not None else 0) + (3 if norm_bwd else 0)

    def body(*refs):
        p = 2 * npair
        r_ref = refs[p] if has_res else None
        p += int(has_res)
        bias_ref = refs[p] if has_bias else None
        p += int(has_bias)
        extra = refs[p:n_in]
        outs = refs[n_in:]
        o_ref = outs[0]
        part = lax.dot_general(refs[0][...], refs[npair][...], dn, preferred_element_type=F32)
        for t in range(1, npair):
            part = part + lax.dot_general(refs[t][...], refs[npair + t][...], dn, preferred_element_type=F32)

        def finish(acc):
            if scale is not None:
                acc = acc * scale
            if has_bias:
                acc = acc + bias_ref[...]
            if has_res:
                acc = r_ref[...] + acc
            if norm_bwd:
                h_ref, g_ref, dh_ref = extra
                dx, dg = _rms_bwd(acc, h_ref[...], g_ref[...])
                acc = dh_ref[...] + dx
                outs[1][...] = acc.astype(BF16)
                outs[2][...] = dg
            if norm_gain is not None:
                outs[1][...] = (acc * _rms(acc) * extra[0][...]).astype(BF16)
            o_ref[...] = acc.astype(o_ref.dtype)

        if nk == 1:
            finish(part)
        else:
            acc_ref = refs[-1]
            k = pl.program_id(2)

            @pl.when(k == 0)
            def _():
                acc_ref[...] = part

            @pl.when(k > 0)
            def _():
                acc_ref[...] += part

            @pl.when(k == nk - 1)
            def _():
                finish(acc_ref[...])

    ins, specs = [a] * npair + [b] * npair, a_specs + b_specs
    if has_res:
        ins.append(res)
        specs.append(res_spec)
    if has_bias:
        ins.append(bias)
        specs.append(bias_spec)
    out_specs, out_shapes = [out_spec], [_hbm_out(out_shape, out_dtype)]
    if norm_gain is not None:
        ins.append(norm_gain)
        specs.append(vec)
        out_specs.append(out_spec)
        out_shapes.append(_hbm_out(out_shape, BF16))
    if norm_bwd:
        h, g, dh = norm_bwd
        ins += [h, g, dh]
        specs += [out_spec, vec, out_spec]
        out_specs += [out_spec, pl.BlockSpec((None, 1, blk[-1]), lambda i, *rest: (i, 0, 0))]
        out_shapes += [_hbm_out(out_shape, BF16), _hbm_out((grid[0], 1, blk[-1]), F32)]
    sem = ("parallel", "parallel") + (("arbitrary",) if len(grid) == 3 else ())
    single = len(out_specs) == 1
    return pl.pallas_call(
        body, name=name, grid=grid, in_specs=specs, out_specs=out_specs[0] if single else out_specs,
        out_shape=out_shapes[0] if single else out_shapes,
        scratch_shapes=[pltpu.VMEM(blk, F32)] if nk > 1 else [],
        compiler_params=_params(sem),
    )(*ins)


def _mm_nn(name, a, b, *, b_lead=(), out_dtype, tm=512, tn=512, res=None, bias=None, scale=None, norm_gain=None):
    M, K = a.shape
    N = b.shape[-1]
    tm, tn = _tile(M, tm), _tile(N, tn)
    nl = len(b_lead)
    return _mm(
        name, "nn", a, b, grid=(M // tm, N // tn),
        a_spec=pl.BlockSpec((tm, K), lambda i, j: (i, 0)),
        b_spec=pl.BlockSpec((None,) * nl + (K, tn), lambda i, j: tuple(b_lead) + (0, j)),
        out_shape=(M, N), out_dtype=out_dtype, out_spec=pl.BlockSpec((tm, tn), lambda i, j: (i, j)),
        res=res, res_spec=pl.BlockSpec((tm, tn), lambda i, j: (i, j)),
        bias=bias, bias_spec=pl.BlockSpec((1, tn), lambda i, j: (0, j)), scale=scale, norm_gain=norm_gain)


def _mm_nt(name, a, b, *, b_lead=(), out_dtype, tm=512, tn=512, tk=2048, res=None, norm_bwd=None):
    M, K = a.shape
    N = b.shape[-2]
    tm, tn, tk = _tile(M, tm), _tile(N, tn), _tile(K, tk)
    nk = K // tk
    nl = len(b_lead)
    return _mm(
        name, "nt", a, b, grid=(M // tm, N // tn, nk), nk=nk,
        a_spec=pl.BlockSpec((tm, tk), lambda i, j, k: (i, k)),
        b_spec=pl.BlockSpec((None,) * nl + (tn, tk), lambda i, j, k: tuple(b_lead) + (j, k)),
        out_shape=(M, N), out_dtype=out_dtype, out_spec=pl.BlockSpec((tm, tn), lambda i, j, k: (i, j)),
        res=res, res_spec=pl.BlockSpec((tm, tn), lambda i, j, k: (i, j)), norm_bwd=norm_bwd)


def _mm_tn(name, a, b, *, out_dtype, tm=512, tn=512):
    S, M = a.shape
    N = b.shape[1]
    tm, tn = _tile(M, tm), _tile(N, tn)
    return _mm(
        name, "tn", a, b, grid=(M // tm, N // tn),
        a_spec=pl.BlockSpec((S, tm), lambda i, j: (0, i)),
        b_spec=pl.BlockSpec((S, tn), lambda i, j: (0, j)),
        out_shape=(M, N), out_dtype=out_dtype, out_spec=pl.BlockSpec((tm, tn), lambda i, j: (i, j)))


def _rmsnorm_fwd(name, h, g, tr=256):
    S, D = h.shape
    tr = _tile(S, tr)

    def body(h_ref, g_ref, o_ref):
        x = h_ref[...]
        r = lax.rsqrt(jnp.mean(x * x, axis=-1, keepdims=True) + EPS)
        o_ref[...] = (x * r * g_ref[...]).astype(o_ref.dtype)

    return pl.pallas_call(
        body, name=name, grid=(S // tr,),
        in_specs=[pl.BlockSpec((tr, D), lambda i: (i, 0)), pl.BlockSpec((1, D), lambda i: (0, 0))],
        out_specs=pl.BlockSpec((tr, D), lambda i: (i, 0)),
        out_shape=_hbm_out((S, D), BF16),
        compiler_params=_params(("parallel",)),
    )(h, g)


def _loss_head(name, h, target, g, tr=256):
    S, D = h.shape
    tr = _tile(S, tr)

    def body(h_ref, t_ref, g_ref, o_ref, ob_ref, dg_ref, loss_ref):
        i = pl.program_id(0)
        x = h_ref[...]
        gg = g_ref[...]
        r = lax.rsqrt(jnp.mean(x * x, axis=-1, keepdims=True) + EPS)
        xr = x * r
        err = xr * gg - t_ref[...]
        lpart = 0.5 * jnp.sum(jnp.mean(err * err, axis=-1, keepdims=True), axis=0, keepdims=True)
        dy = err * (1.0 / D)
        dyg = dy * gg
        dx = r * dyg - xr * (r * jnp.mean(dyg * xr, axis=-1, keepdims=True))
        o_ref[...] = dx
        ob_ref[...] = dx.astype(BF16)
        part = jnp.sum(dy * xr, axis=0, keepdims=True)
        lrow = jnp.broadcast_to(lpart, (1, LANES))

        @pl.when(i == 0)
        def _():
            dg_ref[...] = part
            loss_ref[...] = lrow

        @pl.when(i > 0)
        def _():
            dg_ref[...] += part
            loss_ref[...] += lrow

    row = pl.BlockSpec((tr, D), lambda i: (i, 0))
    vec = pl.BlockSpec((1, D), lambda i: (0, 0))
    return pl.pallas_call(
        body, name=name, grid=(S // tr,),
        in_specs=[row, row, vec], out_specs=[row, row, vec, pl.BlockSpec((1, LANES), lambda i: (0, 0))],
        out_shape=[_hbm_out((S, D), F32), _hbm_out((S, D), BF16),
                   _hbm_out((1, D), F32), _hbm_out((1, LANES), F32)],
        compiler_params=_params(("arbitrary",)),
    )(h, target, g)


def _swiglu_fwd(name, hn, w_in, tm=512):
    S, D = hn.shape
    FH = w_in.shape[-1]
    tm = _tile(S, tm)

    def body(x_ref, wg_ref, wu_ref, z_ref, a_ref):
        x = x_ref[...]
        zg = jnp.dot(x, wg_ref[...], preferred_element_type=F32)
        zu = jnp.dot(x, wu_ref[...], preferred_element_type=F32)
        z_ref[0] = zg.astype(z_ref.dtype)
        z_ref[1] = zu.astype(z_ref.dtype)
        a_ref[...] = (zg * _sigmoid_t(zg) * zu).astype(a_ref.dtype)

    return pl.pallas_call(
        body, name=name, grid=(S // tm, 2),
        in_specs=[pl.BlockSpec((tm, D), lambda i, j: (i, 0)),
                  pl.BlockSpec((None, D, FH), lambda i, j: (j, 0, 0)),
                  pl.BlockSpec((None, D, FH), lambda i, j: (j + 2, 0, 0))],
        out_specs=[pl.BlockSpec((2, tm, FH), lambda i, j: (0, i, j)), pl.BlockSpec((tm, FH), lambda i, j: (i, j))],
        out_shape=[_hbm_out((2, S, 2 * FH), BF16), _hbm_out((S, 2 * FH), BF16)],
        compiler_params=_params(("parallel", "parallel")),
    )(hn, w_in, w_in)


def _swiglu_bwd(name, dhb, w_out, z3, tm=512):
    S, D = dhb.shape
    F = w_out.shape[0]
    FH = F // 2
    tm = _tile(S, tm)

    def body(d_ref, w_ref, z_ref, dz_ref):
        d = lax.dot_general(d_ref[...], w_ref[...], _NT, preferred_element_type=F32)
        zg = z_ref[0].astype(F32)
        zu = z_ref[1].astype(F32)
        sg = _sigmoid_t(zg)
        dz_ref[0] = (d * zu * (sg * (1.0 + zg * (1.0 - sg)))).astype(dz_ref.dtype)
        dz_ref[1] = (d * (zg * sg)).astype(dz_ref.dtype)

    zspec = pl.BlockSpec((2, tm, FH), lambda i, j: (0, i, j))
    return pl.pallas_call(
        body, name=name, grid=(S // tm, 2),
        in_specs=[pl.BlockSpec((tm, D), lambda i, j: (i, 0)), pl.BlockSpec((FH, D), lambda i, j: (j, 0)), zspec],
        out_specs=zspec, out_shape=_hbm_out((2, S, F), BF16),
        compiler_params=_params(("parallel", "parallel")),
    )(dhb, w_out, z3)


SCAN_ROWS = 64


def _group_scan(A, B, reverse):
    n = A.shape[0]
    sub = lax.broadcasted_iota(jnp.int32, A.shape, 0) % SUBLANES
    for d in (1, 2, 4):
        if reverse:
            A_sh, B_sh = pltpu.roll(A, n - d, 0), pltpu.roll(B, n - d, 0)
            keep = sub < SUBLANES - d
        else:
            A_sh, B_sh = pltpu.roll(A, d, 0), pltpu.roll(B, d, 0)
            keep = sub >= d
        B = jnp.where(keep, A * B_sh + B, B)
        A = jnp.where(keep, A * A_sh, A)
    return A, B


def _block_scan(a, u, carry, reverse):
    A, B = _group_scan(a, u, reverse)
    ng = a.shape[0] // SUBLANES
    out = [None] * ng
    order = range(ng - 1, -1, -1) if reverse else range(ng)
    for gi in order:
        sl = slice(gi * SUBLANES, (gi + 1) * SUBLANES)
        hg = A[sl] * carry + B[sl]
        out[gi] = hg
        carry = hg[0:1] if reverse else hg[SUBLANES - 1:SUBLANES]
    return jnp.concatenate(out, axis=0), carry


def _lru_gates(rc, gip, grp, sp):
    gi = _sigmoid(gip)
    gr = _sigmoid(grp)
    la = -LRU_C * gr * sp
    a = jnp.exp(la)
    om = -jnp.tanh(la) * (a * a + 1.0)
    mult = jnp.sqrt(om)
    return gi, gr, a, mult


def _lru_fwd(name, proj, rc, gip, grp, lru_p, tc=256):
    S, C = rc.shape
    tc = _tile(C, tc)
    nb = S // SCAN_ROWS

    def body(gb_ref, rc_ref, gi_ref, gr_ref, l_ref, h_ref, m_ref):
        sp = _softplus(-l_ref[...])

        def step(b, carry):
            rows = pl.ds(pl.multiple_of(b * SCAN_ROWS, SCAN_ROWS), SCAN_ROWS)
            rcb = rc_ref[rows, :]
            gi, _, a, mult = _lru_gates(rcb, gi_ref[rows, :], gr_ref[rows, :], sp)
            h, carry = _block_scan(a, rcb * gi * mult, carry, False)
            h_ref[rows, :] = h
            gel, _ = _gelu_and_grad(gb_ref[rows, :])
            m_ref[rows, :] = (gel * h).astype(m_ref.dtype)
            return carry

        lax.fori_loop(0, nb, step, jnp.zeros((1, tc), F32))

    col = pl.BlockSpec((S, tc), lambda j: (0, j))
    return pl.pallas_call(
        body, name=name, grid=(C // tc,),
        in_specs=[col, col, col, col, pl.BlockSpec((1, tc), lambda j: (0, j))],
        out_specs=[col, col],
        out_shape=[_hbm_out((S, C), F32), _hbm_out((S, C), BF16)],
        compiler_params=_params(("parallel",)),
    )(proj, rc, gip, grp, lru_p)


def _lru_bwd(name, dm, proj, hrec, rc, gip, grp, lru_p, tc=256):
    S, C = rc.shape
    tc = _tile(C, tc)
    nb = S // SCAN_ROWS
    R = SCAN_ROWS

    def body(dm_ref, gb_ref, h_ref, rc_ref, gi_ref, gr_ref, l_ref,
             dgb_ref, dgi_ref, dgr_ref, drc_ref, dbi_ref, dbr_ref, dl_ref):
        lp = l_ref[...]
        sp = _softplus(-lp)
        row = lax.broadcasted_iota(jnp.int32, (R, tc), 0)
        zero = jnp.zeros((1, tc), F32)

        def step(t, carry):
            mu_in, s_i, s_r, s_sp = carry
            b = nb - 1 - t
            r0 = pl.multiple_of(b * R, R)
            rows = pl.ds(r0, R)
            rcb = rc_ref[rows, :]
            gi, gr, a, mult = _lru_gates(rcb, gi_ref[rows, :], gr_ref[rows, :], sp)
            gel, dgel = _gelu_and_grad(gb_ref[rows, :])
            dmb = dm_ref[rows, :]
            h = h_ref[rows, :]
            dgb_ref[rows, :] = (dmb * h * dgel).astype(dgb_ref.dtype)
            dh = dmb * gel
            mu, mu_out = _block_scan(a, a * dh, mu_in, True)
            mu_next = jnp.where(row == R - 1, mu_in, pltpu.roll(mu, R - 1, 0))
            lam = dh + mu_next
            p0 = pl.multiple_of(jnp.maximum(r0 - SUBLANES, 0), SUBLANES)
            prev = h_ref[pl.ds(p0, SUBLANES), :][SUBLANES - 1:SUBLANES]
            prev = jnp.where(b > 0, prev, 0.0)
            h_prev = jnp.where(row == 0, prev, pltpu.roll(h, 1, 0))
            da = lam * h_prev
            d_mult = lam * rcb * gi
            d_la = da * a - d_mult * (a * a) / mult
            d_grp = d_la * (-LRU_C * sp) * gr * (1.0 - gr)
            d_gip = lam * rcb * mult * gi * (1.0 - gi)
            dgr_ref[rows, :] = d_grp.astype(dgr_ref.dtype)
            dgi_ref[rows, :] = d_gip.astype(dgi_ref.dtype)
            drc_ref[rows, :] = lam * gi * mult
            s_i = s_i + jnp.sum(d_gip, axis=0, keepdims=True)
            s_r = s_r + jnp.sum(d_grp, axis=0, keepdims=True)
            s_sp = s_sp + jnp.sum(d_la * gr, axis=0, keepdims=True)
            return mu_out, s_i, s_r, s_sp

        _, s_i, s_r, s_sp = lax.fori_loop(0, nb, step, (zero, zero, zero, zero))
        dbi_ref[...] = s_i
        dbr_ref[...] = s_r
        dl_ref[...] = (-LRU_C * s_sp) * (-_sigmoid(-lp))

    col = pl.BlockSpec((S, tc), lambda j: (0, j))
    vec = pl.BlockSpec((1, tc), lambda j: (0, j))
    return pl.pallas_call(
        body, name=name, grid=(C // tc,),
        in_specs=[col, col, col, col, col, col, vec],
        out_specs=[col, col, col, col, vec, vec, vec],
        out_shape=[_hbm_out((S, C), BF16), _hbm_out((S, C), BF16),
                   _hbm_out((S, C), BF16), _hbm_out((S, C), F32),
                   _hbm_out((1, C), F32), _hbm_out((1, C), F32),
                   _hbm_out((1, C), F32)],
        compiler_params=_params(("parallel",)),
    )(dm, proj, hrec, rc, gip, grp, lru_p)


def _cumsum_rows(name, u, reverse):
    S, C = u.shape
    nb = S // SCAN_ROWS

    def body(u_ref, o_ref):
        def step(t, carry):
            b = nb - 1 - t if reverse else t
            rows = pl.ds(pl.multiple_of(b * SCAN_ROWS, SCAN_ROWS), SCAN_ROWS)
            ub = u_ref[rows, :]
            h, carry = _block_scan(jnp.ones_like(ub), ub, carry, reverse)
            o_ref[rows, :] = h
            return carry

        lax.fori_loop(0, nb, step, jnp.zeros((1, C), F32))

    spec = pl.BlockSpec((S, C), lambda i: (0, 0))
    return pl.pallas_call(
        body, name=name, grid=(1,), in_specs=[spec], out_specs=spec,
        out_shape=_hbm_out((S, C), F32),
        compiler_params=_params(("arbitrary",)),
    )(u)


def _shift_down(x, k):
    row = lax.broadcasted_iota(jnp.int32, x.shape, 0)
    return jnp.where(row >= k, pltpu.roll(x, k, 0), 0.0)


def _shift_up(x, k):
    n = x.shape[0]
    row = lax.broadcasted_iota(jnp.int32, x.shape, 0)
    return jnp.where(row < n - k, pltpu.roll(x, n - k, 0), 0.0)


def _conv_fwd(name, proj, w, b, tc=256):
    S, C2 = proj.shape
    C = C2 // 2
    tc = _tile(C, tc)
    off = C // tc

    def body(x_ref, w_ref, b_ref, o_ref, ob_ref):
        x = x_ref[...]
        out = b_ref[...] + w_ref[3:4, :] * x
        for k in (1, 2, 3):
            out = out + w_ref[3 - k:4 - k, :] * _shift_down(x, k)
        o_ref[...] = out
        ob_ref[...] = out.astype(BF16)

    col = pl.BlockSpec((S, tc), lambda j: (0, j))
    return pl.pallas_call(
        body, name=name, grid=(C // tc,),
        in_specs=[pl.BlockSpec((S, tc), lambda j: (0, off + j)),
                  pl.BlockSpec((4, tc), lambda j: (0, j)), pl.BlockSpec((1, tc), lambda j: (0, j))],
        out_specs=[col, col],
        out_shape=[_hbm_out((S, C), F32), _hbm_out((S, C), BF16)],
        compiler_params=_params(("parallel",)),
    )(proj, w, b)


def _conv_bwd(name, drc, proj, w, tc=256):
    S, C = drc.shape
    tc = _tile(C, tc)
    off = C // tc

    def body(y_ref, x_ref, w_ref, dx_ref, dw_ref, db_ref):
        y = y_ref[...]
        x = x_ref[...]
        dx = w_ref[3:4, :] * y
        dw_ref[3:4, :] = jnp.sum(y * x, axis=0, keepdims=True)
        for k in (1, 2, 3):
            dx = dx + w_ref[3 - k:4 - k, :] * _shift_up(y, k)
            dw_ref[3 - k:4 - k, :] = jnp.sum(y * _shift_down(x, k), axis=0, keepdims=True)
        dx_ref[...] = dx.astype(dx_ref.dtype)
        db_ref[...] = jnp.sum(y, axis=0, keepdims=True)

    col = pl.BlockSpec((S, tc), lambda j: (0, j))
    return pl.pallas_call(
        body, name=name, grid=(C // tc,),
        in_specs=[col, pl.BlockSpec((S, tc), lambda j: (0, off + j)), pl.BlockSpec((4, tc), lambda j: (0, j))],
        out_specs=[col, pl.BlockSpec((4, tc), lambda j: (0, j)), pl.BlockSpec((1, tc), lambda j: (0, j))],
        out_shape=[_hbm_out((S, C), BF16), _hbm_out((4, C), F32),
                   _hbm_out((1, C), F32)],
        compiler_params=_params(("parallel",)),
    )(drc, proj, w)


def _gates_fwd(name, rcb, wg, bg):
    S, C = rcb.shape
    nblk, bw, _ = wg.shape

    def body(x_ref, w_ref, b_ref, gi_ref, gr_ref):
        g = jnp.dot(x_ref[...], w_ref[...], preferred_element_type=F32) + b_ref[...]
        gi_ref[...] = g[:, :bw]
        gr_ref[...] = g[:, bw:]

    col = pl.BlockSpec((S, bw), lambda n: (0, n))
    return pl.pallas_call(
        body, name=name, grid=(nblk,),
        in_specs=[col, pl.BlockSpec((None, bw, 2 * bw), lambda n: (n, 0, 0)),
                  pl.BlockSpec((None, 1, 2 * bw), lambda n: (n, 0, 0))],
        out_specs=[col, col],
        out_shape=[_hbm_out((S, C), F32), _hbm_out((S, C), F32)],
        compiler_params=_params(("parallel",)),
    )(rcb, wg, bg)


def _gates_bwd(name, dgi, dgr, rcb, wg, drc1):
    S, C = rcb.shape
    nblk, bw, _ = wg.shape

    def body(dgi_ref, dgr_ref, x_ref, w_ref, d1_ref, drc_ref, dw_ref):
        w = w_ref[...]
        x = x_ref[...]
        di, dr = dgi_ref[...], dgr_ref[...]
        drc_ref[...] = (d1_ref[...]
                        + lax.dot_general(di, w[:, :bw], _NT, preferred_element_type=F32)
                        + lax.dot_general(dr, w[:, bw:], _NT, preferred_element_type=F32))
        dw_ref[:, :bw] = lax.dot_general(x, di, _TN, preferred_element_type=F32).astype(dw_ref.dtype)
        dw_ref[:, bw:] = lax.dot_general(x, dr, _TN, preferred_element_type=F32).astype(dw_ref.dtype)

    col = pl.BlockSpec((S, bw), lambda n: (0, n))
    wspec = pl.BlockSpec((None, bw, 2 * bw), lambda n: (n, 0, 0))
    return pl.pallas_call(
        body, name=name, grid=(nblk,),
        in_specs=[col, col, col, wspec, col], out_specs=[col, wspec],
        out_shape=[_hbm_out((S, C), F32), _hbm_out((nblk, bw, 2 * bw), BF16)],
        compiler_params=_params(("parallel",)),
    )(dgi, dgr, rcb, wg, drc1)


def _att_tile(S):
    return next(t for t in (512, 256, 128) if S % t == 0)


def _head_lanes(shape):
    return lax.broadcasted_iota(jnp.int32, shape, len(shape) - 1) < HEAD_DIM


def _key_bias(c_blk):
    first = _head_lanes(c_blk.shape)
    rolled = pltpu.roll(c_blk, HEAD_DIM, 1)
    return jnp.where(first, c_blk, rolled), jnp.where(first, rolled, c_blk)


def _over_keys(x, op):
    n = x.shape[0]
    while n > SUBLANES:
        n //= 2
        x = op(x[:n], x[n:2 * n])
    return (jnp.max if op is jnp.maximum else jnp.sum)(x, axis=0, keepdims=True)


def _causal_t(T, cc):
    r = lax.broadcasted_iota(jnp.int32, (T, LANES), 0)
    c = lax.broadcasted_iota(jnp.int32, (T, LANES), 1) + cc * LANES
    return r <= c


def _attn_fwd(name, q, kv, cfull):
    S, D = q.shape
    HP = D // LANES
    T = _att_tile(S)
    nq = S // T
    NC = T // LANES

    def body(q_ref, k_ref, v_ref, c_ref, o_ref, of_ref, lse_ref, bias, vT, acc, m_scr, l_scr):
        def prologue(i, _):
            rows = pl.ds(pl.multiple_of(i * T, T), T)
            bias[0, rows, :], bias[1, rows, :] = _key_bias(c_ref[rows, :])
            vT[i] = v_ref[rows, :].astype(F32).T.astype(BF16)
            return 0

        lax.fori_loop(0, nq, prologue, 0)

        def q_step(qi, _):
            q0 = pl.multiple_of(qi * T, T)
            qb = q_ref[pl.ds(q0, T), :]
            m_scr[...] = jnp.full(m_scr.shape, -jnp.inf, F32)
            l_scr[...] = jnp.zeros(l_scr.shape, F32)
            acc[...] = jnp.zeros(acc.shape, F32)

            def tile(kj, masked):
                ks = pl.ds(pl.multiple_of(kj * T, T), T)
                kf = k_ref[ks, :].astype(F32)
                first = _head_lanes(kf.shape)
                kms = [jnp.where(first if hh == 0 else jnp.logical_not(first), kf, 0.0).astype(BF16) for hh in range(2)]
                sTs = [lax.dot_general(km, qb, _NT, preferred_element_type=F32) for km in kms]
                for hh in range(2):
                    b = bias[hh, ks, :]
                    ps = []
                    for cc in range(NC):
                        cols = slice(cc * LANES, (cc + 1) * LANES)
                        s = sTs[hh][:, cols] + b
                        if masked:
                            s = jnp.where(_causal_t(T, cc), s, -jnp.inf)
                        m_old = m_scr[hh, cc]
                        m_new = jnp.maximum(m_old, _over_keys(s, jnp.maximum))
                        alpha = jnp.exp(m_old - m_new)
                        p = jnp.exp(s - m_new)
                        l_scr[hh, cc] = alpha * l_scr[hh, cc] + _over_keys(p, jnp.add)
                        m_scr[hh, cc] = m_new
                        ps.append(p.astype(BF16))
                        acc[hh, :, cols] = acc[hh, :, cols] * alpha
                    acc[hh] += jnp.dot(vT[kj, hh * HEAD_DIM:(hh + 1) * HEAD_DIM, :], jnp.concatenate(ps, axis=1),
                                       preferred_element_type=F32)

            def inner(kj, _):
                tile(kj, False)
                return 0

            lax.fori_loop(0, qi, inner, 0)
            tile(qi, True)
            outs = []
            for hh in range(2):
                inv = jnp.concatenate([1.0 / l_scr[hh, cc] for cc in range(NC)], axis=1)
                outs.append(acc[hh] * inv)
                for cc in range(NC):
                    lse_ref[hh:hh + 1, pl.ds(q0 + cc * LANES, LANES)] = m_scr[hh, cc] + jnp.log(l_scr[hh, cc])
            out = jnp.concatenate(outs, axis=0).T
            o_ref[pl.ds(q0, T), :] = out.astype(o_ref.dtype)
            of_ref[pl.ds(q0, T), :] = out
            return 0

        lax.fori_loop(0, nq, q_step, 0)

    blk = lambda off: pl.BlockSpec((S, LANES), lambda p: (0, off + p))
    return pl.pallas_call(
        body, name=name, grid=(HP,),
        in_specs=[blk(0), blk(0), blk(HP), blk(0)],
        out_specs=[blk(0), blk(0), pl.BlockSpec((None, 2, S), lambda p: (p, 0, 0))],
        out_shape=[_hbm_out((S, D), BF16), _hbm_out((S, D), F32),
                   _hbm_out((HP, 2, S), F32)],
        scratch_shapes=[pltpu.VMEM((2, S, LANES), F32), pltpu.VMEM((nq, LANES, T), BF16),
                        pltpu.VMEM((2, HEAD_DIM, T), F32), pltpu.VMEM((2, NC, 1, LANES), F32),
                        pltpu.VMEM((2, NC, 1, LANES), F32)],
        compiler_params=_params(("parallel",)),
    )(q, kv, kv, cfull)


def _attn_bwd(name, q, kv, cfull, of, do, lse3):
    S, D = q.shape
    HP = D // LANES
    T = _att_tile(S)
    nq = S // T
    NC = T // LANES
    scale = HEAD_DIM ** -0.5

    def body(q_ref, k_ref, v_ref, c_ref, of_ref, do_ref, lse_ref,
             dq_ref, dk_ref, dv_ref, dck_ref, drq_ref, bias, kT, dqT, delta, dr_scr):
        def prologue(i, _):
            rows = pl.ds(pl.multiple_of(i * T, T), T)
            bias[0, rows, :], bias[1, rows, :] = _key_bias(c_ref[rows, :])
            kT[i] = k_ref[rows, :].astype(F32).T.astype(BF16)
            prodT = (do_ref[rows, :].astype(F32) * of_ref[rows, :]).T
            for hh in range(2):
                delta[hh:hh + 1, rows] = jnp.sum(prodT[hh * HEAD_DIM:(hh + 1) * HEAD_DIM], axis=0, keepdims=True)
            dqT[i] = jnp.zeros((LANES, T), F32)
            return 0

        lax.fori_loop(0, nq, prologue, 0)
        dr_scr[...] = jnp.zeros(dr_scr.shape, F32)

        def kv_step(kj, _):
            ks = pl.ds(pl.multiple_of(kj * T, T), T)
            kf = k_ref[ks, :].astype(F32)
            vf = v_ref[ks, :].astype(F32)
            first = _head_lanes(kf.shape)
            masks = [first, jnp.logical_not(first)]
            kms = [jnp.where(m, kf, 0.0).astype(BF16) for m in masks]
            vms = [jnp.where(m, vf, 0.0).astype(BF16) for m in masks]

            def tile(qi, carry, masked):
                q0 = pl.multiple_of(qi * T, T)
                qb = q_ref[pl.ds(q0, T), :]
                dob = do_ref[pl.ds(q0, T), :]
                sTs = [lax.dot_general(km, qb, _NT, preferred_element_type=F32) for km in kms]
                dpTs = [lax.dot_general(vm, dob, _NT, preferred_element_type=F32) for vm in vms]
                out = []
                for hh in range(2):
                    dk_a, dv_a, dc_a = carry[3 * hh:3 * hh + 3]
                    b = bias[hh, ks, :]
                    head = slice(hh * HEAD_DIM, (hh + 1) * HEAD_DIM)
                    ps, dss = [], []
                    for cc in range(NC):
                        cols = slice(cc * LANES, (cc + 1) * LANES)
                        at = pl.ds(q0 + cc * LANES, LANES)
                        p = jnp.exp(sTs[hh][:, cols] + b - lse_ref[hh:hh + 1, at])
                        if masked:
                            p = jnp.where(_causal_t(T, cc), p, 0.0)
                        ds = p * (dpTs[hh][:, cols] - delta[hh:hh + 1, at])
                        ps.append(p.astype(BF16))
                        dss.append(ds.astype(BF16))
                        dc_a = dc_a + ds
                        dr_scr[hh:hh + 1, at] += _over_keys(ds, jnp.add)
                    pT = jnp.concatenate(ps, axis=1)
                    dsT = jnp.concatenate(dss, axis=1)
                    dv_a = dv_a + jnp.dot(pT, dob, preferred_element_type=F32)
                    dk_a = dk_a + jnp.dot(dsT, qb, preferred_element_type=F32)
                    dqT[qi, head, :] += jnp.dot(kT[kj, head, :], dsT, preferred_element_type=F32)
                    out += [dk_a, dv_a, dc_a]
                return tuple(out)

            zero = jnp.zeros((T, LANES), F32)
            carry = tile(kj, (zero,) * 6, True)
            dk0, dv0, dc0, dk1, dv1, dc1 = lax.fori_loop(kj + 1, nq, lambda qi, c: tile(qi, c, False), carry)
            dk_ref[ks, :] = jnp.where(first, dk0, dk1)
            dv_ref[ks, :] = jnp.where(first, dv0, dv1)
            dck_ref[ks, :] = jnp.where(first, jnp.broadcast_to(-jnp.sum(dc0, axis=1, keepdims=True), (T, LANES)),
                                       jnp.broadcast_to(-jnp.sum(dc1, axis=1, keepdims=True), (T, LANES)))
            return 0

        lax.fori_loop(0, nq, kv_step, 0)

        def epilogue(i, _):
            rows = pl.ds(pl.multiple_of(i * T, T), T)
            dq_ref[rows, :] = (dqT[i].T * scale).astype(dq_ref.dtype)
            return 0

        lax.fori_loop(0, nq, epilogue, 0)
        drq_ref[...] = dr_scr[...]

    blk = lambda off: pl.BlockSpec((S, LANES), lambda p: (0, off + p))
    row_spec = pl.BlockSpec((None, 2, S), lambda p: (p, 0, 0))
    return pl.pallas_call(
        body, name=name, grid=(HP,),
        in_specs=[blk(0), blk(0), blk(HP), blk(0), blk(0), blk(0), row_spec],
        out_specs=[blk(0), blk(0), blk(0), blk(0), row_spec],
        out_shape=[_hbm_out((S, D), BF16), _hbm_out((S, D), F32),
                   _hbm_out((S, D), F32), _hbm_out((S, D), F32),
                   _hbm_out((HP, 2, S), F32)],
        scratch_shapes=[pltpu.VMEM((2, S, LANES), F32), pltpu.VMEM((nq, LANES, T), BF16),
                        pltpu.VMEM((nq, LANES, T), F32), pltpu.VMEM((2, S), F32), pltpu.VMEM((2, S), F32)],
        compiler_params=_params(("parallel",)),
    )(q, kv, kv, cfull, of, do, lse3)


def _logsig_fwd(name, f):
    S, C = f.shape

    def body(f_ref, o_ref):
        o_ref[...] = -_softplus(-f_ref[...])

    spec = pl.BlockSpec((S, C), lambda i: (0, 0))
    return pl.pallas_call(body, name=name, grid=(1,), in_specs=[spec], out_specs=spec,
                          out_shape=_hbm_out((S, C), F32),
                          compiler_params=_params(("arbitrary",)))(f)


def _logsig_bwd(name, dls, f):
    S, C = f.shape

    def body(d_ref, f_ref, o_ref, s_ref):
        df = d_ref[...] * _sigmoid(-f_ref[...])
        o_ref[...] = df.astype(o_ref.dtype)
        s_ref[...] = jnp.sum(df, axis=0, keepdims=True)

    spec = pl.BlockSpec((S, C), lambda i: (0, 0))
    return pl.pallas_call(body, name=name, grid=(1,), in_specs=[spec, spec],
                          out_specs=[spec, pl.BlockSpec((1, C), lambda i: (0, 0))],
                          out_shape=[_hbm_out((S, C), BF16), _hbm_out((1, C), F32)],
                          compiler_params=_params(("arbitrary",)))(dls, f)


def _add_cast(name, parts, out_dtype, tr=256):
    S, C = parts[0].shape
    tr = _tile(S, tr)
    n = len(parts)

    def body(*refs):
        acc = refs[0][...].astype(F32)
        for r in refs[1:n]:
            acc = acc + r[...].astype(F32)
        refs[n][...] = acc.astype(out_dtype)

    spec = pl.BlockSpec((tr, C), lambda i: (i, 0))
    return pl.pallas_call(body, name=name, grid=(S // tr,), in_specs=[spec] * n, out_specs=spec,
                          out_shape=_hbm_out((S, C), out_dtype),
                          compiler_params=_params(("parallel",)))(*parts)


def _local_step(x, target, gains, layer_weights, layer_prefetch, layer_grads):
    S, D = x.shape
    HP = D // LANES
    scale = HEAD_DIM ** -0.5
    tm = _tile(S, 512)
    tx = _tile(S, 256)
    td = _tile(D, 512)
    saved = []
    h = x
    l = 0
    kv = cfull = f_pre = hn_kv = h_kv = None
    while True:
        W = layer_weights(l, "mix", h)
        if W is None:
            break
        recurrent = "w_rec_in" in W
        if l == 0:
            xn = _rmsnorm_fwd("mix_norm_0", h, gains["mix"][0])
        if recurrent:
            CH = W["w_rec_in"].shape[-1]
            C = 2 * CH
            proj = _mm(f"rec_in_{l}", "nn", xn, W["w_rec_in"], grid=(S // tm, N_CHIPS),
                       a_spec=pl.BlockSpec((tm, D), lambda i, j: (i, 0)),
                       b_spec=pl.BlockSpec((None, D, CH), lambda i, j: (j, 0, 0)),
                       out_shape=(S, 2 * C), out_dtype=F32,
                       out_spec=pl.BlockSpec((tm, CH), lambda i, j: (i, j)))
            rc, rcb = _conv_fwd(f"conv_{l}", proj, W["conv_w"], W["conv_b"])
            layer_prefetch(l, "ffn", rcb)
            gip, grp = _gates_fwd(f"gates_{l}", rcb, W["w_gates"], W["b_gates"])
            hrec, m = _lru_fwd(f"lru_{l}", proj, rc, gip, grp, W["lru_param"])
            h_mid, hn = _mm_nn(f"rec_out_{l}", m, W["w_rec_out"], out_dtype=F32, res=h, tn=D, norm_gain=gains["ffn"][l])
            mix_saved = (xn, proj, rc, rcb, gip, grp, hrec, m)
        else:
            if "w_kv" in W:
                h_kv = h
                hn_kv = _rmsnorm_fwd("kv_norm", h, W["norm_kv"])
                kv = _mm_nn("kv_proj", hn_kv, W["w_kv"], out_dtype=BF16)
                f_pre = _mm_nn("f_proj", hn_kv, W["w_f"], out_dtype=F32, bias=W["b_f"])
                c = _cumsum_rows("c_cumsum", _logsig_fwd("logsig", f_pre), False)
                cfull = jnp.repeat(-c[:, :2 * HP], HEAD_DIM, axis=1)
            q = _mm_nn(f"q_proj_{l}", xn, W["w_q"], out_dtype=BF16, scale=scale)
            layer_prefetch(l, "ffn", q)
            o, of, lse = _attn_fwd(f"attn_fwd_{l}", q, kv, cfull)
            h_mid, hn = _mm_nn(f"o_proj_{l}", o, W["w_o"], out_dtype=F32, res=h, tn=D, norm_gain=gains["ffn"][l])
            mix_saved = (xn, q, o, of, lse)
        W = {**W, **layer_weights(l, "ffn", h_mid)}
        z3, act = _swiglu_fwd(f"ffn_in_{l}", hn, W["w_ffn_in"])
        layer_prefetch(l + 1, "mix", act)
        saved.append((W, h, h_mid, mix_saved, (hn, z3, act)))
        l += 1
        if l < len(gains["mix"]):
            h, xn = _mm_nn(f"ffn_out_{l - 1}", act, W["w_ffn_out"], out_dtype=F32, res=h_mid, tn=D,
                           norm_gain=gains["mix"][l])
        else:
            h = _mm_nn(f"ffn_out_{l - 1}", act, W["w_ffn_out"], out_dtype=F32, res=h_mid, tn=D)

    dh, dhb, dg_final, loss_row = _loss_head("loss_head", h, target, gains["final"])

    dk_parts, dv_parts, dc_parts = [], [], []
    token = None
    for l in reversed(range(len(saved))):
        W, h_in, h_mid, mix_saved, (hn, z3, act) = saved[l]
        recurrent = "w_rec_in" in W
        FH = W["w_ffn_in"].shape[-1]
        G = {}
        norm_ffn = gains["ffn"][l]
        if token is not None:
            norm_ffn = norm_ffn + jnp.minimum(token[:1, :1], 0.0)
        G["w_ffn_out"] = _mm_tn(f"d_ffn_out_{l}", act, dhb, out_dtype=BF16, tn=D)
        dz3 = _swiglu_bwd(f"d_act_{l}", dhb, W["w_ffn_out"], z3)
        G["w_ffn_in"] = _mm(
            f"d_ffn_in_{l}", "tn", hn, dz3, grid=(D // td, N_CHIPS),
            a_spec=pl.BlockSpec((S, td), lambda i, j: (0, i)),
            b_spec=pl.BlockSpec((None, S, FH), lambda i, j: (j // 2, 0, j % 2)),
            out_shape=(N_CHIPS, D, FH), out_dtype=BF16,
            out_spec=pl.BlockSpec((None, td, FH), lambda i, j: (j, i, 0)))
        token = layer_grads(l, "ffn", G)
        G = {}
        norm_ffn = norm_ffn + jnp.minimum(token[:1, :1], 0.0)
        dh, dhb, dgp = _mm(f"d_ffn_hn_{l}", "nt", dz3, W["w_ffn_in"], grid=(S // tx, 1),
                           a_spec=[pl.BlockSpec((None, tx, FH), functools.partial(lambda i, j, k: (k // 2, i, k % 2), k=k))
                                   for k in range(N_CHIPS)],
                           b_spec=[pl.BlockSpec((None, D, FH), functools.partial(lambda i, j, k: (k, 0, 0), k=k))
                                   for k in range(N_CHIPS)],
                           out_shape=(S, D), out_dtype=F32, out_spec=pl.BlockSpec((tx, D), lambda i, j: (i, 0)),
                           norm_bwd=(h_mid, norm_ffn, dh))
        G["norm_ffn"] = jnp.sum(dgp, axis=0)
        if recurrent:
            CH = W["w_rec_in"].shape[-1]
            C = 2 * CH
            xn, proj, rc, rcb, gip, grp, hrec, m = mix_saved
            G["w_rec_out"] = _mm_tn(f"d_rec_out_{l}", m, dhb, out_dtype=BF16, tn=D)
            dm = _mm_nt(f"d_m_{l}", dhb, W["w_rec_out"], out_dtype=F32, tn=C)
            dgb, dgi, dgr, drc1, G["b_gi"], G["b_gr"], G["lru_param"] = _lru_bwd(
                f"d_lru_{l}", dm, proj, hrec, rc, gip, grp, W["lru_param"])
            drc, G["w_gates"] = _gates_bwd(f"d_gates_{l}", dgi, dgr, rcb, W["w_gates"], drc1)
            drec, G["conv_w"], G["conv_b"] = _conv_bwd(f"d_conv_{l}", drc, proj, W["conv_w"])
            dproj = jnp.concatenate([dgb, drec], axis=1)
            G["w_rec_in"] = _mm(
                f"d_rec_in_{l}", "tn", xn, dproj, grid=(1, N_CHIPS),
                a_spec=pl.BlockSpec((S, D), lambda i, j: (0, 0)),
                b_spec=pl.BlockSpec((S, CH), lambda i, j: (0, j)),
                out_shape=(N_CHIPS, D, CH), out_dtype=BF16,
                out_spec=pl.BlockSpec((None, D, CH), lambda i, j: (j, 0, 0)))
            dh, dhb, dgp = _mm(f"d_rec_xn_{l}", "nt", dproj, W["w_rec_in"], grid=(S // tx, 1),
                               a_spec=[pl.BlockSpec((tx, CH), functools.partial(lambda i, j, k: (i, k), k=k))
                                       for k in range(N_CHIPS)],
                               b_spec=[pl.BlockSpec((None, D, CH), functools.partial(lambda i, j, k: (k, 0, 0), k=k))
                                       for k in range(N_CHIPS)],
                               out_shape=(S, D), out_dtype=F32, out_spec=pl.BlockSpec((tx, D), lambda i, j: (i, 0)),
                               norm_bwd=(h_in, gains["mix"][l], dh))
        else:
            xn, q, o, of, lse = mix_saved
            G["w_o"] = _mm_tn(f"d_o_proj_{l}", o, dhb, out_dtype=BF16, tn=D)
            do = _mm_nt(f"d_o_{l}", dhb, W["w_o"], out_dtype=BF16, tn=D)
            dq, dk, dv, dck, drq = _attn_bwd(f"attn_bwd_{l}", q, kv, cfull, of, do, lse)
            dk_parts.append(dk)
            dv_parts.append(dv)
            dc_parts.append(dck[:, ::HEAD_DIM] + drq.reshape(2 * HP, S).T)
            G["w_q"] = _mm_tn(f"d_q_proj_{l}", xn, dq, out_dtype=BF16, tn=D)
            dh, dhb, dgp = _mm_nt(f"d_q_xn_{l}", dq, W["w_q"], out_dtype=F32, tn=D, norm_bwd=(h_in, gains["mix"][l], dh))
        G["norm_mix"] = jnp.sum(dgp, axis=0)
        if "w_kv" in W:
            dkb = _add_cast("dk_sum", dk_parts, BF16)
            dvb = _add_cast("dv_sum", dv_parts, BF16)
            dkv = jnp.concatenate([dkb, dvb], axis=1)
            dc = sum(dc_parts[1:], dc_parts[0])
            dc_pad = jnp.pad(dc, ((0, 0), (0, LANES - 2 * HP)))
            dls = _cumsum_rows("dc_cumsum", dc_pad, True)
            dfb, G["b_f"] = _logsig_bwd("d_logsig", dls, f_pre)
            G["w_kv"] = _mm_tn("d_kv_proj", hn_kv, dkv, out_dtype=BF16)
            G["w_f"] = _mm_tn("d_f_proj", hn_kv, dfb, out_dtype=F32)
            dhn_f = _mm_nt("d_f_hn", dfb, W["w_f"], out_dtype=F32, tn=D)
            dh, dhb, dgp = _mm_nt("d_kv_hn", dkv, W["w_kv"], out_dtype=F32, tn=D, res=dhn_f,
                                  norm_bwd=(h_kv, W["norm_kv"], dh))
            G["norm_kv"] = jnp.sum(dgp, axis=0)
        token = layer_grads(l, "mix", G)
    return loss_row, dh, dg_final


_ANY = pl.BlockSpec(memory_space=pl.ANY)


def _position():
    return lax.axis_index("x"), lax.axis_index("y"), lax.axis_index("c")


def _chip_peers(x, y):
    return [(1 - x, y), (x, 1 - y), (1 - x, 1 - y)]


def _half_rows(c, n):
    h = n // 2
    assert h % 16 == 0
    return pl.ds(pl.multiple_of(c * h, 16), h)


def _place_own(name, shard, layer, me):
    _, R, C = shard.shape
    tr = _row_tile(R, C, 2 * shard.dtype.itemsize, target=8 << 20)

    def body(me_ref, x_ref, o_ref):
        o_ref[...] = x_ref[...]

    return pl.pallas_call(
        body, name=name,
        grid_spec=pltpu.PrefetchScalarGridSpec(
            num_scalar_prefetch=1, grid=(R // tr,),
            in_specs=[pl.BlockSpec((None, tr, C), lambda i, me_ref: (layer, i, 0))],
            out_specs=pl.BlockSpec((None, tr, C), lambda i, me_ref: (me_ref[0], i, 0))),
        out_shape=_hbm_out((N_CHIPS, R, C), shard.dtype),
        compiler_params=_params(("parallel",)),
    )(me, shard)


def _gather_smalls(name, smalls):
    ns = len(smalls)

    def body(*refs):
        ins, outs = refs[:ns], refs[ns:2 * ns]
        send_sems, recv_sems, local_sems = refs[2 * ns:]
        x, y, c = _position()
        me = 2 * x + y
        peers = _chip_peers(x, y)

        def remote(t, k, chip):
            px, py = peers[k]
            return pltpu.make_async_remote_copy(
                src_ref=ins[t], dst_ref=outs[t].at[chip], send_sem=send_sems.at[3 * t + k],
                recv_sem=recv_sems.at[3 * t + k], device_id=(px, py, c), device_id_type=MESH)

        local = [pltpu.make_async_copy(ins[t], outs[t].at[me], local_sems.at[t]) for t in range(ns)]
        for t in range(ns):
            local[t].start()
            for k in range(3):
                remote(t, k, me).start()
        for t in range(ns):
            for k in range(3):
                px, py = peers[k]
                remote(t, k, 2 * px + py).wait_recv()
        for t in range(ns):
            for k in range(3):
                remote(t, k, me).wait_send()
            local[t].wait()

    return pl.pallas_call(
        body, name=name, in_specs=[_ANY] * ns, out_specs=[_ANY] * ns,
        out_shape=[_hbm_out((N_CHIPS,) + s.shape, s.dtype) for s in smalls],
        scratch_shapes=[pltpu.SemaphoreType.DMA((3 * ns,)), pltpu.SemaphoreType.DMA((3 * ns,)),
                        pltpu.SemaphoreType.DMA((ns,))],
    )(*smalls)


_SEM = pl.BlockSpec(memory_space=pltpu.SEMAPHORE)
_SPLIT = pltpu.CompilerParams(has_side_effects=pltpu.SideEffectType.DATAFLOW_SIDE_EFFECTING)


def _weight_copy(shards, buf, items, sems, i, k, chip_of_dst, peers, c):
    w, l = items[i]
    px, py = peers[k]
    half = _half_rows(c, shards[w].shape[1])
    return pltpu.make_async_remote_copy(
        src_ref=shards[w].at[l, half], dst_ref=buf.at[chip_of_dst, half],
        send_sem=sems[0].at[3 * i + k], recv_sem=sems[1].at[3 * i + k],
        device_id=(px, py, c), device_id_type=MESH)


def _gather_start(name, shards, bufs, items, after):
    nw, n = len(shards), len(bufs)

    def body(*refs):
        ins, outs, sems = refs[:nw], refs[nw + n + 1:nw + 2 * n + 1], refs[nw + 2 * n + 1:]
        x, y, c = _position()
        peers = _chip_peers(x, y)
        for i in range(n):
            for k in range(3):
                _weight_copy(ins, outs[i], items, sems, i, k, 2 * x + y, peers, c).start()

    res = pl.pallas_call(
        body, name=name, in_specs=[_ANY] * (nw + n + 1), out_specs=[_ANY] * n + [_SEM, _SEM],
        out_shape=[_hbm_out(b.shape, b.dtype) for b in bufs]
        + [pltpu.SemaphoreType.DMA((3 * n,)), pltpu.SemaphoreType.DMA((3 * n,))],
        input_output_aliases={nw + i: i for i in range(n)}, compiler_params=_SPLIT,
    )(*shards, *bufs, after)
    return res[:n], res[n:]


def _gather_wait(name, shards, bufs, items, ids, sems, after):
    nw, m = len(shards), len(ids)

    def body(*refs):
        ins, bs = refs[:nw], refs[nw:nw + m]
        sem_refs = refs[nw + m:nw + m + 2]
        x, y, c = _position()
        peers = _chip_peers(x, y)
        for j, i in enumerate(ids):
            for k in range(3):
                px, py = peers[k]
                _weight_copy(ins, bs[j], items, sem_refs, i, k, 2 * px + py, peers, c).wait_recv()
        for j, i in enumerate(ids):
            for k in range(3):
                _weight_copy(ins, bs[j], items, sem_refs, i, k, 2 * x + y, peers, c).wait_send()

    res = pl.pallas_call(
        body, name=name, in_specs=[_ANY] * (nw + m) + [_SEM, _SEM, _ANY], out_specs=[_ANY] * m,
        out_shape=[_hbm_out(bufs[i].shape, bufs[i].dtype) for i in ids],
        input_output_aliases={nw + j: j for j in range(m)}, compiler_params=_SPLIT,
    )(*shards, *[bufs[i] for i in ids], *sems, after)
    return list(res)


def _forward_copy(src, dst, sems, i, k, core):
    x, y, c = _position()
    px, py = _chip_peers(x, y)[k]
    half = _half_rows(core, src.shape[1])
    return pltpu.make_async_remote_copy(
        src_ref=src.at[2 * px + py, half], dst_ref=dst.at[2 * px + py, half],
        send_sem=sems[0].at[3 * i + k], recv_sem=sems[1].at[3 * i + k],
        device_id=(x, y, 1 - c), device_id_type=MESH)


def _forward_start(name, bufs):
    n = len(bufs)

    def body(*refs):
        ins, outs, sems = refs[:n], refs[n:2 * n], refs[2 * n:]
        c = lax.axis_index("c")
        for i in range(n):
            for k in range(3):
                _forward_copy(ins[i], outs[i], sems, i, k, c).start()

    res = pl.pallas_call(
        body, name=name, in_specs=[_ANY] * n, out_specs=[_ANY] * n + [_SEM, _SEM],
        out_shape=[_hbm_out(g.shape, g.dtype) for g in bufs]
        + [pltpu.SemaphoreType.DMA((3 * n,)), pltpu.SemaphoreType.DMA((3 * n,))],
        input_output_aliases={i: i for i in range(n)}, compiler_params=_SPLIT,
    )(*bufs)
    return list(res[:n]), res[n:]


def _forward_wait(name, bufs, sems, after):
    n = len(bufs)

    def body(*refs):
        bs, sem_refs = refs[:n], refs[n:n + 2]
        c = lax.axis_index("c")
        for i in range(n):
            for k in range(3):
                _forward_copy(bs[i], bs[i], sem_refs, i, k, 1 - c).wait_recv()
        for i in range(n):
            for k in range(3):
                _forward_copy(bs[i], bs[i], sem_refs, i, k, c).wait_send()

    return list(pl.pallas_call(
        body, name=name, in_specs=[_ANY] * n + [_SEM, _SEM, _ANY], out_specs=[_ANY] * n,
        out_shape=[_hbm_out(g.shape, g.dtype) for g in bufs],
        input_output_aliases={i: i for i in range(n)}, compiler_params=_SPLIT,
    )(*bufs, *sems, after))


def _reduce_copy(grads, others, sems, i):
    x, y, c = _position()
    return pltpu.make_async_remote_copy(
        src_ref=grads[i].at[:, _half_rows(1 - c, grads[i].shape[1])], dst_ref=others[i],
        send_sem=sems[0].at[i], recv_sem=sems[1].at[i], device_id=(x, y, 1 - c), device_id_type=MESH)


def _reduce_start(name, grads):
    n = len(grads)

    def body(*refs):
        ins, outs, sems, token = refs[:n], refs[n:2 * n], refs[2 * n:2 * n + 2], refs[2 * n + 2]
        for i in range(n):
            _reduce_copy(ins, outs, sems, i).start()
        token[...] = jnp.zeros_like(token)

    res = pl.pallas_call(
        body, name=name, in_specs=[_ANY] * n,
        out_specs=[_ANY] * n + [_SEM, _SEM, pl.BlockSpec(memory_space=pltpu.VMEM)],
        out_shape=[_hbm_out((N_CHIPS, g.shape[1] // 2, g.shape[2]), g.dtype) for g in grads]
        + [pltpu.SemaphoreType.DMA((n,)), pltpu.SemaphoreType.DMA((n,)), jax.ShapeDtypeStruct((SUBLANES, LANES), F32)],
        compiler_params=_SPLIT,
    )(*grads)
    return list(res[:n]), res[n:n + 2], res[n + 2]


def _reduce_wait(name, grads, others, sems):
    n = len(grads)

    def body(*refs):
        ins, os_, sem_refs = refs[:n], refs[n:2 * n], refs[2 * n:2 * n + 2]
        for i in range(n):
            _reduce_copy(ins, os_, sem_refs, i).wait_recv()
        for i in range(n):
            _reduce_copy(ins, os_, sem_refs, i).wait_send()

    return list(pl.pallas_call(
        body, name=name, in_specs=[_ANY] * (2 * n) + [_SEM, _SEM], out_specs=[_ANY] * n,
        out_shape=[_hbm_out(o.shape, o.dtype) for o in others],
        input_output_aliases={n + i: i for i in range(n)}, compiler_params=_SPLIT,
    )(*grads, *others, *sems))


def _sum_cores(name, g, other, core):
    _, R, C = g.shape
    H = R // 2
    tr = _row_tile(H, C, 3 * 2, target=12 << 20)
    nb = H // tr

    def body(c_ref, g_ref, o_ref, out_ref):
        out_ref[...] = (g_ref[...].astype(F32) + o_ref[...].astype(F32)).astype(out_ref.dtype)

    return pl.pallas_call(
        body, name=name,
        grid_spec=pltpu.PrefetchScalarGridSpec(
            num_scalar_prefetch=1, grid=(N_CHIPS, nb),
            in_specs=[pl.BlockSpec((None, tr, C), lambda j, i, c_ref: (j, c_ref[0] * nb + i, 0)),
                      pl.BlockSpec((None, tr, C), lambda j, i, c_ref: (j, i, 0))],
            out_specs=pl.BlockSpec((None, tr, C), lambda j, i, c_ref: (j, i, 0))),
        out_shape=_hbm_out((N_CHIPS, H, C), BF16),
        compiler_params=_params(("parallel", "parallel")),
    )(core, g, other)


def _sum_chips(name, received, own, full, layer, me_core):
    _, H, C = received.shape
    tr = _row_tile(H, C, 3 * 2 + 2 + 4, target=12 << 20)
    nb = H // tr

    def body(s_ref, r_ref, own_ref, full_ref, out_ref):
        acc = r_ref[0].astype(F32)
        for k in (1, 2):
            acc = acc + r_ref[k].astype(F32)
        out_ref[...] = acc + own_ref[...].astype(F32)

    return pl.pallas_call(
        body, name=name,
        grid_spec=pltpu.PrefetchScalarGridSpec(
            num_scalar_prefetch=1, grid=(nb,),
            in_specs=[pl.BlockSpec((3, tr, C), lambda i, s_ref: (0, i, 0)),
                      pl.BlockSpec((None, tr, C), lambda i, s_ref: (s_ref[0], i, 0)),
                      _ANY],
            out_specs=pl.BlockSpec((None, tr, C), lambda i, s_ref: (layer, s_ref[1] * nb + i, 0))),
        out_shape=_hbm_out(full.shape, full.dtype),
        input_output_aliases={3: 0},
        compiler_params=_params(("parallel",)),
    )(me_core, received, own, full)


def _part_copy(parts, recv, sems, i, k, peers, c):
    px, py = peers[k]
    return pltpu.make_async_remote_copy(
        src_ref=parts[i].at[2 * px + py], dst_ref=recv[i].at[k],
        send_sem=sems[0].at[3 * i + k], recv_sem=sems[1].at[3 * i + k],
        device_id=(px, py, c), device_id_type=MESH)


def _scatter_start(name, parts):
    n = len(parts)

    def body(*refs):
        ins, outs, sems, token = refs[:n], refs[n:2 * n], refs[2 * n:2 * n + 2], refs[2 * n + 2]
        x, y, c = _position()
        peers = _chip_peers(x, y)
        for i in range(n):
            for k in range(3):
                _part_copy(ins, outs, sems, i, k, peers, c).start()
        token[...] = jnp.zeros_like(token)

    res = pl.pallas_call(
        body, name=name, in_specs=[_ANY] * n,
        out_specs=[_ANY] * n + [_SEM, _SEM, pl.BlockSpec(memory_space=pltpu.VMEM)],
        out_shape=[_hbm_out((3,) + p.shape[1:], p.dtype) for p in parts]
        + [pltpu.SemaphoreType.DMA((3 * n,)), pltpu.SemaphoreType.DMA((3 * n,)),
           jax.ShapeDtypeStruct((SUBLANES, LANES), F32)],
        compiler_params=_SPLIT,
    )(*parts)
    return list(res[:n]), res[n:n + 2], res[n + 2]


def _scatter_wait(name, parts, recv, sems):
    n = len(parts)

    def body(*refs):
        ins, rs, sem_refs = refs[:n], refs[n:2 * n], refs[2 * n:2 * n + 2]
        x, y, c = _position()
        peers = _chip_peers(x, y)
        for i in range(n):
            for k in range(3):
                _part_copy(ins, rs, sem_refs, i, k, peers, c).wait_recv()
        for i in range(n):
            for k in range(3):
                _part_copy(ins, rs, sem_refs, i, k, peers, c).wait_send()

    return list(pl.pallas_call(
        body, name=name, in_specs=[_ANY] * (2 * n) + [_SEM, _SEM], out_specs=[_ANY] * n,
        out_shape=[_hbm_out(r.shape, r.dtype) for r in recv],
        input_output_aliases={n + i: i for i in range(n)}, compiler_params=_SPLIT,
    )(*parts, *recv, *sems))


def _share_d2d(name, full):
    n = len(full)

    def body(*refs):
        ins, outs = refs[:n], refs[n:2 * n]
        send_sems, recv_sems = refs[2 * n:]
        x, y, c = _position()

        def remote(w, core):
            half = _half_rows(core, ins[w].shape[1])
            return pltpu.make_async_remote_copy(
                src_ref=ins[w].at[:, half], dst_ref=outs[w].at[:, half],
                send_sem=send_sems.at[w], recv_sem=recv_sems.at[w],
                device_id=(x, y, 1 - c), device_id_type=MESH)

        for w in range(n):
            remote(w, c).start()
        for w in range(n):
            remote(w, 1 - c).wait_recv()
        for w in range(n):
            remote(w, c).wait_send()

    return pl.pallas_call(
        body, name=name, in_specs=[_ANY] * n, out_specs=[_ANY] * n,
        out_shape=[_hbm_out(f.shape, f.dtype) for f in full],
        input_output_aliases={w: w for w in range(n)},
        scratch_shapes=[pltpu.SemaphoreType.DMA((n,)), pltpu.SemaphoreType.DMA((n,))],
    )(*full)


def _gather_all(name, a):
    def body(a_ref, o_ref, send_sems, recv_sems, local_sem):
        x, y, c = _position()
        me = 4 * x + 2 * y + c

        def peer(k):
            return (x ^ ((k >> 2) & 1), y ^ ((k >> 1) & 1), c ^ (k & 1))

        def remote(k, slot):
            return pltpu.make_async_remote_copy(
                src_ref=a_ref, dst_ref=o_ref.at[slot], send_sem=send_sems.at[k - 1], recv_sem=recv_sems.at[k - 1],
                device_id=peer(k), device_id_type=MESH)

        local = pltpu.make_async_copy(a_ref, o_ref.at[me], local_sem)
        local.start()
        for k in range(1, N_DEV):
            remote(k, me).start()
        for k in range(1, N_DEV):
            px, py, pc = peer(k)
            remote(k, 4 * px + 2 * py + pc).wait_recv()
        for k in range(1, N_DEV):
            remote(k, me).wait_send()
        local.wait()

    return pl.pallas_call(
        body, name=name, in_specs=[_ANY], out_specs=_ANY,
        out_shape=_hbm_out((N_DEV,) + a.shape, a.dtype),
        scratch_shapes=[pltpu.SemaphoreType.DMA((N_DEV - 1,)), pltpu.SemaphoreType.DMA((N_DEV - 1,)),
                        pltpu.SemaphoreType.DMA],
    )(a)


def _rows2d(a, lead=0):
    return a.reshape(a.shape[:lead] + (-1, a.shape[-1]))


def _row_tile(rows, cols, itemsize=4, target=1 << 20):
    want = max(SUBLANES, target // (cols * itemsize))
    t = min(rows, (want // 16) * 16)
    while t > 16 and rows % t:
        t -= 16
    return t if rows % t == 0 else rows


def _sum_slots(name, r, out_dtype=F32):
    ns = r.shape[0]
    r2 = _rows2d(r, 1)
    _, rows, cols = r2.shape
    tr = _row_tile(rows, cols)

    def body(r_ref, o_ref):
        acc = r_ref[0].astype(F32)
        for s in range(1, ns):
            acc = acc + r_ref[s].astype(F32)
        o_ref[...] = acc.astype(o_ref.dtype)

    out = pl.pallas_call(
        body, name=name, grid=(rows // tr,),
        in_specs=[pl.BlockSpec((ns, tr, cols), lambda i: (0, i, 0))],
        out_specs=pl.BlockSpec((tr, cols), lambda i: (i, 0)),
        out_shape=_hbm_out((rows, cols), out_dtype),
        compiler_params=_params(("parallel",)),
    )(r2)
    return out.reshape(r.shape[1:])


def _adamw(name, g_parts, w, m, v):
    shape = w.shape
    ng = len(g_parts)
    args = [_rows2d(a) for a in (*g_parts, w, m, v)]
    rows, cols = args[0].shape
    tr = _row_tile(rows, cols, (ng + 7) * 4, target=16 << 20)
    c1 = 1.0 - ADAM_B1 ** ADAM_STEP
    c2 = 1.0 - ADAM_B2 ** ADAM_STEP

    def body(*refs):
        g = refs[0][...]
        for r in refs[1:ng]:
            g = g + r[...]
        w_ref, m_ref, v_ref = refs[ng:ng + 3]
        g_out, d_out, m_out, v_out = refs[ng + 3:]
        mn = ADAM_B1 * m_ref[...] + (1.0 - ADAM_B1) * g
        vn = ADAM_B2 * v_ref[...] + (1.0 - ADAM_B2) * (g * g)
        m_hat = mn / c1
        v_hat = vn / c2
        g_out[...] = g
        d_out[...] = -ADAM_LR * (m_hat / (jnp.sqrt(v_hat) + ADAM_EPS) + ADAM_WD * w_ref[...])
        m_out[...] = mn
        v_out[...] = vn

    spec = pl.BlockSpec((tr, cols), lambda i: (i, 0))
    outs = pl.pallas_call(
        body, name=name, grid=(rows // tr,), in_specs=[spec] * (ng + 3), out_specs=[spec] * 4,
        out_shape=[_hbm_out((rows, cols), F32)] * 4,
        compiler_params=_params(("parallel",)),
    )(*args)
    return tuple(o.reshape(shape) for o in outs)


_WEIGHTS = ["norm_mix", "norm_ffn", "w_ffn_in", "w_ffn_out", "w_rec_in", "conv_w", "conv_b", "w_lru_gates",
            "b_lru_gates", "lru_param", "w_rec_out", "norm_kv", "w_kvf", "b_forget", "w_q", "w_o", "norm_final"]
_BIG = ["w_ffn_in", "w_ffn_out", "w_rec_in", "w_lru_gates", "w_rec_out", "w_kvf", "w_q", "w_o"]


def _stack3(a):
    return a[None] if a.ndim == 2 else a.reshape(a.shape[0], -1, a.shape[-1])


def _pad_lanes(a, n):
    return jnp.pad(a, ((0, 0),) * (a.ndim - 1) + ((0, n - a.shape[-1]),))


def kernel(x, norm_mix, norm_ffn, w_ffn_in, w_ffn_out, w_rec_in, conv_w, conv_b, w_lru_gates, b_lru_gates, lru_param, w_rec_out, norm_kv, w_kvf, b_forget, w_q, w_o, norm_final, loss_target, m_norm_mix, m_norm_ffn, m_w_ffn_in, m_w_ffn_out, m_w_rec_in, m_conv_w, m_conv_b, m_w_lru_gates, m_b_lru_gates, m_lru_param, m_w_rec_out, m_norm_kv, m_w_kvf, m_b_forget, m_w_q, m_w_o, m_norm_final, v_norm_mix, v_norm_ffn, v_w_ffn_in, v_w_ffn_out, v_w_rec_in, v_conv_w, v_conv_b, v_w_lru_gates, v_b_lru_gates, v_lru_param, v_w_rec_out, v_norm_kv, v_w_kvf, v_b_forget, v_w_q, v_w_o, v_norm_final):
    P = dict(norm_mix=norm_mix, norm_ffn=norm_ffn, w_ffn_in=w_ffn_in, w_ffn_out=w_ffn_out, w_rec_in=w_rec_in,
             conv_w=conv_w, conv_b=conv_b, w_lru_gates=w_lru_gates, b_lru_gates=b_lru_gates, lru_param=lru_param,
             w_rec_out=w_rec_out, norm_kv=norm_kv, w_kvf=w_kvf, b_forget=b_forget, w_q=w_q, w_o=w_o,
             norm_final=norm_final)
    M1 = dict(norm_mix=m_norm_mix, norm_ffn=m_norm_ffn, w_ffn_in=m_w_ffn_in, w_ffn_out=m_w_ffn_out,
              w_rec_in=m_w_rec_in, conv_w=m_conv_w, conv_b=m_conv_b, w_lru_gates=m_w_lru_gates,
              b_lru_gates=m_b_lru_gates, lru_param=m_lru_param, w_rec_out=m_w_rec_out, norm_kv=m_norm_kv,
              w_kvf=m_w_kvf, b_forget=m_b_forget, w_q=m_w_q, w_o=m_w_o, norm_final=m_norm_final)
    M2 = dict(norm_mix=v_norm_mix, norm_ffn=v_norm_ffn, w_ffn_in=v_w_ffn_in, w_ffn_out=v_w_ffn_out,
              w_rec_in=v_w_rec_in, conv_w=v_conv_w, conv_b=v_conv_b, w_lru_gates=v_w_lru_gates,
              b_lru_gates=v_b_lru_gates, lru_param=v_lru_param, w_rec_out=v_w_rec_out, norm_kv=v_norm_kv,
              w_kvf=v_w_kvf, b_forget=v_b_forget, w_q=v_w_q, w_o=v_w_o, norm_final=v_norm_final)

    _, S, D = x.shape
    L = norm_mix.shape[0]
    NA, NBLK, BW, GS = w_lru_gates.shape
    NB = w_q.shape[0]
    C = NBLK * BW
    CS = C // N_CHIPS
    H = b_forget.shape[0]
    assert C == D and H * HEAD_DIM == D and H <= LANES
    chip = 2 * lax.axis_index("x") + lax.axis_index("y")

    small_a = jnp.concatenate([conv_w, conv_b[:, None], lru_param[:, None]], axis=1)
    small_a, b_gates = _gather_smalls("gather_smalls", [small_a, b_lru_gates])
    small_a = small_a.transpose(1, 2, 0, 3).reshape(NA, 6, C)
    b_gates = b_gates.transpose(1, 2, 0, 3).reshape(NA, NBLK, 1, N_CHIPS * GS)
    shards = [_stack3(P[w]).astype(BF16) for w in _BIG]
    core = lax.axis_index("c")
    chip_id = jnp.reshape(chip, (1,)).astype(jnp.int32)
    core_id = jnp.reshape(core, (1,)).astype(jnp.int32)
    me_core = jnp.stack([chip, core]).astype(jnp.int32)

    def stage_items(l, part):
        if part == "ffn":
            return [(_BIG.index("w_ffn_in"), l), (_BIG.index("w_ffn_out"), l)]
        if l < NA:
            names, at = ["w_rec_in", "w_lru_gates", "w_rec_out"], l
        else:
            names, at = (["w_kvf"] if l == NA else []) + ["w_q", "w_o"], l - NA
        return [(_BIG.index(n), 0 if n == "w_kvf" else at) for n in names]

    stages = [(l, part) for l in range(L) for part in ("mix", "ffn")]
    items = [it for st in stages for it in stage_items(*st)]
    ids_of = {st: [items.index(it) for it in stage_items(*st)] for st in stages}
    bufs = [_place_own(f"place_{_BIG[w]}_{li}", shards[w], li, chip_id) for w, li in items]
    bufs, gather_sems = _gather_start("gather_start", shards, bufs, items, small_a)

    forwarding = {}

    def layer_prefetch(l, part, after):
        if l < L and (l, part) not in forwarding:
            ids = ids_of[(l, part)]
            got = _gather_wait(f"gather_wait_{part}_{l}", shards, bufs, items, ids, gather_sems, after)
            forwarding[(l, part)] = _forward_start(f"forward_start_{part}_{l}", got)

    def layer_weights(l, part, after):
        if l >= L:
            return None
        layer_prefetch(l, part, after)
        ids = ids_of[(l, part)]
        got, sems = forwarding[(l, part)]
        got = _forward_wait(f"forward_wait_{part}_{l}", got, sems, after)
        B = {_BIG[items[i][0]]: g for i, g in zip(ids, got)}
        if part == "ffn":
            return dict(w_ffn_in=B["w_ffn_in"], w_ffn_out=B["w_ffn_out"].reshape(-1, D))
        W = {}
        if l < NA:
            W.update(w_rec_in=B["w_rec_in"],
                     w_gates=B["w_lru_gates"].reshape(N_CHIPS, NBLK, BW, GS).transpose(1, 2, 0, 3).reshape(
                         NBLK, BW, N_CHIPS * GS),
                     b_gates=b_gates[l], w_rec_out=B["w_rec_out"].reshape(C, D),
                     conv_w=small_a[l, :4], conv_b=small_a[l, 4:5], lru_param=small_a[l, 5:6])
        else:
            W.update(w_q=B["w_q"].reshape(D, D), w_o=B["w_o"].reshape(D, D))
            if l == NA:
                w_kvf_full = B["w_kvf"].transpose(1, 0, 2).reshape(D, -1)
                W.update(norm_kv=norm_kv[None], w_kv=w_kvf_full[:, :2 * D],
                         w_f=_pad_lanes(w_kvf_full[:, 2 * D:], LANES), b_f=_pad_lanes(b_forget[None], LANES))
        return W

    G_small = {l: {} for l in range(L)}
    pending = {}
    reducing = []

    def finish_reduce():
        l, part, its, grads, others, sems = reducing.pop()
        others = _reduce_wait(f"reduce_wait_{part}_{l}", grads, others, sems)
        parts = [_sum_cores(f"sum_cores_{l}_{_BIG[w]}", g, o, core_id) for (w, _), g, o in zip(its, grads, others)]
        recv, sems, token = _scatter_start(f"scatter_start_{part}_{l}", parts)
        pending[(l, part)] = (parts, recv, sems)
        return token

    def layer_grads(l, part, G):
        G_small[l].update(G)
        if reducing:
            finish_reduce()
        by_name = dict(
            w_ffn_in=lambda: G["w_ffn_in"], w_ffn_out=lambda: G["w_ffn_out"].reshape(N_CHIPS, -1, D),
            w_rec_in=lambda: G["w_rec_in"],
            w_lru_gates=lambda: G["w_gates"].reshape(NBLK, BW, N_CHIPS, GS).transpose(2, 0, 1, 3).reshape(
                N_CHIPS, NBLK * BW, GS),
            w_rec_out=lambda: G["w_rec_out"].reshape(N_CHIPS, -1, D),
            w_kvf=lambda: jnp.concatenate([G["w_kv"].astype(F32), G["w_f"][:, :H]], axis=1).reshape(
                D, N_CHIPS, -1).transpose(1, 0, 2).astype(BF16),
            w_q=lambda: G["w_q"].reshape(N_CHIPS, -1, D), w_o=lambda: G["w_o"].reshape(N_CHIPS, -1, D))
        its = stage_items(l, part)
        grads = [by_name[_BIG[w]]() for w, _ in its]
        others, sems, token = _reduce_start(f"reduce_start_{part}_{l}", grads)
        reducing.append((l, part, its, grads, others, sems))
        return finish_reduce() if l == 0 else token

    gains = dict(mix=[norm_mix[l][None] for l in range(L)], ffn=[norm_ffn[l][None] for l in range(L)],
                 final=norm_final[None])
    loss_row, grad_x, dg_final = _local_step(x.reshape(S, D), loss_target.reshape(S, D), gains,
                                             layer_weights, layer_prefetch, layer_grads)

    rows = [*[G_small[l]["norm_mix"] for l in range(L)], *[G_small[l]["norm_ffn"] for l in range(L)],
            G_small[NA]["norm_kv"], dg_final, _pad_lanes(G_small[NA]["b_f"], D), _pad_lanes(loss_row, D)]
    for a in range(NA):
        rows += [G_small[a][n] for n in ("conv_w", "conv_b", "b_gi", "b_gr", "lru_param")]
    packed = jnp.concatenate(rows, axis=0)
    tot = _sum_slots("sum_small", _gather_all("gather_small", packed))
    loss = tot[2 * L + 3, 0]
    g_rep = jnp.concatenate([tot[:2 * L + 2], tot[2 * L + 2:2 * L + 3]], axis=0)
    base = 2 * L + 4
    g_sh = []
    for a in range(NA):
        blk = lax.dynamic_slice_in_dim(tot[base + 8 * a:base + 8 * a + 8], chip * CS, CS, axis=1)
        gi = tot[base + 8 * a + 5].reshape(NBLK, BW)
        gr = tot[base + 8 * a + 6].reshape(NBLK, BW)
        bl = lax.dynamic_slice_in_dim(jnp.concatenate([gi, gr], axis=1), chip * GS, GS, axis=1)
        g_sh += [blk[:5], bl.reshape(-1, CS), blk[7:8]]
    g_sh = jnp.concatenate(g_sh, axis=0)
    nrow = g_sh.shape[0] // NA

    def pack_rep(T):
        return jnp.concatenate([T["norm_mix"], T["norm_ffn"], T["norm_kv"][None], T["norm_final"][None],
                                _pad_lanes(T["b_forget"][None], D)], axis=0)

    def pack_sh(T):
        return jnp.concatenate([jnp.concatenate([T["conv_w"][a], T["conv_b"][a][None],
                                                 T["b_lru_gates"][a].reshape(-1, CS), T["lru_param"][a][None]], axis=0)
                                for a in range(NA)], axis=0)

    rep = _adamw("adamw_replicated", [g_rep], pack_rep(P), pack_rep(M1), pack_rep(M2))
    shd = _adamw("adamw_small_sharded", [g_sh], pack_sh(P), pack_sh(M1), pack_sh(M2))

    def unpack_rep(t):
        return dict(norm_mix=t[:L], norm_ffn=t[L:2 * L], norm_kv=t[2 * L], norm_final=t[2 * L + 1],
                    b_forget=t[2 * L + 2, :H])

    def unpack_sh(t):
        t = t.reshape(NA, nrow, CS)
        return dict(conv_w=t[:, :4], conv_b=t[:, 4], b_lru_gates=t[:, 5:nrow - 1].reshape(NA, NBLK, GS),
                    lru_param=t[:, nrow - 1])

    full = [lax.empty(sh.shape, F32) for sh in shards]
    for l, part in reversed(stages):
        parts, recv, sems = pending[(l, part)]
        recv = _scatter_wait(f"scatter_wait_{part}_{l}", parts, recv, sems)
        for (w, li), own, r in zip(stage_items(l, part), parts, recv):
            full[w] = _sum_chips(f"sum_chips_{l}_{_BIG[w]}", r, own, full[w], li, me_core)
    full = _share_d2d("share_d2d", full)
    big = {w: _adamw(f"adamw_{w}", [g.reshape(P[w].shape)], P[w], M1[w], M2[w]) for w, g in zip(_BIG, full)}

    outs = []
    for i in range(4):
        small = {**unpack_rep(rep[i]), **unpack_sh(shd[i])}
        outs.append([big[w][i] if w in big else small[w] for w in _WEIGHTS])
    return (loss, grad_x.reshape(1, S, D), *outs[0], *outs[1], *outs[2], *outs[3])
```

```python
import functools
import math

import jax
import jax.numpy as jnp
from jax import lax
from jax.experimental import pallas as pl
from jax.experimental.pallas import tpu as pltpu

F32 = jnp.float32
BF16 = jnp.bfloat16

EPS = 1e-6
LRU_C = 8.0
HEAD_DIM = 64
LANES = 128
SUBLANES = 8
VMEM_LIMIT = 48 * 1024 * 1024
N_CHIPS = 4
N_DEV = 8

ADAM_LR = 0.001
ADAM_B1 = 0.9
ADAM_B2 = 0.999
ADAM_EPS = 1e-08
ADAM_WD = 0.01
ADAM_STEP = 10

_NN = (((1,), (0,)), ((), ()))
_NT = (((1,), (1,)), ((), ()))
_TN = (((0,), (0,)), ((), ()))
_DN = {"nn": _NN, "nt": _NT, "tn": _TN}
MESH = pl.DeviceIdType.MESH


def _hbm_out(shape, dtype):
    return pltpu.HBM(shape, dtype)


def _params(sem):
    return pltpu.CompilerParams(dimension_semantics=sem, vmem_limit_bytes=VMEM_LIMIT)


def _tile(n, want):
    if n <= want:
        return n
    t = (want // LANES) * LANES
    while t >= LANES:
        if n % t == 0:
            return t
        t -= LANES
    return n


def _sigmoid(x):
    return 1.0 / (1.0 + jnp.exp(-x))


def _sigmoid_t(x):
    return 0.5 * jnp.tanh(0.5 * x) + 0.5


def _softplus(x):
    return jnp.maximum(x, 0.0) + jnp.log(1.0 + jnp.exp(-jnp.abs(x)))


_GELU_C = math.sqrt(2.0 / math.pi)


def _gelu_and_grad(x):
    inner = _GELU_C * (x + 0.044715 * x * x * x)
    t = jnp.tanh(inner)
    g = 0.5 * x * (1.0 + t)
    dg = 0.5 * (1.0 + t) + 0.5 * x * (1.0 - t * t) * _GELU_C * (1.0 + 3.0 * 0.044715 * x * x)
    return g, dg


def _rms(x):
    return lax.rsqrt(jnp.mean(x * x, axis=-1, keepdims=True) + EPS)


def _rms_bwd(dy, x, g):
    r = _rms(x)
    xr = x * r
    dyg = dy * g
    return r * dyg - xr * (r * jnp.mean(dyg * xr, axis=-1, keepdims=True)), jnp.sum(dy * xr, axis=0, keepdims=True)


def _mm(name, mode, a, b, *, grid, a_spec, b_spec, out_shape, out_dtype, out_spec, nk=1,
        res=None, res_spec=None, bias=None, bias_spec=None, scale=None, norm_gain=None, norm_bwd=None):
    dn = _DN[mode]
    has_res, has_bias = res is not None, bias is not None
    blk = tuple(d for d in out_spec.block_shape if d is not None)
    vec = pl.BlockSpec((1, blk[-1]), lambda *g: (0, 0))
    a_specs = a_spec if isinstance(a_spec, list) else [a_spec]
    b_specs = b_spec if isinstance(b_spec, list) else [b_spec]
    npair = len(a_specs)
    n_in = 2 * npair + int(has_res) + int(has_bias) + (1 if norm_gain is not None else 0) + (3 if norm_bwd else 0)

    def body(*refs):
        p = 2 * npair
        r_ref = refs[p] if has_res else None
        p += int(has_res)
        bias_ref = refs[p] if has_bias else None
        p += int(has_bias)
        extra = refs[p:n_in]
        outs = refs[n_in:]
        o_ref = outs[0]
        part = lax.dot_general(refs[0][...], refs[npair][...], dn, preferred_element_type=F32)
        for t in range(1, npair):
            part = part + lax.dot_general(refs[t][...], refs[npair + t][...], dn, preferred_element_type=F32)

        def finish(acc):
            if scale is not None:
                acc = acc * scale
            if has_bias:
                acc = acc + bias_ref[...]
            if has_res:
                acc = r_ref[...] + acc
            if norm_bwd:
                h_ref, g_ref, dh_ref = extra
                dx, dg = _rms_bwd(acc, h_ref[...], g_ref[...])
                acc = dh_ref[...] + dx
                outs[1][...] = acc.astype(BF16)
                outs[2][...] = dg
            if norm_gain is not None:
                outs[1][...] = (acc * _rms(acc) * extra[0][...]).astype(BF16)
            o_ref[...] = acc.astype(o_ref.dtype)

        if nk == 1:
            finish(part)
        else:
            acc_ref = refs[-1]
            k = pl.program_id(2)

            @pl.when(k == 0)
            def _():
                acc_ref[...] = part

            @pl.when(k > 0)
            def _():
                acc_ref[...] += part

            @pl.when(k == nk - 1)
            def _():
                finish(acc_ref[...])

    ins, specs = [a] * npair + [b] * npair, a_specs + b_specs
    if has_res:
        ins.append(res)
        specs.append(res_spec)
    if has_bias:
        ins.append(bias)
        specs.append(bias_spec)
    out_specs, out_shapes = [out_spec], [_hbm_out(out_shape, out_dtype)]
    if norm_gain is not None:
        ins.append(norm_gain)
        specs.append(vec)
        out_specs.append(out_spec)
        out_shapes.append(_hbm_out(out_shape, BF16))
    if norm_bwd:
        h, g, dh = norm_bwd
        ins += [h, g, dh]
        specs += [out_spec, vec, out_spec]
        out_specs += [out_spec, pl.BlockSpec((None, 1, blk[-1]), lambda i, *rest: (i, 0, 0))]
        out_shapes += [_hbm_out(out_shape, BF16), _hbm_out((grid[0], 1, blk[-1]), F32)]
    sem = ("parallel", "parallel") + (("arbitrary",) if len(grid) == 3 else ())
    single = len(out_specs) == 1
    return pl.pallas_call(
        body, name=name, grid=grid, in_specs=specs, out_specs=out_specs[0] if single else out_specs,
        out_shape=out_shapes[0] if single else out_shapes,
        scratch_shapes=[pltpu.VMEM(blk, F32)] if nk > 1 else [],
        compiler_params=_params(sem),
    )(*ins)


def _mm_nn(name, a, b, *, b_lead=(), out_dtype, tm=512, tn=512, res=None, bias=None, scale=None, norm_gain=None):
    M, K = a.shape
    N = b.shape[-1]
    tm, tn = _tile(M, tm), _tile(N, tn)
    nl = len(b_lead)
    return _mm(
        name, "nn", a, b, grid=(M // tm, N // tn),
        a_spec=pl.BlockSpec((tm, K), lambda i, j: (i, 0)),
        b_spec=pl.BlockSpec((None,) * nl + (K, tn), lambda i, j: tuple(b_lead) + (0, j)),
        out_shape=(M, N), out_dtype=out_dtype, out_spec=pl.BlockSpec((tm, tn), lambda i, j: (i, j)),
        res=res, res_spec=pl.BlockSpec((tm, tn), lambda i, j: (i, j)),
        bias=bias, bias_spec=pl.BlockSpec((1, tn), lambda i, j: (0, j)), scale=scale, norm_gain=norm_gain)


def _mm_nt(name, a, b, *, b_lead=(), out_dtype, tm=512, tn=512, tk=2048, res=None, norm_bwd=None):
    M, K = a.shape
    N = b.shape[-2]
    tm, tn, tk = _tile(M, tm), _tile(N, tn), _tile(K, tk)
    nk = K // tk
    nl = len(b_lead)
    return _mm(
        name, "nt", a, b, grid=(M // tm, N // tn, nk), nk=nk,
        a_spec=pl.BlockSpec((tm, tk), lambda i, j, k: (i, k)),
        b_spec=pl.BlockSpec((None,) * nl + (tn, tk), lambda i, j, k: tuple(b_lead) + (j, k)),
        out_shape=(M, N), out_dtype=out_dtype, out_spec=pl.BlockSpec((tm, tn), lambda i, j, k: (i, j)),
        res=res, res_spec=pl.BlockSpec((tm, tn), lambda i, j, k: (i, j)), norm_bwd=norm_bwd)


def _mm_tn(name, a, b, *, out_dtype, tm=512, tn=512):
    S, M = a.shape
    N = b.shape[1]
    tm, tn = _tile(M, tm), _tile(N, tn)
    return _mm(
        name, "tn", a, b, grid=(M // tm, N // tn),
        a_spec=pl.BlockSpec((S, tm), lambda i, j: (0, i)),
        b_spec=pl.BlockSpec((S, tn), lambda i, j: (0, j)),
        out_shape=(M, N), out_dtype=out_dtype, out_spec=pl.BlockSpec((tm, tn), lambda i, j: (i, j)))


def _rmsnorm_fwd(name, h, g, tr=256):
    S, D = h.shape
    tr = _tile(S, tr)

    def body(h_ref, g_ref, o_ref):
        x = h_ref[...]
        r = lax.rsqrt(jnp.mean(x * x, axis=-1, keepdims=True) + EPS)
        o_ref[...] = (x * r * g_ref[...]).astype(o_ref.dtype)

    return pl.pallas_call(
        body, name=name, grid=(S // tr,),
        in_specs=[pl.BlockSpec((tr, D), lambda i: (i, 0)), pl.BlockSpec((1, D), lambda i: (0, 0))],
        out_specs=pl.BlockSpec((tr, D), lambda i: (i, 0)),
        out_shape=_hbm_out((S, D), BF16),
        compiler_params=_params(("parallel",)),
    )(h, g)


def _loss_head(name, h, target, g, tr=256):
    S, D = h.shape
    tr = _tile(S, tr)

    def body(h_ref, t_ref, g_ref, o_ref, ob_ref, dg_ref, loss_ref):
        i = pl.program_id(0)
        x = h_ref[...]
        gg = g_ref[...]
        r = lax.rsqrt(jnp.mean(x * x, axis=-1, keepdims=True) + EPS)
        xr = x * r
        err = xr * gg - t_ref[...]
        lpart = 0.5 * jnp.sum(jnp.mean(err * err, axis=-1, keepdims=True), axis=0, keepdims=True)
        dy = err * (1.0 / D)
        dyg = dy * gg
        dx = r * dyg - xr * (r * jnp.mean(dyg * xr, axis=-1, keepdims=True))
        o_ref[...] = dx
        ob_ref[...] = dx.astype(BF16)
        part = jnp.sum(dy * xr, axis=0, keepdims=True)
        lrow = jnp.broadcast_to(lpart, (1, LANES))

        @pl.when(i == 0)
        def _():
            dg_ref[...] = part
            loss_ref[...] = lrow

        @pl.when(i > 0)
        def _():
            dg_ref[...] += part
            loss_ref[...] += lrow

    row = pl.BlockSpec((tr, D), lambda i: (i, 0))
    vec = pl.BlockSpec((1, D), lambda i: (0, 0))
    return pl.pallas_call(
        body, name=name, grid=(S // tr,),
        in_specs=[row, row, vec], out_specs=[row, row, vec, pl.BlockSpec((1, LANES), lambda i: (0, 0))],
        out_shape=[_hbm_out((S, D), F32), _hbm_out((S, D), BF16),
                   _hbm_out((1, D), F32), _hbm_out((1, LANES), F32)],
        compiler_params=_params(("arbitrary",)),
    )(h, target, g)


def _swiglu_fwd(name, hn, w_in, tm=512):
    S, D = hn.shape
    FH = w_in.shape[-1]
    tm = _tile(S, tm)

    def body(x_ref, wg_ref, wu_ref, z_ref, a_ref):
        x = x_ref[...]
        zg = jnp.dot(x, wg_ref[...], preferred_element_type=F32)
        zu = jnp.dot(x, wu_ref[...], preferred_element_type=F32)
        z_ref[0] = zg.astype(z_ref.dtype)
        z_ref[1] = zu.astype(z_ref.dtype)
        a_ref[...] = (zg * _sigmoid_t(zg) * zu).astype(a_ref.dtype)

    return pl.pallas_call(
        body, name=name, grid=(S // tm, 2),
        in_specs=[pl.BlockSpec((tm, D), lambda i, j: (i, 0)),
                  pl.BlockSpec((None, D, FH), lambda i, j: (j, 0, 0)),
                  pl.BlockSpec((None, D, FH), lambda i, j: (j + 2, 0, 0))],
        out_specs=[pl.BlockSpec((2, tm, FH), lambda i, j: (0, i, j)), pl.BlockSpec((tm, FH), lambda i, j: (i, j))],
        out_shape=[_hbm_out((2, S, 2 * FH), BF16), _hbm_out((S, 2 * FH), BF16)],
        compiler_params=_params(("parallel", "parallel")),
    )(hn, w_in, w_in)


def _swiglu_bwd(name, dhb, w_out, z3, tm=512):
    S, D = dhb.shape
    F = w_out.shape[0]
    FH = F // 2
    tm = _tile(S, tm)

    def body(d_ref, w_ref, z_ref, dz_ref):
        d = lax.dot_general(d_ref[...], w_ref[...], _NT, preferred_element_type=F32)
        zg = z_ref[0].astype(F32)
        zu = z_ref[1].astype(F32)
        sg = _sigmoid_t(zg)
        dz_ref[0] = (d * zu * (sg * (1.0 + zg * (1.0 - sg)))).astype(dz_ref.dtype)
        dz_ref[1] = (d * (zg * sg)).astype(dz_ref.dtype)

    zspec = pl.BlockSpec((2, tm, FH), lambda i, j: (0, i, j))
    return pl.pallas_call(
        body, name=name, grid=(S // tm, 2),
        in_specs=[pl.BlockSpec((tm, D), lambda i, j: (i, 0)), pl.BlockSpec((FH, D), lambda i, j: (j, 0)), zspec],
        out_specs=zspec, out_shape=_hbm_out((2, S, F), BF16),
        compiler_params=_params(("parallel", "parallel")),
    )(dhb, w_out, z3)


SCAN_ROWS = 64


def _group_scan(A, B, reverse):
    n = A.shape[0]
    sub = lax.broadcasted_iota(jnp.int32, A.shape, 0) % SUBLANES
    for d in (1, 2, 4):
        if reverse:
            A_sh, B_sh = pltpu.roll(A, n - d, 0), pltpu.roll(B, n - d, 0)
            keep = sub < SUBLANES - d
        else:
            A_sh, B_sh = pltpu.roll(A, d, 0), pltpu.roll(B, d, 0)
            keep = sub >= d
        B = jnp.where(keep, A * B_sh + B, B)
        A = jnp.where(keep, A * A_sh, A)
    return A, B


def _block_scan(a, u, carry, reverse):
    A, B = _group_scan(a, u, reverse)
    ng = a.shape[0] // SUBLANES
    out = [None] * ng
    order = range(ng - 1, -1, -1) if reverse else range(ng)
    for gi in order:
        sl = slice(gi * SUBLANES, (gi + 1) * SUBLANES)
        hg = A[sl] * carry + B[sl]
        out[gi] = hg
        carry = hg[0:1] if reverse else hg[SUBLANES - 1:SUBLANES]
    return jnp.concatenate(out, axis=0), carry


def _lru_gates(rc, gip, grp, sp):
    gi = _sigmoid(gip)
    gr = _sigmoid(grp)
    la = -LRU_C * gr * sp
    a = jnp.exp(la)
    om = -jnp.tanh(la) * (a * a + 1.0)
    mult = jnp.sqrt(om)
    return gi, gr, a, mult


def _lru_fwd(name, proj, rc, gip, grp, lru_p, tc=256):
    S, C = rc.shape
    tc = _tile(C, tc)
    nb = S // SCAN_ROWS

    def body(gb_ref, rc_ref, gi_ref, gr_ref, l_ref, h_ref, m_ref):
        sp = _softplus(-l_ref[...])

        def step(b, carry):
            rows = pl.ds(pl.multiple_of(b * SCAN_ROWS, SCAN_ROWS), SCAN_ROWS)
            rcb = rc_ref[rows, :]
            gi, _, a, mult = _lru_gates(rcb, gi_ref[rows, :], gr_ref[rows, :], sp)
            h, carry = _block_scan(a, rcb * gi * mult, carry, False)
            h_ref[rows, :] = h
            gel, _ = _gelu_and_grad(gb_ref[rows, :])
            m_ref[rows, :] = (gel * h).astype(m_ref.dtype)
            return carry

        lax.fori_loop(0, nb, step, jnp.zeros((1, tc), F32))

    col = pl.BlockSpec((S, tc), lambda j: (0, j))
    return pl.pallas_call(
        body, name=name, grid=(C // tc,),
        in_specs=[col, col, col, col, pl.BlockSpec((1, tc), lambda j: (0, j))],
        out_specs=[col, col],
        out_shape=[_hbm_out((S, C), F32), _hbm_out((S, C), BF16)],
        compiler_params=_params(("parallel",)),
    )(proj, rc, gip, grp, lru_p)


def _lru_bwd(name, dm, proj, hrec, rc, gip, grp, lru_p, tc=256):
    S, C = rc.shape
    tc = _tile(C, tc)
    nb = S // SCAN_ROWS
    R = SCAN_ROWS

    def body(dm_ref, gb_ref, h_ref, rc_ref, gi_ref, gr_ref, l_ref,
             dgb_ref, dgi_ref, dgr_ref, drc_ref, dbi_ref, dbr_ref, dl_ref):
        lp = l_ref[...]
        sp = _softplus(-lp)
        row = lax.broadcasted_iota(jnp.int32, (R, tc), 0)
        zero = jnp.zeros((1, tc), F32)

        def step(t, carry):
            mu_in, s_i, s_r, s_sp = carry
            b = nb - 1 - t
            r0 = pl.multiple_of(b * R, R)
            rows = pl.ds(r0, R)
            rcb = rc_ref[rows, :]
            gi, gr, a, mult = _lru_gates(rcb, gi_ref[rows, :], gr_ref[rows, :], sp)
            gel, dgel = _gelu_and_grad(gb_ref[rows, :])
            dmb = dm_ref[rows, :]
            h = h_ref[rows, :]
            dgb_ref[rows, :] = (dmb * h * dgel).astype(dgb_ref.dtype)
            dh = dmb * gel
            mu, mu_out = _block_scan(a, a * dh, mu_in, True)
            mu_next = jnp.where(row == R - 1, mu_in, pltpu.roll(mu, R - 1, 0))
            lam = dh + mu_next
            p0 = pl.multiple_of(jnp.maximum(r0 - SUBLANES, 0), SUBLANES)
            prev = h_ref[pl.ds(p0, SUBLANES), :][SUBLANES - 1:SUBLANES]
            prev = jnp.where(b > 0, prev, 0.0)
            h_prev = jnp.where(row == 0, prev, pltpu.roll(h, 1, 0))
            da = lam * h_prev
            d_mult = lam * rcb * gi
            d_la = da * a - d_mult * (a * a) / mult
            d_grp = d_la * (-LRU_C * sp) * gr * (1.0 - gr)
            d_gip = lam * rcb * mult * gi * (1.0 - gi)
            dgr_ref[rows, :] = d_grp.astype(dgr_ref.dtype)
            dgi_ref[rows, :] = d_gip.astype(dgi_ref.dtype)
            drc_ref[rows, :] = lam * gi * mult
            s_i = s_i + jnp.sum(d_gip, axis=0, keepdims=True)
            s_r = s_r + jnp.sum(d_grp, axis=0, keepdims=True)
            s_sp = s_sp + jnp.sum(d_la * gr, axis=0, keepdims=True)
            return mu_out, s_i, s_r, s_sp

        _, s_i, s_r, s_sp = lax.fori_loop(0, nb, step, (zero, zero, zero, zero))
        dbi_ref[...] = s_i
        dbr_ref[...] = s_r
        dl_ref[...] = (-LRU_C * s_sp) * (-_sigmoid(-lp))

    col = pl.BlockSpec((S, tc), lambda j: (0, j))
    vec = pl.BlockSpec((1, tc), lambda j: (0, j))
    return pl.pallas_call(
        body, name=name, grid=(C // tc,),
        in_specs=[col, col, col, col, col, col, vec],
        out_specs=[col, col, col, col, vec, vec, vec],
        out_shape=[_hbm_out((S, C), BF16), _hbm_out((S, C), BF16),
                   _hbm_out((S, C), BF16), _hbm_out((S, C), F32),
                   _hbm_out((1, C), F32), _hbm_out((1, C), F32),
                   _hbm_out((1, C), F32)],
        compiler_params=_params(("parallel",)),
    )(dm, proj, hrec, rc, gip, grp, lru_p)


def _cumsum_rows(name, u, reverse):
    S, C = u.shape
    nb = S // SCAN_ROWS

    def body(u_ref, o_ref):
        def step(t, carry):
            b = nb - 1 - t if reverse else t
            rows = pl.ds(pl.multiple_of(b * SCAN_ROWS, SCAN_ROWS), SCAN_ROWS)
            ub = u_ref[rows, :]
            h, carry = _block_scan(jnp.ones_like(ub), ub, carry, reverse)
            o_ref[rows, :] = h
            return carry

        lax.fori_loop(0, nb, step, jnp.zeros((1, C), F32))

    spec = pl.BlockSpec((S, C), lambda i: (0, 0))
    return pl.pallas_call(
        body, name=name, grid=(1,), in_specs=[spec], out_specs=spec,
        out_shape=_hbm_out((S, C), F32),
        compiler_params=_params(("arbitrary",)),
    )(u)


def _shift_down(x, k):
    row = lax.broadcasted_iota(jnp.int32, x.shape, 0)
    return jnp.where(row >= k, pltpu.roll(x, k, 0), 0.0)


def _shift_up(x, k):
    n = x.shape[0]
    row = lax.broadcasted_iota(jnp.int32, x.shape, 0)
    return jnp.where(row < n - k, pltpu.roll(x, n - k, 0), 0.0)


def _conv_fwd(name, proj, w, b, tc=256):
    S, C2 = proj.shape
    C = C2 // 2
    tc = _tile(C, tc)
    off = C // tc

    def body(x_ref, w_ref, b_ref, o_ref, ob_ref):
        x = x_ref[...]
        out = b_ref[...] + w_ref[3:4, :] * x
        for k in (1, 2, 3):
            out = out + w_ref[3 - k:4 - k, :] * _shift_down(x, k)
        o_ref[...] = out
        ob_ref[...] = out.astype(BF16)

    col = pl.BlockSpec((S, tc), lambda j: (0, j))
    return pl.pallas_call(
        body, name=name, grid=(C // tc,),
        in_specs=[pl.BlockSpec((S, tc), lambda j: (0, off + j)),
                  pl.BlockSpec((4, tc), lambda j: (0, j)), pl.BlockSpec((1, tc), lambda j: (0, j))],
        out_specs=[col, col],
        out_shape=[_hbm_out((S, C), F32), _hbm_out((S, C), BF16)],
        compiler_params=_params(("parallel",)),
    )(proj, w, b)


def _conv_bwd(name, drc, proj, w, tc=256):
    S, C = drc.shape
    tc = _tile(C, tc)
    off = C // tc

    def body(y_ref, x_ref, w_ref, dx_ref, dw_ref, db_ref):
        y = y_ref[...]
        x = x_ref[...]
        dx = w_ref[3:4, :] * y
        dw_ref[3:4, :] = jnp.sum(y * x, axis=0, keepdims=True)
        for k in (1, 2, 3):
            dx = dx + w_ref[3 - k:4 - k, :] * _shift_up(y, k)
            dw_ref[3 - k:4 - k, :] = jnp.sum(y * _shift_down(x, k), axis=0, keepdims=True)
        dx_ref[...] = dx.astype(dx_ref.dtype)
        db_ref[...] = jnp.sum(y, axis=0, keepdims=True)

    col = pl.BlockSpec((S, tc), lambda j: (0, j))
    return pl.pallas_call(
        body, name=name, grid=(C // tc,),
        in_specs=[col, pl.BlockSpec((S, tc), lambda j: (0, off + j)), pl.BlockSpec((4, tc), lambda j: (0, j))],
        out_specs=[col, pl.BlockSpec((4, tc), lambda j: (0, j)), pl.BlockSpec((1, tc), lambda j: (0, j))],
        out_shape=[_hbm_out((S, C), BF16), _hbm_out((4, C), F32),
                   _hbm_out((1, C), F32)],
        compiler_params=_params(("parallel",)),
    )(drc, proj, w)


def _gates_fwd(name, rcb, wg, bg):
    S, C = rcb.shape
    nblk, bw, _ = wg.shape

    def body(x_ref, w_ref, b_ref, gi_ref, gr_ref):
        g = jnp.dot(x_ref[...], w_ref[...], preferred_element_type=F32) + b_ref[...]
        gi_ref[...] = g[:, :bw]
        gr_ref[...] = g[:, bw:]

    col = pl.BlockSpec((S, bw), lambda n: (0, n))
    return pl.pallas_call(
        body, name=name, grid=(nblk,),
        in_specs=[col, pl.BlockSpec((None, bw, 2 * bw), lambda n: (n, 0, 0)),
                  pl.BlockSpec((None, 1, 2 * bw), lambda n: (n, 0, 0))],
        out_specs=[col, col],
        out_shape=[_hbm_out((S, C), F32), _hbm_out((S, C), F32)],
        compiler_params=_params(("parallel",)),
    )(rcb, wg, bg)


def _gates_bwd(name, dgi, dgr, rcb, wg, drc1):
    S, C = rcb.shape
    nblk, bw, _ = wg.shape

    def body(dgi_ref, dgr_ref, x_ref, w_ref, d1_ref, drc_ref, dw_ref):
        w = w_ref[...]
        x = x_ref[...]
        di, dr = dgi_ref[...], dgr_ref[...]
        drc_ref[...] = (d1_ref[...]
                        + lax.dot_general(di, w[:, :bw], _NT, preferred_element_type=F32)
                        + lax.dot_general(dr, w[:, bw:], _NT, preferred_element_type=F32))
        dw_ref[:, :bw] = lax.dot_general(x, di, _TN, preferred_element_type=F32).astype(dw_ref.dtype)
        dw_ref[:, bw:] = lax.dot_general(x, dr, _TN, preferred_element_type=F32).astype(dw_ref.dtype)

    col = pl.BlockSpec((S, bw), lambda n: (0, n))
    wspec = pl.BlockSpec((None, bw, 2 * bw), lambda n: (n, 0, 0))
    return pl.pallas_call(
        body, name=name, grid=(nblk,),
        in_specs=[col, col, col, wspec, col], out_specs=[col, wspec],
        out_shape=[_hbm_out((S, C), F32), _hbm_out((nblk, bw, 2 * bw), BF16)],
        compiler_params=_params(("parallel",)),
    )(dgi, dgr, rcb, wg, drc1)


def _att_tile(S):
    return next(t for t in (512, 256, 128) if S % t == 0)


def _head_lanes(shape):
    return lax.broadcasted_iota(jnp.int32, shape, len(shape) - 1) < HEAD_DIM


def _key_bias(c_blk):
    first = _head_lanes(c_blk.shape)
    rolled = pltpu.roll(c_blk, HEAD_DIM, 1)
    return jnp.where(first, c_blk, rolled), jnp.where(first, rolled, c_blk)


def _over_keys(x, op):
    n = x.shape[0]
    while n > SUBLANES:
        n //= 2
        x = op(x[:n], x[n:2 * n])
    return (jnp.max if op is jnp.maximum else jnp.sum)(x, axis=0, keepdims=True)


def _causal_t(T, cc):
    r = lax.broadcasted_iota(jnp.int32, (T, LANES), 0)
    c = lax.broadcasted_iota(jnp.int32, (T, LANES), 1) + cc * LANES
    return r <= c


def _attn_fwd(name, q, kv, cfull):
    S, D = q.shape
    HP = D // LANES
    T = _att_tile(S)
    nq = S // T
    NC = T // LANES

    def body(q_ref, k_ref, v_ref, c_ref, o_ref, of_ref, lse_ref, bias, vT, acc, m_scr, l_scr):
        def prologue(i, _):
            rows = pl.ds(pl.multiple_of(i * T, T), T)
            bias[0, rows, :], bias[1, rows, :] = _key_bias(c_ref[rows, :])
            vT[i] = v_ref[rows, :].astype(F32).T.astype(BF16)
            return 0

        lax.fori_loop(0, nq, prologue, 0)

        def q_step(qi, _):
            q0 = pl.multiple_of(qi * T, T)
            qb = q_ref[pl.ds(q0, T), :]
            m_scr[...] = jnp.full(m_scr.shape, -jnp.inf, F32)
            l_scr[...] = jnp.zeros(l_scr.shape, F32)
            acc[...] = jnp.zeros(acc.shape, F32)

            def tile(kj, masked):
                ks = pl.ds(pl.multiple_of(kj * T, T), T)
                kf = k_ref[ks, :].astype(F32)
                first = _head_lanes(kf.shape)
                kms = [jnp.where(first if hh == 0 else jnp.logical_not(first), kf, 0.0).astype(BF16) for hh in range(2)]
                sTs = [lax.dot_general(km, qb, _NT, preferred_element_type=F32) for km in kms]
                for hh in range(2):
                    b = bias[hh, ks, :]
                    ps = []
                    for cc in range(NC):
                        cols = slice(cc * LANES, (cc + 1) * LANES)
                        s = sTs[hh][:, cols] + b
                        if masked:
                            s = jnp.where(_causal_t(T, cc), s, -jnp.inf)
                        m_old = m_scr[hh, cc]
                        m_new = jnp.maximum(m_old, _over_keys(s, jnp.maximum))
                        alpha = jnp.exp(m_old - m_new)
                        p = jnp.exp(s - m_new)
                        l_scr[hh, cc] = alpha * l_scr[hh, cc] + _over_keys(p, jnp.add)
                        m_scr[hh, cc] = m_new
                        ps.append(p.astype(BF16))
                        acc[hh, :, cols] = acc[hh, :, cols] * alpha
                    acc[hh] += jnp.dot(vT[kj, hh * HEAD_DIM:(hh + 1) * HEAD_DIM, :], jnp.concatenate(ps, axis=1),
                                       preferred_element_type=F32)

            def inner(kj, _):
                tile(kj, False)
                return 0

            lax.fori_loop(0, qi, inner, 0)
            tile(qi, True)
            outs = []
            for hh in range(2):
                inv = jnp.concatenate([1.0 / l_scr[hh, cc] for cc in range(NC)], axis=1)
                outs.append(acc[hh] * inv)
                for cc in range(NC):
                    lse_ref[hh:hh + 1, pl.ds(q0 + cc * LANES, LANES)] = m_scr[hh, cc] + jnp.log(l_scr[hh, cc])
            out = jnp.concatenate(outs, axis=0).T
            o_ref[pl.ds(q0, T), :] = out.astype(o_ref.dtype)
            of_ref[pl.ds(q0, T), :] = out
            return 0

        lax.fori_loop(0, nq, q_step, 0)

    blk = lambda off: pl.BlockSpec((S, LANES), lambda p: (0, off + p))
    return pl.pallas_call(
        body, name=name, grid=(HP,),
        in_specs=[blk(0), blk(0), blk(HP), blk(0)],
        out_specs=[blk(0), blk(0), pl.BlockSpec((None, 2, S), lambda p: (p, 0, 0))],
        out_shape=[_hbm_out((S, D), BF16), _hbm_out((S, D), F32),
                   _hbm_out((HP, 2, S), F32)],
        scratch_shapes=[pltpu.VMEM((2, S, LANES), F32), pltpu.VMEM((nq, LANES, T), BF16),
                        pltpu.VMEM((2, HEAD_DIM, T), F32), pltpu.VMEM((2, NC, 1, LANES), F32),
                        pltpu.VMEM((2, NC, 1, LANES), F32)],
        compiler_params=_params(("parallel",)),
    )(q, kv, kv, cfull)


def _attn_bwd(name, q, kv, cfull, of, do, lse3):
    S, D = q.shape
    HP = D // LANES
    T = _att_tile(S)
    nq = S // T
    NC = T // LANES
    scale = HEAD_DIM ** -0.5

    def body(q_ref, k_ref, v_ref, c_ref, of_ref, do_ref, lse_ref,
             dq_ref, dk_ref, dv_ref, dck_ref, drq_ref, bias, kT, dqT, delta, dr_scr):
        def prologue(i, _):
            rows = pl.ds(pl.multiple_of(i * T, T), T)
            bias[0, rows, :], bias[1, rows, :] = _key_bias(c_ref[rows, :])
            kT[i] = k_ref[rows, :].astype(F32).T.astype(BF16)
            prodT = (do_ref[rows, :].astype(F32) * of_ref[rows, :]).T
            for hh in range(2):
                delta[hh:hh + 1, rows] = jnp.sum(prodT[hh * HEAD_DIM:(hh + 1) * HEAD_DIM], axis=0, keepdims=True)
            dqT[i] = jnp.zeros((LANES, T), F32)
            return 0

        lax.fori_loop(0, nq, prologue, 0)
        dr_scr[...] = jnp.zeros(dr_scr.shape, F32)

        def kv_step(kj, _):
            ks = pl.ds(pl.multiple_of(kj * T, T), T)
            kf = k_ref[ks, :].astype(F32)
            vf = v_ref[ks, :].astype(F32)
            first = _head_lanes(kf.shape)
            masks = [first, jnp.logical_not(first)]
            kms = [jnp.where(m, kf, 0.0).astype(BF16) for m in masks]
            vms = [jnp.where(m, vf, 0.0).astype(BF16) for m in masks]

            def tile(qi, carry, masked):
                q0 = pl.multiple_of(qi * T, T)
                qb = q_ref[pl.ds(q0, T), :]
                dob = do_ref[pl.ds(q0, T), :]
                sTs = [lax.dot_general(km, qb, _NT, preferred_element_type=F32) for km in kms]
                dpTs = [lax.dot_general(vm, dob, _NT, preferred_element_type=F32) for vm in vms]
                out = []
                for hh in range(2):
                    dk_a, dv_a, dc_a = carry[3 * hh:3 * hh + 3]
                    b = bias[hh, ks, :]
                    head = slice(hh * HEAD_DIM, (hh + 1) * HEAD_DIM)
                    ps, dss = [], []
                    for cc in range(NC):
                        cols = slice(cc * LANES, (cc + 1) * LANES)
                        at = pl.ds(q0 + cc * LANES, LANES)
                        p = jnp.exp(sTs[hh][:, cols] + b - lse_ref[hh:hh + 1, at])
                        if masked:
                            p = jnp.where(_causal_t(T, cc), p, 0.0)
                        ds = p * (dpTs[hh][:, cols] - delta[hh:hh + 1, at])
                        ps.append(p.astype(BF16))
                        dss.append(ds.astype(BF16))
                        dc_a = dc_a + ds
                        dr_scr[hh:hh + 1, at] += _over_keys(ds, jnp.add)
                    pT = jnp.concatenate(ps, axis=1)
                    dsT = jnp.concatenate(dss, axis=1)
                    dv_a = dv_a + jnp.dot(pT, dob, preferred_element_type=F32)
                    dk_a = dk_a + jnp.dot(dsT, qb, preferred_element_type=F32)
                    dqT[qi, head, :] += jnp.dot(kT[kj, head, :], dsT, preferred_element_type=F32)
                    out += [dk_a, dv_a, dc_a]
                return tuple(out)

            zero = jnp.zeros((T, LANES), F32)
            carry = tile(kj, (zero,) * 6, True)
            dk0, dv0, dc0, dk1, dv1, dc1 = lax.fori_loop(kj + 1, nq, lambda qi, c: tile(qi, c, False), carry)
            dk_ref[ks, :] = jnp.where(first, dk0, dk1)
            dv_ref[ks, :] = jnp.where(first, dv0, dv1)
            dck_ref[ks, :] = jnp.where(first, jnp.broadcast_to(-jnp.sum(dc0, axis=1, keepdims=True), (T, LANES)),
                                       jnp.broadcast_to(-jnp.sum(dc1, axis=1, keepdims=True), (T, LANES)))
            return 0

        lax.fori_loop(0, nq, kv_step, 0)

        def epilogue(i, _):
            rows = pl.ds(pl.multiple_of(i * T, T), T)
            dq_ref[rows, :] = (dqT[i].T * scale).astype(dq_ref.dtype)
            return 0

        lax.fori_loop(0, nq, epilogue, 0)
        drq_ref[...] = dr_scr[...]

    blk = lambda off: pl.BlockSpec((S, LANES), lambda p: (0, off + p))
    row_spec = pl.BlockSpec((None, 2, S), lambda p: (p, 0, 0))
    return pl.pallas_call(
        body, name=name, grid=(HP,),
        in_specs=[blk(0), blk(0), blk(HP), blk(0), blk(0), blk(0), row_spec],
        out_specs=[blk(0), blk(0), blk(0), blk(0), row_spec],
        out_shape=[_hbm_out((S, D), BF16), _hbm_out((S, D), F32),
                   _hbm_out((S, D), F32), _hbm_out((S, D), F32),
                   _hbm_out((HP, 2, S), F32)],
        scratch_shapes=[pltpu.VMEM((2, S, LANES), F32), pltpu.VMEM((nq, LANES, T), BF16),
                        pltpu.VMEM((nq, LANES, T), F32), pltpu.VMEM((2, S), F32), pltpu.VMEM((2, S), F32)],
        compiler_params=_params(("parallel",)),
    )(q, kv, kv, cfull, of, do, lse3)


def _logsig_fwd(name, f):
    S, C = f.shape

    def body(f_ref, o_ref):
        o_ref[...] = -_softplus(-f_ref[...])

    spec = pl.BlockSpec((S, C), lambda i: (0, 0))
    return pl.pallas_call(body, name=name, grid=(1,), in_specs=[spec], out_specs=spec,
                          out_shape=_hbm_out((S, C), F32),
                          compiler_params=_params(("arbitrary",)))(f)


def _logsig_bwd(name, dls, f):
    S, C = f.shape

    def body(d_ref, f_ref, o_ref, s_ref):
        df = d_ref[...] * _sigmoid(-f_ref[...])
        o_ref[...] = df.astype(o_ref.dtype)
        s_ref[...] = jnp.sum(df, axis=0, keepdims=True)

    spec = pl.BlockSpec((S, C), lambda i: (0, 0))
    return pl.pallas_call(body, name=name, grid=(1,), in_specs=[spec, spec],
                          out_specs=[spec, pl.BlockSpec((1, C), lambda i: (0, 0))],
                          out_shape=[_hbm_out((S, C), BF16), _hbm_out((1, C), F32)],
                          compiler_params=_params(("arbitrary",)))(dls, f)


def _add_cast(name, parts, out_dtype, tr=256):
    S, C = parts[0].shape
    tr = _tile(S, tr)
    n = len(parts)

    def body(*refs):
        acc = refs[0][...].astype(F32)
        for r in refs[1:n]:
            acc = acc + r[...].astype(F32)
        refs[n][...] = acc.astype(out_dtype)

    spec = pl.BlockSpec((tr, C), lambda i: (i, 0))
    return pl.pallas_call(body, name=name, grid=(S // tr,), in_specs=[spec] * n, out_specs=spec,
                          out_shape=_hbm_out((S, C), out_dtype),
                          compiler_params=_params(("parallel",)))(*parts)


def _local_step(x, target, gains, layer_weights, layer_prefetch, layer_grads):
    S, D = x.shape
    HP = D // LANES
    scale = HEAD_DIM ** -0.5
    tm = _tile(S, 512)
    tx = _tile(S, 256)
    td = _tile(D, 512)
    saved = []
    h = x
    l = 0
    kv = cfull = f_pre = hn_kv = h_kv = None
    while True:
        W = layer_weights(l, "mix", h)
        if W is None:
            break
        recurrent = "w_rec_in" in W
        if l == 0:
            xn = _rmsnorm_fwd("mix_norm_0", h, gains["mix"][0])
        if recurrent:
            CH = W["w_rec_in"].shape[-1]
            C = 2 * CH
            proj = _mm(f"rec_in_{l}", "nn", xn, W["w_rec_in"], grid=(S // tm, N_CHIPS),
                       a_spec=pl.BlockSpec((tm, D), lambda i, j: (i, 0)),
                       b_spec=pl.BlockSpec((None, D, CH), lambda i, j: (j, 0, 0)),
                       out_shape=(S, 2 * C), out_dtype=F32,
                       out_spec=pl.BlockSpec((tm, CH), lambda i, j: (i, j)))
            rc, rcb = _conv_fwd(f"conv_{l}", proj, W["conv_w"], W["conv_b"])
            gip, grp = _gates_fwd(f"gates_{l}", rcb, W["w_gates"], W["b_gates"])
            hrec, m = _lru_fwd(f"lru_{l}", proj, rc, gip, grp, W["lru_param"])
            layer_prefetch(l, "ffn", m)
            h_mid, hn = _mm_nn(f"rec_out_{l}", m, W["w_rec_out"], out_dtype=F32, res=h, tn=D, norm_gain=gains["ffn"][l])
            mix_saved = (xn, proj, rc, rcb, gip, grp, hrec, m)
        else:
            if "w_kv" in W:
                h_kv = h
                hn_kv = _rmsnorm_fwd("kv_norm", h, W["norm_kv"])
                kv = _mm_nn("kv_proj", hn_kv, W["w_kv"], out_dtype=BF16)
                f_pre = _mm_nn("f_proj", hn_kv, W["w_f"], out_dtype=F32, bias=W["b_f"])
                c = _cumsum_rows("c_cumsum", _logsig_fwd("logsig", f_pre), False)
                cfull = jnp.repeat(-c[:, :2 * HP], HEAD_DIM, axis=1)
            q = _mm_nn(f"q_proj_{l}", xn, W["w_q"], out_dtype=BF16, scale=scale)
            o, of, lse = _attn_fwd(f"attn_fwd_{l}", q, kv, cfull)
            layer_prefetch(l, "ffn", o)
            h_mid, hn = _mm_nn(f"o_proj_{l}", o, W["w_o"], out_dtype=F32, res=h, tn=D, norm_gain=gains["ffn"][l])
            mix_saved = (xn, q, o, of, lse)
        W = {**W, **layer_weights(l, "ffn", h_mid)}
        z3, act = _swiglu_fwd(f"ffn_in_{l}", hn, W["w_ffn_in"])
        layer_prefetch(l + 1, "mix", act)
        saved.append((W, h, h_mid, mix_saved, (hn, z3, act)))
        l += 1
        if l < len(gains["mix"]):
            h, xn = _mm_nn(f"ffn_out_{l - 1}", act, W["w_ffn_out"], out_dtype=F32, res=h_mid, tn=D,
                           norm_gain=gains["mix"][l])
        else:
            h = _mm_nn(f"ffn_out_{l - 1}", act, W["w_ffn_out"], out_dtype=F32, res=h_mid, tn=D)

    dh, dhb, dg_final, loss_row = _loss_head("loss_head", h, target, gains["final"])

    dk_parts, dv_parts, dc_parts = [], [], []
    token = None
    for l in reversed(range(len(saved))):
        W, h_in, h_mid, mix_saved, (hn, z3, act) = saved[l]
        recurrent = "w_rec_in" in W
        FH = W["w_ffn_in"].shape[-1]
        G = {}
        norm_ffn = gains["ffn"][l]
        if token is not None:
            norm_ffn = norm_ffn + jnp.minimum(token[:1, :1], 0.0)
        G["w_ffn_out"] = _mm_tn(f"d_ffn_out_{l}", act, dhb, out_dtype=BF16, tn=D)
        dz3 = _swiglu_bwd(f"d_act_{l}", dhb, W["w_ffn_out"], z3)
        G["w_ffn_in"] = _mm(
            f"d_ffn_in_{l}", "tn", hn, dz3, grid=(D // td, N_CHIPS),
            a_spec=pl.BlockSpec((S, td), lambda i, j: (0, i)),
            b_spec=pl.BlockSpec((None, S, FH), lambda i, j: (j // 2, 0, j % 2)),
            out_shape=(N_CHIPS, D, FH), out_dtype=BF16,
            out_spec=pl.BlockSpec((None, td, FH), lambda i, j: (j, i, 0)))
        token = layer_grads(l, "ffn", G)
        G = {}
        norm_ffn = norm_ffn + jnp.minimum(token[:1, :1], 0.0)
        dh, dhb, dgp = _mm(f"d_ffn_hn_{l}", "nt", dz3, W["w_ffn_in"], grid=(S // tx, 1),
                           a_spec=[pl.BlockSpec((None, tx, FH), functools.partial(lambda i, j, k: (k // 2, i, k % 2), k=k))
                                   for k in range(N_CHIPS)],
                           b_spec=[pl.BlockSpec((None, D, FH), functools.partial(lambda i, j, k: (k, 0, 0), k=k))
                                   for k in range(N_CHIPS)],
                           out_shape=(S, D), out_dtype=F32, out_spec=pl.BlockSpec((tx, D), lambda i, j: (i, 0)),
                           norm_bwd=(h_mid, norm_ffn, dh))
        G["norm_ffn"] = jnp.sum(dgp, axis=0)
        if recurrent:
            CH = W["w_rec_in"].shape[-1]
            C = 2 * CH
            xn, proj, rc, rcb, gip, grp, hrec, m = mix_saved
            G["w_rec_out"] = _mm_tn(f"d_rec_out_{l}", m, dhb, out_dtype=BF16, tn=D)
            dm = _mm_nt(f"d_m_{l}", dhb, W["w_rec_out"], out_dtype=F32, tn=C)
            dgb, dgi, dgr, drc1, G["b_gi"], G["b_gr"], G["lru_param"] = _lru_bwd(
                f"d_lru_{l}", dm, proj, hrec, rc, gip, grp, W["lru_param"])
            drc, G["w_gates"] = _gates_bwd(f"d_gates_{l}", dgi, dgr, rcb, W["w_gates"], drc1)
            drec, G["conv_w"], G["conv_b"] = _conv_bwd(f"d_conv_{l}", drc, proj, W["conv_w"])
            dproj = jnp.concatenate([dgb, drec], axis=1)
            G["w_rec_in"] = _mm(
                f"d_rec_in_{l}", "tn", xn, dproj, grid=(1, N_CHIPS),
                a_spec=pl.BlockSpec((S, D), lambda i, j: (0, 0)),
                b_spec=pl.BlockSpec((S, CH), lambda i, j: (0, j)),
                out_shape=(N_CHIPS, D, CH), out_dtype=BF16,
                out_spec=pl.BlockSpec((None, D, CH), lambda i, j: (j, 0, 0)))
            dh, dhb, dgp = _mm(f"d_rec_xn_{l}", "nt", dproj, W["w_rec_in"], grid=(S // tx, 1),
                               a_spec=[pl.BlockSpec((tx, CH), functools.partial(lambda i, j, k: (i, k), k=k))
                                       for k in range(N_CHIPS)],
                               b_spec=[pl.BlockSpec((None, D, CH), functools.partial(lambda i, j, k: (k, 0, 0), k=k))
                                       for k in range(N_CHIPS)],
                               out_shape=(S, D), out_dtype=F32, out_spec=pl.BlockSpec((tx, D), lambda i, j: (i, 0)),
                               norm_bwd=(h_in, gains["mix"][l], dh))
        else:
            xn, q, o, of, lse = mix_saved
            G["w_o"] = _mm_tn(f"d_o_proj_{l}", o, dhb, out_dtype=BF16, tn=D)
            do = _mm_nt(f"d_o_{l}", dhb, W["w_o"], out_dtype=BF16, tn=D)
            dq, dk, dv, dck, drq = _attn_bwd(f"attn_bwd_{l}", q, kv, cfull, of, do, lse)
            dk_parts.append(dk)
            dv_parts.append(dv)
            dc_parts.append(dck[:, ::HEAD_DIM] + drq.reshape(2 * HP, S).T)
            G["w_q"] = _mm_tn(f"d_q_proj_{l}", xn, dq, out_dtype=BF16, tn=D)
            dh, dhb, dgp = _mm_nt(f"d_q_xn_{l}", dq, W["w_q"], out_dtype=F32, tn=D, norm_bwd=(h_in, gains["mix"][l], dh))
        G["norm_mix"] = jnp.sum(dgp, axis=0)
        if "w_kv" in W:
            dkb = _add_cast("dk_sum", dk_parts, BF16)
            dvb = _add_cast("dv_sum", dv_parts, BF16)
            dkv = jnp.concatenate([dkb, dvb], axis=1)
            dc = sum(dc_parts[1:], dc_parts[0])
            dc_pad = jnp.pad(dc, ((0, 0), (0, LANES - 2 * HP)))
            dls = _cumsum_rows("dc_cumsum", dc_pad, True)
            dfb, G["b_f"] = _logsig_bwd("d_logsig", dls, f_pre)
            G["w_kv"] = _mm_tn("d_kv_proj", hn_kv, dkv, out_dtype=BF16)
            G["w_f"] = _mm_tn("d_f_proj", hn_kv, dfb, out_dtype=F32)
            dhn_f = _mm_nt("d_f_hn", dfb, W["w_f"], out_dtype=F32, tn=D)
            dh, dhb, dgp = _mm_nt("d_kv_hn", dkv, W["w_kv"], out_dtype=F32, tn=D, res=dhn_f,
                                  norm_bwd=(h_kv, W["norm_kv"], dh))
            G["norm_kv"] = jnp.sum(dgp, axis=0)
        token = layer_grads(l, "mix", G)
    return loss_row, dh, dg_final


_ANY = pl.BlockSpec(memory_space=pl.ANY)


def _position():
    return lax.axis_index("x"), lax.axis_index("y"), lax.axis_index("c")


def _chip_peers(x, y):
    return [(1 - x, y), (x, 1 - y), (1 - x, 1 - y)]


def _half_rows(c, n):
    h = n // 2
    assert h % 16 == 0
    return pl.ds(pl.multiple_of(c * h, 16), h)


def _place_own(name, shard, layer, me):
    _, R, C = shard.shape
    tr = _row_tile(R, C, 2 * shard.dtype.itemsize, target=8 << 20)

    def body(me_ref, x_ref, o_ref):
        o_ref[...] = x_ref[...]

    return pl.pallas_call(
        body, name=name,
        grid_spec=pltpu.PrefetchScalarGridSpec(
            num_scalar_prefetch=1, grid=(R // tr,),
            in_specs=[pl.BlockSpec((None, tr, C), lambda i, me_ref: (layer, i, 0))],
            out_specs=pl.BlockSpec((None, tr, C), lambda i, me_ref: (me_ref[0], i, 0))),
        out_shape=_hbm_out((N_CHIPS, R, C), shard.dtype),
        compiler_params=_params(("parallel",)),
    )(me, shard)


def _gather_smalls(name, smalls):
    ns = len(smalls)

    def body(*refs):
        ins, outs = refs[:ns], refs[ns:2 * ns]
        send_sems, recv_sems, local_sems = refs[2 * ns:]
        x, y, c = _position()
        me = 2 * x + y
        peers = _chip_peers(x, y)

        def remote(t, k, chip):
            px, py = peers[k]
            return pltpu.make_async_remote_copy(
                src_ref=ins[t], dst_ref=outs[t].at[chip], send_sem=send_sems.at[3 * t + k],
                recv_sem=recv_sems.at[3 * t + k], device_id=(px, py, c), device_id_type=MESH)

        local = [pltpu.make_async_copy(ins[t], outs[t].at[me], local_sems.at[t]) for t in range(ns)]
        for t in range(ns):
            local[t].start()
            for k in range(3):
                remote(t, k, me).start()
        for t in range(ns):
            for k in range(3):
                px, py = peers[k]
                remote(t, k, 2 * px + py).wait_recv()
        for t in range(ns):
            for k in range(3):
                remote(t, k, me).wait_send()
            local[t].wait()

    return pl.pallas_call(
        body, name=name, in_specs=[_ANY] * ns, out_specs=[_ANY] * ns,
        out_shape=[_hbm_out((N_CHIPS,) + s.shape, s.dtype) for s in smalls],
        scratch_shapes=[pltpu.SemaphoreType.DMA((3 * ns,)), pltpu.SemaphoreType.DMA((3 * ns,)),
                        pltpu.SemaphoreType.DMA((ns,))],
    )(*smalls)


_SEM = pl.BlockSpec(memory_space=pltpu.SEMAPHORE)
_SPLIT = pltpu.CompilerParams(has_side_effects=pltpu.SideEffectType.DATAFLOW_SIDE_EFFECTING)


def _weight_copy(shards, buf, items, sems, i, k, chip_of_dst, peers, c):
    w, l = items[i]
    px, py = peers[k]
    half = _half_rows(c, shards[w].shape[1])
    return pltpu.make_async_remote_copy(
        src_ref=shards[w].at[l, half], dst_ref=buf.at[chip_of_dst, half],
        send_sem=sems[0].at[3 * i + k], recv_sem=sems[1].at[3 * i + k],
        device_id=(px, py, c), device_id_type=MESH)


def _gather_start(name, shards, bufs, items, after):
    nw, n = len(shards), len(bufs)

    def body(*refs):
        ins, outs, sems = refs[:nw], refs[nw + n + 1:nw + 2 * n + 1], refs[nw + 2 * n + 1:]
        x, y, c = _position()
        peers = _chip_peers(x, y)
        for i in range(n):
            for k in range(3):
                _weight_copy(ins, outs[i], items, sems, i, k, 2 * x + y, peers, c).start()

    res = pl.pallas_call(
        body, name=name, in_specs=[_ANY] * (nw + n + 1), out_specs=[_ANY] * n + [_SEM, _SEM],
        out_shape=[_hbm_out(b.shape, b.dtype) for b in bufs]
        + [pltpu.SemaphoreType.DMA((3 * n,)), pltpu.SemaphoreType.DMA((3 * n,))],
        input_output_aliases={nw + i: i for i in range(n)}, compiler_params=_SPLIT,
    )(*shards, *bufs, after)
    return res[:n], res[n:]


def _gather_wait(name, shards, bufs, items, ids, sems, after):
    nw, m = len(shards), len(ids)

    def body(*refs):
        ins, bs = refs[:nw], refs[nw:nw + m]
        sem_refs = refs[nw + m:nw + m + 2]
        x, y, c = _position()
        peers = _chip_peers(x, y)
        for j, i in enumerate(ids):
            for k in range(3):
                px, py = peers[k]
                _weight_copy(ins, bs[j], items, sem_refs, i, k, 2 * px + py, peers, c).wait_recv()
        for j, i in enumerate(ids):
            for k in range(3):
                _weight_copy(ins, bs[j], items, sem_refs, i, k, 2 * x + y, peers, c).wait_send()

    res = pl.pallas_call(
        body, name=name, in_specs=[_ANY] * (nw + m) + [_SEM, _SEM, _ANY], out_specs=[_ANY] * m,
        out_shape=[_hbm_out(bufs[i].shape, bufs[i].dtype) for i in ids],
        input_output_aliases={nw + j: j for j in range(m)}, compiler_params=_SPLIT,
    )(*shards, *[bufs[i] for i in ids], *sems, after)
    return list(res)


def _forward_copy(src, dst, sems, i, k, core):
    x, y, c = _position()
    px, py = _chip_peers(x, y)[k]
    half = _half_rows(core, src.shape[1])
    return pltpu.make_async_remote_copy(
        src_ref=src.at[2 * px + py, half], dst_ref=dst.at[2 * px + py, half],
        send_sem=sems[0].at[3 * i + k], recv_sem=sems[1].at[3 * i + k],
        device_id=(x, y, 1 - c), device_id_type=MESH)


def _forward_start(name, bufs):
    n = len(bufs)

    def body(*refs):
        ins, outs, sems = refs[:n], refs[n:2 * n], refs[2 * n:]
        c = lax.axis_index("c")
        for i in range(n):
            for k in range(3):
                _forward_copy(ins[i], outs[i], sems, i, k, c).start()

    res = pl.pallas_call(
        body, name=name, in_specs=[_ANY] * n, out_specs=[_ANY] * n + [_SEM, _SEM],
        out_shape=[_hbm_out(g.shape, g.dtype) for g in bufs]
        + [pltpu.SemaphoreType.DMA((3 * n,)), pltpu.SemaphoreType.DMA((3 * n,))],
        input_output_aliases={i: i for i in range(n)}, compiler_params=_SPLIT,
    )(*bufs)
    return list(res[:n]), res[n:]


def _forward_wait(name, bufs, sems, after):
    n = len(bufs)

    def body(*refs):
        bs, sem_refs = refs[:n], refs[n:n + 2]
        c = lax.axis_index("c")
        for i in range(n):
            for k in range(3):
                _forward_copy(bs[i], bs[i], sem_refs, i, k, 1 - c).wait_recv()
        for i in range(n):
            for k in range(3):
                _forward_copy(bs[i], bs[i], sem_refs, i, k, c).wait_send()

    return list(pl.pallas_call(
        body, name=name, in_specs=[_ANY] * n + [_SEM, _SEM, _ANY], out_specs=[_ANY] * n,
        out_shape=[_hbm_out(g.shape, g.dtype) for g in bufs],
        input_output_aliases={i: i for i in range(n)}, compiler_params=_SPLIT,
    )(*bufs, *sems, after))


def _reduce_copy(grads, others, sems, i):
    x, y, c = _position()
    return pltpu.make_async_remote_copy(
        src_ref=grads[i].at[:, _half_rows(1 - c, grads[i].shape[1])], dst_ref=others[i],
        send_sem=sems[0].at[i], recv_sem=sems[1].at[i], device_id=(x, y, 1 - c), device_id_type=MESH)


def _reduce_start(name, grads, after):
    n = len(grads)

    def body(*refs):
        ins, outs, sems, token = refs[:n], refs[n + 1:2 * n + 1], refs[2 * n + 1:2 * n + 3], refs[2 * n + 3]
        for i in range(n):
            _reduce_copy(ins, outs, sems, i).start()
        token[...] = jnp.zeros_like(token)

    res = pl.pallas_call(
        body, name=name, in_specs=[_ANY] * (n + 1),
        out_specs=[_ANY] * n + [_SEM, _SEM, pl.BlockSpec(memory_space=pltpu.VMEM)],
        out_shape=[_hbm_out((N_CHIPS, g.shape[1] // 2, g.shape[2]), g.dtype) for g in grads]
        + [pltpu.SemaphoreType.DMA((n,)), pltpu.SemaphoreType.DMA((n,)), jax.ShapeDtypeStruct((SUBLANES, LANES), F32)],
        compiler_params=_SPLIT,
    )(*grads, after)
    return list(res[:n]), res[n:n + 2], res[n + 2]


def _reduce_wait(name, grads, others, sems):
    n = len(grads)

    def body(*refs):
        ins, os_, sem_refs = refs[:n], refs[n:2 * n], refs[2 * n:2 * n + 2]
        for i in range(n):
            _reduce_copy(ins, os_, sem_refs, i).wait_recv()
        for i in range(n):
            _reduce_copy(ins, os_, sem_refs, i).wait_send()

    return list(pl.pallas_call(
        body, name=name, in_specs=[_ANY] * (2 * n) + [_SEM, _SEM], out_specs=[_ANY] * n,
        out_shape=[_hbm_out(o.shape, o.dtype) for o in others],
        input_output_aliases={n + i: i for i in range(n)}, compiler_params=_SPLIT,
    )(*grads, *others, *sems))


def _sum_cores(name, g, other, core):
    _, R, C = g.shape
    H = R // 2
    tr = _row_tile(H, C, 3 * 2, target=12 << 20)
    nb = H // tr

    def body(c_ref, g_ref, o_ref, out_ref):
        out_ref[...] = (g_ref[...].astype(F32) + o_ref[...].astype(F32)).astype(out_ref.dtype)

    return pl.pallas_call(
        body, name=name,
        grid_spec=pltpu.PrefetchScalarGridSpec(
            num_scalar_prefetch=1, grid=(N_CHIPS, nb),
            in_specs=[pl.BlockSpec((None, tr, C), lambda j, i, c_ref: (j, c_ref[0] * nb + i, 0)),
                      pl.BlockSpec((None, tr, C), lambda j, i, c_ref: (j, i, 0))],
            out_specs=pl.BlockSpec((None, tr, C), lambda j, i, c_ref: (j, i, 0))),
        out_shape=_hbm_out((N_CHIPS, H, C), BF16),
        compiler_params=_params(("parallel", "parallel")),
    )(core, g, other)


def _sum_chips(name, received, own, full, layer, me_core):
    _, H, C = received.shape
    tr = _row_tile(H, C, 3 * 2 + 2 + 4, target=12 << 20)
    nb = H // tr

    def body(s_ref, r_ref, own_ref, full_ref, out_ref):
        acc = r_ref[0].astype(F32)
        for k in (1, 2):
            acc = acc + r_ref[k].astype(F32)
        out_ref[...] = acc + own_ref[...].astype(F32)

    return pl.pallas_call(
        body, name=name,
        grid_spec=pltpu.PrefetchScalarGridSpec(
            num_scalar_prefetch=1, grid=(nb,),
            in_specs=[pl.BlockSpec((3, tr, C), lambda i, s_ref: (0, i, 0)),
                      pl.BlockSpec((None, tr, C), lambda i, s_ref: (s_ref[0], i, 0)),
                      _ANY],
            out_specs=pl.BlockSpec((None, tr, C), lambda i, s_ref: (layer, s_ref[1] * nb + i, 0))),
        out_shape=_hbm_out(full.shape, full.dtype),
        input_output_aliases={3: 0},
        compiler_params=_params(("parallel",)),
    )(me_core, received, own, full)


def _part_copy(parts, recv, sems, i, k, peers, c):
    px, py = peers[k]
    return pltpu.make_async_remote_copy(
        src_ref=parts[i].at[2 * px + py], dst_ref=recv[i].at[k],
        send_sem=sems[0].at[3 * i + k], recv_sem=sems[1].at[3 * i + k],
        device_id=(px, py, c), device_id_type=MESH)


def _scatter_start(name, parts):
    n = len(parts)

    def body(*refs):
        ins, outs, sems, token = refs[:n], refs[n:2 * n], refs[2 * n:2 * n + 2], refs[2 * n + 2]
        x, y, c = _position()
        peers = _chip_peers(x, y)
        for i in range(n):
            for k in range(3):
                _part_copy(ins, outs, sems, i, k, peers, c).start()
        token[...] = jnp.zeros_like(token)

    res = pl.pallas_call(
        body, name=name, in_specs=[_ANY] * n,
        out_specs=[_ANY] * n + [_SEM, _SEM, pl.BlockSpec(memory_space=pltpu.VMEM)],
        out_shape=[_hbm_out((3,) + p.shape[1:], p.dtype) for p in parts]
        + [pltpu.SemaphoreType.DMA((3 * n,)), pltpu.SemaphoreType.DMA((3 * n,)),
           jax.ShapeDtypeStruct((SUBLANES, LANES), F32)],
        compiler_params=_SPLIT,
    )(*parts)
    return list(res[:n]), res[n:n + 2], res[n + 2]


def _scatter_wait(name, parts, recv, sems):
    n = len(parts)

    def body(*refs):
        ins, rs, sem_refs = refs[:n], refs[n:2 * n], refs[2 * n:2 * n + 2]
        x, y, c = _position()
        peers = _chip_peers(x, y)
        for i in range(n):
            for k in range(3):
                _part_copy(ins, rs, sem_refs, i, k, peers, c).wait_recv()
        for i in range(n):
            for k in range(3):
                _part_copy(ins, rs, sem_refs, i, k, peers, c).wait_send()

    return list(pl.pallas_call(
        body, name=name, in_specs=[_ANY] * (2 * n) + [_SEM, _SEM], out_specs=[_ANY] * n,
        out_shape=[_hbm_out(r.shape, r.dtype) for r in recv],
        input_output_aliases={n + i: i for i in range(n)}, compiler_params=_SPLIT,
    )(*parts, *recv, *sems))


def _share_d2d(name, full):
    n = len(full)

    def body(*refs):
        ins, outs = refs[:n], refs[n:2 * n]
        send_sems, recv_sems = refs[2 * n:]
        x, y, c = _position()

        def remote(w, core):
            half = _half_rows(core, ins[w].shape[1])
            return pltpu.make_async_remote_copy(
                src_ref=ins[w].at[:, half], dst_ref=outs[w].at[:, half],
                send_sem=send_sems.at[w], recv_sem=recv_sems.at[w],
                device_id=(x, y, 1 - c), device_id_type=MESH)

        for w in range(n):
            remote(w, c).start()
        for w in range(n):
            remote(w, 1 - c).wait_recv()
        for w in range(n):
            remote(w, c).wait_send()

    return pl.pallas_call(
        body, name=name, in_specs=[_ANY] * n, out_specs=[_ANY] * n,
        out_shape=[_hbm_out(f.shape, f.dtype) for f in full],
        input_output_aliases={w: w for w in range(n)},
        scratch_shapes=[pltpu.SemaphoreType.DMA((n,)), pltpu.SemaphoreType.DMA((n,))],
    )(*full)


def _gather_all(name, a):
    def body(a_ref, o_ref, send_sems, recv_sems, local_sem):
        x, y, c = _position()
        me = 4 * x + 2 * y + c

        def peer(k):
            return (x ^ ((k >> 2) & 1), y ^ ((k >> 1) & 1), c ^ (k & 1))

        def remote(k, slot):
            return pltpu.make_async_remote_copy(
                src_ref=a_ref, dst_ref=o_ref.at[slot], send_sem=send_sems.at[k - 1], recv_sem=recv_sems.at[k - 1],
                device_id=peer(k), device_id_type=MESH)

        local = pltpu.make_async_copy(a_ref, o_ref.at[me], local_sem)
        local.start()
        for k in range(1, N_DEV):
            remote(k, me).start()
        for k in range(1, N_DEV):
            px, py, pc = peer(k)
            remote(k, 4 * px + 2 * py + pc).wait_recv()
        for k in range(1, N_DEV):
            remote(k, me).wait_send()
        local.wait()

    return pl.pallas_call(
        body, name=name, in_specs=[_ANY], out_specs=_ANY,
        out_shape=_hbm_out((N_DEV,) + a.shape, a.dtype),
        scratch_shapes=[pltpu.SemaphoreType.DMA((N_DEV - 1,)), pltpu.SemaphoreType.DMA((N_DEV - 1,)),
                        pltpu.SemaphoreType.DMA],
    )(a)


def _rows2d(a, lead=0):
    return a.reshape(a.shape[:lead] + (-1, a.shape[-1]))


def _row_tile(rows, cols, itemsize=4, target=1 << 20):
    want = max(SUBLANES, target // (cols * itemsize))
    t = min(rows, (want // 16) * 16)
    while t > 16 and rows % t:
        t -= 16
    return t if rows % t == 0 else rows


def _sum_slots(name, r, out_dtype=F32):
    ns = r.shape[0]
    r2 = _rows2d(r, 1)
    _, rows, cols = r2.shape
    tr = _row_tile(rows, cols)

    def body(r_ref, o_ref):
        acc = r_ref[0].astype(F32)
        for s in range(1, ns):
            acc = acc + r_ref[s].astype(F32)
        o_ref[...] = acc.astype(o_ref.dtype)

    out = pl.pallas_call(
        body, name=name, grid=(rows // tr,),
        in_specs=[pl.BlockSpec((ns, tr, cols), lambda i: (0, i, 0))],
        out_specs=pl.BlockSpec((tr, cols), lambda i: (i, 0)),
        out_shape=_hbm_out((rows, cols), out_dtype),
        compiler_params=_params(("parallel",)),
    )(r2)
    return out.reshape(r.shape[1:])


def _adamw(name, g_parts, w, m, v):
    shape = w.shape
    ng = len(g_parts)
    args = [_rows2d(a) for a in (*g_parts, w, m, v)]
    rows, cols = args[0].shape
    tr = _row_tile(rows, cols, (ng + 7) * 4, target=16 << 20)
    c1 = 1.0 - ADAM_B1 ** ADAM_STEP
    c2 = 1.0 - ADAM_B2 ** ADAM_STEP

    def body(*refs):
        g = refs[0][...]
        for r in refs[1:ng]:
            g = g + r[...]
        w_ref, m_ref, v_ref = refs[ng:ng + 3]
        g_out, d_out, m_out, v_out = refs[ng + 3:]
        mn = ADAM_B1 * m_ref[...] + (1.0 - ADAM_B1) * g
        vn = ADAM_B2 * v_ref[...] + (1.0 - ADAM_B2) * (g * g)
        m_hat = mn / c1
        v_hat = vn / c2
        g_out[...] = g
        d_out[...] = -ADAM_LR * (m_hat / (jnp.sqrt(v_hat) + ADAM_EPS) + ADAM_WD * w_ref[...])
        m_out[...] = mn
        v_out[...] = vn

    spec = pl.BlockSpec((tr, cols), lambda i: (i, 0))
    outs = pl.pallas_call(
        body, name=name, grid=(rows // tr,), in_specs=[spec] * (ng + 3), out_specs=[spec] * 4,
        out_shape=[_hbm_out((rows, cols), F32)] * 4,
        compiler_params=_params(("parallel",)),
    )(*args)
    return tuple(o.reshape(shape) for o in outs)


_WEIGHTS = ["norm_mix", "norm_ffn", "w_ffn_in", "w_ffn_out", "w_rec_in", "conv_w", "conv_b", "w_lru_gates",
            "b_lru_gates", "lru_param", "w_rec_out", "norm_kv", "w_kvf", "b_forget", "w_q", "w_o", "norm_final"]
_BIG = ["w_ffn_in", "w_ffn_out", "w_rec_in", "w_lru_gates", "w_rec_out", "w_kvf", "w_q", "w_o"]


def _stack3(a):
    return a[None] if a.ndim == 2 else a.reshape(a.shape[0], -1, a.shape[-1])


def _pad_lanes(a, n):
    return jnp.pad(a, ((0, 0),) * (a.ndim - 1) + ((0, n - a.shape[-1]),))


def kernel(x, norm_mix, norm_ffn, w_ffn_in, w_ffn_out, w_rec_in, conv_w, conv_b, w_lru_gates, b_lru_gates, lru_param, w_rec_out, norm_kv, w_kvf, b_forget, w_q, w_o, norm_final, loss_target, m_norm_mix, m_norm_ffn, m_w_ffn_in, m_w_ffn_out, m_w_rec_in, m_conv_w, m_conv_b, m_w_lru_gates, m_b_lru_gates, m_lru_param, m_w_rec_out, m_norm_kv, m_w_kvf, m_b_forget, m_w_q, m_w_o, m_norm_final, v_norm_mix, v_norm_ffn, v_w_ffn_in, v_w_ffn_out, v_w_rec_in, v_conv_w, v_conv_b, v_w_lru_gates, v_b_lru_gates, v_lru_param, v_w_rec_out, v_norm_kv, v_w_kvf, v_b_forget, v_w_q, v_w_o, v_norm_final):
    P = dict(norm_mix=norm_mix, norm_ffn=norm_ffn, w_ffn_in=w_ffn_in, w_ffn_out=w_ffn_out, w_rec_in=w_rec_in,
             conv_w=conv_w, conv_b=conv_b, w_lru_gates=w_lru_gates, b_lru_gates=b_lru_gates, lru_param=lru_param,
             w_rec_out=w_rec_out, norm_kv=norm_kv, w_kvf=w_kvf, b_forget=b_forget, w_q=w_q, w_o=w_o,
             norm_final=norm_final)
    M1 = dict(norm_mix=m_norm_mix, norm_ffn=m_norm_ffn, w_ffn_in=m_w_ffn_in, w_ffn_out=m_w_ffn_out,
              w_rec_in=m_w_rec_in, conv_w=m_conv_w, conv_b=m_conv_b, w_lru_gates=m_w_lru_gates,
              b_lru_gates=m_b_lru_gates, lru_param=m_lru_param, w_rec_out=m_w_rec_out, norm_kv=m_norm_kv,
              w_kvf=m_w_kvf, b_forget=m_b_forget, w_q=m_w_q, w_o=m_w_o, norm_final=m_norm_final)
    M2 = dict(norm_mix=v_norm_mix, norm_ffn=v_norm_ffn, w_ffn_in=v_w_ffn_in, w_ffn_out=v_w_ffn_out,
              w_rec_in=v_w_rec_in, conv_w=v_conv_w, conv_b=v_conv_b, w_lru_gates=v_w_lru_gates,
              b_lru_gates=v_b_lru_gates, lru_param=v_lru_param, w_rec_out=v_w_rec_out, norm_kv=v_norm_kv,
              w_kvf=v_w_kvf, b_forget=v_b_forget, w_q=v_w_q, w_o=v_w_o, norm_final=v_norm_final)

    _, S, D = x.shape
    L = norm_mix.shape[0]
    NA, NBLK, BW, GS = w_lru_gates.shape
    NB = w_q.shape[0]
    C = NBLK * BW
    CS = C // N_CHIPS
    H = b_forget.shape[0]
    assert C == D and H * HEAD_DIM == D and H <= LANES
    chip = 2 * lax.axis_index("x") + lax.axis_index("y")

    small_a = jnp.concatenate([conv_w, conv_b[:, None], lru_param[:, None]], axis=1)
    small_a, b_gates = _gather_smalls("gather_smalls", [small_a, b_lru_gates])
    small_a = small_a.transpose(1, 2, 0, 3).reshape(NA, 6, C)
    b_gates = b_gates.transpose(1, 2, 0, 3).reshape(NA, NBLK, 1, N_CHIPS * GS)
    shards = [_stack3(P[w]).astype(BF16) for w in _BIG]
    core = lax.axis_index("c")
    chip_id = jnp.reshape(chip, (1,)).astype(jnp.int32)
    core_id = jnp.reshape(core, (1,)).astype(jnp.int32)
    me_core = jnp.stack([chip, core]).astype(jnp.int32)

    def stage_items(l, part):
        if part == "ffn":
            return [(_BIG.index("w_ffn_in"), l), (_BIG.index("w_ffn_out"), l)]
        if l < NA:
            names, at = ["w_rec_in", "w_lru_gates", "w_rec_out"], l
        else:
            names, at = (["w_kvf"] if l == NA else []) + ["w_q", "w_o"], l - NA
        return [(_BIG.index(n), 0 if n == "w_kvf" else at) for n in names]

    stages = [(l, part) for l in range(L) for part in ("mix", "ffn")]
    items = [it for st in stages for it in stage_items(*st)]
    ids_of = {st: [items.index(it) for it in stage_items(*st)] for st in stages}
    bufs = [_place_own(f"place_{_BIG[w]}_{li}", shards[w], li, chip_id) for w, li in items]
    bufs, gather_sems = _gather_start("gather_start", shards, bufs, items, small_a)

    forwarding = {}

    def layer_prefetch(l, part, after):
        if l < L and (l, part) not in forwarding:
            ids = ids_of[(l, part)]
            got = _gather_wait(f"gather_wait_{part}_{l}", shards, bufs, items, ids, gather_sems, after)
            forwarding[(l, part)] = _forward_start(f"forward_start_{part}_{l}", got)

    def layer_weights(l, part, after):
        if l >= L:
            return None
        layer_prefetch(l, part, after)
        ids = ids_of[(l, part)]
        got, sems = forwarding[(l, part)]
        got = _forward_wait(f"forward_wait_{part}_{l}", got, sems, after)
        B = {_BIG[items[i][0]]: g for i, g in zip(ids, got)}
        if part == "ffn":
            return dict(w_ffn_in=B["w_ffn_in"], w_ffn_out=B["w_ffn_out"].reshape(-1, D))
        W = {}
        if l < NA:
            W.update(w_rec_in=B["w_rec_in"],
                     w_gates=B["w_lru_gates"].reshape(N_CHIPS, NBLK, BW, GS).transpose(1, 2, 0, 3).reshape(
                         NBLK, BW, N_CHIPS * GS),
                     b_gates=b_gates[l], w_rec_out=B["w_rec_out"].reshape(C, D),
                     conv_w=small_a[l, :4], conv_b=small_a[l, 4:5], lru_param=small_a[l, 5:6])
        else:
            W.update(w_q=B["w_q"].reshape(D, D), w_o=B["w_o"].reshape(D, D))
            if l == NA:
                w_kvf_full = B["w_kvf"].transpose(1, 0, 2).reshape(D, -1)
                W.update(norm_kv=norm_kv[None], w_kv=w_kvf_full[:, :2 * D],
                         w_f=_pad_lanes(w_kvf_full[:, 2 * D:], LANES), b_f=_pad_lanes(b_forget[None], LANES))
        return W

    G_small = {l: {} for l in range(L)}
    pending = {}
    reducing = []

    def finish_reduce():
        l, part, its, grads, others, sems = reducing.pop()
        others = _reduce_wait(f"reduce_wait_{part}_{l}", grads, others, sems)
        parts = [_sum_cores(f"sum_cores_{l}_{_BIG[w]}", g, o, core_id) for (w, _), g, o in zip(its, grads, others)]
        recv, sems, token = _scatter_start(f"scatter_start_{part}_{l}", parts)
        pending[(l, part)] = (parts, recv, sems)
        return token

    def layer_grads(l, part, G):
        G_small[l].update(G)
        after = finish_reduce() if reducing else jnp.zeros((SUBLANES, LANES), F32)
        by_name = dict(
            w_ffn_in=lambda: G["w_ffn_in"], w_ffn_out=lambda: G["w_ffn_out"].reshape(N_CHIPS, -1, D),
            w_rec_in=lambda: G["w_rec_in"],
            w_lru_gates=lambda: G["w_gates"].reshape(NBLK, BW, N_CHIPS, GS).transpose(2, 0, 1, 3).reshape(
                N_CHIPS, NBLK * BW, GS),
            w_rec_out=lambda: G["w_rec_out"].reshape(N_CHIPS, -1, D),
            w_kvf=lambda: jnp.concatenate([G["w_kv"].astype(F32), G["w_f"][:, :H]], axis=1).reshape(
                D, N_CHIPS, -1).transpose(1, 0, 2).astype(BF16),
            w_q=lambda: G["w_q"].reshape(N_CHIPS, -1, D), w_o=lambda: G["w_o"].reshape(N_CHIPS, -1, D))
        its = stage_items(l, part)
        grads = [by_name[_BIG[w]]() for w, _ in its]
        others, sems, token = _reduce_start(f"reduce_start_{part}_{l}", grads, after)
        reducing.append((l, part, its, grads, others, sems))
        return finish_reduce() if l == 0 else token

    gains = dict(mix=[norm_mix[l][None] for l in range(L)], ffn=[norm_ffn[l][None] for l in range(L)],
                 final=norm_final[None])
    loss_row, grad_x, dg_final = _local_step(x.reshape(S, D), loss_target.reshape(S, D), gains,
                                             layer_weights, layer_prefetch, layer_grads)

    rows = [*[G_small[l]["norm_mix"] for l in range(L)], *[G_small[l]["norm_ffn"] for l in range(L)],
            G_small[NA]["norm_kv"], dg_final, _pad_lanes(G_small[NA]["b_f"], D), _pad_lanes(loss_row, D)]
    for a in range(NA):
        rows += [G_small[a][n] for n in ("conv_w", "conv_b", "b_gi", "b_gr", "lru_param")]
    packed = jnp.concatenate(rows, axis=0)
    tot = _sum_slots("sum_small", _gather_all("gather_small", packed))
    loss = tot[2 * L + 3, 0]
    g_rep = jnp.concatenate([tot[:2 * L + 2], tot[2 * L + 2:2 * L + 3]], axis=0)
    base = 2 * L + 4
    g_sh = []
    for a in range(NA):
        blk = lax.dynamic_slice_in_dim(tot[base + 8 * a:base + 8 * a + 8], chip * CS, CS, axis=1)
        gi = tot[base + 8 * a + 5].reshape(NBLK, BW)
        gr = tot[base + 8 * a + 6].reshape(NBLK, BW)
        bl = lax.dynamic_slice_in_dim(jnp.concatenate([gi, gr], axis=1), chip * GS, GS, axis=1)
        g_sh += [blk[:5], bl.reshape(-1, CS), blk[7:8]]
    g_sh = jnp.concatenate(g_sh, axis=0)
    nrow = g_sh.shape[0] // NA

    def pack_rep(T):
        return jnp.concatenate([T["norm_mix"], T["norm_ffn"], T["norm_kv"][None], T["norm_final"][None],
                                _pad_lanes(T["b_forget"][None], D)], axis=0)

    def pack_sh(T):
        return jnp.concatenate([jnp.concatenate([T["conv_w"][a], T["conv_b"][a][None],
                                                 T["b_lru_gates"][a].reshape(-1, CS), T["lru_param"][a][None]], axis=0)
                                for a in range(NA)], axis=0)

    rep = _adamw("adamw_replicated", [g_rep], pack_rep(P), pack_rep(M1), pack_rep(M2))
    shd = _adamw("adamw_small_sharded", [g_sh], pack_sh(P), pack_sh(M1), pack_sh(M2))

    def unpack_rep(t):
        return dict(norm_mix=t[:L], norm_ffn=t[L:2 * L], norm_kv=t[2 * L], norm_final=t[2 * L + 1],
                    b_forget=t[2 * L + 2, :H])

    def unpack_sh(t):
        t = t.reshape(NA, nrow, CS)
        return dict(conv_w=t[:, :4], conv_b=t[:, 4], b_lru_gates=t[:, 5:nrow - 1].reshape(NA, NBLK, GS),
                    lru_param=t[:, nrow - 1])

    full = [lax.empty(sh.shape, F32) for sh in shards]
    for l, part in reversed(stages):
        parts, recv, sems = pending[(l, part)]
        recv = _scatter_wait(f"scatter_wait_{part}_{l}", parts, recv, sems)
        for (w, li), own, r in zip(stage_items(l, part), parts, recv):
            full[w] = _sum_chips(f"sum_chips_{l}_{_BIG[w]}", r, own, full[w], li, me_core)
    full = _share_d2d("share_d2d", full)
    big = {w: _adamw(f"adamw_{w}", [g.reshape(P[w].shape)], P[w], M1[w], M2[w]) for w, g in zip(_BIG, full)}

    outs = []
    for i in range(4):
        small = {**unpack_rep(rep[i]), **unpack_sh(shd[i])}
        outs.append([big[w][i] if w in big else small[w] for w in _WEIGHTS])
    return (loss, grad_x.reshape(1, S, D), *outs[0], *outs[1], *outs[2], *outs[3])
```

```python
import functools
import math

import jax
import jax.numpy as jnp
from jax import lax
from jax.experimental import pallas as pl
from jax.experimental.pallas import tpu as pltpu

F32 = jnp.float32
BF16 = jnp.bfloat16

EPS = 1e-6
LRU_C = 8.0
HEAD_DIM = 64
LANES = 128
SUBLANES = 8
VMEM_LIMIT = 48 * 1024 * 1024
N_CHIPS = 4
N_DEV = 8

ADAM_LR = 0.001
ADAM_B1 = 0.9
ADAM_B2 = 0.999
ADAM_EPS = 1e-08
ADAM_WD = 0.01
ADAM_STEP = 10

_NN = (((1,), (0,)), ((), ()))
_NT = (((1,), (1,)), ((), ()))
_TN = (((0,), (0,)), ((), ()))
_DN = {"nn": _NN, "nt": _NT, "tn": _TN}
MESH = pl.DeviceIdType.MESH


def _hbm_out(shape, dtype):
    return pltpu.HBM(shape, dtype)


def _params(sem):
    return pltpu.CompilerParams(dimension_semantics=sem, vmem_limit_bytes=VMEM_LIMIT)


def _tile(n, want):
    if n <= want:
        return n
    t = (want // LANES) * LANES
    while t >= LANES:
        if n % t == 0:
            return t
        t -= LANES
    return n


def _sigmoid(x):
    return 1.0 / (1.0 + jnp.exp(-x))


def _sigmoid_t(x):
    return 0.5 * jnp.tanh(0.5 * x) + 0.5


def _softplus(x):
    return jnp.maximum(x, 0.0) + jnp.log(1.0 + jnp.exp(-jnp.abs(x)))


_GELU_C = math.sqrt(2.0 / math.pi)


def _gelu_and_grad(x):
    inner = _GELU_C * (x + 0.044715 * x * x * x)
    t = jnp.tanh(inner)
    g = 0.5 * x * (1.0 + t)
    dg = 0.5 * (1.0 + t) + 0.5 * x * (1.0 - t * t) * _GELU_C * (1.0 + 3.0 * 0.044715 * x * x)
    return g, dg


def _rms(x):
    return lax.rsqrt(jnp.mean(x * x, axis=-1, keepdims=True) + EPS)


def _rms_bwd(dy, x, g):
    r = _rms(x)
    xr = x * r
    dyg = dy * g
    return r * dyg - xr * (r * jnp.mean(dyg * xr, axis=-1, keepdims=True)), jnp.sum(dy * xr, axis=0, keepdims=True)


def _mm(name, mode, a, b, *, grid, a_spec, b_spec, out_shape, out_dtype, out_spec, nk=1,
        res=None, res_spec=None, bias=None, bias_spec=None, scale=None, norm_gain=None, norm_bwd=None):
    dn = _DN[mode]
    has_res, has_bias = res is not None, bias is not None
    blk = tuple(d for d in out_spec.block_shape if d is not None)
    vec = pl.BlockSpec((1, blk[-1]), lambda *g: (0, 0))
    a_specs = a_spec if isinstance(a_spec, list) else [a_spec]
    b_specs = b_spec if isinstance(b_spec, list) else [b_spec]
    npair = len(a_specs)
    n_in = 2 * npair + int(has_res) + int(has_bias) + (1 if norm_gain is not None else 0) + (3 if norm_bwd else 0)

    def body(*refs):
        p = 2 * npair
        r_ref = refs[p] if has_res else None
        p += int(has_res)
        bias_ref = refs[p] if has_bias else None
        p += int(has_bias)
        extra = refs[p:n_in]
        outs = refs[n_in:]
        o_ref = outs[0]
        part = lax.dot_general(refs[0][...], refs[npair][...], dn, preferred_element_type=F32)
        for t in range(1, npair):
            part = part + lax.dot_general(refs[t][...], refs[npair + t][...], dn, preferred_element_type=F32)

        def finish(acc):
            if scale is not None:
                acc = acc * scale
            if has_bias:
                acc = acc + bias_ref[...]
            if has_res:
                acc = r_ref[...] + acc
            if norm_bwd:
                h_ref, g_ref, dh_ref = extra
                dx, dg = _rms_bwd(acc, h_ref[...], g_ref[...])
                acc = dh_ref[...] + dx
                outs[1][...] = acc.astype(BF16)
                outs[2][...] = dg
            if norm_gain is not None:
                outs[1][...] = (acc * _rms(acc) * extra[0][...]).astype(BF16)
            o_ref[...] = acc.astype(o_ref.dtype)

        if nk == 1:
            finish(part)
        else:
            acc_ref = refs[-1]
            k = pl.program_id(2)

            @pl.when(k == 0)
            def _():
                acc_ref[...] = part

            @pl.when(k > 0)
            def _():
                acc_ref[...] += part

            @pl.when(k == nk - 1)
            def _():
                finish(acc_ref[...])

    ins, specs = [a] * npair + [b] * npair, a_specs + b_specs
    if has_res:
        ins.append(res)
        specs.append(res_spec)
    if has_bias:
        ins.append(bias)
        specs.append(bias_spec)
    out_specs, out_shapes = [out_spec], [_hbm_out(out_shape, out_dtype)]
    if norm_gain is not None:
        ins.append(norm_gain)
        specs.append(vec)
        out_specs.append(out_spec)
        out_shapes.append(_hbm_out(out_shape, BF16))
    if norm_bwd:
        h, g, dh = norm_bwd
        ins += [h, g, dh]
        specs += [out_spec, vec, out_spec]
        out_specs += [out_spec, pl.BlockSpec((None, 1, blk[-1]), lambda i, *rest: (i, 0, 0))]
        out_shapes += [_hbm_out(out_shape, BF16), _hbm_out((grid[0], 1, blk[-1]), F32)]
    sem = ("parallel", "parallel") + (("arbitrary",) if len(grid) == 3 else ())
    single = len(out_specs) == 1
    return pl.pallas_call(
        body, name=name, grid=grid, in_specs=specs, out_specs=out_specs[0] if single else out_specs,
        out_shape=out_shapes[0] if single else out_shapes,
        scratch_shapes=[pltpu.VMEM(blk, F32)] if nk > 1 else [],
        compiler_params=_params(sem),
    )(*ins)


def _mm_nn(name, a, b, *, b_lead=(), out_dtype, tm=512, tn=512, res=None, bias=None, scale=None, norm_gain=None):
    M, K = a.shape
    N = b.shape[-1]
    tm, tn = _tile(M, tm), _tile(N, tn)
    nl = len(b_lead)
    return _mm(
        name, "nn", a, b, grid=(M // tm, N // tn),
        a_spec=pl.BlockSpec((tm, K), lambda i, j: (i, 0)),
        b_spec=pl.BlockSpec((None,) * nl + (K, tn), lambda i, j: tuple(b_lead) + (0, j)),
        out_shape=(M, N), out_dtype=out_dtype, out_spec=pl.BlockSpec((tm, tn), lambda i, j: (i, j)),
        res=res, res_spec=pl.BlockSpec((tm, tn), lambda i, j: (i, j)),
        bias=bias, bias_spec=pl.BlockSpec((1, tn), lambda i, j: (0, j)), scale=scale, norm_gain=norm_gain)


def _mm_nt(name, a, b, *, b_lead=(), out_dtype, tm=512, tn=512, tk=2048, res=None, norm_bwd=None):
    M, K = a.shape
    N = b.shape[-2]
    tm, tn, tk = _tile(M, tm), _tile(N, tn), _tile(K, tk)
    nk = K // tk
    nl = len(b_lead)
    return _mm(
        name, "nt", a, b, grid=(M // tm, N // tn, nk), nk=nk,
        a_spec=pl.BlockSpec((tm, tk), lambda i, j, k: (i, k)),
        b_spec=pl.BlockSpec((None,) * nl + (tn, tk), lambda i, j, k: tuple(b_lead) + (j, k)),
        out_shape=(M, N), out_dtype=out_dtype, out_spec=pl.BlockSpec((tm, tn), lambda i, j, k: (i, j)),
        res=res, res_spec=pl.BlockSpec((tm, tn), lambda i, j, k: (i, j)), norm_bwd=norm_bwd)


def _mm_tn(name, a, b, *, out_dtype, tm=512, tn=512):
    S, M = a.shape
    N = b.shape[1]
    tm, tn = _tile(M, tm), _tile(N, tn)
    return _mm(
        name, "tn", a, b, grid=(M // tm, N // tn),
        a_spec=pl.BlockSpec((S, tm), lambda i, j: (0, i)),
        b_spec=pl.BlockSpec((S, tn), lambda i, j: (0, j)),
        out_shape=(M, N), out_dtype=out_dtype, out_spec=pl.BlockSpec((tm, tn), lambda i, j: (i, j)))


def _rmsnorm_fwd(name, h, g, tr=256):
    S, D = h.shape
    tr = _tile(S, tr)

    def body(h_ref, g_ref, o_ref):
        x = h_ref[...]
        r = lax.rsqrt(jnp.mean(x * x, axis=-1, keepdims=True) + EPS)
        o_ref[...] = (x * r * g_ref[...]).astype(o_ref.dtype)

    return pl.pallas_call(
        body, name=name, grid=(S // tr,),
        in_specs=[pl.BlockSpec((tr, D), lambda i: (i, 0)), pl.BlockSpec((1, D), lambda i: (0, 0))],
        out_specs=pl.BlockSpec((tr, D), lambda i: (i, 0)),
        out_shape=_hbm_out((S, D), BF16),
        compiler_params=_params(("parallel",)),
    )(h, g)


def _loss_head(name, h, target, g, tr=256):
    S, D = h.shape
    tr = _tile(S, tr)

    def body(h_ref, t_ref, g_ref, o_ref, ob_ref, dg_ref, loss_ref):
        i = pl.program_id(0)
        x = h_ref[...]
        gg = g_ref[...]
        r = lax.rsqrt(jnp.mean(x * x, axis=-1, keepdims=True) + EPS)
        xr = x * r
        err = xr * gg - t_ref[...]
        lpart = 0.5 * jnp.sum(jnp.mean(err * err, axis=-1, keepdims=True), axis=0, keepdims=True)
        dy = err * (1.0 / D)
        dyg = dy * gg
        dx = r * dyg - xr * (r * jnp.mean(dyg * xr, axis=-1, keepdims=True))
        o_ref[...] = dx
        ob_ref[...] = dx.astype(BF16)
        part = jnp.sum(dy * xr, axis=0, keepdims=True)
        lrow = jnp.broadcast_to(lpart, (1, LANES))

        @pl.when(i == 0)
        def _():
            dg_ref[...] = part
            loss_ref[...] = lrow

        @pl.when(i > 0)
        def _():
            dg_ref[...] += part
            loss_ref[...] += lrow

    row = pl.BlockSpec((tr, D), lambda i: (i, 0))
    vec = pl.BlockSpec((1, D), lambda i: (0, 0))
    return pl.pallas_call(
        body, name=name, grid=(S // tr,),
        in_specs=[row, row, vec], out_specs=[row, row, vec, pl.BlockSpec((1, LANES), lambda i: (0, 0))],
        out_shape=[_hbm_out((S, D), F32), _hbm_out((S, D), BF16),
                   _hbm_out((1, D), F32), _hbm_out((1, LANES), F32)],
        compiler_params=_params(("arbitrary",)),
    )(h, target, g)


def _swiglu_fwd(name, hn, w_in, tm=512):
    S, D = hn.shape
    FH = w_in.shape[-1]
    tm = _tile(S, tm)

    def body(x_ref, wg_ref, wu_ref, z_ref, a_ref):
        x = x_ref[...]
        zg = jnp.dot(x, wg_ref[...], preferred_element_type=F32)
        zu = jnp.dot(x, wu_ref[...], preferred_element_type=F32)
        z_ref[0] = zg.astype(z_ref.dtype)
        z_ref[1] = zu.astype(z_ref.dtype)
        a_ref[...] = (zg * _sigmoid_t(zg) * zu).astype(a_ref.dtype)

    return pl.pallas_call(
        body, name=name, grid=(S // tm, 2),
        in_specs=[pl.BlockSpec((tm, D), lambda i, j: (i, 0)),
                  pl.BlockSpec((None, D, FH), lambda i, j: (j, 0, 0)),
                  pl.BlockSpec((None, D, FH), lambda i, j: (j + 2, 0, 0))],
        out_specs=[pl.BlockSpec((2, tm, FH), lambda i, j: (0, i, j)), pl.BlockSpec((tm, FH), lambda i, j: (i, j))],
        out_shape=[_hbm_out((2, S, 2 * FH), BF16), _hbm_out((S, 2 * FH), BF16)],
        compiler_params=_params(("parallel", "parallel")),
    )(hn, w_in, w_in)


def _swiglu_bwd(name, dhb, w_out, z3, tm=512):
    S, D = dhb.shape
    F = w_out.shape[0]
    FH = F // 2
    tm = _tile(S, tm)

    def body(d_ref, w_ref, z_ref, dz_ref):
        d = lax.dot_general(d_ref[...], w_ref[...], _NT, preferred_element_type=F32)
        zg = z_ref[0].astype(F32)
        zu = z_ref[1].astype(F32)
        sg = _sigmoid_t(zg)
        dz_ref[0] = (d * zu * (sg * (1.0 + zg * (1.0 - sg)))).astype(dz_ref.dtype)
        dz_ref[1] = (d * (zg * sg)).astype(dz_ref.dtype)

    zspec = pl.BlockSpec((2, tm, FH), lambda i, j: (0, i, j))
    return pl.pallas_call(
        body, name=name, grid=(S // tm, 2),
        in_specs=[pl.BlockSpec((tm, D), lambda i, j: (i, 0)), pl.BlockSpec((FH, D), lambda i, j: (j, 0)), zspec],
        out_specs=zspec, out_shape=_hbm_out((2, S, F), BF16),
        compiler_params=_params(("parallel", "parallel")),
    )(dhb, w_out, z3)


SCAN_ROWS = 64


def _group_scan(A, B, reverse):
    n = A.shape[0]
    sub = lax.broadcasted_iota(jnp.int32, A.shape, 0) % SUBLANES
    for d in (1, 2, 4):
        if reverse:
            A_sh, B_sh = pltpu.roll(A, n - d, 0), pltpu.roll(B, n - d, 0)
            keep = sub < SUBLANES - d
        else:
            A_sh, B_sh = pltpu.roll(A, d, 0), pltpu.roll(B, d, 0)
            keep = sub >= d
        B = jnp.where(keep, A * B_sh + B, B)
        A = jnp.where(keep, A * A_sh, A)
    return A, B


def _block_scan(a, u, carry, reverse):
    A, B = _group_scan(a, u, reverse)
    ng = a.shape[0] // SUBLANES
    out = [None] * ng
    order = range(ng - 1, -1, -1) if reverse else range(ng)
    for gi in order:
        sl = slice(gi * SUBLANES, (gi + 1) * SUBLANES)
        hg = A[sl] * carry + B[sl]
        out[gi] = hg
        carry = hg[0:1] if reverse else hg[SUBLANES - 1:SUBLANES]
    return jnp.concatenate(out, axis=0), carry


def _lru_gates(rc, gip, grp, sp):
    gi = _sigmoid_t(gip)
    gr = _sigmoid_t(grp)
    la = -LRU_C * gr * sp
    a = jnp.exp(la)
    om = -jnp.tanh(la) * (a * a + 1.0)
    mult = jnp.sqrt(om)
    return gi, gr, a, mult


def _lru_fwd(name, proj, rc, gip, grp, lru_p, tc=256):
    S, C = rc.shape
    tc = _tile(C, tc)
    nb = S // SCAN_ROWS

    def body(gb_ref, rc_ref, gi_ref, gr_ref, l_ref, h_ref, m_ref):
        sp = _softplus(-l_ref[...])

        def step(b, carry):
            rows = pl.ds(pl.multiple_of(b * SCAN_ROWS, SCAN_ROWS), SCAN_ROWS)
            rcb = rc_ref[rows, :]
            gi, _, a, mult = _lru_gates(rcb, gi_ref[rows, :], gr_ref[rows, :], sp)
            h, carry = _block_scan(a, rcb * gi * mult, carry, False)
            h_ref[rows, :] = h
            gel, _ = _gelu_and_grad(gb_ref[rows, :])
            m_ref[rows, :] = (gel * h).astype(m_ref.dtype)
            return carry

        lax.fori_loop(0, nb, step, jnp.zeros((1, tc), F32))

    col = pl.BlockSpec((S, tc), lambda j: (0, j))
    return pl.pallas_call(
        body, name=name, grid=(C // tc,),
        in_specs=[col, col, col, col, pl.BlockSpec((1, tc), lambda j: (0, j))],
        out_specs=[col, col],
        out_shape=[_hbm_out((S, C), F32), _hbm_out((S, C), BF16)],
        compiler_params=_params(("parallel",)),
    )(proj, rc, gip, grp, lru_p)


def _lru_bwd(name, dm, proj, hrec, rc, gip, grp, lru_p, tc=256):
    S, C = rc.shape
    tc = _tile(C, tc)
    nb = S // SCAN_ROWS
    R = SCAN_ROWS

    def body(dm_ref, gb_ref, h_ref, rc_ref, gi_ref, gr_ref, l_ref,
             dgb_ref, dgi_ref, dgr_ref, drc_ref, dbi_ref, dbr_ref, dl_ref):
        lp = l_ref[...]
        sp = _softplus(-lp)
        row = lax.broadcasted_iota(jnp.int32, (R, tc), 0)
        zero = jnp.zeros((1, tc), F32)

        def step(t, carry):
            mu_in, s_i, s_r, s_sp = carry
            b = nb - 1 - t
            r0 = pl.multiple_of(b * R, R)
            rows = pl.ds(r0, R)
            rcb = rc_ref[rows, :]
            gi, gr, a, mult = _lru_gates(rcb, gi_ref[rows, :], gr_ref[rows, :], sp)
            gel, dgel = _gelu_and_grad(gb_ref[rows, :])
            dmb = dm_ref[rows, :]
            h = h_ref[rows, :]
            dgb_ref[rows, :] = (dmb * h * dgel).astype(dgb_ref.dtype)
            dh = dmb * gel
            mu, mu_out = _block_scan(a, a * dh, mu_in, True)
            mu_next = jnp.where(row == R - 1, mu_in, pltpu.roll(mu, R - 1, 0))
            lam = dh + mu_next
            p0 = pl.multiple_of(jnp.maximum(r0 - SUBLANES, 0), SUBLANES)
            prev = h_ref[pl.ds(p0, SUBLANES), :][SUBLANES - 1:SUBLANES]
            prev = jnp.where(b > 0, prev, 0.0)
            h_prev = jnp.where(row == 0, prev, pltpu.roll(h, 1, 0))
            da = lam * h_prev
            d_mult = lam * rcb * gi
            d_la = da * a - d_mult * (a * a) / mult
            d_grp = d_la * (-LRU_C * sp) * gr * (1.0 - gr)
            d_gip = lam * rcb * mult * gi * (1.0 - gi)
            dgr_ref[rows, :] = d_grp.astype(dgr_ref.dtype)
            dgi_ref[rows, :] = d_gip.astype(dgi_ref.dtype)
            drc_ref[rows, :] = lam * gi * mult
            s_i = s_i + jnp.sum(d_gip, axis=0, keepdims=True)
            s_r = s_r + jnp.sum(d_grp, axis=0, keepdims=True)
            s_sp = s_sp + jnp.sum(d_la * gr, axis=0, keepdims=True)
            return mu_out, s_i, s_r, s_sp

        _, s_i, s_r, s_sp = lax.fori_loop(0, nb, step, (zero, zero, zero, zero))
        dbi_ref[...] = s_i
        dbr_ref[...] = s_r
        dl_ref[...] = (-LRU_C * s_sp) * (-_sigmoid(-lp))

    col = pl.BlockSpec((S, tc), lambda j: (0, j))
    vec = pl.BlockSpec((1, tc), lambda j: (0, j))
    return pl.pallas_call(
        body, name=name, grid=(C // tc,),
        in_specs=[col, col, col, col, col, col, vec],
        out_specs=[col, col, col, col, vec, vec, vec],
        out_shape=[_hbm_out((S, C), BF16), _hbm_out((S, C), BF16),
                   _hbm_out((S, C), BF16), _hbm_out((S, C), F32),
                   _hbm_out((1, C), F32), _hbm_out((1, C), F32),
                   _hbm_out((1, C), F32)],
        compiler_params=_params(("parallel",)),
    )(dm, proj, hrec, rc, gip, grp, lru_p)


def _cumsum_rows(name, u, reverse):
    S, C = u.shape
    nb = S // SCAN_ROWS

    def body(u_ref, o_ref):
        def step(t, carry):
            b = nb - 1 - t if reverse else t
            rows = pl.ds(pl.multiple_of(b * SCAN_ROWS, SCAN_ROWS), SCAN_ROWS)
            ub = u_ref[rows, :]
            h, carry = _block_scan(jnp.ones_like(ub), ub, carry, reverse)
            o_ref[rows, :] = h
            return carry

        lax.fori_loop(0, nb, step, jnp.zeros((1, C), F32))

    spec = pl.BlockSpec((S, C), lambda i: (0, 0))
    return pl.pallas_call(
        body, name=name, grid=(1,), in_specs=[spec], out_specs=spec,
        out_shape=_hbm_out((S, C), F32),
        compiler_params=_params(("arbitrary",)),
    )(u)


def _shift_down(x, k):
    row = lax.broadcasted_iota(jnp.int32, x.shape, 0)
    return jnp.where(row >= k, pltpu.roll(x, k, 0), 0.0)


def _shift_up(x, k):
    n = x.shape[0]
    row = lax.broadcasted_iota(jnp.int32, x.shape, 0)
    return jnp.where(row < n - k, pltpu.roll(x, n - k, 0), 0.0)


def _conv_fwd(name, proj, w, b, tc=256):
    S, C2 = proj.shape
    C = C2 // 2
    tc = _tile(C, tc)
    off = C // tc

    def body(x_ref, w_ref, b_ref, o_ref, ob_ref):
        x = x_ref[...]
        out = b_ref[...] + w_ref[3:4, :] * x
        for k in (1, 2, 3):
            out = out + w_ref[3 - k:4 - k, :] * _shift_down(x, k)
        o_ref[...] = out
        ob_ref[...] = out.astype(BF16)

    col = pl.BlockSpec((S, tc), lambda j: (0, j))
    return pl.pallas_call(
        body, name=name, grid=(C // tc,),
        in_specs=[pl.BlockSpec((S, tc), lambda j: (0, off + j)),
                  pl.BlockSpec((4, tc), lambda j: (0, j)), pl.BlockSpec((1, tc), lambda j: (0, j))],
        out_specs=[col, col],
        out_shape=[_hbm_out((S, C), F32), _hbm_out((S, C), BF16)],
        compiler_params=_params(("parallel",)),
    )(proj, w, b)


def _conv_bwd(name, drc, proj, w, tc=256):
    S, C = drc.shape
    tc = _tile(C, tc)
    off = C // tc

    def body(y_ref, x_ref, w_ref, dx_ref, dw_ref, db_ref):
        y = y_ref[...]
        x = x_ref[...]
        dx = w_ref[3:4, :] * y
        dw_ref[3:4, :] = jnp.sum(y * x, axis=0, keepdims=True)
        for k in (1, 2, 3):
            dx = dx + w_ref[3 - k:4 - k, :] * _shift_up(y, k)
            dw_ref[3 - k:4 - k, :] = jnp.sum(y * _shift_down(x, k), axis=0, keepdims=True)
        dx_ref[...] = dx.astype(dx_ref.dtype)
        db_ref[...] = jnp.sum(y, axis=0, keepdims=True)

    col = pl.BlockSpec((S, tc), lambda j: (0, j))
    return pl.pallas_call(
        body, name=name, grid=(C // tc,),
        in_specs=[col, pl.BlockSpec((S, tc), lambda j: (0, off + j)), pl.BlockSpec((4, tc), lambda j: (0, j))],
        out_specs=[col, pl.BlockSpec((4, tc), lambda j: (0, j)), pl.BlockSpec((1, tc), lambda j: (0, j))],
        out_shape=[_hbm_out((S, C), BF16), _hbm_out((4, C), F32),
                   _hbm_out((1, C), F32)],
        compiler_params=_params(("parallel",)),
    )(drc, proj, w)


def _gates_fwd(name, rcb, wg, bg):
    S, C = rcb.shape
    nblk, bw, _ = wg.shape

    def body(x_ref, w_ref, b_ref, gi_ref, gr_ref):
        g = jnp.dot(x_ref[...], w_ref[...], preferred_element_type=F32) + b_ref[...]
        gi_ref[...] = g[:, :bw]
        gr_ref[...] = g[:, bw:]

    col = pl.BlockSpec((S, bw), lambda n: (0, n))
    return pl.pallas_call(
        body, name=name, grid=(nblk,),
        in_specs=[col, pl.BlockSpec((None, bw, 2 * bw), lambda n: (n, 0, 0)),
                  pl.BlockSpec((None, 1, 2 * bw), lambda n: (n, 0, 0))],
        out_specs=[col, col],
        out_shape=[_hbm_out((S, C), F32), _hbm_out((S, C), F32)],
        compiler_params=_params(("parallel",)),
    )(rcb, wg, bg)


def _gates_bwd(name, dgi, dgr, rcb, wg, drc1):
    S, C = rcb.shape
    nblk, bw, _ = wg.shape

    def body(dgi_ref, dgr_ref, x_ref, w_ref, d1_ref, drc_ref, dw_ref):
        w = w_ref[...]
        x = x_ref[...]
        di, dr = dgi_ref[...], dgr_ref[...]
        drc_ref[...] = (d1_ref[...]
                        + lax.dot_general(di, w[:, :bw], _NT, preferred_element_type=F32)
                        + lax.dot_general(dr, w[:, bw:], _NT, preferred_element_type=F32))
        dw_ref[:, :bw] = lax.dot_general(x, di, _TN, preferred_element_type=F32).astype(dw_ref.dtype)
        dw_ref[:, bw:] = lax.dot_general(x, dr, _TN, preferred_element_type=F32).astype(dw_ref.dtype)

    col = pl.BlockSpec((S, bw), lambda n: (0, n))
    wspec = pl.BlockSpec((None, bw, 2 * bw), lambda n: (n, 0, 0))
    return pl.pallas_call(
        body, name=name, grid=(nblk,),
        in_specs=[col, col, col, wspec, col], out_specs=[col, wspec],
        out_shape=[_hbm_out((S, C), F32), _hbm_out((nblk, bw, 2 * bw), BF16)],
        compiler_params=_params(("parallel",)),
    )(dgi, dgr, rcb, wg, drc1)


def _att_tile(S):
    return next(t for t in (512, 256, 128) if S % t == 0)


def _head_lanes(shape):
    return lax.broadcasted_iota(jnp.int32, shape, len(shape) - 1) < HEAD_DIM


def _key_bias(c_blk):
    first = _head_lanes(c_blk.shape)
    rolled = pltpu.roll(c_blk, HEAD_DIM, 1)
    return jnp.where(first, c_blk, rolled), jnp.where(first, rolled, c_blk)


def _over_keys(x, op):
    n = x.shape[0]
    while n > SUBLANES:
        n //= 2
        x = op(x[:n], x[n:2 * n])
    return (jnp.max if op is jnp.maximum else jnp.sum)(x, axis=0, keepdims=True)


def _causal_t(T, cc):
    r = lax.broadcasted_iota(jnp.int32, (T, LANES), 0)
    c = lax.broadcasted_iota(jnp.int32, (T, LANES), 1) + cc * LANES
    return r <= c


def _attn_fwd(name, q, kv, cfull):
    S, D = q.shape
    HP = D // LANES
    T = _att_tile(S)
    nq = S // T
    NC = T // LANES

    def body(q_ref, k_ref, v_ref, c_ref, o_ref, of_ref, lse_ref, bias, vT, acc, m_scr, l_scr):
        def prologue(i, _):
            rows = pl.ds(pl.multiple_of(i * T, T), T)
            bias[0, rows, :], bias[1, rows, :] = _key_bias(c_ref[rows, :])
            vT[i] = v_ref[rows, :].astype(F32).T.astype(BF16)
            return 0

        lax.fori_loop(0, nq, prologue, 0)

        def q_step(qi, _):
            q0 = pl.multiple_of(qi * T, T)
            qb = q_ref[pl.ds(q0, T), :]
            m_scr[...] = jnp.full(m_scr.shape, -jnp.inf, F32)
            l_scr[...] = jnp.zeros(l_scr.shape, F32)
            acc[...] = jnp.zeros(acc.shape, F32)

            def tile(kj, masked):
                ks = pl.ds(pl.multiple_of(kj * T, T), T)
                kf = k_ref[ks, :].astype(F32)
                first = _head_lanes(kf.shape)
                kms = [jnp.where(first if hh == 0 else jnp.logical_not(first), kf, 0.0).astype(BF16) for hh in range(2)]
                sTs = [lax.dot_general(km, qb, _NT, preferred_element_type=F32) for km in kms]
                for hh in range(2):
                    b = bias[hh, ks, :]
                    ps = []
                    for cc in range(NC):
                        cols = slice(cc * LANES, (cc + 1) * LANES)
                        s = sTs[hh][:, cols] + b
                        if masked:
                            s = jnp.where(_causal_t(T, cc), s, -jnp.inf)
                        m_old = m_scr[hh, cc]
                        m_new = jnp.maximum(m_old, _over_keys(s, jnp.maximum))
                        alpha = jnp.exp(m_old - m_new)
                        p = jnp.exp(s - m_new)
                        l_scr[hh, cc] = alpha * l_scr[hh, cc] + _over_keys(p, jnp.add)
                        m_scr[hh, cc] = m_new
                        ps.append(p.astype(BF16))
                        acc[hh, :, cols] = acc[hh, :, cols] * alpha
                    acc[hh] += jnp.dot(vT[kj, hh * HEAD_DIM:(hh + 1) * HEAD_DIM, :], jnp.concatenate(ps, axis=1),
                                       preferred_element_type=F32)

            def inner(kj, _):
                tile(kj, False)
                return 0

            lax.fori_loop(0, qi, inner, 0)
            tile(qi, True)
            outs = []
            for hh in range(2):
                inv = jnp.concatenate([1.0 / l_scr[hh, cc] for cc in range(NC)], axis=1)
                outs.append(acc[hh] * inv)
                for cc in range(NC):
                    lse_ref[hh:hh + 1, pl.ds(q0 + cc * LANES, LANES)] = m_scr[hh, cc] + jnp.log(l_scr[hh, cc])
            out = jnp.concatenate(outs, axis=0).T
            o_ref[pl.ds(q0, T), :] = out.astype(o_ref.dtype)
            of_ref[pl.ds(q0, T), :] = out
            return 0

        lax.fori_loop(0, nq, q_step, 0)

    blk = lambda off: pl.BlockSpec((S, LANES), lambda p: (0, off + p))
    return pl.pallas_call(
        body, name=name, grid=(HP,),
        in_specs=[blk(0), blk(0), blk(HP), blk(0)],
        out_specs=[blk(0), blk(0), pl.BlockSpec((None, 2, S), lambda p: (p, 0, 0))],
        out_shape=[_hbm_out((S, D), BF16), _hbm_out((S, D), F32),
                   _hbm_out((HP, 2, S), F32)],
        scratch_shapes=[pltpu.VMEM((2, S, LANES), F32), pltpu.VMEM((nq, LANES, T), BF16),
                        pltpu.VMEM((2, HEAD_DIM, T), F32), pltpu.VMEM((2, NC, 1, LANES), F32),
                        pltpu.VMEM((2, NC, 1, LANES), F32)],
        compiler_params=_params(("parallel",)),
    )(q, kv, kv, cfull)


def _attn_bwd(name, q, kv, cfull, of, do, lse3):
    S, D = q.shape
    HP = D // LANES
    T = _att_tile(S)
    nq = S // T
    NC = T // LANES
    scale = HEAD_DIM ** -0.5

    def body(q_ref, k_ref, v_ref, c_ref, of_ref, do_ref, lse_ref,
             dq_ref, dk_ref, dv_ref, dck_ref, drq_ref, bias, kT, dqT, delta, dr_scr):
        def prologue(i, _):
            rows = pl.ds(pl.multiple_of(i * T, T), T)
            bias[0, rows, :], bias[1, rows, :] = _key_bias(c_ref[rows, :])
            kT[i] = k_ref[rows, :].astype(F32).T.astype(BF16)
            prodT = (do_ref[rows, :].astype(F32) * of_ref[rows, :]).T
            for hh in range(2):
                delta[hh:hh + 1, rows] = jnp.sum(prodT[hh * HEAD_DIM:(hh + 1) * HEAD_DIM], axis=0, keepdims=True)
            dqT[i] = jnp.zeros((LANES, T), F32)
            return 0

        lax.fori_loop(0, nq, prologue, 0)
        dr_scr[...] = jnp.zeros(dr_scr.shape, F32)

        def kv_step(kj, _):
            ks = pl.ds(pl.multiple_of(kj * T, T), T)
            kf = k_ref[ks, :].astype(F32)
            vf = v_ref[ks, :].astype(F32)
            first = _head_lanes(kf.shape)
            masks = [first, jnp.logical_not(first)]
            kms = [jnp.where(m, kf, 0.0).astype(BF16) for m in masks]
            vms = [jnp.where(m, vf, 0.0).astype(BF16) for m in masks]

            def tile(qi, carry, masked):
                q0 = pl.multiple_of(qi * T, T)
                qb = q_ref[pl.ds(q0, T), :]
                dob = do_ref[pl.ds(q0, T), :]
                sTs = [lax.dot_general(km, qb, _NT, preferred_element_type=F32) for km in kms]
                dpTs = [lax.dot_general(vm, dob, _NT, preferred_element_type=F32) for vm in vms]
                out = []
                for hh in range(2):
                    dk_a, dv_a, dc_a = carry[3 * hh:3 * hh + 3]
                    b = bias[hh, ks, :]
                    head = slice(hh * HEAD_DIM, (hh + 1) * HEAD_DIM)
                    ps, dss = [], []
                    for cc in range(NC):
                        cols = slice(cc * LANES, (cc + 1) * LANES)
                        at = pl.ds(q0 + cc * LANES, LANES)
                        p = jnp.exp(sTs[hh][:, cols] + b - lse_ref[hh:hh + 1, at])
                        if masked:
                            p = jnp.where(_causal_t(T, cc), p, 0.0)
                        ds = p * (dpTs[hh][:, cols] - delta[hh:hh + 1, at])
                        ps.append(p.astype(BF16))
                        dss.append(ds.astype(BF16))
                        dc_a = dc_a + ds
                        dr_scr[hh:hh + 1, at] += _over_keys(ds, jnp.add)
                    pT = jnp.concatenate(ps, axis=1)
                    dsT = jnp.concatenate(dss, axis=1)
                    dv_a = dv_a + jnp.dot(pT, dob, preferred_element_type=F32)
                    dk_a = dk_a + jnp.dot(dsT, qb, preferred_element_type=F32)
                    dqT[qi, head, :] += jnp.dot(kT[kj, head, :], dsT, preferred_element_type=F32)
                    out += [dk_a, dv_a, dc_a]
                return tuple(out)

            zero = jnp.zeros((T, LANES), F32)
            carry = tile(kj, (zero,) * 6, True)
            dk0, dv0, dc0, dk1, dv1, dc1 = lax.fori_loop(kj + 1, nq, lambda qi, c: tile(qi, c, False), carry)
            dk_ref[ks, :] = jnp.where(first, dk0, dk1)
            dv_ref[ks, :] = jnp.where(first, dv0, dv1)
            dck_ref[ks, :] = jnp.where(first, jnp.broadcast_to(-jnp.sum(dc0, axis=1, keepdims=True), (T, LANES)),
                                       jnp.broadcast_to(-jnp.sum(dc1, axis=1, keepdims=True), (T, LANES)))
            return 0

        lax.fori_loop(0, nq, kv_step, 0)

        def epilogue(i, _):
            rows = pl.ds(pl.multiple_of(i * T, T), T)
            dq_ref[rows, :] = (dqT[i].T * scale).astype(dq_ref.dtype)
            return 0

        lax.fori_loop(0, nq, epilogue, 0)
        drq_ref[...] = dr_scr[...]

    blk = lambda off: pl.BlockSpec((S, LANES), lambda p: (0, off + p))
    row_spec = pl.BlockSpec((None, 2, S), lambda p: (p, 0, 0))
    return pl.pallas_call(
        body, name=name, grid=(HP,),
        in_specs=[blk(0), blk(0), blk(HP), blk(0), blk(0), blk(0), row_spec],
        out_specs=[blk(0), blk(0), blk(0), blk(0), row_spec],
        out_shape=[_hbm_out((S, D), BF16), _hbm_out((S, D), F32),
                   _hbm_out((S, D), F32), _hbm_out((S, D), F32),
                   _hbm_out((HP, 2, S), F32)],
        scratch_shapes=[pltpu.VMEM((2, S, LANES), F32), pltpu.VMEM((nq, LANES, T), BF16),
                        pltpu.VMEM((nq, LANES, T), F32), pltpu.VMEM((2, S), F32), pltpu.VMEM((2, S), F32)],
        compiler_params=_params(("parallel",)),
    )(q, kv, kv, cfull, of, do, lse3)


def _logsig_fwd(name, f):
    S, C = f.shape

    def body(f_ref, o_ref):
        o_ref[...] = -_softplus(-f_ref[...])

    spec = pl.BlockSpec((S, C), lambda i: (0, 0))
    return pl.pallas_call(body, name=name, grid=(1,), in_specs=[spec], out_specs=spec,
                          out_shape=_hbm_out((S, C), F32),
                          compiler_params=_params(("arbitrary",)))(f)


def _logsig_bwd(name, dls, f):
    S, C = f.shape

    def body(d_ref, f_ref, o_ref, s_ref):
        df = d_ref[...] * _sigmoid(-f_ref[...])
        o_ref[...] = df.astype(o_ref.dtype)
        s_ref[...] = jnp.sum(df, axis=0, keepdims=True)

    spec = pl.BlockSpec((S, C), lambda i: (0, 0))
    return pl.pallas_call(body, name=name, grid=(1,), in_specs=[spec, spec],
                          out_specs=[spec, pl.BlockSpec((1, C), lambda i: (0, 0))],
                          out_shape=[_hbm_out((S, C), BF16), _hbm_out((1, C), F32)],
                          compiler_params=_params(("arbitrary",)))(dls, f)


def _add_cast(name, parts, out_dtype, tr=256):
    S, C = parts[0].shape
    tr = _tile(S, tr)
    n = len(parts)

    def body(*refs):
        acc = refs[0][...].astype(F32)
        for r in refs[1:n]:
            acc = acc + r[...].astype(F32)
        refs[n][...] = acc.astype(out_dtype)

    spec = pl.BlockSpec((tr, C), lambda i: (i, 0))
    return pl.pallas_call(body, name=name, grid=(S // tr,), in_specs=[spec] * n, out_specs=spec,
                          out_shape=_hbm_out((S, C), out_dtype),
                          compiler_params=_params(("parallel",)))(*parts)


def _local_step(x, target, gains, layer_weights, layer_prefetch, layer_grads):
    S, D = x.shape
    HP = D // LANES
    scale = HEAD_DIM ** -0.5
    tm = _tile(S, 512)
    tx = _tile(S, 256)
    td = _tile(D, 512)
    saved = []
    h = x
    l = 0
    kv = cfull = f_pre = hn_kv = h_kv = None
    while True:
        W = layer_weights(l, "mix", h)
        if W is None:
            break
        recurrent = "w_rec_in" in W
        if l == 0:
            xn = _rmsnorm_fwd("mix_norm_0", h, gains["mix"][0])
        if recurrent:
            CH = W["w_rec_in"].shape[-1]
            C = 2 * CH
            proj = _mm(f"rec_in_{l}", "nn", xn, W["w_rec_in"], grid=(S // tm, N_CHIPS),
                       a_spec=pl.BlockSpec((tm, D), lambda i, j: (i, 0)),
                       b_spec=pl.BlockSpec((None, D, CH), lambda i, j: (j, 0, 0)),
                       out_shape=(S, 2 * C), out_dtype=F32,
                       out_spec=pl.BlockSpec((tm, CH), lambda i, j: (i, j)))
            rc, rcb = _conv_fwd(f"conv_{l}", proj, W["conv_w"], W["conv_b"])
            gip, grp = _gates_fwd(f"gates_{l}", rcb, W["w_gates"], W["b_gates"])
            hrec, m = _lru_fwd(f"lru_{l}", proj, rc, gip, grp, W["lru_param"])
            layer_prefetch(l, "ffn", m)
            h_mid, hn = _mm_nn(f"rec_out_{l}", m, W["w_rec_out"], out_dtype=F32, res=h, tn=D, norm_gain=gains["ffn"][l])
            mix_saved = (xn, proj, rc, rcb, gip, grp, hrec, m)
        else:
            if "w_kv" in W:
                h_kv = h
                hn_kv = _rmsnorm_fwd("kv_norm", h, W["norm_kv"])
                kv = _mm_nn("kv_proj", hn_kv, W["w_kv"], out_dtype=BF16)
                f_pre = _mm_nn("f_proj", hn_kv, W["w_f"], out_dtype=F32, bias=W["b_f"])
                c = _cumsum_rows("c_cumsum", _logsig_fwd("logsig", f_pre), False)
                cfull = jnp.repeat(-c[:, :2 * HP], HEAD_DIM, axis=1)
            q = _mm_nn(f"q_proj_{l}", xn, W["w_q"], out_dtype=BF16, scale=scale)
            o, of, lse = _attn_fwd(f"attn_fwd_{l}", q, kv, cfull)
            layer_prefetch(l, "ffn", o)
            h_mid, hn = _mm_nn(f"o_proj_{l}", o, W["w_o"], out_dtype=F32, res=h, tn=D, norm_gain=gains["ffn"][l])
            mix_saved = (xn, q, o, of, lse)
        W = {**W, **layer_weights(l, "ffn", h_mid)}
        z3, act = _swiglu_fwd(f"ffn_in_{l}", hn, W["w_ffn_in"])
        layer_prefetch(l + 1, "mix", act)
        saved.append((W, h, h_mid, mix_saved, (hn, z3, act)))
        l += 1
        if l < len(gains["mix"]):
            h, xn = _mm_nn(f"ffn_out_{l - 1}", act, W["w_ffn_out"], out_dtype=F32, res=h_mid, tn=D,
                           norm_gain=gains["mix"][l])
        else:
            h = _mm_nn(f"ffn_out_{l - 1}", act, W["w_ffn_out"], out_dtype=F32, res=h_mid, tn=D)

    dh, dhb, dg_final, loss_row = _loss_head("loss_head", h, target, gains["final"])

    dk_parts, dv_parts, dc_parts = [], [], []
    token = None
    for l in reversed(range(len(saved))):
        W, h_in, h_mid, mix_saved, (hn, z3, act) = saved[l]
        recurrent = "w_rec_in" in W
        FH = W["w_ffn_in"].shape[-1]
        G = {}
        norm_ffn = gains["ffn"][l]
        if token is not None:
            norm_ffn = norm_ffn + jnp.minimum(token[:1, :1], 0.0)
        G["w_ffn_out"] = _mm_tn(f"d_ffn_out_{l}", act, dhb, out_dtype=BF16, tn=D)
        dz3 = _swiglu_bwd(f"d_act_{l}", dhb, W["w_ffn_out"], z3)
        G["w_ffn_in"] = _mm(
            f"d_ffn_in_{l}", "tn", hn, dz3, grid=(D // td, N_CHIPS),
            a_spec=pl.BlockSpec((S, td), lambda i, j: (0, i)),
            b_spec=pl.BlockSpec((None, S, FH), lambda i, j: (j // 2, 0, j % 2)),
            out_shape=(N_CHIPS, D, FH), out_dtype=BF16,
            out_spec=pl.BlockSpec((None, td, FH), lambda i, j: (j, i, 0)))
        token = layer_grads(l, "ffn", G)
        G = {}
        norm_ffn = norm_ffn + jnp.minimum(token[:1, :1], 0.0)
        dh, dhb, dgp = _mm(f"d_ffn_hn_{l}", "nt", dz3, W["w_ffn_in"], grid=(S // tx, 1),
                           a_spec=[pl.BlockSpec((None, tx, FH), functools.partial(lambda i, j, k: (k // 2, i, k % 2), k=k))
                                   for k in range(N_CHIPS)],
                           b_spec=[pl.BlockSpec((None, D, FH), functools.partial(lambda i, j, k: (k, 0, 0), k=k))
                                   for k in range(N_CHIPS)],
                           out_shape=(S, D), out_dtype=F32, out_spec=pl.BlockSpec((tx, D), lambda i, j: (i, 0)),
                           norm_bwd=(h_mid, norm_ffn, dh))
        G["norm_ffn"] = jnp.sum(dgp, axis=0)
        if recurrent:
            CH = W["w_rec_in"].shape[-1]
            C = 2 * CH
            xn, proj, rc, rcb, gip, grp, hrec, m = mix_saved
            G["w_rec_out"] = _mm_tn(f"d_rec_out_{l}", m, dhb, out_dtype=BF16, tn=D)
            dm = _mm_nt(f"d_m_{l}", dhb, W["w_rec_out"], out_dtype=F32, tn=C)
            dgb, dgi, dgr, drc1, G["b_gi"], G["b_gr"], G["lru_param"] = _lru_bwd(
                f"d_lru_{l}", dm, proj, hrec, rc, gip, grp, W["lru_param"])
            drc, G["w_gates"] = _gates_bwd(f"d_gates_{l}", dgi, dgr, rcb, W["w_gates"], drc1)
            drec, G["conv_w"], G["conv_b"] = _conv_bwd(f"d_conv_{l}", drc, proj, W["conv_w"])
            dproj = jnp.concatenate([dgb, drec], axis=1)
            G["w_rec_in"] = _mm(
                f"d_rec_in_{l}", "tn", xn, dproj, grid=(1, N_CHIPS),
                a_spec=pl.BlockSpec((S, D), lambda i, j: (0, 0)),
                b_spec=pl.BlockSpec((S, CH), lambda i, j: (0, j)),
                out_shape=(N_CHIPS, D, CH), out_dtype=BF16,
                out_spec=pl.BlockSpec((None, D, CH), lambda i, j: (j, 0, 0)))
            dh, dhb, dgp = _mm(f"d_rec_xn_{l}", "nt", dproj, W["w_rec_in"], grid=(S // tx, 1),
                               a_spec=[pl.BlockSpec((tx, CH), functools.partial(lambda i, j, k: (i, k), k=k))
                                       for k in range(N_CHIPS)],
                               b_spec=[pl.BlockSpec((None, D, CH), functools.partial(lambda i, j, k: (k, 0, 0), k=k))
                                       for k in range(N_CHIPS)],
                               out_shape=(S, D), out_dtype=F32, out_spec=pl.BlockSpec((tx, D), lambda i, j: (i, 0)),
                               norm_bwd=(h_in, gains["mix"][l], dh))
        else:
            xn, q, o, of, lse = mix_saved
            G["w_o"] = _mm_tn(f"d_o_proj_{l}", o, dhb, out_dtype=BF16, tn=D)
            do = _mm_nt(f"d_o_{l}", dhb, W["w_o"], out_dtype=BF16, tn=D)
            dq, dk, dv, dck, drq = _attn_bwd(f"attn_bwd_{l}", q, kv, cfull, of, do, lse)
            dk_parts.append(dk)
            dv_parts.append(dv)
            dc_parts.append(dck[:, ::HEAD_DIM] + drq.reshape(2 * HP, S).T)
            G["w_q"] = _mm_tn(f"d_q_proj_{l}", xn, dq, out_dtype=BF16, tn=D)
            dh, dhb, dgp = _mm_nt(f"d_q_xn_{l}", dq, W["w_q"], out_dtype=F32, tn=D, norm_bwd=(h_in, gains["mix"][l], dh))
        G["norm_mix"] = jnp.sum(dgp, axis=0)
        if "w_kv" in W:
            dkb = _add_cast("dk_sum", dk_parts, BF16)
            dvb = _add_cast("dv_sum", dv_parts, BF16)
            dkv = jnp.concatenate([dkb, dvb], axis=1)
            dc = sum(dc_parts[1:], dc_parts[0])
            dc_pad = jnp.pad(dc, ((0, 0), (0, LANES - 2 * HP)))
            dls = _cumsum_rows("dc_cumsum", dc_pad, True)
            dfb, G["b_f"] = _logsig_bwd("d_logsig", dls, f_pre)
            G["w_kv"] = _mm_tn("d_kv_proj", hn_kv, dkv, out_dtype=BF16)
            G["w_f"] = _mm_tn("d_f_proj", hn_kv, dfb, out_dtype=F32)
            dhn_f = _mm_nt("d_f_hn", dfb, W["w_f"], out_dtype=F32, tn=D)
            dh, dhb, dgp = _mm_nt("d_kv_hn", dkv, W["w_kv"], out_dtype=F32, tn=D, res=dhn_f,
                                  norm_bwd=(h_kv, W["norm_kv"], dh))
            G["norm_kv"] = jnp.sum(dgp, axis=0)
        token = layer_grads(l, "mix", G)
    return loss_row, dh, dg_final


_ANY = pl.BlockSpec(memory_space=pl.ANY)


def _position():
    return lax.axis_index("x"), lax.axis_index("y"), lax.axis_index("c")


def _chip_peers(x, y):
    return [(1 - x, y), (x, 1 - y), (1 - x, 1 - y)]


def _half_rows(c, n):
    h = n // 2
    assert h % 16 == 0
    return pl.ds(pl.multiple_of(c * h, 16), h)


def _place_own(name, shard, layer, me):
    _, R, C = shard.shape
    tr = _row_tile(R, C, 2 * shard.dtype.itemsize, target=8 << 20)

    def body(me_ref, x_ref, o_ref):
        o_ref[...] = x_ref[...]

    return pl.pallas_call(
        body, name=name,
        grid_spec=pltpu.PrefetchScalarGridSpec(
            num_scalar_prefetch=1, grid=(R // tr,),
            in_specs=[pl.BlockSpec((None, tr, C), lambda i, me_ref: (layer, i, 0))],
            out_specs=pl.BlockSpec((None, tr, C), lambda i, me_ref: (me_ref[0], i, 0))),
        out_shape=_hbm_out((N_CHIPS, R, C), shard.dtype),
        compiler_params=_params(("parallel",)),
    )(me, shard)


def _gather_smalls(name, smalls):
    ns = len(smalls)

    def body(*refs):
        ins, outs = refs[:ns], refs[ns:2 * ns]
        send_sems, recv_sems, local_sems = refs[2 * ns:]
        x, y, c = _position()
        me = 2 * x + y
        peers = _chip_peers(x, y)

        def remote(t, k, chip):
            px, py = peers[k]
            return pltpu.make_async_remote_copy(
                src_ref=ins[t], dst_ref=outs[t].at[chip], send_sem=send_sems.at[3 * t + k],
                recv_sem=recv_sems.at[3 * t + k], device_id=(px, py, c), device_id_type=MESH)

        local = [pltpu.make_async_copy(ins[t], outs[t].at[me], local_sems.at[t]) for t in range(ns)]
        for t in range(ns):
            local[t].start()
            for k in range(3):
                remote(t, k, me).start()
        for t in range(ns):
            for k in range(3):
                px, py = peers[k]
                remote(t, k, 2 * px + py).wait_recv()
        for t in range(ns):
            for k in range(3):
                remote(t, k, me).wait_send()
            local[t].wait()

    return pl.pallas_call(
        body, name=name, in_specs=[_ANY] * ns, out_specs=[_ANY] * ns,
        out_shape=[_hbm_out((N_CHIPS,) + s.shape, s.dtype) for s in smalls],
        scratch_shapes=[pltpu.SemaphoreType.DMA((3 * ns,)), pltpu.SemaphoreType.DMA((3 * ns,)),
                        pltpu.SemaphoreType.DMA((ns,))],
    )(*smalls)


_SEM = pl.BlockSpec(memory_space=pltpu.SEMAPHORE)
_SPLIT = pltpu.CompilerParams(has_side_effects=pltpu.SideEffectType.DATAFLOW_SIDE_EFFECTING)


def _weight_copy(shards, buf, items, sems, i, k, chip_of_dst, peers, c):
    w, l = items[i]
    px, py = peers[k]
    half = _half_rows(c, shards[w].shape[1])
    return pltpu.make_async_remote_copy(
        src_ref=shards[w].at[l, half], dst_ref=buf.at[chip_of_dst, half],
        send_sem=sems[0].at[3 * i + k], recv_sem=sems[1].at[3 * i + k],
        device_id=(px, py, c), device_id_type=MESH)


def _gather_start(name, shards, bufs, items, after):
    nw, n = len(shards), len(bufs)

    def body(*refs):
        ins, outs, sems = refs[:nw], refs[nw + n + 1:nw + 2 * n + 1], refs[nw + 2 * n + 1:]
        x, y, c = _position()
        peers = _chip_peers(x, y)
        for i in range(n):
            for k in range(3):
                _weight_copy(ins, outs[i], items, sems, i, k, 2 * x + y, peers, c).start()

    res = pl.pallas_call(
        body, name=name, in_specs=[_ANY] * (nw + n + 1), out_specs=[_ANY] * n + [_SEM, _SEM],
        out_shape=[_hbm_out(b.shape, b.dtype) for b in bufs]
        + [pltpu.SemaphoreType.DMA((3 * n,)), pltpu.SemaphoreType.DMA((3 * n,))],
        input_output_aliases={nw + i: i for i in range(n)}, compiler_params=_SPLIT,
    )(*shards, *bufs, after)
    return res[:n], res[n:]


def _gather_wait(name, shards, bufs, items, ids, sems, after):
    nw, m = len(shards), len(ids)

    def body(*refs):
        ins, bs = refs[:nw], refs[nw:nw + m]
        sem_refs = refs[nw + m:nw + m + 2]
        x, y, c = _position()
        peers = _chip_peers(x, y)
        for j, i in enumerate(ids):
            for k in range(3):
                px, py = peers[k]
                _weight_copy(ins, bs[j], items, sem_refs, i, k, 2 * px + py, peers, c).wait_recv()
        for j, i in enumerate(ids):
            for k in range(3):
                _weight_copy(ins, bs[j], items, sem_refs, i, k, 2 * x + y, peers, c).wait_send()

    res = pl.pallas_call(
        body, name=name, in_specs=[_ANY] * (nw + m) + [_SEM, _SEM, _ANY], out_specs=[_ANY] * m,
        out_shape=[_hbm_out(bufs[i].shape, bufs[i].dtype) for i in ids],
        input_output_aliases={nw + j: j for j in range(m)}, compiler_params=_SPLIT,
    )(*shards, *[bufs[i] for i in ids], *sems, after)
    return list(res)


def _forward_copy(src, dst, sems, i, k, core):
    x, y, c = _position()
    px, py = _chip_peers(x, y)[k]
    half = _half_rows(core, src.shape[1])
    return pltpu.make_async_remote_copy(
        src_ref=src.at[2 * px + py, half], dst_ref=dst.at[2 * px + py, half],
        send_sem=sems[0].at[3 * i + k], recv_sem=sems[1].at[3 * i + k],
        device_id=(x, y, 1 - c), device_id_type=MESH)


def _forward_start(name, bufs):
    n = len(bufs)

    def body(*refs):
        ins, outs, sems = refs[:n], refs[n:2 * n], refs[2 * n:]
        c = lax.axis_index("c")
        for i in range(n):
            for k in range(3):
                _forward_copy(ins[i], outs[i], sems, i, k, c).start()

    res = pl.pallas_call(
        body, name=name, in_specs=[_ANY] * n, out_specs=[_ANY] * n + [_SEM, _SEM],
        out_shape=[_hbm_out(g.shape, g.dtype) for g in bufs]
        + [pltpu.SemaphoreType.DMA((3 * n,)), pltpu.SemaphoreType.DMA((3 * n,))],
        input_output_aliases={i: i for i in range(n)}, compiler_params=_SPLIT,
    )(*bufs)
    return list(res[:n]), res[n:]


def _forward_wait(name, bufs, sems, after):
    n = len(bufs)

    def body(*refs):
        bs, sem_refs = refs[:n], refs[n:n + 2]
        c = lax.axis_index("c")
        for i in range(n):
            for k in range(3):
                _forward_copy(bs[i], bs[i], sem_refs, i, k, 1 - c).wait_recv()
        for i in range(n):
            for k in range(3):
                _forward_copy(bs[i], bs[i], sem_refs, i, k, c).wait_send()

    return list(pl.pallas_call(
        body, name=name, in_specs=[_ANY] * n + [_SEM, _SEM, _ANY], out_specs=[_ANY] * n,
        out_shape=[_hbm_out(g.shape, g.dtype) for g in bufs],
        input_output_aliases={i: i for i in range(n)}, compiler_params=_SPLIT,
    )(*bufs, *sems, after))


def _reduce_copy(grads, others, sems, i):
    x, y, c = _position()
    return pltpu.make_async_remote_copy(
        src_ref=grads[i].at[:, _half_rows(1 - c, grads[i].shape[1])], dst_ref=others[i],
        send_sem=sems[0].at[i], recv_sem=sems[1].at[i], device_id=(x, y, 1 - c), device_id_type=MESH)


def _reduce_start(name, grads, after):
    n = len(grads)

    def body(*refs):
        ins, outs, sems, token = refs[:n], refs[n + 1:2 * n + 1], refs[2 * n + 1:2 * n + 3], refs[2 * n + 3]
        for i in range(n):
            _reduce_copy(ins, outs, sems, i).start()
        token[...] = jnp.zeros_like(token)

    res = pl.pallas_call(
        body, name=name, in_specs=[_ANY] * (n + 1),
        out_specs=[_ANY] * n + [_SEM, _SEM, pl.BlockSpec(memory_space=pltpu.VMEM)],
        out_shape=[_hbm_out((N_CHIPS, g.shape[1] // 2, g.shape[2]), g.dtype) for g in grads]
        + [pltpu.SemaphoreType.DMA((n,)), pltpu.SemaphoreType.DMA((n,)), jax.ShapeDtypeStruct((SUBLANES, LANES), F32)],
        compiler_params=_SPLIT,
    )(*grads, after)
    return list(res[:n]), res[n:n + 2], res[n + 2]


def _reduce_wait(name, grads, others, sems, after):
    n = len(grads)

    def body(*refs):
        ins, os_, sem_refs = refs[:n], refs[n:2 * n], refs[2 * n:2 * n + 2]
        for i in range(n):
            _reduce_copy(ins, os_, sem_refs, i).wait_recv()
        for i in range(n):
            _reduce_copy(ins, os_, sem_refs, i).wait_send()

    return list(pl.pallas_call(
        body, name=name, in_specs=[_ANY] * (2 * n) + [_SEM, _SEM, _ANY], out_specs=[_ANY] * n,
        out_shape=[_hbm_out(o.shape, o.dtype) for o in others],
        input_output_aliases={n + i: i for i in range(n)}, compiler_params=_SPLIT,
    )(*grads, *others, *sems, after))


def _sum_cores(name, g, other, core):
    _, R, C = g.shape
    H = R // 2
    tr = _row_tile(H, C, 3 * 2, target=12 << 20)
    nb = H // tr

    def body(c_ref, g_ref, o_ref, out_ref):
        out_ref[...] = (g_ref[...].astype(F32) + o_ref[...].astype(F32)).astype(out_ref.dtype)

    return pl.pallas_call(
        body, name=name,
        grid_spec=pltpu.PrefetchScalarGridSpec(
            num_scalar_prefetch=1, grid=(N_CHIPS, nb),
            in_specs=[pl.BlockSpec((None, tr, C), lambda j, i, c_ref: (j, c_ref[0] * nb + i, 0)),
                      pl.BlockSpec((None, tr, C), lambda j, i, c_ref: (j, i, 0))],
            out_specs=pl.BlockSpec((None, tr, C), lambda j, i, c_ref: (j, i, 0))),
        out_shape=_hbm_out((N_CHIPS, H, C), BF16),
        compiler_params=_params(("parallel", "parallel")),
    )(core, g, other)


def _sum_chips(name, received, own, full, layer, me_core):
    _, H, C = received.shape
    tr = _row_tile(H, C, 3 * 2 + 2 + 4, target=12 << 20)
    nb = H // tr

    def body(s_ref, r_ref, own_ref, full_ref, out_ref):
        acc = r_ref[0].astype(F32)
        for k in (1, 2):
            acc = acc + r_ref[k].astype(F32)
        out_ref[...] = acc + own_ref[...].astype(F32)

    return pl.pallas_call(
        body, name=name,
        grid_spec=pltpu.PrefetchScalarGridSpec(
            num_scalar_prefetch=1, grid=(nb,),
            in_specs=[pl.BlockSpec((3, tr, C), lambda i, s_ref: (0, i, 0)),
                      pl.BlockSpec((None, tr, C), lambda i, s_ref: (s_ref[0], i, 0)),
                      _ANY],
            out_specs=pl.BlockSpec((None, tr, C), lambda i, s_ref: (layer, s_ref[1] * nb + i, 0))),
        out_shape=_hbm_out(full.shape, full.dtype),
        input_output_aliases={3: 0},
        compiler_params=_params(("parallel",)),
    )(me_core, received, own, full)


def _part_copy(parts, recv, sems, i, k, peers, c):
    px, py = peers[k]
    return pltpu.make_async_remote_copy(
        src_ref=parts[i].at[2 * px + py], dst_ref=recv[i].at[k],
        send_sem=sems[0].at[3 * i + k], recv_sem=sems[1].at[3 * i + k],
        device_id=(px, py, c), device_id_type=MESH)


def _scatter_start(name, parts):
    n = len(parts)

    def body(*refs):
        ins, outs, sems, token = refs[:n], refs[n:2 * n], refs[2 * n:2 * n + 2], refs[2 * n + 2]
        x, y, c = _position()
        peers = _chip_peers(x, y)
        for i in range(n):
            for k in range(3):
                _part_copy(ins, outs, sems, i, k, peers, c).start()
        token[...] = jnp.zeros_like(token)

    res = pl.pallas_call(
        body, name=name, in_specs=[_ANY] * n,
        out_specs=[_ANY] * n + [_SEM, _SEM, pl.BlockSpec(memory_space=pltpu.VMEM)],
        out_shape=[_hbm_out((3,) + p.shape[1:], p.dtype) for p in parts]
        + [pltpu.SemaphoreType.DMA((3 * n,)), pltpu.SemaphoreType.DMA((3 * n,)),
           jax.ShapeDtypeStruct((SUBLANES, LANES), F32)],
        compiler_params=_SPLIT,
    )(*parts)
    return list(res[:n]), res[n:n + 2], res[n + 2]


def _scatter_wait(name, parts, recv, sems):
    n = len(parts)

    def body(*refs):
        ins, rs, sem_refs = refs[:n], refs[n:2 * n], refs[2 * n:2 * n + 2]
        x, y, c = _position()
        peers = _chip_peers(x, y)
        for i in range(n):
            for k in range(3):
                _part_copy(ins, rs, sem_refs, i, k, peers, c).wait_recv()
        for i in range(n):
            for k in range(3):
                _part_copy(ins, rs, sem_refs, i, k, peers, c).wait_send()

    return list(pl.pallas_call(
        body, name=name, in_specs=[_ANY] * (2 * n) + [_SEM, _SEM], out_specs=[_ANY] * n,
        out_shape=[_hbm_out(r.shape, r.dtype) for r in recv],
        input_output_aliases={n + i: i for i in range(n)}, compiler_params=_SPLIT,
    )(*parts, *recv, *sems))


def _share_d2d(name, full):
    n = len(full)

    def body(*refs):
        ins, outs = refs[:n], refs[n:2 * n]
        send_sems, recv_sems = refs[2 * n:]
        x, y, c = _position()

        def remote(w, core):
            half = _half_rows(core, ins[w].shape[1])
            return pltpu.make_async_remote_copy(
                src_ref=ins[w].at[:, half], dst_ref=outs[w].at[:, half],
                send_sem=send_sems.at[w], recv_sem=recv_sems.at[w],
                device_id=(x, y, 1 - c), device_id_type=MESH)

        for w in range(n):
            remote(w, c).start()
        for w in range(n):
            remote(w, 1 - c).wait_recv()
        for w in range(n):
            remote(w, c).wait_send()

    return pl.pallas_call(
        body, name=name, in_specs=[_ANY] * n, out_specs=[_ANY] * n,
        out_shape=[_hbm_out(f.shape, f.dtype) for f in full],
        input_output_aliases={w: w for w in range(n)},
        scratch_shapes=[pltpu.SemaphoreType.DMA((n,)), pltpu.SemaphoreType.DMA((n,))],
    )(*full)


def _gather_all(name, a):
    def body(a_ref, o_ref, send_sems, recv_sems, local_sem):
        x, y, c = _position()
        me = 4 * x + 2 * y + c

        def peer(k):
            return (x ^ ((k >> 2) & 1), y ^ ((k >> 1) & 1), c ^ (k & 1))

        def remote(k, slot):
            return pltpu.make_async_remote_copy(
                src_ref=a_ref, dst_ref=o_ref.at[slot], send_sem=send_sems.at[k - 1], recv_sem=recv_sems.at[k - 1],
                device_id=peer(k), device_id_type=MESH)

        local = pltpu.make_async_copy(a_ref, o_ref.at[me], local_sem)
        local.start()
        for k in range(1, N_DEV):
            remote(k, me).start()
        for k in range(1, N_DEV):
            px, py, pc = peer(k)
            remote(k, 4 * px + 2 * py + pc).wait_recv()
        for k in range(1, N_DEV):
            remote(k, me).wait_send()
        local.wait()

    return pl.pallas_call(
        body, name=name, in_specs=[_ANY], out_specs=_ANY,
        out_shape=_hbm_out((N_DEV,) + a.shape, a.dtype),
        scratch_shapes=[pltpu.SemaphoreType.DMA((N_DEV - 1,)), pltpu.SemaphoreType.DMA((N_DEV - 1,)),
                        pltpu.SemaphoreType.DMA],
    )(a)


def _rows2d(a, lead=0):
    return a.reshape(a.shape[:lead] + (-1, a.shape[-1]))


def _row_tile(rows, cols, itemsize=4, target=1 << 20):
    want = max(SUBLANES, target // (cols * itemsize))
    t = min(rows, (want // 16) * 16)
    while t > 16 and rows % t:
        t -= 16
    return t if rows % t == 0 else rows


def _sum_slots(name, r, out_dtype=F32):
    ns = r.shape[0]
    r2 = _rows2d(r, 1)
    _, rows, cols = r2.shape
    tr = _row_tile(rows, cols)

    def body(r_ref, o_ref):
        acc = r_ref[0].astype(F32)
        for s in range(1, ns):
            acc = acc + r_ref[s].astype(F32)
        o_ref[...] = acc.astype(o_ref.dtype)

    out = pl.pallas_call(
        body, name=name, grid=(rows // tr,),
        in_specs=[pl.BlockSpec((ns, tr, cols), lambda i: (0, i, 0))],
        out_specs=pl.BlockSpec((tr, cols), lambda i: (i, 0)),
        out_shape=_hbm_out((rows, cols), out_dtype),
        compiler_params=_params(("parallel",)),
    )(r2)
    return out.reshape(r.shape[1:])


def _adamw(name, g_parts, w, m, v):
    shape = w.shape
    ng = len(g_parts)
    args = [_rows2d(a) for a in (*g_parts, w, m, v)]
    rows, cols = args[0].shape
    tr = _row_tile(rows, cols, (ng + 7) * 4, target=16 << 20)
    c1 = 1.0 - ADAM_B1 ** ADAM_STEP
    c2 = 1.0 - ADAM_B2 ** ADAM_STEP

    def body(*refs):
        g = refs[0][...]
        for r in refs[1:ng]:
            g = g + r[...]
        w_ref, m_ref, v_ref = refs[ng:ng + 3]
        g_out, d_out, m_out, v_out = refs[ng + 3:]
        mn = ADAM_B1 * m_ref[...] + (1.0 - ADAM_B1) * g
        vn = ADAM_B2 * v_ref[...] + (1.0 - ADAM_B2) * (g * g)
        m_hat = mn / c1
        v_hat = vn / c2
        g_out[...] = g
        d_out[...] = -ADAM_LR * (m_hat / (jnp.sqrt(v_hat) + ADAM_EPS) + ADAM_WD * w_ref[...])
        m_out[...] = mn
        v_out[...] = vn

    spec = pl.BlockSpec((tr, cols), lambda i: (i, 0))
    outs = pl.pallas_call(
        body, name=name, grid=(rows // tr,), in_specs=[spec] * (ng + 3), out_specs=[spec] * 4,
        out_shape=[_hbm_out((rows, cols), F32)] * 4,
        compiler_params=_params(("parallel",)),
    )(*args)
    return tuple(o.reshape(shape) for o in outs)


_WEIGHTS = ["norm_mix", "norm_ffn", "w_ffn_in", "w_ffn_out", "w_rec_in", "conv_w", "conv_b", "w_lru_gates",
            "b_lru_gates", "lru_param", "w_rec_out", "norm_kv", "w_kvf", "b_forget", "w_q", "w_o", "norm_final"]
_BIG = ["w_ffn_in", "w_ffn_out", "w_rec_in", "w_lru_gates", "w_rec_out", "w_kvf", "w_q", "w_o"]


def _stack3(a):
    return a[None] if a.ndim == 2 else a.reshape(a.shape[0], -1, a.shape[-1])


def _pad_lanes(a, n):
    return jnp.pad(a, ((0, 0),) * (a.ndim - 1) + ((0, n - a.shape[-1]),))


def kernel(x, norm_mix, norm_ffn, w_ffn_in, w_ffn_out, w_rec_in, conv_w, conv_b, w_lru_gates, b_lru_gates, lru_param, w_rec_out, norm_kv, w_kvf, b_forget, w_q, w_o, norm_final, loss_target, m_norm_mix, m_norm_ffn, m_w_ffn_in, m_w_ffn_out, m_w_rec_in, m_conv_w, m_conv_b, m_w_lru_gates, m_b_lru_gates, m_lru_param, m_w_rec_out, m_norm_kv, m_w_kvf, m_b_forget, m_w_q, m_w_o, m_norm_final, v_norm_mix, v_norm_ffn, v_w_ffn_in, v_w_ffn_out, v_w_rec_in, v_conv_w, v_conv_b, v_w_lru_gates, v_b_lru_gates, v_lru_param, v_w_rec_out, v_norm_kv, v_w_kvf, v_b_forget, v_w_q, v_w_o, v_norm_final):
    P = dict(norm_mix=norm_mix, norm_ffn=norm_ffn, w_ffn_in=w_ffn_in, w_ffn_out=w_ffn_out, w_rec_in=w_rec_in,
             conv_w=conv_w, conv_b=conv_b, w_lru_gates=w_lru_gates, b_lru_gates=b_lru_gates, lru_param=lru_param,
             w_rec_out=w_rec_out, norm_kv=norm_kv, w_kvf=w_kvf, b_forget=b_forget, w_q=w_q, w_o=w_o,
             norm_final=norm_final)
    M1 = dict(norm_mix=m_norm_mix, norm_ffn=m_norm_ffn, w_ffn_in=m_w_ffn_in, w_ffn_out=m_w_ffn_out,
              w_rec_in=m_w_rec_in, conv_w=m_conv_w, conv_b=m_conv_b, w_lru_gates=m_w_lru_gates,
              b_lru_gates=m_b_lru_gates, lru_param=m_lru_param, w_rec_out=m_w_rec_out, norm_kv=m_norm_kv,
              w_kvf=m_w_kvf, b_forget=m_b_forget, w_q=m_w_q, w_o=m_w_o, norm_final=m_norm_final)
    M2 = dict(norm_mix=v_norm_mix, norm_ffn=v_norm_ffn, w_ffn_in=v_w_ffn_in, w_ffn_out=v_w_ffn_out,
              w_rec_in=v_w_rec_in, conv_w=v_conv_w, conv_b=v_conv_b, w_lru_gates=v_w_lru_gates,
              b_lru_gates=v_b_lru_gates, lru_param=v_lru_param, w_rec_out=v_w_rec_out, norm_kv=v_norm_kv,
              w_kvf=v_w_kvf, b_forget=v_b_forget, w_q=v_w_q, w_o=v_w_o, norm_final=v_norm_final)

    _, S, D = x.shape
    L = norm_mix.shape[0]
    NA, NBLK, BW, GS = w_lru_gates.shape
    NB = w_q.shape[0]
    C = NBLK * BW
    CS = C // N_CHIPS
    H = b_forget.shape[0]
    assert C == D and H * HEAD_DIM == D and H <= LANES
    chip = 2 * lax.axis_index("x") + lax.axis_index("y")

    small_a = jnp.concatenate([conv_w, conv_b[:, None], lru_param[:, None]], axis=1)
    small_a, b_gates = _gather_smalls("gather_smalls", [small_a, b_lru_gates])
    small_a = small_a.transpose(1, 2, 0, 3).reshape(NA, 6, C)
    b_gates = b_gates.transpose(1, 2, 0, 3).reshape(NA, NBLK, 1, N_CHIPS * GS)
    shards = [_stack3(P[w]).astype(BF16) for w in _BIG]
    core = lax.axis_index("c")
    chip_id = jnp.reshape(chip, (1,)).astype(jnp.int32)
    core_id = jnp.reshape(core, (1,)).astype(jnp.int32)
    me_core = jnp.stack([chip, core]).astype(jnp.int32)

    def stage_items(l, part):
        if part == "ffn":
            return [(_BIG.index("w_ffn_in"), l), (_BIG.index("w_ffn_out"), l)]
        if l < NA:
            names, at = ["w_rec_in", "w_lru_gates", "w_rec_out"], l
        else:
            names, at = (["w_kvf"] if l == NA else []) + ["w_q", "w_o"], l - NA
        return [(_BIG.index(n), 0 if n == "w_kvf" else at) for n in names]

    stages = [(l, part) for l in range(L) for part in ("mix", "ffn")]
    items = [it for st in stages for it in stage_items(*st)]
    ids_of = {st: [items.index(it) for it in stage_items(*st)] for st in stages}
    bufs = [_place_own(f"place_{_BIG[w]}_{li}", shards[w], li, chip_id) for w, li in items]
    bufs, gather_sems = _gather_start("gather_start", shards, bufs, items, small_a)

    forwarding = {}

    def layer_prefetch(l, part, after):
        if l < L and (l, part) not in forwarding:
            ids = ids_of[(l, part)]
            got = _gather_wait(f"gather_wait_{part}_{l}", shards, bufs, items, ids, gather_sems, after)
            forwarding[(l, part)] = _forward_start(f"forward_start_{part}_{l}", got)

    def layer_weights(l, part, after):
        if l >= L:
            return None
        layer_prefetch(l, part, after)
        ids = ids_of[(l, part)]
        got, sems = forwarding[(l, part)]
        got = _forward_wait(f"forward_wait_{part}_{l}", got, sems, after)
        B = {_BIG[items[i][0]]: g for i, g in zip(ids, got)}
        if part == "ffn":
            return dict(w_ffn_in=B["w_ffn_in"], w_ffn_out=B["w_ffn_out"].reshape(-1, D))
        W = {}
        if l < NA:
            W.update(w_rec_in=B["w_rec_in"],
                     w_gates=B["w_lru_gates"].reshape(N_CHIPS, NBLK, BW, GS).transpose(1, 2, 0, 3).reshape(
                         NBLK, BW, N_CHIPS * GS),
                     b_gates=b_gates[l], w_rec_out=B["w_rec_out"].reshape(C, D),
                     conv_w=small_a[l, :4], conv_b=small_a[l, 4:5], lru_param=small_a[l, 5:6])
        else:
            W.update(w_q=B["w_q"].reshape(D, D), w_o=B["w_o"].reshape(D, D))
            if l == NA:
                w_kvf_full = B["w_kvf"].transpose(1, 0, 2).reshape(D, -1)
                W.update(norm_kv=norm_kv[None], w_kv=w_kvf_full[:, :2 * D],
                         w_f=_pad_lanes(w_kvf_full[:, 2 * D:], LANES), b_f=_pad_lanes(b_forget[None], LANES))
        return W

    G_small = {l: {} for l in range(L)}
    pending = {}
    reducing = []

    def finish_reduce(after):
        l, part, its, grads, others, sems = reducing.pop()
        others = _reduce_wait(f"reduce_wait_{part}_{l}", grads, others, sems, after)
        parts = [_sum_cores(f"sum_cores_{l}_{_BIG[w]}", g, o, core_id) for (w, _), g, o in zip(its, grads, others)]
        recv, sems, token = _scatter_start(f"scatter_start_{part}_{l}", parts)
        pending[(l, part)] = (parts, recv, sems)
        return token

    def layer_grads(l, part, G):
        G_small[l].update(G)
        after = finish_reduce(G["w_ffn_in" if part == "ffn" else "norm_mix"]) if reducing else jnp.zeros(
            (SUBLANES, LANES), F32)
        by_name = dict(
            w_ffn_in=lambda: G["w_ffn_in"], w_ffn_out=lambda: G["w_ffn_out"].reshape(N_CHIPS, -1, D),
            w_rec_in=lambda: G["w_rec_in"],
            w_lru_gates=lambda: G["w_gates"].reshape(NBLK, BW, N_CHIPS, GS).transpose(2, 0, 1, 3).reshape(
                N_CHIPS, NBLK * BW, GS),
            w_rec_out=lambda: G["w_rec_out"].reshape(N_CHIPS, -1, D),
            w_kvf=lambda: jnp.concatenate([G["w_kv"].astype(F32), G["w_f"][:, :H]], axis=1).reshape(
                D, N_CHIPS, -1).transpose(1, 0, 2).astype(BF16),
            w_q=lambda: G["w_q"].reshape(N_CHIPS, -1, D), w_o=lambda: G["w_o"].reshape(N_CHIPS, -1, D))
        its = stage_items(l, part)
        grads = [by_name[_BIG[w]]() for w, _ in its]
        others, sems, token = _reduce_start(f"reduce_start_{part}_{l}", grads, after)
        reducing.append((l, part, its, grads, others, sems))
        return finish_reduce(token) if l == 0 else token

    gains = dict(mix=[norm_mix[l][None] for l in range(L)], ffn=[norm_ffn[l][None] for l in range(L)],
                 final=norm_final[None])
    loss_row, grad_x, dg_final = _local_step(x.reshape(S, D), loss_target.reshape(S, D), gains,
                                             layer_weights, layer_prefetch, layer_grads)

    rows = [*[G_small[l]["norm_mix"] for l in range(L)], *[G_small[l]["norm_ffn"] for l in range(L)],
            G_small[NA]["norm_kv"], dg_final, _pad_lanes(G_small[NA]["b_f"], D), _pad_lanes(loss_row, D)]
    for a in range(NA):
        rows += [G_small[a][n] for n in ("conv_w", "conv_b", "b_gi", "b_gr", "lru_param")]
    packed = jnp.concatenate(rows, axis=0)
    tot = _sum_slots("sum_small", _gather_all("gather_small", packed))
    loss = tot[2 * L + 3, 0]
    g_rep = jnp.concatenate([tot[:2 * L + 2], tot[2 * L + 2:2 * L + 3]], axis=0)
    base = 2 * L + 4
    g_sh = []
    for a in range(NA):
        blk = lax.dynamic_slice_in_dim(tot[base + 8 * a:base + 8 * a + 8], chip * CS, CS, axis=1)
        gi = tot[base + 8 * a + 5].reshape(NBLK, BW)
        gr = tot[base + 8 * a + 6].reshape(NBLK, BW)
        bl = lax.dynamic_slice_in_dim(jnp.concatenate([gi, gr], axis=1), chip * GS, GS, axis=1)
        g_sh += [blk[:5], bl.reshape(-1, CS), blk[7:8]]
    g_sh = jnp.concatenate(g_sh, axis=0)
    nrow = g_sh.shape[0] // NA

    def pack_rep(T):
        return jnp.concatenate([T["norm_mix"], T["norm_ffn"], T["norm_kv"][None], T["norm_final"][None],
                                _pad_lanes(T["b_forget"][None], D)], axis=0)

    def pack_sh(T):
        return jnp.concatenate([jnp.concatenate([T["conv_w"][a], T["conv_b"][a][None],
                                                 T["b_lru_gates"][a].reshape(-1, CS), T["lru_param"][a][None]], axis=0)
                                for a in range(NA)], axis=0)

    rep = _adamw("adamw_replicated", [g_rep], pack_rep(P), pack_rep(M1), pack_rep(M2))
    shd = _adamw("adamw_small_sharded", [g_sh], pack_sh(P), pack_sh(M1), pack_sh(M2))

    def unpack_rep(t):
        return dict(norm_mix=t[:L], norm_ffn=t[L:2 * L], norm_kv=t[2 * L], norm_final=t[2 * L + 1],
                    b_forget=t[2 * L + 2, :H])

    def unpack_sh(t):
        t = t.reshape(NA, nrow, CS)
        return dict(conv_w=t[:, :4], conv_b=t[:, 4], b_lru_gates=t[:, 5:nrow - 1].reshape(NA, NBLK, GS),
                    lru_param=t[:, nrow - 1])

    full = [lax.empty(sh.shape, F32) for sh in shards]
    for l, part in reversed(stages):
        parts, recv, sems = pending[(l, part)]
        recv = _scatter_wait(f"scatter_wait_{part}_{l}", parts, recv, sems)
        for (w, li), own, r in zip(stage_items(l, part), parts, recv):
            full[w] = _sum_chips(f"sum_chips_{l}_{_BIG[w]}", r, own, full[w], li, me_core)
    full = _share_d2d("share_d2d", full)
    big = {w: _adamw(f"adamw_{w}", [g.reshape(P[w].shape)], P[w], M1[w], M2[w]) for w, g in zip(_BIG, full)}

    outs = []
    for i in range(4):
        small = {**unpack_rep(rep[i]), **unpack_sh(shd[i])}
        outs.append([big[w][i] if w in big else small[w] for w in _WEIGHTS])
    return (loss, grad_x.reshape(1, S, D), *outs[0], *outs[1], *outs[2], *outs[3])
```

```python
import functools
import math

import jax
import jax.numpy as jnp
from jax import lax
from jax.experimental import pallas as pl
from jax.experimental.pallas import tpu as pltpu

F32 = jnp.float32
BF16 = jnp.bfloat16

EPS = 1e-6
LRU_C = 8.0
HEAD_DIM = 64
LANES = 128
SUBLANES = 8
VMEM_LIMIT = 60 * 1024 * 1024
N_CHIPS = 4
N_DEV = 8

ADAM_LR = 0.001
ADAM_B1 = 0.9
ADAM_B2 = 0.999
ADAM_EPS = 1e-08
ADAM_WD = 0.01
ADAM_STEP = 10

_NN = (((1,), (0,)), ((), ()))
_NT = (((1,), (1,)), ((), ()))
_TN = (((0,), (0,)), ((), ()))
_DN = {"nn": _NN, "nt": _NT, "tn": _TN}
MESH = pl.DeviceIdType.MESH


def _hbm_out(shape, dtype):
    return pltpu.HBM(shape, dtype)


def _params(sem):
    return pltpu.CompilerParams(dimension_semantics=sem, vmem_limit_bytes=VMEM_LIMIT)


def _tile(n, want):
    if n <= want:
        return n
    t = (want // LANES) * LANES
    while t >= LANES:
        if n % t == 0:
            return t
        t -= LANES
    return n


def _sigmoid(x):
    return 1.0 / (1.0 + jnp.exp(-x))


def _sigmoid_t(x):
    return 0.5 * jnp.tanh(0.5 * x) + 0.5


def _softplus(x):
    return jnp.maximum(x, 0.0) + jnp.log(1.0 + jnp.exp(-jnp.abs(x)))


_GELU_C = math.sqrt(2.0 / math.pi)


def _gelu_and_grad(x):
    inner = _GELU_C * (x + 0.044715 * x * x * x)
    t = jnp.tanh(inner)
    g = 0.5 * x * (1.0 + t)
    dg = 0.5 * (1.0 + t) + 0.5 * x * (1.0 - t * t) * _GELU_C * (1.0 + 3.0 * 0.044715 * x * x)
    return g, dg


def _rms(x):
    return lax.rsqrt(jnp.mean(x * x, axis=-1, keepdims=True) + EPS)


def _rms_bwd(dy, x, g):
    r = _rms(x)
    xr = x * r
    dyg = dy * g
    return r * dyg - xr * (r * jnp.mean(dyg * xr, axis=-1, keepdims=True)), jnp.sum(dy * xr, axis=0, keepdims=True)


def _mm(name, mode, a, b, *, grid, a_spec, b_spec, out_shape, out_dtype, out_spec, nk=1,
        res=None, res_spec=None, bias=None, bias_spec=None, scale=None, norm_gain=None, norm_bwd=None):
    dn = _DN[mode]
    has_res, has_bias = res is not None, bias is not None
    blk = tuple(d for d in out_spec.block_shape if d is not None)
    vec = pl.BlockSpec((1, blk[-1]), lambda *g: (0, 0))
    a_specs = a_spec if isinstance(a_spec, list) else [a_spec]
    b_specs = b_spec if isinstance(b_spec, list) else [b_spec]
    npair = len(a_specs)
    n_in = 2 * npair + int(has_res) + int(has_bias) + (1 if norm_gain is not None else 0) + (3 if norm_bwd else 0)

    def body(*refs):
        p = 2 * npair
        r_ref = refs[p] if has_res else None
        p += int(has_res)
        bias_ref = refs[p] if has_bias else None
        p += int(has_bias)
        extra = refs[p:n_in]
        outs = refs[n_in:]
        o_ref = outs[0]
        part = lax.dot_general(refs[0][...], refs[npair][...], dn, preferred_element_type=F32)
        for t in range(1, npair):
            part = part + lax.dot_general(refs[t][...], refs[npair + t][...], dn, preferred_element_type=F32)

        def finish(acc):
            if scale is not None:
                acc = acc * scale
            if has_bias:
                acc = acc + bias_ref[...]
            if has_res:
                acc = r_ref[...] + acc
            if norm_bwd:
                h_ref, g_ref, dh_ref = extra
                dx, dg = _rms_bwd(acc, h_ref[...], g_ref[...])
                acc = dh_ref[...] + dx
                outs[1][...] = acc.astype(BF16)
                outs[2][...] = dg
            if norm_gain is not None:
                outs[1][...] = (acc * _rms(acc) * extra[0][...]).astype(BF16)
            o_ref[...] = acc.astype(o_ref.dtype)

        if nk == 1:
            finish(part)
        else:
            acc_ref = refs[-1]
            k = pl.program_id(2)

            @pl.when(k == 0)
            def _():
                acc_ref[...] = part

            @pl.when(k > 0)
            def _():
                acc_ref[...] += part

            @pl.when(k == nk - 1)
            def _():
                finish(acc_ref[...])

    ins, specs = [a] * npair + [b] * npair, a_specs + b_specs
    if has_res:
        ins.append(res)
        specs.append(res_spec)
    if has_bias:
        ins.append(bias)
        specs.append(bias_spec)
    out_specs, out_shapes = [out_spec], [_hbm_out(out_shape, out_dtype)]
    if norm_gain is not None:
        ins.append(norm_gain)
        specs.append(vec)
        out_specs.append(out_spec)
        out_shapes.append(_hbm_out(out_shape, BF16))
    if norm_bwd:
        h, g, dh = norm_bwd
        ins += [h, g, dh]
        specs += [out_spec, vec, out_spec]
        out_specs += [out_spec, pl.BlockSpec((None, 1, blk[-1]), lambda i, *rest: (i, 0, 0))]
        out_shapes += [_hbm_out(out_shape, BF16), _hbm_out((grid[0], 1, blk[-1]), F32)]
    sem = ("parallel", "parallel") + (("arbitrary",) if len(grid) == 3 else ())
    single = len(out_specs) == 1
    return pl.pallas_call(
        body, name=name, grid=grid, in_specs=specs, out_specs=out_specs[0] if single else out_specs,
        out_shape=out_shapes[0] if single else out_shapes,
        scratch_shapes=[pltpu.VMEM(blk, F32)] if nk > 1 else [],
        compiler_params=_params(sem),
    )(*ins)


def _mm_nn(name, a, b, *, b_lead=(), out_dtype, tm=512, tn=512, res=None, bias=None, scale=None, norm_gain=None):
    M, K = a.shape
    N = b.shape[-1]
    tm, tn = _tile(M, tm), _tile(N, tn)
    nl = len(b_lead)
    return _mm(
        name, "nn", a, b, grid=(M // tm, N // tn),
        a_spec=pl.BlockSpec((tm, K), lambda i, j: (i, 0)),
        b_spec=pl.BlockSpec((None,) * nl + (K, tn), lambda i, j: tuple(b_lead) + (0, j)),
        out_shape=(M, N), out_dtype=out_dtype, out_spec=pl.BlockSpec((tm, tn), lambda i, j: (i, j)),
        res=res, res_spec=pl.BlockSpec((tm, tn), lambda i, j: (i, j)),
        bias=bias, bias_spec=pl.BlockSpec((1, tn), lambda i, j: (0, j)), scale=scale, norm_gain=norm_gain)


def _mm_nt(name, a, b, *, b_lead=(), out_dtype, tm=512, tn=512, tk=2048, res=None, norm_bwd=None):
    M, K = a.shape
    N = b.shape[-2]
    tm, tn, tk = _tile(M, tm), _tile(N, tn), _tile(K, tk)
    nk = K // tk
    nl = len(b_lead)
    return _mm(
        name, "nt", a, b, grid=(M // tm, N // tn, nk), nk=nk,
        a_spec=pl.BlockSpec((tm, tk), lambda i, j, k: (i, k)),
        b_spec=pl.BlockSpec((None,) * nl + (tn, tk), lambda i, j, k: tuple(b_lead) + (j, k)),
        out_shape=(M, N), out_dtype=out_dtype, out_spec=pl.BlockSpec((tm, tn), lambda i, j, k: (i, j)),
        res=res, res_spec=pl.BlockSpec((tm, tn), lambda i, j, k: (i, j)), norm_bwd=norm_bwd)


def _mm_tn(name, a, b, *, out_dtype, tm=512, tn=512):
    S, M = a.shape
    N = b.shape[1]
    tm, tn = _tile(M, tm), _tile(N, tn)
    return _mm(
        name, "tn", a, b, grid=(M // tm, N // tn),
        a_spec=pl.BlockSpec((S, tm), lambda i, j: (0, i)),
        b_spec=pl.BlockSpec((S, tn), lambda i, j: (0, j)),
        out_shape=(M, N), out_dtype=out_dtype, out_spec=pl.BlockSpec((tm, tn), lambda i, j: (i, j)))


def _rmsnorm_fwd(name, h, g, tr=256):
    S, D = h.shape
    tr = _tile(S, tr)

    def body(h_ref, g_ref, o_ref):
        x = h_ref[...]
        r = lax.rsqrt(jnp.mean(x * x, axis=-1, keepdims=True) + EPS)
        o_ref[...] = (x * r * g_ref[...]).astype(o_ref.dtype)

    return pl.pallas_call(
        body, name=name, grid=(S // tr,),
        in_specs=[pl.BlockSpec((tr, D), lambda i: (i, 0)), pl.BlockSpec((1, D), lambda i: (0, 0))],
        out_specs=pl.BlockSpec((tr, D), lambda i: (i, 0)),
        out_shape=_hbm_out((S, D), BF16),
        compiler_params=_params(("parallel",)),
    )(h, g)


def _loss_head(name, h, target, g, tr=256):
    S, D = h.shape
    tr = _tile(S, tr)

    def body(h_ref, t_ref, g_ref, o_ref, ob_ref, dg_ref, loss_ref):
        i = pl.program_id(0)
        x = h_ref[...]
        gg = g_ref[...]
        r = lax.rsqrt(jnp.mean(x * x, axis=-1, keepdims=True) + EPS)
        xr = x * r
        err = xr * gg - t_ref[...]
        lpart = 0.5 * jnp.sum(jnp.mean(err * err, axis=-1, keepdims=True), axis=0, keepdims=True)
        dy = err * (1.0 / D)
        dyg = dy * gg
        dx = r * dyg - xr * (r * jnp.mean(dyg * xr, axis=-1, keepdims=True))
        o_ref[...] = dx
        ob_ref[...] = dx.astype(BF16)
        part = jnp.sum(dy * xr, axis=0, keepdims=True)
        lrow = jnp.broadcast_to(lpart, (1, LANES))

        @pl.when(i == 0)
        def _():
            dg_ref[...] = part
            loss_ref[...] = lrow

        @pl.when(i > 0)
        def _():
            dg_ref[...] += part
            loss_ref[...] += lrow

    row = pl.BlockSpec((tr, D), lambda i: (i, 0))
    vec = pl.BlockSpec((1, D), lambda i: (0, 0))
    return pl.pallas_call(
        body, name=name, grid=(S // tr,),
        in_specs=[row, row, vec], out_specs=[row, row, vec, pl.BlockSpec((1, LANES), lambda i: (0, 0))],
        out_shape=[_hbm_out((S, D), F32), _hbm_out((S, D), BF16),
                   _hbm_out((1, D), F32), _hbm_out((1, LANES), F32)],
        compiler_params=_params(("arbitrary",)),
    )(h, target, g)


def _swiglu_fwd(name, hn, w_in, tm=512):
    S, D = hn.shape
    FH = w_in.shape[-1]
    tm = _tile(S, tm)

    def body(x_ref, wg_ref, wu_ref, z_ref, a_ref):
        x = x_ref[...]
        zg = jnp.dot(x, wg_ref[...], preferred_element_type=F32)
        zu = jnp.dot(x, wu_ref[...], preferred_element_type=F32)
        z_ref[0] = zg.astype(z_ref.dtype)
        z_ref[1] = zu.astype(z_ref.dtype)
        a_ref[...] = (zg * _sigmoid_t(zg) * zu).astype(a_ref.dtype)

    return pl.pallas_call(
        body, name=name, grid=(S // tm, 2),
        in_specs=[pl.BlockSpec((tm, D), lambda i, j: (i, 0)),
                  pl.BlockSpec((None, D, FH), lambda i, j: (j, 0, 0)),
                  pl.BlockSpec((None, D, FH), lambda i, j: (j + 2, 0, 0))],
        out_specs=[pl.BlockSpec((2, tm, FH), lambda i, j: (0, i, j)), pl.BlockSpec((tm, FH), lambda i, j: (i, j))],
        out_shape=[_hbm_out((2, S, 2 * FH), BF16), _hbm_out((S, 2 * FH), BF16)],
        compiler_params=_params(("parallel", "parallel")),
    )(hn, w_in, w_in)


def _swiglu_bwd(name, dhb, w_out, z3, tm=512):
    S, D = dhb.shape
    F = w_out.shape[0]
    FH = F // 2
    tm = _tile(S, tm)

    def body(d_ref, w_ref, z_ref, dz_ref):
        d = lax.dot_general(d_ref[...], w_ref[...], _NT, preferred_element_type=F32)
        zg = z_ref[0].astype(F32)
        zu = z_ref[1].astype(F32)
        sg = _sigmoid_t(zg)
        dz_ref[0] = (d * zu * (sg * (1.0 + zg * (1.0 - sg)))).astype(dz_ref.dtype)
        dz_ref[1] = (d * (zg * sg)).astype(dz_ref.dtype)

    zspec = pl.BlockSpec((2, tm, FH), lambda i, j: (0, i, j))
    return pl.pallas_call(
        body, name=name, grid=(S // tm, 2),
        in_specs=[pl.BlockSpec((tm, D), lambda i, j: (i, 0)), pl.BlockSpec((FH, D), lambda i, j: (j, 0)), zspec],
        out_specs=zspec, out_shape=_hbm_out((2, S, F), BF16),
        compiler_params=_params(("parallel", "parallel")),
    )(dhb, w_out, z3)


SCAN_ROWS = 64


def _group_scan(A, B, reverse):
    n = A.shape[0]
    sub = lax.broadcasted_iota(jnp.int32, A.shape, 0) % SUBLANES
    for d in (1, 2, 4):
        if reverse:
            A_sh, B_sh = pltpu.roll(A, n - d, 0), pltpu.roll(B, n - d, 0)
            keep = sub < SUBLANES - d
        else:
            A_sh, B_sh = pltpu.roll(A, d, 0), pltpu.roll(B, d, 0)
            keep = sub >= d
        B = jnp.where(keep, A * B_sh + B, B)
        A = jnp.where(keep, A * A_sh, A)
    return A, B


def _block_scan(a, u, carry, reverse):
    A, B = _group_scan(a, u, reverse)
    ng = a.shape[0] // SUBLANES
    out = [None] * ng
    order = range(ng - 1, -1, -1) if reverse else range(ng)
    for gi in order:
        sl = slice(gi * SUBLANES, (gi + 1) * SUBLANES)
        hg = A[sl] * carry + B[sl]
        out[gi] = hg
        carry = hg[0:1] if reverse else hg[SUBLANES - 1:SUBLANES]
    return jnp.concatenate(out, axis=0), carry


def _lru_gates(rc, gip, grp, sp):
    gi = _sigmoid_t(gip)
    gr = _sigmoid_t(grp)
    la = -LRU_C * gr * sp
    a = jnp.exp(la)
    om = -jnp.tanh(la) * (a * a + 1.0)
    mult = jnp.sqrt(om)
    return gi, gr, a, mult


def _lru_fwd(name, proj, rc, gip, grp, lru_p, tc=256):
    S, C = rc.shape
    tc = _tile(C, tc)
    nb = S // SCAN_ROWS

    def body(gb_ref, rc_ref, gi_ref, gr_ref, l_ref, h_ref, m_ref):
        sp = _softplus(-l_ref[...])

        def step(b, carry):
            rows = pl.ds(pl.multiple_of(b * SCAN_ROWS, SCAN_ROWS), SCAN_ROWS)
            rcb = rc_ref[rows, :]
            gi, _, a, mult = _lru_gates(rcb, gi_ref[rows, :], gr_ref[rows, :], sp)
            h, carry = _block_scan(a, rcb * gi * mult, carry, False)
            h_ref[rows, :] = h
            gel, _ = _gelu_and_grad(gb_ref[rows, :])
            m_ref[rows, :] = (gel * h).astype(m_ref.dtype)
            return carry

        lax.fori_loop(0, nb, step, jnp.zeros((1, tc), F32))

    col = pl.BlockSpec((S, tc), lambda j: (0, j))
    return pl.pallas_call(
        body, name=name, grid=(C // tc,),
        in_specs=[col, col, col, col, pl.BlockSpec((1, tc), lambda j: (0, j))],
        out_specs=[col, col],
        out_shape=[_hbm_out((S, C), F32), _hbm_out((S, C), BF16)],
        compiler_params=_params(("parallel",)),
    )(proj, rc, gip, grp, lru_p)


def _lru_bwd(name, dm, proj, hrec, rc, gip, grp, lru_p, tc=256):
    S, C = rc.shape
    tc = _tile(C, tc)
    nb = S // SCAN_ROWS
    R = SCAN_ROWS

    def body(dm_ref, gb_ref, h_ref, rc_ref, gi_ref, gr_ref, l_ref,
             dgb_ref, dgi_ref, dgr_ref, drc_ref, dbi_ref, dbr_ref, dl_ref):
        lp = l_ref[...]
        sp = _softplus(-lp)
        row = lax.broadcasted_iota(jnp.int32, (R, tc), 0)
        zero = jnp.zeros((1, tc), F32)

        def step(t, carry):
            mu_in, s_i, s_r, s_sp = carry
            b = nb - 1 - t
            r0 = pl.multiple_of(b * R, R)
            rows = pl.ds(r0, R)
            rcb = rc_ref[rows, :]
            gi, gr, a, mult = _lru_gates(rcb, gi_ref[rows, :], gr_ref[rows, :], sp)
            gel, dgel = _gelu_and_grad(gb_ref[rows, :])
            dmb = dm_ref[rows, :]
            h = h_ref[rows, :]
            dgb_ref[rows, :] = (dmb * h * dgel).astype(dgb_ref.dtype)
            dh = dmb * gel
            mu, mu_out = _block_scan(a, a * dh, mu_in, True)
            mu_next = jnp.where(row == R - 1, mu_in, pltpu.roll(mu, R - 1, 0))
            lam = dh + mu_next
            p0 = pl.multiple_of(jnp.maximum(r0 - SUBLANES, 0), SUBLANES)
            prev = h_ref[pl.ds(p0, SUBLANES), :][SUBLANES - 1:SUBLANES]
            prev = jnp.where(b > 0, prev, 0.0)
            h_prev = jnp.where(row == 0, prev, pltpu.roll(h, 1, 0))
            da = lam * h_prev
            d_mult = lam * rcb * gi
            d_la = da * a - d_mult * (a * a) / mult
            d_grp = d_la * (-LRU_C * sp) * gr * (1.0 - gr)
            d_gip = lam * rcb * mult * gi * (1.0 - gi)
            dgr_ref[rows, :] = d_grp.astype(dgr_ref.dtype)
            dgi_ref[rows, :] = d_gip.astype(dgi_ref.dtype)
            drc_ref[rows, :] = lam * gi * mult
            s_i = s_i + jnp.sum(d_gip, axis=0, keepdims=True)
            s_r = s_r + jnp.sum(d_grp, axis=0, keepdims=True)
            s_sp = s_sp + jnp.sum(d_la * gr, axis=0, keepdims=True)
            return mu_out, s_i, s_r, s_sp

        _, s_i, s_r, s_sp = lax.fori_loop(0, nb, step, (zero, zero, zero, zero))
        dbi_ref[...] = s_i
        dbr_ref[...] = s_r
        dl_ref[...] = (-LRU_C * s_sp) * (-_sigmoid(-lp))

    col = pl.BlockSpec((S, tc), lambda j: (0, j))
    vec = pl.BlockSpec((1, tc), lambda j: (0, j))
    return pl.pallas_call(
        body, name=name, grid=(C // tc,),
        in_specs=[col, col, col, col, col, col, vec],
        out_specs=[col, col, col, col, vec, vec, vec],
        out_shape=[_hbm_out((S, C), BF16), _hbm_out((S, C), BF16),
                   _hbm_out((S, C), BF16), _hbm_out((S, C), F32),
                   _hbm_out((1, C), F32), _hbm_out((1, C), F32),
                   _hbm_out((1, C), F32)],
        compiler_params=_params(("parallel",)),
    )(dm, proj, hrec, rc, gip, grp, lru_p)


def _cumsum_rows(name, u, reverse):
    S, C = u.shape
    nb = S // SCAN_ROWS

    def body(u_ref, o_ref):
        def step(t, carry):
            b = nb - 1 - t if reverse else t
            rows = pl.ds(pl.multiple_of(b * SCAN_ROWS, SCAN_ROWS), SCAN_ROWS)
            ub = u_ref[rows, :]
            h, carry = _block_scan(jnp.ones_like(ub), ub, carry, reverse)
            o_ref[rows, :] = h
            return carry

        lax.fori_loop(0, nb, step, jnp.zeros((1, C), F32))

    spec = pl.BlockSpec((S, C), lambda i: (0, 0))
    return pl.pallas_call(
        body, name=name, grid=(1,), in_specs=[spec], out_specs=spec,
        out_shape=_hbm_out((S, C), F32),
        compiler_params=_params(("arbitrary",)),
    )(u)


def _shift_down(x, k):
    row = lax.broadcasted_iota(jnp.int32, x.shape, 0)
    return jnp.where(row >= k, pltpu.roll(x, k, 0), 0.0)


def _shift_up(x, k):
    n = x.shape[0]
    row = lax.broadcasted_iota(jnp.int32, x.shape, 0)
    return jnp.where(row < n - k, pltpu.roll(x, n - k, 0), 0.0)


def _conv_fwd(name, proj, w, b, tc=256):
    S, C2 = proj.shape
    C = C2 // 2
    tc = _tile(C, tc)
    off = C // tc

    def body(x_ref, w_ref, b_ref, o_ref, ob_ref):
        x = x_ref[...]
        out = b_ref[...] + w_ref[3:4, :] * x
        for k in (1, 2, 3):
            out = out + w_ref[3 - k:4 - k, :] * _shift_down(x, k)
        o_ref[...] = out
        ob_ref[...] = out.astype(BF16)

    col = pl.BlockSpec((S, tc), lambda j: (0, j))
    return pl.pallas_call(
        body, name=name, grid=(C // tc,),
        in_specs=[pl.BlockSpec((S, tc), lambda j: (0, off + j)),
                  pl.BlockSpec((4, tc), lambda j: (0, j)), pl.BlockSpec((1, tc), lambda j: (0, j))],
        out_specs=[col, col],
        out_shape=[_hbm_out((S, C), F32), _hbm_out((S, C), BF16)],
        compiler_params=_params(("parallel",)),
    )(proj, w, b)


def _conv_bwd(name, drc, proj, w, tc=256):
    S, C = drc.shape
    tc = _tile(C, tc)
    off = C // tc

    def body(y_ref, x_ref, w_ref, dx_ref, dw_ref, db_ref):
        y = y_ref[...]
        x = x_ref[...]
        dx = w_ref[3:4, :] * y
        dw_ref[3:4, :] = jnp.sum(y * x, axis=0, keepdims=True)
        for k in (1, 2, 3):
            dx = dx + w_ref[3 - k:4 - k, :] * _shift_up(y, k)
            dw_ref[3 - k:4 - k, :] = jnp.sum(y * _shift_down(x, k), axis=0, keepdims=True)
        dx_ref[...] = dx.astype(dx_ref.dtype)
        db_ref[...] = jnp.sum(y, axis=0, keepdims=True)

    col = pl.BlockSpec((S, tc), lambda j: (0, j))
    return pl.pallas_call(
        body, name=name, grid=(C // tc,),
        in_specs=[col, pl.BlockSpec((S, tc), lambda j: (0, off + j)), pl.BlockSpec((4, tc), lambda j: (0, j))],
        out_specs=[col, pl.BlockSpec((4, tc), lambda j: (0, j)), pl.BlockSpec((1, tc), lambda j: (0, j))],
        out_shape=[_hbm_out((S, C), BF16), _hbm_out((4, C), F32),
                   _hbm_out((1, C), F32)],
        compiler_params=_params(("parallel",)),
    )(drc, proj, w)


def _gates_fwd(name, rcb, wg, bg):
    S, C = rcb.shape
    nblk, bw, _ = wg.shape

    def body(x_ref, w_ref, b_ref, gi_ref, gr_ref):
        g = jnp.dot(x_ref[...], w_ref[...], preferred_element_type=F32) + b_ref[...]
        gi_ref[...] = g[:, :bw]
        gr_ref[...] = g[:, bw:]

    col = pl.BlockSpec((S, bw), lambda n: (0, n))
    return pl.pallas_call(
        body, name=name, grid=(nblk,),
        in_specs=[col, pl.BlockSpec((None, bw, 2 * bw), lambda n: (n, 0, 0)),
                  pl.BlockSpec((None, 1, 2 * bw), lambda n: (n, 0, 0))],
        out_specs=[col, col],
        out_shape=[_hbm_out((S, C), F32), _hbm_out((S, C), F32)],
        compiler_params=_params(("parallel",)),
    )(rcb, wg, bg)


def _gates_bwd(name, dgi, dgr, rcb, wg, drc1):
    S, C = rcb.shape
    nblk, bw, _ = wg.shape

    def body(dgi_ref, dgr_ref, x_ref, w_ref, d1_ref, drc_ref, dw_ref):
        w = w_ref[...]
        x = x_ref[...]
        di, dr = dgi_ref[...], dgr_ref[...]
        drc_ref[...] = (d1_ref[...]
                        + lax.dot_general(di, w[:, :bw], _NT, preferred_element_type=F32)
                        + lax.dot_general(dr, w[:, bw:], _NT, preferred_element_type=F32))
        dw_ref[:, :bw] = lax.dot_general(x, di, _TN, preferred_element_type=F32).astype(dw_ref.dtype)
        dw_ref[:, bw:] = lax.dot_general(x, dr, _TN, preferred_element_type=F32).astype(dw_ref.dtype)

    col = pl.BlockSpec((S, bw), lambda n: (0, n))
    wspec = pl.BlockSpec((None, bw, 2 * bw), lambda n: (n, 0, 0))
    return pl.pallas_call(
        body, name=name, grid=(nblk,),
        in_specs=[col, col, col, wspec, col], out_specs=[col, wspec],
        out_shape=[_hbm_out((S, C), F32), _hbm_out((nblk, bw, 2 * bw), BF16)],
        compiler_params=_params(("parallel",)),
    )(dgi, dgr, rcb, wg, drc1)


def _att_tile(S):
    return next(t for t in (512, 256, 128) if S % t == 0)


def _head_lanes(shape):
    return lax.broadcasted_iota(jnp.int32, shape, len(shape) - 1) < HEAD_DIM


def _key_bias(c_blk):
    first = _head_lanes(c_blk.shape)
    rolled = pltpu.roll(c_blk, HEAD_DIM, 1)
    return jnp.where(first, c_blk, rolled), jnp.where(first, rolled, c_blk)


def _over_keys(x, op):
    n = x.shape[0]
    while n > SUBLANES:
        n //= 2
        x = op(x[:n], x[n:2 * n])
    return (jnp.max if op is jnp.maximum else jnp.sum)(x, axis=0, keepdims=True)


def _causal_t(T, cc):
    r = lax.broadcasted_iota(jnp.int32, (T, LANES), 0)
    c = lax.broadcasted_iota(jnp.int32, (T, LANES), 1) + cc * LANES
    return r <= c


def _attn_fwd(name, q, kv, cfull):
    S, D = q.shape
    HP = D // LANES
    T = _att_tile(S)
    nq = S // T
    NC = T // LANES

    def body(q_ref, k_ref, v_ref, c_ref, o_ref, of_ref, lse_ref, bias, vT, acc, m_scr, l_scr):
        def prologue(i, _):
            rows = pl.ds(pl.multiple_of(i * T, T), T)
            bias[0, rows, :], bias[1, rows, :] = _key_bias(c_ref[rows, :])
            vT[i] = v_ref[rows, :].astype(F32).T.astype(BF16)
            return 0

        lax.fori_loop(0, nq, prologue, 0)

        def q_step(qi, _):
            q0 = pl.multiple_of(qi * T, T)
            qb = q_ref[pl.ds(q0, T), :]
            m_scr[...] = jnp.full(m_scr.shape, -jnp.inf, F32)
            l_scr[...] = jnp.zeros(l_scr.shape, F32)
            acc[...] = jnp.zeros(acc.shape, F32)

            def tile(kj, masked):
                ks = pl.ds(pl.multiple_of(kj * T, T), T)
                kf = k_ref[ks, :].astype(F32)
                first = _head_lanes(kf.shape)
                kms = [jnp.where(first if hh == 0 else jnp.logical_not(first), kf, 0.0).astype(BF16) for hh in range(2)]
                sTs = [lax.dot_general(km, qb, _NT, preferred_element_type=F32) for km in kms]
                for hh in range(2):
                    b = bias[hh, ks, :]
                    ps = []
                    for cc in range(NC):
                        cols = slice(cc * LANES, (cc + 1) * LANES)
                        s = sTs[hh][:, cols] + b
                        if masked:
                            s = jnp.where(_causal_t(T, cc), s, -jnp.inf)
                        m_old = m_scr[hh, cc]
                        m_new = jnp.maximum(m_old, _over_keys(s, jnp.maximum))
                        alpha = jnp.exp(m_old - m_new)
                        p = jnp.exp(s - m_new)
                        l_scr[hh, cc] = alpha * l_scr[hh, cc] + _over_keys(p, jnp.add)
                        m_scr[hh, cc] = m_new
                        ps.append(p.astype(BF16))
                        acc[hh, :, cols] = acc[hh, :, cols] * alpha
                    acc[hh] += jnp.dot(vT[kj, hh * HEAD_DIM:(hh + 1) * HEAD_DIM, :], jnp.concatenate(ps, axis=1),
                                       preferred_element_type=F32)

            def inner(kj, _):
                tile(kj, False)
                return 0

            lax.fori_loop(0, qi, inner, 0)
            tile(qi, True)
            outs = []
            for hh in range(2):
                inv = jnp.concatenate([1.0 / l_scr[hh, cc] for cc in range(NC)], axis=1)
                outs.append(acc[hh] * inv)
                for cc in range(NC):
                    lse_ref[hh:hh + 1, pl.ds(q0 + cc * LANES, LANES)] = m_scr[hh, cc] + jnp.log(l_scr[hh, cc])
            out = jnp.concatenate(outs, axis=0).T
            o_ref[pl.ds(q0, T), :] = out.astype(o_ref.dtype)
            of_ref[pl.ds(q0, T), :] = out
            return 0

        lax.fori_loop(0, nq, q_step, 0)

    blk = lambda off: pl.BlockSpec((S, LANES), lambda p: (0, off + p))
    return pl.pallas_call(
        body, name=name, grid=(HP,),
        in_specs=[blk(0), blk(0), blk(HP), blk(0)],
        out_specs=[blk(0), blk(0), pl.BlockSpec((None, 2, S), lambda p: (p, 0, 0))],
        out_shape=[_hbm_out((S, D), BF16), _hbm_out((S, D), F32),
                   _hbm_out((HP, 2, S), F32)],
        scratch_shapes=[pltpu.VMEM((2, S, LANES), F32), pltpu.VMEM((nq, LANES, T), BF16),
                        pltpu.VMEM((2, HEAD_DIM, T), F32), pltpu.VMEM((2, NC, 1, LANES), F32),
                        pltpu.VMEM((2, NC, 1, LANES), F32)],
        compiler_params=_params(("parallel",)),
    )(q, kv, kv, cfull)


def _attn_bwd(name, q, kv, cfull, of, do, lse3):
    S, D = q.shape
    HP = D // LANES
    T = _att_tile(S)
    nq = S // T
    NC = T // LANES
    scale = HEAD_DIM ** -0.5

    def body(q_ref, k_ref, v_ref, c_ref, of_ref, do_ref, lse_ref,
             dq_ref, dk_ref, dv_ref, dck_ref, drq_ref, bias, kT, dqT, delta, dr_scr):
        def prologue(i, _):
            rows = pl.ds(pl.multiple_of(i * T, T), T)
            bias[0, rows, :], bias[1, rows, :] = _key_bias(c_ref[rows, :])
            kT[i] = k_ref[rows, :].astype(F32).T.astype(BF16)
            prodT = (do_ref[rows, :].astype(F32) * of_ref[rows, :]).T
            for hh in range(2):
                delta[hh:hh + 1, rows] = jnp.sum(prodT[hh * HEAD_DIM:(hh + 1) * HEAD_DIM], axis=0, keepdims=True)
            dqT[i] = jnp.zeros((LANES, T), F32)
            return 0

        lax.fori_loop(0, nq, prologue, 0)
        dr_scr[...] = jnp.zeros(dr_scr.shape, F32)

        def kv_step(kj, _):
            ks = pl.ds(pl.multiple_of(kj * T, T), T)
            kf = k_ref[ks, :].astype(F32)
            vf = v_ref[ks, :].astype(F32)
            first = _head_lanes(kf.shape)
            masks = [first, jnp.logical_not(first)]
            kms = [jnp.where(m, kf, 0.0).astype(BF16) for m in masks]
            vms = [jnp.where(m, vf, 0.0).astype(BF16) for m in masks]

            def tile(qi, carry, masked):
                q0 = pl.multiple_of(qi * T, T)
                qb = q_ref[pl.ds(q0, T), :]
                dob = do_ref[pl.ds(q0, T), :]
                sTs = [lax.dot_general(km, qb, _NT, preferred_element_type=F32) for km in kms]
                dpTs = [lax.dot_general(vm, dob, _NT, preferred_element_type=F32) for vm in vms]
                out = []
                for hh in range(2):
                    dk_a, dv_a, dc_a = carry[3 * hh:3 * hh + 3]
                    b = bias[hh, ks, :]
                    head = slice(hh * HEAD_DIM, (hh + 1) * HEAD_DIM)
                    ps, dss = [], []
                    for cc in range(NC):
                        cols = slice(cc * LANES, (cc + 1) * LANES)
                        at = pl.ds(q0 + cc * LANES, LANES)
                        p = jnp.exp(sTs[hh][:, cols] + b - lse_ref[hh:hh + 1, at])
                        if masked:
                            p = jnp.where(_causal_t(T, cc), p, 0.0)
                        ds = p * (dpTs[hh][:, cols] - delta[hh:hh + 1, at])
                        ps.append(p.astype(BF16))
                        dss.append(ds.astype(BF16))
                        dc_a = dc_a + ds
                        dr_scr[hh:hh + 1, at] += _over_keys(ds, jnp.add)
                    pT = jnp.concatenate(ps, axis=1)
                    dsT = jnp.concatenate(dss, axis=1)
                    dv_a = dv_a + jnp.dot(pT, dob, preferred_element_type=F32)
                    dk_a = dk_a + jnp.dot(dsT, qb, preferred_element_type=F32)
                    dqT[qi, head, :] += jnp.dot(kT[kj, head, :], dsT, preferred_element_type=F32)
                    out += [dk_a, dv_a, dc_a]
                return tuple(out)

            zero = jnp.zeros((T, LANES), F32)
            carry = tile(kj, (zero,) * 6, True)
            dk0, dv0, dc0, dk1, dv1, dc1 = lax.fori_loop(kj + 1, nq, lambda qi, c: tile(qi, c, False), carry)
            dk_ref[ks, :] = jnp.where(first, dk0, dk1)
            dv_ref[ks, :] = jnp.where(first, dv0, dv1)
            dck_ref[ks, :] = jnp.where(first, jnp.broadcast_to(-jnp.sum(dc0, axis=1, keepdims=True), (T, LANES)),
                                       jnp.broadcast_to(-jnp.sum(dc1, axis=1, keepdims=True), (T, LANES)))
            return 0

        lax.fori_loop(0, nq, kv_step, 0)

        def epilogue(i, _):
            rows = pl.ds(pl.multiple_of(i * T, T), T)
            dq_ref[rows, :] = (dqT[i].T * scale).astype(dq_ref.dtype)
            return 0

        lax.fori_loop(0, nq, epilogue, 0)
        drq_ref[...] = dr_scr[...]

    blk = lambda off: pl.BlockSpec((S, LANES), lambda p: (0, off + p))
    row_spec = pl.BlockSpec((None, 2, S), lambda p: (p, 0, 0))
    return pl.pallas_call(
        body, name=name, grid=(HP,),
        in_specs=[blk(0), blk(0), blk(HP), blk(0), blk(0), blk(0), row_spec],
        out_specs=[blk(0), blk(0), blk(0), blk(0), row_spec],
        out_shape=[_hbm_out((S, D), BF16), _hbm_out((S, D), F32),
                   _hbm_out((S, D), F32), _hbm_out((S, D), F32),
                   _hbm_out((HP, 2, S), F32)],
        scratch_shapes=[pltpu.VMEM((2, S, LANES), F32), pltpu.VMEM((nq, LANES, T), BF16),
                        pltpu.VMEM((nq, LANES, T), F32), pltpu.VMEM((2, S), F32), pltpu.VMEM((2, S), F32)],
        compiler_params=_params(("parallel",)),
    )(q, kv, kv, cfull, of, do, lse3)


def _logsig_fwd(name, f):
    S, C = f.shape

    def body(f_ref, o_ref):
        o_ref[...] = -_softplus(-f_ref[...])

    spec = pl.BlockSpec((S, C), lambda i: (0, 0))
    return pl.pallas_call(body, name=name, grid=(1,), in_specs=[spec], out_specs=spec,
                          out_shape=_hbm_out((S, C), F32),
                          compiler_params=_params(("arbitrary",)))(f)


def _logsig_bwd(name, dls, f):
    S, C = f.shape

    def body(d_ref, f_ref, o_ref, s_ref):
        df = d_ref[...] * _sigmoid(-f_ref[...])
        o_ref[...] = df.astype(o_ref.dtype)
        s_ref[...] = jnp.sum(df, axis=0, keepdims=True)

    spec = pl.BlockSpec((S, C), lambda i: (0, 0))
    return pl.pallas_call(body, name=name, grid=(1,), in_specs=[spec, spec],
                          out_specs=[spec, pl.BlockSpec((1, C), lambda i: (0, 0))],
                          out_shape=[_hbm_out((S, C), BF16), _hbm_out((1, C), F32)],
                          compiler_params=_params(("arbitrary",)))(dls, f)


def _add_cast(name, parts, out_dtype, tr=256):
    S, C = parts[0].shape
    tr = _tile(S, tr)
    n = len(parts)

    def body(*refs):
        acc = refs[0][...].astype(F32)
        for r in refs[1:n]:
            acc = acc + r[...].astype(F32)
        refs[n][...] = acc.astype(out_dtype)

    spec = pl.BlockSpec((tr, C), lambda i: (i, 0))
    return pl.pallas_call(body, name=name, grid=(S // tr,), in_specs=[spec] * n, out_specs=spec,
                          out_shape=_hbm_out((S, C), out_dtype),
                          compiler_params=_params(("parallel",)))(*parts)


def _local_step(x, target, gains, layer_weights, layer_prefetch, layer_grads):
    S, D = x.shape
    HP = D // LANES
    scale = HEAD_DIM ** -0.5
    tm = _tile(S, 512)
    tx = _tile(S, 256)
    td = _tile(D, 512)
    saved = []
    h = x
    l = 0
    kv = cfull = f_pre = hn_kv = h_kv = None
    while True:
        W = layer_weights(l, "mix", h)
        if W is None:
            break
        recurrent = "w_rec_in" in W
        if l == 0:
            xn = _rmsnorm_fwd("mix_norm_0", h, gains["mix"][0])
        if recurrent:
            CH = W["w_rec_in"].shape[-1]
            C = 2 * CH
            proj = _mm(f"rec_in_{l}", "nn", xn, W["w_rec_in"], grid=(S // tm, N_CHIPS),
                       a_spec=pl.BlockSpec((tm, D), lambda i, j: (i, 0)),
                       b_spec=pl.BlockSpec((None, D, CH), lambda i, j: (j, 0, 0)),
                       out_shape=(S, 2 * C), out_dtype=F32,
                       out_spec=pl.BlockSpec((tm, CH), lambda i, j: (i, j)))
            rc, rcb = _conv_fwd(f"conv_{l}", proj, W["conv_w"], W["conv_b"])
            gip, grp = _gates_fwd(f"gates_{l}", rcb, W["w_gates"], W["b_gates"])
            hrec, m = _lru_fwd(f"lru_{l}", proj, rc, gip, grp, W["lru_param"])
            layer_prefetch(l, "ffn", m)
            h_mid, hn = _mm_nn(f"rec_out_{l}", m, W["w_rec_out"], out_dtype=F32, res=h, tn=D, norm_gain=gains["ffn"][l])
            mix_saved = (xn, proj, rc, rcb, gip, grp, hrec, m)
        else:
            if "w_kv" in W:
                h_kv = h
                hn_kv = _rmsnorm_fwd("kv_norm", h, W["norm_kv"])
                kv = _mm_nn("kv_proj", hn_kv, W["w_kv"], out_dtype=BF16)
                f_pre = _mm_nn("f_proj", hn_kv, W["w_f"], out_dtype=F32, bias=W["b_f"])
                c = _cumsum_rows("c_cumsum", _logsig_fwd("logsig", f_pre), False)
                cfull = jnp.repeat(-c[:, :2 * HP], HEAD_DIM, axis=1)
            q = _mm_nn(f"q_proj_{l}", xn, W["w_q"], out_dtype=BF16, scale=scale)
            o, of, lse = _attn_fwd(f"attn_fwd_{l}", q, kv, cfull)
            layer_prefetch(l, "ffn", o)
            h_mid, hn = _mm_nn(f"o_proj_{l}", o, W["w_o"], out_dtype=F32, res=h, tn=D, norm_gain=gains["ffn"][l])
            mix_saved = (xn, q, o, of, lse)
        W = {**W, **layer_weights(l, "ffn", h_mid)}
        z3, act = _swiglu_fwd(f"ffn_in_{l}", hn, W["w_ffn_in"])
        layer_prefetch(l + 1, "mix", act)
        saved.append((W, h, h_mid, mix_saved, (hn, z3, act)))
        l += 1
        if l < len(gains["mix"]):
            h, xn = _mm_nn(f"ffn_out_{l - 1}", act, W["w_ffn_out"], out_dtype=F32, res=h_mid, tn=D,
                           norm_gain=gains["mix"][l])
        else:
            h = _mm_nn(f"ffn_out_{l - 1}", act, W["w_ffn_out"], out_dtype=F32, res=h_mid, tn=D)

    dh, dhb, dg_final, loss_row = _loss_head("loss_head", h, target, gains["final"])

    dk_parts, dv_parts, dc_parts = [], [], []
    token = None
    for l in reversed(range(len(saved))):
        W, h_in, h_mid, mix_saved, (hn, z3, act) = saved[l]
        recurrent = "w_rec_in" in W
        FH = W["w_ffn_in"].shape[-1]
        G = {}
        norm_ffn = gains["ffn"][l]
        if token is not None:
            norm_ffn = norm_ffn + jnp.minimum(token[:1, :1], 0.0)
        G["w_ffn_out"] = _mm_tn(f"d_ffn_out_{l}", act, dhb, out_dtype=BF16, tn=D)
        dz3 = _swiglu_bwd(f"d_act_{l}", dhb, W["w_ffn_out"], z3)
        G["w_ffn_in"] = _mm(
            f"d_ffn_in_{l}", "tn", hn, dz3, grid=(D // td, N_CHIPS),
            a_spec=pl.BlockSpec((S, td), lambda i, j: (0, i)),
            b_spec=pl.BlockSpec((None, S, FH), lambda i, j: (j // 2, 0, j % 2)),
            out_shape=(N_CHIPS, D, FH), out_dtype=BF16,
            out_spec=pl.BlockSpec((None, td, FH), lambda i, j: (j, i, 0)))
        token = layer_grads(l, "ffn", G)
        G = {}
        norm_ffn = norm_ffn + jnp.minimum(token[:1, :1], 0.0)
        dh, dhb, dgp = _mm(f"d_ffn_hn_{l}", "nt", dz3, W["w_ffn_in"], grid=(S // tx, 1),
                           a_spec=[pl.BlockSpec((None, tx, FH), functools.partial(lambda i, j, k: (k // 2, i, k % 2), k=k))
                                   for k in range(N_CHIPS)],
                           b_spec=[pl.BlockSpec((None, D, FH), functools.partial(lambda i, j, k: (k, 0, 0), k=k))
                                   for k in range(N_CHIPS)],
                           out_shape=(S, D), out_dtype=F32, out_spec=pl.BlockSpec((tx, D), lambda i, j: (i, 0)),
                           norm_bwd=(h_mid, norm_ffn, dh))
        G["norm_ffn"] = jnp.sum(dgp, axis=0)
        if recurrent:
            CH = W["w_rec_in"].shape[-1]
            C = 2 * CH
            xn, proj, rc, rcb, gip, grp, hrec, m = mix_saved
            G["w_rec_out"] = _mm_tn(f"d_rec_out_{l}", m, dhb, out_dtype=BF16, tn=D)
            dm = _mm_nt(f"d_m_{l}", dhb, W["w_rec_out"], out_dtype=F32, tn=C)
            dgb, dgi, dgr, drc1, G["b_gi"], G["b_gr"], G["lru_param"] = _lru_bwd(
                f"d_lru_{l}", dm, proj, hrec, rc, gip, grp, W["lru_param"])
            drc, G["w_gates"] = _gates_bwd(f"d_gates_{l}", dgi, dgr, rcb, W["w_gates"], drc1)
            drec, G["conv_w"], G["conv_b"] = _conv_bwd(f"d_conv_{l}", drc, proj, W["conv_w"])
            dproj = jnp.concatenate([dgb, drec], axis=1)
            G["w_rec_in"] = _mm(
                f"d_rec_in_{l}", "tn", xn, dproj, grid=(1, N_CHIPS),
                a_spec=pl.BlockSpec((S, D), lambda i, j: (0, 0)),
                b_spec=pl.BlockSpec((S, CH), lambda i, j: (0, j)),
                out_shape=(N_CHIPS, D, CH), out_dtype=BF16,
                out_spec=pl.BlockSpec((None, D, CH), lambda i, j: (j, 0, 0)))
            dh, dhb, dgp = _mm(f"d_rec_xn_{l}", "nt", dproj, W["w_rec_in"], grid=(S // tx, 1),
                               a_spec=[pl.BlockSpec((tx, CH), functools.partial(lambda i, j, k: (i, k), k=k))
                                       for k in range(N_CHIPS)],
                               b_spec=[pl.BlockSpec((None, D, CH), functools.partial(lambda i, j, k: (k, 0, 0), k=k))
                                       for k in range(N_CHIPS)],
                               out_shape=(S, D), out_dtype=F32, out_spec=pl.BlockSpec((tx, D), lambda i, j: (i, 0)),
                               norm_bwd=(h_in, gains["mix"][l], dh))
        else:
            xn, q, o, of, lse = mix_saved
            G["w_o"] = _mm_tn(f"d_o_proj_{l}", o, dhb, out_dtype=BF16, tn=D)
            do = _mm_nt(f"d_o_{l}", dhb, W["w_o"], out_dtype=BF16, tn=D)
            dq, dk, dv, dck, drq = _attn_bwd(f"attn_bwd_{l}", q, kv, cfull, of, do, lse)
            dk_parts.append(dk)
            dv_parts.append(dv)
            dc_parts.append(dck[:, ::HEAD_DIM] + drq.reshape(2 * HP, S).T)
            G["w_q"] = _mm_tn(f"d_q_proj_{l}", xn, dq, out_dtype=BF16, tn=D)
            dh, dhb, dgp = _mm_nt(f"d_q_xn_{l}", dq, W["w_q"], out_dtype=F32, tn=D, norm_bwd=(h_in, gains["mix"][l], dh))
        G["norm_mix"] = jnp.sum(dgp, axis=0)
        if "w_kv" in W:
            dkb = _add_cast("dk_sum", dk_parts, BF16)
            dvb = _add_cast("dv_sum", dv_parts, BF16)
            dkv = jnp.concatenate([dkb, dvb], axis=1)
            dc = sum(dc_parts[1:], dc_parts[0])
            dc_pad = jnp.pad(dc, ((0, 0), (0, LANES - 2 * HP)))
            dls = _cumsum_rows("dc_cumsum", dc_pad, True)
            dfb, G["b_f"] = _logsig_bwd("d_logsig", dls, f_pre)
            G["w_kv"] = _mm_tn("d_kv_proj", hn_kv, dkv, out_dtype=BF16)
            G["w_f"] = _mm_tn("d_f_proj", hn_kv, dfb, out_dtype=F32)
            dhn_f = _mm_nt("d_f_hn", dfb, W["w_f"], out_dtype=F32, tn=D)
            dh, dhb, dgp = _mm_nt("d_kv_hn", dkv, W["w_kv"], out_dtype=F32, tn=D, res=dhn_f,
                                  norm_bwd=(h_kv, W["norm_kv"], dh))
            G["norm_kv"] = jnp.sum(dgp, axis=0)
        token = layer_grads(l, "mix", G)
    return loss_row, dh, dg_final


_ANY = pl.BlockSpec(memory_space=pl.ANY)


def _position():
    return lax.axis_index("x"), lax.axis_index("y"), lax.axis_index("c")


def _chip_peers(x, y):
    return [(1 - x, y), (x, 1 - y), (1 - x, 1 - y)]


def _half_rows(c, n):
    h = n // 2
    assert h % 16 == 0
    return pl.ds(pl.multiple_of(c * h, 16), h)


def _place_own(name, shard, layer, me):
    _, R, C = shard.shape
    tr = _row_tile(R, C, 2 * shard.dtype.itemsize, target=8 << 20)

    def body(me_ref, x_ref, o_ref):
        o_ref[...] = x_ref[...]

    return pl.pallas_call(
        body, name=name,
        grid_spec=pltpu.PrefetchScalarGridSpec(
            num_scalar_prefetch=1, grid=(R // tr,),
            in_specs=[pl.BlockSpec((None, tr, C), lambda i, me_ref: (layer, i, 0))],
            out_specs=pl.BlockSpec((None, tr, C), lambda i, me_ref: (me_ref[0], i, 0))),
        out_shape=_hbm_out((N_CHIPS, R, C), shard.dtype),
        compiler_params=_params(("parallel",)),
    )(me, shard)


def _gather_smalls(name, smalls):
    ns = len(smalls)

    def body(*refs):
        ins, outs = refs[:ns], refs[ns:2 * ns]
        send_sems, recv_sems, local_sems = refs[2 * ns:]
        x, y, c = _position()
        me = 2 * x + y
        peers = _chip_peers(x, y)

        def remote(t, k, chip):
            px, py = peers[k]
            return pltpu.make_async_remote_copy(
                src_ref=ins[t], dst_ref=outs[t].at[chip], send_sem=send_sems.at[3 * t + k],
                recv_sem=recv_sems.at[3 * t + k], device_id=(px, py, c), device_id_type=MESH)

        local = [pltpu.make_async_copy(ins[t], outs[t].at[me], local_sems.at[t]) for t in range(ns)]
        for t in range(ns):
            local[t].start()
            for k in range(3):
                remote(t, k, me).start()
        for t in range(ns):
            for k in range(3):
                px, py = peers[k]
                remote(t, k, 2 * px + py).wait_recv()
        for t in range(ns):
            for k in range(3):
                remote(t, k, me).wait_send()
            local[t].wait()

    return pl.pallas_call(
        body, name=name, in_specs=[_ANY] * ns, out_specs=[_ANY] * ns,
        out_shape=[_hbm_out((N_CHIPS,) + s.shape, s.dtype) for s in smalls],
        scratch_shapes=[pltpu.SemaphoreType.DMA((3 * ns,)), pltpu.SemaphoreType.DMA((3 * ns,)),
                        pltpu.SemaphoreType.DMA((ns,))],
    )(*smalls)


_SEM = pl.BlockSpec(memory_space=pltpu.SEMAPHORE)
_SPLIT = pltpu.CompilerParams(has_side_effects=pltpu.SideEffectType.DATAFLOW_SIDE_EFFECTING)


def _weight_copy(shards, buf, items, sems, i, k, chip_of_dst, peers, c):
    w, l = items[i]
    px, py = peers[k]
    half = _half_rows(c, shards[w].shape[1])
    return pltpu.make_async_remote_copy(
        src_ref=shards[w].at[l, half], dst_ref=buf.at[chip_of_dst, half],
        send_sem=sems[0].at[3 * i + k], recv_sem=sems[1].at[3 * i + k],
        device_id=(px, py, c), device_id_type=MESH)


def _gather_start(name, shards, bufs, items, after):
    nw, n = len(shards), len(bufs)

    def body(*refs):
        ins, outs, sems = refs[:nw], refs[nw + n + 1:nw + 2 * n + 1], refs[nw + 2 * n + 1:]
        x, y, c = _position()
        peers = _chip_peers(x, y)
        for i in range(n):
            for k in range(3):
                _weight_copy(ins, outs[i], items, sems, i, k, 2 * x + y, peers, c).start()

    res = pl.pallas_call(
        body, name=name, in_specs=[_ANY] * (nw + n + 1), out_specs=[_ANY] * n + [_SEM, _SEM],
        out_shape=[_hbm_out(b.shape, b.dtype) for b in bufs]
        + [pltpu.SemaphoreType.DMA((3 * n,)), pltpu.SemaphoreType.DMA((3 * n,))],
        input_output_aliases={nw + i: i for i in range(n)}, compiler_params=_SPLIT,
    )(*shards, *bufs, after)
    return res[:n], res[n:]


def _gather_wait(name, shards, bufs, items, ids, sems, after):
    nw, m = len(shards), len(ids)

    def body(*refs):
        ins, bs = refs[:nw], refs[nw:nw + m]
        sem_refs = refs[nw + m:nw + m + 2]
        x, y, c = _position()
        peers = _chip_peers(x, y)
        for j, i in enumerate(ids):
            for k in range(3):
                px, py = peers[k]
                _weight_copy(ins, bs[j], items, sem_refs, i, k, 2 * px + py, peers, c).wait_recv()
        for j, i in enumerate(ids):
            for k in range(3):
                _weight_copy(ins, bs[j], items, sem_refs, i, k, 2 * x + y, peers, c).wait_send()

    res = pl.pallas_call(
        body, name=name, in_specs=[_ANY] * (nw + m) + [_SEM, _SEM, _ANY], out_specs=[_ANY] * m,
        out_shape=[_hbm_out(bufs[i].shape, bufs[i].dtype) for i in ids],
        input_output_aliases={nw + j: j for j in range(m)}, compiler_params=_SPLIT,
    )(*shards, *[bufs[i] for i in ids], *sems, after)
    return list(res)


def _forward_copy(src, dst, sems, i, k, core):
    x, y, c = _position()
    px, py = _chip_peers(x, y)[k]
    half = _half_rows(core, src.shape[1])
    return pltpu.make_async_remote_copy(
        src_ref=src.at[2 * px + py, half], dst_ref=dst.at[2 * px + py, half],
        send_sem=sems[0].at[3 * i + k], recv_sem=sems[1].at[3 * i + k],
        device_id=(x, y, 1 - c), device_id_type=MESH)


def _forward_start(name, bufs):
    n = len(bufs)

    def body(*refs):
        ins, outs, sems = refs[:n], refs[n:2 * n], refs[2 * n:]
        c = lax.axis_index("c")
        for i in range(n):
            for k in range(3):
                _forward_copy(ins[i], outs[i], sems, i, k, c).start()

    res = pl.pallas_call(
        body, name=name, in_specs=[_ANY] * n, out_specs=[_ANY] * n + [_SEM, _SEM],
        out_shape=[_hbm_out(g.shape, g.dtype) for g in bufs]
        + [pltpu.SemaphoreType.DMA((3 * n,)), pltpu.SemaphoreType.DMA((3 * n,))],
        input_output_aliases={i: i for i in range(n)}, compiler_params=_SPLIT,
    )(*bufs)
    return list(res[:n]), res[n:]


def _forward_wait(name, bufs, sems, after):
    n = len(bufs)

    def body(*refs):
        bs, sem_refs = refs[:n], refs[n:n + 2]
        c = lax.axis_index("c")
        for i in range(n):
            for k in range(3):
                _forward_copy(bs[i], bs[i], sem_refs, i, k, 1 - c).wait_recv()
        for i in range(n):
            for k in range(3):
                _forward_copy(bs[i], bs[i], sem_refs, i, k, c).wait_send()

    return list(pl.pallas_call(
        body, name=name, in_specs=[_ANY] * n + [_SEM, _SEM, _ANY], out_specs=[_ANY] * n,
        out_shape=[_hbm_out(g.shape, g.dtype) for g in bufs],
        input_output_aliases={i: i for i in range(n)}, compiler_params=_SPLIT,
    )(*bufs, *sems, after))


def _reduce_copy(grads, others, sems, i):
    x, y, c = _position()
    return pltpu.make_async_remote_copy(
        src_ref=grads[i].at[:, _half_rows(1 - c, grads[i].shape[1])], dst_ref=others[i],
        send_sem=sems[0].at[i], recv_sem=sems[1].at[i], device_id=(x, y, 1 - c), device_id_type=MESH)


def _reduce_start(name, grads, after):
    n = len(grads)

    def body(*refs):
        ins, outs, sems, token = refs[:n], refs[n + 1:2 * n + 1], refs[2 * n + 1:2 * n + 3], refs[2 * n + 3]
        for i in range(n):
            _reduce_copy(ins, outs, sems, i).start()
        token[...] = jnp.zeros_like(token)

    res = pl.pallas_call(
        body, name=name, in_specs=[_ANY] * (n + 1),
        out_specs=[_ANY] * n + [_SEM, _SEM, pl.BlockSpec(memory_space=pltpu.VMEM)],
        out_shape=[_hbm_out((N_CHIPS, g.shape[1] // 2, g.shape[2]), g.dtype) for g in grads]
        + [pltpu.SemaphoreType.DMA((n,)), pltpu.SemaphoreType.DMA((n,)), jax.ShapeDtypeStruct((SUBLANES, LANES), F32)],
        compiler_params=_SPLIT,
    )(*grads, after)
    return list(res[:n]), res[n:n + 2], res[n + 2]


def _reduce_wait(name, grads, others, sems, after):
    n = len(grads)

    def body(*refs):
        ins, os_, sem_refs = refs[:n], refs[n:2 * n], refs[2 * n:2 * n + 2]
        for i in range(n):
            _reduce_copy(ins, os_, sem_refs, i).wait_recv()
        for i in range(n):
            _reduce_copy(ins, os_, sem_refs, i).wait_send()

    return list(pl.pallas_call(
        body, name=name, in_specs=[_ANY] * (2 * n) + [_SEM, _SEM, _ANY], out_specs=[_ANY] * n,
        out_shape=[_hbm_out(o.shape, o.dtype) for o in others],
        input_output_aliases={n + i: i for i in range(n)}, compiler_params=_SPLIT,
    )(*grads, *others, *sems, after))


def _sum_cores(name, g, other, core):
    _, R, C = g.shape
    H = R // 2
    tr = _row_tile(H, C, 3 * 2, target=12 << 20)
    nb = H // tr

    def body(c_ref, g_ref, o_ref, out_ref):
        out_ref[...] = (g_ref[...].astype(F32) + o_ref[...].astype(F32)).astype(out_ref.dtype)

    return pl.pallas_call(
        body, name=name,
        grid_spec=pltpu.PrefetchScalarGridSpec(
            num_scalar_prefetch=1, grid=(N_CHIPS, nb),
            in_specs=[pl.BlockSpec((None, tr, C), lambda j, i, c_ref: (j, c_ref[0] * nb + i, 0)),
                      pl.BlockSpec((None, tr, C), lambda j, i, c_ref: (j, i, 0))],
            out_specs=pl.BlockSpec((None, tr, C), lambda j, i, c_ref: (j, i, 0))),
        out_shape=_hbm_out((N_CHIPS, H, C), BF16),
        compiler_params=_params(("parallel", "parallel")),
    )(core, g, other)


def _sum_chips(name, received, own, full, layer, me_core):
    _, H, C = received.shape
    tr = _row_tile(H, C, 3 * 2 + 2 + 4, target=12 << 20)
    nb = H // tr

    def body(s_ref, r_ref, own_ref, full_ref, out_ref):
        acc = r_ref[0].astype(F32)
        for k in (1, 2):
            acc = acc + r_ref[k].astype(F32)
        out_ref[...] = acc + own_ref[...].astype(F32)

    return pl.pallas_call(
        body, name=name,
        grid_spec=pltpu.PrefetchScalarGridSpec(
            num_scalar_prefetch=1, grid=(nb,),
            in_specs=[pl.BlockSpec((3, tr, C), lambda i, s_ref: (0, i, 0)),
                      pl.BlockSpec((None, tr, C), lambda i, s_ref: (s_ref[0], i, 0)),
                      _ANY],
            out_specs=pl.BlockSpec((None, tr, C), lambda i, s_ref: (layer, s_ref[1] * nb + i, 0))),
        out_shape=_hbm_out(full.shape, full.dtype),
        input_output_aliases={3: 0},
        compiler_params=_params(("parallel",)),
    )(me_core, received, own, full)


def _part_copy(parts, recv, sems, i, k, peers, c):
    px, py = peers[k]
    return pltpu.make_async_remote_copy(
        src_ref=parts[i].at[2 * px + py], dst_ref=recv[i].at[k],
        send_sem=sems[0].at[3 * i + k], recv_sem=sems[1].at[3 * i + k],
        device_id=(px, py, c), device_id_type=MESH)


def _scatter_start(name, parts):
    n = len(parts)

    def body(*refs):
        ins, outs, sems, token = refs[:n], refs[n:2 * n], refs[2 * n:2 * n + 2], refs[2 * n + 2]
        x, y, c = _position()
        peers = _chip_peers(x, y)
        for i in range(n):
            for k in range(3):
                _part_copy(ins, outs, sems, i, k, peers, c).start()
        token[...] = jnp.zeros_like(token)

    res = pl.pallas_call(
        body, name=name, in_specs=[_ANY] * n,
        out_specs=[_ANY] * n + [_SEM, _SEM, pl.BlockSpec(memory_space=pltpu.VMEM)],
        out_shape=[_hbm_out((3,) + p.shape[1:], p.dtype) for p in parts]
        + [pltpu.SemaphoreType.DMA((3 * n,)), pltpu.SemaphoreType.DMA((3 * n,)),
           jax.ShapeDtypeStruct((SUBLANES, LANES), F32)],
        compiler_params=_SPLIT,
    )(*parts)
    return list(res[:n]), res[n:n + 2], res[n + 2]


def _scatter_wait(name, parts, recv, sems):
    n = len(parts)

    def body(*refs):
        ins, rs, sem_refs = refs[:n], refs[n:2 * n], refs[2 * n:2 * n + 2]
        x, y, c = _position()
        peers = _chip_peers(x, y)
        for i in range(n):
            for k in range(3):
                _part_copy(ins, rs, sem_refs, i, k, peers, c).wait_recv()
        for i in range(n):
            for k in range(3):
                _part_copy(ins, rs, sem_refs, i, k, peers, c).wait_send()

    return list(pl.pallas_call(
        body, name=name, in_specs=[_ANY] * (2 * n) + [_SEM, _SEM], out_specs=[_ANY] * n,
        out_shape=[_hbm_out(r.shape, r.dtype) for r in recv],
        input_output_aliases={n + i: i for i in range(n)}, compiler_params=_SPLIT,
    )(*parts, *recv, *sems))


def _share_d2d(name, full):
    n = len(full)

    def body(*refs):
        ins, outs = refs[:n], refs[n:2 * n]
        send_sems, recv_sems = refs[2 * n:]
        x, y, c = _position()

        def remote(w, core):
            half = _half_rows(core, ins[w].shape[1])
            return pltpu.make_async_remote_copy(
                src_ref=ins[w].at[:, half], dst_ref=outs[w].at[:, half],
                send_sem=send_sems.at[w], recv_sem=recv_sems.at[w],
                device_id=(x, y, 1 - c), device_id_type=MESH)

        for w in range(n):
            remote(w, c).start()
        for w in range(n):
            remote(w, 1 - c).wait_recv()
        for w in range(n):
            remote(w, c).wait_send()

    return pl.pallas_call(
        body, name=name, in_specs=[_ANY] * n, out_specs=[_ANY] * n,
        out_shape=[_hbm_out(f.shape, f.dtype) for f in full],
        input_output_aliases={w: w for w in range(n)},
        scratch_shapes=[pltpu.SemaphoreType.DMA((n,)), pltpu.SemaphoreType.DMA((n,))],
    )(*full)


def _gather_all(name, a):
    def body(a_ref, o_ref, send_sems, recv_sems, local_sem):
        x, y, c = _position()
        me = 4 * x + 2 * y + c

        def peer(k):
            return (x ^ ((k >> 2) & 1), y ^ ((k >> 1) & 1), c ^ (k & 1))

        def remote(k, slot):
            return pltpu.make_async_remote_copy(
                src_ref=a_ref, dst_ref=o_ref.at[slot], send_sem=send_sems.at[k - 1], recv_sem=recv_sems.at[k - 1],
                device_id=peer(k), device_id_type=MESH)

        local = pltpu.make_async_copy(a_ref, o_ref.at[me], local_sem)
        local.start()
        for k in range(1, N_DEV):
            remote(k, me).start()
        for k in range(1, N_DEV):
            px, py, pc = peer(k)
            remote(k, 4 * px + 2 * py + pc).wait_recv()
        for k in range(1, N_DEV):
            remote(k, me).wait_send()
        local.wait()

    return pl.pallas_call(
        body, name=name, in_specs=[_ANY], out_specs=_ANY,
        out_shape=_hbm_out((N_DEV,) + a.shape, a.dtype),
        scratch_shapes=[pltpu.SemaphoreType.DMA((N_DEV - 1,)), pltpu.SemaphoreType.DMA((N_DEV - 1,)),
                        pltpu.SemaphoreType.DMA],
    )(a)


def _rows2d(a, lead=0):
    return a.reshape(a.shape[:lead] + (-1, a.shape[-1]))


def _row_tile(rows, cols, itemsize=4, target=1 << 20):
    want = max(SUBLANES, target // (cols * itemsize))
    t = min(rows, (want // 16) * 16)
    while t > 16 and rows % t:
        t -= 16
    return t if rows % t == 0 else rows


def _sum_slots(name, r, out_dtype=F32):
    ns = r.shape[0]
    r2 = _rows2d(r, 1)
    _, rows, cols = r2.shape
    tr = _row_tile(rows, cols)

    def body(r_ref, o_ref):
        acc = r_ref[0].astype(F32)
        for s in range(1, ns):
            acc = acc + r_ref[s].astype(F32)
        o_ref[...] = acc.astype(o_ref.dtype)

    out = pl.pallas_call(
        body, name=name, grid=(rows // tr,),
        in_specs=[pl.BlockSpec((ns, tr, cols), lambda i: (0, i, 0))],
        out_specs=pl.BlockSpec((tr, cols), lambda i: (i, 0)),
        out_shape=_hbm_out((rows, cols), out_dtype),
        compiler_params=_params(("parallel",)),
    )(r2)
    return out.reshape(r.shape[1:])


def _adamw(name, g_parts, w, m, v):
    shape = w.shape
    ng = len(g_parts)
    args = [_rows2d(a) for a in (*g_parts, w, m, v)]
    rows, cols = args[0].shape
    tr = _row_tile(rows, cols, (ng + 7) * 4, target=16 << 20)
    c1 = 1.0 - ADAM_B1 ** ADAM_STEP
    c2 = 1.0 - ADAM_B2 ** ADAM_STEP

    def body(*refs):
        g = refs[0][...]
        for r in refs[1:ng]:
            g = g + r[...]
        w_ref, m_ref, v_ref = refs[ng:ng + 3]
        g_out, d_out, m_out, v_out = refs[ng + 3:]
        mn = ADAM_B1 * m_ref[...] + (1.0 - ADAM_B1) * g
        vn = ADAM_B2 * v_ref[...] + (1.0 - ADAM_B2) * (g * g)
        m_hat = mn / c1
        v_hat = vn / c2
        g_out[...] = g
        d_out[...] = -ADAM_LR * (m_hat / (jnp.sqrt(v_hat) + ADAM_EPS) + ADAM_WD * w_ref[...])
        m_out[...] = mn
        v_out[...] = vn

    spec = pl.BlockSpec((tr, cols), lambda i: (i, 0))
    outs = pl.pallas_call(
        body, name=name, grid=(rows // tr,), in_specs=[spec] * (ng + 3), out_specs=[spec] * 4,
        out_shape=[_hbm_out((rows, cols), F32)] * 4,
        compiler_params=_params(("parallel",)),
    )(*args)
    return tuple(o.reshape(shape) for o in outs)


_WEIGHTS = ["norm_mix", "norm_ffn", "w_ffn_in", "w_ffn_out", "w_rec_in", "conv_w", "conv_b", "w_lru_gates",
            "b_lru_gates", "lru_param", "w_rec_out", "norm_kv", "w_kvf", "b_forget", "w_q", "w_o", "norm_final"]
_BIG = ["w_ffn_in", "w_ffn_out", "w_rec_in", "w_lru_gates", "w_rec_out", "w_kvf", "w_q", "w_o"]


def _stack3(a):
    return a[None] if a.ndim == 2 else a.reshape(a.shape[0], -1, a.shape[-1])


def _pad_lanes(a, n):
    return jnp.pad(a, ((0, 0),) * (a.ndim - 1) + ((0, n - a.shape[-1]),))


def kernel(x, norm_mix, norm_ffn, w_ffn_in, w_ffn_out, w_rec_in, conv_w, conv_b, w_lru_gates, b_lru_gates, lru_param, w_rec_out, norm_kv, w_kvf, b_forget, w_q, w_o, norm_final, loss_target, m_norm_mix, m_norm_ffn, m_w_ffn_in, m_w_ffn_out, m_w_rec_in, m_conv_w, m_conv_b, m_w_lru_gates, m_b_lru_gates, m_lru_param, m_w_rec_out, m_norm_kv, m_w_kvf, m_b_forget, m_w_q, m_w_o, m_norm_final, v_norm_mix, v_norm_ffn, v_w_ffn_in, v_w_ffn_out, v_w_rec_in, v_conv_w, v_conv_b, v_w_lru_gates, v_b_lru_gates, v_lru_param, v_w_rec_out, v_norm_kv, v_w_kvf, v_b_forget, v_w_q, v_w_o, v_norm_final):
    P = dict(norm_mix=norm_mix, norm_ffn=norm_ffn, w_ffn_in=w_ffn_in, w_ffn_out=w_ffn_out, w_rec_in=w_rec_in,
             conv_w=conv_w, conv_b=conv_b, w_lru_gates=w_lru_gates, b_lru_gates=b_lru_gates, lru_param=lru_param,
             w_rec_out=w_rec_out, norm_kv=norm_kv, w_kvf=w_kvf, b_forget=b_forget, w_q=w_q, w_o=w_o,
             norm_final=norm_final)
    M1 = dict(norm_mix=m_norm_mix, norm_ffn=m_norm_ffn, w_ffn_in=m_w_ffn_in, w_ffn_out=m_w_ffn_out,
              w_rec_in=m_w_rec_in, conv_w=m_conv_w, conv_b=m_conv_b, w_lru_gates=m_w_lru_gates,
              b_lru_gates=m_b_lru_gates, lru_param=m_lru_param, w_rec_out=m_w_rec_out, norm_kv=m_norm_kv,
              w_kvf=m_w_kvf, b_forget=m_b_forget, w_q=m_w_q, w_o=m_w_o, norm_final=m_norm_final)
    M2 = dict(norm_mix=v_norm_mix, norm_ffn=v_norm_ffn, w_ffn_in=v_w_ffn_in, w_ffn_out=v_w_ffn_out,
              w_rec_in=v_w_rec_in, conv_w=v_conv_w, conv_b=v_conv_b, w_lru_gates=v_w_lru_gates,
              b_lru_gates=v_b_lru_gates, lru_param=v_lru_param, w_rec_out=v_w_rec_out, norm_kv=v_norm_kv,
              w_kvf=v_w_kvf, b_forget=v_b_forget, w_q=v_w_q, w_o=v_w_o, norm_final=v_norm_final)

    _, S, D = x.shape
    L = norm_mix.shape[0]
    NA, NBLK, BW, GS = w_lru_gates.shape
    NB = w_q.shape[0]
    C = NBLK * BW
    CS = C // N_CHIPS
    H = b_forget.shape[0]
    assert C == D and H * HEAD_DIM == D and H <= LANES
    chip = 2 * lax.axis_index("x") + lax.axis_index("y")

    small_a = jnp.concatenate([conv_w, conv_b[:, None], lru_param[:, None]], axis=1)
    small_a, b_gates = _gather_smalls("gather_smalls", [small_a, b_lru_gates])
    small_a = small_a.transpose(1, 2, 0, 3).reshape(NA, 6, C)
    b_gates = b_gates.transpose(1, 2, 0, 3).reshape(NA, NBLK, 1, N_CHIPS * GS)
    shards = [_stack3(P[w]).astype(BF16) for w in _BIG]
    core = lax.axis_index("c")
    chip_id = jnp.reshape(chip, (1,)).astype(jnp.int32)
    core_id = jnp.reshape(core, (1,)).astype(jnp.int32)
    me_core = jnp.stack([chip, core]).astype(jnp.int32)

    def stage_items(l, part):
        if part == "ffn":
            return [(_BIG.index("w_ffn_in"), l), (_BIG.index("w_ffn_out"), l)]
        if l < NA:
            names, at = ["w_rec_in", "w_lru_gates", "w_rec_out"], l
        else:
            names, at = (["w_kvf"] if l == NA else []) + ["w_q", "w_o"], l - NA
        return [(_BIG.index(n), 0 if n == "w_kvf" else at) for n in names]

    stages = [(l, part) for l in range(L) for part in ("mix", "ffn")]
    items = [it for st in stages for it in stage_items(*st)]
    ids_of = {st: [items.index(it) for it in stage_items(*st)] for st in stages}
    bufs = [_place_own(f"place_{_BIG[w]}_{li}", shards[w], li, chip_id) for w, li in items]
    bufs, gather_sems = _gather_start("gather_start", shards, bufs, items, small_a)

    forwarding = {}

    def layer_prefetch(l, part, after):
        if l < L and (l, part) not in forwarding:
            ids = ids_of[(l, part)]
            got = _gather_wait(f"gather_wait_{part}_{l}", shards, bufs, items, ids, gather_sems, after)
            forwarding[(l, part)] = _forward_start(f"forward_start_{part}_{l}", got)

    def layer_weights(l, part, after):
        if l >= L:
            return None
        layer_prefetch(l, part, after)
        ids = ids_of[(l, part)]
        got, sems = forwarding[(l, part)]
        got = _forward_wait(f"forward_wait_{part}_{l}", got, sems, after)
        B = {_BIG[items[i][0]]: g for i, g in zip(ids, got)}
        if part == "ffn":
            return dict(w_ffn_in=B["w_ffn_in"], w_ffn_out=B["w_ffn_out"].reshape(-1, D))
        W = {}
        if l < NA:
            W.update(w_rec_in=B["w_rec_in"],
                     w_gates=B["w_lru_gates"].reshape(N_CHIPS, NBLK, BW, GS).transpose(1, 2, 0, 3).reshape(
                         NBLK, BW, N_CHIPS * GS),
                     b_gates=b_gates[l], w_rec_out=B["w_rec_out"].reshape(C, D),
                     conv_w=small_a[l, :4], conv_b=small_a[l, 4:5], lru_param=small_a[l, 5:6])
        else:
            W.update(w_q=B["w_q"].reshape(D, D), w_o=B["w_o"].reshape(D, D))
            if l == NA:
                w_kvf_full = B["w_kvf"].transpose(1, 0, 2).reshape(D, -1)
                W.update(norm_kv=norm_kv[None], w_kv=w_kvf_full[:, :2 * D],
                         w_f=_pad_lanes(w_kvf_full[:, 2 * D:], LANES), b_f=_pad_lanes(b_forget[None], LANES))
        return W

    G_small = {l: {} for l in range(L)}
    pending = {}
    reducing = []

    def finish_reduce(after):
        l, part, its, grads, others, sems = reducing.pop()
        others = _reduce_wait(f"reduce_wait_{part}_{l}", grads, others, sems, after)
        parts = [_sum_cores(f"sum_cores_{l}_{_BIG[w]}", g, o, core_id) for (w, _), g, o in zip(its, grads, others)]
        recv, sems, token = _scatter_start(f"scatter_start_{part}_{l}", parts)
        pending[(l, part)] = (parts, recv, sems)
        return token

    def layer_grads(l, part, G):
        G_small[l].update(G)
        after = finish_reduce(G["w_ffn_in" if part == "ffn" else "norm_mix"]) if reducing else jnp.zeros(
            (SUBLANES, LANES), F32)
        by_name = dict(
            w_ffn_in=lambda: G["w_ffn_in"], w_ffn_out=lambda: G["w_ffn_out"].reshape(N_CHIPS, -1, D),
            w_rec_in=lambda: G["w_rec_in"],
            w_lru_gates=lambda: G["w_gates"].reshape(NBLK, BW, N_CHIPS, GS).transpose(2, 0, 1, 3).reshape(
                N_CHIPS, NBLK * BW, GS),
            w_rec_out=lambda: G["w_rec_out"].reshape(N_CHIPS, -1, D),
            w_kvf=lambda: jnp.concatenate([G["w_kv"].astype(F32), G["w_f"][:, :H]], axis=1).reshape(
                D, N_CHIPS, -1).transpose(1, 0, 2).astype(BF16),
            w_q=lambda: G["w_q"].reshape(N_CHIPS, -1, D), w_o=lambda: G["w_o"].reshape(N_CHIPS, -1, D))
        its = stage_items(l, part)
        grads = [by_name[_BIG[w]]() for w, _ in its]
        others, sems, token = _reduce_start(f"reduce_start_{part}_{l}", grads, after)
        reducing.append((l, part, its, grads, others, sems))
        return finish_reduce(token) if l == 0 else token

    gains = dict(mix=[norm_mix[l][None] for l in range(L)], ffn=[norm_ffn[l][None] for l in range(L)],
                 final=norm_final[None])
    loss_row, grad_x, dg_final = _local_step(x.reshape(S, D), loss_target.reshape(S, D), gains,
                                             layer_weights, layer_prefetch, layer_grads)

    rows = [*[G_small[l]["norm_mix"] for l in range(L)], *[G_small[l]["norm_ffn"] for l in range(L)],
            G_small[NA]["norm_kv"], dg_final, _pad_lanes(G_small[NA]["b_f"], D), _pad_lanes(loss_row, D)]
    for a in range(NA):
        rows += [G_small[a][n] for n in ("conv_w", "conv_b", "b_gi", "b_gr", "lru_param")]
    packed = jnp.concatenate(rows, axis=0)
    tot = _sum_slots("sum_small", _gather_all("gather_small", packed))
    loss = tot[2 * L + 3, 0]
    g_rep = jnp.concatenate([tot[:2 * L + 2], tot[2 * L + 2:2 * L + 3]], axis=0)
    base = 2 * L + 4
    g_sh = []
    for a in range(NA):
        blk = lax.dynamic_slice_in_dim(tot[base + 8 * a:base + 8 * a + 8], chip * CS, CS, axis=1)
        gi = tot[base + 8 * a + 5].reshape(NBLK, BW)
        gr = tot[base + 8 * a + 6].reshape(NBLK, BW)
        bl = lax.dynamic_slice_in_dim(jnp.concatenate([gi, gr], axis=1), chip * GS, GS, axis=1)
        g_sh += [blk[:5], bl.reshape(-1, CS), blk[7:8]]
    g_sh = jnp.concatenate(g_sh, axis=0)
    nrow = g_sh.shape[0] // NA

    def pack_rep(T):
        return jnp.concatenate([T["norm_mix"], T["norm_ffn"], T["norm_kv"][None], T["norm_final"][None],
                                _pad_lanes(T["b_forget"][None], D)], axis=0)

    def pack_sh(T):
        return jnp.concatenate([jnp.concatenate([T["conv_w"][a], T["conv_b"][a][None],
                                                 T["b_lru_gates"][a].reshape(-1, CS), T["lru_param"][a][None]], axis=0)
                                for a in range(NA)], axis=0)

    rep = _adamw("adamw_replicated", [g_rep], pack_rep(P), pack_rep(M1), pack_rep(M2))
    shd = _adamw("adamw_small_sharded", [g_sh], pack_sh(P), pack_sh(M1), pack_sh(M2))

    def unpack_rep(t):
        return dict(norm_mix=t[:L], norm_ffn=t[L:2 * L], norm_kv=t[2 * L], norm_final=t[2 * L + 1],
                    b_forget=t[2 * L + 2, :H])

    def unpack_sh(t):
        t = t.reshape(NA, nrow, CS)
        return dict(conv_w=t[:, :4], conv_b=t[:, 4], b_lru_gates=t[:, 5:nrow - 1].reshape(NA, NBLK, GS),
                    lru_param=t[:, nrow - 1])

    full = [lax.empty(sh.shape, F32) for sh in shards]
    for l, part in reversed(stages):
        parts, recv, sems = pending[(l, part)]
        recv = _scatter_wait(f"scatter_wait_{part}_{l}", parts, recv, sems)
        for (w, li), own, r in zip(stage_items(l, part), parts, recv):
            full[w] = _sum_chips(f"sum_chips_{l}_{_BIG[w]}", r, own, full[w], li, me_core)
    full = _share_d2d("share_d2d", full)
    big = {w: _adamw(f"adamw_{w}", [g.reshape(P[w].shape)], P[w], M1[w], M2[w]) for w, g in zip(_BIG, full)}

    outs = []
    for i in range(4):
        small = {**unpack_rep(rep[i]), **unpack_sh(shd[i])}
        outs.append([big[w][i] if w in big else small[w] for w in _WEIGHTS])
    return (loss, grad_x.reshape(1, S, D), *outs[0], *outs[1], *outs[2], *outs[3])
```

```python
import functools
import math

import jax
import jax.numpy as jnp
from jax import lax
from jax.experimental import pallas as pl
from jax.experimental.pallas import tpu as pltpu

F32 = jnp.float32
BF16 = jnp.bfloat16

EPS = 1e-6
LRU_C = 8.0
HEAD_DIM = 64
LANES = 128
SUBLANES = 8
VMEM_SMALL = 32 * 1024 * 1024
VMEM_LARGE = 48 * 1024 * 1024
N_CHIPS = 4
N_DEV = 8

ADAM_LR = 0.001
ADAM_B1 = 0.9
ADAM_B2 = 0.999
ADAM_EPS = 1e-08
ADAM_WD = 0.01
ADAM_STEP = 10

_NN = (((1,), (0,)), ((), ()))
_NT = (((1,), (1,)), ((), ()))
_TN = (((0,), (0,)), ((), ()))
_DN = {"nn": _NN, "nt": _NT, "tn": _TN}
MESH = pl.DeviceIdType.MESH


def _hbm_out(shape, dtype):
    return pltpu.HBM(shape, dtype)


def _params(sem, vmem=VMEM_SMALL):
    return pltpu.CompilerParams(dimension_semantics=sem, vmem_limit_bytes=vmem)


def _block_bytes(spec, dtype):
    return math.prod(d for d in spec.block_shape if d is not None) * jnp.dtype(dtype).itemsize


def _tile(n, want):
    if n <= want:
        return n
    t = (want // LANES) * LANES
    while t >= LANES:
        if n % t == 0:
            return t
        t -= LANES
    return n


def _sigmoid(x):
    return 1.0 / (1.0 + jnp.exp(-x))


def _sigmoid_t(x):
    return 0.5 * jnp.tanh(0.5 * x) + 0.5


def _softplus(x):
    return jnp.maximum(x, 0.0) + jnp.log(1.0 + jnp.exp(-jnp.abs(x)))


_GELU_C = math.sqrt(2.0 / math.pi)


def _gelu_and_grad(x):
    inner = _GELU_C * (x + 0.044715 * x * x * x)
    t = jnp.tanh(inner)
    g = 0.5 * x * (1.0 + t)
    dg = 0.5 * (1.0 + t) + 0.5 * x * (1.0 - t * t) * _GELU_C * (1.0 + 3.0 * 0.044715 * x * x)
    return g, dg


def _rms(x):
    return lax.rsqrt(jnp.mean(x * x, axis=-1, keepdims=True) + EPS)


def _rms_bwd(dy, x, g):
    r = _rms(x)
    xr = x * r
    dyg = dy * g
    return r * dyg - xr * (r * jnp.mean(dyg * xr, axis=-1, keepdims=True)), jnp.sum(dy * xr, axis=0, keepdims=True)


def _mm(name, mode, a, b, *, grid, a_spec, b_spec, out_shape, out_dtype, out_spec, nk=1,
        res=None, res_spec=None, bias=None, bias_spec=None, scale=None, norm_gain=None, norm_bwd=None):
    dn = _DN[mode]
    has_res, has_bias = res is not None, bias is not None
    blk = tuple(d for d in out_spec.block_shape if d is not None)
    vec = pl.BlockSpec((1, blk[-1]), lambda *g: (0, 0))
    a_specs = a_spec if isinstance(a_spec, list) else [a_spec]
    b_specs = b_spec if isinstance(b_spec, list) else [b_spec]
    npair = len(a_specs)
    n_in = 2 * npair + int(has_res) + int(has_bias) + (1 if norm_gain is not None else 0) + (3 if norm_bwd else 0)

    def body(*refs):
        p = 2 * npair
        r_ref = refs[p] if has_res else None
        p += int(has_res)
        bias_ref = refs[p] if has_bias else None
        p += int(has_bias)
        extra = refs[p:n_in]
        outs = refs[n_in:]
        o_ref = outs[0]
        part = lax.dot_general(refs[0][...], refs[npair][...], dn, preferred_element_type=F32)
        for t in range(1, npair):
            part = part + lax.dot_general(refs[t][...], refs[npair + t][...], dn, preferred_element_type=F32)

        def finish(acc):
            if scale is not None:
                acc = acc * scale
            if has_bias:
                acc = acc + bias_ref[...]
            if has_res:
                acc = r_ref[...] + acc
            if norm_bwd:
                h_ref, g_ref, dh_ref = extra
                dx, dg = _rms_bwd(acc, h_ref[...], g_ref[...])
                acc = dh_ref[...] + dx
                outs[1][...] = acc.astype(BF16)
                outs[2][...] = dg
            if norm_gain is not None:
                outs[1][...] = (acc * _rms(acc) * extra[0][...]).astype(BF16)
            o_ref[...] = acc.astype(o_ref.dtype)

        if nk == 1:
            finish(part)
        else:
            acc_ref = refs[-1]
            k = pl.program_id(2)

            @pl.when(k == 0)
            def _():
                acc_ref[...] = part

            @pl.when(k > 0)
            def _():
                acc_ref[...] += part

            @pl.when(k == nk - 1)
            def _():
                finish(acc_ref[...])

    ins, specs = [a] * npair + [b] * npair, a_specs + b_specs
    if has_res:
        ins.append(res)
        specs.append(res_spec)
    if has_bias:
        ins.append(bias)
        specs.append(bias_spec)
    out_specs, out_shapes = [out_spec], [_hbm_out(out_shape, out_dtype)]
    if norm_gain is not None:
        ins.append(norm_gain)
        specs.append(vec)
        out_specs.append(out_spec)
        out_shapes.append(_hbm_out(out_shape, BF16))
    if norm_bwd:
        h, g, dh = norm_bwd
        ins += [h, g, dh]
        specs += [out_spec, vec, out_spec]
        out_specs += [out_spec, pl.BlockSpec((None, 1, blk[-1]), lambda i, *rest: (i, 0, 0))]
        out_shapes += [_hbm_out(out_shape, BF16), _hbm_out((grid[0], 1, blk[-1]), F32)]
    sem = ("parallel", "parallel") + (("arbitrary",) if len(grid) == 3 else ())
    single = len(out_specs) == 1
    need = 2 * sum(_block_bytes(sp, x.dtype) for sp, x in zip(specs + out_specs, ins + out_shapes))
    need += (2 + npair) * math.prod(blk) * 4
    return pl.pallas_call(
        body, name=name, grid=grid, in_specs=specs, out_specs=out_specs[0] if single else out_specs,
        out_shape=out_shapes[0] if single else out_shapes,
        scratch_shapes=[pltpu.VMEM(blk, F32)] if nk > 1 else [],
        compiler_params=_params(sem, VMEM_LARGE if need > VMEM_SMALL * 3 // 4 else VMEM_SMALL),
    )(*ins)


def _mm_nn(name, a, b, *, b_lead=(), out_dtype, tm=512, tn=512, res=None, bias=None, scale=None, norm_gain=None):
    M, K = a.shape
    N = b.shape[-1]
    tm, tn = _tile(M, tm), _tile(N, tn)
    nl = len(b_lead)
    return _mm(
        name, "nn", a, b, grid=(M // tm, N // tn),
        a_spec=pl.BlockSpec((tm, K), lambda i, j: (i, 0)),
        b_spec=pl.BlockSpec((None,) * nl + (K, tn), lambda i, j: tuple(b_lead) + (0, j)),
        out_shape=(M, N), out_dtype=out_dtype, out_spec=pl.BlockSpec((tm, tn), lambda i, j: (i, j)),
        res=res, res_spec=pl.BlockSpec((tm, tn), lambda i, j: (i, j)),
        bias=bias, bias_spec=pl.BlockSpec((1, tn), lambda i, j: (0, j)), scale=scale, norm_gain=norm_gain)


def _mm_nt(name, a, b, *, b_lead=(), out_dtype, tm=512, tn=512, tk=2048, res=None, norm_bwd=None):
    M, K = a.shape
    N = b.shape[-2]
    tm, tn, tk = _tile(M, tm), _tile(N, tn), _tile(K, tk)
    nk = K // tk
    nl = len(b_lead)
    return _mm(
        name, "nt", a, b, grid=(M // tm, N // tn, nk), nk=nk,
        a_spec=pl.BlockSpec((tm, tk), lambda i, j, k: (i, k)),
        b_spec=pl.BlockSpec((None,) * nl + (tn, tk), lambda i, j, k: tuple(b_lead) + (j, k)),
        out_shape=(M, N), out_dtype=out_dtype, out_spec=pl.BlockSpec((tm, tn), lambda i, j, k: (i, j)),
        res=res, res_spec=pl.BlockSpec((tm, tn), lambda i, j, k: (i, j)), norm_bwd=norm_bwd)


def _mm_tn(name, a, b, *, out_dtype, tm=512, tn=512):
    S, M = a.shape
    N = b.shape[1]
    tm, tn = _tile(M, tm), _tile(N, tn)
    return _mm(
        name, "tn", a, b, grid=(M // tm, N // tn),
        a_spec=pl.BlockSpec((S, tm), lambda i, j: (0, i)),
        b_spec=pl.BlockSpec((S, tn), lambda i, j: (0, j)),
        out_shape=(M, N), out_dtype=out_dtype, out_spec=pl.BlockSpec((tm, tn), lambda i, j: (i, j)))


def _rmsnorm_fwd(name, h, g, tr=256):
    S, D = h.shape
    tr = _tile(S, tr)

    def body(h_ref, g_ref, o_ref):
        x = h_ref[...]
        r = lax.rsqrt(jnp.mean(x * x, axis=-1, keepdims=True) + EPS)
        o_ref[...] = (x * r * g_ref[...]).astype(o_ref.dtype)

    return pl.pallas_call(
        body, name=name, grid=(S // tr,),
        in_specs=[pl.BlockSpec((tr, D), lambda i: (i, 0)), pl.BlockSpec((1, D), lambda i: (0, 0))],
        out_specs=pl.BlockSpec((tr, D), lambda i: (i, 0)),
        out_shape=_hbm_out((S, D), BF16),
        compiler_params=_params(("parallel",)),
    )(h, g)


def _loss_head(name, h, target, g, tr=256):
    S, D = h.shape
    tr = _tile(S, tr)

    def body(h_ref, t_ref, g_ref, o_ref, ob_ref, dg_ref, loss_ref):
        i = pl.program_id(0)
        x = h_ref[...]
        gg = g_ref[...]
        r = lax.rsqrt(jnp.mean(x * x, axis=-1, keepdims=True) + EPS)
        xr = x * r
        err = xr * gg - t_ref[...]
        lpart = 0.5 * jnp.sum(jnp.mean(err * err, axis=-1, keepdims=True), axis=0, keepdims=True)
        dy = err * (1.0 / D)
        dyg = dy * gg
        dx = r * dyg - xr * (r * jnp.mean(dyg * xr, axis=-1, keepdims=True))
        o_ref[...] = dx
        ob_ref[...] = dx.astype(BF16)
        part = jnp.sum(dy * xr, axis=0, keepdims=True)
        lrow = jnp.broadcast_to(lpart, (1, LANES))

        @pl.when(i == 0)
        def _():
            dg_ref[...] = part
            loss_ref[...] = lrow

        @pl.when(i > 0)
        def _():
            dg_ref[...] += part
            loss_ref[...] += lrow

    row = pl.BlockSpec((tr, D), lambda i: (i, 0))
    vec = pl.BlockSpec((1, D), lambda i: (0, 0))
    return pl.pallas_call(
        body, name=name, grid=(S // tr,),
        in_specs=[row, row, vec], out_specs=[row, row, vec, pl.BlockSpec((1, LANES), lambda i: (0, 0))],
        out_shape=[_hbm_out((S, D), F32), _hbm_out((S, D), BF16),
                   _hbm_out((1, D), F32), _hbm_out((1, LANES), F32)],
        compiler_params=_params(("arbitrary",)),
    )(h, target, g)


def _swiglu_fwd(name, hn, w_in, tm=512):
    S, D = hn.shape
    FH = w_in.shape[-1]
    tm = _tile(S, tm)

    def body(x_ref, wg_ref, wu_ref, z_ref, a_ref):
        x = x_ref[...]
        zg = jnp.dot(x, wg_ref[...], preferred_element_type=F32)
        zu = jnp.dot(x, wu_ref[...], preferred_element_type=F32)
        z_ref[0] = zg.astype(z_ref.dtype)
        z_ref[1] = zu.astype(z_ref.dtype)
        a_ref[...] = (zg * _sigmoid_t(zg) * zu).astype(a_ref.dtype)

    return pl.pallas_call(
        body, name=name, grid=(S // tm, 2),
        in_specs=[pl.BlockSpec((tm, D), lambda i, j: (i, 0)),
                  pl.BlockSpec((None, D, FH), lambda i, j: (j, 0, 0)),
                  pl.BlockSpec((None, D, FH), lambda i, j: (j + 2, 0, 0))],
        out_specs=[pl.BlockSpec((2, tm, FH), lambda i, j: (0, i, j)), pl.BlockSpec((tm, FH), lambda i, j: (i, j))],
        out_shape=[_hbm_out((2, S, 2 * FH), BF16), _hbm_out((S, 2 * FH), BF16)],
        compiler_params=_params(("parallel", "parallel"), VMEM_LARGE),
    )(hn, w_in, w_in)


def _swiglu_bwd(name, dhb, w_out, z3, tm=512):
    S, D = dhb.shape
    F = w_out.shape[0]
    FH = F // 2
    tm = _tile(S, tm)

    def body(d_ref, w_ref, z_ref, dz_ref):
        d = lax.dot_general(d_ref[...], w_ref[...], _NT, preferred_element_type=F32)
        zg = z_ref[0].astype(F32)
        zu = z_ref[1].astype(F32)
        sg = _sigmoid_t(zg)
        dz_ref[0] = (d * zu * (sg * (1.0 + zg * (1.0 - sg)))).astype(dz_ref.dtype)
        dz_ref[1] = (d * (zg * sg)).astype(dz_ref.dtype)

    zspec = pl.BlockSpec((2, tm, FH), lambda i, j: (0, i, j))
    return pl.pallas_call(
        body, name=name, grid=(S // tm, 2),
        in_specs=[pl.BlockSpec((tm, D), lambda i, j: (i, 0)), pl.BlockSpec((FH, D), lambda i, j: (j, 0)), zspec],
        out_specs=zspec, out_shape=_hbm_out((2, S, F), BF16),
        compiler_params=_params(("parallel", "parallel"), VMEM_LARGE),
    )(dhb, w_out, z3)


SCAN_ROWS = 64


def _group_scan(A, B, reverse):
    n = A.shape[0]
    sub = lax.broadcasted_iota(jnp.int32, A.shape, 0) % SUBLANES
    for d in (1, 2, 4):
        if reverse:
            A_sh, B_sh = pltpu.roll(A, n - d, 0), pltpu.roll(B, n - d, 0)
            keep = sub < SUBLANES - d
        else:
            A_sh, B_sh = pltpu.roll(A, d, 0), pltpu.roll(B, d, 0)
            keep = sub >= d
        B = jnp.where(keep, A * B_sh + B, B)
        A = jnp.where(keep, A * A_sh, A)
    return A, B


def _block_scan(a, u, carry, reverse):
    A, B = _group_scan(a, u, reverse)
    ng = a.shape[0] // SUBLANES
    out = [None] * ng
    order = range(ng - 1, -1, -1) if reverse else range(ng)
    for gi in order:
        sl = slice(gi * SUBLANES, (gi + 1) * SUBLANES)
        hg = A[sl] * carry + B[sl]
        out[gi] = hg
        carry = hg[0:1] if reverse else hg[SUBLANES - 1:SUBLANES]
    return jnp.concatenate(out, axis=0), carry


def _lru_gates(rc, gip, grp, sp):
    gi = _sigmoid_t(gip)
    gr = _sigmoid_t(grp)
    la = -LRU_C * gr * sp
    a = jnp.exp(la)
    om = -jnp.tanh(la) * (a * a + 1.0)
    mult = jnp.sqrt(om)
    return gi, gr, a, mult


def _lru_fwd(name, proj, rc, gip, grp, lru_p, tc=256):
    S, C = rc.shape
    tc = _tile(C, tc)
    nb = S // SCAN_ROWS

    def body(gb_ref, rc_ref, gi_ref, gr_ref, l_ref, h_ref, m_ref):
        sp = _softplus(-l_ref[...])

        def step(b, carry):
            rows = pl.ds(pl.multiple_of(b * SCAN_ROWS, SCAN_ROWS), SCAN_ROWS)
            rcb = rc_ref[rows, :]
            gi, _, a, mult = _lru_gates(rcb, gi_ref[rows, :], gr_ref[rows, :], sp)
            h, carry = _block_scan(a, rcb * gi * mult, carry, False)
            h_ref[rows, :] = h
            gel, _ = _gelu_and_grad(gb_ref[rows, :])
            m_ref[rows, :] = (gel * h).astype(m_ref.dtype)
            return carry

        lax.fori_loop(0, nb, step, jnp.zeros((1, tc), F32))

    col = pl.BlockSpec((S, tc), lambda j: (0, j))
    return pl.pallas_call(
        body, name=name, grid=(C // tc,),
        in_specs=[col, col, col, col, pl.BlockSpec((1, tc), lambda j: (0, j))],
        out_specs=[col, col],
        out_shape=[_hbm_out((S, C), F32), _hbm_out((S, C), BF16)],
        compiler_params=_params(("parallel",), VMEM_LARGE),
    )(proj, rc, gip, grp, lru_p)


def _lru_bwd(name, dm, proj, hrec, rc, gip, grp, lru_p, tc=256):
    S, C = rc.shape
    tc = _tile(C, tc)
    nb = S // SCAN_ROWS
    R = SCAN_ROWS

    def body(dm_ref, gb_ref, h_ref, rc_ref, gi_ref, gr_ref, l_ref,
             dgb_ref, dgi_ref, dgr_ref, drc_ref, dbi_ref, dbr_ref, dl_ref):
        lp = l_ref[...]
        sp = _softplus(-lp)
        row = lax.broadcasted_iota(jnp.int32, (R, tc), 0)
        zero = jnp.zeros((1, tc), F32)

        def step(t, carry):
            mu_in, s_i, s_r, s_sp = carry
            b = nb - 1 - t
            r0 = pl.multiple_of(b * R, R)
            rows = pl.ds(r0, R)
            rcb = rc_ref[rows, :]
            gi, gr, a, mult = _lru_gates(rcb, gi_ref[rows, :], gr_ref[rows, :], sp)
            gel, dgel = _gelu_and_grad(gb_ref[rows, :])
            dmb = dm_ref[rows, :]
            h = h_ref[rows, :]
            dgb_ref[rows, :] = (dmb * h * dgel).astype(dgb_ref.dtype)
            dh = dmb * gel
            mu, mu_out = _block_scan(a, a * dh, mu_in, True)
            mu_next = jnp.where(row == R - 1, mu_in, pltpu.roll(mu, R - 1, 0))
            lam = dh + mu_next
            p0 = pl.multiple_of(jnp.maximum(r0 - SUBLANES, 0), SUBLANES)
            prev = h_ref[pl.ds(p0, SUBLANES), :][SUBLANES - 1:SUBLANES]
            prev = jnp.where(b > 0, prev, 0.0)
            h_prev = jnp.where(row == 0, prev, pltpu.roll(h, 1, 0))
            da = lam * h_prev
            d_mult = lam * rcb * gi
            d_la = da * a - d_mult * (a * a) / mult
            d_grp = d_la * (-LRU_C * sp) * gr * (1.0 - gr)
            d_gip = lam * rcb * mult * gi * (1.0 - gi)
            dgr_ref[rows, :] = d_grp.astype(dgr_ref.dtype)
            dgi_ref[rows, :] = d_gip.astype(dgi_ref.dtype)
            drc_ref[rows, :] = lam * gi * mult
            s_i = s_i + jnp.sum(d_gip, axis=0, keepdims=True)
            s_r = s_r + jnp.sum(d_grp, axis=0, keepdims=True)
            s_sp = s_sp + jnp.sum(d_la * gr, axis=0, keepdims=True)
            return mu_out, s_i, s_r, s_sp

        _, s_i, s_r, s_sp = lax.fori_loop(0, nb, step, (zero, zero, zero, zero))
        dbi_ref[...] = s_i
        dbr_ref[...] = s_r
        dl_ref[...] = (-LRU_C * s_sp) * (-_sigmoid(-lp))

    col = pl.BlockSpec((S, tc), lambda j: (0, j))
    vec = pl.BlockSpec((1, tc), lambda j: (0, j))
    return pl.pallas_call(
        body, name=name, grid=(C // tc,),
        in_specs=[col, col, col, col, col, col, vec],
        out_specs=[col, col, col, col, vec, vec, vec],
        out_shape=[_hbm_out((S, C), BF16), _hbm_out((S, C), BF16),
                   _hbm_out((S, C), BF16), _hbm_out((S, C), F32),
                   _hbm_out((1, C), F32), _hbm_out((1, C), F32),
                   _hbm_out((1, C), F32)],
        compiler_params=_params(("parallel",), VMEM_LARGE),
    )(dm, proj, hrec, rc, gip, grp, lru_p)


def _cumsum_rows(name, u, reverse):
    S, C = u.shape
    nb = S // SCAN_ROWS

    def body(u_ref, o_ref):
        def step(t, carry):
            b = nb - 1 - t if reverse else t
            rows = pl.ds(pl.multiple_of(b * SCAN_ROWS, SCAN_ROWS), SCAN_ROWS)
            ub = u_ref[rows, :]
            h, carry = _block_scan(jnp.ones_like(ub), ub, carry, reverse)
            o_ref[rows, :] = h
            return carry

        lax.fori_loop(0, nb, step, jnp.zeros((1, C), F32))

    spec = pl.BlockSpec((S, C), lambda i: (0, 0))
    return pl.pallas_call(
        body, name=name, grid=(1,), in_specs=[spec], out_specs=spec,
        out_shape=_hbm_out((S, C), F32),
        compiler_params=_params(("arbitrary",)),
    )(u)


def _shift_down(x, k):
    row = lax.broadcasted_iota(jnp.int32, x.shape, 0)
    return jnp.where(row >= k, pltpu.roll(x, k, 0), 0.0)


def _shift_up(x, k):
    n = x.shape[0]
    row = lax.broadcasted_iota(jnp.int32, x.shape, 0)
    return jnp.where(row < n - k, pltpu.roll(x, n - k, 0), 0.0)


def _conv_fwd(name, proj, w, b, tc=256):
    S, C2 = proj.shape
    C = C2 // 2
    tc = _tile(C, tc)
    off = C // tc

    def body(x_ref, w_ref, b_ref, o_ref, ob_ref):
        x = x_ref[...]
        out = b_ref[...] + w_ref[3:4, :] * x
        for k in (1, 2, 3):
            out = out + w_ref[3 - k:4 - k, :] * _shift_down(x, k)
        o_ref[...] = out
        ob_ref[...] = out.astype(BF16)

    col = pl.BlockSpec((S, tc), lambda j: (0, j))
    return pl.pallas_call(
        body, name=name, grid=(C // tc,),
        in_specs=[pl.BlockSpec((S, tc), lambda j: (0, off + j)),
                  pl.BlockSpec((4, tc), lambda j: (0, j)), pl.BlockSpec((1, tc), lambda j: (0, j))],
        out_specs=[col, col],
        out_shape=[_hbm_out((S, C), F32), _hbm_out((S, C), BF16)],
        compiler_params=_params(("parallel",)),
    )(proj, w, b)


def _conv_bwd(name, drc, proj, w, tc=256):
    S, C = drc.shape
    tc = _tile(C, tc)
    off = C // tc

    def body(y_ref, x_ref, w_ref, dx_ref, dw_ref, db_ref):
        y = y_ref[...]
        x = x_ref[...]
        dx = w_ref[3:4, :] * y
        dw_ref[3:4, :] = jnp.sum(y * x, axis=0, keepdims=True)
        for k in (1, 2, 3):
            dx = dx + w_ref[3 - k:4 - k, :] * _shift_up(y, k)
            dw_ref[3 - k:4 - k, :] = jnp.sum(y * _shift_down(x, k), axis=0, keepdims=True)
        dx_ref[...] = dx.astype(dx_ref.dtype)
        db_ref[...] = jnp.sum(y, axis=0, keepdims=True)

    col = pl.BlockSpec((S, tc), lambda j: (0, j))
    return pl.pallas_call(
        body, name=name, grid=(C // tc,),
        in_specs=[col, pl.BlockSpec((S, tc), lambda j: (0, off + j)), pl.BlockSpec((4, tc), lambda j: (0, j))],
        out_specs=[col, pl.BlockSpec((4, tc), lambda j: (0, j)), pl.BlockSpec((1, tc), lambda j: (0, j))],
        out_shape=[_hbm_out((S, C), BF16), _hbm_out((4, C), F32),
                   _hbm_out((1, C), F32)],
        compiler_params=_params(("parallel",)),
    )(drc, proj, w)


def _gates_fwd(name, rcb, wg, bg):
    S, C = rcb.shape
    nblk, bw, _ = wg.shape

    def body(x_ref, w_ref, b_ref, gi_ref, gr_ref):
        g = jnp.dot(x_ref[...], w_ref[...], preferred_element_type=F32) + b_ref[...]
        gi_ref[...] = g[:, :bw]
        gr_ref[...] = g[:, bw:]

    col = pl.BlockSpec((S, bw), lambda n: (0, n))
    return pl.pallas_call(
        body, name=name, grid=(nblk,),
        in_specs=[col, pl.BlockSpec((None, bw, 2 * bw), lambda n: (n, 0, 0)),
                  pl.BlockSpec((None, 1, 2 * bw), lambda n: (n, 0, 0))],
        out_specs=[col, col],
        out_shape=[_hbm_out((S, C), F32), _hbm_out((S, C), F32)],
        compiler_params=_params(("parallel",)),
    )(rcb, wg, bg)


def _gates_bwd(name, dgi, dgr, rcb, wg, drc1):
    S, C = rcb.shape
    nblk, bw, _ = wg.shape

    def body(dgi_ref, dgr_ref, x_ref, w_ref, d1_ref, drc_ref, dw_ref):
        w = w_ref[...]
        x = x_ref[...]
        di, dr = dgi_ref[...], dgr_ref[...]
        drc_ref[...] = (d1_ref[...]
                        + lax.dot_general(di, w[:, :bw], _NT, preferred_element_type=F32)
                        + lax.dot_general(dr, w[:, bw:], _NT, preferred_element_type=F32))
        dw_ref[:, :bw] = lax.dot_general(x, di, _TN, preferred_element_type=F32).astype(dw_ref.dtype)
        dw_ref[:, bw:] = lax.dot_general(x, dr, _TN, preferred_element_type=F32).astype(dw_ref.dtype)

    col = pl.BlockSpec((S, bw), lambda n: (0, n))
    wspec = pl.BlockSpec((None, bw, 2 * bw), lambda n: (n, 0, 0))
    return pl.pallas_call(
        body, name=name, grid=(nblk,),
        in_specs=[col, col, col, wspec, col], out_specs=[col, wspec],
        out_shape=[_hbm_out((S, C), F32), _hbm_out((nblk, bw, 2 * bw), BF16)],
        compiler_params=_params(("parallel",)),
    )(dgi, dgr, rcb, wg, drc1)


def _att_tile(S):
    return next(t for t in (512, 256, 128) if S % t == 0)


def _head_lanes(shape):
    return lax.broadcasted_iota(jnp.int32, shape, len(shape) - 1) < HEAD_DIM


def _key_bias(c_blk):
    first = _head_lanes(c_blk.shape)
    rolled = pltpu.roll(c_blk, HEAD_DIM, 1)
    return jnp.where(first, c_blk, rolled), jnp.where(first, rolled, c_blk)


def _over_keys(x, op):
    n = x.shape[0]
    while n > SUBLANES:
        n //= 2
        x = op(x[:n], x[n:2 * n])
    return (jnp.max if op is jnp.maximum else jnp.sum)(x, axis=0, keepdims=True)


def _causal_t(T, cc):
    r = lax.broadcasted_iota(jnp.int32, (T, LANES), 0)
    c = lax.broadcasted_iota(jnp.int32, (T, LANES), 1) + cc * LANES
    return r <= c


def _attn_fwd(name, q, kv, cfull):
    S, D = q.shape
    HP = D // LANES
    T = _att_tile(S)
    nq = S // T
    NC = T // LANES

    def body(q_ref, k_ref, v_ref, c_ref, o_ref, of_ref, lse_ref, bias, vT, acc, m_scr, l_scr):
        def prologue(i, _):
            rows = pl.ds(pl.multiple_of(i * T, T), T)
            bias[0, rows, :], bias[1, rows, :] = _key_bias(c_ref[rows, :])
            vT[i] = v_ref[rows, :].astype(F32).T.astype(BF16)
            return 0

        lax.fori_loop(0, nq, prologue, 0)

        def q_step(qi, _):
            q0 = pl.multiple_of(qi * T, T)
            qb = q_ref[pl.ds(q0, T), :]
            m_scr[...] = jnp.full(m_scr.shape, -jnp.inf, F32)
            l_scr[...] = jnp.zeros(l_scr.shape, F32)
            acc[...] = jnp.zeros(acc.shape, F32)

            def tile(kj, masked):
                ks = pl.ds(pl.multiple_of(kj * T, T), T)
                kf = k_ref[ks, :].astype(F32)
                first = _head_lanes(kf.shape)
                kms = [jnp.where(first if hh == 0 else jnp.logical_not(first), kf, 0.0).astype(BF16) for hh in range(2)]
                sTs = [lax.dot_general(km, qb, _NT, preferred_element_type=F32) for km in kms]
                for hh in range(2):
                    b = bias[hh, ks, :]
                    ps = []
                    for cc in range(NC):
                        cols = slice(cc * LANES, (cc + 1) * LANES)
                        s = sTs[hh][:, cols] + b
                        if masked:
                            s = jnp.where(_causal_t(T, cc), s, -jnp.inf)
                        m_old = m_scr[hh, cc]
                        m_new = jnp.maximum(m_old, _over_keys(s, jnp.maximum))
                        alpha = jnp.exp(m_old - m_new)
                        p = jnp.exp(s - m_new)
                        l_scr[hh, cc] = alpha * l_scr[hh, cc] + _over_keys(p, jnp.add)
                        m_scr[hh, cc] = m_new
                        ps.append(p.astype(BF16))
                        acc[hh, :, cols] = acc[hh, :, cols] * alpha
                    acc[hh] += jnp.dot(vT[kj, hh * HEAD_DIM:(hh + 1) * HEAD_DIM, :], jnp.concatenate(ps, axis=1),
                                       preferred_element_type=F32)

            def inner(kj, _):
                tile(kj, False)
                return 0

            lax.fori_loop(0, qi, inner, 0)
            tile(qi, True)
            outs = []
            for hh in range(2):
                inv = jnp.concatenate([1.0 / l_scr[hh, cc] for cc in range(NC)], axis=1)
                outs.append(acc[hh] * inv)
                for cc in range(NC):
                    lse_ref[hh:hh + 1, pl.ds(q0 + cc * LANES, LANES)] = m_scr[hh, cc] + jnp.log(l_scr[hh, cc])
            out = jnp.concatenate(outs, axis=0).T
            o_ref[pl.ds(q0, T), :] = out.astype(o_ref.dtype)
            of_ref[pl.ds(q0, T), :] = out
            return 0

        lax.fori_loop(0, nq, q_step, 0)

    blk = lambda off: pl.BlockSpec((S, LANES), lambda p: (0, off + p))
    return pl.pallas_call(
        body, name=name, grid=(HP,),
        in_specs=[blk(0), blk(0), blk(HP), blk(0)],
        out_specs=[blk(0), blk(0), pl.BlockSpec((None, 2, S), lambda p: (p, 0, 0))],
        out_shape=[_hbm_out((S, D), BF16), _hbm_out((S, D), F32),
                   _hbm_out((HP, 2, S), F32)],
        scratch_shapes=[pltpu.VMEM((2, S, LANES), F32), pltpu.VMEM((nq, LANES, T), BF16),
                        pltpu.VMEM((2, HEAD_DIM, T), F32), pltpu.VMEM((2, NC, 1, LANES), F32),
                        pltpu.VMEM((2, NC, 1, LANES), F32)],
        compiler_params=_params(("parallel",), VMEM_LARGE),
    )(q, kv, kv, cfull)


def _attn_bwd(name, q, kv, cfull, of, do, lse3):
    S, D = q.shape
    HP = D // LANES
    T = _att_tile(S)
    nq = S // T
    NC = T // LANES
    scale = HEAD_DIM ** -0.5

    def body(q_ref, k_ref, v_ref, c_ref, of_ref, do_ref, lse_ref,
             dq_ref, dk_ref, dv_ref, dck_ref, drq_ref, bias, kT, dqT, delta, dr_scr):
        def prologue(i, _):
            rows = pl.ds(pl.multiple_of(i * T, T), T)
            bias[0, rows, :], bias[1, rows, :] = _key_bias(c_ref[rows, :])
            kT[i] = k_ref[rows, :].astype(F32).T.astype(BF16)
            prodT = (do_ref[rows, :].astype(F32) * of_ref[rows, :]).T
            for hh in range(2):
                delta[hh:hh + 1, rows] = jnp.sum(prodT[hh * HEAD_DIM:(hh + 1) * HEAD_DIM], axis=0, keepdims=True)
            dqT[i] = jnp.zeros((LANES, T), F32)
            return 0

        lax.fori_loop(0, nq, prologue, 0)
        dr_scr[...] = jnp.zeros(dr_scr.shape, F32)

        def kv_step(kj, _):
            ks = pl.ds(pl.multiple_of(kj * T, T), T)
            kf = k_ref[ks, :].astype(F32)
            vf = v_ref[ks, :].astype(F32)
            first = _head_lanes(kf.shape)
            masks = [first, jnp.logical_not(first)]
            kms = [jnp.where(m, kf, 0.0).astype(BF16) for m in masks]
            vms = [jnp.where(m, vf, 0.0).astype(BF16) for m in masks]

            def tile(qi, carry, masked):
                q0 = pl.multiple_of(qi * T, T)
                qb = q_ref[pl.ds(q0, T), :]
                dob = do_ref[pl.ds(q0, T), :]
                sTs = [lax.dot_general(km, qb, _NT, preferred_element_type=F32) for km in kms]
                dpTs = [lax.dot_general(vm, dob, _NT, preferred_element_type=F32) for vm in vms]
                out = []
                for hh in range(2):
                    dk_a, dv_a, dc_a = carry[3 * hh:3 * hh + 3]
                    b = bias[hh, ks, :]
                    head = slice(hh * HEAD_DIM, (hh + 1) * HEAD_DIM)
                    ps, dss = [], []
                    for cc in range(NC):
                        cols = slice(cc * LANES, (cc + 1) * LANES)
                        at = pl.ds(q0 + cc * LANES, LANES)
                        p = jnp.exp(sTs[hh][:, cols] + b - lse_ref[hh:hh + 1, at])
                        if masked:
                            p = jnp.where(_causal_t(T, cc), p, 0.0)
                        ds = p * (dpTs[hh][:, cols] - delta[hh:hh + 1, at])
                        ps.append(p.astype(BF16))
                        dss.append(ds.astype(BF16))
                        dc_a = dc_a + ds
                        dr_scr[hh:hh + 1, at] += _over_keys(ds, jnp.add)
                    pT = jnp.concatenate(ps, axis=1)
                    dsT = jnp.concatenate(dss, axis=1)
                    dv_a = dv_a + jnp.dot(pT, dob, preferred_element_type=F32)
                    dk_a = dk_a + jnp.dot(dsT, qb, preferred_element_type=F32)
                    dqT[qi, head, :] += jnp.dot(kT[kj, head, :], dsT, preferred_element_type=F32)
                    out += [dk_a, dv_a, dc_a]
                return tuple(out)

            zero = jnp.zeros((T, LANES), F32)
            carry = tile(kj, (zero,) * 6, True)
            dk0, dv0, dc0, dk1, dv1, dc1 = lax.fori_loop(kj + 1, nq, lambda qi, c: tile(qi, c, False), carry)
            dk_ref[ks, :] = jnp.where(first, dk0, dk1)
            dv_ref[ks, :] = jnp.where(first, dv0, dv1)
            dck_ref[ks, :] = jnp.where(first, jnp.broadcast_to(-jnp.sum(dc0, axis=1, keepdims=True), (T, LANES)),
                                       jnp.broadcast_to(-jnp.sum(dc1, axis=1, keepdims=True), (T, LANES)))
            return 0

        lax.fori_loop(0, nq, kv_step, 0)

        def epilogue(i, _):
            rows = pl.ds(pl.multiple_of(i * T, T), T)
            dq_ref[rows, :] = (dqT[i].T * scale).astype(dq_ref.dtype)
            return 0

        lax.fori_loop(0, nq, epilogue, 0)
        drq_ref[...] = dr_scr[...]

    blk = lambda off: pl.BlockSpec((S, LANES), lambda p: (0, off + p))
    row_spec = pl.BlockSpec((None, 2, S), lambda p: (p, 0, 0))
    return pl.pallas_call(
        body, name=name, grid=(HP,),
        in_specs=[blk(0), blk(0), blk(HP), blk(0), blk(0), blk(0), row_spec],
        out_specs=[blk(0), blk(0), blk(0), blk(0), row_spec],
        out_shape=[_hbm_out((S, D), BF16), _hbm_out((S, D), F32),
                   _hbm_out((S, D), F32), _hbm_out((S, D), F32),
                   _hbm_out((HP, 2, S), F32)],
        scratch_shapes=[pltpu.VMEM((2, S, LANES), F32), pltpu.VMEM((nq, LANES, T), BF16),
                        pltpu.VMEM((nq, LANES, T), F32), pltpu.VMEM((2, S), F32), pltpu.VMEM((2, S), F32)],
        compiler_params=_params(("parallel",), VMEM_LARGE),
    )(q, kv, kv, cfull, of, do, lse3)


def _logsig_fwd(name, f):
    S, C = f.shape

    def body(f_ref, o_ref):
        o_ref[...] = -_softplus(-f_ref[...])

    spec = pl.BlockSpec((S, C), lambda i: (0, 0))
    return pl.pallas_call(body, name=name, grid=(1,), in_specs=[spec], out_specs=spec,
                          out_shape=_hbm_out((S, C), F32),
                          compiler_params=_params(("arbitrary",)))(f)


def _logsig_bwd(name, dls, f):
    S, C = f.shape

    def body(d_ref, f_ref, o_ref, s_ref):
        df = d_ref[...] * _sigmoid(-f_ref[...])
        o_ref[...] = df.astype(o_ref.dtype)
        s_ref[...] = jnp.sum(df, axis=0, keepdims=True)

    spec = pl.BlockSpec((S, C), lambda i: (0, 0))
    return pl.pallas_call(body, name=name, grid=(1,), in_specs=[spec, spec],
                          out_specs=[spec, pl.BlockSpec((1, C), lambda i: (0, 0))],
                          out_shape=[_hbm_out((S, C), BF16), _hbm_out((1, C), F32)],
                          compiler_params=_params(("arbitrary",)))(dls, f)


def _add_cast(name, parts, out_dtype, tr=256):
    S, C = parts[0].shape
    tr = _tile(S, tr)
    n = len(parts)

    def body(*refs):
        acc = refs[0][...].astype(F32)
        for r in refs[1:n]:
            acc = acc + r[...].astype(F32)
        refs[n][...] = acc.astype(out_dtype)

    spec = pl.BlockSpec((tr, C), lambda i: (i, 0))
    return pl.pallas_call(body, name=name, grid=(S // tr,), in_specs=[spec] * n, out_specs=spec,
                          out_shape=_hbm_out((S, C), out_dtype),
                          compiler_params=_params(("parallel",)))(*parts)


def _local_step(x, target, gains, layer_weights, layer_prefetch, layer_grads):
    S, D = x.shape
    HP = D // LANES
    scale = HEAD_DIM ** -0.5
    tm = _tile(S, 512)
    tx = _tile(S, 256)
    td = _tile(D, 512)
    saved = []
    h = x
    l = 0
    kv = cfull = f_pre = hn_kv = h_kv = None
    while True:
        W = layer_weights(l, "mix", h)
        if W is None:
            break
        recurrent = "w_rec_in" in W
        if l == 0:
            xn = _rmsnorm_fwd("mix_norm_0", h, gains["mix"][0])
        if recurrent:
            CH = W["w_rec_in"].shape[-1]
            C = 2 * CH
            proj = _mm(f"rec_in_{l}", "nn", xn, W["w_rec_in"], grid=(S // tm, N_CHIPS),
                       a_spec=pl.BlockSpec((tm, D), lambda i, j: (i, 0)),
                       b_spec=pl.BlockSpec((None, D, CH), lambda i, j: (j, 0, 0)),
                       out_shape=(S, 2 * C), out_dtype=F32,
                       out_spec=pl.BlockSpec((tm, CH), lambda i, j: (i, j)))
            rc, rcb = _conv_fwd(f"conv_{l}", proj, W["conv_w"], W["conv_b"])
            gip, grp = _gates_fwd(f"gates_{l}", rcb, W["w_gates"], W["b_gates"])
            hrec, m = _lru_fwd(f"lru_{l}", proj, rc, gip, grp, W["lru_param"])
            layer_prefetch(l, "ffn", m)
            h_mid, hn = _mm_nn(f"rec_out_{l}", m, W["w_rec_out"], out_dtype=F32, res=h, tn=D, norm_gain=gains["ffn"][l])
            mix_saved = (xn, proj, rc, rcb, gip, grp, hrec, m)
        else:
            if "w_kv" in W:
                h_kv = h
                hn_kv = _rmsnorm_fwd("kv_norm", h, W["norm_kv"])
                kv = _mm_nn("kv_proj", hn_kv, W["w_kv"], out_dtype=BF16)
                f_pre = _mm_nn("f_proj", hn_kv, W["w_f"], out_dtype=F32, bias=W["b_f"])
                c = _cumsum_rows("c_cumsum", _logsig_fwd("logsig", f_pre), False)
                cfull = jnp.repeat(-c[:, :2 * HP], HEAD_DIM, axis=1)
            q = _mm_nn(f"q_proj_{l}", xn, W["w_q"], out_dtype=BF16, scale=scale)
            o, of, lse = _attn_fwd(f"attn_fwd_{l}", q, kv, cfull)
            layer_prefetch(l, "ffn", o)
            h_mid, hn = _mm_nn(f"o_proj_{l}", o, W["w_o"], out_dtype=F32, res=h, tn=D, norm_gain=gains["ffn"][l])
            mix_saved = (xn, q, o, of, lse)
        W = {**W, **layer_weights(l, "ffn", h_mid)}
        z3, act = _swiglu_fwd(f"ffn_in_{l}", hn, W["w_ffn_in"])
        layer_prefetch(l + 1, "mix", act)
        saved.append((W, h, h_mid, mix_saved, (hn, z3, act)))
        l += 1
        if l < len(gains["mix"]):
            h, xn = _mm_nn(f"ffn_out_{l - 1}", act, W["w_ffn_out"], out_dtype=F32, res=h_mid, tn=D,
                           norm_gain=gains["mix"][l])
        else:
            h = _mm_nn(f"ffn_out_{l - 1}", act, W["w_ffn_out"], out_dtype=F32, res=h_mid, tn=D)

    dh, dhb, dg_final, loss_row = _loss_head("loss_head", h, target, gains["final"])

    dk_parts, dv_parts, dc_parts = [], [], []
    token = None
    for l in reversed(range(len(saved))):
        W, h_in, h_mid, mix_saved, (hn, z3, act) = saved[l]
        recurrent = "w_rec_in" in W
        FH = W["w_ffn_in"].shape[-1]
        G = {}
        norm_ffn = gains["ffn"][l]
        if token is not None:
            norm_ffn = norm_ffn + jnp.minimum(token[:1, :1], 0.0)
        G["w_ffn_out"] = _mm_tn(f"d_ffn_out_{l}", act, dhb, out_dtype=BF16, tn=D)
        dz3 = _swiglu_bwd(f"d_act_{l}", dhb, W["w_ffn_out"], z3)
        G["w_ffn_in"] = _mm(
            f"d_ffn_in_{l}", "tn", hn, dz3, grid=(D // td, N_CHIPS),
            a_spec=pl.BlockSpec((S, td), lambda i, j: (0, i)),
            b_spec=pl.BlockSpec((None, S, FH), lambda i, j: (j // 2, 0, j % 2)),
            out_shape=(N_CHIPS, D, FH), out_dtype=BF16,
            out_spec=pl.BlockSpec((None, td, FH), lambda i, j: (j, i, 0)))
        token = layer_grads(l, "ffn", G)
        G = {}
        norm_ffn = norm_ffn + jnp.minimum(token[:1, :1], 0.0)
        dh, dhb, dgp = _mm(f"d_ffn_hn_{l}", "nt", dz3, W["w_ffn_in"], grid=(S // tx, 1),
                           a_spec=[pl.BlockSpec((None, tx, FH), functools.partial(lambda i, j, k: (k // 2, i, k % 2), k=k))
                                   for k in range(N_CHIPS)],
                           b_spec=[pl.BlockSpec((None, D, FH), functools.partial(lambda i, j, k: (k, 0, 0), k=k))
                                   for k in range(N_CHIPS)],
                           out_shape=(S, D), out_dtype=F32, out_spec=pl.BlockSpec((tx, D), lambda i, j: (i, 0)),
                           norm_bwd=(h_mid, norm_ffn, dh))
        G["norm_ffn"] = jnp.sum(dgp, axis=0)
        if recurrent:
            CH = W["w_rec_in"].shape[-1]
            C = 2 * CH
            xn, proj, rc, rcb, gip, grp, hrec, m = mix_saved
            G["w_rec_out"] = _mm_tn(f"d_rec_out_{l}", m, dhb, out_dtype=BF16, tn=D)
            dm = _mm_nt(f"d_m_{l}", dhb, W["w_rec_out"], out_dtype=F32, tn=C)
            dgb, dgi, dgr, drc1, G["b_gi"], G["b_gr"], G["lru_param"] = _lru_bwd(
                f"d_lru_{l}", dm, proj, hrec, rc, gip, grp, W["lru_param"])
            drc, G["w_gates"] = _gates_bwd(f"d_gates_{l}", dgi, dgr, rcb, W["w_gates"], drc1)
            drec, G["conv_w"], G["conv_b"] = _conv_bwd(f"d_conv_{l}", drc, proj, W["conv_w"])
            dproj = jnp.concatenate([dgb, drec], axis=1)
            G["w_rec_in"] = _mm(
                f"d_rec_in_{l}", "tn", xn, dproj, grid=(1, N_CHIPS),
                a_spec=pl.BlockSpec((S, D), lambda i, j: (0, 0)),
                b_spec=pl.BlockSpec((S, CH), lambda i, j: (0, j)),
                out_shape=(N_CHIPS, D, CH), out_dtype=BF16,
                out_spec=pl.BlockSpec((None, D, CH), lambda i, j: (j, 0, 0)))
            dh, dhb, dgp = _mm(f"d_rec_xn_{l}", "nt", dproj, W["w_rec_in"], grid=(S // tx, 1),
                               a_spec=[pl.BlockSpec((tx, CH), functools.partial(lambda i, j, k: (i, k), k=k))
                                       for k in range(N_CHIPS)],
                               b_spec=[pl.BlockSpec((None, D, CH), functools.partial(lambda i, j, k: (k, 0, 0), k=k))
                                       for k in range(N_CHIPS)],
                               out_shape=(S, D), out_dtype=F32, out_spec=pl.BlockSpec((tx, D), lambda i, j: (i, 0)),
                               norm_bwd=(h_in, gains["mix"][l], dh))
        else:
            xn, q, o, of, lse = mix_saved
            G["w_o"] = _mm_tn(f"d_o_proj_{l}", o, dhb, out_dtype=BF16, tn=D)
            do = _mm_nt(f"d_o_{l}", dhb, W["w_o"], out_dtype=BF16, tn=D)
            dq, dk, dv, dck, drq = _attn_bwd(f"attn_bwd_{l}", q, kv, cfull, of, do, lse)
            dk_parts.append(dk)
            dv_parts.append(dv)
            dc_parts.append(dck[:, ::HEAD_DIM] + drq.reshape(2 * HP, S).T)
            G["w_q"] = _mm_tn(f"d_q_proj_{l}", xn, dq, out_dtype=BF16, tn=D)
            dh, dhb, dgp = _mm_nt(f"d_q_xn_{l}", dq, W["w_q"], out_dtype=F32, tn=D, norm_bwd=(h_in, gains["mix"][l], dh))
        G["norm_mix"] = jnp.sum(dgp, axis=0)
        if "w_kv" in W:
            dkb = _add_cast("dk_sum", dk_parts, BF16)
            dvb = _add_cast("dv_sum", dv_parts, BF16)
            dkv = jnp.concatenate([dkb, dvb], axis=1)
            dc = sum(dc_parts[1:], dc_parts[0])
            dc_pad = jnp.pad(dc, ((0, 0), (0, LANES - 2 * HP)))
            dls = _cumsum_rows("dc_cumsum", dc_pad, True)
            dfb, G["b_f"] = _logsig_bwd("d_logsig", dls, f_pre)
            G["w_kv"] = _mm_tn("d_kv_proj", hn_kv, dkv, out_dtype=BF16)
            G["w_f"] = _mm_tn("d_f_proj", hn_kv, dfb, out_dtype=F32)
            dhn_f = _mm_nt("d_f_hn", dfb, W["w_f"], out_dtype=F32, tn=D)
            dh, dhb, dgp = _mm_nt("d_kv_hn", dkv, W["w_kv"], out_dtype=F32, tn=D, res=dhn_f,
                                  norm_bwd=(h_kv, W["norm_kv"], dh))
            G["norm_kv"] = jnp.sum(dgp, axis=0)
        token = layer_grads(l, "mix", G)
    return loss_row, dh, dg_final


_ANY = pl.BlockSpec(memory_space=pl.ANY)


def _position():
    return lax.axis_index("x"), lax.axis_index("y"), lax.axis_index("c")


def _chip_peers(x, y):
    return [(1 - x, y), (x, 1 - y), (1 - x, 1 - y)]


def _half_rows(c, n):
    h = n // 2
    assert h % 16 == 0
    return pl.ds(pl.multiple_of(c * h, 16), h)


def _place_own(name, shard, layer, me):
    _, R, C = shard.shape
    tr = _row_tile(R, C, 2 * shard.dtype.itemsize, target=8 << 20)

    def body(me_ref, x_ref, o_ref):
        o_ref[...] = x_ref[...]

    return pl.pallas_call(
        body, name=name,
        grid_spec=pltpu.PrefetchScalarGridSpec(
            num_scalar_prefetch=1, grid=(R // tr,),
            in_specs=[pl.BlockSpec((None, tr, C), lambda i, me_ref: (layer, i, 0))],
            out_specs=pl.BlockSpec((None, tr, C), lambda i, me_ref: (me_ref[0], i, 0))),
        out_shape=_hbm_out((N_CHIPS, R, C), shard.dtype),
        compiler_params=_params(("parallel",)),
    )(me, shard)


def _gather_smalls(name, smalls):
    ns = len(smalls)

    def body(*refs):
        ins, outs = refs[:ns], refs[ns:2 * ns]
        send_sems, recv_sems, local_sems = refs[2 * ns:]
        x, y, c = _position()
        me = 2 * x + y
        peers = _chip_peers(x, y)

        def remote(t, k, chip):
            px, py = peers[k]
            return pltpu.make_async_remote_copy(
                src_ref=ins[t], dst_ref=outs[t].at[chip], send_sem=send_sems.at[3 * t + k],
                recv_sem=recv_sems.at[3 * t + k], device_id=(px, py, c), device_id_type=MESH)

        local = [pltpu.make_async_copy(ins[t], outs[t].at[me], local_sems.at[t]) for t in range(ns)]
        for t in range(ns):
            local[t].start()
            for k in range(3):
                remote(t, k, me).start()
        for t in range(ns):
            for k in range(3):
                px, py = peers[k]
                remote(t, k, 2 * px + py).wait_recv()
        for t in range(ns):
            for k in range(3):
                remote(t, k, me).wait_send()
            local[t].wait()

    return pl.pallas_call(
        body, name=name, in_specs=[_ANY] * ns, out_specs=[_ANY] * ns,
        out_shape=[_hbm_out((N_CHIPS,) + s.shape, s.dtype) for s in smalls],
        scratch_shapes=[pltpu.SemaphoreType.DMA((3 * ns,)), pltpu.SemaphoreType.DMA((3 * ns,)),
                        pltpu.SemaphoreType.DMA((ns,))],
    )(*smalls)


_SEM = pl.BlockSpec(memory_space=pltpu.SEMAPHORE)
_SPLIT = pltpu.CompilerParams(has_side_effects=pltpu.SideEffectType.DATAFLOW_SIDE_EFFECTING)


def _weight_copy(shards, buf, items, sems, i, k, chip_of_dst, peers, c):
    w, l = items[i]
    px, py = peers[k]
    half = _half_rows(c, shards[w].shape[1])
    return pltpu.make_async_remote_copy(
        src_ref=shards[w].at[l, half], dst_ref=buf.at[chip_of_dst, half],
        send_sem=sems[0].at[3 * i + k], recv_sem=sems[1].at[3 * i + k],
        device_id=(px, py, c), device_id_type=MESH)


def _gather_start(name, shards, bufs, items, after):
    nw, n = len(shards), len(bufs)

    def body(*refs):
        ins, outs, sems = refs[:nw], refs[nw + n + 1:nw + 2 * n + 1], refs[nw + 2 * n + 1:]
        x, y, c = _position()
        peers = _chip_peers(x, y)
        for i in range(n):
            for k in range(3):
                _weight_copy(ins, outs[i], items, sems, i, k, 2 * x + y, peers, c).start()

    res = pl.pallas_call(
        body, name=name, in_specs=[_ANY] * (nw + n + 1), out_specs=[_ANY] * n + [_SEM, _SEM],
        out_shape=[_hbm_out(b.shape, b.dtype) for b in bufs]
        + [pltpu.SemaphoreType.DMA((3 * n,)), pltpu.SemaphoreType.DMA((3 * n,))],
        input_output_aliases={nw + i: i for i in range(n)}, compiler_params=_SPLIT,
    )(*shards, *bufs, after)
    return res[:n], res[n:]


def _gather_wait(name, shards, bufs, items, ids, sems, after):
    nw, m = len(shards), len(ids)

    def body(*refs):
        ins, bs = refs[:nw], refs[nw:nw + m]
        sem_refs = refs[nw + m:nw + m + 2]
        x, y, c = _position()
        peers = _chip_peers(x, y)
        for j, i in enumerate(ids):
            for k in range(3):
                px, py = peers[k]
                _weight_copy(ins, bs[j], items, sem_refs, i, k, 2 * px + py, peers, c).wait_recv()
        for j, i in enumerate(ids):
            for k in range(3):
                _weight_copy(ins, bs[j], items, sem_refs, i, k, 2 * x + y, peers, c).wait_send()

    res = pl.pallas_call(
        body, name=name, in_specs=[_ANY] * (nw + m) + [_SEM, _SEM, _ANY], out_specs=[_ANY] * m,
        out_shape=[_hbm_out(bufs[i].shape, bufs[i].dtype) for i in ids],
        input_output_aliases={nw + j: j for j in range(m)}, compiler_params=_SPLIT,
    )(*shards, *[bufs[i] for i in ids], *sems, after)
    return list(res)


def _forward_copy(src, dst, sems, i, k, core):
    x, y, c = _position()
    px, py = _chip_peers(x, y)[k]
    half = _half_rows(core, src.shape[1])
    return pltpu.make_async_remote_copy(
        src_ref=src.at[2 * px + py, half], dst_ref=dst.at[2 * px + py, half],
        send_sem=sems[0].at[3 * i + k], recv_sem=sems[1].at[3 * i + k],
        device_id=(x, y, 1 - c), device_id_type=MESH)


def _forward_start(name, bufs):
    n = len(bufs)

    def body(*refs):
        ins, outs, sems = refs[:n], refs[n:2 * n], refs[2 * n:]
        c = lax.axis_index("c")
        for i in range(n):
            for k in range(3):
                _forward_copy(ins[i], outs[i], sems, i, k, c).start()

    res = pl.pallas_call(
        body, name=name, in_specs=[_ANY] * n, out_specs=[_ANY] * n + [_SEM, _SEM],
        out_shape=[_hbm_out(g.shape, g.dtype) for g in bufs]
        + [pltpu.SemaphoreType.DMA((3 * n,)), pltpu.SemaphoreType.DMA((3 * n,))],
        input_output_aliases={i: i for i in range(n)}, compiler_params=_SPLIT,
    )(*bufs)
    return list(res[:n]), res[n:]


def _forward_wait(name, bufs, sems, after):
    n = len(bufs)

    def body(*refs):
        bs, sem_refs = refs[:n], refs[n:n + 2]
        c = lax.axis_index("c")
        for i in range(n):
            for k in range(3):
                _forward_copy(bs[i], bs[i], sem_refs, i, k, 1 - c).wait_recv()
        for i in range(n):
            for k in range(3):
                _forward_copy(bs[i], bs[i], sem_refs, i, k, c).wait_send()

    return list(pl.pallas_call(
        body, name=name, in_specs=[_ANY] * n + [_SEM, _SEM, _ANY], out_specs=[_ANY] * n,
        out_shape=[_hbm_out(g.shape, g.dtype) for g in bufs],
        input_output_aliases={i: i for i in range(n)}, compiler_params=_SPLIT,
    )(*bufs, *sems, after))


def _reduce_copy(grads, others, sems, i):
    x, y, c = _position()
    return pltpu.make_async_remote_copy(
        src_ref=grads[i].at[:, _half_rows(1 - c, grads[i].shape[1])], dst_ref=others[i],
        send_sem=sems[0].at[i], recv_sem=sems[1].at[i], device_id=(x, y, 1 - c), device_id_type=MESH)


def _reduce_start(name, grads, after):
    n = len(grads)

    def body(*refs):
        ins, outs, sems, token = refs[:n], refs[n + 1:2 * n + 1], refs[2 * n + 1:2 * n + 3], refs[2 * n + 3]
        for i in range(n):
            _reduce_copy(ins, outs, sems, i).start()
        token[...] = jnp.zeros_like(token)

    res = pl.pallas_call(
        body, name=name, in_specs=[_ANY] * (n + 1),
        out_specs=[_ANY] * n + [_SEM, _SEM, pl.BlockSpec(memory_space=pltpu.VMEM)],
        out_shape=[_hbm_out((N_CHIPS, g.shape[1] // 2, g.shape[2]), g.dtype) for g in grads]
        + [pltpu.SemaphoreType.DMA((n,)), pltpu.SemaphoreType.DMA((n,)), jax.ShapeDtypeStruct((SUBLANES, LANES), F32)],
        compiler_params=_SPLIT,
    )(*grads, after)
    return list(res[:n]), res[n:n + 2], res[n + 2]


def _reduce_wait(name, grads, others, sems, after):
    n = len(grads)

    def body(*refs):
        ins, os_, sem_refs = refs[:n], refs[n:2 * n], refs[2 * n:2 * n + 2]
        for i in range(n):
            _reduce_copy(ins, os_, sem_refs, i).wait_recv()
        for i in range(n):
            _reduce_copy(ins, os_, sem_refs, i).wait_send()

    return list(pl.pallas_call(
        body, name=name, in_specs=[_ANY] * (2 * n) + [_SEM, _SEM, _ANY], out_specs=[_ANY] * n,
        out_shape=[_hbm_out(o.shape, o.dtype) for o in others],
        input_output_aliases={n + i: i for i in range(n)}, compiler_params=_SPLIT,
    )(*grads, *others, *sems, after))


def _sum_cores(name, g, other, core):
    _, R, C = g.shape
    H = R // 2
    tr = _row_tile(H, C, 3 * 2, target=12 << 20)
    nb = H // tr

    def body(c_ref, g_ref, o_ref, out_ref):
        out_ref[...] = (g_ref[...].astype(F32) + o_ref[...].astype(F32)).astype(out_ref.dtype)

    return pl.pallas_call(
        body, name=name,
        grid_spec=pltpu.PrefetchScalarGridSpec(
            num_scalar_prefetch=1, grid=(N_CHIPS, nb),
            in_specs=[pl.BlockSpec((None, tr, C), lambda j, i, c_ref: (j, c_ref[0] * nb + i, 0)),
                      pl.BlockSpec((None, tr, C), lambda j, i, c_ref: (j, i, 0))],
            out_specs=pl.BlockSpec((None, tr, C), lambda j, i, c_ref: (j, i, 0))),
        out_shape=_hbm_out((N_CHIPS, H, C), BF16),
        compiler_params=_params(("parallel", "parallel")),
    )(core, g, other)


def _sum_chips(name, received, own, full, layer, me_core):
    _, H, C = received.shape
    tr = _row_tile(H, C, 3 * 2 + 2 + 4, target=12 << 20)
    nb = H // tr

    def body(s_ref, r_ref, own_ref, full_ref, out_ref):
        acc = r_ref[0].astype(F32)
        for k in (1, 2):
            acc = acc + r_ref[k].astype(F32)
        out_ref[...] = acc + own_ref[...].astype(F32)

    return pl.pallas_call(
        body, name=name,
        grid_spec=pltpu.PrefetchScalarGridSpec(
            num_scalar_prefetch=1, grid=(nb,),
            in_specs=[pl.BlockSpec((3, tr, C), lambda i, s_ref: (0, i, 0)),
                      pl.BlockSpec((None, tr, C), lambda i, s_ref: (s_ref[0], i, 0)),
                      _ANY],
            out_specs=pl.BlockSpec((None, tr, C), lambda i, s_ref: (layer, s_ref[1] * nb + i, 0))),
        out_shape=_hbm_out(full.shape, full.dtype),
        input_output_aliases={3: 0},
        compiler_params=_params(("parallel",)),
    )(me_core, received, own, full)


def _part_copy(parts, recv, sems, i, k, peers, c):
    px, py = peers[k]
    return pltpu.make_async_remote_copy(
        src_ref=parts[i].at[2 * px + py], dst_ref=recv[i].at[k],
        send_sem=sems[0].at[3 * i + k], recv_sem=sems[1].at[3 * i + k],
        device_id=(px, py, c), device_id_type=MESH)


def _scatter_start(name, parts):
    n = len(parts)

    def body(*refs):
        ins, outs, sems, token = refs[:n], refs[n:2 * n], refs[2 * n:2 * n + 2], refs[2 * n + 2]
        x, y, c = _position()
        peers = _chip_peers(x, y)
        for i in range(n):
            for k in range(3):
                _part_copy(ins, outs, sems, i, k, peers, c).start()
        token[...] = jnp.zeros_like(token)

    res = pl.pallas_call(
        body, name=name, in_specs=[_ANY] * n,
        out_specs=[_ANY] * n + [_SEM, _SEM, pl.BlockSpec(memory_space=pltpu.VMEM)],
        out_shape=[_hbm_out((3,) + p.shape[1:], p.dtype) for p in parts]
        + [pltpu.SemaphoreType.DMA((3 * n,)), pltpu.SemaphoreType.DMA((3 * n,)),
           jax.ShapeDtypeStruct((SUBLANES, LANES), F32)],
        compiler_params=_SPLIT,
    )(*parts)
    return list(res[:n]), res[n:n + 2], res[n + 2]


def _scatter_wait(name, parts, recv, sems):
    n = len(parts)

    def body(*refs):
        ins, rs, sem_refs = refs[:n], refs[n:2 * n], refs[2 * n:2 * n + 2]
        x, y, c = _position()
        peers = _chip_peers(x, y)
        for i in range(n):
            for k in range(3):
                _part_copy(ins, rs, sem_refs, i, k, peers, c).wait_recv()
        for i in range(n):
            for k in range(3):
                _part_copy(ins, rs, sem_refs, i, k, peers, c).wait_send()

    return list(pl.pallas_call(
        body, name=name, in_specs=[_ANY] * (2 * n) + [_SEM, _SEM], out_specs=[_ANY] * n,
        out_shape=[_hbm_out(r.shape, r.dtype) for r in recv],
        input_output_aliases={n + i: i for i in range(n)}, compiler_params=_SPLIT,
    )(*parts, *recv, *sems))


def _share_d2d(name, full):
    n = len(full)

    def body(*refs):
        ins, outs = refs[:n], refs[n:2 * n]
        send_sems, recv_sems = refs[2 * n:]
        x, y, c = _position()

        def remote(w, core):
            half = _half_rows(core, ins[w].shape[1])
            return pltpu.make_async_remote_copy(
                src_ref=ins[w].at[:, half], dst_ref=outs[w].at[:, half],
                send_sem=send_sems.at[w], recv_sem=recv_sems.at[w],
                device_id=(x, y, 1 - c), device_id_type=MESH)

        for w in range(n):
            remote(w, c).start()
        for w in range(n):
            remote(w, 1 - c).wait_recv()
        for w in range(n):
            remote(w, c).wait_send()

    return pl.pallas_call(
        body, name=name, in_specs=[_ANY] * n, out_specs=[_ANY] * n,
        out_shape=[_hbm_out(f.shape, f.dtype) for f in full],
        input_output_aliases={w: w for w in range(n)},
        scratch_shapes=[pltpu.SemaphoreType.DMA((n,)), pltpu.SemaphoreType.DMA((n,))],
    )(*full)


def _gather_all(name, a):
    def body(a_ref, o_ref, send_sems, recv_sems, local_sem):
        x, y, c = _position()
        me = 4 * x + 2 * y + c

        def peer(k):
            return (x ^ ((k >> 2) & 1), y ^ ((k >> 1) & 1), c ^ (k & 1))

        def remote(k, slot):
            return pltpu.make_async_remote_copy(
                src_ref=a_ref, dst_ref=o_ref.at[slot], send_sem=send_sems.at[k - 1], recv_sem=recv_sems.at[k - 1],
                device_id=peer(k), device_id_type=MESH)

        local = pltpu.make_async_copy(a_ref, o_ref.at[me], local_sem)
        local.start()
        for k in range(1, N_DEV):
            remote(k, me).start()
        for k in range(1, N_DEV):
            px, py, pc = peer(k)
            remote(k, 4 * px + 2 * py + pc).wait_recv()
        for k in range(1, N_DEV):
            remote(k, me).wait_send()
        local.wait()

    return pl.pallas_call(
        body, name=name, in_specs=[_ANY], out_specs=_ANY,
        out_shape=_hbm_out((N_DEV,) + a.shape, a.dtype),
        scratch_shapes=[pltpu.SemaphoreType.DMA((N_DEV - 1,)), pltpu.SemaphoreType.DMA((N_DEV - 1,)),
                        pltpu.SemaphoreType.DMA],
    )(a)


def _rows2d(a, lead=0):
    return a.reshape(a.shape[:lead] + (-1, a.shape[-1]))


def _row_tile(rows, cols, itemsize=4, target=1 << 20):
    want = max(SUBLANES, target // (cols * itemsize))
    t = min(rows, (want // 16) * 16)
    while t > 16 and rows % t:
        t -= 16
    return t if rows % t == 0 else rows


def _sum_slots(name, r, out_dtype=F32):
    ns = r.shape[0]
    r2 = _rows2d(r, 1)
    _, rows, cols = r2.shape
    tr = _row_tile(rows, cols)

    def body(r_ref, o_ref):
        acc = r_ref[0].astype(F32)
        for s in range(1, ns):
            acc = acc + r_ref[s].astype(F32)
        o_ref[...] = acc.astype(o_ref.dtype)

    out = pl.pallas_call(
        body, name=name, grid=(rows // tr,),
        in_specs=[pl.BlockSpec((ns, tr, cols), lambda i: (0, i, 0))],
        out_specs=pl.BlockSpec((tr, cols), lambda i: (i, 0)),
        out_shape=_hbm_out((rows, cols), out_dtype),
        compiler_params=_params(("parallel",)),
    )(r2)
    return out.reshape(r.shape[1:])


def _adamw(name, g_parts, w, m, v):
    shape = w.shape
    ng = len(g_parts)
    args = [_rows2d(a) for a in (*g_parts, w, m, v)]
    rows, cols = args[0].shape
    tr = _row_tile(rows, cols, (ng + 7) * 4, target=16 << 20)
    c1 = 1.0 - ADAM_B1 ** ADAM_STEP
    c2 = 1.0 - ADAM_B2 ** ADAM_STEP

    def body(*refs):
        g = refs[0][...]
        for r in refs[1:ng]:
            g = g + r[...]
        w_ref, m_ref, v_ref = refs[ng:ng + 3]
        g_out, d_out, m_out, v_out = refs[ng + 3:]
        mn = ADAM_B1 * m_ref[...] + (1.0 - ADAM_B1) * g
        vn = ADAM_B2 * v_ref[...] + (1.0 - ADAM_B2) * (g * g)
        m_hat = mn / c1
        v_hat = vn / c2
        g_out[...] = g
        d_out[...] = -ADAM_LR * (m_hat / (jnp.sqrt(v_hat) + ADAM_EPS) + ADAM_WD * w_ref[...])
        m_out[...] = mn
        v_out[...] = vn

    spec = pl.BlockSpec((tr, cols), lambda i: (i, 0))
    outs = pl.pallas_call(
        body, name=name, grid=(rows // tr,), in_specs=[spec] * (ng + 3), out_specs=[spec] * 4,
        out_shape=[_hbm_out((rows, cols), F32)] * 4,
        compiler_params=_params(("parallel",)),
    )(*args)
    return tuple(o.reshape(shape) for o in outs)


_WEIGHTS = ["norm_mix", "norm_ffn", "w_ffn_in", "w_ffn_out", "w_rec_in", "conv_w", "conv_b", "w_lru_gates",
            "b_lru_gates", "lru_param", "w_rec_out", "norm_kv", "w_kvf", "b_forget", "w_q", "w_o", "norm_final"]
_BIG = ["w_ffn_in", "w_ffn_out", "w_rec_in", "w_lru_gates", "w_rec_out", "w_kvf", "w_q", "w_o"]


def _stack3(a):
    return a[None] if a.ndim == 2 else a.reshape(a.shape[0], -1, a.shape[-1])


def _pad_lanes(a, n):
    return jnp.pad(a, ((0, 0),) * (a.ndim - 1) + ((0, n - a.shape[-1]),))


def kernel(x, norm_mix, norm_ffn, w_ffn_in, w_ffn_out, w_rec_in, conv_w, conv_b, w_lru_gates, b_lru_gates, lru_param, w_rec_out, norm_kv, w_kvf, b_forget, w_q, w_o, norm_final, loss_target, m_norm_mix, m_norm_ffn, m_w_ffn_in, m_w_ffn_out, m_w_rec_in, m_conv_w, m_conv_b, m_w_lru_gates, m_b_lru_gates, m_lru_param, m_w_rec_out, m_norm_kv, m_w_kvf, m_b_forget, m_w_q, m_w_o, m_norm_final, v_norm_mix, v_norm_ffn, v_w_ffn_in, v_w_ffn_out, v_w_rec_in, v_conv_w, v_conv_b, v_w_lru_gates, v_b_lru_gates, v_lru_param, v_w_rec_out, v_norm_kv, v_w_kvf, v_b_forget, v_w_q, v_w_o, v_norm_final):
    P = dict(norm_mix=norm_mix, norm_ffn=norm_ffn, w_ffn_in=w_ffn_in, w_ffn_out=w_ffn_out, w_rec_in=w_rec_in,
             conv_w=conv_w, conv_b=conv_b, w_lru_gates=w_lru_gates, b_lru_gates=b_lru_gates, lru_param=lru_param,
             w_rec_out=w_rec_out, norm_kv=norm_kv, w_kvf=w_kvf, b_forget=b_forget, w_q=w_q, w_o=w_o,
             norm_final=norm_final)
    M1 = dict(norm_mix=m_norm_mix, norm_ffn=m_norm_ffn, w_ffn_in=m_w_ffn_in, w_ffn_out=m_w_ffn_out,
              w_rec_in=m_w_rec_in, conv_w=m_conv_w, conv_b=m_conv_b, w_lru_gates=m_w_lru_gates,
              b_lru_gates=m_b_lru_gates, lru_param=m_lru_param, w_rec_out=m_w_rec_out, norm_kv=m_norm_kv,
              w_kvf=m_w_kvf, b_forget=m_b_forget, w_q=m_w_q, w_o=m_w_o, norm_final=m_norm_final)
    M2 = dict(norm_mix=v_norm_mix, norm_ffn=v_norm_ffn, w_ffn_in=v_w_ffn_in, w_ffn_out=v_w_ffn_out,
              w_rec_in=v_w_rec_in, conv_w=v_conv_w, conv_b=v_conv_b, w_lru_gates=v_w_lru_gates,
              b_lru_gates=v_b_lru_gates, lru_param=v_lru_param, w_rec_out=v_w_rec_out, norm_kv=v_norm_kv,
              w_kvf=v_w_kvf, b_forget=v_b_forget, w_q=v_w_q, w_o=v_w_o, norm_final=v_norm_final)

    _, S, D = x.shape
    L = norm_mix.shape[0]
    NA, NBLK, BW, GS = w_lru_gates.shape
    NB = w_q.shape[0]
    C = NBLK * BW
    CS = C // N_CHIPS
    H = b_forget.shape[0]
    assert C == D and H * HEAD_DIM == D and H <= LANES
    chip = 2 * lax.axis_index("x") + lax.axis_index("y")

    small_a = jnp.concatenate([conv_w, conv_b[:, None], lru_param[:, None]], axis=1)
    small_a, b_gates = _gather_smalls("gather_smalls", [small_a, b_lru_gates])
    small_a = small_a.transpose(1, 2, 0, 3).reshape(NA, 6, C)
    b_gates = b_gates.transpose(1, 2, 0, 3).reshape(NA, NBLK, 1, N_CHIPS * GS)
    shards = [_stack3(P[w]).astype(BF16) for w in _BIG]
    core = lax.axis_index("c")
    chip_id = jnp.reshape(chip, (1,)).astype(jnp.int32)
    core_id = jnp.reshape(core, (1,)).astype(jnp.int32)
    me_core = jnp.stack([chip, core]).astype(jnp.int32)

    def stage_items(l, part):
        if part == "ffn":
            return [(_BIG.index("w_ffn_in"), l), (_BIG.index("w_ffn_out"), l)]
        if l < NA:
            names, at = ["w_rec_in", "w_lru_gates", "w_rec_out"], l
        else:
            names, at = (["w_kvf"] if l == NA else []) + ["w_q", "w_o"], l - NA
        return [(_BIG.index(n), 0 if n == "w_kvf" else at) for n in names]

    stages = [(l, part) for l in range(L) for part in ("mix", "ffn")]
    items = [it for st in stages for it in stage_items(*st)]
    ids_of = {st: [items.index(it) for it in stage_items(*st)] for st in stages}
    bufs = [_place_own(f"place_{_BIG[w]}_{li}", shards[w], li, chip_id) for w, li in items]
    bufs, gather_sems = _gather_start("gather_start", shards, bufs, items, small_a)

    forwarding = {}

    def layer_prefetch(l, part, after):
        if l < L and (l, part) not in forwarding:
            ids = ids_of[(l, part)]
            got = _gather_wait(f"gather_wait_{part}_{l}", shards, bufs, items, ids, gather_sems, after)
            forwarding[(l, part)] = _forward_start(f"forward_start_{part}_{l}", got)

    def layer_weights(l, part, after):
        if l >= L:
            return None
        layer_prefetch(l, part, after)
        ids = ids_of[(l, part)]
        got, sems = forwarding[(l, part)]
        got = _forward_wait(f"forward_wait_{part}_{l}", got, sems, after)
        B = {_BIG[items[i][0]]: g for i, g in zip(ids, got)}
        if part == "ffn":
            return dict(w_ffn_in=B["w_ffn_in"], w_ffn_out=B["w_ffn_out"].reshape(-1, D))
        W = {}
        if l < NA:
            W.update(w_rec_in=B["w_rec_in"],
                     w_gates=B["w_lru_gates"].reshape(N_CHIPS, NBLK, BW, GS).transpose(1, 2, 0, 3).reshape(
                         NBLK, BW, N_CHIPS * GS),
                     b_gates=b_gates[l], w_rec_out=B["w_rec_out"].reshape(C, D),
                     conv_w=small_a[l, :4], conv_b=small_a[l, 4:5], lru_param=small_a[l, 5:6])
        else:
            W.update(w_q=B["w_q"].reshape(D, D), w_o=B["w_o"].reshape(D, D))
            if l == NA:
                w_kvf_full = B["w_kvf"].transpose(1, 0, 2).reshape(D, -1)
                W.update(norm_kv=norm_kv[None], w_kv=w_kvf_full[:, :2 * D],
                         w_f=_pad_lanes(w_kvf_full[:, 2 * D:], LANES), b_f=_pad_lanes(b_forget[None], LANES))
        return W

    G_small = {l: {} for l in range(L)}
    pending = {}
    reducing = []

    def finish_reduce(after):
        l, part, its, grads, others, sems = reducing.pop()
        others = _reduce_wait(f"reduce_wait_{part}_{l}", grads, others, sems, after)
        parts = [_sum_cores(f"sum_cores_{l}_{_BIG[w]}", g, o, core_id) for (w, _), g, o in zip(its, grads, others)]
        recv, sems, token = _scatter_start(f"scatter_start_{part}_{l}", parts)
        pending[(l, part)] = (parts, recv, sems)
        return token

    def layer_grads(l, part, G):
        G_small[l].update(G)
        after = finish_reduce(G["w_ffn_in" if part == "ffn" else "norm_mix"]) if reducing else jnp.zeros(
            (SUBLANES, LANES), F32)
        by_name = dict(
            w_ffn_in=lambda: G["w_ffn_in"], w_ffn_out=lambda: G["w_ffn_out"].reshape(N_CHIPS, -1, D),
            w_rec_in=lambda: G["w_rec_in"],
            w_lru_gates=lambda: G["w_gates"].reshape(NBLK, BW, N_CHIPS, GS).transpose(2, 0, 1, 3).reshape(
                N_CHIPS, NBLK * BW, GS),
            w_rec_out=lambda: G["w_rec_out"].reshape(N_CHIPS, -1, D),
            w_kvf=lambda: jnp.concatenate([G["w_kv"].astype(F32), G["w_f"][:, :H]], axis=1).reshape(
                D, N_CHIPS, -1).transpose(1, 0, 2).astype(BF16),
            w_q=lambda: G["w_q"].reshape(N_CHIPS, -1, D), w_o=lambda: G["w_o"].reshape(N_CHIPS, -1, D))
        its = stage_items(l, part)
        grads = [by_name[_BIG[w]]() for w, _ in its]
        others, sems, token = _reduce_start(f"reduce_start_{part}_{l}", grads, after)
        reducing.append((l, part, its, grads, others, sems))
        return finish_reduce(token) if l == 0 else token

    gains = dict(mix=[norm_mix[l][None] for l in range(L)], ffn=[norm_ffn[l][None] for l in range(L)],
                 final=norm_final[None])
    loss_row, grad_x, dg_final = _local_step(x.reshape(S, D), loss_target.reshape(S, D), gains,
                                             layer_weights, layer_prefetch, layer_grads)

    rows = [*[G_small[l]["norm_mix"] for l in range(L)], *[G_small[l]["norm_ffn"] for l in range(L)],
            G_small[NA]["norm_kv"], dg_final, _pad_lanes(G_small[NA]["b_f"], D), _pad_lanes(loss_row, D)]
    for a in range(NA):
        rows += [G_small[a][n] for n in ("conv_w", "conv_b", "b_gi", "b_gr", "lru_param")]
    packed = jnp.concatenate(rows, axis=0)
    tot = _sum_slots("sum_small", _gather_all("gather_small", packed))
    loss = tot[2 * L + 3, 0]
    g_rep = jnp.concatenate([tot[:2 * L + 2], tot[2 * L + 2:2 * L + 3]], axis=0)
    base = 2 * L + 4
    g_sh = []
    for a in range(NA):
        blk = lax.dynamic_slice_in_dim(tot[base + 8 * a:base + 8 * a + 8], chip * CS, CS, axis=1)
        gi = tot[base + 8 * a + 5].reshape(NBLK, BW)
        gr = tot[base + 8 * a + 6].reshape(NBLK, BW)
        bl = lax.dynamic_slice_in_dim(jnp.concatenate([gi, gr], axis=1), chip * GS, GS, axis=1)
        g_sh += [blk[:5], bl.reshape(-1, CS), blk[7:8]]
    g_sh = jnp.concatenate(g_sh, axis=0)
    nrow = g_sh.shape[0] // NA

    def pack_rep(T):
        return jnp.concatenate([T["norm_mix"], T["norm_ffn"], T["norm_kv"][None], T["norm_final"][None],
                                _pad_lanes(T["b_forget"][None], D)], axis=0)

    def pack_sh(T):
        return jnp.concatenate([jnp.concatenate([T["conv_w"][a], T["conv_b"][a][None],
                                                 T["b_lru_gates"][a].reshape(-1, CS), T["lru_param"][a][None]], axis=0)
                                for a in range(NA)], axis=0)

    rep = _adamw("adamw_replicated", [g_rep], pack_rep(P), pack_rep(M1), pack_rep(M2))
    shd = _adamw("adamw_small_sharded", [g_sh], pack_sh(P), pack_sh(M1), pack_sh(M2))

    def unpack_rep(t):
        return dict(norm_mix=t[:L], norm_ffn=t[L:2 * L], norm_kv=t[2 * L], norm_final=t[2 * L + 1],
                    b_forget=t[2 * L + 2, :H])

    def unpack_sh(t):
        t = t.reshape(NA, nrow, CS)
        return dict(conv_w=t[:, :4], conv_b=t[:, 4], b_lru_gates=t[:, 5:nrow - 1].reshape(NA, NBLK, GS),
                    lru_param=t[:, nrow - 1])

    full = [lax.empty(sh.shape, F32) for sh in shards]
    for l, part in reversed(stages):
        parts, recv, sems = pending[(l, part)]
        recv = _scatter_wait(f"scatter_wait_{part}_{l}", parts, recv, sems)
        for (w, li), own, r in zip(stage_items(l, part), parts, recv):
            full[w] = _sum_chips(f"sum_chips_{l}_{_BIG[w]}", r, own, full[w], li, me_core)
    full = _share_d2d("share_d2d", full)
    big = {w: _adamw(f"adamw_{w}", [g.reshape(P[w].shape)], P[w], M1[w], M2[w]) for w, g in zip(_BIG, full)}

    outs = []
    for i in range(4):
        small = {**unpack_rep(rep[i]), **unpack_sh(shd[i])}
        outs.append([big[w][i] if w in big else small[w] for w in _WEIGHTS])
    return (loss, grad_x.reshape(1, S, D), *outs[0], *outs[1], *outs[2], *outs[3])
```

```python
import functools
import math

import jax
import jax.numpy as jnp
from jax import lax
from jax.experimental import pallas as pl
from jax.experimental.pallas import tpu as pltpu

F32 = jnp.float32
BF16 = jnp.bfloat16

EPS = 1e-6
LRU_C = 8.0
HEAD_DIM = 64
LANES = 128
SUBLANES = 8
VMEM_LIMIT = 48 * 1024 * 1024
N_CHIPS = 4
N_DEV = 8

ADAM_LR = 0.001
ADAM_B1 = 0.9
ADAM_B2 = 0.999
ADAM_EPS = 1e-08
ADAM_WD = 0.01
ADAM_STEP = 10

_NN = (((1,), (0,)), ((), ()))
_NT = (((1,), (1,)), ((), ()))
_TN = (((0,), (0,)), ((), ()))
_DN = {"nn": _NN, "nt": _NT, "tn": _TN}
MESH = pl.DeviceIdType.MESH


def _hbm_out(shape, dtype):
    return pltpu.HBM(shape, dtype)


def _params(sem):
    return pltpu.CompilerParams(dimension_semantics=sem, vmem_limit_bytes=VMEM_LIMIT)


def _tile(n, want):
    if n <= want:
        return n
    t = (want // LANES) * LANES
    while t >= LANES:
        if n % t == 0:
            return t
        t -= LANES
    return n


def _sigmoid(x):
    return 1.0 / (1.0 + jnp.exp(-x))


def _sigmoid_t(x):
    return 0.5 * jnp.tanh(0.5 * x) + 0.5


def _softplus(x):
    return jnp.maximum(x, 0.0) + jnp.log(1.0 + jnp.exp(-jnp.abs(x)))


_GELU_C = math.sqrt(2.0 / math.pi)


def _gelu_and_grad(x):
    inner = _GELU_C * (x + 0.044715 * x * x * x)
    t = jnp.tanh(inner)
    g = 0.5 * x * (1.0 + t)
    dg = 0.5 * (1.0 + t) + 0.5 * x * (1.0 - t * t) * _GELU_C * (1.0 + 3.0 * 0.044715 * x * x)
    return g, dg


def _rms(x):
    return lax.rsqrt(jnp.mean(x * x, axis=-1, keepdims=True) + EPS)


def _rms_bwd(dy, x, g):
    r = _rms(x)
    xr = x * r
    dyg = dy * g
    return r * dyg - xr * (r * jnp.mean(dyg * xr, axis=-1, keepdims=True)), jnp.sum(dy * xr, axis=0, keepdims=True)


def _mm(name, mode, a, b, *, grid, a_spec, b_spec, out_shape, out_dtype, out_spec, nk=1,
        res=None, res_spec=None, bias=None, bias_spec=None, scale=None, norm_gain=None, norm_bwd=None):
    dn = _DN[mode]
    has_res, has_bias = res is not None, bias is not None
    blk = tuple(d for d in out_spec.block_shape if d is not None)
    vec = pl.BlockSpec((1, blk[-1]), lambda *g: (0, 0))
    a_specs = a_spec if isinstance(a_spec, list) else [a_spec]
    b_specs = b_spec if isinstance(b_spec, list) else [b_spec]
    npair = len(a_specs)
    n_in = 2 * npair + int(has_res) + int(has_bias) + (1 if norm_gain is not None else 0) + (3 if norm_bwd else 0)

    def body(*refs):
        p = 2 * npair
        r_ref = refs[p] if has_res else None
        p += int(has_res)
        bias_ref = refs[p] if has_bias else None
        p += int(has_bias)
        extra = refs[p:n_in]
        outs = refs[n_in:]
        o_ref = outs[0]
        part = lax.dot_general(refs[0][...], refs[npair][...], dn, preferred_element_type=F32)
        for t in range(1, npair):
            part = part + lax.dot_general(refs[t][...], refs[npair + t][...], dn, preferred_element_type=F32)

        def finish(acc):
            if scale is not None:
                acc = acc * scale
            if has_bias:
                acc = acc + bias_ref[...]
            if has_res:
                acc = r_ref[...] + acc
            if norm_bwd:
                h_ref, g_ref, dh_ref = extra
                dx, dg = _rms_bwd(acc, h_ref[...], g_ref[...])
                acc = dh_ref[...] + dx
                outs[1][...] = acc.astype(BF16)
                outs[2][...] = dg
            if norm_gain is not None:
                outs[1][...] = (acc * _rms(acc) * extra[0][...]).astype(BF16)
            o_ref[...] = acc.astype(o_ref.dtype)

        if nk == 1:
            finish(part)
        else:
            acc_ref = refs[-1]
            k = pl.program_id(2)

            @pl.when(k == 0)
            def _():
                acc_ref[...] = part

            @pl.when(k > 0)
            def _():
                acc_ref[...] += part

            @pl.when(k == nk - 1)
            def _():
                finish(acc_ref[...])

    ins, specs = [a] * npair + [b] * npair, a_specs + b_specs
    if has_res:
        ins.append(res)
        specs.append(res_spec)
    if has_bias:
        ins.append(bias)
        specs.append(bias_spec)
    out_specs, out_shapes = [out_spec], [_hbm_out(out_shape, out_dtype)]
    if norm_gain is not None:
        ins.append(norm_gain)
        specs.append(vec)
        out_specs.append(out_spec)
        out_shapes.append(_hbm_out(out_shape, BF16))
    if norm_bwd:
        h, g, dh = norm_bwd
        ins += [h, g, dh]
        specs += [out_spec, vec, out_spec]
        out_specs += [out_spec, pl.BlockSpec((None, 1, blk[-1]), lambda i, *rest: (i, 0, 0))]
        out_shapes += [_hbm_out(out_shape, BF16), _hbm_out((grid[0], 1, blk[-1]), F32)]
    sem = ("parallel", "parallel") + (("arbitrary",) if len(grid) == 3 else ())
    single = len(out_specs) == 1
    return pl.pallas_call(
        body, name=name, grid=grid, in_specs=specs, out_specs=out_specs[0] if single else out_specs,
        out_shape=out_shapes[0] if single else out_shapes,
        scratch_shapes=[pltpu.VMEM(blk, F32)] if nk > 1 else [],
        compiler_params=_params(sem),
    )(*ins)


def _mm_nn(name, a, b, *, b_lead=(), out_dtype, tm=512, tn=512, res=None, bias=None, scale=None, norm_gain=None):
    M, K = a.shape
    N = b.shape[-1]
    tm, tn = _tile(M, tm), _tile(N, tn)
    nl = len(b_lead)
    return _mm(
        name, "nn", a, b, grid=(M // tm, N // tn),
        a_spec=pl.BlockSpec((tm, K), lambda i, j: (i, 0)),
        b_spec=pl.BlockSpec((None,) * nl + (K, tn), lambda i, j: tuple(b_lead) + (0, j)),
        out_shape=(M, N), out_dtype=out_dtype, out_spec=pl.BlockSpec((tm, tn), lambda i, j: (i, j)),
        res=res, res_spec=pl.BlockSpec((tm, tn), lambda i, j: (i, j)),
        bias=bias, bias_spec=pl.BlockSpec((1, tn), lambda i, j: (0, j)), scale=scale, norm_gain=norm_gain)


def _mm_nt(name, a, b, *, b_lead=(), out_dtype, tm=512, tn=512, tk=2048, res=None, norm_bwd=None):
    M, K = a.shape
    N = b.shape[-2]
    tm, tn, tk = _tile(M, tm), _tile(N, tn), _tile(K, tk)
    nk = K // tk
    nl = len(b_lead)
    return _mm(
        name, "nt", a, b, grid=(M // tm, N // tn, nk), nk=nk,
        a_spec=pl.BlockSpec((tm, tk), lambda i, j, k: (i, k)),
        b_spec=pl.BlockSpec((None,) * nl + (tn, tk), lambda i, j, k: tuple(b_lead) + (j, k)),
        out_shape=(M, N), out_dtype=out_dtype, out_spec=pl.BlockSpec((tm, tn), lambda i, j, k: (i, j)),
        res=res, res_spec=pl.BlockSpec((tm, tn), lambda i, j, k: (i, j)), norm_bwd=norm_bwd)


def _mm_tn(name, a, b, *, out_dtype, tm=512, tn=512):
    S, M = a.shape
    N = b.shape[1]
    tm, tn = _tile(M, tm), _tile(N, tn)
    return _mm(
        name, "tn", a, b, grid=(M // tm, N // tn),
        a_spec=pl.BlockSpec((S, tm), lambda i, j: (0, i)),
        b_spec=pl.BlockSpec((S, tn), lambda i, j: (0, j)),
        out_shape=(M, N), out_dtype=out_dtype, out_spec=pl.BlockSpec((tm, tn), lambda i, j: (i, j)))


def _rmsnorm_fwd(name, h, g, tr=256):
    S, D = h.shape
    tr = _tile(S, tr)

    def body(h_ref, g_ref, o_ref):
        x = h_ref[...]
        r = lax.rsqrt(jnp.mean(x * x, axis=-1, keepdims=True) + EPS)
        o_ref[...] = (x * r * g_ref[...]).astype(o_ref.dtype)

    return pl.pallas_call(
        body, name=name, grid=(S // tr,),
        in_specs=[pl.BlockSpec((tr, D), lambda i: (i, 0)), pl.BlockSpec((1, D), lambda i: (0, 0))],
        out_specs=pl.BlockSpec((tr, D), lambda i: (i, 0)),
        out_shape=_hbm_out((S, D), BF16),
        compiler_params=_params(("parallel",)),
    )(h, g)


def _loss_head(name, h, target, g, tr=256):
    S, D = h.shape
    tr = _tile(S, tr)

    def body(h_ref, t_ref, g_ref, o_ref, ob_ref, dg_ref, loss_ref):
        i = pl.program_id(0)
        x = h_ref[...]
        gg = g_ref[...]
        r = lax.rsqrt(jnp.mean(x * x, axis=-1, keepdims=True) + EPS)
        xr = x * r
        err = xr * gg - t_ref[...]
        lpart = 0.5 * jnp.sum(jnp.mean(err * err, axis=-1, keepdims=True), axis=0, keepdims=True)
        dy = err * (1.0 / D)
        dyg = dy * gg
        dx = r * dyg - xr * (r * jnp.mean(dyg * xr, axis=-1, keepdims=True))
        o_ref[...] = dx
        ob_ref[...] = dx.astype(BF16)
        part = jnp.sum(dy * xr, axis=0, keepdims=True)
        lrow = jnp.broadcast_to(lpart, (1, LANES))

        @pl.when(i == 0)
        def _():
            dg_ref[...] = part
            loss_ref[...] = lrow

        @pl.when(i > 0)
        def _():
            dg_ref[...] += part
            loss_ref[...] += lrow

    row = pl.BlockSpec((tr, D), lambda i: (i, 0))
    vec = pl.BlockSpec((1, D), lambda i: (0, 0))
    return pl.pallas_call(
        body, name=name, grid=(S // tr,),
        in_specs=[row, row, vec], out_specs=[row, row, vec, pl.BlockSpec((1, LANES), lambda i: (0, 0))],
        out_shape=[_hbm_out((S, D), F32), _hbm_out((S, D), BF16),
                   _hbm_out((1, D), F32), _hbm_out((1, LANES), F32)],
        compiler_params=_params(("arbitrary",)),
    )(h, target, g)


def _swiglu_fwd(name, hn, w_in, tm=512):
    S, D = hn.shape
    FH = w_in.shape[-1]
    tm = _tile(S, tm)

    def body(x_ref, wg_ref, wu_ref, z_ref, a_ref):
        x = x_ref[...]
        zg = jnp.dot(x, wg_ref[...], preferred_element_type=F32)
        zu = jnp.dot(x, wu_ref[...], preferred_element_type=F32)
        z_ref[0] = zg.astype(z_ref.dtype)
        z_ref[1] = zu.astype(z_ref.dtype)
        a_ref[...] = (zg * _sigmoid_t(zg) * zu).astype(a_ref.dtype)

    return pl.pallas_call(
        body, name=name, grid=(S // tm, 2),
        in_specs=[pl.BlockSpec((tm, D), lambda i, j: (i, 0)),
                  pl.BlockSpec((None, D, FH), lambda i, j: (j, 0, 0)),
                  pl.BlockSpec((None, D, FH), lambda i, j: (j + 2, 0, 0))],
        out_specs=[pl.BlockSpec((2, tm, FH), lambda i, j: (0, i, j)), pl.BlockSpec((tm, FH), lambda i, j: (i, j))],
        out_shape=[_hbm_out((2, S, 2 * FH), BF16), _hbm_out((S, 2 * FH), BF16)],
        compiler_params=_params(("parallel", "parallel")),
    )(hn, w_in, w_in)


def _swiglu_bwd(name, dhb, w_out, z3, tm=512):
    S, D = dhb.shape
    F = w_out.shape[0]
    FH = F // 2
    tm = _tile(S, tm)

    def body(d_ref, w_ref, z_ref, dz_ref):
        d = lax.dot_general(d_ref[...], w_ref[...], _NT, preferred_element_type=F32)
        zg = z_ref[0].astype(F32)
        zu = z_ref[1].astype(F32)
        sg = _sigmoid_t(zg)
        dz_ref[0] = (d * zu * (sg * (1.0 + zg * (1.0 - sg)))).astype(dz_ref.dtype)
        dz_ref[1] = (d * (zg * sg)).astype(dz_ref.dtype)

    zspec = pl.BlockSpec((2, tm, FH), lambda i, j: (0, i, j))
    return pl.pallas_call(
        body, name=name, grid=(S // tm, 2),
        in_specs=[pl.BlockSpec((tm, D), lambda i, j: (i, 0)), pl.BlockSpec((FH, D), lambda i, j: (j, 0)), zspec],
        out_specs=zspec, out_shape=_hbm_out((2, S, F), BF16),
        compiler_params=_params(("parallel", "parallel")),
    )(dhb, w_out, z3)


SCAN_ROWS = 64


def _group_scan(A, B, reverse):
    n = A.shape[0]
    sub = lax.broadcasted_iota(jnp.int32, A.shape, 0) % SUBLANES
    for d in (1, 2, 4):
        if reverse:
            A_sh, B_sh = pltpu.roll(A, n - d, 0), pltpu.roll(B, n - d, 0)
            keep = sub < SUBLANES - d
        else:
            A_sh, B_sh = pltpu.roll(A, d, 0), pltpu.roll(B, d, 0)
            keep = sub >= d
        B = jnp.where(keep, A * B_sh + B, B)
        A = jnp.where(keep, A * A_sh, A)
    return A, B


def _block_scan(a, u, carry, reverse):
    A, B = _group_scan(a, u, reverse)
    ng = a.shape[0] // SUBLANES
    out = [None] * ng
    order = range(ng - 1, -1, -1) if reverse else range(ng)
    for gi in order:
        sl = slice(gi * SUBLANES, (gi + 1) * SUBLANES)
        hg = A[sl] * carry + B[sl]
        out[gi] = hg
        carry = hg[0:1] if reverse else hg[SUBLANES - 1:SUBLANES]
    return jnp.concatenate(out, axis=0), carry


def _lru_gates(rc, gip, grp, sp):
    gi = _sigmoid_t(gip)
    gr = _sigmoid_t(grp)
    la = -LRU_C * gr * sp
    a = jnp.exp(la)
    om = -jnp.tanh(la) * (a * a + 1.0)
    mult = jnp.sqrt(om)
    return gi, gr, a, mult


def _lru_fwd(name, proj, rc, gip, grp, lru_p, tc=256):
    S, C = rc.shape
    tc = _tile(C, tc)
    nb = S // SCAN_ROWS

    def body(gb_ref, rc_ref, gi_ref, gr_ref, l_ref, h_ref, m_ref):
        sp = _softplus(-l_ref[...])

        def step(b, carry):
            rows = pl.ds(pl.multiple_of(b * SCAN_ROWS, SCAN_ROWS), SCAN_ROWS)
            rcb = rc_ref[rows, :]
            gi, _, a, mult = _lru_gates(rcb, gi_ref[rows, :], gr_ref[rows, :], sp)
            h, carry = _block_scan(a, rcb * gi * mult, carry, False)
            h_ref[rows, :] = h
            gel, _ = _gelu_and_grad(gb_ref[rows, :])
            m_ref[rows, :] = (gel * h).astype(m_ref.dtype)
            return carry

        lax.fori_loop(0, nb, step, jnp.zeros((1, tc), F32))

    col = pl.BlockSpec((S, tc), lambda j: (0, j))
    return pl.pallas_call(
        body, name=name, grid=(C // tc,),
        in_specs=[col, col, col, col, pl.BlockSpec((1, tc), lambda j: (0, j))],
        out_specs=[col, col],
        out_shape=[_hbm_out((S, C), F32), _hbm_out((S, C), BF16)],
        compiler_params=_params(("parallel",)),
    )(proj, rc, gip, grp, lru_p)


def _lru_bwd(name, dm, proj, hrec, rc, gip, grp, lru_p, tc=256):
    S, C = rc.shape
    tc = _tile(C, tc)
    nb = S // SCAN_ROWS
    R = SCAN_ROWS

    def body(dm_ref, gb_ref, h_ref, rc_ref, gi_ref, gr_ref, l_ref,
             dgb_ref, dgi_ref, dgr_ref, drc_ref, dbi_ref, dbr_ref, dl_ref):
        lp = l_ref[...]
        sp = _softplus(-lp)
        row = lax.broadcasted_iota(jnp.int32, (R, tc), 0)
        zero = jnp.zeros((1, tc), F32)

        def step(t, carry):
            mu_in, s_i, s_r, s_sp = carry
            b = nb - 1 - t
            r0 = pl.multiple_of(b * R, R)
            rows = pl.ds(r0, R)
            rcb = rc_ref[rows, :]
            gi, gr, a, mult = _lru_gates(rcb, gi_ref[rows, :], gr_ref[rows, :], sp)
            gel, dgel = _gelu_and_grad(gb_ref[rows, :])
            dmb = dm_ref[rows, :]
            h = h_ref[rows, :]
            dgb_ref[rows, :] = (dmb * h * dgel).astype(dgb_ref.dtype)
            dh = dmb * gel
            mu, mu_out = _block_scan(a, a * dh, mu_in, True)
            mu_next = jnp.where(row == R - 1, mu_in, pltpu.roll(mu, R - 1, 0))
            lam = dh + mu_next
            p0 = pl.multiple_of(jnp.maximum(r0 - SUBLANES, 0), SUBLANES)
            prev = h_ref[pl.ds(p0, SUBLANES), :][SUBLANES - 1:SUBLANES]
            prev = jnp.where(b > 0, prev, 0.0)
            h_prev = jnp.where(row == 0, prev, pltpu.roll(h, 1, 0))
            da = lam * h_prev
            d_mult = lam * rcb * gi
            d_la = da * a - d_mult * (a * a) / mult
            d_grp = d_la * (-LRU_C * sp) * gr * (1.0 - gr)
            d_gip = lam * rcb * mult * gi * (1.0 - gi)
            dgr_ref[rows, :] = d_grp.astype(dgr_ref.dtype)
            dgi_ref[rows, :] = d_gip.astype(dgi_ref.dtype)
            drc_ref[rows, :] = lam * gi * mult
            s_i = s_i + jnp.sum(d_gip, axis=0, keepdims=True)
            s_r = s_r + jnp.sum(d_grp, axis=0, keepdims=True)
            s_sp = s_sp + jnp.sum(d_la * gr, axis=0, keepdims=True)
            return mu_out, s_i, s_r, s_sp

        _, s_i, s_r, s_sp = lax.fori_loop(0, nb, step, (zero, zero, zero, zero))
        dbi_ref[...] = s_i
        dbr_ref[...] = s_r
        dl_ref[...] = (-LRU_C * s_sp) * (-_sigmoid(-lp))

    col = pl.BlockSpec((S, tc), lambda j: (0, j))
    vec = pl.BlockSpec((1, tc), lambda j: (0, j))
    return pl.pallas_call(
        body, name=name, grid=(C // tc,),
        in_specs=[col, col, col, col, col, col, vec],
        out_specs=[col, col, col, col, vec, vec, vec],
        out_shape=[_hbm_out((S, C), BF16), _hbm_out((S, C), BF16),
                   _hbm_out((S, C), BF16), _hbm_out((S, C), F32),
                   _hbm_out((1, C), F32), _hbm_out((1, C), F32),
                   _hbm_out((1, C), F32)],
        compiler_params=_params(("parallel",)),
    )(dm, proj, hrec, rc, gip, grp, lru_p)


def _cumsum_rows(name, u, reverse):
    S, C = u.shape
    nb = S // SCAN_ROWS

    def body(u_ref, o_ref):
        def step(t, carry):
            b = nb - 1 - t if reverse else t
            rows = pl.ds(pl.multiple_of(b * SCAN_ROWS, SCAN_ROWS), SCAN_ROWS)
            ub = u_ref[rows, :]
            h, carry = _block_scan(jnp.ones_like(ub), ub, carry, reverse)
            o_ref[rows, :] = h
            return carry

        lax.fori_loop(0, nb, step, jnp.zeros((1, C), F32))

    spec = pl.BlockSpec((S, C), lambda i: (0, 0))
    return pl.pallas_call(
        body, name=name, grid=(1,), in_specs=[spec], out_specs=spec,
        out_shape=_hbm_out((S, C), F32),
        compiler_params=_params(("arbitrary",)),
    )(u)


def _shift_down(x, k):
    row = lax.broadcasted_iota(jnp.int32, x.shape, 0)
    return jnp.where(row >= k, pltpu.roll(x, k, 0), 0.0)


def _shift_up(x, k):
    n = x.shape[0]
    row = lax.broadcasted_iota(jnp.int32, x.shape, 0)
    return jnp.where(row < n - k, pltpu.roll(x, n - k, 0), 0.0)


def _conv_fwd(name, proj, w, b, tc=256):
    S, C2 = proj.shape
    C = C2 // 2
    tc = _tile(C, tc)
    off = C // tc

    def body(x_ref, w_ref, b_ref, o_ref, ob_ref):
        x = x_ref[...]
        out = b_ref[...] + w_ref[3:4, :] * x
        for k in (1, 2, 3):
            out = out + w_ref[3 - k:4 - k, :] * _shift_down(x, k)
        o_ref[...] = out
        ob_ref[...] = out.astype(BF16)

    col = pl.BlockSpec((S, tc), lambda j: (0, j))
    return pl.pallas_call(
        body, name=name, grid=(C // tc,),
        in_specs=[pl.BlockSpec((S, tc), lambda j: (0, off + j)),
                  pl.BlockSpec((4, tc), lambda j: (0, j)), pl.BlockSpec((1, tc), lambda j: (0, j))],
        out_specs=[col, col],
        out_shape=[_hbm_out((S, C), F32), _hbm_out((S, C), BF16)],
        compiler_params=_params(("parallel",)),
    )(proj, w, b)


def _conv_bwd(name, drc, proj, w, tc=256):
    S, C = drc.shape
    tc = _tile(C, tc)
    off = C // tc

    def body(y_ref, x_ref, w_ref, dx_ref, dw_ref, db_ref):
        y = y_ref[...]
        x = x_ref[...]
        dx = w_ref[3:4, :] * y
        dw_ref[3:4, :] = jnp.sum(y * x, axis=0, keepdims=True)
        for k in (1, 2, 3):
            dx = dx + w_ref[3 - k:4 - k, :] * _shift_up(y, k)
            dw_ref[3 - k:4 - k, :] = jnp.sum(y * _shift_down(x, k), axis=0, keepdims=True)
        dx_ref[...] = dx.astype(dx_ref.dtype)
        db_ref[...] = jnp.sum(y, axis=0, keepdims=True)

    col = pl.BlockSpec((S, tc), lambda j: (0, j))
    return pl.pallas_call(
        body, name=name, grid=(C // tc,),
        in_specs=[col, pl.BlockSpec((S, tc), lambda j: (0, off + j)), pl.BlockSpec((4, tc), lambda j: (0, j))],
        out_specs=[col, pl.BlockSpec((4, tc), lambda j: (0, j)), pl.BlockSpec((1, tc), lambda j: (0, j))],
        out_shape=[_hbm_out((S, C), BF16), _hbm_out((4, C), F32),
                   _hbm_out((1, C), F32)],
        compiler_params=_params(("parallel",)),
    )(drc, proj, w)


def _gates_fwd(name, rcb, wg, bg):
    S, C = rcb.shape
    nblk, bw, _ = wg.shape

    def body(x_ref, w_ref, b_ref, gi_ref, gr_ref):
        g = jnp.dot(x_ref[...], w_ref[...], preferred_element_type=F32) + b_ref[...]
        gi_ref[...] = g[:, :bw]
        gr_ref[...] = g[:, bw:]

    col = pl.BlockSpec((S, bw), lambda n: (0, n))
    return pl.pallas_call(
        body, name=name, grid=(nblk,),
        in_specs=[col, pl.BlockSpec((None, bw, 2 * bw), lambda n: (n, 0, 0)),
                  pl.BlockSpec((None, 1, 2 * bw), lambda n: (n, 0, 0))],
        out_specs=[col, col],
        out_shape=[_hbm_out((S, C), F32), _hbm_out((S, C), F32)],
        compiler_params=_params(("parallel",)),
    )(rcb, wg, bg)


def _gates_bwd(name, dgi, dgr, rcb, wg, drc1):
    S, C = rcb.shape
    nblk, bw, _ = wg.shape

    def body(dgi_ref, dgr_ref, x_ref, w_ref, d1_ref, drc_ref, dw_ref):
        w = w_ref[...]
        x = x_ref[...]
        di, dr = dgi_ref[...], dgr_ref[...]
        drc_ref[...] = (d1_ref[...]
                        + lax.dot_general(di, w[:, :bw], _NT, preferred_element_type=F32)
                        + lax.dot_general(dr, w[:, bw:], _NT, preferred_element_type=F32))
        dw_ref[:, :bw] = lax.dot_general(x, di, _TN, preferred_element_type=F32).astype(dw_ref.dtype)
        dw_ref[:, bw:] = lax.dot_general(x, dr, _TN, preferred_element_type=F32).astype(dw_ref.dtype)

    col = pl.BlockSpec((S, bw), lambda n: (0, n))
    wspec = pl.BlockSpec((None, bw, 2 * bw), lambda n: (n, 0, 0))
    return pl.pallas_call(
        body, name=name, grid=(nblk,),
        in_specs=[col, col, col, wspec, col], out_specs=[col, wspec],
        out_shape=[_hbm_out((S, C), F32), _hbm_out((nblk, bw, 2 * bw), BF16)],
        compiler_params=_params(("parallel",)),
    )(dgi, dgr, rcb, wg, drc1)


def _att_tile(S):
    return next(t for t in (512, 256, 128) if S % t == 0)


def _head_lanes(shape):
    return lax.broadcasted_iota(jnp.int32, shape, len(shape) - 1) < HEAD_DIM


def _key_bias(c_blk):
    first = _head_lanes(c_blk.shape)
    rolled = pltpu.roll(c_blk, HEAD_DIM, 1)
    return jnp.where(first, c_blk, rolled), jnp.where(first, rolled, c_blk)


def _over_keys(x, op):
    n = x.shape[0]
    while n > SUBLANES:
        n //= 2
        x = op(x[:n], x[n:2 * n])
    return (jnp.max if op is jnp.maximum else jnp.sum)(x, axis=0, keepdims=True)


def _causal_t(T, cc):
    r = lax.broadcasted_iota(jnp.int32, (T, LANES), 0)
    c = lax.broadcasted_iota(jnp.int32, (T, LANES), 1) + cc * LANES
    return r <= c


def _attn_fwd(name, q, kv, cfull):
    S, D = q.shape
    HP = D // LANES
    T = _att_tile(S)
    nq = S // T
    NC = T // LANES

    def body(q_ref, k_ref, v_ref, c_ref, o_ref, of_ref, lse_ref, bias, vT, acc, m_scr, l_scr):
        def prologue(i, _):
            rows = pl.ds(pl.multiple_of(i * T, T), T)
            bias[0, rows, :], bias[1, rows, :] = _key_bias(c_ref[rows, :])
            vT[i] = v_ref[rows, :].astype(F32).T.astype(BF16)
            return 0

        lax.fori_loop(0, nq, prologue, 0)

        def q_step(qi, _):
            q0 = pl.multiple_of(qi * T, T)
            qb = q_ref[pl.ds(q0, T), :]
            m_scr[...] = jnp.full(m_scr.shape, -jnp.inf, F32)
            l_scr[...] = jnp.zeros(l_scr.shape, F32)
            acc[...] = jnp.zeros(acc.shape, F32)

            def tile(kj, masked):
                ks = pl.ds(pl.multiple_of(kj * T, T), T)
                kf = k_ref[ks, :].astype(F32)
                first = _head_lanes(kf.shape)
                kms = [jnp.where(first if hh == 0 else jnp.logical_not(first), kf, 0.0).astype(BF16) for hh in range(2)]
                sTs = [lax.dot_general(km, qb, _NT, preferred_element_type=F32) for km in kms]
                for hh in range(2):
                    b = bias[hh, ks, :]
                    ps = []
                    for cc in range(NC):
                        cols = slice(cc * LANES, (cc + 1) * LANES)
                        s = sTs[hh][:, cols] + b
                        if masked:
                            s = jnp.where(_causal_t(T, cc), s, -jnp.inf)
                        m_old = m_scr[hh, cc]
                        m_new = jnp.maximum(m_old, _over_keys(s, jnp.maximum))
                        alpha = jnp.exp(m_old - m_new)
                        p = jnp.exp(s - m_new)
                        l_scr[hh, cc] = alpha * l_scr[hh, cc] + _over_keys(p, jnp.add)
                        m_scr[hh, cc] = m_new
                        ps.append(p.astype(BF16))
                        acc[hh, :, cols] = acc[hh, :, cols] * alpha
                    acc[hh] += jnp.dot(vT[kj, hh * HEAD_DIM:(hh + 1) * HEAD_DIM, :], jnp.concatenate(ps, axis=1),
                                       preferred_element_type=F32)

            def inner(kj, _):
                tile(kj, False)
                return 0

            lax.fori_loop(0, qi, inner, 0)
            tile(qi, True)
            outs = []
            for hh in range(2):
                inv = jnp.concatenate([1.0 / l_scr[hh, cc] for cc in range(NC)], axis=1)
                outs.append(acc[hh] * inv)
                for cc in range(NC):
                    lse_ref[hh:hh + 1, pl.ds(q0 + cc * LANES, LANES)] = m_scr[hh, cc] + jnp.log(l_scr[hh, cc])
            out = jnp.concatenate(outs, axis=0).T
            o_ref[pl.ds(q0, T), :] = out.astype(o_ref.dtype)
            of_ref[pl.ds(q0, T), :] = out
            return 0

        lax.fori_loop(0, nq, q_step, 0)

    blk = lambda off: pl.BlockSpec((S, LANES), lambda p: (0, off + p))
    return pl.pallas_call(
        body, name=name, grid=(HP,),
        in_specs=[blk(0), blk(0), blk(HP), blk(0)],
        out_specs=[blk(0), blk(0), pl.BlockSpec((None, 2, S), lambda p: (p, 0, 0))],
        out_shape=[_hbm_out((S, D), BF16), _hbm_out((S, D), F32),
                   _hbm_out((HP, 2, S), F32)],
        scratch_shapes=[pltpu.VMEM((2, S, LANES), F32), pltpu.VMEM((nq, LANES, T), BF16),
                        pltpu.VMEM((2, HEAD_DIM, T), F32), pltpu.VMEM((2, NC, 1, LANES), F32),
                        pltpu.VMEM((2, NC, 1, LANES), F32)],
        compiler_params=_params(("parallel",)),
    )(q, kv, kv, cfull)


def _attn_bwd(name, q, kv, cfull, of, do, lse3):
    S, D = q.shape
    HP = D // LANES
    T = _att_tile(S)
    nq = S // T
    NC = T // LANES
    scale = HEAD_DIM ** -0.5

    def body(q_ref, k_ref, v_ref, c_ref, of_ref, do_ref, lse_ref,
             dq_ref, dk_ref, dv_ref, dck_ref, drq_ref, bias, kT, dqT, delta, dr_scr):
        def prologue(i, _):
            rows = pl.ds(pl.multiple_of(i * T, T), T)
            bias[0, rows, :], bias[1, rows, :] = _key_bias(c_ref[rows, :])
            kT[i] = k_ref[rows, :].astype(F32).T.astype(BF16)
            prodT = (do_ref[rows, :].astype(F32) * of_ref[rows, :]).T
            for hh in range(2):
                delta[hh:hh + 1, rows] = jnp.sum(prodT[hh * HEAD_DIM:(hh + 1) * HEAD_DIM], axis=0, keepdims=True)
            dqT[i] = jnp.zeros((LANES, T), F32)
            return 0

        lax.fori_loop(0, nq, prologue, 0)
        dr_scr[...] = jnp.zeros(dr_scr.shape, F32)

        def kv_step(kj, _):
            ks = pl.ds(pl.multiple_of(kj * T, T), T)
            kf = k_ref[ks, :].astype(F32)
            vf = v_ref[ks, :].astype(F32)
            first = _head_lanes(kf.shape)
            masks = [first, jnp.logical_not(first)]
            kms = [jnp.where(m, kf, 0.0).astype(BF16) for m in masks]
            vms = [jnp.where(m, vf, 0.0).astype(BF16) for m in masks]

            def tile(qi, carry, masked):
                q0 = pl.multiple_of(qi * T, T)
                qb = q_ref[pl.ds(q0, T), :]
                dob = do_ref[pl.ds(q0, T), :]
                sTs = [lax.dot_general(km, qb, _NT, preferred_element_type=F32) for km in kms]
                dpTs = [lax.dot_general(vm, dob, _NT, preferred_element_type=F32) for vm in vms]
                out = []
                for hh in range(2):
                    dk_a, dv_a, dc_a = carry[3 * hh:3 * hh + 3]
                    b = bias[hh, ks, :]
                    head = slice(hh * HEAD_DIM, (hh + 1) * HEAD_DIM)
                    ps, dss = [], []
                    for cc in range(NC):
                        cols = slice(cc * LANES, (cc + 1) * LANES)
                        at = pl.ds(q0 + cc * LANES, LANES)
                        p = jnp.exp(sTs[hh][:, cols] + b - lse_ref[hh:hh + 1, at])
                        if masked:
                            p = jnp.where(_causal_t(T, cc), p, 0.0)
                        ds = p * (dpTs[hh][:, cols] - delta[hh:hh + 1, at])
                        ps.append(p.astype(BF16))
                        dss.append(ds.astype(BF16))
                        dc_a = dc_a + ds
                        dr_scr[hh:hh + 1, at] += _over_keys(ds, jnp.add)
                    pT = jnp.concatenate(ps, axis=1)
                    dsT = jnp.concatenate(dss, axis=1)
                    dv_a = dv_a + jnp.dot(pT, dob, preferred_element_type=F32)
                    dk_a = dk_a + jnp.dot(dsT, qb, preferred_element_type=F32)
                    dqT[qi, head, :] += jnp.dot(kT[kj, head, :], dsT, preferred_element_type=F32)
                    out += [dk_a, dv_a, dc_a]
                return tuple(out)

            zero = jnp.zeros((T, LANES), F32)
            carry = tile(kj, (zero,) * 6, True)
            dk0, dv0, dc0, dk1, dv1, dc1 = lax.fori_loop(kj + 1, nq, lambda qi, c: tile(qi, c, False), carry)
            dk_ref[ks, :] = jnp.where(first, dk0, dk1)
            dv_ref[ks, :] = jnp.where(first, dv0, dv1)
            dck_ref[ks, :] = jnp.where(first, jnp.broadcast_to(-jnp.sum(dc0, axis=1, keepdims=True), (T, LANES)),
                                       jnp.broadcast_to(-jnp.sum(dc1, axis=1, keepdims=True), (T, LANES)))
            return 0

        lax.fori_loop(0, nq, kv_step, 0)

        def epilogue(i, _):
            rows = pl.ds(pl.multiple_of(i * T, T), T)
            dq_ref[rows, :] = (dqT[i].T * scale).astype(dq_ref.dtype)
            return 0

        lax.fori_loop(0, nq, epilogue, 0)
        drq_ref[...] = dr_scr[...]

    blk = lambda off: pl.BlockSpec((S, LANES), lambda p: (0, off + p))
    row_spec = pl.BlockSpec((None, 2, S), lambda p: (p, 0, 0))
    return pl.pallas_call(
        body, name=name, grid=(HP,),
        in_specs=[blk(0), blk(0), blk(HP), blk(0), blk(0), blk(0), row_spec],
        out_specs=[blk(0), blk(0), blk(0), blk(0), row_spec],
        out_shape=[_hbm_out((S, D), BF16), _hbm_out((S, D), F32),
                   _hbm_out((S, D), F32), _hbm_out((S, D), F32),
                   _hbm_out((HP, 2, S), F32)],
        scratch_shapes=[pltpu.VMEM((2, S, LANES), F32), pltpu.VMEM((nq, LANES, T), BF16),
                        pltpu.VMEM((nq, LANES, T), F32), pltpu.VMEM((2, S), F32), pltpu.VMEM((2, S), F32)],
        compiler_params=_params(("parallel",)),
    )(q, kv, kv, cfull, of, do, lse3)


def _logsig_fwd(name, f):
    S, C = f.shape

    def body(f_ref, o_ref):
        o_ref[...] = -_softplus(-f_ref[...])

    spec = pl.BlockSpec((S, C), lambda i: (0, 0))
    return pl.pallas_call(body, name=name, grid=(1,), in_specs=[spec], out_specs=spec,
                          out_shape=_hbm_out((S, C), F32),
                          compiler_params=_params(("arbitrary",)))(f)


def _logsig_bwd(name, dls, f):
    S, C = f.shape

    def body(d_ref, f_ref, o_ref, s_ref):
        df = d_ref[...] * _sigmoid(-f_ref[...])
        o_ref[...] = df.astype(o_ref.dtype)
        s_ref[...] = jnp.sum(df, axis=0, keepdims=True)

    spec = pl.BlockSpec((S, C), lambda i: (0, 0))
    return pl.pallas_call(body, name=name, grid=(1,), in_specs=[spec, spec],
                          out_specs=[spec, pl.BlockSpec((1, C), lambda i: (0, 0))],
                          out_shape=[_hbm_out((S, C), BF16), _hbm_out((1, C), F32)],
                          compiler_params=_params(("arbitrary",)))(dls, f)


def _add_cast(name, parts, out_dtype, tr=256):
    S, C = parts[0].shape
    tr = _tile(S, tr)
    n = len(parts)

    def body(*refs):
        acc = refs[0][...].astype(F32)
        for r in refs[1:n]:
            acc = acc + r[...].astype(F32)
        refs[n][...] = acc.astype(out_dtype)

    spec = pl.BlockSpec((tr, C), lambda i: (i, 0))
    return pl.pallas_call(body, name=name, grid=(S // tr,), in_specs=[spec] * n, out_specs=spec,
                          out_shape=_hbm_out((S, C), out_dtype),
                          compiler_params=_params(("parallel",)))(*parts)


def _local_step(x, target, gains, layer_weights, layer_prefetch, layer_grads):
    S, D = x.shape
    HP = D // LANES
    scale = HEAD_DIM ** -0.5
    tm = _tile(S, 512)
    tx = _tile(S, 256)
    td = _tile(D, 512)
    saved = []
    h = x
    l = 0
    kv = cfull = f_pre = hn_kv = h_kv = None
    while True:
        W = layer_weights(l, "mix", h)
        if W is None:
            break
        recurrent = "w_rec_in" in W
        if l == 0:
            xn = _rmsnorm_fwd("mix_norm_0", h, gains["mix"][0])
        if recurrent:
            CH = W["w_rec_in"].shape[-1]
            C = 2 * CH
            proj = _mm(f"rec_in_{l}", "nn", xn, W["w_rec_in"], grid=(S // tm, N_CHIPS),
                       a_spec=pl.BlockSpec((tm, D), lambda i, j: (i, 0)),
                       b_spec=pl.BlockSpec((None, D, CH), lambda i, j: (j, 0, 0)),
                       out_shape=(S, 2 * C), out_dtype=F32,
                       out_spec=pl.BlockSpec((tm, CH), lambda i, j: (i, j)))
            layer_prefetch(l, "mix2", proj)
            rc, rcb = _conv_fwd(f"conv_{l}", proj, W["conv_w"], W["conv_b"])
            W = {**W, **layer_weights(l, "mix2", rcb)}
            gip, grp = _gates_fwd(f"gates_{l}", rcb, W["w_gates"], W["b_gates"])
            hrec, m = _lru_fwd(f"lru_{l}", proj, rc, gip, grp, W["lru_param"])
            layer_prefetch(l, "ffn", m)
            h_mid, hn = _mm_nn(f"rec_out_{l}", m, W["w_rec_out"], out_dtype=F32, res=h, tn=D, norm_gain=gains["ffn"][l])
            mix_saved = (xn, proj, rc, rcb, gip, grp, hrec, m)
        else:
            if "w_kv" in W:
                h_kv = h
                hn_kv = _rmsnorm_fwd("kv_norm", h, W["norm_kv"])
                kv = _mm_nn("kv_proj", hn_kv, W["w_kv"], out_dtype=BF16)
                f_pre = _mm_nn("f_proj", hn_kv, W["w_f"], out_dtype=F32, bias=W["b_f"])
                c = _cumsum_rows("c_cumsum", _logsig_fwd("logsig", f_pre), False)
                cfull = jnp.repeat(-c[:, :2 * HP], HEAD_DIM, axis=1)
            q = _mm_nn(f"q_proj_{l}", xn, W["w_q"], out_dtype=BF16, scale=scale)
            layer_prefetch(l, "mix2", q)
            o, of, lse = _attn_fwd(f"attn_fwd_{l}", q, kv, cfull)
            W = {**W, **layer_weights(l, "mix2", o)}
            layer_prefetch(l, "ffn", o)
            h_mid, hn = _mm_nn(f"o_proj_{l}", o, W["w_o"], out_dtype=F32, res=h, tn=D, norm_gain=gains["ffn"][l])
            mix_saved = (xn, q, o, of, lse)
        W = {**W, **layer_weights(l, "ffn", h_mid)}
        z3, act = _swiglu_fwd(f"ffn_in_{l}", hn, W["w_ffn_in"])
        layer_prefetch(l + 1, "mix", act)
        saved.append((W, h, h_mid, mix_saved, (hn, z3, act)))
        l += 1
        if l < len(gains["mix"]):
            h, xn = _mm_nn(f"ffn_out_{l - 1}", act, W["w_ffn_out"], out_dtype=F32, res=h_mid, tn=D,
                           norm_gain=gains["mix"][l])
        else:
            h = _mm_nn(f"ffn_out_{l - 1}", act, W["w_ffn_out"], out_dtype=F32, res=h_mid, tn=D)

    dh, dhb, dg_final, loss_row = _loss_head("loss_head", h, target, gains["final"])

    dk_parts, dv_parts, dc_parts = [], [], []
    token = None
    for l in reversed(range(len(saved))):
        W, h_in, h_mid, mix_saved, (hn, z3, act) = saved[l]
        recurrent = "w_rec_in" in W
        FH = W["w_ffn_in"].shape[-1]
        G = {}
        norm_ffn = gains["ffn"][l]
        if token is not None:
            norm_ffn = norm_ffn + jnp.minimum(token[:1, :1], 0.0)
        G["w_ffn_out"] = _mm_tn(f"d_ffn_out_{l}", act, dhb, out_dtype=BF16, tn=D)
        dz3 = _swiglu_bwd(f"d_act_{l}", dhb, W["w_ffn_out"], z3)
        G["w_ffn_in"] = _mm(
            f"d_ffn_in_{l}", "tn", hn, dz3, grid=(D // td, N_CHIPS),
            a_spec=pl.BlockSpec((S, td), lambda i, j: (0, i)),
            b_spec=pl.BlockSpec((None, S, FH), lambda i, j: (j // 2, 0, j % 2)),
            out_shape=(N_CHIPS, D, FH), out_dtype=BF16,
            out_spec=pl.BlockSpec((None, td, FH), lambda i, j: (j, i, 0)))
        ffn_token = layer_grads(l, "ffn", G)
        G = {}
        if ffn_token is not None:
            norm_ffn = norm_ffn + jnp.minimum(ffn_token[:1, :1], 0.0)
        dh, dhb, dgp = _mm(f"d_ffn_hn_{l}", "nt", dz3, W["w_ffn_in"], grid=(S // tx, 1),
                           a_spec=[pl.BlockSpec((None, tx, FH), functools.partial(lambda i, j, k: (k // 2, i, k % 2), k=k))
                                   for k in range(N_CHIPS)],
                           b_spec=[pl.BlockSpec((None, D, FH), functools.partial(lambda i, j, k: (k, 0, 0), k=k))
                                   for k in range(N_CHIPS)],
                           out_shape=(S, D), out_dtype=F32, out_spec=pl.BlockSpec((tx, D), lambda i, j: (i, 0)),
                           norm_bwd=(h_mid, norm_ffn, dh))
        G["norm_ffn"] = jnp.sum(dgp, axis=0)
        if recurrent:
            CH = W["w_rec_in"].shape[-1]
            C = 2 * CH
            xn, proj, rc, rcb, gip, grp, hrec, m = mix_saved
            G["w_rec_out"] = _mm_tn(f"d_rec_out_{l}", m, dhb, out_dtype=BF16, tn=D)
            dm = _mm_nt(f"d_m_{l}", dhb, W["w_rec_out"], out_dtype=F32, tn=C)
            dgb, dgi, dgr, drc1, G["b_gi"], G["b_gr"], G["lru_param"] = _lru_bwd(
                f"d_lru_{l}", dm, proj, hrec, rc, gip, grp, W["lru_param"])
            drc, G["w_gates"] = _gates_bwd(f"d_gates_{l}", dgi, dgr, rcb, W["w_gates"], drc1)
            mix_token = layer_grads(l, "mix2", {n: G[n] for n in ("w_rec_out", "w_gates")})
            drec, G["conv_w"], G["conv_b"] = _conv_bwd(f"d_conv_{l}", drc, proj, W["conv_w"])
            dproj = jnp.concatenate([dgb, drec], axis=1)
            norm_mix = gains["mix"][l] if mix_token is None else gains["mix"][l] + jnp.minimum(mix_token[:1, :1], 0.0)
            G["w_rec_in"] = _mm(
                f"d_rec_in_{l}", "tn", xn, dproj, grid=(1, N_CHIPS),
                a_spec=pl.BlockSpec((S, D), lambda i, j: (0, 0)),
                b_spec=pl.BlockSpec((S, CH), lambda i, j: (0, j)),
                out_shape=(N_CHIPS, D, CH), out_dtype=BF16,
                out_spec=pl.BlockSpec((None, D, CH), lambda i, j: (j, 0, 0)))
            dh, dhb, dgp = _mm(f"d_rec_xn_{l}", "nt", dproj, W["w_rec_in"], grid=(S // tx, 1),
                               a_spec=[pl.BlockSpec((tx, CH), functools.partial(lambda i, j, k: (i, k), k=k))
                                       for k in range(N_CHIPS)],
                               b_spec=[pl.BlockSpec((None, D, CH), functools.partial(lambda i, j, k: (k, 0, 0), k=k))
                                       for k in range(N_CHIPS)],
                               out_shape=(S, D), out_dtype=F32, out_spec=pl.BlockSpec((tx, D), lambda i, j: (i, 0)),
                               norm_bwd=(h_in, norm_mix, dh))
        else:
            xn, q, o, of, lse = mix_saved
            G["w_o"] = _mm_tn(f"d_o_proj_{l}", o, dhb, out_dtype=BF16, tn=D)
            do = _mm_nt(f"d_o_{l}", dhb, W["w_o"], out_dtype=BF16, tn=D)
            mix_token = layer_grads(l, "mix2", {"w_o": G["w_o"]})
            dq, dk, dv, dck, drq = _attn_bwd(f"attn_bwd_{l}", q, kv, cfull, of, do, lse)
            dk_parts.append(dk)
            dv_parts.append(dv)
            dc_parts.append(dck[:, ::HEAD_DIM] + drq.reshape(2 * HP, S).T)
            G["w_q"] = _mm_tn(f"d_q_proj_{l}", xn, dq, out_dtype=BF16, tn=D)
            norm_mix = gains["mix"][l] if mix_token is None else gains["mix"][l] + jnp.minimum(mix_token[:1, :1], 0.0)
            dh, dhb, dgp = _mm_nt(f"d_q_xn_{l}", dq, W["w_q"], out_dtype=F32, tn=D, norm_bwd=(h_in, norm_mix, dh))
        G["norm_mix"] = jnp.sum(dgp, axis=0)
        if "w_kv" in W:
            dkb = _add_cast("dk_sum", dk_parts, BF16)
            dvb = _add_cast("dv_sum", dv_parts, BF16)
            dkv = jnp.concatenate([dkb, dvb], axis=1)
            dc = sum(dc_parts[1:], dc_parts[0])
            dc_pad = jnp.pad(dc, ((0, 0), (0, LANES - 2 * HP)))
            dls = _cumsum_rows("dc_cumsum", dc_pad, True)
            dfb, G["b_f"] = _logsig_bwd("d_logsig", dls, f_pre)
            G["w_kv"] = _mm_tn("d_kv_proj", hn_kv, dkv, out_dtype=BF16)
            G["w_f"] = _mm_tn("d_f_proj", hn_kv, dfb, out_dtype=F32)
            dhn_f = _mm_nt("d_f_hn", dfb, W["w_f"], out_dtype=F32, tn=D)
            dh, dhb, dgp = _mm_nt("d_kv_hn", dkv, W["w_kv"], out_dtype=F32, tn=D, res=dhn_f,
                                  norm_bwd=(h_kv, W["norm_kv"], dh))
            G["norm_kv"] = jnp.sum(dgp, axis=0)
        token = layer_grads(l, "mix", G)
    return loss_row, dh, dg_final


_ANY = pl.BlockSpec(memory_space=pl.ANY)


def _position():
    return lax.axis_index("x"), lax.axis_index("y"), lax.axis_index("c")


def _chip_peers(x, y):
    return [(1 - x, y), (x, 1 - y), (1 - x, 1 - y)]


def _half_rows(c, n):
    h = n // 2
    assert h % 16 == 0
    return pl.ds(pl.multiple_of(c * h, 16), h)


def _place_own(name, shard, layer, me):
    _, R, C = shard.shape
    tr = _row_tile(R, C, 2 * shard.dtype.itemsize, target=8 << 20)

    def body(me_ref, x_ref, o_ref):
        o_ref[...] = x_ref[...]

    return pl.pallas_call(
        body, name=name,
        grid_spec=pltpu.PrefetchScalarGridSpec(
            num_scalar_prefetch=1, grid=(R // tr,),
            in_specs=[pl.BlockSpec((None, tr, C), lambda i, me_ref: (layer, i, 0))],
            out_specs=pl.BlockSpec((None, tr, C), lambda i, me_ref: (me_ref[0], i, 0))),
        out_shape=_hbm_out((N_CHIPS, R, C), shard.dtype),
        compiler_params=_params(("parallel",)),
    )(me, shard)


def _gather_smalls(name, smalls):
    ns = len(smalls)

    def body(*refs):
        ins, outs = refs[:ns], refs[ns:2 * ns]
        send_sems, recv_sems, local_sems = refs[2 * ns:]
        x, y, c = _position()
        me = 2 * x + y
        peers = _chip_peers(x, y)

        def remote(t, k, chip):
            px, py = peers[k]
            return pltpu.make_async_remote_copy(
                src_ref=ins[t], dst_ref=outs[t].at[chip], send_sem=send_sems.at[3 * t + k],
                recv_sem=recv_sems.at[3 * t + k], device_id=(px, py, c), device_id_type=MESH)

        local = [pltpu.make_async_copy(ins[t], outs[t].at[me], local_sems.at[t]) for t in range(ns)]
        for t in range(ns):
            local[t].start()
            for k in range(3):
                remote(t, k, me).start()
        for t in range(ns):
            for k in range(3):
                px, py = peers[k]
                remote(t, k, 2 * px + py).wait_recv()
        for t in range(ns):
            for k in range(3):
                remote(t, k, me).wait_send()
            local[t].wait()

    return pl.pallas_call(
        body, name=name, in_specs=[_ANY] * ns, out_specs=[_ANY] * ns,
        out_shape=[_hbm_out((N_CHIPS,) + s.shape, s.dtype) for s in smalls],
        scratch_shapes=[pltpu.SemaphoreType.DMA((3 * ns,)), pltpu.SemaphoreType.DMA((3 * ns,)),
                        pltpu.SemaphoreType.DMA((ns,))],
    )(*smalls)


_SEM = pl.BlockSpec(memory_space=pltpu.SEMAPHORE)
_SPLIT = pltpu.CompilerParams(has_side_effects=pltpu.SideEffectType.DATAFLOW_SIDE_EFFECTING)


def _weight_copy(shards, buf, items, sems, i, k, chip_of_dst, peers, c):
    w, l = items[i]
    px, py = peers[k]
    half = _half_rows(c, shards[w].shape[1])
    return pltpu.make_async_remote_copy(
        src_ref=shards[w].at[l, half], dst_ref=buf.at[chip_of_dst, half],
        send_sem=sems[0].at[3 * i + k], recv_sem=sems[1].at[3 * i + k],
        device_id=(px, py, c), device_id_type=MESH)


def _gather_start(name, shards, bufs, items, after):
    nw, n = len(shards), len(bufs)

    def body(*refs):
        ins, outs, sems = refs[:nw], refs[nw + n + 1:nw + 2 * n + 1], refs[nw + 2 * n + 1:]
        x, y, c = _position()
        peers = _chip_peers(x, y)
        for i in range(n):
            for k in range(3):
                _weight_copy(ins, outs[i], items, sems, i, k, 2 * x + y, peers, c).start()

    res = pl.pallas_call(
        body, name=name, in_specs=[_ANY] * (nw + n + 1), out_specs=[_ANY] * n + [_SEM, _SEM],
        out_shape=[_hbm_out(b.shape, b.dtype) for b in bufs]
        + [pltpu.SemaphoreType.DMA((3 * n,)), pltpu.SemaphoreType.DMA((3 * n,))],
        input_output_aliases={nw + i: i for i in range(n)}, compiler_params=_SPLIT,
    )(*shards, *bufs, after)
    return res[:n], res[n:]


def _gather_wait(name, shards, bufs, items, ids, sems, after):
    nw, m = len(shards), len(ids)

    def body(*refs):
        ins, bs = refs[:nw], refs[nw:nw + m]
        sem_refs = refs[nw + m:nw + m + 2]
        x, y, c = _position()
        peers = _chip_peers(x, y)
        for j, i in enumerate(ids):
            for k in range(3):
                px, py = peers[k]
                _weight_copy(ins, bs[j], items, sem_refs, i, k, 2 * px + py, peers, c).wait_recv()
        for j, i in enumerate(ids):
            for k in range(3):
                _weight_copy(ins, bs[j], items, sem_refs, i, k, 2 * x + y, peers, c).wait_send()

    res = pl.pallas_call(
        body, name=name, in_specs=[_ANY] * (nw + m) + [_SEM, _SEM, _ANY], out_specs=[_ANY] * m,
        out_shape=[_hbm_out(bufs[i].shape, bufs[i].dtype) for i in ids],
        input_output_aliases={nw + j: j for j in range(m)}, compiler_params=_SPLIT,
    )(*shards, *[bufs[i] for i in ids], *sems, after)
    return list(res)


def _forward_copy(src, dst, sems, i, k, core):
    x, y, c = _position()
    px, py = _chip_peers(x, y)[k]
    half = _half_rows(core, src.shape[1])
    return pltpu.make_async_remote_copy(
        src_ref=src.at[2 * px + py, half], dst_ref=dst.at[2 * px + py, half],
        send_sem=sems[0].at[3 * i + k], recv_sem=sems[1].at[3 * i + k],
        device_id=(x, y, 1 - c), device_id_type=MESH)


def _forward_start(name, bufs):
    n = len(bufs)

    def body(*refs):
        ins, outs, sems = refs[:n], refs[n:2 * n], refs[2 * n:]
        c = lax.axis_index("c")
        for i in range(n):
            for k in range(3):
                _forward_copy(ins[i], outs[i], sems, i, k, c).start()

    res = pl.pallas_call(
        body, name=name, in_specs=[_ANY] * n, out_specs=[_ANY] * n + [_SEM, _SEM],
        out_shape=[_hbm_out(g.shape, g.dtype) for g in bufs]
        + [pltpu.SemaphoreType.DMA((3 * n,)), pltpu.SemaphoreType.DMA((3 * n,))],
        input_output_aliases={i: i for i in range(n)}, compiler_params=_SPLIT,
    )(*bufs)
    return list(res[:n]), res[n:]


def _forward_wait(name, bufs, sems, after):
    n = len(bufs)

    def body(*refs):
        bs, sem_refs = refs[:n], refs[n:n + 2]
        c = lax.axis_index("c")
        for i in range(n):
            for k in range(3):
                _forward_copy(bs[i], bs[i], sem_refs, i, k, 1 - c).wait_recv()
        for i in range(n):
            for k in range(3):
                _forward_copy(bs[i], bs[i], sem_refs, i, k, c).wait_send()

    return list(pl.pallas_call(
        body, name=name, in_specs=[_ANY] * n + [_SEM, _SEM, _ANY], out_specs=[_ANY] * n,
        out_shape=[_hbm_out(g.shape, g.dtype) for g in bufs],
        input_output_aliases={i: i for i in range(n)}, compiler_params=_SPLIT,
    )(*bufs, *sems, after))


def _reduce_copy(grads, others, sems, i):
    x, y, c = _position()
    return pltpu.make_async_remote_copy(
        src_ref=grads[i].at[:, _half_rows(1 - c, grads[i].shape[1])], dst_ref=others[i],
        send_sem=sems[0].at[i], recv_sem=sems[1].at[i], device_id=(x, y, 1 - c), device_id_type=MESH)


def _reduce_start(name, grads, after):
    n = len(grads)

    def body(*refs):
        ins, outs, sems, token = refs[:n], refs[n + 1:2 * n + 1], refs[2 * n + 1:2 * n + 3], refs[2 * n + 3]
        for i in range(n):
            _reduce_copy(ins, outs, sems, i).start()
        token[...] = jnp.zeros_like(token)

    res = pl.pallas_call(
        body, name=name, in_specs=[_ANY] * (n + 1),
        out_specs=[_ANY] * n + [_SEM, _SEM, pl.BlockSpec(memory_space=pltpu.VMEM)],
        out_shape=[_hbm_out((N_CHIPS, g.shape[1] // 2, g.shape[2]), g.dtype) for g in grads]
        + [pltpu.SemaphoreType.DMA((n,)), pltpu.SemaphoreType.DMA((n,)), jax.ShapeDtypeStruct((SUBLANES, LANES), F32)],
        compiler_params=_SPLIT,
    )(*grads, after)
    return list(res[:n]), res[n:n + 2], res[n + 2]


def _reduce_wait(name, grads, others, sems, after):
    n = len(grads)

    def body(*refs):
        ins, os_, sem_refs = refs[:n], refs[n:2 * n], refs[2 * n:2 * n + 2]
        for i in range(n):
            _reduce_copy(ins, os_, sem_refs, i).wait_recv()
        for i in range(n):
            _reduce_copy(ins, os_, sem_refs, i).wait_send()

    return list(pl.pallas_call(
        body, name=name, in_specs=[_ANY] * (2 * n) + [_SEM, _SEM, _ANY], out_specs=[_ANY] * n,
        out_shape=[_hbm_out(o.shape, o.dtype) for o in others],
        input_output_aliases={n + i: i for i in range(n)}, compiler_params=_SPLIT,
    )(*grads, *others, *sems, after))


def _sum_cores(name, g, other, core):
    _, R, C = g.shape
    H = R // 2
    tr = _row_tile(H, C, 3 * 2, target=12 << 20)
    nb = H // tr

    def body(c_ref, g_ref, o_ref, out_ref):
        out_ref[...] = (g_ref[...].astype(F32) + o_ref[...].astype(F32)).astype(out_ref.dtype)

    return pl.pallas_call(
        body, name=name,
        grid_spec=pltpu.PrefetchScalarGridSpec(
            num_scalar_prefetch=1, grid=(N_CHIPS, nb),
            in_specs=[pl.BlockSpec((None, tr, C), lambda j, i, c_ref: (j, c_ref[0] * nb + i, 0)),
                      pl.BlockSpec((None, tr, C), lambda j, i, c_ref: (j, i, 0))],
            out_specs=pl.BlockSpec((None, tr, C), lambda j, i, c_ref: (j, i, 0))),
        out_shape=_hbm_out((N_CHIPS, H, C), BF16),
        compiler_params=_params(("parallel", "parallel")),
    )(core, g, other)


def _sum_chips(name, received, own, full, layer, me_core):
    _, H, C = received.shape
    tr = _row_tile(H, C, 3 * 2 + 2 + 4, target=12 << 20)
    nb = H // tr

    def body(s_ref, r_ref, own_ref, full_ref, out_ref):
        acc = r_ref[0].astype(F32)
        for k in (1, 2):
            acc = acc + r_ref[k].astype(F32)
        out_ref[...] = acc + own_ref[...].astype(F32)

    return pl.pallas_call(
        body, name=name,
        grid_spec=pltpu.PrefetchScalarGridSpec(
            num_scalar_prefetch=1, grid=(nb,),
            in_specs=[pl.BlockSpec((3, tr, C), lambda i, s_ref: (0, i, 0)),
                      pl.BlockSpec((None, tr, C), lambda i, s_ref: (s_ref[0], i, 0)),
                      _ANY],
            out_specs=pl.BlockSpec((None, tr, C), lambda i, s_ref: (layer, s_ref[1] * nb + i, 0))),
        out_shape=_hbm_out(full.shape, full.dtype),
        input_output_aliases={3: 0},
        compiler_params=_params(("parallel",)),
    )(me_core, received, own, full)


def _part_copy(parts, recv, sems, i, k, peers, c):
    px, py = peers[k]
    return pltpu.make_async_remote_copy(
        src_ref=parts[i].at[2 * px + py], dst_ref=recv[i].at[k],
        send_sem=sems[0].at[3 * i + k], recv_sem=sems[1].at[3 * i + k],
        device_id=(px, py, c), device_id_type=MESH)


def _scatter_start(name, parts):
    n = len(parts)

    def body(*refs):
        ins, outs, sems, token = refs[:n], refs[n:2 * n], refs[2 * n:2 * n + 2], refs[2 * n + 2]
        x, y, c = _position()
        peers = _chip_peers(x, y)
        for i in range(n):
            for k in range(3):
                _part_copy(ins, outs, sems, i, k, peers, c).start()
        token[...] = jnp.zeros_like(token)

    res = pl.pallas_call(
        body, name=name, in_specs=[_ANY] * n,
        out_specs=[_ANY] * n + [_SEM, _SEM, pl.BlockSpec(memory_space=pltpu.VMEM)],
        out_shape=[_hbm_out((3,) + p.shape[1:], p.dtype) for p in parts]
        + [pltpu.SemaphoreType.DMA((3 * n,)), pltpu.SemaphoreType.DMA((3 * n,)),
           jax.ShapeDtypeStruct((SUBLANES, LANES), F32)],
        compiler_params=_SPLIT,
    )(*parts)
    return list(res[:n]), res[n:n + 2], res[n + 2]


def _scatter_wait(name, parts, recv, sems):
    n = len(parts)

    def body(*refs):
        ins, rs, sem_refs = refs[:n], refs[n:2 * n], refs[2 * n:2 * n + 2]
        x, y, c = _position()
        peers = _chip_peers(x, y)
        for i in range(n):
            for k in range(3):
                _part_copy(ins, rs, sem_refs, i, k, peers, c).wait_recv()
        for i in range(n):
            for k in range(3):
                _part_copy(ins, rs, sem_refs, i, k, peers, c).wait_send()

    return list(pl.pallas_call(
        body, name=name, in_specs=[_ANY] * (2 * n) + [_SEM, _SEM], out_specs=[_ANY] * n,
        out_shape=[_hbm_out(r.shape, r.dtype) for r in recv],
        input_output_aliases={n + i: i for i in range(n)}, compiler_params=_SPLIT,
    )(*parts, *recv, *sems))


def _share_d2d(name, full):
    n = len(full)

    def body(*refs):
        ins, outs = refs[:n], refs[n:2 * n]
        send_sems, recv_sems = refs[2 * n:]
        x, y, c = _position()

        def remote(w, core):
            half = _half_rows(core, ins[w].shape[1])
            return pltpu.make_async_remote_copy(
                src_ref=ins[w].at[:, half], dst_ref=outs[w].at[:, half],
                send_sem=send_sems.at[w], recv_sem=recv_sems.at[w],
                device_id=(x, y, 1 - c), device_id_type=MESH)

        for w in range(n):
            remote(w, c).start()
        for w in range(n):
            remote(w, 1 - c).wait_recv()
        for w in range(n):
            remote(w, c).wait_send()

    return pl.pallas_call(
        body, name=name, in_specs=[_ANY] * n, out_specs=[_ANY] * n,
        out_shape=[_hbm_out(f.shape, f.dtype) for f in full],
        input_output_aliases={w: w for w in range(n)},
        scratch_shapes=[pltpu.SemaphoreType.DMA((n,)), pltpu.SemaphoreType.DMA((n,))],
    )(*full)


def _gather_all(name, a):
    def body(a_ref, o_ref, send_sems, recv_sems, local_sem):
        x, y, c = _position()
        me = 4 * x + 2 * y + c

        def peer(k):
            return (x ^ ((k >> 2) & 1), y ^ ((k >> 1) & 1), c ^ (k & 1))

        def remote(k, slot):
            return pltpu.make_async_remote_copy(
                src_ref=a_ref, dst_ref=o_ref.at[slot], send_sem=send_sems.at[k - 1], recv_sem=recv_sems.at[k - 1],
                device_id=peer(k), device_id_type=MESH)

        local = pltpu.make_async_copy(a_ref, o_ref.at[me], local_sem)
        local.start()
        for k in range(1, N_DEV):
            remote(k, me).start()
        for k in range(1, N_DEV):
            px, py, pc = peer(k)
            remote(k, 4 * px + 2 * py + pc).wait_recv()
        for k in range(1, N_DEV):
            remote(k, me).wait_send()
        local.wait()

    return pl.pallas_call(
        body, name=name, in_specs=[_ANY], out_specs=_ANY,
        out_shape=_hbm_out((N_DEV,) + a.shape, a.dtype),
        scratch_shapes=[pltpu.SemaphoreType.DMA((N_DEV - 1,)), pltpu.SemaphoreType.DMA((N_DEV - 1,)),
                        pltpu.SemaphoreType.DMA],
    )(a)


def _rows2d(a, lead=0):
    return a.reshape(a.shape[:lead] + (-1, a.shape[-1]))


def _row_tile(rows, cols, itemsize=4, target=1 << 20):
    want = max(SUBLANES, target // (cols * itemsize))
    t = min(rows, (want // 16) * 16)
    while t > 16 and rows % t:
        t -= 16
    return t if rows % t == 0 else rows


def _sum_slots(name, r, out_dtype=F32):
    ns = r.shape[0]
    r2 = _rows2d(r, 1)
    _, rows, cols = r2.shape
    tr = _row_tile(rows, cols)

    def body(r_ref, o_ref):
        acc = r_ref[0].astype(F32)
        for s in range(1, ns):
            acc = acc + r_ref[s].astype(F32)
        o_ref[...] = acc.astype(o_ref.dtype)

    out = pl.pallas_call(
        body, name=name, grid=(rows // tr,),
        in_specs=[pl.BlockSpec((ns, tr, cols), lambda i: (0, i, 0))],
        out_specs=pl.BlockSpec((tr, cols), lambda i: (i, 0)),
        out_shape=_hbm_out((rows, cols), out_dtype),
        compiler_params=_params(("parallel",)),
    )(r2)
    return out.reshape(r.shape[1:])


def _adamw(name, g_parts, w, m, v):
    shape = w.shape
    ng = len(g_parts)
    args = [_rows2d(a) for a in (*g_parts, w, m, v)]
    rows, cols = args[0].shape
    tr = _row_tile(rows, cols, (ng + 7) * 4, target=16 << 20)
    c1 = 1.0 - ADAM_B1 ** ADAM_STEP
    c2 = 1.0 - ADAM_B2 ** ADAM_STEP

    def body(*refs):
        g = refs[0][...]
        for r in refs[1:ng]:
            g = g + r[...]
        w_ref, m_ref, v_ref = refs[ng:ng + 3]
        g_out, d_out, m_out, v_out = refs[ng + 3:]
        mn = ADAM_B1 * m_ref[...] + (1.0 - ADAM_B1) * g
        vn = ADAM_B2 * v_ref[...] + (1.0 - ADAM_B2) * (g * g)
        m_hat = mn / c1
        v_hat = vn / c2
        g_out[...] = g
        d_out[...] = -ADAM_LR * (m_hat / (jnp.sqrt(v_hat) + ADAM_EPS) + ADAM_WD * w_ref[...])
        m_out[...] = mn
        v_out[...] = vn

    spec = pl.BlockSpec((tr, cols), lambda i: (i, 0))
    outs = pl.pallas_call(
        body, name=name, grid=(rows // tr,), in_specs=[spec] * (ng + 3), out_specs=[spec] * 4,
        out_shape=[_hbm_out((rows, cols), F32)] * 4,
        compiler_params=_params(("parallel",)),
    )(*args)
    return tuple(o.reshape(shape) for o in outs)


_WEIGHTS = ["norm_mix", "norm_ffn", "w_ffn_in", "w_ffn_out", "w_rec_in", "conv_w", "conv_b", "w_lru_gates",
            "b_lru_gates", "lru_param", "w_rec_out", "norm_kv", "w_kvf", "b_forget", "w_q", "w_o", "norm_final"]
_BIG = ["w_ffn_in", "w_ffn_out", "w_rec_in", "w_lru_gates", "w_rec_out", "w_kvf", "w_q", "w_o"]


def _stack3(a):
    return a[None] if a.ndim == 2 else a.reshape(a.shape[0], -1, a.shape[-1])


def _pad_lanes(a, n):
    return jnp.pad(a, ((0, 0),) * (a.ndim - 1) + ((0, n - a.shape[-1]),))


def kernel(x, norm_mix, norm_ffn, w_ffn_in, w_ffn_out, w_rec_in, conv_w, conv_b, w_lru_gates, b_lru_gates, lru_param, w_rec_out, norm_kv, w_kvf, b_forget, w_q, w_o, norm_final, loss_target, m_norm_mix, m_norm_ffn, m_w_ffn_in, m_w_ffn_out, m_w_rec_in, m_conv_w, m_conv_b, m_w_lru_gates, m_b_lru_gates, m_lru_param, m_w_rec_out, m_norm_kv, m_w_kvf, m_b_forget, m_w_q, m_w_o, m_norm_final, v_norm_mix, v_norm_ffn, v_w_ffn_in, v_w_ffn_out, v_w_rec_in, v_conv_w, v_conv_b, v_w_lru_gates, v_b_lru_gates, v_lru_param, v_w_rec_out, v_norm_kv, v_w_kvf, v_b_forget, v_w_q, v_w_o, v_norm_final):
    P = dict(norm_mix=norm_mix, norm_ffn=norm_ffn, w_ffn_in=w_ffn_in, w_ffn_out=w_ffn_out, w_rec_in=w_rec_in,
             conv_w=conv_w, conv_b=conv_b, w_lru_gates=w_lru_gates, b_lru_gates=b_lru_gates, lru_param=lru_param,
             w_rec_out=w_rec_out, norm_kv=norm_kv, w_kvf=w_kvf, b_forget=b_forget, w_q=w_q, w_o=w_o,
             norm_final=norm_final)
    M1 = dict(norm_mix=m_norm_mix, norm_ffn=m_norm_ffn, w_ffn_in=m_w_ffn_in, w_ffn_out=m_w_ffn_out,
              w_rec_in=m_w_rec_in, conv_w=m_conv_w, conv_b=m_conv_b, w_lru_gates=m_w_lru_gates,
              b_lru_gates=m_b_lru_gates, lru_param=m_lru_param, w_rec_out=m_w_rec_out, norm_kv=m_norm_kv,
              w_kvf=m_w_kvf, b_forget=m_b_forget, w_q=m_w_q, w_o=m_w_o, norm_final=m_norm_final)
    M2 = dict(norm_mix=v_norm_mix, norm_ffn=v_norm_ffn, w_ffn_in=v_w_ffn_in, w_ffn_out=v_w_ffn_out,
              w_rec_in=v_w_rec_in, conv_w=v_conv_w, conv_b=v_conv_b, w_lru_gates=v_w_lru_gates,
              b_lru_gates=v_b_lru_gates, lru_param=v_lru_param, w_rec_out=v_w_rec_out, norm_kv=v_norm_kv,
              w_kvf=v_w_kvf, b_forget=v_b_forget, w_q=v_w_q, w_o=v_w_o, norm_final=v_norm_final)

    _, S, D = x.shape
    L = norm_mix.shape[0]
    NA, NBLK, BW, GS = w_lru_gates.shape
    NB = w_q.shape[0]
    C = NBLK * BW
    CS = C // N_CHIPS
    H = b_forget.shape[0]
    assert C == D and H * HEAD_DIM == D and H <= LANES
    chip = 2 * lax.axis_index("x") + lax.axis_index("y")

    small_a = jnp.concatenate([conv_w, conv_b[:, None], lru_param[:, None]], axis=1)
    small_a, b_gates = _gather_smalls("gather_smalls", [small_a, b_lru_gates])
    small_a = small_a.transpose(1, 2, 0, 3).reshape(NA, 6, C)
    b_gates = b_gates.transpose(1, 2, 0, 3).reshape(NA, NBLK, 1, N_CHIPS * GS)
    shards = [_stack3(P[w]).astype(BF16) for w in _BIG]
    core = lax.axis_index("c")
    chip_id = jnp.reshape(chip, (1,)).astype(jnp.int32)
    core_id = jnp.reshape(core, (1,)).astype(jnp.int32)
    me_core = jnp.stack([chip, core]).astype(jnp.int32)

    parts_of_layer = ("mix", "mix2", "ffn")

    def part_items(l, part):
        if part == "ffn":
            names, at = ["w_ffn_in", "w_ffn_out"], l
        elif l < NA:
            names, at = (["w_rec_in"] if part == "mix" else ["w_lru_gates", "w_rec_out"]), l
        else:
            names, at = ((["w_kvf"] if l == NA else []) + ["w_q"] if part == "mix" else ["w_o"]), l - NA
        return [(_BIG.index(n), 0 if n == "w_kvf" else at) for n in names]

    def stage_of(l, part):
        return (l, part) if l == 0 else (l, "all")

    def stage_items(st):
        l, part = st
        return [it for p in (parts_of_layer if part == "all" else (part,)) for it in part_items(l, p)]

    stages = [(0, p) for p in parts_of_layer] + [(l, "all") for l in range(1, L)]
    items = [it for st in stages for it in stage_items(st)]
    ids_of = {st: [items.index(it) for it in stage_items(st)] for st in stages}
    bufs = [_place_own(f"place_{_BIG[w]}_{li}", shards[w], li, chip_id) for w, li in items]
    bufs, gather_sems = _gather_start("gather_start", shards, bufs, items, small_a)

    forwarding, fetched = {}, {}

    def layer_prefetch(l, part, after):
        st = stage_of(l, part)
        if l < L and st not in forwarding:
            got = _gather_wait(f"gather_wait_{st[1]}_{l}", shards, bufs, items, ids_of[st], gather_sems, after)
            forwarding[st] = _forward_start(f"forward_start_{st[1]}_{l}", got)

    def layer_weights(l, part, after):
        if l >= L:
            return None
        st = stage_of(l, part)
        if st not in fetched:
            layer_prefetch(l, part, after)
            got, sems = forwarding[st]
            got = _forward_wait(f"forward_wait_{st[1]}_{l}", got, sems, after)
            fetched[st] = {_BIG[items[i][0]]: g for i, g in zip(ids_of[st], got)}
        B = fetched[st]
        if part == "ffn":
            return dict(w_ffn_in=B["w_ffn_in"], w_ffn_out=B["w_ffn_out"].reshape(-1, D))
        if l < NA and part == "mix":
            return dict(w_rec_in=B["w_rec_in"], conv_w=small_a[l, :4], conv_b=small_a[l, 4:5])
        if l < NA:
            return dict(w_gates=B["w_lru_gates"].reshape(N_CHIPS, NBLK, BW, GS).transpose(1, 2, 0, 3).reshape(
                NBLK, BW, N_CHIPS * GS), b_gates=b_gates[l], w_rec_out=B["w_rec_out"].reshape(C, D),
                lru_param=small_a[l, 5:6])
        if part == "mix2":
            return dict(w_o=B["w_o"].reshape(D, D))
        W = dict(w_q=B["w_q"].reshape(D, D))
        if l == NA:
            w_kvf_full = B["w_kvf"].transpose(1, 0, 2).reshape(D, -1)
            W.update(norm_kv=norm_kv[None], w_kv=w_kvf_full[:, :2 * D],
                     w_f=_pad_lanes(w_kvf_full[:, 2 * D:], LANES), b_f=_pad_lanes(b_forget[None], LANES))
        return W

    G_small = {l: {} for l in range(L)}
    stash = {st: {} for st in stages}
    pending = {}
    reducing = []

    def finish_reduce(after):
        st, its, grads, others, sems = reducing.pop()
        l, part = st
        others = _reduce_wait(f"reduce_wait_{part}_{l}", grads, others, sems, after)
        parts = [_sum_cores(f"sum_cores_{l}_{_BIG[w]}", g, o, core_id) for (w, _), g, o in zip(its, grads, others)]
        recv, sems, token = _scatter_start(f"scatter_start_{part}_{l}", parts)
        pending[st] = (parts, recv, sems)
        return token

    def layer_grads(l, part, G_part):
        G_small[l].update(G_part)
        st = stage_of(l, part)
        stash[st].update(G_part)
        if st[1] == "all" and part != "mix":
            return None
        G = stash[st]
        late = {"ffn": "w_ffn_in", "mix": "norm_mix"}.get(part) or ("w_gates" if l < NA else "w_o")
        after = finish_reduce(G_part[late]) if reducing else jnp.zeros((SUBLANES, LANES), F32)
        by_name = dict(
            w_ffn_in=lambda: G["w_ffn_in"], w_ffn_out=lambda: G["w_ffn_out"].reshape(N_CHIPS, -1, D),
            w_rec_in=lambda: G["w_rec_in"],
            w_lru_gates=lambda: G["w_gates"].reshape(NBLK, BW, N_CHIPS, GS).transpose(2, 0, 1, 3).reshape(
                N_CHIPS, NBLK * BW, GS),
            w_rec_out=lambda: G["w_rec_out"].reshape(N_CHIPS, -1, D),
            w_kvf=lambda: jnp.concatenate([G["w_kv"].astype(F32), G["w_f"][:, :H]], axis=1).reshape(
                D, N_CHIPS, -1).transpose(1, 0, 2).astype(BF16),
            w_q=lambda: G["w_q"].reshape(N_CHIPS, -1, D), w_o=lambda: G["w_o"].reshape(N_CHIPS, -1, D))
        its = stage_items(st)
        grads = [by_name[_BIG[w]]() for w, _ in its]
        others, sems, token = _reduce_start(f"reduce_start_{st[1]}_{l}", grads, after)
        reducing.append((st, its, grads, others, sems))
        return finish_reduce(token) if l == 0 else token

    gains = dict(mix=[norm_mix[l][None] for l in range(L)], ffn=[norm_ffn[l][None] for l in range(L)],
                 final=norm_final[None])
    loss_row, grad_x, dg_final = _local_step(x.reshape(S, D), loss_target.reshape(S, D), gains,
                                             layer_weights, layer_prefetch, layer_grads)

    rows = [*[G_small[l]["norm_mix"] for l in range(L)], *[G_small[l]["norm_ffn"] for l in range(L)],
            G_small[NA]["norm_kv"], dg_final, _pad_lanes(G_small[NA]["b_f"], D), _pad_lanes(loss_row, D)]
    for a in range(NA):
        rows += [G_small[a][n] for n in ("conv_w", "conv_b", "b_gi", "b_gr", "lru_param")]
    packed = jnp.concatenate(rows, axis=0)
    tot = _sum_slots("sum_small", _gather_all("gather_small", packed))
    loss = tot[2 * L + 3, 0]
    g_rep = jnp.concatenate([tot[:2 * L + 2], tot[2 * L + 2:2 * L + 3]], axis=0)
    base = 2 * L + 4
    g_sh = []
    for a in range(NA):
        blk = lax.dynamic_slice_in_dim(tot[base + 8 * a:base + 8 * a + 8], chip * CS, CS, axis=1)
        gi = tot[base + 8 * a + 5].reshape(NBLK, BW)
        gr = tot[base + 8 * a + 6].reshape(NBLK, BW)
        bl = lax.dynamic_slice_in_dim(jnp.concatenate([gi, gr], axis=1), chip * GS, GS, axis=1)
        g_sh += [blk[:5], bl.reshape(-1, CS), blk[7:8]]
    g_sh = jnp.concatenate(g_sh, axis=0)
    nrow = g_sh.shape[0] // NA

    def pack_rep(T):
        return jnp.concatenate([T["norm_mix"], T["norm_ffn"], T["norm_kv"][None], T["norm_final"][None],
                                _pad_lanes(T["b_forget"][None], D)], axis=0)

    def pack_sh(T):
        return jnp.concatenate([jnp.concatenate([T["conv_w"][a], T["conv_b"][a][None],
                                                 T["b_lru_gates"][a].reshape(-1, CS), T["lru_param"][a][None]], axis=0)
                                for a in range(NA)], axis=0)

    rep = _adamw("adamw_replicated", [g_rep], pack_rep(P), pack_rep(M1), pack_rep(M2))
    shd = _adamw("adamw_small_sharded", [g_sh], pack_sh(P), pack_sh(M1), pack_sh(M2))

    def unpack_rep(t):
        return dict(norm_mix=t[:L], norm_ffn=t[L:2 * L], norm_kv=t[2 * L], norm_final=t[2 * L + 1],
                    b_forget=t[2 * L + 2, :H])

    def unpack_sh(t):
        t = t.reshape(NA, nrow, CS)
        return dict(conv_w=t[:, :4], conv_b=t[:, 4], b_lru_gates=t[:, 5:nrow - 1].reshape(NA, NBLK, GS),
                    lru_param=t[:, nrow - 1])

    full = [lax.empty(sh.shape, F32) for sh in shards]
    for st in reversed(stages):
        l, part = st
        parts, recv, sems = pending[st]
        recv = _scatter_wait(f"scatter_wait_{part}_{l}", parts, recv, sems)
        for (w, li), own, r in zip(stage_items(st), parts, recv):
            full[w] = _sum_chips(f"sum_chips_{l}_{_BIG[w]}", r, own, full[w], li, me_core)
    full = _share_d2d("share_d2d", full)
    big = {w: _adamw(f"adamw_{w}", [g.reshape(P[w].shape)], P[w], M1[w], M2[w]) for w, g in zip(_BIG, full)}

    outs = []
    for i in range(4):
        small = {**unpack_rep(rep[i]), **unpack_sh(shd[i])}
        outs.append([big[w][i] if w in big else small[w] for w in _WEIGHTS])
    return (loss, grad_x.reshape(1, S, D), *outs[0], *outs[1], *outs[2], *outs[3])
```

```python
import functools
import math

import jax
import jax.numpy as jnp
from jax import lax
from jax.experimental import pallas as pl
from jax.experimental.pallas import tpu as pltpu

F32 = jnp.float32
BF16 = jnp.bfloat16

EPS = 1e-6
LRU_C = 8.0
HEAD_DIM = 64
LANES = 128
SUBLANES = 8
VMEM_LIMIT = 48 * 1024 * 1024
N_CHIPS = 4
N_DEV = 8

ADAM_LR = 0.001
ADAM_B1 = 0.9
ADAM_B2 = 0.999
ADAM_EPS = 1e-08
ADAM_WD = 0.01
ADAM_STEP = 10

_NN = (((1,), (0,)), ((), ()))
_NT = (((1,), (1,)), ((), ()))
_TN = (((0,), (0,)), ((), ()))
_DN = {"nn": _NN, "nt": _NT, "tn": _TN}
MESH = pl.DeviceIdType.MESH


def _hbm_out(shape, dtype):
    return pltpu.HBM(shape, dtype)


def _params(sem):
    return pltpu.CompilerParams(dimension_semantics=sem, vmem_limit_bytes=VMEM_LIMIT)


def _tile(n, want):
    if n <= want:
        return n
    t = (want // LANES) * LANES
    while t >= LANES:
        if n % t == 0:
            return t
        t -= LANES
    return n


def _sigmoid(x):
    return 1.0 / (1.0 + jnp.exp(-x))


def _sigmoid_t(x):
    return 0.5 * jnp.tanh(0.5 * x) + 0.5


def _softplus(x):
    return jnp.maximum(x, 0.0) + jnp.log(1.0 + jnp.exp(-jnp.abs(x)))


_GELU_C = math.sqrt(2.0 / math.pi)


def _gelu_and_grad(x):
    inner = _GELU_C * (x + 0.044715 * x * x * x)
    t = jnp.tanh(inner)
    g = 0.5 * x * (1.0 + t)
    dg = 0.5 * (1.0 + t) + 0.5 * x * (1.0 - t * t) * _GELU_C * (1.0 + 3.0 * 0.044715 * x * x)
    return g, dg


def _rms(x):
    return lax.rsqrt(jnp.mean(x * x, axis=-1, keepdims=True) + EPS)


def _rms_bwd(dy, x, g):
    r = _rms(x)
    xr = x * r
    dyg = dy * g
    return r * dyg - xr * (r * jnp.mean(dyg * xr, axis=-1, keepdims=True)), jnp.sum(dy * xr, axis=0, keepdims=True)


def _mm(name, mode, a, b, *, grid, a_spec, b_spec, out_shape, out_dtype, out_spec, nk=1,
        res=None, res_spec=None, bias=None, bias_spec=None, scale=None, norm_gain=None, norm_bwd=None):
    dn = _DN[mode]
    has_res, has_bias = res is not None, bias is not None
    blk = tuple(d for d in out_spec.block_shape if d is not None)
    vec = pl.BlockSpec((1, blk[-1]), lambda *g: (0, 0))
    a_specs = a_spec if isinstance(a_spec, list) else [a_spec]
    b_specs = b_spec if isinstance(b_spec, list) else [b_spec]
    npair = len(a_specs)
    n_in = 2 * npair + int(has_res) + int(has_bias) + (1 if norm_gain is not None else 0) + (3 if norm_bwd else 0)

    def body(*refs):
        p = 2 * npair
        r_ref = refs[p] if has_res else None
        p += int(has_res)
        bias_ref = refs[p] if has_bias else None
        p += int(has_bias)
        extra = refs[p:n_in]
        outs = refs[n_in:]
        o_ref = outs[0]
        part = lax.dot_general(refs[0][...], refs[npair][...], dn, preferred_element_type=F32)
        for t in range(1, npair):
            part = part + lax.dot_general(refs[t][...], refs[npair + t][...], dn, preferred_element_type=F32)

        def finish(acc):
            if scale is not None:
                acc = acc * scale
            if has_bias:
                acc = acc + bias_ref[...]
            if has_res:
                acc = r_ref[...] + acc
            if norm_bwd:
                h_ref, g_ref, dh_ref = extra
                dx, dg = _rms_bwd(acc, h_ref[...], g_ref[...])
                acc = dh_ref[...] + dx
                outs[1][...] = acc.astype(BF16)
                outs[2][...] = dg
            if norm_gain is not None:
                outs[1][...] = (acc * _rms(acc) * extra[0][...]).astype(BF16)
            o_ref[...] = acc.astype(o_ref.dtype)

        if nk == 1:
            finish(part)
        else:
            acc_ref = refs[-1]
            k = pl.program_id(2)

            @pl.when(k == 0)
            def _():
                acc_ref[...] = part

            @pl.when(k > 0)
            def _():
                acc_ref[...] += part

            @pl.when(k == nk - 1)
            def _():
                finish(acc_ref[...])

    ins, specs = [a] * npair + [b] * npair, a_specs + b_specs
    if has_res:
        ins.append(res)
        specs.append(res_spec)
    if has_bias:
        ins.append(bias)
        specs.append(bias_spec)
    out_specs, out_shapes = [out_spec], [_hbm_out(out_shape, out_dtype)]
    if norm_gain is not None:
        ins.append(norm_gain)
        specs.append(vec)
        out_specs.append(out_spec)
        out_shapes.append(_hbm_out(out_shape, BF16))
    if norm_bwd:
        h, g, dh = norm_bwd
        ins += [h, g, dh]
        specs += [out_spec, vec, out_spec]
        out_specs += [out_spec, pl.BlockSpec((None, 1, blk[-1]), lambda i, *rest: (i, 0, 0))]
        out_shapes += [_hbm_out(out_shape, BF16), _hbm_out((grid[0], 1, blk[-1]), F32)]
    sem = ("parallel", "parallel") + (("arbitrary",) if len(grid) == 3 else ())
    single = len(out_specs) == 1
    return pl.pallas_call(
        body, name=name, grid=grid, in_specs=specs, out_specs=out_specs[0] if single else out_specs,
        out_shape=out_shapes[0] if single else out_shapes,
        scratch_shapes=[pltpu.VMEM(blk, F32)] if nk > 1 else [],
        compiler_params=_params(sem),
    )(*ins)


def _mm_nn(name, a, b, *, b_lead=(), out_dtype, tm=512, tn=512, res=None, bias=None, scale=None, norm_gain=None):
    M, K = a.shape
    N = b.shape[-1]
    tm, tn = _tile(M, tm), _tile(N, tn)
    nl = len(b_lead)
    return _mm(
        name, "nn", a, b, grid=(M // tm, N // tn),
        a_spec=pl.BlockSpec((tm, K), lambda i, j: (i, 0)),
        b_spec=pl.BlockSpec((None,) * nl + (K, tn), lambda i, j: tuple(b_lead) + (0, j)),
        out_shape=(M, N), out_dtype=out_dtype, out_spec=pl.BlockSpec((tm, tn), lambda i, j: (i, j)),
        res=res, res_spec=pl.BlockSpec((tm, tn), lambda i, j: (i, j)),
        bias=bias, bias_spec=pl.BlockSpec((1, tn), lambda i, j: (0, j)), scale=scale, norm_gain=norm_gain)


def _mm_nt(name, a, b, *, b_lead=(), out_dtype, tm=512, tn=512, tk=2048, res=None, norm_bwd=None):
    M, K = a.shape
    N = b.shape[-2]
    tm, tn, tk = _tile(M, tm), _tile(N, tn), _tile(K, tk)
    nk = K // tk
    nl = len(b_lead)
    return _mm(
        name, "nt", a, b, grid=(M // tm, N // tn, nk), nk=nk,
        a_spec=pl.BlockSpec((tm, tk), lambda i, j, k: (i, k)),
        b_spec=pl.BlockSpec((None,) * nl + (tn, tk), lambda i, j, k: tuple(b_lead) + (j, k)),
        out_shape=(M, N), out_dtype=out_dtype, out_spec=pl.BlockSpec((tm, tn), lambda i, j, k: (i, j)),
        res=res, res_spec=pl.BlockSpec((tm, tn), lambda i, j, k: (i, j)), norm_bwd=norm_bwd)


def _mm_tn(name, a, b, *, out_dtype, tm=512, tn=512):
    S, M = a.shape
    N = b.shape[1]
    tm, tn = _tile(M, tm), _tile(N, tn)
    return _mm(
        name, "tn", a, b, grid=(M // tm, N // tn),
        a_spec=pl.BlockSpec((S, tm), lambda i, j: (0, i)),
        b_spec=pl.BlockSpec((S, tn), lambda i, j: (0, j)),
        out_shape=(M, N), out_dtype=out_dtype, out_spec=pl.BlockSpec((tm, tn), lambda i, j: (i, j)))


def _rmsnorm_fwd(name, h, g, tr=256):
    S, D = h.shape
    tr = _tile(S, tr)

    def body(h_ref, g_ref, o_ref):
        x = h_ref[...]
        r = lax.rsqrt(jnp.mean(x * x, axis=-1, keepdims=True) + EPS)
        o_ref[...] = (x * r * g_ref[...]).astype(o_ref.dtype)

    return pl.pallas_call(
        body, name=name, grid=(S // tr,),
        in_specs=[pl.BlockSpec((tr, D), lambda i: (i, 0)), pl.BlockSpec((1, D), lambda i: (0, 0))],
        out_specs=pl.BlockSpec((tr, D), lambda i: (i, 0)),
        out_shape=_hbm_out((S, D), BF16),
        compiler_params=_params(("parallel",)),
    )(h, g)


def _loss_head(name, h, target, g, tr=256):
    S, D = h.shape
    tr = _tile(S, tr)

    def body(h_ref, t_ref, g_ref, o_ref, ob_ref, dg_ref, loss_ref):
        i = pl.program_id(0)
        x = h_ref[...]
        gg = g_ref[...]
        r = lax.rsqrt(jnp.mean(x * x, axis=-1, keepdims=True) + EPS)
        xr = x * r
        err = xr * gg - t_ref[...]
        lpart = 0.5 * jnp.sum(jnp.mean(err * err, axis=-1, keepdims=True), axis=0, keepdims=True)
        dy = err * (1.0 / D)
        dyg = dy * gg
        dx = r * dyg - xr * (r * jnp.mean(dyg * xr, axis=-1, keepdims=True))
        o_ref[...] = dx
        ob_ref[...] = dx.astype(BF16)
        part = jnp.sum(dy * xr, axis=0, keepdims=True)
        lrow = jnp.broadcast_to(lpart, (1, LANES))

        @pl.when(i == 0)
        def _():
            dg_ref[...] = part
            loss_ref[...] = lrow

        @pl.when(i > 0)
        def _():
            dg_ref[...] += part
            loss_ref[...] += lrow

    row = pl.BlockSpec((tr, D), lambda i: (i, 0))
    vec = pl.BlockSpec((1, D), lambda i: (0, 0))
    return pl.pallas_call(
        body, name=name, grid=(S // tr,),
        in_specs=[row, row, vec], out_specs=[row, row, vec, pl.BlockSpec((1, LANES), lambda i: (0, 0))],
        out_shape=[_hbm_out((S, D), F32), _hbm_out((S, D), BF16),
                   _hbm_out((1, D), F32), _hbm_out((1, LANES), F32)],
        compiler_params=_params(("arbitrary",)),
    )(h, target, g)


def _swiglu_fwd(name, hn, w_in, tm=512):
    S, D = hn.shape
    FH = w_in.shape[-1]
    tm = _tile(S, tm)

    def body(x_ref, wg_ref, wu_ref, z_ref, a_ref):
        x = x_ref[...]
        zg = jnp.dot(x, wg_ref[...], preferred_element_type=F32)
        zu = jnp.dot(x, wu_ref[...], preferred_element_type=F32)
        z_ref[0] = zg.astype(z_ref.dtype)
        z_ref[1] = zu.astype(z_ref.dtype)
        a_ref[...] = (zg * _sigmoid_t(zg) * zu).astype(a_ref.dtype)

    return pl.pallas_call(
        body, name=name, grid=(S // tm, 2),
        in_specs=[pl.BlockSpec((tm, D), lambda i, j: (i, 0)),
                  pl.BlockSpec((None, D, FH), lambda i, j: (j, 0, 0)),
                  pl.BlockSpec((None, D, FH), lambda i, j: (j + 2, 0, 0))],
        out_specs=[pl.BlockSpec((2, tm, FH), lambda i, j: (0, i, j)), pl.BlockSpec((tm, FH), lambda i, j: (i, j))],
        out_shape=[_hbm_out((2, S, 2 * FH), BF16), _hbm_out((S, 2 * FH), BF16)],
        compiler_params=_params(("parallel", "parallel")),
    )(hn, w_in, w_in)


def _swiglu_bwd(name, dhb, w_out, z3, tm=512):
    S, D = dhb.shape
    F = w_out.shape[0]
    FH = F // 2
    tm = _tile(S, tm)

    def body(d_ref, w_ref, z_ref, dz_ref):
        d = lax.dot_general(d_ref[...], w_ref[...], _NT, preferred_element_type=F32)
        zg = z_ref[0].astype(F32)
        zu = z_ref[1].astype(F32)
        sg = _sigmoid_t(zg)
        dz_ref[0] = (d * zu * (sg * (1.0 + zg * (1.0 - sg)))).astype(dz_ref.dtype)
        dz_ref[1] = (d * (zg * sg)).astype(dz_ref.dtype)

    zspec = pl.BlockSpec((2, tm, FH), lambda i, j: (0, i, j))
    return pl.pallas_call(
        body, name=name, grid=(S // tm, 2),
        in_specs=[pl.BlockSpec((tm, D), lambda i, j: (i, 0)), pl.BlockSpec((FH, D), lambda i, j: (j, 0)), zspec],
        out_specs=zspec, out_shape=_hbm_out((2, S, F), BF16),
        compiler_params=_params(("parallel", "parallel")),
    )(dhb, w_out, z3)


SCAN_ROWS = 64


def _group_scan(A, B, reverse):
    n = A.shape[0]
    sub = lax.broadcasted_iota(jnp.int32, A.shape, 0) % SUBLANES
    for d in (1, 2, 4):
        if reverse:
            A_sh, B_sh = pltpu.roll(A, n - d, 0), pltpu.roll(B, n - d, 0)
            keep = sub < SUBLANES - d
        else:
            A_sh, B_sh = pltpu.roll(A, d, 0), pltpu.roll(B, d, 0)
            keep = sub >= d
        B = jnp.where(keep, A * B_sh + B, B)
        A = jnp.where(keep, A * A_sh, A)
    return A, B


def _block_scan(a, u, carry, reverse):
    A, B = _group_scan(a, u, reverse)
    ng = a.shape[0] // SUBLANES
    out = [None] * ng
    order = range(ng - 1, -1, -1) if reverse else range(ng)
    for gi in order:
        sl = slice(gi * SUBLANES, (gi + 1) * SUBLANES)
        hg = A[sl] * carry + B[sl]
        out[gi] = hg
        carry = hg[0:1] if reverse else hg[SUBLANES - 1:SUBLANES]
    return jnp.concatenate(out, axis=0), carry


def _lru_gates(rc, gip, grp, sp):
    gi = _sigmoid_t(gip)
    gr = _sigmoid_t(grp)
    la = -LRU_C * gr * sp
    a = jnp.exp(la)
    om = -jnp.tanh(la) * (a * a + 1.0)
    mult = jnp.sqrt(om)
    return gi, gr, a, mult


def _lru_fwd(name, proj, rc, gip, grp, lru_p, tc=256):
    S, C = rc.shape
    tc = _tile(C, tc)
    nb = S // SCAN_ROWS

    def body(gb_ref, rc_ref, gi_ref, gr_ref, l_ref, h_ref, m_ref):
        sp = _softplus(-l_ref[...])

        def step(b, carry):
            rows = pl.ds(pl.multiple_of(b * SCAN_ROWS, SCAN_ROWS), SCAN_ROWS)
            rcb = rc_ref[rows, :]
            gi, _, a, mult = _lru_gates(rcb, gi_ref[rows, :], gr_ref[rows, :], sp)
            h, carry = _block_scan(a, rcb * gi * mult, carry, False)
            h_ref[rows, :] = h
            gel, _ = _gelu_and_grad(gb_ref[rows, :])
            m_ref[rows, :] = (gel * h).astype(m_ref.dtype)
            return carry

        lax.fori_loop(0, nb, step, jnp.zeros((1, tc), F32))

    col = pl.BlockSpec((S, tc), lambda j: (0, j))
    return pl.pallas_call(
        body, name=name, grid=(C // tc,),
        in_specs=[col, col, col, col, pl.BlockSpec((1, tc), lambda j: (0, j))],
        out_specs=[col, col],
        out_shape=[_hbm_out((S, C), F32), _hbm_out((S, C), BF16)],
        compiler_params=_params(("parallel",)),
    )(proj, rc, gip, grp, lru_p)


def _lru_bwd(name, dm, proj, hrec, rc, gip, grp, lru_p, tc=256):
    S, C = rc.shape
    tc = _tile(C, tc)
    nb = S // SCAN_ROWS
    R = SCAN_ROWS

    def body(dm_ref, gb_ref, h_ref, rc_ref, gi_ref, gr_ref, l_ref,
             dgb_ref, dgi_ref, dgr_ref, drc_ref, dbi_ref, dbr_ref, dl_ref):
        lp = l_ref[...]
        sp = _softplus(-lp)
        row = lax.broadcasted_iota(jnp.int32, (R, tc), 0)
        zero = jnp.zeros((1, tc), F32)

        def step(t, carry):
            mu_in, s_i, s_r, s_sp = carry
            b = nb - 1 - t
            r0 = pl.multiple_of(b * R, R)
            rows = pl.ds(r0, R)
            rcb = rc_ref[rows, :]
            gi, gr, a, mult = _lru_gates(rcb, gi_ref[rows, :], gr_ref[rows, :], sp)
            gel, dgel = _gelu_and_grad(gb_ref[rows, :])
            dmb = dm_ref[rows, :]
            h = h_ref[rows, :]
            dgb_ref[rows, :] = (dmb * h * dgel).astype(dgb_ref.dtype)
            dh = dmb * gel
            mu, mu_out = _block_scan(a, a * dh, mu_in, True)
            mu_next = jnp.where(row == R - 1, mu_in, pltpu.roll(mu, R - 1, 0))
            lam = dh + mu_next
            p0 = pl.multiple_of(jnp.maximum(r0 - SUBLANES, 0), SUBLANES)
            prev = h_ref[pl.ds(p0, SUBLANES), :][SUBLANES - 1:SUBLANES]
            prev = jnp.where(b > 0, prev, 0.0)
            h_prev = jnp.where(row == 0, prev, pltpu.roll(h, 1, 0))
            da = lam * h_prev
            d_mult = lam * rcb * gi
            d_la = da * a - d_mult * (a * a) / mult
            d_grp = d_la * (-LRU_C * sp) * gr * (1.0 - gr)
            d_gip = lam * rcb * mult * gi * (1.0 - gi)
            dgr_ref[rows, :] = d_grp.astype(dgr_ref.dtype)
            dgi_ref[rows, :] = d_gip.astype(dgi_ref.dtype)
            drc_ref[rows, :] = lam * gi * mult
            s_i = s_i + jnp.sum(d_gip, axis=0, keepdims=True)
            s_r = s_r + jnp.sum(d_grp, axis=0, keepdims=True)
            s_sp = s_sp + jnp.sum(d_la * gr, axis=0, keepdims=True)
            return mu_out, s_i, s_r, s_sp

        _, s_i, s_r, s_sp = lax.fori_loop(0, nb, step, (zero, zero, zero, zero))
        dbi_ref[...] = s_i
        dbr_ref[...] = s_r
        dl_ref[...] = (-LRU_C * s_sp) * (-_sigmoid(-lp))

    col = pl.BlockSpec((S, tc), lambda j: (0, j))
    vec = pl.BlockSpec((1, tc), lambda j: (0, j))
    return pl.pallas_call(
        body, name=name, grid=(C // tc,),
        in_specs=[col, col, col, col, col, col, vec],
        out_specs=[col, col, col, col, vec, vec, vec],
        out_shape=[_hbm_out((S, C), BF16), _hbm_out((S, C), BF16),
                   _hbm_out((S, C), BF16), _hbm_out((S, C), F32),
                   _hbm_out((1, C), F32), _hbm_out((1, C), F32),
                   _hbm_out((1, C), F32)],
        compiler_params=_params(("parallel",)),
    )(dm, proj, hrec, rc, gip, grp, lru_p)


def _cumsum_rows(name, u, reverse):
    S, C = u.shape
    nb = S // SCAN_ROWS

    def body(u_ref, o_ref):
        def step(t, carry):
            b = nb - 1 - t if reverse else t
            rows = pl.ds(pl.multiple_of(b * SCAN_ROWS, SCAN_ROWS), SCAN_ROWS)
            ub = u_ref[rows, :]
            h, carry = _block_scan(jnp.ones_like(ub), ub, carry, reverse)
            o_ref[rows, :] = h
            return carry

        lax.fori_loop(0, nb, step, jnp.zeros((1, C), F32))

    spec = pl.BlockSpec((S, C), lambda i: (0, 0))
    return pl.pallas_call(
        body, name=name, grid=(1,), in_specs=[spec], out_specs=spec,
        out_shape=_hbm_out((S, C), F32),
        compiler_params=_params(("arbitrary",)),
    )(u)


def _shift_down(x, k):
    row = lax.broadcasted_iota(jnp.int32, x.shape, 0)
    return jnp.where(row >= k, pltpu.roll(x, k, 0), 0.0)


def _shift_up(x, k):
    n = x.shape[0]
    row = lax.broadcasted_iota(jnp.int32, x.shape, 0)
    return jnp.where(row < n - k, pltpu.roll(x, n - k, 0), 0.0)


def _conv_fwd(name, proj, w, b, tc=256):
    S, C2 = proj.shape
    C = C2 // 2
    tc = _tile(C, tc)
    off = C // tc

    def body(x_ref, w_ref, b_ref, o_ref, ob_ref):
        x = x_ref[...]
        out = b_ref[...] + w_ref[3:4, :] * x
        for k in (1, 2, 3):
            out = out + w_ref[3 - k:4 - k, :] * _shift_down(x, k)
        o_ref[...] = out
        ob_ref[...] = out.astype(BF16)

    col = pl.BlockSpec((S, tc), lambda j: (0, j))
    return pl.pallas_call(
        body, name=name, grid=(C // tc,),
        in_specs=[pl.BlockSpec((S, tc), lambda j: (0, off + j)),
                  pl.BlockSpec((4, tc), lambda j: (0, j)), pl.BlockSpec((1, tc), lambda j: (0, j))],
        out_specs=[col, col],
        out_shape=[_hbm_out((S, C), F32), _hbm_out((S, C), BF16)],
        compiler_params=_params(("parallel",)),
    )(proj, w, b)


def _conv_bwd(name, drc, proj, w, tc=256):
    S, C = drc.shape
    tc = _tile(C, tc)
    off = C // tc

    def body(y_ref, x_ref, w_ref, dx_ref, dw_ref, db_ref):
        y = y_ref[...]
        x = x_ref[...]
        dx = w_ref[3:4, :] * y
        dw_ref[3:4, :] = jnp.sum(y * x, axis=0, keepdims=True)
        for k in (1, 2, 3):
            dx = dx + w_ref[3 - k:4 - k, :] * _shift_up(y, k)
            dw_ref[3 - k:4 - k, :] = jnp.sum(y * _shift_down(x, k), axis=0, keepdims=True)
        dx_ref[...] = dx.astype(dx_ref.dtype)
        db_ref[...] = jnp.sum(y, axis=0, keepdims=True)

    col = pl.BlockSpec((S, tc), lambda j: (0, j))
    return pl.pallas_call(
        body, name=name, grid=(C // tc,),
        in_specs=[col, pl.BlockSpec((S, tc), lambda j: (0, off + j)), pl.BlockSpec((4, tc), lambda j: (0, j))],
        out_specs=[col, pl.BlockSpec((4, tc), lambda j: (0, j)), pl.BlockSpec((1, tc), lambda j: (0, j))],
        out_shape=[_hbm_out((S, C), BF16), _hbm_out((4, C), F32),
                   _hbm_out((1, C), F32)],
        compiler_params=_params(("parallel",)),
    )(drc, proj, w)


def _gates_fwd(name, rcb, wg, bg):
    S, C = rcb.shape
    nblk, bw, _ = wg.shape

    def body(x_ref, w_ref, b_ref, gi_ref, gr_ref):
        g = jnp.dot(x_ref[...], w_ref[...], preferred_element_type=F32) + b_ref[...]
        gi_ref[...] = g[:, :bw]
        gr_ref[...] = g[:, bw:]

    col = pl.BlockSpec((S, bw), lambda n: (0, n))
    return pl.pallas_call(
        body, name=name, grid=(nblk,),
        in_specs=[col, pl.BlockSpec((None, bw, 2 * bw), lambda n: (n, 0, 0)),
                  pl.BlockSpec((None, 1, 2 * bw), lambda n: (n, 0, 0))],
        out_specs=[col, col],
        out_shape=[_hbm_out((S, C), F32), _hbm_out((S, C), F32)],
        compiler_params=_params(("parallel",)),
    )(rcb, wg, bg)


def _gates_bwd(name, dgi, dgr, rcb, wg, drc1):
    S, C = rcb.shape
    nblk, bw, _ = wg.shape

    def body(dgi_ref, dgr_ref, x_ref, w_ref, d1_ref, drc_ref, dw_ref):
        w = w_ref[...]
        x = x_ref[...]
        di, dr = dgi_ref[...], dgr_ref[...]
        drc_ref[...] = (d1_ref[...]
                        + lax.dot_general(di, w[:, :bw], _NT, preferred_element_type=F32)
                        + lax.dot_general(dr, w[:, bw:], _NT, preferred_element_type=F32))
        dw_ref[:, :bw] = lax.dot_general(x, di, _TN, preferred_element_type=F32).astype(dw_ref.dtype)
        dw_ref[:, bw:] = lax.dot_general(x, dr, _TN, preferred_element_type=F32).astype(dw_ref.dtype)

    col = pl.BlockSpec((S, bw), lambda n: (0, n))
    wspec = pl.BlockSpec((None, bw, 2 * bw), lambda n: (n, 0, 0))
    return pl.pallas_call(
        body, name=name, grid=(nblk,),
        in_specs=[col, col, col, wspec, col], out_specs=[col, wspec],
        out_shape=[_hbm_out((S, C), F32), _hbm_out((nblk, bw, 2 * bw), BF16)],
        compiler_params=_params(("parallel",)),
    )(dgi, dgr, rcb, wg, drc1)


def _att_tile(S):
    return next(t for t in (512, 256, 128) if S % t == 0)


def _head_lanes(shape):
    return lax.broadcasted_iota(jnp.int32, shape, len(shape) - 1) < HEAD_DIM


def _key_bias(c_blk):
    first = _head_lanes(c_blk.shape)
    rolled = pltpu.roll(c_blk, HEAD_DIM, 1)
    return jnp.where(first, c_blk, rolled), jnp.where(first, rolled, c_blk)


def _over_keys(x, op):
    n = x.shape[0]
    while n > SUBLANES:
        n //= 2
        x = op(x[:n], x[n:2 * n])
    return (jnp.max if op is jnp.maximum else jnp.sum)(x, axis=0, keepdims=True)


def _causal_t(T, cc):
    r = lax.broadcasted_iota(jnp.int32, (T, LANES), 0)
    c = lax.broadcasted_iota(jnp.int32, (T, LANES), 1) + cc * LANES
    return r <= c


def _attn_fwd(name, q, kv, cfull):
    S, D = q.shape
    HP = D // LANES
    T = _att_tile(S)
    nq = S // T
    NC = T // LANES

    def body(q_ref, k_ref, v_ref, c_ref, o_ref, of_ref, lse_ref, bias, vT, acc, m_scr, l_scr):
        def prologue(i, _):
            rows = pl.ds(pl.multiple_of(i * T, T), T)
            bias[0, rows, :], bias[1, rows, :] = _key_bias(c_ref[rows, :])
            vT[i] = v_ref[rows, :].astype(F32).T.astype(BF16)
            return 0

        lax.fori_loop(0, nq, prologue, 0)

        def q_step(qi, _):
            q0 = pl.multiple_of(qi * T, T)
            qb = q_ref[pl.ds(q0, T), :]
            m_scr[...] = jnp.full(m_scr.shape, -jnp.inf, F32)
            l_scr[...] = jnp.zeros(l_scr.shape, F32)
            acc[...] = jnp.zeros(acc.shape, F32)

            def tile(kj, masked):
                ks = pl.ds(pl.multiple_of(kj * T, T), T)
                kf = k_ref[ks, :].astype(F32)
                first = _head_lanes(kf.shape)
                kms = [jnp.where(first if hh == 0 else jnp.logical_not(first), kf, 0.0).astype(BF16) for hh in range(2)]
                sTs = [lax.dot_general(km, qb, _NT, preferred_element_type=F32) for km in kms]
                for hh in range(2):
                    b = bias[hh, ks, :]
                    ps = []
                    for cc in range(NC):
                        cols = slice(cc * LANES, (cc + 1) * LANES)
                        s = sTs[hh][:, cols] + b
                        if masked:
                            s = jnp.where(_causal_t(T, cc), s, -jnp.inf)
                        m_old = m_scr[hh, cc]
                        m_new = jnp.maximum(m_old, _over_keys(s, jnp.maximum))
                        alpha = jnp.exp(m_old - m_new)
                        p = jnp.exp(s - m_new)
                        l_scr[hh, cc] = alpha * l_scr[hh, cc] + _over_keys(p, jnp.add)
                        m_scr[hh, cc] = m_new
                        ps.append(p.astype(BF16))
                        acc[hh, :, cols] = acc[hh, :, cols] * alpha
                    acc[hh] += jnp.dot(vT[kj, hh * HEAD_DIM:(hh + 1) * HEAD_DIM, :], jnp.concatenate(ps, axis=1),
                                       preferred_element_type=F32)

            def inner(kj, _):
                tile(kj, False)
                return 0

            lax.fori_loop(0, qi, inner, 0)
            tile(qi, True)
            outs = []
            for hh in range(2):
                inv = jnp.concatenate([1.0 / l_scr[hh, cc] for cc in range(NC)], axis=1)
                outs.append(acc[hh] * inv)
                for cc in range(NC):
                    lse_ref[hh:hh + 1, pl.ds(q0 + cc * LANES, LANES)] = m_scr[hh, cc] + jnp.log(l_scr[hh, cc])
            out = jnp.concatenate(outs, axis=0).T
            o_ref[pl.ds(q0, T), :] = out.astype(o_ref.dtype)
            of_ref[pl.ds(q0, T), :] = out
            return 0

        lax.fori_loop(0, nq, q_step, 0)

    blk = lambda off: pl.BlockSpec((S, LANES), lambda p: (0, off + p))
    return pl.pallas_call(
        body, name=name, grid=(HP,),
        in_specs=[blk(0), blk(0), blk(HP), blk(0)],
        out_specs=[blk(0), blk(0), pl.BlockSpec((None, 2, S), lambda p: (p, 0, 0))],
        out_shape=[_hbm_out((S, D), BF16), _hbm_out((S, D), F32),
                   _hbm_out((HP, 2, S), F32)],
        scratch_shapes=[pltpu.VMEM((2, S, LANES), F32), pltpu.VMEM((nq, LANES, T), BF16),
                        pltpu.VMEM((2, HEAD_DIM, T), F32), pltpu.VMEM((2, NC, 1, LANES), F32),
                        pltpu.VMEM((2, NC, 1, LANES), F32)],
        compiler_params=_params(("parallel",)),
    )(q, kv, kv, cfull)


def _attn_bwd(name, q, kv, cfull, of, do, lse3):
    S, D = q.shape
    HP = D // LANES
    T = _att_tile(S)
    nq = S // T
    NC = T // LANES
    scale = HEAD_DIM ** -0.5

    def body(q_ref, k_ref, v_ref, c_ref, of_ref, do_ref, lse_ref,
             dq_ref, dk_ref, dv_ref, dck_ref, drq_ref, bias, kT, dqT, delta, dr_scr):
        def prologue(i, _):
            rows = pl.ds(pl.multiple_of(i * T, T), T)
            bias[0, rows, :], bias[1, rows, :] = _key_bias(c_ref[rows, :])
            kT[i] = k_ref[rows, :].astype(F32).T.astype(BF16)
            prodT = (do_ref[rows, :].astype(F32) * of_ref[rows, :]).T
            for hh in range(2):
                delta[hh:hh + 1, rows] = jnp.sum(prodT[hh * HEAD_DIM:(hh + 1) * HEAD_DIM], axis=0, keepdims=True)
            dqT[i] = jnp.zeros((LANES, T), F32)
            return 0

        lax.fori_loop(0, nq, prologue, 0)
        dr_scr[...] = jnp.zeros(dr_scr.shape, F32)

        def kv_step(kj, _):
            ks = pl.ds(pl.multiple_of(kj * T, T), T)
            kf = k_ref[ks, :].astype(F32)
            vf = v_ref[ks, :].astype(F32)
            first = _head_lanes(kf.shape)
            masks = [first, jnp.logical_not(first)]
            kms = [jnp.where(m, kf, 0.0).astype(BF16) for m in masks]
            vms = [jnp.where(m, vf, 0.0).astype(BF16) for m in masks]

            def tile(qi, carry, masked):
                q0 = pl.multiple_of(qi * T, T)
                qb = q_ref[pl.ds(q0, T), :]
                dob = do_ref[pl.ds(q0, T), :]
                sTs = [lax.dot_general(km, qb, _NT, preferred_element_type=F32) for km in kms]
                dpTs = [lax.dot_general(vm, dob, _NT, preferred_element_type=F32) for vm in vms]
                out = []
                for hh in range(2):
                    dk_a, dv_a, dc_a = carry[3 * hh:3 * hh + 3]
                    b = bias[hh, ks, :]
                    head = slice(hh * HEAD_DIM, (hh + 1) * HEAD_DIM)
                    ps, dss = [], []
                    for cc in range(NC):
                        cols = slice(cc * LANES, (cc + 1) * LANES)
                        at = pl.ds(q0 + cc * LANES, LANES)
                        p = jnp.exp(sTs[hh][:, cols] + b - lse_ref[hh:hh + 1, at])
                        if masked:
                            p = jnp.where(_causal_t(T, cc), p, 0.0)
                        ds = p * (dpTs[hh][:, cols] - delta[hh:hh + 1, at])
                        ps.append(p.astype(BF16))
                        dss.append(ds.astype(BF16))
                        dc_a = dc_a + ds
                        dr_scr[hh:hh + 1, at] += _over_keys(ds, jnp.add)
                    pT = jnp.concatenate(ps, axis=1)
                    dsT = jnp.concatenate(dss, axis=1)
                    dv_a = dv_a + jnp.dot(pT, dob, preferred_element_type=F32)
                    dk_a = dk_a + jnp.dot(dsT, qb, preferred_element_type=F32)
                    dqT[qi, head, :] += jnp.dot(kT[kj, head, :], dsT, preferred_element_type=F32)
                    out += [dk_a, dv_a, dc_a]
                return tuple(out)

            zero = jnp.zeros((T, LANES), F32)
            carry = tile(kj, (zero,) * 6, True)
            dk0, dv0, dc0, dk1, dv1, dc1 = lax.fori_loop(kj + 1, nq, lambda qi, c: tile(qi, c, False), carry)
            dk_ref[ks, :] = jnp.where(first, dk0, dk1)
            dv_ref[ks, :] = jnp.where(first, dv0, dv1)
            dck_ref[ks, :] = jnp.where(first, jnp.broadcast_to(-jnp.sum(dc0, axis=1, keepdims=True), (T, LANES)),
                                       jnp.broadcast_to(-jnp.sum(dc1, axis=1, keepdims=True), (T, LANES)))
            return 0

        lax.fori_loop(0, nq, kv_step, 0)

        def epilogue(i, _):
            rows = pl.ds(pl.multiple_of(i * T, T), T)
            dq_ref[rows, :] = (dqT[i].T * scale).astype(dq_ref.dtype)
            return 0

        lax.fori_loop(0, nq, epilogue, 0)
        drq_ref[...] = dr_scr[...]

    blk = lambda off: pl.BlockSpec((S, LANES), lambda p: (0, off + p))
    row_spec = pl.BlockSpec((None, 2, S), lambda p: (p, 0, 0))
    return pl.pallas_call(
        body, name=name, grid=(HP,),
        in_specs=[blk(0), blk(0), blk(HP), blk(0), blk(0), blk(0), row_spec],
        out_specs=[blk(0), blk(0), blk(0), blk(0), row_spec],
        out_shape=[_hbm_out((S, D), BF16), _hbm_out((S, D), F32),
                   _hbm_out((S, D), F32), _hbm_out((S, D), F32),
                   _hbm_out((HP, 2, S), F32)],
        scratch_shapes=[pltpu.VMEM((2, S, LANES), F32), pltpu.VMEM((nq, LANES, T), BF16),
                        pltpu.VMEM((nq, LANES, T), F32), pltpu.VMEM((2, S), F32), pltpu.VMEM((2, S), F32)],
        compiler_params=_params(("parallel",)),
    )(q, kv, kv, cfull, of, do, lse3)


def _logsig_fwd(name, f):
    S, C = f.shape

    def body(f_ref, o_ref):
        o_ref[...] = -_softplus(-f_ref[...])

    spec = pl.BlockSpec((S, C), lambda i: (0, 0))
    return pl.pallas_call(body, name=name, grid=(1,), in_specs=[spec], out_specs=spec,
                          out_shape=_hbm_out((S, C), F32),
                          compiler_params=_params(("arbitrary",)))(f)


def _logsig_bwd(name, dls, f):
    S, C = f.shape

    def body(d_ref, f_ref, o_ref, s_ref):
        df = d_ref[...] * _sigmoid(-f_ref[...])
        o_ref[...] = df.astype(o_ref.dtype)
        s_ref[...] = jnp.sum(df, axis=0, keepdims=True)

    spec = pl.BlockSpec((S, C), lambda i: (0, 0))
    return pl.pallas_call(body, name=name, grid=(1,), in_specs=[spec, spec],
                          out_specs=[spec, pl.BlockSpec((1, C), lambda i: (0, 0))],
                          out_shape=[_hbm_out((S, C), BF16), _hbm_out((1, C), F32)],
                          compiler_params=_params(("arbitrary",)))(dls, f)


def _add_cast(name, parts, out_dtype, tr=256):
    S, C = parts[0].shape
    tr = _tile(S, tr)
    n = len(parts)

    def body(*refs):
        acc = refs[0][...].astype(F32)
        for r in refs[1:n]:
            acc = acc + r[...].astype(F32)
        refs[n][...] = acc.astype(out_dtype)

    spec = pl.BlockSpec((tr, C), lambda i: (i, 0))
    return pl.pallas_call(body, name=name, grid=(S // tr,), in_specs=[spec] * n, out_specs=spec,
                          out_shape=_hbm_out((S, C), out_dtype),
                          compiler_params=_params(("parallel",)))(*parts)


def _local_step(x, target, gains, layer_weights, layer_prefetch, layer_grads):
    S, D = x.shape
    HP = D // LANES
    scale = HEAD_DIM ** -0.5
    tm = _tile(S, 512)
    tx = _tile(S, 256)
    td = _tile(D, 512)
    saved = []
    h = x
    l = 0
    kv = cfull = f_pre = hn_kv = h_kv = None
    while True:
        W = layer_weights(l, "mix", h)
        if W is None:
            break
        recurrent = "w_rec_in" in W
        if l == 0:
            xn = _rmsnorm_fwd("mix_norm_0", h, gains["mix"][0])
        if recurrent:
            CH = W["w_rec_in"].shape[-1]
            C = 2 * CH
            proj = _mm(f"rec_in_{l}", "nn", xn, W["w_rec_in"], grid=(S // tm, N_CHIPS),
                       a_spec=pl.BlockSpec((tm, D), lambda i, j: (i, 0)),
                       b_spec=pl.BlockSpec((None, D, CH), lambda i, j: (j, 0, 0)),
                       out_shape=(S, 2 * C), out_dtype=F32,
                       out_spec=pl.BlockSpec((tm, CH), lambda i, j: (i, j)))
            layer_prefetch(l, "mix2", proj)
            rc, rcb = _conv_fwd(f"conv_{l}", proj, W["conv_w"], W["conv_b"])
            W = {**W, **layer_weights(l, "mix2", rcb)}
            gip, grp = _gates_fwd(f"gates_{l}", rcb, W["w_gates"], W["b_gates"])
            hrec, m = _lru_fwd(f"lru_{l}", proj, rc, gip, grp, W["lru_param"])
            layer_prefetch(l, "ffn", m)
            h_mid, hn = _mm_nn(f"rec_out_{l}", m, W["w_rec_out"], out_dtype=F32, res=h, tn=D, norm_gain=gains["ffn"][l])
            mix_saved = (xn, proj, rc, rcb, gip, grp, hrec, m)
        else:
            if "w_kv" in W:
                h_kv = h
                hn_kv = _rmsnorm_fwd("kv_norm", h, W["norm_kv"])
                kv = _mm_nn("kv_proj", hn_kv, W["w_kv"], out_dtype=BF16)
                f_pre = _mm_nn("f_proj", hn_kv, W["w_f"], out_dtype=F32, bias=W["b_f"])
                c = _cumsum_rows("c_cumsum", _logsig_fwd("logsig", f_pre), False)
                cfull = jnp.repeat(-c[:, :2 * HP], HEAD_DIM, axis=1)
            q = _mm_nn(f"q_proj_{l}", xn, W["w_q"], out_dtype=BF16, scale=scale)
            layer_prefetch(l, "mix2", q)
            o, of, lse = _attn_fwd(f"attn_fwd_{l}", q, kv, cfull)
            W = {**W, **layer_weights(l, "mix2", o)}
            layer_prefetch(l, "ffn", o)
            h_mid, hn = _mm_nn(f"o_proj_{l}", o, W["w_o"], out_dtype=F32, res=h, tn=D, norm_gain=gains["ffn"][l])
            mix_saved = (xn, q, o, of, lse)
        W = {**W, **layer_weights(l, "ffn", h_mid)}
        z3, act = _swiglu_fwd(f"ffn_in_{l}", hn, W["w_ffn_in"])
        layer_prefetch(l + 1, "mix", act)
        saved.append((W, h, h_mid, mix_saved, (hn, z3, act)))
        l += 1
        if l < len(gains["mix"]):
            h, xn = _mm_nn(f"ffn_out_{l - 1}", act, W["w_ffn_out"], out_dtype=F32, res=h_mid, tn=D,
                           norm_gain=gains["mix"][l])
        else:
            h = _mm_nn(f"ffn_out_{l - 1}", act, W["w_ffn_out"], out_dtype=F32, res=h_mid, tn=D)

    dh, dhb, dg_final, loss_row = _loss_head("loss_head", h, target, gains["final"])

    dk_parts, dv_parts, dc_parts = [], [], []
    token = None
    for l in reversed(range(len(saved))):
        W, h_in, h_mid, mix_saved, (hn, z3, act) = saved[l]
        recurrent = "w_rec_in" in W
        FH = W["w_ffn_in"].shape[-1]
        G = {}
        norm_ffn = gains["ffn"][l]
        if token is not None:
            norm_ffn = norm_ffn + jnp.minimum(token[:1, :1], 0.0)
        G["w_ffn_out"] = _mm_tn(f"d_ffn_out_{l}", act, dhb, out_dtype=BF16, tn=D)
        dz3 = _swiglu_bwd(f"d_act_{l}", dhb, W["w_ffn_out"], z3)
        G["w_ffn_in"] = _mm(
            f"d_ffn_in_{l}", "tn", hn, dz3, grid=(D // td, N_CHIPS),
            a_spec=pl.BlockSpec((S, td), lambda i, j: (0, i)),
            b_spec=pl.BlockSpec((None, S, FH), lambda i, j: (j // 2, 0, j % 2)),
            out_shape=(N_CHIPS, D, FH), out_dtype=BF16,
            out_spec=pl.BlockSpec((None, td, FH), lambda i, j: (j, i, 0)))
        ffn_token = layer_grads(l, "ffn", G)
        G = {}
        if ffn_token is not None:
            norm_ffn = norm_ffn + jnp.minimum(ffn_token[:1, :1], 0.0)
        dh, dhb, dgp = _mm(f"d_ffn_hn_{l}", "nt", dz3, W["w_ffn_in"], grid=(S // tx, 1),
                           a_spec=[pl.BlockSpec((None, tx, FH), functools.partial(lambda i, j, k: (k // 2, i, k % 2), k=k))
                                   for k in range(N_CHIPS)],
                           b_spec=[pl.BlockSpec((None, D, FH), functools.partial(lambda i, j, k: (k, 0, 0), k=k))
                                   for k in range(N_CHIPS)],
                           out_shape=(S, D), out_dtype=F32, out_spec=pl.BlockSpec((tx, D), lambda i, j: (i, 0)),
                           norm_bwd=(h_mid, norm_ffn, dh))
        G["norm_ffn"] = jnp.sum(dgp, axis=0)
        if recurrent:
            CH = W["w_rec_in"].shape[-1]
            C = 2 * CH
            xn, proj, rc, rcb, gip, grp, hrec, m = mix_saved
            G["w_rec_out"] = _mm_tn(f"d_rec_out_{l}", m, dhb, out_dtype=BF16, tn=D)
            dm = _mm_nt(f"d_m_{l}", dhb, W["w_rec_out"], out_dtype=F32, tn=C)
            dgb, dgi, dgr, drc1, G["b_gi"], G["b_gr"], G["lru_param"] = _lru_bwd(
                f"d_lru_{l}", dm, proj, hrec, rc, gip, grp, W["lru_param"])
            drc, G["w_gates"] = _gates_bwd(f"d_gates_{l}", dgi, dgr, rcb, W["w_gates"], drc1)
            mix_token = layer_grads(l, "mix2", {n: G[n] for n in ("w_rec_out", "w_gates")})
            drec, G["conv_w"], G["conv_b"] = _conv_bwd(f"d_conv_{l}", drc, proj, W["conv_w"])
            dproj = jnp.concatenate([dgb, drec], axis=1)
            norm_mix = gains["mix"][l] if mix_token is None else gains["mix"][l] + jnp.minimum(mix_token[:1, :1], 0.0)
            G["w_rec_in"] = _mm(
                f"d_rec_in_{l}", "tn", xn, dproj, grid=(1, N_CHIPS),
                a_spec=pl.BlockSpec((S, D), lambda i, j: (0, 0)),
                b_spec=pl.BlockSpec((S, CH), lambda i, j: (0, j)),
                out_shape=(N_CHIPS, D, CH), out_dtype=BF16,
                out_spec=pl.BlockSpec((None, D, CH), lambda i, j: (j, 0, 0)))
            dh, dhb, dgp = _mm(f"d_rec_xn_{l}", "nt", dproj, W["w_rec_in"], grid=(S // tx, 1),
                               a_spec=[pl.BlockSpec((tx, CH), functools.partial(lambda i, j, k: (i, k), k=k))
                                       for k in range(N_CHIPS)],
                               b_spec=[pl.BlockSpec((None, D, CH), functools.partial(lambda i, j, k: (k, 0, 0), k=k))
                                       for k in range(N_CHIPS)],
                               out_shape=(S, D), out_dtype=F32, out_spec=pl.BlockSpec((tx, D), lambda i, j: (i, 0)),
                               norm_bwd=(h_in, norm_mix, dh))
        else:
            xn, q, o, of, lse = mix_saved
            G["w_o"] = _mm_tn(f"d_o_proj_{l}", o, dhb, out_dtype=BF16, tn=D)
            do = _mm_nt(f"d_o_{l}", dhb, W["w_o"], out_dtype=BF16, tn=D)
            mix_token = layer_grads(l, "mix2", {"w_o": G["w_o"]})
            dq, dk, dv, dck, drq = _attn_bwd(f"attn_bwd_{l}", q, kv, cfull, of, do, lse)
            dk_parts.append(dk)
            dv_parts.append(dv)
            dc_parts.append(dck[:, ::HEAD_DIM] + drq.reshape(2 * HP, S).T)
            G["w_q"] = _mm_tn(f"d_q_proj_{l}", xn, dq, out_dtype=BF16, tn=D)
            norm_mix = gains["mix"][l] if mix_token is None else gains["mix"][l] + jnp.minimum(mix_token[:1, :1], 0.0)
            dh, dhb, dgp = _mm_nt(f"d_q_xn_{l}", dq, W["w_q"], out_dtype=F32, tn=D, norm_bwd=(h_in, norm_mix, dh))
        G["norm_mix"] = jnp.sum(dgp, axis=0)
        if "w_kv" in W:
            dkb = _add_cast("dk_sum", dk_parts, BF16)
            dvb = _add_cast("dv_sum", dv_parts, BF16)
            dkv = jnp.concatenate([dkb, dvb], axis=1)
            dc = sum(dc_parts[1:], dc_parts[0])
            dc_pad = jnp.pad(dc, ((0, 0), (0, LANES - 2 * HP)))
            dls = _cumsum_rows("dc_cumsum", dc_pad, True)
            dfb, G["b_f"] = _logsig_bwd("d_logsig", dls, f_pre)
            G["w_kv"] = _mm_tn("d_kv_proj", hn_kv, dkv, out_dtype=BF16)
            G["w_f"] = _mm_tn("d_f_proj", hn_kv, dfb, out_dtype=F32)
            dhn_f = _mm_nt("d_f_hn", dfb, W["w_f"], out_dtype=F32, tn=D)
            dh, dhb, dgp = _mm_nt("d_kv_hn", dkv, W["w_kv"], out_dtype=F32, tn=D, res=dhn_f,
                                  norm_bwd=(h_kv, W["norm_kv"], dh))
            G["norm_kv"] = jnp.sum(dgp, axis=0)
        token = layer_grads(l, "mix", G)
    return loss_row, dh, dg_final


_ANY = pl.BlockSpec(memory_space=pl.ANY)


def _position():
    return lax.axis_index("x"), lax.axis_index("y"), lax.axis_index("c")


def _chip_peers(x, y):
    return [(1 - x, y), (x, 1 - y), (1 - x, 1 - y)]


def _half_rows(c, n):
    h = n // 2
    assert h % 16 == 0
    return pl.ds(pl.multiple_of(c * h, 16), h)


def _place_own(name, shard, layer, me):
    _, R, C = shard.shape
    tr = _row_tile(R, C, 2 * shard.dtype.itemsize, target=8 << 20)

    def body(me_ref, x_ref, o_ref):
        o_ref[...] = x_ref[...]

    return pl.pallas_call(
        body, name=name,
        grid_spec=pltpu.PrefetchScalarGridSpec(
            num_scalar_prefetch=1, grid=(R // tr,),
            in_specs=[pl.BlockSpec((None, tr, C), lambda i, me_ref: (layer, i, 0))],
            out_specs=pl.BlockSpec((None, tr, C), lambda i, me_ref: (me_ref[0], i, 0))),
        out_shape=_hbm_out((N_CHIPS, R, C), shard.dtype),
        compiler_params=_params(("parallel",)),
    )(me, shard)


def _gather_smalls(name, smalls):
    ns = len(smalls)

    def body(*refs):
        ins, outs = refs[:ns], refs[ns:2 * ns]
        send_sems, recv_sems, local_sems = refs[2 * ns:]
        x, y, c = _position()
        me = 2 * x + y
        peers = _chip_peers(x, y)

        def remote(t, k, chip):
            px, py = peers[k]
            return pltpu.make_async_remote_copy(
                src_ref=ins[t], dst_ref=outs[t].at[chip], send_sem=send_sems.at[3 * t + k],
                recv_sem=recv_sems.at[3 * t + k], device_id=(px, py, c), device_id_type=MESH)

        local = [pltpu.make_async_copy(ins[t], outs[t].at[me], local_sems.at[t]) for t in range(ns)]
        for t in range(ns):
            local[t].start()
            for k in range(3):
                remote(t, k, me).start()
        for t in range(ns):
            for k in range(3):
                px, py = peers[k]
                remote(t, k, 2 * px + py).wait_recv()
        for t in range(ns):
            for k in range(3):
                remote(t, k, me).wait_send()
            local[t].wait()

    return pl.pallas_call(
        body, name=name, in_specs=[_ANY] * ns, out_specs=[_ANY] * ns,
        out_shape=[_hbm_out((N_CHIPS,) + s.shape, s.dtype) for s in smalls],
        scratch_shapes=[pltpu.SemaphoreType.DMA((3 * ns,)), pltpu.SemaphoreType.DMA((3 * ns,)),
                        pltpu.SemaphoreType.DMA((ns,))],
    )(*smalls)


_SEM = pl.BlockSpec(memory_space=pltpu.SEMAPHORE)
_SPLIT = pltpu.CompilerParams(has_side_effects=pltpu.SideEffectType.DATAFLOW_SIDE_EFFECTING)


def _weight_copy(shards, buf, items, sems, i, k, chip_of_dst, peers, c):
    w, l = items[i]
    px, py = peers[k]
    half = _half_rows(c, shards[w].shape[1])
    return pltpu.make_async_remote_copy(
        src_ref=shards[w].at[l, half], dst_ref=buf.at[chip_of_dst, half],
        send_sem=sems[0].at[3 * i + k], recv_sem=sems[1].at[3 * i + k],
        device_id=(px, py, c), device_id_type=MESH)


def _gather_start(name, shards, bufs, items, after):
    nw, n = len(shards), len(bufs)

    def body(*refs):
        ins, outs, sems = refs[:nw], refs[nw + n + 1:nw + 2 * n + 1], refs[nw + 2 * n + 1:]
        x, y, c = _position()
        peers = _chip_peers(x, y)
        for i in range(n):
            for k in range(3):
                _weight_copy(ins, outs[i], items, sems, i, k, 2 * x + y, peers, c).start()

    res = pl.pallas_call(
        body, name=name, in_specs=[_ANY] * (nw + n + 1), out_specs=[_ANY] * n + [_SEM, _SEM],
        out_shape=[_hbm_out(b.shape, b.dtype) for b in bufs]
        + [pltpu.SemaphoreType.DMA((3 * n,)), pltpu.SemaphoreType.DMA((3 * n,))],
        input_output_aliases={nw + i: i for i in range(n)}, compiler_params=_SPLIT,
    )(*shards, *bufs, after)
    return res[:n], res[n:]


def _gather_wait(name, shards, bufs, items, ids, sems, after):
    nw, m = len(shards), len(ids)

    def body(*refs):
        ins, bs = refs[:nw], refs[nw:nw + m]
        sem_refs = refs[nw + m:nw + m + 2]
        x, y, c = _position()
        peers = _chip_peers(x, y)
        for j, i in enumerate(ids):
            for k in range(3):
                px, py = peers[k]
                _weight_copy(ins, bs[j], items, sem_refs, i, k, 2 * px + py, peers, c).wait_recv()
        for j, i in enumerate(ids):
            for k in range(3):
                _weight_copy(ins, bs[j], items, sem_refs, i, k, 2 * x + y, peers, c).wait_send()

    res = pl.pallas_call(
        body, name=name, in_specs=[_ANY] * (nw + m) + [_SEM, _SEM, _ANY], out_specs=[_ANY] * m,
        out_shape=[_hbm_out(bufs[i].shape, bufs[i].dtype) for i in ids],
        input_output_aliases={nw + j: j for j in range(m)}, compiler_params=_SPLIT,
    )(*shards, *[bufs[i] for i in ids], *sems, after)
    return list(res)


def _forward_copy(src, dst, sems, i, k, core):
    x, y, c = _position()
    px, py = _chip_peers(x, y)[k]
    half = _half_rows(core, src.shape[1])
    return pltpu.make_async_remote_copy(
        src_ref=src.at[2 * px + py, half], dst_ref=dst.at[2 * px + py, half],
        send_sem=sems[0].at[3 * i + k], recv_sem=sems[1].at[3 * i + k],
        device_id=(x, y, 1 - c), device_id_type=MESH)


def _forward_start(name, bufs):
    n = len(bufs)

    def body(*refs):
        ins, outs, sems = refs[:n], refs[n:2 * n], refs[2 * n:]
        c = lax.axis_index("c")
        for i in range(n):
            for k in range(3):
                _forward_copy(ins[i], outs[i], sems, i, k, c).start()

    res = pl.pallas_call(
        body, name=name, in_specs=[_ANY] * n, out_specs=[_ANY] * n + [_SEM, _SEM],
        out_shape=[_hbm_out(g.shape, g.dtype) for g in bufs]
        + [pltpu.SemaphoreType.DMA((3 * n,)), pltpu.SemaphoreType.DMA((3 * n,))],
        input_output_aliases={i: i for i in range(n)}, compiler_params=_SPLIT,
    )(*bufs)
    return list(res[:n]), res[n:]


def _forward_wait(name, bufs, sems, after):
    n = len(bufs)

    def body(*refs):
        bs, sem_refs = refs[:n], refs[n:n + 2]
        c = lax.axis_index("c")
        for i in range(n):
            for k in range(3):
                _forward_copy(bs[i], bs[i], sem_refs, i, k, 1 - c).wait_recv()
        for i in range(n):
            for k in range(3):
                _forward_copy(bs[i], bs[i], sem_refs, i, k, c).wait_send()

    return list(pl.pallas_call(
        body, name=name, in_specs=[_ANY] * n + [_SEM, _SEM, _ANY], out_specs=[_ANY] * n,
        out_shape=[_hbm_out(g.shape, g.dtype) for g in bufs],
        input_output_aliases={i: i for i in range(n)}, compiler_params=_SPLIT,
    )(*bufs, *sems, after))


def _reduce_copy(grads, others, sems, i):
    x, y, c = _position()
    return pltpu.make_async_remote_copy(
        src_ref=grads[i].at[:, _half_rows(1 - c, grads[i].shape[1])], dst_ref=others[i],
        send_sem=sems[0].at[i], recv_sem=sems[1].at[i], device_id=(x, y, 1 - c), device_id_type=MESH)


def _reduce_start(name, grads, after):
    n = len(grads)

    def body(*refs):
        ins, outs, sems, token = refs[:n], refs[n + 1:2 * n + 1], refs[2 * n + 1:2 * n + 3], refs[2 * n + 3]
        for i in range(n):
            _reduce_copy(ins, outs, sems, i).start()
        token[...] = jnp.zeros_like(token)

    res = pl.pallas_call(
        body, name=name, in_specs=[_ANY] * (n + 1),
        out_specs=[_ANY] * n + [_SEM, _SEM, pl.BlockSpec(memory_space=pltpu.VMEM)],
        out_shape=[_hbm_out((N_CHIPS, g.shape[1] // 2, g.shape[2]), g.dtype) for g in grads]
        + [pltpu.SemaphoreType.DMA((n,)), pltpu.SemaphoreType.DMA((n,)), jax.ShapeDtypeStruct((SUBLANES, LANES), F32)],
        compiler_params=_SPLIT,
    )(*grads, after)
    return list(res[:n]), res[n:n + 2], res[n + 2]


def _reduce_wait(name, grads, others, sems, after):
    n = len(grads)

    def body(*refs):
        ins, os_, sem_refs = refs[:n], refs[n:2 * n], refs[2 * n:2 * n + 2]
        for i in range(n):
            _reduce_copy(ins, os_, sem_refs, i).wait_recv()
        for i in range(n):
            _reduce_copy(ins, os_, sem_refs, i).wait_send()

    return list(pl.pallas_call(
        body, name=name, in_specs=[_ANY] * (2 * n) + [_SEM, _SEM, _ANY], out_specs=[_ANY] * n,
        out_shape=[_hbm_out(o.shape, o.dtype) for o in others],
        input_output_aliases={n + i: i for i in range(n)}, compiler_params=_SPLIT,
    )(*grads, *others, *sems, after))


def _sum_cores(name, g, other, core):
    _, R, C = g.shape
    H = R // 2
    tr = _row_tile(H, C, 3 * 2, target=12 << 20)
    nb = H // tr

    def body(c_ref, g_ref, o_ref, out_ref):
        out_ref[...] = (g_ref[...].astype(F32) + o_ref[...].astype(F32)).astype(out_ref.dtype)

    return pl.pallas_call(
        body, name=name,
        grid_spec=pltpu.PrefetchScalarGridSpec(
            num_scalar_prefetch=1, grid=(N_CHIPS, nb),
            in_specs=[pl.BlockSpec((None, tr, C), lambda j, i, c_ref: (j, c_ref[0] * nb + i, 0)),
                      pl.BlockSpec((None, tr, C), lambda j, i, c_ref: (j, i, 0))],
            out_specs=pl.BlockSpec((None, tr, C), lambda j, i, c_ref: (j, i, 0))),
        out_shape=_hbm_out((N_CHIPS, H, C), BF16),
        compiler_params=_params(("parallel", "parallel")),
    )(core, g, other)


def _sum_chips(name, received, own, full, layer, me_core):
    _, H, C = received.shape
    tr = _row_tile(H, C, 3 * 2 + 2 + 4, target=12 << 20)
    nb = H // tr

    def body(s_ref, r_ref, own_ref, full_ref, out_ref):
        acc = r_ref[0].astype(F32)
        for k in (1, 2):
            acc = acc + r_ref[k].astype(F32)
        out_ref[...] = acc + own_ref[...].astype(F32)

    return pl.pallas_call(
        body, name=name,
        grid_spec=pltpu.PrefetchScalarGridSpec(
            num_scalar_prefetch=1, grid=(nb,),
            in_specs=[pl.BlockSpec((3, tr, C), lambda i, s_ref: (0, i, 0)),
                      pl.BlockSpec((None, tr, C), lambda i, s_ref: (s_ref[0], i, 0)),
                      _ANY],
            out_specs=pl.BlockSpec((None, tr, C), lambda i, s_ref: (layer, s_ref[1] * nb + i, 0))),
        out_shape=_hbm_out(full.shape, full.dtype),
        input_output_aliases={3: 0},
        compiler_params=_params(("parallel",)),
    )(me_core, received, own, full)


def _part_copy(parts, recv, sems, i, k, peers, c):
    px, py = peers[k]
    return pltpu.make_async_remote_copy(
        src_ref=parts[i].at[2 * px + py], dst_ref=recv[i].at[k],
        send_sem=sems[0].at[3 * i + k], recv_sem=sems[1].at[3 * i + k],
        device_id=(px, py, c), device_id_type=MESH)


def _scatter_start(name, parts):
    n = len(parts)

    def body(*refs):
        ins, outs, sems, token = refs[:n], refs[n:2 * n], refs[2 * n:2 * n + 2], refs[2 * n + 2]
        x, y, c = _position()
        peers = _chip_peers(x, y)
        for i in range(n):
            for k in range(3):
                _part_copy(ins, outs, sems, i, k, peers, c).start()
        token[...] = jnp.zeros_like(token)

    res = pl.pallas_call(
        body, name=name, in_specs=[_ANY] * n,
        out_specs=[_ANY] * n + [_SEM, _SEM, pl.BlockSpec(memory_space=pltpu.VMEM)],
        out_shape=[_hbm_out((3,) + p.shape[1:], p.dtype) for p in parts]
        + [pltpu.SemaphoreType.DMA((3 * n,)), pltpu.SemaphoreType.DMA((3 * n,)),
           jax.ShapeDtypeStruct((SUBLANES, LANES), F32)],
        compiler_params=_SPLIT,
    )(*parts)
    return list(res[:n]), res[n:n + 2], res[n + 2]


def _scatter_wait(name, parts, recv, sems):
    n = len(parts)

    def body(*refs):
        ins, rs, sem_refs = refs[:n], refs[n:2 * n], refs[2 * n:2 * n + 2]
        x, y, c = _position()
        peers = _chip_peers(x, y)
        for i in range(n):
            for k in range(3):
                _part_copy(ins, rs, sem_refs, i, k, peers, c).wait_recv()
        for i in range(n):
            for k in range(3):
                _part_copy(ins, rs, sem_refs, i, k, peers, c).wait_send()

    return list(pl.pallas_call(
        body, name=name, in_specs=[_ANY] * (2 * n) + [_SEM, _SEM], out_specs=[_ANY] * n,
        out_shape=[_hbm_out(r.shape, r.dtype) for r in recv],
        input_output_aliases={n + i: i for i in range(n)}, compiler_params=_SPLIT,
    )(*parts, *recv, *sems))


def _share_d2d(name, full):
    n = len(full)

    def body(*refs):
        ins, outs = refs[:n], refs[n:2 * n]
        send_sems, recv_sems = refs[2 * n:]
        x, y, c = _position()

        def remote(w, core):
            half = _half_rows(core, ins[w].shape[1])
            return pltpu.make_async_remote_copy(
                src_ref=ins[w].at[:, half], dst_ref=outs[w].at[:, half],
                send_sem=send_sems.at[w], recv_sem=recv_sems.at[w],
                device_id=(x, y, 1 - c), device_id_type=MESH)

        for w in range(n):
            remote(w, c).start()
        for w in range(n):
            remote(w, 1 - c).wait_recv()
        for w in range(n):
            remote(w, c).wait_send()

    return pl.pallas_call(
        body, name=name, in_specs=[_ANY] * n, out_specs=[_ANY] * n,
        out_shape=[_hbm_out(f.shape, f.dtype) for f in full],
        input_output_aliases={w: w for w in range(n)},
        scratch_shapes=[pltpu.SemaphoreType.DMA((n,)), pltpu.SemaphoreType.DMA((n,))],
    )(*full)


def _gather_all(name, a):
    def body(a_ref, o_ref, send_sems, recv_sems, local_sem):
        x, y, c = _position()
        me = 4 * x + 2 * y + c

        def peer(k):
            return (x ^ ((k >> 2) & 1), y ^ ((k >> 1) & 1), c ^ (k & 1))

        def remote(k, slot):
            return pltpu.make_async_remote_copy(
                src_ref=a_ref, dst_ref=o_ref.at[slot], send_sem=send_sems.at[k - 1], recv_sem=recv_sems.at[k - 1],
                device_id=peer(k), device_id_type=MESH)

        local = pltpu.make_async_copy(a_ref, o_ref.at[me], local_sem)
        local.start()
        for k in range(1, N_DEV):
            remote(k, me).start()
        for k in range(1, N_DEV):
            px, py, pc = peer(k)
            remote(k, 4 * px + 2 * py + pc).wait_recv()
        for k in range(1, N_DEV):
            remote(k, me).wait_send()
        local.wait()

    return pl.pallas_call(
        body, name=name, in_specs=[_ANY], out_specs=_ANY,
        out_shape=_hbm_out((N_DEV,) + a.shape, a.dtype),
        scratch_shapes=[pltpu.SemaphoreType.DMA((N_DEV - 1,)), pltpu.SemaphoreType.DMA((N_DEV - 1,)),
                        pltpu.SemaphoreType.DMA],
    )(a)


def _rows2d(a, lead=0):
    return a.reshape(a.shape[:lead] + (-1, a.shape[-1]))


def _row_tile(rows, cols, itemsize=4, target=1 << 20):
    want = max(SUBLANES, target // (cols * itemsize))
    t = min(rows, (want // 16) * 16)
    while t > 16 and rows % t:
        t -= 16
    return t if rows % t == 0 else rows


def _sum_slots(name, r, out_dtype=F32):
    ns = r.shape[0]
    r2 = _rows2d(r, 1)
    _, rows, cols = r2.shape
    tr = _row_tile(rows, cols)

    def body(r_ref, o_ref):
        acc = r_ref[0].astype(F32)
        for s in range(1, ns):
            acc = acc + r_ref[s].astype(F32)
        o_ref[...] = acc.astype(o_ref.dtype)

    out = pl.pallas_call(
        body, name=name, grid=(rows // tr,),
        in_specs=[pl.BlockSpec((ns, tr, cols), lambda i: (0, i, 0))],
        out_specs=pl.BlockSpec((tr, cols), lambda i: (i, 0)),
        out_shape=_hbm_out((rows, cols), out_dtype),
        compiler_params=_params(("parallel",)),
    )(r2)
    return out.reshape(r.shape[1:])


def _adamw(name, g_parts, w, m, v):
    shape = w.shape
    ng = len(g_parts)
    args = [_rows2d(a) for a in (*g_parts, w, m, v)]
    rows, cols = args[0].shape
    tr = _row_tile(rows, cols, (ng + 7) * 4, target=16 << 20)
    c1 = 1.0 - ADAM_B1 ** ADAM_STEP
    c2 = 1.0 - ADAM_B2 ** ADAM_STEP

    def body(*refs):
        g = refs[0][...]
        for r in refs[1:ng]:
            g = g + r[...]
        w_ref, m_ref, v_ref = refs[ng:ng + 3]
        g_out, d_out, m_out, v_out = refs[ng + 3:]
        mn = ADAM_B1 * m_ref[...] + (1.0 - ADAM_B1) * g
        vn = ADAM_B2 * v_ref[...] + (1.0 - ADAM_B2) * (g * g)
        m_hat = mn / c1
        v_hat = vn / c2
        g_out[...] = g
        d_out[...] = -ADAM_LR * (m_hat / (jnp.sqrt(v_hat) + ADAM_EPS) + ADAM_WD * w_ref[...])
        m_out[...] = mn
        v_out[...] = vn

    spec = pl.BlockSpec((tr, cols), lambda i: (i, 0))
    outs = pl.pallas_call(
        body, name=name, grid=(rows // tr,), in_specs=[spec] * (ng + 3), out_specs=[spec] * 4,
        out_shape=[_hbm_out((rows, cols), F32)] * 4,
        compiler_params=_params(("parallel",)),
    )(*args)
    return tuple(o.reshape(shape) for o in outs)


_WEIGHTS = ["norm_mix", "norm_ffn", "w_ffn_in", "w_ffn_out", "w_rec_in", "conv_w", "conv_b", "w_lru_gates",
            "b_lru_gates", "lru_param", "w_rec_out", "norm_kv", "w_kvf", "b_forget", "w_q", "w_o", "norm_final"]
_BIG = ["w_ffn_in", "w_ffn_out", "w_rec_in", "w_lru_gates", "w_rec_out", "w_kvf", "w_q", "w_o"]


def _stack3(a):
    return a[None] if a.ndim == 2 else a.reshape(a.shape[0], -1, a.shape[-1])


def _pad_lanes(a, n):
    return jnp.pad(a, ((0, 0),) * (a.ndim - 1) + ((0, n - a.shape[-1]),))


def kernel(x, norm_mix, norm_ffn, w_ffn_in, w_ffn_out, w_rec_in, conv_w, conv_b, w_lru_gates, b_lru_gates, lru_param, w_rec_out, norm_kv, w_kvf, b_forget, w_q, w_o, norm_final, loss_target, m_norm_mix, m_norm_ffn, m_w_ffn_in, m_w_ffn_out, m_w_rec_in, m_conv_w, m_conv_b, m_w_lru_gates, m_b_lru_gates, m_lru_param, m_w_rec_out, m_norm_kv, m_w_kvf, m_b_forget, m_w_q, m_w_o, m_norm_final, v_norm_mix, v_norm_ffn, v_w_ffn_in, v_w_ffn_out, v_w_rec_in, v_conv_w, v_conv_b, v_w_lru_gates, v_b_lru_gates, v_lru_param, v_w_rec_out, v_norm_kv, v_w_kvf, v_b_forget, v_w_q, v_w_o, v_norm_final):
    P = dict(norm_mix=norm_mix, norm_ffn=norm_ffn, w_ffn_in=w_ffn_in, w_ffn_out=w_ffn_out, w_rec_in=w_rec_in,
             conv_w=conv_w, conv_b=conv_b, w_lru_gates=w_lru_gates, b_lru_gates=b_lru_gates, lru_param=lru_param,
             w_rec_out=w_rec_out, norm_kv=norm_kv, w_kvf=w_kvf, b_forget=b_forget, w_q=w_q, w_o=w_o,
             norm_final=norm_final)
    M1 = dict(norm_mix=m_norm_mix, norm_ffn=m_norm_ffn, w_ffn_in=m_w_ffn_in, w_ffn_out=m_w_ffn_out,
              w_rec_in=m_w_rec_in, conv_w=m_conv_w, conv_b=m_conv_b, w_lru_gates=m_w_lru_gates,
              b_lru_gates=m_b_lru_gates, lru_param=m_lru_param, w_rec_out=m_w_rec_out, norm_kv=m_norm_kv,
              w_kvf=m_w_kvf, b_forget=m_b_forget, w_q=m_w_q, w_o=m_w_o, norm_final=m_norm_final)
    M2 = dict(norm_mix=v_norm_mix, norm_ffn=v_norm_ffn, w_ffn_in=v_w_ffn_in, w_ffn_out=v_w_ffn_out,
              w_rec_in=v_w_rec_in, conv_w=v_conv_w, conv_b=v_conv_b, w_lru_gates=v_w_lru_gates,
              b_lru_gates=v_b_lru_gates, lru_param=v_lru_param, w_rec_out=v_w_rec_out, norm_kv=v_norm_kv,
              w_kvf=v_w_kvf, b_forget=v_b_forget, w_q=v_w_q, w_o=v_w_o, norm_final=v_norm_final)

    _, S, D = x.shape
    L = norm_mix.shape[0]
    NA, NBLK, BW, GS = w_lru_gates.shape
    NB = w_q.shape[0]
    C = NBLK * BW
    CS = C // N_CHIPS
    H = b_forget.shape[0]
    assert C == D and H * HEAD_DIM == D and H <= LANES
    chip = 2 * lax.axis_index("x") + lax.axis_index("y")

    small_a = jnp.concatenate([conv_w, conv_b[:, None], lru_param[:, None]], axis=1)
    small_a, b_gates = _gather_smalls("gather_smalls", [small_a, b_lru_gates])
    small_a = small_a.transpose(1, 2, 0, 3).reshape(NA, 6, C)
    b_gates = b_gates.transpose(1, 2, 0, 3).reshape(NA, NBLK, 1, N_CHIPS * GS)
    shards = [_stack3(P[w]).astype(BF16) for w in _BIG]
    core = lax.axis_index("c")
    chip_id = jnp.reshape(chip, (1,)).astype(jnp.int32)
    core_id = jnp.reshape(core, (1,)).astype(jnp.int32)
    me_core = jnp.stack([chip, core]).astype(jnp.int32)

    parts_of_layer = ("mix", "mix2", "ffn")

    def part_items(l, part):
        if part == "ffn":
            names, at = ["w_ffn_in", "w_ffn_out"], l
        elif l < NA:
            names, at = (["w_rec_in"] if part == "mix" else ["w_lru_gates", "w_rec_out"]), l
        else:
            names, at = ((["w_kvf"] if l == NA else []) + ["w_q"] if part == "mix" else ["w_o"]), l - NA
        return [(_BIG.index(n), 0 if n == "w_kvf" else at) for n in names]

    def stage_of(l, part):
        return (l, part) if l == 0 or part == "ffn" else (l, "mixer")

    def stage_items(st):
        l, part = st
        return [it for p in (("mix", "mix2") if part == "mixer" else (part,)) for it in part_items(l, p)]

    stages = [(0, p) for p in parts_of_layer] + [(l, p) for l in range(1, L) for p in ("mixer", "ffn")]
    items = [it for st in stages for it in stage_items(st)]
    ids_of = {st: [items.index(it) for it in stage_items(st)] for st in stages}
    bufs = [_place_own(f"place_{_BIG[w]}_{li}", shards[w], li, chip_id) for w, li in items]
    bufs, gather_sems = _gather_start("gather_start", shards, bufs, items, small_a)

    forwarding, fetched = {}, {}

    def layer_prefetch(l, part, after):
        st = stage_of(l, part)
        if l < L and st not in forwarding:
            got = _gather_wait(f"gather_wait_{st[1]}_{l}", shards, bufs, items, ids_of[st], gather_sems, after)
            forwarding[st] = _forward_start(f"forward_start_{st[1]}_{l}", got)

    def layer_weights(l, part, after):
        if l >= L:
            return None
        st = stage_of(l, part)
        if st not in fetched:
            layer_prefetch(l, part, after)
            got, sems = forwarding[st]
            got = _forward_wait(f"forward_wait_{st[1]}_{l}", got, sems, after)
            fetched[st] = {_BIG[items[i][0]]: g for i, g in zip(ids_of[st], got)}
        B = fetched[st]
        if part == "ffn":
            return dict(w_ffn_in=B["w_ffn_in"], w_ffn_out=B["w_ffn_out"].reshape(-1, D))
        if l < NA and part == "mix":
            return dict(w_rec_in=B["w_rec_in"], conv_w=small_a[l, :4], conv_b=small_a[l, 4:5])
        if l < NA:
            return dict(w_gates=B["w_lru_gates"].reshape(N_CHIPS, NBLK, BW, GS).transpose(1, 2, 0, 3).reshape(
                NBLK, BW, N_CHIPS * GS), b_gates=b_gates[l], w_rec_out=B["w_rec_out"].reshape(C, D),
                lru_param=small_a[l, 5:6])
        if part == "mix2":
            return dict(w_o=B["w_o"].reshape(D, D))
        W = dict(w_q=B["w_q"].reshape(D, D))
        if l == NA:
            w_kvf_full = B["w_kvf"].transpose(1, 0, 2).reshape(D, -1)
            W.update(norm_kv=norm_kv[None], w_kv=w_kvf_full[:, :2 * D],
                     w_f=_pad_lanes(w_kvf_full[:, 2 * D:], LANES), b_f=_pad_lanes(b_forget[None], LANES))
        return W

    G_small = {l: {} for l in range(L)}
    stash = {st: {} for st in stages}
    pending = {}
    reducing = []

    def finish_reduce(after):
        st, its, grads, others, sems = reducing.pop()
        l, part = st
        others = _reduce_wait(f"reduce_wait_{part}_{l}", grads, others, sems, after)
        parts = [_sum_cores(f"sum_cores_{l}_{_BIG[w]}", g, o, core_id) for (w, _), g, o in zip(its, grads, others)]
        recv, sems, token = _scatter_start(f"scatter_start_{part}_{l}", parts)
        pending[st] = (parts, recv, sems)
        return token

    def layer_grads(l, part, G_part):
        G_small[l].update(G_part)
        st = stage_of(l, part)
        stash[st].update(G_part)
        if st[1] == "mixer" and part != "mix":
            return None
        G = stash[st]
        late = {"ffn": "w_ffn_in", "mix": "norm_mix"}.get(part) or ("w_gates" if l < NA else "w_o")
        after = finish_reduce(G_part[late]) if reducing else jnp.zeros((SUBLANES, LANES), F32)
        by_name = dict(
            w_ffn_in=lambda: G["w_ffn_in"], w_ffn_out=lambda: G["w_ffn_out"].reshape(N_CHIPS, -1, D),
            w_rec_in=lambda: G["w_rec_in"],
            w_lru_gates=lambda: G["w_gates"].reshape(NBLK, BW, N_CHIPS, GS).transpose(2, 0, 1, 3).reshape(
                N_CHIPS, NBLK * BW, GS),
            w_rec_out=lambda: G["w_rec_out"].reshape(N_CHIPS, -1, D),
            w_kvf=lambda: jnp.concatenate([G["w_kv"].astype(F32), G["w_f"][:, :H]], axis=1).reshape(
                D, N_CHIPS, -1).transpose(1, 0, 2).astype(BF16),
            w_q=lambda: G["w_q"].reshape(N_CHIPS, -1, D), w_o=lambda: G["w_o"].reshape(N_CHIPS, -1, D))
        its = stage_items(st)
        grads = [by_name[_BIG[w]]() for w, _ in its]
        others, sems, token = _reduce_start(f"reduce_start_{st[1]}_{l}", grads, after)
        reducing.append((st, its, grads, others, sems))
        return finish_reduce(token) if l == 0 else token

    gains = dict(mix=[norm_mix[l][None] for l in range(L)], ffn=[norm_ffn[l][None] for l in range(L)],
                 final=norm_final[None])
    loss_row, grad_x, dg_final = _local_step(x.reshape(S, D), loss_target.reshape(S, D), gains,
                                             layer_weights, layer_prefetch, layer_grads)

    rows = [*[G_small[l]["norm_mix"] for l in range(L)], *[G_small[l]["norm_ffn"] for l in range(L)],
            G_small[NA]["norm_kv"], dg_final, _pad_lanes(G_small[NA]["b_f"], D), _pad_lanes(loss_row, D)]
    for a in range(NA):
        rows += [G_small[a][n] for n in ("conv_w", "conv_b", "b_gi", "b_gr", "lru_param")]
    packed = jnp.concatenate(rows, axis=0)
    tot = _sum_slots("sum_small", _gather_all("gather_small", packed))
    loss = tot[2 * L + 3, 0]
    g_rep = jnp.concatenate([tot[:2 * L + 2], tot[2 * L + 2:2 * L + 3]], axis=0)
    base = 2 * L + 4
    g_sh = []
    for a in range(NA):
        blk = lax.dynamic_slice_in_dim(tot[base + 8 * a:base + 8 * a + 8], chip * CS, CS, axis=1)
        gi = tot[base + 8 * a + 5].reshape(NBLK, BW)
        gr = tot[base + 8 * a + 6].reshape(NBLK, BW)
        bl = lax.dynamic_slice_in_dim(jnp.concatenate([gi, gr], axis=1), chip * GS, GS, axis=1)
        g_sh += [blk[:5], bl.reshape(-1, CS), blk[7:8]]
    g_sh = jnp.concatenate(g_sh, axis=0)
    nrow = g_sh.shape[0] // NA

    def pack_rep(T):
        return jnp.concatenate([T["norm_mix"], T["norm_ffn"], T["norm_kv"][None], T["norm_final"][None],
                                _pad_lanes(T["b_forget"][None], D)], axis=0)

    def pack_sh(T):
        return jnp.concatenate([jnp.concatenate([T["conv_w"][a], T["conv_b"][a][None],
                                                 T["b_lru_gates"][a].reshape(-1, CS), T["lru_param"][a][None]], axis=0)
                                for a in range(NA)], axis=0)

    rep = _adamw("adamw_replicated", [g_rep], pack_rep(P), pack_rep(M1), pack_rep(M2))
    shd = _adamw("adamw_small_sharded", [g_sh], pack_sh(P), pack_sh(M1), pack_sh(M2))

    def unpack_rep(t):
        return dict(norm_mix=t[:L], norm_ffn=t[L:2 * L], norm_kv=t[2 * L], norm_final=t[2 * L + 1],
                    b_forget=t[2 * L + 2, :H])

    def unpack_sh(t):
        t = t.reshape(NA, nrow, CS)
        return dict(conv_w=t[:, :4], conv_b=t[:, 4], b_lru_gates=t[:, 5:nrow - 1].reshape(NA, NBLK, GS),
                    lru_param=t[:, nrow - 1])

    full = [lax.empty(sh.shape, F32) for sh in shards]
    for st in reversed(stages):
        l, part = st
        parts, recv, sems = pending[st]
        recv = _scatter_wait(f"scatter_wait_{part}_{l}", parts, recv, sems)
        for (w, li), own, r in zip(stage_items(st), parts, recv):
            full[w] = _sum_chips(f"sum_chips_{l}_{_BIG[w]}", r, own, full[w], li, me_core)
    full = _share_d2d("share_d2d", full)
    big = {w: _adamw(f"adamw_{w}", [g.reshape(P[w].shape)], P[w], M1[w], M2[w]) for w, g in zip(_BIG, full)}

    outs = []
    for i in range(4):
        small = {**unpack_rep(rep[i]), **unpack_sh(shd[i])}
        outs.append([big[w][i] if w in big else small[w] for w in _WEIGHTS])
    return (loss, grad_x.reshape(1, S, D), *outs[0], *outs[1], *outs[2], *outs[3])
```

```python
import functools
import math

import jax
import jax.numpy as jnp
from jax import lax
from jax.experimental import pallas as pl
from jax.experimental.pallas import tpu as pltpu

F32 = jnp.float32
BF16 = jnp.bfloat16

EPS = 1e-6
LRU_C = 8.0
HEAD_DIM = 64
LANES = 128
SUBLANES = 8
VMEM_LIMIT = 48 * 1024 * 1024
N_CHIPS = 4
N_DEV = 8

ADAM_LR = 0.001
ADAM_B1 = 0.9
ADAM_B2 = 0.999
ADAM_EPS = 1e-08
ADAM_WD = 0.01
ADAM_STEP = 10

_NN = (((1,), (0,)), ((), ()))
_NT = (((1,), (1,)), ((), ()))
_TN = (((0,), (0,)), ((), ()))
_DN = {"nn": _NN, "nt": _NT, "tn": _TN}
MESH = pl.DeviceIdType.MESH


def _hbm_out(shape, dtype):
    return pltpu.HBM(shape, dtype)


def _params(sem):
    return pltpu.CompilerParams(dimension_semantics=sem, vmem_limit_bytes=VMEM_LIMIT)


def _tile(n, want):
    if n <= want:
        return n
    t = (want // LANES) * LANES
    while t >= LANES:
        if n % t == 0:
            return t
        t -= LANES
    return n


def _sigmoid(x):
    return 1.0 / (1.0 + jnp.exp(-x))


def _sigmoid_t(x):
    return 0.5 * jnp.tanh(0.5 * x) + 0.5


def _softplus(x):
    return jnp.maximum(x, 0.0) + jnp.log(1.0 + jnp.exp(-jnp.abs(x)))


_GELU_C = math.sqrt(2.0 / math.pi)


def _gelu_and_grad(x):
    inner = _GELU_C * (x + 0.044715 * x * x * x)
    t = jnp.tanh(inner)
    g = 0.5 * x * (1.0 + t)
    dg = 0.5 * (1.0 + t) + 0.5 * x * (1.0 - t * t) * _GELU_C * (1.0 + 3.0 * 0.044715 * x * x)
    return g, dg


def _rms(x):
    return lax.rsqrt(jnp.mean(x * x, axis=-1, keepdims=True) + EPS)


def _rms_bwd(dy, x, g):
    r = _rms(x)
    xr = x * r
    dyg = dy * g
    return r * dyg - xr * (r * jnp.mean(dyg * xr, axis=-1, keepdims=True)), jnp.sum(dy * xr, axis=0, keepdims=True)


def _mm(name, mode, a, b, *, grid, a_spec, b_spec, out_shape, out_dtype, out_spec, nk=1,
        res=None, res_spec=None, bias=None, bias_spec=None, scale=None, norm_gain=None, norm_bwd=None):
    dn = _DN[mode]
    has_res, has_bias = res is not None, bias is not None
    blk = tuple(d for d in out_spec.block_shape if d is not None)
    vec = pl.BlockSpec((1, blk[-1]), lambda *g: (0, 0))
    a_specs = a_spec if isinstance(a_spec, list) else [a_spec]
    b_specs = b_spec if isinstance(b_spec, list) else [b_spec]
    npair = len(a_specs)
    n_in = 2 * npair + int(has_res) + int(has_bias) + (1 if norm_gain is not None else 0) + (3 if norm_bwd else 0)

    def body(*refs):
        p = 2 * npair
        r_ref = refs[p] if has_res else None
        p += int(has_res)
        bias_ref = refs[p] if has_bias else None
        p += int(has_bias)
        extra = refs[p:n_in]
        outs = refs[n_in:]
        o_ref = outs[0]
        part = lax.dot_general(refs[0][...], refs[npair][...], dn, preferred_element_type=F32)
        for t in range(1, npair):
            part = part + lax.dot_general(refs[t][...], refs[npair + t][...], dn, preferred_element_type=F32)

        def finish(acc):
            if scale is not None:
                acc = acc * scale
            if has_bias:
                acc = acc + bias_ref[...]
            if has_res:
                acc = r_ref[...] + acc
            if norm_bwd:
                h_ref, g_ref, dh_ref = extra
                dx, dg = _rms_bwd(acc, h_ref[...], g_ref[...])
                acc = dh_ref[...] + dx
                outs[1][...] = acc.astype(BF16)
                outs[2][...] = dg
            if norm_gain is not None:
                outs[1][...] = (acc * _rms(acc) * extra[0][...]).astype(BF16)
            o_ref[...] = acc.astype(o_ref.dtype)

        if nk == 1:
            finish(part)
        else:
            acc_ref = refs[-1]
            k = pl.program_id(2)

            @pl.when(k == 0)
            def _():
                acc_ref[...] = part

            @pl.when(k > 0)
            def _():
                acc_ref[...] += part

            @pl.when(k == nk - 1)
            def _():
                finish(acc_ref[...])

    ins, specs = [a] * npair + [b] * npair, a_specs + b_specs
    if has_res:
        ins.append(res)
        specs.append(res_spec)
    if has_bias:
        ins.append(bias)
        specs.append(bias_spec)
    out_specs, out_shapes = [out_spec], [_hbm_out(out_shape, out_dtype)]
    if norm_gain is not None:
        ins.append(norm_gain)
        specs.append(vec)
        out_specs.append(out_spec)
        out_shapes.append(_hbm_out(out_shape, BF16))
    if norm_bwd:
        h, g, dh = norm_bwd
        ins += [h, g, dh]
        specs += [out_spec, vec, out_spec]
        out_specs += [out_spec, pl.BlockSpec((None, 1, blk[-1]), lambda i, *rest: (i, 0, 0))]
        out_shapes += [_hbm_out(out_shape, BF16), _hbm_out((grid[0], 1, blk[-1]), F32)]
    sem = ("parallel", "parallel") + (("arbitrary",) if len(grid) == 3 else ())
    single = len(out_specs) == 1
    return pl.pallas_call(
        body, name=name, grid=grid, in_specs=specs, out_specs=out_specs[0] if single else out_specs,
        out_shape=out_shapes[0] if single else out_shapes,
        scratch_shapes=[pltpu.VMEM(blk, F32)] if nk > 1 else [],
        compiler_params=_params(sem),
    )(*ins)


def _mm_nn(name, a, b, *, b_lead=(), out_dtype, tm=512, tn=512, res=None, bias=None, scale=None, norm_gain=None):
    M, K = a.shape
    N = b.shape[-1]
    tm, tn = _tile(M, tm), _tile(N, tn)
    nl = len(b_lead)
    return _mm(
        name, "nn", a, b, grid=(M // tm, N // tn),
        a_spec=pl.BlockSpec((tm, K), lambda i, j: (i, 0)),
        b_spec=pl.BlockSpec((None,) * nl + (K, tn), lambda i, j: tuple(b_lead) + (0, j)),
        out_shape=(M, N), out_dtype=out_dtype, out_spec=pl.BlockSpec((tm, tn), lambda i, j: (i, j)),
        res=res, res_spec=pl.BlockSpec((tm, tn), lambda i, j: (i, j)),
        bias=bias, bias_spec=pl.BlockSpec((1, tn), lambda i, j: (0, j)), scale=scale, norm_gain=norm_gain)


def _mm_nt(name, a, b, *, b_lead=(), out_dtype, tm=512, tn=512, tk=2048, res=None, norm_bwd=None):
    M, K = a.shape
    N = b.shape[-2]
    tm, tn, tk = _tile(M, tm), _tile(N, tn), _tile(K, tk)
    nk = K // tk
    nl = len(b_lead)
    return _mm(
        name, "nt", a, b, grid=(M // tm, N // tn, nk), nk=nk,
        a_spec=pl.BlockSpec((tm, tk), lambda i, j, k: (i, k)),
        b_spec=pl.BlockSpec((None,) * nl + (tn, tk), lambda i, j, k: tuple(b_lead) + (j, k)),
        out_shape=(M, N), out_dtype=out_dtype, out_spec=pl.BlockSpec((tm, tn), lambda i, j, k: (i, j)),
        res=res, res_spec=pl.BlockSpec((tm, tn), lambda i, j, k: (i, j)), norm_bwd=norm_bwd)


def _mm_tn(name, a, b, *, out_dtype, tm=512, tn=512):
    S, M = a.shape
    N = b.shape[1]
    tm, tn = _tile(M, tm), _tile(N, tn)
    return _mm(
        name, "tn", a, b, grid=(M // tm, N // tn),
        a_spec=pl.BlockSpec((S, tm), lambda i, j: (0, i)),
        b_spec=pl.BlockSpec((S, tn), lambda i, j: (0, j)),
        out_shape=(M, N), out_dtype=out_dtype, out_spec=pl.BlockSpec((tm, tn), lambda i, j: (i, j)))


def _rmsnorm_fwd(name, h, g, tr=256):
    S, D = h.shape
    tr = _tile(S, tr)

    def body(h_ref, g_ref, o_ref):
        x = h_ref[...]
        r = lax.rsqrt(jnp.mean(x * x, axis=-1, keepdims=True) + EPS)
        o_ref[...] = (x * r * g_ref[...]).astype(o_ref.dtype)

    return pl.pallas_call(
        body, name=name, grid=(S // tr,),
        in_specs=[pl.BlockSpec((tr, D), lambda i: (i, 0)), pl.BlockSpec((1, D), lambda i: (0, 0))],
        out_specs=pl.BlockSpec((tr, D), lambda i: (i, 0)),
        out_shape=_hbm_out((S, D), BF16),
        compiler_params=_params(("parallel",)),
    )(h, g)


def _loss_head(name, h, target, g, tr=256):
    S, D = h.shape
    tr = _tile(S, tr)

    def body(h_ref, t_ref, g_ref, o_ref, ob_ref, dg_ref, loss_ref):
        i = pl.program_id(0)
        x = h_ref[...]
        gg = g_ref[...]
        r = lax.rsqrt(jnp.mean(x * x, axis=-1, keepdims=True) + EPS)
        xr = x * r
        err = xr * gg - t_ref[...]
        lpart = 0.5 * jnp.sum(jnp.mean(err * err, axis=-1, keepdims=True), axis=0, keepdims=True)
        dy = err * (1.0 / D)
        dyg = dy * gg
        dx = r * dyg - xr * (r * jnp.mean(dyg * xr, axis=-1, keepdims=True))
        o_ref[...] = dx
        ob_ref[...] = dx.astype(BF16)
        part = jnp.sum(dy * xr, axis=0, keepdims=True)
        lrow = jnp.broadcast_to(lpart, (1, LANES))

        @pl.when(i == 0)
        def _():
            dg_ref[...] = part
            loss_ref[...] = lrow

        @pl.when(i > 0)
        def _():
            dg_ref[...] += part
            loss_ref[...] += lrow

    row = pl.BlockSpec((tr, D), lambda i: (i, 0))
    vec = pl.BlockSpec((1, D), lambda i: (0, 0))
    return pl.pallas_call(
        body, name=name, grid=(S // tr,),
        in_specs=[row, row, vec], out_specs=[row, row, vec, pl.BlockSpec((1, LANES), lambda i: (0, 0))],
        out_shape=[_hbm_out((S, D), F32), _hbm_out((S, D), BF16),
                   _hbm_out((1, D), F32), _hbm_out((1, LANES), F32)],
        compiler_params=_params(("arbitrary",)),
    )(h, target, g)


def _swiglu_fwd(name, hn, w_in, tm=512):
    S, D = hn.shape
    FH = w_in.shape[-1]
    tm = _tile(S, tm)

    def body(x_ref, wg_ref, wu_ref, z_ref, a_ref):
        x = x_ref[...]
        zg = jnp.dot(x, wg_ref[...], preferred_element_type=F32)
        zu = jnp.dot(x, wu_ref[...], preferred_element_type=F32)
        z_ref[0] = zg.astype(z_ref.dtype)
        z_ref[1] = zu.astype(z_ref.dtype)
        a_ref[...] = (zg * _sigmoid_t(zg) * zu).astype(a_ref.dtype)

    return pl.pallas_call(
        body, name=name, grid=(S // tm, 2),
        in_specs=[pl.BlockSpec((tm, D), lambda i, j: (i, 0)),
                  pl.BlockSpec((None, D, FH), lambda i, j: (j, 0, 0)),
                  pl.BlockSpec((None, D, FH), lambda i, j: (j + 2, 0, 0))],
        out_specs=[pl.BlockSpec((2, tm, FH), lambda i, j: (0, i, j)), pl.BlockSpec((tm, FH), lambda i, j: (i, j))],
        out_shape=[_hbm_out((2, S, 2 * FH), BF16), _hbm_out((S, 2 * FH), BF16)],
        compiler_params=_params(("parallel", "parallel")),
    )(hn, w_in, w_in)


def _swiglu_bwd(name, dhb, w_out, z3, tm=512):
    S, D = dhb.shape
    F = w_out.shape[0]
    FH = F // 2
    tm = _tile(S, tm)

    def body(d_ref, w_ref, z_ref, dz_ref):
        d = lax.dot_general(d_ref[...], w_ref[...], _NT, preferred_element_type=F32)
        zg = z_ref[0].astype(F32)
        zu = z_ref[1].astype(F32)
        sg = _sigmoid_t(zg)
        dz_ref[0] = (d * zu * (sg * (1.0 + zg * (1.0 - sg)))).astype(dz_ref.dtype)
        dz_ref[1] = (d * (zg * sg)).astype(dz_ref.dtype)

    zspec = pl.BlockSpec((2, tm, FH), lambda i, j: (0, i, j))
    return pl.pallas_call(
        body, name=name, grid=(S // tm, 2),
        in_specs=[pl.BlockSpec((tm, D), lambda i, j: (i, 0)), pl.BlockSpec((FH, D), lambda i, j: (j, 0)), zspec],
        out_specs=zspec, out_shape=_hbm_out((2, S, F), BF16),
        compiler_params=_params(("parallel", "parallel")),
    )(dhb, w_out, z3)


SCAN_ROWS = 64


def _group_scan(A, B, reverse):
    n = A.shape[0]
    sub = lax.broadcasted_iota(jnp.int32, A.shape, 0) % SUBLANES
    for d in (1, 2, 4):
        if reverse:
            A_sh, B_sh = pltpu.roll(A, n - d, 0), pltpu.roll(B, n - d, 0)
            keep = sub < SUBLANES - d
        else:
            A_sh, B_sh = pltpu.roll(A, d, 0), pltpu.roll(B, d, 0)
            keep = sub >= d
        B = jnp.where(keep, A * B_sh + B, B)
        A = jnp.where(keep, A * A_sh, A)
    return A, B


def _block_scan(a, u, carry, reverse):
    A, B = _group_scan(a, u, reverse)
    ng = a.shape[0] // SUBLANES
    out = [None] * ng
    order = range(ng - 1, -1, -1) if reverse else range(ng)
    for gi in order:
        sl = slice(gi * SUBLANES, (gi + 1) * SUBLANES)
        hg = A[sl] * carry + B[sl]
        out[gi] = hg
        carry = hg[0:1] if reverse else hg[SUBLANES - 1:SUBLANES]
    return jnp.concatenate(out, axis=0), carry


def _lru_gates(rc, gip, grp, sp):
    gi = _sigmoid_t(gip)
    gr = _sigmoid_t(grp)
    la = -LRU_C * gr * sp
    a = jnp.exp(la)
    om = -jnp.tanh(la) * (a * a + 1.0)
    mult = jnp.sqrt(om)
    return gi, gr, a, mult


def _lru_fwd(name, proj, rc, gip, grp, lru_p, tc=256):
    S, C = rc.shape
    tc = _tile(C, tc)
    nb = S // SCAN_ROWS

    def body(gb_ref, rc_ref, gi_ref, gr_ref, l_ref, h_ref, m_ref):
        sp = _softplus(-l_ref[...])

        def step(b, carry):
            rows = pl.ds(pl.multiple_of(b * SCAN_ROWS, SCAN_ROWS), SCAN_ROWS)
            rcb = rc_ref[rows, :]
            gi, _, a, mult = _lru_gates(rcb, gi_ref[rows, :], gr_ref[rows, :], sp)
            h, carry = _block_scan(a, rcb * gi * mult, carry, False)
            h_ref[rows, :] = h
            gel, _ = _gelu_and_grad(gb_ref[rows, :])
            m_ref[rows, :] = (gel * h).astype(m_ref.dtype)
            return carry

        lax.fori_loop(0, nb, step, jnp.zeros((1, tc), F32))

    col = pl.BlockSpec((S, tc), lambda j: (0, j))
    return pl.pallas_call(
        body, name=name, grid=(C // tc,),
        in_specs=[col, col, col, col, pl.BlockSpec((1, tc), lambda j: (0, j))],
        out_specs=[col, col],
        out_shape=[_hbm_out((S, C), F32), _hbm_out((S, C), BF16)],
        compiler_params=_params(("parallel",)),
    )(proj, rc, gip, grp, lru_p)


def _lru_bwd(name, dm, proj, hrec, rc, gip, grp, lru_p, tc=256):
    S, C = rc.shape
    tc = _tile(C, tc)
    nb = S // SCAN_ROWS
    R = SCAN_ROWS

    def body(dm_ref, gb_ref, h_ref, rc_ref, gi_ref, gr_ref, l_ref,
             dgb_ref, dgi_ref, dgr_ref, drc_ref, dbi_ref, dbr_ref, dl_ref):
        lp = l_ref[...]
        sp = _softplus(-lp)
        row = lax.broadcasted_iota(jnp.int32, (R, tc), 0)
        zero = jnp.zeros((1, tc), F32)

        def step(t, carry):
            mu_in, s_i, s_r, s_sp = carry
            b = nb - 1 - t
            r0 = pl.multiple_of(b * R, R)
            rows = pl.ds(r0, R)
            rcb = rc_ref[rows, :]
            gi, gr, a, mult = _lru_gates(rcb, gi_ref[rows, :], gr_ref[rows, :], sp)
            gel, dgel = _gelu_and_grad(gb_ref[rows, :])
            dmb = dm_ref[rows, :]
            h = h_ref[rows, :]
            dgb_ref[rows, :] = (dmb * h * dgel).astype(dgb_ref.dtype)
            dh = dmb * gel
            mu, mu_out = _block_scan(a, a * dh, mu_in, True)
            mu_next = jnp.where(row == R - 1, mu_in, pltpu.roll(mu, R - 1, 0))
            lam = dh + mu_next
            p0 = pl.multiple_of(jnp.maximum(r0 - SUBLANES, 0), SUBLANES)
            prev = h_ref[pl.ds(p0, SUBLANES), :][SUBLANES - 1:SUBLANES]
            prev = jnp.where(b > 0, prev, 0.0)
            h_prev = jnp.where(row == 0, prev, pltpu.roll(h, 1, 0))
            da = lam * h_prev
            d_mult = lam * rcb * gi
            d_la = da * a - d_mult * (a * a) / mult
            d_grp = d_la * (-LRU_C * sp) * gr * (1.0 - gr)
            d_gip = lam * rcb * mult * gi * (1.0 - gi)
            dgr_ref[rows, :] = d_grp.astype(dgr_ref.dtype)
            dgi_ref[rows, :] = d_gip.astype(dgi_ref.dtype)
            drc_ref[rows, :] = lam * gi * mult
            s_i = s_i + jnp.sum(d_gip, axis=0, keepdims=True)
            s_r = s_r + jnp.sum(d_grp, axis=0, keepdims=True)
            s_sp = s_sp + jnp.sum(d_la * gr, axis=0, keepdims=True)
            return mu_out, s_i, s_r, s_sp

        _, s_i, s_r, s_sp = lax.fori_loop(0, nb, step, (zero, zero, zero, zero))
        dbi_ref[...] = s_i
        dbr_ref[...] = s_r
        dl_ref[...] = (-LRU_C * s_sp) * (-_sigmoid(-lp))

    col = pl.BlockSpec((S, tc), lambda j: (0, j))
    vec = pl.BlockSpec((1, tc), lambda j: (0, j))
    return pl.pallas_call(
        body, name=name, grid=(C // tc,),
        in_specs=[col, col, col, col, col, col, vec],
        out_specs=[col, col, col, col, vec, vec, vec],
        out_shape=[_hbm_out((S, C), BF16), _hbm_out((S, C), BF16),
                   _hbm_out((S, C), BF16), _hbm_out((S, C), F32),
                   _hbm_out((1, C), F32), _hbm_out((1, C), F32),
                   _hbm_out((1, C), F32)],
        compiler_params=_params(("parallel",)),
    )(dm, proj, hrec, rc, gip, grp, lru_p)


def _cumsum_rows(name, u, reverse):
    S, C = u.shape
    nb = S // SCAN_ROWS

    def body(u_ref, o_ref):
        def step(t, carry):
            b = nb - 1 - t if reverse else t
            rows = pl.ds(pl.multiple_of(b * SCAN_ROWS, SCAN_ROWS), SCAN_ROWS)
            ub = u_ref[rows, :]
            h, carry = _block_scan(jnp.ones_like(ub), ub, carry, reverse)
            o_ref[rows, :] = h
            return carry

        lax.fori_loop(0, nb, step, jnp.zeros((1, C), F32))

    spec = pl.BlockSpec((S, C), lambda i: (0, 0))
    return pl.pallas_call(
        body, name=name, grid=(1,), in_specs=[spec], out_specs=spec,
        out_shape=_hbm_out((S, C), F32),
        compiler_params=_params(("arbitrary",)),
    )(u)


def _shift_down(x, k):
    row = lax.broadcasted_iota(jnp.int32, x.shape, 0)
    return jnp.where(row >= k, pltpu.roll(x, k, 0), 0.0)


def _shift_up(x, k):
    n = x.shape[0]
    row = lax.broadcasted_iota(jnp.int32, x.shape, 0)
    return jnp.where(row < n - k, pltpu.roll(x, n - k, 0), 0.0)


def _conv_fwd(name, proj, w, b, tc=256):
    S, C2 = proj.shape
    C = C2 // 2
    tc = _tile(C, tc)
    off = C // tc

    def body(x_ref, w_ref, b_ref, o_ref, ob_ref):
        x = x_ref[...]
        out = b_ref[...] + w_ref[3:4, :] * x
        for k in (1, 2, 3):
            out = out + w_ref[3 - k:4 - k, :] * _shift_down(x, k)
        o_ref[...] = out
        ob_ref[...] = out.astype(BF16)

    col = pl.BlockSpec((S, tc), lambda j: (0, j))
    return pl.pallas_call(
        body, name=name, grid=(C // tc,),
        in_specs=[pl.BlockSpec((S, tc), lambda j: (0, off + j)),
                  pl.BlockSpec((4, tc), lambda j: (0, j)), pl.BlockSpec((1, tc), lambda j: (0, j))],
        out_specs=[col, col],
        out_shape=[_hbm_out((S, C), F32), _hbm_out((S, C), BF16)],
        compiler_params=_params(("parallel",)),
    )(proj, w, b)


def _conv_bwd(name, drc, proj, w, tc=256):
    S, C = drc.shape
    tc = _tile(C, tc)
    off = C // tc

    def body(y_ref, x_ref, w_ref, dx_ref, dw_ref, db_ref):
        y = y_ref[...]
        x = x_ref[...]
        dx = w_ref[3:4, :] * y
        dw_ref[3:4, :] = jnp.sum(y * x, axis=0, keepdims=True)
        for k in (1, 2, 3):
            dx = dx + w_ref[3 - k:4 - k, :] * _shift_up(y, k)
            dw_ref[3 - k:4 - k, :] = jnp.sum(y * _shift_down(x, k), axis=0, keepdims=True)
        dx_ref[...] = dx.astype(dx_ref.dtype)
        db_ref[...] = jnp.sum(y, axis=0, keepdims=True)

    col = pl.BlockSpec((S, tc), lambda j: (0, j))
    return pl.pallas_call(
        body, name=name, grid=(C // tc,),
        in_specs=[col, pl.BlockSpec((S, tc), lambda j: (0, off + j)), pl.BlockSpec((4, tc), lambda j: (0, j))],
        out_specs=[col, pl.BlockSpec((4, tc), lambda j: (0, j)), pl.BlockSpec((1, tc), lambda j: (0, j))],
        out_shape=[_hbm_out((S, C), BF16), _hbm_out((4, C), F32),
                   _hbm_out((1, C), F32)],
        compiler_params=_params(("parallel",)),
    )(drc, proj, w)


def _gates_fwd(name, rcb, wg, bg):
    S, C = rcb.shape
    nblk, bw, _ = wg.shape

    def body(x_ref, w_ref, b_ref, gi_ref, gr_ref):
        g = jnp.dot(x_ref[...], w_ref[...], preferred_element_type=F32) + b_ref[...]
        gi_ref[...] = g[:, :bw]
        gr_ref[...] = g[:, bw:]

    col = pl.BlockSpec((S, bw), lambda n: (0, n))
    return pl.pallas_call(
        body, name=name, grid=(nblk,),
        in_specs=[col, pl.BlockSpec((None, bw, 2 * bw), lambda n: (n, 0, 0)),
                  pl.BlockSpec((None, 1, 2 * bw), lambda n: (n, 0, 0))],
        out_specs=[col, col],
        out_shape=[_hbm_out((S, C), F32), _hbm_out((S, C), F32)],
        compiler_params=_params(("parallel",)),
    )(rcb, wg, bg)


def _gates_bwd(name, dgi, dgr, rcb, wg, drc1):
    S, C = rcb.shape
    nblk, bw, _ = wg.shape

    def body(dgi_ref, dgr_ref, x_ref, w_ref, d1_ref, drc_ref, dw_ref):
        w = w_ref[...]
        x = x_ref[...]
        di, dr = dgi_ref[...], dgr_ref[...]
        drc_ref[...] = (d1_ref[...]
                        + lax.dot_general(di, w[:, :bw], _NT, preferred_element_type=F32)
                        + lax.dot_general(dr, w[:, bw:], _NT, preferred_element_type=F32))
        dw_ref[:, :bw] = lax.dot_general(x, di, _TN, preferred_element_type=F32).astype(dw_ref.dtype)
        dw_ref[:, bw:] = lax.dot_general(x, dr, _TN, preferred_element_type=F32).astype(dw_ref.dtype)

    col = pl.BlockSpec((S, bw), lambda n: (0, n))
    wspec = pl.BlockSpec((None, bw, 2 * bw), lambda n: (n, 0, 0))
    return pl.pallas_call(
        body, name=name, grid=(nblk,),
        in_specs=[col, col, col, wspec, col], out_specs=[col, wspec],
        out_shape=[_hbm_out((S, C), F32), _hbm_out((nblk, bw, 2 * bw), BF16)],
        compiler_params=_params(("parallel",)),
    )(dgi, dgr, rcb, wg, drc1)


def _att_tile(S):
    return next(t for t in (512, 256, 128) if S % t == 0)


def _head_lanes(shape):
    return lax.broadcasted_iota(jnp.int32, shape, len(shape) - 1) < HEAD_DIM


def _key_bias(c_blk):
    first = _head_lanes(c_blk.shape)
    rolled = pltpu.roll(c_blk, HEAD_DIM, 1)
    return jnp.where(first, c_blk, rolled), jnp.where(first, rolled, c_blk)


def _over_keys(x, op):
    n = x.shape[0]
    while n > SUBLANES:
        n //= 2
        x = op(x[:n], x[n:2 * n])
    return (jnp.max if op is jnp.maximum else jnp.sum)(x, axis=0, keepdims=True)


def _causal_t(T, cc):
    r = lax.broadcasted_iota(jnp.int32, (T, LANES), 0)
    c = lax.broadcasted_iota(jnp.int32, (T, LANES), 1) + cc * LANES
    return r <= c


def _attn_fwd(name, q, kv, cfull):
    S, D = q.shape
    HP = D // LANES
    T = _att_tile(S)
    nq = S // T
    NC = T // LANES

    def body(q_ref, k_ref, v_ref, c_ref, o_ref, of_ref, lse_ref, bias, vT, acc, m_scr, l_scr):
        def prologue(i, _):
            rows = pl.ds(pl.multiple_of(i * T, T), T)
            bias[0, rows, :], bias[1, rows, :] = _key_bias(c_ref[rows, :])
            vT[i] = v_ref[rows, :].astype(F32).T.astype(BF16)
            return 0

        lax.fori_loop(0, nq, prologue, 0)

        def q_step(qi, _):
            q0 = pl.multiple_of(qi * T, T)
            qb = q_ref[pl.ds(q0, T), :]
            m_scr[...] = jnp.full(m_scr.shape, -jnp.inf, F32)
            l_scr[...] = jnp.zeros(l_scr.shape, F32)
            acc[...] = jnp.zeros(acc.shape, F32)

            def tile(kj, masked):
                ks = pl.ds(pl.multiple_of(kj * T, T), T)
                kf = k_ref[ks, :].astype(F32)
                first = _head_lanes(kf.shape)
                kms = [jnp.where(first if hh == 0 else jnp.logical_not(first), kf, 0.0).astype(BF16) for hh in range(2)]
                sTs = [lax.dot_general(km, qb, _NT, preferred_element_type=F32) for km in kms]
                for hh in range(2):
                    b = bias[hh, ks, :]
                    ps = []
                    for cc in range(NC):
                        cols = slice(cc * LANES, (cc + 1) * LANES)
                        s = sTs[hh][:, cols] + b
                        if masked:
                            s = jnp.where(_causal_t(T, cc), s, -jnp.inf)
                        m_old = m_scr[hh, cc]
                        m_new = jnp.maximum(m_old, _over_keys(s, jnp.maximum))
                        alpha = jnp.exp(m_old - m_new)
                        p = jnp.exp(s - m_new)
                        l_scr[hh, cc] = alpha * l_scr[hh, cc] + _over_keys(p, jnp.add)
                        m_scr[hh, cc] = m_new
                        ps.append(p.astype(BF16))
                        acc[hh, :, cols] = acc[hh, :, cols] * alpha
                    acc[hh] += jnp.dot(vT[kj, hh * HEAD_DIM:(hh + 1) * HEAD_DIM, :], jnp.concatenate(ps, axis=1),
                                       preferred_element_type=F32)

            def inner(kj, _):
                tile(kj, False)
                return 0

            lax.fori_loop(0, qi, inner, 0)
            tile(qi, True)
            outs = []
            for hh in range(2):
                inv = jnp.concatenate([1.0 / l_scr[hh, cc] for cc in range(NC)], axis=1)
                outs.append(acc[hh] * inv)
                for cc in range(NC):
                    lse_ref[hh:hh + 1, pl.ds(q0 + cc * LANES, LANES)] = m_scr[hh, cc] + jnp.log(l_scr[hh, cc])
            out = jnp.concatenate(outs, axis=0).T
            o_ref[pl.ds(q0, T), :] = out.astype(o_ref.dtype)
            of_ref[pl.ds(q0, T), :] = out
            return 0

        lax.fori_loop(0, nq, q_step, 0)

    blk = lambda off: pl.BlockSpec((S, LANES), lambda p: (0, off + p))
    return pl.pallas_call(
        body, name=name, grid=(HP,),
        in_specs=[blk(0), blk(0), blk(HP), blk(0)],
        out_specs=[blk(0), blk(0), pl.BlockSpec((None, 2, S), lambda p: (p, 0, 0))],
        out_shape=[_hbm_out((S, D), BF16), _hbm_out((S, D), F32),
                   _hbm_out((HP, 2, S), F32)],
        scratch_shapes=[pltpu.VMEM((2, S, LANES), F32), pltpu.VMEM((nq, LANES, T), BF16),
                        pltpu.VMEM((2, HEAD_DIM, T), F32), pltpu.VMEM((2, NC, 1, LANES), F32),
                        pltpu.VMEM((2, NC, 1, LANES), F32)],
        compiler_params=_params(("parallel",)),
    )(q, kv, kv, cfull)


def _attn_bwd(name, q, kv, cfull, of, do, lse3):
    S, D = q.shape
    HP = D // LANES
    T = _att_tile(S)
    nq = S // T
    NC = T // LANES
    scale = HEAD_DIM ** -0.5

    def body(q_ref, k_ref, v_ref, c_ref, of_ref, do_ref, lse_ref,
             dq_ref, dk_ref, dv_ref, dck_ref, drq_ref, bias, kT, dqT, delta, dr_scr, dk_acc, dv_acc, dc_acc):
        def prologue(i, _):
            rows = pl.ds(pl.multiple_of(i * T, T), T)
            bias[0, rows, :], bias[1, rows, :] = _key_bias(c_ref[rows, :])
            kT[i] = k_ref[rows, :].astype(F32).T.astype(BF16)
            prodT = (do_ref[rows, :].astype(F32) * of_ref[rows, :]).T
            for hh in range(2):
                delta[hh:hh + 1, rows] = jnp.sum(prodT[hh * HEAD_DIM:(hh + 1) * HEAD_DIM], axis=0, keepdims=True)
            dqT[i] = jnp.zeros((LANES, T), F32)
            return 0

        lax.fori_loop(0, nq, prologue, 0)
        dr_scr[...] = jnp.zeros(dr_scr.shape, F32)

        def kv_step(kj, _):
            ks = pl.ds(pl.multiple_of(kj * T, T), T)
            kf = k_ref[ks, :].astype(F32)
            vf = v_ref[ks, :].astype(F32)
            first = _head_lanes(kf.shape)
            masks = [first, jnp.logical_not(first)]
            kms = [jnp.where(m, kf, 0.0).astype(BF16) for m in masks]
            vms = [jnp.where(m, vf, 0.0).astype(BF16) for m in masks]

            for acc in (dk_acc, dv_acc, dc_acc):
                acc[...] = jnp.zeros(acc.shape, F32)

            def tile(qi, masked):
                q0 = pl.multiple_of(qi * T, T)
                qb = q_ref[pl.ds(q0, T), :]
                dob = do_ref[pl.ds(q0, T), :]
                sTs = [lax.dot_general(km, qb, _NT, preferred_element_type=F32) for km in kms]
                dpTs = [lax.dot_general(vm, dob, _NT, preferred_element_type=F32) for vm in vms]
                for hh in range(2):
                    b = bias[hh, ks, :]
                    head = slice(hh * HEAD_DIM, (hh + 1) * HEAD_DIM)
                    ps, dss = [], []
                    for cc in range(NC):
                        cols = slice(cc * LANES, (cc + 1) * LANES)
                        at = pl.ds(q0 + cc * LANES, LANES)
                        p = jnp.exp(sTs[hh][:, cols] + b - lse_ref[hh:hh + 1, at])
                        if masked:
                            p = jnp.where(_causal_t(T, cc), p, 0.0)
                        ds = p * (dpTs[hh][:, cols] - delta[hh:hh + 1, at])
                        ps.append(p.astype(BF16))
                        dss.append(ds.astype(BF16))
                        dc_acc[hh] += ds
                        dr_scr[hh:hh + 1, at] += _over_keys(ds, jnp.add)
                    pT = jnp.concatenate(ps, axis=1)
                    dsT = jnp.concatenate(dss, axis=1)
                    dv_acc[hh] += jnp.dot(pT, dob, preferred_element_type=F32)
                    dk_acc[hh] += jnp.dot(dsT, qb, preferred_element_type=F32)
                    dqT[qi, head, :] += jnp.dot(kT[kj, head, :], dsT, preferred_element_type=F32)

            def inner(qi, _):
                tile(qi, False)
                return 0

            tile(kj, True)
            lax.fori_loop(kj + 1, nq, inner, 0)
            dk_ref[ks, :] = jnp.where(first, dk_acc[0], dk_acc[1])
            dv_ref[ks, :] = jnp.where(first, dv_acc[0], dv_acc[1])
            dck_ref[ks, :] = jnp.where(first, jnp.broadcast_to(-jnp.sum(dc_acc[0], axis=1, keepdims=True), (T, LANES)),
                                       jnp.broadcast_to(-jnp.sum(dc_acc[1], axis=1, keepdims=True), (T, LANES)))
            return 0

        lax.fori_loop(0, nq, kv_step, 0)

        def epilogue(i, _):
            rows = pl.ds(pl.multiple_of(i * T, T), T)
            dq_ref[rows, :] = (dqT[i].T * scale).astype(dq_ref.dtype)
            return 0

        lax.fori_loop(0, nq, epilogue, 0)
        drq_ref[...] = dr_scr[...]

    blk = lambda off: pl.BlockSpec((S, LANES), lambda p: (0, off + p))
    row_spec = pl.BlockSpec((None, 2, S), lambda p: (p, 0, 0))
    return pl.pallas_call(
        body, name=name, grid=(HP,),
        in_specs=[blk(0), blk(0), blk(HP), blk(0), blk(0), blk(0), row_spec],
        out_specs=[blk(0), blk(0), blk(0), blk(0), row_spec],
        out_shape=[_hbm_out((S, D), BF16), _hbm_out((S, D), F32),
                   _hbm_out((S, D), F32), _hbm_out((S, D), F32),
                   _hbm_out((HP, 2, S), F32)],
        scratch_shapes=[pltpu.VMEM((2, S, LANES), F32), pltpu.VMEM((nq, LANES, T), BF16),
                        pltpu.VMEM((nq, LANES, T), F32), pltpu.VMEM((2, S), F32), pltpu.VMEM((2, S), F32)]
        + [pltpu.VMEM((2, T, LANES), F32)] * 3,
        compiler_params=_params(("parallel",)),
    )(q, kv, kv, cfull, of, do, lse3)


def _logsig_fwd(name, f):
    S, C = f.shape

    def body(f_ref, o_ref):
        o_ref[...] = -_softplus(-f_ref[...])

    spec = pl.BlockSpec((S, C), lambda i: (0, 0))
    return pl.pallas_call(body, name=name, grid=(1,), in_specs=[spec], out_specs=spec,
                          out_shape=_hbm_out((S, C), F32),
                          compiler_params=_params(("arbitrary",)))(f)


def _logsig_bwd(name, dls, f):
    S, C = f.shape

    def body(d_ref, f_ref, o_ref, s_ref):
        df = d_ref[...] * _sigmoid(-f_ref[...])
        o_ref[...] = df.astype(o_ref.dtype)
        s_ref[...] = jnp.sum(df, axis=0, keepdims=True)

    spec = pl.BlockSpec((S, C), lambda i: (0, 0))
    return pl.pallas_call(body, name=name, grid=(1,), in_specs=[spec, spec],
                          out_specs=[spec, pl.BlockSpec((1, C), lambda i: (0, 0))],
                          out_shape=[_hbm_out((S, C), BF16), _hbm_out((1, C), F32)],
                          compiler_params=_params(("arbitrary",)))(dls, f)


def _add_cast(name, parts, out_dtype, tr=256):
    S, C = parts[0].shape
    tr = _tile(S, tr)
    n = len(parts)

    def body(*refs):
        acc = refs[0][...].astype(F32)
        for r in refs[1:n]:
            acc = acc + r[...].astype(F32)
        refs[n][...] = acc.astype(out_dtype)

    spec = pl.BlockSpec((tr, C), lambda i: (i, 0))
    return pl.pallas_call(body, name=name, grid=(S // tr,), in_specs=[spec] * n, out_specs=spec,
                          out_shape=_hbm_out((S, C), out_dtype),
                          compiler_params=_params(("parallel",)))(*parts)


def _local_step(x, target, gains, layer_weights, layer_prefetch, layer_grads):
    S, D = x.shape
    HP = D // LANES
    scale = HEAD_DIM ** -0.5
    tm = _tile(S, 512)
    tx = _tile(S, 256)
    td = _tile(D, 512)
    saved = []
    h = x
    l = 0
    kv = cfull = f_pre = hn_kv = h_kv = None
    while True:
        W = layer_weights(l, "mix", h)
        if W is None:
            break
        recurrent = "w_rec_in" in W
        if l == 0:
            xn = _rmsnorm_fwd("mix_norm_0", h, gains["mix"][0])
        if recurrent:
            CH = W["w_rec_in"].shape[-1]
            C = 2 * CH
            proj = _mm(f"rec_in_{l}", "nn", xn, W["w_rec_in"], grid=(S // tm, N_CHIPS),
                       a_spec=pl.BlockSpec((tm, D), lambda i, j: (i, 0)),
                       b_spec=pl.BlockSpec((None, D, CH), lambda i, j: (j, 0, 0)),
                       out_shape=(S, 2 * C), out_dtype=F32,
                       out_spec=pl.BlockSpec((tm, CH), lambda i, j: (i, j)))
            layer_prefetch(l, "mix2", proj)
            rc, rcb = _conv_fwd(f"conv_{l}", proj, W["conv_w"], W["conv_b"])
            W = {**W, **layer_weights(l, "mix2", rcb)}
            gip, grp = _gates_fwd(f"gates_{l}", rcb, W["w_gates"], W["b_gates"])
            hrec, m = _lru_fwd(f"lru_{l}", proj, rc, gip, grp, W["lru_param"])
            layer_prefetch(l, "ffn", m)
            h_mid, hn = _mm_nn(f"rec_out_{l}", m, W["w_rec_out"], out_dtype=F32, res=h, tn=D, norm_gain=gains["ffn"][l])
            mix_saved = (xn, proj, rc, rcb, gip, grp, hrec, m)
        else:
            if "w_kv" in W:
                h_kv = h
                hn_kv = _rmsnorm_fwd("kv_norm", h, W["norm_kv"])
                kv = _mm_nn("kv_proj", hn_kv, W["w_kv"], out_dtype=BF16)
                f_pre = _mm_nn("f_proj", hn_kv, W["w_f"], out_dtype=F32, bias=W["b_f"])
                c = _cumsum_rows("c_cumsum", _logsig_fwd("logsig", f_pre), False)
                cfull = jnp.repeat(-c[:, :2 * HP], HEAD_DIM, axis=1)
            q = _mm_nn(f"q_proj_{l}", xn, W["w_q"], out_dtype=BF16, scale=scale)
            layer_prefetch(l, "mix2", q)
            o, of, lse = _attn_fwd(f"attn_fwd_{l}", q, kv, cfull)
            W = {**W, **layer_weights(l, "mix2", o)}
            layer_prefetch(l, "ffn", o)
            h_mid, hn = _mm_nn(f"o_proj_{l}", o, W["w_o"], out_dtype=F32, res=h, tn=D, norm_gain=gains["ffn"][l])
            mix_saved = (xn, q, o, of, lse)
        W = {**W, **layer_weights(l, "ffn", h_mid)}
        z3, act = _swiglu_fwd(f"ffn_in_{l}", hn, W["w_ffn_in"])
        layer_prefetch(l + 1, "mix", act)
        saved.append((W, h, h_mid, mix_saved, (hn, z3, act)))
        l += 1
        if l < len(gains["mix"]):
            h, xn = _mm_nn(f"ffn_out_{l - 1}", act, W["w_ffn_out"], out_dtype=F32, res=h_mid, tn=D,
                           norm_gain=gains["mix"][l])
        else:
            h = _mm_nn(f"ffn_out_{l - 1}", act, W["w_ffn_out"], out_dtype=F32, res=h_mid, tn=D)

    dh, dhb, dg_final, loss_row = _loss_head("loss_head", h, target, gains["final"])

    dk_parts, dv_parts, dc_parts = [], [], []
    token = None
    for l in reversed(range(len(saved))):
        W, h_in, h_mid, mix_saved, (hn, z3, act) = saved[l]
        recurrent = "w_rec_in" in W
        FH = W["w_ffn_in"].shape[-1]
        G = {}
        norm_ffn = gains["ffn"][l]
        if token is not None:
            norm_ffn = norm_ffn + jnp.minimum(token[:1, :1], 0.0)
        G["w_ffn_out"] = _mm_tn(f"d_ffn_out_{l}", act, dhb, out_dtype=BF16, tn=D)
        dz3 = _swiglu_bwd(f"d_act_{l}", dhb, W["w_ffn_out"], z3)
        G["w_ffn_in"] = _mm(
            f"d_ffn_in_{l}", "tn", hn, dz3, grid=(D // td, N_CHIPS),
            a_spec=pl.BlockSpec((S, td), lambda i, j: (0, i)),
            b_spec=pl.BlockSpec((None, S, FH), lambda i, j: (j // 2, 0, j % 2)),
            out_shape=(N_CHIPS, D, FH), out_dtype=BF16,
            out_spec=pl.BlockSpec((None, td, FH), lambda i, j: (j, i, 0)))
        ffn_token = layer_grads(l, "ffn", G)
        G = {}
        if ffn_token is not None:
            norm_ffn = norm_ffn + jnp.minimum(ffn_token[:1, :1], 0.0)
        dh, dhb, dgp = _mm(f"d_ffn_hn_{l}", "nt", dz3, W["w_ffn_in"], grid=(S // tx, 1),
                           a_spec=[pl.BlockSpec((None, tx, FH), functools.partial(lambda i, j, k: (k // 2, i, k % 2), k=k))
                                   for k in range(N_CHIPS)],
                           b_spec=[pl.BlockSpec((None, D, FH), functools.partial(lambda i, j, k: (k, 0, 0), k=k))
                                   for k in range(N_CHIPS)],
                           out_shape=(S, D), out_dtype=F32, out_spec=pl.BlockSpec((tx, D), lambda i, j: (i, 0)),
                           norm_bwd=(h_mid, norm_ffn, dh))
        G["norm_ffn"] = jnp.sum(dgp, axis=0)
        if recurrent:
            CH = W["w_rec_in"].shape[-1]
            C = 2 * CH
            xn, proj, rc, rcb, gip, grp, hrec, m = mix_saved
            G["w_rec_out"] = _mm_tn(f"d_rec_out_{l}", m, dhb, out_dtype=BF16, tn=D)
            dm = _mm_nt(f"d_m_{l}", dhb, W["w_rec_out"], out_dtype=F32, tn=C)
            dgb, dgi, dgr, drc1, G["b_gi"], G["b_gr"], G["lru_param"] = _lru_bwd(
                f"d_lru_{l}", dm, proj, hrec, rc, gip, grp, W["lru_param"])
            drc, G["w_gates"] = _gates_bwd(f"d_gates_{l}", dgi, dgr, rcb, W["w_gates"], drc1)
            mix_token = layer_grads(l, "mix2", {n: G[n] for n in ("w_rec_out", "w_gates")})
            drec, G["conv_w"], G["conv_b"] = _conv_bwd(f"d_conv_{l}", drc, proj, W["conv_w"])
            dproj = jnp.concatenate([dgb, drec], axis=1)
            norm_mix = gains["mix"][l] if mix_token is None else gains["mix"][l] + jnp.minimum(mix_token[:1, :1], 0.0)
            G["w_rec_in"] = _mm(
                f"d_rec_in_{l}", "tn", xn, dproj, grid=(1, N_CHIPS),
                a_spec=pl.BlockSpec((S, D), lambda i, j: (0, 0)),
                b_spec=pl.BlockSpec((S, CH), lambda i, j: (0, j)),
                out_shape=(N_CHIPS, D, CH), out_dtype=BF16,
                out_spec=pl.BlockSpec((None, D, CH), lambda i, j: (j, 0, 0)))
            dh, dhb, dgp = _mm(f"d_rec_xn_{l}", "nt", dproj, W["w_rec_in"], grid=(S // tx, 1),
                               a_spec=[pl.BlockSpec((tx, CH), functools.partial(lambda i, j, k: (i, k), k=k))
                                       for k in range(N_CHIPS)],
                               b_spec=[pl.BlockSpec((None, D, CH), functools.partial(lambda i, j, k: (k, 0, 0), k=k))
                                       for k in range(N_CHIPS)],
                               out_shape=(S, D), out_dtype=F32, out_spec=pl.BlockSpec((tx, D), lambda i, j: (i, 0)),
                               norm_bwd=(h_in, norm_mix, dh))
        else:
            xn, q, o, of, lse = mix_saved
            G["w_o"] = _mm_tn(f"d_o_proj_{l}", o, dhb, out_dtype=BF16, tn=D)
            do = _mm_nt(f"d_o_{l}", dhb, W["w_o"], out_dtype=BF16, tn=D)
            mix_token = layer_grads(l, "mix2", {"w_o": G["w_o"]})
            dq, dk, dv, dck, drq = _attn_bwd(f"attn_bwd_{l}", q, kv, cfull, of, do, lse)
            dk_parts.append(dk)
            dv_parts.append(dv)
            dc_parts.append(dck[:, ::HEAD_DIM] + drq.reshape(2 * HP, S).T)
            G["w_q"] = _mm_tn(f"d_q_proj_{l}", xn, dq, out_dtype=BF16, tn=D)
            norm_mix = gains["mix"][l] if mix_token is None else gains["mix"][l] + jnp.minimum(mix_token[:1, :1], 0.0)
            dh, dhb, dgp = _mm_nt(f"d_q_xn_{l}", dq, W["w_q"], out_dtype=F32, tn=D, norm_bwd=(h_in, norm_mix, dh))
        G["norm_mix"] = jnp.sum(dgp, axis=0)
        if "w_kv" in W:
            dkb = _add_cast("dk_sum", dk_parts, BF16)
            dvb = _add_cast("dv_sum", dv_parts, BF16)
            dkv = jnp.concatenate([dkb, dvb], axis=1)
            dc = sum(dc_parts[1:], dc_parts[0])
            dc_pad = jnp.pad(dc, ((0, 0), (0, LANES - 2 * HP)))
            dls = _cumsum_rows("dc_cumsum", dc_pad, True)
            dfb, G["b_f"] = _logsig_bwd("d_logsig", dls, f_pre)
            G["w_kv"] = _mm_tn("d_kv_proj", hn_kv, dkv, out_dtype=BF16)
            G["w_f"] = _mm_tn("d_f_proj", hn_kv, dfb, out_dtype=F32)
            dhn_f = _mm_nt("d_f_hn", dfb, W["w_f"], out_dtype=F32, tn=D)
            dh, dhb, dgp = _mm_nt("d_kv_hn", dkv, W["w_kv"], out_dtype=F32, tn=D, res=dhn_f,
                                  norm_bwd=(h_kv, W["norm_kv"], dh))
            G["norm_kv"] = jnp.sum(dgp, axis=0)
        token = layer_grads(l, "mix", G)
    return loss_row, dh, dg_final


_ANY = pl.BlockSpec(memory_space=pl.ANY)


def _position():
    return lax.axis_index("x"), lax.axis_index("y"), lax.axis_index("c")


def _chip_peers(x, y):
    return [(1 - x, y), (x, 1 - y), (1 - x, 1 - y)]


def _half_rows(c, n):
    h = n // 2
    assert h % 16 == 0
    return pl.ds(pl.multiple_of(c * h, 16), h)


def _place_own(name, shard, layer, me):
    _, R, C = shard.shape
    tr = _row_tile(R, C, 2 * shard.dtype.itemsize, target=8 << 20)

    def body(me_ref, x_ref, o_ref):
        o_ref[...] = x_ref[...]

    return pl.pallas_call(
        body, name=name,
        grid_spec=pltpu.PrefetchScalarGridSpec(
            num_scalar_prefetch=1, grid=(R // tr,),
            in_specs=[pl.BlockSpec((None, tr, C), lambda i, me_ref: (layer, i, 0))],
            out_specs=pl.BlockSpec((None, tr, C), lambda i, me_ref: (me_ref[0], i, 0))),
        out_shape=_hbm_out((N_CHIPS, R, C), shard.dtype),
        compiler_params=_params(("parallel",)),
    )(me, shard)


def _gather_smalls(name, smalls):
    ns = len(smalls)

    def body(*refs):
        ins, outs = refs[:ns], refs[ns:2 * ns]
        send_sems, recv_sems, local_sems = refs[2 * ns:]
        x, y, c = _position()
        me = 2 * x + y
        peers = _chip_peers(x, y)

        def remote(t, k, chip):
            px, py = peers[k]
            return pltpu.make_async_remote_copy(
                src_ref=ins[t], dst_ref=outs[t].at[chip], send_sem=send_sems.at[3 * t + k],
                recv_sem=recv_sems.at[3 * t + k], device_id=(px, py, c), device_id_type=MESH)

        local = [pltpu.make_async_copy(ins[t], outs[t].at[me], local_sems.at[t]) for t in range(ns)]
        for t in range(ns):
            local[t].start()
            for k in range(3):
                remote(t, k, me).start()
        for t in range(ns):
            for k in range(3):
                px, py = peers[k]
                remote(t, k, 2 * px + py).wait_recv()
        for t in range(ns):
            for k in range(3):
                remote(t, k, me).wait_send()
            local[t].wait()

    return pl.pallas_call(
        body, name=name, in_specs=[_ANY] * ns, out_specs=[_ANY] * ns,
        out_shape=[_hbm_out((N_CHIPS,) + s.shape, s.dtype) for s in smalls],
        scratch_shapes=[pltpu.SemaphoreType.DMA((3 * ns,)), pltpu.SemaphoreType.DMA((3 * ns,)),
                        pltpu.SemaphoreType.DMA((ns,))],
    )(*smalls)


_SEM = pl.BlockSpec(memory_space=pltpu.SEMAPHORE)
_SPLIT = pltpu.CompilerParams(has_side_effects=pltpu.SideEffectType.DATAFLOW_SIDE_EFFECTING)


def _weight_copy(shards, buf, items, sems, i, k, chip_of_dst, peers, c):
    w, l = items[i]
    px, py = peers[k]
    half = _half_rows(c, shards[w].shape[1])
    return pltpu.make_async_remote_copy(
        src_ref=shards[w].at[l, half], dst_ref=buf.at[chip_of_dst, half],
        send_sem=sems[0].at[3 * i + k], recv_sem=sems[1].at[3 * i + k],
        device_id=(px, py, c), device_id_type=MESH)


def _gather_start(name, shards, bufs, items, after):
    nw, n = len(shards), len(bufs)

    def body(*refs):
        ins, outs, sems = refs[:nw], refs[nw + n + 1:nw + 2 * n + 1], refs[nw + 2 * n + 1:]
        x, y, c = _position()
        peers = _chip_peers(x, y)
        for i in range(n):
            for k in range(3):
                _weight_copy(ins, outs[i], items, sems, i, k, 2 * x + y, peers, c).start()

    res = pl.pallas_call(
        body, name=name, in_specs=[_ANY] * (nw + n + 1), out_specs=[_ANY] * n + [_SEM, _SEM],
        out_shape=[_hbm_out(b.shape, b.dtype) for b in bufs]
        + [pltpu.SemaphoreType.DMA((3 * n,)), pltpu.SemaphoreType.DMA((3 * n,))],
        input_output_aliases={nw + i: i for i in range(n)}, compiler_params=_SPLIT,
    )(*shards, *bufs, after)
    return res[:n], res[n:]


def _gather_wait(name, shards, bufs, items, ids, sems, after):
    nw, m = len(shards), len(ids)

    def body(*refs):
        ins, bs = refs[:nw], refs[nw:nw + m]
        sem_refs = refs[nw + m:nw + m + 2]
        x, y, c = _position()
        peers = _chip_peers(x, y)
        for j, i in enumerate(ids):
            for k in range(3):
                px, py = peers[k]
                _weight_copy(ins, bs[j], items, sem_refs, i, k, 2 * px + py, peers, c).wait_recv()
        for j, i in enumerate(ids):
            for k in range(3):
                _weight_copy(ins, bs[j], items, sem_refs, i, k, 2 * x + y, peers, c).wait_send()

    res = pl.pallas_call(
        body, name=name, in_specs=[_ANY] * (nw + m) + [_SEM, _SEM, _ANY], out_specs=[_ANY] * m,
        out_shape=[_hbm_out(bufs[i].shape, bufs[i].dtype) for i in ids],
        input_output_aliases={nw + j: j for j in range(m)}, compiler_params=_SPLIT,
    )(*shards, *[bufs[i] for i in ids], *sems, after)
    return list(res)


def _forward_copy(src, dst, sems, i, k, core):
    x, y, c = _position()
    px, py = _chip_peers(x, y)[k]
    half = _half_rows(core, src.shape[1])
    return pltpu.make_async_remote_copy(
        src_ref=src.at[2 * px + py, half], dst_ref=dst.at[2 * px + py, half],
        send_sem=sems[0].at[3 * i + k], recv_sem=sems[1].at[3 * i + k],
        device_id=(x, y, 1 - c), device_id_type=MESH)


def _forward_start(name, bufs):
    n = len(bufs)

    def body(*refs):
        ins, outs, sems = refs[:n], refs[n:2 * n], refs[2 * n:]
        c = lax.axis_index("c")
        for i in range(n):
            for k in range(3):
                _forward_copy(ins[i], outs[i], sems, i, k, c).start()

    res = pl.pallas_call(
        body, name=name, in_specs=[_ANY] * n, out_specs=[_ANY] * n + [_SEM, _SEM],
        out_shape=[_hbm_out(g.shape, g.dtype) for g in bufs]
        + [pltpu.SemaphoreType.DMA((3 * n,)), pltpu.SemaphoreType.DMA((3 * n,))],
        input_output_aliases={i: i for i in range(n)}, compiler_params=_SPLIT,
    )(*bufs)
    return list(res[:n]), res[n:]


def _forward_wait(name, bufs, sems, after):
    n = len(bufs)

    def body(*refs):
        bs, sem_refs = refs[:n], refs[n:n + 2]
        c = lax.axis_index("c")
        for i in range(n):
            for k in range(3):
                _forward_copy(bs[i], bs[i], sem_refs, i, k, 1 - c).wait_recv()
        for i in range(n):
            for k in range(3):
                _forward_copy(bs[i], bs[i], sem_refs, i, k, c).wait_send()

    return list(pl.pallas_call(
        body, name=name, in_specs=[_ANY] * n + [_SEM, _SEM, _ANY], out_specs=[_ANY] * n,
        out_shape=[_hbm_out(g.shape, g.dtype) for g in bufs],
        input_output_aliases={i: i for i in range(n)}, compiler_params=_SPLIT,
    )(*bufs, *sems, after))


def _reduce_copy(grads, others, sems, i):
    x, y, c = _position()
    return pltpu.make_async_remote_copy(
        src_ref=grads[i].at[:, _half_rows(1 - c, grads[i].shape[1])], dst_ref=others[i],
        send_sem=sems[0].at[i], recv_sem=sems[1].at[i], device_id=(x, y, 1 - c), device_id_type=MESH)


def _reduce_start(name, grads, after):
    n = len(grads)

    def body(*refs):
        ins, outs, sems, token = refs[:n], refs[n + 1:2 * n + 1], refs[2 * n + 1:2 * n + 3], refs[2 * n + 3]
        for i in range(n):
            _reduce_copy(ins, outs, sems, i).start()
        token[...] = jnp.zeros_like(token)

    res = pl.pallas_call(
        body, name=name, in_specs=[_ANY] * (n + 1),
        out_specs=[_ANY] * n + [_SEM, _SEM, pl.BlockSpec(memory_space=pltpu.VMEM)],
        out_shape=[_hbm_out((N_CHIPS, g.shape[1] // 2, g.shape[2]), g.dtype) for g in grads]
        + [pltpu.SemaphoreType.DMA((n,)), pltpu.SemaphoreType.DMA((n,)), jax.ShapeDtypeStruct((SUBLANES, LANES), F32)],
        compiler_params=_SPLIT,
    )(*grads, after)
    return list(res[:n]), res[n:n + 2], res[n + 2]


def _reduce_wait(name, grads, others, sems, after):
    n = len(grads)

    def body(*refs):
        ins, os_, sem_refs = refs[:n], refs[n:2 * n], refs[2 * n:2 * n + 2]
        for i in range(n):
            _reduce_copy(ins, os_, sem_refs, i).wait_recv()
        for i in range(n):
            _reduce_copy(ins, os_, sem_refs, i).wait_send()

    return list(pl.pallas_call(
        body, name=name, in_specs=[_ANY] * (2 * n) + [_SEM, _SEM, _ANY], out_specs=[_ANY] * n,
        out_shape=[_hbm_out(o.shape, o.dtype) for o in others],
        input_output_aliases={n + i: i for i in range(n)}, compiler_params=_SPLIT,
    )(*grads, *others, *sems, after))


def _sum_cores(name, g, other, core):
    _, R, C = g.shape
    H = R // 2
    tr = _row_tile(H, C, 3 * 2, target=12 << 20)
    nb = H // tr

    def body(c_ref, g_ref, o_ref, out_ref):
        out_ref[...] = (g_ref[...].astype(F32) + o_ref[...].astype(F32)).astype(out_ref.dtype)

    return pl.pallas_call(
        body, name=name,
        grid_spec=pltpu.PrefetchScalarGridSpec(
            num_scalar_prefetch=1, grid=(N_CHIPS, nb),
            in_specs=[pl.BlockSpec((None, tr, C), lambda j, i, c_ref: (j, c_ref[0] * nb + i, 0)),
                      pl.BlockSpec((None, tr, C), lambda j, i, c_ref: (j, i, 0))],
            out_specs=pl.BlockSpec((None, tr, C), lambda j, i, c_ref: (j, i, 0))),
        out_shape=_hbm_out((N_CHIPS, H, C), BF16),
        compiler_params=_params(("parallel", "parallel")),
    )(core, g, other)


def _sum_chips(name, received, own, full, layer, me_core):
    _, H, C = received.shape
    tr = _row_tile(H, C, 3 * 2 + 2 + 4, target=12 << 20)
    nb = H // tr

    def body(s_ref, r_ref, own_ref, full_ref, out_ref):
        acc = r_ref[0].astype(F32)
        for k in (1, 2):
            acc = acc + r_ref[k].astype(F32)
        out_ref[...] = acc + own_ref[...].astype(F32)

    return pl.pallas_call(
        body, name=name,
        grid_spec=pltpu.PrefetchScalarGridSpec(
            num_scalar_prefetch=1, grid=(nb,),
            in_specs=[pl.BlockSpec((3, tr, C), lambda i, s_ref: (0, i, 0)),
                      pl.BlockSpec((None, tr, C), lambda i, s_ref: (s_ref[0], i, 0)),
                      _ANY],
            out_specs=pl.BlockSpec((None, tr, C), lambda i, s_ref: (layer, s_ref[1] * nb + i, 0))),
        out_shape=_hbm_out(full.shape, full.dtype),
        input_output_aliases={3: 0},
        compiler_params=_params(("parallel",)),
    )(me_core, received, own, full)


def _part_copy(parts, recv, sems, i, k, peers, c):
    px, py = peers[k]
    return pltpu.make_async_remote_copy(
        src_ref=parts[i].at[2 * px + py], dst_ref=recv[i].at[k],
        send_sem=sems[0].at[3 * i + k], recv_sem=sems[1].at[3 * i + k],
        device_id=(px, py, c), device_id_type=MESH)


def _scatter_start(name, parts):
    n = len(parts)

    def body(*refs):
        ins, outs, sems, token = refs[:n], refs[n:2 * n], refs[2 * n:2 * n + 2], refs[2 * n + 2]
        x, y, c = _position()
        peers = _chip_peers(x, y)
        for i in range(n):
            for k in range(3):
                _part_copy(ins, outs, sems, i, k, peers, c).start()
        token[...] = jnp.zeros_like(token)

    res = pl.pallas_call(
        body, name=name, in_specs=[_ANY] * n,
        out_specs=[_ANY] * n + [_SEM, _SEM, pl.BlockSpec(memory_space=pltpu.VMEM)],
        out_shape=[_hbm_out((3,) + p.shape[1:], p.dtype) for p in parts]
        + [pltpu.SemaphoreType.DMA((3 * n,)), pltpu.SemaphoreType.DMA((3 * n,)),
           jax.ShapeDtypeStruct((SUBLANES, LANES), F32)],
        compiler_params=_SPLIT,
    )(*parts)
    return list(res[:n]), res[n:n + 2], res[n + 2]


def _scatter_wait(name, parts, recv, sems):
    n = len(parts)

    def body(*refs):
        ins, rs, sem_refs = refs[:n], refs[n:2 * n], refs[2 * n:2 * n + 2]
        x, y, c = _position()
        peers = _chip_peers(x, y)
        for i in range(n):
            for k in range(3):
                _part_copy(ins, rs, sem_refs, i, k, peers, c).wait_recv()
        for i in range(n):
            for k in range(3):
                _part_copy(ins, rs, sem_refs, i, k, peers, c).wait_send()

    return list(pl.pallas_call(
        body, name=name, in_specs=[_ANY] * (2 * n) + [_SEM, _SEM], out_specs=[_ANY] * n,
        out_shape=[_hbm_out(r.shape, r.dtype) for r in recv],
        input_output_aliases={n + i: i for i in range(n)}, compiler_params=_SPLIT,
    )(*parts, *recv, *sems))


def _share_d2d(name, full):
    n = len(full)

    def body(*refs):
        ins, outs = refs[:n], refs[n:2 * n]
        send_sems, recv_sems = refs[2 * n:]
        x, y, c = _position()

        def remote(w, core):
            half = _half_rows(core, ins[w].shape[1])
            return pltpu.make_async_remote_copy(
                src_ref=ins[w].at[:, half], dst_ref=outs[w].at[:, half],
                send_sem=send_sems.at[w], recv_sem=recv_sems.at[w],
                device_id=(x, y, 1 - c), device_id_type=MESH)

        for w in range(n):
            remote(w, c).start()
        for w in range(n):
            remote(w, 1 - c).wait_recv()
        for w in range(n):
            remote(w, c).wait_send()

    return pl.pallas_call(
        body, name=name, in_specs=[_ANY] * n, out_specs=[_ANY] * n,
        out_shape=[_hbm_out(f.shape, f.dtype) for f in full],
        input_output_aliases={w: w for w in range(n)},
        scratch_shapes=[pltpu.SemaphoreType.DMA((n,)), pltpu.SemaphoreType.DMA((n,))],
    )(*full)


def _gather_all(name, a):
    def body(a_ref, o_ref, send_sems, recv_sems, local_sem):
        x, y, c = _position()
        me = 4 * x + 2 * y + c

        def peer(k):
            return (x ^ ((k >> 2) & 1), y ^ ((k >> 1) & 1), c ^ (k & 1))

        def remote(k, slot):
            return pltpu.make_async_remote_copy(
                src_ref=a_ref, dst_ref=o_ref.at[slot], send_sem=send_sems.at[k - 1], recv_sem=recv_sems.at[k - 1],
                device_id=peer(k), device_id_type=MESH)

        local = pltpu.make_async_copy(a_ref, o_ref.at[me], local_sem)
        local.start()
        for k in range(1, N_DEV):
            remote(k, me).start()
        for k in range(1, N_DEV):
            px, py, pc = peer(k)
            remote(k, 4 * px + 2 * py + pc).wait_recv()
        for k in range(1, N_DEV):
            remote(k, me).wait_send()
        local.wait()

    return pl.pallas_call(
        body, name=name, in_specs=[_ANY], out_specs=_ANY,
        out_shape=_hbm_out((N_DEV,) + a.shape, a.dtype),
        scratch_shapes=[pltpu.SemaphoreType.DMA((N_DEV - 1,)), pltpu.SemaphoreType.DMA((N_DEV - 1,)),
                        pltpu.SemaphoreType.DMA],
    )(a)


def _rows2d(a, lead=0):
    return a.reshape(a.shape[:lead] + (-1, a.shape[-1]))


def _row_tile(rows, cols, itemsize=4, target=1 << 20):
    want = max(SUBLANES, target // (cols * itemsize))
    t = min(rows, (want // 16) * 16)
    while t > 16 and rows % t:
        t -= 16
    return t if rows % t == 0 else rows


def _sum_slots(name, r, out_dtype=F32):
    ns = r.shape[0]
    r2 = _rows2d(r, 1)
    _, rows, cols = r2.shape
    tr = _row_tile(rows, cols)

    def body(r_ref, o_ref):
        acc = r_ref[0].astype(F32)
        for s in range(1, ns):
            acc = acc + r_ref[s].astype(F32)
        o_ref[...] = acc.astype(o_ref.dtype)

    out = pl.pallas_call(
        body, name=name, grid=(rows // tr,),
        in_specs=[pl.BlockSpec((ns, tr, cols), lambda i: (0, i, 0))],
        out_specs=pl.BlockSpec((tr, cols), lambda i: (i, 0)),
        out_shape=_hbm_out((rows, cols), out_dtype),
        compiler_params=_params(("parallel",)),
    )(r2)
    return out.reshape(r.shape[1:])


def _adamw(name, g_parts, w, m, v):
    shape = w.shape
    ng = len(g_parts)
    args = [_rows2d(a) for a in (*g_parts, w, m, v)]
    rows, cols = args[0].shape
    tr = _row_tile(rows, cols, (ng + 7) * 4, target=16 << 20)
    c1 = 1.0 - ADAM_B1 ** ADAM_STEP
    c2 = 1.0 - ADAM_B2 ** ADAM_STEP

    def body(*refs):
        g = refs[0][...]
        for r in refs[1:ng]:
            g = g + r[...]
        w_ref, m_ref, v_ref = refs[ng:ng + 3]
        g_out, d_out, m_out, v_out = refs[ng + 3:]
        mn = ADAM_B1 * m_ref[...] + (1.0 - ADAM_B1) * g
        vn = ADAM_B2 * v_ref[...] + (1.0 - ADAM_B2) * (g * g)
        m_hat = mn / c1
        v_hat = vn / c2
        g_out[...] = g
        d_out[...] = -ADAM_LR * (m_hat / (jnp.sqrt(v_hat) + ADAM_EPS) + ADAM_WD * w_ref[...])
        m_out[...] = mn
        v_out[...] = vn

    spec = pl.BlockSpec((tr, cols), lambda i: (i, 0))
    outs = pl.pallas_call(
        body, name=name, grid=(rows // tr,), in_specs=[spec] * (ng + 3), out_specs=[spec] * 4,
        out_shape=[_hbm_out((rows, cols), F32)] * 4,
        compiler_params=_params(("parallel",)),
    )(*args)
    return tuple(o.reshape(shape) for o in outs)


_WEIGHTS = ["norm_mix", "norm_ffn", "w_ffn_in", "w_ffn_out", "w_rec_in", "conv_w", "conv_b", "w_lru_gates",
            "b_lru_gates", "lru_param", "w_rec_out", "norm_kv", "w_kvf", "b_forget", "w_q", "w_o", "norm_final"]
_BIG = ["w_ffn_in", "w_ffn_out", "w_rec_in", "w_lru_gates", "w_rec_out", "w_kvf", "w_q", "w_o"]


def _stack3(a):
    return a[None] if a.ndim == 2 else a.reshape(a.shape[0], -1, a.shape[-1])


def _pad_lanes(a, n):
    return jnp.pad(a, ((0, 0),) * (a.ndim - 1) + ((0, n - a.shape[-1]),))


def kernel(x, norm_mix, norm_ffn, w_ffn_in, w_ffn_out, w_rec_in, conv_w, conv_b, w_lru_gates, b_lru_gates, lru_param, w_rec_out, norm_kv, w_kvf, b_forget, w_q, w_o, norm_final, loss_target, m_norm_mix, m_norm_ffn, m_w_ffn_in, m_w_ffn_out, m_w_rec_in, m_conv_w, m_conv_b, m_w_lru_gates, m_b_lru_gates, m_lru_param, m_w_rec_out, m_norm_kv, m_w_kvf, m_b_forget, m_w_q, m_w_o, m_norm_final, v_norm_mix, v_norm_ffn, v_w_ffn_in, v_w_ffn_out, v_w_rec_in, v_conv_w, v_conv_b, v_w_lru_gates, v_b_lru_gates, v_lru_param, v_w_rec_out, v_norm_kv, v_w_kvf, v_b_forget, v_w_q, v_w_o, v_norm_final):
    P = dict(norm_mix=norm_mix, norm_ffn=norm_ffn, w_ffn_in=w_ffn_in, w_ffn_out=w_ffn_out, w_rec_in=w_rec_in,
             conv_w=conv_w, conv_b=conv_b, w_lru_gates=w_lru_gates, b_lru_gates=b_lru_gates, lru_param=lru_param,
             w_rec_out=w_rec_out, norm_kv=norm_kv, w_kvf=w_kvf, b_forget=b_forget, w_q=w_q, w_o=w_o,
             norm_final=norm_final)
    M1 = dict(norm_mix=m_norm_mix, norm_ffn=m_norm_ffn, w_ffn_in=m_w_ffn_in, w_ffn_out=m_w_ffn_out,
              w_rec_in=m_w_rec_in, conv_w=m_conv_w, conv_b=m_conv_b, w_lru_gates=m_w_lru_gates,
              b_lru_gates=m_b_lru_gates, lru_param=m_lru_param, w_rec_out=m_w_rec_out, norm_kv=m_norm_kv,
              w_kvf=m_w_kvf, b_forget=m_b_forget, w_q=m_w_q, w_o=m_w_o, norm_final=m_norm_final)
    M2 = dict(norm_mix=v_norm_mix, norm_ffn=v_norm_ffn, w_ffn_in=v_w_ffn_in, w_ffn_out=v_w_ffn_out,
              w_rec_in=v_w_rec_in, conv_w=v_conv_w, conv_b=v_conv_b, w_lru_gates=v_w_lru_gates,
              b_lru_gates=v_b_lru_gates, lru_param=v_lru_param, w_rec_out=v_w_rec_out, norm_kv=v_norm_kv,
              w_kvf=v_w_kvf, b_forget=v_b_forget, w_q=v_w_q, w_o=v_w_o, norm_final=v_norm_final)

    _, S, D = x.shape
    L = norm_mix.shape[0]
    NA, NBLK, BW, GS = w_lru_gates.shape
    C = NBLK * BW
    CS = C // N_CHIPS
    H = b_forget.shape[0]
    assert C == D and H * HEAD_DIM == D and H <= LANES
    chip = 2 * lax.axis_index("x") + lax.axis_index("y")

    small_a = jnp.concatenate([conv_w, conv_b[:, None], lru_param[:, None]], axis=1)
    small_a, b_gates = _gather_smalls("gather_smalls", [small_a, b_lru_gates])
    small_a = small_a.transpose(1, 2, 0, 3).reshape(NA, 6, C)
    b_gates = b_gates.transpose(1, 2, 0, 3).reshape(NA, NBLK, 1, N_CHIPS * GS)
    shards = [_stack3(P[w]).astype(BF16) for w in _BIG]
    core = lax.axis_index("c")
    chip_id = jnp.reshape(chip, (1,)).astype(jnp.int32)
    core_id = jnp.reshape(core, (1,)).astype(jnp.int32)
    me_core = jnp.stack([chip, core]).astype(jnp.int32)

    parts_of_layer = ("mix", "mix2", "ffn")

    def part_items(l, part):
        if part == "ffn":
            names, at = ["w_ffn_in", "w_ffn_out"], l
        elif l < NA:
            names, at = (["w_rec_in"] if part == "mix" else ["w_lru_gates", "w_rec_out"]), l
        else:
            names, at = ((["w_kvf"] if l == NA else []) + ["w_q"] if part == "mix" else ["w_o"]), l - NA
        return [(_BIG.index(n), 0 if n == "w_kvf" else at) for n in names]

    def stage_of(l, part):
        return (l, part) if l == 0 or part == "ffn" else (l, "mixer")

    def stage_items(st):
        l, part = st
        return [it for p in (("mix", "mix2") if part == "mixer" else (part,)) for it in part_items(l, p)]

    stages = [(0, p) for p in parts_of_layer] + [(l, p) for l in range(1, L) for p in ("mixer", "ffn")]
    items = [it for st in stages for it in stage_items(st)]
    ids_of = {st: [items.index(it) for it in stage_items(st)] for st in stages}
    bufs = [_place_own(f"place_{_BIG[w]}_{li}", shards[w], li, chip_id) for w, li in items]
    bufs, gather_sems = _gather_start("gather_start", shards, bufs, items, small_a)

    forwarding, fetched = {}, {}

    def layer_prefetch(l, part, after):
        st = stage_of(l, part)
        if l < L and st not in forwarding:
            got = _gather_wait(f"gather_wait_{st[1]}_{l}", shards, bufs, items, ids_of[st], gather_sems, after)
            forwarding[st] = _forward_start(f"forward_start_{st[1]}_{l}", got)

    def layer_weights(l, part, after):
        if l >= L:
            return None
        st = stage_of(l, part)
        if st not in fetched:
            layer_prefetch(l, part, after)
            got, sems = forwarding[st]
            got = _forward_wait(f"forward_wait_{st[1]}_{l}", got, sems, after)
            fetched[st] = {_BIG[items[i][0]]: g for i, g in zip(ids_of[st], got)}
        B = fetched[st]
        if part == "ffn":
            return dict(w_ffn_in=B["w_ffn_in"], w_ffn_out=B["w_ffn_out"].reshape(-1, D))
        if l < NA and part == "mix":
            return dict(w_rec_in=B["w_rec_in"], conv_w=small_a[l, :4], conv_b=small_a[l, 4:5])
        if l < NA:
            return dict(w_gates=B["w_lru_gates"].reshape(N_CHIPS, NBLK, BW, GS).transpose(1, 2, 0, 3).reshape(
                NBLK, BW, N_CHIPS * GS), b_gates=b_gates[l], w_rec_out=B["w_rec_out"].reshape(C, D),
                lru_param=small_a[l, 5:6])
        if part == "mix2":
            return dict(w_o=B["w_o"].reshape(D, D))
        W = dict(w_q=B["w_q"].reshape(D, D))
        if l == NA:
            w_kvf_full = B["w_kvf"].transpose(1, 0, 2).reshape(D, -1)
            W.update(norm_kv=norm_kv[None], w_kv=w_kvf_full[:, :2 * D],
                     w_f=_pad_lanes(w_kvf_full[:, 2 * D:], LANES), b_f=_pad_lanes(b_forget[None], LANES))
        return W

    G_small = {l: {} for l in range(L)}
    stash = {st: {} for st in stages}
    pending = {}
    reducing = []

    def finish_reduce(after):
        st, its, grads, others, sems = reducing.pop()
        l, part = st
        others = _reduce_wait(f"reduce_wait_{part}_{l}", grads, others, sems, after)
        parts = [_sum_cores(f"sum_cores_{l}_{_BIG[w]}", g, o, core_id) for (w, _), g, o in zip(its, grads, others)]
        recv, sems, token = _scatter_start(f"scatter_start_{part}_{l}", parts)
        pending[st] = (parts, recv, sems)
        return token

    def layer_grads(l, part, G_part):
        G_small[l].update(G_part)
        st = stage_of(l, part)
        stash[st].update(G_part)
        if st[1] == "mixer" and part != "mix":
            return None
        G = stash[st]
        late = {"ffn": "w_ffn_in", "mix": "norm_mix"}.get(part) or ("w_gates" if l < NA else "w_o")
        after = finish_reduce(G_part[late]) if reducing else jnp.zeros((SUBLANES, LANES), F32)
        by_name = dict(
            w_ffn_in=lambda: G["w_ffn_in"], w_ffn_out=lambda: G["w_ffn_out"].reshape(N_CHIPS, -1, D),
            w_rec_in=lambda: G["w_rec_in"],
            w_lru_gates=lambda: G["w_gates"].reshape(NBLK, BW, N_CHIPS, GS).transpose(2, 0, 1, 3).reshape(
                N_CHIPS, NBLK * BW, GS),
            w_rec_out=lambda: G["w_rec_out"].reshape(N_CHIPS, -1, D),
            w_kvf=lambda: jnp.concatenate([G["w_kv"].astype(F32), G["w_f"][:, :H]], axis=1).reshape(
                D, N_CHIPS, -1).transpose(1, 0, 2).astype(BF16),
            w_q=lambda: G["w_q"].reshape(N_CHIPS, -1, D), w_o=lambda: G["w_o"].reshape(N_CHIPS, -1, D))
        its = stage_items(st)
        grads = [by_name[_BIG[w]]() for w, _ in its]
        others, sems, token = _reduce_start(f"reduce_start_{st[1]}_{l}", grads, after)
        reducing.append((st, its, grads, others, sems))
        return finish_reduce(token) if l == 0 else token

    gains = dict(mix=[norm_mix[l][None] for l in range(L)], ffn=[norm_ffn[l][None] for l in range(L)],
                 final=norm_final[None])
    loss_row, grad_x, dg_final = _local_step(x.reshape(S, D), loss_target.reshape(S, D), gains,
                                             layer_weights, layer_prefetch, layer_grads)

    rows = [*[G_small[l]["norm_mix"] for l in range(L)], *[G_small[l]["norm_ffn"] for l in range(L)],
            G_small[NA]["norm_kv"], dg_final, _pad_lanes(G_small[NA]["b_f"], D), _pad_lanes(loss_row, D)]
    for a in range(NA):
        rows += [G_small[a][n] for n in ("conv_w", "conv_b", "b_gi", "b_gr", "lru_param")]
    packed = jnp.concatenate(rows, axis=0)
    tot = _sum_slots("sum_small", _gather_all("gather_small", packed))
    loss = tot[2 * L + 3, 0]
    g_rep = jnp.concatenate([tot[:2 * L + 2], tot[2 * L + 2:2 * L + 3]], axis=0)
    base = 2 * L + 4
    g_sh = []
    for a in range(NA):
        blk = lax.dynamic_slice_in_dim(tot[base + 8 * a:base + 8 * a + 8], chip * CS, CS, axis=1)
        gi = tot[base + 8 * a + 5].reshape(NBLK, BW)
        gr = tot[base + 8 * a + 6].reshape(NBLK, BW)
        bl = lax.dynamic_slice_in_dim(jnp.concatenate([gi, gr], axis=1), chip * GS, GS, axis=1)
        g_sh += [blk[:5], bl.reshape(-1, CS), blk[7:8]]
    g_sh = jnp.concatenate(g_sh, axis=0)
    nrow = g_sh.shape[0] // NA

    def pack_rep(T):
        return jnp.concatenate([T["norm_mix"], T["norm_ffn"], T["norm_kv"][None], T["norm_final"][None],
                                _pad_lanes(T["b_forget"][None], D)], axis=0)

    def pack_sh(T):
        return jnp.concatenate([jnp.concatenate([T["conv_w"][a], T["conv_b"][a][None],
                                                 T["b_lru_gates"][a].reshape(-1, CS), T["lru_param"][a][None]], axis=0)
                                for a in range(NA)], axis=0)

    rep = _adamw("adamw_replicated", [g_rep], pack_rep(P), pack_rep(M1), pack_rep(M2))
    shd = _adamw("adamw_small_sharded", [g_sh], pack_sh(P), pack_sh(M1), pack_sh(M2))

    def unpack_rep(t):
        return dict(norm_mix=t[:L], norm_ffn=t[L:2 * L], norm_kv=t[2 * L], norm_final=t[2 * L + 1],
                    b_forget=t[2 * L + 2, :H])

    def unpack_sh(t):
        t = t.reshape(NA, nrow, CS)
        return dict(conv_w=t[:, :4], conv_b=t[:, 4], b_lru_gates=t[:, 5:nrow - 1].reshape(NA, NBLK, GS),
                    lru_param=t[:, nrow - 1])

    full = [lax.empty(sh.shape, F32) for sh in shards]
    for st in reversed(stages):
        l, part = st
        parts, recv, sems = pending[st]
        recv = _scatter_wait(f"scatter_wait_{part}_{l}", parts, recv, sems)
        for (w, li), own, r in zip(stage_items(st), parts, recv):
            full[w] = _sum_chips(f"sum_chips_{l}_{_BIG[w]}", r, own, full[w], li, me_core)
    full = _share_d2d("share_d2d", full)
    big = {w: _adamw(f"adamw_{w}", [g.reshape(P[w].shape)], P[w], M1[w], M2[w]) for w, g in zip(_BIG, full)}

    outs = []
    for i in range(4):
        small = {**unpack_rep(rep[i]), **unpack_sh(shd[i])}
        outs.append([big[w][i] if w in big else small[w] for w in _WEIGHTS])
    return (loss, grad_x.reshape(1, S, D), *outs[0], *outs[1], *outs[2], *outs[3])
```

```python
import functools
import math

import jax
import jax.numpy as jnp
from jax import lax
from jax.experimental import pallas as pl
from jax.experimental.pallas import tpu as pltpu

F32 = jnp.float32
BF16 = jnp.bfloat16

EPS = 1e-6
LRU_C = 8.0
HEAD_DIM = 64
LANES = 128
SUBLANES = 8
VMEM_LIMIT = 48 * 1024 * 1024
N_CHIPS = 4
N_DEV = 8

ADAM_LR = 0.001
ADAM_B1 = 0.9
ADAM_B2 = 0.999
ADAM_EPS = 1e-08
ADAM_WD = 0.01
ADAM_STEP = 10

_NN = (((1,), (0,)), ((), ()))
_NT = (((1,), (1,)), ((), ()))
_TN = (((0,), (0,)), ((), ()))
_DN = {"nn": _NN, "nt": _NT, "tn": _TN}
MESH = pl.DeviceIdType.MESH


def _hbm_out(shape, dtype):
    return pltpu.HBM(shape, dtype)


def _params(sem):
    return pltpu.CompilerParams(dimension_semantics=sem, vmem_limit_bytes=VMEM_LIMIT)


def _tile(n, want):
    if n <= want:
        return n
    t = (want // LANES) * LANES
    while t >= LANES:
        if n % t == 0:
            return t
        t -= LANES
    return n


def _sigmoid(x):
    return 1.0 / (1.0 + jnp.exp(-x))


def _sigmoid_t(x):
    return 0.5 * jnp.tanh(0.5 * x) + 0.5


def _softplus(x):
    return jnp.maximum(x, 0.0) + jnp.log(1.0 + jnp.exp(-jnp.abs(x)))


_GELU_C = math.sqrt(2.0 / math.pi)


def _gelu_and_grad(x):
    inner = _GELU_C * (x + 0.044715 * x * x * x)
    t = jnp.tanh(inner)
    g = 0.5 * x * (1.0 + t)
    dg = 0.5 * (1.0 + t) + 0.5 * x * (1.0 - t * t) * _GELU_C * (1.0 + 3.0 * 0.044715 * x * x)
    return g, dg


def _rms(x):
    return lax.rsqrt(jnp.mean(x * x, axis=-1, keepdims=True) + EPS)


def _rms_bwd(dy, x, g):
    r = _rms(x)
    xr = x * r
    dyg = dy * g
    return r * dyg - xr * (r * jnp.mean(dyg * xr, axis=-1, keepdims=True)), jnp.sum(dy * xr, axis=0, keepdims=True)


def _mm(name, mode, a, b, *, grid, a_spec, b_spec, out_shape, out_dtype, out_spec, nk=1,
        res=None, res_spec=None, bias=None, bias_spec=None, scale=None, norm_gain=None, norm_bwd=None):
    dn = _DN[mode]
    has_res, has_bias = res is not None, bias is not None
    blk = tuple(d for d in out_spec.block_shape if d is not None)
    vec = pl.BlockSpec((1, blk[-1]), lambda *g: (0, 0))
    a_specs = a_spec if isinstance(a_spec, list) else [a_spec]
    b_specs = b_spec if isinstance(b_spec, list) else [b_spec]
    npair = len(a_specs)
    n_in = 2 * npair + int(has_res) + int(has_bias) + (1 if norm_gain is not None else 0) + (3 if norm_bwd else 0)

    def body(*refs):
        p = 2 * npair
        r_ref = refs[p] if has_res else None
        p += int(has_res)
        bias_ref = refs[p] if has_bias else None
        p += int(has_bias)
        extra = refs[p:n_in]
        outs = refs[n_in:]
        o_ref = outs[0]
        part = lax.dot_general(refs[0][...], refs[npair][...], dn, preferred_element_type=F32)
        for t in range(1, npair):
            part = part + lax.dot_general(refs[t][...], refs[npair + t][...], dn, preferred_element_type=F32)

        def finish(acc):
            if scale is not None:
                acc = acc * scale
            if has_bias:
                acc = acc + bias_ref[...]
            if has_res:
                acc = r_ref[...] + acc
            if norm_bwd:
                h_ref, g_ref, dh_ref = extra
                dx, dg = _rms_bwd(acc, h_ref[...], g_ref[...])
                acc = dh_ref[...] + dx
                outs[1][...] = acc.astype(BF16)
                outs[2][...] = dg
            if norm_gain is not None:
                outs[1][...] = (acc * _rms(acc) * extra[0][...]).astype(BF16)
            o_ref[...] = acc.astype(o_ref.dtype)

        if nk == 1:
            finish(part)
        else:
            acc_ref = refs[-1]
            k = pl.program_id(2)

            @pl.when(k == 0)
            def _():
                acc_ref[...] = part

            @pl.when(k > 0)
            def _():
                acc_ref[...] += part

            @pl.when(k == nk - 1)
            def _():
                finish(acc_ref[...])

    ins, specs = [a] * npair + [b] * npair, a_specs + b_specs
    if has_res:
        ins.append(res)
        specs.append(res_spec)
    if has_bias:
        ins.append(bias)
        specs.append(bias_spec)
    out_specs, out_shapes = [out_spec], [_hbm_out(out_shape, out_dtype)]
    if norm_gain is not None:
        ins.append(norm_gain)
        specs.append(vec)
        out_specs.append(out_spec)
        out_shapes.append(_hbm_out(out_shape, BF16))
    if norm_bwd:
        h, g, dh = norm_bwd
        ins += [h, g, dh]
        specs += [out_spec, vec, out_spec]
        out_specs += [out_spec, pl.BlockSpec((None, 1, blk[-1]), lambda i, *rest: (i, 0, 0))]
        out_shapes += [_hbm_out(out_shape, BF16), _hbm_out((grid[0], 1, blk[-1]), F32)]
    sem = ("parallel", "parallel") + (("arbitrary",) if len(grid) == 3 else ())
    single = len(out_specs) == 1
    return pl.pallas_call(
        body, name=name, grid=grid, in_specs=specs, out_specs=out_specs[0] if single else out_specs,
        out_shape=out_shapes[0] if single else out_shapes,
        scratch_shapes=[pltpu.VMEM(blk, F32)] if nk > 1 else [],
        compiler_params=_params(sem),
    )(*ins)


def _mm_nn(name, a, b, *, b_lead=(), out_dtype, tm=512, tn=512, res=None, bias=None, scale=None, norm_gain=None):
    M, K = a.shape
    N = b.shape[-1]
    tm, tn = _tile(M, tm), _tile(N, tn)
    nl = len(b_lead)
    return _mm(
        name, "nn", a, b, grid=(M // tm, N // tn),
        a_spec=pl.BlockSpec((tm, K), lambda i, j: (i, 0)),
        b_spec=pl.BlockSpec((None,) * nl + (K, tn), lambda i, j: tuple(b_lead) + (0, j)),
        out_shape=(M, N), out_dtype=out_dtype, out_spec=pl.BlockSpec((tm, tn), lambda i, j: (i, j)),
        res=res, res_spec=pl.BlockSpec((tm, tn), lambda i, j: (i, j)),
        bias=bias, bias_spec=pl.BlockSpec((1, tn), lambda i, j: (0, j)), scale=scale, norm_gain=norm_gain)


def _mm_nt(name, a, b, *, b_lead=(), out_dtype, tm=512, tn=512, tk=2048, res=None, norm_bwd=None):
    M, K = a.shape
    N = b.shape[-2]
    tm, tn, tk = _tile(M, tm), _tile(N, tn), _tile(K, tk)
    nk = K // tk
    nl = len(b_lead)
    return _mm(
        name, "nt", a, b, grid=(M // tm, N // tn, nk), nk=nk,
        a_spec=pl.BlockSpec((tm, tk), lambda i, j, k: (i, k)),
        b_spec=pl.BlockSpec((None,) * nl + (tn, tk), lambda i, j, k: tuple(b_lead) + (j, k)),
        out_shape=(M, N), out_dtype=out_dtype, out_spec=pl.BlockSpec((tm, tn), lambda i, j, k: (i, j)),
        res=res, res_spec=pl.BlockSpec((tm, tn), lambda i, j, k: (i, j)), norm_bwd=norm_bwd)


def _mm_tn(name, a, b, *, out_dtype, tm=512, tn=512):
    S, M = a.shape
    N = b.shape[1]
    tm, tn = _tile(M, tm), _tile(N, tn)
    return _mm(
        name, "tn", a, b, grid=(M // tm, N // tn),
        a_spec=pl.BlockSpec((S, tm), lambda i, j: (0, i)),
        b_spec=pl.BlockSpec((S, tn), lambda i, j: (0, j)),
        out_shape=(M, N), out_dtype=out_dtype, out_spec=pl.BlockSpec((tm, tn), lambda i, j: (i, j)))


def _rmsnorm_fwd(name, h, g, tr=256):
    S, D = h.shape
    tr = _tile(S, tr)

    def body(h_ref, g_ref, o_ref):
        x = h_ref[...]
        r = lax.rsqrt(jnp.mean(x * x, axis=-1, keepdims=True) + EPS)
        o_ref[...] = (x * r * g_ref[...]).astype(o_ref.dtype)

    return pl.pallas_call(
        body, name=name, grid=(S // tr,),
        in_specs=[pl.BlockSpec((tr, D), lambda i: (i, 0)), pl.BlockSpec((1, D), lambda i: (0, 0))],
        out_specs=pl.BlockSpec((tr, D), lambda i: (i, 0)),
        out_shape=_hbm_out((S, D), BF16),
        compiler_params=_params(("parallel",)),
    )(h, g)


def _loss_head(name, h, target, g, tr=256):
    S, D = h.shape
    tr = _tile(S, tr)

    def body(h_ref, t_ref, g_ref, o_ref, ob_ref, dg_ref, loss_ref):
        i = pl.program_id(0)
        x = h_ref[...]
        gg = g_ref[...]
        r = lax.rsqrt(jnp.mean(x * x, axis=-1, keepdims=True) + EPS)
        xr = x * r
        err = xr * gg - t_ref[...]
        lpart = 0.5 * jnp.sum(jnp.mean(err * err, axis=-1, keepdims=True), axis=0, keepdims=True)
        dy = err * (1.0 / D)
        dyg = dy * gg
        dx = r * dyg - xr * (r * jnp.mean(dyg * xr, axis=-1, keepdims=True))
        o_ref[...] = dx
        ob_ref[...] = dx.astype(BF16)
        part = jnp.sum(dy * xr, axis=0, keepdims=True)
        lrow = jnp.broadcast_to(lpart, (1, LANES))

        @pl.when(i == 0)
        def _():
            dg_ref[...] = part
            loss_ref[...] = lrow

        @pl.when(i > 0)
        def _():
            dg_ref[...] += part
            loss_ref[...] += lrow

    row = pl.BlockSpec((tr, D), lambda i: (i, 0))
    vec = pl.BlockSpec((1, D), lambda i: (0, 0))
    return pl.pallas_call(
        body, name=name, grid=(S // tr,),
        in_specs=[row, row, vec], out_specs=[row, row, vec, pl.BlockSpec((1, LANES), lambda i: (0, 0))],
        out_shape=[_hbm_out((S, D), F32), _hbm_out((S, D), BF16),
                   _hbm_out((1, D), F32), _hbm_out((1, LANES), F32)],
        compiler_params=_params(("arbitrary",)),
    )(h, target, g)


def _swiglu_fwd(name, hn, w_in, tm=512):
    S, D = hn.shape
    FH = w_in.shape[-1]
    tm = _tile(S, tm)

    def body(x_ref, wg_ref, wu_ref, z_ref, a_ref):
        x = x_ref[...]
        zg = jnp.dot(x, wg_ref[...], preferred_element_type=F32)
        zu = jnp.dot(x, wu_ref[...], preferred_element_type=F32)
        z_ref[0] = zg.astype(z_ref.dtype)
        z_ref[1] = zu.astype(z_ref.dtype)
        a_ref[...] = (zg * _sigmoid_t(zg) * zu).astype(a_ref.dtype)

    return pl.pallas_call(
        body, name=name, grid=(S // tm, 2),
        in_specs=[pl.BlockSpec((tm, D), lambda i, j: (i, 0)),
                  pl.BlockSpec((None, D, FH), lambda i, j: (j, 0, 0)),
                  pl.BlockSpec((None, D, FH), lambda i, j: (j + 2, 0, 0))],
        out_specs=[pl.BlockSpec((2, tm, FH), lambda i, j: (0, i, j)), pl.BlockSpec((tm, FH), lambda i, j: (i, j))],
        out_shape=[_hbm_out((2, S, 2 * FH), BF16), _hbm_out((S, 2 * FH), BF16)],
        compiler_params=_params(("parallel", "parallel")),
    )(hn, w_in, w_in)


def _swiglu_bwd(name, dhb, w_out, z3, tm=512):
    S, D = dhb.shape
    F = w_out.shape[0]
    FH = F // 2
    tm = _tile(S, tm)

    def body(d_ref, w_ref, z_ref, dz_ref):
        d = lax.dot_general(d_ref[...], w_ref[...], _NT, preferred_element_type=F32)
        zg = z_ref[0].astype(F32)
        zu = z_ref[1].astype(F32)
        sg = _sigmoid_t(zg)
        dz_ref[0] = (d * zu * (sg * (1.0 + zg * (1.0 - sg)))).astype(dz_ref.dtype)
        dz_ref[1] = (d * (zg * sg)).astype(dz_ref.dtype)

    zspec = pl.BlockSpec((2, tm, FH), lambda i, j: (0, i, j))
    return pl.pallas_call(
        body, name=name, grid=(S // tm, 2),
        in_specs=[pl.BlockSpec((tm, D), lambda i, j: (i, 0)), pl.BlockSpec((FH, D), lambda i, j: (j, 0)), zspec],
        out_specs=zspec, out_shape=_hbm_out((2, S, F), BF16),
        compiler_params=_params(("parallel", "parallel")),
    )(dhb, w_out, z3)


SCAN_ROWS = 64


def _group_scan(A, B, reverse):
    n = A.shape[0]
    sub = lax.broadcasted_iota(jnp.int32, A.shape, 0) % SUBLANES
    for d in (1, 2, 4):
        if reverse:
            A_sh, B_sh = pltpu.roll(A, n - d, 0), pltpu.roll(B, n - d, 0)
            keep = sub < SUBLANES - d
        else:
            A_sh, B_sh = pltpu.roll(A, d, 0), pltpu.roll(B, d, 0)
            keep = sub >= d
        B = jnp.where(keep, A * B_sh + B, B)
        A = jnp.where(keep, A * A_sh, A)
    return A, B


def _block_scan(a, u, carry, reverse):
    A, B = _group_scan(a, u, reverse)
    ng = a.shape[0] // SUBLANES
    out = [None] * ng
    order = range(ng - 1, -1, -1) if reverse else range(ng)
    for gi in order:
        sl = slice(gi * SUBLANES, (gi + 1) * SUBLANES)
        hg = A[sl] * carry + B[sl]
        out[gi] = hg
        carry = hg[0:1] if reverse else hg[SUBLANES - 1:SUBLANES]
    return jnp.concatenate(out, axis=0), carry


def _lru_gates(rc, gip, grp, sp):
    gi = _sigmoid_t(gip)
    gr = _sigmoid_t(grp)
    la = -LRU_C * gr * sp
    a = jnp.exp(la)
    om = -jnp.tanh(la) * (a * a + 1.0)
    mult = jnp.sqrt(om)
    return gi, gr, a, mult


def _lru_fwd(name, proj, rc, gip, grp, lru_p, tc=256):
    S, C = rc.shape
    tc = _tile(C, tc)
    nb = S // SCAN_ROWS

    def body(gb_ref, rc_ref, gi_ref, gr_ref, l_ref, h_ref, m_ref):
        sp = _softplus(-l_ref[...])

        def step(b, carry):
            rows = pl.ds(pl.multiple_of(b * SCAN_ROWS, SCAN_ROWS), SCAN_ROWS)
            rcb = rc_ref[rows, :]
            gi, _, a, mult = _lru_gates(rcb, gi_ref[rows, :], gr_ref[rows, :], sp)
            h, carry = _block_scan(a, rcb * gi * mult, carry, False)
            h_ref[rows, :] = h
            gel, _ = _gelu_and_grad(gb_ref[rows, :])
            m_ref[rows, :] = (gel * h).astype(m_ref.dtype)
            return carry

        lax.fori_loop(0, nb, step, jnp.zeros((1, tc), F32))

    col = pl.BlockSpec((S, tc), lambda j: (0, j))
    return pl.pallas_call(
        body, name=name, grid=(C // tc,),
        in_specs=[col, col, col, col, pl.BlockSpec((1, tc), lambda j: (0, j))],
        out_specs=[col, col],
        out_shape=[_hbm_out((S, C), F32), _hbm_out((S, C), BF16)],
        compiler_params=_params(("parallel",)),
    )(proj, rc, gip, grp, lru_p)


def _lru_bwd(name, dm, proj, hrec, rc, gip, grp, lru_p, tc=256):
    S, C = rc.shape
    tc = _tile(C, tc)
    nb = S // SCAN_ROWS
    R = SCAN_ROWS

    def body(dm_ref, gb_ref, h_ref, rc_ref, gi_ref, gr_ref, l_ref,
             dgb_ref, dgi_ref, dgr_ref, drc_ref, dbi_ref, dbr_ref, dl_ref):
        lp = l_ref[...]
        sp = _softplus(-lp)
        row = lax.broadcasted_iota(jnp.int32, (R, tc), 0)
        zero = jnp.zeros((1, tc), F32)

        def step(t, carry):
            mu_in, s_i, s_r, s_sp = carry
            b = nb - 1 - t
            r0 = pl.multiple_of(b * R, R)
            rows = pl.ds(r0, R)
            rcb = rc_ref[rows, :]
            gi, gr, a, mult = _lru_gates(rcb, gi_ref[rows, :], gr_ref[rows, :], sp)
            gel, dgel = _gelu_and_grad(gb_ref[rows, :])
            dmb = dm_ref[rows, :]
            h = h_ref[rows, :]
            dgb_ref[rows, :] = (dmb * h * dgel).astype(dgb_ref.dtype)
            dh = dmb * gel
            mu, mu_out = _block_scan(a, a * dh, mu_in, True)
            mu_next = jnp.where(row == R - 1, mu_in, pltpu.roll(mu, R - 1, 0))
            lam = dh + mu_next
            p0 = pl.multiple_of(jnp.maximum(r0 - SUBLANES, 0), SUBLANES)
            prev = h_ref[pl.ds(p0, SUBLANES), :][SUBLANES - 1:SUBLANES]
            prev = jnp.where(b > 0, prev, 0.0)
            h_prev = jnp.where(row == 0, prev, pltpu.roll(h, 1, 0))
            da = lam * h_prev
            d_mult = lam * rcb * gi
            d_la = da * a - d_mult * (a * a) / mult
            d_grp = d_la * (-LRU_C * sp) * gr * (1.0 - gr)
            d_gip = lam * rcb * mult * gi * (1.0 - gi)
            dgr_ref[rows, :] = d_grp.astype(dgr_ref.dtype)
            dgi_ref[rows, :] = d_gip.astype(dgi_ref.dtype)
            drc_ref[rows, :] = lam * gi * mult
            s_i = s_i + jnp.sum(d_gip, axis=0, keepdims=True)
            s_r = s_r + jnp.sum(d_grp, axis=0, keepdims=True)
            s_sp = s_sp + jnp.sum(d_la * gr, axis=0, keepdims=True)
            return mu_out, s_i, s_r, s_sp

        _, s_i, s_r, s_sp = lax.fori_loop(0, nb, step, (zero, zero, zero, zero))
        dbi_ref[...] = s_i
        dbr_ref[...] = s_r
        dl_ref[...] = (-LRU_C * s_sp) * (-_sigmoid(-lp))

    col = pl.BlockSpec((S, tc), lambda j: (0, j))
    vec = pl.BlockSpec((1, tc), lambda j: (0, j))
    return pl.pallas_call(
        body, name=name, grid=(C // tc,),
        in_specs=[col, col, col, col, col, col, vec],
        out_specs=[col, col, col, col, vec, vec, vec],
        out_shape=[_hbm_out((S, C), BF16), _hbm_out((S, C), BF16),
                   _hbm_out((S, C), BF16), _hbm_out((S, C), F32),
                   _hbm_out((1, C), F32), _hbm_out((1, C), F32),
                   _hbm_out((1, C), F32)],
        compiler_params=_params(("parallel",)),
    )(dm, proj, hrec, rc, gip, grp, lru_p)


def _cumsum_rows(name, u, reverse):
    S, C = u.shape
    nb = S // SCAN_ROWS

    def body(u_ref, o_ref):
        def step(t, carry):
            b = nb - 1 - t if reverse else t
            rows = pl.ds(pl.multiple_of(b * SCAN_ROWS, SCAN_ROWS), SCAN_ROWS)
            ub = u_ref[rows, :]
            h, carry = _block_scan(jnp.ones_like(ub), ub, carry, reverse)
            o_ref[rows, :] = h
            return carry

        lax.fori_loop(0, nb, step, jnp.zeros((1, C), F32))

    spec = pl.BlockSpec((S, C), lambda i: (0, 0))
    return pl.pallas_call(
        body, name=name, grid=(1,), in_specs=[spec], out_specs=spec,
        out_shape=_hbm_out((S, C), F32),
        compiler_params=_params(("arbitrary",)),
    )(u)


def _shift_down(x, k):
    row = lax.broadcasted_iota(jnp.int32, x.shape, 0)
    return jnp.where(row >= k, pltpu.roll(x, k, 0), 0.0)


def _shift_up(x, k):
    n = x.shape[0]
    row = lax.broadcasted_iota(jnp.int32, x.shape, 0)
    return jnp.where(row < n - k, pltpu.roll(x, n - k, 0), 0.0)


def _conv_fwd(name, proj, w, b, tc=256):
    S, C2 = proj.shape
    C = C2 // 2
    tc = _tile(C, tc)
    off = C // tc

    def body(x_ref, w_ref, b_ref, o_ref, ob_ref):
        x = x_ref[...]
        out = b_ref[...] + w_ref[3:4, :] * x
        for k in (1, 2, 3):
            out = out + w_ref[3 - k:4 - k, :] * _shift_down(x, k)
        o_ref[...] = out
        ob_ref[...] = out.astype(BF16)

    col = pl.BlockSpec((S, tc), lambda j: (0, j))
    return pl.pallas_call(
        body, name=name, grid=(C // tc,),
        in_specs=[pl.BlockSpec((S, tc), lambda j: (0, off + j)),
                  pl.BlockSpec((4, tc), lambda j: (0, j)), pl.BlockSpec((1, tc), lambda j: (0, j))],
        out_specs=[col, col],
        out_shape=[_hbm_out((S, C), F32), _hbm_out((S, C), BF16)],
        compiler_params=_params(("parallel",)),
    )(proj, w, b)


def _conv_bwd(name, drc, proj, w, tc=256):
    S, C = drc.shape
    tc = _tile(C, tc)
    off = C // tc

    def body(y_ref, x_ref, w_ref, dx_ref, dw_ref, db_ref):
        y = y_ref[...]
        x = x_ref[...]
        dx = w_ref[3:4, :] * y
        dw_ref[3:4, :] = jnp.sum(y * x, axis=0, keepdims=True)
        for k in (1, 2, 3):
            dx = dx + w_ref[3 - k:4 - k, :] * _shift_up(y, k)
            dw_ref[3 - k:4 - k, :] = jnp.sum(y * _shift_down(x, k), axis=0, keepdims=True)
        dx_ref[...] = dx.astype(dx_ref.dtype)
        db_ref[...] = jnp.sum(y, axis=0, keepdims=True)

    col = pl.BlockSpec((S, tc), lambda j: (0, j))
    return pl.pallas_call(
        body, name=name, grid=(C // tc,),
        in_specs=[col, pl.BlockSpec((S, tc), lambda j: (0, off + j)), pl.BlockSpec((4, tc), lambda j: (0, j))],
        out_specs=[col, pl.BlockSpec((4, tc), lambda j: (0, j)), pl.BlockSpec((1, tc), lambda j: (0, j))],
        out_shape=[_hbm_out((S, C), BF16), _hbm_out((4, C), F32),
                   _hbm_out((1, C), F32)],
        compiler_params=_params(("parallel",)),
    )(drc, proj, w)


def _gates_fwd(name, rcb, wg, bg):
    S, C = rcb.shape
    nblk, bw, _ = wg.shape

    def body(x_ref, w_ref, b_ref, gi_ref, gr_ref):
        g = jnp.dot(x_ref[...], w_ref[...], preferred_element_type=F32) + b_ref[...]
        gi_ref[...] = g[:, :bw]
        gr_ref[...] = g[:, bw:]

    col = pl.BlockSpec((S, bw), lambda n: (0, n))
    return pl.pallas_call(
        body, name=name, grid=(nblk,),
        in_specs=[col, pl.BlockSpec((None, bw, 2 * bw), lambda n: (n, 0, 0)),
                  pl.BlockSpec((None, 1, 2 * bw), lambda n: (n, 0, 0))],
        out_specs=[col, col],
        out_shape=[_hbm_out((S, C), F32), _hbm_out((S, C), F32)],
        compiler_params=_params(("parallel",)),
    )(rcb, wg, bg)


def _gates_bwd(name, dgi, dgr, rcb, wg, drc1):
    S, C = rcb.shape
    nblk, bw, _ = wg.shape

    def body(dgi_ref, dgr_ref, x_ref, w_ref, d1_ref, drc_ref, dw_ref):
        w = w_ref[...]
        x = x_ref[...]
        di, dr = dgi_ref[...], dgr_ref[...]
        drc_ref[...] = (d1_ref[...]
                        + lax.dot_general(di, w[:, :bw], _NT, preferred_element_type=F32)
                        + lax.dot_general(dr, w[:, bw:], _NT, preferred_element_type=F32))
        dw_ref[:, :bw] = lax.dot_general(x, di, _TN, preferred_element_type=F32).astype(dw_ref.dtype)
        dw_ref[:, bw:] = lax.dot_general(x, dr, _TN, preferred_element_type=F32).astype(dw_ref.dtype)

    col = pl.BlockSpec((S, bw), lambda n: (0, n))
    wspec = pl.BlockSpec((None, bw, 2 * bw), lambda n: (n, 0, 0))
    return pl.pallas_call(
        body, name=name, grid=(nblk,),
        in_specs=[col, col, col, wspec, col], out_specs=[col, wspec],
        out_shape=[_hbm_out((S, C), F32), _hbm_out((nblk, bw, 2 * bw), BF16)],
        compiler_params=_params(("parallel",)),
    )(dgi, dgr, rcb, wg, drc1)


def _att_tile(S):
    return next(t for t in (512, 256, 128) if S % t == 0)


def _head_lanes(shape):
    return lax.broadcasted_iota(jnp.int32, shape, len(shape) - 1) < HEAD_DIM


def _key_bias(c_blk):
    first = _head_lanes(c_blk.shape)
    rolled = pltpu.roll(c_blk, HEAD_DIM, 1)
    return jnp.where(first, c_blk, rolled), jnp.where(first, rolled, c_blk)


def _over_keys(x, op):
    n = x.shape[0]
    while n > SUBLANES:
        n //= 2
        x = op(x[:n], x[n:2 * n])
    return (jnp.max if op is jnp.maximum else jnp.sum)(x, axis=0, keepdims=True)


def _causal_t(T, cc):
    r = lax.broadcasted_iota(jnp.int32, (T, LANES), 0)
    c = lax.broadcasted_iota(jnp.int32, (T, LANES), 1) + cc * LANES
    return r <= c


def _attn_fwd(name, q, kv, cfull):
    S, D = q.shape
    HP = D // LANES
    T = _att_tile(S)
    nq = S // T
    NC = T // LANES

    def body(q_ref, k_ref, v_ref, c_ref, o_ref, of_ref, lse_ref, bias, vT, acc, m_scr, l_scr):
        def prologue(i, _):
            rows = pl.ds(pl.multiple_of(i * T, T), T)
            bias[0, rows, :], bias[1, rows, :] = _key_bias(c_ref[rows, :])
            vT[i] = v_ref[rows, :].astype(F32).T.astype(BF16)
            return 0

        lax.fori_loop(0, nq, prologue, 0)

        def q_step(qi, _):
            q0 = pl.multiple_of(qi * T, T)
            qb = q_ref[pl.ds(q0, T), :]
            m_scr[...] = jnp.full(m_scr.shape, -jnp.inf, F32)
            l_scr[...] = jnp.zeros(l_scr.shape, F32)
            acc[...] = jnp.zeros(acc.shape, F32)

            def tile(kj, masked):
                ks = pl.ds(pl.multiple_of(kj * T, T), T)
                kf = k_ref[ks, :].astype(F32)
                first = _head_lanes(kf.shape)
                kms = [jnp.where(first if hh == 0 else jnp.logical_not(first), kf, 0.0).astype(BF16) for hh in range(2)]
                sTs = [lax.dot_general(km, qb, _NT, preferred_element_type=F32) for km in kms]
                for hh in range(2):
                    b = bias[hh, ks, :]
                    ps = []
                    for cc in range(NC):
                        cols = slice(cc * LANES, (cc + 1) * LANES)
                        s = sTs[hh][:, cols] + b
                        if masked:
                            s = jnp.where(_causal_t(T, cc), s, -jnp.inf)
                        m_old = m_scr[hh, cc]
                        m_new = jnp.maximum(m_old, _over_keys(s, jnp.maximum))
                        alpha = jnp.exp(m_old - m_new)
                        p = jnp.exp(s - m_new)
                        l_scr[hh, cc] = alpha * l_scr[hh, cc] + _over_keys(p, jnp.add)
                        m_scr[hh, cc] = m_new
                        ps.append(p.astype(BF16))
                        acc[hh, :, cols] = acc[hh, :, cols] * alpha
                    acc[hh] += jnp.dot(vT[kj, hh * HEAD_DIM:(hh + 1) * HEAD_DIM, :], jnp.concatenate(ps, axis=1),
                                       preferred_element_type=F32)

            def inner(kj, _):
                tile(kj, False)
                return 0

            lax.fori_loop(0, qi, inner, 0)
            tile(qi, True)
            outs = []
            for hh in range(2):
                inv = jnp.concatenate([1.0 / l_scr[hh, cc] for cc in range(NC)], axis=1)
                outs.append(acc[hh] * inv)
                for cc in range(NC):
                    lse_ref[hh:hh + 1, pl.ds(q0 + cc * LANES, LANES)] = m_scr[hh, cc] + jnp.log(l_scr[hh, cc])
            out = jnp.concatenate(outs, axis=0).T
            o_ref[pl.ds(q0, T), :] = out.astype(o_ref.dtype)
            of_ref[pl.ds(q0, T), :] = out
            return 0

        lax.fori_loop(0, nq, q_step, 0)

    blk = lambda off: pl.BlockSpec((S, LANES), lambda p: (0, off + p))
    return pl.pallas_call(
        body, name=name, grid=(HP,),
        in_specs=[blk(0), blk(0), blk(HP), blk(0)],
        out_specs=[blk(0), blk(0), pl.BlockSpec((None, 2, S), lambda p: (p, 0, 0))],
        out_shape=[_hbm_out((S, D), BF16), _hbm_out((S, D), F32),
                   _hbm_out((HP, 2, S), F32)],
        scratch_shapes=[pltpu.VMEM((2, S, LANES), F32), pltpu.VMEM((nq, LANES, T), BF16),
                        pltpu.VMEM((2, HEAD_DIM, T), F32), pltpu.VMEM((2, NC, 1, LANES), F32),
                        pltpu.VMEM((2, NC, 1, LANES), F32)],
        compiler_params=_params(("parallel",)),
    )(q, kv, kv, cfull)


def _attn_bwd(name, q, kv, cfull, of, do, lse3):
    S, D = q.shape
    HP = D // LANES
    T = _att_tile(S)
    nq = S // T
    NC = T // LANES
    scale = HEAD_DIM ** -0.5

    def body(q_ref, k_ref, v_ref, c_ref, of_ref, do_ref, lse_ref,
             dq_ref, dk_ref, dv_ref, dck_ref, drq_ref, bias, kT, dqT, delta, dr_scr, dk_acc, dv_acc, dc_acc):
        def prologue(i, _):
            rows = pl.ds(pl.multiple_of(i * T, T), T)
            bias[0, rows, :], bias[1, rows, :] = _key_bias(c_ref[rows, :])
            kT[i] = k_ref[rows, :].astype(F32).T.astype(BF16)
            prodT = (do_ref[rows, :].astype(F32) * of_ref[rows, :]).T
            for hh in range(2):
                delta[hh:hh + 1, rows] = jnp.sum(prodT[hh * HEAD_DIM:(hh + 1) * HEAD_DIM], axis=0, keepdims=True)
            dqT[i] = jnp.zeros((LANES, T), F32)
            return 0

        lax.fori_loop(0, nq, prologue, 0)
        dr_scr[...] = jnp.zeros(dr_scr.shape, F32)

        def kv_step(kj, _):
            ks = pl.ds(pl.multiple_of(kj * T, T), T)
            kf = k_ref[ks, :].astype(F32)
            vf = v_ref[ks, :].astype(F32)
            first = _head_lanes(kf.shape)
            masks = [first, jnp.logical_not(first)]
            kms = [jnp.where(m, kf, 0.0).astype(BF16) for m in masks]
            vms = [jnp.where(m, vf, 0.0).astype(BF16) for m in masks]

            for acc in (dk_acc, dv_acc, dc_acc):
                acc[...] = jnp.zeros(acc.shape, F32)

            def tile(qi, masked):
                q0 = pl.multiple_of(qi * T, T)
                qb = q_ref[pl.ds(q0, T), :]
                dob = do_ref[pl.ds(q0, T), :]
                sTs = [lax.dot_general(km, qb, _NT, preferred_element_type=F32) for km in kms]
                dpTs = [lax.dot_general(vm, dob, _NT, preferred_element_type=F32) for vm in vms]
                for hh in range(2):
                    b = bias[hh, ks, :]
                    head = slice(hh * HEAD_DIM, (hh + 1) * HEAD_DIM)
                    ps, dss = [], []
                    for cc in range(NC):
                        cols = slice(cc * LANES, (cc + 1) * LANES)
                        at = pl.ds(q0 + cc * LANES, LANES)
                        p = jnp.exp(sTs[hh][:, cols] + b - lse_ref[hh:hh + 1, at])
                        if masked:
                            p = jnp.where(_causal_t(T, cc), p, 0.0)
                        ds = p * (dpTs[hh][:, cols] - delta[hh:hh + 1, at])
                        ps.append(p.astype(BF16))
                        dss.append(ds.astype(BF16))
                        dc_acc[hh] += ds
                        dr_scr[hh:hh + 1, at] += _over_keys(ds, jnp.add)
                    pT = jnp.concatenate(ps, axis=1)
                    dsT = jnp.concatenate(dss, axis=1)
                    dv_acc[hh] += jnp.dot(pT, dob, preferred_element_type=F32)
                    dk_acc[hh] += jnp.dot(dsT, qb, preferred_element_type=F32)
                    dqT[qi, head, :] += jnp.dot(kT[kj, head, :], dsT, preferred_element_type=F32)

            def inner(qi, _):
                tile(qi, False)
                return 0

            tile(kj, True)
            lax.fori_loop(kj + 1, nq, inner, 0)
            dk_ref[ks, :] = jnp.where(first, dk_acc[0], dk_acc[1])
            dv_ref[ks, :] = jnp.where(first, dv_acc[0], dv_acc[1])
            for hh in range(2):
                dck_ref[hh:hh + 1, ks] = -jnp.sum(dc_acc[hh].T, axis=0, keepdims=True)
            return 0

        lax.fori_loop(0, nq, kv_step, 0)

        def epilogue(i, _):
            rows = pl.ds(pl.multiple_of(i * T, T), T)
            dq_ref[rows, :] = (dqT[i].T * scale).astype(dq_ref.dtype)
            return 0

        lax.fori_loop(0, nq, epilogue, 0)
        drq_ref[...] = dr_scr[...]

    blk = lambda off: pl.BlockSpec((S, LANES), lambda p: (0, off + p))
    row_spec = pl.BlockSpec((None, 2, S), lambda p: (p, 0, 0))
    return pl.pallas_call(
        body, name=name, grid=(HP,),
        in_specs=[blk(0), blk(0), blk(HP), blk(0), blk(0), blk(0), row_spec],
        out_specs=[blk(0), blk(0), blk(0), row_spec, row_spec],
        out_shape=[_hbm_out((S, D), BF16), _hbm_out((S, D), F32),
                   _hbm_out((S, D), F32), _hbm_out((HP, 2, S), F32),
                   _hbm_out((HP, 2, S), F32)],
        scratch_shapes=[pltpu.VMEM((2, S, LANES), F32), pltpu.VMEM((nq, LANES, T), BF16),
                        pltpu.VMEM((nq, LANES, T), F32), pltpu.VMEM((2, S), F32), pltpu.VMEM((2, S), F32)]
        + [pltpu.VMEM((2, T, LANES), F32)] * 3,
        compiler_params=_params(("parallel",)),
    )(q, kv, kv, cfull, of, do, lse3)


def _logsig_fwd(name, f):
    S, C = f.shape

    def body(f_ref, o_ref):
        o_ref[...] = -_softplus(-f_ref[...])

    spec = pl.BlockSpec((S, C), lambda i: (0, 0))
    return pl.pallas_call(body, name=name, grid=(1,), in_specs=[spec], out_specs=spec,
                          out_shape=_hbm_out((S, C), F32),
                          compiler_params=_params(("arbitrary",)))(f)


def _logsig_bwd(name, dls, f):
    S, C = f.shape

    def body(d_ref, f_ref, o_ref, s_ref):
        df = d_ref[...] * _sigmoid(-f_ref[...])
        o_ref[...] = df.astype(o_ref.dtype)
        s_ref[...] = jnp.sum(df, axis=0, keepdims=True)

    spec = pl.BlockSpec((S, C), lambda i: (0, 0))
    return pl.pallas_call(body, name=name, grid=(1,), in_specs=[spec, spec],
                          out_specs=[spec, pl.BlockSpec((1, C), lambda i: (0, 0))],
                          out_shape=[_hbm_out((S, C), BF16), _hbm_out((1, C), F32)],
                          compiler_params=_params(("arbitrary",)))(dls, f)


def _add_cast(name, parts, out_dtype, tr=256):
    S, C = parts[0].shape
    tr = _tile(S, tr)
    n = len(parts)

    def body(*refs):
        acc = refs[0][...].astype(F32)
        for r in refs[1:n]:
            acc = acc + r[...].astype(F32)
        refs[n][...] = acc.astype(out_dtype)

    spec = pl.BlockSpec((tr, C), lambda i: (i, 0))
    return pl.pallas_call(body, name=name, grid=(S // tr,), in_specs=[spec] * n, out_specs=spec,
                          out_shape=_hbm_out((S, C), out_dtype),
                          compiler_params=_params(("parallel",)))(*parts)


def _local_step(x, target, gains, layer_weights, layer_prefetch, layer_grads):
    S, D = x.shape
    HP = D // LANES
    scale = HEAD_DIM ** -0.5
    tm = _tile(S, 512)
    tx = _tile(S, 256)
    td = _tile(D, 512)
    saved = []
    h = x
    l = 0
    kv = cfull = f_pre = hn_kv = h_kv = None
    while True:
        W = layer_weights(l, "mix", h)
        if W is None:
            break
        recurrent = "w_rec_in" in W
        if l == 0:
            xn = _rmsnorm_fwd("mix_norm_0", h, gains["mix"][0])
        if recurrent:
            CH = W["w_rec_in"].shape[-1]
            C = 2 * CH
            proj = _mm(f"rec_in_{l}", "nn", xn, W["w_rec_in"], grid=(S // tm, N_CHIPS),
                       a_spec=pl.BlockSpec((tm, D), lambda i, j: (i, 0)),
                       b_spec=pl.BlockSpec((None, D, CH), lambda i, j: (j, 0, 0)),
                       out_shape=(S, 2 * C), out_dtype=F32,
                       out_spec=pl.BlockSpec((tm, CH), lambda i, j: (i, j)))
            layer_prefetch(l, "mix2", proj)
            rc, rcb = _conv_fwd(f"conv_{l}", proj, W["conv_w"], W["conv_b"])
            W = {**W, **layer_weights(l, "mix2", rcb)}
            gip, grp = _gates_fwd(f"gates_{l}", rcb, W["w_gates"], W["b_gates"])
            hrec, m = _lru_fwd(f"lru_{l}", proj, rc, gip, grp, W["lru_param"])
            layer_prefetch(l, "ffn", m)
            h_mid, hn = _mm_nn(f"rec_out_{l}", m, W["w_rec_out"], out_dtype=F32, res=h, tn=D, norm_gain=gains["ffn"][l])
            mix_saved = (xn, proj, rc, rcb, gip, grp, hrec, m)
        else:
            if "w_kv" in W:
                h_kv = h
                hn_kv = _rmsnorm_fwd("kv_norm", h, W["norm_kv"])
                kv = _mm_nn("kv_proj", hn_kv, W["w_kv"], out_dtype=BF16, tm=1024, tn=1024)
                f_pre = _mm_nn("f_proj", hn_kv, W["w_f"], out_dtype=F32, bias=W["b_f"])
                c = _cumsum_rows("c_cumsum", _logsig_fwd("logsig", f_pre), False)
                cfull = jnp.repeat(-c[:, :2 * HP], HEAD_DIM, axis=1)
            q = _mm_nn(f"q_proj_{l}", xn, W["w_q"], out_dtype=BF16, scale=scale, tm=1024, tn=1024)
            layer_prefetch(l, "mix2", q)
            o, of, lse = _attn_fwd(f"attn_fwd_{l}", q, kv, cfull)
            W = {**W, **layer_weights(l, "mix2", o)}
            layer_prefetch(l, "ffn", o)
            h_mid, hn = _mm_nn(f"o_proj_{l}", o, W["w_o"], out_dtype=F32, res=h, tn=D, norm_gain=gains["ffn"][l])
            mix_saved = (xn, q, o, of, lse)
        W = {**W, **layer_weights(l, "ffn", h_mid)}
        z3, act = _swiglu_fwd(f"ffn_in_{l}", hn, W["w_ffn_in"])
        layer_prefetch(l + 1, "mix", act)
        saved.append((W, h, h_mid, mix_saved, (hn, z3, act)))
        l += 1
        if l < len(gains["mix"]):
            h, xn = _mm_nn(f"ffn_out_{l - 1}", act, W["w_ffn_out"], out_dtype=F32, res=h_mid, tn=D,
                           norm_gain=gains["mix"][l])
        else:
            h = _mm_nn(f"ffn_out_{l - 1}", act, W["w_ffn_out"], out_dtype=F32, res=h_mid, tn=D)

    dh, dhb, dg_final, loss_row = _loss_head("loss_head", h, target, gains["final"])

    dk_parts, dv_parts, dc_parts = [], [], []
    token = None
    for l in reversed(range(len(saved))):
        W, h_in, h_mid, mix_saved, (hn, z3, act) = saved[l]
        recurrent = "w_rec_in" in W
        FH = W["w_ffn_in"].shape[-1]
        G = {}
        norm_ffn = gains["ffn"][l]
        if token is not None:
            norm_ffn = norm_ffn + jnp.minimum(token[:1, :1], 0.0)
        G["w_ffn_out"] = _mm_tn(f"d_ffn_out_{l}", act, dhb, out_dtype=BF16, tn=D)
        dz3 = _swiglu_bwd(f"d_act_{l}", dhb, W["w_ffn_out"], z3)
        G["w_ffn_in"] = _mm(
            f"d_ffn_in_{l}", "tn", hn, dz3, grid=(D // td, N_CHIPS),
            a_spec=pl.BlockSpec((S, td), lambda i, j: (0, i)),
            b_spec=pl.BlockSpec((None, S, FH), lambda i, j: (j // 2, 0, j % 2)),
            out_shape=(N_CHIPS, D, FH), out_dtype=BF16,
            out_spec=pl.BlockSpec((None, td, FH), lambda i, j: (j, i, 0)))
        ffn_token = layer_grads(l, "ffn", G)
        G = {}
        if ffn_token is not None:
            norm_ffn = norm_ffn + jnp.minimum(ffn_token[:1, :1], 0.0)
        dh, dhb, dgp = _mm(f"d_ffn_hn_{l}", "nt", dz3, W["w_ffn_in"], grid=(S // tx, 1),
                           a_spec=[pl.BlockSpec((None, tx, FH), functools.partial(lambda i, j, k: (k // 2, i, k % 2), k=k))
                                   for k in range(N_CHIPS)],
                           b_spec=[pl.BlockSpec((None, D, FH), functools.partial(lambda i, j, k: (k, 0, 0), k=k))
                                   for k in range(N_CHIPS)],
                           out_shape=(S, D), out_dtype=F32, out_spec=pl.BlockSpec((tx, D), lambda i, j: (i, 0)),
                           norm_bwd=(h_mid, norm_ffn, dh))
        G["norm_ffn"] = jnp.sum(dgp, axis=0)
        if recurrent:
            CH = W["w_rec_in"].shape[-1]
            C = 2 * CH
            xn, proj, rc, rcb, gip, grp, hrec, m = mix_saved
            G["w_rec_out"] = _mm_tn(f"d_rec_out_{l}", m, dhb, out_dtype=BF16, tn=D)
            dm = _mm_nt(f"d_m_{l}", dhb, W["w_rec_out"], out_dtype=F32, tn=C)
            dgb, dgi, dgr, drc1, G["b_gi"], G["b_gr"], G["lru_param"] = _lru_bwd(
                f"d_lru_{l}", dm, proj, hrec, rc, gip, grp, W["lru_param"])
            drc, G["w_gates"] = _gates_bwd(f"d_gates_{l}", dgi, dgr, rcb, W["w_gates"], drc1)
            mix_token = layer_grads(l, "mix2", {n: G[n] for n in ("w_rec_out", "w_gates")})
            drec, G["conv_w"], G["conv_b"] = _conv_bwd(f"d_conv_{l}", drc, proj, W["conv_w"])
            dproj = jnp.concatenate([dgb, drec], axis=1)
            norm_mix = gains["mix"][l] if mix_token is None else gains["mix"][l] + jnp.minimum(mix_token[:1, :1], 0.0)
            G["w_rec_in"] = _mm(
                f"d_rec_in_{l}", "tn", xn, dproj, grid=(1, N_CHIPS),
                a_spec=pl.BlockSpec((S, D), lambda i, j: (0, 0)),
                b_spec=pl.BlockSpec((S, CH), lambda i, j: (0, j)),
                out_shape=(N_CHIPS, D, CH), out_dtype=BF16,
                out_spec=pl.BlockSpec((None, D, CH), lambda i, j: (j, 0, 0)))
            dh, dhb, dgp = _mm(f"d_rec_xn_{l}", "nt", dproj, W["w_rec_in"], grid=(S // tx, 1),
                               a_spec=[pl.BlockSpec((tx, CH), functools.partial(lambda i, j, k: (i, k), k=k))
                                       for k in range(N_CHIPS)],
                               b_spec=[pl.BlockSpec((None, D, CH), functools.partial(lambda i, j, k: (k, 0, 0), k=k))
                                       for k in range(N_CHIPS)],
                               out_shape=(S, D), out_dtype=F32, out_spec=pl.BlockSpec((tx, D), lambda i, j: (i, 0)),
                               norm_bwd=(h_in, norm_mix, dh))
        else:
            xn, q, o, of, lse = mix_saved
            G["w_o"] = _mm_tn(f"d_o_proj_{l}", o, dhb, out_dtype=BF16, tn=D)
            do = _mm_nt(f"d_o_{l}", dhb, W["w_o"], out_dtype=BF16, tn=D)
            mix_token = layer_grads(l, "mix2", {"w_o": G["w_o"]})
            dq, dk, dv, dck, drq = _attn_bwd(f"attn_bwd_{l}", q, kv, cfull, of, do, lse)
            dk_parts.append(dk)
            dv_parts.append(dv)
            dc_parts.append((dck + drq).reshape(2 * HP, S).T)
            G["w_q"] = _mm_tn(f"d_q_proj_{l}", xn, dq, out_dtype=BF16, tn=D)
            norm_mix = gains["mix"][l] if mix_token is None else gains["mix"][l] + jnp.minimum(mix_token[:1, :1], 0.0)
            dh, dhb, dgp = _mm_nt(f"d_q_xn_{l}", dq, W["w_q"], out_dtype=F32, tn=D, norm_bwd=(h_in, norm_mix, dh))
        G["norm_mix"] = jnp.sum(dgp, axis=0)
        if "w_kv" in W:
            dkb = _add_cast("dk_sum", dk_parts, BF16)
            dvb = _add_cast("dv_sum", dv_parts, BF16)
            dkv = jnp.concatenate([dkb, dvb], axis=1)
            dc = sum(dc_parts[1:], dc_parts[0])
            dc_pad = jnp.pad(dc, ((0, 0), (0, LANES - 2 * HP)))
            dls = _cumsum_rows("dc_cumsum", dc_pad, True)
            dfb, G["b_f"] = _logsig_bwd("d_logsig", dls, f_pre)
            G["w_kv"] = _mm_tn("d_kv_proj", hn_kv, dkv, out_dtype=BF16)
            G["w_f"] = _mm_tn("d_f_proj", hn_kv, dfb, out_dtype=F32)
            dhn_f = _mm_nt("d_f_hn", dfb, W["w_f"], out_dtype=F32, tn=D)
            dh, dhb, dgp = _mm_nt("d_kv_hn", dkv, W["w_kv"], out_dtype=F32, tn=D, res=dhn_f,
                                  norm_bwd=(h_kv, W["norm_kv"], dh))
            G["norm_kv"] = jnp.sum(dgp, axis=0)
        token = layer_grads(l, "mix", G)
    return loss_row, dh, dg_final


_ANY = pl.BlockSpec(memory_space=pl.ANY)


def _position():
    return lax.axis_index("x"), lax.axis_index("y"), lax.axis_index("c")


def _chip_peers(x, y):
    return [(1 - x, y), (x, 1 - y), (1 - x, 1 - y)]


def _half_rows(c, n):
    h = n // 2
    assert h % 16 == 0
    return pl.ds(pl.multiple_of(c * h, 16), h)


def _place_own(name, shard, layer, me):
    _, R, C = shard.shape
    tr = _row_tile(R, C, 2 * shard.dtype.itemsize, target=8 << 20)

    def body(me_ref, x_ref, o_ref):
        o_ref[...] = x_ref[...]

    return pl.pallas_call(
        body, name=name,
        grid_spec=pltpu.PrefetchScalarGridSpec(
            num_scalar_prefetch=1, grid=(R // tr,),
            in_specs=[pl.BlockSpec((None, tr, C), lambda i, me_ref: (layer, i, 0))],
            out_specs=pl.BlockSpec((None, tr, C), lambda i, me_ref: (me_ref[0], i, 0))),
        out_shape=_hbm_out((N_CHIPS, R, C), shard.dtype),
        compiler_params=_params(("parallel",)),
    )(me, shard)


def _gather_smalls(name, smalls):
    ns = len(smalls)

    def body(*refs):
        ins, outs = refs[:ns], refs[ns:2 * ns]
        send_sems, recv_sems, local_sems = refs[2 * ns:]
        x, y, c = _position()
        me = 2 * x + y
        peers = _chip_peers(x, y)

        def remote(t, k, chip):
            px, py = peers[k]
            return pltpu.make_async_remote_copy(
                src_ref=ins[t], dst_ref=outs[t].at[chip], send_sem=send_sems.at[3 * t + k],
                recv_sem=recv_sems.at[3 * t + k], device_id=(px, py, c), device_id_type=MESH)

        local = [pltpu.make_async_copy(ins[t], outs[t].at[me], local_sems.at[t]) for t in range(ns)]
        for t in range(ns):
            local[t].start()
            for k in range(3):
                remote(t, k, me).start()
        for t in range(ns):
            for k in range(3):
                px, py = peers[k]
                remote(t, k, 2 * px + py).wait_recv()
        for t in range(ns):
            for k in range(3):
                remote(t, k, me).wait_send()
            local[t].wait()

    return pl.pallas_call(
        body, name=name, in_specs=[_ANY] * ns, out_specs=[_ANY] * ns,
        out_shape=[_hbm_out((N_CHIPS,) + s.shape, s.dtype) for s in smalls],
        scratch_shapes=[pltpu.SemaphoreType.DMA((3 * ns,)), pltpu.SemaphoreType.DMA((3 * ns,)),
                        pltpu.SemaphoreType.DMA((ns,))],
    )(*smalls)


_SEM = pl.BlockSpec(memory_space=pltpu.SEMAPHORE)
_SPLIT = pltpu.CompilerParams(has_side_effects=pltpu.SideEffectType.DATAFLOW_SIDE_EFFECTING)


def _weight_copy(shards, buf, items, sems, i, k, chip_of_dst, peers, c):
    w, l = items[i]
    px, py = peers[k]
    half = _half_rows(c, shards[w].shape[1])
    return pltpu.make_async_remote_copy(
        src_ref=shards[w].at[l, half], dst_ref=buf.at[chip_of_dst, half],
        send_sem=sems[0].at[3 * i + k], recv_sem=sems[1].at[3 * i + k],
        device_id=(px, py, c), device_id_type=MESH)


def _gather_start(name, shards, bufs, items, after):
    nw, n = len(shards), len(bufs)

    def body(*refs):
        ins, outs, sems = refs[:nw], refs[nw + n + 1:nw + 2 * n + 1], refs[nw + 2 * n + 1:]
        x, y, c = _position()
        peers = _chip_peers(x, y)
        for i in range(n):
            for k in range(3):
                _weight_copy(ins, outs[i], items, sems, i, k, 2 * x + y, peers, c).start()

    res = pl.pallas_call(
        body, name=name, in_specs=[_ANY] * (nw + n + 1), out_specs=[_ANY] * n + [_SEM, _SEM],
        out_shape=[_hbm_out(b.shape, b.dtype) for b in bufs]
        + [pltpu.SemaphoreType.DMA((3 * n,)), pltpu.SemaphoreType.DMA((3 * n,))],
        input_output_aliases={nw + i: i for i in range(n)}, compiler_params=_SPLIT,
    )(*shards, *bufs, after)
    return res[:n], res[n:]


def _gather_wait(name, shards, bufs, items, ids, sems, after):
    nw, m = len(shards), len(ids)

    def body(*refs):
        ins, bs = refs[:nw], refs[nw:nw + m]
        sem_refs = refs[nw + m:nw + m + 2]
        x, y, c = _position()
        peers = _chip_peers(x, y)
        for j, i in enumerate(ids):
            for k in range(3):
                px, py = peers[k]
                _weight_copy(ins, bs[j], items, sem_refs, i, k, 2 * px + py, peers, c).wait_recv()
        for j, i in enumerate(ids):
            for k in range(3):
                _weight_copy(ins, bs[j], items, sem_refs, i, k, 2 * x + y, peers, c).wait_send()

    res = pl.pallas_call(
        body, name=name, in_specs=[_ANY] * (nw + m) + [_SEM, _SEM, _ANY], out_specs=[_ANY] * m,
        out_shape=[_hbm_out(bufs[i].shape, bufs[i].dtype) for i in ids],
        input_output_aliases={nw + j: j for j in range(m)}, compiler_params=_SPLIT,
    )(*shards, *[bufs[i] for i in ids], *sems, after)
    return list(res)


def _forward_copy(src, dst, sems, i, k, core):
    x, y, c = _position()
    px, py = _chip_peers(x, y)[k]
    half = _half_rows(core, src.shape[1])
    return pltpu.make_async_remote_copy(
        src_ref=src.at[2 * px + py, half], dst_ref=dst.at[2 * px + py, half],
        send_sem=sems[0].at[3 * i + k], recv_sem=sems[1].at[3 * i + k],
        device_id=(x, y, 1 - c), device_id_type=MESH)


def _forward_start(name, bufs):
    n = len(bufs)

    def body(*refs):
        ins, outs, sems = refs[:n], refs[n:2 * n], refs[2 * n:]
        c = lax.axis_index("c")
        for i in range(n):
            for k in range(3):
                _forward_copy(ins[i], outs[i], sems, i, k, c).start()

    res = pl.pallas_call(
        body, name=name, in_specs=[_ANY] * n, out_specs=[_ANY] * n + [_SEM, _SEM],
        out_shape=[_hbm_out(g.shape, g.dtype) for g in bufs]
        + [pltpu.SemaphoreType.DMA((3 * n,)), pltpu.SemaphoreType.DMA((3 * n,))],
        input_output_aliases={i: i for i in range(n)}, compiler_params=_SPLIT,
    )(*bufs)
    return list(res[:n]), res[n:]


def _forward_wait(name, bufs, sems, after):
    n = len(bufs)

    def body(*refs):
        bs, sem_refs = refs[:n], refs[n:n + 2]
        c = lax.axis_index("c")
        for i in range(n):
            for k in range(3):
                _forward_copy(bs[i], bs[i], sem_refs, i, k, 1 - c).wait_recv()
        for i in range(n):
            for k in range(3):
                _forward_copy(bs[i], bs[i], sem_refs, i, k, c).wait_send()

    return list(pl.pallas_call(
        body, name=name, in_specs=[_ANY] * n + [_SEM, _SEM, _ANY], out_specs=[_ANY] * n,
        out_shape=[_hbm_out(g.shape, g.dtype) for g in bufs],
        input_output_aliases={i: i for i in range(n)}, compiler_params=_SPLIT,
    )(*bufs, *sems, after))


def _reduce_copy(grads, others, sems, i):
    x, y, c = _position()
    return pltpu.make_async_remote_copy(
        src_ref=grads[i].at[:, _half_rows(1 - c, grads[i].shape[1])], dst_ref=others[i],
        send_sem=sems[0].at[i], recv_sem=sems[1].at[i], device_id=(x, y, 1 - c), device_id_type=MESH)


def _reduce_start(name, grads, after):
    n = len(grads)

    def body(*refs):
        ins, outs, sems, token = refs[:n], refs[n + 1:2 * n + 1], refs[2 * n + 1:2 * n + 3], refs[2 * n + 3]
        for i in range(n):
            _reduce_copy(ins, outs, sems, i).start()
        token[...] = jnp.zeros_like(token)

    res = pl.pallas_call(
        body, name=name, in_specs=[_ANY] * (n + 1),
        out_specs=[_ANY] * n + [_SEM, _SEM, pl.BlockSpec(memory_space=pltpu.VMEM)],
        out_shape=[_hbm_out((N_CHIPS, g.shape[1] // 2, g.shape[2]), g.dtype) for g in grads]
        + [pltpu.SemaphoreType.DMA((n,)), pltpu.SemaphoreType.DMA((n,)), jax.ShapeDtypeStruct((SUBLANES, LANES), F32)],
        compiler_params=_SPLIT,
    )(*grads, after)
    return list(res[:n]), res[n:n + 2], res[n + 2]


def _reduce_wait(name, grads, others, sems, after):
    n = len(grads)

    def body(*refs):
        ins, os_, sem_refs = refs[:n], refs[n:2 * n], refs[2 * n:2 * n + 2]
        for i in range(n):
            _reduce_copy(ins, os_, sem_refs, i).wait_recv()
        for i in range(n):
            _reduce_copy(ins, os_, sem_refs, i).wait_send()

    return list(pl.pallas_call(
        body, name=name, in_specs=[_ANY] * (2 * n) + [_SEM, _SEM, _ANY], out_specs=[_ANY] * n,
        out_shape=[_hbm_out(o.shape, o.dtype) for o in others],
        input_output_aliases={n + i: i for i in range(n)}, compiler_params=_SPLIT,
    )(*grads, *others, *sems, after))


def _sum_cores(name, g, other, core):
    _, R, C = g.shape
    H = R // 2
    tr = _row_tile(H, C, 3 * 2, target=12 << 20)
    nb = H // tr

    def body(c_ref, g_ref, o_ref, out_ref):
        out_ref[...] = (g_ref[...].astype(F32) + o_ref[...].astype(F32)).astype(out_ref.dtype)

    return pl.pallas_call(
        body, name=name,
        grid_spec=pltpu.PrefetchScalarGridSpec(
            num_scalar_prefetch=1, grid=(N_CHIPS, nb),
            in_specs=[pl.BlockSpec((None, tr, C), lambda j, i, c_ref: (j, c_ref[0] * nb + i, 0)),
                      pl.BlockSpec((None, tr, C), lambda j, i, c_ref: (j, i, 0))],
            out_specs=pl.BlockSpec((None, tr, C), lambda j, i, c_ref: (j, i, 0))),
        out_shape=_hbm_out((N_CHIPS, H, C), BF16),
        compiler_params=_params(("parallel", "parallel")),
    )(core, g, other)


def _sum_chips(name, received, own, full, layer, me_core):
    _, H, C = received.shape
    tr = _row_tile(H, C, 3 * 2 + 2 + 4, target=12 << 20)
    nb = H // tr

    def body(s_ref, r_ref, own_ref, full_ref, out_ref):
        acc = r_ref[0].astype(F32)
        for k in (1, 2):
            acc = acc + r_ref[k].astype(F32)
        out_ref[...] = acc + own_ref[...].astype(F32)

    return pl.pallas_call(
        body, name=name,
        grid_spec=pltpu.PrefetchScalarGridSpec(
            num_scalar_prefetch=1, grid=(nb,),
            in_specs=[pl.BlockSpec((3, tr, C), lambda i, s_ref: (0, i, 0)),
                      pl.BlockSpec((None, tr, C), lambda i, s_ref: (s_ref[0], i, 0)),
                      _ANY],
            out_specs=pl.BlockSpec((None, tr, C), lambda i, s_ref: (layer, s_ref[1] * nb + i, 0))),
        out_shape=_hbm_out(full.shape, full.dtype),
        input_output_aliases={3: 0},
        compiler_params=_params(("parallel",)),
    )(me_core, received, own, full)


def _part_copy(parts, recv, sems, i, k, peers, c):
    px, py = peers[k]
    return pltpu.make_async_remote_copy(
        src_ref=parts[i].at[2 * px + py], dst_ref=recv[i].at[k],
        send_sem=sems[0].at[3 * i + k], recv_sem=sems[1].at[3 * i + k],
        device_id=(px, py, c), device_id_type=MESH)


def _scatter_start(name, parts):
    n = len(parts)

    def body(*refs):
        ins, outs, sems, token = refs[:n], refs[n:2 * n], refs[2 * n:2 * n + 2], refs[2 * n + 2]
        x, y, c = _position()
        peers = _chip_peers(x, y)
        for i in range(n):
            for k in range(3):
                _part_copy(ins, outs, sems, i, k, peers, c).start()
        token[...] = jnp.zeros_like(token)

    res = pl.pallas_call(
        body, name=name, in_specs=[_ANY] * n,
        out_specs=[_ANY] * n + [_SEM, _SEM, pl.BlockSpec(memory_space=pltpu.VMEM)],
        out_shape=[_hbm_out((3,) + p.shape[1:], p.dtype) for p in parts]
        + [pltpu.SemaphoreType.DMA((3 * n,)), pltpu.SemaphoreType.DMA((3 * n,)),
           jax.ShapeDtypeStruct((SUBLANES, LANES), F32)],
        compiler_params=_SPLIT,
    )(*parts)
    return list(res[:n]), res[n:n + 2], res[n + 2]


def _scatter_wait(name, parts, recv, sems):
    n = len(parts)

    def body(*refs):
        ins, rs, sem_refs = refs[:n], refs[n:2 * n], refs[2 * n:2 * n + 2]
        x, y, c = _position()
        peers = _chip_peers(x, y)
        for i in range(n):
            for k in range(3):
                _part_copy(ins, rs, sem_refs, i, k, peers, c).wait_recv()
        for i in range(n):
            for k in range(3):
                _part_copy(ins, rs, sem_refs, i, k, peers, c).wait_send()

    return list(pl.pallas_call(
        body, name=name, in_specs=[_ANY] * (2 * n) + [_SEM, _SEM], out_specs=[_ANY] * n,
        out_shape=[_hbm_out(r.shape, r.dtype) for r in recv],
        input_output_aliases={n + i: i for i in range(n)}, compiler_params=_SPLIT,
    )(*parts, *recv, *sems))


def _share_d2d(name, full):
    n = len(full)

    def body(*refs):
        ins, outs = refs[:n], refs[n:2 * n]
        send_sems, recv_sems = refs[2 * n:]
        x, y, c = _position()

        def remote(w, core):
            half = _half_rows(core, ins[w].shape[1])
            return pltpu.make_async_remote_copy(
                src_ref=ins[w].at[:, half], dst_ref=outs[w].at[:, half],
                send_sem=send_sems.at[w], recv_sem=recv_sems.at[w],
                device_id=(x, y, 1 - c), device_id_type=MESH)

        for w in range(n):
            remote(w, c).start()
        for w in range(n):
            remote(w, 1 - c).wait_recv()
        for w in range(n):
            remote(w, c).wait_send()

    return pl.pallas_call(
        body, name=name, in_specs=[_ANY] * n, out_specs=[_ANY] * n,
        out_shape=[_hbm_out(f.shape, f.dtype) for f in full],
        input_output_aliases={w: w for w in range(n)},
        scratch_shapes=[pltpu.SemaphoreType.DMA((n,)), pltpu.SemaphoreType.DMA((n,))],
    )(*full)


def _gather_all(name, a):
    def body(a_ref, o_ref, send_sems, recv_sems, local_sem):
        x, y, c = _position()
        me = 4 * x + 2 * y + c

        def peer(k):
            return (x ^ ((k >> 2) & 1), y ^ ((k >> 1) & 1), c ^ (k & 1))

        def remote(k, slot):
            return pltpu.make_async_remote_copy(
                src_ref=a_ref, dst_ref=o_ref.at[slot], send_sem=send_sems.at[k - 1], recv_sem=recv_sems.at[k - 1],
                device_id=peer(k), device_id_type=MESH)

        local = pltpu.make_async_copy(a_ref, o_ref.at[me], local_sem)
        local.start()
        for k in range(1, N_DEV):
            remote(k, me).start()
        for k in range(1, N_DEV):
            px, py, pc = peer(k)
            remote(k, 4 * px + 2 * py + pc).wait_recv()
        for k in range(1, N_DEV):
            remote(k, me).wait_send()
        local.wait()

    return pl.pallas_call(
        body, name=name, in_specs=[_ANY], out_specs=_ANY,
        out_shape=_hbm_out((N_DEV,) + a.shape, a.dtype),
        scratch_shapes=[pltpu.SemaphoreType.DMA((N_DEV - 1,)), pltpu.SemaphoreType.DMA((N_DEV - 1,)),
                        pltpu.SemaphoreType.DMA],
    )(a)


def _rows2d(a, lead=0):
    return a.reshape(a.shape[:lead] + (-1, a.shape[-1]))


def _row_tile(rows, cols, itemsize=4, target=1 << 20):
    want = max(SUBLANES, target // (cols * itemsize))
    t = min(rows, (want // 16) * 16)
    while t > 16 and rows % t:
        t -= 16
    return t if rows % t == 0 else rows


def _sum_slots(name, r, out_dtype=F32):
    ns = r.shape[0]
    r2 = _rows2d(r, 1)
    _, rows, cols = r2.shape
    tr = _row_tile(rows, cols)

    def body(r_ref, o_ref):
        acc = r_ref[0].astype(F32)
        for s in range(1, ns):
            acc = acc + r_ref[s].astype(F32)
        o_ref[...] = acc.astype(o_ref.dtype)

    out = pl.pallas_call(
        body, name=name, grid=(rows // tr,),
        in_specs=[pl.BlockSpec((ns, tr, cols), lambda i: (0, i, 0))],
        out_specs=pl.BlockSpec((tr, cols), lambda i: (i, 0)),
        out_shape=_hbm_out((rows, cols), out_dtype),
        compiler_params=_params(("parallel",)),
    )(r2)
    return out.reshape(r.shape[1:])


def _adamw(name, g_parts, w, m, v):
    shape = w.shape
    ng = len(g_parts)
    args = [_rows2d(a) for a in (*g_parts, w, m, v)]
    rows, cols = args[0].shape
    tr = _row_tile(rows, cols, (ng + 7) * 4, target=16 << 20)
    c1 = 1.0 - ADAM_B1 ** ADAM_STEP
    c2 = 1.0 - ADAM_B2 ** ADAM_STEP

    def body(*refs):
        g = refs[0][...]
        for r in refs[1:ng]:
            g = g + r[...]
        w_ref, m_ref, v_ref = refs[ng:ng + 3]
        g_out, d_out, m_out, v_out = refs[ng + 3:]
        mn = ADAM_B1 * m_ref[...] + (1.0 - ADAM_B1) * g
        vn = ADAM_B2 * v_ref[...] + (1.0 - ADAM_B2) * (g * g)
        m_hat = mn / c1
        v_hat = vn / c2
        g_out[...] = g
        d_out[...] = -ADAM_LR * (m_hat / (jnp.sqrt(v_hat) + ADAM_EPS) + ADAM_WD * w_ref[...])
        m_out[...] = mn
        v_out[...] = vn

    spec = pl.BlockSpec((tr, cols), lambda i: (i, 0))
    outs = pl.pallas_call(
        body, name=name, grid=(rows // tr,), in_specs=[spec] * (ng + 3), out_specs=[spec] * 4,
        out_shape=[_hbm_out((rows, cols), F32)] * 4,
        compiler_params=_params(("parallel",)),
    )(*args)
    return tuple(o.reshape(shape) for o in outs)


_WEIGHTS = ["norm_mix", "norm_ffn", "w_ffn_in", "w_ffn_out", "w_rec_in", "conv_w", "conv_b", "w_lru_gates",
            "b_lru_gates", "lru_param", "w_rec_out", "norm_kv", "w_kvf", "b_forget", "w_q", "w_o", "norm_final"]
_BIG = ["w_ffn_in", "w_ffn_out", "w_rec_in", "w_lru_gates", "w_rec_out", "w_kvf", "w_q", "w_o"]


def _stack3(a):
    return a[None] if a.ndim == 2 else a.reshape(a.shape[0], -1, a.shape[-1])


def _pad_lanes(a, n):
    return jnp.pad(a, ((0, 0),) * (a.ndim - 1) + ((0, n - a.shape[-1]),))


def kernel(x, norm_mix, norm_ffn, w_ffn_in, w_ffn_out, w_rec_in, conv_w, conv_b, w_lru_gates, b_lru_gates, lru_param, w_rec_out, norm_kv, w_kvf, b_forget, w_q, w_o, norm_final, loss_target, m_norm_mix, m_norm_ffn, m_w_ffn_in, m_w_ffn_out, m_w_rec_in, m_conv_w, m_conv_b, m_w_lru_gates, m_b_lru_gates, m_lru_param, m_w_rec_out, m_norm_kv, m_w_kvf, m_b_forget, m_w_q, m_w_o, m_norm_final, v_norm_mix, v_norm_ffn, v_w_ffn_in, v_w_ffn_out, v_w_rec_in, v_conv_w, v_conv_b, v_w_lru_gates, v_b_lru_gates, v_lru_param, v_w_rec_out, v_norm_kv, v_w_kvf, v_b_forget, v_w_q, v_w_o, v_norm_final):
    P = dict(norm_mix=norm_mix, norm_ffn=norm_ffn, w_ffn_in=w_ffn_in, w_ffn_out=w_ffn_out, w_rec_in=w_rec_in,
             conv_w=conv_w, conv_b=conv_b, w_lru_gates=w_lru_gates, b_lru_gates=b_lru_gates, lru_param=lru_param,
             w_rec_out=w_rec_out, norm_kv=norm_kv, w_kvf=w_kvf, b_forget=b_forget, w_q=w_q, w_o=w_o,
             norm_final=norm_final)
    M1 = dict(norm_mix=m_norm_mix, norm_ffn=m_norm_ffn, w_ffn_in=m_w_ffn_in, w_ffn_out=m_w_ffn_out,
              w_rec_in=m_w_rec_in, conv_w=m_conv_w, conv_b=m_conv_b, w_lru_gates=m_w_lru_gates,
              b_lru_gates=m_b_lru_gates, lru_param=m_lru_param, w_rec_out=m_w_rec_out, norm_kv=m_norm_kv,
              w_kvf=m_w_kvf, b_forget=m_b_forget, w_q=m_w_q, w_o=m_w_o, norm_final=m_norm_final)
    M2 = dict(norm_mix=v_norm_mix, norm_ffn=v_norm_ffn, w_ffn_in=v_w_ffn_in, w_ffn_out=v_w_ffn_out,
              w_rec_in=v_w_rec_in, conv_w=v_conv_w, conv_b=v_conv_b, w_lru_gates=v_w_lru_gates,
              b_lru_gates=v_b_lru_gates, lru_param=v_lru_param, w_rec_out=v_w_rec_out, norm_kv=v_norm_kv,
              w_kvf=v_w_kvf, b_forget=v_b_forget, w_q=v_w_q, w_o=v_w_o, norm_final=v_norm_final)

    _, S, D = x.shape
    L = norm_mix.shape[0]
    NA, NBLK, BW, GS = w_lru_gates.shape
    C = NBLK * BW
    CS = C // N_CHIPS
    H = b_forget.shape[0]
    assert C == D and H * HEAD_DIM == D and H <= LANES
    chip = 2 * lax.axis_index("x") + lax.axis_index("y")

    small_a = jnp.concatenate([conv_w, conv_b[:, None], lru_param[:, None]], axis=1)
    small_a, b_gates = _gather_smalls("gather_smalls", [small_a, b_lru_gates])
    small_a = small_a.transpose(1, 2, 0, 3).reshape(NA, 6, C)
    b_gates = b_gates.transpose(1, 2, 0, 3).reshape(NA, NBLK, 1, N_CHIPS * GS)
    shards = [_stack3(P[w]).astype(BF16) for w in _BIG]
    core = lax.axis_index("c")
    chip_id = jnp.reshape(chip, (1,)).astype(jnp.int32)
    core_id = jnp.reshape(core, (1,)).astype(jnp.int32)
    me_core = jnp.stack([chip, core]).astype(jnp.int32)

    parts_of_layer = ("mix", "mix2", "ffn")

    def part_items(l, part):
        if part == "ffn":
            names, at = ["w_ffn_in", "w_ffn_out"], l
        elif l < NA:
            names, at = (["w_rec_in"] if part == "mix" else ["w_lru_gates", "w_rec_out"]), l
        else:
            names, at = ((["w_kvf"] if l == NA else []) + ["w_q"] if part == "mix" else ["w_o"]), l - NA
        return [(_BIG.index(n), 0 if n == "w_kvf" else at) for n in names]

    def stage_of(l, part):
        return (l, part) if l == 0 or part == "ffn" else (l, "mixer")

    def stage_items(st):
        l, part = st
        return [it for p in (("mix", "mix2") if part == "mixer" else (part,)) for it in part_items(l, p)]

    stages = [(0, p) for p in parts_of_layer] + [(l, p) for l in range(1, L) for p in ("mixer", "ffn")]
    items = [it for st in stages for it in stage_items(st)]
    ids_of = {st: [items.index(it) for it in stage_items(st)] for st in stages}
    bufs = [_place_own(f"place_{_BIG[w]}_{li}", shards[w], li, chip_id) for w, li in items]
    bufs, gather_sems = _gather_start("gather_start", shards, bufs, items, small_a)

    forwarding, fetched = {}, {}

    def layer_prefetch(l, part, after):
        st = stage_of(l, part)
        if l < L and st not in forwarding:
            got = _gather_wait(f"gather_wait_{st[1]}_{l}", shards, bufs, items, ids_of[st], gather_sems, after)
            forwarding[st] = _forward_start(f"forward_start_{st[1]}_{l}", got)

    def layer_weights(l, part, after):
        if l >= L:
            return None
        st = stage_of(l, part)
        if st not in fetched:
            layer_prefetch(l, part, after)
            got, sems = forwarding[st]
            got = _forward_wait(f"forward_wait_{st[1]}_{l}", got, sems, after)
            fetched[st] = {_BIG[items[i][0]]: g for i, g in zip(ids_of[st], got)}
        B = fetched[st]
        if part == "ffn":
            return dict(w_ffn_in=B["w_ffn_in"], w_ffn_out=B["w_ffn_out"].reshape(-1, D))
        if l < NA and part == "mix":
            return dict(w_rec_in=B["w_rec_in"], conv_w=small_a[l, :4], conv_b=small_a[l, 4:5])
        if l < NA:
            return dict(w_gates=B["w_lru_gates"].reshape(N_CHIPS, NBLK, BW, GS).transpose(1, 2, 0, 3).reshape(
                NBLK, BW, N_CHIPS * GS), b_gates=b_gates[l], w_rec_out=B["w_rec_out"].reshape(C, D),
                lru_param=small_a[l, 5:6])
        if part == "mix2":
            return dict(w_o=B["w_o"].reshape(D, D))
        W = dict(w_q=B["w_q"].reshape(D, D))
        if l == NA:
            w_kvf_full = B["w_kvf"].transpose(1, 0, 2).reshape(D, -1)
            W.update(norm_kv=norm_kv[None], w_kv=w_kvf_full[:, :2 * D],
                     w_f=_pad_lanes(w_kvf_full[:, 2 * D:], LANES), b_f=_pad_lanes(b_forget[None], LANES))
        return W

    G_small = {l: {} for l in range(L)}
    stash = {st: {} for st in stages}
    pending = {}
    reducing = []

    def finish_reduce(after):
        st, its, grads, others, sems = reducing.pop()
        l, part = st
        others = _reduce_wait(f"reduce_wait_{part}_{l}", grads, others, sems, after)
        parts = [_sum_cores(f"sum_cores_{l}_{_BIG[w]}", g, o, core_id) for (w, _), g, o in zip(its, grads, others)]
        recv, sems, token = _scatter_start(f"scatter_start_{part}_{l}", parts)
        pending[st] = (parts, recv, sems)
        return token

    def layer_grads(l, part, G_part):
        G_small[l].update(G_part)
        st = stage_of(l, part)
        stash[st].update(G_part)
        if st[1] == "mixer" and part != "mix":
            return None
        G = stash[st]
        late = {"ffn": "w_ffn_in", "mix": "norm_mix"}.get(part) or ("w_gates" if l < NA else "w_o")
        after = finish_reduce(G_part[late]) if reducing else jnp.zeros((SUBLANES, LANES), F32)
        by_name = dict(
            w_ffn_in=lambda: G["w_ffn_in"], w_ffn_out=lambda: G["w_ffn_out"].reshape(N_CHIPS, -1, D),
            w_rec_in=lambda: G["w_rec_in"],
            w_lru_gates=lambda: G["w_gates"].reshape(NBLK, BW, N_CHIPS, GS).transpose(2, 0, 1, 3).reshape(
                N_CHIPS, NBLK * BW, GS),
            w_rec_out=lambda: G["w_rec_out"].reshape(N_CHIPS, -1, D),
            w_kvf=lambda: jnp.concatenate([G["w_kv"].astype(F32), G["w_f"][:, :H]], axis=1).reshape(
                D, N_CHIPS, -1).transpose(1, 0, 2).astype(BF16),
            w_q=lambda: G["w_q"].reshape(N_CHIPS, -1, D), w_o=lambda: G["w_o"].reshape(N_CHIPS, -1, D))
        its = stage_items(st)
        grads = [by_name[_BIG[w]]() for w, _ in its]
        others, sems, token = _reduce_start(f"reduce_start_{st[1]}_{l}", grads, after)
        reducing.append((st, its, grads, others, sems))
        return finish_reduce(token) if l == 0 else token

    gains = dict(mix=[norm_mix[l][None] for l in range(L)], ffn=[norm_ffn[l][None] for l in range(L)],
                 final=norm_final[None])
    loss_row, grad_x, dg_final = _local_step(x.reshape(S, D), loss_target.reshape(S, D), gains,
                                             layer_weights, layer_prefetch, layer_grads)

    rows = [*[G_small[l]["norm_mix"] for l in range(L)], *[G_small[l]["norm_ffn"] for l in range(L)],
            G_small[NA]["norm_kv"], dg_final, _pad_lanes(G_small[NA]["b_f"], D), _pad_lanes(loss_row, D)]
    for a in range(NA):
        rows += [G_small[a][n] for n in ("conv_w", "conv_b", "b_gi", "b_gr", "lru_param")]
    packed = jnp.concatenate(rows, axis=0)
    tot = _sum_slots("sum_small", _gather_all("gather_small", packed))
    loss = tot[2 * L + 3, 0]
    g_rep = jnp.concatenate([tot[:2 * L + 2], tot[2 * L + 2:2 * L + 3]], axis=0)
    base = 2 * L + 4
    g_sh = []
    for a in range(NA):
        blk = lax.dynamic_slice_in_dim(tot[base + 8 * a:base + 8 * a + 8], chip * CS, CS, axis=1)
        gi = tot[base + 8 * a + 5].reshape(NBLK, BW)
        gr = tot[base + 8 * a + 6].reshape(NBLK, BW)
        bl = lax.dynamic_slice_in_dim(jnp.concatenate([gi, gr], axis=1), chip * GS, GS, axis=1)
        g_sh += [blk[:5], bl.reshape(-1, CS), blk[7:8]]
    g_sh = jnp.concatenate(g_sh, axis=0)
    nrow = g_sh.shape[0] // NA

    def pack_rep(T):
        return jnp.concatenate([T["norm_mix"], T["norm_ffn"], T["norm_kv"][None], T["norm_final"][None],
                                _pad_lanes(T["b_forget"][None], D)], axis=0)

    def pack_sh(T):
        return jnp.concatenate([jnp.concatenate([T["conv_w"][a], T["conv_b"][a][None],
                                                 T["b_lru_gates"][a].reshape(-1, CS), T["lru_param"][a][None]], axis=0)
                                for a in range(NA)], axis=0)

    rep = _adamw("adamw_replicated", [g_rep], pack_rep(P), pack_rep(M1), pack_rep(M2))
    shd = _adamw("adamw_small_sharded", [g_sh], pack_sh(P), pack_sh(M1), pack_sh(M2))

    def unpack_rep(t):
        return dict(norm_mix=t[:L], norm_ffn=t[L:2 * L], norm_kv=t[2 * L], norm_final=t[2 * L + 1],
                    b_forget=t[2 * L + 2, :H])

    def unpack_sh(t):
        t = t.reshape(NA, nrow, CS)
        return dict(conv_w=t[:, :4], conv_b=t[:, 4], b_lru_gates=t[:, 5:nrow - 1].reshape(NA, NBLK, GS),
                    lru_param=t[:, nrow - 1])

    full = [lax.empty(sh.shape, F32) for sh in shards]
    for st in reversed(stages):
        l, part = st
        parts, recv, sems = pending[st]
        recv = _scatter_wait(f"scatter_wait_{part}_{l}", parts, recv, sems)
        for (w, li), own, r in zip(stage_items(st), parts, recv):
            full[w] = _sum_chips(f"sum_chips_{l}_{_BIG[w]}", r, own, full[w], li, me_core)
    full = _share_d2d("share_d2d", full)
    big = {w: _adamw(f"adamw_{w}", [g.reshape(P[w].shape)], P[w], M1[w], M2[w]) for w, g in zip(_BIG, full)}

    outs = []
    for i in range(4):
        small = {**unpack_rep(rep[i]), **unpack_sh(shd[i])}
        outs.append([big[w][i] if w in big else small[w] for w in _WEIGHTS])
    return (loss, grad_x.reshape(1, S, D), *outs[0], *outs[1], *outs[2], *outs[3])
```

```python
import functools
import math

import jax
import jax.numpy as jnp
from jax import lax
from jax.experimental import pallas as pl
from jax.experimental.pallas import tpu as pltpu

F32 = jnp.float32
BF16 = jnp.bfloat16

EPS = 1e-6
LRU_C = 8.0
HEAD_DIM = 64
LANES = 128
SUBLANES = 8
VMEM_LIMIT = 48 * 1024 * 1024
N_CHIPS = 4
N_DEV = 8

ADAM_LR = 0.001
ADAM_B1 = 0.9
ADAM_B2 = 0.999
ADAM_EPS = 1e-08
ADAM_WD = 0.01
ADAM_STEP = 10

_NN = (((1,), (0,)), ((), ()))
_NT = (((1,), (1,)), ((), ()))
_TN = (((0,), (0,)), ((), ()))
_DN = {"nn": _NN, "nt": _NT, "tn": _TN}
MESH = pl.DeviceIdType.MESH


def _hbm_out(shape, dtype):
    return pltpu.HBM(shape, dtype)


def _params(sem):
    return pltpu.CompilerParams(dimension_semantics=sem, vmem_limit_bytes=VMEM_LIMIT)


def _tile(n, want):
    if n <= want:
        return n
    t = (want // LANES) * LANES
    while t >= LANES:
        if n % t == 0:
            return t
        t -= LANES
    return n


def _sigmoid(x):
    return 1.0 / (1.0 + jnp.exp(-x))


def _sigmoid_t(x):
    return 0.5 * jnp.tanh(0.5 * x) + 0.5


def _softplus(x):
    return jnp.maximum(x, 0.0) + jnp.log(1.0 + jnp.exp(-jnp.abs(x)))


_GELU_C = math.sqrt(2.0 / math.pi)


def _gelu_and_grad(x):
    inner = _GELU_C * (x + 0.044715 * x * x * x)
    t = jnp.tanh(inner)
    g = 0.5 * x * (1.0 + t)
    dg = 0.5 * (1.0 + t) + 0.5 * x * (1.0 - t * t) * _GELU_C * (1.0 + 3.0 * 0.044715 * x * x)
    return g, dg


def _rms(x):
    return lax.rsqrt(jnp.mean(x * x, axis=-1, keepdims=True) + EPS)


def _rms_bwd(dy, x, g):
    r = _rms(x)
    xr = x * r
    dyg = dy * g
    return r * dyg - xr * (r * jnp.mean(dyg * xr, axis=-1, keepdims=True)), jnp.sum(dy * xr, axis=0, keepdims=True)


def _mm(name, mode, a, b, *, grid, a_spec, b_spec, out_shape, out_dtype, out_spec, nk=1,
        res=None, res_spec=None, bias=None, bias_spec=None, scale=None, norm_gain=None, norm_bwd=None):
    dn = _DN[mode]
    has_res, has_bias = res is not None, bias is not None
    blk = tuple(d for d in out_spec.block_shape if d is not None)
    vec = pl.BlockSpec((1, blk[-1]), lambda *g: (0, 0))
    a_specs = a_spec if isinstance(a_spec, list) else [a_spec]
    b_specs = b_spec if isinstance(b_spec, list) else [b_spec]
    npair = len(a_specs)
    n_in = 2 * npair + int(has_res) + int(has_bias) + (1 if norm_gain is not None else 0) + (3 if norm_bwd else 0)

    def body(*refs):
        p = 2 * npair
        r_ref = refs[p] if has_res else None
        p += int(has_res)
        bias_ref = refs[p] if has_bias else None
        p += int(has_bias)
        extra = refs[p:n_in]
        outs = refs[n_in:]
        o_ref = outs[0]
        part = lax.dot_general(refs[0][...], refs[npair][...], dn, preferred_element_type=F32)
        for t in range(1, npair):
            part = part + lax.dot_general(refs[t][...], refs[npair + t][...], dn, preferred_element_type=F32)

        def finish(acc):
            if scale is not None:
                acc = acc * scale
            if has_bias:
                acc = acc + bias_ref[...]
            if has_res:
                acc = r_ref[...] + acc
            if norm_bwd:
                h_ref, g_ref, dh_ref = extra
                dx, dg = _rms_bwd(acc, h_ref[...], g_ref[...])
                acc = dh_ref[...] + dx
                outs[1][...] = acc.astype(BF16)
                outs[2][...] = dg
            if norm_gain is not None:
                outs[1][...] = (acc * _rms(acc) * extra[0][...]).astype(BF16)
            o_ref[...] = acc.astype(o_ref.dtype)

        if nk == 1:
            finish(part)
        else:
            acc_ref = refs[-1]
            k = pl.program_id(2)

            @pl.when(k == 0)
            def _():
                acc_ref[...] = part

            @pl.when(k > 0)
            def _():
                acc_ref[...] += part

            @pl.when(k == nk - 1)
            def _():
                finish(acc_ref[...])

    ins, specs = [a] * npair + [b] * npair, a_specs + b_specs
    if has_res:
        ins.append(res)
        specs.append(res_spec)
    if has_bias:
        ins.append(bias)
        specs.append(bias_spec)
    out_specs, out_shapes = [out_spec], [_hbm_out(out_shape, out_dtype)]
    if norm_gain is not None:
        ins.append(norm_gain)
        specs.append(vec)
        out_specs.append(out_spec)
        out_shapes.append(_hbm_out(out_shape, BF16))
    if norm_bwd:
        h, g, dh = norm_bwd
        ins += [h, g, dh]
        specs += [out_spec, vec, out_spec]
        out_specs += [out_spec, pl.BlockSpec((None, 1, blk[-1]), lambda i, *rest: (i, 0, 0))]
        out_shapes += [_hbm_out(out_shape, BF16), _hbm_out((grid[0], 1, blk[-1]), F32)]
    sem = ("parallel", "parallel") + (("arbitrary",) if len(grid) == 3 else ())
    single = len(out_specs) == 1
    return pl.pallas_call(
        body, name=name, grid=grid, in_specs=specs, out_specs=out_specs[0] if single else out_specs,
        out_shape=out_shapes[0] if single else out_shapes,
        scratch_shapes=[pltpu.VMEM(blk, F32)] if nk > 1 else [],
        compiler_params=_params(sem),
    )(*ins)


def _mm_nn(name, a, b, *, b_lead=(), out_dtype, tm=512, tn=512, res=None, bias=None, scale=None, norm_gain=None):
    M, K = a.shape
    N = b.shape[-1]
    tm, tn = _tile(M, tm), _tile(N, tn)
    nl = len(b_lead)
    return _mm(
        name, "nn", a, b, grid=(M // tm, N // tn),
        a_spec=pl.BlockSpec((tm, K), lambda i, j: (i, 0)),
        b_spec=pl.BlockSpec((None,) * nl + (K, tn), lambda i, j: tuple(b_lead) + (0, j)),
        out_shape=(M, N), out_dtype=out_dtype, out_spec=pl.BlockSpec((tm, tn), lambda i, j: (i, j)),
        res=res, res_spec=pl.BlockSpec((tm, tn), lambda i, j: (i, j)),
        bias=bias, bias_spec=pl.BlockSpec((1, tn), lambda i, j: (0, j)), scale=scale, norm_gain=norm_gain)


def _mm_nt(name, a, b, *, b_lead=(), out_dtype, tm=512, tn=512, tk=2048, res=None, norm_bwd=None):
    M, K = a.shape
    N = b.shape[-2]
    tm, tn, tk = _tile(M, tm), _tile(N, tn), _tile(K, tk)
    nk = K // tk
    nl = len(b_lead)
    return _mm(
        name, "nt", a, b, grid=(M // tm, N // tn, nk), nk=nk,
        a_spec=pl.BlockSpec((tm, tk), lambda i, j, k: (i, k)),
        b_spec=pl.BlockSpec((None,) * nl + (tn, tk), lambda i, j, k: tuple(b_lead) + (j, k)),
        out_shape=(M, N), out_dtype=out_dtype, out_spec=pl.BlockSpec((tm, tn), lambda i, j, k: (i, j)),
        res=res, res_spec=pl.BlockSpec((tm, tn), lambda i, j, k: (i, j)), norm_bwd=norm_bwd)


def _mm_tn(name, a, b, *, out_dtype, tm=512, tn=512):
    S, M = a.shape
    N = b.shape[1]
    tm, tn = _tile(M, tm), _tile(N, tn)
    return _mm(
        name, "tn", a, b, grid=(M // tm, N // tn),
        a_spec=pl.BlockSpec((S, tm), lambda i, j: (0, i)),
        b_spec=pl.BlockSpec((S, tn), lambda i, j: (0, j)),
        out_shape=(M, N), out_dtype=out_dtype, out_spec=pl.BlockSpec((tm, tn), lambda i, j: (i, j)))


def _rmsnorm_fwd(name, h, g, tr=256):
    S, D = h.shape
    tr = _tile(S, tr)

    def body(h_ref, g_ref, o_ref):
        x = h_ref[...]
        r = lax.rsqrt(jnp.mean(x * x, axis=-1, keepdims=True) + EPS)
        o_ref[...] = (x * r * g_ref[...]).astype(o_ref.dtype)

    return pl.pallas_call(
        body, name=name, grid=(S // tr,),
        in_specs=[pl.BlockSpec((tr, D), lambda i: (i, 0)), pl.BlockSpec((1, D), lambda i: (0, 0))],
        out_specs=pl.BlockSpec((tr, D), lambda i: (i, 0)),
        out_shape=_hbm_out((S, D), BF16),
        compiler_params=_params(("parallel",)),
    )(h, g)


def _loss_head(name, h, target, g, tr=256):
    S, D = h.shape
    tr = _tile(S, tr)

    def body(h_ref, t_ref, g_ref, o_ref, ob_ref, dg_ref, loss_ref):
        i = pl.program_id(0)
        x = h_ref[...]
        gg = g_ref[...]
        r = lax.rsqrt(jnp.mean(x * x, axis=-1, keepdims=True) + EPS)
        xr = x * r
        err = xr * gg - t_ref[...]
        lpart = 0.5 * jnp.sum(jnp.mean(err * err, axis=-1, keepdims=True), axis=0, keepdims=True)
        dy = err * (1.0 / D)
        dyg = dy * gg
        dx = r * dyg - xr * (r * jnp.mean(dyg * xr, axis=-1, keepdims=True))
        o_ref[...] = dx
        ob_ref[...] = dx.astype(BF16)
        part = jnp.sum(dy * xr, axis=0, keepdims=True)
        lrow = jnp.broadcast_to(lpart, (1, LANES))

        @pl.when(i == 0)
        def _():
            dg_ref[...] = part
            loss_ref[...] = lrow

        @pl.when(i > 0)
        def _():
            dg_ref[...] += part
            loss_ref[...] += lrow

    row = pl.BlockSpec((tr, D), lambda i: (i, 0))
    vec = pl.BlockSpec((1, D), lambda i: (0, 0))
    return pl.pallas_call(
        body, name=name, grid=(S // tr,),
        in_specs=[row, row, vec], out_specs=[row, row, vec, pl.BlockSpec((1, LANES), lambda i: (0, 0))],
        out_shape=[_hbm_out((S, D), F32), _hbm_out((S, D), BF16),
                   _hbm_out((1, D), F32), _hbm_out((1, LANES), F32)],
        compiler_params=_params(("arbitrary",)),
    )(h, target, g)


def _swiglu_fwd(name, hn, w_in, tm=512):
    S, D = hn.shape
    FH = w_in.shape[-1]
    tm = _tile(S, tm)

    def body(x_ref, wg_ref, wu_ref, z_ref, a_ref):
        x = x_ref[...]
        zg = jnp.dot(x, wg_ref[...], preferred_element_type=F32)
        zu = jnp.dot(x, wu_ref[...], preferred_element_type=F32)
        z_ref[0] = zg.astype(z_ref.dtype)
        z_ref[1] = zu.astype(z_ref.dtype)
        a_ref[...] = (zg * _sigmoid_t(zg) * zu).astype(a_ref.dtype)

    return pl.pallas_call(
        body, name=name, grid=(S // tm, 2),
        in_specs=[pl.BlockSpec((tm, D), lambda i, j: (i, 0)),
                  pl.BlockSpec((None, D, FH), lambda i, j: (j, 0, 0)),
                  pl.BlockSpec((None, D, FH), lambda i, j: (j + 2, 0, 0))],
        out_specs=[pl.BlockSpec((2, tm, FH), lambda i, j: (0, i, j)), pl.BlockSpec((tm, FH), lambda i, j: (i, j))],
        out_shape=[_hbm_out((2, S, 2 * FH), BF16), _hbm_out((S, 2 * FH), BF16)],
        compiler_params=_params(("parallel", "parallel")),
    )(hn, w_in, w_in)


def _swiglu_bwd(name, dhb, w_out, z3, tm=512):
    S, D = dhb.shape
    F = w_out.shape[0]
    FH = F // 2
    tm = _tile(S, tm)

    def body(d_ref, w_ref, z_ref, dz_ref):
        d = lax.dot_general(d_ref[...], w_ref[...], _NT, preferred_element_type=F32)
        zg = z_ref[0].astype(F32)
        zu = z_ref[1].astype(F32)
        sg = _sigmoid_t(zg)
        dz_ref[0] = (d * zu * (sg * (1.0 + zg * (1.0 - sg)))).astype(dz_ref.dtype)
        dz_ref[1] = (d * (zg * sg)).astype(dz_ref.dtype)

    zspec = pl.BlockSpec((2, tm, FH), lambda i, j: (0, i, j))
    return pl.pallas_call(
        body, name=name, grid=(S // tm, 2),
        in_specs=[pl.BlockSpec((tm, D), lambda i, j: (i, 0)), pl.BlockSpec((FH, D), lambda i, j: (j, 0)), zspec],
        out_specs=zspec, out_shape=_hbm_out((2, S, F), BF16),
        compiler_params=_params(("parallel", "parallel")),
    )(dhb, w_out, z3)


SCAN_ROWS = 64


def _group_scan(A, B, reverse):
    n = A.shape[0]
    sub = lax.broadcasted_iota(jnp.int32, A.shape, 0) % SUBLANES
    for d in (1, 2, 4):
        if reverse:
            A_sh, B_sh = pltpu.roll(A, n - d, 0), pltpu.roll(B, n - d, 0)
            keep = sub < SUBLANES - d
        else:
            A_sh, B_sh = pltpu.roll(A, d, 0), pltpu.roll(B, d, 0)
            keep = sub >= d
        B = jnp.where(keep, A * B_sh + B, B)
        A = jnp.where(keep, A * A_sh, A)
    return A, B


def _block_scan(a, u, carry, reverse):
    A, B = _group_scan(a, u, reverse)
    ng = a.shape[0] // SUBLANES
    out = [None] * ng
    order = range(ng - 1, -1, -1) if reverse else range(ng)
    for gi in order:
        sl = slice(gi * SUBLANES, (gi + 1) * SUBLANES)
        hg = A[sl] * carry + B[sl]
        out[gi] = hg
        carry = hg[0:1] if reverse else hg[SUBLANES - 1:SUBLANES]
    return jnp.concatenate(out, axis=0), carry


def _lru_gates(rc, gip, grp, sp):
    gi = _sigmoid_t(gip)
    gr = _sigmoid_t(grp)
    la = -LRU_C * gr * sp
    a = jnp.exp(la)
    om = -jnp.tanh(la) * (a * a + 1.0)
    mult = jnp.sqrt(om)
    return gi, gr, a, mult


def _lru_fwd(name, proj, rc, gip, grp, lru_p, tc=256):
    S, C = rc.shape
    tc = _tile(C, tc)
    nb = S // SCAN_ROWS

    def body(gb_ref, rc_ref, gi_ref, gr_ref, l_ref, h_ref, m_ref):
        sp = _softplus(-l_ref[...])

        def step(b, carry):
            rows = pl.ds(pl.multiple_of(b * SCAN_ROWS, SCAN_ROWS), SCAN_ROWS)
            rcb = rc_ref[rows, :]
            gi, _, a, mult = _lru_gates(rcb, gi_ref[rows, :], gr_ref[rows, :], sp)
            h, carry = _block_scan(a, rcb * gi * mult, carry, False)
            h_ref[rows, :] = h
            gel, _ = _gelu_and_grad(gb_ref[rows, :])
            m_ref[rows, :] = (gel * h).astype(m_ref.dtype)
            return carry

        lax.fori_loop(0, nb, step, jnp.zeros((1, tc), F32))

    col = pl.BlockSpec((S, tc), lambda j: (0, j))
    return pl.pallas_call(
        body, name=name, grid=(C // tc,),
        in_specs=[col, col, col, col, pl.BlockSpec((1, tc), lambda j: (0, j))],
        out_specs=[col, col],
        out_shape=[_hbm_out((S, C), F32), _hbm_out((S, C), BF16)],
        compiler_params=_params(("parallel",)),
    )(proj, rc, gip, grp, lru_p)


def _lru_bwd(name, dm, proj, hrec, rc, gip, grp, lru_p, tc=256):
    S, C = rc.shape
    tc = _tile(C, tc)
    nb = S // SCAN_ROWS
    R = SCAN_ROWS

    def body(dm_ref, gb_ref, h_ref, rc_ref, gi_ref, gr_ref, l_ref,
             dgb_ref, dgi_ref, dgr_ref, drc_ref, dbi_ref, dbr_ref, dl_ref):
        lp = l_ref[...]
        sp = _softplus(-lp)
        row = lax.broadcasted_iota(jnp.int32, (R, tc), 0)
        zero = jnp.zeros((1, tc), F32)

        def step(t, carry):
            mu_in, s_i, s_r, s_sp = carry
            b = nb - 1 - t
            r0 = pl.multiple_of(b * R, R)
            rows = pl.ds(r0, R)
            rcb = rc_ref[rows, :]
            gi, gr, a, mult = _lru_gates(rcb, gi_ref[rows, :], gr_ref[rows, :], sp)
            gel, dgel = _gelu_and_grad(gb_ref[rows, :])
            dmb = dm_ref[rows, :]
            h = h_ref[rows, :]
            dgb_ref[rows, :] = (dmb * h * dgel).astype(dgb_ref.dtype)
            dh = dmb * gel
            mu, mu_out = _block_scan(a, a * dh, mu_in, True)
            mu_next = jnp.where(row == R - 1, mu_in, pltpu.roll(mu, R - 1, 0))
            lam = dh + mu_next
            p0 = pl.multiple_of(jnp.maximum(r0 - SUBLANES, 0), SUBLANES)
            prev = h_ref[pl.ds(p0, SUBLANES), :][SUBLANES - 1:SUBLANES]
            prev = jnp.where(b > 0, prev, 0.0)
            h_prev = jnp.where(row == 0, prev, pltpu.roll(h, 1, 0))
            da = lam * h_prev
            d_mult = lam * rcb * gi
            d_la = da * a - d_mult * (a * a) / mult
            d_grp = d_la * (-LRU_C * sp) * gr * (1.0 - gr)
            d_gip = lam * rcb * mult * gi * (1.0 - gi)
            dgr_ref[rows, :] = d_grp.astype(dgr_ref.dtype)
            dgi_ref[rows, :] = d_gip.astype(dgi_ref.dtype)
            drc_ref[rows, :] = lam * gi * mult
            s_i = s_i + jnp.sum(d_gip, axis=0, keepdims=True)
            s_r = s_r + jnp.sum(d_grp, axis=0, keepdims=True)
            s_sp = s_sp + jnp.sum(d_la * gr, axis=0, keepdims=True)
            return mu_out, s_i, s_r, s_sp

        _, s_i, s_r, s_sp = lax.fori_loop(0, nb, step, (zero, zero, zero, zero))
        dbi_ref[...] = s_i
        dbr_ref[...] = s_r
        dl_ref[...] = (-LRU_C * s_sp) * (-_sigmoid(-lp))

    col = pl.BlockSpec((S, tc), lambda j: (0, j))
    vec = pl.BlockSpec((1, tc), lambda j: (0, j))
    return pl.pallas_call(
        body, name=name, grid=(C // tc,),
        in_specs=[col, col, col, col, col, col, vec],
        out_specs=[col, col, col, col, vec, vec, vec],
        out_shape=[_hbm_out((S, C), BF16), _hbm_out((S, C), BF16),
                   _hbm_out((S, C), BF16), _hbm_out((S, C), F32),
                   _hbm_out((1, C), F32), _hbm_out((1, C), F32),
                   _hbm_out((1, C), F32)],
        compiler_params=_params(("parallel",)),
    )(dm, proj, hrec, rc, gip, grp, lru_p)


def _cumsum_rows(name, u, reverse):
    S, C = u.shape
    nb = S // SCAN_ROWS

    def body(u_ref, o_ref):
        def step(t, carry):
            b = nb - 1 - t if reverse else t
            rows = pl.ds(pl.multiple_of(b * SCAN_ROWS, SCAN_ROWS), SCAN_ROWS)
            ub = u_ref[rows, :]
            h, carry = _block_scan(jnp.ones_like(ub), ub, carry, reverse)
            o_ref[rows, :] = h
            return carry

        lax.fori_loop(0, nb, step, jnp.zeros((1, C), F32))

    spec = pl.BlockSpec((S, C), lambda i: (0, 0))
    return pl.pallas_call(
        body, name=name, grid=(1,), in_specs=[spec], out_specs=spec,
        out_shape=_hbm_out((S, C), F32),
        compiler_params=_params(("arbitrary",)),
    )(u)


def _shift_down(x, k):
    row = lax.broadcasted_iota(jnp.int32, x.shape, 0)
    return jnp.where(row >= k, pltpu.roll(x, k, 0), 0.0)


def _shift_up(x, k):
    n = x.shape[0]
    row = lax.broadcasted_iota(jnp.int32, x.shape, 0)
    return jnp.where(row < n - k, pltpu.roll(x, n - k, 0), 0.0)


def _conv_fwd(name, proj, w, b, tc=256):
    S, C2 = proj.shape
    C = C2 // 2
    tc = _tile(C, tc)
    off = C // tc

    def body(x_ref, w_ref, b_ref, o_ref, ob_ref):
        x = x_ref[...]
        out = b_ref[...] + w_ref[3:4, :] * x
        for k in (1, 2, 3):
            out = out + w_ref[3 - k:4 - k, :] * _shift_down(x, k)
        o_ref[...] = out
        ob_ref[...] = out.astype(BF16)

    col = pl.BlockSpec((S, tc), lambda j: (0, j))
    return pl.pallas_call(
        body, name=name, grid=(C // tc,),
        in_specs=[pl.BlockSpec((S, tc), lambda j: (0, off + j)),
                  pl.BlockSpec((4, tc), lambda j: (0, j)), pl.BlockSpec((1, tc), lambda j: (0, j))],
        out_specs=[col, col],
        out_shape=[_hbm_out((S, C), F32), _hbm_out((S, C), BF16)],
        compiler_params=_params(("parallel",)),
    )(proj, w, b)


def _conv_bwd(name, drc, proj, w, tc=256):
    S, C = drc.shape
    tc = _tile(C, tc)
    off = C // tc

    def body(y_ref, x_ref, w_ref, dx_ref, dw_ref, db_ref):
        y = y_ref[...]
        x = x_ref[...]
        dx = w_ref[3:4, :] * y
        dw_ref[3:4, :] = jnp.sum(y * x, axis=0, keepdims=True)
        for k in (1, 2, 3):
            dx = dx + w_ref[3 - k:4 - k, :] * _shift_up(y, k)
            dw_ref[3 - k:4 - k, :] = jnp.sum(y * _shift_down(x, k), axis=0, keepdims=True)
        dx_ref[...] = dx.astype(dx_ref.dtype)
        db_ref[...] = jnp.sum(y, axis=0, keepdims=True)

    col = pl.BlockSpec((S, tc), lambda j: (0, j))
    return pl.pallas_call(
        body, name=name, grid=(C // tc,),
        in_specs=[col, pl.BlockSpec((S, tc), lambda j: (0, off + j)), pl.BlockSpec((4, tc), lambda j: (0, j))],
        out_specs=[col, pl.BlockSpec((4, tc), lambda j: (0, j)), pl.BlockSpec((1, tc), lambda j: (0, j))],
        out_shape=[_hbm_out((S, C), BF16), _hbm_out((4, C), F32),
                   _hbm_out((1, C), F32)],
        compiler_params=_params(("parallel",)),
    )(drc, proj, w)


def _gates_fwd(name, rcb, wg, bg):
    S, C = rcb.shape
    nblk, bw, _ = wg.shape

    def body(x_ref, w_ref, b_ref, gi_ref, gr_ref):
        g = jnp.dot(x_ref[...], w_ref[...], preferred_element_type=F32) + b_ref[...]
        gi_ref[...] = g[:, :bw]
        gr_ref[...] = g[:, bw:]

    col = pl.BlockSpec((S, bw), lambda n: (0, n))
    return pl.pallas_call(
        body, name=name, grid=(nblk,),
        in_specs=[col, pl.BlockSpec((None, bw, 2 * bw), lambda n: (n, 0, 0)),
                  pl.BlockSpec((None, 1, 2 * bw), lambda n: (n, 0, 0))],
        out_specs=[col, col],
        out_shape=[_hbm_out((S, C), F32), _hbm_out((S, C), F32)],
        compiler_params=_params(("parallel",)),
    )(rcb, wg, bg)


def _gates_bwd(name, dgi, dgr, rcb, wg, drc1):
    S, C = rcb.shape
    nblk, bw, _ = wg.shape

    def body(dgi_ref, dgr_ref, x_ref, w_ref, d1_ref, drc_ref, dw_ref):
        w = w_ref[...]
        x = x_ref[...]
        di, dr = dgi_ref[...], dgr_ref[...]
        drc_ref[...] = (d1_ref[...]
                        + lax.dot_general(di, w[:, :bw], _NT, preferred_element_type=F32)
                        + lax.dot_general(dr, w[:, bw:], _NT, preferred_element_type=F32))
        dw_ref[:, :bw] = lax.dot_general(x, di, _TN, preferred_element_type=F32).astype(dw_ref.dtype)
        dw_ref[:, bw:] = lax.dot_general(x, dr, _TN, preferred_element_type=F32).astype(dw_ref.dtype)

    col = pl.BlockSpec((S, bw), lambda n: (0, n))
    wspec = pl.BlockSpec((None, bw, 2 * bw), lambda n: (n, 0, 0))
    return pl.pallas_call(
        body, name=name, grid=(nblk,),
        in_specs=[col, col, col, wspec, col], out_specs=[col, wspec],
        out_shape=[_hbm_out((S, C), F32), _hbm_out((nblk, bw, 2 * bw), BF16)],
        compiler_params=_params(("parallel",)),
    )(dgi, dgr, rcb, wg, drc1)


def _att_tile(S):
    return next(t for t in (512, 256, 128) if S % t == 0)


def _head_lanes(shape):
    return lax.broadcasted_iota(jnp.int32, shape, len(shape) - 1) < HEAD_DIM


def _key_bias(c_ref, rows, hh):
    return jnp.broadcast_to(c_ref[rows, hh:hh + 1], (rows.size, LANES))


def _over_keys(x, op):
    n = x.shape[0]
    while n > SUBLANES:
        n //= 2
        x = op(x[:n], x[n:2 * n])
    return (jnp.max if op is jnp.maximum else jnp.sum)(x, axis=0, keepdims=True)


def _causal_t(T, cc):
    r = lax.broadcasted_iota(jnp.int32, (T, LANES), 0)
    c = lax.broadcasted_iota(jnp.int32, (T, LANES), 1) + cc * LANES
    return r <= c


def _attn_fwd(name, q, kv, c3):
    S, D = q.shape
    HP = D // LANES
    T = _att_tile(S)
    nq = S // T
    NC = T // LANES

    def body(q_ref, k_ref, v_ref, c_ref, o_ref, of_ref, lse_ref, bias, vT, acc, m_scr, l_scr):
        def prologue(i, _):
            rows = pl.ds(pl.multiple_of(i * T, T), T)
            for hh in range(2):
                bias[hh, rows, :] = _key_bias(c_ref, rows, hh)
            vT[i] = v_ref[rows, :].astype(F32).T.astype(BF16)
            return 0

        lax.fori_loop(0, nq, prologue, 0)

        def q_step(qi, _):
            q0 = pl.multiple_of(qi * T, T)
            qb = q_ref[pl.ds(q0, T), :]
            m_scr[...] = jnp.full(m_scr.shape, -jnp.inf, F32)
            l_scr[...] = jnp.zeros(l_scr.shape, F32)
            acc[...] = jnp.zeros(acc.shape, F32)

            def tile(kj, masked):
                ks = pl.ds(pl.multiple_of(kj * T, T), T)
                kf = k_ref[ks, :].astype(F32)
                first = _head_lanes(kf.shape)
                kms = [jnp.where(first if hh == 0 else jnp.logical_not(first), kf, 0.0).astype(BF16) for hh in range(2)]
                sTs = [lax.dot_general(km, qb, _NT, preferred_element_type=F32) for km in kms]
                for hh in range(2):
                    b = bias[hh, ks, :]
                    ps = []
                    for cc in range(NC):
                        cols = slice(cc * LANES, (cc + 1) * LANES)
                        s = sTs[hh][:, cols] + b
                        if masked:
                            s = jnp.where(_causal_t(T, cc), s, -jnp.inf)
                        m_old = m_scr[hh, cc]
                        m_new = jnp.maximum(m_old, _over_keys(s, jnp.maximum))
                        alpha = jnp.exp(m_old - m_new)
                        p = jnp.exp(s - m_new)
                        l_scr[hh, cc] = alpha * l_scr[hh, cc] + _over_keys(p, jnp.add)
                        m_scr[hh, cc] = m_new
                        ps.append(p.astype(BF16))
                        acc[hh, :, cols] = acc[hh, :, cols] * alpha
                    acc[hh] += jnp.dot(vT[kj, hh * HEAD_DIM:(hh + 1) * HEAD_DIM, :], jnp.concatenate(ps, axis=1),
                                       preferred_element_type=F32)

            def inner(kj, _):
                tile(kj, False)
                return 0

            lax.fori_loop(0, qi, inner, 0)
            tile(qi, True)
            outs = []
            for hh in range(2):
                inv = jnp.concatenate([1.0 / l_scr[hh, cc] for cc in range(NC)], axis=1)
                outs.append(acc[hh] * inv)
                for cc in range(NC):
                    lse_ref[hh:hh + 1, pl.ds(q0 + cc * LANES, LANES)] = m_scr[hh, cc] + jnp.log(l_scr[hh, cc])
            out = jnp.concatenate(outs, axis=0).T
            o_ref[pl.ds(q0, T), :] = out.astype(o_ref.dtype)
            of_ref[pl.ds(q0, T), :] = out
            return 0

        lax.fori_loop(0, nq, q_step, 0)

    blk = lambda off: pl.BlockSpec((S, LANES), lambda p: (0, off + p))
    return pl.pallas_call(
        body, name=name, grid=(HP,),
        in_specs=[blk(0), blk(0), blk(HP), pl.BlockSpec((None, S, 2), lambda p: (p, 0, 0))],
        out_specs=[blk(0), blk(0), pl.BlockSpec((None, 2, S), lambda p: (p, 0, 0))],
        out_shape=[_hbm_out((S, D), BF16), _hbm_out((S, D), F32),
                   _hbm_out((HP, 2, S), F32)],
        scratch_shapes=[pltpu.VMEM((2, S, LANES), F32), pltpu.VMEM((nq, LANES, T), BF16),
                        pltpu.VMEM((2, HEAD_DIM, T), F32), pltpu.VMEM((2, NC, 1, LANES), F32),
                        pltpu.VMEM((2, NC, 1, LANES), F32)],
        compiler_params=_params(("parallel",)),
    )(q, kv, kv, c3)


def _attn_bwd(name, q, kv, c3, of, do, lse3):
    S, D = q.shape
    HP = D // LANES
    T = _att_tile(S)
    nq = S // T
    NC = T // LANES
    scale = HEAD_DIM ** -0.5

    def body(q_ref, k_ref, v_ref, c_ref, of_ref, do_ref, lse_ref,
             dq_ref, dk_ref, dv_ref, dck_ref, drq_ref, bias, kT, dqT, delta, dr_scr, dk_acc, dv_acc, dc_acc):
        def prologue(i, _):
            rows = pl.ds(pl.multiple_of(i * T, T), T)
            for hh in range(2):
                bias[hh, rows, :] = _key_bias(c_ref, rows, hh)
            kT[i] = k_ref[rows, :].astype(F32).T.astype(BF16)
            prodT = (do_ref[rows, :].astype(F32) * of_ref[rows, :]).T
            for hh in range(2):
                delta[hh:hh + 1, rows] = jnp.sum(prodT[hh * HEAD_DIM:(hh + 1) * HEAD_DIM], axis=0, keepdims=True)
            dqT[i] = jnp.zeros((LANES, T), F32)
            return 0

        lax.fori_loop(0, nq, prologue, 0)
        dr_scr[...] = jnp.zeros(dr_scr.shape, F32)

        def kv_step(kj, _):
            ks = pl.ds(pl.multiple_of(kj * T, T), T)
            kf = k_ref[ks, :].astype(F32)
            vf = v_ref[ks, :].astype(F32)
            first = _head_lanes(kf.shape)
            masks = [first, jnp.logical_not(first)]
            kms = [jnp.where(m, kf, 0.0).astype(BF16) for m in masks]
            vms = [jnp.where(m, vf, 0.0).astype(BF16) for m in masks]

            for acc in (dk_acc, dv_acc, dc_acc):
                acc[...] = jnp.zeros(acc.shape, F32)

            def tile(qi, masked):
                q0 = pl.multiple_of(qi * T, T)
                qb = q_ref[pl.ds(q0, T), :]
                dob = do_ref[pl.ds(q0, T), :]
                sTs = [lax.dot_general(km, qb, _NT, preferred_element_type=F32) for km in kms]
                dpTs = [lax.dot_general(vm, dob, _NT, preferred_element_type=F32) for vm in vms]
                for hh in range(2):
                    b = bias[hh, ks, :]
                    head = slice(hh * HEAD_DIM, (hh + 1) * HEAD_DIM)
                    ps, dss = [], []
                    for cc in range(NC):
                        cols = slice(cc * LANES, (cc + 1) * LANES)
                        at = pl.ds(q0 + cc * LANES, LANES)
                        p = jnp.exp(sTs[hh][:, cols] + b - lse_ref[hh:hh + 1, at])
                        if masked:
                            p = jnp.where(_causal_t(T, cc), p, 0.0)
                        ds = p * (dpTs[hh][:, cols] - delta[hh:hh + 1, at])
                        ps.append(p.astype(BF16))
                        dss.append(ds.astype(BF16))
                        dc_acc[hh] += ds
                        dr_scr[hh:hh + 1, at] += _over_keys(ds, jnp.add)
                    pT = jnp.concatenate(ps, axis=1)
                    dsT = jnp.concatenate(dss, axis=1)
                    dv_acc[hh] += jnp.dot(pT, dob, preferred_element_type=F32)
                    dk_acc[hh] += jnp.dot(dsT, qb, preferred_element_type=F32)
                    dqT[qi, head, :] += jnp.dot(kT[kj, head, :], dsT, preferred_element_type=F32)

            def inner(qi, _):
                tile(qi, False)
                return 0

            tile(kj, True)
            lax.fori_loop(kj + 1, nq, inner, 0)
            dk_ref[ks, :] = jnp.where(first, dk_acc[0], dk_acc[1])
            dv_ref[ks, :] = jnp.where(first, dv_acc[0], dv_acc[1])
            for hh in range(2):
                dck_ref[hh:hh + 1, ks] = -jnp.sum(dc_acc[hh].T, axis=0, keepdims=True)
            return 0

        lax.fori_loop(0, nq, kv_step, 0)

        def epilogue(i, _):
            rows = pl.ds(pl.multiple_of(i * T, T), T)
            dq_ref[rows, :] = (dqT[i].T * scale).astype(dq_ref.dtype)
            return 0

        lax.fori_loop(0, nq, epilogue, 0)
        drq_ref[...] = dr_scr[...]

    blk = lambda off: pl.BlockSpec((S, LANES), lambda p: (0, off + p))
    row_spec = pl.BlockSpec((None, 2, S), lambda p: (p, 0, 0))
    return pl.pallas_call(
        body, name=name, grid=(HP,),
        in_specs=[blk(0), blk(0), blk(HP), pl.BlockSpec((None, S, 2), lambda p: (p, 0, 0)), blk(0), blk(0), row_spec],
        out_specs=[blk(0), blk(0), blk(0), row_spec, row_spec],
        out_shape=[_hbm_out((S, D), BF16), _hbm_out((S, D), F32),
                   _hbm_out((S, D), F32), _hbm_out((HP, 2, S), F32),
                   _hbm_out((HP, 2, S), F32)],
        scratch_shapes=[pltpu.VMEM((2, S, LANES), F32), pltpu.VMEM((nq, LANES, T), BF16),
                        pltpu.VMEM((nq, LANES, T), F32), pltpu.VMEM((2, S), F32), pltpu.VMEM((2, S), F32)]
        + [pltpu.VMEM((2, T, LANES), F32)] * 3,
        compiler_params=_params(("parallel",)),
    )(q, kv, kv, c3, of, do, lse3)


def _logsig_fwd(name, f):
    S, C = f.shape

    def body(f_ref, o_ref):
        o_ref[...] = -_softplus(-f_ref[...])

    spec = pl.BlockSpec((S, C), lambda i: (0, 0))
    return pl.pallas_call(body, name=name, grid=(1,), in_specs=[spec], out_specs=spec,
                          out_shape=_hbm_out((S, C), F32),
                          compiler_params=_params(("arbitrary",)))(f)


def _logsig_bwd(name, dls, f):
    S, C = f.shape

    def body(d_ref, f_ref, o_ref, s_ref):
        df = d_ref[...] * _sigmoid(-f_ref[...])
        o_ref[...] = df.astype(o_ref.dtype)
        s_ref[...] = jnp.sum(df, axis=0, keepdims=True)

    spec = pl.BlockSpec((S, C), lambda i: (0, 0))
    return pl.pallas_call(body, name=name, grid=(1,), in_specs=[spec, spec],
                          out_specs=[spec, pl.BlockSpec((1, C), lambda i: (0, 0))],
                          out_shape=[_hbm_out((S, C), BF16), _hbm_out((1, C), F32)],
                          compiler_params=_params(("arbitrary",)))(dls, f)


def _add_cast(name, parts, out_dtype, tr=256):
    S, C = parts[0].shape
    tr = _tile(S, tr)
    n = len(parts)

    def body(*refs):
        acc = refs[0][...].astype(F32)
        for r in refs[1:n]:
            acc = acc + r[...].astype(F32)
        refs[n][...] = acc.astype(out_dtype)

    spec = pl.BlockSpec((tr, C), lambda i: (i, 0))
    return pl.pallas_call(body, name=name, grid=(S // tr,), in_specs=[spec] * n, out_specs=spec,
                          out_shape=_hbm_out((S, C), out_dtype),
                          compiler_params=_params(("parallel",)))(*parts)


def _local_step(x, target, gains, layer_weights, layer_prefetch, layer_grads):
    S, D = x.shape
    HP = D // LANES
    scale = HEAD_DIM ** -0.5
    tm = _tile(S, 512)
    tx = _tile(S, 256)
    td = _tile(D, 512)
    saved = []
    h = x
    l = 0
    kv = c3 = f_pre = hn_kv = h_kv = None
    while True:
        W = layer_weights(l, "mix", h)
        if W is None:
            break
        recurrent = "w_rec_in" in W
        if l == 0:
            xn = _rmsnorm_fwd("mix_norm_0", h, gains["mix"][0])
        if recurrent:
            CH = W["w_rec_in"].shape[-1]
            C = 2 * CH
            proj = _mm(f"rec_in_{l}", "nn", xn, W["w_rec_in"], grid=(S // tm, N_CHIPS),
                       a_spec=pl.BlockSpec((tm, D), lambda i, j: (i, 0)),
                       b_spec=pl.BlockSpec((None, D, CH), lambda i, j: (j, 0, 0)),
                       out_shape=(S, 2 * C), out_dtype=F32,
                       out_spec=pl.BlockSpec((tm, CH), lambda i, j: (i, j)))
            layer_prefetch(l, "mix2", proj)
            rc, rcb = _conv_fwd(f"conv_{l}", proj, W["conv_w"], W["conv_b"])
            W = {**W, **layer_weights(l, "mix2", rcb)}
            gip, grp = _gates_fwd(f"gates_{l}", rcb, W["w_gates"], W["b_gates"])
            hrec, m = _lru_fwd(f"lru_{l}", proj, rc, gip, grp, W["lru_param"])
            layer_prefetch(l, "ffn", m)
            h_mid, hn = _mm_nn(f"rec_out_{l}", m, W["w_rec_out"], out_dtype=F32, res=h, tn=D, norm_gain=gains["ffn"][l])
            mix_saved = (xn, proj, rc, rcb, gip, grp, hrec, m)
        else:
            if "w_kv" in W:
                h_kv = h
                hn_kv = _rmsnorm_fwd("kv_norm", h, W["norm_kv"])
                kv = _mm_nn("kv_proj", hn_kv, W["w_kv"], out_dtype=BF16, tm=1024, tn=1024)
                f_pre = _mm_nn("f_proj", hn_kv, W["w_f"], out_dtype=F32, bias=W["b_f"])
                c = _cumsum_rows("c_cumsum", _logsig_fwd("logsig", f_pre), False)
                c3 = (-c[:, :2 * HP]).reshape(S, HP, 2).transpose(1, 0, 2)
            q = _mm_nn(f"q_proj_{l}", xn, W["w_q"], out_dtype=BF16, scale=scale, tm=1024, tn=1024)
            layer_prefetch(l, "mix2", q)
            o, of, lse = _attn_fwd(f"attn_fwd_{l}", q, kv, c3)
            W = {**W, **layer_weights(l, "mix2", o)}
            layer_prefetch(l, "ffn", o)
            h_mid, hn = _mm_nn(f"o_proj_{l}", o, W["w_o"], out_dtype=F32, res=h, tn=D, norm_gain=gains["ffn"][l])
            mix_saved = (xn, q, o, of, lse)
        W = {**W, **layer_weights(l, "ffn", h_mid)}
        z3, act = _swiglu_fwd(f"ffn_in_{l}", hn, W["w_ffn_in"])
        layer_prefetch(l + 1, "mix", act)
        saved.append((W, h, h_mid, mix_saved, (hn, z3, act)))
        l += 1
        if l < len(gains["mix"]):
            h, xn = _mm_nn(f"ffn_out_{l - 1}", act, W["w_ffn_out"], out_dtype=F32, res=h_mid, tn=D,
                           norm_gain=gains["mix"][l])
        else:
            h = _mm_nn(f"ffn_out_{l - 1}", act, W["w_ffn_out"], out_dtype=F32, res=h_mid, tn=D)

    dh, dhb, dg_final, loss_row = _loss_head("loss_head", h, target, gains["final"])

    dk_parts, dv_parts, dc_parts = [], [], []
    token = None
    for l in reversed(range(len(saved))):
        W, h_in, h_mid, mix_saved, (hn, z3, act) = saved[l]
        recurrent = "w_rec_in" in W
        FH = W["w_ffn_in"].shape[-1]
        G = {}
        norm_ffn = gains["ffn"][l]
        if token is not None:
            norm_ffn = norm_ffn + jnp.minimum(token[:1, :1], 0.0)
        G["w_ffn_out"] = _mm_tn(f"d_ffn_out_{l}", act, dhb, out_dtype=BF16, tn=D)
        dz3 = _swiglu_bwd(f"d_act_{l}", dhb, W["w_ffn_out"], z3)
        G["w_ffn_in"] = _mm(
            f"d_ffn_in_{l}", "tn", hn, dz3, grid=(D // td, N_CHIPS),
            a_spec=pl.BlockSpec((S, td), lambda i, j: (0, i)),
            b_spec=pl.BlockSpec((None, S, FH), lambda i, j: (j // 2, 0, j % 2)),
            out_shape=(N_CHIPS, D, FH), out_dtype=BF16,
            out_spec=pl.BlockSpec((None, td, FH), lambda i, j: (j, i, 0)))
        ffn_token = layer_grads(l, "ffn", G)
        G = {}
        if ffn_token is not None:
            norm_ffn = norm_ffn + jnp.minimum(ffn_token[:1, :1], 0.0)
        dh, dhb, dgp = _mm(f"d_ffn_hn_{l}", "nt", dz3, W["w_ffn_in"], grid=(S // tx, 1),
                           a_spec=[pl.BlockSpec((None, tx, FH), functools.partial(lambda i, j, k: (k // 2, i, k % 2), k=k))
                                   for k in range(N_CHIPS)],
                           b_spec=[pl.BlockSpec((None, D, FH), functools.partial(lambda i, j, k: (k, 0, 0), k=k))
                                   for k in range(N_CHIPS)],
                           out_shape=(S, D), out_dtype=F32, out_spec=pl.BlockSpec((tx, D), lambda i, j: (i, 0)),
                           norm_bwd=(h_mid, norm_ffn, dh))
        G["norm_ffn"] = jnp.sum(dgp, axis=0)
        if recurrent:
            CH = W["w_rec_in"].shape[-1]
            C = 2 * CH
            xn, proj, rc, rcb, gip, grp, hrec, m = mix_saved
            G["w_rec_out"] = _mm_tn(f"d_rec_out_{l}", m, dhb, out_dtype=BF16, tn=D)
            dm = _mm_nt(f"d_m_{l}", dhb, W["w_rec_out"], out_dtype=F32, tn=C)
            dgb, dgi, dgr, drc1, G["b_gi"], G["b_gr"], G["lru_param"] = _lru_bwd(
                f"d_lru_{l}", dm, proj, hrec, rc, gip, grp, W["lru_param"])
            drc, G["w_gates"] = _gates_bwd(f"d_gates_{l}", dgi, dgr, rcb, W["w_gates"], drc1)
            mix_token = layer_grads(l, "mix2", {n: G[n] for n in ("w_rec_out", "w_gates")})
            drec, G["conv_w"], G["conv_b"] = _conv_bwd(f"d_conv_{l}", drc, proj, W["conv_w"])
            dproj = jnp.concatenate([dgb, drec], axis=1)
            norm_mix = gains["mix"][l] if mix_token is None else gains["mix"][l] + jnp.minimum(mix_token[:1, :1], 0.0)
            G["w_rec_in"] = _mm(
                f"d_rec_in_{l}", "tn", xn, dproj, grid=(1, N_CHIPS),
                a_spec=pl.BlockSpec((S, D), lambda i, j: (0, 0)),
                b_spec=pl.BlockSpec((S, CH), lambda i, j: (0, j)),
                out_shape=(N_CHIPS, D, CH), out_dtype=BF16,
                out_spec=pl.BlockSpec((None, D, CH), lambda i, j: (j, 0, 0)))
            dh, dhb, dgp = _mm(f"d_rec_xn_{l}", "nt", dproj, W["w_rec_in"], grid=(S // tx, 1),
                               a_spec=[pl.BlockSpec((tx, CH), functools.partial(lambda i, j, k: (i, k), k=k))
                                       for k in range(N_CHIPS)],
                               b_spec=[pl.BlockSpec((None, D, CH), functools.partial(lambda i, j, k: (k, 0, 0), k=k))
                                       for k in range(N_CHIPS)],
                               out_shape=(S, D), out_dtype=F32, out_spec=pl.BlockSpec((tx, D), lambda i, j: (i, 0)),
                               norm_bwd=(h_in, norm_mix, dh))
        else:
            xn, q, o, of, lse = mix_saved
            G["w_o"] = _mm_tn(f"d_o_proj_{l}", o, dhb, out_dtype=BF16, tn=D)
            do = _mm_nt(f"d_o_{l}", dhb, W["w_o"], out_dtype=BF16, tm=1024, tn=D)
            mix_token = layer_grads(l, "mix2", {"w_o": G["w_o"]})
            dq, dk, dv, dck, drq = _attn_bwd(f"attn_bwd_{l}", q, kv, c3, of, do, lse)
            dk_parts.append(dk)
            dv_parts.append(dv)
            dc_parts.append((dck + drq).reshape(2 * HP, S).T)
            G["w_q"] = _mm_tn(f"d_q_proj_{l}", xn, dq, out_dtype=BF16, tn=D)
            norm_mix = gains["mix"][l] if mix_token is None else gains["mix"][l] + jnp.minimum(mix_token[:1, :1], 0.0)
            dh, dhb, dgp = _mm_nt(f"d_q_xn_{l}", dq, W["w_q"], out_dtype=F32, tn=D, norm_bwd=(h_in, norm_mix, dh))
        G["norm_mix"] = jnp.sum(dgp, axis=0)
        if "w_kv" in W:
            dkb = _add_cast("dk_sum", dk_parts, BF16)
            dvb = _add_cast("dv_sum", dv_parts, BF16)
            dkv = jnp.concatenate([dkb, dvb], axis=1)
            dc = sum(dc_parts[1:], dc_parts[0])
            dc_pad = jnp.pad(dc, ((0, 0), (0, LANES - 2 * HP)))
            dls = _cumsum_rows("dc_cumsum", dc_pad, True)
            dfb, G["b_f"] = _logsig_bwd("d_logsig", dls, f_pre)
            G["w_kv"] = _mm_tn("d_kv_proj", hn_kv, dkv, out_dtype=BF16, tn=1024)
            G["w_f"] = _mm_tn("d_f_proj", hn_kv, dfb, out_dtype=F32)
            dhn_f = _mm_nt("d_f_hn", dfb, W["w_f"], out_dtype=F32, tn=D)
            dh, dhb, dgp = _mm_nt("d_kv_hn", dkv, W["w_kv"], out_dtype=F32, tn=D, res=dhn_f,
                                  norm_bwd=(h_kv, W["norm_kv"], dh))
            G["norm_kv"] = jnp.sum(dgp, axis=0)
        token = layer_grads(l, "mix", G)
    return loss_row, dh, dg_final


_ANY = pl.BlockSpec(memory_space=pl.ANY)


def _position():
    return lax.axis_index("x"), lax.axis_index("y"), lax.axis_index("c")


def _chip_peers(x, y):
    return [(1 - x, y), (x, 1 - y), (1 - x, 1 - y)]


def _half_rows(c, n):
    h = n // 2
    assert h % 16 == 0
    return pl.ds(pl.multiple_of(c * h, 16), h)


def _place_own(name, shard, layer, me):
    _, R, C = shard.shape
    tr = _row_tile(R, C, 2 * shard.dtype.itemsize, target=8 << 20)

    def body(me_ref, x_ref, o_ref):
        o_ref[...] = x_ref[...]

    return pl.pallas_call(
        body, name=name,
        grid_spec=pltpu.PrefetchScalarGridSpec(
            num_scalar_prefetch=1, grid=(R // tr,),
            in_specs=[pl.BlockSpec((None, tr, C), lambda i, me_ref: (layer, i, 0))],
            out_specs=pl.BlockSpec((None, tr, C), lambda i, me_ref: (me_ref[0], i, 0))),
        out_shape=_hbm_out((N_CHIPS, R, C), shard.dtype),
        compiler_params=_params(("parallel",)),
    )(me, shard)


def _gather_smalls(name, smalls):
    ns = len(smalls)

    def body(*refs):
        ins, outs = refs[:ns], refs[ns:2 * ns]
        send_sems, recv_sems, local_sems = refs[2 * ns:]
        x, y, c = _position()
        me = 2 * x + y
        peers = _chip_peers(x, y)

        def remote(t, k, chip):
            px, py = peers[k]
            return pltpu.make_async_remote_copy(
                src_ref=ins[t], dst_ref=outs[t].at[chip], send_sem=send_sems.at[3 * t + k],
                recv_sem=recv_sems.at[3 * t + k], device_id=(px, py, c), device_id_type=MESH)

        local = [pltpu.make_async_copy(ins[t], outs[t].at[me], local_sems.at[t]) for t in range(ns)]
        for t in range(ns):
            local[t].start()
            for k in range(3):
                remote(t, k, me).start()
        for t in range(ns):
            for k in range(3):
                px, py = peers[k]
                remote(t, k, 2 * px + py).wait_recv()
        for t in range(ns):
            for k in range(3):
                remote(t, k, me).wait_send()
            local[t].wait()

    return pl.pallas_call(
        body, name=name, in_specs=[_ANY] * ns, out_specs=[_ANY] * ns,
        out_shape=[_hbm_out((N_CHIPS,) + s.shape, s.dtype) for s in smalls],
        scratch_shapes=[pltpu.SemaphoreType.DMA((3 * ns,)), pltpu.SemaphoreType.DMA((3 * ns,)),
                        pltpu.SemaphoreType.DMA((ns,))],
    )(*smalls)


_SEM = pl.BlockSpec(memory_space=pltpu.SEMAPHORE)
_SPLIT = pltpu.CompilerParams(has_side_effects=pltpu.SideEffectType.DATAFLOW_SIDE_EFFECTING)


def _weight_copy(shards, buf, items, sems, i, k, chip_of_dst, peers, c):
    w, l = items[i]
    px, py = peers[k]
    half = _half_rows(c, shards[w].shape[1])
    return pltpu.make_async_remote_copy(
        src_ref=shards[w].at[l, half], dst_ref=buf.at[chip_of_dst, half],
        send_sem=sems[0].at[3 * i + k], recv_sem=sems[1].at[3 * i + k],
        device_id=(px, py, c), device_id_type=MESH)


def _gather_start(name, shards, bufs, items, after):
    nw, n = len(shards), len(bufs)

    def body(*refs):
        ins, outs, sems = refs[:nw], refs[nw + n + 1:nw + 2 * n + 1], refs[nw + 2 * n + 1:]
        x, y, c = _position()
        peers = _chip_peers(x, y)
        for i in range(n):
            for k in range(3):
                _weight_copy(ins, outs[i], items, sems, i, k, 2 * x + y, peers, c).start()

    res = pl.pallas_call(
        body, name=name, in_specs=[_ANY] * (nw + n + 1), out_specs=[_ANY] * n + [_SEM, _SEM],
        out_shape=[_hbm_out(b.shape, b.dtype) for b in bufs]
        + [pltpu.SemaphoreType.DMA((3 * n,)), pltpu.SemaphoreType.DMA((3 * n,))],
        input_output_aliases={nw + i: i for i in range(n)}, compiler_params=_SPLIT,
    )(*shards, *bufs, after)
    return res[:n], res[n:]


def _gather_wait(name, shards, bufs, items, ids, sems, after):
    nw, m = len(shards), len(ids)

    def body(*refs):
        ins, bs = refs[:nw], refs[nw:nw + m]
        sem_refs = refs[nw + m:nw + m + 2]
        x, y, c = _position()
        peers = _chip_peers(x, y)
        for j, i in enumerate(ids):
            for k in range(3):
                px, py = peers[k]
                _weight_copy(ins, bs[j], items, sem_refs, i, k, 2 * px + py, peers, c).wait_recv()
        for j, i in enumerate(ids):
            for k in range(3):
                _weight_copy(ins, bs[j], items, sem_refs, i, k, 2 * x + y, peers, c).wait_send()

    res = pl.pallas_call(
        body, name=name, in_specs=[_ANY] * (nw + m) + [_SEM, _SEM, _ANY], out_specs=[_ANY] * m,
        out_shape=[_hbm_out(bufs[i].shape, bufs[i].dtype) for i in ids],
        input_output_aliases={nw + j: j for j in range(m)}, compiler_params=_SPLIT,
    )(*shards, *[bufs[i] for i in ids], *sems, after)
    return list(res)


def _forward_copy(src, dst, sems, i, k, core):
    x, y, c = _position()
    px, py = _chip_peers(x, y)[k]
    half = _half_rows(core, src.shape[1])
    return pltpu.make_async_remote_copy(
        src_ref=src.at[2 * px + py, half], dst_ref=dst.at[2 * px + py, half],
        send_sem=sems[0].at[3 * i + k], recv_sem=sems[1].at[3 * i + k],
        device_id=(x, y, 1 - c), device_id_type=MESH)


def _forward_start(name, bufs):
    n = len(bufs)

    def body(*refs):
        ins, outs, sems = refs[:n], refs[n:2 * n], refs[2 * n:]
        c = lax.axis_index("c")
        for i in range(n):
            for k in range(3):
                _forward_copy(ins[i], outs[i], sems, i, k, c).start()

    res = pl.pallas_call(
        body, name=name, in_specs=[_ANY] * n, out_specs=[_ANY] * n + [_SEM, _SEM],
        out_shape=[_hbm_out(g.shape, g.dtype) for g in bufs]
        + [pltpu.SemaphoreType.DMA((3 * n,)), pltpu.SemaphoreType.DMA((3 * n,))],
        input_output_aliases={i: i for i in range(n)}, compiler_params=_SPLIT,
    )(*bufs)
    return list(res[:n]), res[n:]


def _forward_wait(name, bufs, sems, after):
    n = len(bufs)

    def body(*refs):
        bs, sem_refs = refs[:n], refs[n:n + 2]
        c = lax.axis_index("c")
        for i in range(n):
            for k in range(3):
                _forward_copy(bs[i], bs[i], sem_refs, i, k, 1 - c).wait_recv()
        for i in range(n):
            for k in range(3):
                _forward_copy(bs[i], bs[i], sem_refs, i, k, c).wait_send()

    return list(pl.pallas_call(
        body, name=name, in_specs=[_ANY] * n + [_SEM, _SEM, _ANY], out_specs=[_ANY] * n,
        out_shape=[_hbm_out(g.shape, g.dtype) for g in bufs],
        input_output_aliases={i: i for i in range(n)}, compiler_params=_SPLIT,
    )(*bufs, *sems, after))


def _reduce_copy(grads, others, sems, i):
    x, y, c = _position()
    return pltpu.make_async_remote_copy(
        src_ref=grads[i].at[:, _half_rows(1 - c, grads[i].shape[1])], dst_ref=others[i],
        send_sem=sems[0].at[i], recv_sem=sems[1].at[i], device_id=(x, y, 1 - c), device_id_type=MESH)


def _reduce_start(name, grads, after):
    n = len(grads)

    def body(*refs):
        ins, outs, sems, token = refs[:n], refs[n + 1:2 * n + 1], refs[2 * n + 1:2 * n + 3], refs[2 * n + 3]
        for i in range(n):
            _reduce_copy(ins, outs, sems, i).start()
        token[...] = jnp.zeros_like(token)

    res = pl.pallas_call(
        body, name=name, in_specs=[_ANY] * (n + 1),
        out_specs=[_ANY] * n + [_SEM, _SEM, pl.BlockSpec(memory_space=pltpu.VMEM)],
        out_shape=[_hbm_out((N_CHIPS, g.shape[1] // 2, g.shape[2]), g.dtype) for g in grads]
        + [pltpu.SemaphoreType.DMA((n,)), pltpu.SemaphoreType.DMA((n,)), jax.ShapeDtypeStruct((SUBLANES, LANES), F32)],
        compiler_params=_SPLIT,
    )(*grads, after)
    return list(res[:n]), res[n:n + 2], res[n + 2]


def _reduce_wait(name, grads, others, sems, after):
    n = len(grads)

    def body(*refs):
        ins, os_, sem_refs = refs[:n], refs[n:2 * n], refs[2 * n:2 * n + 2]
        for i in range(n):
            _reduce_copy(ins, os_, sem_refs, i).wait_recv()
        for i in range(n):
            _reduce_copy(ins, os_, sem_refs, i).wait_send()

    return list(pl.pallas_call(
        body, name=name, in_specs=[_ANY] * (2 * n) + [_SEM, _SEM, _ANY], out_specs=[_ANY] * n,
        out_shape=[_hbm_out(o.shape, o.dtype) for o in others],
        input_output_aliases={n + i: i for i in range(n)}, compiler_params=_SPLIT,
    )(*grads, *others, *sems, after))


def _sum_cores(name, g, other, core):
    _, R, C = g.shape
    H = R // 2
    tr = _row_tile(H, C, 3 * 2, target=12 << 20)
    nb = H // tr

    def body(c_ref, g_ref, o_ref, out_ref):
        out_ref[...] = (g_ref[...].astype(F32) + o_ref[...].astype(F32)).astype(out_ref.dtype)

    return pl.pallas_call(
        body, name=name,
        grid_spec=pltpu.PrefetchScalarGridSpec(
            num_scalar_prefetch=1, grid=(N_CHIPS, nb),
            in_specs=[pl.BlockSpec((None, tr, C), lambda j, i, c_ref: (j, c_ref[0] * nb + i, 0)),
                      pl.BlockSpec((None, tr, C), lambda j, i, c_ref: (j, i, 0))],
            out_specs=pl.BlockSpec((None, tr, C), lambda j, i, c_ref: (j, i, 0))),
        out_shape=_hbm_out((N_CHIPS, H, C), BF16),
        compiler_params=_params(("parallel", "parallel")),
    )(core, g, other)


def _sum_chips(name, received, own, full, layer, me_core):
    _, H, C = received.shape
    tr = _row_tile(H, C, 3 * 2 + 2 + 4, target=12 << 20)
    nb = H // tr

    def body(s_ref, r_ref, own_ref, full_ref, out_ref):
        acc = r_ref[0].astype(F32)
        for k in (1, 2):
            acc = acc + r_ref[k].astype(F32)
        out_ref[...] = acc + own_ref[...].astype(F32)

    return pl.pallas_call(
        body, name=name,
        grid_spec=pltpu.PrefetchScalarGridSpec(
            num_scalar_prefetch=1, grid=(nb,),
            in_specs=[pl.BlockSpec((3, tr, C), lambda i, s_ref: (0, i, 0)),
                      pl.BlockSpec((None, tr, C), lambda i, s_ref: (s_ref[0], i, 0)),
                      _ANY],
            out_specs=pl.BlockSpec((None, tr, C), lambda i, s_ref: (layer, s_ref[1] * nb + i, 0))),
        out_shape=_hbm_out(full.shape, full.dtype),
        input_output_aliases={3: 0},
        compiler_params=_params(("parallel",)),
    )(me_core, received, own, full)


def _part_copy(parts, recv, sems, i, k, peers, c):
    px, py = peers[k]
    return pltpu.make_async_remote_copy(
        src_ref=parts[i].at[2 * px + py], dst_ref=recv[i].at[k],
        send_sem=sems[0].at[3 * i + k], recv_sem=sems[1].at[3 * i + k],
        device_id=(px, py, c), device_id_type=MESH)


def _scatter_start(name, parts):
    n = len(parts)

    def body(*refs):
        ins, outs, sems, token = refs[:n], refs[n:2 * n], refs[2 * n:2 * n + 2], refs[2 * n + 2]
        x, y, c = _position()
        peers = _chip_peers(x, y)
        for i in range(n):
            for k in range(3):
                _part_copy(ins, outs, sems, i, k, peers, c).start()
        token[...] = jnp.zeros_like(token)

    res = pl.pallas_call(
        body, name=name, in_specs=[_ANY] * n,
        out_specs=[_ANY] * n + [_SEM, _SEM, pl.BlockSpec(memory_space=pltpu.VMEM)],
        out_shape=[_hbm_out((3,) + p.shape[1:], p.dtype) for p in parts]
        + [pltpu.SemaphoreType.DMA((3 * n,)), pltpu.SemaphoreType.DMA((3 * n,)),
           jax.ShapeDtypeStruct((SUBLANES, LANES), F32)],
        compiler_params=_SPLIT,
    )(*parts)
    return list(res[:n]), res[n:n + 2], res[n + 2]


def _scatter_wait(name, parts, recv, sems):
    n = len(parts)

    def body(*refs):
        ins, rs, sem_refs = refs[:n], refs[n:2 * n], refs[2 * n:2 * n + 2]
        x, y, c = _position()
        peers = _chip_peers(x, y)
        for i in range(n):
            for k in range(3):
                _part_copy(ins, rs, sem_refs, i, k, peers, c).wait_recv()
        for i in range(n):
            for k in range(3):
                _part_copy(ins, rs, sem_refs, i, k, peers, c).wait_send()

    return list(pl.pallas_call(
        body, name=name, in_specs=[_ANY] * (2 * n) + [_SEM, _SEM], out_specs=[_ANY] * n,
        out_shape=[_hbm_out(r.shape, r.dtype) for r in recv],
        input_output_aliases={n + i: i for i in range(n)}, compiler_params=_SPLIT,
    )(*parts, *recv, *sems))


def _share_d2d(name, full):
    n = len(full)

    def body(*refs):
        ins, outs = refs[:n], refs[n:2 * n]
        send_sems, recv_sems = refs[2 * n:]
        x, y, c = _position()

        def remote(w, core):
            half = _half_rows(core, ins[w].shape[1])
            return pltpu.make_async_remote_copy(
                src_ref=ins[w].at[:, half], dst_ref=outs[w].at[:, half],
                send_sem=send_sems.at[w], recv_sem=recv_sems.at[w],
                device_id=(x, y, 1 - c), device_id_type=MESH)

        for w in range(n):
            remote(w, c).start()
        for w in range(n):
            remote(w, 1 - c).wait_recv()
        for w in range(n):
            remote(w, c).wait_send()

    return pl.pallas_call(
        body, name=name, in_specs=[_ANY] * n, out_specs=[_ANY] * n,
        out_shape=[_hbm_out(f.shape, f.dtype) for f in full],
        input_output_aliases={w: w for w in range(n)},
        scratch_shapes=[pltpu.SemaphoreType.DMA((n,)), pltpu.SemaphoreType.DMA((n,))],
    )(*full)


def _gather_all(name, a):
    def body(a_ref, o_ref, send_sems, recv_sems, local_sem):
        x, y, c = _position()
        me = 4 * x + 2 * y + c

        def peer(k):
            return (x ^ ((k >> 2) & 1), y ^ ((k >> 1) & 1), c ^ (k & 1))

        def remote(k, slot):
            return pltpu.make_async_remote_copy(
                src_ref=a_ref, dst_ref=o_ref.at[slot], send_sem=send_sems.at[k - 1], recv_sem=recv_sems.at[k - 1],
                device_id=peer(k), device_id_type=MESH)

        local = pltpu.make_async_copy(a_ref, o_ref.at[me], local_sem)
        local.start()
        for k in range(1, N_DEV):
            remote(k, me).start()
        for k in range(1, N_DEV):
            px, py, pc = peer(k)
            remote(k, 4 * px + 2 * py + pc).wait_recv()
        for k in range(1, N_DEV):
            remote(k, me).wait_send()
        local.wait()

    return pl.pallas_call(
        body, name=name, in_specs=[_ANY], out_specs=_ANY,
        out_shape=_hbm_out((N_DEV,) + a.shape, a.dtype),
        scratch_shapes=[pltpu.SemaphoreType.DMA((N_DEV - 1,)), pltpu.SemaphoreType.DMA((N_DEV - 1,)),
                        pltpu.SemaphoreType.DMA],
    )(a)


def _rows2d(a, lead=0):
    return a.reshape(a.shape[:lead] + (-1, a.shape[-1]))


def _row_tile(rows, cols, itemsize=4, target=1 << 20):
    want = max(SUBLANES, target // (cols * itemsize))
    t = min(rows, (want // 16) * 16)
    while t > 16 and rows % t:
        t -= 16
    return t if rows % t == 0 else rows


def _sum_slots(name, r, out_dtype=F32):
    ns = r.shape[0]
    r2 = _rows2d(r, 1)
    _, rows, cols = r2.shape
    tr = _row_tile(rows, cols)

    def body(r_ref, o_ref):
        acc = r_ref[0].astype(F32)
        for s in range(1, ns):
            acc = acc + r_ref[s].astype(F32)
        o_ref[...] = acc.astype(o_ref.dtype)

    out = pl.pallas_call(
        body, name=name, grid=(rows // tr,),
        in_specs=[pl.BlockSpec((ns, tr, cols), lambda i: (0, i, 0))],
        out_specs=pl.BlockSpec((tr, cols), lambda i: (i, 0)),
        out_shape=_hbm_out((rows, cols), out_dtype),
        compiler_params=_params(("parallel",)),
    )(r2)
    return out.reshape(r.shape[1:])


def _adamw(name, g_parts, w, m, v):
    shape = w.shape
    ng = len(g_parts)
    args = [_rows2d(a) for a in (*g_parts, w, m, v)]
    rows, cols = args[0].shape
    tr = _row_tile(rows, cols, (ng + 7) * 4, target=16 << 20)
    c1 = 1.0 - ADAM_B1 ** ADAM_STEP
    c2 = 1.0 - ADAM_B2 ** ADAM_STEP

    def body(*refs):
        g = refs[0][...]
        for r in refs[1:ng]:
            g = g + r[...]
        w_ref, m_ref, v_ref = refs[ng:ng + 3]
        g_out, d_out, m_out, v_out = refs[ng + 3:]
        mn = ADAM_B1 * m_ref[...] + (1.0 - ADAM_B1) * g
        vn = ADAM_B2 * v_ref[...] + (1.0 - ADAM_B2) * (g * g)
        m_hat = mn / c1
        v_hat = vn / c2
        g_out[...] = g
        d_out[...] = -ADAM_LR * (m_hat / (jnp.sqrt(v_hat) + ADAM_EPS) + ADAM_WD * w_ref[...])
        m_out[...] = mn
        v_out[...] = vn

    spec = pl.BlockSpec((tr, cols), lambda i: (i, 0))
    outs = pl.pallas_call(
        body, name=name, grid=(rows // tr,), in_specs=[spec] * (ng + 3), out_specs=[spec] * 4,
        out_shape=[_hbm_out((rows, cols), F32)] * 4,
        compiler_params=_params(("parallel",)),
    )(*args)
    return tuple(o.reshape(shape) for o in outs)


_WEIGHTS = ["norm_mix", "norm_ffn", "w_ffn_in", "w_ffn_out", "w_rec_in", "conv_w", "conv_b", "w_lru_gates",
            "b_lru_gates", "lru_param", "w_rec_out", "norm_kv", "w_kvf", "b_forget", "w_q", "w_o", "norm_final"]
_BIG = ["w_ffn_in", "w_ffn_out", "w_rec_in", "w_lru_gates", "w_rec_out", "w_kvf", "w_q", "w_o"]


def _stack3(a):
    return a[None] if a.ndim == 2 else a.reshape(a.shape[0], -1, a.shape[-1])


def _pad_lanes(a, n):
    return jnp.pad(a, ((0, 0),) * (a.ndim - 1) + ((0, n - a.shape[-1]),))


def kernel(x, norm_mix, norm_ffn, w_ffn_in, w_ffn_out, w_rec_in, conv_w, conv_b, w_lru_gates, b_lru_gates, lru_param, w_rec_out, norm_kv, w_kvf, b_forget, w_q, w_o, norm_final, loss_target, m_norm_mix, m_norm_ffn, m_w_ffn_in, m_w_ffn_out, m_w_rec_in, m_conv_w, m_conv_b, m_w_lru_gates, m_b_lru_gates, m_lru_param, m_w_rec_out, m_norm_kv, m_w_kvf, m_b_forget, m_w_q, m_w_o, m_norm_final, v_norm_mix, v_norm_ffn, v_w_ffn_in, v_w_ffn_out, v_w_rec_in, v_conv_w, v_conv_b, v_w_lru_gates, v_b_lru_gates, v_lru_param, v_w_rec_out, v_norm_kv, v_w_kvf, v_b_forget, v_w_q, v_w_o, v_norm_final):
    P = dict(norm_mix=norm_mix, norm_ffn=norm_ffn, w_ffn_in=w_ffn_in, w_ffn_out=w_ffn_out, w_rec_in=w_rec_in,
             conv_w=conv_w, conv_b=conv_b, w_lru_gates=w_lru_gates, b_lru_gates=b_lru_gates, lru_param=lru_param,
             w_rec_out=w_rec_out, norm_kv=norm_kv, w_kvf=w_kvf, b_forget=b_forget, w_q=w_q, w_o=w_o,
             norm_final=norm_final)
    M1 = dict(norm_mix=m_norm_mix, norm_ffn=m_norm_ffn, w_ffn_in=m_w_ffn_in, w_ffn_out=m_w_ffn_out,
              w_rec_in=m_w_rec_in, conv_w=m_conv_w, conv_b=m_conv_b, w_lru_gates=m_w_lru_gates,
              b_lru_gates=m_b_lru_gates, lru_param=m_lru_param, w_rec_out=m_w_rec_out, norm_kv=m_norm_kv,
              w_kvf=m_w_kvf, b_forget=m_b_forget, w_q=m_w_q, w_o=m_w_o, norm_final=m_norm_final)
    M2 = dict(norm_mix=v_norm_mix, norm_ffn=v_norm_ffn, w_ffn_in=v_w_ffn_in, w_ffn_out=v_w_ffn_out,
              w_rec_in=v_w_rec_in, conv_w=v_conv_w, conv_b=v_conv_b, w_lru_gates=v_w_lru_gates,
              b_lru_gates=v_b_lru_gates, lru_param=v_lru_param, w_rec_out=v_w_rec_out, norm_kv=v_norm_kv,
              w_kvf=v_w_kvf, b_forget=v_b_forget, w_q=v_w_q, w_o=v_w_o, norm_final=v_norm_final)

    _, S, D = x.shape
    L = norm_mix.shape[0]
    NA, NBLK, BW, GS = w_lru_gates.shape
    C = NBLK * BW
    CS = C // N_CHIPS
    H = b_forget.shape[0]
    assert C == D and H * HEAD_DIM == D and H <= LANES
    chip = 2 * lax.axis_index("x") + lax.axis_index("y")

    small_a = jnp.concatenate([conv_w, conv_b[:, None], lru_param[:, None]], axis=1)
    small_a, b_gates = _gather_smalls("gather_smalls", [small_a, b_lru_gates])
    small_a = small_a.transpose(1, 2, 0, 3).reshape(NA, 6, C)
    b_gates = b_gates.transpose(1, 2, 0, 3).reshape(NA, NBLK, 1, N_CHIPS * GS)
    shards = [_stack3(P[w]).astype(BF16) for w in _BIG]
    core = lax.axis_index("c")
    chip_id = jnp.reshape(chip, (1,)).astype(jnp.int32)
    core_id = jnp.reshape(core, (1,)).astype(jnp.int32)
    me_core = jnp.stack([chip, core]).astype(jnp.int32)

    parts_of_layer = ("mix", "mix2", "ffn")

    def part_items(l, part):
        if part == "ffn":
            names, at = ["w_ffn_in", "w_ffn_out"], l
        elif l < NA:
            names, at = (["w_rec_in"] if part == "mix" else ["w_lru_gates", "w_rec_out"]), l
        else:
            names, at = ((["w_kvf"] if l == NA else []) + ["w_q"] if part == "mix" else ["w_o"]), l - NA
        return [(_BIG.index(n), 0 if n == "w_kvf" else at) for n in names]

    def stage_of(l, part):
        return (l, part) if l == 0 or part == "ffn" else (l, "mixer")

    def stage_items(st):
        l, part = st
        return [it for p in (("mix", "mix2") if part == "mixer" else (part,)) for it in part_items(l, p)]

    stages = [(0, p) for p in parts_of_layer] + [(l, p) for l in range(1, L) for p in ("mixer", "ffn")]
    items = [it for st in stages for it in stage_items(st)]
    ids_of = {st: [items.index(it) for it in stage_items(st)] for st in stages}
    bufs = [_place_own(f"place_{_BIG[w]}_{li}", shards[w], li, chip_id) for w, li in items]
    bufs, gather_sems = _gather_start("gather_start", shards, bufs, items, small_a)

    forwarding, fetched = {}, {}

    def layer_prefetch(l, part, after):
        st = stage_of(l, part)
        if l < L and st not in forwarding:
            got = _gather_wait(f"gather_wait_{st[1]}_{l}", shards, bufs, items, ids_of[st], gather_sems, after)
            forwarding[st] = _forward_start(f"forward_start_{st[1]}_{l}", got)

    def layer_weights(l, part, after):
        if l >= L:
            return None
        st = stage_of(l, part)
        if st not in fetched:
            layer_prefetch(l, part, after)
            got, sems = forwarding[st]
            got = _forward_wait(f"forward_wait_{st[1]}_{l}", got, sems, after)
            fetched[st] = {_BIG[items[i][0]]: g for i, g in zip(ids_of[st], got)}
        B = fetched[st]
        if part == "ffn":
            return dict(w_ffn_in=B["w_ffn_in"], w_ffn_out=B["w_ffn_out"].reshape(-1, D))
        if l < NA and part == "mix":
            return dict(w_rec_in=B["w_rec_in"], conv_w=small_a[l, :4], conv_b=small_a[l, 4:5])
        if l < NA:
            return dict(w_gates=B["w_lru_gates"].reshape(N_CHIPS, NBLK, BW, GS).transpose(1, 2, 0, 3).reshape(
                NBLK, BW, N_CHIPS * GS), b_gates=b_gates[l], w_rec_out=B["w_rec_out"].reshape(C, D),
                lru_param=small_a[l, 5:6])
        if part == "mix2":
            return dict(w_o=B["w_o"].reshape(D, D))
        W = dict(w_q=B["w_q"].reshape(D, D))
        if l == NA:
            w_kvf_full = B["w_kvf"].transpose(1, 0, 2).reshape(D, -1)
            W.update(norm_kv=norm_kv[None], w_kv=w_kvf_full[:, :2 * D],
                     w_f=_pad_lanes(w_kvf_full[:, 2 * D:], LANES), b_f=_pad_lanes(b_forget[None], LANES))
        return W

    G_small = {l: {} for l in range(L)}
    stash = {st: {} for st in stages}
    pending = {}
    reducing = []

    def finish_reduce(after):
        st, its, grads, others, sems = reducing.pop()
        l, part = st
        others = _reduce_wait(f"reduce_wait_{part}_{l}", grads, others, sems, after)
        parts = [_sum_cores(f"sum_cores_{l}_{_BIG[w]}", g, o, core_id) for (w, _), g, o in zip(its, grads, others)]
        recv, sems, token = _scatter_start(f"scatter_start_{part}_{l}", parts)
        pending[st] = (parts, recv, sems)
        return token

    def layer_grads(l, part, G_part):
        G_small[l].update(G_part)
        st = stage_of(l, part)
        stash[st].update(G_part)
        if st[1] == "mixer" and part != "mix":
            return None
        G = stash[st]
        late = {"ffn": "w_ffn_in", "mix": "norm_mix"}.get(part) or ("w_gates" if l < NA else "w_o")
        after = finish_reduce(G_part[late]) if reducing else jnp.zeros((SUBLANES, LANES), F32)
        by_name = dict(
            w_ffn_in=lambda: G["w_ffn_in"], w_ffn_out=lambda: G["w_ffn_out"].reshape(N_CHIPS, -1, D),
            w_rec_in=lambda: G["w_rec_in"],
            w_lru_gates=lambda: G["w_gates"].reshape(NBLK, BW, N_CHIPS, GS).transpose(2, 0, 1, 3).reshape(
                N_CHIPS, NBLK * BW, GS),
            w_rec_out=lambda: G["w_rec_out"].reshape(N_CHIPS, -1, D),
            w_kvf=lambda: jnp.concatenate([G["w_kv"].astype(F32), G["w_f"][:, :H]], axis=1).reshape(
                D, N_CHIPS, -1).transpose(1, 0, 2).astype(BF16),
            w_q=lambda: G["w_q"].reshape(N_CHIPS, -1, D), w_o=lambda: G["w_o"].reshape(N_CHIPS, -1, D))
        its = stage_items(st)
        grads = [by_name[_BIG[w]]() for w, _ in its]
        others, sems, token = _reduce_start(f"reduce_start_{st[1]}_{l}", grads, after)
        reducing.append((st, its, grads, others, sems))
        return finish_reduce(token) if l == 0 else token

    gains = dict(mix=[norm_mix[l][None] for l in range(L)], ffn=[norm_ffn[l][None] for l in range(L)],
                 final=norm_final[None])
    loss_row, grad_x, dg_final = _local_step(x.reshape(S, D), loss_target.reshape(S, D), gains,
                                             layer_weights, layer_prefetch, layer_grads)

    rows = [*[G_small[l]["norm_mix"] for l in range(L)], *[G_small[l]["norm_ffn"] for l in range(L)],
            G_small[NA]["norm_kv"], dg_final, _pad_lanes(G_small[NA]["b_f"], D), _pad_lanes(loss_row, D)]
    for a in range(NA):
        rows += [G_small[a][n] for n in ("conv_w", "conv_b", "b_gi", "b_gr", "lru_param")]
    packed = jnp.concatenate(rows, axis=0)
    tot = _sum_slots("sum_small", _gather_all("gather_small", packed))
    loss = tot[2 * L + 3, 0]
    g_rep = jnp.concatenate([tot[:2 * L + 2], tot[2 * L + 2:2 * L + 3]], axis=0)
    base = 2 * L + 4
    g_sh = []
    for a in range(NA):
        blk = lax.dynamic_slice_in_dim(tot[base + 8 * a:base + 8 * a + 8], chip * CS, CS, axis=1)
        gi = tot[base + 8 * a + 5].reshape(NBLK, BW)
        gr = tot[base + 8 * a + 6].reshape(NBLK, BW)
        bl = lax.dynamic_slice_in_dim(jnp.concatenate([gi, gr], axis=1), chip * GS, GS, axis=1)
        g_sh += [blk[:5], bl.reshape(-1, CS), blk[7:8]]
    g_sh = jnp.concatenate(g_sh, axis=0)
    nrow = g_sh.shape[0] // NA

    def pack_rep(T):
        return jnp.concatenate([T["norm_mix"], T["norm_ffn"], T["norm_kv"][None], T["norm_final"][None],
                                _pad_lanes(T["b_forget"][None], D)], axis=0)

    def pack_sh(T):
        return jnp.concatenate([jnp.concatenate([T["conv_w"][a], T["conv_b"][a][None],
                                                 T["b_lru_gates"][a].reshape(-1, CS), T["lru_param"][a][None]], axis=0)
                                for a in range(NA)], axis=0)

    rep = _adamw("adamw_replicated", [g_rep], pack_rep(P), pack_rep(M1), pack_rep(M2))
    shd = _adamw("adamw_small_sharded", [g_sh], pack_sh(P), pack_sh(M1), pack_sh(M2))

    def unpack_rep(t):
        return dict(norm_mix=t[:L], norm_ffn=t[L:2 * L], norm_kv=t[2 * L], norm_final=t[2 * L + 1],
                    b_forget=t[2 * L + 2, :H])

    def unpack_sh(t):
        t = t.reshape(NA, nrow, CS)
        return dict(conv_w=t[:, :4], conv_b=t[:, 4], b_lru_gates=t[:, 5:nrow - 1].reshape(NA, NBLK, GS),
                    lru_param=t[:, nrow - 1])

    full = [lax.empty(sh.shape, F32) for sh in shards]
    for st in reversed(stages):
        l, part = st
        parts, recv, sems = pending[st]
        recv = _scatter_wait(f"scatter_wait_{part}_{l}", parts, recv, sems)
        for (w, li), own, r in zip(stage_items(st), parts, recv):
            full[w] = _sum_chips(f"sum_chips_{l}_{_BIG[w]}", r, own, full[w], li, me_core)
    full = _share_d2d("share_d2d", full)
    big = {w: _adamw(f"adamw_{w}", [g.reshape(P[w].shape)], P[w], M1[w], M2[w]) for w, g in zip(_BIG, full)}

    outs = []
    for i in range(4):
        small = {**unpack_rep(rep[i]), **unpack_sh(shd[i])}
        outs.append([big[w][i] if w in big else small[w] for w in _WEIGHTS])
    return (loss, grad_x.reshape(1, S, D), *outs[0], *outs[1], *outs[2], *outs[3])
```

```python
import functools
import math

import jax
import jax.numpy as jnp
from jax import lax
from jax.experimental import pallas as pl
from jax.experimental.pallas import tpu as pltpu

F32 = jnp.float32
BF16 = jnp.bfloat16

EPS = 1e-6
LRU_C = 8.0
HEAD_DIM = 64
LANES = 128
SUBLANES = 8
VMEM_LIMIT = 48 * 1024 * 1024
N_CHIPS = 4
N_DEV = 8

ADAM_LR = 0.001
ADAM_B1 = 0.9
ADAM_B2 = 0.999
ADAM_EPS = 1e-08
ADAM_WD = 0.01
ADAM_STEP = 10

_NN = (((1,), (0,)), ((), ()))
_NT = (((1,), (1,)), ((), ()))
_TN = (((0,), (0,)), ((), ()))
_DN = {"nn": _NN, "nt": _NT, "tn": _TN}
MESH = pl.DeviceIdType.MESH


def _hbm_out(shape, dtype):
    return pltpu.HBM(shape, dtype)


def _params(sem):
    return pltpu.CompilerParams(dimension_semantics=sem, vmem_limit_bytes=VMEM_LIMIT)


def _tile(n, want):
    if n <= want:
        return n
    t = (want // LANES) * LANES
    while t >= LANES:
        if n % t == 0:
            return t
        t -= LANES
    return n


def _sigmoid(x):
    return 1.0 / (1.0 + jnp.exp(-x))


def _sigmoid_t(x):
    return 0.5 * jnp.tanh(0.5 * x) + 0.5


def _softplus(x):
    return jnp.maximum(x, 0.0) + jnp.log(1.0 + jnp.exp(-jnp.abs(x)))


_GELU_C = math.sqrt(2.0 / math.pi)


def _gelu_and_grad(x):
    inner = _GELU_C * (x + 0.044715 * x * x * x)
    t = jnp.tanh(inner)
    g = 0.5 * x * (1.0 + t)
    dg = 0.5 * (1.0 + t) + 0.5 * x * (1.0 - t * t) * _GELU_C * (1.0 + 3.0 * 0.044715 * x * x)
    return g, dg


def _rms(x):
    return lax.rsqrt(jnp.mean(x * x, axis=-1, keepdims=True) + EPS)


def _rms_bwd(dy, x, g):
    r = _rms(x)
    xr = x * r
    dyg = dy * g
    return r * dyg - xr * (r * jnp.mean(dyg * xr, axis=-1, keepdims=True)), jnp.sum(dy * xr, axis=0, keepdims=True)


def _mm(name, mode, a, b, *, grid, a_spec, b_spec, out_shape, out_dtype, out_spec, nk=1,
        res=None, res_spec=None, bias=None, bias_spec=None, scale=None, norm_gain=None, norm_bwd=None):
    dn = _DN[mode]
    has_res, has_bias = res is not None, bias is not None
    blk = tuple(d for d in out_spec.block_shape if d is not None)
    vec = pl.BlockSpec((1, blk[-1]), lambda *g: (0, 0))
    a_specs = a_spec if isinstance(a_spec, list) else [a_spec]
    b_specs = b_spec if isinstance(b_spec, list) else [b_spec]
    npair = len(a_specs)
    n_in = 2 * npair + int(has_res) + int(has_bias) + (1 if norm_gain is not None else 0) + (3 if norm_bwd else 0)

    def body(*refs):
        p = 2 * npair
        r_ref = refs[p] if has_res else None
        p += int(has_res)
        bias_ref = refs[p] if has_bias else None
        p += int(has_bias)
        extra = refs[p:n_in]
        outs = refs[n_in:]
        o_ref = outs[0]
        part = lax.dot_general(refs[0][...], refs[npair][...], dn, preferred_element_type=F32)
        for t in range(1, npair):
            part = part + lax.dot_general(refs[t][...], refs[npair + t][...], dn, preferred_element_type=F32)

        def finish(acc):
            if scale is not None:
                acc = acc * scale
            if has_bias:
                acc = acc + bias_ref[...]
            if has_res:
                acc = r_ref[...] + acc
            if norm_bwd:
                h_ref, g_ref, dh_ref = extra
                dx, dg = _rms_bwd(acc, h_ref[...], g_ref[...])
                acc = dh_ref[...] + dx
                outs[1][...] = acc.astype(BF16)
                outs[2][...] = dg
            if norm_gain is not None:
                outs[1][...] = (acc * _rms(acc) * extra[0][...]).astype(BF16)
            o_ref[...] = acc.astype(o_ref.dtype)

        if nk == 1:
            finish(part)
        else:
            acc_ref = refs[-1]
            k = pl.program_id(2)

            @pl.when(k == 0)
            def _():
                acc_ref[...] = part

            @pl.when(k > 0)
            def _():
                acc_ref[...] += part

            @pl.when(k == nk - 1)
            def _():
                finish(acc_ref[...])

    ins, specs = [a] * npair + [b] * npair, a_specs + b_specs
    if has_res:
        ins.append(res)
        specs.append(res_spec)
    if has_bias:
        ins.append(bias)
        specs.append(bias_spec)
    out_specs, out_shapes = [out_spec], [_hbm_out(out_shape, out_dtype)]
    if norm_gain is not None:
        ins.append(norm_gain)
        specs.append(vec)
        out_specs.append(out_spec)
        out_shapes.append(_hbm_out(out_shape, BF16))
    if norm_bwd:
        h, g, dh = norm_bwd
        ins += [h, g, dh]
        specs += [out_spec, vec, out_spec]
        out_specs += [out_spec, pl.BlockSpec((None, 1, blk[-1]), lambda i, *rest: (i, 0, 0))]
        out_shapes += [_hbm_out(out_shape, BF16), _hbm_out((grid[0], 1, blk[-1]), F32)]
    sem = ("parallel", "parallel") + (("arbitrary",) if len(grid) == 3 else ())
    single = len(out_specs) == 1
    return pl.pallas_call(
        body, name=name, grid=grid, in_specs=specs, out_specs=out_specs[0] if single else out_specs,
        out_shape=out_shapes[0] if single else out_shapes,
        scratch_shapes=[pltpu.VMEM(blk, F32)] if nk > 1 else [],
        compiler_params=_params(sem),
    )(*ins)


def _mm_nn(name, a, b, *, b_lead=(), out_dtype, tm=512, tn=512, res=None, bias=None, scale=None, norm_gain=None):
    M, K = a.shape
    N = b.shape[-1]
    tm, tn = _tile(M, tm), _tile(N, tn)
    nl = len(b_lead)
    return _mm(
        name, "nn", a, b, grid=(M // tm, N // tn),
        a_spec=pl.BlockSpec((tm, K), lambda i, j: (i, 0)),
        b_spec=pl.BlockSpec((None,) * nl + (K, tn), lambda i, j: tuple(b_lead) + (0, j)),
        out_shape=(M, N), out_dtype=out_dtype, out_spec=pl.BlockSpec((tm, tn), lambda i, j: (i, j)),
        res=res, res_spec=pl.BlockSpec((tm, tn), lambda i, j: (i, j)),
        bias=bias, bias_spec=pl.BlockSpec((1, tn), lambda i, j: (0, j)), scale=scale, norm_gain=norm_gain)


def _mm_nt(name, a, b, *, b_lead=(), out_dtype, tm=512, tn=512, tk=2048, res=None, norm_bwd=None):
    M, K = a.shape
    N = b.shape[-2]
    tm, tn, tk = _tile(M, tm), _tile(N, tn), _tile(K, tk)
    nk = K // tk
    nl = len(b_lead)
    return _mm(
        name, "nt", a, b, grid=(M // tm, N // tn, nk), nk=nk,
        a_spec=pl.BlockSpec((tm, tk), lambda i, j, k: (i, k)),
        b_spec=pl.BlockSpec((None,) * nl + (tn, tk), lambda i, j, k: tuple(b_lead) + (j, k)),
        out_shape=(M, N), out_dtype=out_dtype, out_spec=pl.BlockSpec((tm, tn), lambda i, j, k: (i, j)),
        res=res, res_spec=pl.BlockSpec((tm, tn), lambda i, j, k: (i, j)), norm_bwd=norm_bwd)


def _mm_tn(name, a, b, *, out_dtype, tm=512, tn=512):
    S, M = a.shape
    N = b.shape[1]
    tm, tn = _tile(M, tm), _tile(N, tn)
    return _mm(
        name, "tn", a, b, grid=(M // tm, N // tn),
        a_spec=pl.BlockSpec((S, tm), lambda i, j: (0, i)),
        b_spec=pl.BlockSpec((S, tn), lambda i, j: (0, j)),
        out_shape=(M, N), out_dtype=out_dtype, out_spec=pl.BlockSpec((tm, tn), lambda i, j: (i, j)))


def _rmsnorm_fwd(name, h, g, tr=256):
    S, D = h.shape
    tr = _tile(S, tr)

    def body(h_ref, g_ref, o_ref):
        x = h_ref[...]
        r = lax.rsqrt(jnp.mean(x * x, axis=-1, keepdims=True) + EPS)
        o_ref[...] = (x * r * g_ref[...]).astype(o_ref.dtype)

    return pl.pallas_call(
        body, name=name, grid=(S // tr,),
        in_specs=[pl.BlockSpec((tr, D), lambda i: (i, 0)), pl.BlockSpec((1, D), lambda i: (0, 0))],
        out_specs=pl.BlockSpec((tr, D), lambda i: (i, 0)),
        out_shape=_hbm_out((S, D), BF16),
        compiler_params=_params(("parallel",)),
    )(h, g)


def _loss_head(name, h, target, g, tr=256):
    S, D = h.shape
    tr = _tile(S, tr)

    def body(h_ref, t_ref, g_ref, o_ref, ob_ref, dg_ref, loss_ref):
        i = pl.program_id(0)
        x = h_ref[...]
        gg = g_ref[...]
        r = lax.rsqrt(jnp.mean(x * x, axis=-1, keepdims=True) + EPS)
        xr = x * r
        err = xr * gg - t_ref[...]
        lpart = 0.5 * jnp.sum(jnp.mean(err * err, axis=-1, keepdims=True), axis=0, keepdims=True)
        dy = err * (1.0 / D)
        dyg = dy * gg
        dx = r * dyg - xr * (r * jnp.mean(dyg * xr, axis=-1, keepdims=True))
        o_ref[...] = dx
        ob_ref[...] = dx.astype(BF16)
        part = jnp.sum(dy * xr, axis=0, keepdims=True)
        lrow = jnp.broadcast_to(lpart, (1, LANES))

        @pl.when(i == 0)
        def _():
            dg_ref[...] = part
            loss_ref[...] = lrow

        @pl.when(i > 0)
        def _():
            dg_ref[...] += part
            loss_ref[...] += lrow

    row = pl.BlockSpec((tr, D), lambda i: (i, 0))
    vec = pl.BlockSpec((1, D), lambda i: (0, 0))
    return pl.pallas_call(
        body, name=name, grid=(S // tr,),
        in_specs=[row, row, vec], out_specs=[row, row, vec, pl.BlockSpec((1, LANES), lambda i: (0, 0))],
        out_shape=[_hbm_out((S, D), F32), _hbm_out((S, D), BF16),
                   _hbm_out((1, D), F32), _hbm_out((1, LANES), F32)],
        compiler_params=_params(("arbitrary",)),
    )(h, target, g)


def _swiglu_fwd(name, hn, w_in, tm=512):
    S, D = hn.shape
    FH = w_in.shape[-1]
    tm = _tile(S, tm)

    def body(x_ref, wg_ref, wu_ref, z_ref, a_ref):
        x = x_ref[...]
        zg = jnp.dot(x, wg_ref[...], preferred_element_type=F32)
        zu = jnp.dot(x, wu_ref[...], preferred_element_type=F32)
        sg = _sigmoid_t(zg)
        silu = zg * sg
        z_ref[0] = (zu * (sg * (1.0 + zg * (1.0 - sg)))).astype(z_ref.dtype)
        z_ref[1] = silu.astype(z_ref.dtype)
        a_ref[...] = (silu * zu).astype(a_ref.dtype)

    return pl.pallas_call(
        body, name=name, grid=(S // tm, 2),
        in_specs=[pl.BlockSpec((tm, D), lambda i, j: (i, 0)),
                  pl.BlockSpec((None, D, FH), lambda i, j: (j, 0, 0)),
                  pl.BlockSpec((None, D, FH), lambda i, j: (j + 2, 0, 0))],
        out_specs=[pl.BlockSpec((2, tm, FH), lambda i, j: (0, i, j)), pl.BlockSpec((tm, FH), lambda i, j: (i, j))],
        out_shape=[_hbm_out((2, S, 2 * FH), BF16), _hbm_out((S, 2 * FH), BF16)],
        compiler_params=_params(("parallel", "parallel")),
    )(hn, w_in, w_in)


def _swiglu_bwd(name, dhb, w_out, z3, tm=512):
    S, D = dhb.shape
    F = w_out.shape[0]
    FH = F // 2
    tm = _tile(S, tm)

    def body(d_ref, w_ref, z_ref, dz_ref):
        d = lax.dot_general(d_ref[...], w_ref[...], _NT, preferred_element_type=F32)
        dz_ref[0] = (d * z_ref[0].astype(F32)).astype(dz_ref.dtype)
        dz_ref[1] = (d * z_ref[1].astype(F32)).astype(dz_ref.dtype)

    zspec = pl.BlockSpec((2, tm, FH), lambda i, j: (0, i, j))
    return pl.pallas_call(
        body, name=name, grid=(S // tm, 2),
        in_specs=[pl.BlockSpec((tm, D), lambda i, j: (i, 0)), pl.BlockSpec((FH, D), lambda i, j: (j, 0)), zspec],
        out_specs=zspec, out_shape=_hbm_out((2, S, F), BF16),
        compiler_params=_params(("parallel", "parallel")),
    )(dhb, w_out, z3)


SCAN_ROWS = 64


def _group_scan(A, B, reverse):
    n = A.shape[0]
    sub = lax.broadcasted_iota(jnp.int32, A.shape, 0) % SUBLANES
    for d in (1, 2, 4):
        if reverse:
            A_sh, B_sh = pltpu.roll(A, n - d, 0), pltpu.roll(B, n - d, 0)
            keep = sub < SUBLANES - d
        else:
            A_sh, B_sh = pltpu.roll(A, d, 0), pltpu.roll(B, d, 0)
            keep = sub >= d
        B = jnp.where(keep, A * B_sh + B, B)
        A = jnp.where(keep, A * A_sh, A)
    return A, B


def _block_scan(a, u, carry, reverse):
    A, B = _group_scan(a, u, reverse)
    ng = a.shape[0] // SUBLANES
    out = [None] * ng
    order = range(ng - 1, -1, -1) if reverse else range(ng)
    for gi in order:
        sl = slice(gi * SUBLANES, (gi + 1) * SUBLANES)
        hg = A[sl] * carry + B[sl]
        out[gi] = hg
        carry = hg[0:1] if reverse else hg[SUBLANES - 1:SUBLANES]
    return jnp.concatenate(out, axis=0), carry


def _lru_gates(rc, gip, grp, sp):
    gi = _sigmoid_t(gip)
    gr = _sigmoid_t(grp)
    la = -LRU_C * gr * sp
    a = jnp.exp(la)
    om = -jnp.tanh(la) * (a * a + 1.0)
    mult = jnp.sqrt(om)
    return gi, gr, a, mult


def _lru_fwd(name, proj, rc, gip, grp, lru_p, tc=256):
    S, C = rc.shape
    tc = _tile(C, tc)
    nb = S // SCAN_ROWS

    def body(gb_ref, rc_ref, gi_ref, gr_ref, l_ref, h_ref, m_ref):
        sp = _softplus(-l_ref[...])

        def step(b, carry):
            rows = pl.ds(pl.multiple_of(b * SCAN_ROWS, SCAN_ROWS), SCAN_ROWS)
            rcb = rc_ref[rows, :]
            gi, _, a, mult = _lru_gates(rcb, gi_ref[rows, :], gr_ref[rows, :], sp)
            h, carry = _block_scan(a, rcb * gi * mult, carry, False)
            h_ref[rows, :] = h
            gel, _ = _gelu_and_grad(gb_ref[rows, :])
            m_ref[rows, :] = (gel * h).astype(m_ref.dtype)
            return carry

        lax.fori_loop(0, nb, step, jnp.zeros((1, tc), F32))

    col = pl.BlockSpec((S, tc), lambda j: (0, j))
    return pl.pallas_call(
        body, name=name, grid=(C // tc,),
        in_specs=[col, col, col, col, pl.BlockSpec((1, tc), lambda j: (0, j))],
        out_specs=[col, col],
        out_shape=[_hbm_out((S, C), F32), _hbm_out((S, C), BF16)],
        compiler_params=_params(("parallel",)),
    )(proj, rc, gip, grp, lru_p)


def _lru_bwd(name, dm, proj, hrec, rc, gip, grp, lru_p, tc=256):
    S, C = rc.shape
    tc = _tile(C, tc)
    nb = S // SCAN_ROWS
    R = SCAN_ROWS

    def body(dm_ref, gb_ref, h_ref, rc_ref, gi_ref, gr_ref, l_ref,
             dgb_ref, dgi_ref, dgr_ref, drc_ref, dbi_ref, dbr_ref, dl_ref):
        lp = l_ref[...]
        sp = _softplus(-lp)
        row = lax.broadcasted_iota(jnp.int32, (R, tc), 0)
        zero = jnp.zeros((1, tc), F32)

        def step(t, carry):
            mu_in, s_i, s_r, s_sp = carry
            b = nb - 1 - t
            r0 = pl.multiple_of(b * R, R)
            rows = pl.ds(r0, R)
            rcb = rc_ref[rows, :]
            gi, gr, a, mult = _lru_gates(rcb, gi_ref[rows, :], gr_ref[rows, :], sp)
            gel, dgel = _gelu_and_grad(gb_ref[rows, :])
            dmb = dm_ref[rows, :]
            h = h_ref[rows, :]
            dgb_ref[rows, :] = (dmb * h * dgel).astype(dgb_ref.dtype)
            dh = dmb * gel
            mu, mu_out = _block_scan(a, a * dh, mu_in, True)
            mu_next = jnp.where(row == R - 1, mu_in, pltpu.roll(mu, R - 1, 0))
            lam = dh + mu_next
            p0 = pl.multiple_of(jnp.maximum(r0 - SUBLANES, 0), SUBLANES)
            prev = h_ref[pl.ds(p0, SUBLANES), :][SUBLANES - 1:SUBLANES]
            prev = jnp.where(b > 0, prev, 0.0)
            h_prev = jnp.where(row == 0, prev, pltpu.roll(h, 1, 0))
            da = lam * h_prev
            d_mult = lam * rcb * gi
            d_la = da * a - d_mult * (a * a) / mult
            d_grp = d_la * (-LRU_C * sp) * gr * (1.0 - gr)
            d_gip = lam * rcb * mult * gi * (1.0 - gi)
            dgr_ref[rows, :] = d_grp.astype(dgr_ref.dtype)
            dgi_ref[rows, :] = d_gip.astype(dgi_ref.dtype)
            drc_ref[rows, :] = lam * gi * mult
            s_i = s_i + jnp.sum(d_gip, axis=0, keepdims=True)
            s_r = s_r + jnp.sum(d_grp, axis=0, keepdims=True)
            s_sp = s_sp + jnp.sum(d_la * gr, axis=0, keepdims=True)
            return mu_out, s_i, s_r, s_sp

        _, s_i, s_r, s_sp = lax.fori_loop(0, nb, step, (zero, zero, zero, zero))
        dbi_ref[...] = s_i
        dbr_ref[...] = s_r
        dl_ref[...] = (-LRU_C * s_sp) * (-_sigmoid(-lp))

    col = pl.BlockSpec((S, tc), lambda j: (0, j))
    vec = pl.BlockSpec((1, tc), lambda j: (0, j))
    return pl.pallas_call(
        body, name=name, grid=(C // tc,),
        in_specs=[col, col, col, col, col, col, vec],
        out_specs=[col, col, col, col, vec, vec, vec],
        out_shape=[_hbm_out((S, C), BF16), _hbm_out((S, C), BF16),
                   _hbm_out((S, C), BF16), _hbm_out((S, C), F32),
                   _hbm_out((1, C), F32), _hbm_out((1, C), F32),
                   _hbm_out((1, C), F32)],
        compiler_params=_params(("parallel",)),
    )(dm, proj, hrec, rc, gip, grp, lru_p)


def _cumsum_rows(name, u, reverse):
    S, C = u.shape
    nb = S // SCAN_ROWS

    def body(u_ref, o_ref):
        def step(t, carry):
            b = nb - 1 - t if reverse else t
            rows = pl.ds(pl.multiple_of(b * SCAN_ROWS, SCAN_ROWS), SCAN_ROWS)
            ub = u_ref[rows, :]
            h, carry = _block_scan(jnp.ones_like(ub), ub, carry, reverse)
            o_ref[rows, :] = h
            return carry

        lax.fori_loop(0, nb, step, jnp.zeros((1, C), F32))

    spec = pl.BlockSpec((S, C), lambda i: (0, 0))
    return pl.pallas_call(
        body, name=name, grid=(1,), in_specs=[spec], out_specs=spec,
        out_shape=_hbm_out((S, C), F32),
        compiler_params=_params(("arbitrary",)),
    )(u)


def _shift_down(x, k):
    row = lax.broadcasted_iota(jnp.int32, x.shape, 0)
    return jnp.where(row >= k, pltpu.roll(x, k, 0), 0.0)


def _shift_up(x, k):
    n = x.shape[0]
    row = lax.broadcasted_iota(jnp.int32, x.shape, 0)
    return jnp.where(row < n - k, pltpu.roll(x, n - k, 0), 0.0)


def _conv_fwd(name, proj, w, b, tc=256):
    S, C2 = proj.shape
    C = C2 // 2
    tc = _tile(C, tc)
    off = C // tc

    def body(x_ref, w_ref, b_ref, o_ref, ob_ref):
        x = x_ref[...]
        out = b_ref[...] + w_ref[3:4, :] * x
        for k in (1, 2, 3):
            out = out + w_ref[3 - k:4 - k, :] * _shift_down(x, k)
        o_ref[...] = out
        ob_ref[...] = out.astype(BF16)

    col = pl.BlockSpec((S, tc), lambda j: (0, j))
    return pl.pallas_call(
        body, name=name, grid=(C // tc,),
        in_specs=[pl.BlockSpec((S, tc), lambda j: (0, off + j)),
                  pl.BlockSpec((4, tc), lambda j: (0, j)), pl.BlockSpec((1, tc), lambda j: (0, j))],
        out_specs=[col, col],
        out_shape=[_hbm_out((S, C), F32), _hbm_out((S, C), BF16)],
        compiler_params=_params(("parallel",)),
    )(proj, w, b)


def _conv_bwd(name, drc, proj, w, tc=256):
    S, C = drc.shape
    tc = _tile(C, tc)
    off = C // tc

    def body(y_ref, x_ref, w_ref, dx_ref, dw_ref, db_ref):
        y = y_ref[...]
        x = x_ref[...]
        dx = w_ref[3:4, :] * y
        dw_ref[3:4, :] = jnp.sum(y * x, axis=0, keepdims=True)
        for k in (1, 2, 3):
            dx = dx + w_ref[3 - k:4 - k, :] * _shift_up(y, k)
            dw_ref[3 - k:4 - k, :] = jnp.sum(y * _shift_down(x, k), axis=0, keepdims=True)
        dx_ref[...] = dx.astype(dx_ref.dtype)
        db_ref[...] = jnp.sum(y, axis=0, keepdims=True)

    col = pl.BlockSpec((S, tc), lambda j: (0, j))
    return pl.pallas_call(
        body, name=name, grid=(C // tc,),
        in_specs=[col, pl.BlockSpec((S, tc), lambda j: (0, off + j)), pl.BlockSpec((4, tc), lambda j: (0, j))],
        out_specs=[col, pl.BlockSpec((4, tc), lambda j: (0, j)), pl.BlockSpec((1, tc), lambda j: (0, j))],
        out_shape=[_hbm_out((S, C), BF16), _hbm_out((4, C), F32),
                   _hbm_out((1, C), F32)],
        compiler_params=_params(("parallel",)),
    )(drc, proj, w)


def _gates_fwd(name, rcb, wg, bg):
    S, C = rcb.shape
    nblk, bw, _ = wg.shape

    def body(x_ref, w_ref, b_ref, gi_ref, gr_ref):
        g = jnp.dot(x_ref[...], w_ref[...], preferred_element_type=F32) + b_ref[...]
        gi_ref[...] = g[:, :bw]
        gr_ref[...] = g[:, bw:]

    col = pl.BlockSpec((S, bw), lambda n: (0, n))
    return pl.pallas_call(
        body, name=name, grid=(nblk,),
        in_specs=[col, pl.BlockSpec((None, bw, 2 * bw), lambda n: (n, 0, 0)),
                  pl.BlockSpec((None, 1, 2 * bw), lambda n: (n, 0, 0))],
        out_specs=[col, col],
        out_shape=[_hbm_out((S, C), F32), _hbm_out((S, C), F32)],
        compiler_params=_params(("parallel",)),
    )(rcb, wg, bg)


def _gates_bwd(name, dgi, dgr, rcb, wg, drc1):
    S, C = rcb.shape
    nblk, bw, _ = wg.shape

    def body(dgi_ref, dgr_ref, x_ref, w_ref, d1_ref, drc_ref, dw_ref):
        w = w_ref[...]
        x = x_ref[...]
        di, dr = dgi_ref[...], dgr_ref[...]
        drc_ref[...] = (d1_ref[...]
                        + lax.dot_general(di, w[:, :bw], _NT, preferred_element_type=F32)
                        + lax.dot_general(dr, w[:, bw:], _NT, preferred_element_type=F32))
        dw_ref[:, :bw] = lax.dot_general(x, di, _TN, preferred_element_type=F32).astype(dw_ref.dtype)
        dw_ref[:, bw:] = lax.dot_general(x, dr, _TN, preferred_element_type=F32).astype(dw_ref.dtype)

    col = pl.BlockSpec((S, bw), lambda n: (0, n))
    wspec = pl.BlockSpec((None, bw, 2 * bw), lambda n: (n, 0, 0))
    return pl.pallas_call(
        body, name=name, grid=(nblk,),
        in_specs=[col, col, col, wspec, col], out_specs=[col, wspec],
        out_shape=[_hbm_out((S, C), F32), _hbm_out((nblk, bw, 2 * bw), BF16)],
        compiler_params=_params(("parallel",)),
    )(dgi, dgr, rcb, wg, drc1)


def _att_tile(S):
    return next(t for t in (512, 256, 128) if S % t == 0)


def _head_lanes(shape):
    return lax.broadcasted_iota(jnp.int32, shape, len(shape) - 1) < HEAD_DIM


def _key_bias(c_ref, rows, hh):
    return jnp.broadcast_to(c_ref[rows, hh:hh + 1], (rows.size, LANES))


def _over_keys(x, op):
    n = x.shape[0]
    while n > SUBLANES:
        n //= 2
        x = op(x[:n], x[n:2 * n])
    return (jnp.max if op is jnp.maximum else jnp.sum)(x, axis=0, keepdims=True)


def _causal_t(T, cc):
    r = lax.broadcasted_iota(jnp.int32, (T, LANES), 0)
    c = lax.broadcasted_iota(jnp.int32, (T, LANES), 1) + cc * LANES
    return r <= c


def _attn_fwd(name, q, kv, c3):
    S, D = q.shape
    HP = D // LANES
    T = _att_tile(S)
    nq = S // T
    NC = T // LANES

    def body(q_ref, k_ref, v_ref, c_ref, o_ref, of_ref, lse_ref, bias, vT, acc, m_scr, l_scr):
        def prologue(i, _):
            rows = pl.ds(pl.multiple_of(i * T, T), T)
            for hh in range(2):
                bias[hh, rows, :] = _key_bias(c_ref, rows, hh)
            vT[i] = v_ref[rows, :].astype(F32).T.astype(BF16)
            return 0

        lax.fori_loop(0, nq, prologue, 0)

        def q_step(qi, _):
            q0 = pl.multiple_of(qi * T, T)
            qb = q_ref[pl.ds(q0, T), :]
            m_scr[...] = jnp.full(m_scr.shape, -jnp.inf, F32)
            l_scr[...] = jnp.zeros(l_scr.shape, F32)
            acc[...] = jnp.zeros(acc.shape, F32)

            def tile(kj, masked):
                ks = pl.ds(pl.multiple_of(kj * T, T), T)
                kf = k_ref[ks, :].astype(F32)
                first = _head_lanes(kf.shape)
                kms = [jnp.where(first if hh == 0 else jnp.logical_not(first), kf, 0.0).astype(BF16) for hh in range(2)]
                sTs = [lax.dot_general(km, qb, _NT, preferred_element_type=F32) for km in kms]
                for hh in range(2):
                    b = bias[hh, ks, :]
                    ps = []
                    for cc in range(NC):
                        cols = slice(cc * LANES, (cc + 1) * LANES)
                        s = sTs[hh][:, cols] + b
                        if masked:
                            s = jnp.where(_causal_t(T, cc), s, -jnp.inf)
                        m_old = m_scr[hh, cc]
                        m_new = jnp.maximum(m_old, _over_keys(s, jnp.maximum))
                        alpha = jnp.exp(m_old - m_new)
                        p = jnp.exp(s - m_new)
                        l_scr[hh, cc] = alpha * l_scr[hh, cc] + _over_keys(p, jnp.add)
                        m_scr[hh, cc] = m_new
                        ps.append(p.astype(BF16))
                        acc[hh, :, cols] = acc[hh, :, cols] * alpha
                    acc[hh] += jnp.dot(vT[kj, hh * HEAD_DIM:(hh + 1) * HEAD_DIM, :], jnp.concatenate(ps, axis=1),
                                       preferred_element_type=F32)

            def inner(kj, _):
                tile(kj, False)
                return 0

            lax.fori_loop(0, qi, inner, 0)
            tile(qi, True)
            outs = []
            for hh in range(2):
                inv = jnp.concatenate([1.0 / l_scr[hh, cc] for cc in range(NC)], axis=1)
                outs.append(acc[hh] * inv)
                for cc in range(NC):
                    lse_ref[hh:hh + 1, pl.ds(q0 + cc * LANES, LANES)] = m_scr[hh, cc] + jnp.log(l_scr[hh, cc])
            out = jnp.concatenate(outs, axis=0).T
            o_ref[pl.ds(q0, T), :] = out.astype(o_ref.dtype)
            of_ref[pl.ds(q0, T), :] = out
            return 0

        lax.fori_loop(0, nq, q_step, 0)

    blk = lambda off: pl.BlockSpec((S, LANES), lambda p: (0, off + p))
    return pl.pallas_call(
        body, name=name, grid=(HP,),
        in_specs=[blk(0), blk(0), blk(HP), pl.BlockSpec((None, S, 2), lambda p: (p, 0, 0))],
        out_specs=[blk(0), blk(0), pl.BlockSpec((None, 2, S), lambda p: (p, 0, 0))],
        out_shape=[_hbm_out((S, D), BF16), _hbm_out((S, D), F32),
                   _hbm_out((HP, 2, S), F32)],
        scratch_shapes=[pltpu.VMEM((2, S, LANES), F32), pltpu.VMEM((nq, LANES, T), BF16),
                        pltpu.VMEM((2, HEAD_DIM, T), F32), pltpu.VMEM((2, NC, 1, LANES), F32),
                        pltpu.VMEM((2, NC, 1, LANES), F32)],
        compiler_params=_params(("parallel",)),
    )(q, kv, kv, c3)


def _attn_bwd(name, q, kv, c3, of, do, lse3):
    S, D = q.shape
    HP = D // LANES
    T = _att_tile(S)
    nq = S // T
    NC = T // LANES
    scale = HEAD_DIM ** -0.5

    def body(q_ref, k_ref, v_ref, c_ref, of_ref, do_ref, lse_ref,
             dq_ref, dk_ref, dv_ref, dck_ref, drq_ref, bias, kT, dqT, delta, dr_scr, dk_acc, dv_acc, dc_acc):
        def prologue(i, _):
            rows = pl.ds(pl.multiple_of(i * T, T), T)
            for hh in range(2):
                bias[hh, rows, :] = _key_bias(c_ref, rows, hh)
            kT[i] = k_ref[rows, :].astype(F32).T.astype(BF16)
            prodT = (do_ref[rows, :].astype(F32) * of_ref[rows, :]).T
            for hh in range(2):
                delta[hh:hh + 1, rows] = jnp.sum(prodT[hh * HEAD_DIM:(hh + 1) * HEAD_DIM], axis=0, keepdims=True)
            dqT[i] = jnp.zeros((LANES, T), F32)
            return 0

        lax.fori_loop(0, nq, prologue, 0)
        dr_scr[...] = jnp.zeros(dr_scr.shape, F32)

        def kv_step(kj, _):
            ks = pl.ds(pl.multiple_of(kj * T, T), T)
            kf = k_ref[ks, :].astype(F32)
            vf = v_ref[ks, :].astype(F32)
            first = _head_lanes(kf.shape)
            masks = [first, jnp.logical_not(first)]
            kms = [jnp.where(m, kf, 0.0).astype(BF16) for m in masks]
            vms = [jnp.where(m, vf, 0.0).astype(BF16) for m in masks]

            for acc in (dk_acc, dv_acc, dc_acc):
                acc[...] = jnp.zeros(acc.shape, F32)

            def tile(qi, masked):
                q0 = pl.multiple_of(qi * T, T)
                qb = q_ref[pl.ds(q0, T), :]
                dob = do_ref[pl.ds(q0, T), :]
                sTs = [lax.dot_general(km, qb, _NT, preferred_element_type=F32) for km in kms]
                dpTs = [lax.dot_general(vm, dob, _NT, preferred_element_type=F32) for vm in vms]
                for hh in range(2):
                    b = bias[hh, ks, :]
                    head = slice(hh * HEAD_DIM, (hh + 1) * HEAD_DIM)
                    ps, dss = [], []
                    for cc in range(NC):
                        cols = slice(cc * LANES, (cc + 1) * LANES)
                        at = pl.ds(q0 + cc * LANES, LANES)
                        p = jnp.exp(sTs[hh][:, cols] + b - lse_ref[hh:hh + 1, at])
                        if masked:
                            p = jnp.where(_causal_t(T, cc), p, 0.0)
                        ds = p * (dpTs[hh][:, cols] - delta[hh:hh + 1, at])
                        ps.append(p.astype(BF16))
                        dss.append(ds.astype(BF16))
                        dc_acc[hh] += ds
                        dr_scr[hh:hh + 1, at] += _over_keys(ds, jnp.add)
                    pT = jnp.concatenate(ps, axis=1)
                    dsT = jnp.concatenate(dss, axis=1)
                    dv_acc[hh] += jnp.dot(pT, dob, preferred_element_type=F32)
                    dk_acc[hh] += jnp.dot(dsT, qb, preferred_element_type=F32)
                    dqT[qi, head, :] += jnp.dot(kT[kj, head, :], dsT, preferred_element_type=F32)

            def inner(qi, _):
                tile(qi, False)
                return 0

            tile(kj, True)
            lax.fori_loop(kj + 1, nq, inner, 0)
            dk_ref[ks, :] = jnp.where(first, dk_acc[0], dk_acc[1])
            dv_ref[ks, :] = jnp.where(first, dv_acc[0], dv_acc[1])
            for hh in range(2):
                dck_ref[hh:hh + 1, ks] = -jnp.sum(dc_acc[hh].T, axis=0, keepdims=True)
            return 0

        lax.fori_loop(0, nq, kv_step, 0)

        def epilogue(i, _):
            rows = pl.ds(pl.multiple_of(i * T, T), T)
            dq_ref[rows, :] = (dqT[i].T * scale).astype(dq_ref.dtype)
            return 0

        lax.fori_loop(0, nq, epilogue, 0)
        drq_ref[...] = dr_scr[...]

    blk = lambda off: pl.BlockSpec((S, LANES), lambda p: (0, off + p))
    row_spec = pl.BlockSpec((None, 2, S), lambda p: (p, 0, 0))
    return pl.pallas_call(
        body, name=name, grid=(HP,),
        in_specs=[blk(0), blk(0), blk(HP), pl.BlockSpec((None, S, 2), lambda p: (p, 0, 0)), blk(0), blk(0), row_spec],
        out_specs=[blk(0), blk(0), blk(0), row_spec, row_spec],
        out_shape=[_hbm_out((S, D), BF16), _hbm_out((S, D), F32),
                   _hbm_out((S, D), F32), _hbm_out((HP, 2, S), F32),
                   _hbm_out((HP, 2, S), F32)],
        scratch_shapes=[pltpu.VMEM((2, S, LANES), F32), pltpu.VMEM((nq, LANES, T), BF16),
                        pltpu.VMEM((nq, LANES, T), F32), pltpu.VMEM((2, S), F32), pltpu.VMEM((2, S), F32)]
        + [pltpu.VMEM((2, T, LANES), F32)] * 3,
        compiler_params=_params(("parallel",)),
    )(q, kv, kv, c3, of, do, lse3)


def _logsig_fwd(name, f):
    S, C = f.shape

    def body(f_ref, o_ref):
        o_ref[...] = -_softplus(-f_ref[...])

    spec = pl.BlockSpec((S, C), lambda i: (0, 0))
    return pl.pallas_call(body, name=name, grid=(1,), in_specs=[spec], out_specs=spec,
                          out_shape=_hbm_out((S, C), F32),
                          compiler_params=_params(("arbitrary",)))(f)


def _logsig_bwd(name, dls, f):
    S, C = f.shape

    def body(d_ref, f_ref, o_ref, s_ref):
        df = d_ref[...] * _sigmoid(-f_ref[...])
        o_ref[...] = df.astype(o_ref.dtype)
        s_ref[...] = jnp.sum(df, axis=0, keepdims=True)

    spec = pl.BlockSpec((S, C), lambda i: (0, 0))
    return pl.pallas_call(body, name=name, grid=(1,), in_specs=[spec, spec],
                          out_specs=[spec, pl.BlockSpec((1, C), lambda i: (0, 0))],
                          out_shape=[_hbm_out((S, C), BF16), _hbm_out((1, C), F32)],
                          compiler_params=_params(("arbitrary",)))(dls, f)


def _add_cast(name, parts, out_dtype, tr=256):
    S, C = parts[0].shape
    tr = _tile(S, tr)
    n = len(parts)

    def body(*refs):
        acc = refs[0][...].astype(F32)
        for r in refs[1:n]:
            acc = acc + r[...].astype(F32)
        refs[n][...] = acc.astype(out_dtype)

    spec = pl.BlockSpec((tr, C), lambda i: (i, 0))
    return pl.pallas_call(body, name=name, grid=(S // tr,), in_specs=[spec] * n, out_specs=spec,
                          out_shape=_hbm_out((S, C), out_dtype),
                          compiler_params=_params(("parallel",)))(*parts)


def _local_step(x, target, gains, layer_weights, layer_prefetch, layer_grads):
    S, D = x.shape
    HP = D // LANES
    scale = HEAD_DIM ** -0.5
    tm = _tile(S, 512)
    tx = _tile(S, 256)
    td = _tile(D, 512)
    saved = []
    h = x
    l = 0
    kv = c3 = f_pre = hn_kv = h_kv = None
    while True:
        W = layer_weights(l, "mix", h)
        if W is None:
            break
        recurrent = "w_rec_in" in W
        if l == 0:
            xn = _rmsnorm_fwd("mix_norm_0", h, gains["mix"][0])
        if recurrent:
            CH = W["w_rec_in"].shape[-1]
            C = 2 * CH
            proj = _mm(f"rec_in_{l}", "nn", xn, W["w_rec_in"], grid=(S // tm, N_CHIPS),
                       a_spec=pl.BlockSpec((tm, D), lambda i, j: (i, 0)),
                       b_spec=pl.BlockSpec((None, D, CH), lambda i, j: (j, 0, 0)),
                       out_shape=(S, 2 * C), out_dtype=F32,
                       out_spec=pl.BlockSpec((tm, CH), lambda i, j: (i, j)))
            layer_prefetch(l, "mix2", proj)
            rc, rcb = _conv_fwd(f"conv_{l}", proj, W["conv_w"], W["conv_b"])
            W = {**W, **layer_weights(l, "mix2", rcb)}
            gip, grp = _gates_fwd(f"gates_{l}", rcb, W["w_gates"], W["b_gates"])
            hrec, m = _lru_fwd(f"lru_{l}", proj, rc, gip, grp, W["lru_param"])
            layer_prefetch(l, "ffn", m)
            h_mid, hn = _mm_nn(f"rec_out_{l}", m, W["w_rec_out"], out_dtype=F32, res=h, tn=D, norm_gain=gains["ffn"][l])
            mix_saved = (xn, proj, rc, rcb, gip, grp, hrec, m)
        else:
            if "w_kv" in W:
                h_kv = h
                hn_kv = _rmsnorm_fwd("kv_norm", h, W["norm_kv"])
                kv = _mm_nn("kv_proj", hn_kv, W["w_kv"], out_dtype=BF16, tm=1024, tn=1024)
                f_pre = _mm_nn("f_proj", hn_kv, W["w_f"], out_dtype=F32, bias=W["b_f"])
                c = _cumsum_rows("c_cumsum", _logsig_fwd("logsig", f_pre), False)
                c3 = (-c[:, :2 * HP]).reshape(S, HP, 2).transpose(1, 0, 2)
            q = _mm_nn(f"q_proj_{l}", xn, W["w_q"], out_dtype=BF16, scale=scale, tm=1024, tn=1024)
            layer_prefetch(l, "mix2", q)
            o, of, lse = _attn_fwd(f"attn_fwd_{l}", q, kv, c3)
            W = {**W, **layer_weights(l, "mix2", o)}
            layer_prefetch(l, "ffn", o)
            h_mid, hn = _mm_nn(f"o_proj_{l}", o, W["w_o"], out_dtype=F32, res=h, tn=D, norm_gain=gains["ffn"][l])
            mix_saved = (xn, q, o, of, lse)
        W = {**W, **layer_weights(l, "ffn", h_mid)}
        z3, act = _swiglu_fwd(f"ffn_in_{l}", hn, W["w_ffn_in"])
        layer_prefetch(l + 1, "mix", act)
        saved.append((W, h, h_mid, mix_saved, (hn, z3, act)))
        l += 1
        if l < len(gains["mix"]):
            h, xn = _mm_nn(f"ffn_out_{l - 1}", act, W["w_ffn_out"], out_dtype=F32, res=h_mid, tn=D,
                           norm_gain=gains["mix"][l])
        else:
            h = _mm_nn(f"ffn_out_{l - 1}", act, W["w_ffn_out"], out_dtype=F32, res=h_mid, tn=D)

    dh, dhb, dg_final, loss_row = _loss_head("loss_head", h, target, gains["final"])

    dk_parts, dv_parts, dc_parts = [], [], []
    token = None
    for l in reversed(range(len(saved))):
        W, h_in, h_mid, mix_saved, (hn, z3, act) = saved[l]
        recurrent = "w_rec_in" in W
        FH = W["w_ffn_in"].shape[-1]
        G = {}
        norm_ffn = gains["ffn"][l]
        if token is not None:
            norm_ffn = norm_ffn + jnp.minimum(token[:1, :1], 0.0)
        G["w_ffn_out"] = _mm_tn(f"d_ffn_out_{l}", act, dhb, out_dtype=BF16, tn=D)
        dz3 = _swiglu_bwd(f"d_act_{l}", dhb, W["w_ffn_out"], z3)
        G["w_ffn_in"] = _mm(
            f"d_ffn_in_{l}", "tn", hn, dz3, grid=(D // td, N_CHIPS),
            a_spec=pl.BlockSpec((S, td), lambda i, j: (0, i)),
            b_spec=pl.BlockSpec((None, S, FH), lambda i, j: (j // 2, 0, j % 2)),
            out_shape=(N_CHIPS, D, FH), out_dtype=BF16,
            out_spec=pl.BlockSpec((None, td, FH), lambda i, j: (j, i, 0)))
        ffn_token = layer_grads(l, "ffn", G)
        G = {}
        if ffn_token is not None:
            norm_ffn = norm_ffn + jnp.minimum(ffn_token[:1, :1], 0.0)
        dh, dhb, dgp = _mm(f"d_ffn_hn_{l}", "nt", dz3, W["w_ffn_in"], grid=(S // tx, 1),
                           a_spec=[pl.BlockSpec((None, tx, FH), functools.partial(lambda i, j, k: (k // 2, i, k % 2), k=k))
                                   for k in range(N_CHIPS)],
                           b_spec=[pl.BlockSpec((None, D, FH), functools.partial(lambda i, j, k: (k, 0, 0), k=k))
                                   for k in range(N_CHIPS)],
                           out_shape=(S, D), out_dtype=F32, out_spec=pl.BlockSpec((tx, D), lambda i, j: (i, 0)),
                           norm_bwd=(h_mid, norm_ffn, dh))
        G["norm_ffn"] = jnp.sum(dgp, axis=0)
        if recurrent:
            CH = W["w_rec_in"].shape[-1]
            C = 2 * CH
            xn, proj, rc, rcb, gip, grp, hrec, m = mix_saved
            G["w_rec_out"] = _mm_tn(f"d_rec_out_{l}", m, dhb, out_dtype=BF16, tn=D)
            dm = _mm_nt(f"d_m_{l}", dhb, W["w_rec_out"], out_dtype=F32, tn=C)
            dgb, dgi, dgr, drc1, G["b_gi"], G["b_gr"], G["lru_param"] = _lru_bwd(
                f"d_lru_{l}", dm, proj, hrec, rc, gip, grp, W["lru_param"])
            drc, G["w_gates"] = _gates_bwd(f"d_gates_{l}", dgi, dgr, rcb, W["w_gates"], drc1)
            mix_token = layer_grads(l, "mix2", {n: G[n] for n in ("w_rec_out", "w_gates")})
            drec, G["conv_w"], G["conv_b"] = _conv_bwd(f"d_conv_{l}", drc, proj, W["conv_w"])
            dproj = jnp.concatenate([dgb, drec], axis=1)
            norm_mix = gains["mix"][l] if mix_token is None else gains["mix"][l] + jnp.minimum(mix_token[:1, :1], 0.0)
            G["w_rec_in"] = _mm(
                f"d_rec_in_{l}", "tn", xn, dproj, grid=(1, N_CHIPS),
                a_spec=pl.BlockSpec((S, D), lambda i, j: (0, 0)),
                b_spec=pl.BlockSpec((S, CH), lambda i, j: (0, j)),
                out_shape=(N_CHIPS, D, CH), out_dtype=BF16,
                out_spec=pl.BlockSpec((None, D, CH), lambda i, j: (j, 0, 0)))
            dh, dhb, dgp = _mm(f"d_rec_xn_{l}", "nt", dproj, W["w_rec_in"], grid=(S // tx, 1),
                               a_spec=[pl.BlockSpec((tx, CH), functools.partial(lambda i, j, k: (i, k), k=k))
                                       for k in range(N_CHIPS)],
                               b_spec=[pl.BlockSpec((None, D, CH), functools.partial(lambda i, j, k: (k, 0, 0), k=k))
                                       for k in range(N_CHIPS)],
                               out_shape=(S, D), out_dtype=F32, out_spec=pl.BlockSpec((tx, D), lambda i, j: (i, 0)),
                               norm_bwd=(h_in, norm_mix, dh))
        else:
            xn, q, o, of, lse = mix_saved
            G["w_o"] = _mm_tn(f"d_o_proj_{l}", o, dhb, out_dtype=BF16, tn=D)
            do = _mm_nt(f"d_o_{l}", dhb, W["w_o"], out_dtype=BF16, tm=1024, tn=D)
            mix_token = layer_grads(l, "mix2", {"w_o": G["w_o"]})
            dq, dk, dv, dck, drq = _attn_bwd(f"attn_bwd_{l}", q, kv, c3, of, do, lse)
            dk_parts.append(dk)
            dv_parts.append(dv)
            dc_parts.append((dck + drq).reshape(2 * HP, S).T)
            G["w_q"] = _mm_tn(f"d_q_proj_{l}", xn, dq, out_dtype=BF16, tn=D)
            norm_mix = gains["mix"][l] if mix_token is None else gains["mix"][l] + jnp.minimum(mix_token[:1, :1], 0.0)
            dh, dhb, dgp = _mm_nt(f"d_q_xn_{l}", dq, W["w_q"], out_dtype=F32, tn=D, norm_bwd=(h_in, norm_mix, dh))
        G["norm_mix"] = jnp.sum(dgp, axis=0)
        if "w_kv" in W:
            dkb = _add_cast("dk_sum", dk_parts, BF16)
            dvb = _add_cast("dv_sum", dv_parts, BF16)
            dkv = jnp.concatenate([dkb, dvb], axis=1)
            dc = sum(dc_parts[1:], dc_parts[0])
            dc_pad = jnp.pad(dc, ((0, 0), (0, LANES - 2 * HP)))
            dls = _cumsum_rows("dc_cumsum", dc_pad, True)
            dfb, G["b_f"] = _logsig_bwd("d_logsig", dls, f_pre)
            G["w_kv"] = _mm_tn("d_kv_proj", hn_kv, dkv, out_dtype=BF16, tn=1024)
            G["w_f"] = _mm_tn("d_f_proj", hn_kv, dfb, out_dtype=F32)
            dhn_f = _mm_nt("d_f_hn", dfb, W["w_f"], out_dtype=F32, tn=D)
            dh, dhb, dgp = _mm_nt("d_kv_hn", dkv, W["w_kv"], out_dtype=F32, tn=D, res=dhn_f,
                                  norm_bwd=(h_kv, W["norm_kv"], dh))
            G["norm_kv"] = jnp.sum(dgp, axis=0)
        token = layer_grads(l, "mix", G)
    return loss_row, dh, dg_final


_ANY = pl.BlockSpec(memory_space=pl.ANY)


def _position():
    return lax.axis_index("x"), lax.axis_index("y"), lax.axis_index("c")


def _chip_peers(x, y):
    return [(1 - x, y), (x, 1 - y), (1 - x, 1 - y)]


def _half_rows(c, n):
    h = n // 2
    assert h % 16 == 0
    return pl.ds(pl.multiple_of(c * h, 16), h)


def _place_own(name, shard, layer, me):
    _, R, C = shard.shape
    tr = _row_tile(R, C, 2 * shard.dtype.itemsize, target=8 << 20)

    def body(me_ref, x_ref, o_ref):
        o_ref[...] = x_ref[...]

    return pl.pallas_call(
        body, name=name,
        grid_spec=pltpu.PrefetchScalarGridSpec(
            num_scalar_prefetch=1, grid=(R // tr,),
            in_specs=[pl.BlockSpec((None, tr, C), lambda i, me_ref: (layer, i, 0))],
            out_specs=pl.BlockSpec((None, tr, C), lambda i, me_ref: (me_ref[0], i, 0))),
        out_shape=_hbm_out((N_CHIPS, R, C), shard.dtype),
        compiler_params=_params(("parallel",)),
    )(me, shard)


def _gather_smalls(name, smalls):
    ns = len(smalls)

    def body(*refs):
        ins, outs = refs[:ns], refs[ns:2 * ns]
        send_sems, recv_sems, local_sems = refs[2 * ns:]
        x, y, c = _position()
        me = 2 * x + y
        peers = _chip_peers(x, y)

        def remote(t, k, chip):
            px, py = peers[k]
            return pltpu.make_async_remote_copy(
                src_ref=ins[t], dst_ref=outs[t].at[chip], send_sem=send_sems.at[3 * t + k],
                recv_sem=recv_sems.at[3 * t + k], device_id=(px, py, c), device_id_type=MESH)

        local = [pltpu.make_async_copy(ins[t], outs[t].at[me], local_sems.at[t]) for t in range(ns)]
        for t in range(ns):
            local[t].start()
            for k in range(3):
                remote(t, k, me).start()
        for t in range(ns):
            for k in range(3):
                px, py = peers[k]
                remote(t, k, 2 * px + py).wait_recv()
        for t in range(ns):
            for k in range(3):
                remote(t, k, me).wait_send()
            local[t].wait()

    return pl.pallas_call(
        body, name=name, in_specs=[_ANY] * ns, out_specs=[_ANY] * ns,
        out_shape=[_hbm_out((N_CHIPS,) + s.shape, s.dtype) for s in smalls],
        scratch_shapes=[pltpu.SemaphoreType.DMA((3 * ns,)), pltpu.SemaphoreType.DMA((3 * ns,)),
                        pltpu.SemaphoreType.DMA((ns,))],
    )(*smalls)


_SEM = pl.BlockSpec(memory_space=pltpu.SEMAPHORE)
_SPLIT = pltpu.CompilerParams(has_side_effects=pltpu.SideEffectType.DATAFLOW_SIDE_EFFECTING)


def _weight_copy(shards, buf, items, sems, i, k, chip_of_dst, peers, c):
    w, l = items[i]
    px, py = peers[k]
    half = _half_rows(c, shards[w].shape[1])
    return pltpu.make_async_remote_copy(
        src_ref=shards[w].at[l, half], dst_ref=buf.at[chip_of_dst, half],
        send_sem=sems[0].at[3 * i + k], recv_sem=sems[1].at[3 * i + k],
        device_id=(px, py, c), device_id_type=MESH)


def _gather_start(name, shards, bufs, items, after):
    nw, n = len(shards), len(bufs)

    def body(*refs):
        ins, outs, sems = refs[:nw], refs[nw + n + 1:nw + 2 * n + 1], refs[nw + 2 * n + 1:]
        x, y, c = _position()
        peers = _chip_peers(x, y)
        for i in range(n):
            for k in range(3):
                _weight_copy(ins, outs[i], items, sems, i, k, 2 * x + y, peers, c).start()

    res = pl.pallas_call(
        body, name=name, in_specs=[_ANY] * (nw + n + 1), out_specs=[_ANY] * n + [_SEM, _SEM],
        out_shape=[_hbm_out(b.shape, b.dtype) for b in bufs]
        + [pltpu.SemaphoreType.DMA((3 * n,)), pltpu.SemaphoreType.DMA((3 * n,))],
        input_output_aliases={nw + i: i for i in range(n)}, compiler_params=_SPLIT,
    )(*shards, *bufs, after)
    return res[:n], res[n:]


def _gather_wait(name, shards, bufs, items, ids, sems, after):
    nw, m = len(shards), len(ids)

    def body(*refs):
        ins, bs = refs[:nw], refs[nw:nw + m]
        sem_refs = refs[nw + m:nw + m + 2]
        x, y, c = _position()
        peers = _chip_peers(x, y)
        for j, i in enumerate(ids):
            for k in range(3):
                px, py = peers[k]
                _weight_copy(ins, bs[j], items, sem_refs, i, k, 2 * px + py, peers, c).wait_recv()
        for j, i in enumerate(ids):
            for k in range(3):
                _weight_copy(ins, bs[j], items, sem_refs, i, k, 2 * x + y, peers, c).wait_send()

    res = pl.pallas_call(
        body, name=name, in_specs=[_ANY] * (nw + m) + [_SEM, _SEM, _ANY], out_specs=[_ANY] * m,
        out_shape=[_hbm_out(bufs[i].shape, bufs[i].dtype) for i in ids],
        input_output_aliases={nw + j: j for j in range(m)}, compiler_params=_SPLIT,
    )(*shards, *[bufs[i] for i in ids], *sems, after)
    return list(res)


def _forward_copy(src, dst, sems, i, k, core):
    x, y, c = _position()
    px, py = _chip_peers(x, y)[k]
    half = _half_rows(core, src.shape[1])
    return pltpu.make_async_remote_copy(
        src_ref=src.at[2 * px + py, half], dst_ref=dst.at[2 * px + py, half],
        send_sem=sems[0].at[3 * i + k], recv_sem=sems[1].at[3 * i + k],
        device_id=(x, y, 1 - c), device_id_type=MESH)


def _forward_start(name, bufs):
    n = len(bufs)

    def body(*refs):
        ins, outs, sems = refs[:n], refs[n:2 * n], refs[2 * n:]
        c = lax.axis_index("c")
        for i in range(n):
            for k in range(3):
                _forward_copy(ins[i], outs[i], sems, i, k, c).start()

    res = pl.pallas_call(
        body, name=name, in_specs=[_ANY] * n, out_specs=[_ANY] * n + [_SEM, _SEM],
        out_shape=[_hbm_out(g.shape, g.dtype) for g in bufs]
        + [pltpu.SemaphoreType.DMA((3 * n,)), pltpu.SemaphoreType.DMA((3 * n,))],
        input_output_aliases={i: i for i in range(n)}, compiler_params=_SPLIT,
    )(*bufs)
    return list(res[:n]), res[n:]


def _forward_wait(name, bufs, sems, after):
    n = len(bufs)

    def body(*refs):
        bs, sem_refs = refs[:n], refs[n:n + 2]
        c = lax.axis_index("c")
        for i in range(n):
            for k in range(3):
                _forward_copy(bs[i], bs[i], sem_refs, i, k, 1 - c).wait_recv()
        for i in range(n):
            for k in range(3):
                _forward_copy(bs[i], bs[i], sem_refs, i, k, c).wait_send()

    return list(pl.pallas_call(
        body, name=name, in_specs=[_ANY] * n + [_SEM, _SEM, _ANY], out_specs=[_ANY] * n,
        out_shape=[_hbm_out(g.shape, g.dtype) for g in bufs],
        input_output_aliases={i: i for i in range(n)}, compiler_params=_SPLIT,
    )(*bufs, *sems, after))


def _reduce_copy(grads, others, sems, i):
    x, y, c = _position()
    return pltpu.make_async_remote_copy(
        src_ref=grads[i].at[:, _half_rows(1 - c, grads[i].shape[1])], dst_ref=others[i],
        send_sem=sems[0].at[i], recv_sem=sems[1].at[i], device_id=(x, y, 1 - c), device_id_type=MESH)


def _reduce_start(name, grads, after):
    n = len(grads)

    def body(*refs):
        ins, outs, sems, token = refs[:n], refs[n + 1:2 * n + 1], refs[2 * n + 1:2 * n + 3], refs[2 * n + 3]
        for i in range(n):
            _reduce_copy(ins, outs, sems, i).start()
        token[...] = jnp.zeros_like(token)

    res = pl.pallas_call(
        body, name=name, in_specs=[_ANY] * (n + 1),
        out_specs=[_ANY] * n + [_SEM, _SEM, pl.BlockSpec(memory_space=pltpu.VMEM)],
        out_shape=[_hbm_out((N_CHIPS, g.shape[1] // 2, g.shape[2]), g.dtype) for g in grads]
        + [pltpu.SemaphoreType.DMA((n,)), pltpu.SemaphoreType.DMA((n,)), jax.ShapeDtypeStruct((SUBLANES, LANES), F32)],
        compiler_params=_SPLIT,
    )(*grads, after)
    return list(res[:n]), res[n:n + 2], res[n + 2]


def _reduce_wait(name, grads, others, sems, after):
    n = len(grads)

    def body(*refs):
        ins, os_, sem_refs = refs[:n], refs[n:2 * n], refs[2 * n:2 * n + 2]
        for i in range(n):
            _reduce_copy(ins, os_, sem_refs, i).wait_recv()
        for i in range(n):
            _reduce_copy(ins, os_, sem_refs, i).wait_send()

    return list(pl.pallas_call(
        body, name=name, in_specs=[_ANY] * (2 * n) + [_SEM, _SEM, _ANY], out_specs=[_ANY] * n,
        out_shape=[_hbm_out(o.shape, o.dtype) for o in others],
        input_output_aliases={n + i: i for i in range(n)}, compiler_params=_SPLIT,
    )(*grads, *others, *sems, after))


def _sum_cores(name, g, other, core):
    _, R, C = g.shape
    H = R // 2
    tr = _row_tile(H, C, 3 * 2, target=12 << 20)
    nb = H // tr

    def body(c_ref, g_ref, o_ref, out_ref):
        out_ref[...] = (g_ref[...].astype(F32) + o_ref[...].astype(F32)).astype(out_ref.dtype)

    return pl.pallas_call(
        body, name=name,
        grid_spec=pltpu.PrefetchScalarGridSpec(
            num_scalar_prefetch=1, grid=(N_CHIPS, nb),
            in_specs=[pl.BlockSpec((None, tr, C), lambda j, i, c_ref: (j, c_ref[0] * nb + i, 0)),
                      pl.BlockSpec((None, tr, C), lambda j, i, c_ref: (j, i, 0))],
            out_specs=pl.BlockSpec((None, tr, C), lambda j, i, c_ref: (j, i, 0))),
        out_shape=_hbm_out((N_CHIPS, H, C), BF16),
        compiler_params=_params(("parallel", "parallel")),
    )(core, g, other)


def _sum_chips(name, received, own, full, layer, me_core):
    _, H, C = received.shape
    tr = _row_tile(H, C, 3 * 2 + 2 + 4, target=12 << 20)
    nb = H // tr

    def body(s_ref, r_ref, own_ref, full_ref, out_ref):
        acc = r_ref[0].astype(F32)
        for k in (1, 2):
            acc = acc + r_ref[k].astype(F32)
        out_ref[...] = acc + own_ref[...].astype(F32)

    return pl.pallas_call(
        body, name=name,
        grid_spec=pltpu.PrefetchScalarGridSpec(
            num_scalar_prefetch=1, grid=(nb,),
            in_specs=[pl.BlockSpec((3, tr, C), lambda i, s_ref: (0, i, 0)),
                      pl.BlockSpec((None, tr, C), lambda i, s_ref: (s_ref[0], i, 0)),
                      _ANY],
            out_specs=pl.BlockSpec((None, tr, C), lambda i, s_ref: (layer, s_ref[1] * nb + i, 0))),
        out_shape=_hbm_out(full.shape, full.dtype),
        input_output_aliases={3: 0},
        compiler_params=_params(("parallel",)),
    )(me_core, received, own, full)


def _part_copy(parts, recv, sems, i, k, peers, c):
    px, py = peers[k]
    return pltpu.make_async_remote_copy(
        src_ref=parts[i].at[2 * px + py], dst_ref=recv[i].at[k],
        send_sem=sems[0].at[3 * i + k], recv_sem=sems[1].at[3 * i + k],
        device_id=(px, py, c), device_id_type=MESH)


def _scatter_start(name, parts):
    n = len(parts)

    def body(*refs):
        ins, outs, sems, token = refs[:n], refs[n:2 * n], refs[2 * n:2 * n + 2], refs[2 * n + 2]
        x, y, c = _position()
        peers = _chip_peers(x, y)
        for i in range(n):
            for k in range(3):
                _part_copy(ins, outs, sems, i, k, peers, c).start()
        token[...] = jnp.zeros_like(token)

    res = pl.pallas_call(
        body, name=name, in_specs=[_ANY] * n,
        out_specs=[_ANY] * n + [_SEM, _SEM, pl.BlockSpec(memory_space=pltpu.VMEM)],
        out_shape=[_hbm_out((3,) + p.shape[1:], p.dtype) for p in parts]
        + [pltpu.SemaphoreType.DMA((3 * n,)), pltpu.SemaphoreType.DMA((3 * n,)),
           jax.ShapeDtypeStruct((SUBLANES, LANES), F32)],
        compiler_params=_SPLIT,
    )(*parts)
    return list(res[:n]), res[n:n + 2], res[n + 2]


def _scatter_wait(name, parts, recv, sems):
    n = len(parts)

    def body(*refs):
        ins, rs, sem_refs = refs[:n], refs[n:2 * n], refs[2 * n:2 * n + 2]
        x, y, c = _position()
        peers = _chip_peers(x, y)
        for i in range(n):
            for k in range(3):
                _part_copy(ins, rs, sem_refs, i, k, peers, c).wait_recv()
        for i in range(n):
            for k in range(3):
                _part_copy(ins, rs, sem_refs, i, k, peers, c).wait_send()

    return list(pl.pallas_call(
        body, name=name, in_specs=[_ANY] * (2 * n) + [_SEM, _SEM], out_specs=[_ANY] * n,
        out_shape=[_hbm_out(r.shape, r.dtype) for r in recv],
        input_output_aliases={n + i: i for i in range(n)}, compiler_params=_SPLIT,
    )(*parts, *recv, *sems))


def _share_d2d(name, full):
    n = len(full)

    def body(*refs):
        ins, outs = refs[:n], refs[n:2 * n]
        send_sems, recv_sems = refs[2 * n:]
        x, y, c = _position()

        def remote(w, core):
            half = _half_rows(core, ins[w].shape[1])
            return pltpu.make_async_remote_copy(
                src_ref=ins[w].at[:, half], dst_ref=outs[w].at[:, half],
                send_sem=send_sems.at[w], recv_sem=recv_sems.at[w],
                device_id=(x, y, 1 - c), device_id_type=MESH)

        for w in range(n):
            remote(w, c).start()
        for w in range(n):
            remote(w, 1 - c).wait_recv()
        for w in range(n):
            remote(w, c).wait_send()

    return pl.pallas_call(
        body, name=name, in_specs=[_ANY] * n, out_specs=[_ANY] * n,
        out_shape=[_hbm_out(f.shape, f.dtype) for f in full],
        input_output_aliases={w: w for w in range(n)},
        scratch_shapes=[pltpu.SemaphoreType.DMA((n,)), pltpu.SemaphoreType.DMA((n,))],
    )(*full)


def _gather_all(name, a):
    def body(a_ref, o_ref, send_sems, recv_sems, local_sem):
        x, y, c = _position()
        me = 4 * x + 2 * y + c

        def peer(k):
            return (x ^ ((k >> 2) & 1), y ^ ((k >> 1) & 1), c ^ (k & 1))

        def remote(k, slot):
            return pltpu.make_async_remote_copy(
                src_ref=a_ref, dst_ref=o_ref.at[slot], send_sem=send_sems.at[k - 1], recv_sem=recv_sems.at[k - 1],
                device_id=peer(k), device_id_type=MESH)

        local = pltpu.make_async_copy(a_ref, o_ref.at[me], local_sem)
        local.start()
        for k in range(1, N_DEV):
            remote(k, me).start()
        for k in range(1, N_DEV):
            px, py, pc = peer(k)
            remote(k, 4 * px + 2 * py + pc).wait_recv()
        for k in range(1, N_DEV):
            remote(k, me).wait_send()
        local.wait()

    return pl.pallas_call(
        body, name=name, in_specs=[_ANY], out_specs=_ANY,
        out_shape=_hbm_out((N_DEV,) + a.shape, a.dtype),
        scratch_shapes=[pltpu.SemaphoreType.DMA((N_DEV - 1,)), pltpu.SemaphoreType.DMA((N_DEV - 1,)),
                        pltpu.SemaphoreType.DMA],
    )(a)


def _rows2d(a, lead=0):
    return a.reshape(a.shape[:lead] + (-1, a.shape[-1]))


def _row_tile(rows, cols, itemsize=4, target=1 << 20):
    want = max(SUBLANES, target // (cols * itemsize))
    t = min(rows, (want // 16) * 16)
    while t > 16 and rows % t:
        t -= 16
    return t if rows % t == 0 else rows


def _sum_slots(name, r, out_dtype=F32):
    ns = r.shape[0]
    r2 = _rows2d(r, 1)
    _, rows, cols = r2.shape
    tr = _row_tile(rows, cols)

    def body(r_ref, o_ref):
        acc = r_ref[0].astype(F32)
        for s in range(1, ns):
            acc = acc + r_ref[s].astype(F32)
        o_ref[...] = acc.astype(o_ref.dtype)

    out = pl.pallas_call(
        body, name=name, grid=(rows // tr,),
        in_specs=[pl.BlockSpec((ns, tr, cols), lambda i: (0, i, 0))],
        out_specs=pl.BlockSpec((tr, cols), lambda i: (i, 0)),
        out_shape=_hbm_out((rows, cols), out_dtype),
        compiler_params=_params(("parallel",)),
    )(r2)
    return out.reshape(r.shape[1:])


def _adamw(name, g_parts, w, m, v):
    shape = w.shape
    ng = len(g_parts)
    args = [_rows2d(a) for a in (*g_parts, w, m, v)]
    rows, cols = args[0].shape
    tr = _row_tile(rows, cols, (ng + 7) * 4, target=16 << 20)
    c1 = 1.0 - ADAM_B1 ** ADAM_STEP
    c2 = 1.0 - ADAM_B2 ** ADAM_STEP

    def body(*refs):
        g = refs[0][...]
        for r in refs[1:ng]:
            g = g + r[...]
        w_ref, m_ref, v_ref = refs[ng:ng + 3]
        g_out, d_out, m_out, v_out = refs[ng + 3:]
        mn = ADAM_B1 * m_ref[...] + (1.0 - ADAM_B1) * g
        vn = ADAM_B2 * v_ref[...] + (1.0 - ADAM_B2) * (g * g)
        m_hat = mn / c1
        v_hat = vn / c2
        g_out[...] = g
        d_out[...] = -ADAM_LR * (m_hat / (jnp.sqrt(v_hat) + ADAM_EPS) + ADAM_WD * w_ref[...])
        m_out[...] = mn
        v_out[...] = vn

    spec = pl.BlockSpec((tr, cols), lambda i: (i, 0))
    outs = pl.pallas_call(
        body, name=name, grid=(rows // tr,), in_specs=[spec] * (ng + 3), out_specs=[spec] * 4,
        out_shape=[_hbm_out((rows, cols), F32)] * 4,
        compiler_params=_params(("parallel",)),
    )(*args)
    return tuple(o.reshape(shape) for o in outs)


_WEIGHTS = ["norm_mix", "norm_ffn", "w_ffn_in", "w_ffn_out", "w_rec_in", "conv_w", "conv_b", "w_lru_gates",
            "b_lru_gates", "lru_param", "w_rec_out", "norm_kv", "w_kvf", "b_forget", "w_q", "w_o", "norm_final"]
_BIG = ["w_ffn_in", "w_ffn_out", "w_rec_in", "w_lru_gates", "w_rec_out", "w_kvf", "w_q", "w_o"]


def _stack3(a):
    return a[None] if a.ndim == 2 else a.reshape(a.shape[0], -1, a.shape[-1])


def _pad_lanes(a, n):
    return jnp.pad(a, ((0, 0),) * (a.ndim - 1) + ((0, n - a.shape[-1]),))


def kernel(x, norm_mix, norm_ffn, w_ffn_in, w_ffn_out, w_rec_in, conv_w, conv_b, w_lru_gates, b_lru_gates, lru_param, w_rec_out, norm_kv, w_kvf, b_forget, w_q, w_o, norm_final, loss_target, m_norm_mix, m_norm_ffn, m_w_ffn_in, m_w_ffn_out, m_w_rec_in, m_conv_w, m_conv_b, m_w_lru_gates, m_b_lru_gates, m_lru_param, m_w_rec_out, m_norm_kv, m_w_kvf, m_b_forget, m_w_q, m_w_o, m_norm_final, v_norm_mix, v_norm_ffn, v_w_ffn_in, v_w_ffn_out, v_w_rec_in, v_conv_w, v_conv_b, v_w_lru_gates, v_b_lru_gates, v_lru_param, v_w_rec_out, v_norm_kv, v_w_kvf, v_b_forget, v_w_q, v_w_o, v_norm_final):
    P = dict(norm_mix=norm_mix, norm_ffn=norm_ffn, w_ffn_in=w_ffn_in, w_ffn_out=w_ffn_out, w_rec_in=w_rec_in,
             conv_w=conv_w, conv_b=conv_b, w_lru_gates=w_lru_gates, b_lru_gates=b_lru_gates, lru_param=lru_param,
             w_rec_out=w_rec_out, norm_kv=norm_kv, w_kvf=w_kvf, b_forget=b_forget, w_q=w_q, w_o=w_o,
             norm_final=norm_final)
    M1 = dict(norm_mix=m_norm_mix, norm_ffn=m_norm_ffn, w_ffn_in=m_w_ffn_in, w_ffn_out=m_w_ffn_out,
              w_rec_in=m_w_rec_in, conv_w=m_conv_w, conv_b=m_conv_b, w_lru_gates=m_w_lru_gates,
              b_lru_gates=m_b_lru_gates, lru_param=m_lru_param, w_rec_out=m_w_rec_out, norm_kv=m_norm_kv,
              w_kvf=m_w_kvf, b_forget=m_b_forget, w_q=m_w_q, w_o=m_w_o, norm_final=m_norm_final)
    M2 = dict(norm_mix=v_norm_mix, norm_ffn=v_norm_ffn, w_ffn_in=v_w_ffn_in, w_ffn_out=v_w_ffn_out,
              w_rec_in=v_w_rec_in, conv_w=v_conv_w, conv_b=v_conv_b, w_lru_gates=v_w_lru_gates,
              b_lru_gates=v_b_lru_gates, lru_param=v_lru_param, w_rec_out=v_w_rec_out, norm_kv=v_norm_kv,
              w_kvf=v_w_kvf, b_forget=v_b_forget, w_q=v_w_q, w_o=v_w_o, norm_final=v_norm_final)

    _, S, D = x.shape
    L = norm_mix.shape[0]
    NA, NBLK, BW, GS = w_lru_gates.shape
    C = NBLK * BW
    CS = C // N_CHIPS
    H = b_forget.shape[0]
    assert C == D and H * HEAD_DIM == D and H <= LANES
    chip = 2 * lax.axis_index("x") + lax.axis_index("y")

    small_a = jnp.concatenate([conv_w, conv_b[:, None], lru_param[:, None]], axis=1)
    small_a, b_gates = _gather_smalls("gather_smalls", [small_a, b_lru_gates])
    small_a = small_a.transpose(1, 2, 0, 3).reshape(NA, 6, C)
    b_gates = b_gates.transpose(1, 2, 0, 3).reshape(NA, NBLK, 1, N_CHIPS * GS)
    shards = [_stack3(P[w]).astype(BF16) for w in _BIG]
    core = lax.axis_index("c")
    chip_id = jnp.reshape(chip, (1,)).astype(jnp.int32)
    core_id = jnp.reshape(core, (1,)).astype(jnp.int32)
    me_core = jnp.stack([chip, core]).astype(jnp.int32)

    parts_of_layer = ("mix", "mix2", "ffn")

    def part_items(l, part):
        if part == "ffn":
            names, at = ["w_ffn_in", "w_ffn_out"], l
        elif l < NA:
            names, at = (["w_rec_in"] if part == "mix" else ["w_lru_gates", "w_rec_out"]), l
        else:
            names, at = ((["w_kvf"] if l == NA else []) + ["w_q"] if part == "mix" else ["w_o"]), l - NA
        return [(_BIG.index(n), 0 if n == "w_kvf" else at) for n in names]

    def stage_of(l, part):
        return (l, part) if l == 0 or part == "ffn" else (l, "mixer")

    def stage_items(st):
        l, part = st
        return [it for p in (("mix", "mix2") if part == "mixer" else (part,)) for it in part_items(l, p)]

    stages = [(0, p) for p in parts_of_layer] + [(l, p) for l in range(1, L) for p in ("mixer", "ffn")]
    items = [it for st in stages for it in stage_items(st)]
    ids_of = {st: [items.index(it) for it in stage_items(st)] for st in stages}
    bufs = [_place_own(f"place_{_BIG[w]}_{li}", shards[w], li, chip_id) for w, li in items]
    bufs, gather_sems = _gather_start("gather_start", shards, bufs, items, small_a)

    forwarding, fetched = {}, {}

    def layer_prefetch(l, part, after):
        st = stage_of(l, part)
        if l < L and st not in forwarding:
            got = _gather_wait(f"gather_wait_{st[1]}_{l}", shards, bufs, items, ids_of[st], gather_sems, after)
            forwarding[st] = _forward_start(f"forward_start_{st[1]}_{l}", got)

    def layer_weights(l, part, after):
        if l >= L:
            return None
        st = stage_of(l, part)
        if st not in fetched:
            layer_prefetch(l, part, after)
            got, sems = forwarding[st]
            got = _forward_wait(f"forward_wait_{st[1]}_{l}", got, sems, after)
            fetched[st] = {_BIG[items[i][0]]: g for i, g in zip(ids_of[st], got)}
        B = fetched[st]
        if part == "ffn":
            return dict(w_ffn_in=B["w_ffn_in"], w_ffn_out=B["w_ffn_out"].reshape(-1, D))
        if l < NA and part == "mix":
            return dict(w_rec_in=B["w_rec_in"], conv_w=small_a[l, :4], conv_b=small_a[l, 4:5])
        if l < NA:
            return dict(w_gates=B["w_lru_gates"].reshape(N_CHIPS, NBLK, BW, GS).transpose(1, 2, 0, 3).reshape(
                NBLK, BW, N_CHIPS * GS), b_gates=b_gates[l], w_rec_out=B["w_rec_out"].reshape(C, D),
                lru_param=small_a[l, 5:6])
        if part == "mix2":
            return dict(w_o=B["w_o"].reshape(D, D))
        W = dict(w_q=B["w_q"].reshape(D, D))
        if l == NA:
            w_kvf_full = B["w_kvf"].transpose(1, 0, 2).reshape(D, -1)
            W.update(norm_kv=norm_kv[None], w_kv=w_kvf_full[:, :2 * D],
                     w_f=_pad_lanes(w_kvf_full[:, 2 * D:], LANES), b_f=_pad_lanes(b_forget[None], LANES))
        return W

    G_small = {l: {} for l in range(L)}
    stash = {st: {} for st in stages}
    pending = {}
    reducing = []

    def finish_reduce(after):
        st, its, grads, others, sems = reducing.pop()
        l, part = st
        others = _reduce_wait(f"reduce_wait_{part}_{l}", grads, others, sems, after)
        parts = [_sum_cores(f"sum_cores_{l}_{_BIG[w]}", g, o, core_id) for (w, _), g, o in zip(its, grads, others)]
        recv, sems, token = _scatter_start(f"scatter_start_{part}_{l}", parts)
        pending[st] = (parts, recv, sems)
        return token

    def layer_grads(l, part, G_part):
        G_small[l].update(G_part)
        st = stage_of(l, part)
        stash[st].update(G_part)
        if st[1] == "mixer" and part != "mix":
            return None
        G = stash[st]
        late = {"ffn": "w_ffn_in", "mix": "norm_mix"}.get(part) or ("w_gates" if l < NA else "w_o")
        after = finish_reduce(G_part[late]) if reducing else jnp.zeros((SUBLANES, LANES), F32)
        by_name = dict(
            w_ffn_in=lambda: G["w_ffn_in"], w_ffn_out=lambda: G["w_ffn_out"].reshape(N_CHIPS, -1, D),
            w_rec_in=lambda: G["w_rec_in"],
            w_lru_gates=lambda: G["w_gates"].reshape(NBLK, BW, N_CHIPS, GS).transpose(2, 0, 1, 3).reshape(
                N_CHIPS, NBLK * BW, GS),
            w_rec_out=lambda: G["w_rec_out"].reshape(N_CHIPS, -1, D),
            w_kvf=lambda: jnp.concatenate([G["w_kv"].astype(F32), G["w_f"][:, :H]], axis=1).reshape(
                D, N_CHIPS, -1).transpose(1, 0, 2).astype(BF16),
            w_q=lambda: G["w_q"].reshape(N_CHIPS, -1, D), w_o=lambda: G["w_o"].reshape(N_CHIPS, -1, D))
        its = stage_items(st)
        grads = [by_name[_BIG[w]]() for w, _ in its]
        others, sems, token = _reduce_start(f"reduce_start_{st[1]}_{l}", grads, after)
        reducing.append((st, its, grads, others, sems))
        return finish_reduce(token) if l == 0 else token

    gains = dict(mix=[norm_mix[l][None] for l in range(L)], ffn=[norm_ffn[l][None] for l in range(L)],
                 final=norm_final[None])
    loss_row, grad_x, dg_final = _local_step(x.reshape(S, D), loss_target.reshape(S, D), gains,
                                             layer_weights, layer_prefetch, layer_grads)

    rows = [*[G_small[l]["norm_mix"] for l in range(L)], *[G_small[l]["norm_ffn"] for l in range(L)],
            G_small[NA]["norm_kv"], dg_final, _pad_lanes(G_small[NA]["b_f"], D), _pad_lanes(loss_row, D)]
    for a in range(NA):
        rows += [G_small[a][n] for n in ("conv_w", "conv_b", "b_gi", "b_gr", "lru_param")]
    packed = jnp.concatenate(rows, axis=0)
    tot = _sum_slots("sum_small", _gather_all("gather_small", packed))
    loss = tot[2 * L + 3, 0]
    g_rep = jnp.concatenate([tot[:2 * L + 2], tot[2 * L + 2:2 * L + 3]], axis=0)
    base = 2 * L + 4
    g_sh = []
    for a in range(NA):
        blk = lax.dynamic_slice_in_dim(tot[base + 8 * a:base + 8 * a + 8], chip * CS, CS, axis=1)
        gi = tot[base + 8 * a + 5].reshape(NBLK, BW)
        gr = tot[base + 8 * a + 6].reshape(NBLK, BW)
        bl = lax.dynamic_slice_in_dim(jnp.concatenate([gi, gr], axis=1), chip * GS, GS, axis=1)
        g_sh += [blk[:5], bl.reshape(-1, CS), blk[7:8]]
    g_sh = jnp.concatenate(g_sh, axis=0)
    nrow = g_sh.shape[0] // NA

    def pack_rep(T):
        return jnp.concatenate([T["norm_mix"], T["norm_ffn"], T["norm_kv"][None], T["norm_final"][None],
                                _pad_lanes(T["b_forget"][None], D)], axis=0)

    def pack_sh(T):
        return jnp.concatenate([jnp.concatenate([T["conv_w"][a], T["conv_b"][a][None],
                                                 T["b_lru_gates"][a].reshape(-1, CS), T["lru_param"][a][None]], axis=0)
                                for a in range(NA)], axis=0)

    rep = _adamw("adamw_replicated", [g_rep], pack_rep(P), pack_rep(M1), pack_rep(M2))
    shd = _adamw("adamw_small_sharded", [g_sh], pack_sh(P), pack_sh(M1), pack_sh(M2))

    def unpack_rep(t):
        return dict(norm_mix=t[:L], norm_ffn=t[L:2 * L], norm_kv=t[2 * L], norm_final=t[2 * L + 1],
                    b_forget=t[2 * L + 2, :H])

    def unpack_sh(t):
        t = t.reshape(NA, nrow, CS)
        return dict(conv_w=t[:, :4], conv_b=t[:, 4], b_lru_gates=t[:, 5:nrow - 1].reshape(NA, NBLK, GS),
                    lru_param=t[:, nrow - 1])

    full = [lax.empty(sh.shape, F32) for sh in shards]
    for st in reversed(stages):
        l, part = st
        parts, recv, sems = pending[st]
        recv = _scatter_wait(f"scatter_wait_{part}_{l}", parts, recv, sems)
        for (w, li), own, r in zip(stage_items(st), parts, recv):
            full[w] = _sum_chips(f"sum_chips_{l}_{_BIG[w]}", r, own, full[w], li, me_core)
    full = _share_d2d("share_d2d", full)
    big = {w: _adamw(f"adamw_{w}", [g.reshape(P[w].shape)], P[w], M1[w], M2[w]) for w, g in zip(_BIG, full)}

    outs = []
    for i in range(4):
        small = {**unpack_rep(rep[i]), **unpack_sh(shd[i])}
        outs.append([big[w][i] if w in big else small[w] for w in _WEIGHTS])
    return (loss, grad_x.reshape(1, S, D), *outs[0], *outs[1], *outs[2], *outs[3])
```

```python
import functools
import math

import jax
import jax.numpy as jnp
from jax import lax
from jax.experimental import pallas as pl
from jax.experimental.pallas import tpu as pltpu

F32 = jnp.float32
BF16 = jnp.bfloat16

EPS = 1e-6
LRU_C = 8.0
HEAD_DIM = 64
LANES = 128
SUBLANES = 8
VMEM_LIMIT = 48 * 1024 * 1024
N_CHIPS = 4
N_DEV = 8

ADAM_LR = 0.001
ADAM_B1 = 0.9
ADAM_B2 = 0.999
ADAM_EPS = 1e-08
ADAM_WD = 0.01
ADAM_STEP = 10

_NN = (((1,), (0,)), ((), ()))
_NT = (((1,), (1,)), ((), ()))
_TN = (((0,), (0,)), ((), ()))
_DN = {"nn": _NN, "nt": _NT, "tn": _TN}
MESH = pl.DeviceIdType.MESH


def _hbm_out(shape, dtype):
    return pltpu.HBM(shape, dtype)


def _params(sem):
    return pltpu.CompilerParams(dimension_semantics=sem, vmem_limit_bytes=VMEM_LIMIT)


def _tile(n, want):
    if n <= want:
        return n
    t = (want // LANES) * LANES
    while t >= LANES:
        if n % t == 0:
            return t
        t -= LANES
    return n


def _sigmoid(x):
    return 1.0 / (1.0 + jnp.exp(-x))


def _sigmoid_t(x):
    return 0.5 * jnp.tanh(0.5 * x) + 0.5


def _softplus(x):
    return jnp.maximum(x, 0.0) + jnp.log(1.0 + jnp.exp(-jnp.abs(x)))


_GELU_C = math.sqrt(2.0 / math.pi)


def _gelu_and_grad(x):
    inner = _GELU_C * (x + 0.044715 * x * x * x)
    t = jnp.tanh(inner)
    g = 0.5 * x * (1.0 + t)
    dg = 0.5 * (1.0 + t) + 0.5 * x * (1.0 - t * t) * _GELU_C * (1.0 + 3.0 * 0.044715 * x * x)
    return g, dg


def _rms(x):
    return lax.rsqrt(jnp.mean(x * x, axis=-1, keepdims=True) + EPS)


def _rms_bwd(dy, x, g):
    r = _rms(x)
    xr = x * r
    dyg = dy * g
    return r * dyg - xr * (r * jnp.mean(dyg * xr, axis=-1, keepdims=True)), jnp.sum(dy * xr, axis=0, keepdims=True)


def _mm(name, mode, a, b, *, grid, a_spec, b_spec, out_shape, out_dtype, out_spec, nk=1,
        res=None, res_spec=None, bias=None, bias_spec=None, scale=None, norm_gain=None, norm_bwd=None):
    dn = _DN[mode]
    has_res, has_bias = res is not None, bias is not None
    blk = tuple(d for d in out_spec.block_shape if d is not None)
    vec = pl.BlockSpec((1, blk[-1]), lambda *g: (0, 0))
    a_specs = a_spec if isinstance(a_spec, list) else [a_spec]
    b_specs = b_spec if isinstance(b_spec, list) else [b_spec]
    npair = len(a_specs)
    n_in = 2 * npair + int(has_res) + int(has_bias) + (1 if norm_gain is not None else 0) + (3 if norm_bwd else 0)

    def body(*refs):
        p = 2 * npair
        r_ref = refs[p] if has_res else None
        p += int(has_res)
        bias_ref = refs[p] if has_bias else None
        p += int(has_bias)
        extra = refs[p:n_in]
        outs = refs[n_in:]
        o_ref = outs[0]
        part = lax.dot_general(refs[0][...], refs[npair][...], dn, preferred_element_type=F32)
        for t in range(1, npair):
            part = part + lax.dot_general(refs[t][...], refs[npair + t][...], dn, preferred_element_type=F32)

        def finish(acc):
            if scale is not None:
                acc = acc * scale
            if has_bias:
                acc = acc + bias_ref[...]
            if has_res:
                acc = r_ref[...] + acc
            if norm_bwd:
                h_ref, g_ref, dh_ref = extra
                dx, dg = _rms_bwd(acc, h_ref[...], g_ref[...])
                acc = dh_ref[...] + dx
                outs[1][...] = acc.astype(BF16)
                outs[2][...] = dg
            if norm_gain is not None:
                outs[1][...] = (acc * _rms(acc) * extra[0][...]).astype(BF16)
            o_ref[...] = acc.astype(o_ref.dtype)

        if nk == 1:
            finish(part)
        else:
            acc_ref = refs[-1]
            k = pl.program_id(2)

            @pl.when(k == 0)
            def _():
                acc_ref[...] = part

            @pl.when(k > 0)
            def _():
                acc_ref[...] += part

            @pl.when(k == nk - 1)
            def _():
                finish(acc_ref[...])

    ins, specs = [a] * npair + [b] * npair, a_specs + b_specs
    if has_res:
        ins.append(res)
        specs.append(res_spec)
    if has_bias:
        ins.append(bias)
        specs.append(bias_spec)
    out_specs, out_shapes = [out_spec], [_hbm_out(out_shape, out_dtype)]
    if norm_gain is not None:
        ins.append(norm_gain)
        specs.append(vec)
        out_specs.append(out_spec)
        out_shapes.append(_hbm_out(out_shape, BF16))
    if norm_bwd:
        h, g, dh = norm_bwd
        ins += [h, g, dh]
        specs += [out_spec, vec, out_spec]
        out_specs += [out_spec, pl.BlockSpec((None, 1, blk[-1]), lambda i, *rest: (i, 0, 0))]
        out_shapes += [_hbm_out(out_shape, BF16), _hbm_out((grid[0], 1, blk[-1]), F32)]
    sem = ("parallel", "parallel") + (("arbitrary",) if len(grid) == 3 else ())
    single = len(out_specs) == 1
    return pl.pallas_call(
        body, name=name, grid=grid, in_specs=specs, out_specs=out_specs[0] if single else out_specs,
        out_shape=out_shapes[0] if single else out_shapes,
        scratch_shapes=[pltpu.VMEM(blk, F32)] if nk > 1 else [],
        compiler_params=_params(sem),
    )(*ins)


def _mm_nn(name, a, b, *, b_lead=(), out_dtype, tm=512, tn=512, res=None, bias=None, scale=None, norm_gain=None):
    M, K = a.shape
    N = b.shape[-1]
    tm, tn = _tile(M, tm), _tile(N, tn)
    nl = len(b_lead)
    return _mm(
        name, "nn", a, b, grid=(M // tm, N // tn),
        a_spec=pl.BlockSpec((tm, K), lambda i, j: (i, 0)),
        b_spec=pl.BlockSpec((None,) * nl + (K, tn), lambda i, j: tuple(b_lead) + (0, j)),
        out_shape=(M, N), out_dtype=out_dtype, out_spec=pl.BlockSpec((tm, tn), lambda i, j: (i, j)),
        res=res, res_spec=pl.BlockSpec((tm, tn), lambda i, j: (i, j)),
        bias=bias, bias_spec=pl.BlockSpec((1, tn), lambda i, j: (0, j)), scale=scale, norm_gain=norm_gain)


def _mm_nt(name, a, b, *, b_lead=(), out_dtype, tm=512, tn=512, tk=2048, res=None, norm_bwd=None):
    M, K = a.shape
    N = b.shape[-2]
    tm, tn, tk = _tile(M, tm), _tile(N, tn), _tile(K, tk)
    nk = K // tk
    nl = len(b_lead)
    return _mm(
        name, "nt", a, b, grid=(M // tm, N // tn, nk), nk=nk,
        a_spec=pl.BlockSpec((tm, tk), lambda i, j, k: (i, k)),
        b_spec=pl.BlockSpec((None,) * nl + (tn, tk), lambda i, j, k: tuple(b_lead) + (j, k)),
        out_shape=(M, N), out_dtype=out_dtype, out_spec=pl.BlockSpec((tm, tn), lambda i, j, k: (i, j)),
        res=res, res_spec=pl.BlockSpec((tm, tn), lambda i, j, k: (i, j)), norm_bwd=norm_bwd)


def _mm_tn(name, a, b, *, out_dtype, tm=512, tn=512):
    S, M = a.shape
    N = b.shape[1]
    tm, tn = _tile(M, tm), _tile(N, tn)
    return _mm(
        name, "tn", a, b, grid=(M // tm, N // tn),
        a_spec=pl.BlockSpec((S, tm), lambda i, j: (0, i)),
        b_spec=pl.BlockSpec((S, tn), lambda i, j: (0, j)),
        out_shape=(M, N), out_dtype=out_dtype, out_spec=pl.BlockSpec((tm, tn), lambda i, j: (i, j)))


def _rmsnorm_fwd(name, h, g, tr=256):
    S, D = h.shape
    tr = _tile(S, tr)

    def body(h_ref, g_ref, o_ref):
        x = h_ref[...]
        r = lax.rsqrt(jnp.mean(x * x, axis=-1, keepdims=True) + EPS)
        o_ref[...] = (x * r * g_ref[...]).astype(o_ref.dtype)

    return pl.pallas_call(
        body, name=name, grid=(S // tr,),
        in_specs=[pl.BlockSpec((tr, D), lambda i: (i, 0)), pl.BlockSpec((1, D), lambda i: (0, 0))],
        out_specs=pl.BlockSpec((tr, D), lambda i: (i, 0)),
        out_shape=_hbm_out((S, D), BF16),
        compiler_params=_params(("parallel",)),
    )(h, g)


def _loss_head(name, h, target, g, tr=256):
    S, D = h.shape
    tr = _tile(S, tr)

    def body(h_ref, t_ref, g_ref, o_ref, ob_ref, dg_ref, loss_ref):
        i = pl.program_id(0)
        x = h_ref[...]
        gg = g_ref[...]
        r = lax.rsqrt(jnp.mean(x * x, axis=-1, keepdims=True) + EPS)
        xr = x * r
        err = xr * gg - t_ref[...]
        lpart = 0.5 * jnp.sum(jnp.mean(err * err, axis=-1, keepdims=True), axis=0, keepdims=True)
        dy = err * (1.0 / D)
        dyg = dy * gg
        dx = r * dyg - xr * (r * jnp.mean(dyg * xr, axis=-1, keepdims=True))
        o_ref[...] = dx
        ob_ref[...] = dx.astype(BF16)
        part = jnp.sum(dy * xr, axis=0, keepdims=True)
        lrow = jnp.broadcast_to(lpart, (1, LANES))

        @pl.when(i == 0)
        def _():
            dg_ref[...] = part
            loss_ref[...] = lrow

        @pl.when(i > 0)
        def _():
            dg_ref[...] += part
            loss_ref[...] += lrow

    row = pl.BlockSpec((tr, D), lambda i: (i, 0))
    vec = pl.BlockSpec((1, D), lambda i: (0, 0))
    return pl.pallas_call(
        body, name=name, grid=(S // tr,),
        in_specs=[row, row, vec], out_specs=[row, row, vec, pl.BlockSpec((1, LANES), lambda i: (0, 0))],
        out_shape=[_hbm_out((S, D), F32), _hbm_out((S, D), BF16),
                   _hbm_out((1, D), F32), _hbm_out((1, LANES), F32)],
        compiler_params=_params(("arbitrary",)),
    )(h, target, g)


def _swiglu_fwd(name, hn, w_in, tm=512):
    S, D = hn.shape
    FH = w_in.shape[-1]
    tm = _tile(S, tm)

    def body(x_ref, wg_ref, wu_ref, z_ref, a_ref):
        x = x_ref[...]
        zg = jnp.dot(x, wg_ref[...], preferred_element_type=F32)
        zu = jnp.dot(x, wu_ref[...], preferred_element_type=F32)
        sg = _sigmoid_t(zg)
        silu = zg * sg
        z_ref[0] = (zu * (sg * (1.0 + zg * (1.0 - sg)))).astype(z_ref.dtype)
        z_ref[1] = silu.astype(z_ref.dtype)
        a_ref[...] = (silu * zu).astype(a_ref.dtype)

    return pl.pallas_call(
        body, name=name, grid=(2, S // tm),
        in_specs=[pl.BlockSpec((tm, D), lambda j, i: (i, 0)),
                  pl.BlockSpec((None, D, FH), lambda j, i: (j, 0, 0)),
                  pl.BlockSpec((None, D, FH), lambda j, i: (j + 2, 0, 0))],
        out_specs=[pl.BlockSpec((2, tm, FH), lambda j, i: (0, i, j)), pl.BlockSpec((tm, FH), lambda j, i: (i, j))],
        out_shape=[_hbm_out((2, S, 2 * FH), BF16), _hbm_out((S, 2 * FH), BF16)],
        compiler_params=_params(("parallel", "parallel")),
    )(hn, w_in, w_in)


def _swiglu_bwd(name, dhb, w_out, z3, tm=512):
    S, D = dhb.shape
    F = w_out.shape[0]
    FH = F // 2
    tm = _tile(S, tm)

    def body(d_ref, w_ref, z_ref, dz_ref):
        d = lax.dot_general(d_ref[...], w_ref[...], _NT, preferred_element_type=F32)
        dz_ref[0] = (d * z_ref[0].astype(F32)).astype(dz_ref.dtype)
        dz_ref[1] = (d * z_ref[1].astype(F32)).astype(dz_ref.dtype)

    zspec = pl.BlockSpec((2, tm, FH), lambda j, i: (0, i, j))
    return pl.pallas_call(
        body, name=name, grid=(2, S // tm),
        in_specs=[pl.BlockSpec((tm, D), lambda j, i: (i, 0)), pl.BlockSpec((FH, D), lambda j, i: (j, 0)), zspec],
        out_specs=zspec, out_shape=_hbm_out((2, S, F), BF16),
        compiler_params=_params(("parallel", "parallel")),
    )(dhb, w_out, z3)


SCAN_ROWS = 64


def _group_scan(A, B, reverse):
    n = A.shape[0]
    sub = lax.broadcasted_iota(jnp.int32, A.shape, 0) % SUBLANES
    for d in (1, 2, 4):
        if reverse:
            A_sh, B_sh = pltpu.roll(A, n - d, 0), pltpu.roll(B, n - d, 0)
            keep = sub < SUBLANES - d
        else:
            A_sh, B_sh = pltpu.roll(A, d, 0), pltpu.roll(B, d, 0)
            keep = sub >= d
        B = jnp.where(keep, A * B_sh + B, B)
        A = jnp.where(keep, A * A_sh, A)
    return A, B


def _block_scan(a, u, carry, reverse):
    A, B = _group_scan(a, u, reverse)
    ng = a.shape[0] // SUBLANES
    out = [None] * ng
    order = range(ng - 1, -1, -1) if reverse else range(ng)
    for gi in order:
        sl = slice(gi * SUBLANES, (gi + 1) * SUBLANES)
        hg = A[sl] * carry + B[sl]
        out[gi] = hg
        carry = hg[0:1] if reverse else hg[SUBLANES - 1:SUBLANES]
    return jnp.concatenate(out, axis=0), carry


def _lru_gates(rc, gip, grp, sp):
    gi = _sigmoid_t(gip)
    gr = _sigmoid_t(grp)
    la = -LRU_C * gr * sp
    a = jnp.exp(la)
    om = -jnp.tanh(la) * (a * a + 1.0)
    mult = jnp.sqrt(om)
    return gi, gr, a, mult


def _lru_fwd(name, proj, rc, gip, grp, lru_p, tc=256):
    S, C = rc.shape
    tc = _tile(C, tc)
    nb = S // SCAN_ROWS

    def body(gb_ref, rc_ref, gi_ref, gr_ref, l_ref, h_ref, m_ref):
        sp = _softplus(-l_ref[...])

        def step(b, carry):
            rows = pl.ds(pl.multiple_of(b * SCAN_ROWS, SCAN_ROWS), SCAN_ROWS)
            rcb = rc_ref[rows, :]
            gi, _, a, mult = _lru_gates(rcb, gi_ref[rows, :], gr_ref[rows, :], sp)
            h, carry = _block_scan(a, rcb * gi * mult, carry, False)
            h_ref[rows, :] = h
            gel, _ = _gelu_and_grad(gb_ref[rows, :])
            m_ref[rows, :] = (gel * h).astype(m_ref.dtype)
            return carry

        lax.fori_loop(0, nb, step, jnp.zeros((1, tc), F32))

    col = pl.BlockSpec((S, tc), lambda j: (0, j))
    return pl.pallas_call(
        body, name=name, grid=(C // tc,),
        in_specs=[col, col, col, col, pl.BlockSpec((1, tc), lambda j: (0, j))],
        out_specs=[col, col],
        out_shape=[_hbm_out((S, C), F32), _hbm_out((S, C), BF16)],
        compiler_params=_params(("parallel",)),
    )(proj, rc, gip, grp, lru_p)


def _lru_bwd(name, dm, proj, hrec, rc, gip, grp, lru_p, tc=256):
    S, C = rc.shape
    tc = _tile(C, tc)
    nb = S // SCAN_ROWS
    R = SCAN_ROWS

    def body(dm_ref, gb_ref, h_ref, rc_ref, gi_ref, gr_ref, l_ref,
             dgb_ref, dgi_ref, dgr_ref, drc_ref, dbi_ref, dbr_ref, dl_ref):
        lp = l_ref[...]
        sp = _softplus(-lp)
        row = lax.broadcasted_iota(jnp.int32, (R, tc), 0)
        zero = jnp.zeros((1, tc), F32)

        def step(t, carry):
            mu_in, s_i, s_r, s_sp = carry
            b = nb - 1 - t
            r0 = pl.multiple_of(b * R, R)
            rows = pl.ds(r0, R)
            rcb = rc_ref[rows, :]
            gi, gr, a, mult = _lru_gates(rcb, gi_ref[rows, :], gr_ref[rows, :], sp)
            gel, dgel = _gelu_and_grad(gb_ref[rows, :])
            dmb = dm_ref[rows, :]
            h = h_ref[rows, :]
            dgb_ref[rows, :] = (dmb * h * dgel).astype(dgb_ref.dtype)
            dh = dmb * gel
            mu, mu_out = _block_scan(a, a * dh, mu_in, True)
            mu_next = jnp.where(row == R - 1, mu_in, pltpu.roll(mu, R - 1, 0))
            lam = dh + mu_next
            p0 = pl.multiple_of(jnp.maximum(r0 - SUBLANES, 0), SUBLANES)
            prev = h_ref[pl.ds(p0, SUBLANES), :][SUBLANES - 1:SUBLANES]
            prev = jnp.where(b > 0, prev, 0.0)
            h_prev = jnp.where(row == 0, prev, pltpu.roll(h, 1, 0))
            da = lam * h_prev
            d_mult = lam * rcb * gi
            d_la = da * a - d_mult * (a * a) / mult
            d_grp = d_la * (-LRU_C * sp) * gr * (1.0 - gr)
            d_gip = lam * rcb * mult * gi * (1.0 - gi)
            dgr_ref[rows, :] = d_grp.astype(dgr_ref.dtype)
            dgi_ref[rows, :] = d_gip.astype(dgi_ref.dtype)
            drc_ref[rows, :] = lam * gi * mult
            s_i = s_i + jnp.sum(d_gip, axis=0, keepdims=True)
            s_r = s_r + jnp.sum(d_grp, axis=0, keepdims=True)
            s_sp = s_sp + jnp.sum(d_la * gr, axis=0, keepdims=True)
            return mu_out, s_i, s_r, s_sp

        _, s_i, s_r, s_sp = lax.fori_loop(0, nb, step, (zero, zero, zero, zero))
        dbi_ref[...] = s_i
        dbr_ref[...] = s_r
        dl_ref[...] = (-LRU_C * s_sp) * (-_sigmoid(-lp))

    col = pl.BlockSpec((S, tc), lambda j: (0, j))
    vec = pl.BlockSpec((1, tc), lambda j: (0, j))
    return pl.pallas_call(
        body, name=name, grid=(C // tc,),
        in_specs=[col, col, col, col, col, col, vec],
        out_specs=[col, col, col, col, vec, vec, vec],
        out_shape=[_hbm_out((S, C), BF16), _hbm_out((S, C), BF16),
                   _hbm_out((S, C), BF16), _hbm_out((S, C), F32),
                   _hbm_out((1, C), F32), _hbm_out((1, C), F32),
                   _hbm_out((1, C), F32)],
        compiler_params=_params(("parallel",)),
    )(dm, proj, hrec, rc, gip, grp, lru_p)


def _cumsum_rows(name, u, reverse):
    S, C = u.shape
    nb = S // SCAN_ROWS

    def body(u_ref, o_ref):
        def step(t, carry):
            b = nb - 1 - t if reverse else t
            rows = pl.ds(pl.multiple_of(b * SCAN_ROWS, SCAN_ROWS), SCAN_ROWS)
            ub = u_ref[rows, :]
            h, carry = _block_scan(jnp.ones_like(ub), ub, carry, reverse)
            o_ref[rows, :] = h
            return carry

        lax.fori_loop(0, nb, step, jnp.zeros((1, C), F32))

    spec = pl.BlockSpec((S, C), lambda i: (0, 0))
    return pl.pallas_call(
        body, name=name, grid=(1,), in_specs=[spec], out_specs=spec,
        out_shape=_hbm_out((S, C), F32),
        compiler_params=_params(("arbitrary",)),
    )(u)


def _shift_down(x, k):
    row = lax.broadcasted_iota(jnp.int32, x.shape, 0)
    return jnp.where(row >= k, pltpu.roll(x, k, 0), 0.0)


def _shift_up(x, k):
    n = x.shape[0]
    row = lax.broadcasted_iota(jnp.int32, x.shape, 0)
    return jnp.where(row < n - k, pltpu.roll(x, n - k, 0), 0.0)


def _conv_fwd(name, proj, w, b, tc=256):
    S, C2 = proj.shape
    C = C2 // 2
    tc = _tile(C, tc)
    off = C // tc

    def body(x_ref, w_ref, b_ref, o_ref, ob_ref):
        x = x_ref[...]
        out = b_ref[...] + w_ref[3:4, :] * x
        for k in (1, 2, 3):
            out = out + w_ref[3 - k:4 - k, :] * _shift_down(x, k)
        o_ref[...] = out
        ob_ref[...] = out.astype(BF16)

    col = pl.BlockSpec((S, tc), lambda j: (0, j))
    return pl.pallas_call(
        body, name=name, grid=(C // tc,),
        in_specs=[pl.BlockSpec((S, tc), lambda j: (0, off + j)),
                  pl.BlockSpec((4, tc), lambda j: (0, j)), pl.BlockSpec((1, tc), lambda j: (0, j))],
        out_specs=[col, col],
        out_shape=[_hbm_out((S, C), F32), _hbm_out((S, C), BF16)],
        compiler_params=_params(("parallel",)),
    )(proj, w, b)


def _conv_bwd(name, drc, proj, w, tc=256):
    S, C = drc.shape
    tc = _tile(C, tc)
    off = C // tc

    def body(y_ref, x_ref, w_ref, dx_ref, dw_ref, db_ref):
        y = y_ref[...]
        x = x_ref[...]
        dx = w_ref[3:4, :] * y
        dw_ref[3:4, :] = jnp.sum(y * x, axis=0, keepdims=True)
        for k in (1, 2, 3):
            dx = dx + w_ref[3 - k:4 - k, :] * _shift_up(y, k)
            dw_ref[3 - k:4 - k, :] = jnp.sum(y * _shift_down(x, k), axis=0, keepdims=True)
        dx_ref[...] = dx.astype(dx_ref.dtype)
        db_ref[...] = jnp.sum(y, axis=0, keepdims=True)

    col = pl.BlockSpec((S, tc), lambda j: (0, j))
    return pl.pallas_call(
        body, name=name, grid=(C // tc,),
        in_specs=[col, pl.BlockSpec((S, tc), lambda j: (0, off + j)), pl.BlockSpec((4, tc), lambda j: (0, j))],
        out_specs=[col, pl.BlockSpec((4, tc), lambda j: (0, j)), pl.BlockSpec((1, tc), lambda j: (0, j))],
        out_shape=[_hbm_out((S, C), BF16), _hbm_out((4, C), F32),
                   _hbm_out((1, C), F32)],
        compiler_params=_params(("parallel",)),
    )(drc, proj, w)


def _gates_fwd(name, rcb, wg, bg):
    S, C = rcb.shape
    nblk, bw, _ = wg.shape

    def body(x_ref, w_ref, b_ref, gi_ref, gr_ref):
        g = jnp.dot(x_ref[...], w_ref[...], preferred_element_type=F32) + b_ref[...]
        gi_ref[...] = g[:, :bw]
        gr_ref[...] = g[:, bw:]

    col = pl.BlockSpec((S, bw), lambda n: (0, n))
    return pl.pallas_call(
        body, name=name, grid=(nblk,),
        in_specs=[col, pl.BlockSpec((None, bw, 2 * bw), lambda n: (n, 0, 0)),
                  pl.BlockSpec((None, 1, 2 * bw), lambda n: (n, 0, 0))],
        out_specs=[col, col],
        out_shape=[_hbm_out((S, C), F32), _hbm_out((S, C), F32)],
        compiler_params=_params(("parallel",)),
    )(rcb, wg, bg)


def _gates_bwd(name, dgi, dgr, rcb, wg, drc1):
    S, C = rcb.shape
    nblk, bw, _ = wg.shape

    def body(dgi_ref, dgr_ref, x_ref, w_ref, d1_ref, drc_ref, dw_ref):
        w = w_ref[...]
        x = x_ref[...]
        di, dr = dgi_ref[...], dgr_ref[...]
        drc_ref[...] = (d1_ref[...]
                        + lax.dot_general(di, w[:, :bw], _NT, preferred_element_type=F32)
                        + lax.dot_general(dr, w[:, bw:], _NT, preferred_element_type=F32))
        dw_ref[:, :bw] = lax.dot_general(x, di, _TN, preferred_element_type=F32).astype(dw_ref.dtype)
        dw_ref[:, bw:] = lax.dot_general(x, dr, _TN, preferred_element_type=F32).astype(dw_ref.dtype)

    col = pl.BlockSpec((S, bw), lambda n: (0, n))
    wspec = pl.BlockSpec((None, bw, 2 * bw), lambda n: (n, 0, 0))
    return pl.pallas_call(
        body, name=name, grid=(nblk,),
        in_specs=[col, col, col, wspec, col], out_specs=[col, wspec],
        out_shape=[_hbm_out((S, C), F32), _hbm_out((nblk, bw, 2 * bw), BF16)],
        compiler_params=_params(("parallel",)),
    )(dgi, dgr, rcb, wg, drc1)


def _att_tile(S):
    return next(t for t in (512, 256, 128) if S % t == 0)


def _head_lanes(shape):
    return lax.broadcasted_iota(jnp.int32, shape, len(shape) - 1) < HEAD_DIM


def _key_bias(c_ref, rows, hh):
    return jnp.broadcast_to(c_ref[rows, hh:hh + 1], (rows.size, LANES))


def _over_keys(x, op):
    n = x.shape[0]
    while n > SUBLANES:
        n //= 2
        x = op(x[:n], x[n:2 * n])
    return (jnp.max if op is jnp.maximum else jnp.sum)(x, axis=0, keepdims=True)


def _causal_t(T, cc):
    r = lax.broadcasted_iota(jnp.int32, (T, LANES), 0)
    c = lax.broadcasted_iota(jnp.int32, (T, LANES), 1) + cc * LANES
    return r <= c


def _attn_fwd(name, q, kv, c3):
    S, D = q.shape
    HP = D // LANES
    T = _att_tile(S)
    nq = S // T
    NC = T // LANES

    def body(q_ref, k_ref, v_ref, c_ref, o_ref, of_ref, lse_ref, bias, vT, acc, m_scr, l_scr):
        def prologue(i, _):
            rows = pl.ds(pl.multiple_of(i * T, T), T)
            for hh in range(2):
                bias[hh, rows, :] = _key_bias(c_ref, rows, hh)
            vT[i] = v_ref[rows, :].astype(F32).T.astype(BF16)
            return 0

        lax.fori_loop(0, nq, prologue, 0)

        def q_step(qi, _):
            q0 = pl.multiple_of(qi * T, T)
            qb = q_ref[pl.ds(q0, T), :]
            m_scr[...] = jnp.full(m_scr.shape, -jnp.inf, F32)
            l_scr[...] = jnp.zeros(l_scr.shape, F32)
            acc[...] = jnp.zeros(acc.shape, F32)

            def tile(kj, masked):
                ks = pl.ds(pl.multiple_of(kj * T, T), T)
                kf = k_ref[ks, :].astype(F32)
                first = _head_lanes(kf.shape)
                kms = [jnp.where(first if hh == 0 else jnp.logical_not(first), kf, 0.0).astype(BF16) for hh in range(2)]
                sTs = [lax.dot_general(km, qb, _NT, preferred_element_type=F32) for km in kms]
                for hh in range(2):
                    b = bias[hh, ks, :]
                    ps = []
                    for cc in range(NC):
                        cols = slice(cc * LANES, (cc + 1) * LANES)
                        s = sTs[hh][:, cols] + b
                        if masked:
                            s = jnp.where(_causal_t(T, cc), s, -jnp.inf)
                        m_old = m_scr[hh, cc]
                        m_new = jnp.maximum(m_old, _over_keys(s, jnp.maximum))
                        alpha = jnp.exp(m_old - m_new)
                        p = jnp.exp(s - m_new)
                        l_scr[hh, cc] = alpha * l_scr[hh, cc] + _over_keys(p, jnp.add)
                        m_scr[hh, cc] = m_new
                        ps.append(p.astype(BF16))
                        acc[hh, :, cols] = acc[hh, :, cols] * alpha
                    acc[hh] += jnp.dot(vT[kj, hh * HEAD_DIM:(hh + 1) * HEAD_DIM, :], jnp.concatenate(ps, axis=1),
                                       preferred_element_type=F32)

            def inner(kj, _):
                tile(kj, False)
                return 0

            lax.fori_loop(0, qi, inner, 0)
            tile(qi, True)
            outs = []
            for hh in range(2):
                inv = jnp.concatenate([1.0 / l_scr[hh, cc] for cc in range(NC)], axis=1)
                outs.append(acc[hh] * inv)
                for cc in range(NC):
                    lse_ref[hh:hh + 1, pl.ds(q0 + cc * LANES, LANES)] = m_scr[hh, cc] + jnp.log(l_scr[hh, cc])
            out = jnp.concatenate(outs, axis=0).T
            o_ref[pl.ds(q0, T), :] = out.astype(o_ref.dtype)
            of_ref[pl.ds(q0, T), :] = out
            return 0

        lax.fori_loop(0, nq, q_step, 0)

    blk = lambda off: pl.BlockSpec((S, LANES), lambda p: (0, off + p))
    return pl.pallas_call(
        body, name=name, grid=(HP,),
        in_specs=[blk(0), blk(0), blk(HP), pl.BlockSpec((None, S, 2), lambda p: (p, 0, 0))],
        out_specs=[blk(0), blk(0), pl.BlockSpec((None, 2, S), lambda p: (p, 0, 0))],
        out_shape=[_hbm_out((S, D), BF16), _hbm_out((S, D), F32),
                   _hbm_out((HP, 2, S), F32)],
        scratch_shapes=[pltpu.VMEM((2, S, LANES), F32), pltpu.VMEM((nq, LANES, T), BF16),
                        pltpu.VMEM((2, HEAD_DIM, T), F32), pltpu.VMEM((2, NC, 1, LANES), F32),
                        pltpu.VMEM((2, NC, 1, LANES), F32)],
        compiler_params=_params(("parallel",)),
    )(q, kv, kv, c3)


def _attn_bwd(name, q, kv, c3, of, do, lse3):
    S, D = q.shape
    HP = D // LANES
    T = _att_tile(S)
    nq = S // T
    NC = T // LANES
    scale = HEAD_DIM ** -0.5

    def body(q_ref, k_ref, v_ref, c_ref, of_ref, do_ref, lse_ref,
             dq_ref, dk_ref, dv_ref, dck_ref, drq_ref, bias, kT, dqT, delta, dr_scr, dk_acc, dv_acc, dc_acc):
        def prologue(i, _):
            rows = pl.ds(pl.multiple_of(i * T, T), T)
            for hh in range(2):
                bias[hh, rows, :] = _key_bias(c_ref, rows, hh)
            kT[i] = k_ref[rows, :].astype(F32).T.astype(BF16)
            prodT = (do_ref[rows, :].astype(F32) * of_ref[rows, :]).T
            for hh in range(2):
                delta[hh:hh + 1, rows] = jnp.sum(prodT[hh * HEAD_DIM:(hh + 1) * HEAD_DIM], axis=0, keepdims=True)
            dqT[i] = jnp.zeros((LANES, T), F32)
            return 0

        lax.fori_loop(0, nq, prologue, 0)
        dr_scr[...] = jnp.zeros(dr_scr.shape, F32)

        def kv_step(kj, _):
            ks = pl.ds(pl.multiple_of(kj * T, T), T)
            kf = k_ref[ks, :].astype(F32)
            vf = v_ref[ks, :].astype(F32)
            first = _head_lanes(kf.shape)
            masks = [first, jnp.logical_not(first)]
            kms = [jnp.where(m, kf, 0.0).astype(BF16) for m in masks]
            vms = [jnp.where(m, vf, 0.0).astype(BF16) for m in masks]

            for acc in (dk_acc, dv_acc, dc_acc):
                acc[...] = jnp.zeros(acc.shape, F32)

            def tile(qi, masked):
                q0 = pl.multiple_of(qi * T, T)
                qb = q_ref[pl.ds(q0, T), :]
                dob = do_ref[pl.ds(q0, T), :]
                sTs = [lax.dot_general(km, qb, _NT, preferred_element_type=F32) for km in kms]
                dpTs = [lax.dot_general(vm, dob, _NT, preferred_element_type=F32) for vm in vms]
                for hh in range(2):
                    b = bias[hh, ks, :]
                    head = slice(hh * HEAD_DIM, (hh + 1) * HEAD_DIM)
                    ps, dss = [], []
                    for cc in range(NC):
                        cols = slice(cc * LANES, (cc + 1) * LANES)
                        at = pl.ds(q0 + cc * LANES, LANES)
                        p = jnp.exp(sTs[hh][:, cols] + b - lse_ref[hh:hh + 1, at])
                        if masked:
                            p = jnp.where(_causal_t(T, cc), p, 0.0)
                        ds = p * (dpTs[hh][:, cols] - delta[hh:hh + 1, at])
                        ps.append(p.astype(BF16))
                        dss.append(ds.astype(BF16))
                        dc_acc[hh] += ds
                        dr_scr[hh:hh + 1, at] += _over_keys(ds, jnp.add)
                    pT = jnp.concatenate(ps, axis=1)
                    dsT = jnp.concatenate(dss, axis=1)
                    dv_acc[hh] += jnp.dot(pT, dob, preferred_element_type=F32)
                    dk_acc[hh] += jnp.dot(dsT, qb, preferred_element_type=F32)
                    dqT[qi, head, :] += jnp.dot(kT[kj, head, :], dsT, preferred_element_type=F32)

            def inner(qi, _):
                tile(qi, False)
                return 0

            tile(kj, True)
            lax.fori_loop(kj + 1, nq, inner, 0)
            dk_ref[ks, :] = jnp.where(first, dk_acc[0], dk_acc[1])
            dv_ref[ks, :] = jnp.where(first, dv_acc[0], dv_acc[1])
            for hh in range(2):
                dck_ref[hh:hh + 1, ks] = -jnp.sum(dc_acc[hh].T, axis=0, keepdims=True)
            return 0

        lax.fori_loop(0, nq, kv_step, 0)

        def epilogue(i, _):
            rows = pl.ds(pl.multiple_of(i * T, T), T)
            dq_ref[rows, :] = (dqT[i].T * scale).astype(dq_ref.dtype)
            return 0

        lax.fori_loop(0, nq, epilogue, 0)
        drq_ref[...] = dr_scr[...]

    blk = lambda off: pl.BlockSpec((S, LANES), lambda p: (0, off + p))
    row_spec = pl.BlockSpec((None, 2, S), lambda p: (p, 0, 0))
    return pl.pallas_call(
        body, name=name, grid=(HP,),
        in_specs=[blk(0), blk(0), blk(HP), pl.BlockSpec((None, S, 2), lambda p: (p, 0, 0)), blk(0), blk(0), row_spec],
        out_specs=[blk(0), blk(0), blk(0), row_spec, row_spec],
        out_shape=[_hbm_out((S, D), BF16), _hbm_out((S, D), F32),
                   _hbm_out((S, D), F32), _hbm_out((HP, 2, S), F32),
                   _hbm_out((HP, 2, S), F32)],
        scratch_shapes=[pltpu.VMEM((2, S, LANES), F32), pltpu.VMEM((nq, LANES, T), BF16),
                        pltpu.VMEM((nq, LANES, T), F32), pltpu.VMEM((2, S), F32), pltpu.VMEM((2, S), F32)]
        + [pltpu.VMEM((2, T, LANES), F32)] * 3,
        compiler_params=_params(("parallel",)),
    )(q, kv, kv, c3, of, do, lse3)


def _logsig_fwd(name, f):
    S, C = f.shape

    def body(f_ref, o_ref):
        o_ref[...] = -_softplus(-f_ref[...])

    spec = pl.BlockSpec((S, C), lambda i: (0, 0))
    return pl.pallas_call(body, name=name, grid=(1,), in_specs=[spec], out_specs=spec,
                          out_shape=_hbm_out((S, C), F32),
                          compiler_params=_params(("arbitrary",)))(f)


def _logsig_bwd(name, dls, f):
    S, C = f.shape

    def body(d_ref, f_ref, o_ref, s_ref):
        df = d_ref[...] * _sigmoid(-f_ref[...])
        o_ref[...] = df.astype(o_ref.dtype)
        s_ref[...] = jnp.sum(df, axis=0, keepdims=True)

    spec = pl.BlockSpec((S, C), lambda i: (0, 0))
    return pl.pallas_call(body, name=name, grid=(1,), in_specs=[spec, spec],
                          out_specs=[spec, pl.BlockSpec((1, C), lambda i: (0, 0))],
                          out_shape=[_hbm_out((S, C), BF16), _hbm_out((1, C), F32)],
                          compiler_params=_params(("arbitrary",)))(dls, f)


def _add_cast(name, parts, out_dtype, tr=256):
    S, C = parts[0].shape
    tr = _tile(S, tr)
    n = len(parts)

    def body(*refs):
        acc = refs[0][...].astype(F32)
        for r in refs[1:n]:
            acc = acc + r[...].astype(F32)
        refs[n][...] = acc.astype(out_dtype)

    spec = pl.BlockSpec((tr, C), lambda i: (i, 0))
    return pl.pallas_call(body, name=name, grid=(S // tr,), in_specs=[spec] * n, out_specs=spec,
                          out_shape=_hbm_out((S, C), out_dtype),
                          compiler_params=_params(("parallel",)))(*parts)


def _local_step(x, target, gains, layer_weights, layer_prefetch, layer_grads):
    S, D = x.shape
    HP = D // LANES
    scale = HEAD_DIM ** -0.5
    tx = _tile(S, 256)
    saved = []
    h = x
    l = 0
    kv = c3 = f_pre = hn_kv = h_kv = None
    while True:
        W = layer_weights(l, "mix", h)
        if W is None:
            break
        recurrent = "w_rec_in" in W
        if l == 0:
            xn = _rmsnorm_fwd("mix_norm_0", h, gains["mix"][0])
        if recurrent:
            CH = W["w_rec_in"].shape[-1]
            C = 2 * CH
            proj = _mm(f"rec_in_{l}", "nn", xn, W["w_rec_in"], grid=(1, N_CHIPS),
                       a_spec=pl.BlockSpec((S, D), lambda i, j: (0, 0)),
                       b_spec=pl.BlockSpec((None, D, CH), lambda i, j: (j, 0, 0)),
                       out_shape=(S, 2 * C), out_dtype=F32,
                       out_spec=pl.BlockSpec((S, CH), lambda i, j: (0, j)))
            layer_prefetch(l, "mix2", proj)
            rc, rcb = _conv_fwd(f"conv_{l}", proj, W["conv_w"], W["conv_b"])
            W = {**W, **layer_weights(l, "mix2", rcb)}
            gip, grp = _gates_fwd(f"gates_{l}", rcb, W["w_gates"], W["b_gates"])
            hrec, m = _lru_fwd(f"lru_{l}", proj, rc, gip, grp, W["lru_param"])
            layer_prefetch(l, "ffn", m)
            h_mid, hn = _mm_nn(f"rec_out_{l}", m, W["w_rec_out"], out_dtype=F32, res=h, tn=D, norm_gain=gains["ffn"][l])
            mix_saved = (xn, proj, rc, rcb, gip, grp, hrec, m)
        else:
            if "w_kv" in W:
                h_kv = h
                hn_kv = _rmsnorm_fwd("kv_norm", h, W["norm_kv"])
                kv = _mm_nn("kv_proj", hn_kv, W["w_kv"], out_dtype=BF16, tm=1024, tn=1024)
                f_pre = _mm_nn("f_proj", hn_kv, W["w_f"], out_dtype=F32, bias=W["b_f"])
                c = _cumsum_rows("c_cumsum", _logsig_fwd("logsig", f_pre), False)
                c3 = (-c[:, :2 * HP]).reshape(S, HP, 2).transpose(1, 0, 2)
            q = _mm_nn(f"q_proj_{l}", xn, W["w_q"], out_dtype=BF16, scale=scale, tm=1024, tn=1024)
            layer_prefetch(l, "mix2", q)
            o, of, lse = _attn_fwd(f"attn_fwd_{l}", q, kv, c3)
            W = {**W, **layer_weights(l, "mix2", o)}
            layer_prefetch(l, "ffn", o)
            h_mid, hn = _mm_nn(f"o_proj_{l}", o, W["w_o"], out_dtype=F32, res=h, tn=D, norm_gain=gains["ffn"][l])
            mix_saved = (xn, q, o, of, lse)
        W = {**W, **layer_weights(l, "ffn", h_mid)}
        z3, act = _swiglu_fwd(f"ffn_in_{l}", hn, W["w_ffn_in"])
        layer_prefetch(l + 1, "mix", act)
        saved.append((W, h, h_mid, mix_saved, (hn, z3, act)))
        l += 1
        if l < len(gains["mix"]):
            h, xn = _mm_nn(f"ffn_out_{l - 1}", act, W["w_ffn_out"], out_dtype=F32, res=h_mid, tn=D,
                           norm_gain=gains["mix"][l])
        else:
            h = _mm_nn(f"ffn_out_{l - 1}", act, W["w_ffn_out"], out_dtype=F32, res=h_mid, tn=D)

    dh, dhb, dg_final, loss_row = _loss_head("loss_head", h, target, gains["final"])

    dk_parts, dv_parts, dc_parts = [], [], []
    token = None
    for l in reversed(range(len(saved))):
        W, h_in, h_mid, mix_saved, (hn, z3, act) = saved[l]
        recurrent = "w_rec_in" in W
        FH = W["w_ffn_in"].shape[-1]
        G = {}
        norm_ffn = gains["ffn"][l]
        if token is not None:
            norm_ffn = norm_ffn + jnp.minimum(token[:1, :1], 0.0)
        G["w_ffn_out"] = _mm_tn(f"d_ffn_out_{l}", act, dhb, out_dtype=BF16, tn=D)
        dz3 = _swiglu_bwd(f"d_act_{l}", dhb, W["w_ffn_out"], z3)
        G["w_ffn_in"] = _mm(
            f"d_ffn_in_{l}", "tn", hn, dz3, grid=(1, N_CHIPS),
            a_spec=pl.BlockSpec((S, D), lambda i, j: (0, 0)),
            b_spec=pl.BlockSpec((None, S, FH), lambda i, j: (j // 2, 0, j % 2)),
            out_shape=(N_CHIPS, D, FH), out_dtype=BF16,
            out_spec=pl.BlockSpec((None, D, FH), lambda i, j: (j, 0, 0)))
        ffn_token = layer_grads(l, "ffn", G)
        G = {}
        if ffn_token is not None:
            norm_ffn = norm_ffn + jnp.minimum(ffn_token[:1, :1], 0.0)
        dh, dhb, dgp = _mm(f"d_ffn_hn_{l}", "nt", dz3, W["w_ffn_in"], grid=(S // tx, 1),
                           a_spec=[pl.BlockSpec((None, tx, FH), functools.partial(lambda i, j, k: (k // 2, i, k % 2), k=k))
                                   for k in range(N_CHIPS)],
                           b_spec=[pl.BlockSpec((None, D, FH), functools.partial(lambda i, j, k: (k, 0, 0), k=k))
                                   for k in range(N_CHIPS)],
                           out_shape=(S, D), out_dtype=F32, out_spec=pl.BlockSpec((tx, D), lambda i, j: (i, 0)),
                           norm_bwd=(h_mid, norm_ffn, dh))
        G["norm_ffn"] = jnp.sum(dgp, axis=0)
        if recurrent:
            CH = W["w_rec_in"].shape[-1]
            C = 2 * CH
            xn, proj, rc, rcb, gip, grp, hrec, m = mix_saved
            G["w_rec_out"] = _mm_tn(f"d_rec_out_{l}", m, dhb, out_dtype=BF16, tn=D)
            dm = _mm_nt(f"d_m_{l}", dhb, W["w_rec_out"], out_dtype=F32, tn=C)
            dgb, dgi, dgr, drc1, G["b_gi"], G["b_gr"], G["lru_param"] = _lru_bwd(
                f"d_lru_{l}", dm, proj, hrec, rc, gip, grp, W["lru_param"])
            drc, G["w_gates"] = _gates_bwd(f"d_gates_{l}", dgi, dgr, rcb, W["w_gates"], drc1)
            mix_token = layer_grads(l, "mix2", {n: G[n] for n in ("w_rec_out", "w_gates")})
            drec, G["conv_w"], G["conv_b"] = _conv_bwd(f"d_conv_{l}", drc, proj, W["conv_w"])
            dproj = jnp.concatenate([dgb, drec], axis=1)
            norm_mix = gains["mix"][l] if mix_token is None else gains["mix"][l] + jnp.minimum(mix_token[:1, :1], 0.0)
            G["w_rec_in"] = _mm(
                f"d_rec_in_{l}", "tn", xn, dproj, grid=(1, N_CHIPS),
                a_spec=pl.BlockSpec((S, D), lambda i, j: (0, 0)),
                b_spec=pl.BlockSpec((S, CH), lambda i, j: (0, j)),
                out_shape=(N_CHIPS, D, CH), out_dtype=BF16,
                out_spec=pl.BlockSpec((None, D, CH), lambda i, j: (j, 0, 0)))
            dh, dhb, dgp = _mm(f"d_rec_xn_{l}", "nt", dproj, W["w_rec_in"], grid=(S // tx, 1),
                               a_spec=[pl.BlockSpec((tx, CH), functools.partial(lambda i, j, k: (i, k), k=k))
                                       for k in range(N_CHIPS)],
                               b_spec=[pl.BlockSpec((None, D, CH), functools.partial(lambda i, j, k: (k, 0, 0), k=k))
                                       for k in range(N_CHIPS)],
                               out_shape=(S, D), out_dtype=F32, out_spec=pl.BlockSpec((tx, D), lambda i, j: (i, 0)),
                               norm_bwd=(h_in, norm_mix, dh))
        else:
            xn, q, o, of, lse = mix_saved
            G["w_o"] = _mm_tn(f"d_o_proj_{l}", o, dhb, out_dtype=BF16, tn=D)
            do = _mm_nt(f"d_o_{l}", dhb, W["w_o"], out_dtype=BF16, tm=1024, tn=D)
            mix_token = layer_grads(l, "mix2", {"w_o": G["w_o"]})
            dq, dk, dv, dck, drq = _attn_bwd(f"attn_bwd_{l}", q, kv, c3, of, do, lse)
            dk_parts.append(dk)
            dv_parts.append(dv)
            dc_parts.append((dck + drq).reshape(2 * HP, S).T)
            G["w_q"] = _mm_tn(f"d_q_proj_{l}", xn, dq, out_dtype=BF16, tn=D)
            norm_mix = gains["mix"][l] if mix_token is None else gains["mix"][l] + jnp.minimum(mix_token[:1, :1], 0.0)
            dh, dhb, dgp = _mm_nt(f"d_q_xn_{l}", dq, W["w_q"], out_dtype=F32, tn=D, norm_bwd=(h_in, norm_mix, dh))
        G["norm_mix"] = jnp.sum(dgp, axis=0)
        if "w_kv" in W:
            dkb = _add_cast("dk_sum", dk_parts, BF16)
            dvb = _add_cast("dv_sum", dv_parts, BF16)
            dkv = jnp.concatenate([dkb, dvb], axis=1)
            dc = sum(dc_parts[1:], dc_parts[0])
            dc_pad = jnp.pad(dc, ((0, 0), (0, LANES - 2 * HP)))
            dls = _cumsum_rows("dc_cumsum", dc_pad, True)
            dfb, G["b_f"] = _logsig_bwd("d_logsig", dls, f_pre)
            G["w_kv"] = _mm_tn("d_kv_proj", hn_kv, dkv, out_dtype=BF16, tn=1024)
            G["w_f"] = _mm_tn("d_f_proj", hn_kv, dfb, out_dtype=F32)
            dhn_f = _mm_nt("d_f_hn", dfb, W["w_f"], out_dtype=F32, tn=D)
            dh, dhb, dgp = _mm_nt("d_kv_hn", dkv, W["w_kv"], out_dtype=F32, tn=D, res=dhn_f,
                                  norm_bwd=(h_kv, W["norm_kv"], dh))
            G["norm_kv"] = jnp.sum(dgp, axis=0)
        token = layer_grads(l, "mix", G)
    return loss_row, dh, dg_final


_ANY = pl.BlockSpec(memory_space=pl.ANY)


def _position():
    return lax.axis_index("x"), lax.axis_index("y"), lax.axis_index("c")


def _chip_peers(x, y):
    return [(1 - x, y), (x, 1 - y), (1 - x, 1 - y)]


def _half_rows(c, n):
    h = n // 2
    assert h % 16 == 0
    return pl.ds(pl.multiple_of(c * h, 16), h)


def _place_own(name, shard, layer, me):
    _, R, C = shard.shape
    tr = _row_tile(R, C, 2 * shard.dtype.itemsize, target=8 << 20)

    def body(me_ref, x_ref, o_ref):
        o_ref[...] = x_ref[...]

    return pl.pallas_call(
        body, name=name,
        grid_spec=pltpu.PrefetchScalarGridSpec(
            num_scalar_prefetch=1, grid=(R // tr,),
            in_specs=[pl.BlockSpec((None, tr, C), lambda i, me_ref: (layer, i, 0))],
            out_specs=pl.BlockSpec((None, tr, C), lambda i, me_ref: (me_ref[0], i, 0))),
        out_shape=_hbm_out((N_CHIPS, R, C), shard.dtype),
        compiler_params=_params(("parallel",)),
    )(me, shard)


def _gather_smalls(name, smalls):
    ns = len(smalls)

    def body(*refs):
        ins, outs = refs[:ns], refs[ns:2 * ns]
        send_sems, recv_sems, local_sems = refs[2 * ns:]
        x, y, c = _position()
        me = 2 * x + y
        peers = _chip_peers(x, y)

        def remote(t, k, chip):
            px, py = peers[k]
            return pltpu.make_async_remote_copy(
                src_ref=ins[t], dst_ref=outs[t].at[chip], send_sem=send_sems.at[3 * t + k],
                recv_sem=recv_sems.at[3 * t + k], device_id=(px, py, c), device_id_type=MESH)

        local = [pltpu.make_async_copy(ins[t], outs[t].at[me], local_sems.at[t]) for t in range(ns)]
        for t in range(ns):
            local[t].start()
            for k in range(3):
                remote(t, k, me).start()
        for t in range(ns):
            for k in range(3):
                px, py = peers[k]
                remote(t, k, 2 * px + py).wait_recv()
        for t in range(ns):
            for k in range(3):
                remote(t, k, me).wait_send()
            local[t].wait()

    return pl.pallas_call(
        body, name=name, in_specs=[_ANY] * ns, out_specs=[_ANY] * ns,
        out_shape=[_hbm_out((N_CHIPS,) + s.shape, s.dtype) for s in smalls],
        scratch_shapes=[pltpu.SemaphoreType.DMA((3 * ns,)), pltpu.SemaphoreType.DMA((3 * ns,)),
                        pltpu.SemaphoreType.DMA((ns,))],
    )(*smalls)


_SEM = pl.BlockSpec(memory_space=pltpu.SEMAPHORE)
_SPLIT = pltpu.CompilerParams(has_side_effects=pltpu.SideEffectType.DATAFLOW_SIDE_EFFECTING)


def _weight_copy(shards, buf, items, sems, i, k, chip_of_dst, peers, c):
    w, l = items[i]
    px, py = peers[k]
    half = _half_rows(c, shards[w].shape[1])
    return pltpu.make_async_remote_copy(
        src_ref=shards[w].at[l, half], dst_ref=buf.at[chip_of_dst, half],
        send_sem=sems[0].at[3 * i + k], recv_sem=sems[1].at[3 * i + k],
        device_id=(px, py, c), device_id_type=MESH)


def _gather_start(name, shards, bufs, items, after):
    nw, n = len(shards), len(bufs)

    def body(*refs):
        ins, outs, sems = refs[:nw], refs[nw + n + 1:nw + 2 * n + 1], refs[nw + 2 * n + 1:]
        x, y, c = _position()
        peers = _chip_peers(x, y)
        for i in range(n):
            for k in range(3):
                _weight_copy(ins, outs[i], items, sems, i, k, 2 * x + y, peers, c).start()

    res = pl.pallas_call(
        body, name=name, in_specs=[_ANY] * (nw + n + 1), out_specs=[_ANY] * n + [_SEM, _SEM],
        out_shape=[_hbm_out(b.shape, b.dtype) for b in bufs]
        + [pltpu.SemaphoreType.DMA((3 * n,)), pltpu.SemaphoreType.DMA((3 * n,))],
        input_output_aliases={nw + i: i for i in range(n)}, compiler_params=_SPLIT,
    )(*shards, *bufs, after)
    return res[:n], res[n:]


def _gather_wait(name, shards, bufs, items, ids, sems, after):
    nw, m = len(shards), len(ids)

    def body(*refs):
        ins, bs = refs[:nw], refs[nw:nw + m]
        sem_refs = refs[nw + m:nw + m + 2]
        x, y, c = _position()
        peers = _chip_peers(x, y)
        for j, i in enumerate(ids):
            for k in range(3):
                px, py = peers[k]
                _weight_copy(ins, bs[j], items, sem_refs, i, k, 2 * px + py, peers, c).wait_recv()
        for j, i in enumerate(ids):
            for k in range(3):
                _weight_copy(ins, bs[j], items, sem_refs, i, k, 2 * x + y, peers, c).wait_send()

    res = pl.pallas_call(
        body, name=name, in_specs=[_ANY] * (nw + m) + [_SEM, _SEM, _ANY], out_specs=[_ANY] * m,
        out_shape=[_hbm_out(bufs[i].shape, bufs[i].dtype) for i in ids],
        input_output_aliases={nw + j: j for j in range(m)}, compiler_params=_SPLIT,
    )(*shards, *[bufs[i] for i in ids], *sems, after)
    return list(res)


def _forward_copy(src, dst, sems, i, k, core):
    x, y, c = _position()
    px, py = _chip_peers(x, y)[k]
    half = _half_rows(core, src.shape[1])
    return pltpu.make_async_remote_copy(
        src_ref=src.at[2 * px + py, half], dst_ref=dst.at[2 * px + py, half],
        send_sem=sems[0].at[3 * i + k], recv_sem=sems[1].at[3 * i + k],
        device_id=(x, y, 1 - c), device_id_type=MESH)


def _forward_start(name, bufs):
    n = len(bufs)

    def body(*refs):
        ins, outs, sems = refs[:n], refs[n:2 * n], refs[2 * n:]
        c = lax.axis_index("c")
        for i in range(n):
            for k in range(3):
                _forward_copy(ins[i], outs[i], sems, i, k, c).start()

    res = pl.pallas_call(
        body, name=name, in_specs=[_ANY] * n, out_specs=[_ANY] * n + [_SEM, _SEM],
        out_shape=[_hbm_out(g.shape, g.dtype) for g in bufs]
        + [pltpu.SemaphoreType.DMA((3 * n,)), pltpu.SemaphoreType.DMA((3 * n,))],
        input_output_aliases={i: i for i in range(n)}, compiler_params=_SPLIT,
    )(*bufs)
    return list(res[:n]), res[n:]


def _forward_wait(name, bufs, sems, after):
    n = len(bufs)

    def body(*refs):
        bs, sem_refs = refs[:n], refs[n:n + 2]
        c = lax.axis_index("c")
        for i in range(n):
            for k in range(3):
                _forward_copy(bs[i], bs[i], sem_refs, i, k, 1 - c).wait_recv()
        for i in range(n):
            for k in range(3):
                _forward_copy(bs[i], bs[i], sem_refs, i, k, c).wait_send()

    return list(pl.pallas_call(
        body, name=name, in_specs=[_ANY] * n + [_SEM, _SEM, _ANY], out_specs=[_ANY] * n,
        out_shape=[_hbm_out(g.shape, g.dtype) for g in bufs],
        input_output_aliases={i: i for i in range(n)}, compiler_params=_SPLIT,
    )(*bufs, *sems, after))


def _reduce_copy(grads, others, sems, i):
    x, y, c = _position()
    return pltpu.make_async_remote_copy(
        src_ref=grads[i].at[:, _half_rows(1 - c, grads[i].shape[1])], dst_ref=others[i],
        send_sem=sems[0].at[i], recv_sem=sems[1].at[i], device_id=(x, y, 1 - c), device_id_type=MESH)


def _reduce_start(name, grads, after):
    n = len(grads)

    def body(*refs):
        ins, outs, sems, token = refs[:n], refs[n + 1:2 * n + 1], refs[2 * n + 1:2 * n + 3], refs[2 * n + 3]
        for i in range(n):
            _reduce_copy(ins, outs, sems, i).start()
        token[...] = jnp.zeros_like(token)

    res = pl.pallas_call(
        body, name=name, in_specs=[_ANY] * (n + 1),
        out_specs=[_ANY] * n + [_SEM, _SEM, pl.BlockSpec(memory_space=pltpu.VMEM)],
        out_shape=[_hbm_out((N_CHIPS, g.shape[1] // 2, g.shape[2]), g.dtype) for g in grads]
        + [pltpu.SemaphoreType.DMA((n,)), pltpu.SemaphoreType.DMA((n,)), jax.ShapeDtypeStruct((SUBLANES, LANES), F32)],
        compiler_params=_SPLIT,
    )(*grads, after)
    return list(res[:n]), res[n:n + 2], res[n + 2]


def _reduce_wait(name, grads, others, sems, after):
    n = len(grads)

    def body(*refs):
        ins, os_, sem_refs = refs[:n], refs[n:2 * n], refs[2 * n:2 * n + 2]
        for i in range(n):
            _reduce_copy(ins, os_, sem_refs, i).wait_recv()
        for i in range(n):
            _reduce_copy(ins, os_, sem_refs, i).wait_send()

    return list(pl.pallas_call(
        body, name=name, in_specs=[_ANY] * (2 * n) + [_SEM, _SEM, _ANY], out_specs=[_ANY] * n,
        out_shape=[_hbm_out(o.shape, o.dtype) for o in others],
        input_output_aliases={n + i: i for i in range(n)}, compiler_params=_SPLIT,
    )(*grads, *others, *sems, after))


def _sum_cores(name, g, other, core):
    _, R, C = g.shape
    H = R // 2
    tr = _row_tile(H, C, 3 * 2, target=12 << 20)
    nb = H // tr

    def body(c_ref, g_ref, o_ref, out_ref):
        out_ref[...] = (g_ref[...].astype(F32) + o_ref[...].astype(F32)).astype(out_ref.dtype)

    return pl.pallas_call(
        body, name=name,
        grid_spec=pltpu.PrefetchScalarGridSpec(
            num_scalar_prefetch=1, grid=(N_CHIPS, nb),
            in_specs=[pl.BlockSpec((None, tr, C), lambda j, i, c_ref: (j, c_ref[0] * nb + i, 0)),
                      pl.BlockSpec((None, tr, C), lambda j, i, c_ref: (j, i, 0))],
            out_specs=pl.BlockSpec((None, tr, C), lambda j, i, c_ref: (j, i, 0))),
        out_shape=_hbm_out((N_CHIPS, H, C), BF16),
        compiler_params=_params(("parallel", "parallel")),
    )(core, g, other)


def _sum_chips(name, received, own, full, layer, me_core):
    _, H, C = received.shape
    tr = _row_tile(H, C, 3 * 2 + 2 + 4, target=12 << 20)
    nb = H // tr

    def body(s_ref, r_ref, own_ref, full_ref, out_ref):
        acc = r_ref[0].astype(F32)
        for k in (1, 2):
            acc = acc + r_ref[k].astype(F32)
        out_ref[...] = acc + own_ref[...].astype(F32)

    return pl.pallas_call(
        body, name=name,
        grid_spec=pltpu.PrefetchScalarGridSpec(
            num_scalar_prefetch=1, grid=(nb,),
            in_specs=[pl.BlockSpec((3, tr, C), lambda i, s_ref: (0, i, 0)),
                      pl.BlockSpec((None, tr, C), lambda i, s_ref: (s_ref[0], i, 0)),
                      _ANY],
            out_specs=pl.BlockSpec((None, tr, C), lambda i, s_ref: (layer, s_ref[1] * nb + i, 0))),
        out_shape=_hbm_out(full.shape, full.dtype),
        input_output_aliases={3: 0},
        compiler_params=_params(("parallel",)),
    )(me_core, received, own, full)


def _part_copy(parts, recv, sems, i, k, peers, c):
    px, py = peers[k]
    return pltpu.make_async_remote_copy(
        src_ref=parts[i].at[2 * px + py], dst_ref=recv[i].at[k],
        send_sem=sems[0].at[3 * i + k], recv_sem=sems[1].at[3 * i + k],
        device_id=(px, py, c), device_id_type=MESH)


def _scatter_start(name, parts):
    n = len(parts)

    def body(*refs):
        ins, outs, sems, token = refs[:n], refs[n:2 * n], refs[2 * n:2 * n + 2], refs[2 * n + 2]
        x, y, c = _position()
        peers = _chip_peers(x, y)
        for i in range(n):
            for k in range(3):
                _part_copy(ins, outs, sems, i, k, peers, c).start()
        token[...] = jnp.zeros_like(token)

    res = pl.pallas_call(
        body, name=name, in_specs=[_ANY] * n,
        out_specs=[_ANY] * n + [_SEM, _SEM, pl.BlockSpec(memory_space=pltpu.VMEM)],
        out_shape=[_hbm_out((3,) + p.shape[1:], p.dtype) for p in parts]
        + [pltpu.SemaphoreType.DMA((3 * n,)), pltpu.SemaphoreType.DMA((3 * n,)),
           jax.ShapeDtypeStruct((SUBLANES, LANES), F32)],
        compiler_params=_SPLIT,
    )(*parts)
    return list(res[:n]), res[n:n + 2], res[n + 2]


def _scatter_wait(name, parts, recv, sems):
    n = len(parts)

    def body(*refs):
        ins, rs, sem_refs = refs[:n], refs[n:2 * n], refs[2 * n:2 * n + 2]
        x, y, c = _position()
        peers = _chip_peers(x, y)
        for i in range(n):
            for k in range(3):
                _part_copy(ins, rs, sem_refs, i, k, peers, c).wait_recv()
        for i in range(n):
            for k in range(3):
                _part_copy(ins, rs, sem_refs, i, k, peers, c).wait_send()

    return list(pl.pallas_call(
        body, name=name, in_specs=[_ANY] * (2 * n) + [_SEM, _SEM], out_specs=[_ANY] * n,
        out_shape=[_hbm_out(r.shape, r.dtype) for r in recv],
        input_output_aliases={n + i: i for i in range(n)}, compiler_params=_SPLIT,
    )(*parts, *recv, *sems))


def _share_d2d(name, full):
    n = len(full)

    def body(*refs):
        ins, outs = refs[:n], refs[n:2 * n]
        send_sems, recv_sems = refs[2 * n:]
        x, y, c = _position()

        def remote(w, core):
            half = _half_rows(core, ins[w].shape[1])
            return pltpu.make_async_remote_copy(
                src_ref=ins[w].at[:, half], dst_ref=outs[w].at[:, half],
                send_sem=send_sems.at[w], recv_sem=recv_sems.at[w],
                device_id=(x, y, 1 - c), device_id_type=MESH)

        for w in range(n):
            remote(w, c).start()
        for w in range(n):
            remote(w, 1 - c).wait_recv()
        for w in range(n):
            remote(w, c).wait_send()

    return pl.pallas_call(
        body, name=name, in_specs=[_ANY] * n, out_specs=[_ANY] * n,
        out_shape=[_hbm_out(f.shape, f.dtype) for f in full],
        input_output_aliases={w: w for w in range(n)},
        scratch_shapes=[pltpu.SemaphoreType.DMA((n,)), pltpu.SemaphoreType.DMA((n,))],
    )(*full)


def _gather_all(name, a):
    def body(a_ref, o_ref, send_sems, recv_sems, local_sem):
        x, y, c = _position()
        me = 4 * x + 2 * y + c

        def peer(k):
            return (x ^ ((k >> 2) & 1), y ^ ((k >> 1) & 1), c ^ (k & 1))

        def remote(k, slot):
            return pltpu.make_async_remote_copy(
                src_ref=a_ref, dst_ref=o_ref.at[slot], send_sem=send_sems.at[k - 1], recv_sem=recv_sems.at[k - 1],
                device_id=peer(k), device_id_type=MESH)

        local = pltpu.make_async_copy(a_ref, o_ref.at[me], local_sem)
        local.start()
        for k in range(1, N_DEV):
            remote(k, me).start()
        for k in range(1, N_DEV):
            px, py, pc = peer(k)
            remote(k, 4 * px + 2 * py + pc).wait_recv()
        for k in range(1, N_DEV):
            remote(k, me).wait_send()
        local.wait()

    return pl.pallas_call(
        body, name=name, in_specs=[_ANY], out_specs=_ANY,
        out_shape=_hbm_out((N_DEV,) + a.shape, a.dtype),
        scratch_shapes=[pltpu.SemaphoreType.DMA((N_DEV - 1,)), pltpu.SemaphoreType.DMA((N_DEV - 1,)),
                        pltpu.SemaphoreType.DMA],
    )(a)


def _rows2d(a, lead=0):
    return a.reshape(a.shape[:lead] + (-1, a.shape[-1]))


def _row_tile(rows, cols, itemsize=4, target=1 << 20):
    want = max(SUBLANES, target // (cols * itemsize))
    t = min(rows, (want // 16) * 16)
    while t > 16 and rows % t:
        t -= 16
    return t if rows % t == 0 else rows


def _sum_slots(name, r, out_dtype=F32):
    ns = r.shape[0]
    r2 = _rows2d(r, 1)
    _, rows, cols = r2.shape
    tr = _row_tile(rows, cols)

    def body(r_ref, o_ref):
        acc = r_ref[0].astype(F32)
        for s in range(1, ns):
            acc = acc + r_ref[s].astype(F32)
        o_ref[...] = acc.astype(o_ref.dtype)

    out = pl.pallas_call(
        body, name=name, grid=(rows // tr,),
        in_specs=[pl.BlockSpec((ns, tr, cols), lambda i: (0, i, 0))],
        out_specs=pl.BlockSpec((tr, cols), lambda i: (i, 0)),
        out_shape=_hbm_out((rows, cols), out_dtype),
        compiler_params=_params(("parallel",)),
    )(r2)
    return out.reshape(r.shape[1:])


def _adamw(name, g_parts, w, m, v):
    shape = w.shape
    ng = len(g_parts)
    args = [_rows2d(a) for a in (*g_parts, w, m, v)]
    rows, cols = args[0].shape
    tr = _row_tile(rows, cols, (ng + 7) * 4, target=16 << 20)
    c1 = 1.0 - ADAM_B1 ** ADAM_STEP
    c2 = 1.0 - ADAM_B2 ** ADAM_STEP

    def body(*refs):
        g = refs[0][...]
        for r in refs[1:ng]:
            g = g + r[...]
        w_ref, m_ref, v_ref = refs[ng:ng + 3]
        g_out, d_out, m_out, v_out = refs[ng + 3:]
        mn = ADAM_B1 * m_ref[...] + (1.0 - ADAM_B1) * g
        vn = ADAM_B2 * v_ref[...] + (1.0 - ADAM_B2) * (g * g)
        m_hat = mn / c1
        v_hat = vn / c2
        g_out[...] = g
        d_out[...] = -ADAM_LR * (m_hat / (jnp.sqrt(v_hat) + ADAM_EPS) + ADAM_WD * w_ref[...])
        m_out[...] = mn
        v_out[...] = vn

    spec = pl.BlockSpec((tr, cols), lambda i: (i, 0))
    outs = pl.pallas_call(
        body, name=name, grid=(rows // tr,), in_specs=[spec] * (ng + 3), out_specs=[spec] * 4,
        out_shape=[_hbm_out((rows, cols), F32)] * 4,
        compiler_params=_params(("parallel",)),
    )(*args)
    return tuple(o.reshape(shape) for o in outs)


_WEIGHTS = ["norm_mix", "norm_ffn", "w_ffn_in", "w_ffn_out", "w_rec_in", "conv_w", "conv_b", "w_lru_gates",
            "b_lru_gates", "lru_param", "w_rec_out", "norm_kv", "w_kvf", "b_forget", "w_q", "w_o", "norm_final"]
_BIG = ["w_ffn_in", "w_ffn_out", "w_rec_in", "w_lru_gates", "w_rec_out", "w_kvf", "w_q", "w_o"]


def _stack3(a):
    return a[None] if a.ndim == 2 else a.reshape(a.shape[0], -1, a.shape[-1])


def _pad_lanes(a, n):
    return jnp.pad(a, ((0, 0),) * (a.ndim - 1) + ((0, n - a.shape[-1]),))


def kernel(x, norm_mix, norm_ffn, w_ffn_in, w_ffn_out, w_rec_in, conv_w, conv_b, w_lru_gates, b_lru_gates, lru_param, w_rec_out, norm_kv, w_kvf, b_forget, w_q, w_o, norm_final, loss_target, m_norm_mix, m_norm_ffn, m_w_ffn_in, m_w_ffn_out, m_w_rec_in, m_conv_w, m_conv_b, m_w_lru_gates, m_b_lru_gates, m_lru_param, m_w_rec_out, m_norm_kv, m_w_kvf, m_b_forget, m_w_q, m_w_o, m_norm_final, v_norm_mix, v_norm_ffn, v_w_ffn_in, v_w_ffn_out, v_w_rec_in, v_conv_w, v_conv_b, v_w_lru_gates, v_b_lru_gates, v_lru_param, v_w_rec_out, v_norm_kv, v_w_kvf, v_b_forget, v_w_q, v_w_o, v_norm_final):
    P = dict(norm_mix=norm_mix, norm_ffn=norm_ffn, w_ffn_in=w_ffn_in, w_ffn_out=w_ffn_out, w_rec_in=w_rec_in,
             conv_w=conv_w, conv_b=conv_b, w_lru_gates=w_lru_gates, b_lru_gates=b_lru_gates, lru_param=lru_param,
             w_rec_out=w_rec_out, norm_kv=norm_kv, w_kvf=w_kvf, b_forget=b_forget, w_q=w_q, w_o=w_o,
             norm_final=norm_final)
    M1 = dict(norm_mix=m_norm_mix, norm_ffn=m_norm_ffn, w_ffn_in=m_w_ffn_in, w_ffn_out=m_w_ffn_out,
              w_rec_in=m_w_rec_in, conv_w=m_conv_w, conv_b=m_conv_b, w_lru_gates=m_w_lru_gates,
              b_lru_gates=m_b_lru_gates, lru_param=m_lru_param, w_rec_out=m_w_rec_out, norm_kv=m_norm_kv,
              w_kvf=m_w_kvf, b_forget=m_b_forget, w_q=m_w_q, w_o=m_w_o, norm_final=m_norm_final)
    M2 = dict(norm_mix=v_norm_mix, norm_ffn=v_norm_ffn, w_ffn_in=v_w_ffn_in, w_ffn_out=v_w_ffn_out,
              w_rec_in=v_w_rec_in, conv_w=v_conv_w, conv_b=v_conv_b, w_lru_gates=v_w_lru_gates,
              b_lru_gates=v_b_lru_gates, lru_param=v_lru_param, w_rec_out=v_w_rec_out, norm_kv=v_norm_kv,
              w_kvf=v_w_kvf, b_forget=v_b_forget, w_q=v_w_q, w_o=v_w_o, norm_final=v_norm_final)

    _, S, D = x.shape
    L = norm_mix.shape[0]
    NA, NBLK, BW, GS = w_lru_gates.shape
    C = NBLK * BW
    CS = C // N_CHIPS
    H = b_forget.shape[0]
    assert C == D and H * HEAD_DIM == D and H <= LANES
    chip = 2 * lax.axis_index("x") + lax.axis_index("y")

    small_a = jnp.concatenate([conv_w, conv_b[:, None], lru_param[:, None]], axis=1)
    small_a, b_gates = _gather_smalls("gather_smalls", [small_a, b_lru_gates])
    small_a = small_a.transpose(1, 2, 0, 3).reshape(NA, 6, C)
    b_gates = b_gates.transpose(1, 2, 0, 3).reshape(NA, NBLK, 1, N_CHIPS * GS)
    shards = [_stack3(P[w]).astype(BF16) for w in _BIG]
    core = lax.axis_index("c")
    chip_id = jnp.reshape(chip, (1,)).astype(jnp.int32)
    core_id = jnp.reshape(core, (1,)).astype(jnp.int32)
    me_core = jnp.stack([chip, core]).astype(jnp.int32)

    parts_of_layer = ("mix", "mix2", "ffn")

    def part_items(l, part):
        if part == "ffn":
            names, at = ["w_ffn_in", "w_ffn_out"], l
        elif l < NA:
            names, at = (["w_rec_in"] if part == "mix" else ["w_lru_gates", "w_rec_out"]), l
        else:
            names, at = ((["w_kvf"] if l == NA else []) + ["w_q"] if part == "mix" else ["w_o"]), l - NA
        return [(_BIG.index(n), 0 if n == "w_kvf" else at) for n in names]

    def stage_of(l, part):
        return (l, part) if l == 0 or part == "ffn" else (l, "mixer")

    def stage_items(st):
        l, part = st
        return [it for p in (("mix", "mix2") if part == "mixer" else (part,)) for it in part_items(l, p)]

    stages = [(0, p) for p in parts_of_layer] + [(l, p) for l in range(1, L) for p in ("mixer", "ffn")]
    items = [it for st in stages for it in stage_items(st)]
    ids_of = {st: [items.index(it) for it in stage_items(st)] for st in stages}
    bufs = [_place_own(f"place_{_BIG[w]}_{li}", shards[w], li, chip_id) for w, li in items]
    bufs, gather_sems = _gather_start("gather_start", shards, bufs, items, small_a)

    forwarding, fetched = {}, {}

    def layer_prefetch(l, part, after):
        st = stage_of(l, part)
        if l < L and st not in forwarding:
            got = _gather_wait(f"gather_wait_{st[1]}_{l}", shards, bufs, items, ids_of[st], gather_sems, after)
            forwarding[st] = _forward_start(f"forward_start_{st[1]}_{l}", got)

    def layer_weights(l, part, after):
        if l >= L:
            return None
        st = stage_of(l, part)
        if st not in fetched:
            layer_prefetch(l, part, after)
            got, sems = forwarding[st]
            got = _forward_wait(f"forward_wait_{st[1]}_{l}", got, sems, after)
            fetched[st] = {_BIG[items[i][0]]: g for i, g in zip(ids_of[st], got)}
        B = fetched[st]
        if part == "ffn":
            return dict(w_ffn_in=B["w_ffn_in"], w_ffn_out=B["w_ffn_out"].reshape(-1, D))
        if l < NA and part == "mix":
            return dict(w_rec_in=B["w_rec_in"], conv_w=small_a[l, :4], conv_b=small_a[l, 4:5])
        if l < NA:
            return dict(w_gates=B["w_lru_gates"].reshape(N_CHIPS, NBLK, BW, GS).transpose(1, 2, 0, 3).reshape(
                NBLK, BW, N_CHIPS * GS), b_gates=b_gates[l], w_rec_out=B["w_rec_out"].reshape(C, D),
                lru_param=small_a[l, 5:6])
        if part == "mix2":
            return dict(w_o=B["w_o"].reshape(D, D))
        W = dict(w_q=B["w_q"].reshape(D, D))
        if l == NA:
            w_kvf_full = B["w_kvf"].transpose(1, 0, 2).reshape(D, -1)
            W.update(norm_kv=norm_kv[None], w_kv=w_kvf_full[:, :2 * D],
                     w_f=_pad_lanes(w_kvf_full[:, 2 * D:], LANES), b_f=_pad_lanes(b_forget[None], LANES))
        return W

    G_small = {l: {} for l in range(L)}
    stash = {st: {} for st in stages}
    pending = {}
    reducing = []

    def finish_reduce(after):
        st, its, grads, others, sems = reducing.pop()
        l, part = st
        others = _reduce_wait(f"reduce_wait_{part}_{l}", grads, others, sems, after)
        parts = [_sum_cores(f"sum_cores_{l}_{_BIG[w]}", g, o, core_id) for (w, _), g, o in zip(its, grads, others)]
        recv, sems, token = _scatter_start(f"scatter_start_{part}_{l}", parts)
        pending[st] = (parts, recv, sems)
        return token

    def layer_grads(l, part, G_part):
        G_small[l].update(G_part)
        st = stage_of(l, part)
        stash[st].update(G_part)
        if st[1] == "mixer" and part != "mix":
            return None
        G = stash[st]
        late = {"ffn": "w_ffn_in", "mix": "norm_mix"}.get(part) or ("w_gates" if l < NA else "w_o")
        after = finish_reduce(G_part[late]) if reducing else jnp.zeros((SUBLANES, LANES), F32)
        by_name = dict(
            w_ffn_in=lambda: G["w_ffn_in"], w_ffn_out=lambda: G["w_ffn_out"].reshape(N_CHIPS, -1, D),
            w_rec_in=lambda: G["w_rec_in"],
            w_lru_gates=lambda: G["w_gates"].reshape(NBLK, BW, N_CHIPS, GS).transpose(2, 0, 1, 3).reshape(
                N_CHIPS, NBLK * BW, GS),
            w_rec_out=lambda: G["w_rec_out"].reshape(N_CHIPS, -1, D),
            w_kvf=lambda: jnp.concatenate([G["w_kv"].astype(F32), G["w_f"][:, :H]], axis=1).reshape(
                D, N_CHIPS, -1).transpose(1, 0, 2).astype(BF16),
            w_q=lambda: G["w_q"].reshape(N_CHIPS, -1, D), w_o=lambda: G["w_o"].reshape(N_CHIPS, -1, D))
        its = stage_items(st)
        grads = [by_name[_BIG[w]]() for w, _ in its]
        others, sems, token = _reduce_start(f"reduce_start_{st[1]}_{l}", grads, after)
        reducing.append((st, its, grads, others, sems))
        return finish_reduce(token) if l == 0 else token

    gains = dict(mix=[norm_mix[l][None] for l in range(L)], ffn=[norm_ffn[l][None] for l in range(L)],
                 final=norm_final[None])
    loss_row, grad_x, dg_final = _local_step(x.reshape(S, D), loss_target.reshape(S, D), gains,
                                             layer_weights, layer_prefetch, layer_grads)

    rows = [*[G_small[l]["norm_mix"] for l in range(L)], *[G_small[l]["norm_ffn"] for l in range(L)],
            G_small[NA]["norm_kv"], dg_final, _pad_lanes(G_small[NA]["b_f"], D), _pad_lanes(loss_row, D)]
    for a in range(NA):
        rows += [G_small[a][n] for n in ("conv_w", "conv_b", "b_gi", "b_gr", "lru_param")]
    packed = jnp.concatenate(rows, axis=0)
    tot = _sum_slots("sum_small", _gather_all("gather_small", packed))
    loss = tot[2 * L + 3, 0]
    g_rep = jnp.concatenate([tot[:2 * L + 2], tot[2 * L + 2:2 * L + 3]], axis=0)
    base = 2 * L + 4
    g_sh = []
    for a in range(NA):
        blk = lax.dynamic_slice_in_dim(tot[base + 8 * a:base + 8 * a + 8], chip * CS, CS, axis=1)
        gi = tot[base + 8 * a + 5].reshape(NBLK, BW)
        gr = tot[base + 8 * a + 6].reshape(NBLK, BW)
        bl = lax.dynamic_slice_in_dim(jnp.concatenate([gi, gr], axis=1), chip * GS, GS, axis=1)
        g_sh += [blk[:5], bl.reshape(-1, CS), blk[7:8]]
    g_sh = jnp.concatenate(g_sh, axis=0)
    nrow = g_sh.shape[0] // NA

    def pack_rep(T):
        return jnp.concatenate([T["norm_mix"], T["norm_ffn"], T["norm_kv"][None], T["norm_final"][None],
                                _pad_lanes(T["b_forget"][None], D)], axis=0)

    def pack_sh(T):
        return jnp.concatenate([jnp.concatenate([T["conv_w"][a], T["conv_b"][a][None],
                                                 T["b_lru_gates"][a].reshape(-1, CS), T["lru_param"][a][None]], axis=0)
                                for a in range(NA)], axis=0)

    rep = _adamw("adamw_replicated", [g_rep], pack_rep(P), pack_rep(M1), pack_rep(M2))
    shd = _adamw("adamw_small_sharded", [g_sh], pack_sh(P), pack_sh(M1), pack_sh(M2))

    def unpack_rep(t):
        return dict(norm_mix=t[:L], norm_ffn=t[L:2 * L], norm_kv=t[2 * L], norm_final=t[2 * L + 1],
                    b_forget=t[2 * L + 2, :H])

    def unpack_sh(t):
        t = t.reshape(NA, nrow, CS)
        return dict(conv_w=t[:, :4], conv_b=t[:, 4], b_lru_gates=t[:, 5:nrow - 1].reshape(NA, NBLK, GS),
                    lru_param=t[:, nrow - 1])

    full = [lax.empty(sh.shape, F32) for sh in shards]
    for st in reversed(stages):
        l, part = st
        parts, recv, sems = pending[st]
        recv = _scatter_wait(f"scatter_wait_{part}_{l}", parts, recv, sems)
        for (w, li), own, r in zip(stage_items(st), parts, recv):
            full[w] = _sum_chips(f"sum_chips_{l}_{_BIG[w]}", r, own, full[w], li, me_core)
    full = _share_d2d("share_d2d", full)
    big = {w: _adamw(f"adamw_{w}", [g.reshape(P[w].shape)], P[w], M1[w], M2[w]) for w, g in zip(_BIG, full)}

    outs = []
    for i in range(4):
        small = {**unpack_rep(rep[i]), **unpack_sh(shd[i])}
        outs.append([big[w][i] if w in big else small[w] for w in _WEIGHTS])
    return (loss, grad_x.reshape(1, S, D), *outs[0], *outs[1], *outs[2], *outs[3])
```

```python
import functools
import math

import jax
import jax.numpy as jnp
from jax import lax
from jax.experimental import pallas as pl
from jax.experimental.pallas import tpu as pltpu

F32 = jnp.float32
BF16 = jnp.bfloat16

EPS = 1e-6
LRU_C = 8.0
HEAD_DIM = 64
LANES = 128
SUBLANES = 8
VMEM_LIMIT = 48 * 1024 * 1024
N_CHIPS = 4
N_DEV = 8

ADAM_LR = 0.001
ADAM_B1 = 0.9
ADAM_B2 = 0.999
ADAM_EPS = 1e-08
ADAM_WD = 0.01
ADAM_STEP = 10

_NN = (((1,), (0,)), ((), ()))
_NT = (((1,), (1,)), ((), ()))
_TN = (((0,), (0,)), ((), ()))
_DN = {"nn": _NN, "nt": _NT, "tn": _TN}
MESH = pl.DeviceIdType.MESH


def _hbm_out(shape, dtype):
    return pltpu.HBM(shape, dtype)


def _params(sem):
    return pltpu.CompilerParams(dimension_semantics=sem, vmem_limit_bytes=VMEM_LIMIT)


def _tile(n, want):
    if n <= want:
        return n
    t = (want // LANES) * LANES
    while t >= LANES:
        if n % t == 0:
            return t
        t -= LANES
    return n


def _sigmoid(x):
    return 1.0 / (1.0 + jnp.exp(-x))


def _sigmoid_t(x):
    return 0.5 * jnp.tanh(0.5 * x) + 0.5


def _softplus(x):
    return jnp.maximum(x, 0.0) + jnp.log(1.0 + jnp.exp(-jnp.abs(x)))


_GELU_C = math.sqrt(2.0 / math.pi)


def _gelu_and_grad(x):
    inner = _GELU_C * (x + 0.044715 * x * x * x)
    t = jnp.tanh(inner)
    g = 0.5 * x * (1.0 + t)
    dg = 0.5 * (1.0 + t) + 0.5 * x * (1.0 - t * t) * _GELU_C * (1.0 + 3.0 * 0.044715 * x * x)
    return g, dg


def _rms(x):
    return lax.rsqrt(jnp.mean(x * x, axis=-1, keepdims=True) + EPS)


def _rms_bwd(dy, x, g):
    r = _rms(x)
    xr = x * r
    dyg = dy * g
    return r * dyg - xr * (r * jnp.mean(dyg * xr, axis=-1, keepdims=True)), jnp.sum(dy * xr, axis=0, keepdims=True)


def _mm(name, mode, a, b, *, grid, a_spec, b_spec, out_shape, out_dtype, out_spec, nk=1,
        res=None, res_spec=None, bias=None, bias_spec=None, scale=None, norm_gain=None, norm_bwd=None):
    dn = _DN[mode]
    has_res, has_bias = res is not None, bias is not None
    blk = tuple(d for d in out_spec.block_shape if d is not None)
    vec = pl.BlockSpec((1, blk[-1]), lambda *g: (0, 0))
    a_specs = a_spec if isinstance(a_spec, list) else [a_spec]
    b_specs = b_spec if isinstance(b_spec, list) else [b_spec]
    npair = len(a_specs)
    n_in = 2 * npair + int(has_res) + int(has_bias) + (1 if norm_gain is not None else 0) + (3 if norm_bwd else 0)

    def body(*refs):
        p = 2 * npair
        r_ref = refs[p] if has_res else None
        p += int(has_res)
        bias_ref = refs[p] if has_bias else None
        p += int(has_bias)
        extra = refs[p:n_in]
        outs = refs[n_in:]
        o_ref = outs[0]
        part = lax.dot_general(refs[0][...], refs[npair][...], dn, preferred_element_type=F32)
        for t in range(1, npair):
            part = part + lax.dot_general(refs[t][...], refs[npair + t][...], dn, preferred_element_type=F32)

        def finish(acc):
            if scale is not None:
                acc = acc * scale
            if has_bias:
                acc = acc + bias_ref[...]
            if has_res:
                acc = r_ref[...] + acc
            if norm_bwd:
                h_ref, g_ref, dh_ref = extra
                dx, dg = _rms_bwd(acc, h_ref[...], g_ref[...])
                acc = dh_ref[...] + dx
                outs[1][...] = acc.astype(BF16)
                outs[2][...] = dg
            if norm_gain is not None:
                outs[1][...] = (acc * _rms(acc) * extra[0][...]).astype(BF16)
            o_ref[...] = acc.astype(o_ref.dtype)

        if nk == 1:
            finish(part)
        else:
            acc_ref = refs[-1]
            k = pl.program_id(2)

            @pl.when(k == 0)
            def _():
                acc_ref[...] = part

            @pl.when(k > 0)
            def _():
                acc_ref[...] += part

            @pl.when(k == nk - 1)
            def _():
                finish(acc_ref[...])

    ins, specs = [a] * npair + [b] * npair, a_specs + b_specs
    if has_res:
        ins.append(res)
        specs.append(res_spec)
    if has_bias:
        ins.append(bias)
        specs.append(bias_spec)
    out_specs, out_shapes = [out_spec], [_hbm_out(out_shape, out_dtype)]
    if norm_gain is not None:
        ins.append(norm_gain)
        specs.append(vec)
        out_specs.append(out_spec)
        out_shapes.append(_hbm_out(out_shape, BF16))
    if norm_bwd:
        h, g, dh = norm_bwd
        ins += [h, g, dh]
        specs += [out_spec, vec, out_spec]
        out_specs += [out_spec, pl.BlockSpec((None, 1, blk[-1]), lambda i, *rest: (i, 0, 0))]
        out_shapes += [_hbm_out(out_shape, BF16), _hbm_out((grid[0], 1, blk[-1]), F32)]
    sem = ("parallel", "parallel") + (("arbitrary",) if len(grid) == 3 else ())
    single = len(out_specs) == 1
    return pl.pallas_call(
        body, name=name, grid=grid, in_specs=specs, out_specs=out_specs[0] if single else out_specs,
        out_shape=out_shapes[0] if single else out_shapes,
        scratch_shapes=[pltpu.VMEM(blk, F32)] if nk > 1 else [],
        compiler_params=_params(sem),
    )(*ins)


def _mm_nn(name, a, b, *, b_lead=(), out_dtype, tm=512, tn=512, res=None, bias=None, scale=None, norm_gain=None):
    M, K = a.shape
    N = b.shape[-1]
    tm, tn = _tile(M, tm), _tile(N, tn)
    nl = len(b_lead)
    return _mm(
        name, "nn", a, b, grid=(M // tm, N // tn),
        a_spec=pl.BlockSpec((tm, K), lambda i, j: (i, 0)),
        b_spec=pl.BlockSpec((None,) * nl + (K, tn), lambda i, j: tuple(b_lead) + (0, j)),
        out_shape=(M, N), out_dtype=out_dtype, out_spec=pl.BlockSpec((tm, tn), lambda i, j: (i, j)),
        res=res, res_spec=pl.BlockSpec((tm, tn), lambda i, j: (i, j)),
        bias=bias, bias_spec=pl.BlockSpec((1, tn), lambda i, j: (0, j)), scale=scale, norm_gain=norm_gain)


def _mm_nt(name, a, b, *, b_lead=(), out_dtype, tm=512, tn=512, tk=2048, res=None, norm_bwd=None):
    M, K = a.shape
    N = b.shape[-2]
    tm, tn, tk = _tile(M, tm), _tile(N, tn), _tile(K, tk)
    nk = K // tk
    nl = len(b_lead)
    return _mm(
        name, "nt", a, b, grid=(M // tm, N // tn, nk), nk=nk,
        a_spec=pl.BlockSpec((tm, tk), lambda i, j, k: (i, k)),
        b_spec=pl.BlockSpec((None,) * nl + (tn, tk), lambda i, j, k: tuple(b_lead) + (j, k)),
        out_shape=(M, N), out_dtype=out_dtype, out_spec=pl.BlockSpec((tm, tn), lambda i, j, k: (i, j)),
        res=res, res_spec=pl.BlockSpec((tm, tn), lambda i, j, k: (i, j)), norm_bwd=norm_bwd)


def _mm_tn(name, a, b, *, out_dtype, tm=512, tn=512):
    S, M = a.shape
    N = b.shape[1]
    tm, tn = _tile(M, tm), _tile(N, tn)
    return _mm(
        name, "tn", a, b, grid=(M // tm, N // tn),
        a_spec=pl.BlockSpec((S, tm), lambda i, j: (0, i)),
        b_spec=pl.BlockSpec((S, tn), lambda i, j: (0, j)),
        out_shape=(M, N), out_dtype=out_dtype, out_spec=pl.BlockSpec((tm, tn), lambda i, j: (i, j)))


def _rmsnorm_fwd(name, h, g, tr=256):
    S, D = h.shape
    tr = _tile(S, tr)

    def body(h_ref, g_ref, o_ref):
        x = h_ref[...]
        r = lax.rsqrt(jnp.mean(x * x, axis=-1, keepdims=True) + EPS)
        o_ref[...] = (x * r * g_ref[...]).astype(o_ref.dtype)

    return pl.pallas_call(
        body, name=name, grid=(S // tr,),
        in_specs=[pl.BlockSpec((tr, D), lambda i: (i, 0)), pl.BlockSpec((1, D), lambda i: (0, 0))],
        out_specs=pl.BlockSpec((tr, D), lambda i: (i, 0)),
        out_shape=_hbm_out((S, D), BF16),
        compiler_params=_params(("parallel",)),
    )(h, g)


def _loss_head(name, h, target, g, tr=256):
    S, D = h.shape
    tr = _tile(S, tr)

    def body(h_ref, t_ref, g_ref, o_ref, ob_ref, dg_ref, loss_ref):
        i = pl.program_id(0)
        x = h_ref[...]
        gg = g_ref[...]
        r = lax.rsqrt(jnp.mean(x * x, axis=-1, keepdims=True) + EPS)
        xr = x * r
        err = xr * gg - t_ref[...]
        lpart = 0.5 * jnp.sum(jnp.mean(err * err, axis=-1, keepdims=True), axis=0, keepdims=True)
        dy = err * (1.0 / D)
        dyg = dy * gg
        dx = r * dyg - xr * (r * jnp.mean(dyg * xr, axis=-1, keepdims=True))
        o_ref[...] = dx
        ob_ref[...] = dx.astype(BF16)
        part = jnp.sum(dy * xr, axis=0, keepdims=True)
        lrow = jnp.broadcast_to(lpart, (1, LANES))

        @pl.when(i == 0)
        def _():
            dg_ref[...] = part
            loss_ref[...] = lrow

        @pl.when(i > 0)
        def _():
            dg_ref[...] += part
            loss_ref[...] += lrow

    row = pl.BlockSpec((tr, D), lambda i: (i, 0))
    vec = pl.BlockSpec((1, D), lambda i: (0, 0))
    return pl.pallas_call(
        body, name=name, grid=(S // tr,),
        in_specs=[row, row, vec], out_specs=[row, row, vec, pl.BlockSpec((1, LANES), lambda i: (0, 0))],
        out_shape=[_hbm_out((S, D), F32), _hbm_out((S, D), BF16),
                   _hbm_out((1, D), F32), _hbm_out((1, LANES), F32)],
        compiler_params=_params(("arbitrary",)),
    )(h, target, g)


def _swiglu_fwd(name, hn, w_in, tm=512):
    S, D = hn.shape
    FH = w_in.shape[-1]
    tm = _tile(S, tm)

    def body(x_ref, wg_ref, wu_ref, z_ref, a_ref):
        x = x_ref[...]
        zg = jnp.dot(x, wg_ref[...], preferred_element_type=F32)
        zu = jnp.dot(x, wu_ref[...], preferred_element_type=F32)
        sg = _sigmoid_t(zg)
        silu = zg * sg
        z_ref[0] = (zu * (sg * (1.0 + zg * (1.0 - sg)))).astype(z_ref.dtype)
        z_ref[1] = silu.astype(z_ref.dtype)
        a_ref[...] = (silu * zu).astype(a_ref.dtype)

    return pl.pallas_call(
        body, name=name, grid=(2, S // tm),
        in_specs=[pl.BlockSpec((tm, D), lambda j, i: (i, 0)),
                  pl.BlockSpec((None, D, FH), lambda j, i: (j, 0, 0)),
                  pl.BlockSpec((None, D, FH), lambda j, i: (j + 2, 0, 0))],
        out_specs=[pl.BlockSpec((2, tm, FH), lambda j, i: (0, i, j)), pl.BlockSpec((tm, FH), lambda j, i: (i, j))],
        out_shape=[_hbm_out((2, S, 2 * FH), BF16), _hbm_out((S, 2 * FH), BF16)],
        compiler_params=_params(("parallel", "parallel")),
    )(hn, w_in, w_in)


def _swiglu_bwd(name, dhb, w_out, z3, tm=512):
    S, D = dhb.shape
    F = w_out.shape[0]
    FH = F // 2
    tm = _tile(S, tm)

    def body(d_ref, w_ref, z_ref, dz_ref):
        d = lax.dot_general(d_ref[...], w_ref[...], _NT, preferred_element_type=F32)
        dz_ref[0] = (d * z_ref[0].astype(F32)).astype(dz_ref.dtype)
        dz_ref[1] = (d * z_ref[1].astype(F32)).astype(dz_ref.dtype)

    zspec = pl.BlockSpec((2, tm, FH), lambda j, i: (0, i, j))
    return pl.pallas_call(
        body, name=name, grid=(2, S // tm),
        in_specs=[pl.BlockSpec((tm, D), lambda j, i: (i, 0)), pl.BlockSpec((FH, D), lambda j, i: (j, 0)), zspec],
        out_specs=zspec, out_shape=_hbm_out((2, S, F), BF16),
        compiler_params=_params(("parallel", "parallel")),
    )(dhb, w_out, z3)


SCAN_ROWS = 64


def _group_scan(A, B, reverse):
    n = A.shape[0]
    sub = lax.broadcasted_iota(jnp.int32, A.shape, 0) % SUBLANES
    for d in (1, 2, 4):
        if reverse:
            A_sh, B_sh = pltpu.roll(A, n - d, 0), pltpu.roll(B, n - d, 0)
            keep = sub < SUBLANES - d
        else:
            A_sh, B_sh = pltpu.roll(A, d, 0), pltpu.roll(B, d, 0)
            keep = sub >= d
        B = jnp.where(keep, A * B_sh + B, B)
        A = jnp.where(keep, A * A_sh, A)
    return A, B


def _block_scan(a, u, carry, reverse):
    A, B = _group_scan(a, u, reverse)
    ng = a.shape[0] // SUBLANES
    out = [None] * ng
    order = range(ng - 1, -1, -1) if reverse else range(ng)
    for gi in order:
        sl = slice(gi * SUBLANES, (gi + 1) * SUBLANES)
        hg = A[sl] * carry + B[sl]
        out[gi] = hg
        carry = hg[0:1] if reverse else hg[SUBLANES - 1:SUBLANES]
    return jnp.concatenate(out, axis=0), carry


def _lru_gates(rc, gip, grp, sp):
    gi = _sigmoid_t(gip)
    gr = _sigmoid_t(grp)
    la = -LRU_C * gr * sp
    a = jnp.exp(la)
    om = -jnp.tanh(la) * (a * a + 1.0)
    mult = jnp.sqrt(om)
    return gi, gr, a, mult


def _lru_fwd(name, proj, rc, gip, grp, lru_p, tc=256):
    S, C = rc.shape
    tc = _tile(C, tc)
    nb = S // SCAN_ROWS

    def body(gb_ref, rc_ref, gi_ref, gr_ref, l_ref, h_ref, m_ref):
        sp = _softplus(-l_ref[...])

        def step(b, carry):
            rows = pl.ds(pl.multiple_of(b * SCAN_ROWS, SCAN_ROWS), SCAN_ROWS)
            rcb = rc_ref[rows, :]
            gi, _, a, mult = _lru_gates(rcb, gi_ref[rows, :], gr_ref[rows, :], sp)
            h, carry = _block_scan(a, rcb * gi * mult, carry, False)
            h_ref[rows, :] = h
            gel, _ = _gelu_and_grad(gb_ref[rows, :])
            m_ref[rows, :] = (gel * h).astype(m_ref.dtype)
            return carry

        lax.fori_loop(0, nb, step, jnp.zeros((1, tc), F32))

    col = pl.BlockSpec((S, tc), lambda j: (0, j))
    return pl.pallas_call(
        body, name=name, grid=(C // tc,),
        in_specs=[col, col, col, col, pl.BlockSpec((1, tc), lambda j: (0, j))],
        out_specs=[col, col],
        out_shape=[_hbm_out((S, C), F32), _hbm_out((S, C), BF16)],
        compiler_params=_params(("parallel",)),
    )(proj, rc, gip, grp, lru_p)


def _lru_bwd(name, dm, proj, hrec, rc, gip, grp, lru_p, tc=256):
    S, C = rc.shape
    tc = _tile(C, tc)
    nb = S // SCAN_ROWS
    R = SCAN_ROWS

    def body(dm_ref, gb_ref, h_ref, rc_ref, gi_ref, gr_ref, l_ref,
             dgb_ref, dgi_ref, dgr_ref, drc_ref, dbi_ref, dbr_ref, dl_ref):
        lp = l_ref[...]
        sp = _softplus(-lp)
        row = lax.broadcasted_iota(jnp.int32, (R, tc), 0)
        zero = jnp.zeros((1, tc), F32)

        def step(t, carry):
            mu_in, s_i, s_r, s_sp = carry
            b = nb - 1 - t
            r0 = pl.multiple_of(b * R, R)
            rows = pl.ds(r0, R)
            rcb = rc_ref[rows, :]
            gi, gr, a, mult = _lru_gates(rcb, gi_ref[rows, :], gr_ref[rows, :], sp)
            gel, dgel = _gelu_and_grad(gb_ref[rows, :])
            dmb = dm_ref[rows, :]
            h = h_ref[rows, :]
            dgb_ref[rows, :] = (dmb * h * dgel).astype(dgb_ref.dtype)
            dh = dmb * gel
            mu, mu_out = _block_scan(a, a * dh, mu_in, True)
            mu_next = jnp.where(row == R - 1, mu_in, pltpu.roll(mu, R - 1, 0))
            lam = dh + mu_next
            p0 = pl.multiple_of(jnp.maximum(r0 - SUBLANES, 0), SUBLANES)
            prev = h_ref[pl.ds(p0, SUBLANES), :][SUBLANES - 1:SUBLANES]
            prev = jnp.where(b > 0, prev, 0.0)
            h_prev = jnp.where(row == 0, prev, pltpu.roll(h, 1, 0))
            da = lam * h_prev
            d_mult = lam * rcb * gi
            d_la = da * a - d_mult * (a * a) / mult
            d_grp = d_la * (-LRU_C * sp) * gr * (1.0 - gr)
            d_gip = lam * rcb * mult * gi * (1.0 - gi)
            dgr_ref[rows, :] = d_grp.astype(dgr_ref.dtype)
            dgi_ref[rows, :] = d_gip.astype(dgi_ref.dtype)
            drc_ref[rows, :] = lam * gi * mult
            s_i = s_i + jnp.sum(d_gip, axis=0, keepdims=True)
            s_r = s_r + jnp.sum(d_grp, axis=0, keepdims=True)
            s_sp = s_sp + jnp.sum(d_la * gr, axis=0, keepdims=True)
            return mu_out, s_i, s_r, s_sp

        _, s_i, s_r, s_sp = lax.fori_loop(0, nb, step, (zero, zero, zero, zero))
        dbi_ref[...] = s_i
        dbr_ref[...] = s_r
        dl_ref[...] = (-LRU_C * s_sp) * (-_sigmoid(-lp))

    col = pl.BlockSpec((S, tc), lambda j: (0, j))
    vec = pl.BlockSpec((1, tc), lambda j: (0, j))
    return pl.pallas_call(
        body, name=name, grid=(C // tc,),
        in_specs=[col, col, col, col, col, col, vec],
        out_specs=[col, col, col, col, vec, vec, vec],
        out_shape=[_hbm_out((S, C), BF16), _hbm_out((S, C), BF16),
                   _hbm_out((S, C), BF16), _hbm_out((S, C), F32),
                   _hbm_out((1, C), F32), _hbm_out((1, C), F32),
                   _hbm_out((1, C), F32)],
        compiler_params=_params(("parallel",)),
    )(dm, proj, hrec, rc, gip, grp, lru_p)


def _cumsum_rows(name, u, reverse):
    S, C = u.shape
    nb = S // SCAN_ROWS

    def body(u_ref, o_ref):
        def step(t, carry):
            b = nb - 1 - t if reverse else t
            rows = pl.ds(pl.multiple_of(b * SCAN_ROWS, SCAN_ROWS), SCAN_ROWS)
            ub = u_ref[rows, :]
            h, carry = _block_scan(jnp.ones_like(ub), ub, carry, reverse)
            o_ref[rows, :] = h
            return carry

        lax.fori_loop(0, nb, step, jnp.zeros((1, C), F32))

    spec = pl.BlockSpec((S, C), lambda i: (0, 0))
    return pl.pallas_call(
        body, name=name, grid=(1,), in_specs=[spec], out_specs=spec,
        out_shape=_hbm_out((S, C), F32),
        compiler_params=_params(("arbitrary",)),
    )(u)


def _shift_down(x, k):
    row = lax.broadcasted_iota(jnp.int32, x.shape, 0)
    return jnp.where(row >= k, pltpu.roll(x, k, 0), 0.0)


def _shift_up(x, k):
    n = x.shape[0]
    row = lax.broadcasted_iota(jnp.int32, x.shape, 0)
    return jnp.where(row < n - k, pltpu.roll(x, n - k, 0), 0.0)


def _conv_fwd(name, proj, w, b, tc=256):
    S, C2 = proj.shape
    C = C2 // 2
    tc = _tile(C, tc)
    off = C // tc

    def body(x_ref, w_ref, b_ref, o_ref, ob_ref):
        x = x_ref[...]
        out = b_ref[...] + w_ref[3:4, :] * x
        for k in (1, 2, 3):
            out = out + w_ref[3 - k:4 - k, :] * _shift_down(x, k)
        o_ref[...] = out
        ob_ref[...] = out.astype(BF16)

    col = pl.BlockSpec((S, tc), lambda j: (0, j))
    return pl.pallas_call(
        body, name=name, grid=(C // tc,),
        in_specs=[pl.BlockSpec((S, tc), lambda j: (0, off + j)),
                  pl.BlockSpec((4, tc), lambda j: (0, j)), pl.BlockSpec((1, tc), lambda j: (0, j))],
        out_specs=[col, col],
        out_shape=[_hbm_out((S, C), F32), _hbm_out((S, C), BF16)],
        compiler_params=_params(("parallel",)),
    )(proj, w, b)


def _conv_bwd(name, drc, proj, w, tc=256):
    S, C = drc.shape
    tc = _tile(C, tc)
    off = C // tc

    def body(y_ref, x_ref, w_ref, dx_ref, dw_ref, db_ref):
        y = y_ref[...]
        x = x_ref[...]
        dx = w_ref[3:4, :] * y
        dw_ref[3:4, :] = jnp.sum(y * x, axis=0, keepdims=True)
        for k in (1, 2, 3):
            dx = dx + w_ref[3 - k:4 - k, :] * _shift_up(y, k)
            dw_ref[3 - k:4 - k, :] = jnp.sum(y * _shift_down(x, k), axis=0, keepdims=True)
        dx_ref[...] = dx.astype(dx_ref.dtype)
        db_ref[...] = jnp.sum(y, axis=0, keepdims=True)

    col = pl.BlockSpec((S, tc), lambda j: (0, j))
    return pl.pallas_call(
        body, name=name, grid=(C // tc,),
        in_specs=[col, pl.BlockSpec((S, tc), lambda j: (0, off + j)), pl.BlockSpec((4, tc), lambda j: (0, j))],
        out_specs=[col, pl.BlockSpec((4, tc), lambda j: (0, j)), pl.BlockSpec((1, tc), lambda j: (0, j))],
        out_shape=[_hbm_out((S, C), BF16), _hbm_out((4, C), F32),
                   _hbm_out((1, C), F32)],
        compiler_params=_params(("parallel",)),
    )(drc, proj, w)


def _gates_fwd(name, rcb, wg, bg):
    S, C = rcb.shape
    nblk, bw, _ = wg.shape

    def body(x_ref, w_ref, b_ref, gi_ref, gr_ref):
        g = jnp.dot(x_ref[...], w_ref[...], preferred_element_type=F32) + b_ref[...]
        gi_ref[...] = g[:, :bw]
        gr_ref[...] = g[:, bw:]

    col = pl.BlockSpec((S, bw), lambda n: (0, n))
    return pl.pallas_call(
        body, name=name, grid=(nblk,),
        in_specs=[col, pl.BlockSpec((None, bw, 2 * bw), lambda n: (n, 0, 0)),
                  pl.BlockSpec((None, 1, 2 * bw), lambda n: (n, 0, 0))],
        out_specs=[col, col],
        out_shape=[_hbm_out((S, C), F32), _hbm_out((S, C), F32)],
        compiler_params=_params(("parallel",)),
    )(rcb, wg, bg)


def _gates_bwd(name, dgi, dgr, rcb, wg, drc1):
    S, C = rcb.shape
    nblk, bw, _ = wg.shape

    def body(dgi_ref, dgr_ref, x_ref, w_ref, d1_ref, drc_ref, dw_ref):
        w = w_ref[...]
        x = x_ref[...]
        di, dr = dgi_ref[...], dgr_ref[...]
        drc_ref[...] = (d1_ref[...]
                        + lax.dot_general(di, w[:, :bw], _NT, preferred_element_type=F32)
                        + lax.dot_general(dr, w[:, bw:], _NT, preferred_element_type=F32))
        dw_ref[:, :bw] = lax.dot_general(x, di, _TN, preferred_element_type=F32).astype(dw_ref.dtype)
        dw_ref[:, bw:] = lax.dot_general(x, dr, _TN, preferred_element_type=F32).astype(dw_ref.dtype)

    col = pl.BlockSpec((S, bw), lambda n: (0, n))
    wspec = pl.BlockSpec((None, bw, 2 * bw), lambda n: (n, 0, 0))
    return pl.pallas_call(
        body, name=name, grid=(nblk,),
        in_specs=[col, col, col, wspec, col], out_specs=[col, wspec],
        out_shape=[_hbm_out((S, C), F32), _hbm_out((nblk, bw, 2 * bw), BF16)],
        compiler_params=_params(("parallel",)),
    )(dgi, dgr, rcb, wg, drc1)


def _att_tile(S):
    return next(t for t in (512, 256, 128) if S % t == 0)


def _head_lanes(shape):
    return lax.broadcasted_iota(jnp.int32, shape, len(shape) - 1) < HEAD_DIM


def _key_bias(c_ref, rows, hh):
    return jnp.broadcast_to(c_ref[rows, hh:hh + 1], (rows.size, LANES))


def _over_keys(x, op):
    n = x.shape[0]
    while n > SUBLANES:
        n //= 2
        x = op(x[:n], x[n:2 * n])
    return (jnp.max if op is jnp.maximum else jnp.sum)(x, axis=0, keepdims=True)


def _causal_t(T, cc):
    r = lax.broadcasted_iota(jnp.int32, (T, LANES), 0)
    c = lax.broadcasted_iota(jnp.int32, (T, LANES), 1) + cc * LANES
    return r <= c


def _attn_fwd(name, q, kv, c3):
    S, D = q.shape
    HP = D // LANES
    T = _att_tile(S)
    nq = S // T
    NC = T // LANES

    def body(q_ref, k_ref, v_ref, c_ref, o_ref, of_ref, lse_ref, bias, vT, acc, m_scr, l_scr):
        def prologue(i, _):
            rows = pl.ds(pl.multiple_of(i * T, T), T)
            for hh in range(2):
                bias[hh, rows, :] = _key_bias(c_ref, rows, hh)
            vT[i] = v_ref[rows, :].astype(F32).T.astype(BF16)
            return 0

        lax.fori_loop(0, nq, prologue, 0)

        def q_step(qi, _):
            q0 = pl.multiple_of(qi * T, T)
            qb = q_ref[pl.ds(q0, T), :]
            m_scr[...] = jnp.full(m_scr.shape, -jnp.inf, F32)
            l_scr[...] = jnp.zeros(l_scr.shape, F32)
            acc[...] = jnp.zeros(acc.shape, F32)

            def tile(kj, masked):
                ks = pl.ds(pl.multiple_of(kj * T, T), T)
                kf = k_ref[ks, :].astype(F32)
                first = _head_lanes(kf.shape)
                kms = [jnp.where(first if hh == 0 else jnp.logical_not(first), kf, 0.0).astype(BF16) for hh in range(2)]
                sTs = [lax.dot_general(km, qb, _NT, preferred_element_type=F32) for km in kms]
                for hh in range(2):
                    b = bias[hh, ks, :]
                    ps = []
                    for cc in range(NC):
                        cols = slice(cc * LANES, (cc + 1) * LANES)
                        s = sTs[hh][:, cols] + b
                        if masked:
                            s = jnp.where(_causal_t(T, cc), s, -jnp.inf)
                        m_old = m_scr[hh, cc]
                        m_new = jnp.maximum(m_old, _over_keys(s, jnp.maximum))
                        alpha = jnp.exp(m_old - m_new)
                        p = jnp.exp(s - m_new)
                        l_scr[hh, cc] = alpha * l_scr[hh, cc] + _over_keys(p, jnp.add)
                        m_scr[hh, cc] = m_new
                        ps.append(p.astype(BF16))
                        acc[hh, :, cols] = acc[hh, :, cols] * alpha
                    acc[hh] += jnp.dot(vT[kj, hh * HEAD_DIM:(hh + 1) * HEAD_DIM, :], jnp.concatenate(ps, axis=1),
                                       preferred_element_type=F32)

            def inner(kj, _):
                tile(kj, False)
                return 0

            lax.fori_loop(0, qi, inner, 0)
            tile(qi, True)
            outs = []
            for hh in range(2):
                inv = jnp.concatenate([1.0 / l_scr[hh, cc] for cc in range(NC)], axis=1)
                outs.append(acc[hh] * inv)
                for cc in range(NC):
                    lse_ref[hh:hh + 1, pl.ds(q0 + cc * LANES, LANES)] = m_scr[hh, cc] + jnp.log(l_scr[hh, cc])
            out = jnp.concatenate(outs, axis=0).T
            o_ref[pl.ds(q0, T), :] = out.astype(o_ref.dtype)
            of_ref[pl.ds(q0, T), :] = out
            return 0

        lax.fori_loop(0, nq, q_step, 0)

    blk = lambda off: pl.BlockSpec((S, LANES), lambda p: (0, off + p))
    return pl.pallas_call(
        body, name=name, grid=(HP,),
        in_specs=[blk(0), blk(0), blk(HP), pl.BlockSpec((None, S, 2), lambda p: (p, 0, 0))],
        out_specs=[blk(0), blk(0), pl.BlockSpec((None, 2, S), lambda p: (p, 0, 0))],
        out_shape=[_hbm_out((S, D), BF16), _hbm_out((S, D), F32),
                   _hbm_out((HP, 2, S), F32)],
        scratch_shapes=[pltpu.VMEM((2, S, LANES), F32), pltpu.VMEM((nq, LANES, T), BF16),
                        pltpu.VMEM((2, HEAD_DIM, T), F32), pltpu.VMEM((2, NC, 1, LANES), F32),
                        pltpu.VMEM((2, NC, 1, LANES), F32)],
        compiler_params=_params(("parallel",)),
    )(q, kv, kv, c3)


def _attn_bwd(name, q, kv, c3, of, do, lse3):
    S, D = q.shape
    HP = D // LANES
    T = _att_tile(S)
    nq = S // T
    NC = T // LANES
    scale = HEAD_DIM ** -0.5

    def body(q_ref, k_ref, v_ref, c_ref, of_ref, do_ref, lse_ref,
             dq_ref, dk_ref, dv_ref, dck_ref, drq_ref, bias, kT, dqT, delta, dr_scr, dk_acc, dv_acc, dc_acc):
        def prologue(i, _):
            rows = pl.ds(pl.multiple_of(i * T, T), T)
            for hh in range(2):
                bias[hh, rows, :] = _key_bias(c_ref, rows, hh)
            kT[i] = k_ref[rows, :].astype(F32).T.astype(BF16)
            prodT = (do_ref[rows, :].astype(F32) * of_ref[rows, :]).T
            for hh in range(2):
                delta[hh:hh + 1, rows] = jnp.sum(prodT[hh * HEAD_DIM:(hh + 1) * HEAD_DIM], axis=0, keepdims=True)
            dqT[i] = jnp.zeros((LANES, T), F32)
            return 0

        lax.fori_loop(0, nq, prologue, 0)
        dr_scr[...] = jnp.zeros(dr_scr.shape, F32)

        def kv_step(kj, _):
            ks = pl.ds(pl.multiple_of(kj * T, T), T)
            kf = k_ref[ks, :].astype(F32)
            vf = v_ref[ks, :].astype(F32)
            first = _head_lanes(kf.shape)
            masks = [first, jnp.logical_not(first)]
            kms = [jnp.where(m, kf, 0.0).astype(BF16) for m in masks]
            vms = [jnp.where(m, vf, 0.0).astype(BF16) for m in masks]

            for acc in (dk_acc, dv_acc, dc_acc):
                acc[...] = jnp.zeros(acc.shape, F32)

            def tile(qi, masked):
                q0 = pl.multiple_of(qi * T, T)
                qb = q_ref[pl.ds(q0, T), :]
                dob = do_ref[pl.ds(q0, T), :]
                sTs = [lax.dot_general(km, qb, _NT, preferred_element_type=F32) for km in kms]
                dpTs = [lax.dot_general(vm, dob, _NT, preferred_element_type=F32) for vm in vms]
                for hh in range(2):
                    b = bias[hh, ks, :]
                    head = slice(hh * HEAD_DIM, (hh + 1) * HEAD_DIM)
                    ps, dss = [], []
                    for cc in range(NC):
                        cols = slice(cc * LANES, (cc + 1) * LANES)
                        at = pl.ds(q0 + cc * LANES, LANES)
                        p = jnp.exp(sTs[hh][:, cols] + b - lse_ref[hh:hh + 1, at])
                        if masked:
                            p = jnp.where(_causal_t(T, cc), p, 0.0)
                        ds = p * (dpTs[hh][:, cols] - delta[hh:hh + 1, at])
                        ps.append(p.astype(BF16))
                        dss.append(ds.astype(BF16))
                        dc_acc[hh] += ds
                        dr_scr[hh:hh + 1, at] += _over_keys(ds, jnp.add)
                    pT = jnp.concatenate(ps, axis=1)
                    dsT = jnp.concatenate(dss, axis=1)
                    dv_acc[hh] += jnp.dot(pT, dob, preferred_element_type=F32)
                    dk_acc[hh] += jnp.dot(dsT, qb, preferred_element_type=F32)
                    dqT[qi, head, :] += jnp.dot(kT[kj, head, :], dsT, preferred_element_type=F32)

            def inner(qi, _):
                tile(qi, False)
                return 0

            tile(kj, True)
            lax.fori_loop(kj + 1, nq, inner, 0)
            dk_ref[ks, :] = jnp.where(first, dk_acc[0], dk_acc[1])
            dv_ref[ks, :] = jnp.where(first, dv_acc[0], dv_acc[1])
            for hh in range(2):
                dck_ref[hh:hh + 1, ks] = -jnp.sum(dc_acc[hh].T, axis=0, keepdims=True)
            return 0

        lax.fori_loop(0, nq, kv_step, 0)

        def epilogue(i, _):
            rows = pl.ds(pl.multiple_of(i * T, T), T)
            dq_ref[rows, :] = (dqT[i].T * scale).astype(dq_ref.dtype)
            return 0

        lax.fori_loop(0, nq, epilogue, 0)
        drq_ref[...] = dr_scr[...]

    blk = lambda off: pl.BlockSpec((S, LANES), lambda p: (0, off + p))
    row_spec = pl.BlockSpec((None, 2, S), lambda p: (p, 0, 0))
    return pl.pallas_call(
        body, name=name, grid=(HP,),
        in_specs=[blk(0), blk(0), blk(HP), pl.BlockSpec((None, S, 2), lambda p: (p, 0, 0)), blk(0), blk(0), row_spec],
        out_specs=[blk(0), blk(0), blk(0), row_spec, row_spec],
        out_shape=[_hbm_out((S, D), BF16), _hbm_out((S, D), F32),
                   _hbm_out((S, D), F32), _hbm_out((HP, 2, S), F32),
                   _hbm_out((HP, 2, S), F32)],
        scratch_shapes=[pltpu.VMEM((2, S, LANES), F32), pltpu.VMEM((nq, LANES, T), BF16),
                        pltpu.VMEM((nq, LANES, T), F32), pltpu.VMEM((2, S), F32), pltpu.VMEM((2, S), F32)]
        + [pltpu.VMEM((2, T, LANES), F32)] * 3,
        compiler_params=_params(("parallel",)),
    )(q, kv, kv, c3, of, do, lse3)


def _logsig_fwd(name, f):
    S, C = f.shape

    def body(f_ref, o_ref):
        o_ref[...] = -_softplus(-f_ref[...])

    spec = pl.BlockSpec((S, C), lambda i: (0, 0))
    return pl.pallas_call(body, name=name, grid=(1,), in_specs=[spec], out_specs=spec,
                          out_shape=_hbm_out((S, C), F32),
                          compiler_params=_params(("arbitrary",)))(f)


def _logsig_bwd(name, dls, f):
    S, C = f.shape

    def body(d_ref, f_ref, o_ref, s_ref):
        df = d_ref[...] * _sigmoid(-f_ref[...])
        o_ref[...] = df.astype(o_ref.dtype)
        s_ref[...] = jnp.sum(df, axis=0, keepdims=True)

    spec = pl.BlockSpec((S, C), lambda i: (0, 0))
    return pl.pallas_call(body, name=name, grid=(1,), in_specs=[spec, spec],
                          out_specs=[spec, pl.BlockSpec((1, C), lambda i: (0, 0))],
                          out_shape=[_hbm_out((S, C), BF16), _hbm_out((1, C), F32)],
                          compiler_params=_params(("arbitrary",)))(dls, f)


def _add_cast(name, parts, out_dtype, tr=256):
    S, C = parts[0].shape
    tr = _tile(S, tr)
    n = len(parts)

    def body(*refs):
        acc = refs[0][...].astype(F32)
        for r in refs[1:n]:
            acc = acc + r[...].astype(F32)
        refs[n][...] = acc.astype(out_dtype)

    spec = pl.BlockSpec((tr, C), lambda i: (i, 0))
    return pl.pallas_call(body, name=name, grid=(S // tr,), in_specs=[spec] * n, out_specs=spec,
                          out_shape=_hbm_out((S, C), out_dtype),
                          compiler_params=_params(("parallel",)))(*parts)


def _local_step(x, target, gains, layer_weights, layer_prefetch, layer_grads):
    S, D = x.shape
    HP = D // LANES
    scale = HEAD_DIM ** -0.5
    tx = _tile(S, 256)
    saved = []
    h = x
    l = 0
    kv = c3 = f_pre = hn_kv = h_kv = None
    while True:
        W = layer_weights(l, "mix", h)
        if W is None:
            break
        recurrent = "w_rec_in" in W
        if l == 0:
            xn = _rmsnorm_fwd("mix_norm_0", h, gains["mix"][0])
        if recurrent:
            CH = W["w_rec_in"].shape[-1]
            C = 2 * CH
            proj = _mm(f"rec_in_{l}", "nn", xn, W["w_rec_in"], grid=(1, N_CHIPS),
                       a_spec=pl.BlockSpec((S, D), lambda i, j: (0, 0)),
                       b_spec=pl.BlockSpec((None, D, CH), lambda i, j: (j, 0, 0)),
                       out_shape=(S, 2 * C), out_dtype=F32,
                       out_spec=pl.BlockSpec((S, CH), lambda i, j: (0, j)))
            layer_prefetch(l, "mix2", proj)
            rc, rcb = _conv_fwd(f"conv_{l}", proj, W["conv_w"], W["conv_b"])
            W = {**W, **layer_weights(l, "mix2", rcb)}
            gip, grp = _gates_fwd(f"gates_{l}", rcb, W["w_gates"], W["b_gates"])
            hrec, m = _lru_fwd(f"lru_{l}", proj, rc, gip, grp, W["lru_param"])
            layer_prefetch(l, "ffn", m)
            h_mid, hn = _mm_nn(f"rec_out_{l}", m, W["w_rec_out"], out_dtype=F32, res=h, tn=D, norm_gain=gains["ffn"][l])
            mix_saved = (xn, proj, rc, rcb, gip, grp, hrec, m)
        else:
            if "w_kv" in W:
                h_kv = h
                hn_kv = _rmsnorm_fwd("kv_norm", h, W["norm_kv"])
                kv = _mm_nn("kv_proj", hn_kv, W["w_kv"], out_dtype=BF16, tm=1024, tn=1024)
                f_pre = _mm_nn("f_proj", hn_kv, W["w_f"], out_dtype=F32, bias=W["b_f"])
                c = _cumsum_rows("c_cumsum", _logsig_fwd("logsig", f_pre), False)
                c3 = (-c[:, :2 * HP]).reshape(S, HP, 2).transpose(1, 0, 2)
            q = _mm_nn(f"q_proj_{l}", xn, W["w_q"], out_dtype=BF16, scale=scale, tm=1024, tn=1024)
            layer_prefetch(l, "mix2", q)
            o, of, lse = _attn_fwd(f"attn_fwd_{l}", q, kv, c3)
            W = {**W, **layer_weights(l, "mix2", o)}
            layer_prefetch(l, "ffn", o)
            h_mid, hn = _mm_nn(f"o_proj_{l}", o, W["w_o"], out_dtype=F32, res=h, tn=D, norm_gain=gains["ffn"][l])
            mix_saved = (xn, q, o, of, lse)
        W = {**W, **layer_weights(l, "ffn", h_mid)}
        z3, act = _swiglu_fwd(f"ffn_in_{l}", hn, W["w_ffn_in"])
        layer_prefetch(l + 1, "mix", act)
        saved.append((W, h, h_mid, mix_saved, (hn, z3, act)))
        l += 1
        if l < len(gains["mix"]):
            h, xn = _mm_nn(f"ffn_out_{l - 1}", act, W["w_ffn_out"], out_dtype=F32, res=h_mid, tn=D,
                           norm_gain=gains["mix"][l])
        else:
            h = _mm_nn(f"ffn_out_{l - 1}", act, W["w_ffn_out"], out_dtype=F32, res=h_mid, tn=D)

    dh, dhb, dg_final, loss_row = _loss_head("loss_head", h, target, gains["final"])

    dk_parts, dv_parts, dc_parts = [], [], []
    token = None
    for l in reversed(range(len(saved))):
        W, h_in, h_mid, mix_saved, (hn, z3, act) = saved[l]
        recurrent = "w_rec_in" in W
        FH = W["w_ffn_in"].shape[-1]
        G = {}
        norm_ffn = gains["ffn"][l]
        if token is not None:
            norm_ffn = norm_ffn + jnp.minimum(token[:1, :1], 0.0)
        G["w_ffn_out"] = _mm_tn(f"d_ffn_out_{l}", act, dhb, out_dtype=BF16, tn=D)
        dz3 = _swiglu_bwd(f"d_act_{l}", dhb, W["w_ffn_out"], z3)
        G["w_ffn_in"] = _mm(
            f"d_ffn_in_{l}", "tn", hn, dz3, grid=(1, N_CHIPS),
            a_spec=pl.BlockSpec((S, D), lambda i, j: (0, 0)),
            b_spec=pl.BlockSpec((None, S, FH), lambda i, j: (j // 2, 0, j % 2)),
            out_shape=(N_CHIPS, D, FH), out_dtype=BF16,
            out_spec=pl.BlockSpec((None, D, FH), lambda i, j: (j, 0, 0)))
        ffn_token = layer_grads(l, "ffn", G)
        G = {}
        if ffn_token is not None:
            norm_ffn = norm_ffn + jnp.minimum(ffn_token[:1, :1], 0.0)
        dh, dhb, dgp = _mm(f"d_ffn_hn_{l}", "nt", dz3, W["w_ffn_in"], grid=(S // tx, 1),
                           a_spec=[pl.BlockSpec((None, tx, FH), functools.partial(lambda i, j, k: (k // 2, i, k % 2), k=k))
                                   for k in range(N_CHIPS)],
                           b_spec=[pl.BlockSpec((None, D, FH), functools.partial(lambda i, j, k: (k, 0, 0), k=k))
                                   for k in range(N_CHIPS)],
                           out_shape=(S, D), out_dtype=F32, out_spec=pl.BlockSpec((tx, D), lambda i, j: (i, 0)),
                           norm_bwd=(h_mid, norm_ffn, dh))
        G["norm_ffn"] = jnp.sum(dgp, axis=0)
        if recurrent:
            CH = W["w_rec_in"].shape[-1]
            C = 2 * CH
            xn, proj, rc, rcb, gip, grp, hrec, m = mix_saved
            G["w_rec_out"] = _mm_tn(f"d_rec_out_{l}", m, dhb, out_dtype=BF16, tn=D)
            dm = _mm_nt(f"d_m_{l}", dhb, W["w_rec_out"], out_dtype=F32, tn=C)
            dgb, dgi, dgr, drc1, G["b_gi"], G["b_gr"], G["lru_param"] = _lru_bwd(
                f"d_lru_{l}", dm, proj, hrec, rc, gip, grp, W["lru_param"])
            drc, G["w_gates"] = _gates_bwd(f"d_gates_{l}", dgi, dgr, rcb, W["w_gates"], drc1)
            mix_token = layer_grads(l, "mix2", {n: G[n] for n in ("w_rec_out", "w_gates")})
            drec, G["conv_w"], G["conv_b"] = _conv_bwd(f"d_conv_{l}", drc, proj, W["conv_w"])
            dproj = jnp.concatenate([dgb, drec], axis=1)
            norm_mix = gains["mix"][l] if mix_token is None else gains["mix"][l] + jnp.minimum(mix_token[:1, :1], 0.0)
            G["w_rec_in"] = _mm(
                f"d_rec_in_{l}", "tn", xn, dproj, grid=(1, N_CHIPS),
                a_spec=pl.BlockSpec((S, D), lambda i, j: (0, 0)),
                b_spec=pl.BlockSpec((S, CH), lambda i, j: (0, j)),
                out_shape=(N_CHIPS, D, CH), out_dtype=BF16,
                out_spec=pl.BlockSpec((None, D, CH), lambda i, j: (j, 0, 0)))
            dh, dhb, dgp = _mm(f"d_rec_xn_{l}", "nt", dproj, W["w_rec_in"], grid=(S // tx, 1),
                               a_spec=[pl.BlockSpec((tx, CH), functools.partial(lambda i, j, k: (i, k), k=k))
                                       for k in range(N_CHIPS)],
                               b_spec=[pl.BlockSpec((None, D, CH), functools.partial(lambda i, j, k: (k, 0, 0), k=k))
                                       for k in range(N_CHIPS)],
                               out_shape=(S, D), out_dtype=F32, out_spec=pl.BlockSpec((tx, D), lambda i, j: (i, 0)),
                               norm_bwd=(h_in, norm_mix, dh))
        else:
            xn, q, o, of, lse = mix_saved
            G["w_o"] = _mm_tn(f"d_o_proj_{l}", o, dhb, out_dtype=BF16, tn=D)
            do = _mm_nt(f"d_o_{l}", dhb, W["w_o"], out_dtype=BF16, tm=1024, tn=D)
            mix_token = layer_grads(l, "mix2", {"w_o": G["w_o"]})
            dq, dk, dv, dck, drq = _attn_bwd(f"attn_bwd_{l}", q, kv, c3, of, do, lse)
            dk_parts.append(dk)
            dv_parts.append(dv)
            dc_parts.append((dck + drq).reshape(2 * HP, S).T)
            G["w_q"] = _mm_tn(f"d_q_proj_{l}", xn, dq, out_dtype=BF16, tn=D)
            norm_mix = gains["mix"][l] if mix_token is None else gains["mix"][l] + jnp.minimum(mix_token[:1, :1], 0.0)
            dh, dhb, dgp = _mm_nt(f"d_q_xn_{l}", dq, W["w_q"], out_dtype=F32, tn=D, norm_bwd=(h_in, norm_mix, dh))
        G["norm_mix"] = jnp.sum(dgp, axis=0)
        if "w_kv" in W:
            dkb = _add_cast("dk_sum", dk_parts, BF16)
            dvb = _add_cast("dv_sum", dv_parts, BF16)
            dkv = jnp.concatenate([dkb, dvb], axis=1)
            dc = sum(dc_parts[1:], dc_parts[0])
            dc_pad = jnp.pad(dc, ((0, 0), (0, LANES - 2 * HP)))
            dls = _cumsum_rows("dc_cumsum", dc_pad, True)
            dfb, G["b_f"] = _logsig_bwd("d_logsig", dls, f_pre)
            G["w_kv"] = _mm_tn("d_kv_proj", hn_kv, dkv, out_dtype=BF16, tn=1024)
            G["w_f"] = _mm_tn("d_f_proj", hn_kv, dfb, out_dtype=F32)
            dhn_f = _mm_nt("d_f_hn", dfb, W["w_f"], out_dtype=F32, tn=D)
            dh, dhb, dgp = _mm_nt("d_kv_hn", dkv, W["w_kv"], out_dtype=F32, tn=D, res=dhn_f,
                                  norm_bwd=(h_kv, W["norm_kv"], dh))
            G["norm_kv"] = jnp.sum(dgp, axis=0)
        token = layer_grads(l, "mix", G)
    return loss_row, dh, dg_final


_ANY = pl.BlockSpec(memory_space=pl.ANY)


def _position():
    return lax.axis_index("x"), lax.axis_index("y"), lax.axis_index("c")


def _chip_peers(x, y):
    return [(1 - x, y), (x, 1 - y), (1 - x, 1 - y)]


def _half_rows(c, n):
    h = n // 2
    assert h % 16 == 0
    return pl.ds(pl.multiple_of(c * h, 16), h)


def _place_own(name, shard, layer, me):
    _, R, C = shard.shape
    tr = _row_tile(R, C, 2 * shard.dtype.itemsize, target=8 << 20)

    def body(me_ref, x_ref, o_ref):
        o_ref[...] = x_ref[...]

    return pl.pallas_call(
        body, name=name,
        grid_spec=pltpu.PrefetchScalarGridSpec(
            num_scalar_prefetch=1, grid=(R // tr,),
            in_specs=[pl.BlockSpec((None, tr, C), lambda i, me_ref: (layer, i, 0))],
            out_specs=pl.BlockSpec((None, tr, C), lambda i, me_ref: (me_ref[0], i, 0))),
        out_shape=_hbm_out((N_CHIPS, R, C), shard.dtype),
        compiler_params=_params(("parallel",)),
    )(me, shard)


def _gather_smalls(name, smalls):
    ns = len(smalls)

    def body(*refs):
        ins, outs = refs[:ns], refs[ns:2 * ns]
        send_sems, recv_sems, local_sems = refs[2 * ns:]
        x, y, c = _position()
        me = 2 * x + y
        peers = _chip_peers(x, y)

        def remote(t, k, chip):
            px, py = peers[k]
            return pltpu.make_async_remote_copy(
                src_ref=ins[t], dst_ref=outs[t].at[chip], send_sem=send_sems.at[3 * t + k],
                recv_sem=recv_sems.at[3 * t + k], device_id=(px, py, c), device_id_type=MESH)

        local = [pltpu.make_async_copy(ins[t], outs[t].at[me], local_sems.at[t]) for t in range(ns)]
        for t in range(ns):
            local[t].start()
            for k in range(3):
                remote(t, k, me).start()
        for t in range(ns):
            for k in range(3):
                px, py = peers[k]
                remote(t, k, 2 * px + py).wait_recv()
        for t in range(ns):
            for k in range(3):
                remote(t, k, me).wait_send()
            local[t].wait()

    return pl.pallas_call(
        body, name=name, in_specs=[_ANY] * ns, out_specs=[_ANY] * ns,
        out_shape=[_hbm_out((N_CHIPS,) + s.shape, s.dtype) for s in smalls],
        scratch_shapes=[pltpu.SemaphoreType.DMA((3 * ns,)), pltpu.SemaphoreType.DMA((3 * ns,)),
                        pltpu.SemaphoreType.DMA((ns,))],
    )(*smalls)


_SEM = pl.BlockSpec(memory_space=pltpu.SEMAPHORE)
_SPLIT = pltpu.CompilerParams(has_side_effects=pltpu.SideEffectType.DATAFLOW_SIDE_EFFECTING)


def _weight_copy(shards, buf, items, sems, i, k, chip_of_dst, peers, c):
    w, l = items[i]
    px, py = peers[k]
    half = _half_rows(c, shards[w].shape[1])
    return pltpu.make_async_remote_copy(
        src_ref=shards[w].at[l, half], dst_ref=buf.at[chip_of_dst, half],
        send_sem=sems[0].at[3 * i + k], recv_sem=sems[1].at[3 * i + k],
        device_id=(px, py, c), device_id_type=MESH)


def _gather_start(name, shards, bufs, items, after):
    nw, n = len(shards), len(bufs)

    def body(*refs):
        ins, outs, sems = refs[:nw], refs[nw + n + 1:nw + 2 * n + 1], refs[nw + 2 * n + 1:]
        x, y, c = _position()
        peers = _chip_peers(x, y)
        for i in range(n):
            for k in range(3):
                _weight_copy(ins, outs[i], items, sems, i, k, 2 * x + y, peers, c).start()

    res = pl.pallas_call(
        body, name=name, in_specs=[_ANY] * (nw + n + 1), out_specs=[_ANY] * n + [_SEM, _SEM],
        out_shape=[_hbm_out(b.shape, b.dtype) for b in bufs]
        + [pltpu.SemaphoreType.DMA((3 * n,)), pltpu.SemaphoreType.DMA((3 * n,))],
        input_output_aliases={nw + i: i for i in range(n)}, compiler_params=_SPLIT,
    )(*shards, *bufs, after)
    return res[:n], res[n:]


def _gather_wait(name, shards, bufs, items, ids, sems, after):
    nw, m = len(shards), len(ids)

    def body(*refs):
        ins, bs = refs[:nw], refs[nw:nw + m]
        sem_refs = refs[nw + m:nw + m + 2]
        x, y, c = _position()
        peers = _chip_peers(x, y)
        for j, i in enumerate(ids):
            for k in range(3):
                px, py = peers[k]
                _weight_copy(ins, bs[j], items, sem_refs, i, k, 2 * px + py, peers, c).wait_recv()
        for j, i in enumerate(ids):
            for k in range(3):
                _weight_copy(ins, bs[j], items, sem_refs, i, k, 2 * x + y, peers, c).wait_send()

    res = pl.pallas_call(
        body, name=name, in_specs=[_ANY] * (nw + m) + [_SEM, _SEM, _ANY], out_specs=[_ANY] * m,
        out_shape=[_hbm_out(bufs[i].shape, bufs[i].dtype) for i in ids],
        input_output_aliases={nw + j: j for j in range(m)}, compiler_params=_SPLIT,
    )(*shards, *[bufs[i] for i in ids], *sems, after)
    return list(res)


def _forward_copy(src, dst, sems, i, k, core):
    x, y, c = _position()
    px, py = _chip_peers(x, y)[k]
    half = _half_rows(core, src.shape[1])
    return pltpu.make_async_remote_copy(
        src_ref=src.at[2 * px + py, half], dst_ref=dst.at[2 * px + py, half],
        send_sem=sems[0].at[3 * i + k], recv_sem=sems[1].at[3 * i + k],
        device_id=(x, y, 1 - c), device_id_type=MESH)


def _forward_start(name, bufs):
    n = len(bufs)

    def body(*refs):
        ins, outs, sems = refs[:n], refs[n:2 * n], refs[2 * n:]
        c = lax.axis_index("c")
        for i in range(n):
            for k in range(3):
                _forward_copy(ins[i], outs[i], sems, i, k, c).start()

    res = pl.pallas_call(
        body, name=name, in_specs=[_ANY] * n, out_specs=[_ANY] * n + [_SEM, _SEM],
        out_shape=[_hbm_out(g.shape, g.dtype) for g in bufs]
        + [pltpu.SemaphoreType.DMA((3 * n,)), pltpu.SemaphoreType.DMA((3 * n,))],
        input_output_aliases={i: i for i in range(n)}, compiler_params=_SPLIT,
    )(*bufs)
    return list(res[:n]), res[n:]


def _forward_wait(name, bufs, sems, after):
    n = len(bufs)

    def body(*refs):
        bs, sem_refs = refs[:n], refs[n:n + 2]
        c = lax.axis_index("c")
        for i in range(n):
            for k in range(3):
                _forward_copy(bs[i], bs[i], sem_refs, i, k, 1 - c).wait_recv()
        for i in range(n):
            for k in range(3):
                _forward_copy(bs[i], bs[i], sem_refs, i, k, c).wait_send()

    return list(pl.pallas_call(
        body, name=name, in_specs=[_ANY] * n + [_SEM, _SEM, _ANY], out_specs=[_ANY] * n,
        out_shape=[_hbm_out(g.shape, g.dtype) for g in bufs],
        input_output_aliases={i: i for i in range(n)}, compiler_params=_SPLIT,
    )(*bufs, *sems, after))


def _reduce_copy(grads, others, sems, i):
    x, y, c = _position()
    return pltpu.make_async_remote_copy(
        src_ref=grads[i].at[:, _half_rows(1 - c, grads[i].shape[1])], dst_ref=others[i],
        send_sem=sems[0].at[i], recv_sem=sems[1].at[i], device_id=(x, y, 1 - c), device_id_type=MESH)


def _reduce_start(name, grads, after):
    n = len(grads)

    def body(*refs):
        ins, outs, sems, token = refs[:n], refs[n + 1:2 * n + 1], refs[2 * n + 1:2 * n + 3], refs[2 * n + 3]
        for i in range(n):
            _reduce_copy(ins, outs, sems, i).start()
        token[...] = jnp.zeros_like(token)

    res = pl.pallas_call(
        body, name=name, in_specs=[_ANY] * (n + 1),
        out_specs=[_ANY] * n + [_SEM, _SEM, pl.BlockSpec(memory_space=pltpu.VMEM)],
        out_shape=[_hbm_out((N_CHIPS, g.shape[1] // 2, g.shape[2]), g.dtype) for g in grads]
        + [pltpu.SemaphoreType.DMA((n,)), pltpu.SemaphoreType.DMA((n,)), jax.ShapeDtypeStruct((SUBLANES, LANES), F32)],
        compiler_params=_SPLIT,
    )(*grads, after)
    return list(res[:n]), res[n:n + 2], res[n + 2]


def _reduce_wait(name, grads, others, sems, after):
    n = len(grads)

    def body(*refs):
        ins, os_, sem_refs = refs[:n], refs[n:2 * n], refs[2 * n:2 * n + 2]
        for i in range(n):
            _reduce_copy(ins, os_, sem_refs, i).wait_recv()
        for i in range(n):
            _reduce_copy(ins, os_, sem_refs, i).wait_send()

    return list(pl.pallas_call(
        body, name=name, in_specs=[_ANY] * (2 * n) + [_SEM, _SEM, _ANY], out_specs=[_ANY] * n,
        out_shape=[_hbm_out(o.shape, o.dtype) for o in others],
        input_output_aliases={n + i: i for i in range(n)}, compiler_params=_SPLIT,
    )(*grads, *others, *sems, after))


def _sum_cores(name, g, other, core):
    _, R, C = g.shape
    H = R // 2
    tr = _row_tile(H, C, 3 * 2, target=12 << 20)
    nb = H // tr

    def body(c_ref, g_ref, o_ref, out_ref):
        out_ref[...] = (g_ref[...].astype(F32) + o_ref[...].astype(F32)).astype(out_ref.dtype)

    return pl.pallas_call(
        body, name=name,
        grid_spec=pltpu.PrefetchScalarGridSpec(
            num_scalar_prefetch=1, grid=(N_CHIPS, nb),
            in_specs=[pl.BlockSpec((None, tr, C), lambda j, i, c_ref: (j, c_ref[0] * nb + i, 0)),
                      pl.BlockSpec((None, tr, C), lambda j, i, c_ref: (j, i, 0))],
            out_specs=pl.BlockSpec((None, tr, C), lambda j, i, c_ref: (j, i, 0))),
        out_shape=_hbm_out((N_CHIPS, H, C), BF16),
        compiler_params=_params(("parallel", "parallel")),
    )(core, g, other)


def _sum_chips(name, received, own, full, layer, me_core):
    _, H, C = received.shape
    tr = _row_tile(H, C, 3 * 2 + 2 + 4, target=12 << 20)
    nb = H // tr

    def body(s_ref, r_ref, own_ref, full_ref, out_ref):
        acc = r_ref[0].astype(F32)
        for k in (1, 2):
            acc = acc + r_ref[k].astype(F32)
        out_ref[...] = acc + own_ref[...].astype(F32)

    return pl.pallas_call(
        body, name=name,
        grid_spec=pltpu.PrefetchScalarGridSpec(
            num_scalar_prefetch=1, grid=(nb,),
            in_specs=[pl.BlockSpec((3, tr, C), lambda i, s_ref: (0, i, 0)),
                      pl.BlockSpec((None, tr, C), lambda i, s_ref: (s_ref[0], i, 0)),
                      _ANY],
            out_specs=pl.BlockSpec((None, tr, C), lambda i, s_ref: (layer, s_ref[1] * nb + i, 0))),
        out_shape=_hbm_out(full.shape, full.dtype),
        input_output_aliases={3: 0},
        compiler_params=_params(("parallel",)),
    )(me_core, received, own, full)


def _part_copy(parts, recv, sems, i, k, peers, c):
    px, py = peers[k]
    return pltpu.make_async_remote_copy(
        src_ref=parts[i].at[2 * px + py], dst_ref=recv[i].at[k],
        send_sem=sems[0].at[3 * i + k], recv_sem=sems[1].at[3 * i + k],
        device_id=(px, py, c), device_id_type=MESH)


def _scatter_start(name, parts):
    n = len(parts)

    def body(*refs):
        ins, outs, sems, token = refs[:n], refs[n:2 * n], refs[2 * n:2 * n + 2], refs[2 * n + 2]
        x, y, c = _position()
        peers = _chip_peers(x, y)
        for i in range(n):
            for k in range(3):
                _part_copy(ins, outs, sems, i, k, peers, c).start()
        token[...] = jnp.zeros_like(token)

    res = pl.pallas_call(
        body, name=name, in_specs=[_ANY] * n,
        out_specs=[_ANY] * n + [_SEM, _SEM, pl.BlockSpec(memory_space=pltpu.VMEM)],
        out_shape=[_hbm_out((3,) + p.shape[1:], p.dtype) for p in parts]
        + [pltpu.SemaphoreType.DMA((3 * n,)), pltpu.SemaphoreType.DMA((3 * n,)),
           jax.ShapeDtypeStruct((SUBLANES, LANES), F32)],
        compiler_params=_SPLIT,
    )(*parts)
    return list(res[:n]), res[n:n + 2], res[n + 2]


def _scatter_wait(name, parts, recv, sems, after):
    n = len(parts)

    def body(*refs):
        ins, rs, sem_refs = refs[:n], refs[n:2 * n], refs[2 * n:2 * n + 2]
        x, y, c = _position()
        peers = _chip_peers(x, y)
        for i in range(n):
            for k in range(3):
                _part_copy(ins, rs, sem_refs, i, k, peers, c).wait_recv()
        for i in range(n):
            for k in range(3):
                _part_copy(ins, rs, sem_refs, i, k, peers, c).wait_send()

    return list(pl.pallas_call(
        body, name=name, in_specs=[_ANY] * (2 * n) + [_SEM, _SEM, _ANY], out_specs=[_ANY] * n,
        out_shape=[_hbm_out(r.shape, r.dtype) for r in recv],
        input_output_aliases={n + i: i for i in range(n)}, compiler_params=_SPLIT,
    )(*parts, *recv, *sems, after))


def _share_d2d(name, full):
    n = len(full)

    def body(*refs):
        ins, outs = refs[:n], refs[n:2 * n]
        send_sems, recv_sems = refs[2 * n:]
        x, y, c = _position()

        def remote(w, core):
            half = _half_rows(core, ins[w].shape[1])
            return pltpu.make_async_remote_copy(
                src_ref=ins[w].at[:, half], dst_ref=outs[w].at[:, half],
                send_sem=send_sems.at[w], recv_sem=recv_sems.at[w],
                device_id=(x, y, 1 - c), device_id_type=MESH)

        for w in range(n):
            remote(w, c).start()
        for w in range(n):
            remote(w, 1 - c).wait_recv()
        for w in range(n):
            remote(w, c).wait_send()

    return pl.pallas_call(
        body, name=name, in_specs=[_ANY] * n, out_specs=[_ANY] * n,
        out_shape=[_hbm_out(f.shape, f.dtype) for f in full],
        input_output_aliases={w: w for w in range(n)},
        scratch_shapes=[pltpu.SemaphoreType.DMA((n,)), pltpu.SemaphoreType.DMA((n,))],
    )(*full)


def _all_copy(a_ref, o_ref, sems, k, slot):
    x, y, c = _position()
    return pltpu.make_async_remote_copy(
        src_ref=a_ref, dst_ref=o_ref.at[slot], send_sem=sems[0].at[k - 1], recv_sem=sems[1].at[k - 1],
        device_id=(x ^ ((k >> 2) & 1), y ^ ((k >> 1) & 1), c ^ (k & 1)), device_id_type=MESH)


def _gather_all_start(name, a):
    def body(a_ref, o_ref, send_sem, recv_sem, token):
        x, y, c = _position()
        for k in range(1, N_DEV):
            _all_copy(a_ref, o_ref, (send_sem, recv_sem), k, 4 * x + 2 * y + c).start()
        token[...] = jnp.zeros_like(token)

    out, send_sem, recv_sem, token = pl.pallas_call(
        body, name=name, in_specs=[_ANY], out_specs=[_ANY, _SEM, _SEM, pl.BlockSpec(memory_space=pltpu.VMEM)],
        out_shape=[_hbm_out((N_DEV,) + a.shape, a.dtype), pltpu.SemaphoreType.DMA((N_DEV - 1,)),
                   pltpu.SemaphoreType.DMA((N_DEV - 1,)), jax.ShapeDtypeStruct((SUBLANES, LANES), F32)],
        compiler_params=_SPLIT,
    )(a)
    return out, (send_sem, recv_sem), token


def _gather_all_wait(name, a, out, sems, after):
    def body(a_ref, o_ref, send_sem, recv_sem, after_ref, res_ref):
        x, y, c = _position()
        for k in range(1, N_DEV):
            peer = 4 * (x ^ ((k >> 2) & 1)) + 2 * (y ^ ((k >> 1) & 1)) + (c ^ (k & 1))
            _all_copy(a_ref, o_ref, (send_sem, recv_sem), k, peer).wait_recv()
        for k in range(1, N_DEV):
            _all_copy(a_ref, o_ref, (send_sem, recv_sem), k, 4 * x + 2 * y + c).wait_send()

    return pl.pallas_call(
        body, name=name, in_specs=[_ANY, _ANY, _SEM, _SEM, _ANY], out_specs=_ANY,
        out_shape=_hbm_out(out.shape, out.dtype), input_output_aliases={1: 0}, compiler_params=_SPLIT,
    )(a, out, *sems, after)


def _rows2d(a, lead=0):
    return a.reshape(a.shape[:lead] + (-1, a.shape[-1]))


def _row_tile(rows, cols, itemsize=4, target=1 << 20):
    want = max(SUBLANES, target // (cols * itemsize))
    t = min(rows, (want // 16) * 16)
    while t > 16 and rows % t:
        t -= 16
    return t if rows % t == 0 else rows


def _sum_slots(name, r, out_dtype=F32):
    ns = r.shape[0]
    r2 = _rows2d(r, 1)
    _, rows, cols = r2.shape
    tr = _row_tile(rows, cols)

    def body(r_ref, o_ref):
        acc = r_ref[0].astype(F32)
        for s in range(1, ns):
            acc = acc + r_ref[s].astype(F32)
        o_ref[...] = acc.astype(o_ref.dtype)

    out = pl.pallas_call(
        body, name=name, grid=(rows // tr,),
        in_specs=[pl.BlockSpec((ns, tr, cols), lambda i: (0, i, 0))],
        out_specs=pl.BlockSpec((tr, cols), lambda i: (i, 0)),
        out_shape=_hbm_out((rows, cols), out_dtype),
        compiler_params=_params(("parallel",)),
    )(r2)
    return out.reshape(r.shape[1:])


def _adamw(name, g_parts, w, m, v):
    shape = w.shape
    ng = len(g_parts)
    args = [_rows2d(a) for a in (*g_parts, w, m, v)]
    rows, cols = args[0].shape
    tr = _row_tile(rows, cols, (ng + 7) * 4, target=16 << 20)
    c1 = 1.0 - ADAM_B1 ** ADAM_STEP
    c2 = 1.0 - ADAM_B2 ** ADAM_STEP

    def body(*refs):
        g = refs[0][...]
        for r in refs[1:ng]:
            g = g + r[...]
        w_ref, m_ref, v_ref = refs[ng:ng + 3]
        g_out, d_out, m_out, v_out = refs[ng + 3:]
        mn = ADAM_B1 * m_ref[...] + (1.0 - ADAM_B1) * g
        vn = ADAM_B2 * v_ref[...] + (1.0 - ADAM_B2) * (g * g)
        m_hat = mn / c1
        v_hat = vn / c2
        g_out[...] = g
        d_out[...] = -ADAM_LR * (m_hat / (jnp.sqrt(v_hat) + ADAM_EPS) + ADAM_WD * w_ref[...])
        m_out[...] = mn
        v_out[...] = vn

    spec = pl.BlockSpec((tr, cols), lambda i: (i, 0))
    outs = pl.pallas_call(
        body, name=name, grid=(rows // tr,), in_specs=[spec] * (ng + 3), out_specs=[spec] * 4,
        out_shape=[_hbm_out((rows, cols), F32)] * 4,
        compiler_params=_params(("parallel",)),
    )(*args)
    return tuple(o.reshape(shape) for o in outs)


_WEIGHTS = ["norm_mix", "norm_ffn", "w_ffn_in", "w_ffn_out", "w_rec_in", "conv_w", "conv_b", "w_lru_gates",
            "b_lru_gates", "lru_param", "w_rec_out", "norm_kv", "w_kvf", "b_forget", "w_q", "w_o", "norm_final"]
_BIG = ["w_ffn_in", "w_ffn_out", "w_rec_in", "w_lru_gates", "w_rec_out", "w_kvf", "w_q", "w_o"]


def _stack3(a):
    return a[None] if a.ndim == 2 else a.reshape(a.shape[0], -1, a.shape[-1])


def _pad_lanes(a, n):
    return jnp.pad(a, ((0, 0),) * (a.ndim - 1) + ((0, n - a.shape[-1]),))


def kernel(x, norm_mix, norm_ffn, w_ffn_in, w_ffn_out, w_rec_in, conv_w, conv_b, w_lru_gates, b_lru_gates, lru_param, w_rec_out, norm_kv, w_kvf, b_forget, w_q, w_o, norm_final, loss_target, m_norm_mix, m_norm_ffn, m_w_ffn_in, m_w_ffn_out, m_w_rec_in, m_conv_w, m_conv_b, m_w_lru_gates, m_b_lru_gates, m_lru_param, m_w_rec_out, m_norm_kv, m_w_kvf, m_b_forget, m_w_q, m_w_o, m_norm_final, v_norm_mix, v_norm_ffn, v_w_ffn_in, v_w_ffn_out, v_w_rec_in, v_conv_w, v_conv_b, v_w_lru_gates, v_b_lru_gates, v_lru_param, v_w_rec_out, v_norm_kv, v_w_kvf, v_b_forget, v_w_q, v_w_o, v_norm_final):
    P = dict(norm_mix=norm_mix, norm_ffn=norm_ffn, w_ffn_in=w_ffn_in, w_ffn_out=w_ffn_out, w_rec_in=w_rec_in,
             conv_w=conv_w, conv_b=conv_b, w_lru_gates=w_lru_gates, b_lru_gates=b_lru_gates, lru_param=lru_param,
             w_rec_out=w_rec_out, norm_kv=norm_kv, w_kvf=w_kvf, b_forget=b_forget, w_q=w_q, w_o=w_o,
             norm_final=norm_final)
    M1 = dict(norm_mix=m_norm_mix, norm_ffn=m_norm_ffn, w_ffn_in=m_w_ffn_in, w_ffn_out=m_w_ffn_out,
              w_rec_in=m_w_rec_in, conv_w=m_conv_w, conv_b=m_conv_b, w_lru_gates=m_w_lru_gates,
              b_lru_gates=m_b_lru_gates, lru_param=m_lru_param, w_rec_out=m_w_rec_out, norm_kv=m_norm_kv,
              w_kvf=m_w_kvf, b_forget=m_b_forget, w_q=m_w_q, w_o=m_w_o, norm_final=m_norm_final)
    M2 = dict(norm_mix=v_norm_mix, norm_ffn=v_norm_ffn, w_ffn_in=v_w_ffn_in, w_ffn_out=v_w_ffn_out,
              w_rec_in=v_w_rec_in, conv_w=v_conv_w, conv_b=v_conv_b, w_lru_gates=v_w_lru_gates,
              b_lru_gates=v_b_lru_gates, lru_param=v_lru_param, w_rec_out=v_w_rec_out, norm_kv=v_norm_kv,
              w_kvf=v_w_kvf, b_forget=v_b_forget, w_q=v_w_q, w_o=v_w_o, norm_final=v_norm_final)

    _, S, D = x.shape
    L = norm_mix.shape[0]
    NA, NBLK, BW, GS = w_lru_gates.shape
    C = NBLK * BW
    CS = C // N_CHIPS
    H = b_forget.shape[0]
    assert C == D and H * HEAD_DIM == D and H <= LANES
    chip = 2 * lax.axis_index("x") + lax.axis_index("y")

    small_a = jnp.concatenate([conv_w, conv_b[:, None], lru_param[:, None]], axis=1)
    small_a, b_gates = _gather_smalls("gather_smalls", [small_a, b_lru_gates])
    small_a = small_a.transpose(1, 2, 0, 3).reshape(NA, 6, C)
    b_gates = b_gates.transpose(1, 2, 0, 3).reshape(NA, NBLK, 1, N_CHIPS * GS)
    shards = [_stack3(P[w]).astype(BF16) for w in _BIG]
    core = lax.axis_index("c")
    chip_id = jnp.reshape(chip, (1,)).astype(jnp.int32)
    core_id = jnp.reshape(core, (1,)).astype(jnp.int32)
    me_core = jnp.stack([chip, core]).astype(jnp.int32)

    parts_of_layer = ("mix", "mix2", "ffn")

    def part_items(l, part):
        if part == "ffn":
            names, at = ["w_ffn_in", "w_ffn_out"], l
        elif l < NA:
            names, at = (["w_rec_in"] if part == "mix" else ["w_lru_gates", "w_rec_out"]), l
        else:
            names, at = ((["w_kvf"] if l == NA else []) + ["w_q"] if part == "mix" else ["w_o"]), l - NA
        return [(_BIG.index(n), 0 if n == "w_kvf" else at) for n in names]

    def stage_of(l, part):
        return (l, part) if l == 0 or part == "ffn" else (l, "mixer")

    def stage_items(st):
        l, part = st
        return [it for p in (("mix", "mix2") if part == "mixer" else (part,)) for it in part_items(l, p)]

    stages = [(0, p) for p in parts_of_layer] + [(l, p) for l in range(1, L) for p in ("mixer", "ffn")]
    items = [it for st in stages for it in stage_items(st)]
    ids_of = {st: [items.index(it) for it in stage_items(st)] for st in stages}
    bufs = [_place_own(f"place_{_BIG[w]}_{li}", shards[w], li, chip_id) for w, li in items]
    bufs, gather_sems = _gather_start("gather_start", shards, bufs, items, small_a)

    forwarding, fetched = {}, {}

    def layer_prefetch(l, part, after):
        st = stage_of(l, part)
        if l < L and st not in forwarding:
            got = _gather_wait(f"gather_wait_{st[1]}_{l}", shards, bufs, items, ids_of[st], gather_sems, after)
            forwarding[st] = _forward_start(f"forward_start_{st[1]}_{l}", got)

    def layer_weights(l, part, after):
        if l >= L:
            return None
        st = stage_of(l, part)
        if st not in fetched:
            layer_prefetch(l, part, after)
            got, sems = forwarding[st]
            got = _forward_wait(f"forward_wait_{st[1]}_{l}", got, sems, after)
            fetched[st] = {_BIG[items[i][0]]: g for i, g in zip(ids_of[st], got)}
        B = fetched[st]
        if part == "ffn":
            return dict(w_ffn_in=B["w_ffn_in"], w_ffn_out=B["w_ffn_out"].reshape(-1, D))
        if l < NA and part == "mix":
            return dict(w_rec_in=B["w_rec_in"], conv_w=small_a[l, :4], conv_b=small_a[l, 4:5])
        if l < NA:
            return dict(w_gates=B["w_lru_gates"].reshape(N_CHIPS, NBLK, BW, GS).transpose(1, 2, 0, 3).reshape(
                NBLK, BW, N_CHIPS * GS), b_gates=b_gates[l], w_rec_out=B["w_rec_out"].reshape(C, D),
                lru_param=small_a[l, 5:6])
        if part == "mix2":
            return dict(w_o=B["w_o"].reshape(D, D))
        W = dict(w_q=B["w_q"].reshape(D, D))
        if l == NA:
            w_kvf_full = B["w_kvf"].transpose(1, 0, 2).reshape(D, -1)
            W.update(norm_kv=norm_kv[None], w_kv=w_kvf_full[:, :2 * D],
                     w_f=_pad_lanes(w_kvf_full[:, 2 * D:], LANES), b_f=_pad_lanes(b_forget[None], LANES))
        return W

    G_small = {l: {} for l in range(L)}
    stash = {st: {} for st in stages}
    pending = {}
    reducing = []

    def finish_reduce(after):
        st, its, grads, others, sems = reducing.pop()
        l, part = st
        others = _reduce_wait(f"reduce_wait_{part}_{l}", grads, others, sems, after)
        parts = [_sum_cores(f"sum_cores_{l}_{_BIG[w]}", g, o, core_id) for (w, _), g, o in zip(its, grads, others)]
        recv, sems, token = _scatter_start(f"scatter_start_{part}_{l}", parts)
        pending[st] = (parts, recv, sems)
        return token

    def layer_grads(l, part, G_part):
        G_small[l].update(G_part)
        st = stage_of(l, part)
        stash[st].update(G_part)
        if st[1] == "mixer" and part != "mix":
            return None
        G = stash[st]
        late = {"ffn": "w_ffn_in", "mix": "norm_mix"}.get(part) or ("w_gates" if l < NA else "w_o")
        after = finish_reduce(G_part[late]) if reducing else jnp.zeros((SUBLANES, LANES), F32)
        by_name = dict(
            w_ffn_in=lambda: G["w_ffn_in"], w_ffn_out=lambda: G["w_ffn_out"].reshape(N_CHIPS, -1, D),
            w_rec_in=lambda: G["w_rec_in"],
            w_lru_gates=lambda: G["w_gates"].reshape(NBLK, BW, N_CHIPS, GS).transpose(2, 0, 1, 3).reshape(
                N_CHIPS, NBLK * BW, GS),
            w_rec_out=lambda: G["w_rec_out"].reshape(N_CHIPS, -1, D),
            w_kvf=lambda: jnp.concatenate([G["w_kv"].astype(F32), G["w_f"][:, :H]], axis=1).reshape(
                D, N_CHIPS, -1).transpose(1, 0, 2).astype(BF16),
            w_q=lambda: G["w_q"].reshape(N_CHIPS, -1, D), w_o=lambda: G["w_o"].reshape(N_CHIPS, -1, D))
        its = stage_items(st)
        grads = [by_name[_BIG[w]]() for w, _ in its]
        others, sems, token = _reduce_start(f"reduce_start_{st[1]}_{l}", grads, after)
        reducing.append((st, its, grads, others, sems))
        return finish_reduce(token) if l == 0 else token

    gains = dict(mix=[norm_mix[l][None] for l in range(L)], ffn=[norm_ffn[l][None] for l in range(L)],
                 final=norm_final[None])
    loss_row, grad_x, dg_final = _local_step(x.reshape(S, D), loss_target.reshape(S, D), gains,
                                             layer_weights, layer_prefetch, layer_grads)

    rows = [*[G_small[l]["norm_mix"] for l in range(L)], *[G_small[l]["norm_ffn"] for l in range(L)],
            G_small[NA]["norm_kv"], dg_final, _pad_lanes(G_small[NA]["b_f"], D), _pad_lanes(loss_row, D)]
    for a in range(NA):
        rows += [G_small[a][n] for n in ("conv_w", "conv_b", "b_gi", "b_gr", "lru_param")]
    packed = jnp.concatenate(rows, axis=0)
    everyone, small_sems, small_token = _gather_all_start("gather_small_start", packed)

    full = [lax.empty(sh.shape, F32) for sh in shards]
    for st in reversed(stages):
        l, part = st
        parts, recv, sems = pending[st]
        recv = _scatter_wait(f"scatter_wait_{part}_{l}", parts, recv, sems, small_token)
        for (w, li), own, r in zip(stage_items(st), parts, recv):
            full[w] = _sum_chips(f"sum_chips_{l}_{_BIG[w]}", r, own, full[w], li, me_core)
    full = _share_d2d("share_d2d", full)
    big = {w: _adamw(f"adamw_{w}", [g.reshape(P[w].shape)], P[w], M1[w], M2[w]) for w, g in zip(_BIG, full)}

    everyone = _gather_all_wait("gather_small_wait", packed, everyone, small_sems, big[_BIG[-1]][1])
    everyone = lax.dynamic_update_slice(everyone, packed[None], (2 * chip + core, 0, 0))
    tot = _sum_slots("sum_small", everyone)
    loss = tot[2 * L + 3, 0]
    g_rep = jnp.concatenate([tot[:2 * L + 2], tot[2 * L + 2:2 * L + 3]], axis=0)
    base = 2 * L + 4
    g_sh = []
    for a in range(NA):
        blk = lax.dynamic_slice_in_dim(tot[base + 8 * a:base + 8 * a + 8], chip * CS, CS, axis=1)
        gi = tot[base + 8 * a + 5].reshape(NBLK, BW)
        gr = tot[base + 8 * a + 6].reshape(NBLK, BW)
        bl = lax.dynamic_slice_in_dim(jnp.concatenate([gi, gr], axis=1), chip * GS, GS, axis=1)
        g_sh += [blk[:5], bl.reshape(-1, CS), blk[7:8]]
    g_sh = jnp.concatenate(g_sh, axis=0)
    nrow = g_sh.shape[0] // NA

    def pack_rep(T):
        return jnp.concatenate([T["norm_mix"], T["norm_ffn"], T["norm_kv"][None], T["norm_final"][None],
                                _pad_lanes(T["b_forget"][None], D)], axis=0)

    def pack_sh(T):
        return jnp.concatenate([jnp.concatenate([T["conv_w"][a], T["conv_b"][a][None],
                                                 T["b_lru_gates"][a].reshape(-1, CS), T["lru_param"][a][None]], axis=0)
                                for a in range(NA)], axis=0)

    rep = _adamw("adamw_replicated", [g_rep], pack_rep(P), pack_rep(M1), pack_rep(M2))
    shd = _adamw("adamw_small_sharded", [g_sh], pack_sh(P), pack_sh(M1), pack_sh(M2))

    def unpack_rep(t):
        return dict(norm_mix=t[:L], norm_ffn=t[L:2 * L], norm_kv=t[2 * L], norm_final=t[2 * L + 1],
                    b_forget=t[2 * L + 2, :H])

    def unpack_sh(t):
        t = t.reshape(NA, nrow, CS)
        return dict(conv_w=t[:, :4], conv_b=t[:, 4], b_lru_gates=t[:, 5:nrow - 1].reshape(NA, NBLK, GS),
                    lru_param=t[:, nrow - 1])

    outs = []
    for i in range(4):
        small = {**unpack_rep(rep[i]), **unpack_sh(shd[i])}
        outs.append([big[w][i] if w in big else small[w] for w in _WEIGHTS])
    return (loss, grad_x.reshape(1, S, D), *outs[0], *outs[1], *outs[2], *outs[3])
```

```python
import functools
import math

import jax
import jax.numpy as jnp
from jax import lax
from jax.experimental import pallas as pl
from jax.experimental.pallas import tpu as pltpu

F32 = jnp.float32
BF16 = jnp.bfloat16

EPS = 1e-6
LRU_C = 8.0
HEAD_DIM = 64
LANES = 128
SUBLANES = 8
VMEM_LIMIT = 48 * 1024 * 1024
N_CHIPS = 4
N_DEV = 8

ADAM_LR = 0.001
ADAM_B1 = 0.9
ADAM_B2 = 0.999
ADAM_EPS = 1e-08
ADAM_WD = 0.01
ADAM_STEP = 10

_NN = (((1,), (0,)), ((), ()))
_NT = (((1,), (1,)), ((), ()))
_TN = (((0,), (0,)), ((), ()))
_DN = {"nn": _NN, "nt": _NT, "tn": _TN}
MESH = pl.DeviceIdType.MESH


def _hbm_out(shape, dtype):
    return pltpu.HBM(shape, dtype)


def _params(sem):
    return pltpu.CompilerParams(dimension_semantics=sem, vmem_limit_bytes=VMEM_LIMIT)


def _tile(n, want):
    if n <= want:
        return n
    t = (want // LANES) * LANES
    while t >= LANES:
        if n % t == 0:
            return t
        t -= LANES
    return n


def _sigmoid(x):
    return 1.0 / (1.0 + jnp.exp(-x))


def _sigmoid_t(x):
    return 0.5 * jnp.tanh(0.5 * x) + 0.5


def _softplus(x):
    return jnp.maximum(x, 0.0) + jnp.log(1.0 + jnp.exp(-jnp.abs(x)))


_GELU_C = math.sqrt(2.0 / math.pi)


def _gelu_and_grad(x):
    inner = _GELU_C * (x + 0.044715 * x * x * x)
    t = jnp.tanh(inner)
    g = 0.5 * x * (1.0 + t)
    dg = 0.5 * (1.0 + t) + 0.5 * x * (1.0 - t * t) * _GELU_C * (1.0 + 3.0 * 0.044715 * x * x)
    return g, dg


def _rms(x):
    return lax.rsqrt(jnp.mean(x * x, axis=-1, keepdims=True) + EPS)


def _rms_bwd(dy, x, g):
    r = _rms(x)
    xr = x * r
    dyg = dy * g
    return r * dyg - xr * (r * jnp.mean(dyg * xr, axis=-1, keepdims=True)), jnp.sum(dy * xr, axis=0, keepdims=True)


def _mm(name, mode, a, b, *, grid, a_spec, b_spec, out_shape, out_dtype, out_spec, nk=1,
        res=None, res_spec=None, bias=None, bias_spec=None, scale=None, norm_gain=None, norm_bwd=None):
    dn = _DN[mode]
    has_res, has_bias = res is not None, bias is not None
    blk = tuple(d for d in out_spec.block_shape if d is not None)
    vec = pl.BlockSpec((1, blk[-1]), lambda *g: (0, 0))
    a_specs = a_spec if isinstance(a_spec, list) else [a_spec]
    b_specs = b_spec if isinstance(b_spec, list) else [b_spec]
    npair = len(a_specs)
    n_in = 2 * npair + int(has_res) + int(has_bias) + (1 if norm_gain is not None else 0) + (3 if norm_bwd else 0)

    def body(*refs):
        p = 2 * npair
        r_ref = refs[p] if has_res else None
        p += int(has_res)
        bias_ref = refs[p] if has_bias else None
        p += int(has_bias)
        extra = refs[p:n_in]
        outs = refs[n_in:]
        o_ref = outs[0]
        part = lax.dot_general(refs[0][...], refs[npair][...], dn, preferred_element_type=F32)
        for t in range(1, npair):
            part = part + lax.dot_general(refs[t][...], refs[npair + t][...], dn, preferred_element_type=F32)

        def finish(acc):
            if scale is not None:
                acc = acc * scale
            if has_bias:
                acc = acc + bias_ref[...]
            if has_res:
                acc = r_ref[...] + acc
            if norm_bwd:
                h_ref, g_ref, dh_ref = extra
                dx, dg = _rms_bwd(acc, h_ref[...], g_ref[...])
                acc = dh_ref[...] + dx
                outs[1][...] = acc.astype(BF16)
                outs[2][...] = dg
            if norm_gain is not None:
                outs[1][...] = (acc * _rms(acc) * extra[0][...]).astype(BF16)
            o_ref[...] = acc.astype(o_ref.dtype)

        if nk == 1:
            finish(part)
        else:
            acc_ref = refs[-1]
            k = pl.program_id(2)

            @pl.when(k == 0)
            def _():
                acc_ref[...] = part

            @pl.when(k > 0)
            def _():
                acc_ref[...] += part

            @pl.when(k == nk - 1)
            def _():
                finish(acc_ref[...])

    ins, specs = [a] * npair + [b] * npair, a_specs + b_specs
    if has_res:
        ins.append(res)
        specs.append(res_spec)
    if has_bias:
        ins.append(bias)
        specs.append(bias_spec)
    out_specs, out_shapes = [out_spec], [_hbm_out(out_shape, out_dtype)]
    if norm_gain is not None:
        ins.append(norm_gain)
        specs.append(vec)
        out_specs.append(out_spec)
        out_shapes.append(_hbm_out(out_shape, BF16))
    if norm_bwd:
        h, g, dh = norm_bwd
        ins += [h, g, dh]
        specs += [out_spec, vec, out_spec]
        out_specs += [out_spec, pl.BlockSpec((None, 1, blk[-1]), lambda i, *rest: (i, 0, 0))]
        out_shapes += [_hbm_out(out_shape, BF16), _hbm_out((grid[0], 1, blk[-1]), F32)]
    sem = ("parallel", "parallel") + (("arbitrary",) if len(grid) == 3 else ())
    single = len(out_specs) == 1
    return pl.pallas_call(
        body, name=name, grid=grid, in_specs=specs, out_specs=out_specs[0] if single else out_specs,
        out_shape=out_shapes[0] if single else out_shapes,
        scratch_shapes=[pltpu.VMEM(blk, F32)] if nk > 1 else [],
        compiler_params=_params(sem),
    )(*ins)


def _mm_nn(name, a, b, *, b_lead=(), out_dtype, tm=512, tn=512, res=None, bias=None, scale=None, norm_gain=None):
    M, K = a.shape
    N = b.shape[-1]
    tm, tn = _tile(M, tm), _tile(N, tn)
    nl = len(b_lead)
    return _mm(
        name, "nn", a, b, grid=(M // tm, N // tn),
        a_spec=pl.BlockSpec((tm, K), lambda i, j: (i, 0)),
        b_spec=pl.BlockSpec((None,) * nl + (K, tn), lambda i, j: tuple(b_lead) + (0, j)),
        out_shape=(M, N), out_dtype=out_dtype, out_spec=pl.BlockSpec((tm, tn), lambda i, j: (i, j)),
        res=res, res_spec=pl.BlockSpec((tm, tn), lambda i, j: (i, j)),
        bias=bias, bias_spec=pl.BlockSpec((1, tn), lambda i, j: (0, j)), scale=scale, norm_gain=norm_gain)


def _mm_nt(name, a, b, *, b_lead=(), out_dtype, tm=512, tn=512, tk=2048, res=None, norm_bwd=None):
    M, K = a.shape
    N = b.shape[-2]
    tm, tn, tk = _tile(M, tm), _tile(N, tn), _tile(K, tk)
    nk = K // tk
    nl = len(b_lead)
    return _mm(
        name, "nt", a, b, grid=(M // tm, N // tn, nk), nk=nk,
        a_spec=pl.BlockSpec((tm, tk), lambda i, j, k: (i, k)),
        b_spec=pl.BlockSpec((None,) * nl + (tn, tk), lambda i, j, k: tuple(b_lead) + (j, k)),
        out_shape=(M, N), out_dtype=out_dtype, out_spec=pl.BlockSpec((tm, tn), lambda i, j, k: (i, j)),
        res=res, res_spec=pl.BlockSpec((tm, tn), lambda i, j, k: (i, j)), norm_bwd=norm_bwd)


def _mm_tn(name, a, b, *, out_dtype, tm=512, tn=512):
    S, M = a.shape
    N = b.shape[1]
    tm, tn = _tile(M, tm), _tile(N, tn)
    return _mm(
        name, "tn", a, b, grid=(M // tm, N // tn),
        a_spec=pl.BlockSpec((S, tm), lambda i, j: (0, i)),
        b_spec=pl.BlockSpec((S, tn), lambda i, j: (0, j)),
        out_shape=(M, N), out_dtype=out_dtype, out_spec=pl.BlockSpec((tm, tn), lambda i, j: (i, j)))


def _rmsnorm_fwd(name, h, g, tr=256):
    S, D = h.shape
    tr = _tile(S, tr)

    def body(h_ref, g_ref, o_ref):
        x = h_ref[...]
        r = lax.rsqrt(jnp.mean(x * x, axis=-1, keepdims=True) + EPS)
        o_ref[...] = (x * r * g_ref[...]).astype(o_ref.dtype)

    return pl.pallas_call(
        body, name=name, grid=(S // tr,),
        in_specs=[pl.BlockSpec((tr, D), lambda i: (i, 0)), pl.BlockSpec((1, D), lambda i: (0, 0))],
        out_specs=pl.BlockSpec((tr, D), lambda i: (i, 0)),
        out_shape=_hbm_out((S, D), BF16),
        compiler_params=_params(("parallel",)),
    )(h, g)


def _loss_head(name, h, target, g, tr=256):
    S, D = h.shape
    tr = _tile(S, tr)

    def body(h_ref, t_ref, g_ref, o_ref, ob_ref, dg_ref, loss_ref):
        i = pl.program_id(0)
        x = h_ref[...]
        gg = g_ref[...]
        r = lax.rsqrt(jnp.mean(x * x, axis=-1, keepdims=True) + EPS)
        xr = x * r
        err = xr * gg - t_ref[...]
        lpart = 0.5 * jnp.sum(jnp.mean(err * err, axis=-1, keepdims=True), axis=0, keepdims=True)
        dy = err * (1.0 / D)
        dyg = dy * gg
        dx = r * dyg - xr * (r * jnp.mean(dyg * xr, axis=-1, keepdims=True))
        o_ref[...] = dx
        ob_ref[...] = dx.astype(BF16)
        part = jnp.sum(dy * xr, axis=0, keepdims=True)
        lrow = jnp.broadcast_to(lpart, (1, LANES))

        @pl.when(i == 0)
        def _():
            dg_ref[...] = part
            loss_ref[...] = lrow

        @pl.when(i > 0)
        def _():
            dg_ref[...] += part
            loss_ref[...] += lrow

    row = pl.BlockSpec((tr, D), lambda i: (i, 0))
    vec = pl.BlockSpec((1, D), lambda i: (0, 0))
    return pl.pallas_call(
        body, name=name, grid=(S // tr,),
        in_specs=[row, row, vec], out_specs=[row, row, vec, pl.BlockSpec((1, LANES), lambda i: (0, 0))],
        out_shape=[_hbm_out((S, D), F32), _hbm_out((S, D), BF16),
                   _hbm_out((1, D), F32), _hbm_out((1, LANES), F32)],
        compiler_params=_params(("arbitrary",)),
    )(h, target, g)


def _swiglu_fwd(name, hn, w_in, tm=512):
    S, D = hn.shape
    FH = w_in.shape[-1]
    tm = _tile(S, tm)

    def body(x_ref, wg_ref, wu_ref, z_ref, a_ref):
        x = x_ref[...]
        zg = jnp.dot(x, wg_ref[...], preferred_element_type=F32)
        zu = jnp.dot(x, wu_ref[...], preferred_element_type=F32)
        sg = _sigmoid_t(zg)
        silu = zg * sg
        z_ref[0] = (zu * (sg * (1.0 + zg * (1.0 - sg)))).astype(z_ref.dtype)
        z_ref[1] = silu.astype(z_ref.dtype)
        a_ref[...] = (silu * zu).astype(a_ref.dtype)

    return pl.pallas_call(
        body, name=name, grid=(2, S // tm),
        in_specs=[pl.BlockSpec((tm, D), lambda j, i: (i, 0)),
                  pl.BlockSpec((None, D, FH), lambda j, i: (j, 0, 0)),
                  pl.BlockSpec((None, D, FH), lambda j, i: (j + 2, 0, 0))],
        out_specs=[pl.BlockSpec((2, tm, FH), lambda j, i: (0, i, j)), pl.BlockSpec((tm, FH), lambda j, i: (i, j))],
        out_shape=[_hbm_out((2, S, 2 * FH), BF16), _hbm_out((S, 2 * FH), BF16)],
        compiler_params=_params(("parallel", "parallel")),
    )(hn, w_in, w_in)


def _swiglu_bwd(name, dhb, w_out, z3, tm=512):
    S, D = dhb.shape
    F = w_out.shape[0]
    FH = F // 2
    tm = _tile(S, tm)

    def body(d_ref, w_ref, z_ref, dz_ref):
        d = lax.dot_general(d_ref[...], w_ref[...], _NT, preferred_element_type=F32)
        dz_ref[0] = (d * z_ref[0].astype(F32)).astype(dz_ref.dtype)
        dz_ref[1] = (d * z_ref[1].astype(F32)).astype(dz_ref.dtype)

    zspec = pl.BlockSpec((2, tm, FH), lambda j, i: (0, i, j))
    return pl.pallas_call(
        body, name=name, grid=(2, S // tm),
        in_specs=[pl.BlockSpec((tm, D), lambda j, i: (i, 0)), pl.BlockSpec((FH, D), lambda j, i: (j, 0)), zspec],
        out_specs=zspec, out_shape=_hbm_out((2, S, F), BF16),
        compiler_params=_params(("parallel", "parallel")),
    )(dhb, w_out, z3)


SCAN_ROWS = 64


def _group_scan(A, B, reverse):
    n = A.shape[0]
    sub = lax.broadcasted_iota(jnp.int32, A.shape, 0) % SUBLANES
    for d in (1, 2, 4):
        if reverse:
            A_sh, B_sh = pltpu.roll(A, n - d, 0), pltpu.roll(B, n - d, 0)
            keep = sub < SUBLANES - d
        else:
            A_sh, B_sh = pltpu.roll(A, d, 0), pltpu.roll(B, d, 0)
            keep = sub >= d
        B = jnp.where(keep, A * B_sh + B, B)
        A = jnp.where(keep, A * A_sh, A)
    return A, B


def _block_scan(a, u, carry, reverse):
    A, B = _group_scan(a, u, reverse)
    ng = a.shape[0] // SUBLANES
    out = [None] * ng
    order = range(ng - 1, -1, -1) if reverse else range(ng)
    for gi in order:
        sl = slice(gi * SUBLANES, (gi + 1) * SUBLANES)
        hg = A[sl] * carry + B[sl]
        out[gi] = hg
        carry = hg[0:1] if reverse else hg[SUBLANES - 1:SUBLANES]
    return jnp.concatenate(out, axis=0), carry


def _lru_gates(rc, gip, grp, sp):
    gi = _sigmoid_t(gip)
    gr = _sigmoid_t(grp)
    la = -LRU_C * gr * sp
    a = jnp.exp(la)
    om = -jnp.tanh(la) * (a * a + 1.0)
    mult = jnp.sqrt(om)
    return gi, gr, a, mult


def _lru_fwd(name, proj, rc, gip, grp, lru_p, tc=256):
    S, C = rc.shape
    tc = _tile(C, tc)
    nb = S // SCAN_ROWS

    def body(gb_ref, rc_ref, gi_ref, gr_ref, l_ref, h_ref, m_ref):
        sp = _softplus(-l_ref[...])

        def step(b, carry):
            rows = pl.ds(pl.multiple_of(b * SCAN_ROWS, SCAN_ROWS), SCAN_ROWS)
            rcb = rc_ref[rows, :]
            gi, _, a, mult = _lru_gates(rcb, gi_ref[rows, :], gr_ref[rows, :], sp)
            h, carry = _block_scan(a, rcb * gi * mult, carry, False)
            h_ref[rows, :] = h
            gel, _ = _gelu_and_grad(gb_ref[rows, :])
            m_ref[rows, :] = (gel * h).astype(m_ref.dtype)
            return carry

        lax.fori_loop(0, nb, step, jnp.zeros((1, tc), F32))

    col = pl.BlockSpec((S, tc), lambda j: (0, j))
    return pl.pallas_call(
        body, name=name, grid=(C // tc,),
        in_specs=[col, col, col, col, pl.BlockSpec((1, tc), lambda j: (0, j))],
        out_specs=[col, col],
        out_shape=[_hbm_out((S, C), F32), _hbm_out((S, C), BF16)],
        compiler_params=_params(("parallel",)),
    )(proj, rc, gip, grp, lru_p)


def _lru_bwd(name, dm, proj, hrec, rc, gip, grp, lru_p, tc=256):
    S, C = rc.shape
    tc = _tile(C, tc)
    nb = S // SCAN_ROWS
    R = SCAN_ROWS

    def body(dm_ref, gb_ref, h_ref, rc_ref, gi_ref, gr_ref, l_ref,
             dgb_ref, dgi_ref, dgr_ref, drc_ref, dbi_ref, dbr_ref, dl_ref):
        lp = l_ref[...]
        sp = _softplus(-lp)
        row = lax.broadcasted_iota(jnp.int32, (R, tc), 0)
        zero = jnp.zeros((1, tc), F32)

        def step(t, carry):
            mu_in, s_i, s_r, s_sp = carry
            b = nb - 1 - t
            r0 = pl.multiple_of(b * R, R)
            rows = pl.ds(r0, R)
            rcb = rc_ref[rows, :]
            gi, gr, a, mult = _lru_gates(rcb, gi_ref[rows, :], gr_ref[rows, :], sp)
            gel, dgel = _gelu_and_grad(gb_ref[rows, :])
            dmb = dm_ref[rows, :]
            h = h_ref[rows, :]
            dgb_ref[rows, :] = (dmb * h * dgel).astype(dgb_ref.dtype)
            dh = dmb * gel
            mu, mu_out = _block_scan(a, a * dh, mu_in, True)
            mu_next = jnp.where(row == R - 1, mu_in, pltpu.roll(mu, R - 1, 0))
            lam = dh + mu_next
            p0 = pl.multiple_of(jnp.maximum(r0 - SUBLANES, 0), SUBLANES)
            prev = h_ref[pl.ds(p0, SUBLANES), :][SUBLANES - 1:SUBLANES]
            prev = jnp.where(b > 0, prev, 0.0)
            h_prev = jnp.where(row == 0, prev, pltpu.roll(h, 1, 0))
            da = lam * h_prev
            d_mult = lam * rcb * gi
            d_la = da * a - d_mult * (a * a) / mult
            d_grp = d_la * (-LRU_C * sp) * gr * (1.0 - gr)
            d_gip = lam * rcb * mult * gi * (1.0 - gi)
            dgr_ref[rows, :] = d_grp.astype(dgr_ref.dtype)
            dgi_ref[rows, :] = d_gip.astype(dgi_ref.dtype)
            drc_ref[rows, :] = lam * gi * mult
            s_i = s_i + jnp.sum(d_gip, axis=0, keepdims=True)
            s_r = s_r + jnp.sum(d_grp, axis=0, keepdims=True)
            s_sp = s_sp + jnp.sum(d_la * gr, axis=0, keepdims=True)
            return mu_out, s_i, s_r, s_sp

        _, s_i, s_r, s_sp = lax.fori_loop(0, nb, step, (zero, zero, zero, zero))
        dbi_ref[...] = s_i
        dbr_ref[...] = s_r
        dl_ref[...] = (-LRU_C * s_sp) * (-_sigmoid(-lp))

    col = pl.BlockSpec((S, tc), lambda j: (0, j))
    vec = pl.BlockSpec((1, tc), lambda j: (0, j))
    return pl.pallas_call(
        body, name=name, grid=(C // tc,),
        in_specs=[col, col, col, col, col, col, vec],
        out_specs=[col, col, col, col, vec, vec, vec],
        out_shape=[_hbm_out((S, C), BF16), _hbm_out((S, C), BF16),
                   _hbm_out((S, C), BF16), _hbm_out((S, C), F32),
                   _hbm_out((1, C), F32), _hbm_out((1, C), F32),
                   _hbm_out((1, C), F32)],
        compiler_params=_params(("parallel",)),
    )(dm, proj, hrec, rc, gip, grp, lru_p)


def _cumsum_rows(name, u, reverse):
    S, C = u.shape
    nb = S // SCAN_ROWS

    def body(u_ref, o_ref):
        def step(t, carry):
            b = nb - 1 - t if reverse else t
            rows = pl.ds(pl.multiple_of(b * SCAN_ROWS, SCAN_ROWS), SCAN_ROWS)
            ub = u_ref[rows, :]
            h, carry = _block_scan(jnp.ones_like(ub), ub, carry, reverse)
            o_ref[rows, :] = h
            return carry

        lax.fori_loop(0, nb, step, jnp.zeros((1, C), F32))

    spec = pl.BlockSpec((S, C), lambda i: (0, 0))
    return pl.pallas_call(
        body, name=name, grid=(1,), in_specs=[spec], out_specs=spec,
        out_shape=_hbm_out((S, C), F32),
        compiler_params=_params(("arbitrary",)),
    )(u)


def _shift_down(x, k):
    row = lax.broadcasted_iota(jnp.int32, x.shape, 0)
    return jnp.where(row >= k, pltpu.roll(x, k, 0), 0.0)


def _shift_up(x, k):
    n = x.shape[0]
    row = lax.broadcasted_iota(jnp.int32, x.shape, 0)
    return jnp.where(row < n - k, pltpu.roll(x, n - k, 0), 0.0)


def _conv_fwd(name, proj, w, b, tc=256):
    S, C2 = proj.shape
    C = C2 // 2
    tc = _tile(C, tc)
    off = C // tc

    def body(x_ref, w_ref, b_ref, o_ref, ob_ref):
        x = x_ref[...]
        out = b_ref[...] + w_ref[3:4, :] * x
        for k in (1, 2, 3):
            out = out + w_ref[3 - k:4 - k, :] * _shift_down(x, k)
        o_ref[...] = out
        ob_ref[...] = out.astype(BF16)

    col = pl.BlockSpec((S, tc), lambda j: (0, j))
    return pl.pallas_call(
        body, name=name, grid=(C // tc,),
        in_specs=[pl.BlockSpec((S, tc), lambda j: (0, off + j)),
                  pl.BlockSpec((4, tc), lambda j: (0, j)), pl.BlockSpec((1, tc), lambda j: (0, j))],
        out_specs=[col, col],
        out_shape=[_hbm_out((S, C), F32), _hbm_out((S, C), BF16)],
        compiler_params=_params(("parallel",)),
    )(proj, w, b)


def _conv_bwd(name, drc, proj, w, tc=256):
    S, C = drc.shape
    tc = _tile(C, tc)
    off = C // tc

    def body(y_ref, x_ref, w_ref, dx_ref, dw_ref, db_ref):
        y = y_ref[...]
        x = x_ref[...]
        dx = w_ref[3:4, :] * y
        dw_ref[3:4, :] = jnp.sum(y * x, axis=0, keepdims=True)
        for k in (1, 2, 3):
            dx = dx + w_ref[3 - k:4 - k, :] * _shift_up(y, k)
            dw_ref[3 - k:4 - k, :] = jnp.sum(y * _shift_down(x, k), axis=0, keepdims=True)
        dx_ref[...] = dx.astype(dx_ref.dtype)
        db_ref[...] = jnp.sum(y, axis=0, keepdims=True)

    col = pl.BlockSpec((S, tc), lambda j: (0, j))
    return pl.pallas_call(
        body, name=name, grid=(C // tc,),
        in_specs=[col, pl.BlockSpec((S, tc), lambda j: (0, off + j)), pl.BlockSpec((4, tc), lambda j: (0, j))],
        out_specs=[col, pl.BlockSpec((4, tc), lambda j: (0, j)), pl.BlockSpec((1, tc), lambda j: (0, j))],
        out_shape=[_hbm_out((S, C), BF16), _hbm_out((4, C), F32),
                   _hbm_out((1, C), F32)],
        compiler_params=_params(("parallel",)),
    )(drc, proj, w)


def _gates_fwd(name, rcb, wg, bg):
    S, C = rcb.shape
    nblk, bw, _ = wg.shape

    def body(x_ref, w_ref, b_ref, gi_ref, gr_ref):
        g = jnp.dot(x_ref[...], w_ref[...], preferred_element_type=F32) + b_ref[...]
        gi_ref[...] = g[:, :bw]
        gr_ref[...] = g[:, bw:]

    col = pl.BlockSpec((S, bw), lambda n: (0, n))
    return pl.pallas_call(
        body, name=name, grid=(nblk,),
        in_specs=[col, pl.BlockSpec((None, bw, 2 * bw), lambda n: (n, 0, 0)),
                  pl.BlockSpec((None, 1, 2 * bw), lambda n: (n, 0, 0))],
        out_specs=[col, col],
        out_shape=[_hbm_out((S, C), F32), _hbm_out((S, C), F32)],
        compiler_params=_params(("parallel",)),
    )(rcb, wg, bg)


def _gates_bwd(name, dgi, dgr, rcb, wg, drc1):
    S, C = rcb.shape
    nblk, bw, _ = wg.shape

    def body(dgi_ref, dgr_ref, x_ref, w_ref, d1_ref, drc_ref, dw_ref):
        w = w_ref[...]
        x = x_ref[...]
        di, dr = dgi_ref[...], dgr_ref[...]
        drc_ref[...] = (d1_ref[...]
                        + lax.dot_general(di, w[:, :bw], _NT, preferred_element_type=F32)
                        + lax.dot_general(dr, w[:, bw:], _NT, preferred_element_type=F32))
        dw_ref[:, :bw] = lax.dot_general(x, di, _TN, preferred_element_type=F32).astype(dw_ref.dtype)
        dw_ref[:, bw:] = lax.dot_general(x, dr, _TN, preferred_element_type=F32).astype(dw_ref.dtype)

    col = pl.BlockSpec((S, bw), lambda n: (0, n))
    wspec = pl.BlockSpec((None, bw, 2 * bw), lambda n: (n, 0, 0))
    return pl.pallas_call(
        body, name=name, grid=(nblk,),
        in_specs=[col, col, col, wspec, col], out_specs=[col, wspec],
        out_shape=[_hbm_out((S, C), F32), _hbm_out((nblk, bw, 2 * bw), BF16)],
        compiler_params=_params(("parallel",)),
    )(dgi, dgr, rcb, wg, drc1)


def _att_tile(S):
    return next(t for t in (512, 256, 128) if S % t == 0)


def _head_lanes(shape):
    return lax.broadcasted_iota(jnp.int32, shape, len(shape) - 1) < HEAD_DIM


def _key_bias(c_ref, rows, hh):
    return jnp.broadcast_to(c_ref[rows, hh:hh + 1], (rows.size, LANES))


def _over_keys(x, op):
    n = x.shape[0]
    while n > SUBLANES:
        n //= 2
        x = op(x[:n], x[n:2 * n])
    return (jnp.max if op is jnp.maximum else jnp.sum)(x, axis=0, keepdims=True)


def _causal_t(T, cc):
    r = lax.broadcasted_iota(jnp.int32, (T, LANES), 0)
    c = lax.broadcasted_iota(jnp.int32, (T, LANES), 1) + cc * LANES
    return r <= c


def _attn_fwd(name, q, kv, c3):
    S, D = q.shape
    HP = D // LANES
    T = _att_tile(S)
    nq = S // T
    NC = T // LANES

    def body(q_ref, k_ref, v_ref, c_ref, o_ref, of_ref, lse_ref, bias, vT, acc, m_scr, l_scr):
        def prologue(i, _):
            rows = pl.ds(pl.multiple_of(i * T, T), T)
            for hh in range(2):
                bias[hh, rows, :] = _key_bias(c_ref, rows, hh)
            vT[i] = v_ref[rows, :].astype(F32).T.astype(BF16)
            return 0

        lax.fori_loop(0, nq, prologue, 0)

        def q_step(qi, _):
            q0 = pl.multiple_of(qi * T, T)
            qb = q_ref[pl.ds(q0, T), :]
            m_scr[...] = jnp.full(m_scr.shape, -jnp.inf, F32)
            l_scr[...] = jnp.zeros(l_scr.shape, F32)
            acc[...] = jnp.zeros(acc.shape, F32)

            def tile(kj, masked):
                ks = pl.ds(pl.multiple_of(kj * T, T), T)
                kf = k_ref[ks, :].astype(F32)
                first = _head_lanes(kf.shape)
                kms = [jnp.where(first if hh == 0 else jnp.logical_not(first), kf, 0.0).astype(BF16) for hh in range(2)]
                sTs = [lax.dot_general(km, qb, _NT, preferred_element_type=F32) for km in kms]
                for hh in range(2):
                    b = bias[hh, ks, :]
                    ps = []
                    for cc in range(NC):
                        cols = slice(cc * LANES, (cc + 1) * LANES)
                        s = sTs[hh][:, cols] + b
                        if masked:
                            s = jnp.where(_causal_t(T, cc), s, -jnp.inf)
                        m_old = m_scr[hh, cc]
                        m_new = jnp.maximum(m_old, _over_keys(s, jnp.maximum))
                        alpha = jnp.exp(m_old - m_new)
                        p = jnp.exp(s - m_new)
                        l_scr[hh, cc] = alpha * l_scr[hh, cc] + _over_keys(p, jnp.add)
                        m_scr[hh, cc] = m_new
                        ps.append(p.astype(BF16))
                        acc[hh, :, cols] = acc[hh, :, cols] * alpha
                    acc[hh] += jnp.dot(vT[kj, hh * HEAD_DIM:(hh + 1) * HEAD_DIM, :], jnp.concatenate(ps, axis=1),
                                       preferred_element_type=F32)

            def inner(kj, _):
                tile(kj, False)
                return 0

            lax.fori_loop(0, qi, inner, 0)
            tile(qi, True)
            outs = []
            for hh in range(2):
                inv = jnp.concatenate([1.0 / l_scr[hh, cc] for cc in range(NC)], axis=1)
                outs.append(acc[hh] * inv)
                for cc in range(NC):
                    lse_ref[hh:hh + 1, pl.ds(q0 + cc * LANES, LANES)] = m_scr[hh, cc] + jnp.log(l_scr[hh, cc])
            out = jnp.concatenate(outs, axis=0).T
            o_ref[pl.ds(q0, T), :] = out.astype(o_ref.dtype)
            of_ref[pl.ds(q0, T), :] = out
            return 0

        lax.fori_loop(0, nq, q_step, 0)

    blk = lambda off: pl.BlockSpec((S, LANES), lambda p: (0, off + p))
    return pl.pallas_call(
        body, name=name, grid=(HP,),
        in_specs=[blk(0), blk(0), blk(HP), pl.BlockSpec((None, S, 2), lambda p: (p, 0, 0))],
        out_specs=[blk(0), blk(0), pl.BlockSpec((None, 2, S), lambda p: (p, 0, 0))],
        out_shape=[_hbm_out((S, D), BF16), _hbm_out((S, D), F32),
                   _hbm_out((HP, 2, S), F32)],
        scratch_shapes=[pltpu.VMEM((2, S, LANES), F32), pltpu.VMEM((nq, LANES, T), BF16),
                        pltpu.VMEM((2, HEAD_DIM, T), F32), pltpu.VMEM((2, NC, 1, LANES), F32),
                        pltpu.VMEM((2, NC, 1, LANES), F32)],
        compiler_params=_params(("parallel",)),
    )(q, kv, kv, c3)


def _attn_bwd(name, q, kv, c3, of, do, lse3):
    S, D = q.shape
    HP = D // LANES
    T = _att_tile(S)
    nq = S // T
    NC = T // LANES
    scale = HEAD_DIM ** -0.5

    def body(q_ref, k_ref, v_ref, c_ref, of_ref, do_ref, lse_ref,
             dq_ref, dk_ref, dv_ref, dck_ref, drq_ref, bias, kT, dqT, delta, dr_scr, dk_acc, dv_acc, dc_acc):
        def prologue(i, _):
            rows = pl.ds(pl.multiple_of(i * T, T), T)
            for hh in range(2):
                bias[hh, rows, :] = _key_bias(c_ref, rows, hh)
            kT[i] = k_ref[rows, :].astype(F32).T.astype(BF16)
            prodT = (do_ref[rows, :].astype(F32) * of_ref[rows, :]).T
            for hh in range(2):
                delta[hh:hh + 1, rows] = jnp.sum(prodT[hh * HEAD_DIM:(hh + 1) * HEAD_DIM], axis=0, keepdims=True)
            dqT[i] = jnp.zeros((LANES, T), F32)
            return 0

        lax.fori_loop(0, nq, prologue, 0)
        dr_scr[...] = jnp.zeros(dr_scr.shape, F32)

        def kv_step(kj, _):
            ks = pl.ds(pl.multiple_of(kj * T, T), T)
            kf = k_ref[ks, :].astype(F32)
            vf = v_ref[ks, :].astype(F32)
            first = _head_lanes(kf.shape)
            masks = [first, jnp.logical_not(first)]
            kms = [jnp.where(m, kf, 0.0).astype(BF16) for m in masks]
            vms = [jnp.where(m, vf, 0.0).astype(BF16) for m in masks]

            for acc in (dk_acc, dv_acc, dc_acc):
                acc[...] = jnp.zeros(acc.shape, F32)

            def tile(qi, masked):
                q0 = pl.multiple_of(qi * T, T)
                qb = q_ref[pl.ds(q0, T), :]
                dob = do_ref[pl.ds(q0, T), :]
                sTs = [lax.dot_general(km, qb, _NT, preferred_element_type=F32) for km in kms]
                dpTs = [lax.dot_general(vm, dob, _NT, preferred_element_type=F32) for vm in vms]
                for hh in range(2):
                    b = bias[hh, ks, :]
                    head = slice(hh * HEAD_DIM, (hh + 1) * HEAD_DIM)
                    ps, dss = [], []
                    for cc in range(NC):
                        cols = slice(cc * LANES, (cc + 1) * LANES)
                        at = pl.ds(q0 + cc * LANES, LANES)
                        p = jnp.exp(sTs[hh][:, cols] + b - lse_ref[hh:hh + 1, at])
                        if masked:
                            p = jnp.where(_causal_t(T, cc), p, 0.0)
                        ds = p * (dpTs[hh][:, cols] - delta[hh:hh + 1, at])
                        ps.append(p.astype(BF16))
                        dss.append(ds.astype(BF16))
                        dc_acc[hh] += ds
                        dr_scr[hh:hh + 1, at] += _over_keys(ds, jnp.add)
                    pT = jnp.concatenate(ps, axis=1)
                    dsT = jnp.concatenate(dss, axis=1)
                    dv_acc[hh] += jnp.dot(pT, dob, preferred_element_type=F32)
                    dk_acc[hh] += jnp.dot(dsT, qb, preferred_element_type=F32)
                    dqT[qi, head, :] += jnp.dot(kT[kj, head, :], dsT, preferred_element_type=F32)

            def inner(qi, _):
                tile(qi, False)
                return 0

            tile(kj, True)
            lax.fori_loop(kj + 1, nq, inner, 0)
            dk_ref[ks, :] = jnp.where(first, dk_acc[0], dk_acc[1])
            dv_ref[ks, :] = jnp.where(first, dv_acc[0], dv_acc[1])
            for hh in range(2):
                dck_ref[hh:hh + 1, ks] = -jnp.sum(dc_acc[hh].T, axis=0, keepdims=True)
            return 0

        lax.fori_loop(0, nq, kv_step, 0)

        def epilogue(i, _):
            rows = pl.ds(pl.multiple_of(i * T, T), T)
            dq_ref[rows, :] = (dqT[i].T * scale).astype(dq_ref.dtype)
            return 0

        lax.fori_loop(0, nq, epilogue, 0)
        drq_ref[...] = dr_scr[...]

    blk = lambda off: pl.BlockSpec((S, LANES), lambda p: (0, off + p))
    row_spec = pl.BlockSpec((None, 2, S), lambda p: (p, 0, 0))
    return pl.pallas_call(
        body, name=name, grid=(HP,),
        in_specs=[blk(0), blk(0), blk(HP), pl.BlockSpec((None, S, 2), lambda p: (p, 0, 0)), blk(0), blk(0), row_spec],
        out_specs=[blk(0), blk(0), blk(0), row_spec, row_spec],
        out_shape=[_hbm_out((S, D), BF16), _hbm_out((S, D), F32),
                   _hbm_out((S, D), F32), _hbm_out((HP, 2, S), F32),
                   _hbm_out((HP, 2, S), F32)],
        scratch_shapes=[pltpu.VMEM((2, S, LANES), F32), pltpu.VMEM((nq, LANES, T), BF16),
                        pltpu.VMEM((nq, LANES, T), F32), pltpu.VMEM((2, S), F32), pltpu.VMEM((2, S), F32)]
        + [pltpu.VMEM((2, T, LANES), F32)] * 3,
        compiler_params=_params(("parallel",)),
    )(q, kv, kv, c3, of, do, lse3)


def _logsig_fwd(name, f):
    S, C = f.shape

    def body(f_ref, o_ref):
        o_ref[...] = -_softplus(-f_ref[...])

    spec = pl.BlockSpec((S, C), lambda i: (0, 0))
    return pl.pallas_call(body, name=name, grid=(1,), in_specs=[spec], out_specs=spec,
                          out_shape=_hbm_out((S, C), F32),
                          compiler_params=_params(("arbitrary",)))(f)


def _logsig_bwd(name, dls, f):
    S, C = f.shape

    def body(d_ref, f_ref, o_ref, s_ref):
        df = d_ref[...] * _sigmoid(-f_ref[...])
        o_ref[...] = df.astype(o_ref.dtype)
        s_ref[...] = jnp.sum(df, axis=0, keepdims=True)

    spec = pl.BlockSpec((S, C), lambda i: (0, 0))
    return pl.pallas_call(body, name=name, grid=(1,), in_specs=[spec, spec],
                          out_specs=[spec, pl.BlockSpec((1, C), lambda i: (0, 0))],
                          out_shape=[_hbm_out((S, C), BF16), _hbm_out((1, C), F32)],
                          compiler_params=_params(("arbitrary",)))(dls, f)


def _add_cast(name, parts, out_dtype, tr=256):
    S, C = parts[0].shape
    tr = _tile(S, tr)
    n = len(parts)

    def body(*refs):
        acc = refs[0][...].astype(F32)
        for r in refs[1:n]:
            acc = acc + r[...].astype(F32)
        refs[n][...] = acc.astype(out_dtype)

    spec = pl.BlockSpec((tr, C), lambda i: (i, 0))
    return pl.pallas_call(body, name=name, grid=(S // tr,), in_specs=[spec] * n, out_specs=spec,
                          out_shape=_hbm_out((S, C), out_dtype),
                          compiler_params=_params(("parallel",)))(*parts)


def _local_step(x, target, gains, layer_weights, layer_prefetch, layer_grads):
    S, D = x.shape
    HP = D // LANES
    scale = HEAD_DIM ** -0.5
    tx = _tile(S, 256)
    saved = []
    h = x
    l = 0
    kv = c3 = f_pre = hn_kv = h_kv = None
    while True:
        W = layer_weights(l, "mix", h)
        if W is None:
            break
        recurrent = "w_rec_in" in W
        if l == 0:
            xn = _rmsnorm_fwd("mix_norm_0", h, gains["mix"][0])
        if recurrent:
            CH = W["w_rec_in"].shape[-1]
            C = 2 * CH
            proj = _mm(f"rec_in_{l}", "nn", xn, W["w_rec_in"], grid=(1, N_CHIPS),
                       a_spec=pl.BlockSpec((S, D), lambda i, j: (0, 0)),
                       b_spec=pl.BlockSpec((None, D, CH), lambda i, j: (j, 0, 0)),
                       out_shape=(S, 2 * C), out_dtype=F32,
                       out_spec=pl.BlockSpec((S, CH), lambda i, j: (0, j)))
            layer_prefetch(l, "mix2", proj)
            rc, rcb = _conv_fwd(f"conv_{l}", proj, W["conv_w"], W["conv_b"])
            W = {**W, **layer_weights(l, "mix2", rcb)}
            gip, grp = _gates_fwd(f"gates_{l}", rcb, W["w_gates"], W["b_gates"])
            hrec, m = _lru_fwd(f"lru_{l}", proj, rc, gip, grp, W["lru_param"])
            layer_prefetch(l, "ffn", m)
            h_mid, hn = _mm_nn(f"rec_out_{l}", m, W["w_rec_out"], out_dtype=F32, res=h, tn=D, norm_gain=gains["ffn"][l])
            mix_saved = (xn, proj, rc, rcb, gip, grp, hrec, m)
        else:
            if "w_kv" in W:
                h_kv = h
                hn_kv = _rmsnorm_fwd("kv_norm", h, W["norm_kv"])
                kv = _mm_nn("kv_proj", hn_kv, W["w_kv"], out_dtype=BF16, tm=1024, tn=1024)
                f_pre = _mm_nn("f_proj", hn_kv, W["w_f"], out_dtype=F32, bias=W["b_f"])
                c = _cumsum_rows("c_cumsum", _logsig_fwd("logsig", f_pre), False)
                c3 = (-c[:, :2 * HP]).reshape(S, HP, 2).transpose(1, 0, 2)
            q = _mm_nn(f"q_proj_{l}", xn, W["w_q"], out_dtype=BF16, scale=scale, tm=1024, tn=1024)
            layer_prefetch(l, "mix2", q)
            o, of, lse = _attn_fwd(f"attn_fwd_{l}", q, kv, c3)
            W = {**W, **layer_weights(l, "mix2", o)}
            layer_prefetch(l, "ffn", o)
            h_mid, hn = _mm_nn(f"o_proj_{l}", o, W["w_o"], out_dtype=F32, res=h, tn=D, norm_gain=gains["ffn"][l])
            mix_saved = (xn, q, o, of, lse)
        W = {**W, **layer_weights(l, "ffn", h_mid)}
        z3, act = _swiglu_fwd(f"ffn_in_{l}", hn, W["w_ffn_in"])
        layer_prefetch(l + 1, "mix", act)
        saved.append((W, h, h_mid, mix_saved, (hn, z3, act)))
        l += 1
        if l < len(gains["mix"]):
            h, xn = _mm_nn(f"ffn_out_{l - 1}", act, W["w_ffn_out"], out_dtype=F32, res=h_mid, tn=D,
                           norm_gain=gains["mix"][l])
        else:
            h = _mm_nn(f"ffn_out_{l - 1}", act, W["w_ffn_out"], out_dtype=F32, res=h_mid, tn=D)

    dh, dhb, dg_final, loss_row = _loss_head("loss_head", h, target, gains["final"])

    dk_parts, dv_parts, dc_parts = [], [], []
    token = None
    for l in reversed(range(len(saved))):
        W, h_in, h_mid, mix_saved, (hn, z3, act) = saved[l]
        recurrent = "w_rec_in" in W
        FH = W["w_ffn_in"].shape[-1]
        G = {}
        norm_ffn = gains["ffn"][l]
        if token is not None:
            norm_ffn = norm_ffn + jnp.minimum(token[:1, :1], 0.0)
        G["w_ffn_out"] = _mm_tn(f"d_ffn_out_{l}", act, dhb, out_dtype=BF16, tn=D)
        dz3 = _swiglu_bwd(f"d_act_{l}", dhb, W["w_ffn_out"], z3)
        G["w_ffn_in"] = _mm(
            f"d_ffn_in_{l}", "tn", hn, dz3, grid=(1, N_CHIPS),
            a_spec=pl.BlockSpec((S, D), lambda i, j: (0, 0)),
            b_spec=pl.BlockSpec((None, S, FH), lambda i, j: (j // 2, 0, j % 2)),
            out_shape=(N_CHIPS, D, FH), out_dtype=BF16,
            out_spec=pl.BlockSpec((None, D, FH), lambda i, j: (j, 0, 0)))
        ffn_token = layer_grads(l, "ffn", G)
        G = {}
        if ffn_token is not None:
            norm_ffn = norm_ffn + jnp.minimum(ffn_token[:1, :1], 0.0)
        dh, dhb, dgp = _mm(f"d_ffn_hn_{l}", "nt", dz3, W["w_ffn_in"], grid=(S // tx, 1),
                           a_spec=[pl.BlockSpec((None, tx, FH), functools.partial(lambda i, j, k: (k // 2, i, k % 2), k=k))
                                   for k in range(N_CHIPS)],
                           b_spec=[pl.BlockSpec((None, D, FH), functools.partial(lambda i, j, k: (k, 0, 0), k=k))
                                   for k in range(N_CHIPS)],
                           out_shape=(S, D), out_dtype=F32, out_spec=pl.BlockSpec((tx, D), lambda i, j: (i, 0)),
                           norm_bwd=(h_mid, norm_ffn, dh))
        G["norm_ffn"] = jnp.sum(dgp, axis=0)
        if recurrent:
            CH = W["w_rec_in"].shape[-1]
            C = 2 * CH
            xn, proj, rc, rcb, gip, grp, hrec, m = mix_saved
            G["w_rec_out"] = _mm_tn(f"d_rec_out_{l}", m, dhb, out_dtype=BF16, tn=D)
            dm = _mm_nt(f"d_m_{l}", dhb, W["w_rec_out"], out_dtype=F32, tn=C)
            dgb, dgi, dgr, drc1, G["b_gi"], G["b_gr"], G["lru_param"] = _lru_bwd(
                f"d_lru_{l}", dm, proj, hrec, rc, gip, grp, W["lru_param"])
            drc, G["w_gates"] = _gates_bwd(f"d_gates_{l}", dgi, dgr, rcb, W["w_gates"], drc1)
            mix_token = layer_grads(l, "mix2", {n: G[n] for n in ("w_rec_out", "w_gates")})
            drec, G["conv_w"], G["conv_b"] = _conv_bwd(f"d_conv_{l}", drc, proj, W["conv_w"])
            dproj = jnp.concatenate([dgb, drec], axis=1)
            norm_mix = gains["mix"][l] if mix_token is None else gains["mix"][l] + jnp.minimum(mix_token[:1, :1], 0.0)
            G["w_rec_in"] = _mm(
                f"d_rec_in_{l}", "tn", xn, dproj, grid=(1, N_CHIPS),
                a_spec=pl.BlockSpec((S, D), lambda i, j: (0, 0)),
                b_spec=pl.BlockSpec((S, CH), lambda i, j: (0, j)),
                out_shape=(N_CHIPS, D, CH), out_dtype=BF16,
                out_spec=pl.BlockSpec((None, D, CH), lambda i, j: (j, 0, 0)))
            dh, dhb, dgp = _mm(f"d_rec_xn_{l}", "nt", dproj, W["w_rec_in"], grid=(S // tx, 1),
                               a_spec=[pl.BlockSpec((tx, CH), functools.partial(lambda i, j, k: (i, k), k=k))
                                       for k in range(N_CHIPS)],
                               b_spec=[pl.BlockSpec((None, D, CH), functools.partial(lambda i, j, k: (k, 0, 0), k=k))
                                       for k in range(N_CHIPS)],
                               out_shape=(S, D), out_dtype=F32, out_spec=pl.BlockSpec((tx, D), lambda i, j: (i, 0)),
                               norm_bwd=(h_in, norm_mix, dh))
        else:
            xn, q, o, of, lse = mix_saved
            G["w_o"] = _mm_tn(f"d_o_proj_{l}", o, dhb, out_dtype=BF16, tn=D)
            do = _mm_nt(f"d_o_{l}", dhb, W["w_o"], out_dtype=BF16, tm=1024, tn=D)
            mix_token = layer_grads(l, "mix2", {"w_o": G["w_o"]})
            dq, dk, dv, dck, drq = _attn_bwd(f"attn_bwd_{l}", q, kv, c3, of, do, lse)
            dk_parts.append(dk)
            dv_parts.append(dv)
            dc_parts.append((dck + drq).reshape(2 * HP, S).T)
            G["w_q"] = _mm_tn(f"d_q_proj_{l}", xn, dq, out_dtype=BF16, tn=D)
            norm_mix = gains["mix"][l] if mix_token is None else gains["mix"][l] + jnp.minimum(mix_token[:1, :1], 0.0)
            dh, dhb, dgp = _mm_nt(f"d_q_xn_{l}", dq, W["w_q"], out_dtype=F32, tn=D, norm_bwd=(h_in, norm_mix, dh))
        G["norm_mix"] = jnp.sum(dgp, axis=0)
        if "w_kv" in W:
            dkb = _add_cast("dk_sum", dk_parts, BF16)
            dvb = _add_cast("dv_sum", dv_parts, BF16)
            dkv = jnp.concatenate([dkb, dvb], axis=1)
            dc = sum(dc_parts[1:], dc_parts[0])
            dc_pad = jnp.pad(dc, ((0, 0), (0, LANES - 2 * HP)))
            dls = _cumsum_rows("dc_cumsum", dc_pad, True)
            dfb, G["b_f"] = _logsig_bwd("d_logsig", dls, f_pre)
            G["w_kv"] = _mm_tn("d_kv_proj", hn_kv, dkv, out_dtype=BF16, tn=1024)
            G["w_f"] = _mm_tn("d_f_proj", hn_kv, dfb, out_dtype=F32)
            dhn_f = _mm_nt("d_f_hn", dfb, W["w_f"], out_dtype=F32, tn=D)
            dh, dhb, dgp = _mm_nt("d_kv_hn", dkv, W["w_kv"], out_dtype=F32, tn=D, res=dhn_f,
                                  norm_bwd=(h_kv, W["norm_kv"], dh))
            G["norm_kv"] = jnp.sum(dgp, axis=0)
        token = layer_grads(l, "mix", G)
    return loss_row, dh, dg_final


_ANY = pl.BlockSpec(memory_space=pl.ANY)


def _position():
    return lax.axis_index("x"), lax.axis_index("y"), lax.axis_index("c")


def _chip_peers(x, y):
    return [(1 - x, y), (x, 1 - y), (1 - x, 1 - y)]


def _half_rows(c, n):
    h = n // 2
    assert h % 16 == 0
    return pl.ds(pl.multiple_of(c * h, 16), h)


def _place_own(name, shard, layer, me):
    _, R, C = shard.shape
    tr = _row_tile(R, C, 2 * shard.dtype.itemsize, target=8 << 20)

    def body(me_ref, x_ref, o_ref):
        o_ref[...] = x_ref[...]

    return pl.pallas_call(
        body, name=name,
        grid_spec=pltpu.PrefetchScalarGridSpec(
            num_scalar_prefetch=1, grid=(R // tr,),
            in_specs=[pl.BlockSpec((None, tr, C), lambda i, me_ref: (layer, i, 0))],
            out_specs=pl.BlockSpec((None, tr, C), lambda i, me_ref: (me_ref[0], i, 0))),
        out_shape=_hbm_out((N_CHIPS, R, C), shard.dtype),
        compiler_params=_params(("parallel",)),
    )(me, shard)


def _gather_smalls(name, smalls):
    ns = len(smalls)

    def body(*refs):
        ins, outs = refs[:ns], refs[ns:2 * ns]
        send_sems, recv_sems, local_sems = refs[2 * ns:]
        x, y, c = _position()
        me = 2 * x + y
        peers = _chip_peers(x, y)

        def remote(t, k, chip):
            px, py = peers[k]
            return pltpu.make_async_remote_copy(
                src_ref=ins[t], dst_ref=outs[t].at[chip], send_sem=send_sems.at[3 * t + k],
                recv_sem=recv_sems.at[3 * t + k], device_id=(px, py, c), device_id_type=MESH)

        local = [pltpu.make_async_copy(ins[t], outs[t].at[me], local_sems.at[t]) for t in range(ns)]
        for t in range(ns):
            local[t].start()
            for k in range(3):
                remote(t, k, me).start()
        for t in range(ns):
            for k in range(3):
                px, py = peers[k]
                remote(t, k, 2 * px + py).wait_recv()
        for t in range(ns):
            for k in range(3):
                remote(t, k, me).wait_send()
            local[t].wait()

    return pl.pallas_call(
        body, name=name, in_specs=[_ANY] * ns, out_specs=[_ANY] * ns,
        out_shape=[_hbm_out((N_CHIPS,) + s.shape, s.dtype) for s in smalls],
        scratch_shapes=[pltpu.SemaphoreType.DMA((3 * ns,)), pltpu.SemaphoreType.DMA((3 * ns,)),
                        pltpu.SemaphoreType.DMA((ns,))],
    )(*smalls)


_SEM = pl.BlockSpec(memory_space=pltpu.SEMAPHORE)
_SPLIT = pltpu.CompilerParams(has_side_effects=pltpu.SideEffectType.DATAFLOW_SIDE_EFFECTING)


def _weight_copy(shards, buf, items, sems, i, k, chip_of_dst, peers, c):
    w, l = items[i]
    px, py = peers[k]
    half = _half_rows(c, shards[w].shape[1])
    return pltpu.make_async_remote_copy(
        src_ref=shards[w].at[l, half], dst_ref=buf.at[chip_of_dst, half],
        send_sem=sems[0].at[3 * i + k], recv_sem=sems[1].at[3 * i + k],
        device_id=(px, py, c), device_id_type=MESH)


def _gather_start(name, shards, bufs, items, after):
    nw, n = len(shards), len(bufs)

    def body(*refs):
        ins, outs, sems = refs[:nw], refs[nw + n + 1:nw + 2 * n + 1], refs[nw + 2 * n + 1:]
        x, y, c = _position()
        peers = _chip_peers(x, y)
        for i in range(n):
            for k in range(3):
                _weight_copy(ins, outs[i], items, sems, i, k, 2 * x + y, peers, c).start()

    res = pl.pallas_call(
        body, name=name, in_specs=[_ANY] * (nw + n + 1), out_specs=[_ANY] * n + [_SEM, _SEM],
        out_shape=[_hbm_out(b.shape, b.dtype) for b in bufs]
        + [pltpu.SemaphoreType.DMA((3 * n,)), pltpu.SemaphoreType.DMA((3 * n,))],
        input_output_aliases={nw + i: i for i in range(n)}, compiler_params=_SPLIT,
    )(*shards, *bufs, after)
    return res[:n], res[n:]


def _gather_wait(name, shards, bufs, items, ids, sems, after):
    nw, m = len(shards), len(ids)

    def body(*refs):
        ins, bs = refs[:nw], refs[nw:nw + m]
        sem_refs = refs[nw + m:nw + m + 2]
        x, y, c = _position()
        peers = _chip_peers(x, y)
        for j, i in enumerate(ids):
            for k in range(3):
                px, py = peers[k]
                _weight_copy(ins, bs[j], items, sem_refs, i, k, 2 * px + py, peers, c).wait_recv()
        for j, i in enumerate(ids):
            for k in range(3):
                _weight_copy(ins, bs[j], items, sem_refs, i, k, 2 * x + y, peers, c).wait_send()

    res = pl.pallas_call(
        body, name=name, in_specs=[_ANY] * (nw + m) + [_SEM, _SEM, _ANY], out_specs=[_ANY] * m,
        out_shape=[_hbm_out(bufs[i].shape, bufs[i].dtype) for i in ids],
        input_output_aliases={nw + j: j for j in range(m)}, compiler_params=_SPLIT,
    )(*shards, *[bufs[i] for i in ids], *sems, after)
    return list(res)


def _forward_copy(src, dst, sems, i, k, core):
    x, y, c = _position()
    px, py = _chip_peers(x, y)[k]
    half = _half_rows(core, src.shape[1])
    return pltpu.make_async_remote_copy(
        src_ref=src.at[2 * px + py, half], dst_ref=dst.at[2 * px + py, half],
        send_sem=sems[0].at[3 * i + k], recv_sem=sems[1].at[3 * i + k],
        device_id=(x, y, 1 - c), device_id_type=MESH)


def _forward_start(name, bufs):
    n = len(bufs)

    def body(*refs):
        ins, outs, sems = refs[:n], refs[n:2 * n], refs[2 * n:]
        c = lax.axis_index("c")
        for i in range(n):
            for k in range(3):
                _forward_copy(ins[i], outs[i], sems, i, k, c).start()

    res = pl.pallas_call(
        body, name=name, in_specs=[_ANY] * n, out_specs=[_ANY] * n + [_SEM, _SEM],
        out_shape=[_hbm_out(g.shape, g.dtype) for g in bufs]
        + [pltpu.SemaphoreType.DMA((3 * n,)), pltpu.SemaphoreType.DMA((3 * n,))],
        input_output_aliases={i: i for i in range(n)}, compiler_params=_SPLIT,
    )(*bufs)
    return list(res[:n]), res[n:]


def _forward_wait(name, bufs, sems, after):
    n = len(bufs)

    def body(*refs):
        bs, sem_refs = refs[:n], refs[n:n + 2]
        c = lax.axis_index("c")
        for i in range(n):
            for k in range(3):
                _forward_copy(bs[i], bs[i], sem_refs, i, k, 1 - c).wait_recv()
        for i in range(n):
            for k in range(3):
                _forward_copy(bs[i], bs[i], sem_refs, i, k, c).wait_send()

    return list(pl.pallas_call(
        body, name=name, in_specs=[_ANY] * n + [_SEM, _SEM, _ANY], out_specs=[_ANY] * n,
        out_shape=[_hbm_out(g.shape, g.dtype) for g in bufs],
        input_output_aliases={i: i for i in range(n)}, compiler_params=_SPLIT,
    )(*bufs, *sems, after))


def _reduce_copy(grads, others, sems, i):
    x, y, c = _position()
    return pltpu.make_async_remote_copy(
        src_ref=grads[i].at[:, _half_rows(1 - c, grads[i].shape[1])], dst_ref=others[i],
        send_sem=sems[0].at[i], recv_sem=sems[1].at[i], device_id=(x, y, 1 - c), device_id_type=MESH)


def _reduce_start(name, grads, after):
    n = len(grads)

    def body(*refs):
        ins, outs, sems, token = refs[:n], refs[n + 1:2 * n + 1], refs[2 * n + 1:2 * n + 3], refs[2 * n + 3]
        for i in range(n):
            _reduce_copy(ins, outs, sems, i).start()
        token[...] = jnp.zeros_like(token)

    res = pl.pallas_call(
        body, name=name, in_specs=[_ANY] * (n + 1),
        out_specs=[_ANY] * n + [_SEM, _SEM, pl.BlockSpec(memory_space=pltpu.VMEM)],
        out_shape=[_hbm_out((N_CHIPS, g.shape[1] // 2, g.shape[2]), g.dtype) for g in grads]
        + [pltpu.SemaphoreType.DMA((n,)), pltpu.SemaphoreType.DMA((n,)), jax.ShapeDtypeStruct((SUBLANES, LANES), F32)],
        compiler_params=_SPLIT,
    )(*grads, after)
    return list(res[:n]), res[n:n + 2], res[n + 2]


def _reduce_wait(name, grads, others, sems, after):
    n = len(grads)

    def body(*refs):
        ins, os_, sem_refs = refs[:n], refs[n:2 * n], refs[2 * n:2 * n + 2]
        for i in range(n):
            _reduce_copy(ins, os_, sem_refs, i).wait_recv()
        for i in range(n):
            _reduce_copy(ins, os_, sem_refs, i).wait_send()

    return list(pl.pallas_call(
        body, name=name, in_specs=[_ANY] * (2 * n) + [_SEM, _SEM, _ANY], out_specs=[_ANY] * n,
        out_shape=[_hbm_out(o.shape, o.dtype) for o in others],
        input_output_aliases={n + i: i for i in range(n)}, compiler_params=_SPLIT,
    )(*grads, *others, *sems, after))


def _sum_cores(name, g, other, core):
    _, R, C = g.shape
    H = R // 2
    tr = _row_tile(H, C, 3 * 2, target=12 << 20)
    nb = H // tr

    def body(c_ref, g_ref, o_ref, out_ref):
        out_ref[...] = (g_ref[...].astype(F32) + o_ref[...].astype(F32)).astype(out_ref.dtype)

    return pl.pallas_call(
        body, name=name,
        grid_spec=pltpu.PrefetchScalarGridSpec(
            num_scalar_prefetch=1, grid=(N_CHIPS, nb),
            in_specs=[pl.BlockSpec((None, tr, C), lambda j, i, c_ref: (j, c_ref[0] * nb + i, 0)),
                      pl.BlockSpec((None, tr, C), lambda j, i, c_ref: (j, i, 0))],
            out_specs=pl.BlockSpec((None, tr, C), lambda j, i, c_ref: (j, i, 0))),
        out_shape=_hbm_out((N_CHIPS, H, C), BF16),
        compiler_params=_params(("parallel", "parallel")),
    )(core, g, other)


def _sum_chips(name, received, own, full, layer, me_core):
    _, H, C = received.shape
    tr = _row_tile(H, C, 3 * 2 + 2 + 4, target=12 << 20)
    nb = H // tr

    def body(s_ref, r_ref, own_ref, full_ref, out_ref):
        acc = r_ref[0].astype(F32)
        for k in (1, 2):
            acc = acc + r_ref[k].astype(F32)
        out_ref[...] = acc + own_ref[...].astype(F32)

    return pl.pallas_call(
        body, name=name,
        grid_spec=pltpu.PrefetchScalarGridSpec(
            num_scalar_prefetch=1, grid=(nb,),
            in_specs=[pl.BlockSpec((3, tr, C), lambda i, s_ref: (0, i, 0)),
                      pl.BlockSpec((None, tr, C), lambda i, s_ref: (s_ref[0], i, 0)),
                      _ANY],
            out_specs=pl.BlockSpec((None, tr, C), lambda i, s_ref: (layer, s_ref[1] * nb + i, 0))),
        out_shape=_hbm_out(full.shape, full.dtype),
        input_output_aliases={3: 0},
        compiler_params=_params(("parallel",)),
    )(me_core, received, own, full)


def _part_copy(parts, recv, sems, i, k, peers, c):
    px, py = peers[k]
    return pltpu.make_async_remote_copy(
        src_ref=parts[i].at[2 * px + py], dst_ref=recv[i].at[k],
        send_sem=sems[0].at[3 * i + k], recv_sem=sems[1].at[3 * i + k],
        device_id=(px, py, c), device_id_type=MESH)


def _scatter_start(name, parts):
    n = len(parts)

    def body(*refs):
        ins, outs, sems, token = refs[:n], refs[n:2 * n], refs[2 * n:2 * n + 2], refs[2 * n + 2]
        x, y, c = _position()
        peers = _chip_peers(x, y)
        for i in range(n):
            for k in range(3):
                _part_copy(ins, outs, sems, i, k, peers, c).start()
        token[...] = jnp.zeros_like(token)

    res = pl.pallas_call(
        body, name=name, in_specs=[_ANY] * n,
        out_specs=[_ANY] * n + [_SEM, _SEM, pl.BlockSpec(memory_space=pltpu.VMEM)],
        out_shape=[_hbm_out((3,) + p.shape[1:], p.dtype) for p in parts]
        + [pltpu.SemaphoreType.DMA((3 * n,)), pltpu.SemaphoreType.DMA((3 * n,)),
           jax.ShapeDtypeStruct((SUBLANES, LANES), F32)],
        compiler_params=_SPLIT,
    )(*parts)
    return list(res[:n]), res[n:n + 2], res[n + 2]


def _scatter_reduce_start(name, parts, grads):
    n, m = len(parts), len(grads)

    def body(*refs):
        ps, gs = refs[:n], refs[n:n + m]
        recv, others = refs[n + m:2 * n + m], refs[2 * n + m:2 * (n + m)]
        ssems, rsems, token = refs[2 * (n + m):2 * (n + m) + 2], refs[2 * (n + m) + 2:2 * (n + m) + 4], refs[-1]
        x, y, c = _position()
        peers = _chip_peers(x, y)
        for i in range(n):
            for k in range(3):
                _part_copy(ps, recv, ssems, i, k, peers, c).start()
        for i in range(m):
            _reduce_copy(gs, others, rsems, i).start()
        token[...] = jnp.zeros_like(token)

    res = pl.pallas_call(
        body, name=name, in_specs=[_ANY] * (n + m),
        out_specs=[_ANY] * (n + m) + [_SEM] * 4 + [pl.BlockSpec(memory_space=pltpu.VMEM)],
        out_shape=[_hbm_out((3,) + p.shape[1:], p.dtype) for p in parts]
        + [_hbm_out((N_CHIPS, g.shape[1] // 2, g.shape[2]), g.dtype) for g in grads]
        + [pltpu.SemaphoreType.DMA((3 * n,)), pltpu.SemaphoreType.DMA((3 * n,)),
           pltpu.SemaphoreType.DMA((m,)), pltpu.SemaphoreType.DMA((m,)), jax.ShapeDtypeStruct((SUBLANES, LANES), F32)],
        compiler_params=_SPLIT,
    )(*parts, *grads)
    k = n + m
    return (list(res[:n]), res[k:k + 2]), (list(res[n:k]), res[k + 2:k + 4]), res[k + 4]


def _scatter_wait(name, parts, recv, sems, after):
    n = len(parts)

    def body(*refs):
        ins, rs, sem_refs = refs[:n], refs[n:2 * n], refs[2 * n:2 * n + 2]
        x, y, c = _position()
        peers = _chip_peers(x, y)
        for i in range(n):
            for k in range(3):
                _part_copy(ins, rs, sem_refs, i, k, peers, c).wait_recv()
        for i in range(n):
            for k in range(3):
                _part_copy(ins, rs, sem_refs, i, k, peers, c).wait_send()

    return list(pl.pallas_call(
        body, name=name, in_specs=[_ANY] * (2 * n) + [_SEM, _SEM, _ANY], out_specs=[_ANY] * n,
        out_shape=[_hbm_out(r.shape, r.dtype) for r in recv],
        input_output_aliases={n + i: i for i in range(n)}, compiler_params=_SPLIT,
    )(*parts, *recv, *sems, after))


def _share_d2d(name, full):
    n = len(full)

    def body(*refs):
        ins, outs = refs[:n], refs[n:2 * n]
        send_sems, recv_sems = refs[2 * n:]
        x, y, c = _position()

        def remote(w, core):
            half = _half_rows(core, ins[w].shape[1])
            return pltpu.make_async_remote_copy(
                src_ref=ins[w].at[:, half], dst_ref=outs[w].at[:, half],
                send_sem=send_sems.at[w], recv_sem=recv_sems.at[w],
                device_id=(x, y, 1 - c), device_id_type=MESH)

        for w in range(n):
            remote(w, c).start()
        for w in range(n):
            remote(w, 1 - c).wait_recv()
        for w in range(n):
            remote(w, c).wait_send()

    return pl.pallas_call(
        body, name=name, in_specs=[_ANY] * n, out_specs=[_ANY] * n,
        out_shape=[_hbm_out(f.shape, f.dtype) for f in full],
        input_output_aliases={w: w for w in range(n)},
        scratch_shapes=[pltpu.SemaphoreType.DMA((n,)), pltpu.SemaphoreType.DMA((n,))],
    )(*full)


def _all_copy(a_ref, o_ref, sems, k, slot):
    x, y, c = _position()
    return pltpu.make_async_remote_copy(
        src_ref=a_ref, dst_ref=o_ref.at[slot], send_sem=sems[0].at[k - 1], recv_sem=sems[1].at[k - 1],
        device_id=(x ^ ((k >> 2) & 1), y ^ ((k >> 1) & 1), c ^ (k & 1)), device_id_type=MESH)


def _gather_all_start(name, a):
    def body(a_ref, o_ref, send_sem, recv_sem, token):
        x, y, c = _position()
        for k in range(1, N_DEV):
            _all_copy(a_ref, o_ref, (send_sem, recv_sem), k, 4 * x + 2 * y + c).start()
        token[...] = jnp.zeros_like(token)

    out, send_sem, recv_sem, token = pl.pallas_call(
        body, name=name, in_specs=[_ANY], out_specs=[_ANY, _SEM, _SEM, pl.BlockSpec(memory_space=pltpu.VMEM)],
        out_shape=[_hbm_out((N_DEV,) + a.shape, a.dtype), pltpu.SemaphoreType.DMA((N_DEV - 1,)),
                   pltpu.SemaphoreType.DMA((N_DEV - 1,)), jax.ShapeDtypeStruct((SUBLANES, LANES), F32)],
        compiler_params=_SPLIT,
    )(a)
    return out, (send_sem, recv_sem), token


def _gather_all_wait(name, a, out, sems, after):
    def body(a_ref, o_ref, send_sem, recv_sem, after_ref, res_ref):
        x, y, c = _position()
        for k in range(1, N_DEV):
            peer = 4 * (x ^ ((k >> 2) & 1)) + 2 * (y ^ ((k >> 1) & 1)) + (c ^ (k & 1))
            _all_copy(a_ref, o_ref, (send_sem, recv_sem), k, peer).wait_recv()
        for k in range(1, N_DEV):
            _all_copy(a_ref, o_ref, (send_sem, recv_sem), k, 4 * x + 2 * y + c).wait_send()

    return pl.pallas_call(
        body, name=name, in_specs=[_ANY, _ANY, _SEM, _SEM, _ANY], out_specs=_ANY,
        out_shape=_hbm_out(out.shape, out.dtype), input_output_aliases={1: 0}, compiler_params=_SPLIT,
    )(a, out, *sems, after)


def _rows2d(a, lead=0):
    return a.reshape(a.shape[:lead] + (-1, a.shape[-1]))


def _row_tile(rows, cols, itemsize=4, target=1 << 20):
    want = max(SUBLANES, target // (cols * itemsize))
    t = min(rows, (want // 16) * 16)
    while t > 16 and rows % t:
        t -= 16
    return t if rows % t == 0 else rows


def _sum_slots(name, r, out_dtype=F32):
    ns = r.shape[0]
    r2 = _rows2d(r, 1)
    _, rows, cols = r2.shape
    tr = _row_tile(rows, cols)

    def body(r_ref, o_ref):
        acc = r_ref[0].astype(F32)
        for s in range(1, ns):
            acc = acc + r_ref[s].astype(F32)
        o_ref[...] = acc.astype(o_ref.dtype)

    out = pl.pallas_call(
        body, name=name, grid=(rows // tr,),
        in_specs=[pl.BlockSpec((ns, tr, cols), lambda i: (0, i, 0))],
        out_specs=pl.BlockSpec((tr, cols), lambda i: (i, 0)),
        out_shape=_hbm_out((rows, cols), out_dtype),
        compiler_params=_params(("parallel",)),
    )(r2)
    return out.reshape(r.shape[1:])


def _adamw(name, g_parts, w, m, v):
    shape = w.shape
    ng = len(g_parts)
    args = [_rows2d(a) for a in (*g_parts, w, m, v)]
    rows, cols = args[0].shape
    tr = _row_tile(rows, cols, (ng + 7) * 4, target=16 << 20)
    c1 = 1.0 - ADAM_B1 ** ADAM_STEP
    c2 = 1.0 - ADAM_B2 ** ADAM_STEP

    def body(*refs):
        g = refs[0][...]
        for r in refs[1:ng]:
            g = g + r[...]
        w_ref, m_ref, v_ref = refs[ng:ng + 3]
        g_out, d_out, m_out, v_out = refs[ng + 3:]
        mn = ADAM_B1 * m_ref[...] + (1.0 - ADAM_B1) * g
        vn = ADAM_B2 * v_ref[...] + (1.0 - ADAM_B2) * (g * g)
        m_hat = mn / c1
        v_hat = vn / c2
        g_out[...] = g
        d_out[...] = -ADAM_LR * (m_hat / (jnp.sqrt(v_hat) + ADAM_EPS) + ADAM_WD * w_ref[...])
        m_out[...] = mn
        v_out[...] = vn

    spec = pl.BlockSpec((tr, cols), lambda i: (i, 0))
    outs = pl.pallas_call(
        body, name=name, grid=(rows // tr,), in_specs=[spec] * (ng + 3), out_specs=[spec] * 4,
        out_shape=[_hbm_out((rows, cols), F32)] * 4,
        compiler_params=_params(("parallel",)),
    )(*args)
    return tuple(o.reshape(shape) for o in outs)


_WEIGHTS = ["norm_mix", "norm_ffn", "w_ffn_in", "w_ffn_out", "w_rec_in", "conv_w", "conv_b", "w_lru_gates",
            "b_lru_gates", "lru_param", "w_rec_out", "norm_kv", "w_kvf", "b_forget", "w_q", "w_o", "norm_final"]
_BIG = ["w_ffn_in", "w_ffn_out", "w_rec_in", "w_lru_gates", "w_rec_out", "w_kvf", "w_q", "w_o"]


def _stack3(a):
    return a[None] if a.ndim == 2 else a.reshape(a.shape[0], -1, a.shape[-1])


def _pad_lanes(a, n):
    return jnp.pad(a, ((0, 0),) * (a.ndim - 1) + ((0, n - a.shape[-1]),))


def kernel(x, norm_mix, norm_ffn, w_ffn_in, w_ffn_out, w_rec_in, conv_w, conv_b, w_lru_gates, b_lru_gates, lru_param, w_rec_out, norm_kv, w_kvf, b_forget, w_q, w_o, norm_final, loss_target, m_norm_mix, m_norm_ffn, m_w_ffn_in, m_w_ffn_out, m_w_rec_in, m_conv_w, m_conv_b, m_w_lru_gates, m_b_lru_gates, m_lru_param, m_w_rec_out, m_norm_kv, m_w_kvf, m_b_forget, m_w_q, m_w_o, m_norm_final, v_norm_mix, v_norm_ffn, v_w_ffn_in, v_w_ffn_out, v_w_rec_in, v_conv_w, v_conv_b, v_w_lru_gates, v_b_lru_gates, v_lru_param, v_w_rec_out, v_norm_kv, v_w_kvf, v_b_forget, v_w_q, v_w_o, v_norm_final):
    P = dict(norm_mix=norm_mix, norm_ffn=norm_ffn, w_ffn_in=w_ffn_in, w_ffn_out=w_ffn_out, w_rec_in=w_rec_in,
             conv_w=conv_w, conv_b=conv_b, w_lru_gates=w_lru_gates, b_lru_gates=b_lru_gates, lru_param=lru_param,
             w_rec_out=w_rec_out, norm_kv=norm_kv, w_kvf=w_kvf, b_forget=b_forget, w_q=w_q, w_o=w_o,
             norm_final=norm_final)
    M1 = dict(norm_mix=m_norm_mix, norm_ffn=m_norm_ffn, w_ffn_in=m_w_ffn_in, w_ffn_out=m_w_ffn_out,
              w_rec_in=m_w_rec_in, conv_w=m_conv_w, conv_b=m_conv_b, w_lru_gates=m_w_lru_gates,
              b_lru_gates=m_b_lru_gates, lru_param=m_lru_param, w_rec_out=m_w_rec_out, norm_kv=m_norm_kv,
              w_kvf=m_w_kvf, b_forget=m_b_forget, w_q=m_w_q, w_o=m_w_o, norm_final=m_norm_final)
    M2 = dict(norm_mix=v_norm_mix, norm_ffn=v_norm_ffn, w_ffn_in=v_w_ffn_in, w_ffn_out=v_w_ffn_out,
              w_rec_in=v_w_rec_in, conv_w=v_conv_w, conv_b=v_conv_b, w_lru_gates=v_w_lru_gates,
              b_lru_gates=v_b_lru_gates, lru_param=v_lru_param, w_rec_out=v_w_rec_out, norm_kv=v_norm_kv,
              w_kvf=v_w_kvf, b_forget=v_b_forget, w_q=v_w_q, w_o=v_w_o, norm_final=v_norm_final)

    _, S, D = x.shape
    L = norm_mix.shape[0]
    NA, NBLK, BW, GS = w_lru_gates.shape
    C = NBLK * BW
    CS = C // N_CHIPS
    H = b_forget.shape[0]
    assert C == D and H * HEAD_DIM == D and H <= LANES
    chip = 2 * lax.axis_index("x") + lax.axis_index("y")

    small_a = jnp.concatenate([conv_w, conv_b[:, None], lru_param[:, None]], axis=1)
    small_a, b_gates = _gather_smalls("gather_smalls", [small_a, b_lru_gates])
    small_a = small_a.transpose(1, 2, 0, 3).reshape(NA, 6, C)
    b_gates = b_gates.transpose(1, 2, 0, 3).reshape(NA, NBLK, 1, N_CHIPS * GS)
    shards = [_stack3(P[w]).astype(BF16) for w in _BIG]
    core = lax.axis_index("c")
    chip_id = jnp.reshape(chip, (1,)).astype(jnp.int32)
    core_id = jnp.reshape(core, (1,)).astype(jnp.int32)
    me_core = jnp.stack([chip, core]).astype(jnp.int32)

    parts_of_layer = ("mix", "mix2", "ffn")

    def part_items(l, part):
        if part == "ffn":
            names, at = ["w_ffn_in", "w_ffn_out"], l
        elif l < NA:
            names, at = (["w_rec_in"] if part == "mix" else ["w_lru_gates", "w_rec_out"]), l
        else:
            names, at = ((["w_kvf"] if l == NA else []) + ["w_q"] if part == "mix" else ["w_o"]), l - NA
        return [(_BIG.index(n), 0 if n == "w_kvf" else at) for n in names]

    def stage_of(l, part):
        return (l, part) if l == 0 or part == "ffn" else (l, "mixer")

    def stage_items(st):
        l, part = st
        return [it for p in (("mix", "mix2") if part == "mixer" else (part,)) for it in part_items(l, p)]

    stages = [(0, p) for p in parts_of_layer] + [(l, p) for l in range(1, L) for p in ("mixer", "ffn")]
    items = [it for st in stages for it in stage_items(st)]
    ids_of = {st: [items.index(it) for it in stage_items(st)] for st in stages}
    bufs = [_place_own(f"place_{_BIG[w]}_{li}", shards[w], li, chip_id) for w, li in items]
    bufs, gather_sems = _gather_start("gather_start", shards, bufs, items, small_a)

    forwarding, fetched = {}, {}

    def layer_prefetch(l, part, after):
        st = stage_of(l, part)
        if l < L and st not in forwarding:
            got = _gather_wait(f"gather_wait_{st[1]}_{l}", shards, bufs, items, ids_of[st], gather_sems, after)
            forwarding[st] = _forward_start(f"forward_start_{st[1]}_{l}", got)

    def layer_weights(l, part, after):
        if l >= L:
            return None
        st = stage_of(l, part)
        if st not in fetched:
            layer_prefetch(l, part, after)
            got, sems = forwarding[st]
            got = _forward_wait(f"forward_wait_{st[1]}_{l}", got, sems, after)
            fetched[st] = {_BIG[items[i][0]]: g for i, g in zip(ids_of[st], got)}
        B = fetched[st]
        if part == "ffn":
            return dict(w_ffn_in=B["w_ffn_in"], w_ffn_out=B["w_ffn_out"].reshape(-1, D))
        if l < NA and part == "mix":
            return dict(w_rec_in=B["w_rec_in"], conv_w=small_a[l, :4], conv_b=small_a[l, 4:5])
        if l < NA:
            return dict(w_gates=B["w_lru_gates"].reshape(N_CHIPS, NBLK, BW, GS).transpose(1, 2, 0, 3).reshape(
                NBLK, BW, N_CHIPS * GS), b_gates=b_gates[l], w_rec_out=B["w_rec_out"].reshape(C, D),
                lru_param=small_a[l, 5:6])
        if part == "mix2":
            return dict(w_o=B["w_o"].reshape(D, D))
        W = dict(w_q=B["w_q"].reshape(D, D))
        if l == NA:
            w_kvf_full = B["w_kvf"].transpose(1, 0, 2).reshape(D, -1)
            W.update(norm_kv=norm_kv[None], w_kv=w_kvf_full[:, :2 * D],
                     w_f=_pad_lanes(w_kvf_full[:, 2 * D:], LANES), b_f=_pad_lanes(b_forget[None], LANES))
        return W

    G_small = {l: {} for l in range(L)}
    stash = {st: {} for st in stages}
    pending = {}
    reducing = []

    def finish_reduce(after):
        st, its, grads, others, sems = reducing.pop()
        l, part = st
        others = _reduce_wait(f"reduce_wait_{part}_{l}", grads, others, sems, after)
        parts = [_sum_cores(f"sum_cores_{l}_{_BIG[w]}", g, o, core_id) for (w, _), g, o in zip(its, grads, others)]
        recv, sems, token = _scatter_start(f"scatter_start_{part}_{l}", parts)
        pending[st] = (parts, recv, sems)
        return token

    def layer_grads(l, part, G_part):
        G_small[l].update(G_part)
        st = stage_of(l, part)
        stash[st].update(G_part)
        if st[1] == "mixer" and part != "mix":
            return None
        G = stash[st]
        late = {"ffn": "w_ffn_in", "mix": "norm_mix"}.get(part) or ("w_gates" if l < NA else "w_o")
        by_name = dict(
            w_ffn_in=lambda: G["w_ffn_in"], w_ffn_out=lambda: G["w_ffn_out"].reshape(N_CHIPS, -1, D),
            w_rec_in=lambda: G["w_rec_in"],
            w_lru_gates=lambda: G["w_gates"].reshape(NBLK, BW, N_CHIPS, GS).transpose(2, 0, 1, 3).reshape(
                N_CHIPS, NBLK * BW, GS),
            w_rec_out=lambda: G["w_rec_out"].reshape(N_CHIPS, -1, D),
            w_kvf=lambda: jnp.concatenate([G["w_kv"].astype(F32), G["w_f"][:, :H]], axis=1).reshape(
                D, N_CHIPS, -1).transpose(1, 0, 2).astype(BF16),
            w_q=lambda: G["w_q"].reshape(N_CHIPS, -1, D), w_o=lambda: G["w_o"].reshape(N_CHIPS, -1, D))
        its = stage_items(st)
        grads = [by_name[_BIG[w]]() for w, _ in its]
        if reducing:
            pst, pits, pgrads, pothers, psems = reducing.pop()
            pothers = _reduce_wait(f"reduce_wait_{pst[1]}_{pst[0]}", pgrads, pothers, psems, G_part[late])
            pparts = [_sum_cores(f"sum_cores_{pst[0]}_{_BIG[w]}", g, o, core_id)
                      for (w, _), g, o in zip(pits, pgrads, pothers)]
            (recv, ssems), (others, sems), token = _scatter_reduce_start(
                f"scatter_reduce_start_{st[1]}_{l}", pparts, grads)
            pending[pst] = (pparts, recv, ssems)
        else:
            others, sems, token = _reduce_start(f"reduce_start_{st[1]}_{l}", grads, jnp.zeros((SUBLANES, LANES), F32))
        reducing.append((st, its, grads, others, sems))
        return finish_reduce(token) if l == 0 else token

    gains = dict(mix=[norm_mix[l][None] for l in range(L)], ffn=[norm_ffn[l][None] for l in range(L)],
                 final=norm_final[None])
    loss_row, grad_x, dg_final = _local_step(x.reshape(S, D), loss_target.reshape(S, D), gains,
                                             layer_weights, layer_prefetch, layer_grads)

    rows = [*[G_small[l]["norm_mix"] for l in range(L)], *[G_small[l]["norm_ffn"] for l in range(L)],
            G_small[NA]["norm_kv"], dg_final, _pad_lanes(G_small[NA]["b_f"], D), _pad_lanes(loss_row, D)]
    for a in range(NA):
        rows += [G_small[a][n] for n in ("conv_w", "conv_b", "b_gi", "b_gr", "lru_param")]
    packed = jnp.concatenate(rows, axis=0)
    everyone, small_sems, small_token = _gather_all_start("gather_small_start", packed)

    full = [lax.empty(sh.shape, F32) for sh in shards]
    for st in reversed(stages):
        l, part = st
        parts, recv, sems = pending[st]
        recv = _scatter_wait(f"scatter_wait_{part}_{l}", parts, recv, sems, small_token)
        for (w, li), own, r in zip(stage_items(st), parts, recv):
            full[w] = _sum_chips(f"sum_chips_{l}_{_BIG[w]}", r, own, full[w], li, me_core)
    full = _share_d2d("share_d2d", full)
    big = {w: _adamw(f"adamw_{w}", [g.reshape(P[w].shape)], P[w], M1[w], M2[w]) for w, g in zip(_BIG, full)}

    everyone = _gather_all_wait("gather_small_wait", packed, everyone, small_sems, big[_BIG[-1]][1])
    everyone = lax.dynamic_update_slice(everyone, packed[None], (2 * chip + core, 0, 0))
    tot = _sum_slots("sum_small", everyone)
    loss = tot[2 * L + 3, 0]
    g_rep = jnp.concatenate([tot[:2 * L + 2], tot[2 * L + 2:2 * L + 3]], axis=0)
    base = 2 * L + 4
    g_sh = []
    for a in range(NA):
        blk = lax.dynamic_slice_in_dim(tot[base + 8 * a:base + 8 * a + 8], chip * CS, CS, axis=1)
        gi = tot[base + 8 * a + 5].reshape(NBLK, BW)
        gr = tot[base + 8 * a + 6].reshape(NBLK, BW)
        bl = lax.dynamic_slice_in_dim(jnp.concatenate([gi, gr], axis=1), chip * GS, GS, axis=1)
        g_sh += [blk[:5], bl.reshape(-1, CS), blk[7:8]]
    g_sh = jnp.concatenate(g_sh, axis=0)
    nrow = g_sh.shape[0] // NA

    def pack_rep(T):
        return jnp.concatenate([T["norm_mix"], T["norm_ffn"], T["norm_kv"][None], T["norm_final"][None],
                                _pad_lanes(T["b_forget"][None], D)], axis=0)

    def pack_sh(T):
        return jnp.concatenate([jnp.concatenate([T["conv_w"][a], T["conv_b"][a][None],
                                                 T["b_lru_gates"][a].reshape(-1, CS), T["lru_param"][a][None]], axis=0)
                                for a in range(NA)], axis=0)

    rep = _adamw("adamw_replicated", [g_rep], pack_rep(P), pack_rep(M1), pack_rep(M2))
    shd = _adamw("adamw_small_sharded", [g_sh], pack_sh(P), pack_sh(M1), pack_sh(M2))

    def unpack_rep(t):
        return dict(norm_mix=t[:L], norm_ffn=t[L:2 * L], norm_kv=t[2 * L], norm_final=t[2 * L + 1],
                    b_forget=t[2 * L + 2, :H])

    def unpack_sh(t):
        t = t.reshape(NA, nrow, CS)
        return dict(conv_w=t[:, :4], conv_b=t[:, 4], b_lru_gates=t[:, 5:nrow - 1].reshape(NA, NBLK, GS),
                    lru_param=t[:, nrow - 1])

    outs = []
    for i in range(4):
        small = {**unpack_rep(rep[i]), **unpack_sh(shd[i])}
        outs.append([big[w][i] if w in big else small[w] for w in _WEIGHTS])
    return (loss, grad_x.reshape(1, S, D), *outs[0], *outs[1], *outs[2], *outs[3])
```

```python
import functools
import math

import jax
import jax.numpy as jnp
from jax import lax
from jax.experimental import pallas as pl
from jax.experimental.pallas import tpu as pltpu

F32 = jnp.float32
BF16 = jnp.bfloat16

EPS = 1e-6
LRU_C = 8.0
HEAD_DIM = 64
LANES = 128
SUBLANES = 8
VMEM_LIMIT = 48 * 1024 * 1024
N_CHIPS = 4
N_DEV = 8

ADAM_LR = 0.001
ADAM_B1 = 0.9
ADAM_B2 = 0.999
ADAM_EPS = 1e-08
ADAM_WD = 0.01
ADAM_STEP = 10

_NN = (((1,), (0,)), ((), ()))
_NT = (((1,), (1,)), ((), ()))
_TN = (((0,), (0,)), ((), ()))
_DN = {"nn": _NN, "nt": _NT, "tn": _TN}
MESH = pl.DeviceIdType.MESH


def _hbm_out(shape, dtype):
    return pltpu.HBM(shape, dtype)


def _params(sem):
    return pltpu.CompilerParams(dimension_semantics=sem, vmem_limit_bytes=VMEM_LIMIT)


def _tile(n, want):
    if n <= want:
        return n
    t = (want // LANES) * LANES
    while t >= LANES:
        if n % t == 0:
            return t
        t -= LANES
    return n


def _sigmoid(x):
    return 1.0 / (1.0 + jnp.exp(-x))


def _sigmoid_t(x):
    return 0.5 * jnp.tanh(0.5 * x) + 0.5


def _softplus(x):
    return jnp.maximum(x, 0.0) + jnp.log(1.0 + jnp.exp(-jnp.abs(x)))


_GELU_C = math.sqrt(2.0 / math.pi)


def _gelu_and_grad(x):
    inner = _GELU_C * (x + 0.044715 * x * x * x)
    t = jnp.tanh(inner)
    g = 0.5 * x * (1.0 + t)
    dg = 0.5 * (1.0 + t) + 0.5 * x * (1.0 - t * t) * _GELU_C * (1.0 + 3.0 * 0.044715 * x * x)
    return g, dg


def _rms(x):
    return lax.rsqrt(jnp.mean(x * x, axis=-1, keepdims=True) + EPS)


def _rms_bwd(dy, x, g):
    r = _rms(x)
    xr = x * r
    dyg = dy * g
    return r * dyg - xr * (r * jnp.mean(dyg * xr, axis=-1, keepdims=True)), jnp.sum(dy * xr, axis=0, keepdims=True)


def _mm(name, mode, a, b, *, grid, a_spec, b_spec, out_shape, out_dtype, out_spec, nk=1,
        res=None, res_spec=None, bias=None, bias_spec=None, scale=None, norm_gain=None, norm_bwd=None):
    dn = _DN[mode]
    has_res, has_bias = res is not None, bias is not None
    blk = tuple(d for d in out_spec.block_shape if d is not None)
    vec = pl.BlockSpec((1, blk[-1]), lambda *g: (0, 0))
    a_specs = a_spec if isinstance(a_spec, list) else [a_spec]
    b_specs = b_spec if isinstance(b_spec, list) else [b_spec]
    npair = len(a_specs)
    n_in = 2 * npair + int(has_res) + int(has_bias) + (1 if norm_gain is not None else 0) + (3 if norm_bwd else 0)

    def body(*refs):
        p = 2 * npair
        r_ref = refs[p] if has_res else None
        p += int(has_res)
        bias_ref = refs[p] if has_bias else None
        p += int(has_bias)
        extra = refs[p:n_in]
        outs = refs[n_in:]
        o_ref = outs[0]
        part = lax.dot_general(refs[0][...], refs[npair][...], dn, preferred_element_type=F32)
        for t in range(1, npair):
            part = part + lax.dot_general(refs[t][...], refs[npair + t][...], dn, preferred_element_type=F32)

        def finish(acc):
            if scale is not None:
                acc = acc * scale
            if has_bias:
                acc = acc + bias_ref[...]
            if has_res:
                acc = r_ref[...] + acc
            if norm_bwd:
                h_ref, g_ref, dh_ref = extra
                dx, dg = _rms_bwd(acc, h_ref[...], g_ref[...])
                acc = dh_ref[...] + dx
                outs[1][...] = acc.astype(BF16)
                outs[2][...] = dg
            if norm_gain is not None:
                outs[1][...] = (acc * _rms(acc) * extra[0][...]).astype(BF16)
            o_ref[...] = acc.astype(o_ref.dtype)

        if nk == 1:
            finish(part)
        else:
            acc_ref = refs[-1]
            k = pl.program_id(2)

            @pl.when(k == 0)
            def _():
                acc_ref[...] = part

            @pl.when(k > 0)
            def _():
                acc_ref[...] += part

            @pl.when(k == nk - 1)
            def _():
                finish(acc_ref[...])

    ins, specs = [a] * npair + [b] * npair, a_specs + b_specs
    if has_res:
        ins.append(res)
        specs.append(res_spec)
    if has_bias:
        ins.append(bias)
        specs.append(bias_spec)
    out_specs, out_shapes = [out_spec], [_hbm_out(out_shape, out_dtype)]
    if norm_gain is not None:
        ins.append(norm_gain)
        specs.append(vec)
        out_specs.append(out_spec)
        out_shapes.append(_hbm_out(out_shape, BF16))
    if norm_bwd:
        h, g, dh = norm_bwd
        ins += [h, g, dh]
        specs += [out_spec, vec, out_spec]
        out_specs += [out_spec, pl.BlockSpec((None, 1, blk[-1]), lambda i, *rest: (i, 0, 0))]
        out_shapes += [_hbm_out(out_shape, BF16), _hbm_out((grid[0], 1, blk[-1]), F32)]
    sem = ("parallel", "parallel") + (("arbitrary",) if len(grid) == 3 else ())
    single = len(out_specs) == 1
    return pl.pallas_call(
        body, name=name, grid=grid, in_specs=specs, out_specs=out_specs[0] if single else out_specs,
        out_shape=out_shapes[0] if single else out_shapes,
        scratch_shapes=[pltpu.VMEM(blk, F32)] if nk > 1 else [],
        compiler_params=_params(sem),
    )(*ins)


def _mm_nn(name, a, b, *, b_lead=(), out_dtype, tm=512, tn=512, res=None, bias=None, scale=None, norm_gain=None):
    M, K = a.shape
    N = b.shape[-1]
    tm, tn = _tile(M, tm), _tile(N, tn)
    nl = len(b_lead)
    return _mm(
        name, "nn", a, b, grid=(M // tm, N // tn),
        a_spec=pl.BlockSpec((tm, K), lambda i, j: (i, 0)),
        b_spec=pl.BlockSpec((None,) * nl + (K, tn), lambda i, j: tuple(b_lead) + (0, j)),
        out_shape=(M, N), out_dtype=out_dtype, out_spec=pl.BlockSpec((tm, tn), lambda i, j: (i, j)),
        res=res, res_spec=pl.BlockSpec((tm, tn), lambda i, j: (i, j)),
        bias=bias, bias_spec=pl.BlockSpec((1, tn), lambda i, j: (0, j)), scale=scale, norm_gain=norm_gain)


def _mm_nt(name, a, b, *, b_lead=(), out_dtype, tm=512, tn=512, tk=2048, res=None, norm_bwd=None):
    M, K = a.shape
    N = b.shape[-2]
    tm, tn, tk = _tile(M, tm), _tile(N, tn), _tile(K, tk)
    nk = K // tk
    nl = len(b_lead)
    return _mm(
        name, "nt", a, b, grid=(M // tm, N // tn, nk), nk=nk,
        a_spec=pl.BlockSpec((tm, tk), lambda i, j, k: (i, k)),
        b_spec=pl.BlockSpec((None,) * nl + (tn, tk), lambda i, j, k: tuple(b_lead) + (j, k)),
        out_shape=(M, N), out_dtype=out_dtype, out_spec=pl.BlockSpec((tm, tn), lambda i, j, k: (i, j)),
        res=res, res_spec=pl.BlockSpec((tm, tn), lambda i, j, k: (i, j)), norm_bwd=norm_bwd)


def _mm_tn(name, a, b, *, out_dtype, tm=512, tn=512):
    S, M = a.shape
    N = b.shape[1]
    tm, tn = _tile(M, tm), _tile(N, tn)
    return _mm(
        name, "tn", a, b, grid=(M // tm, N // tn),
        a_spec=pl.BlockSpec((S, tm), lambda i, j: (0, i)),
        b_spec=pl.BlockSpec((S, tn), lambda i, j: (0, j)),
        out_shape=(M, N), out_dtype=out_dtype, out_spec=pl.BlockSpec((tm, tn), lambda i, j: (i, j)))


def _rmsnorm_fwd(name, h, g, tr=256):
    S, D = h.shape
    tr = _tile(S, tr)

    def body(h_ref, g_ref, o_ref):
        x = h_ref[...]
        r = lax.rsqrt(jnp.mean(x * x, axis=-1, keepdims=True) + EPS)
        o_ref[...] = (x * r * g_ref[...]).astype(o_ref.dtype)

    return pl.pallas_call(
        body, name=name, grid=(S // tr,),
        in_specs=[pl.BlockSpec((tr, D), lambda i: (i, 0)), pl.BlockSpec((1, D), lambda i: (0, 0))],
        out_specs=pl.BlockSpec((tr, D), lambda i: (i, 0)),
        out_shape=_hbm_out((S, D), BF16),
        compiler_params=_params(("parallel",)),
    )(h, g)


def _loss_head(name, h, target, g, tr=256):
    S, D = h.shape
    tr = _tile(S, tr)

    def body(h_ref, t_ref, g_ref, o_ref, ob_ref, dg_ref, loss_ref):
        i = pl.program_id(0)
        x = h_ref[...]
        gg = g_ref[...]
        r = lax.rsqrt(jnp.mean(x * x, axis=-1, keepdims=True) + EPS)
        xr = x * r
        err = xr * gg - t_ref[...]
        lpart = 0.5 * jnp.sum(jnp.mean(err * err, axis=-1, keepdims=True), axis=0, keepdims=True)
        dy = err * (1.0 / D)
        dyg = dy * gg
        dx = r * dyg - xr * (r * jnp.mean(dyg * xr, axis=-1, keepdims=True))
        o_ref[...] = dx
        ob_ref[...] = dx.astype(BF16)
        part = jnp.sum(dy * xr, axis=0, keepdims=True)
        lrow = jnp.broadcast_to(lpart, (1, LANES))

        @pl.when(i == 0)
        def _():
            dg_ref[...] = part
            loss_ref[...] = lrow

        @pl.when(i > 0)
        def _():
            dg_ref[...] += part
            loss_ref[...] += lrow

    row = pl.BlockSpec((tr, D), lambda i: (i, 0))
    vec = pl.BlockSpec((1, D), lambda i: (0, 0))
    return pl.pallas_call(
        body, name=name, grid=(S // tr,),
        in_specs=[row, row, vec], out_specs=[row, row, vec, pl.BlockSpec((1, LANES), lambda i: (0, 0))],
        out_shape=[_hbm_out((S, D), F32), _hbm_out((S, D), BF16),
                   _hbm_out((1, D), F32), _hbm_out((1, LANES), F32)],
        compiler_params=_params(("arbitrary",)),
    )(h, target, g)


def _swiglu_fwd(name, hn, w_in, tm=512):
    S, D = hn.shape
    FH = w_in.shape[-1]
    tm = _tile(S, tm)

    def body(x_ref, wg_ref, wu_ref, z_ref, a_ref):
        x = x_ref[...]
        zg = jnp.dot(x, wg_ref[...], preferred_element_type=F32)
        zu = jnp.dot(x, wu_ref[...], preferred_element_type=F32)
        sg = _sigmoid_t(zg)
        silu = zg * sg
        z_ref[0] = (zu * (sg * (1.0 + zg * (1.0 - sg)))).astype(z_ref.dtype)
        z_ref[1] = silu.astype(z_ref.dtype)
        a_ref[...] = (silu * zu).astype(a_ref.dtype)

    return pl.pallas_call(
        body, name=name, grid=(2, S // tm),
        in_specs=[pl.BlockSpec((tm, D), lambda j, i: (i, 0)),
                  pl.BlockSpec((None, D, FH), lambda j, i: (j, 0, 0)),
                  pl.BlockSpec((None, D, FH), lambda j, i: (j + 2, 0, 0))],
        out_specs=[pl.BlockSpec((2, tm, FH), lambda j, i: (0, i, j)), pl.BlockSpec((tm, FH), lambda j, i: (i, j))],
        out_shape=[_hbm_out((2, S, 2 * FH), BF16), _hbm_out((S, 2 * FH), BF16)],
        compiler_params=_params(("parallel", "parallel")),
    )(hn, w_in, w_in)


def _swiglu_bwd(name, dhb, w_out, z3, tm=512):
    S, D = dhb.shape
    F = w_out.shape[0]
    FH = F // 2
    tm = _tile(S, tm)

    def body(d_ref, w_ref, z_ref, dz_ref):
        d = lax.dot_general(d_ref[...], w_ref[...], _NT, preferred_element_type=F32)
        dz_ref[0] = (d * z_ref[0].astype(F32)).astype(dz_ref.dtype)
        dz_ref[1] = (d * z_ref[1].astype(F32)).astype(dz_ref.dtype)

    zspec = pl.BlockSpec((2, tm, FH), lambda j, i: (0, i, j))
    return pl.pallas_call(
        body, name=name, grid=(2, S // tm),
        in_specs=[pl.BlockSpec((tm, D), lambda j, i: (i, 0)), pl.BlockSpec((FH, D), lambda j, i: (j, 0)), zspec],
        out_specs=zspec, out_shape=_hbm_out((2, S, F), BF16),
        compiler_params=_params(("parallel", "parallel")),
    )(dhb, w_out, z3)


SCAN_ROWS = 64


def _group_scan(A, B, reverse):
    n = A.shape[0]
    sub = lax.broadcasted_iota(jnp.int32, A.shape, 0) % SUBLANES
    for d in (1, 2, 4):
        if reverse:
            A_sh, B_sh = pltpu.roll(A, n - d, 0), pltpu.roll(B, n - d, 0)
            keep = sub < SUBLANES - d
        else:
            A_sh, B_sh = pltpu.roll(A, d, 0), pltpu.roll(B, d, 0)
            keep = sub >= d
        B = jnp.where(keep, A * B_sh + B, B)
        A = jnp.where(keep, A * A_sh, A)
    return A, B


def _block_scan(a, u, carry, reverse):
    A, B = _group_scan(a, u, reverse)
    ng = a.shape[0] // SUBLANES
    out = [None] * ng
    order = range(ng - 1, -1, -1) if reverse else range(ng)
    for gi in order:
        sl = slice(gi * SUBLANES, (gi + 1) * SUBLANES)
        hg = A[sl] * carry + B[sl]
        out[gi] = hg
        carry = hg[0:1] if reverse else hg[SUBLANES - 1:SUBLANES]
    return jnp.concatenate(out, axis=0), carry


def _lru_gates(rc, gip, grp, sp):
    gi = _sigmoid_t(gip)
    gr = _sigmoid_t(grp)
    la = -LRU_C * gr * sp
    a = jnp.exp(la)
    om = -jnp.tanh(la) * (a * a + 1.0)
    mult = jnp.sqrt(om)
    return gi, gr, a, mult


def _lru_fwd(name, proj, rc, gip, grp, lru_p, tc=256):
    S, C = rc.shape
    tc = _tile(C, tc)
    nb = S // SCAN_ROWS

    def body(gb_ref, rc_ref, gi_ref, gr_ref, l_ref, h_ref, m_ref):
        sp = _softplus(-l_ref[...])

        def step(b, carry):
            rows = pl.ds(pl.multiple_of(b * SCAN_ROWS, SCAN_ROWS), SCAN_ROWS)
            rcb = rc_ref[rows, :]
            gi, _, a, mult = _lru_gates(rcb, gi_ref[rows, :], gr_ref[rows, :], sp)
            h, carry = _block_scan(a, rcb * gi * mult, carry, False)
            h_ref[rows, :] = h
            gel, _ = _gelu_and_grad(gb_ref[rows, :])
            m_ref[rows, :] = (gel * h).astype(m_ref.dtype)
            return carry

        lax.fori_loop(0, nb, step, jnp.zeros((1, tc), F32))

    col = pl.BlockSpec((S, tc), lambda j: (0, j))
    return pl.pallas_call(
        body, name=name, grid=(C // tc,),
        in_specs=[col, col, col, col, pl.BlockSpec((1, tc), lambda j: (0, j))],
        out_specs=[col, col],
        out_shape=[_hbm_out((S, C), F32), _hbm_out((S, C), BF16)],
        compiler_params=_params(("parallel",)),
    )(proj, rc, gip, grp, lru_p)


def _lru_bwd(name, dm, proj, hrec, rc, gip, grp, lru_p, tc=256):
    S, C = rc.shape
    tc = _tile(C, tc)
    nb = S // SCAN_ROWS
    R = SCAN_ROWS

    def body(dm_ref, gb_ref, h_ref, rc_ref, gi_ref, gr_ref, l_ref,
             dgb_ref, dgi_ref, dgr_ref, drc_ref, dbi_ref, dbr_ref, dl_ref):
        lp = l_ref[...]
        sp = _softplus(-lp)
        row = lax.broadcasted_iota(jnp.int32, (R, tc), 0)
        zero = jnp.zeros((1, tc), F32)

        def step(t, carry):
            mu_in, s_i, s_r, s_sp = carry
            b = nb - 1 - t
            r0 = pl.multiple_of(b * R, R)
            rows = pl.ds(r0, R)
            rcb = rc_ref[rows, :]
            gi, gr, a, mult = _lru_gates(rcb, gi_ref[rows, :], gr_ref[rows, :], sp)
            gel, dgel = _gelu_and_grad(gb_ref[rows, :])
            dmb = dm_ref[rows, :]
            h = h_ref[rows, :]
            dgb_ref[rows, :] = (dmb * h * dgel).astype(dgb_ref.dtype)
            dh = dmb * gel
            mu, mu_out = _block_scan(a, a * dh, mu_in, True)
            mu_next = jnp.where(row == R - 1, mu_in, pltpu.roll(mu, R - 1, 0))
            lam = dh + mu_next
            p0 = pl.multiple_of(jnp.maximum(r0 - SUBLANES, 0), SUBLANES)
            prev = h_ref[pl.ds(p0, SUBLANES), :][SUBLANES - 1:SUBLANES]
            prev = jnp.where(b > 0, prev, 0.0)
            h_prev = jnp.where(row == 0, prev, pltpu.roll(h, 1, 0))
            da = lam * h_prev
            d_mult = lam * rcb * gi
            d_la = da * a - d_mult * (a * a) / mult
            d_grp = d_la * (-LRU_C * sp) * gr * (1.0 - gr)
            d_gip = lam * rcb * mult * gi * (1.0 - gi)
            dgr_ref[rows, :] = d_grp.astype(dgr_ref.dtype)
            dgi_ref[rows, :] = d_gip.astype(dgi_ref.dtype)
            drc_ref[rows, :] = lam * gi * mult
            s_i = s_i + jnp.sum(d_gip, axis=0, keepdims=True)
            s_r = s_r + jnp.sum(d_grp, axis=0, keepdims=True)
            s_sp = s_sp + jnp.sum(d_la * gr, axis=0, keepdims=True)
            return mu_out, s_i, s_r, s_sp

        _, s_i, s_r, s_sp = lax.fori_loop(0, nb, step, (zero, zero, zero, zero))
        dbi_ref[...] = s_i
        dbr_ref[...] = s_r
        dl_ref[...] = (-LRU_C * s_sp) * (-_sigmoid(-lp))

    col = pl.BlockSpec((S, tc), lambda j: (0, j))
    vec = pl.BlockSpec((1, tc), lambda j: (0, j))
    return pl.pallas_call(
        body, name=name, grid=(C // tc,),
        in_specs=[col, col, col, col, col, col, vec],
        out_specs=[col, col, col, col, vec, vec, vec],
        out_shape=[_hbm_out((S, C), BF16), _hbm_out((S, C), BF16),
                   _hbm_out((S, C), BF16), _hbm_out((S, C), F32),
                   _hbm_out((1, C), F32), _hbm_out((1, C), F32),
                   _hbm_out((1, C), F32)],
        compiler_params=_params(("parallel",)),
    )(dm, proj, hrec, rc, gip, grp, lru_p)


def _cumsum_rows(name, u, reverse):
    S, C = u.shape
    nb = S // SCAN_ROWS

    def body(u_ref, o_ref):
        def step(t, carry):
            b = nb - 1 - t if reverse else t
            rows = pl.ds(pl.multiple_of(b * SCAN_ROWS, SCAN_ROWS), SCAN_ROWS)
            ub = u_ref[rows, :]
            h, carry = _block_scan(jnp.ones_like(ub), ub, carry, reverse)
            o_ref[rows, :] = h
            return carry

        lax.fori_loop(0, nb, step, jnp.zeros((1, C), F32))

    spec = pl.BlockSpec((S, C), lambda i: (0, 0))
    return pl.pallas_call(
        body, name=name, grid=(1,), in_specs=[spec], out_specs=spec,
        out_shape=_hbm_out((S, C), F32),
        compiler_params=_params(("arbitrary",)),
    )(u)


def _shift_down(x, k):
    row = lax.broadcasted_iota(jnp.int32, x.shape, 0)
    return jnp.where(row >= k, pltpu.roll(x, k, 0), 0.0)


def _shift_up(x, k):
    n = x.shape[0]
    row = lax.broadcasted_iota(jnp.int32, x.shape, 0)
    return jnp.where(row < n - k, pltpu.roll(x, n - k, 0), 0.0)


def _conv_fwd(name, proj, w, b, tc=256):
    S, C2 = proj.shape
    C = C2 // 2
    tc = _tile(C, tc)
    off = C // tc

    def body(x_ref, w_ref, b_ref, o_ref, ob_ref):
        x = x_ref[...]
        out = b_ref[...] + w_ref[3:4, :] * x
        for k in (1, 2, 3):
            out = out + w_ref[3 - k:4 - k, :] * _shift_down(x, k)
        o_ref[...] = out
        ob_ref[...] = out.astype(BF16)

    col = pl.BlockSpec((S, tc), lambda j: (0, j))
    return pl.pallas_call(
        body, name=name, grid=(C // tc,),
        in_specs=[pl.BlockSpec((S, tc), lambda j: (0, off + j)),
                  pl.BlockSpec((4, tc), lambda j: (0, j)), pl.BlockSpec((1, tc), lambda j: (0, j))],
        out_specs=[col, col],
        out_shape=[_hbm_out((S, C), F32), _hbm_out((S, C), BF16)],
        compiler_params=_params(("parallel",)),
    )(proj, w, b)


def _conv_bwd(name, drc, proj, w, tc=256):
    S, C = drc.shape
    tc = _tile(C, tc)
    off = C // tc

    def body(y_ref, x_ref, w_ref, dx_ref, dw_ref, db_ref):
        y = y_ref[...]
        x = x_ref[...]
        dx = w_ref[3:4, :] * y
        dw_ref[3:4, :] = jnp.sum(y * x, axis=0, keepdims=True)
        for k in (1, 2, 3):
            dx = dx + w_ref[3 - k:4 - k, :] * _shift_up(y, k)
            dw_ref[3 - k:4 - k, :] = jnp.sum(y * _shift_down(x, k), axis=0, keepdims=True)
        dx_ref[...] = dx.astype(dx_ref.dtype)
        db_ref[...] = jnp.sum(y, axis=0, keepdims=True)

    col = pl.BlockSpec((S, tc), lambda j: (0, j))
    return pl.pallas_call(
        body, name=name, grid=(C // tc,),
        in_specs=[col, pl.BlockSpec((S, tc), lambda j: (0, off + j)), pl.BlockSpec((4, tc), lambda j: (0, j))],
        out_specs=[col, pl.BlockSpec((4, tc), lambda j: (0, j)), pl.BlockSpec((1, tc), lambda j: (0, j))],
        out_shape=[_hbm_out((S, C), BF16), _hbm_out((4, C), F32),
                   _hbm_out((1, C), F32)],
        compiler_params=_params(("parallel",)),
    )(drc, proj, w)


def _gates_fwd(name, rcb, wg, bg):
    S, C = rcb.shape
    nblk, bw, _ = wg.shape

    def body(x_ref, w_ref, b_ref, gi_ref, gr_ref):
        g = jnp.dot(x_ref[...], w_ref[...], preferred_element_type=F32) + b_ref[...]
        gi_ref[...] = g[:, :bw]
        gr_ref[...] = g[:, bw:]

    col = pl.BlockSpec((S, bw), lambda n: (0, n))
    return pl.pallas_call(
        body, name=name, grid=(nblk,),
        in_specs=[col, pl.BlockSpec((None, bw, 2 * bw), lambda n: (n, 0, 0)),
                  pl.BlockSpec((None, 1, 2 * bw), lambda n: (n, 0, 0))],
        out_specs=[col, col],
        out_shape=[_hbm_out((S, C), F32), _hbm_out((S, C), F32)],
        compiler_params=_params(("parallel",)),
    )(rcb, wg, bg)


def _gates_bwd(name, dgi, dgr, rcb, wg, drc1):
    S, C = rcb.shape
    nblk, bw, _ = wg.shape

    def body(dgi_ref, dgr_ref, x_ref, w_ref, d1_ref, drc_ref, dw_ref):
        w = w_ref[...]
        x = x_ref[...]
        di, dr = dgi_ref[...], dgr_ref[...]
        drc_ref[...] = (d1_ref[...]
                        + lax.dot_general(di, w[:, :bw], _NT, preferred_element_type=F32)
                        + lax.dot_general(dr, w[:, bw:], _NT, preferred_element_type=F32))
        dw_ref[:, :bw] = lax.dot_general(x, di, _TN, preferred_element_type=F32).astype(dw_ref.dtype)
        dw_ref[:, bw:] = lax.dot_general(x, dr, _TN, preferred_element_type=F32).astype(dw_ref.dtype)

    col = pl.BlockSpec((S, bw), lambda n: (0, n))
    wspec = pl.BlockSpec((None, bw, 2 * bw), lambda n: (n, 0, 0))
    return pl.pallas_call(
        body, name=name, grid=(nblk,),
        in_specs=[col, col, col, wspec, col], out_specs=[col, wspec],
        out_shape=[_hbm_out((S, C), F32), _hbm_out((nblk, bw, 2 * bw), BF16)],
        compiler_params=_params(("parallel",)),
    )(dgi, dgr, rcb, wg, drc1)


def _att_tile(S):
    return next(t for t in (512, 256, 128) if S % t == 0)


def _head_lanes(shape):
    return lax.broadcasted_iota(jnp.int32, shape, len(shape) - 1) < HEAD_DIM


def _key_bias(c_ref, rows, hh):
    return jnp.broadcast_to(c_ref[rows, hh:hh + 1], (rows.size, LANES))


def _over_keys(x, op):
    n = x.shape[0]
    while n > SUBLANES:
        n //= 2
        x = op(x[:n], x[n:2 * n])
    return (jnp.max if op is jnp.maximum else jnp.sum)(x, axis=0, keepdims=True)


def _causal_t(T, cc):
    r = lax.broadcasted_iota(jnp.int32, (T, LANES), 0)
    c = lax.broadcasted_iota(jnp.int32, (T, LANES), 1) + cc * LANES
    return r <= c


def _attn_fwd(name, q, kv, c3):
    S, D = q.shape
    HP = D // LANES
    T = _att_tile(S)
    nq = S // T
    NC = T // LANES

    def body(q_ref, k_ref, v_ref, c_ref, o_ref, of_ref, lse_ref, bias, vT, acc, m_scr, l_scr):
        def prologue(i, _):
            rows = pl.ds(pl.multiple_of(i * T, T), T)
            for hh in range(2):
                bias[hh, rows, :] = _key_bias(c_ref, rows, hh)
            vT[i] = v_ref[rows, :].astype(F32).T.astype(BF16)
            return 0

        lax.fori_loop(0, nq, prologue, 0)

        def q_step(qi, _):
            q0 = pl.multiple_of(qi * T, T)
            qb = q_ref[pl.ds(q0, T), :]
            m_scr[...] = jnp.full(m_scr.shape, -jnp.inf, F32)
            l_scr[...] = jnp.zeros(l_scr.shape, F32)
            acc[...] = jnp.zeros(acc.shape, F32)

            def tile(kj, masked):
                ks = pl.ds(pl.multiple_of(kj * T, T), T)
                kf = k_ref[ks, :].astype(F32)
                first = _head_lanes(kf.shape)
                kms = [jnp.where(first if hh == 0 else jnp.logical_not(first), kf, 0.0).astype(BF16) for hh in range(2)]
                sTs = [lax.dot_general(km, qb, _NT, preferred_element_type=F32) for km in kms]
                for hh in range(2):
                    b = bias[hh, ks, :]
                    ps, alphas = [], []
                    for cc in range(NC):
                        cols = slice(cc * LANES, (cc + 1) * LANES)
                        s = sTs[hh][:, cols] + b
                        if masked:
                            s = jnp.where(_causal_t(T, cc), s, -jnp.inf)
                        m_old = m_scr[hh, cc]
                        m_new = jnp.maximum(m_old, _over_keys(s, jnp.maximum))
                        alpha = jnp.exp(m_old - m_new)
                        p = jnp.exp(s - m_new)
                        l_scr[hh, cc] = alpha * l_scr[hh, cc] + _over_keys(p, jnp.add)
                        m_scr[hh, cc] = m_new
                        ps.append(p.astype(BF16))
                        alphas.append(alpha)
                    acc[hh] = acc[hh] * jnp.concatenate(alphas, axis=1) + jnp.dot(
                        vT[kj, hh * HEAD_DIM:(hh + 1) * HEAD_DIM, :], jnp.concatenate(ps, axis=1),
                        preferred_element_type=F32)

            def inner(kj, _):
                tile(kj, False)
                return 0

            lax.fori_loop(0, qi, inner, 0)
            tile(qi, True)
            outs = []
            for hh in range(2):
                inv = jnp.concatenate([1.0 / l_scr[hh, cc] for cc in range(NC)], axis=1)
                outs.append(acc[hh] * inv)
                for cc in range(NC):
                    lse_ref[hh:hh + 1, pl.ds(q0 + cc * LANES, LANES)] = m_scr[hh, cc] + jnp.log(l_scr[hh, cc])
            out = jnp.concatenate(outs, axis=0).T
            o_ref[pl.ds(q0, T), :] = out.astype(o_ref.dtype)
            of_ref[pl.ds(q0, T), :] = out
            return 0

        lax.fori_loop(0, nq, q_step, 0)

    blk = lambda off: pl.BlockSpec((S, LANES), lambda p: (0, off + p))
    return pl.pallas_call(
        body, name=name, grid=(HP,),
        in_specs=[blk(0), blk(0), blk(HP), pl.BlockSpec((None, S, 2), lambda p: (p, 0, 0))],
        out_specs=[blk(0), blk(0), pl.BlockSpec((None, 2, S), lambda p: (p, 0, 0))],
        out_shape=[_hbm_out((S, D), BF16), _hbm_out((S, D), F32),
                   _hbm_out((HP, 2, S), F32)],
        scratch_shapes=[pltpu.VMEM((2, S, LANES), F32), pltpu.VMEM((nq, LANES, T), BF16),
                        pltpu.VMEM((2, HEAD_DIM, T), F32), pltpu.VMEM((2, NC, 1, LANES), F32),
                        pltpu.VMEM((2, NC, 1, LANES), F32)],
        compiler_params=_params(("parallel",)),
    )(q, kv, kv, c3)


def _attn_bwd(name, q, kv, c3, of, do, lse3):
    S, D = q.shape
    HP = D // LANES
    T = _att_tile(S)
    nq = S // T
    NC = T // LANES
    scale = HEAD_DIM ** -0.5

    def body(q_ref, k_ref, v_ref, c_ref, of_ref, do_ref, lse_ref,
             dq_ref, dk_ref, dv_ref, dck_ref, drq_ref, bias, kT, dqT, delta, dr_scr, dk_acc, dv_acc, dc_acc):
        def prologue(i, _):
            rows = pl.ds(pl.multiple_of(i * T, T), T)
            for hh in range(2):
                bias[hh, rows, :] = _key_bias(c_ref, rows, hh)
            kT[i] = k_ref[rows, :].astype(F32).T.astype(BF16)
            prodT = (do_ref[rows, :].astype(F32) * of_ref[rows, :]).T
            for hh in range(2):
                delta[hh:hh + 1, rows] = jnp.sum(prodT[hh * HEAD_DIM:(hh + 1) * HEAD_DIM], axis=0, keepdims=True)
            dqT[i] = jnp.zeros((LANES, T), F32)
            return 0

        lax.fori_loop(0, nq, prologue, 0)
        dr_scr[...] = jnp.zeros(dr_scr.shape, F32)

        def kv_step(kj, _):
            ks = pl.ds(pl.multiple_of(kj * T, T), T)
            kf = k_ref[ks, :].astype(F32)
            vf = v_ref[ks, :].astype(F32)
            first = _head_lanes(kf.shape)
            masks = [first, jnp.logical_not(first)]
            kms = [jnp.where(m, kf, 0.0).astype(BF16) for m in masks]
            vms = [jnp.where(m, vf, 0.0).astype(BF16) for m in masks]

            for acc in (dk_acc, dv_acc, dc_acc):
                acc[...] = jnp.zeros(acc.shape, F32)

            def tile(qi, masked):
                q0 = pl.multiple_of(qi * T, T)
                qb = q_ref[pl.ds(q0, T), :]
                dob = do_ref[pl.ds(q0, T), :]
                sTs = [lax.dot_general(km, qb, _NT, preferred_element_type=F32) for km in kms]
                dpTs = [lax.dot_general(vm, dob, _NT, preferred_element_type=F32) for vm in vms]
                for hh in range(2):
                    b = bias[hh, ks, :]
                    head = slice(hh * HEAD_DIM, (hh + 1) * HEAD_DIM)
                    ps, dss = [], []
                    for cc in range(NC):
                        cols = slice(cc * LANES, (cc + 1) * LANES)
                        at = pl.ds(q0 + cc * LANES, LANES)
                        p = jnp.exp(sTs[hh][:, cols] + b - lse_ref[hh:hh + 1, at])
                        if masked:
                            p = jnp.where(_causal_t(T, cc), p, 0.0)
                        ds = p * (dpTs[hh][:, cols] - delta[hh:hh + 1, at])
                        ps.append(p.astype(BF16))
                        dss.append(ds.astype(BF16))
                        dc_acc[hh] += ds
                        dr_scr[hh:hh + 1, at] += _over_keys(ds, jnp.add)
                    pT = jnp.concatenate(ps, axis=1)
                    dsT = jnp.concatenate(dss, axis=1)
                    dv_acc[hh] += jnp.dot(pT, dob, preferred_element_type=F32)
                    dk_acc[hh] += jnp.dot(dsT, qb, preferred_element_type=F32)
                    dqT[qi, head, :] += jnp.dot(kT[kj, head, :], dsT, preferred_element_type=F32)

            def inner(qi, _):
                tile(qi, False)
                return 0

            tile(kj, True)
            lax.fori_loop(kj + 1, nq, inner, 0)
            dk_ref[ks, :] = jnp.where(first, dk_acc[0], dk_acc[1])
            dv_ref[ks, :] = jnp.where(first, dv_acc[0], dv_acc[1])
            for hh in range(2):
                dck_ref[hh:hh + 1, ks] = -jnp.sum(dc_acc[hh].T, axis=0, keepdims=True)
            return 0

        lax.fori_loop(0, nq, kv_step, 0)

        def epilogue(i, _):
            rows = pl.ds(pl.multiple_of(i * T, T), T)
            dq_ref[rows, :] = (dqT[i].T * scale).astype(dq_ref.dtype)
            return 0

        lax.fori_loop(0, nq, epilogue, 0)
        drq_ref[...] = dr_scr[...]

    blk = lambda off: pl.BlockSpec((S, LANES), lambda p: (0, off + p))
    row_spec = pl.BlockSpec((None, 2, S), lambda p: (p, 0, 0))
    return pl.pallas_call(
        body, name=name, grid=(HP,),
        in_specs=[blk(0), blk(0), blk(HP), pl.BlockSpec((None, S, 2), lambda p: (p, 0, 0)), blk(0), blk(0), row_spec],
        out_specs=[blk(0), blk(0), blk(0), row_spec, row_spec],
        out_shape=[_hbm_out((S, D), BF16), _hbm_out((S, D), F32),
                   _hbm_out((S, D), F32), _hbm_out((HP, 2, S), F32),
                   _hbm_out((HP, 2, S), F32)],
        scratch_shapes=[pltpu.VMEM((2, S, LANES), F32), pltpu.VMEM((nq, LANES, T), BF16),
                        pltpu.VMEM((nq, LANES, T), F32), pltpu.VMEM((2, S), F32), pltpu.VMEM((2, S), F32)]
        + [pltpu.VMEM((2, T, LANES), F32)] * 3,
        compiler_params=_params(("parallel",)),
    )(q, kv, kv, c3, of, do, lse3)


def _logsig_fwd(name, f):
    S, C = f.shape

    def body(f_ref, o_ref):
        o_ref[...] = -_softplus(-f_ref[...])

    spec = pl.BlockSpec((S, C), lambda i: (0, 0))
    return pl.pallas_call(body, name=name, grid=(1,), in_specs=[spec], out_specs=spec,
                          out_shape=_hbm_out((S, C), F32),
                          compiler_params=_params(("arbitrary",)))(f)


def _logsig_bwd(name, dls, f):
    S, C = f.shape

    def body(d_ref, f_ref, o_ref, s_ref):
        df = d_ref[...] * _sigmoid(-f_ref[...])
        o_ref[...] = df.astype(o_ref.dtype)
        s_ref[...] = jnp.sum(df, axis=0, keepdims=True)

    spec = pl.BlockSpec((S, C), lambda i: (0, 0))
    return pl.pallas_call(body, name=name, grid=(1,), in_specs=[spec, spec],
                          out_specs=[spec, pl.BlockSpec((1, C), lambda i: (0, 0))],
                          out_shape=[_hbm_out((S, C), BF16), _hbm_out((1, C), F32)],
                          compiler_params=_params(("arbitrary",)))(dls, f)


def _kv_sum(name, dk_parts, dv_parts, tr=256):
    S, D = dk_parts[0].shape
    tr = _tile(S, tr)
    n = len(dk_parts)

    def body(*refs):
        for half, group in enumerate((refs[:n], refs[n:2 * n])):
            acc = group[0][...]
            for r in group[1:]:
                acc = acc + r[...]
            refs[2 * n][:, half * D:(half + 1) * D] = acc.astype(BF16)

    spec = pl.BlockSpec((tr, D), lambda i: (i, 0))
    return pl.pallas_call(body, name=name, grid=(S // tr,), in_specs=[spec] * (2 * n),
                          out_specs=pl.BlockSpec((tr, 2 * D), lambda i: (i, 0)),
                          out_shape=_hbm_out((S, 2 * D), BF16),
                          compiler_params=_params(("parallel",)))(*dk_parts, *dv_parts)


def _local_step(x, target, gains, layer_weights, layer_prefetch, layer_grads):
    S, D = x.shape
    HP = D // LANES
    scale = HEAD_DIM ** -0.5
    tx = _tile(S, 256)
    saved = []
    h = x
    l = 0
    kv = c3 = f_pre = hn_kv = h_kv = None
    while True:
        W = layer_weights(l, "mix", h)
        if W is None:
            break
        recurrent = "w_rec_in" in W
        if l == 0:
            xn = _rmsnorm_fwd("mix_norm_0", h, gains["mix"][0])
        if recurrent:
            CH = W["w_rec_in"].shape[-1]
            C = 2 * CH
            proj = _mm(f"rec_in_{l}", "nn", xn, W["w_rec_in"], grid=(1, N_CHIPS),
                       a_spec=pl.BlockSpec((S, D), lambda i, j: (0, 0)),
                       b_spec=pl.BlockSpec((None, D, CH), lambda i, j: (j, 0, 0)),
                       out_shape=(S, 2 * C), out_dtype=F32,
                       out_spec=pl.BlockSpec((S, CH), lambda i, j: (0, j)))
            layer_prefetch(l, "mix2", proj)
            rc, rcb = _conv_fwd(f"conv_{l}", proj, W["conv_w"], W["conv_b"])
            W = {**W, **layer_weights(l, "mix2", rcb)}
            gip, grp = _gates_fwd(f"gates_{l}", rcb, W["w_gates"], W["b_gates"])
            hrec, m = _lru_fwd(f"lru_{l}", proj, rc, gip, grp, W["lru_param"])
            layer_prefetch(l, "ffn", m)
            h_mid, hn = _mm_nn(f"rec_out_{l}", m, W["w_rec_out"], out_dtype=F32, res=h, tn=D, norm_gain=gains["ffn"][l])
            mix_saved = (xn, proj, rc, rcb, gip, grp, hrec, m)
        else:
            if "w_kv" in W:
                h_kv = h
                hn_kv = _rmsnorm_fwd("kv_norm", h, W["norm_kv"])
                kv = _mm_nn("kv_proj", hn_kv, W["w_kv"], out_dtype=BF16, tm=1024, tn=1024)
                f_pre = _mm_nn("f_proj", hn_kv, W["w_f"], out_dtype=F32, bias=W["b_f"])
                c = _cumsum_rows("c_cumsum", _logsig_fwd("logsig", f_pre), False)
                c3 = (-c[:, :2 * HP]).reshape(S, HP, 2).transpose(1, 0, 2)
            q = _mm_nn(f"q_proj_{l}", xn, W["w_q"], out_dtype=BF16, scale=scale, tm=1024, tn=1024)
            layer_prefetch(l, "mix2", q)
            o, of, lse = _attn_fwd(f"attn_fwd_{l}", q, kv, c3)
            W = {**W, **layer_weights(l, "mix2", o)}
            layer_prefetch(l, "ffn", o)
            h_mid, hn = _mm_nn(f"o_proj_{l}", o, W["w_o"], out_dtype=F32, res=h, tn=D, norm_gain=gains["ffn"][l])
            mix_saved = (xn, q, o, of, lse)
        W = {**W, **layer_weights(l, "ffn", h_mid)}
        z3, act = _swiglu_fwd(f"ffn_in_{l}", hn, W["w_ffn_in"])
        layer_prefetch(l + 1, "mix", act)
        saved.append((W, h, h_mid, mix_saved, (hn, z3, act)))
        l += 1
        if l < len(gains["mix"]):
            h, xn = _mm_nn(f"ffn_out_{l - 1}", act, W["w_ffn_out"], out_dtype=F32, res=h_mid, tn=D,
                           norm_gain=gains["mix"][l])
        else:
            h = _mm_nn(f"ffn_out_{l - 1}", act, W["w_ffn_out"], out_dtype=F32, res=h_mid, tn=D)

    dh, dhb, dg_final, loss_row = _loss_head("loss_head", h, target, gains["final"])

    dk_parts, dv_parts, dc_parts = [], [], []
    token = None
    for l in reversed(range(len(saved))):
        W, h_in, h_mid, mix_saved, (hn, z3, act) = saved[l]
        recurrent = "w_rec_in" in W
        FH = W["w_ffn_in"].shape[-1]
        G = {}
        norm_ffn = gains["ffn"][l]
        if token is not None:
            norm_ffn = norm_ffn + jnp.minimum(token[:1, :1], 0.0)
        G["w_ffn_out"] = _mm_tn(f"d_ffn_out_{l}", act, dhb, out_dtype=BF16, tn=D)
        dz3 = _swiglu_bwd(f"d_act_{l}", dhb, W["w_ffn_out"], z3)
        G["w_ffn_in"] = _mm(
            f"d_ffn_in_{l}", "tn", hn, dz3, grid=(1, N_CHIPS),
            a_spec=pl.BlockSpec((S, D), lambda i, j: (0, 0)),
            b_spec=pl.BlockSpec((None, S, FH), lambda i, j: (j // 2, 0, j % 2)),
            out_shape=(N_CHIPS, D, FH), out_dtype=BF16,
            out_spec=pl.BlockSpec((None, D, FH), lambda i, j: (j, 0, 0)))
        ffn_token = layer_grads(l, "ffn", G)
        G = {}
        if ffn_token is not None:
            norm_ffn = norm_ffn + jnp.minimum(ffn_token[:1, :1], 0.0)
        dh, dhb, dgp = _mm(f"d_ffn_hn_{l}", "nt", dz3, W["w_ffn_in"], grid=(S // tx, 1),
                           a_spec=[pl.BlockSpec((None, tx, FH), functools.partial(lambda i, j, k: (k // 2, i, k % 2), k=k))
                                   for k in range(N_CHIPS)],
                           b_spec=[pl.BlockSpec((None, D, FH), functools.partial(lambda i, j, k: (k, 0, 0), k=k))
                                   for k in range(N_CHIPS)],
                           out_shape=(S, D), out_dtype=F32, out_spec=pl.BlockSpec((tx, D), lambda i, j: (i, 0)),
                           norm_bwd=(h_mid, norm_ffn, dh))
        G["norm_ffn"] = jnp.sum(dgp, axis=0)
        if recurrent:
            CH = W["w_rec_in"].shape[-1]
            C = 2 * CH
            xn, proj, rc, rcb, gip, grp, hrec, m = mix_saved
            G["w_rec_out"] = _mm_tn(f"d_rec_out_{l}", m, dhb, out_dtype=BF16, tn=D)
            dm = _mm_nt(f"d_m_{l}", dhb, W["w_rec_out"], out_dtype=F32, tn=C)
            dgb, dgi, dgr, drc1, G["b_gi"], G["b_gr"], G["lru_param"] = _lru_bwd(
                f"d_lru_{l}", dm, proj, hrec, rc, gip, grp, W["lru_param"])
            drc, G["w_gates"] = _gates_bwd(f"d_gates_{l}", dgi, dgr, rcb, W["w_gates"], drc1)
            mix_token = layer_grads(l, "mix2", {n: G[n] for n in ("w_rec_out", "w_gates")})
            drec, G["conv_w"], G["conv_b"] = _conv_bwd(f"d_conv_{l}", drc, proj, W["conv_w"])
            dproj = jnp.concatenate([dgb, drec], axis=1)
            norm_mix = gains["mix"][l] if mix_token is None else gains["mix"][l] + jnp.minimum(mix_token[:1, :1], 0.0)
            G["w_rec_in"] = _mm(
                f"d_rec_in_{l}", "tn", xn, dproj, grid=(1, N_CHIPS),
                a_spec=pl.BlockSpec((S, D), lambda i, j: (0, 0)),
                b_spec=pl.BlockSpec((S, CH), lambda i, j: (0, j)),
                out_shape=(N_CHIPS, D, CH), out_dtype=BF16,
                out_spec=pl.BlockSpec((None, D, CH), lambda i, j: (j, 0, 0)))
            dh, dhb, dgp = _mm(f"d_rec_xn_{l}", "nt", dproj, W["w_rec_in"], grid=(S // tx, 1),
                               a_spec=[pl.BlockSpec((tx, CH), functools.partial(lambda i, j, k: (i, k), k=k))
                                       for k in range(N_CHIPS)],
                               b_spec=[pl.BlockSpec((None, D, CH), functools.partial(lambda i, j, k: (k, 0, 0), k=k))
                                       for k in range(N_CHIPS)],
                               out_shape=(S, D), out_dtype=F32, out_spec=pl.BlockSpec((tx, D), lambda i, j: (i, 0)),
                               norm_bwd=(h_in, norm_mix, dh))
        else:
            xn, q, o, of, lse = mix_saved
            G["w_o"] = _mm_tn(f"d_o_proj_{l}", o, dhb, out_dtype=BF16, tn=D)
            do = _mm_nt(f"d_o_{l}", dhb, W["w_o"], out_dtype=BF16, tm=1024, tn=D)
            mix_token = layer_grads(l, "mix2", {"w_o": G["w_o"]})
            dq, dk, dv, dck, drq = _attn_bwd(f"attn_bwd_{l}", q, kv, c3, of, do, lse)
            dk_parts.append(dk)
            dv_parts.append(dv)
            dc_parts.append((dck + drq).reshape(2 * HP, S).T)
            G["w_q"] = _mm_tn(f"d_q_proj_{l}", xn, dq, out_dtype=BF16, tn=D)
            norm_mix = gains["mix"][l] if mix_token is None else gains["mix"][l] + jnp.minimum(mix_token[:1, :1], 0.0)
            dh, dhb, dgp = _mm_nt(f"d_q_xn_{l}", dq, W["w_q"], out_dtype=F32, tn=D, norm_bwd=(h_in, norm_mix, dh))
        G["norm_mix"] = jnp.sum(dgp, axis=0)
        if "w_kv" in W:
            dkv = _kv_sum("dkv_sum", dk_parts, dv_parts)
            dc = sum(dc_parts[1:], dc_parts[0])
            dc_pad = jnp.pad(dc, ((0, 0), (0, LANES - 2 * HP)))
            dls = _cumsum_rows("dc_cumsum", dc_pad, True)
            dfb, G["b_f"] = _logsig_bwd("d_logsig", dls, f_pre)
            G["w_kv"] = _mm_tn("d_kv_proj", hn_kv, dkv, out_dtype=BF16, tn=1024)
            G["w_f"] = _mm_tn("d_f_proj", hn_kv, dfb, out_dtype=F32)
            dhn_f = _mm_nt("d_f_hn", dfb, W["w_f"], out_dtype=F32, tn=D)
            dh, dhb, dgp = _mm_nt("d_kv_hn", dkv, W["w_kv"], out_dtype=F32, tn=D, res=dhn_f,
                                  norm_bwd=(h_kv, W["norm_kv"], dh))
            G["norm_kv"] = jnp.sum(dgp, axis=0)
        token = layer_grads(l, "mix", G)
    return loss_row, dh, dg_final


_ANY = pl.BlockSpec(memory_space=pl.ANY)


def _position():
    return lax.axis_index("x"), lax.axis_index("y"), lax.axis_index("c")


def _chip_peers(x, y):
    return [(1 - x, y), (x, 1 - y), (1 - x, 1 - y)]


def _half_rows(c, n):
    h = n // 2
    assert h % 16 == 0
    return pl.ds(pl.multiple_of(c * h, 16), h)


def _place_own(name, shard, layer, me):
    _, R, C = shard.shape
    tr = _row_tile(R, C, 2 * shard.dtype.itemsize, target=8 << 20)

    def body(me_ref, x_ref, o_ref):
        o_ref[...] = x_ref[...]

    return pl.pallas_call(
        body, name=name,
        grid_spec=pltpu.PrefetchScalarGridSpec(
            num_scalar_prefetch=1, grid=(R // tr,),
            in_specs=[pl.BlockSpec((None, tr, C), lambda i, me_ref: (layer, i, 0))],
            out_specs=pl.BlockSpec((None, tr, C), lambda i, me_ref: (me_ref[0], i, 0))),
        out_shape=_hbm_out((N_CHIPS, R, C), shard.dtype),
        compiler_params=_params(("parallel",)),
    )(me, shard)


def _gather_smalls(name, smalls):
    ns = len(smalls)

    def body(*refs):
        ins, outs = refs[:ns], refs[ns:2 * ns]
        send_sems, recv_sems, local_sems = refs[2 * ns:]
        x, y, c = _position()
        me = 2 * x + y
        peers = _chip_peers(x, y)

        def remote(t, k, chip):
            px, py = peers[k]
            return pltpu.make_async_remote_copy(
                src_ref=ins[t], dst_ref=outs[t].at[chip], send_sem=send_sems.at[3 * t + k],
                recv_sem=recv_sems.at[3 * t + k], device_id=(px, py, c), device_id_type=MESH)

        local = [pltpu.make_async_copy(ins[t], outs[t].at[me], local_sems.at[t]) for t in range(ns)]
        for t in range(ns):
            local[t].start()
            for k in range(3):
                remote(t, k, me).start()
        for t in range(ns):
            for k in range(3):
                px, py = peers[k]
                remote(t, k, 2 * px + py).wait_recv()
        for t in range(ns):
            for k in range(3):
                remote(t, k, me).wait_send()
            local[t].wait()

    return pl.pallas_call(
        body, name=name, in_specs=[_ANY] * ns, out_specs=[_ANY] * ns,
        out_shape=[_hbm_out((N_CHIPS,) + s.shape, s.dtype) for s in smalls],
        scratch_shapes=[pltpu.SemaphoreType.DMA((3 * ns,)), pltpu.SemaphoreType.DMA((3 * ns,)),
                        pltpu.SemaphoreType.DMA((ns,))],
    )(*smalls)


_SEM = pl.BlockSpec(memory_space=pltpu.SEMAPHORE)
_SPLIT = pltpu.CompilerParams(has_side_effects=pltpu.SideEffectType.DATAFLOW_SIDE_EFFECTING)


def _weight_copy(shards, buf, items, sems, i, k, chip_of_dst, peers, c):
    w, l = items[i]
    px, py = peers[k]
    half = _half_rows(c, shards[w].shape[1])
    return pltpu.make_async_remote_copy(
        src_ref=shards[w].at[l, half], dst_ref=buf.at[chip_of_dst, half],
        send_sem=sems[0].at[3 * i + k], recv_sem=sems[1].at[3 * i + k],
        device_id=(px, py, c), device_id_type=MESH)


def _gather_start(name, shards, bufs, items, after):
    nw, n = len(shards), len(bufs)

    def body(*refs):
        ins, outs, sems = refs[:nw], refs[nw + n + 1:nw + 2 * n + 1], refs[nw + 2 * n + 1:]
        x, y, c = _position()
        peers = _chip_peers(x, y)
        for i in range(n):
            for k in range(3):
                _weight_copy(ins, outs[i], items, sems, i, k, 2 * x + y, peers, c).start()

    res = pl.pallas_call(
        body, name=name, in_specs=[_ANY] * (nw + n + 1), out_specs=[_ANY] * n + [_SEM, _SEM],
        out_shape=[_hbm_out(b.shape, b.dtype) for b in bufs]
        + [pltpu.SemaphoreType.DMA((3 * n,)), pltpu.SemaphoreType.DMA((3 * n,))],
        input_output_aliases={nw + i: i for i in range(n)}, compiler_params=_SPLIT,
    )(*shards, *bufs, after)
    return res[:n], res[n:]


def _gather_wait(name, shards, bufs, items, ids, sems, after):
    nw, m = len(shards), len(ids)

    def body(*refs):
        ins, bs = refs[:nw], refs[nw:nw + m]
        sem_refs = refs[nw + m:nw + m + 2]
        x, y, c = _position()
        peers = _chip_peers(x, y)
        for j, i in enumerate(ids):
            for k in range(3):
                px, py = peers[k]
                _weight_copy(ins, bs[j], items, sem_refs, i, k, 2 * px + py, peers, c).wait_recv()
        for j, i in enumerate(ids):
            for k in range(3):
                _weight_copy(ins, bs[j], items, sem_refs, i, k, 2 * x + y, peers, c).wait_send()

    res = pl.pallas_call(
        body, name=name, in_specs=[_ANY] * (nw + m) + [_SEM, _SEM, _ANY], out_specs=[_ANY] * m,
        out_shape=[_hbm_out(bufs[i].shape, bufs[i].dtype) for i in ids],
        input_output_aliases={nw + j: j for j in range(m)}, compiler_params=_SPLIT,
    )(*shards, *[bufs[i] for i in ids], *sems, after)
    return list(res)


def _forward_copy(src, dst, sems, i, k, core):
    x, y, c = _position()
    px, py = _chip_peers(x, y)[k]
    half = _half_rows(core, src.shape[1])
    return pltpu.make_async_remote_copy(
        src_ref=src.at[2 * px + py, half], dst_ref=dst.at[2 * px + py, half],
        send_sem=sems[0].at[3 * i + k], recv_sem=sems[1].at[3 * i + k],
        device_id=(x, y, 1 - c), device_id_type=MESH)


def _forward_start(name, bufs):
    n = len(bufs)

    def body(*refs):
        ins, outs, sems = refs[:n], refs[n:2 * n], refs[2 * n:]
        c = lax.axis_index("c")
        for i in range(n):
            for k in range(3):
                _forward_copy(ins[i], outs[i], sems, i, k, c).start()

    res = pl.pallas_call(
        body, name=name, in_specs=[_ANY] * n, out_specs=[_ANY] * n + [_SEM, _SEM],
        out_shape=[_hbm_out(g.shape, g.dtype) for g in bufs]
        + [pltpu.SemaphoreType.DMA((3 * n,)), pltpu.SemaphoreType.DMA((3 * n,))],
        input_output_aliases={i: i for i in range(n)}, compiler_params=_SPLIT,
    )(*bufs)
    return list(res[:n]), res[n:]


def _forward_wait(name, bufs, sems, after):
    n = len(bufs)

    def body(*refs):
        bs, sem_refs = refs[:n], refs[n:n + 2]
        c = lax.axis_index("c")
        for i in range(n):
            for k in range(3):
                _forward_copy(bs[i], bs[i], sem_refs, i, k, 1 - c).wait_recv()
        for i in range(n):
            for k in range(3):
                _forward_copy(bs[i], bs[i], sem_refs, i, k, c).wait_send()

    return list(pl.pallas_call(
        body, name=name, in_specs=[_ANY] * n + [_SEM, _SEM, _ANY], out_specs=[_ANY] * n,
        out_shape=[_hbm_out(g.shape, g.dtype) for g in bufs],
        input_output_aliases={i: i for i in range(n)}, compiler_params=_SPLIT,
    )(*bufs, *sems, after))


def _reduce_copy(grads, others, sems, i):
    x, y, c = _position()
    return pltpu.make_async_remote_copy(
        src_ref=grads[i].at[:, _half_rows(1 - c, grads[i].shape[1])], dst_ref=others[i],
        send_sem=sems[0].at[i], recv_sem=sems[1].at[i], device_id=(x, y, 1 - c), device_id_type=MESH)


def _reduce_start(name, grads, after):
    n = len(grads)

    def body(*refs):
        ins, outs, sems, token = refs[:n], refs[n + 1:2 * n + 1], refs[2 * n + 1:2 * n + 3], refs[2 * n + 3]
        for i in range(n):
            _reduce_copy(ins, outs, sems, i).start()
        token[...] = jnp.zeros_like(token)

    res = pl.pallas_call(
        body, name=name, in_specs=[_ANY] * (n + 1),
        out_specs=[_ANY] * n + [_SEM, _SEM, pl.BlockSpec(memory_space=pltpu.VMEM)],
        out_shape=[_hbm_out((N_CHIPS, g.shape[1] // 2, g.shape[2]), g.dtype) for g in grads]
        + [pltpu.SemaphoreType.DMA((n,)), pltpu.SemaphoreType.DMA((n,)), jax.ShapeDtypeStruct((SUBLANES, LANES), F32)],
        compiler_params=_SPLIT,
    )(*grads, after)
    return list(res[:n]), res[n:n + 2], res[n + 2]


def _reduce_wait(name, grads, others, sems, after):
    n = len(grads)

    def body(*refs):
        ins, os_, sem_refs = refs[:n], refs[n:2 * n], refs[2 * n:2 * n + 2]
        for i in range(n):
            _reduce_copy(ins, os_, sem_refs, i).wait_recv()
        for i in range(n):
            _reduce_copy(ins, os_, sem_refs, i).wait_send()

    return list(pl.pallas_call(
        body, name=name, in_specs=[_ANY] * (2 * n) + [_SEM, _SEM, _ANY], out_specs=[_ANY] * n,
        out_shape=[_hbm_out(o.shape, o.dtype) for o in others],
        input_output_aliases={n + i: i for i in range(n)}, compiler_params=_SPLIT,
    )(*grads, *others, *sems, after))


def _sum_cores(name, g, other, core):
    _, R, C = g.shape
    H = R // 2
    tr = _row_tile(H, C, 3 * 2, target=12 << 20)
    nb = H // tr

    def body(c_ref, g_ref, o_ref, out_ref):
        out_ref[...] = (g_ref[...].astype(F32) + o_ref[...].astype(F32)).astype(out_ref.dtype)

    return pl.pallas_call(
        body, name=name,
        grid_spec=pltpu.PrefetchScalarGridSpec(
            num_scalar_prefetch=1, grid=(N_CHIPS, nb),
            in_specs=[pl.BlockSpec((None, tr, C), lambda j, i, c_ref: (j, c_ref[0] * nb + i, 0)),
                      pl.BlockSpec((None, tr, C), lambda j, i, c_ref: (j, i, 0))],
            out_specs=pl.BlockSpec((None, tr, C), lambda j, i, c_ref: (j, i, 0))),
        out_shape=_hbm_out((N_CHIPS, H, C), BF16),
        compiler_params=_params(("parallel", "parallel")),
    )(core, g, other)


def _sum_chips(name, received, own, full, layer, me_core):
    _, H, C = received.shape
    tr = _row_tile(H, C, 3 * 2 + 2 + 4, target=12 << 20)
    nb = H // tr

    def body(s_ref, r_ref, own_ref, full_ref, out_ref):
        acc = r_ref[0].astype(F32)
        for k in (1, 2):
            acc = acc + r_ref[k].astype(F32)
        out_ref[...] = acc + own_ref[...].astype(F32)

    return pl.pallas_call(
        body, name=name,
        grid_spec=pltpu.PrefetchScalarGridSpec(
            num_scalar_prefetch=1, grid=(nb,),
            in_specs=[pl.BlockSpec((3, tr, C), lambda i, s_ref: (0, i, 0)),
                      pl.BlockSpec((None, tr, C), lambda i, s_ref: (s_ref[0], i, 0)),
                      _ANY],
            out_specs=pl.BlockSpec((None, tr, C), lambda i, s_ref: (layer, s_ref[1] * nb + i, 0))),
        out_shape=_hbm_out(full.shape, full.dtype),
        input_output_aliases={3: 0},
        compiler_params=_params(("parallel",)),
    )(me_core, received, own, full)


def _part_copy(parts, recv, sems, i, k, peers, c):
    px, py = peers[k]
    return pltpu.make_async_remote_copy(
        src_ref=parts[i].at[2 * px + py], dst_ref=recv[i].at[k],
        send_sem=sems[0].at[3 * i + k], recv_sem=sems[1].at[3 * i + k],
        device_id=(px, py, c), device_id_type=MESH)


def _scatter_start(name, parts):
    n = len(parts)

    def body(*refs):
        ins, outs, sems, token = refs[:n], refs[n:2 * n], refs[2 * n:2 * n + 2], refs[2 * n + 2]
        x, y, c = _position()
        peers = _chip_peers(x, y)
        for i in range(n):
            for k in range(3):
                _part_copy(ins, outs, sems, i, k, peers, c).start()
        token[...] = jnp.zeros_like(token)

    res = pl.pallas_call(
        body, name=name, in_specs=[_ANY] * n,
        out_specs=[_ANY] * n + [_SEM, _SEM, pl.BlockSpec(memory_space=pltpu.VMEM)],
        out_shape=[_hbm_out((3,) + p.shape[1:], p.dtype) for p in parts]
        + [pltpu.SemaphoreType.DMA((3 * n,)), pltpu.SemaphoreType.DMA((3 * n,)),
           jax.ShapeDtypeStruct((SUBLANES, LANES), F32)],
        compiler_params=_SPLIT,
    )(*parts)
    return list(res[:n]), res[n:n + 2], res[n + 2]


def _scatter_reduce_start(name, parts, grads):
    n, m = len(parts), len(grads)

    def body(*refs):
        ps, gs = refs[:n], refs[n:n + m]
        recv, others = refs[n + m:2 * n + m], refs[2 * n + m:2 * (n + m)]
        ssems, rsems, token = refs[2 * (n + m):2 * (n + m) + 2], refs[2 * (n + m) + 2:2 * (n + m) + 4], refs[-1]
        x, y, c = _position()
        peers = _chip_peers(x, y)
        for i in range(n):
            for k in range(3):
                _part_copy(ps, recv, ssems, i, k, peers, c).start()
        for i in range(m):
            _reduce_copy(gs, others, rsems, i).start()
        token[...] = jnp.zeros_like(token)

    res = pl.pallas_call(
        body, name=name, in_specs=[_ANY] * (n + m),
        out_specs=[_ANY] * (n + m) + [_SEM] * 4 + [pl.BlockSpec(memory_space=pltpu.VMEM)],
        out_shape=[_hbm_out((3,) + p.shape[1:], p.dtype) for p in parts]
        + [_hbm_out((N_CHIPS, g.shape[1] // 2, g.shape[2]), g.dtype) for g in grads]
        + [pltpu.SemaphoreType.DMA((3 * n,)), pltpu.SemaphoreType.DMA((3 * n,)),
           pltpu.SemaphoreType.DMA((m,)), pltpu.SemaphoreType.DMA((m,)), jax.ShapeDtypeStruct((SUBLANES, LANES), F32)],
        compiler_params=_SPLIT,
    )(*parts, *grads)
    k = n + m
    return (list(res[:n]), res[k:k + 2]), (list(res[n:k]), res[k + 2:k + 4]), res[k + 4]


def _scatter_wait(name, parts, recv, sems, after):
    n = len(parts)

    def body(*refs):
        ins, rs, sem_refs = refs[:n], refs[n:2 * n], refs[2 * n:2 * n + 2]
        x, y, c = _position()
        peers = _chip_peers(x, y)
        for i in range(n):
            for k in range(3):
                _part_copy(ins, rs, sem_refs, i, k, peers, c).wait_recv()
        for i in range(n):
            for k in range(3):
                _part_copy(ins, rs, sem_refs, i, k, peers, c).wait_send()

    return list(pl.pallas_call(
        body, name=name, in_specs=[_ANY] * (2 * n) + [_SEM, _SEM, _ANY], out_specs=[_ANY] * n,
        out_shape=[_hbm_out(r.shape, r.dtype) for r in recv],
        input_output_aliases={n + i: i for i in range(n)}, compiler_params=_SPLIT,
    )(*parts, *recv, *sems, after))


def _share_d2d(name, full):
    n = len(full)

    def body(*refs):
        ins, outs = refs[:n], refs[n:2 * n]
        send_sems, recv_sems = refs[2 * n:]
        x, y, c = _position()

        def remote(w, core):
            half = _half_rows(core, ins[w].shape[1])
            return pltpu.make_async_remote_copy(
                src_ref=ins[w].at[:, half], dst_ref=outs[w].at[:, half],
                send_sem=send_sems.at[w], recv_sem=recv_sems.at[w],
                device_id=(x, y, 1 - c), device_id_type=MESH)

        for w in range(n):
            remote(w, c).start()
        for w in range(n):
            remote(w, 1 - c).wait_recv()
        for w in range(n):
            remote(w, c).wait_send()

    return pl.pallas_call(
        body, name=name, in_specs=[_ANY] * n, out_specs=[_ANY] * n,
        out_shape=[_hbm_out(f.shape, f.dtype) for f in full],
        input_output_aliases={w: w for w in range(n)},
        scratch_shapes=[pltpu.SemaphoreType.DMA((n,)), pltpu.SemaphoreType.DMA((n,))],
    )(*full)


def _all_copy(a_ref, o_ref, sems, k, slot):
    x, y, c = _position()
    return pltpu.make_async_remote_copy(
        src_ref=a_ref, dst_ref=o_ref.at[slot], send_sem=sems[0].at[k - 1], recv_sem=sems[1].at[k - 1],
        device_id=(x ^ ((k >> 2) & 1), y ^ ((k >> 1) & 1), c ^ (k & 1)), device_id_type=MESH)


def _gather_all_start(name, a):
    def body(a_ref, o_ref, send_sem, recv_sem, token):
        x, y, c = _position()
        for k in range(1, N_DEV):
            _all_copy(a_ref, o_ref, (send_sem, recv_sem), k, 4 * x + 2 * y + c).start()
        token[...] = jnp.zeros_like(token)

    out, send_sem, recv_sem, token = pl.pallas_call(
        body, name=name, in_specs=[_ANY], out_specs=[_ANY, _SEM, _SEM, pl.BlockSpec(memory_space=pltpu.VMEM)],
        out_shape=[_hbm_out((N_DEV,) + a.shape, a.dtype), pltpu.SemaphoreType.DMA((N_DEV - 1,)),
                   pltpu.SemaphoreType.DMA((N_DEV - 1,)), jax.ShapeDtypeStruct((SUBLANES, LANES), F32)],
        compiler_params=_SPLIT,
    )(a)
    return out, (send_sem, recv_sem), token


def _gather_all_wait(name, a, out, sems, after):
    def body(a_ref, o_ref, send_sem, recv_sem, after_ref, res_ref):
        x, y, c = _position()
        for k in range(1, N_DEV):
            peer = 4 * (x ^ ((k >> 2) & 1)) + 2 * (y ^ ((k >> 1) & 1)) + (c ^ (k & 1))
            _all_copy(a_ref, o_ref, (send_sem, recv_sem), k, peer).wait_recv()
        for k in range(1, N_DEV):
            _all_copy(a_ref, o_ref, (send_sem, recv_sem), k, 4 * x + 2 * y + c).wait_send()

    return pl.pallas_call(
        body, name=name, in_specs=[_ANY, _ANY, _SEM, _SEM, _ANY], out_specs=_ANY,
        out_shape=_hbm_out(out.shape, out.dtype), input_output_aliases={1: 0}, compiler_params=_SPLIT,
    )(a, out, *sems, after)


def _rows2d(a, lead=0):
    return a.reshape(a.shape[:lead] + (-1, a.shape[-1]))


def _row_tile(rows, cols, itemsize=4, target=1 << 20):
    want = max(SUBLANES, target // (cols * itemsize))
    t = min(rows, (want // 16) * 16)
    while t > 16 and rows % t:
        t -= 16
    return t if rows % t == 0 else rows


def _sum_slots(name, r, out_dtype=F32):
    ns = r.shape[0]
    r2 = _rows2d(r, 1)
    _, rows, cols = r2.shape
    tr = _row_tile(rows, cols)

    def body(r_ref, o_ref):
        acc = r_ref[0].astype(F32)
        for s in range(1, ns):
            acc = acc + r_ref[s].astype(F32)
        o_ref[...] = acc.astype(o_ref.dtype)

    out = pl.pallas_call(
        body, name=name, grid=(rows // tr,),
        in_specs=[pl.BlockSpec((ns, tr, cols), lambda i: (0, i, 0))],
        out_specs=pl.BlockSpec((tr, cols), lambda i: (i, 0)),
        out_shape=_hbm_out((rows, cols), out_dtype),
        compiler_params=_params(("parallel",)),
    )(r2)
    return out.reshape(r.shape[1:])


def _adamw(name, g_parts, w, m, v):
    shape = w.shape
    ng = len(g_parts)
    args = [_rows2d(a) for a in (*g_parts, w, m, v)]
    rows, cols = args[0].shape
    tr = _row_tile(rows, cols, (ng + 7) * 4, target=16 << 20)
    c1 = 1.0 - ADAM_B1 ** ADAM_STEP
    c2 = 1.0 - ADAM_B2 ** ADAM_STEP

    def body(*refs):
        g = refs[0][...]
        for r in refs[1:ng]:
            g = g + r[...]
        w_ref, m_ref, v_ref = refs[ng:ng + 3]
        g_out, d_out, m_out, v_out = refs[ng + 3:]
        mn = ADAM_B1 * m_ref[...] + (1.0 - ADAM_B1) * g
        vn = ADAM_B2 * v_ref[...] + (1.0 - ADAM_B2) * (g * g)
        m_hat = mn / c1
        v_hat = vn / c2
        g_out[...] = g
        d_out[...] = -ADAM_LR * (m_hat / (jnp.sqrt(v_hat) + ADAM_EPS) + ADAM_WD * w_ref[...])
        m_out[...] = mn
        v_out[...] = vn

    spec = pl.BlockSpec((tr, cols), lambda i: (i, 0))
    outs = pl.pallas_call(
        body, name=name, grid=(rows // tr,), in_specs=[spec] * (ng + 3), out_specs=[spec] * 4,
        out_shape=[_hbm_out((rows, cols), F32)] * 4,
        compiler_params=_params(("parallel",)),
    )(*args)
    return tuple(o.reshape(shape) for o in outs)


_WEIGHTS = ["norm_mix", "norm_ffn", "w_ffn_in", "w_ffn_out", "w_rec_in", "conv_w", "conv_b", "w_lru_gates",
            "b_lru_gates", "lru_param", "w_rec_out", "norm_kv", "w_kvf", "b_forget", "w_q", "w_o", "norm_final"]
_BIG = ["w_ffn_in", "w_ffn_out", "w_rec_in", "w_lru_gates", "w_rec_out", "w_kvf", "w_q", "w_o"]


def _stack3(a):
    return a[None] if a.ndim == 2 else a.reshape(a.shape[0], -1, a.shape[-1])


def _pad_lanes(a, n):
    return jnp.pad(a, ((0, 0),) * (a.ndim - 1) + ((0, n - a.shape[-1]),))


def kernel(x, norm_mix, norm_ffn, w_ffn_in, w_ffn_out, w_rec_in, conv_w, conv_b, w_lru_gates, b_lru_gates, lru_param, w_rec_out, norm_kv, w_kvf, b_forget, w_q, w_o, norm_final, loss_target, m_norm_mix, m_norm_ffn, m_w_ffn_in, m_w_ffn_out, m_w_rec_in, m_conv_w, m_conv_b, m_w_lru_gates, m_b_lru_gates, m_lru_param, m_w_rec_out, m_norm_kv, m_w_kvf, m_b_forget, m_w_q, m_w_o, m_norm_final, v_norm_mix, v_norm_ffn, v_w_ffn_in, v_w_ffn_out, v_w_rec_in, v_conv_w, v_conv_b, v_w_lru_gates, v_b_lru_gates, v_lru_param, v_w_rec_out, v_norm_kv, v_w_kvf, v_b_forget, v_w_q, v_w_o, v_norm_final):
    P = dict(norm_mix=norm_mix, norm_ffn=norm_ffn, w_ffn_in=w_ffn_in, w_ffn_out=w_ffn_out, w_rec_in=w_rec_in,
             conv_w=conv_w, conv_b=conv_b, w_lru_gates=w_lru_gates, b_lru_gates=b_lru_gates, lru_param=lru_param,
             w_rec_out=w_rec_out, norm_kv=norm_kv, w_kvf=w_kvf, b_forget=b_forget, w_q=w_q, w_o=w_o,
             norm_final=norm_final)
    M1 = dict(norm_mix=m_norm_mix, norm_ffn=m_norm_ffn, w_ffn_in=m_w_ffn_in, w_ffn_out=m_w_ffn_out,
              w_rec_in=m_w_rec_in, conv_w=m_conv_w, conv_b=m_conv_b, w_lru_gates=m_w_lru_gates,
              b_lru_gates=m_b_lru_gates, lru_param=m_lru_param, w_rec_out=m_w_rec_out, norm_kv=m_norm_kv,
              w_kvf=m_w_kvf, b_forget=m_b_forget, w_q=m_w_q, w_o=m_w_o, norm_final=m_norm_final)
    M2 = dict(norm_mix=v_norm_mix, norm_ffn=v_norm_ffn, w_ffn_in=v_w_ffn_in, w_ffn_out=v_w_ffn_out,
              w_rec_in=v_w_rec_in, conv_w=v_conv_w, conv_b=v_conv_b, w_lru_gates=v_w_lru_gates,
              b_lru_gates=v_b_lru_gates, lru_param=v_lru_param, w_rec_out=v_w_rec_out, norm_kv=v_norm_kv,
              w_kvf=v_w_kvf, b_forget=v_b_forget, w_q=v_w_q, w_o=v_w_o, norm_final=v_norm_final)

    _, S, D = x.shape
    L = norm_mix.shape[0]
    NA, NBLK, BW, GS = w_lru_gates.shape
    C = NBLK * BW
    CS = C // N_CHIPS
    H = b_forget.shape[0]
    assert C == D and H * HEAD_DIM == D and H <= LANES
    chip = 2 * lax.axis_index("x") + lax.axis_index("y")

    small_a = jnp.concatenate([conv_w, conv_b[:, None], lru_param[:, None]], axis=1)
    small_a, b_gates = _gather_smalls("gather_smalls", [small_a, b_lru_gates])
    small_a = small_a.transpose(1, 2, 0, 3).reshape(NA, 6, C)
    b_gates = b_gates.transpose(1, 2, 0, 3).reshape(NA, NBLK, 1, N_CHIPS * GS)
    shards = [_stack3(P[w]).astype(BF16) for w in _BIG]
    core = lax.axis_index("c")
    chip_id = jnp.reshape(chip, (1,)).astype(jnp.int32)
    core_id = jnp.reshape(core, (1,)).astype(jnp.int32)
    me_core = jnp.stack([chip, core]).astype(jnp.int32)

    parts_of_layer = ("mix", "mix2", "ffn")

    def part_items(l, part):
        if part == "ffn":
            names, at = ["w_ffn_in", "w_ffn_out"], l
        elif l < NA:
            names, at = (["w_rec_in"] if part == "mix" else ["w_lru_gates", "w_rec_out"]), l
        else:
            names, at = ((["w_kvf"] if l == NA else []) + ["w_q"] if part == "mix" else ["w_o"]), l - NA
        return [(_BIG.index(n), 0 if n == "w_kvf" else at) for n in names]

    def stage_of(l, part):
        return (l, part) if l == 0 or part == "ffn" else (l, "mixer")

    def stage_items(st):
        l, part = st
        return [it for p in (("mix", "mix2") if part == "mixer" else (part,)) for it in part_items(l, p)]

    stages = [(0, p) for p in parts_of_layer] + [(l, p) for l in range(1, L) for p in ("mixer", "ffn")]
    items = [it for st in stages for it in stage_items(st)]
    ids_of = {st: [items.index(it) for it in stage_items(st)] for st in stages}
    bufs = [_place_own(f"place_{_BIG[w]}_{li}", shards[w], li, chip_id) for w, li in items]
    bufs, gather_sems = _gather_start("gather_start", shards, bufs, items, small_a)

    forwarding, fetched = {}, {}

    def layer_prefetch(l, part, after):
        st = stage_of(l, part)
        if l < L and st not in forwarding:
            got = _gather_wait(f"gather_wait_{st[1]}_{l}", shards, bufs, items, ids_of[st], gather_sems, after)
            forwarding[st] = _forward_start(f"forward_start_{st[1]}_{l}", got)

    def layer_weights(l, part, after):
        if l >= L:
            return None
        st = stage_of(l, part)
        if st not in fetched:
            layer_prefetch(l, part, after)
            got, sems = forwarding[st]
            got = _forward_wait(f"forward_wait_{st[1]}_{l}", got, sems, after)
            fetched[st] = {_BIG[items[i][0]]: g for i, g in zip(ids_of[st], got)}
        B = fetched[st]
        if part == "ffn":
            return dict(w_ffn_in=B["w_ffn_in"], w_ffn_out=B["w_ffn_out"].reshape(-1, D))
        if l < NA and part == "mix":
            return dict(w_rec_in=B["w_rec_in"], conv_w=small_a[l, :4], conv_b=small_a[l, 4:5])
        if l < NA:
            return dict(w_gates=B["w_lru_gates"].reshape(N_CHIPS, NBLK, BW, GS).transpose(1, 2, 0, 3).reshape(
                NBLK, BW, N_CHIPS * GS), b_gates=b_gates[l], w_rec_out=B["w_rec_out"].reshape(C, D),
                lru_param=small_a[l, 5:6])
        if part == "mix2":
            return dict(w_o=B["w_o"].reshape(D, D))
        W = dict(w_q=B["w_q"].reshape(D, D))
        if l == NA:
            w_kvf_full = B["w_kvf"].transpose(1, 0, 2).reshape(D, -1)
            W.update(norm_kv=norm_kv[None], w_kv=w_kvf_full[:, :2 * D],
                     w_f=_pad_lanes(w_kvf_full[:, 2 * D:], LANES), b_f=_pad_lanes(b_forget[None], LANES))
        return W

    G_small = {l: {} for l in range(L)}
    stash = {st: {} for st in stages}
    pending = {}
    reducing = []

    def finish_reduce(after):
        st, its, grads, others, sems = reducing.pop()
        l, part = st
        others = _reduce_wait(f"reduce_wait_{part}_{l}", grads, others, sems, after)
        parts = [_sum_cores(f"sum_cores_{l}_{_BIG[w]}", g, o, core_id) for (w, _), g, o in zip(its, grads, others)]
        recv, sems, token = _scatter_start(f"scatter_start_{part}_{l}", parts)
        pending[st] = (parts, recv, sems)
        return token

    def layer_grads(l, part, G_part):
        G_small[l].update(G_part)
        st = stage_of(l, part)
        stash[st].update(G_part)
        if st[1] == "mixer" and part != "mix":
            return None
        G = stash[st]
        late = {"ffn": "w_ffn_in", "mix": "norm_mix"}.get(part) or ("w_gates" if l < NA else "w_o")
        by_name = dict(
            w_ffn_in=lambda: G["w_ffn_in"], w_ffn_out=lambda: G["w_ffn_out"].reshape(N_CHIPS, -1, D),
            w_rec_in=lambda: G["w_rec_in"],
            w_lru_gates=lambda: G["w_gates"].reshape(NBLK, BW, N_CHIPS, GS).transpose(2, 0, 1, 3).reshape(
                N_CHIPS, NBLK * BW, GS),
            w_rec_out=lambda: G["w_rec_out"].reshape(N_CHIPS, -1, D),
            w_kvf=lambda: jnp.concatenate([G["w_kv"].astype(F32), G["w_f"][:, :H]], axis=1).reshape(
                D, N_CHIPS, -1).transpose(1, 0, 2).astype(BF16),
            w_q=lambda: G["w_q"].reshape(N_CHIPS, -1, D), w_o=lambda: G["w_o"].reshape(N_CHIPS, -1, D))
        its = stage_items(st)
        grads = [by_name[_BIG[w]]() for w, _ in its]
        if reducing:
            pst, pits, pgrads, pothers, psems = reducing.pop()
            pothers = _reduce_wait(f"reduce_wait_{pst[1]}_{pst[0]}", pgrads, pothers, psems, G_part[late])
            pparts = [_sum_cores(f"sum_cores_{pst[0]}_{_BIG[w]}", g, o, core_id)
                      for (w, _), g, o in zip(pits, pgrads, pothers)]
            (recv, ssems), (others, sems), token = _scatter_reduce_start(
                f"scatter_reduce_start_{st[1]}_{l}", pparts, grads)
            pending[pst] = (pparts, recv, ssems)
        else:
            others, sems, token = _reduce_start(f"reduce_start_{st[1]}_{l}", grads, jnp.zeros((SUBLANES, LANES), F32))
        reducing.append((st, its, grads, others, sems))
        return finish_reduce(token) if l == 0 else token

    gains = dict(mix=[norm_mix[l][None] for l in range(L)], ffn=[norm_ffn[l][None] for l in range(L)],
                 final=norm_final[None])
    loss_row, grad_x, dg_final = _local_step(x.reshape(S, D), loss_target.reshape(S, D), gains,
                                             layer_weights, layer_prefetch, layer_grads)

    rows = [*[G_small[l]["norm_mix"] for l in range(L)], *[G_small[l]["norm_ffn"] for l in range(L)],
            G_small[NA]["norm_kv"], dg_final, _pad_lanes(G_small[NA]["b_f"], D), _pad_lanes(loss_row, D)]
    for a in range(NA):
        rows += [G_small[a][n] for n in ("conv_w", "conv_b", "b_gi", "b_gr", "lru_param")]
    packed = jnp.concatenate(rows, axis=0)
    everyone, small_sems, small_token = _gather_all_start("gather_small_start", packed)

    full = [lax.empty(sh.shape, F32) for sh in shards]
    for st in reversed(stages):
        l, part = st
        parts, recv, sems = pending[st]
        recv = _scatter_wait(f"scatter_wait_{part}_{l}", parts, recv, sems, small_token)
        for (w, li), own, r in zip(stage_items(st), parts, recv):
            full[w] = _sum_chips(f"sum_chips_{l}_{_BIG[w]}", r, own, full[w], li, me_core)
    full = _share_d2d("share_d2d", full)
    big = {w: _adamw(f"adamw_{w}", [g.reshape(P[w].shape)], P[w], M1[w], M2[w]) for w, g in zip(_BIG, full)}

    everyone = _gather_all_wait("gather_small_wait", packed, everyone, small_sems, big[_BIG[-1]][1])
    everyone = lax.dynamic_update_slice(everyone, packed[None], (2 * chip + core, 0, 0))
    tot = _sum_slots("sum_small", everyone)
    loss = tot[2 * L + 3, 0]
    g_rep = jnp.concatenate([tot[:2 * L + 2], tot[2 * L + 2:2 * L + 3]], axis=0)
    base = 2 * L + 4
    g_sh = []
    for a in range(NA):
        blk = lax.dynamic_slice_in_dim(tot[base + 8 * a:base + 8 * a + 8], chip * CS, CS, axis=1)
        gi = tot[base + 8 * a + 5].reshape(NBLK, BW)
        gr = tot[base + 8 * a + 6].reshape(NBLK, BW)
        bl = lax.dynamic_slice_in_dim(jnp.concatenate([gi, gr], axis=1), chip * GS, GS, axis=1)
        g_sh += [blk[:5], bl.reshape(-1, CS), blk[7:8]]
    g_sh = jnp.concatenate(g_sh, axis=0)
    nrow = g_sh.shape[0] // NA

    def pack_rep(T):
        return jnp.concatenate([T["norm_mix"], T["norm_ffn"], T["norm_kv"][None], T["norm_final"][None],
                                _pad_lanes(T["b_forget"][None], D)], axis=0)

    def pack_sh(T):
        return jnp.concatenate([jnp.concatenate([T["conv_w"][a], T["conv_b"][a][None],
                                                 T["b_lru_gates"][a].reshape(-1, CS), T["lru_param"][a][None]], axis=0)
                                for a in range(NA)], axis=0)

    rep = _adamw("adamw_replicated", [g_rep], pack_rep(P), pack_rep(M1), pack_rep(M2))
    shd = _adamw("adamw_small_sharded", [g_sh], pack_sh(P), pack_sh(M1), pack_sh(M2))

    def unpack_rep(t):
        return dict(norm_mix=t[:L], norm_ffn=t[L:2 * L], norm_kv=t[2 * L], norm_final=t[2 * L + 1],
                    b_forget=t[2 * L + 2, :H])

    def unpack_sh(t):
        t = t.reshape(NA, nrow, CS)
        return dict(conv_w=t[:, :4], conv_b=t[:, 4], b_lru_gates=t[:, 5:nrow - 1].reshape(NA, NBLK, GS),
                    lru_param=t[:, nrow - 1])

    outs = []
    for i in range(4):
        small = {**unpack_rep(rep[i]), **unpack_sh(shd[i])}
        outs.append([big[w][i] if w in big else small[w] for w in _WEIGHTS])
    return (loss, grad_x.reshape(1, S, D), *outs[0], *outs[1], *outs[2], *outs[3])
```

```python
import functools
import math

import jax
import jax.numpy as jnp
from jax import lax
from jax.experimental import pallas as pl
from jax.experimental.pallas import tpu as pltpu

F32 = jnp.float32
BF16 = jnp.bfloat16

EPS = 1e-6
LRU_C = 8.0
HEAD_DIM = 64
LANES = 128
SUBLANES = 8
VMEM_LIMIT = 48 * 1024 * 1024
N_CHIPS = 4
N_DEV = 8

ADAM_LR = 0.001
ADAM_B1 = 0.9
ADAM_B2 = 0.999
ADAM_EPS = 1e-08
ADAM_WD = 0.01
ADAM_STEP = 10

_NN = (((1,), (0,)), ((), ()))
_NT = (((1,), (1,)), ((), ()))
_TN = (((0,), (0,)), ((), ()))
_DN = {"nn": _NN, "nt": _NT, "tn": _TN}
MESH = pl.DeviceIdType.MESH


def _hbm_out(shape, dtype):
    return pltpu.HBM(shape, dtype)


def _params(sem):
    return pltpu.CompilerParams(dimension_semantics=sem, vmem_limit_bytes=VMEM_LIMIT)


def _tile(n, want):
    if n <= want:
        return n
    t = (want // LANES) * LANES
    while t >= LANES:
        if n % t == 0:
            return t
        t -= LANES
    return n


def _sigmoid(x):
    return 1.0 / (1.0 + jnp.exp(-x))


def _sigmoid_t(x):
    return 0.5 * jnp.tanh(0.5 * x) + 0.5


def _softplus(x):
    return jnp.maximum(x, 0.0) + jnp.log(1.0 + jnp.exp(-jnp.abs(x)))


_GELU_C = math.sqrt(2.0 / math.pi)


def _gelu_and_grad(x):
    inner = _GELU_C * (x + 0.044715 * x * x * x)
    t = jnp.tanh(inner)
    g = 0.5 * x * (1.0 + t)
    dg = 0.5 * (1.0 + t) + 0.5 * x * (1.0 - t * t) * _GELU_C * (1.0 + 3.0 * 0.044715 * x * x)
    return g, dg


def _rms(x):
    return lax.rsqrt(jnp.mean(x * x, axis=-1, keepdims=True) + EPS)


def _rms_bwd(dy, x, g):
    r = _rms(x)
    xr = x * r
    dyg = dy * g
    return r * dyg - xr * (r * jnp.mean(dyg * xr, axis=-1, keepdims=True)), jnp.sum(dy * xr, axis=0, keepdims=True)


def _mm(name, mode, a, b, *, grid, a_spec, b_spec, out_shape, out_dtype, out_spec, nk=1,
        res=None, res_spec=None, bias=None, bias_spec=None, scale=None, norm_gain=None, norm_bwd=None):
    dn = _DN[mode]
    has_res, has_bias = res is not None, bias is not None
    blk = tuple(d for d in out_spec.block_shape if d is not None)
    vec = pl.BlockSpec((1, blk[-1]), lambda *g: (0, 0))
    a_specs = a_spec if isinstance(a_spec, list) else [a_spec]
    b_specs = b_spec if isinstance(b_spec, list) else [b_spec]
    npair = len(a_specs)
    n_in = 2 * npair + int(has_res) + int(has_bias) + (1 if norm_gain is not None else 0) + (3 if norm_bwd else 0)

    def body(*refs):
        p = 2 * npair
        r_ref = refs[p] if has_res else None
        p += int(has_res)
        bias_ref = refs[p] if has_bias else None
        p += int(has_bias)
        extra = refs[p:n_in]
        outs = refs[n_in:]
        o_ref = outs[0]
        part = lax.dot_general(refs[0][...], refs[npair][...], dn, preferred_element_type=F32)
        for t in range(1, npair):
            part = part + lax.dot_general(refs[t][...], refs[npair + t][...], dn, preferred_element_type=F32)

        def finish(acc):
            if scale is not None:
                acc = acc * scale
            if has_bias:
                acc = acc + bias_ref[...]
            if has_res:
                acc = r_ref[...] + acc
            if norm_bwd:
                h_ref, g_ref, dh_ref = extra
                dx, dg = _rms_bwd(acc, h_ref[...], g_ref[...])
                acc = dh_ref[...] + dx
                outs[1][...] = acc.astype(BF16)
                outs[2][...] = dg
            if norm_gain is not None:
                outs[1][...] = (acc * _rms(acc) * extra[0][...]).astype(BF16)
            o_ref[...] = acc.astype(o_ref.dtype)

        if nk == 1:
            finish(part)
        else:
            acc_ref = refs[-1]
            k = pl.program_id(2)

            @pl.when(k == 0)
            def _():
                acc_ref[...] = part

            @pl.when(k > 0)
            def _():
                acc_ref[...] += part

            @pl.when(k == nk - 1)
            def _():
                finish(acc_ref[...])

    ins, specs = [a] * npair + [b] * npair, a_specs + b_specs
    if has_res:
        ins.append(res)
        specs.append(res_spec)
    if has_bias:
        ins.append(bias)
        specs.append(bias_spec)
    out_specs, out_shapes = [out_spec], [_hbm_out(out_shape, out_dtype)]
    if norm_gain is not None:
        ins.append(norm_gain)
        specs.append(vec)
        out_specs.append(out_spec)
        out_shapes.append(_hbm_out(out_shape, BF16))
    if norm_bwd:
        h, g, dh = norm_bwd
        ins += [h, g, dh]
        specs += [out_spec, vec, out_spec]
        out_specs += [out_spec, pl.BlockSpec((None, 1, blk[-1]), lambda i, *rest: (i, 0, 0))]
        out_shapes += [_hbm_out(out_shape, BF16), _hbm_out((grid[0], 1, blk[-1]), F32)]
    sem = ("parallel", "parallel") + (("arbitrary",) if len(grid) == 3 else ())
    single = len(out_specs) == 1
    return pl.pallas_call(
        body, name=name, grid=grid, in_specs=specs, out_specs=out_specs[0] if single else out_specs,
        out_shape=out_shapes[0] if single else out_shapes,
        scratch_shapes=[pltpu.VMEM(blk, F32)] if nk > 1 else [],
        compiler_params=_params(sem),
    )(*ins)


def _mm_nn(name, a, b, *, b_lead=(), out_dtype, tm=512, tn=512, res=None, bias=None, scale=None, norm_gain=None):
    M, K = a.shape
    N = b.shape[-1]
    tm, tn = _tile(M, tm), _tile(N, tn)
    nl = len(b_lead)
    return _mm(
        name, "nn", a, b, grid=(M // tm, N // tn),
        a_spec=pl.BlockSpec((tm, K), lambda i, j: (i, 0)),
        b_spec=pl.BlockSpec((None,) * nl + (K, tn), lambda i, j: tuple(b_lead) + (0, j)),
        out_shape=(M, N), out_dtype=out_dtype, out_spec=pl.BlockSpec((tm, tn), lambda i, j: (i, j)),
        res=res, res_spec=pl.BlockSpec((tm, tn), lambda i, j: (i, j)),
        bias=bias, bias_spec=pl.BlockSpec((1, tn), lambda i, j: (0, j)), scale=scale, norm_gain=norm_gain)


def _mm_nt(name, a, b, *, b_lead=(), out_dtype, tm=512, tn=512, tk=2048, res=None, norm_bwd=None):
    M, K = a.shape
    N = b.shape[-2]
    tm, tn, tk = _tile(M, tm), _tile(N, tn), _tile(K, tk)
    nk = K // tk
    nl = len(b_lead)
    return _mm(
        name, "nt", a, b, grid=(M // tm, N // tn, nk), nk=nk,
        a_spec=pl.BlockSpec((tm, tk), lambda i, j, k: (i, k)),
        b_spec=pl.BlockSpec((None,) * nl + (tn, tk), lambda i, j, k: tuple(b_lead) + (j, k)),
        out_shape=(M, N), out_dtype=out_dtype, out_spec=pl.BlockSpec((tm, tn), lambda i, j, k: (i, j)),
        res=res, res_spec=pl.BlockSpec((tm, tn), lambda i, j, k: (i, j)), norm_bwd=norm_bwd)


def _mm_tn(name, a, b, *, out_dtype, tm=512, tn=512):
    S, M = a.shape
    N = b.shape[1]
    tm, tn = _tile(M, tm), _tile(N, tn)
    return _mm(
        name, "tn", a, b, grid=(M // tm, N // tn),
        a_spec=pl.BlockSpec((S, tm), lambda i, j: (0, i)),
        b_spec=pl.BlockSpec((S, tn), lambda i, j: (0, j)),
        out_shape=(M, N), out_dtype=out_dtype, out_spec=pl.BlockSpec((tm, tn), lambda i, j: (i, j)))


def _rmsnorm_fwd(name, h, g, tr=256):
    S, D = h.shape
    tr = _tile(S, tr)

    def body(h_ref, g_ref, o_ref):
        x = h_ref[...]
        r = lax.rsqrt(jnp.mean(x * x, axis=-1, keepdims=True) + EPS)
        o_ref[...] = (x * r * g_ref[...]).astype(o_ref.dtype)

    return pl.pallas_call(
        body, name=name, grid=(S // tr,),
        in_specs=[pl.BlockSpec((tr, D), lambda i: (i, 0)), pl.BlockSpec((1, D), lambda i: (0, 0))],
        out_specs=pl.BlockSpec((tr, D), lambda i: (i, 0)),
        out_shape=_hbm_out((S, D), BF16),
        compiler_params=_params(("parallel",)),
    )(h, g)


def _loss_head(name, h, target, g, tr=256):
    S, D = h.shape
    tr = _tile(S, tr)

    def body(h_ref, t_ref, g_ref, o_ref, ob_ref, dg_ref, loss_ref):
        i = pl.program_id(0)
        x = h_ref[...]
        gg = g_ref[...]
        r = lax.rsqrt(jnp.mean(x * x, axis=-1, keepdims=True) + EPS)
        xr = x * r
        err = xr * gg - t_ref[...]
        lpart = 0.5 * jnp.sum(jnp.mean(err * err, axis=-1, keepdims=True), axis=0, keepdims=True)
        dy = err * (1.0 / D)
        dyg = dy * gg
        dx = r * dyg - xr * (r * jnp.mean(dyg * xr, axis=-1, keepdims=True))
        o_ref[...] = dx
        ob_ref[...] = dx.astype(BF16)
        part = jnp.sum(dy * xr, axis=0, keepdims=True)
        lrow = jnp.broadcast_to(lpart, (1, LANES))

        @pl.when(i == 0)
        def _():
            dg_ref[...] = part
            loss_ref[...] = lrow

        @pl.when(i > 0)
        def _():
            dg_ref[...] += part
            loss_ref[...] += lrow

    row = pl.BlockSpec((tr, D), lambda i: (i, 0))
    vec = pl.BlockSpec((1, D), lambda i: (0, 0))
    return pl.pallas_call(
        body, name=name, grid=(S // tr,),
        in_specs=[row, row, vec], out_specs=[row, row, vec, pl.BlockSpec((1, LANES), lambda i: (0, 0))],
        out_shape=[_hbm_out((S, D), F32), _hbm_out((S, D), BF16),
                   _hbm_out((1, D), F32), _hbm_out((1, LANES), F32)],
        compiler_params=_params(("arbitrary",)),
    )(h, target, g)


def _swiglu_fwd(name, hn, w_in, tm=512):
    S, D = hn.shape
    FH = w_in.shape[-1]
    tm = _tile(S, tm)

    def body(x_ref, wg_ref, wu_ref, z_ref, a_ref):
        x = x_ref[...]
        zg = jnp.dot(x, wg_ref[...], preferred_element_type=F32)
        zu = jnp.dot(x, wu_ref[...], preferred_element_type=F32)
        sg = _sigmoid_t(zg)
        silu = zg * sg
        z_ref[0] = (zu * (sg * (1.0 + zg * (1.0 - sg)))).astype(z_ref.dtype)
        z_ref[1] = silu.astype(z_ref.dtype)
        a_ref[...] = (silu * zu).astype(a_ref.dtype)

    return pl.pallas_call(
        body, name=name, grid=(2, S // tm),
        in_specs=[pl.BlockSpec((tm, D), lambda j, i: (i, 0)),
                  pl.BlockSpec((None, D, FH), lambda j, i: (j, 0, 0)),
                  pl.BlockSpec((None, D, FH), lambda j, i: (j + 2, 0, 0))],
        out_specs=[pl.BlockSpec((2, tm, FH), lambda j, i: (0, i, j)), pl.BlockSpec((tm, FH), lambda j, i: (i, j))],
        out_shape=[_hbm_out((2, S, 2 * FH), BF16), _hbm_out((S, 2 * FH), BF16)],
        compiler_params=_params(("parallel", "parallel")),
    )(hn, w_in, w_in)


def _swiglu_bwd(name, dhb, w_out, z3, tm=512):
    S, D = dhb.shape
    F = w_out.shape[0]
    FH = F // 2
    tm = _tile(S, tm)

    def body(d_ref, w_ref, z_ref, dz_ref):
        d = lax.dot_general(d_ref[...], w_ref[...], _NT, preferred_element_type=F32)
        dz_ref[0] = (d * z_ref[0].astype(F32)).astype(dz_ref.dtype)
        dz_ref[1] = (d * z_ref[1].astype(F32)).astype(dz_ref.dtype)

    zspec = pl.BlockSpec((2, tm, FH), lambda j, i: (0, i, j))
    return pl.pallas_call(
        body, name=name, grid=(2, S // tm),
        in_specs=[pl.BlockSpec((tm, D), lambda j, i: (i, 0)), pl.BlockSpec((FH, D), lambda j, i: (j, 0)), zspec],
        out_specs=zspec, out_shape=_hbm_out((2, S, F), BF16),
        compiler_params=_params(("parallel", "parallel")),
    )(dhb, w_out, z3)


SCAN_ROWS = 64


def _group_scan(A, B, reverse):
    n = A.shape[0]
    sub = lax.broadcasted_iota(jnp.int32, A.shape, 0) % SUBLANES
    for d in (1, 2, 4):
        if reverse:
            A_sh, B_sh = pltpu.roll(A, n - d, 0), pltpu.roll(B, n - d, 0)
            keep = sub < SUBLANES - d
        else:
            A_sh, B_sh = pltpu.roll(A, d, 0), pltpu.roll(B, d, 0)
            keep = sub >= d
        B = jnp.where(keep, A * B_sh + B, B)
        A = jnp.where(keep, A * A_sh, A)
    return A, B


def _block_scan(a, u, carry, reverse):
    A, B = _group_scan(a, u, reverse)
    ng = a.shape[0] // SUBLANES
    out = [None] * ng
    order = range(ng - 1, -1, -1) if reverse else range(ng)
    for gi in order:
        sl = slice(gi * SUBLANES, (gi + 1) * SUBLANES)
        hg = A[sl] * carry + B[sl]
        out[gi] = hg
        carry = hg[0:1] if reverse else hg[SUBLANES - 1:SUBLANES]
    return jnp.concatenate(out, axis=0), carry


def _lru_gates(rc, gip, grp, sp):
    gi = _sigmoid_t(gip)
    gr = _sigmoid_t(grp)
    la = -LRU_C * gr * sp
    a = jnp.exp(la)
    om = -jnp.tanh(la) * (a * a + 1.0)
    mult = jnp.sqrt(om)
    return gi, gr, a, mult


def _lru_fwd(name, proj, rc, gip, grp, lru_p, tc=256):
    S, C = rc.shape
    tc = _tile(C, tc)
    nb = S // SCAN_ROWS

    def body(gb_ref, rc_ref, gi_ref, gr_ref, l_ref, h_ref, m_ref):
        sp = _softplus(-l_ref[...])

        def step(b, carry):
            rows = pl.ds(pl.multiple_of(b * SCAN_ROWS, SCAN_ROWS), SCAN_ROWS)
            rcb = rc_ref[rows, :]
            gi, _, a, mult = _lru_gates(rcb, gi_ref[rows, :], gr_ref[rows, :], sp)
            h, carry = _block_scan(a, rcb * gi * mult, carry, False)
            h_ref[rows, :] = h
            gel, _ = _gelu_and_grad(gb_ref[rows, :])
            m_ref[rows, :] = (gel * h).astype(m_ref.dtype)
            return carry

        lax.fori_loop(0, nb, step, jnp.zeros((1, tc), F32))

    col = pl.BlockSpec((S, tc), lambda j: (0, j))
    return pl.pallas_call(
        body, name=name, grid=(C // tc,),
        in_specs=[col, col, col, col, pl.BlockSpec((1, tc), lambda j: (0, j))],
        out_specs=[col, col],
        out_shape=[_hbm_out((S, C), F32), _hbm_out((S, C), BF16)],
        compiler_params=_params(("parallel",)),
    )(proj, rc, gip, grp, lru_p)


def _lru_bwd(name, dm, proj, hrec, rc, gip, grp, lru_p, tc=256):
    S, C = rc.shape
    tc = _tile(C, tc)
    nb = S // SCAN_ROWS
    R = SCAN_ROWS

    def body(dm_ref, gb_ref, h_ref, rc_ref, gi_ref, gr_ref, l_ref,
             dgb_ref, dgi_ref, dgr_ref, drc_ref, dbi_ref, dbr_ref, dl_ref):
        lp = l_ref[...]
        sp = _softplus(-lp)
        row = lax.broadcasted_iota(jnp.int32, (R, tc), 0)
        zero = jnp.zeros((1, tc), F32)

        def step(t, carry):
            mu_in, s_i, s_r, s_sp = carry
            b = nb - 1 - t
            r0 = pl.multiple_of(b * R, R)
            rows = pl.ds(r0, R)
            rcb = rc_ref[rows, :]
            gi, gr, a, mult = _lru_gates(rcb, gi_ref[rows, :], gr_ref[rows, :], sp)
            gel, dgel = _gelu_and_grad(gb_ref[rows, :])
            dmb = dm_ref[rows, :]
            h = h_ref[rows, :]
            dgb_ref[rows, :] = (dmb * h * dgel).astype(dgb_ref.dtype)
            dh = dmb * gel
            mu, mu_out = _block_scan(a, a * dh, mu_in, True)
            mu_next = jnp.where(row == R - 1, mu_in, pltpu.roll(mu, R - 1, 0))
            lam = dh + mu_next
            p0 = pl.multiple_of(jnp.maximum(r0 - SUBLANES, 0), SUBLANES)
            prev = h_ref[pl.ds(p0, SUBLANES), :][SUBLANES - 1:SUBLANES]
            prev = jnp.where(b > 0, prev, 0.0)
            h_prev = jnp.where(row == 0, prev, pltpu.roll(h, 1, 0))
            da = lam * h_prev
            d_mult = lam * rcb * gi
            d_la = da * a - d_mult * (a * a) / mult
            d_grp = d_la * (-LRU_C * sp) * gr * (1.0 - gr)
            d_gip = lam * rcb * mult * gi * (1.0 - gi)
            dgr_ref[rows, :] = d_grp.astype(dgr_ref.dtype)
            dgi_ref[rows, :] = d_gip.astype(dgi_ref.dtype)
            drc_ref[rows, :] = lam * gi * mult
            s_i = s_i + jnp.sum(d_gip, axis=0, keepdims=True)
            s_r = s_r + jnp.sum(d_grp, axis=0, keepdims=True)
            s_sp = s_sp + jnp.sum(d_la * gr, axis=0, keepdims=True)
            return mu_out, s_i, s_r, s_sp

        _, s_i, s_r, s_sp = lax.fori_loop(0, nb, step, (zero, zero, zero, zero))
        dbi_ref[...] = s_i
        dbr_ref[...] = s_r
        dl_ref[...] = (-LRU_C * s_sp) * (-_sigmoid(-lp))

    col = pl.BlockSpec((S, tc), lambda j: (0, j))
    vec = pl.BlockSpec((1, tc), lambda j: (0, j))
    return pl.pallas_call(
        body, name=name, grid=(C // tc,),
        in_specs=[col, col, col, col, col, col, vec],
        out_specs=[col, col, col, col, vec, vec, vec],
        out_shape=[_hbm_out((S, C), BF16), _hbm_out((S, C), BF16),
                   _hbm_out((S, C), BF16), _hbm_out((S, C), F32),
                   _hbm_out((1, C), F32), _hbm_out((1, C), F32),
                   _hbm_out((1, C), F32)],
        compiler_params=_params(("parallel",)),
    )(dm, proj, hrec, rc, gip, grp, lru_p)


def _cumsum_rows(name, u, reverse):
    S, C = u.shape
    nb = S // SCAN_ROWS

    def body(u_ref, o_ref):
        def step(t, carry):
            b = nb - 1 - t if reverse else t
            rows = pl.ds(pl.multiple_of(b * SCAN_ROWS, SCAN_ROWS), SCAN_ROWS)
            ub = u_ref[rows, :]
            h, carry = _block_scan(jnp.ones_like(ub), ub, carry, reverse)
            o_ref[rows, :] = h
            return carry

        lax.fori_loop(0, nb, step, jnp.zeros((1, C), F32))

    spec = pl.BlockSpec((S, C), lambda i: (0, 0))
    return pl.pallas_call(
        body, name=name, grid=(1,), in_specs=[spec], out_specs=spec,
        out_shape=_hbm_out((S, C), F32),
        compiler_params=_params(("arbitrary",)),
    )(u)


def _shift_down(x, k):
    row = lax.broadcasted_iota(jnp.int32, x.shape, 0)
    return jnp.where(row >= k, pltpu.roll(x, k, 0), 0.0)


def _shift_up(x, k):
    n = x.shape[0]
    row = lax.broadcasted_iota(jnp.int32, x.shape, 0)
    return jnp.where(row < n - k, pltpu.roll(x, n - k, 0), 0.0)


def _conv_fwd(name, proj, w, b, tc=256):
    S, C2 = proj.shape
    C = C2 // 2
    tc = _tile(C, tc)
    off = C // tc

    def body(x_ref, w_ref, b_ref, o_ref, ob_ref):
        x = x_ref[...]
        out = b_ref[...] + w_ref[3:4, :] * x
        for k in (1, 2, 3):
            out = out + w_ref[3 - k:4 - k, :] * _shift_down(x, k)
        o_ref[...] = out
        ob_ref[...] = out.astype(BF16)

    col = pl.BlockSpec((S, tc), lambda j: (0, j))
    return pl.pallas_call(
        body, name=name, grid=(C // tc,),
        in_specs=[pl.BlockSpec((S, tc), lambda j: (0, off + j)),
                  pl.BlockSpec((4, tc), lambda j: (0, j)), pl.BlockSpec((1, tc), lambda j: (0, j))],
        out_specs=[col, col],
        out_shape=[_hbm_out((S, C), F32), _hbm_out((S, C), BF16)],
        compiler_params=_params(("parallel",)),
    )(proj, w, b)


def _conv_bwd(name, drc, proj, w, tc=256):
    S, C = drc.shape
    tc = _tile(C, tc)
    off = C // tc

    def body(y_ref, x_ref, w_ref, dx_ref, dw_ref, db_ref):
        y = y_ref[...]
        x = x_ref[...]
        dx = w_ref[3:4, :] * y
        dw_ref[3:4, :] = jnp.sum(y * x, axis=0, keepdims=True)
        for k in (1, 2, 3):
            dx = dx + w_ref[3 - k:4 - k, :] * _shift_up(y, k)
            dw_ref[3 - k:4 - k, :] = jnp.sum(y * _shift_down(x, k), axis=0, keepdims=True)
        dx_ref[...] = dx.astype(dx_ref.dtype)
        db_ref[...] = jnp.sum(y, axis=0, keepdims=True)

    col = pl.BlockSpec((S, tc), lambda j: (0, j))
    return pl.pallas_call(
        body, name=name, grid=(C // tc,),
        in_specs=[col, pl.BlockSpec((S, tc), lambda j: (0, off + j)), pl.BlockSpec((4, tc), lambda j: (0, j))],
        out_specs=[col, pl.BlockSpec((4, tc), lambda j: (0, j)), pl.BlockSpec((1, tc), lambda j: (0, j))],
        out_shape=[_hbm_out((S, C), BF16), _hbm_out((4, C), F32),
                   _hbm_out((1, C), F32)],
        compiler_params=_params(("parallel",)),
    )(drc, proj, w)


def _gates_fwd(name, rcb, wg, bg):
    S, C = rcb.shape
    nblk, bw, _ = wg.shape

    def body(x_ref, w_ref, b_ref, gi_ref, gr_ref):
        g = jnp.dot(x_ref[...], w_ref[...], preferred_element_type=F32) + b_ref[...]
        gi_ref[...] = g[:, :bw]
        gr_ref[...] = g[:, bw:]

    col = pl.BlockSpec((S, bw), lambda n: (0, n))
    return pl.pallas_call(
        body, name=name, grid=(nblk,),
        in_specs=[col, pl.BlockSpec((None, bw, 2 * bw), lambda n: (n, 0, 0)),
                  pl.BlockSpec((None, 1, 2 * bw), lambda n: (n, 0, 0))],
        out_specs=[col, col],
        out_shape=[_hbm_out((S, C), F32), _hbm_out((S, C), F32)],
        compiler_params=_params(("parallel",)),
    )(rcb, wg, bg)


def _gates_bwd(name, dgi, dgr, rcb, wg, drc1):
    S, C = rcb.shape
    nblk, bw, _ = wg.shape

    def body(dgi_ref, dgr_ref, x_ref, w_ref, d1_ref, drc_ref, dw_ref):
        w = w_ref[...]
        x = x_ref[...]
        di, dr = dgi_ref[...], dgr_ref[...]
        drc_ref[...] = (d1_ref[...]
                        + lax.dot_general(di, w[:, :bw], _NT, preferred_element_type=F32)
                        + lax.dot_general(dr, w[:, bw:], _NT, preferred_element_type=F32))
        dw_ref[:, :bw] = lax.dot_general(x, di, _TN, preferred_element_type=F32).astype(dw_ref.dtype)
        dw_ref[:, bw:] = lax.dot_general(x, dr, _TN, preferred_element_type=F32).astype(dw_ref.dtype)

    col = pl.BlockSpec((S, bw), lambda n: (0, n))
    wspec = pl.BlockSpec((None, bw, 2 * bw), lambda n: (n, 0, 0))
    return pl.pallas_call(
        body, name=name, grid=(nblk,),
        in_specs=[col, col, col, wspec, col], out_specs=[col, wspec],
        out_shape=[_hbm_out((S, C), F32), _hbm_out((nblk, bw, 2 * bw), BF16)],
        compiler_params=_params(("parallel",)),
    )(dgi, dgr, rcb, wg, drc1)


def _att_tile(S):
    return next(t for t in (512, 256, 128) if S % t == 0)


def _head_lanes(shape):
    return lax.broadcasted_iota(jnp.int32, shape, len(shape) - 1) < HEAD_DIM


def _key_bias(c_ref, rows, hh):
    return jnp.broadcast_to(c_ref[rows, hh:hh + 1], (rows.size, LANES))


def _over_keys(x, op):
    n = x.shape[0]
    while n > SUBLANES:
        n //= 2
        x = op(x[:n], x[n:2 * n])
    return (jnp.max if op is jnp.maximum else jnp.sum)(x, axis=0, keepdims=True)


def _causal_t(T, cc):
    r = lax.broadcasted_iota(jnp.int32, (T, LANES), 0)
    c = lax.broadcasted_iota(jnp.int32, (T, LANES), 1) + cc * LANES
    return r <= c


def _attn_fwd(name, q, kv, c3):
    S, D = q.shape
    HP = D // LANES
    T = _att_tile(S)
    nq = S // T
    NC = T // LANES

    def body(q_ref, k_ref, v_ref, c_ref, o_ref, of_ref, lse_ref, bias, vT, acc, m_scr, l_scr):
        def prologue(i, _):
            rows = pl.ds(pl.multiple_of(i * T, T), T)
            for hh in range(2):
                bias[hh, rows, :] = _key_bias(c_ref, rows, hh)
            vT[i] = v_ref[rows, :].astype(F32).T.astype(BF16)
            return 0

        lax.fori_loop(0, nq, prologue, 0)

        def q_step(qi, _):
            q0 = pl.multiple_of(qi * T, T)
            qb = q_ref[pl.ds(q0, T), :]
            m_scr[...] = jnp.full(m_scr.shape, -jnp.inf, F32)
            l_scr[...] = jnp.zeros(l_scr.shape, F32)
            acc[...] = jnp.zeros(acc.shape, F32)

            def tile(kj, masked):
                ks = pl.ds(pl.multiple_of(kj * T, T), T)
                kf = k_ref[ks, :].astype(F32)
                first = _head_lanes(kf.shape)
                kms = [jnp.where(first if hh == 0 else jnp.logical_not(first), kf, 0.0).astype(BF16) for hh in range(2)]
                sTs = [lax.dot_general(km, qb, _NT, preferred_element_type=F32) for km in kms]
                for hh in range(2):
                    b = bias[hh, ks, :]
                    ps, alphas = [], []
                    for cc in range(NC):
                        cols = slice(cc * LANES, (cc + 1) * LANES)
                        s = sTs[hh][:, cols] + b
                        if masked:
                            s = jnp.where(_causal_t(T, cc), s, -jnp.inf)
                        m_old = m_scr[hh, cc]
                        m_new = jnp.maximum(m_old, _over_keys(s, jnp.maximum))
                        alpha = jnp.exp(m_old - m_new)
                        p = jnp.exp(s - m_new)
                        l_scr[hh, cc] = alpha * l_scr[hh, cc] + _over_keys(p, jnp.add)
                        m_scr[hh, cc] = m_new
                        ps.append(p.astype(BF16))
                        alphas.append(alpha)
                    acc[hh] = acc[hh] * jnp.concatenate(alphas, axis=1) + jnp.dot(
                        vT[kj, hh * HEAD_DIM:(hh + 1) * HEAD_DIM, :], jnp.concatenate(ps, axis=1),
                        preferred_element_type=F32)

            def inner(kj, _):
                tile(kj, False)
                return 0

            lax.fori_loop(0, qi, inner, 0)
            tile(qi, True)
            outs = []
            for hh in range(2):
                inv = jnp.concatenate([1.0 / l_scr[hh, cc] for cc in range(NC)], axis=1)
                outs.append(acc[hh] * inv)
                for cc in range(NC):
                    lse_ref[hh:hh + 1, pl.ds(q0 + cc * LANES, LANES)] = m_scr[hh, cc] + jnp.log(l_scr[hh, cc])
            out = jnp.concatenate(outs, axis=0).T
            o_ref[pl.ds(q0, T), :] = out.astype(o_ref.dtype)
            of_ref[pl.ds(q0, T), :] = out
            return 0

        lax.fori_loop(0, nq, q_step, 0)

    blk = lambda off: pl.BlockSpec((S, LANES), lambda p: (0, off + p))
    return pl.pallas_call(
        body, name=name, grid=(HP,),
        in_specs=[blk(0), blk(0), blk(HP), pl.BlockSpec((None, S, 2), lambda p: (p, 0, 0))],
        out_specs=[blk(0), blk(0), pl.BlockSpec((None, 2, S), lambda p: (p, 0, 0))],
        out_shape=[_hbm_out((S, D), BF16), _hbm_out((S, D), F32),
                   _hbm_out((HP, 2, S), F32)],
        scratch_shapes=[pltpu.VMEM((2, S, LANES), F32), pltpu.VMEM((nq, LANES, T), BF16),
                        pltpu.VMEM((2, HEAD_DIM, T), F32), pltpu.VMEM((2, NC, 1, LANES), F32),
                        pltpu.VMEM((2, NC, 1, LANES), F32)],
        compiler_params=_params(("parallel",)),
    )(q, kv, kv, c3)


def _attn_bwd(name, q, kv, c3, of, do, lse3):
    S, D = q.shape
    HP = D // LANES
    T = _att_tile(S)
    nq = S // T
    NC = T // LANES
    scale = HEAD_DIM ** -0.5

    def body(q_ref, k_ref, v_ref, c_ref, of_ref, do_ref, lse_ref,
             dq_ref, dk_ref, dv_ref, dck_ref, drq_ref, bias, kT, dqT, delta, dr_scr, dk_acc, dv_acc, dc_acc):
        def prologue(i, _):
            rows = pl.ds(pl.multiple_of(i * T, T), T)
            for hh in range(2):
                bias[hh, rows, :] = _key_bias(c_ref, rows, hh)
            kT[i] = k_ref[rows, :].astype(F32).T.astype(BF16)
            prodT = (do_ref[rows, :].astype(F32) * of_ref[rows, :]).T
            for hh in range(2):
                delta[hh:hh + 1, rows] = jnp.sum(prodT[hh * HEAD_DIM:(hh + 1) * HEAD_DIM], axis=0, keepdims=True)
            dqT[i] = jnp.zeros((LANES, T), F32)
            return 0

        lax.fori_loop(0, nq, prologue, 0)
        dr_scr[...] = jnp.zeros(dr_scr.shape, F32)

        def kv_step(kj, _):
            ks = pl.ds(pl.multiple_of(kj * T, T), T)
            kf = k_ref[ks, :].astype(F32)
            vf = v_ref[ks, :].astype(F32)
            first = _head_lanes(kf.shape)
            masks = [first, jnp.logical_not(first)]
            kms = [jnp.where(m, kf, 0.0).astype(BF16) for m in masks]
            vms = [jnp.where(m, vf, 0.0).astype(BF16) for m in masks]

            for acc in (dk_acc, dv_acc, dc_acc):
                acc[...] = jnp.zeros(acc.shape, F32)

            def tile(qi, masked):
                q0 = pl.multiple_of(qi * T, T)
                qb = q_ref[pl.ds(q0, T), :]
                dob = do_ref[pl.ds(q0, T), :]
                sTs = [lax.dot_general(km, qb, _NT, preferred_element_type=F32) for km in kms]
                dpTs = [lax.dot_general(vm, dob, _NT, preferred_element_type=F32) for vm in vms]
                for hh in range(2):
                    b = bias[hh, ks, :]
                    head = slice(hh * HEAD_DIM, (hh + 1) * HEAD_DIM)
                    ps, dss = [], []
                    for cc in range(NC):
                        cols = slice(cc * LANES, (cc + 1) * LANES)
                        at = pl.ds(q0 + cc * LANES, LANES)
                        p = jnp.exp(sTs[hh][:, cols] + b - lse_ref[hh:hh + 1, at])
                        if masked:
                            p = jnp.where(_causal_t(T, cc), p, 0.0)
                        ds = p * (dpTs[hh][:, cols] - delta[hh:hh + 1, at])
                        ps.append(p.astype(BF16))
                        dss.append(ds.astype(BF16))
                        dc_acc[hh] += ds
                        dr_scr[hh:hh + 1, at] += _over_keys(ds, jnp.add)
                    pT = jnp.concatenate(ps, axis=1)
                    dsT = jnp.concatenate(dss, axis=1)
                    dv_acc[hh] += jnp.dot(pT, dob, preferred_element_type=F32)
                    dk_acc[hh] += jnp.dot(dsT, qb, preferred_element_type=F32)
                    dqT[qi, head, :] += jnp.dot(kT[kj, head, :], dsT, preferred_element_type=F32)

            def inner(qi, _):
                tile(qi, False)
                return 0

            tile(kj, True)
            lax.fori_loop(kj + 1, nq, inner, 0)
            dk_ref[ks, :] = jnp.where(first, dk_acc[0], dk_acc[1])
            dv_ref[ks, :] = jnp.where(first, dv_acc[0], dv_acc[1])
            for hh in range(2):
                dck_ref[hh:hh + 1, ks] = -jnp.sum(dc_acc[hh].T, axis=0, keepdims=True)
            return 0

        lax.fori_loop(0, nq, kv_step, 0)

        def epilogue(i, _):
            rows = pl.ds(pl.multiple_of(i * T, T), T)
            dq_ref[rows, :] = (dqT[i].T * scale).astype(dq_ref.dtype)
            return 0

        lax.fori_loop(0, nq, epilogue, 0)
        drq_ref[...] = dr_scr[...]

    blk = lambda off: pl.BlockSpec((S, LANES), lambda p: (0, off + p))
    row_spec = pl.BlockSpec((None, 2, S), lambda p: (p, 0, 0))
    return pl.pallas_call(
        body, name=name, grid=(HP,),
        in_specs=[blk(0), blk(0), blk(HP), pl.BlockSpec((None, S, 2), lambda p: (p, 0, 0)), blk(0), blk(0), row_spec],
        out_specs=[blk(0), blk(0), blk(0), row_spec, row_spec],
        out_shape=[_hbm_out((S, D), BF16), _hbm_out((S, D), F32),
                   _hbm_out((S, D), F32), _hbm_out((HP, 2, S), F32),
                   _hbm_out((HP, 2, S), F32)],
        scratch_shapes=[pltpu.VMEM((2, S, LANES), F32), pltpu.VMEM((nq, LANES, T), BF16),
                        pltpu.VMEM((nq, LANES, T), F32), pltpu.VMEM((2, S), F32), pltpu.VMEM((2, S), F32)]
        + [pltpu.VMEM((2, T, LANES), F32)] * 3,
        compiler_params=_params(("parallel",)),
    )(q, kv, kv, c3, of, do, lse3)


def _logsig_fwd(name, f):
    S, C = f.shape

    def body(f_ref, o_ref):
        o_ref[...] = -_softplus(-f_ref[...])

    spec = pl.BlockSpec((S, C), lambda i: (0, 0))
    return pl.pallas_call(body, name=name, grid=(1,), in_specs=[spec], out_specs=spec,
                          out_shape=_hbm_out((S, C), F32),
                          compiler_params=_params(("arbitrary",)))(f)


def _logsig_bwd(name, dls, f):
    S, C = f.shape

    def body(d_ref, f_ref, o_ref, s_ref):
        df = d_ref[...] * _sigmoid(-f_ref[...])
        o_ref[...] = df.astype(o_ref.dtype)
        s_ref[...] = jnp.sum(df, axis=0, keepdims=True)

    spec = pl.BlockSpec((S, C), lambda i: (0, 0))
    return pl.pallas_call(body, name=name, grid=(1,), in_specs=[spec, spec],
                          out_specs=[spec, pl.BlockSpec((1, C), lambda i: (0, 0))],
                          out_shape=[_hbm_out((S, C), BF16), _hbm_out((1, C), F32)],
                          compiler_params=_params(("arbitrary",)))(dls, f)


def _kv_sum(name, dk_parts, dv_parts, tr=256):
    S, D = dk_parts[0].shape
    tr = _tile(S, tr)
    n = len(dk_parts)

    def body(*refs):
        for half, group in enumerate((refs[:n], refs[n:2 * n])):
            acc = group[0][...]
            for r in group[1:]:
                acc = acc + r[...]
            refs[2 * n][:, half * D:(half + 1) * D] = acc.astype(BF16)

    spec = pl.BlockSpec((tr, D), lambda i: (i, 0))
    return pl.pallas_call(body, name=name, grid=(S // tr,), in_specs=[spec] * (2 * n),
                          out_specs=pl.BlockSpec((tr, 2 * D), lambda i: (i, 0)),
                          out_shape=_hbm_out((S, 2 * D), BF16),
                          compiler_params=_params(("parallel",)))(*dk_parts, *dv_parts)


def _local_step(x, target, gains, layer_weights, layer_prefetch, layer_grads):
    S, D = x.shape
    HP = D // LANES
    scale = HEAD_DIM ** -0.5
    tx = _tile(S, 256)
    saved = []
    h = x
    l = 0
    kv = c3 = f_pre = hn_kv = h_kv = None
    while True:
        W = layer_weights(l, "mix", h)
        if W is None:
            break
        recurrent = "w_rec_in" in W
        if l == 0:
            xn = _rmsnorm_fwd("mix_norm_0", h, gains["mix"][0])
        if recurrent:
            CH = W["w_rec_in"].shape[-1]
            C = 2 * CH
            proj = _mm(f"rec_in_{l}", "nn", xn, W["w_rec_in"], grid=(1, N_CHIPS),
                       a_spec=pl.BlockSpec((S, D), lambda i, j: (0, 0)),
                       b_spec=pl.BlockSpec((None, D, CH), lambda i, j: (j, 0, 0)),
                       out_shape=(S, 2 * C), out_dtype=F32,
                       out_spec=pl.BlockSpec((S, CH), lambda i, j: (0, j)))
            layer_prefetch(l, "mix2", proj)
            rc, rcb = _conv_fwd(f"conv_{l}", proj, W["conv_w"], W["conv_b"])
            W = {**W, **layer_weights(l, "mix2", rcb)}
            gip, grp = _gates_fwd(f"gates_{l}", rcb, W["w_gates"], W["b_gates"])
            hrec, m = _lru_fwd(f"lru_{l}", proj, rc, gip, grp, W["lru_param"])
            layer_prefetch(l, "ffn", m)
            h_mid, hn = _mm_nn(f"rec_out_{l}", m, W["w_rec_out"], out_dtype=F32, res=h, tn=D, norm_gain=gains["ffn"][l])
            mix_saved = (xn, proj, rc, rcb, gip, grp, hrec, m)
        else:
            if "w_kv" in W:
                h_kv = h
                hn_kv = _rmsnorm_fwd("kv_norm", h, W["norm_kv"])
                kv = _mm_nn("kv_proj", hn_kv, W["w_kv"], out_dtype=BF16, tm=1024, tn=1024)
                f_pre = _mm_nn("f_proj", hn_kv, W["w_f"], out_dtype=F32, bias=W["b_f"])
                c = _cumsum_rows("c_cumsum", _logsig_fwd("logsig", f_pre), False)
                c3 = (-c[:, :2 * HP]).reshape(S, HP, 2).transpose(1, 0, 2)
            q = _mm_nn(f"q_proj_{l}", xn, W["w_q"], out_dtype=BF16, scale=scale, tm=1024, tn=1024)
            layer_prefetch(l, "mix2", q)
            o, of, lse = _attn_fwd(f"attn_fwd_{l}", q, kv, c3)
            W = {**W, **layer_weights(l, "mix2", o)}
            layer_prefetch(l, "ffn", o)
            h_mid, hn = _mm_nn(f"o_proj_{l}", o, W["w_o"], out_dtype=F32, res=h, tn=D, norm_gain=gains["ffn"][l])
            mix_saved = (xn, q, o, of, lse)
        W = {**W, **layer_weights(l, "ffn", h_mid)}
        z3, act = _swiglu_fwd(f"ffn_in_{l}", hn, W["w_ffn_in"])
        layer_prefetch(l + 1, "mix", act)
        saved.append((W, h, h_mid, mix_saved, (hn, z3, act)))
        l += 1
        if l < len(gains["mix"]):
            h, xn = _mm_nn(f"ffn_out_{l - 1}", act, W["w_ffn_out"], out_dtype=F32, res=h_mid, tn=D,
                           norm_gain=gains["mix"][l])
        else:
            h = _mm_nn(f"ffn_out_{l - 1}", act, W["w_ffn_out"], out_dtype=F32, res=h_mid, tn=D)

    dh, dhb, dg_final, loss_row = _loss_head("loss_head", h, target, gains["final"])

    dk_parts, dv_parts, dc_parts = [], [], []
    token = None
    for l in reversed(range(len(saved))):
        W, h_in, h_mid, mix_saved, (hn, z3, act) = saved[l]
        recurrent = "w_rec_in" in W
        FH = W["w_ffn_in"].shape[-1]
        G = {}
        norm_ffn = gains["ffn"][l]
        if token is not None:
            norm_ffn = norm_ffn + jnp.minimum(token[:1, :1], 0.0)
        G["w_ffn_out"] = _mm_tn(f"d_ffn_out_{l}", act, dhb, out_dtype=BF16, tn=D)
        dz3 = _swiglu_bwd(f"d_act_{l}", dhb, W["w_ffn_out"], z3)
        G["w_ffn_in"] = _mm(
            f"d_ffn_in_{l}", "tn", hn, dz3, grid=(1, N_CHIPS),
            a_spec=pl.BlockSpec((S, D), lambda i, j: (0, 0)),
            b_spec=pl.BlockSpec((None, S, FH), lambda i, j: (j // 2, 0, j % 2)),
            out_shape=(N_CHIPS, D, FH), out_dtype=BF16,
            out_spec=pl.BlockSpec((None, D, FH), lambda i, j: (j, 0, 0)))
        ffn_token = layer_grads(l, "ffn", G)
        G = {}
        if ffn_token is not None:
            norm_ffn = norm_ffn + jnp.minimum(ffn_token[:1, :1], 0.0)
        dh, dhb, dgp = _mm(f"d_ffn_hn_{l}", "nt", dz3, W["w_ffn_in"], grid=(S // tx, 1),
                           a_spec=[pl.BlockSpec((None, tx, FH), functools.partial(lambda i, j, k: (k // 2, i, k % 2), k=k))
                                   for k in range(N_CHIPS)],
                           b_spec=[pl.BlockSpec((None, D, FH), functools.partial(lambda i, j, k: (k, 0, 0), k=k))
                                   for k in range(N_CHIPS)],
                           out_shape=(S, D), out_dtype=F32, out_spec=pl.BlockSpec((tx, D), lambda i, j: (i, 0)),
                           norm_bwd=(h_mid, norm_ffn, dh))
        G["norm_ffn"] = jnp.sum(dgp, axis=0)
        if recurrent:
            CH = W["w_rec_in"].shape[-1]
            C = 2 * CH
            xn, proj, rc, rcb, gip, grp, hrec, m = mix_saved
            G["w_rec_out"] = _mm_tn(f"d_rec_out_{l}", m, dhb, out_dtype=BF16, tn=D)
            dm = _mm_nt(f"d_m_{l}", dhb, W["w_rec_out"], out_dtype=F32, tn=C)
            dgb, dgi, dgr, drc1, G["b_gi"], G["b_gr"], G["lru_param"] = _lru_bwd(
                f"d_lru_{l}", dm, proj, hrec, rc, gip, grp, W["lru_param"])
            drc, G["w_gates"] = _gates_bwd(f"d_gates_{l}", dgi, dgr, rcb, W["w_gates"], drc1)
            mix_token = layer_grads(l, "mix2", {n: G[n] for n in ("w_rec_out", "w_gates")})
            drec, G["conv_w"], G["conv_b"] = _conv_bwd(f"d_conv_{l}", drc, proj, W["conv_w"])
            dproj = jnp.concatenate([dgb, drec], axis=1)
            norm_mix = gains["mix"][l] if mix_token is None else gains["mix"][l] + jnp.minimum(mix_token[:1, :1], 0.0)
            G["w_rec_in"] = _mm(
                f"d_rec_in_{l}", "tn", xn, dproj, grid=(1, N_CHIPS),
                a_spec=pl.BlockSpec((S, D), lambda i, j: (0, 0)),
                b_spec=pl.BlockSpec((S, CH), lambda i, j: (0, j)),
                out_shape=(N_CHIPS, D, CH), out_dtype=BF16,
                out_spec=pl.BlockSpec((None, D, CH), lambda i, j: (j, 0, 0)))
            dh, dhb, dgp = _mm(f"d_rec_xn_{l}", "nt", dproj, W["w_rec_in"], grid=(S // tx, 1),
                               a_spec=[pl.BlockSpec((tx, CH), functools.partial(lambda i, j, k: (i, k), k=k))
                                       for k in range(N_CHIPS)],
                               b_spec=[pl.BlockSpec((None, D, CH), functools.partial(lambda i, j, k: (k, 0, 0), k=k))
                                       for k in range(N_CHIPS)],
                               out_shape=(S, D), out_dtype=F32, out_spec=pl.BlockSpec((tx, D), lambda i, j: (i, 0)),
                               norm_bwd=(h_in, norm_mix, dh))
        else:
            xn, q, o, of, lse = mix_saved
            G["w_o"] = _mm_tn(f"d_o_proj_{l}", o, dhb, out_dtype=BF16, tn=D)
            do = _mm_nt(f"d_o_{l}", dhb, W["w_o"], out_dtype=BF16, tm=1024, tn=D)
            mix_token = layer_grads(l, "mix2", {"w_o": G["w_o"]})
            dq, dk, dv, dck, drq = _attn_bwd(f"attn_bwd_{l}", q, kv, c3, of, do, lse)
            dk_parts.append(dk)
            dv_parts.append(dv)
            dc_parts.append((dck + drq).reshape(2 * HP, S).T)
            G["w_q"] = _mm_tn(f"d_q_proj_{l}", xn, dq, out_dtype=BF16, tn=D)
            norm_mix = gains["mix"][l] if mix_token is None else gains["mix"][l] + jnp.minimum(mix_token[:1, :1], 0.0)
            dh, dhb, dgp = _mm_nt(f"d_q_xn_{l}", dq, W["w_q"], out_dtype=F32, tn=D, norm_bwd=(h_in, norm_mix, dh))
        G["norm_mix"] = jnp.sum(dgp, axis=0)
        if "w_kv" in W:
            dkv = _kv_sum("dkv_sum", dk_parts, dv_parts)
            dc = sum(dc_parts[1:], dc_parts[0])
            dc_pad = jnp.pad(dc, ((0, 0), (0, LANES - 2 * HP)))
            dls = _cumsum_rows("dc_cumsum", dc_pad, True)
            dfb, G["b_f"] = _logsig_bwd("d_logsig", dls, f_pre)
            G["w_kv"] = _mm_tn("d_kv_proj", hn_kv, dkv, out_dtype=BF16, tn=1024)
            G["w_f"] = _mm_tn("d_f_proj", hn_kv, dfb, out_dtype=F32)
            dhn_f = _mm_nt("d_f_hn", dfb, W["w_f"], out_dtype=F32, tn=D)
            dh, dhb, dgp = _mm_nt("d_kv_hn", dkv, W["w_kv"], out_dtype=F32, tn=D, res=dhn_f,
                                  norm_bwd=(h_kv, W["norm_kv"], dh))
            G["norm_kv"] = jnp.sum(dgp, axis=0)
        token = layer_grads(l, "mix", G)
    return loss_row, dh, dg_final


_ANY = pl.BlockSpec(memory_space=pl.ANY)


def _position():
    return lax.axis_index("x"), lax.axis_index("y"), lax.axis_index("c")


def _chip_peers(x, y):
    return [(1 - x, y), (x, 1 - y), (1 - x, 1 - y)]


def _half_rows(c, n):
    h = n // 2
    assert h % 16 == 0
    return pl.ds(pl.multiple_of(c * h, 16), h)


def _place_own(name, shard, layer, me):
    _, R, C = shard.shape
    tr = _row_tile(R, C, 2 * shard.dtype.itemsize, target=8 << 20)

    def body(me_ref, x_ref, o_ref):
        o_ref[...] = x_ref[...]

    return pl.pallas_call(
        body, name=name,
        grid_spec=pltpu.PrefetchScalarGridSpec(
            num_scalar_prefetch=1, grid=(R // tr,),
            in_specs=[pl.BlockSpec((None, tr, C), lambda i, me_ref: (layer, i, 0))],
            out_specs=pl.BlockSpec((None, tr, C), lambda i, me_ref: (me_ref[0], i, 0))),
        out_shape=_hbm_out((N_CHIPS, R, C), shard.dtype),
        compiler_params=_params(("parallel",)),
    )(me, shard)


def _gather_smalls(name, smalls):
    ns = len(smalls)

    def body(*refs):
        ins, outs = refs[:ns], refs[ns:2 * ns]
        send_sems, recv_sems, local_sems = refs[2 * ns:]
        x, y, c = _position()
        me = 2 * x + y
        peers = _chip_peers(x, y)

        def remote(t, k, chip):
            px, py = peers[k]
            return pltpu.make_async_remote_copy(
                src_ref=ins[t], dst_ref=outs[t].at[chip], send_sem=send_sems.at[3 * t + k],
                recv_sem=recv_sems.at[3 * t + k], device_id=(px, py, c), device_id_type=MESH)

        local = [pltpu.make_async_copy(ins[t], outs[t].at[me], local_sems.at[t]) for t in range(ns)]
        for t in range(ns):
            local[t].start()
            for k in range(3):
                remote(t, k, me).start()
        for t in range(ns):
            for k in range(3):
                px, py = peers[k]
                remote(t, k, 2 * px + py).wait_recv()
        for t in range(ns):
            for k in range(3):
                remote(t, k, me).wait_send()
            local[t].wait()

    return pl.pallas_call(
        body, name=name, in_specs=[_ANY] * ns, out_specs=[_ANY] * ns,
        out_shape=[_hbm_out((N_CHIPS,) + s.shape, s.dtype) for s in smalls],
        scratch_shapes=[pltpu.SemaphoreType.DMA((3 * ns,)), pltpu.SemaphoreType.DMA((3 * ns,)),
                        pltpu.SemaphoreType.DMA((ns,))],
    )(*smalls)


_SEM = pl.BlockSpec(memory_space=pltpu.SEMAPHORE)
_SPLIT = pltpu.CompilerParams(has_side_effects=pltpu.SideEffectType.DATAFLOW_SIDE_EFFECTING)


def _weight_copy(shards, buf, items, sems, i, k, chip_of_dst, peers, c):
    w, l = items[i]
    px, py = peers[k]
    half = _half_rows(c, shards[w].shape[1])
    return pltpu.make_async_remote_copy(
        src_ref=shards[w].at[l, half], dst_ref=buf.at[chip_of_dst, half],
        send_sem=sems[0].at[3 * i + k], recv_sem=sems[1].at[3 * i + k],
        device_id=(px, py, c), device_id_type=MESH)


def _gather_start(name, shards, bufs, items, after):
    nw, n = len(shards), len(bufs)

    def body(*refs):
        ins, outs, sems = refs[:nw], refs[nw + n + 1:nw + 2 * n + 1], refs[nw + 2 * n + 1:]
        x, y, c = _position()
        peers = _chip_peers(x, y)
        for i in range(n):
            for k in range(3):
                _weight_copy(ins, outs[i], items, sems, i, k, 2 * x + y, peers, c).start()

    res = pl.pallas_call(
        body, name=name, in_specs=[_ANY] * (nw + n + 1), out_specs=[_ANY] * n + [_SEM, _SEM],
        out_shape=[_hbm_out(b.shape, b.dtype) for b in bufs]
        + [pltpu.SemaphoreType.DMA((3 * n,)), pltpu.SemaphoreType.DMA((3 * n,))],
        input_output_aliases={nw + i: i for i in range(n)}, compiler_params=_SPLIT,
    )(*shards, *bufs, after)
    return res[:n], res[n:]


def _gather_wait(name, shards, bufs, items, ids, sems, after):
    nw, m = len(shards), len(ids)

    def body(*refs):
        ins, bs = refs[:nw], refs[nw:nw + m]
        sem_refs = refs[nw + m:nw + m + 2]
        x, y, c = _position()
        peers = _chip_peers(x, y)
        for j, i in enumerate(ids):
            for k in range(3):
                px, py = peers[k]
                _weight_copy(ins, bs[j], items, sem_refs, i, k, 2 * px + py, peers, c).wait_recv()
        for j, i in enumerate(ids):
            for k in range(3):
                _weight_copy(ins, bs[j], items, sem_refs, i, k, 2 * x + y, peers, c).wait_send()

    res = pl.pallas_call(
        body, name=name, in_specs=[_ANY] * (nw + m) + [_SEM, _SEM, _ANY], out_specs=[_ANY] * m,
        out_shape=[_hbm_out(bufs[i].shape, bufs[i].dtype) for i in ids],
        input_output_aliases={nw + j: j for j in range(m)}, compiler_params=_SPLIT,
    )(*shards, *[bufs[i] for i in ids], *sems, after)
    return list(res)


def _forward_copy(src, dst, sems, i, k, core):
    x, y, c = _position()
    px, py = _chip_peers(x, y)[k]
    half = _half_rows(core, src.shape[1])
    return pltpu.make_async_remote_copy(
        src_ref=src.at[2 * px + py, half], dst_ref=dst.at[2 * px + py, half],
        send_sem=sems[0].at[3 * i + k], recv_sem=sems[1].at[3 * i + k],
        device_id=(x, y, 1 - c), device_id_type=MESH)


def _forward_start(name, bufs):
    n = len(bufs)

    def body(*refs):
        ins, outs, sems = refs[:n], refs[n:2 * n], refs[2 * n:]
        c = lax.axis_index("c")
        for i in range(n):
            for k in range(3):
                _forward_copy(ins[i], outs[i], sems, i, k, c).start()

    res = pl.pallas_call(
        body, name=name, in_specs=[_ANY] * n, out_specs=[_ANY] * n + [_SEM, _SEM],
        out_shape=[_hbm_out(g.shape, g.dtype) for g in bufs]
        + [pltpu.SemaphoreType.DMA((3 * n,)), pltpu.SemaphoreType.DMA((3 * n,))],
        input_output_aliases={i: i for i in range(n)}, compiler_params=_SPLIT,
    )(*bufs)
    return list(res[:n]), res[n:]


def _forward_wait(name, bufs, sems, after):
    n = len(bufs)

    def body(*refs):
        bs, sem_refs = refs[:n], refs[n:n + 2]
        c = lax.axis_index("c")
        for i in range(n):
            for k in range(3):
                _forward_copy(bs[i], bs[i], sem_refs, i, k, 1 - c).wait_recv()
        for i in range(n):
            for k in range(3):
                _forward_copy(bs[i], bs[i], sem_refs, i, k, c).wait_send()

    return list(pl.pallas_call(
        body, name=name, in_specs=[_ANY] * n + [_SEM, _SEM, _ANY], out_specs=[_ANY] * n,
        out_shape=[_hbm_out(g.shape, g.dtype) for g in bufs],
        input_output_aliases={i: i for i in range(n)}, compiler_params=_SPLIT,
    )(*bufs, *sems, after))


def _reduce_copy(grads, others, sems, i):
    x, y, c = _position()
    return pltpu.make_async_remote_copy(
        src_ref=grads[i].at[:, _half_rows(1 - c, grads[i].shape[1])], dst_ref=others[i],
        send_sem=sems[0].at[i], recv_sem=sems[1].at[i], device_id=(x, y, 1 - c), device_id_type=MESH)


def _reduce_start(name, grads, after):
    n = len(grads)

    def body(*refs):
        ins, outs, sems, token = refs[:n], refs[n + 1:2 * n + 1], refs[2 * n + 1:2 * n + 3], refs[2 * n + 3]
        for i in range(n):
            _reduce_copy(ins, outs, sems, i).start()
        token[...] = jnp.zeros_like(token)

    res = pl.pallas_call(
        body, name=name, in_specs=[_ANY] * (n + 1),
        out_specs=[_ANY] * n + [_SEM, _SEM, pl.BlockSpec(memory_space=pltpu.VMEM)],
        out_shape=[_hbm_out((N_CHIPS, g.shape[1] // 2, g.shape[2]), g.dtype) for g in grads]
        + [pltpu.SemaphoreType.DMA((n,)), pltpu.SemaphoreType.DMA((n,)), jax.ShapeDtypeStruct((SUBLANES, LANES), F32)],
        compiler_params=_SPLIT,
    )(*grads, after)
    return list(res[:n]), res[n:n + 2], res[n + 2]


def _reduce_wait(name, grads, others, sems, after):
    n = len(grads)

    def body(*refs):
        ins, os_, sem_refs = refs[:n], refs[n:2 * n], refs[2 * n:2 * n + 2]
        for i in range(n):
            _reduce_copy(ins, os_, sem_refs, i).wait_recv()
        for i in range(n):
            _reduce_copy(ins, os_, sem_refs, i).wait_send()

    return list(pl.pallas_call(
        body, name=name, in_specs=[_ANY] * (2 * n) + [_SEM, _SEM, _ANY], out_specs=[_ANY] * n,
        out_shape=[_hbm_out(o.shape, o.dtype) for o in others],
        input_output_aliases={n + i: i for i in range(n)}, compiler_params=_SPLIT,
    )(*grads, *others, *sems, after))


def _sum_cores(name, g, other, core):
    _, R, C = g.shape
    H = R // 2
    tr = _row_tile(H, C, 3 * 2, target=12 << 20)
    nb = H // tr

    def body(c_ref, g_ref, o_ref, out_ref):
        out_ref[...] = (g_ref[...].astype(F32) + o_ref[...].astype(F32)).astype(out_ref.dtype)

    return pl.pallas_call(
        body, name=name,
        grid_spec=pltpu.PrefetchScalarGridSpec(
            num_scalar_prefetch=1, grid=(N_CHIPS, nb),
            in_specs=[pl.BlockSpec((None, tr, C), lambda j, i, c_ref: (j, c_ref[0] * nb + i, 0)),
                      pl.BlockSpec((None, tr, C), lambda j, i, c_ref: (j, i, 0))],
            out_specs=pl.BlockSpec((None, tr, C), lambda j, i, c_ref: (j, i, 0))),
        out_shape=_hbm_out((N_CHIPS, H, C), BF16),
        compiler_params=_params(("parallel", "parallel")),
    )(core, g, other)


def _sum_chips(name, received, own, full, layer, me_core):
    _, H, C = received.shape
    tr = _row_tile(H, C, 3 * 2 + 2 + 4, target=12 << 20)
    nb = H // tr

    def body(s_ref, r_ref, own_ref, full_ref, out_ref):
        acc = r_ref[0].astype(F32)
        for k in (1, 2):
            acc = acc + r_ref[k].astype(F32)
        out_ref[...] = acc + own_ref[...].astype(F32)

    return pl.pallas_call(
        body, name=name,
        grid_spec=pltpu.PrefetchScalarGridSpec(
            num_scalar_prefetch=1, grid=(nb,),
            in_specs=[pl.BlockSpec((3, tr, C), lambda i, s_ref: (0, i, 0)),
                      pl.BlockSpec((None, tr, C), lambda i, s_ref: (s_ref[0], i, 0)),
                      _ANY],
            out_specs=pl.BlockSpec((None, tr, C), lambda i, s_ref: (layer, s_ref[1] * nb + i, 0))),
        out_shape=_hbm_out(full.shape, full.dtype),
        input_output_aliases={3: 0},
        compiler_params=_params(("parallel",)),
    )(me_core, received, own, full)


def _part_copy(parts, recv, sems, i, k, peers, c):
    px, py = peers[k]
    return pltpu.make_async_remote_copy(
        src_ref=parts[i].at[2 * px + py], dst_ref=recv[i].at[k],
        send_sem=sems[0].at[3 * i + k], recv_sem=sems[1].at[3 * i + k],
        device_id=(px, py, c), device_id_type=MESH)


def _scatter_start(name, parts):
    n = len(parts)

    def body(*refs):
        ins, outs, sems, token = refs[:n], refs[n:2 * n], refs[2 * n:2 * n + 2], refs[2 * n + 2]
        x, y, c = _position()
        peers = _chip_peers(x, y)
        for i in range(n):
            for k in range(3):
                _part_copy(ins, outs, sems, i, k, peers, c).start()
        token[...] = jnp.zeros_like(token)

    res = pl.pallas_call(
        body, name=name, in_specs=[_ANY] * n,
        out_specs=[_ANY] * n + [_SEM, _SEM, pl.BlockSpec(memory_space=pltpu.VMEM)],
        out_shape=[_hbm_out((3,) + p.shape[1:], p.dtype) for p in parts]
        + [pltpu.SemaphoreType.DMA((3 * n,)), pltpu.SemaphoreType.DMA((3 * n,)),
           jax.ShapeDtypeStruct((SUBLANES, LANES), F32)],
        compiler_params=_SPLIT,
    )(*parts)
    return list(res[:n]), res[n:n + 2], res[n + 2]


def _scatter_reduce_start(name, parts, grads):
    n, m = len(parts), len(grads)

    def body(*refs):
        ps, gs = refs[:n], refs[n:n + m]
        recv, others = refs[n + m:2 * n + m], refs[2 * n + m:2 * (n + m)]
        ssems, rsems, token = refs[2 * (n + m):2 * (n + m) + 2], refs[2 * (n + m) + 2:2 * (n + m) + 4], refs[-1]
        x, y, c = _position()
        peers = _chip_peers(x, y)
        for i in range(n):
            for k in range(3):
                _part_copy(ps, recv, ssems, i, k, peers, c).start()
        for i in range(m):
            _reduce_copy(gs, others, rsems, i).start()
        token[...] = jnp.zeros_like(token)

    res = pl.pallas_call(
        body, name=name, in_specs=[_ANY] * (n + m),
        out_specs=[_ANY] * (n + m) + [_SEM] * 4 + [pl.BlockSpec(memory_space=pltpu.VMEM)],
        out_shape=[_hbm_out((3,) + p.shape[1:], p.dtype) for p in parts]
        + [_hbm_out((N_CHIPS, g.shape[1] // 2, g.shape[2]), g.dtype) for g in grads]
        + [pltpu.SemaphoreType.DMA((3 * n,)), pltpu.SemaphoreType.DMA((3 * n,)),
           pltpu.SemaphoreType.DMA((m,)), pltpu.SemaphoreType.DMA((m,)), jax.ShapeDtypeStruct((SUBLANES, LANES), F32)],
        compiler_params=_SPLIT,
    )(*parts, *grads)
    k = n + m
    return (list(res[:n]), res[k:k + 2]), (list(res[n:k]), res[k + 2:k + 4]), res[k + 4]


def _scatter_wait(name, parts, recv, sems, after):
    n = len(parts)

    def body(*refs):
        ins, rs, sem_refs = refs[:n], refs[n:2 * n], refs[2 * n:2 * n + 2]
        x, y, c = _position()
        peers = _chip_peers(x, y)
        for i in range(n):
            for k in range(3):
                _part_copy(ins, rs, sem_refs, i, k, peers, c).wait_recv()
        for i in range(n):
            for k in range(3):
                _part_copy(ins, rs, sem_refs, i, k, peers, c).wait_send()

    return list(pl.pallas_call(
        body, name=name, in_specs=[_ANY] * (2 * n) + [_SEM, _SEM, _ANY], out_specs=[_ANY] * n,
        out_shape=[_hbm_out(r.shape, r.dtype) for r in recv],
        input_output_aliases={n + i: i for i in range(n)}, compiler_params=_SPLIT,
    )(*parts, *recv, *sems, after))


def _share_copy(src, dst, sems, w, core):
    x, y, c = _position()
    half = _half_rows(core, src.shape[1])
    return pltpu.make_async_remote_copy(
        src_ref=src.at[:, half], dst_ref=dst.at[:, half], send_sem=sems[0].at[w], recv_sem=sems[1].at[w],
        device_id=(x, y, 1 - c), device_id_type=MESH)


def _share_start(name, full):
    n = len(full)

    def body(*refs):
        ins, outs, sems, token = refs[:n], refs[n:2 * n], refs[2 * n:2 * n + 2], refs[2 * n + 2]
        c = lax.axis_index("c")
        for w in range(n):
            _share_copy(ins[w], outs[w], sems, w, c).start()
        token[...] = jnp.zeros_like(token)

    res = pl.pallas_call(
        body, name=name, in_specs=[_ANY] * n, out_specs=[_ANY] * n + [_SEM, _SEM, pl.BlockSpec(memory_space=pltpu.VMEM)],
        out_shape=[_hbm_out(f.shape, f.dtype) for f in full]
        + [pltpu.SemaphoreType.DMA((n,)), pltpu.SemaphoreType.DMA((n,)), jax.ShapeDtypeStruct((SUBLANES, LANES), F32)],
        input_output_aliases={w: w for w in range(n)}, compiler_params=_SPLIT,
    )(*full)
    return list(res[:n]), res[n:n + 2], res[n + 2]


def _share_wait(name, full, sems, after):
    n = len(full)

    def body(*refs):
        fs, sem_refs = refs[:n], refs[n:n + 2]
        c = lax.axis_index("c")
        for w in range(n):
            _share_copy(fs[w], fs[w], sem_refs, w, 1 - c).wait_recv()
        for w in range(n):
            _share_copy(fs[w], fs[w], sem_refs, w, c).wait_send()

    return list(pl.pallas_call(
        body, name=name, in_specs=[_ANY] * n + [_SEM, _SEM, _ANY], out_specs=[_ANY] * n,
        out_shape=[_hbm_out(f.shape, f.dtype) for f in full],
        input_output_aliases={w: w for w in range(n)}, compiler_params=_SPLIT,
    )(*full, *sems, after))


def _all_copy(a_ref, o_ref, sems, k, slot):
    x, y, c = _position()
    return pltpu.make_async_remote_copy(
        src_ref=a_ref, dst_ref=o_ref.at[slot], send_sem=sems[0].at[k - 1], recv_sem=sems[1].at[k - 1],
        device_id=(x ^ ((k >> 2) & 1), y ^ ((k >> 1) & 1), c ^ (k & 1)), device_id_type=MESH)


def _gather_all_start(name, a):
    def body(a_ref, o_ref, send_sem, recv_sem, token):
        x, y, c = _position()
        for k in range(1, N_DEV):
            _all_copy(a_ref, o_ref, (send_sem, recv_sem), k, 4 * x + 2 * y + c).start()
        token[...] = jnp.zeros_like(token)

    out, send_sem, recv_sem, token = pl.pallas_call(
        body, name=name, in_specs=[_ANY], out_specs=[_ANY, _SEM, _SEM, pl.BlockSpec(memory_space=pltpu.VMEM)],
        out_shape=[_hbm_out((N_DEV,) + a.shape, a.dtype), pltpu.SemaphoreType.DMA((N_DEV - 1,)),
                   pltpu.SemaphoreType.DMA((N_DEV - 1,)), jax.ShapeDtypeStruct((SUBLANES, LANES), F32)],
        compiler_params=_SPLIT,
    )(a)
    return out, (send_sem, recv_sem), token


def _gather_all_wait(name, a, out, sems, after):
    def body(a_ref, o_ref, send_sem, recv_sem, after_ref, res_ref):
        x, y, c = _position()
        for k in range(1, N_DEV):
            peer = 4 * (x ^ ((k >> 2) & 1)) + 2 * (y ^ ((k >> 1) & 1)) + (c ^ (k & 1))
            _all_copy(a_ref, o_ref, (send_sem, recv_sem), k, peer).wait_recv()
        for k in range(1, N_DEV):
            _all_copy(a_ref, o_ref, (send_sem, recv_sem), k, 4 * x + 2 * y + c).wait_send()

    return pl.pallas_call(
        body, name=name, in_specs=[_ANY, _ANY, _SEM, _SEM, _ANY], out_specs=_ANY,
        out_shape=_hbm_out(out.shape, out.dtype), input_output_aliases={1: 0}, compiler_params=_SPLIT,
    )(a, out, *sems, after)


def _rows2d(a, lead=0):
    return a.reshape(a.shape[:lead] + (-1, a.shape[-1]))


def _row_tile(rows, cols, itemsize=4, target=1 << 20):
    want = max(SUBLANES, target // (cols * itemsize))
    t = min(rows, (want // 16) * 16)
    while t > 16 and rows % t:
        t -= 16
    return t if rows % t == 0 else rows


def _sum_slots(name, r, out_dtype=F32):
    ns = r.shape[0]
    r2 = _rows2d(r, 1)
    _, rows, cols = r2.shape
    tr = _row_tile(rows, cols)

    def body(r_ref, o_ref):
        acc = r_ref[0].astype(F32)
        for s in range(1, ns):
            acc = acc + r_ref[s].astype(F32)
        o_ref[...] = acc.astype(o_ref.dtype)

    out = pl.pallas_call(
        body, name=name, grid=(rows // tr,),
        in_specs=[pl.BlockSpec((ns, tr, cols), lambda i: (0, i, 0))],
        out_specs=pl.BlockSpec((tr, cols), lambda i: (i, 0)),
        out_shape=_hbm_out((rows, cols), out_dtype),
        compiler_params=_params(("parallel",)),
    )(r2)
    return out.reshape(r.shape[1:])


def _adamw(name, g_parts, w, m, v):
    shape = w.shape
    ng = len(g_parts)
    args = [_rows2d(a) for a in (*g_parts, w, m, v)]
    rows, cols = args[0].shape
    tr = _row_tile(rows, cols, (ng + 7) * 4, target=16 << 20)
    c1 = 1.0 - ADAM_B1 ** ADAM_STEP
    c2 = 1.0 - ADAM_B2 ** ADAM_STEP

    def body(*refs):
        g = refs[0][...]
        for r in refs[1:ng]:
            g = g + r[...]
        w_ref, m_ref, v_ref = refs[ng:ng + 3]
        g_out, d_out, m_out, v_out = refs[ng + 3:]
        mn = ADAM_B1 * m_ref[...] + (1.0 - ADAM_B1) * g
        vn = ADAM_B2 * v_ref[...] + (1.0 - ADAM_B2) * (g * g)
        m_hat = mn / c1
        v_hat = vn / c2
        g_out[...] = g
        d_out[...] = -ADAM_LR * (m_hat / (jnp.sqrt(v_hat) + ADAM_EPS) + ADAM_WD * w_ref[...])
        m_out[...] = mn
        v_out[...] = vn

    spec = pl.BlockSpec((tr, cols), lambda i: (i, 0))
    outs = pl.pallas_call(
        body, name=name, grid=(rows // tr,), in_specs=[spec] * (ng + 3), out_specs=[spec] * 4,
        out_shape=[_hbm_out((rows, cols), F32)] * 4,
        compiler_params=_params(("parallel",)),
    )(*args)
    return tuple(o.reshape(shape) for o in outs)


_WEIGHTS = ["norm_mix", "norm_ffn", "w_ffn_in", "w_ffn_out", "w_rec_in", "conv_w", "conv_b", "w_lru_gates",
            "b_lru_gates", "lru_param", "w_rec_out", "norm_kv", "w_kvf", "b_forget", "w_q", "w_o", "norm_final"]
_BIG = ["w_ffn_in", "w_ffn_out", "w_rec_in", "w_lru_gates", "w_rec_out", "w_kvf", "w_q", "w_o"]


def _stack3(a):
    return a[None] if a.ndim == 2 else a.reshape(a.shape[0], -1, a.shape[-1])


def _pad_lanes(a, n):
    return jnp.pad(a, ((0, 0),) * (a.ndim - 1) + ((0, n - a.shape[-1]),))


def kernel(x, norm_mix, norm_ffn, w_ffn_in, w_ffn_out, w_rec_in, conv_w, conv_b, w_lru_gates, b_lru_gates, lru_param, w_rec_out, norm_kv, w_kvf, b_forget, w_q, w_o, norm_final, loss_target, m_norm_mix, m_norm_ffn, m_w_ffn_in, m_w_ffn_out, m_w_rec_in, m_conv_w, m_conv_b, m_w_lru_gates, m_b_lru_gates, m_lru_param, m_w_rec_out, m_norm_kv, m_w_kvf, m_b_forget, m_w_q, m_w_o, m_norm_final, v_norm_mix, v_norm_ffn, v_w_ffn_in, v_w_ffn_out, v_w_rec_in, v_conv_w, v_conv_b, v_w_lru_gates, v_b_lru_gates, v_lru_param, v_w_rec_out, v_norm_kv, v_w_kvf, v_b_forget, v_w_q, v_w_o, v_norm_final):
    P = dict(norm_mix=norm_mix, norm_ffn=norm_ffn, w_ffn_in=w_ffn_in, w_ffn_out=w_ffn_out, w_rec_in=w_rec_in,
             conv_w=conv_w, conv_b=conv_b, w_lru_gates=w_lru_gates, b_lru_gates=b_lru_gates, lru_param=lru_param,
             w_rec_out=w_rec_out, norm_kv=norm_kv, w_kvf=w_kvf, b_forget=b_forget, w_q=w_q, w_o=w_o,
             norm_final=norm_final)
    M1 = dict(norm_mix=m_norm_mix, norm_ffn=m_norm_ffn, w_ffn_in=m_w_ffn_in, w_ffn_out=m_w_ffn_out,
              w_rec_in=m_w_rec_in, conv_w=m_conv_w, conv_b=m_conv_b, w_lru_gates=m_w_lru_gates,
              b_lru_gates=m_b_lru_gates, lru_param=m_lru_param, w_rec_out=m_w_rec_out, norm_kv=m_norm_kv,
              w_kvf=m_w_kvf, b_forget=m_b_forget, w_q=m_w_q, w_o=m_w_o, norm_final=m_norm_final)
    M2 = dict(norm_mix=v_norm_mix, norm_ffn=v_norm_ffn, w_ffn_in=v_w_ffn_in, w_ffn_out=v_w_ffn_out,
              w_rec_in=v_w_rec_in, conv_w=v_conv_w, conv_b=v_conv_b, w_lru_gates=v_w_lru_gates,
              b_lru_gates=v_b_lru_gates, lru_param=v_lru_param, w_rec_out=v_w_rec_out, norm_kv=v_norm_kv,
              w_kvf=v_w_kvf, b_forget=v_b_forget, w_q=v_w_q, w_o=v_w_o, norm_final=v_norm_final)

    _, S, D = x.shape
    L = norm_mix.shape[0]
    NA, NBLK, BW, GS = w_lru_gates.shape
    C = NBLK * BW
    CS = C // N_CHIPS
    H = b_forget.shape[0]
    assert C == D and H * HEAD_DIM == D and H <= LANES
    chip = 2 * lax.axis_index("x") + lax.axis_index("y")

    small_a = jnp.concatenate([conv_w, conv_b[:, None], lru_param[:, None]], axis=1)
    small_a, b_gates = _gather_smalls("gather_smalls", [small_a, b_lru_gates])
    small_a = small_a.transpose(1, 2, 0, 3).reshape(NA, 6, C)
    b_gates = b_gates.transpose(1, 2, 0, 3).reshape(NA, NBLK, 1, N_CHIPS * GS)
    shards = [_stack3(P[w]).astype(BF16) for w in _BIG]
    core = lax.axis_index("c")
    chip_id = jnp.reshape(chip, (1,)).astype(jnp.int32)
    core_id = jnp.reshape(core, (1,)).astype(jnp.int32)
    me_core = jnp.stack([chip, core]).astype(jnp.int32)

    parts_of_layer = ("mix", "mix2", "ffn")

    def part_items(l, part):
        if part == "ffn":
            names, at = ["w_ffn_in", "w_ffn_out"], l
        elif l < NA:
            names, at = (["w_rec_in"] if part == "mix" else ["w_lru_gates", "w_rec_out"]), l
        else:
            names, at = ((["w_kvf"] if l == NA else []) + ["w_q"] if part == "mix" else ["w_o"]), l - NA
        return [(_BIG.index(n), 0 if n == "w_kvf" else at) for n in names]

    def stage_of(l, part):
        return (l, part) if l == 0 or part == "ffn" else (l, "mixer")

    def stage_items(st):
        l, part = st
        return [it for p in (("mix", "mix2") if part == "mixer" else (part,)) for it in part_items(l, p)]

    stages = [(0, p) for p in parts_of_layer] + [(l, p) for l in range(1, L) for p in ("mixer", "ffn")]
    items = [it for st in stages for it in stage_items(st)]
    ids_of = {st: [items.index(it) for it in stage_items(st)] for st in stages}
    bufs = [_place_own(f"place_{_BIG[w]}_{li}", shards[w], li, chip_id) for w, li in items]
    bufs, gather_sems = _gather_start("gather_start", shards, bufs, items, small_a)

    forwarding, fetched = {}, {}

    def layer_prefetch(l, part, after):
        st = stage_of(l, part)
        if l < L and st not in forwarding:
            got = _gather_wait(f"gather_wait_{st[1]}_{l}", shards, bufs, items, ids_of[st], gather_sems, after)
            forwarding[st] = _forward_start(f"forward_start_{st[1]}_{l}", got)

    def layer_weights(l, part, after):
        if l >= L:
            return None
        st = stage_of(l, part)
        if st not in fetched:
            layer_prefetch(l, part, after)
            got, sems = forwarding[st]
            got = _forward_wait(f"forward_wait_{st[1]}_{l}", got, sems, after)
            fetched[st] = {_BIG[items[i][0]]: g for i, g in zip(ids_of[st], got)}
        B = fetched[st]
        if part == "ffn":
            return dict(w_ffn_in=B["w_ffn_in"], w_ffn_out=B["w_ffn_out"].reshape(-1, D))
        if l < NA and part == "mix":
            return dict(w_rec_in=B["w_rec_in"], conv_w=small_a[l, :4], conv_b=small_a[l, 4:5])
        if l < NA:
            return dict(w_gates=B["w_lru_gates"].reshape(N_CHIPS, NBLK, BW, GS).transpose(1, 2, 0, 3).reshape(
                NBLK, BW, N_CHIPS * GS), b_gates=b_gates[l], w_rec_out=B["w_rec_out"].reshape(C, D),
                lru_param=small_a[l, 5:6])
        if part == "mix2":
            return dict(w_o=B["w_o"].reshape(D, D))
        W = dict(w_q=B["w_q"].reshape(D, D))
        if l == NA:
            w_kvf_full = B["w_kvf"].transpose(1, 0, 2).reshape(D, -1)
            W.update(norm_kv=norm_kv[None], w_kv=w_kvf_full[:, :2 * D],
                     w_f=_pad_lanes(w_kvf_full[:, 2 * D:], LANES), b_f=_pad_lanes(b_forget[None], LANES))
        return W

    G_small = {l: {} for l in range(L)}
    stash = {st: {} for st in stages}
    pending = {}
    reducing = []

    def finish_reduce(after):
        st, its, grads, others, sems = reducing.pop()
        l, part = st
        others = _reduce_wait(f"reduce_wait_{part}_{l}", grads, others, sems, after)
        parts = [_sum_cores(f"sum_cores_{l}_{_BIG[w]}", g, o, core_id) for (w, _), g, o in zip(its, grads, others)]
        recv, sems, token = _scatter_start(f"scatter_start_{part}_{l}", parts)
        pending[st] = (parts, recv, sems)
        return token

    def layer_grads(l, part, G_part):
        G_small[l].update(G_part)
        st = stage_of(l, part)
        stash[st].update(G_part)
        if st[1] == "mixer" and part != "mix":
            return None
        G = stash[st]
        late = {"ffn": "w_ffn_in", "mix": "norm_mix"}.get(part) or ("w_gates" if l < NA else "w_o")
        by_name = dict(
            w_ffn_in=lambda: G["w_ffn_in"], w_ffn_out=lambda: G["w_ffn_out"].reshape(N_CHIPS, -1, D),
            w_rec_in=lambda: G["w_rec_in"],
            w_lru_gates=lambda: G["w_gates"].reshape(NBLK, BW, N_CHIPS, GS).transpose(2, 0, 1, 3).reshape(
                N_CHIPS, NBLK * BW, GS),
            w_rec_out=lambda: G["w_rec_out"].reshape(N_CHIPS, -1, D),
            w_kvf=lambda: jnp.concatenate([G["w_kv"].astype(F32), G["w_f"][:, :H]], axis=1).reshape(
                D, N_CHIPS, -1).transpose(1, 0, 2).astype(BF16),
            w_q=lambda: G["w_q"].reshape(N_CHIPS, -1, D), w_o=lambda: G["w_o"].reshape(N_CHIPS, -1, D))
        its = stage_items(st)
        grads = [by_name[_BIG[w]]() for w, _ in its]
        if reducing:
            pst, pits, pgrads, pothers, psems = reducing.pop()
            pothers = _reduce_wait(f"reduce_wait_{pst[1]}_{pst[0]}", pgrads, pothers, psems, G_part[late])
            pparts = [_sum_cores(f"sum_cores_{pst[0]}_{_BIG[w]}", g, o, core_id)
                      for (w, _), g, o in zip(pits, pgrads, pothers)]
            (recv, ssems), (others, sems), token = _scatter_reduce_start(
                f"scatter_reduce_start_{st[1]}_{l}", pparts, grads)
            pending[pst] = (pparts, recv, ssems)
        else:
            others, sems, token = _reduce_start(f"reduce_start_{st[1]}_{l}", grads, jnp.zeros((SUBLANES, LANES), F32))
        reducing.append((st, its, grads, others, sems))
        return finish_reduce(token) if l == 0 else token

    gains = dict(mix=[norm_mix[l][None] for l in range(L)], ffn=[norm_ffn[l][None] for l in range(L)],
                 final=norm_final[None])
    loss_row, grad_x, dg_final = _local_step(x.reshape(S, D), loss_target.reshape(S, D), gains,
                                             layer_weights, layer_prefetch, layer_grads)

    rows = [*[G_small[l]["norm_mix"] for l in range(L)], *[G_small[l]["norm_ffn"] for l in range(L)],
            G_small[NA]["norm_kv"], dg_final, _pad_lanes(G_small[NA]["b_f"], D), _pad_lanes(loss_row, D)]
    for a in range(NA):
        rows += [G_small[a][n] for n in ("conv_w", "conv_b", "b_gi", "b_gr", "lru_param")]
    packed = jnp.concatenate(rows, axis=0)
    everyone, small_sems, small_token = _gather_all_start("gather_small_start", packed)

    full = [lax.empty(sh.shape, F32) for sh in shards]
    for st in reversed(stages):
        l, part = st
        parts, recv, sems = pending[st]
        recv = _scatter_wait(f"scatter_wait_{part}_{l}", parts, recv, sems, small_token)
        for (w, li), own, r in zip(stage_items(st), parts, recv):
            full[w] = _sum_chips(f"sum_chips_{l}_{_BIG[w]}", r, own, full[w], li, me_core)
    full, share_sems, share_token = _share_start("share_start", full)

    everyone = _gather_all_wait("gather_small_wait", packed, everyone, small_sems, share_token)
    everyone = lax.dynamic_update_slice(everyone, packed[None], (2 * chip + core, 0, 0))
    tot = _sum_slots("sum_small", everyone)
    loss = tot[2 * L + 3, 0]
    g_rep = jnp.concatenate([tot[:2 * L + 2], tot[2 * L + 2:2 * L + 3]], axis=0)
    base = 2 * L + 4
    g_sh = []
    for a in range(NA):
        blk = lax.dynamic_slice_in_dim(tot[base + 8 * a:base + 8 * a + 8], chip * CS, CS, axis=1)
        gi = tot[base + 8 * a + 5].reshape(NBLK, BW)
        gr = tot[base + 8 * a + 6].reshape(NBLK, BW)
        bl = lax.dynamic_slice_in_dim(jnp.concatenate([gi, gr], axis=1), chip * GS, GS, axis=1)
        g_sh += [blk[:5], bl.reshape(-1, CS), blk[7:8]]
    g_sh = jnp.concatenate(g_sh, axis=0)
    nrow = g_sh.shape[0] // NA

    def pack_rep(T):
        return jnp.concatenate([T["norm_mix"], T["norm_ffn"], T["norm_kv"][None], T["norm_final"][None],
                                _pad_lanes(T["b_forget"][None], D)], axis=0)

    def pack_sh(T):
        return jnp.concatenate([jnp.concatenate([T["conv_w"][a], T["conv_b"][a][None],
                                                 T["b_lru_gates"][a].reshape(-1, CS), T["lru_param"][a][None]], axis=0)
                                for a in range(NA)], axis=0)

    rep = _adamw("adamw_replicated", [g_rep], pack_rep(P), pack_rep(M1), pack_rep(M2))
    shd = _adamw("adamw_small_sharded", [g_sh], pack_sh(P), pack_sh(M1), pack_sh(M2))

    full = _share_wait("share_wait", full, share_sems, shd[1])
    big = {w: _adamw(f"adamw_{w}", [g.reshape(P[w].shape)], P[w], M1[w], M2[w]) for w, g in zip(_BIG, full)}

    def unpack_rep(t):
        return dict(norm_mix=t[:L], norm_ffn=t[L:2 * L], norm_kv=t[2 * L], norm_final=t[2 * L + 1],
                    b_forget=t[2 * L + 2, :H])

    def unpack_sh(t):
        t = t.reshape(NA, nrow, CS)
        return dict(conv_w=t[:, :4], conv_b=t[:, 4], b_lru_gates=t[:, 5:nrow - 1].reshape(NA, NBLK, GS),
                    lru_param=t[:, nrow - 1])

    outs = []
    for i in range(4):
        small = {**unpack_rep(rep[i]), **unpack_sh(shd[i])}
        outs.append([big[w][i] if w in big else small[w] for w in _WEIGHTS])
    return (loss, grad_x.reshape(1, S, D), *outs[0], *outs[1], *outs[2], *outs[3])
```

```python
import functools
import math

import jax
import jax.numpy as jnp
from jax import lax
from jax.experimental import pallas as pl
from jax.experimental.pallas import tpu as pltpu

F32 = jnp.float32
BF16 = jnp.bfloat16

EPS = 1e-6
LRU_C = 8.0
HEAD_DIM = 64
LANES = 128
SUBLANES = 8
VMEM_LIMIT = 48 * 1024 * 1024
N_CHIPS = 4
N_DEV = 8

ADAM_LR = 0.001
ADAM_B1 = 0.9
ADAM_B2 = 0.999
ADAM_EPS = 1e-08
ADAM_WD = 0.01
ADAM_STEP = 10

_NN = (((1,), (0,)), ((), ()))
_NT = (((1,), (1,)), ((), ()))
_TN = (((0,), (0,)), ((), ()))
_DN = {"nn": _NN, "nt": _NT, "tn": _TN}
MESH = pl.DeviceIdType.MESH


def _hbm_out(shape, dtype):
    return pltpu.HBM(shape, dtype)


def _params(sem):
    return pltpu.CompilerParams(dimension_semantics=sem, vmem_limit_bytes=VMEM_LIMIT)


def _tile(n, want):
    if n <= want:
        return n
    t = (want // LANES) * LANES
    while t >= LANES:
        if n % t == 0:
            return t
        t -= LANES
    return n


def _sigmoid(x):
    return 1.0 / (1.0 + jnp.exp(-x))


def _sigmoid_t(x):
    return 0.5 * jnp.tanh(0.5 * x) + 0.5


def _softplus(x):
    return jnp.maximum(x, 0.0) + jnp.log(1.0 + jnp.exp(-jnp.abs(x)))


_GELU_C = math.sqrt(2.0 / math.pi)


def _gelu_and_grad(x):
    inner = _GELU_C * (x + 0.044715 * x * x * x)
    t = jnp.tanh(inner)
    g = 0.5 * x * (1.0 + t)
    dg = 0.5 * (1.0 + t) + 0.5 * x * (1.0 - t * t) * _GELU_C * (1.0 + 3.0 * 0.044715 * x * x)
    return g, dg


def _rms(x):
    return lax.rsqrt(jnp.mean(x * x, axis=-1, keepdims=True) + EPS)


def _rms_bwd(dy, x, g):
    r = _rms(x)
    xr = x * r
    dyg = dy * g
    return r * dyg - xr * (r * jnp.mean(dyg * xr, axis=-1, keepdims=True)), jnp.sum(dy * xr, axis=0, keepdims=True)


def _mm(name, mode, a, b, *, grid, a_spec, b_spec, out_shape, out_dtype, out_spec, nk=1,
        res=None, res_spec=None, bias=None, bias_spec=None, scale=None, norm_gain=None, norm_bwd=None):
    dn = _DN[mode]
    has_res, has_bias = res is not None, bias is not None
    blk = tuple(d for d in out_spec.block_shape if d is not None)
    vec = pl.BlockSpec((1, blk[-1]), lambda *g: (0, 0))
    a_specs = a_spec if isinstance(a_spec, list) else [a_spec]
    b_specs = b_spec if isinstance(b_spec, list) else [b_spec]
    npair = len(a_specs)
    n_in = 2 * npair + int(has_res) + int(has_bias) + (1 if norm_gain is not None else 0) + (3 if norm_bwd else 0)

    def body(*refs):
        p = 2 * npair
        r_ref = refs[p] if has_res else None
        p += int(has_res)
        bias_ref = refs[p] if has_bias else None
        p += int(has_bias)
        extra = refs[p:n_in]
        outs = refs[n_in:]
        o_ref = outs[0]
        part = lax.dot_general(refs[0][...], refs[npair][...], dn, preferred_element_type=F32)
        for t in range(1, npair):
            part = part + lax.dot_general(refs[t][...], refs[npair + t][...], dn, preferred_element_type=F32)

        def finish(acc):
            if scale is not None:
                acc = acc * scale
            if has_bias:
                acc = acc + bias_ref[...]
            if has_res:
                acc = r_ref[...] + acc
            if norm_bwd:
                h_ref, g_ref, dh_ref = extra
                dx, dg = _rms_bwd(acc, h_ref[...], g_ref[...])
                acc = dh_ref[...] + dx
                outs[1][...] = acc.astype(BF16)
                outs[2][...] = dg
            if norm_gain is not None:
                outs[1][...] = (acc * _rms(acc) * extra[0][...]).astype(BF16)
            o_ref[...] = acc.astype(o_ref.dtype)

        if nk == 1:
            finish(part)
        else:
            acc_ref = refs[-1]
            k = pl.program_id(2)

            @pl.when(k == 0)
            def _():
                acc_ref[...] = part

            @pl.when(k > 0)
            def _():
                acc_ref[...] += part

            @pl.when(k == nk - 1)
            def _():
                finish(acc_ref[...])

    ins, specs = [a] * npair + [b] * npair, a_specs + b_specs
    if has_res:
        ins.append(res)
        specs.append(res_spec)
    if has_bias:
        ins.append(bias)
        specs.append(bias_spec)
    out_specs, out_shapes = [out_spec], [_hbm_out(out_shape, out_dtype)]
    if norm_gain is not None:
        ins.append(norm_gain)
        specs.append(vec)
        out_specs.append(out_spec)
        out_shapes.append(_hbm_out(out_shape, BF16))
    if norm_bwd:
        h, g, dh = norm_bwd
        ins += [h, g, dh]
        specs += [out_spec, vec, out_spec]
        out_specs += [out_spec, pl.BlockSpec((None, 1, blk[-1]), lambda i, *rest: (i, 0, 0))]
        out_shapes += [_hbm_out(out_shape, BF16), _hbm_out((grid[0], 1, blk[-1]), F32)]
    sem = ("parallel", "parallel") + (("arbitrary",) if len(grid) == 3 else ())
    single = len(out_specs) == 1
    return pl.pallas_call(
        body, name=name, grid=grid, in_specs=specs, out_specs=out_specs[0] if single else out_specs,
        out_shape=out_shapes[0] if single else out_shapes,
        scratch_shapes=[pltpu.VMEM(blk, F32)] if nk > 1 else [],
        compiler_params=_params(sem),
    )(*ins)


def _mm_nn(name, a, b, *, b_lead=(), out_dtype, tm=512, tn=512, res=None, bias=None, scale=None, norm_gain=None):
    M, K = a.shape
    N = b.shape[-1]
    tm, tn = _tile(M, tm), _tile(N, tn)
    nl = len(b_lead)
    return _mm(
        name, "nn", a, b, grid=(M // tm, N // tn),
        a_spec=pl.BlockSpec((tm, K), lambda i, j: (i, 0)),
        b_spec=pl.BlockSpec((None,) * nl + (K, tn), lambda i, j: tuple(b_lead) + (0, j)),
        out_shape=(M, N), out_dtype=out_dtype, out_spec=pl.BlockSpec((tm, tn), lambda i, j: (i, j)),
        res=res, res_spec=pl.BlockSpec((tm, tn), lambda i, j: (i, j)),
        bias=bias, bias_spec=pl.BlockSpec((1, tn), lambda i, j: (0, j)), scale=scale, norm_gain=norm_gain)


def _mm_nt(name, a, b, *, b_lead=(), out_dtype, tm=512, tn=512, tk=2048, res=None, norm_bwd=None):
    M, K = a.shape
    N = b.shape[-2]
    tm, tn, tk = _tile(M, tm), _tile(N, tn), _tile(K, tk)
    nk = K // tk
    nl = len(b_lead)
    return _mm(
        name, "nt", a, b, grid=(M // tm, N // tn, nk), nk=nk,
        a_spec=pl.BlockSpec((tm, tk), lambda i, j, k: (i, k)),
        b_spec=pl.BlockSpec((None,) * nl + (tn, tk), lambda i, j, k: tuple(b_lead) + (j, k)),
        out_shape=(M, N), out_dtype=out_dtype, out_spec=pl.BlockSpec((tm, tn), lambda i, j, k: (i, j)),
        res=res, res_spec=pl.BlockSpec((tm, tn), lambda i, j, k: (i, j)), norm_bwd=norm_bwd)


def _mm_tn(name, a, b, *, out_dtype, tm=512, tn=512):
    S, M = a.shape
    N = b.shape[1]
    tm, tn = _tile(M, tm), _tile(N, tn)
    return _mm(
        name, "tn", a, b, grid=(M // tm, N // tn),
        a_spec=pl.BlockSpec((S, tm), lambda i, j: (0, i)),
        b_spec=pl.BlockSpec((S, tn), lambda i, j: (0, j)),
        out_shape=(M, N), out_dtype=out_dtype, out_spec=pl.BlockSpec((tm, tn), lambda i, j: (i, j)))


def _rmsnorm_fwd(name, h, g, tr=256):
    S, D = h.shape
    tr = _tile(S, tr)

    def body(h_ref, g_ref, o_ref):
        x = h_ref[...]
        r = lax.rsqrt(jnp.mean(x * x, axis=-1, keepdims=True) + EPS)
        o_ref[...] = (x * r * g_ref[...]).astype(o_ref.dtype)

    return pl.pallas_call(
        body, name=name, grid=(S // tr,),
        in_specs=[pl.BlockSpec((tr, D), lambda i: (i, 0)), pl.BlockSpec((1, D), lambda i: (0, 0))],
        out_specs=pl.BlockSpec((tr, D), lambda i: (i, 0)),
        out_shape=_hbm_out((S, D), BF16),
        compiler_params=_params(("parallel",)),
    )(h, g)


def _loss_head(name, h, target, g, tr=256):
    S, D = h.shape
    tr = _tile(S, tr)

    def body(h_ref, t_ref, g_ref, o_ref, ob_ref, dg_ref, loss_ref):
        i = pl.program_id(0)
        x = h_ref[...]
        gg = g_ref[...]
        r = lax.rsqrt(jnp.mean(x * x, axis=-1, keepdims=True) + EPS)
        xr = x * r
        err = xr * gg - t_ref[...]
        lpart = 0.5 * jnp.sum(jnp.mean(err * err, axis=-1, keepdims=True), axis=0, keepdims=True)
        dy = err * (1.0 / D)
        dyg = dy * gg
        dx = r * dyg - xr * (r * jnp.mean(dyg * xr, axis=-1, keepdims=True))
        o_ref[...] = dx
        ob_ref[...] = dx.astype(BF16)
        part = jnp.sum(dy * xr, axis=0, keepdims=True)
        lrow = jnp.broadcast_to(lpart, (1, LANES))

        @pl.when(i == 0)
        def _():
            dg_ref[...] = part
            loss_ref[...] = lrow

        @pl.when(i > 0)
        def _():
            dg_ref[...] += part
            loss_ref[...] += lrow

    row = pl.BlockSpec((tr, D), lambda i: (i, 0))
    vec = pl.BlockSpec((1, D), lambda i: (0, 0))
    return pl.pallas_call(
        body, name=name, grid=(S // tr,),
        in_specs=[row, row, vec], out_specs=[row, row, vec, pl.BlockSpec((1, LANES), lambda i: (0, 0))],
        out_shape=[_hbm_out((S, D), F32), _hbm_out((S, D), BF16),
                   _hbm_out((1, D), F32), _hbm_out((1, LANES), F32)],
        compiler_params=_params(("arbitrary",)),
    )(h, target, g)


def _swiglu_fwd(name, hn, w_in, tm=512):
    S, D = hn.shape
    FH = w_in.shape[-1]
    tm = _tile(S, tm)

    def body(x_ref, wg_ref, wu_ref, z_ref, a_ref):
        x = x_ref[...]
        zg = jnp.dot(x, wg_ref[...], preferred_element_type=F32)
        zu = jnp.dot(x, wu_ref[...], preferred_element_type=F32)
        sg = _sigmoid_t(zg)
        silu = zg * sg
        z_ref[0] = (zu * (sg * (1.0 + zg * (1.0 - sg)))).astype(z_ref.dtype)
        z_ref[1] = silu.astype(z_ref.dtype)
        a_ref[...] = (silu * zu).astype(a_ref.dtype)

    return pl.pallas_call(
        body, name=name, grid=(2, S // tm),
        in_specs=[pl.BlockSpec((tm, D), lambda j, i: (i, 0)),
                  pl.BlockSpec((None, D, FH), lambda j, i: (j, 0, 0)),
                  pl.BlockSpec((None, D, FH), lambda j, i: (j + 2, 0, 0))],
        out_specs=[pl.BlockSpec((2, tm, FH), lambda j, i: (0, i, j)), pl.BlockSpec((tm, FH), lambda j, i: (i, j))],
        out_shape=[_hbm_out((2, S, 2 * FH), BF16), _hbm_out((S, 2 * FH), BF16)],
        compiler_params=_params(("parallel", "parallel")),
    )(hn, w_in, w_in)


def _swiglu_bwd(name, dhb, w_out, z3, tm=512):
    S, D = dhb.shape
    F = w_out.shape[0]
    FH = F // 2
    tm = _tile(S, tm)

    def body(d_ref, w_ref, z_ref, dz_ref):
        d = lax.dot_general(d_ref[...], w_ref[...], _NT, preferred_element_type=F32)
        dz_ref[0] = (d * z_ref[0].astype(F32)).astype(dz_ref.dtype)
        dz_ref[1] = (d * z_ref[1].astype(F32)).astype(dz_ref.dtype)

    zspec = pl.BlockSpec((2, tm, FH), lambda j, i: (0, i, j))
    return pl.pallas_call(
        body, name=name, grid=(2, S // tm),
        in_specs=[pl.BlockSpec((tm, D), lambda j, i: (i, 0)), pl.BlockSpec((FH, D), lambda j, i: (j, 0)), zspec],
        out_specs=zspec, out_shape=_hbm_out((2, S, F), BF16),
        compiler_params=_params(("parallel", "parallel")),
    )(dhb, w_out, z3)


SCAN_ROWS = 64


def _group_scan(A, B, reverse):
    n = A.shape[0]
    sub = lax.broadcasted_iota(jnp.int32, A.shape, 0) % SUBLANES
    for d in (1, 2, 4):
        if reverse:
            A_sh, B_sh = pltpu.roll(A, n - d, 0), pltpu.roll(B, n - d, 0)
            keep = sub < SUBLANES - d
        else:
            A_sh, B_sh = pltpu.roll(A, d, 0), pltpu.roll(B, d, 0)
            keep = sub >= d
        B = jnp.where(keep, A * B_sh + B, B)
        A = jnp.where(keep, A * A_sh, A)
    return A, B


def _block_scan(a, u, carry, reverse):
    A, B = _group_scan(a, u, reverse)
    ng = a.shape[0] // SUBLANES
    out = [None] * ng
    order = range(ng - 1, -1, -1) if reverse else range(ng)
    for gi in order:
        sl = slice(gi * SUBLANES, (gi + 1) * SUBLANES)
        hg = A[sl] * carry + B[sl]
        out[gi] = hg
        carry = hg[0:1] if reverse else hg[SUBLANES - 1:SUBLANES]
    return jnp.concatenate(out, axis=0), carry


def _lru_gates(rc, gip, grp, sp):
    gi = _sigmoid_t(gip)
    gr = _sigmoid_t(grp)
    la = -LRU_C * gr * sp
    a = jnp.exp(la)
    om = -jnp.tanh(la) * (a * a + 1.0)
    mult = jnp.sqrt(om)
    return gi, gr, a, mult


def _lru_fwd(name, proj, rc, gip, grp, lru_p, tc=256):
    S, C = rc.shape
    tc = _tile(C, tc)
    nb = S // SCAN_ROWS

    def body(gb_ref, rc_ref, gi_ref, gr_ref, l_ref, h_ref, m_ref):
        sp = _softplus(-l_ref[...])

        def step(b, carry):
            rows = pl.ds(pl.multiple_of(b * SCAN_ROWS, SCAN_ROWS), SCAN_ROWS)
            rcb = rc_ref[rows, :]
            gi, _, a, mult = _lru_gates(rcb, gi_ref[rows, :], gr_ref[rows, :], sp)
            h, carry = _block_scan(a, rcb * gi * mult, carry, False)
            h_ref[rows, :] = h
            gel, _ = _gelu_and_grad(gb_ref[rows, :])
            m_ref[rows, :] = (gel * h).astype(m_ref.dtype)
            return carry

        lax.fori_loop(0, nb, step, jnp.zeros((1, tc), F32))

    col = pl.BlockSpec((S, tc), lambda j: (0, j))
    return pl.pallas_call(
        body, name=name, grid=(C // tc,),
        in_specs=[col, col, col, col, pl.BlockSpec((1, tc), lambda j: (0, j))],
        out_specs=[col, col],
        out_shape=[_hbm_out((S, C), F32), _hbm_out((S, C), BF16)],
        compiler_params=_params(("parallel",)),
    )(proj, rc, gip, grp, lru_p)


def _lru_bwd(name, dm, proj, hrec, rc, gip, grp, lru_p, tc=256):
    S, C = rc.shape
    tc = _tile(C, tc)
    nb = S // SCAN_ROWS
    R = SCAN_ROWS

    def body(dm_ref, gb_ref, h_ref, rc_ref, gi_ref, gr_ref, l_ref,
             dgb_ref, dgi_ref, dgr_ref, drc_ref, dbi_ref, dbr_ref, dl_ref):
        lp = l_ref[...]
        sp = _softplus(-lp)
        row = lax.broadcasted_iota(jnp.int32, (R, tc), 0)
        zero = jnp.zeros((1, tc), F32)

        def step(t, carry):
            mu_in, s_i, s_r, s_sp = carry
            b = nb - 1 - t
            r0 = pl.multiple_of(b * R, R)
            rows = pl.ds(r0, R)
            rcb = rc_ref[rows, :]
            gi, gr, a, mult = _lru_gates(rcb, gi_ref[rows, :], gr_ref[rows, :], sp)
            gel, dgel = _gelu_and_grad(gb_ref[rows, :])
            dmb = dm_ref[rows, :]
            h = h_ref[rows, :]
            dgb_ref[rows, :] = (dmb * h * dgel).astype(dgb_ref.dtype)
            dh = dmb * gel
            mu, mu_out = _block_scan(a, a * dh, mu_in, True)
            mu_next = jnp.where(row == R - 1, mu_in, pltpu.roll(mu, R - 1, 0))
            lam = dh + mu_next
            p0 = pl.multiple_of(jnp.maximum(r0 - SUBLANES, 0), SUBLANES)
            prev = h_ref[pl.ds(p0, SUBLANES), :][SUBLANES - 1:SUBLANES]
            prev = jnp.where(b > 0, prev, 0.0)
            h_prev = jnp.where(row == 0, prev, pltpu.roll(h, 1, 0))
            da = lam * h_prev
            d_mult = lam * rcb * gi
            d_la = da * a - d_mult * (a * a) / mult
            d_grp = d_la * (-LRU_C * sp) * gr * (1.0 - gr)
            d_gip = lam * rcb * mult * gi * (1.0 - gi)
            dgr_ref[rows, :] = d_grp.astype(dgr_ref.dtype)
            dgi_ref[rows, :] = d_gip.astype(dgi_ref.dtype)
            drc_ref[rows, :] = lam * gi * mult
            s_i = s_i + jnp.sum(d_gip, axis=0, keepdims=True)
            s_r = s_r + jnp.sum(d_grp, axis=0, keepdims=True)
            s_sp = s_sp + jnp.sum(d_la * gr, axis=0, keepdims=True)
            return mu_out, s_i, s_r, s_sp

        _, s_i, s_r, s_sp = lax.fori_loop(0, nb, step, (zero, zero, zero, zero))
        dbi_ref[...] = s_i
        dbr_ref[...] = s_r
        dl_ref[...] = (-LRU_C * s_sp) * (-_sigmoid(-lp))

    col = pl.BlockSpec((S, tc), lambda j: (0, j))
    vec = pl.BlockSpec((1, tc), lambda j: (0, j))
    return pl.pallas_call(
        body, name=name, grid=(C // tc,),
        in_specs=[col, col, col, col, col, col, vec],
        out_specs=[col, col, col, col, vec, vec, vec],
        out_shape=[_hbm_out((S, C), BF16), _hbm_out((S, C), BF16),
                   _hbm_out((S, C), BF16), _hbm_out((S, C), F32),
                   _hbm_out((1, C), F32), _hbm_out((1, C), F32),
                   _hbm_out((1, C), F32)],
        compiler_params=_params(("parallel",)),
    )(dm, proj, hrec, rc, gip, grp, lru_p)


def _cumsum_rows(name, u, reverse):
    S, C = u.shape
    nb = S // SCAN_ROWS

    def body(u_ref, o_ref):
        def step(t, carry):
            b = nb - 1 - t if reverse else t
            rows = pl.ds(pl.multiple_of(b * SCAN_ROWS, SCAN_ROWS), SCAN_ROWS)
            ub = u_ref[rows, :]
            h, carry = _block_scan(jnp.ones_like(ub), ub, carry, reverse)
            o_ref[rows, :] = h
            return carry

        lax.fori_loop(0, nb, step, jnp.zeros((1, C), F32))

    spec = pl.BlockSpec((S, C), lambda i: (0, 0))
    return pl.pallas_call(
        body, name=name, grid=(1,), in_specs=[spec], out_specs=spec,
        out_shape=_hbm_out((S, C), F32),
        compiler_params=_params(("arbitrary",)),
    )(u)


def _shift_down(x, k):
    row = lax.broadcasted_iota(jnp.int32, x.shape, 0)
    return jnp.where(row >= k, pltpu.roll(x, k, 0), 0.0)


def _shift_up(x, k):
    n = x.shape[0]
    row = lax.broadcasted_iota(jnp.int32, x.shape, 0)
    return jnp.where(row < n - k, pltpu.roll(x, n - k, 0), 0.0)


def _conv_fwd(name, proj, w, b, tc=256):
    S, C2 = proj.shape
    C = C2 // 2
    tc = _tile(C, tc)
    off = C // tc

    def body(x_ref, w_ref, b_ref, o_ref, ob_ref):
        x = x_ref[...]
        out = b_ref[...] + w_ref[3:4, :] * x
        for k in (1, 2, 3):
            out = out + w_ref[3 - k:4 - k, :] * _shift_down(x, k)
        o_ref[...] = out
        ob_ref[...] = out.astype(BF16)

    col = pl.BlockSpec((S, tc), lambda j: (0, j))
    return pl.pallas_call(
        body, name=name, grid=(C // tc,),
        in_specs=[pl.BlockSpec((S, tc), lambda j: (0, off + j)),
                  pl.BlockSpec((4, tc), lambda j: (0, j)), pl.BlockSpec((1, tc), lambda j: (0, j))],
        out_specs=[col, col],
        out_shape=[_hbm_out((S, C), F32), _hbm_out((S, C), BF16)],
        compiler_params=_params(("parallel",)),
    )(proj, w, b)


def _conv_bwd(name, drc, proj, w, tc=256):
    S, C = drc.shape
    tc = _tile(C, tc)
    off = C // tc

    def body(y_ref, x_ref, w_ref, dx_ref, dw_ref, db_ref):
        y = y_ref[...]
        x = x_ref[...]
        dx = w_ref[3:4, :] * y
        dw_ref[3:4, :] = jnp.sum(y * x, axis=0, keepdims=True)
        for k in (1, 2, 3):
            dx = dx + w_ref[3 - k:4 - k, :] * _shift_up(y, k)
            dw_ref[3 - k:4 - k, :] = jnp.sum(y * _shift_down(x, k), axis=0, keepdims=True)
        dx_ref[...] = dx.astype(dx_ref.dtype)
        db_ref[...] = jnp.sum(y, axis=0, keepdims=True)

    col = pl.BlockSpec((S, tc), lambda j: (0, j))
    return pl.pallas_call(
        body, name=name, grid=(C // tc,),
        in_specs=[col, pl.BlockSpec((S, tc), lambda j: (0, off + j)), pl.BlockSpec((4, tc), lambda j: (0, j))],
        out_specs=[col, pl.BlockSpec((4, tc), lambda j: (0, j)), pl.BlockSpec((1, tc), lambda j: (0, j))],
        out_shape=[_hbm_out((S, C), BF16), _hbm_out((4, C), F32),
                   _hbm_out((1, C), F32)],
        compiler_params=_params(("parallel",)),
    )(drc, proj, w)


def _gates_fwd(name, rcb, wg, bg):
    S, C = rcb.shape
    nblk, bw, _ = wg.shape

    def body(x_ref, w_ref, b_ref, gi_ref, gr_ref):
        g = jnp.dot(x_ref[...], w_ref[...], preferred_element_type=F32) + b_ref[...]
        gi_ref[...] = g[:, :bw]
        gr_ref[...] = g[:, bw:]

    col = pl.BlockSpec((S, bw), lambda n: (0, n))
    return pl.pallas_call(
        body, name=name, grid=(nblk,),
        in_specs=[col, pl.BlockSpec((None, bw, 2 * bw), lambda n: (n, 0, 0)),
                  pl.BlockSpec((None, 1, 2 * bw), lambda n: (n, 0, 0))],
        out_specs=[col, col],
        out_shape=[_hbm_out((S, C), F32), _hbm_out((S, C), F32)],
        compiler_params=_params(("parallel",)),
    )(rcb, wg, bg)


def _gates_bwd(name, dgi, dgr, rcb, wg, drc1):
    S, C = rcb.shape
    nblk, bw, _ = wg.shape

    def body(dgi_ref, dgr_ref, x_ref, w_ref, d1_ref, drc_ref, dw_ref):
        w = w_ref[...]
        x = x_ref[...]
        di, dr = dgi_ref[...], dgr_ref[...]
        drc_ref[...] = (d1_ref[...]
                        + lax.dot_general(di, w[:, :bw], _NT, preferred_element_type=F32)
                        + lax.dot_general(dr, w[:, bw:], _NT, preferred_element_type=F32))
        dw_ref[:, :bw] = lax.dot_general(x, di, _TN, preferred_element_type=F32).astype(dw_ref.dtype)
        dw_ref[:, bw:] = lax.dot_general(x, dr, _TN, preferred_element_type=F32).astype(dw_ref.dtype)

    col = pl.BlockSpec((S, bw), lambda n: (0, n))
    wspec = pl.BlockSpec((None, bw, 2 * bw), lambda n: (n, 0, 0))
    return pl.pallas_call(
        body, name=name, grid=(nblk,),
        in_specs=[col, col, col, wspec, col], out_specs=[col, wspec],
        out_shape=[_hbm_out((S, C), F32), _hbm_out((nblk, bw, 2 * bw), BF16)],
        compiler_params=_params(("parallel",)),
    )(dgi, dgr, rcb, wg, drc1)


def _att_tile(S):
    return next(t for t in (512, 256, 128) if S % t == 0)


def _head_lanes(shape):
    return lax.broadcasted_iota(jnp.int32, shape, len(shape) - 1) < HEAD_DIM


def _key_bias(c_ref, rows, hh):
    return jnp.broadcast_to(c_ref[rows, hh:hh + 1], (rows.size, LANES))


def _over_keys(x, op):
    n = x.shape[0]
    while n > SUBLANES:
        n //= 2
        x = op(x[:n], x[n:2 * n])
    return (jnp.max if op is jnp.maximum else jnp.sum)(x, axis=0, keepdims=True)


def _causal_t(T, cc):
    r = lax.broadcasted_iota(jnp.int32, (T, LANES), 0)
    c = lax.broadcasted_iota(jnp.int32, (T, LANES), 1) + cc * LANES
    return r <= c


def _attn_fwd(name, q, kv, c3):
    S, D = q.shape
    HP = D // LANES
    T = _att_tile(S)
    nq = S // T
    NC = T // LANES

    def body(q_ref, k_ref, v_ref, c_ref, o_ref, of_ref, lse_ref, bias, vT, acc, m_scr, l_scr):
        def prologue(i, _):
            rows = pl.ds(pl.multiple_of(i * T, T), T)
            for hh in range(2):
                bias[hh, rows, :] = _key_bias(c_ref, rows, hh)
            vT[i] = v_ref[rows, :].astype(F32).T.astype(BF16)
            return 0

        lax.fori_loop(0, nq, prologue, 0)

        def q_step(qi, _):
            q0 = pl.multiple_of(qi * T, T)
            qb = q_ref[pl.ds(q0, T), :]
            m_scr[...] = jnp.full(m_scr.shape, -jnp.inf, F32)
            l_scr[...] = jnp.zeros(l_scr.shape, F32)
            acc[...] = jnp.zeros(acc.shape, F32)

            def tile(kj, masked):
                ks = pl.ds(pl.multiple_of(kj * T, T), T)
                kf = k_ref[ks, :].astype(F32)
                first = _head_lanes(kf.shape)
                kms = [jnp.where(first if hh == 0 else jnp.logical_not(first), kf, 0.0).astype(BF16) for hh in range(2)]
                sTs = [lax.dot_general(km, qb, _NT, preferred_element_type=F32) for km in kms]
                for hh in range(2):
                    b = bias[hh, ks, :]
                    ps, alphas = [], []
                    for cc in range(NC):
                        cols = slice(cc * LANES, (cc + 1) * LANES)
                        s = sTs[hh][:, cols] + b
                        if masked:
                            s = jnp.where(_causal_t(T, cc), s, -jnp.inf)
                        m_old = m_scr[hh, cc]
                        m_new = jnp.maximum(m_old, _over_keys(s, jnp.maximum))
                        alpha = jnp.exp(m_old - m_new)
                        p = jnp.exp(s - m_new)
                        l_scr[hh, cc] = alpha * l_scr[hh, cc] + _over_keys(p, jnp.add)
                        m_scr[hh, cc] = m_new
                        ps.append(p.astype(BF16))
                        alphas.append(alpha)
                    acc[hh] = acc[hh] * jnp.concatenate(alphas, axis=1) + jnp.dot(
                        vT[kj, hh * HEAD_DIM:(hh + 1) * HEAD_DIM, :], jnp.concatenate(ps, axis=1),
                        preferred_element_type=F32)

            def inner(kj, _):
                tile(kj, False)
                return 0

            lax.fori_loop(0, qi, inner, 0)
            tile(qi, True)
            outs = []
            for hh in range(2):
                inv = jnp.concatenate([1.0 / l_scr[hh, cc] for cc in range(NC)], axis=1)
                outs.append(acc[hh] * inv)
                for cc in range(NC):
                    lse_ref[hh:hh + 1, pl.ds(q0 + cc * LANES, LANES)] = m_scr[hh, cc] + jnp.log(l_scr[hh, cc])
            out = jnp.concatenate(outs, axis=0).T
            o_ref[pl.ds(q0, T), :] = out.astype(o_ref.dtype)
            of_ref[pl.ds(q0, T), :] = out
            return 0

        lax.fori_loop(0, nq, q_step, 0)

    blk = lambda off: pl.BlockSpec((S, LANES), lambda p: (0, off + p))
    return pl.pallas_call(
        body, name=name, grid=(HP,),
        in_specs=[blk(0), blk(0), blk(HP), pl.BlockSpec((None, S, 2), lambda p: (p, 0, 0))],
        out_specs=[blk(0), blk(0), pl.BlockSpec((None, 2, S), lambda p: (p, 0, 0))],
        out_shape=[_hbm_out((S, D), BF16), _hbm_out((S, D), F32),
                   _hbm_out((HP, 2, S), F32)],
        scratch_shapes=[pltpu.VMEM((2, S, LANES), F32), pltpu.VMEM((nq, LANES, T), BF16),
                        pltpu.VMEM((2, HEAD_DIM, T), F32), pltpu.VMEM((2, NC, 1, LANES), F32),
                        pltpu.VMEM((2, NC, 1, LANES), F32)],
        compiler_params=_params(("parallel",)),
    )(q, kv, kv, c3)


def _attn_bwd(name, q, kv, c3, of, do, lse3):
    S, D = q.shape
    HP = D // LANES
    T = _att_tile(S)
    nq = S // T
    NC = T // LANES
    scale = HEAD_DIM ** -0.5

    def body(q_ref, k_ref, v_ref, c_ref, of_ref, do_ref, lse_ref,
             dq_ref, dk_ref, dv_ref, dck_ref, drq_ref, bias, kT, dqT, delta, dr_scr, dk_acc, dv_acc, dc_acc):
        def prologue(i, _):
            rows = pl.ds(pl.multiple_of(i * T, T), T)
            for hh in range(2):
                bias[hh, rows, :] = _key_bias(c_ref, rows, hh)
            kT[i] = k_ref[rows, :].astype(F32).T.astype(BF16)
            prodT = (do_ref[rows, :].astype(F32) * of_ref[rows, :]).T
            for hh in range(2):
                delta[hh:hh + 1, rows] = jnp.sum(prodT[hh * HEAD_DIM:(hh + 1) * HEAD_DIM], axis=0, keepdims=True)
            dqT[i] = jnp.zeros((LANES, T), F32)
            return 0

        lax.fori_loop(0, nq, prologue, 0)
        dr_scr[...] = jnp.zeros(dr_scr.shape, F32)

        def kv_step(kj, _):
            ks = pl.ds(pl.multiple_of(kj * T, T), T)
            kf = k_ref[ks, :].astype(F32)
            vf = v_ref[ks, :].astype(F32)
            first = _head_lanes(kf.shape)
            masks = [first, jnp.logical_not(first)]
            kms = [jnp.where(m, kf, 0.0).astype(BF16) for m in masks]
            vms = [jnp.where(m, vf, 0.0).astype(BF16) for m in masks]

            for acc in (dk_acc, dv_acc, dc_acc):
                acc[...] = jnp.zeros(acc.shape, F32)

            def tile(qi, masked):
                q0 = pl.multiple_of(qi * T, T)
                qb = q_ref[pl.ds(q0, T), :]
                dob = do_ref[pl.ds(q0, T), :]
                sTs = [lax.dot_general(km, qb, _NT, preferred_element_type=F32) for km in kms]
                dpTs = [lax.dot_general(vm, dob, _NT, preferred_element_type=F32) for vm in vms]
                for hh in range(2):
                    b = bias[hh, ks, :]
                    head = slice(hh * HEAD_DIM, (hh + 1) * HEAD_DIM)
                    ps, dss = [], []
                    for cc in range(NC):
                        cols = slice(cc * LANES, (cc + 1) * LANES)
                        at = pl.ds(q0 + cc * LANES, LANES)
                        p = jnp.exp(sTs[hh][:, cols] + b - lse_ref[hh:hh + 1, at])
                        if masked:
                            p = jnp.where(_causal_t(T, cc), p, 0.0)
                        ds = p * (dpTs[hh][:, cols] - delta[hh:hh + 1, at])
                        ps.append(p.astype(BF16))
                        dss.append(ds.astype(BF16))
                        dc_acc[hh] += ds
                        dr_scr[hh:hh + 1, at] += _over_keys(ds, jnp.add)
                    pT = jnp.concatenate(ps, axis=1)
                    dsT = jnp.concatenate(dss, axis=1)
                    dv_acc[hh] += jnp.dot(pT, dob, preferred_element_type=F32)
                    dk_acc[hh] += jnp.dot(dsT, qb, preferred_element_type=F32)
                    dqT[qi, head, :] += jnp.dot(kT[kj, head, :], dsT, preferred_element_type=F32)

            def inner(qi, _):
                tile(qi, False)
                return 0

            tile(kj, True)
            lax.fori_loop(kj + 1, nq, inner, 0)
            dk_ref[ks, :] = jnp.where(first, dk_acc[0], dk_acc[1])
            dv_ref[ks, :] = jnp.where(first, dv_acc[0], dv_acc[1])
            for hh in range(2):
                dck_ref[hh:hh + 1, ks] = -jnp.sum(dc_acc[hh].T, axis=0, keepdims=True)
            return 0

        lax.fori_loop(0, nq, kv_step, 0)

        def epilogue(i, _):
            rows = pl.ds(pl.multiple_of(i * T, T), T)
            dq_ref[rows, :] = (dqT[i].T * scale).astype(dq_ref.dtype)
            return 0

        lax.fori_loop(0, nq, epilogue, 0)
        drq_ref[...] = dr_scr[...]

    blk = lambda off: pl.BlockSpec((S, LANES), lambda p: (0, off + p))
    row_spec = pl.BlockSpec((None, 2, S), lambda p: (p, 0, 0))
    return pl.pallas_call(
        body, name=name, grid=(HP,),
        in_specs=[blk(0), blk(0), blk(HP), pl.BlockSpec((None, S, 2), lambda p: (p, 0, 0)), blk(0), blk(0), row_spec],
        out_specs=[blk(0), blk(0), blk(0), row_spec, row_spec],
        out_shape=[_hbm_out((S, D), BF16), _hbm_out((S, D), F32),
                   _hbm_out((S, D), F32), _hbm_out((HP, 2, S), F32),
                   _hbm_out((HP, 2, S), F32)],
        scratch_shapes=[pltpu.VMEM((2, S, LANES), F32), pltpu.VMEM((nq, LANES, T), BF16),
                        pltpu.VMEM((nq, LANES, T), F32), pltpu.VMEM((2, S), F32), pltpu.VMEM((2, S), F32)]
        + [pltpu.VMEM((2, T, LANES), F32)] * 3,
        compiler_params=_params(("parallel",)),
    )(q, kv, kv, c3, of, do, lse3)


def _logsig_fwd(name, f):
    S, C = f.shape

    def body(f_ref, o_ref):
        o_ref[...] = -_softplus(-f_ref[...])

    spec = pl.BlockSpec((S, C), lambda i: (0, 0))
    return pl.pallas_call(body, name=name, grid=(1,), in_specs=[spec], out_specs=spec,
                          out_shape=_hbm_out((S, C), F32),
                          compiler_params=_params(("arbitrary",)))(f)


def _logsig_bwd(name, dls, f):
    S, C = f.shape

    def body(d_ref, f_ref, o_ref, s_ref):
        df = d_ref[...] * _sigmoid(-f_ref[...])
        o_ref[...] = df.astype(o_ref.dtype)
        s_ref[...] = jnp.sum(df, axis=0, keepdims=True)

    spec = pl.BlockSpec((S, C), lambda i: (0, 0))
    return pl.pallas_call(body, name=name, grid=(1,), in_specs=[spec, spec],
                          out_specs=[spec, pl.BlockSpec((1, C), lambda i: (0, 0))],
                          out_shape=[_hbm_out((S, C), BF16), _hbm_out((1, C), F32)],
                          compiler_params=_params(("arbitrary",)))(dls, f)


def _kv_sum(name, dk_parts, dv_parts, tr=256):
    S, D = dk_parts[0].shape
    tr = _tile(S, tr)
    n = len(dk_parts)

    def body(*refs):
        for half, group in enumerate((refs[:n], refs[n:2 * n])):
            acc = group[0][...]
            for r in group[1:]:
                acc = acc + r[...]
            refs[2 * n][:, half * D:(half + 1) * D] = acc.astype(BF16)

    spec = pl.BlockSpec((tr, D), lambda i: (i, 0))
    return pl.pallas_call(body, name=name, grid=(S // tr,), in_specs=[spec] * (2 * n),
                          out_specs=pl.BlockSpec((tr, 2 * D), lambda i: (i, 0)),
                          out_shape=_hbm_out((S, 2 * D), BF16),
                          compiler_params=_params(("parallel",)))(*dk_parts, *dv_parts)


def _local_step(x, target, gains, layer_weights, layer_prefetch, layer_grads):
    S, D = x.shape
    HP = D // LANES
    scale = HEAD_DIM ** -0.5
    tx = _tile(S, 256)
    saved = []
    h = x
    l = 0
    kv = c3 = f_pre = hn_kv = h_kv = None
    while True:
        W = layer_weights(l, "mix", h)
        if W is None:
            break
        recurrent = "w_rec_in" in W
        if l == 0:
            xn = _rmsnorm_fwd("mix_norm_0", h, gains["mix"][0])
        if recurrent:
            CH = W["w_rec_in"].shape[-1]
            C = 2 * CH
            proj = _mm(f"rec_in_{l}", "nn", xn, W["w_rec_in"], grid=(1, N_CHIPS),
                       a_spec=pl.BlockSpec((S, D), lambda i, j: (0, 0)),
                       b_spec=pl.BlockSpec((None, D, CH), lambda i, j: (j, 0, 0)),
                       out_shape=(S, 2 * C), out_dtype=F32,
                       out_spec=pl.BlockSpec((S, CH), lambda i, j: (0, j)))
            layer_prefetch(l, "mix2", proj)
            rc, rcb = _conv_fwd(f"conv_{l}", proj, W["conv_w"], W["conv_b"])
            W = {**W, **layer_weights(l, "mix2", rcb)}
            gip, grp = _gates_fwd(f"gates_{l}", rcb, W["w_gates"], W["b_gates"])
            hrec, m = _lru_fwd(f"lru_{l}", proj, rc, gip, grp, W["lru_param"])
            layer_prefetch(l, "ffn", m)
            h_mid, hn = _mm_nn(f"rec_out_{l}", m, W["w_rec_out"], out_dtype=F32, res=h, tn=D, norm_gain=gains["ffn"][l])
            mix_saved = (xn, proj, rc, rcb, gip, grp, hrec, m)
        else:
            if "w_kv" in W:
                h_kv = h
                hn_kv = _rmsnorm_fwd("kv_norm", h, W["norm_kv"])
                kv = _mm_nn("kv_proj", hn_kv, W["w_kv"], out_dtype=BF16, tm=1024, tn=1024)
                f_pre = _mm_nn("f_proj", hn_kv, W["w_f"], out_dtype=F32, bias=W["b_f"])
                c = _cumsum_rows("c_cumsum", _logsig_fwd("logsig", f_pre), False)
                c3 = (-c[:, :2 * HP]).reshape(S, HP, 2).transpose(1, 0, 2)
            q = _mm_nn(f"q_proj_{l}", xn, W["w_q"], out_dtype=BF16, scale=scale, tm=1024, tn=1024)
            layer_prefetch(l, "mix2", q)
            o, of, lse = _attn_fwd(f"attn_fwd_{l}", q, kv, c3)
            W = {**W, **layer_weights(l, "mix2", o)}
            layer_prefetch(l, "ffn", o)
            h_mid, hn = _mm_nn(f"o_proj_{l}", o, W["w_o"], out_dtype=F32, res=h, tn=D, norm_gain=gains["ffn"][l])
            mix_saved = (xn, q, o, of, lse)
        W = {**W, **layer_weights(l, "ffn", h_mid)}
        z3, act = _swiglu_fwd(f"ffn_in_{l}", hn, W["w_ffn_in"])
        layer_prefetch(l + 1, "mix", act)
        saved.append((W, h, h_mid, mix_saved, (hn, z3, act)))
        l += 1
        if l < len(gains["mix"]):
            h, xn = _mm_nn(f"ffn_out_{l - 1}", act, W["w_ffn_out"], out_dtype=F32, res=h_mid, tn=D,
                           norm_gain=gains["mix"][l])
        else:
            h = _mm_nn(f"ffn_out_{l - 1}", act, W["w_ffn_out"], out_dtype=F32, res=h_mid, tn=D)

    dh, dhb, dg_final, loss_row = _loss_head("loss_head", h, target, gains["final"])

    dk_parts, dv_parts, dc_parts = [], [], []
    token = None
    for l in reversed(range(len(saved))):
        W, h_in, h_mid, mix_saved, (hn, z3, act) = saved[l]
        recurrent = "w_rec_in" in W
        FH = W["w_ffn_in"].shape[-1]
        G = {}
        norm_ffn = gains["ffn"][l]
        if token is not None:
            norm_ffn = norm_ffn + jnp.minimum(token[:1, :1], 0.0)
        G["w_ffn_out"] = _mm_tn(f"d_ffn_out_{l}", act, dhb, out_dtype=BF16, tn=D)
        dz3 = _swiglu_bwd(f"d_act_{l}", dhb, W["w_ffn_out"], z3)
        G["w_ffn_in"] = _mm(
            f"d_ffn_in_{l}", "tn", hn, dz3, grid=(1, N_CHIPS),
            a_spec=pl.BlockSpec((S, D), lambda i, j: (0, 0)),
            b_spec=pl.BlockSpec((None, S, FH), lambda i, j: (j // 2, 0, j % 2)),
            out_shape=(N_CHIPS, D, FH), out_dtype=BF16,
            out_spec=pl.BlockSpec((None, D, FH), lambda i, j: (j, 0, 0)))
        ffn_token = layer_grads(l, "ffn", G)
        G = {}
        if ffn_token is not None:
            norm_ffn = norm_ffn + jnp.minimum(ffn_token[:1, :1], 0.0)
        dh, dhb, dgp = _mm(f"d_ffn_hn_{l}", "nt", dz3, W["w_ffn_in"], grid=(S // tx, 1),
                           a_spec=[pl.BlockSpec((None, tx, FH), functools.partial(lambda i, j, k: (k // 2, i, k % 2), k=k))
                                   for k in range(N_CHIPS)],
                           b_spec=[pl.BlockSpec((None, D, FH), functools.partial(lambda i, j, k: (k, 0, 0), k=k))
                                   for k in range(N_CHIPS)],
                           out_shape=(S, D), out_dtype=F32, out_spec=pl.BlockSpec((tx, D), lambda i, j: (i, 0)),
                           norm_bwd=(h_mid, norm_ffn, dh))
        G["norm_ffn"] = jnp.sum(dgp, axis=0)
        if recurrent:
            CH = W["w_rec_in"].shape[-1]
            C = 2 * CH
            xn, proj, rc, rcb, gip, grp, hrec, m = mix_saved
            G["w_rec_out"] = _mm_tn(f"d_rec_out_{l}", m, dhb, out_dtype=BF16, tn=D)
            dm = _mm_nt(f"d_m_{l}", dhb, W["w_rec_out"], out_dtype=F32, tn=C)
            dgb, dgi, dgr, drc1, G["b_gi"], G["b_gr"], G["lru_param"] = _lru_bwd(
                f"d_lru_{l}", dm, proj, hrec, rc, gip, grp, W["lru_param"])
            drc, G["w_gates"] = _gates_bwd(f"d_gates_{l}", dgi, dgr, rcb, W["w_gates"], drc1)
            mix_token = layer_grads(l, "mix2", {n: G[n] for n in ("w_rec_out", "w_gates")})
            drec, G["conv_w"], G["conv_b"] = _conv_bwd(f"d_conv_{l}", drc, proj, W["conv_w"])
            dproj = jnp.concatenate([dgb, drec], axis=1)
            norm_mix = gains["mix"][l] if mix_token is None else gains["mix"][l] + jnp.minimum(mix_token[:1, :1], 0.0)
            G["w_rec_in"] = _mm(
                f"d_rec_in_{l}", "tn", xn, dproj, grid=(1, N_CHIPS),
                a_spec=pl.BlockSpec((S, D), lambda i, j: (0, 0)),
                b_spec=pl.BlockSpec((S, CH), lambda i, j: (0, j)),
                out_shape=(N_CHIPS, D, CH), out_dtype=BF16,
                out_spec=pl.BlockSpec((None, D, CH), lambda i, j: (j, 0, 0)))
            dh, dhb, dgp = _mm(f"d_rec_xn_{l}", "nt", dproj, W["w_rec_in"], grid=(S // tx, 1),
                               a_spec=[pl.BlockSpec((tx, CH), functools.partial(lambda i, j, k: (i, k), k=k))
                                       for k in range(N_CHIPS)],
                               b_spec=[pl.BlockSpec((None, D, CH), functools.partial(lambda i, j, k: (k, 0, 0), k=k))
                                       for k in range(N_CHIPS)],
                               out_shape=(S, D), out_dtype=F32, out_spec=pl.BlockSpec((tx, D), lambda i, j: (i, 0)),
                               norm_bwd=(h_in, norm_mix, dh))
        else:
            xn, q, o, of, lse = mix_saved
            G["w_o"] = _mm_tn(f"d_o_proj_{l}", o, dhb, out_dtype=BF16, tn=D)
            do = _mm_nt(f"d_o_{l}", dhb, W["w_o"], out_dtype=BF16, tm=1024, tn=D)
            mix_token = layer_grads(l, "mix2", {"w_o": G["w_o"]})
            dq, dk, dv, dck, drq = _attn_bwd(f"attn_bwd_{l}", q, kv, c3, of, do, lse)
            dk_parts.append(dk)
            dv_parts.append(dv)
            dc_parts.append((dck + drq).reshape(2 * HP, S).T)
            G["w_q"] = _mm_tn(f"d_q_proj_{l}", xn, dq, out_dtype=BF16, tn=D)
            norm_mix = gains["mix"][l] if mix_token is None else gains["mix"][l] + jnp.minimum(mix_token[:1, :1], 0.0)
            dh, dhb, dgp = _mm_nt(f"d_q_xn_{l}", dq, W["w_q"], out_dtype=F32, tn=D, norm_bwd=(h_in, norm_mix, dh))
        G["norm_mix"] = jnp.sum(dgp, axis=0)
        if "w_kv" in W:
            dkv = _kv_sum("dkv_sum", dk_parts, dv_parts)
            dc = sum(dc_parts[1:], dc_parts[0])
            dc_pad = jnp.pad(dc, ((0, 0), (0, LANES - 2 * HP)))
            dls = _cumsum_rows("dc_cumsum", dc_pad, True)
            dfb, G["b_f"] = _logsig_bwd("d_logsig", dls, f_pre)
            G["w_kv"] = _mm_tn("d_kv_proj", hn_kv, dkv, out_dtype=BF16, tn=1024)
            G["w_f"] = _mm_tn("d_f_proj", hn_kv, dfb, out_dtype=F32)
            dhn_f = _mm_nt("d_f_hn", dfb, W["w_f"], out_dtype=F32, tn=D)
            dh, dhb, dgp = _mm_nt("d_kv_hn", dkv, W["w_kv"], out_dtype=F32, tn=D, res=dhn_f,
                                  norm_bwd=(h_kv, W["norm_kv"], dh))
            G["norm_kv"] = jnp.sum(dgp, axis=0)
        token = layer_grads(l, "mix", G)
    return loss_row, dh, dg_final


_ANY = pl.BlockSpec(memory_space=pl.ANY)


def _position():
    return lax.axis_index("x"), lax.axis_index("y"), lax.axis_index("c")


def _chip_peers(x, y):
    return [(1 - x, y), (x, 1 - y), (1 - x, 1 - y)]


def _half_rows(c, n):
    h = n // 2
    assert h % 16 == 0
    return pl.ds(pl.multiple_of(c * h, 16), h)


def _place_own(name, shard, layer, me):
    _, R, C = shard.shape
    tr = _row_tile(R, C, 2 * shard.dtype.itemsize, target=8 << 20)

    def body(me_ref, x_ref, o_ref):
        o_ref[...] = x_ref[...]

    return pl.pallas_call(
        body, name=name,
        grid_spec=pltpu.PrefetchScalarGridSpec(
            num_scalar_prefetch=1, grid=(R // tr,),
            in_specs=[pl.BlockSpec((None, tr, C), lambda i, me_ref: (layer, i, 0))],
            out_specs=pl.BlockSpec((None, tr, C), lambda i, me_ref: (me_ref[0], i, 0))),
        out_shape=_hbm_out((N_CHIPS, R, C), shard.dtype),
        compiler_params=_params(("parallel",)),
    )(me, shard)


def _gather_smalls(name, smalls):
    ns = len(smalls)

    def body(*refs):
        ins, outs = refs[:ns], refs[ns:2 * ns]
        send_sems, recv_sems, local_sems = refs[2 * ns:]
        x, y, c = _position()
        me = 2 * x + y
        peers = _chip_peers(x, y)

        def remote(t, k, chip):
            px, py = peers[k]
            return pltpu.make_async_remote_copy(
                src_ref=ins[t], dst_ref=outs[t].at[chip], send_sem=send_sems.at[3 * t + k],
                recv_sem=recv_sems.at[3 * t + k], device_id=(px, py, c), device_id_type=MESH)

        local = [pltpu.make_async_copy(ins[t], outs[t].at[me], local_sems.at[t]) for t in range(ns)]
        for t in range(ns):
            local[t].start()
            for k in range(3):
                remote(t, k, me).start()
        for t in range(ns):
            for k in range(3):
                px, py = peers[k]
                remote(t, k, 2 * px + py).wait_recv()
        for t in range(ns):
            for k in range(3):
                remote(t, k, me).wait_send()
            local[t].wait()

    return pl.pallas_call(
        body, name=name, in_specs=[_ANY] * ns, out_specs=[_ANY] * ns,
        out_shape=[_hbm_out((N_CHIPS,) + s.shape, s.dtype) for s in smalls],
        scratch_shapes=[pltpu.SemaphoreType.DMA((3 * ns,)), pltpu.SemaphoreType.DMA((3 * ns,)),
                        pltpu.SemaphoreType.DMA((ns,))],
    )(*smalls)


_SEM = pl.BlockSpec(memory_space=pltpu.SEMAPHORE)
_SPLIT = pltpu.CompilerParams(has_side_effects=pltpu.SideEffectType.DATAFLOW_SIDE_EFFECTING)


def _weight_copy(shards, buf, items, sems, i, k, chip_of_dst, peers, c):
    w, l = items[i]
    px, py = peers[k]
    half = _half_rows(c, shards[w].shape[1])
    return pltpu.make_async_remote_copy(
        src_ref=shards[w].at[l, half], dst_ref=buf.at[chip_of_dst, half],
        send_sem=sems[0].at[3 * i + k], recv_sem=sems[1].at[3 * i + k],
        device_id=(px, py, c), device_id_type=MESH)


def _gather_start(name, shards, bufs, items, after):
    nw, n = len(shards), len(bufs)

    def body(*refs):
        ins, outs, sems = refs[:nw], refs[nw + n + 1:nw + 2 * n + 1], refs[nw + 2 * n + 1:]
        x, y, c = _position()
        peers = _chip_peers(x, y)
        for i in range(n):
            for k in range(3):
                _weight_copy(ins, outs[i], items, sems, i, k, 2 * x + y, peers, c).start()

    res = pl.pallas_call(
        body, name=name, in_specs=[_ANY] * (nw + n + 1), out_specs=[_ANY] * n + [_SEM, _SEM],
        out_shape=[_hbm_out(b.shape, b.dtype) for b in bufs]
        + [pltpu.SemaphoreType.DMA((3 * n,)), pltpu.SemaphoreType.DMA((3 * n,))],
        input_output_aliases={nw + i: i for i in range(n)}, compiler_params=_SPLIT,
    )(*shards, *bufs, after)
    return res[:n], res[n:]


def _gather_wait(name, shards, bufs, items, ids, sems, after):
    nw, m = len(shards), len(ids)

    def body(*refs):
        ins, bs = refs[:nw], refs[nw:nw + m]
        sem_refs = refs[nw + m:nw + m + 2]
        x, y, c = _position()
        peers = _chip_peers(x, y)
        for j, i in enumerate(ids):
            for k in range(3):
                px, py = peers[k]
                _weight_copy(ins, bs[j], items, sem_refs, i, k, 2 * px + py, peers, c).wait_recv()
        for j, i in enumerate(ids):
            for k in range(3):
                _weight_copy(ins, bs[j], items, sem_refs, i, k, 2 * x + y, peers, c).wait_send()

    res = pl.pallas_call(
        body, name=name, in_specs=[_ANY] * (nw + m) + [_SEM, _SEM, _ANY], out_specs=[_ANY] * m,
        out_shape=[_hbm_out(bufs[i].shape, bufs[i].dtype) for i in ids],
        input_output_aliases={nw + j: j for j in range(m)}, compiler_params=_SPLIT,
    )(*shards, *[bufs[i] for i in ids], *sems, after)
    return list(res)


def _forward_copy(src, dst, sems, i, k, core):
    x, y, c = _position()
    px, py = _chip_peers(x, y)[k]
    half = _half_rows(core, src.shape[1])
    return pltpu.make_async_remote_copy(
        src_ref=src.at[2 * px + py, half], dst_ref=dst.at[2 * px + py, half],
        send_sem=sems[0].at[3 * i + k], recv_sem=sems[1].at[3 * i + k],
        device_id=(x, y, 1 - c), device_id_type=MESH)


def _forward_start(name, bufs):
    n = len(bufs)

    def body(*refs):
        ins, outs, sems = refs[:n], refs[n:2 * n], refs[2 * n:]
        c = lax.axis_index("c")
        for i in range(n):
            for k in range(3):
                _forward_copy(ins[i], outs[i], sems, i, k, c).start()

    res = pl.pallas_call(
        body, name=name, in_specs=[_ANY] * n, out_specs=[_ANY] * n + [_SEM, _SEM],
        out_shape=[_hbm_out(g.shape, g.dtype) for g in bufs]
        + [pltpu.SemaphoreType.DMA((3 * n,)), pltpu.SemaphoreType.DMA((3 * n,))],
        input_output_aliases={i: i for i in range(n)}, compiler_params=_SPLIT,
    )(*bufs)
    return list(res[:n]), res[n:]


def _forward_wait(name, bufs, sems, after):
    n = len(bufs)

    def body(*refs):
        bs, sem_refs = refs[:n], refs[n:n + 2]
        c = lax.axis_index("c")
        for i in range(n):
            for k in range(3):
                _forward_copy(bs[i], bs[i], sem_refs, i, k, 1 - c).wait_recv()
        for i in range(n):
            for k in range(3):
                _forward_copy(bs[i], bs[i], sem_refs, i, k, c).wait_send()

    return list(pl.pallas_call(
        body, name=name, in_specs=[_ANY] * n + [_SEM, _SEM, _ANY], out_specs=[_ANY] * n,
        out_shape=[_hbm_out(g.shape, g.dtype) for g in bufs],
        input_output_aliases={i: i for i in range(n)}, compiler_params=_SPLIT,
    )(*bufs, *sems, after))


def _reduce_copy(grads, others, sems, i):
    x, y, c = _position()
    return pltpu.make_async_remote_copy(
        src_ref=grads[i].at[:, _half_rows(1 - c, grads[i].shape[1])], dst_ref=others[i],
        send_sem=sems[0].at[i], recv_sem=sems[1].at[i], device_id=(x, y, 1 - c), device_id_type=MESH)


def _reduce_start(name, grads, after):
    n = len(grads)

    def body(*refs):
        ins, outs, sems, token = refs[:n], refs[n + 1:2 * n + 1], refs[2 * n + 1:2 * n + 3], refs[2 * n + 3]
        for i in range(n):
            _reduce_copy(ins, outs, sems, i).start()
        token[...] = jnp.zeros_like(token)

    res = pl.pallas_call(
        body, name=name, in_specs=[_ANY] * (n + 1),
        out_specs=[_ANY] * n + [_SEM, _SEM, pl.BlockSpec(memory_space=pltpu.VMEM)],
        out_shape=[_hbm_out((N_CHIPS, g.shape[1] // 2, g.shape[2]), g.dtype) for g in grads]
        + [pltpu.SemaphoreType.DMA((n,)), pltpu.SemaphoreType.DMA((n,)), jax.ShapeDtypeStruct((SUBLANES, LANES), F32)],
        compiler_params=_SPLIT,
    )(*grads, after)
    return list(res[:n]), res[n:n + 2], res[n + 2]


def _reduce_wait(name, grads, others, sems, after):
    n = len(grads)

    def body(*refs):
        ins, os_, sem_refs = refs[:n], refs[n:2 * n], refs[2 * n:2 * n + 2]
        for i in range(n):
            _reduce_copy(ins, os_, sem_refs, i).wait_recv()
        for i in range(n):
            _reduce_copy(ins, os_, sem_refs, i).wait_send()

    return list(pl.pallas_call(
        body, name=name, in_specs=[_ANY] * (2 * n) + [_SEM, _SEM, _ANY], out_specs=[_ANY] * n,
        out_shape=[_hbm_out(o.shape, o.dtype) for o in others],
        input_output_aliases={n + i: i for i in range(n)}, compiler_params=_SPLIT,
    )(*grads, *others, *sems, after))


def _sum_cores(name, g, other, core):
    _, R, C = g.shape
    H = R // 2
    tr = _row_tile(H, C, 3 * 2, target=12 << 20)
    nb = H // tr

    def body(c_ref, g_ref, o_ref, out_ref):
        out_ref[...] = (g_ref[...].astype(F32) + o_ref[...].astype(F32)).astype(out_ref.dtype)

    return pl.pallas_call(
        body, name=name,
        grid_spec=pltpu.PrefetchScalarGridSpec(
            num_scalar_prefetch=1, grid=(N_CHIPS, nb),
            in_specs=[pl.BlockSpec((None, tr, C), lambda j, i, c_ref: (j, c_ref[0] * nb + i, 0)),
                      pl.BlockSpec((None, tr, C), lambda j, i, c_ref: (j, i, 0))],
            out_specs=pl.BlockSpec((None, tr, C), lambda j, i, c_ref: (j, i, 0))),
        out_shape=_hbm_out((N_CHIPS, H, C), BF16),
        compiler_params=_params(("parallel", "parallel")),
    )(core, g, other)


def _sum_chips(name, received, own, full, layer, me_core):
    _, H, C = received.shape
    tr = _row_tile(H, C, 3 * 2 + 2 + 4, target=12 << 20)
    nb = H // tr

    def body(s_ref, r_ref, own_ref, full_ref, out_ref):
        acc = r_ref[0].astype(F32)
        for k in (1, 2):
            acc = acc + r_ref[k].astype(F32)
        out_ref[...] = acc + own_ref[...].astype(F32)

    return pl.pallas_call(
        body, name=name,
        grid_spec=pltpu.PrefetchScalarGridSpec(
            num_scalar_prefetch=1, grid=(nb,),
            in_specs=[pl.BlockSpec((3, tr, C), lambda i, s_ref: (0, i, 0)),
                      pl.BlockSpec((None, tr, C), lambda i, s_ref: (s_ref[0], i, 0)),
                      _ANY],
            out_specs=pl.BlockSpec((None, tr, C), lambda i, s_ref: (layer, s_ref[1] * nb + i, 0))),
        out_shape=_hbm_out(full.shape, full.dtype),
        input_output_aliases={3: 0},
        compiler_params=_params(("parallel",)),
    )(me_core, received, own, full)


def _part_copy(parts, recv, sems, i, k, peers, c):
    px, py = peers[k]
    return pltpu.make_async_remote_copy(
        src_ref=parts[i].at[2 * px + py], dst_ref=recv[i].at[k],
        send_sem=sems[0].at[3 * i + k], recv_sem=sems[1].at[3 * i + k],
        device_id=(px, py, c), device_id_type=MESH)


def _scatter_start(name, parts):
    n = len(parts)

    def body(*refs):
        ins, outs, sems, token = refs[:n], refs[n:2 * n], refs[2 * n:2 * n + 2], refs[2 * n + 2]
        x, y, c = _position()
        peers = _chip_peers(x, y)
        for i in range(n):
            for k in range(3):
                _part_copy(ins, outs, sems, i, k, peers, c).start()
        token[...] = jnp.zeros_like(token)

    res = pl.pallas_call(
        body, name=name, in_specs=[_ANY] * n,
        out_specs=[_ANY] * n + [_SEM, _SEM, pl.BlockSpec(memory_space=pltpu.VMEM)],
        out_shape=[_hbm_out((3,) + p.shape[1:], p.dtype) for p in parts]
        + [pltpu.SemaphoreType.DMA((3 * n,)), pltpu.SemaphoreType.DMA((3 * n,)),
           jax.ShapeDtypeStruct((SUBLANES, LANES), F32)],
        compiler_params=_SPLIT,
    )(*parts)
    return list(res[:n]), res[n:n + 2], res[n + 2]


def _scatter_reduce_start(name, parts, grads):
    n, m = len(parts), len(grads)

    def body(*refs):
        ps, gs = refs[:n], refs[n:n + m]
        recv, others = refs[n + m:2 * n + m], refs[2 * n + m:2 * (n + m)]
        ssems, rsems, token = refs[2 * (n + m):2 * (n + m) + 2], refs[2 * (n + m) + 2:2 * (n + m) + 4], refs[-1]
        x, y, c = _position()
        peers = _chip_peers(x, y)
        for i in range(n):
            for k in range(3):
                _part_copy(ps, recv, ssems, i, k, peers, c).start()
        for i in range(m):
            _reduce_copy(gs, others, rsems, i).start()
        token[...] = jnp.zeros_like(token)

    res = pl.pallas_call(
        body, name=name, in_specs=[_ANY] * (n + m),
        out_specs=[_ANY] * (n + m) + [_SEM] * 4 + [pl.BlockSpec(memory_space=pltpu.VMEM)],
        out_shape=[_hbm_out((3,) + p.shape[1:], p.dtype) for p in parts]
        + [_hbm_out((N_CHIPS, g.shape[1] // 2, g.shape[2]), g.dtype) for g in grads]
        + [pltpu.SemaphoreType.DMA((3 * n,)), pltpu.SemaphoreType.DMA((3 * n,)),
           pltpu.SemaphoreType.DMA((m,)), pltpu.SemaphoreType.DMA((m,)), jax.ShapeDtypeStruct((SUBLANES, LANES), F32)],
        compiler_params=_SPLIT,
    )(*parts, *grads)
    k = n + m
    return (list(res[:n]), res[k:k + 2]), (list(res[n:k]), res[k + 2:k + 4]), res[k + 4]


def _scatter_wait(name, parts, recv, sems, after):
    n = len(parts)

    def body(*refs):
        ins, rs, sem_refs = refs[:n], refs[n:2 * n], refs[2 * n:2 * n + 2]
        x, y, c = _position()
        peers = _chip_peers(x, y)
        for i in range(n):
            for k in range(3):
                _part_copy(ins, rs, sem_refs, i, k, peers, c).wait_recv()
        for i in range(n):
            for k in range(3):
                _part_copy(ins, rs, sem_refs, i, k, peers, c).wait_send()

    return list(pl.pallas_call(
        body, name=name, in_specs=[_ANY] * (2 * n) + [_SEM, _SEM, _ANY], out_specs=[_ANY] * n,
        out_shape=[_hbm_out(r.shape, r.dtype) for r in recv],
        input_output_aliases={n + i: i for i in range(n)}, compiler_params=_SPLIT,
    )(*parts, *recv, *sems, after))


def _share_copy(src, dst, sems, w, core):
    x, y, c = _position()
    half = _half_rows(core, src.shape[1])
    return pltpu.make_async_remote_copy(
        src_ref=src.at[:, half], dst_ref=dst.at[:, half], send_sem=sems[0].at[w], recv_sem=sems[1].at[w],
        device_id=(x, y, 1 - c), device_id_type=MESH)


def _share_start(name, full):
    n = len(full)

    def body(*refs):
        ins, outs, sems, token = refs[:n], refs[n:2 * n], refs[2 * n:2 * n + 2], refs[2 * n + 2]
        c = lax.axis_index("c")
        for w in range(n):
            _share_copy(ins[w], outs[w], sems, w, c).start()
        token[...] = jnp.zeros_like(token)

    res = pl.pallas_call(
        body, name=name, in_specs=[_ANY] * n, out_specs=[_ANY] * n + [_SEM, _SEM, pl.BlockSpec(memory_space=pltpu.VMEM)],
        out_shape=[_hbm_out(f.shape, f.dtype) for f in full]
        + [pltpu.SemaphoreType.DMA((n,)), pltpu.SemaphoreType.DMA((n,)), jax.ShapeDtypeStruct((SUBLANES, LANES), F32)],
        input_output_aliases={w: w for w in range(n)}, compiler_params=_SPLIT,
    )(*full)
    return list(res[:n]), res[n:n + 2], res[n + 2]


def _share_wait(name, full, sems, after):
    n = len(full)

    def body(*refs):
        fs, sem_refs = refs[:n], refs[n:n + 2]
        c = lax.axis_index("c")
        for w in range(n):
            _share_copy(fs[w], fs[w], sem_refs, w, 1 - c).wait_recv()
        for w in range(n):
            _share_copy(fs[w], fs[w], sem_refs, w, c).wait_send()

    return list(pl.pallas_call(
        body, name=name, in_specs=[_ANY] * n + [_SEM, _SEM, _ANY], out_specs=[_ANY] * n,
        out_shape=[_hbm_out(f.shape, f.dtype) for f in full],
        input_output_aliases={w: w for w in range(n)}, compiler_params=_SPLIT,
    )(*full, *sems, after))


def _all_copy(a_ref, o_ref, sems, k, slot):
    x, y, c = _position()
    return pltpu.make_async_remote_copy(
        src_ref=a_ref, dst_ref=o_ref.at[slot], send_sem=sems[0].at[k - 1], recv_sem=sems[1].at[k - 1],
        device_id=(x ^ ((k >> 2) & 1), y ^ ((k >> 1) & 1), c ^ (k & 1)), device_id_type=MESH)


def _gather_all_start(name, a):
    def body(a_ref, o_ref, send_sem, recv_sem, token):
        x, y, c = _position()
        for k in range(1, N_DEV):
            _all_copy(a_ref, o_ref, (send_sem, recv_sem), k, 4 * x + 2 * y + c).start()
        token[...] = jnp.zeros_like(token)

    out, send_sem, recv_sem, token = pl.pallas_call(
        body, name=name, in_specs=[_ANY], out_specs=[_ANY, _SEM, _SEM, pl.BlockSpec(memory_space=pltpu.VMEM)],
        out_shape=[_hbm_out((N_DEV,) + a.shape, a.dtype), pltpu.SemaphoreType.DMA((N_DEV - 1,)),
                   pltpu.SemaphoreType.DMA((N_DEV - 1,)), jax.ShapeDtypeStruct((SUBLANES, LANES), F32)],
        compiler_params=_SPLIT,
    )(a)
    return out, (send_sem, recv_sem), token


def _gather_all_wait(name, a, out, sems, after):
    def body(a_ref, o_ref, send_sem, recv_sem, after_ref, res_ref):
        x, y, c = _position()
        for k in range(1, N_DEV):
            peer = 4 * (x ^ ((k >> 2) & 1)) + 2 * (y ^ ((k >> 1) & 1)) + (c ^ (k & 1))
            _all_copy(a_ref, o_ref, (send_sem, recv_sem), k, peer).wait_recv()
        for k in range(1, N_DEV):
            _all_copy(a_ref, o_ref, (send_sem, recv_sem), k, 4 * x + 2 * y + c).wait_send()

    return pl.pallas_call(
        body, name=name, in_specs=[_ANY, _ANY, _SEM, _SEM, _ANY], out_specs=_ANY,
        out_shape=_hbm_out(out.shape, out.dtype), input_output_aliases={1: 0}, compiler_params=_SPLIT,
    )(a, out, *sems, after)


def _rows2d(a, lead=0):
    return a.reshape(a.shape[:lead] + (-1, a.shape[-1]))


def _row_tile(rows, cols, itemsize=4, target=1 << 20):
    want = max(SUBLANES, target // (cols * itemsize))
    t = min(rows, (want // 16) * 16)
    while t > 16 and rows % t:
        t -= 16
    return t if rows % t == 0 else rows


def _sum_slots(name, r, out_dtype=F32):
    ns = r.shape[0]
    r2 = _rows2d(r, 1)
    _, rows, cols = r2.shape
    tr = _row_tile(rows, cols)

    def body(r_ref, o_ref):
        acc = r_ref[0].astype(F32)
        for s in range(1, ns):
            acc = acc + r_ref[s].astype(F32)
        o_ref[...] = acc.astype(o_ref.dtype)

    out = pl.pallas_call(
        body, name=name, grid=(rows // tr,),
        in_specs=[pl.BlockSpec((ns, tr, cols), lambda i: (0, i, 0))],
        out_specs=pl.BlockSpec((tr, cols), lambda i: (i, 0)),
        out_shape=_hbm_out((rows, cols), out_dtype),
        compiler_params=_params(("parallel",)),
    )(r2)
    return out.reshape(r.shape[1:])


def _adamw(name, g_parts, w, m, v):
    shape = w.shape
    ng = len(g_parts)
    args = [_rows2d(a) for a in (*g_parts, w, m, v)]
    rows, cols = args[0].shape
    c1 = 1.0 - ADAM_B1 ** ADAM_STEP
    c2 = 1.0 - ADAM_B2 ** ADAM_STEP
    nin, nbuf = ng + 3, 3
    tp = _row_tile(rows, cols, (nbuf * nin + 2 * 4) * 4, target=24 << 20)
    steps = rows // tp
    if steps >= 4:
        def piped(*refs):
            ins, outs = refs[:nin], refs[nin:nin + 4]
            ibuf, obuf = refs[nin + 4:2 * nin + 4], refs[2 * nin + 4:2 * nin + 8]
            isem, osem = refs[2 * nin + 8:]

            def rd(s, j):
                return pltpu.make_async_copy(ins[j].at[pl.ds(pl.multiple_of(s * tp, tp), tp)], ibuf[j].at[s % nbuf],
                                             isem.at[s % nbuf, j])

            def wr(s, j):
                return pltpu.make_async_copy(obuf[j].at[s % 2], outs[j].at[pl.ds(pl.multiple_of(s * tp, tp), tp)],
                                             osem.at[s % 2, j])

            for s in range(nbuf):
                for j in range(nin):
                    rd(s, j).start()

            def step(s, carry):
                for j in range(nin):
                    rd(s, j).wait()

                @pl.when(s >= 2)
                def _():
                    for j in range(4):
                        wr(s - 2, j).wait()

                b, ob = s % nbuf, s % 2
                g = ibuf[0][b]
                for j in range(1, ng):
                    g = g + ibuf[j][b]
                mn = ADAM_B1 * ibuf[ng + 1][b] + (1.0 - ADAM_B1) * g
                vn = ADAM_B2 * ibuf[ng + 2][b] + (1.0 - ADAM_B2) * (g * g)
                obuf[0][ob] = g
                obuf[1][ob] = -ADAM_LR * ((mn / c1) / (jnp.sqrt(vn / c2) + ADAM_EPS) + ADAM_WD * ibuf[ng][b])
                obuf[2][ob] = mn
                obuf[3][ob] = vn
                for j in range(4):
                    wr(s, j).start()

                @pl.when(s + nbuf < steps)
                def _():
                    for j in range(nin):
                        rd(s + nbuf, j).start()
                return carry

            lax.fori_loop(0, steps, step, 0)
            for s in (steps - 2, steps - 1):
                for j in range(4):
                    wr(s, j).wait()

        outs = pl.pallas_call(
            piped, name=name, in_specs=[_ANY] * nin, out_specs=[_ANY] * 4,
            out_shape=[_hbm_out((rows, cols), F32)] * 4,
            scratch_shapes=[pltpu.VMEM((nbuf, tp, cols), F32)] * nin + [pltpu.VMEM((2, tp, cols), F32)] * 4
            + [pltpu.SemaphoreType.DMA((nbuf, nin)), pltpu.SemaphoreType.DMA((2, 4))],
            compiler_params=pltpu.CompilerParams(vmem_limit_bytes=VMEM_LIMIT),
        )(*args)
        return tuple(o.reshape(shape) for o in outs)

    tr = _row_tile(rows, cols, (ng + 7) * 4, target=16 << 20)

    def body(*refs):
        g = refs[0][...]
        for r in refs[1:ng]:
            g = g + r[...]
        w_ref, m_ref, v_ref = refs[ng:ng + 3]
        g_out, d_out, m_out, v_out = refs[ng + 3:]
        mn = ADAM_B1 * m_ref[...] + (1.0 - ADAM_B1) * g
        vn = ADAM_B2 * v_ref[...] + (1.0 - ADAM_B2) * (g * g)
        m_hat = mn / c1
        v_hat = vn / c2
        g_out[...] = g
        d_out[...] = -ADAM_LR * (m_hat / (jnp.sqrt(v_hat) + ADAM_EPS) + ADAM_WD * w_ref[...])
        m_out[...] = mn
        v_out[...] = vn

    spec = pl.BlockSpec((tr, cols), lambda i: (i, 0))
    outs = pl.pallas_call(
        body, name=name, grid=(rows // tr,), in_specs=[spec] * (ng + 3), out_specs=[spec] * 4,
        out_shape=[_hbm_out((rows, cols), F32)] * 4,
        compiler_params=_params(("parallel",)),
    )(*args)
    return tuple(o.reshape(shape) for o in outs)


_WEIGHTS = ["norm_mix", "norm_ffn", "w_ffn_in", "w_ffn_out", "w_rec_in", "conv_w", "conv_b", "w_lru_gates",
            "b_lru_gates", "lru_param", "w_rec_out", "norm_kv", "w_kvf", "b_forget", "w_q", "w_o", "norm_final"]
_BIG = ["w_ffn_in", "w_ffn_out", "w_rec_in", "w_lru_gates", "w_rec_out", "w_kvf", "w_q", "w_o"]


def _stack3(a):
    return a[None] if a.ndim == 2 else a.reshape(a.shape[0], -1, a.shape[-1])


def _pad_lanes(a, n):
    return jnp.pad(a, ((0, 0),) * (a.ndim - 1) + ((0, n - a.shape[-1]),))


def kernel(x, norm_mix, norm_ffn, w_ffn_in, w_ffn_out, w_rec_in, conv_w, conv_b, w_lru_gates, b_lru_gates, lru_param, w_rec_out, norm_kv, w_kvf, b_forget, w_q, w_o, norm_final, loss_target, m_norm_mix, m_norm_ffn, m_w_ffn_in, m_w_ffn_out, m_w_rec_in, m_conv_w, m_conv_b, m_w_lru_gates, m_b_lru_gates, m_lru_param, m_w_rec_out, m_norm_kv, m_w_kvf, m_b_forget, m_w_q, m_w_o, m_norm_final, v_norm_mix, v_norm_ffn, v_w_ffn_in, v_w_ffn_out, v_w_rec_in, v_conv_w, v_conv_b, v_w_lru_gates, v_b_lru_gates, v_lru_param, v_w_rec_out, v_norm_kv, v_w_kvf, v_b_forget, v_w_q, v_w_o, v_norm_final):
    P = dict(norm_mix=norm_mix, norm_ffn=norm_ffn, w_ffn_in=w_ffn_in, w_ffn_out=w_ffn_out, w_rec_in=w_rec_in,
             conv_w=conv_w, conv_b=conv_b, w_lru_gates=w_lru_gates, b_lru_gates=b_lru_gates, lru_param=lru_param,
             w_rec_out=w_rec_out, norm_kv=norm_kv, w_kvf=w_kvf, b_forget=b_forget, w_q=w_q, w_o=w_o,
             norm_final=norm_final)
    M1 = dict(norm_mix=m_norm_mix, norm_ffn=m_norm_ffn, w_ffn_in=m_w_ffn_in, w_ffn_out=m_w_ffn_out,
              w_rec_in=m_w_rec_in, conv_w=m_conv_w, conv_b=m_conv_b, w_lru_gates=m_w_lru_gates,
              b_lru_gates=m_b_lru_gates, lru_param=m_lru_param, w_rec_out=m_w_rec_out, norm_kv=m_norm_kv,
              w_kvf=m_w_kvf, b_forget=m_b_forget, w_q=m_w_q, w_o=m_w_o, norm_final=m_norm_final)
    M2 = dict(norm_mix=v_norm_mix, norm_ffn=v_norm_ffn, w_ffn_in=v_w_ffn_in, w_ffn_out=v_w_ffn_out,
              w_rec_in=v_w_rec_in, conv_w=v_conv_w, conv_b=v_conv_b, w_lru_gates=v_w_lru_gates,
              b_lru_gates=v_b_lru_gates, lru_param=v_lru_param, w_rec_out=v_w_rec_out, norm_kv=v_norm_kv,
              w_kvf=v_w_kvf, b_forget=v_b_forget, w_q=v_w_q, w_o=v_w_o, norm_final=v_norm_final)

    _, S, D = x.shape
    L = norm_mix.shape[0]
    NA, NBLK, BW, GS = w_lru_gates.shape
    C = NBLK * BW
    CS = C // N_CHIPS
    H = b_forget.shape[0]
    assert C == D and H * HEAD_DIM == D and H <= LANES
    chip = 2 * lax.axis_index("x") + lax.axis_index("y")

    small_a = jnp.concatenate([conv_w, conv_b[:, None], lru_param[:, None]], axis=1)
    small_a, b_gates = _gather_smalls("gather_smalls", [small_a, b_lru_gates])
    small_a = small_a.transpose(1, 2, 0, 3).reshape(NA, 6, C)
    b_gates = b_gates.transpose(1, 2, 0, 3).reshape(NA, NBLK, 1, N_CHIPS * GS)
    shards = [_stack3(P[w]).astype(BF16) for w in _BIG]
    core = lax.axis_index("c")
    chip_id = jnp.reshape(chip, (1,)).astype(jnp.int32)
    core_id = jnp.reshape(core, (1,)).astype(jnp.int32)
    me_core = jnp.stack([chip, core]).astype(jnp.int32)

    parts_of_layer = ("mix", "mix2", "ffn")

    def part_items(l, part):
        if part == "ffn":
            names, at = ["w_ffn_in", "w_ffn_out"], l
        elif l < NA:
            names, at = (["w_rec_in"] if part == "mix" else ["w_lru_gates", "w_rec_out"]), l
        else:
            names, at = ((["w_kvf"] if l == NA else []) + ["w_q"] if part == "mix" else ["w_o"]), l - NA
        return [(_BIG.index(n), 0 if n == "w_kvf" else at) for n in names]

    def stage_of(l, part):
        return (l, part) if l == 0 or part == "ffn" else (l, "mixer")

    def stage_items(st):
        l, part = st
        return [it for p in (("mix", "mix2") if part == "mixer" else (part,)) for it in part_items(l, p)]

    stages = [(0, p) for p in parts_of_layer] + [(l, p) for l in range(1, L) for p in ("mixer", "ffn")]
    items = [it for st in stages for it in stage_items(st)]
    ids_of = {st: [items.index(it) for it in stage_items(st)] for st in stages}
    bufs = [_place_own(f"place_{_BIG[w]}_{li}", shards[w], li, chip_id) for w, li in items]
    bufs, gather_sems = _gather_start("gather_start", shards, bufs, items, small_a)

    forwarding, fetched = {}, {}

    def layer_prefetch(l, part, after):
        st = stage_of(l, part)
        if l < L and st not in forwarding:
            got = _gather_wait(f"gather_wait_{st[1]}_{l}", shards, bufs, items, ids_of[st], gather_sems, after)
            forwarding[st] = _forward_start(f"forward_start_{st[1]}_{l}", got)

    def layer_weights(l, part, after):
        if l >= L:
            return None
        st = stage_of(l, part)
        if st not in fetched:
            layer_prefetch(l, part, after)
            got, sems = forwarding[st]
            got = _forward_wait(f"forward_wait_{st[1]}_{l}", got, sems, after)
            fetched[st] = {_BIG[items[i][0]]: g for i, g in zip(ids_of[st], got)}
        B = fetched[st]
        if part == "ffn":
            return dict(w_ffn_in=B["w_ffn_in"], w_ffn_out=B["w_ffn_out"].reshape(-1, D))
        if l < NA and part == "mix":
            return dict(w_rec_in=B["w_rec_in"], conv_w=small_a[l, :4], conv_b=small_a[l, 4:5])
        if l < NA:
            return dict(w_gates=B["w_lru_gates"].reshape(N_CHIPS, NBLK, BW, GS).transpose(1, 2, 0, 3).reshape(
                NBLK, BW, N_CHIPS * GS), b_gates=b_gates[l], w_rec_out=B["w_rec_out"].reshape(C, D),
                lru_param=small_a[l, 5:6])
        if part == "mix2":
            return dict(w_o=B["w_o"].reshape(D, D))
        W = dict(w_q=B["w_q"].reshape(D, D))
        if l == NA:
            w_kvf_full = B["w_kvf"].transpose(1, 0, 2).reshape(D, -1)
            W.update(norm_kv=norm_kv[None], w_kv=w_kvf_full[:, :2 * D],
                     w_f=_pad_lanes(w_kvf_full[:, 2 * D:], LANES), b_f=_pad_lanes(b_forget[None], LANES))
        return W

    G_small = {l: {} for l in range(L)}
    stash = {st: {} for st in stages}
    pending = {}
    reducing = []

    def finish_reduce(after):
        st, its, grads, others, sems = reducing.pop()
        l, part = st
        others = _reduce_wait(f"reduce_wait_{part}_{l}", grads, others, sems, after)
        parts = [_sum_cores(f"sum_cores_{l}_{_BIG[w]}", g, o, core_id) for (w, _), g, o in zip(its, grads, others)]
        recv, sems, token = _scatter_start(f"scatter_start_{part}_{l}", parts)
        pending[st] = (parts, recv, sems)
        return token

    def layer_grads(l, part, G_part):
        G_small[l].update(G_part)
        st = stage_of(l, part)
        stash[st].update(G_part)
        if st[1] == "mixer" and part != "mix":
            return None
        G = stash[st]
        late = {"ffn": "w_ffn_in", "mix": "norm_mix"}.get(part) or ("w_gates" if l < NA else "w_o")
        by_name = dict(
            w_ffn_in=lambda: G["w_ffn_in"], w_ffn_out=lambda: G["w_ffn_out"].reshape(N_CHIPS, -1, D),
            w_rec_in=lambda: G["w_rec_in"],
            w_lru_gates=lambda: G["w_gates"].reshape(NBLK, BW, N_CHIPS, GS).transpose(2, 0, 1, 3).reshape(
                N_CHIPS, NBLK * BW, GS),
            w_rec_out=lambda: G["w_rec_out"].reshape(N_CHIPS, -1, D),
            w_kvf=lambda: jnp.concatenate([G["w_kv"].astype(F32), G["w_f"][:, :H]], axis=1).reshape(
                D, N_CHIPS, -1).transpose(1, 0, 2).astype(BF16),
            w_q=lambda: G["w_q"].reshape(N_CHIPS, -1, D), w_o=lambda: G["w_o"].reshape(N_CHIPS, -1, D))
        its = stage_items(st)
        grads = [by_name[_BIG[w]]() for w, _ in its]
        if reducing:
            pst, pits, pgrads, pothers, psems = reducing.pop()
            pothers = _reduce_wait(f"reduce_wait_{pst[1]}_{pst[0]}", pgrads, pothers, psems, G_part[late])
            pparts = [_sum_cores(f"sum_cores_{pst[0]}_{_BIG[w]}", g, o, core_id)
                      for (w, _), g, o in zip(pits, pgrads, pothers)]
            (recv, ssems), (others, sems), token = _scatter_reduce_start(
                f"scatter_reduce_start_{st[1]}_{l}", pparts, grads)
            pending[pst] = (pparts, recv, ssems)
        else:
            others, sems, token = _reduce_start(f"reduce_start_{st[1]}_{l}", grads, jnp.zeros((SUBLANES, LANES), F32))
        reducing.append((st, its, grads, others, sems))
        return finish_reduce(token) if l == 0 else token

    gains = dict(mix=[norm_mix[l][None] for l in range(L)], ffn=[norm_ffn[l][None] for l in range(L)],
                 final=norm_final[None])
    loss_row, grad_x, dg_final = _local_step(x.reshape(S, D), loss_target.reshape(S, D), gains,
                                             layer_weights, layer_prefetch, layer_grads)

    rows = [*[G_small[l]["norm_mix"] for l in range(L)], *[G_small[l]["norm_ffn"] for l in range(L)],
            G_small[NA]["norm_kv"], dg_final, _pad_lanes(G_small[NA]["b_f"], D), _pad_lanes(loss_row, D)]
    for a in range(NA):
        rows += [G_small[a][n] for n in ("conv_w", "conv_b", "b_gi", "b_gr", "lru_param")]
    packed = jnp.concatenate(rows, axis=0)
    everyone, small_sems, small_token = _gather_all_start("gather_small_start", packed)

    full = [lax.empty(sh.shape, F32) for sh in shards]
    for st in reversed(stages):
        l, part = st
        parts, recv, sems = pending[st]
        recv = _scatter_wait(f"scatter_wait_{part}_{l}", parts, recv, sems, small_token)
        for (w, li), own, r in zip(stage_items(st), parts, recv):
            full[w] = _sum_chips(f"sum_chips_{l}_{_BIG[w]}", r, own, full[w], li, me_core)
    full, share_sems, share_token = _share_start("share_start", full)

    everyone = _gather_all_wait("gather_small_wait", packed, everyone, small_sems, share_token)
    everyone = lax.dynamic_update_slice(everyone, packed[None], (2 * chip + core, 0, 0))
    tot = _sum_slots("sum_small", everyone)
    loss = tot[2 * L + 3, 0]
    g_rep = jnp.concatenate([tot[:2 * L + 2], tot[2 * L + 2:2 * L + 3]], axis=0)
    base = 2 * L + 4
    g_sh = []
    for a in range(NA):
        blk = lax.dynamic_slice_in_dim(tot[base + 8 * a:base + 8 * a + 8], chip * CS, CS, axis=1)
        gi = tot[base + 8 * a + 5].reshape(NBLK, BW)
        gr = tot[base + 8 * a + 6].reshape(NBLK, BW)
        bl = lax.dynamic_slice_in_dim(jnp.concatenate([gi, gr], axis=1), chip * GS, GS, axis=1)
        g_sh += [blk[:5], bl.reshape(-1, CS), blk[7:8]]
    g_sh = jnp.concatenate(g_sh, axis=0)
    nrow = g_sh.shape[0] // NA

    def pack_rep(T):
        return jnp.concatenate([T["norm_mix"], T["norm_ffn"], T["norm_kv"][None], T["norm_final"][None],
                                _pad_lanes(T["b_forget"][None], D)], axis=0)

    def pack_sh(T):
        return jnp.concatenate([jnp.concatenate([T["conv_w"][a], T["conv_b"][a][None],
                                                 T["b_lru_gates"][a].reshape(-1, CS), T["lru_param"][a][None]], axis=0)
                                for a in range(NA)], axis=0)

    rep = _adamw("adamw_replicated", [g_rep], pack_rep(P), pack_rep(M1), pack_rep(M2))
    shd = _adamw("adamw_small_sharded", [g_sh], pack_sh(P), pack_sh(M1), pack_sh(M2))

    full = _share_wait("share_wait", full, share_sems, shd[1])
    big = {w: _adamw(f"adamw_{w}", [g.reshape(P[w].shape)], P[w], M1[w], M2[w]) for w, g in zip(_BIG, full)}

    def unpack_rep(t):
        return dict(norm_mix=t[:L], norm_ffn=t[L:2 * L], norm_kv=t[2 * L], norm_final=t[2 * L + 1],
                    b_forget=t[2 * L + 2, :H])

    def unpack_sh(t):
        t = t.reshape(NA, nrow, CS)
        return dict(conv_w=t[:, :4], conv_b=t[:, 4], b_lru_gates=t[:, 5:nrow - 1].reshape(NA, NBLK, GS),
                    lru_param=t[:, nrow - 1])

    outs = []
    for i in range(4):
        small = {**unpack_rep(rep[i]), **unpack_sh(shd[i])}
        outs.append([big[w][i] if w in big else small[w] for w in _WEIGHTS])
    return (loss, grad_x.reshape(1, S, D), *outs[0], *outs[1], *outs[2], *outs[3])
```
